```python
import jax, jax.numpy as jnp
from jax import lax
import numpy as np

D_MODEL = 1024
BATCH = 8
SEQ = 2048
DEPTH = 1

HEAD_DIM = 64
N_SLOT_HEADS = 8
DILATED_PATTERNS = ((128, 1), (512, 4), (2048, 16))
N_GROUPS = len(DILATED_PATTERNS)
ATTN_HEADS = N_GROUPS * N_SLOT_HEADS
ATTN_QKV_WIDTH = ATTN_HEADS * HEAD_DIM
ATTN_OUT_WIDTH = N_SLOT_HEADS * HEAD_DIM
BLK = max(w // (2 * d) for (w, d) in DILATED_PATTERNS)
ROPE_THETA = 500000.0
ROT_DIM = HEAD_DIM // 4
CONV_CH = D_MODEL // 2
CONV_WIDTH = 31
N_BRANCH = 2
D_FF = -(-8 * D_MODEL // 768) * 256
IN_WIDTH = 3 * ATTN_QKV_WIDTH + 2 * CONV_CH + N_BRANCH * D_MODEL
EPS = 1e-6
NEG_INF = -1e30

kernel_name = 'hybrid_dilated_attn_conformer_conv_block'


def rmsnorm(t, w):
    tf = t.astype(jnp.float32)
    y = tf * lax.rsqrt(jnp.mean(tf * tf, axis=-1, keepdims=True) + EPS)
    return (y * w.astype(jnp.float32)).astype(t.dtype)


def layernorm(t, w, b):
    tf = t.astype(jnp.float32)
    mu = jnp.mean(tf, axis=-1, keepdims=True)
    var = jnp.mean(jnp.square(tf - mu), axis=-1, keepdims=True)
    y = (tf - mu) * lax.rsqrt(var + EPS)
    return (y * w.astype(jnp.float32) + b.astype(jnp.float32)).astype(t.dtype)


def partial_rope(t, cos, sin):
    tf = t.astype(jnp.float32)
    half = ROT_DIM // 2
    t1, t2, rest = tf[..., :half], tf[..., half:ROT_DIM], tf[..., ROT_DIM:]
    rot = jnp.concatenate([t1 * cos - t2 * sin, t2 * cos + t1 * sin, rest], axis=-1)
    return rot.astype(t.dtype)


def dilated_window_attention(q, k, v, dilation, half_span):
    B, S, H, Dh = q.shape
    L = S // dilation
    nb = -(-L // BLK)
    Lp = nb * BLK

    def residue_major(t):
        return t.reshape(B, L, dilation, H, Dh).transpose(0, 2, 3, 1, 4)

    qr, kr, vr = residue_major(q), residue_major(k), residue_major(v)
    qb = jnp.pad(qr, [(0, 0)] * 3 + [(0, Lp - L), (0, 0)]).reshape(B, dilation, H, nb, BLK, Dh)

    def banded(t):
        tb = jnp.pad(t, [(0, 0)] * 3 + [(BLK, Lp - L + BLK), (0, 0)])
        tb = tb.reshape(B, dilation, H, nb + 2, BLK, Dh)
        return jnp.concatenate([tb[:, :, :, j:j + nb] for j in range(3)], axis=4)

    kb, vb = banded(kr), banded(vr)
    qpos = jnp.arange(nb)[:, None] * BLK + jnp.arange(BLK)[None, :]
    kpos = jnp.arange(nb)[:, None] * BLK - BLK + jnp.arange(3 * BLK)[None, :]
    dist = jnp.abs(qpos[:, :, None] - kpos[:, None, :])
    valid = (dist <= half_span) & (kpos[:, None, :] >= 0) & (kpos[:, None, :] < L)

    s = jnp.einsum('bdhnqc,bdhnkc->bdhnqk', qb.astype(jnp.float32), kb.astype(jnp.float32))
    s = jnp.where(valid, s * (HEAD_DIM ** -0.5), NEG_INF)
    m = jnp.max(s, axis=-1, keepdims=True)
    p = jnp.exp(s - m)
    den = jnp.sum(p, axis=-1, keepdims=True)
    o = jnp.einsum('bdhnqk,bdhnkc->bdhnqc', p, vb.astype(jnp.float32)) / den
    lse = (m + jnp.log(den))[..., 0]
    o = o.reshape(B, dilation, H, Lp, Dh)[:, :, :, :L].transpose(0, 3, 1, 2, 4).reshape(B, S, H, Dh)
    lse = lse.reshape(B, dilation, H, Lp)[..., :L].transpose(0, 3, 1, 2).reshape(B, S, H)
    return o, lse


def depthwise_conv(u, w, b):
    pad = (CONV_WIDTH - 1) // 2
    y = lax.conv_general_dilated(
        u, w[:, None, :].astype(u.dtype), window_strides=(1,), padding=[(pad, pad)],
        dimension_numbers=('NWC', 'WIO', 'NWC'), feature_group_count=CONV_CH)
    return y + b.astype(u.dtype)


def _fwd_setup_inputs(seed: int = 0) -> dict:
    key = jax.random.key(seed)
    ks = jax.random.split(key, 20)
    f32 = jnp.float32

    def nrm(k, shape, scale):
        return jax.random.normal(k, shape, f32) * scale

    x = jax.random.normal(ks[0], (BATCH, SEQ, D_MODEL), f32)
    offsets = jax.random.randint(ks[1], (BATCH, 1), 0, 4096, dtype=jnp.int32)
    positions = (offsets + jnp.arange(SEQ, dtype=jnp.int32)[None, :]).astype(jnp.int32)
    return {
        'x': x,
        'positions': positions,
        'norm1_w': 1.0 + nrm(ks[2], (DEPTH, D_MODEL), 0.02),
        'w_in': nrm(ks[3], (DEPTH, D_MODEL, IN_WIDTH), D_MODEL ** -0.5),
        'b_gate': nrm(ks[4], (DEPTH, N_BRANCH, D_MODEL), 0.02),
        'q_norm_w': 1.0 + nrm(ks[5], (DEPTH, HEAD_DIM), 0.02),
        'k_norm_w': 1.0 + nrm(ks[6], (DEPTH, HEAD_DIM), 0.02),
        'w_o_attn': nrm(ks[7], (DEPTH, ATTN_OUT_WIDTH, D_MODEL), ATTN_OUT_WIDTH ** -0.5),
        'conv_w': nrm(ks[8], (DEPTH, CONV_WIDTH, CONV_CH), CONV_WIDTH ** -0.5),
        'conv_b': nrm(ks[9], (DEPTH, CONV_CH), 0.02),
        'conv_ln_w': 1.0 + nrm(ks[10], (DEPTH, CONV_CH), 0.02),
        'conv_ln_b': nrm(ks[11], (DEPTH, CONV_CH), 0.02),
        'w_pw_conv': nrm(ks[12], (DEPTH, CONV_CH, D_MODEL), CONV_CH ** -0.5),
        'w_out': nrm(ks[13], (DEPTH, D_MODEL, D_MODEL), D_MODEL ** -0.5),
        'norm2_w': 1.0 + nrm(ks[14], (DEPTH, D_MODEL), 0.02),
        'w_ffn_in': nrm(ks[15], (DEPTH, D_MODEL, 2 * D_FF), D_MODEL ** -0.5),
        'w_ffn_out': nrm(ks[16], (DEPTH, D_FF, D_MODEL), D_FF ** -0.5),
    }


def _fwd_reference(x, positions, norm1_w, w_in, b_gate, q_norm_w, k_norm_w, w_o_attn,
              conv_w, conv_b, conv_ln_w, conv_ln_b, w_pw_conv, w_out, norm2_w,
              w_ffn_in, w_ffn_out):
    B, S, _ = x.shape
    inv_freq = ROPE_THETA ** (-jnp.arange(0, ROT_DIM, 2, dtype=jnp.float32) / ROT_DIM)
    ang = positions.astype(jnp.float32)[..., None] * inv_freq
    cos = jnp.cos(ang)[:, :, None, None, :]
    sin = jnp.sin(ang)[:, :, None, None, :]
    split_at = [ATTN_QKV_WIDTH, 2 * ATTN_QKV_WIDTH, 3 * ATTN_QKV_WIDTH,
                3 * ATTN_QKV_WIDTH + 2 * CONV_CH]

    for l in range(DEPTH):
        h = rmsnorm(x, norm1_w[l])
        proj = h @ w_in[l].astype(h.dtype)
        q, k, v, conv_in, gate_logits = jnp.split(proj, split_at, axis=-1)
        hshape = (B, S, N_GROUPS, N_SLOT_HEADS, HEAD_DIM)
        q = partial_rope(rmsnorm(q.reshape(hshape), q_norm_w[l]), cos, sin)
        k = partial_rope(rmsnorm(k.reshape(hshape), k_norm_w[l]), cos, sin)
        v = v.reshape(hshape)

        outs, lses = [], []
        for g, (window, dilation) in enumerate(DILATED_PATTERNS):
            o_g, lse_g = dilated_window_attention(q[:, :, g], k[:, :, g], v[:, :, g],
                                                  dilation, window // (2 * dilation))
            outs.append(o_g)
            lses.append(lse_g)
        mix = jax.nn.softmax(jnp.stack(lses, axis=0), axis=0)
        attn = jnp.sum(mix[..., None] * jnp.stack(outs, axis=0), axis=0)
        attn = attn.reshape(B, S, ATTN_OUT_WIDTH).astype(x.dtype)
        y_a = attn @ w_o_attn[l].astype(x.dtype)

        a, b = jnp.split(conv_in, 2, axis=-1)
        u = a * jax.nn.sigmoid(b)
        u = depthwise_conv(u, conv_w[l], conv_b[l])
        u = jax.nn.silu(layernorm(u, conv_ln_w[l], conv_ln_b[l]))
        y_b = u @ w_pw_conv[l].astype(u.dtype)

        gates = jax.nn.sigmoid(gate_logits + b_gate[l].reshape(N_BRANCH * D_MODEL).astype(x.dtype))
        g_a, g_b = jnp.split(gates, 2, axis=-1)
        x = x + (g_a * y_a + g_b * y_b) @ w_out[l].astype(x.dtype)

        h2 = rmsnorm(x, norm2_w[l])
        gt, up = jnp.split(h2 @ w_ffn_in[l].astype(h2.dtype), 2, axis=-1)
        x = x + (jax.nn.silu(gt) * up) @ w_ffn_out[l].astype(x.dtype)
    return x


import jax as _jax
import jax.numpy as _jnp

TWIN_FORMAT = 'train_step'
FWD_PARAMS = ['x', 'positions', 'norm1_w', 'w_in', 'b_gate', 'q_norm_w', 'k_norm_w', 'w_o_attn', 'conv_w', 'conv_b', 'conv_ln_w', 'conv_ln_b', 'w_pw_conv', 'w_out', 'norm2_w', 'w_ffn_in', 'w_ffn_out']
TWIN_WEIGHTS = ['norm1_w', 'w_in', 'b_gate', 'q_norm_w', 'k_norm_w', 'w_o_attn', 'conv_w', 'conv_b', 'conv_ln_w', 'conv_ln_b', 'w_pw_conv', 'w_out', 'norm2_w', 'w_ffn_in', 'w_ffn_out']
TWIN_DIFF_INPUT = 'x'
TWIN_INPUTS = ['x', 'positions', 'norm1_w', 'w_in', 'b_gate', 'q_norm_w', 'k_norm_w', 'w_o_attn', 'conv_w', 'conv_b', 'conv_ln_w', 'conv_ln_b', 'w_pw_conv', 'w_out', 'norm2_w', 'w_ffn_in', 'w_ffn_out', 'loss_target', 'm_norm1_w', 'm_w_in', 'm_b_gate', 'm_q_norm_w', 'm_k_norm_w', 'm_w_o_attn', 'm_conv_w', 'm_conv_b', 'm_conv_ln_w', 'm_conv_ln_b', 'm_w_pw_conv', 'm_w_out', 'm_norm2_w', 'm_w_ffn_in', 'm_w_ffn_out', 'v_norm1_w', 'v_w_in', 'v_b_gate', 'v_q_norm_w', 'v_k_norm_w', 'v_w_o_attn', 'v_conv_w', 'v_conv_b', 'v_conv_ln_w', 'v_conv_ln_b', 'v_w_pw_conv', 'v_w_out', 'v_norm2_w', 'v_w_ffn_in', 'v_w_ffn_out']
TWIN_OUTPUTS = ['loss', 'grad_x', 'grad_norm1_w', 'grad_w_in', 'grad_b_gate', 'grad_q_norm_w', 'grad_k_norm_w', 'grad_w_o_attn', 'grad_conv_w', 'grad_conv_b', 'grad_conv_ln_w', 'grad_conv_ln_b', 'grad_w_pw_conv', 'grad_w_out', 'grad_norm2_w', 'grad_w_ffn_in', 'grad_w_ffn_out', 'delta_norm1_w', 'delta_w_in', 'delta_b_gate', 'delta_q_norm_w', 'delta_k_norm_w', 'delta_w_o_attn', 'delta_conv_w', 'delta_conv_b', 'delta_conv_ln_w', 'delta_conv_ln_b', 'delta_w_pw_conv', 'delta_w_out', 'delta_norm2_w', 'delta_w_ffn_in', 'delta_w_ffn_out', 'new_m_norm1_w', 'new_m_w_in', 'new_m_b_gate', 'new_m_q_norm_w', 'new_m_k_norm_w', 'new_m_w_o_attn', 'new_m_conv_w', 'new_m_conv_b', 'new_m_conv_ln_w', 'new_m_conv_ln_b', 'new_m_w_pw_conv', 'new_m_w_out', 'new_m_norm2_w', 'new_m_w_ffn_in', 'new_m_w_ffn_out', 'new_v_norm1_w', 'new_v_w_in', 'new_v_b_gate', 'new_v_q_norm_w', 'new_v_k_norm_w', 'new_v_w_o_attn', 'new_v_conv_w', 'new_v_conv_b', 'new_v_conv_ln_w', 'new_v_conv_ln_b', 'new_v_w_pw_conv', 'new_v_w_out', 'new_v_norm2_w', 'new_v_w_ffn_in', 'new_v_w_ffn_out']
TWIN_LEAF_KINDS = {'loss': 'loss', 'grad_x': 'grad_x', 'grad_norm1_w': 'grad_w', 'grad_w_in': 'grad_w', 'grad_b_gate': 'grad_w', 'grad_q_norm_w': 'grad_w', 'grad_k_norm_w': 'grad_w', 'grad_w_o_attn': 'grad_w', 'grad_conv_w': 'grad_w', 'grad_conv_b': 'grad_w', 'grad_conv_ln_w': 'grad_w', 'grad_conv_ln_b': 'grad_w', 'grad_w_pw_conv': 'grad_w', 'grad_w_out': 'grad_w', 'grad_norm2_w': 'grad_w', 'grad_w_ffn_in': 'grad_w', 'grad_w_ffn_out': 'grad_w', 'delta_norm1_w': 'delta_w', 'delta_w_in': 'delta_w', 'delta_b_gate': 'delta_w', 'delta_q_norm_w': 'delta_w', 'delta_k_norm_w': 'delta_w', 'delta_w_o_attn': 'delta_w', 'delta_conv_w': 'delta_w', 'delta_conv_b': 'delta_w', 'delta_conv_ln_w': 'delta_w', 'delta_conv_ln_b': 'delta_w', 'delta_w_pw_conv': 'delta_w', 'delta_w_out': 'delta_w', 'delta_norm2_w': 'delta_w', 'delta_w_ffn_in': 'delta_w', 'delta_w_ffn_out': 'delta_w', 'new_m_norm1_w': 'new_m', 'new_m_w_in': 'new_m', 'new_m_b_gate': 'new_m', 'new_m_q_norm_w': 'new_m', 'new_m_k_norm_w': 'new_m', 'new_m_w_o_attn': 'new_m', 'new_m_conv_w': 'new_m', 'new_m_conv_b': 'new_m', 'new_m_conv_ln_w': 'new_m', 'new_m_conv_ln_b': 'new_m', 'new_m_w_pw_conv': 'new_m', 'new_m_w_out': 'new_m', 'new_m_norm2_w': 'new_m', 'new_m_w_ffn_in': 'new_m', 'new_m_w_ffn_out': 'new_m', 'new_v_norm1_w': 'new_v', 'new_v_w_in': 'new_v', 'new_v_b_gate': 'new_v', 'new_v_q_norm_w': 'new_v', 'new_v_k_norm_w': 'new_v', 'new_v_w_o_attn': 'new_v', 'new_v_conv_w': 'new_v', 'new_v_conv_b': 'new_v', 'new_v_conv_ln_w': 'new_v', 'new_v_conv_ln_b': 'new_v', 'new_v_w_pw_conv': 'new_v', 'new_v_w_out': 'new_v', 'new_v_norm2_w': 'new_v', 'new_v_w_ffn_in': 'new_v', 'new_v_w_ffn_out': 'new_v'}


def _forward(args):
    return _fwd_reference(*[args[k] for k in FWD_PARAMS])


def _output_shape():
    out = _jax.eval_shape(lambda: _forward(_fwd_setup_inputs(0)))
    return out.shape, out.dtype

N_MICROBATCH = 1
ADAM_LR = 0.001
ADAM_B1 = 0.9
ADAM_B2 = 0.999
ADAM_EPS = 1e-08
ADAM_WD = 0.01
ADAM_STEP = 10
PER_EXAMPLE_BATCH_AXIS = {'x': 0, 'positions': 0, 'loss_target': 0}
SHARED_INPUTS = []
_WEIGHT_DTYPES = {'norm1_w': _jnp.float32, 'w_in': _jnp.float32, 'b_gate': _jnp.float32, 'q_norm_w': _jnp.float32, 'k_norm_w': _jnp.float32, 'w_o_attn': _jnp.float32, 'conv_w': _jnp.float32, 'conv_b': _jnp.float32, 'conv_ln_w': _jnp.float32, 'conv_ln_b': _jnp.float32, 'w_pw_conv': _jnp.float32, 'w_out': _jnp.float32, 'norm2_w': _jnp.float32, 'w_ffn_in': _jnp.float32, 'w_ffn_out': _jnp.float32}
MOMENT_SCALE = {'norm1_w': 2.093772e-01, 'w_in': 4.176218e-02, 'b_gate': 4.358522e-01, 'q_norm_w': 3.889154e-01, 'k_norm_w': 3.874671e-01, 'w_o_attn': 2.034757e-02, 'conv_w': 1.826727e-01, 'conv_b': 2.586824e+00, 'conv_ln_w': 4.245250e+00, 'conv_ln_b': 2.965591e+00, 'w_pw_conv': 3.748435e-01, 'w_out': 3.097523e-01, 'norm2_w': 1.238502e+01, 'w_ffn_in': 1.395100e-01, 'w_ffn_out': 1.675672e-01}


def _to_microbatches(a, axis):
    t = _jnp.moveaxis(a, axis, 0)
    t = t.reshape((N_MICROBATCH, t.shape[0] // N_MICROBATCH) + t.shape[1:])
    return _jnp.moveaxis(t, 1, axis + 1)


def setup_inputs(seed: int = 0) -> dict:
    inp = _fwd_setup_inputs(seed)
    key = _jax.random.fold_in(_jax.random.key(seed), 7919)
    shape, _ = _output_shape()
    out = dict(inp)
    out["loss_target"] = _jax.random.normal(_jax.random.fold_in(key, 0), shape, _jnp.float32)
    for i, name in enumerate(TWIN_WEIGHTS):
        w = inp[name].astype(_jnp.float32)
        if MOMENT_SCALE is None:
            s = _jnp.sqrt(_jnp.mean(_jnp.square(w)) + 1e-30)
        else:
            s = MOMENT_SCALE[name]
        km, kv = _jax.random.split(_jax.random.fold_in(key, i + 1))
        out[name] = w
        out["m_" + name] = s * _jax.random.normal(km, w.shape, _jnp.float32)
        out["v_" + name] = (s * s) * _jax.random.uniform(kv, w.shape, _jnp.float32, 0.5, 1.5)
    if N_MICROBATCH > 1:
        for name, axis in PER_EXAMPLE_BATCH_AXIS.items():
            out[name] = _to_microbatches(out[name], axis)
    return {'x': out['x'], 'positions': out['positions'], 'norm1_w': out['norm1_w'], 'w_in': out['w_in'], 'b_gate': out['b_gate'], 'q_norm_w': out['q_norm_w'], 'k_norm_w': out['k_norm_w'], 'w_o_attn': out['w_o_attn'], 'conv_w': out['conv_w'], 'conv_b': out['conv_b'], 'conv_ln_w': out['conv_ln_w'], 'conv_ln_b': out['conv_ln_b'], 'w_pw_conv': out['w_pw_conv'], 'w_out': out['w_out'], 'norm2_w': out['norm2_w'], 'w_ffn_in': out['w_ffn_in'], 'w_ffn_out': out['w_ffn_out'], 'loss_target': out['loss_target'], 'm_norm1_w': out['m_norm1_w'], 'm_w_in': out['m_w_in'], 'm_b_gate': out['m_b_gate'], 'm_q_norm_w': out['m_q_norm_w'], 'm_k_norm_w': out['m_k_norm_w'], 'm_w_o_attn': out['m_w_o_attn'], 'm_conv_w': out['m_conv_w'], 'm_conv_b': out['m_conv_b'], 'm_conv_ln_w': out['m_conv_ln_w'], 'm_conv_ln_b': out['m_conv_ln_b'], 'm_w_pw_conv': out['m_w_pw_conv'], 'm_w_out': out['m_w_out'], 'm_norm2_w': out['m_norm2_w'], 'm_w_ffn_in': out['m_w_ffn_in'], 'm_w_ffn_out': out['m_w_ffn_out'], 'v_norm1_w': out['v_norm1_w'], 'v_w_in': out['v_w_in'], 'v_b_gate': out['v_b_gate'], 'v_q_norm_w': out['v_q_norm_w'], 'v_k_norm_w': out['v_k_norm_w'], 'v_w_o_attn': out['v_w_o_attn'], 'v_conv_w': out['v_conv_w'], 'v_conv_b': out['v_conv_b'], 'v_conv_ln_w': out['v_conv_ln_w'], 'v_conv_ln_b': out['v_conv_ln_b'], 'v_w_pw_conv': out['v_w_pw_conv'], 'v_w_out': out['v_w_out'], 'v_norm2_w': out['v_norm2_w'], 'v_w_ffn_in': out['v_w_ffn_in'], 'v_w_ffn_out': out['v_w_ffn_out']}


def _loss(weights, diff, rest, loss_target):
    with _jax.named_scope("forward"):
        args = {**rest, TWIN_DIFF_INPUT: diff, **{k: w.astype(_WEIGHT_DTYPES[k]) for k, w in weights.items()}}
        y = _forward(args)
    with _jax.named_scope("loss_head"):
        err = _jnp.square(y.astype(_jnp.float32) - loss_target)
        return 0.5 * _jnp.sum(_jnp.mean(err, axis=-1)) if err.ndim else 0.5 * err


def _adamw(w, g, m, v):
    m = ADAM_B1 * m + (1.0 - ADAM_B1) * g
    v = ADAM_B2 * v + (1.0 - ADAM_B2) * _jnp.square(g)
    m_hat = m / (1.0 - ADAM_B1 ** ADAM_STEP)
    v_hat = v / (1.0 - ADAM_B2 ** ADAM_STEP)
    delta = -ADAM_LR * (m_hat / (_jnp.sqrt(v_hat) + ADAM_EPS) + ADAM_WD * w)
    return delta, m, v


def reference(x, positions, norm1_w, w_in, b_gate, q_norm_w, k_norm_w, w_o_attn, conv_w, conv_b, conv_ln_w, conv_ln_b, w_pw_conv, w_out, norm2_w, w_ffn_in, w_ffn_out, loss_target, m_norm1_w, m_w_in, m_b_gate, m_q_norm_w, m_k_norm_w, m_w_o_attn, m_conv_w, m_conv_b, m_conv_ln_w, m_conv_ln_b, m_w_pw_conv, m_w_out, m_norm2_w, m_w_ffn_in, m_w_ffn_out, v_norm1_w, v_w_in, v_b_gate, v_q_norm_w, v_k_norm_w, v_w_o_attn, v_conv_w, v_conv_b, v_conv_ln_w, v_conv_ln_b, v_w_pw_conv, v_w_out, v_norm2_w, v_w_ffn_in, v_w_ffn_out):
    given = dict(x=x, positions=positions, norm1_w=norm1_w, w_in=w_in, b_gate=b_gate, q_norm_w=q_norm_w, k_norm_w=k_norm_w, w_o_attn=w_o_attn, conv_w=conv_w, conv_b=conv_b, conv_ln_w=conv_ln_w, conv_ln_b=conv_ln_b, w_pw_conv=w_pw_conv, w_out=w_out, norm2_w=norm2_w, w_ffn_in=w_ffn_in, w_ffn_out=w_ffn_out, loss_target=loss_target, m_norm1_w=m_norm1_w, m_w_in=m_w_in, m_b_gate=m_b_gate, m_q_norm_w=m_q_norm_w, m_k_norm_w=m_k_norm_w, m_w_o_attn=m_w_o_attn, m_conv_w=m_conv_w, m_conv_b=m_conv_b, m_conv_ln_w=m_conv_ln_w, m_conv_ln_b=m_conv_ln_b, m_w_pw_conv=m_w_pw_conv, m_w_out=m_w_out, m_norm2_w=m_norm2_w, m_w_ffn_in=m_w_ffn_in, m_w_ffn_out=m_w_ffn_out, v_norm1_w=v_norm1_w, v_w_in=v_w_in, v_b_gate=v_b_gate, v_q_norm_w=v_q_norm_w, v_k_norm_w=v_k_norm_w, v_w_o_attn=v_w_o_attn, v_conv_w=v_conv_w, v_conv_b=v_conv_b, v_conv_ln_w=v_conv_ln_w, v_conv_ln_b=v_conv_ln_b, v_w_pw_conv=v_w_pw_conv, v_w_out=v_w_out, v_norm2_w=v_norm2_w, v_w_ffn_in=v_w_ffn_in, v_w_ffn_out=v_w_ffn_out)
    weights = {n: given[n] for n in TWIN_WEIGHTS}
    shared = {n: given[n] for n in SHARED_INPUTS}
    per_example = {n: given[n] for n in ['x', 'positions']}
    grad_fn = _jax.value_and_grad(_loss, argnums=(0, 1))

    def one_microbatch(ex, loss_target):
        ex = dict(ex)
        diff = ex.pop(TWIN_DIFF_INPUT)
        return grad_fn(weights, diff, {**shared, **ex}, loss_target)

    if N_MICROBATCH == 1:
        loss, (grad_w, grad_x) = one_microbatch(per_example, given["loss_target"])
    else:
        def body(carry, xs):
            loss_sum, grad_sum = carry
            l_k, (gw_k, gx_k) = one_microbatch(xs[0], xs[1])
            with _jax.named_scope("update"):
                return (loss_sum + l_k, _jax.tree.map(_jnp.add, grad_sum, gw_k)), gx_k

        init = (_jnp.zeros((), _jnp.float32), _jax.tree.map(_jnp.zeros_like, weights))
        (loss, grad_w), grad_x = _jax.lax.scan(body, init, (per_example, given["loss_target"]))
    with _jax.named_scope("update"):
        delta_w, new_m, new_v = {}, {}, {}
        for n in TWIN_WEIGHTS:
            delta_w[n], new_m[n], new_v[n] = _adamw(weights[n], grad_w[n], given["m_" + n], given["v_" + n])
    return (loss, grad_x, *[grad_w[n] for n in TWIN_WEIGHTS], *[delta_w[n] for n in TWIN_WEIGHTS],
            *[new_m[n] for n in TWIN_WEIGHTS], *[new_v[n] for n in TWIN_WEIGHTS])
```

```python
import functools

import numpy as np
import jax
import jax.numpy as jnp
from jax import lax
from jax.experimental import pallas as pl
from jax.experimental.pallas import tpu as pltpu

F32 = jnp.float32
BF16 = jnp.bfloat16

S = 2048
D = 1024
HD = 64
QKV = 1536
CC = 512
KW = 31
FF = 2816
INW = 7680
OFF_Q, OFF_K, OFF_V, OFF_CA, OFF_CB, OFF_GA, OFF_GB = 0, 1536, 3072, 4608, 5120, 5632, 6656
DILATIONS = (1, 4, 16)
HALF_SPAN = 64
EPS = 1e-6
NEG_INF = -1e30
ROPE_THETA = 500000.0
ROT_DIM = 16

ADAM_LR = 0.001
ADAM_B1 = 0.9
ADAM_B2 = 0.999
ADAM_EPS = 1e-08
ADAM_WD = 0.01
ADAM_STEP = 10

NDEV = 8
LANES = 128
TM = 256
TQ = 128
VMEM_LIMIT = 56 * 1024 * 1024
MESH = pl.DeviceIdType.MESH


def _cp(**kw):
    return pltpu.CompilerParams(vmem_limit_bytes=VMEM_LIMIT, **kw)


def _row(width, col=0, tm=TM):
    return pl.BlockSpec((tm, width), lambda i: (i, col))


def _res(shape):
    nd = len(shape)
    return pl.BlockSpec(shape, lambda *_: (0,) * nd, pipeline_mode=pl.Buffered(1))


def _dot(a, b):
    return jnp.dot(a, b, preferred_element_type=F32)


def _dot_nt(a, b):
    return lax.dot_general(a, b, (((1,), (1,)), ((), ())), preferred_element_type=F32)


def _dot_tn(a, b):
    return lax.dot_general(a, b, (((0,), (0,)), ((), ())), preferred_element_type=F32)


def _sigmoid(x):
    return jax.nn.sigmoid(x)


def _dsilu(x, sg):
    return sg * (1.0 + x * (1.0 - sg))


def _inv_freq_lanes():
    inv = np.float32(ROPE_THETA) ** (-np.arange(0, ROT_DIM, 2, dtype=np.float32) / np.float32(ROT_DIM))
    lane = np.arange(LANES) % HD
    out = np.where(lane < ROT_DIM, inv[lane % (ROT_DIM // 2)], 0.0).astype(np.float32)
    return jnp.asarray(out.reshape(1, LANES))


def rope_tables(pos_col):
    def body(p_ref, f_ref, c_ref, s1_ref, s2_ref):
        ang = p_ref[...].astype(F32) * f_ref[...]
        lane = lax.broadcasted_iota(jnp.int32, ang.shape, 1) % HD
        cs = jnp.cos(ang)
        sn = jnp.sin(ang)
        c_ref[...] = jnp.where(lane < ROT_DIM, cs, 1.0)
        s1_ref[...] = jnp.where(lane < ROT_DIM // 2, -sn, 0.0)
        s2_ref[...] = jnp.where(lane < ROT_DIM // 2, 0.0, jnp.where(lane < ROT_DIM, sn, 0.0))

    sds = jax.ShapeDtypeStruct((S, LANES), F32)
    return pl.pallas_call(
        body, name="rope_tables", grid=(S // TM,),
        in_specs=[_row(1), pl.BlockSpec((1, LANES), lambda i: (0, 0))],
        out_specs=[_row(LANES)] * 3, out_shape=[sds] * 3,
    )(pos_col, _inv_freq_lanes())


def _rope(v, c, s1, s2):
    return v * c + pltpu.roll(v, LANES - 8, axis=1) * s1 + pltpu.roll(v, 8, axis=1) * s2


def _rope_t(d, c, s1, s2):
    return d * c - pltpu.roll(d, LANES - 8, axis=1) * s1 - pltpu.roll(d, 8, axis=1) * s2


def _head_mean(t, lo):
    a = jnp.sum(jnp.where(lo, t, 0.0), axis=-1, keepdims=True)
    b = jnp.sum(jnp.where(lo, 0.0, t), axis=-1, keepdims=True)
    return jnp.where(lo, a, b) * (1.0 / HD)


def in_proj(x, norm_w, w_in):
    nchunk = 5
    cw = INW // nchunk

    def body(x_ref, nw_ref, w_ref, h_ref, p_ref):
        xv = x_ref[...]
        r = lax.rsqrt(jnp.mean(xv * xv, axis=-1, keepdims=True) + EPS)
        h = (xv * r * nw_ref[...]).astype(BF16)
        h_ref[...] = h
        for j in range(nchunk):
            p_ref[:, j * cw:(j + 1) * cw] = _dot(h, w_ref[:, j * cw:(j + 1) * cw])

    return pl.pallas_call(
        body, name="in_proj", grid=(S // TM,),
        in_specs=[_row(D), _res((1, D)), _res((D, INW))],
        out_specs=[_row(D), _row(INW)],
        out_shape=[jax.ShapeDtypeStruct((S, D), BF16), jax.ShapeDtypeStruct((S, INW), F32)],
        compiler_params=_cp(dimension_semantics=("arbitrary",)),
    )(x, norm_w, w_in)


def _qk_specs():
    nb = QKV // LANES
    return [pl.BlockSpec((S, LANES), functools.partial(lambda hp, g, o: (0, o + g * 4 + hp), o=o))
            for o in (OFF_Q // LANES, OFF_K // LANES, OFF_V // LANES)]


def _tab_specs():
    return [pl.BlockSpec((S, LANES), lambda hp, g: (0, 0), pipeline_mode=pl.Buffered(1))] * 3


def _vec_spec():
    return pl.BlockSpec((1, LANES), lambda hp, g: (0, 0))


def _sub_rows(r, d, start, n):
    if d == 1:
        return pl.ds(start, n)
    return pl.ds(r + d * start, n, stride=d)


def _band_window(i, L):
    W = min(2 * TQ, L)
    q0 = pl.multiple_of(i * TQ, TQ)
    k0 = pl.multiple_of(jnp.clip(q0 - HALF_SPAN, 0, L - W), HALF_SPAN)
    qpos = q0 + lax.broadcasted_iota(jnp.int32, (TQ, W), 0)
    kpos = k0 + lax.broadcasted_iota(jnp.int32, (TQ, W), 1)
    valid = jnp.abs(qpos - kpos) <= HALF_SPAN
    return W, q0, k0, valid


def attn_fwd(proj, tabs, qw2, kw2):
    CH = 256

    def body(q_ref, k_ref, v_ref, c_ref, s1_ref, s2_ref, qw_ref, kw_ref, at_ref, ls_ref,
             qs, ks, vs, osub, lsub, onat, lnat):
        g = pl.program_id(1)
        lo = lax.broadcasted_iota(jnp.int32, (1, LANES), 1) < HD

        def prep(t, w, c, s1, s2):
            r = lax.rsqrt(_head_mean(t * t, lo) + EPS)
            return _rope(t * r * w, c, s1, s2)

        def group(gi, d):
            L = S // d

            def residue(r, _):
                for c0 in range(0, L, CH):
                    n = min(CH, L)
                    rows = _sub_rows(r, d, c0, n)
                    c, s1, s2 = c_ref[rows, :], s1_ref[rows, :], s2_ref[rows, :]
                    dst = pl.ds(c0, n)
                    qs[dst, :] = (prep(q_ref[rows, :], qw_ref[...], c, s1, s2) * (HD ** -0.5)).astype(BF16)
                    ks[dst, :] = prep(k_ref[rows, :], kw_ref[...], c, s1, s2).astype(BF16)
                    vs[dst, :] = v_ref[rows, :].astype(BF16)

                def blk(i, _):
                    W, q0, k0, valid = _band_window(i, L)
                    q = qs[pl.ds(q0, TQ), :]
                    kk = ks[pl.ds(k0, W), :]
                    vv = vs[pl.ds(k0, W), :]
                    o_h, l_h = [], []
                    for h in range(2):
                        hm = lo if h == 0 else jnp.logical_not(lo)
                        sc = _dot_nt(jnp.where(hm, q, jnp.zeros_like(q)), kk)
                        sc = jnp.where(valid, sc, NEG_INF)
                        m = jnp.max(sc, axis=-1, keepdims=True)
                        p = jnp.exp(sc - m)
                        den = jnp.sum(p, axis=-1, keepdims=True)
                        o_h.append(_dot(p.astype(BF16), vv) / den)
                        l_h.append(m + jnp.log(den))
                    osub[pl.ds(q0, TQ), :] = jnp.where(lo, o_h[0], o_h[1])
                    lsub[pl.ds(q0, TQ), :] = jnp.where(lo, l_h[0], l_h[1])
                    return 0

                lax.fori_loop(0, L // TQ, blk, 0)
                for c0 in range(0, L, CH):
                    n = min(CH, L)
                    rows = _sub_rows(r, d, c0, n)
                    onat[gi, rows, :] = osub[pl.ds(c0, n), :]
                    lnat[gi, rows, :] = lsub[pl.ds(c0, n), :]
                return 0

            lax.fori_loop(0, d, residue, 0)

        for gi, d in enumerate(DILATIONS):
            pl.when(g == gi)(functools.partial(group, gi, d))

        @pl.when(g == len(DILATIONS) - 1)
        def _():
            def mix(i, _):
                rows = pl.ds(pl.multiple_of(i * CH, CH), CH)
                l0, l1, l2 = lnat[0, rows, :], lnat[1, rows, :], lnat[2, rows, :]
                m = jnp.maximum(jnp.maximum(l0, l1), l2)
                e0, e1, e2 = jnp.exp(l0 - m), jnp.exp(l1 - m), jnp.exp(l2 - m)
                den = e0 + e1 + e2
                a = (e0 * onat[0, rows, :] + e1 * onat[1, rows, :] + e2 * onat[2, rows, :]) / den
                at_ref[rows, :] = a.astype(BF16)
                ls_ref[rows, :] = m + jnp.log(den)
                return 0

            lax.fori_loop(0, S // CH, mix, 0)

    out_spec = pl.BlockSpec((S, LANES), lambda hp, g: (0, hp))
    return pl.pallas_call(
        body, name="attn_fwd", grid=(4, 3),
        in_specs=_qk_specs() + _tab_specs() + [_vec_spec(), _vec_spec()],
        out_specs=[out_spec, out_spec],
        out_shape=[jax.ShapeDtypeStruct((S, CC), BF16), jax.ShapeDtypeStruct((S, CC), F32)],
        scratch_shapes=[pltpu.VMEM((S, LANES), BF16)] * 3 + [pltpu.VMEM((S, LANES), F32)] * 2
        + [pltpu.VMEM((3, S, LANES), F32)] * 2,
        compiler_params=_cp(dimension_semantics=("arbitrary", "arbitrary")),
    )(proj, proj, proj, *tabs, qw2, kw2)


def attn_bwd(proj, tabs, qw2, kw2, d_attn, attn, lse):
    CH = 256

    def body(q_ref, k_ref, v_ref, c_ref, s1_ref, s2_ref, qw_ref, kw_ref, do_ref, at_ref, ls_ref,
             dq_ref, dk_ref, dv_ref, gqw_ref, gkw_ref,
             qs, ks, vs, dos, dsub, lsub, dqs, dks, dvs, dnat, dqn, dkn, dvn):
        hp, g = pl.program_id(0), pl.program_id(1)
        lo = lax.broadcasted_iota(jnp.int32, (1, LANES), 1) < HD

        @pl.when((hp == 0) & (g == 0))
        def _():
            gqw_ref[...] = jnp.zeros_like(gqw_ref)
            gkw_ref[...] = jnp.zeros_like(gkw_ref)

        def dsum(i, _):
            rows = pl.ds(pl.multiple_of(i * CH, CH), CH)
            dnat[rows, :] = _head_mean(do_ref[rows, :] * at_ref[rows, :].astype(F32), lo) * float(HD)
            return 0

        lax.fori_loop(0, S // CH, dsum, 0)

        def group(d):
            L = S // d

            def residue(r, _):
                for c0 in range(0, L, CH):
                    n = min(CH, L)
                    rows = _sub_rows(r, d, c0, n)
                    c, s1, s2 = c_ref[rows, :], s1_ref[rows, :], s2_ref[rows, :]
                    dst = pl.ds(c0, n)
                    qv, kv = q_ref[rows, :], k_ref[rows, :]
                    rq = lax.rsqrt(_head_mean(qv * qv, lo) + EPS)
                    rk = lax.rsqrt(_head_mean(kv * kv, lo) + EPS)
                    qs[dst, :] = (_rope(qv * rq * qw_ref[...], c, s1, s2) * (HD ** -0.5)).astype(BF16)
                    ks[dst, :] = _rope(kv * rk * kw_ref[...], c, s1, s2).astype(BF16)
                    vs[dst, :] = v_ref[rows, :].astype(BF16)
                    dos[dst, :] = do_ref[rows, :].astype(BF16)
                    dsub[dst, :] = dnat[rows, :]
                    lsub[dst, :] = ls_ref[rows, :]
                    dks[dst, :] = jnp.zeros((n, LANES), F32)
                    dvs[dst, :] = jnp.zeros((n, LANES), F32)

                def blk(i, _):
                    W, q0, k0, valid = _band_window(i, L)
                    qrows, krows = pl.ds(q0, TQ), pl.ds(k0, W)
                    q, do = qs[qrows, :], dos[qrows, :]
                    kk, vv = ks[krows, :], vs[krows, :]
                    lse_b, dd_b = lsub[qrows, :], dsub[qrows, :]
                    dq = jnp.zeros((TQ, LANES), F32)
                    dk = jnp.zeros((W, LANES), F32)
                    dv = jnp.zeros((W, LANES), F32)
                    for h in range(2):
                        hm = lo if h == 0 else jnp.logical_not(lo)
                        lane0 = h * HD
                        qh = jnp.where(hm, q, jnp.zeros_like(q))
                        doh = jnp.where(hm, do, jnp.zeros_like(do))
                        sc = jnp.where(valid, _dot_nt(qh, kk), NEG_INF)
                        p = jnp.exp(sc - lse_b[:, lane0:lane0 + 1])
                        dp = _dot_nt(doh, vv)
                        ds = (p * (dp - dd_b[:, lane0:lane0 + 1])).astype(BF16)
                        dv = dv + _dot_tn(p.astype(BF16), doh)
                        dk = dk + _dot_tn(ds, qh)
                        dq = dq + jnp.where(hm, _dot(ds, kk), 0.0)
                    dqs[qrows, :] = dq
                    dks[krows, :] = dks[krows, :] + dk
                    dvs[krows, :] = dvs[krows, :] + dv
                    return 0

                lax.fori_loop(0, L // TQ, blk, 0)

                gq = jnp.zeros((1, LANES), F32)
                gk = jnp.zeros((1, LANES), F32)
                for c0 in range(0, L, CH):
                    n = min(CH, L)
                    rows = _sub_rows(r, d, c0, n)
                    src = pl.ds(c0, n)
                    c, s1, s2 = c_ref[rows, :], s1_ref[rows, :], s2_ref[rows, :]
                    for (raw_ref, w_ref, gsub, scale, nat) in (
                            (q_ref, qw_ref, dqs, HD ** -0.5, dqn), (k_ref, kw_ref, dks, 1.0, dkn)):
                        t = raw_ref[rows, :]
                        rr = lax.rsqrt(_head_mean(t * t, lo) + EPS)
                        tn = t * rr
                        dy = _rope_t(gsub[src, :] * scale, c, s1, s2)
                        gw = jnp.sum(dy * tn, axis=0, keepdims=True)
                        if raw_ref is q_ref:
                            gq = gq + gw
                        else:
                            gk = gk + gw
                        dtn = dy * w_ref[...]
                        nat[rows, :] = rr * (dtn - tn * _head_mean(dtn * tn, lo))
                    dvn[rows, :] = dvs[src, :]
                gqw_ref[0:1, :] = gqw_ref[0:1, :] + gq
                gkw_ref[0:1, :] = gkw_ref[0:1, :] + gk
                return 0

            lax.fori_loop(0, d, residue, 0)

        for gi, d in enumerate(DILATIONS):
            pl.when(g == gi)(functools.partial(group, d))

        def emit(i, _):
            rows = pl.ds(pl.multiple_of(i * CH, CH), CH)
            dq_ref[rows, :] = dqn[rows, :].astype(BF16)
            dk_ref[rows, :] = dkn[rows, :].astype(BF16)
            dv_ref[rows, :] = dvn[rows, :].astype(BF16)
            return 0

        lax.fori_loop(0, S // CH, emit, 0)

    nat_spec = pl.BlockSpec((S, LANES), lambda hp, g: (0, hp))
    out_spec = pl.BlockSpec((S, LANES), lambda hp, g: (0, g * 4 + hp))
    acc_spec = pl.BlockSpec((8, LANES), lambda hp, g: (0, 0))
    return pl.pallas_call(
        body, name="attn_bwd", grid=(4, 3),
        in_specs=_qk_specs() + _tab_specs() + [_vec_spec(), _vec_spec(), nat_spec, nat_spec, nat_spec],
        out_specs=[out_spec] * 3 + [acc_spec] * 2,
        out_shape=[jax.ShapeDtypeStruct((S, QKV), BF16)] * 3 + [jax.ShapeDtypeStruct((8, LANES), F32)] * 2,
        scratch_shapes=[pltpu.VMEM((S, LANES), BF16)] * 4 + [pltpu.VMEM((S, LANES), F32)] * 9,
        compiler_params=_cp(dimension_semantics=("arbitrary", "arbitrary")),
    )(proj, proj, proj, *tabs, qw2, kw2, d_attn, attn, lse)


PADR = 16
CT = 128


def _conv_specs():
    return [pl.BlockSpec((S, CC), lambda i: (0, OFF_CA // CC)), pl.BlockSpec((S, CC), lambda i: (0, OFF_CB // CC))]


NCB = CC // LANES


def _pad_zero(pad):
    for cb in range(NCB):
        pad[cb, 0:PADR, :] = jnp.zeros((PADR, LANES), F32)
        pad[cb, PADR + S:PADR + S + PADR, :] = jnp.zeros((PADR, LANES), F32)


def _pad_store(pad, row0, n, val):
    for cb in range(NCB):
        pad[cb, pl.ds(pl.multiple_of(row0 + PADR, 8), n), :] = val[:, cb * LANES:(cb + 1) * LANES]


def _taps(pad_ref, cb, s0, weights):
    acc = jnp.zeros((CT, LANES), F32)
    for k in range(KW):
        acc = acc + weights[k] * pad_ref[cb, pl.ds(s0 + k + 1, CT), :]
    return acc


def conv_fwd(proj, conv_w, conv_b, ln_w, ln_b):
    def body(a_ref, b_ref, w_ref, cb_ref, lw_ref, lb_ref, c_ref, u3_ref, upad):
        _pad_zero(upad)

        def glu(i, _):
            rows = pl.ds(pl.multiple_of(i * TM, TM), TM)
            _pad_store(upad, i * TM, TM, a_ref[rows, :] * _sigmoid(b_ref[rows, :]))
            return 0

        lax.fori_loop(0, S // TM, glu, 0)

        def chunk(i, _):
            s0 = pl.multiple_of(i * CT, CT)
            for cb in range(CC // LANES):
                cols = slice(cb * LANES, (cb + 1) * LANES)
                w = [w_ref[k:k + 1, cols] for k in range(KW)]
                c_ref[pl.ds(s0, CT), cols] = _taps(upad, cb, s0, w) + cb_ref[:, cols]
            cv = c_ref[pl.ds(s0, CT), :]
            mu = jnp.mean(cv, axis=-1, keepdims=True)
            xc = cv - mu
            rstd = lax.rsqrt(jnp.mean(xc * xc, axis=-1, keepdims=True) + EPS)
            yl = xc * rstd * lw_ref[...] + lb_ref[...]
            u3_ref[pl.ds(s0, CT), :] = (yl * _sigmoid(yl)).astype(BF16)
            return 0

        lax.fori_loop(0, S // CT, chunk, 0)

    vec = pl.BlockSpec((1, CC), lambda i: (0, 0))
    full = pl.BlockSpec((S, CC), lambda i: (0, 0))
    return pl.pallas_call(
        body, name="conv_fwd", grid=(1,),
        in_specs=_conv_specs() + [pl.BlockSpec((KW, CC), lambda i: (0, 0)), vec, vec, vec],
        out_specs=[full, full],
        out_shape=[jax.ShapeDtypeStruct((S, CC), F32), jax.ShapeDtypeStruct((S, CC), BF16)],
        scratch_shapes=[pltpu.VMEM((NCB, S + 2 * PADR, LANES), F32)],
        compiler_params=_cp(dimension_semantics=("arbitrary",)),
    )(proj, proj, conv_w, conv_b, ln_w, ln_b)


def conv_bwd(proj, cpre, d_u3, conv_w, conv_w_rev, ln_w, ln_b):
    def body(a_ref, b_ref, c_ref, du3_ref, w_ref, wr_ref, lw_ref, lb_ref,
             dc_ref, gw_ref, gcb_ref, glw_ref, glb_ref, upad, dpad):
        _pad_zero(upad)
        _pad_zero(dpad)
        gw_ref[...] = jnp.zeros_like(gw_ref)

        def ln_bwd(i, carry):
            gcb, glw, glb = carry
            rows = pl.ds(pl.multiple_of(i * TM, TM), TM)
            _pad_store(upad, i * TM, TM, a_ref[rows, :] * _sigmoid(b_ref[rows, :]))
            cv = c_ref[rows, :]
            mu = jnp.mean(cv, axis=-1, keepdims=True)
            xc = cv - mu
            rstd = lax.rsqrt(jnp.mean(xc * xc, axis=-1, keepdims=True) + EPS)
            xh = xc * rstd
            yl = xh * lw_ref[...] + lb_ref[...]
            dyl = du3_ref[rows, :] * _dsilu(yl, _sigmoid(yl))
            dxh = dyl * lw_ref[...]
            dcv = rstd * (dxh - jnp.mean(dxh, axis=-1, keepdims=True)
                          - xh * jnp.mean(dxh * xh, axis=-1, keepdims=True))
            _pad_store(dpad, i * TM, TM, dcv)
            return (gcb + jnp.sum(dcv, axis=0, keepdims=True),
                    glw + jnp.sum(dyl * xh, axis=0, keepdims=True),
                    glb + jnp.sum(dyl, axis=0, keepdims=True))

        z = jnp.zeros((1, CC), F32)
        gcb, glw, glb = lax.fori_loop(0, S // TM, ln_bwd, (z, z, z))
        gcb_ref[...] = gcb
        glw_ref[...] = glw
        glb_ref[...] = glb

        def chunk(i, _):
            s0 = pl.multiple_of(i * CT, CT)
            for cb in range(CC // LANES):
                cols = slice(cb * LANES, (cb + 1) * LANES)
                wr = [wr_ref[k:k + 1, cols] for k in range(KW)]
                du = _taps(dpad, cb, s0, wr)
                dcv = dpad[cb, pl.ds(s0 + PADR, CT), :]
                for k in range(KW):
                    gw_ref[k:k + 1, cols] = gw_ref[k:k + 1, cols] + jnp.sum(
                        upad[cb, pl.ds(s0 + k + 1, CT), :] * dcv, axis=0, keepdims=True)
                av = a_ref[pl.ds(s0, CT), cols]
                sb = _sigmoid(b_ref[pl.ds(s0, CT), cols])
                dc_ref[pl.ds(s0, CT), cols] = (du * sb).astype(BF16)
                dc_ref[pl.ds(s0, CT), slice(CC + cb * LANES, CC + (cb + 1) * LANES)] = (
                    du * av * sb * (1.0 - sb)).astype(BF16)
            return 0

        lax.fori_loop(0, S // CT, chunk, 0)

    vec = pl.BlockSpec((1, CC), lambda i: (0, 0))
    full = pl.BlockSpec((S, CC), lambda i: (0, 0))
    wsp = pl.BlockSpec((KW, CC), lambda i: (0, 0))
    return pl.pallas_call(
        body, name="conv_bwd", grid=(1,),
        in_specs=_conv_specs() + [full, full, wsp, wsp, vec, vec],
        out_specs=[pl.BlockSpec((S, 2 * CC), lambda i: (0, 0)), wsp, vec, vec, vec],
        out_shape=[jax.ShapeDtypeStruct((S, 2 * CC), BF16), jax.ShapeDtypeStruct((KW, CC), F32)]
        + [jax.ShapeDtypeStruct((1, CC), F32)] * 3,
        scratch_shapes=[pltpu.VMEM((NCB, S + 2 * PADR, LANES), F32)] * 2,
        compiler_params=_cp(dimension_semantics=("arbitrary",)),
    )(proj, proj, cpre, d_u3, conv_w, conv_w_rev, ln_w, ln_b)


def _gate_specs():
    return [_row(CC, col=OFF_GA // CC + j) for j in range(4)]


def _gates(g_refs, bg_ref):
    ga = _sigmoid(jnp.concatenate([g_refs[0][...], g_refs[1][...]], axis=1) + bg_ref[0:1, :])
    gb = _sigmoid(jnp.concatenate([g_refs[2][...], g_refs[3][...]], axis=1) + bg_ref[1:2, :])
    return ga, gb


def mix_out(x, proj, b_gate, attn, u3, w_o, w_pw, w_out):
    def body(x_ref, g0, g1, g2, g3, bg_ref, at_ref, u3_ref, wo_ref, wp_ref, wout_ref,
             x1_ref, z_ref, ya_ref, yb_ref):
        ga, gb = _gates((g0, g1, g2, g3), bg_ref)
        ya = _dot(at_ref[...], wo_ref[...])
        yb = _dot(u3_ref[...], wp_ref[...])
        z = (ga * ya + gb * yb).astype(BF16)
        ya_ref[...] = ya.astype(BF16)
        yb_ref[...] = yb.astype(BF16)
        z_ref[...] = z
        x1_ref[...] = x_ref[...] + _dot(z, wout_ref[...])

    return pl.pallas_call(
        body, name="mix_out", grid=(S // TM,),
        in_specs=[_row(D)] + _gate_specs() + [_res((2, D)), _row(CC), _row(CC),
                                              _res((CC, D)), _res((CC, D)), _res((D, D))],
        out_specs=[_row(D)] * 4,
        out_shape=[jax.ShapeDtypeStruct((S, D), F32)] + [jax.ShapeDtypeStruct((S, D), BF16)] * 3,
        compiler_params=_cp(dimension_semantics=("arbitrary",)),
    )(x, proj, proj, proj, proj, b_gate, attn, u3, w_o, w_pw, w_out)


def out_bwd(d_x1b, proj, b_gate, ya, yb, w_o, w_pw, w_out):
    def body(dx_ref, g0, g1, g2, g3, bg_ref, ya_ref, yb_ref, wo_ref, wp_ref, wout_ref,
             dya_ref, dyb_ref, dgl_ref, dat_ref, du3_ref, gbg_ref):
        @pl.when(pl.program_id(0) == 0)
        def _():
            gbg_ref[...] = jnp.zeros_like(gbg_ref)

        ga, gb = _gates((g0, g1, g2, g3), bg_ref)
        dz = _dot_nt(dx_ref[...], wout_ref[...])
        dya = (dz * ga).astype(BF16)
        dyb = (dz * gb).astype(BF16)
        dgla = dz * ya_ref[...].astype(F32) * ga * (1.0 - ga)
        dglb = dz * yb_ref[...].astype(F32) * gb * (1.0 - gb)
        dya_ref[...] = dya
        dyb_ref[...] = dyb
        dgl_ref[:, 0:D] = dgla.astype(BF16)
        dgl_ref[:, D:2 * D] = dglb.astype(BF16)
        gbg_ref[0:1, :] = gbg_ref[0:1, :] + jnp.sum(dgla, axis=0, keepdims=True)
        gbg_ref[1:2, :] = gbg_ref[1:2, :] + jnp.sum(dglb, axis=0, keepdims=True)
        dat_ref[...] = _dot_nt(dya, wo_ref[...])
        du3_ref[...] = _dot_nt(dyb, wp_ref[...])

    return pl.pallas_call(
        body, name="out_bwd", grid=(S // TM,),
        in_specs=[_row(D)] + _gate_specs() + [_res((2, D)), _row(D), _row(D),
                                              _res((CC, D)), _res((CC, D)), _res((D, D))],
        out_specs=[_row(D), _row(D), _row(2 * D), _row(CC), _row(CC), pl.BlockSpec((2, D), lambda i: (0, 0))],
        out_shape=[jax.ShapeDtypeStruct((S, D), BF16)] * 2 + [jax.ShapeDtypeStruct((S, 2 * D), BF16)]
        + [jax.ShapeDtypeStruct((S, CC), F32)] * 2 + [jax.ShapeDtypeStruct((2, D), F32)],
        compiler_params=_cp(dimension_semantics=("arbitrary",)),
    )(d_x1b, proj, proj, proj, proj, b_gate, ya, yb, w_o, w_pw, w_out)


def ffn_in(x1, norm_w, w_ffn_in):
    half = FF // 2

    def body(x_ref, nw_ref, w_ref, h_ref, gu_ref, f_ref):
        xv = x_ref[...]
        r = lax.rsqrt(jnp.mean(xv * xv, axis=-1, keepdims=True) + EPS)
        h = (xv * r * nw_ref[...]).astype(BF16)
        h_ref[...] = h
        for j in range(2):
            gt = _dot(h, w_ref[:, j * half:(j + 1) * half])
            up = _dot(h, w_ref[:, FF + j * half:FF + (j + 1) * half])
            gu_ref[:, j * half:(j + 1) * half] = gt.astype(BF16)
            gu_ref[:, FF + j * half:FF + (j + 1) * half] = up.astype(BF16)
            f_ref[:, j * half:(j + 1) * half] = (gt * _sigmoid(gt) * up).astype(BF16)

    return pl.pallas_call(
        body, name="ffn_in", grid=(S // TM,),
        in_specs=[_row(D), _res((1, D)), _res((D, 2 * FF))],
        out_specs=[_row(D), _row(2 * FF), _row(FF)],
        out_shape=[jax.ShapeDtypeStruct((S, D), BF16), jax.ShapeDtypeStruct((S, 2 * FF), BF16),
                   jax.ShapeDtypeStruct((S, FF), BF16)],
        compiler_params=_cp(dimension_semantics=("arbitrary",)),
    )(x1, norm_w, w_ffn_in)


def ffn_out_loss(x1, f, w_ffn_out, target):
    def body(x_ref, f_ref, w_ref, t_ref, dy_ref, dyb_ref, sq_ref):
        @pl.when(pl.program_id(0) == 0)
        def _():
            sq_ref[...] = jnp.zeros_like(sq_ref)

        diff = x_ref[...] + _dot(f_ref[...], w_ref[...]) - t_ref[...]
        dy = diff * (1.0 / D)
        dy_ref[...] = dy
        dyb_ref[...] = dy.astype(BF16)
        sq_ref[...] = sq_ref[...] + jnp.sum((diff * diff).reshape(TM // 8, 8, D), axis=0)

    return pl.pallas_call(
        body, name="ffn_out_loss", grid=(S // TM,),
        in_specs=[_row(D), _row(FF), _res((FF, D)), _row(D)],
        out_specs=[_row(D), _row(D), pl.BlockSpec((8, D), lambda i: (0, 0))],
        out_shape=[jax.ShapeDtypeStruct((S, D), F32), jax.ShapeDtypeStruct((S, D), BF16),
                   jax.ShapeDtypeStruct((8, D), F32)],
        compiler_params=_cp(dimension_semantics=("arbitrary",)),
    )(x1, f, w_ffn_out, target)


def _rms_bwd(xv, nw, dh):
    r = lax.rsqrt(jnp.mean(xv * xv, axis=-1, keepdims=True) + EPS)
    xn = xv * r
    dxn = dh * nw
    dx = r * (dxn - xn * jnp.mean(dxn * xn, axis=-1, keepdims=True))
    return dx, dh * xn


def ffn_bwd(dy, dyb, gu, x1, norm_w, w_ffn_in, w_ffn_out):
    def body(dy_ref, dyb_ref, gu_ref, x_ref, nw_ref, wi_ref, wo_ref, dgu_ref, dx_ref, dxb_ref, gn_ref):
        @pl.when(pl.program_id(0) == 0)
        def _():
            gn_ref[...] = jnp.zeros_like(gn_ref)

        df = _dot_nt(dyb_ref[...], wo_ref[...])
        gt = gu_ref[:, 0:FF].astype(F32)
        up = gu_ref[:, FF:2 * FF].astype(F32)
        sg = _sigmoid(gt)
        dgt = (df * up * _dsilu(gt, sg)).astype(BF16)
        dup = (df * gt * sg).astype(BF16)
        dgu_ref[:, 0:FF] = dgt
        dgu_ref[:, FF:2 * FF] = dup
        dh = _dot_nt(dgt, wi_ref[:, 0:FF]) + _dot_nt(dup, wi_ref[:, FF:2 * FF])
        dxn, gw = _rms_bwd(x_ref[...], nw_ref[...], dh)
        dx = dy_ref[...] + dxn
        dx_ref[...] = dx
        dxb_ref[...] = dx.astype(BF16)
        gn_ref[...] = gn_ref[...] + jnp.sum(gw, axis=0, keepdims=True)

    return pl.pallas_call(
        body, name="ffn_bwd", grid=(S // TM,),
        in_specs=[_row(D), _row(D), _row(2 * FF), _row(D), _res((1, D)), _res((D, 2 * FF)), _res((FF, D))],
        out_specs=[_row(2 * FF), _row(D), _row(D), pl.BlockSpec((1, D), lambda i: (0, 0))],
        out_shape=[jax.ShapeDtypeStruct((S, 2 * FF), BF16), jax.ShapeDtypeStruct((S, D), F32),
                   jax.ShapeDtypeStruct((S, D), BF16), jax.ShapeDtypeStruct((1, D), F32)],
        compiler_params=_cp(dimension_semantics=("arbitrary",)),
    )(dy, dyb, gu, x1, norm_w, w_ffn_in, w_ffn_out)


def in_bwd(d_q, d_k, d_v, d_conv, d_gl, w_in, x, d_x1, norm_w):
    segs = ((OFF_Q, QKV), (OFF_K, QKV), (OFF_V, QKV), (OFF_CA, 2 * CC), (OFF_GA, 2 * D))

    def body(dq_ref, dk_ref, dv_ref, dc_ref, dg_ref, w_ref, x_ref, dx1_ref, nw_ref, gx_ref, gn_ref):
        @pl.when(pl.program_id(0) == 0)
        def _():
            gn_ref[...] = jnp.zeros_like(gn_ref)

        dh = jnp.zeros((TM, D), F32)
        for ref, (off, width) in zip((dq_ref, dk_ref, dv_ref, dc_ref, dg_ref), segs):
            dh = dh + _dot_nt(ref[...], w_ref[:, off:off + width])
        dxn, gw = _rms_bwd(x_ref[...], nw_ref[...], dh)
        gx_ref[...] = dx1_ref[...] + dxn
        gn_ref[...] = gn_ref[...] + jnp.sum(gw, axis=0, keepdims=True)

    return pl.pallas_call(
        body, name="in_bwd", grid=(S // TM,),
        in_specs=[_row(QKV)] * 3 + [_row(2 * CC), _row(2 * D), _res((D, INW)), _row(D), _row(D), _res((1, D))],
        out_specs=[_row(D), pl.BlockSpec((1, D), lambda i: (0, 0))],
        out_shape=[jax.ShapeDtypeStruct((S, D), F32), jax.ShapeDtypeStruct((1, D), F32)],
        compiler_params=_cp(dimension_semantics=("arbitrary",)),
    )(d_q, d_k, d_v, d_conv, d_gl, w_in, x, d_x1, norm_w)


def mm_tn(name, a, b, tm, tn):
    M, N = a.shape[1], b.shape[1]

    def body(a_ref, b_ref, o_ref):
        o_ref[...] = _dot_tn(a_ref[...], b_ref[...])

    return pl.pallas_call(
        body, name=name, grid=(M // tm, N // tn),
        in_specs=[pl.BlockSpec((S, tm), lambda i, j: (0, i)), pl.BlockSpec((S, tn), lambda i, j: (0, j))],
        out_specs=pl.BlockSpec((tm, tn), lambda i, j: (i, j)),
        out_shape=jax.ShapeDtypeStruct((M, N), F32),
        compiler_params=_cp(dimension_semantics=("arbitrary", "arbitrary")),
    )(a, b)


def local_step(x, pos_col, target, p):
    tabs = rope_tables(pos_col)
    qw2 = jnp.tile(p["q_norm_w"], (1, 2))
    kw2 = jnp.tile(p["k_norm_w"], (1, 2))

    h, proj = in_proj(x, p["norm1_w"], p["w_in"])
    attn, lse = attn_fwd(proj, tabs, qw2, kw2)
    cpre, u3 = conv_fwd(proj, p["conv_w"], p["conv_b"], p["conv_ln_w"], p["conv_ln_b"])
    x1, z, ya, yb = mix_out(x, proj, p["b_gate"], attn, u3, p["w_o_attn"], p["w_pw_conv"], p["w_out"])
    h2, gu, f = ffn_in(x1, p["norm2_w"], p["w_ffn_in"])
    dy, dyb, sq = ffn_out_loss(x1, f, p["w_ffn_out"], target)

    g = {}
    g["w_ffn_out"] = mm_tn("gw_ffn_out", f, dyb, FF // 2, D)
    d_gu, d_x1, d_x1b, g["norm2_w"] = ffn_bwd(dy, dyb, gu, x1, p["norm2_w"], p["w_ffn_in"], p["w_ffn_out"])
    g["w_ffn_in"] = mm_tn("gw_ffn_in", h2, d_gu, D // 2, FF // 2)
    g["w_out"] = mm_tn("gw_out", z, d_x1b, D // 2, D)
    d_ya, d_yb, d_gl, d_attn, d_u3, g["b_gate"] = out_bwd(
        d_x1b, proj, p["b_gate"], ya, yb, p["w_o_attn"], p["w_pw_conv"], p["w_out"])
    g["w_o_attn"] = mm_tn("gw_o_attn", attn, d_ya, CC, D)
    g["w_pw_conv"] = mm_tn("gw_pw_conv", u3, d_yb, CC, D)
    d_conv, g["conv_w"], g["conv_b"], g["conv_ln_w"], g["conv_ln_b"] = conv_bwd(
        proj, cpre, d_u3, p["conv_w"], p["conv_w"][::-1], p["conv_ln_w"], p["conv_ln_b"])
    d_q, d_k, d_v, gqw, gkw = attn_bwd(proj, tabs, qw2, kw2, d_attn, attn, lse)
    g["q_norm_w"] = gqw[0:1, 0:HD] + gqw[0:1, HD:LANES]
    g["k_norm_w"] = gkw[0:1, 0:HD] + gkw[0:1, HD:LANES]
    g["w_in"] = jnp.concatenate([
        mm_tn("gw_in_q", h, d_q, D // 2, QKV), mm_tn("gw_in_k", h, d_k, D // 2, QKV),
        mm_tn("gw_in_v", h, d_v, D // 2, QKV), mm_tn("gw_in_c", h, d_conv, D // 2, 2 * CC),
        mm_tn("gw_in_g", h, d_gl, D // 2, 2 * D)], axis=1)
    grad_x, g["norm1_w"] = in_bwd(d_q, d_k, d_v, d_conv, d_gl, p["w_in"], x, d_x1, p["norm1_w"])
    return sq, grad_x, g


ANY = pl.BlockSpec(memory_space=pl.ANY)
VMEM = pl.BlockSpec(memory_space=pltpu.VMEM)


def _place():
    x, y, c = lax.axis_index("x"), lax.axis_index("y"), lax.axis_index("c")
    chips = [(1 - x, y), (x, 1 - y), (1 - x, 1 - y)]
    return x, y, c, chips


def all_gather(shards, out_dtypes):
    n = len(shards)

    def body(*refs):
        ins, outs, stage = refs[:n], refs[n:2 * n], refs[2 * n:3 * n]
        send_sems, recv_sems, local_sems = refs[3 * n:]
        x, y, c, chips = _place()
        me, sib = (x, y, c), (x, y, 1 - c)

        def blk(px, py, pc):
            return 4 * px + 2 * py + pc

        def copy(t, k, block, to, src=None):
            dst = outs[t].at[blk(*block)]
            return pltpu.make_async_remote_copy(
                src_ref=dst if src is None else src, dst_ref=dst,
                send_sem=send_sems.at[7 * t + k], recv_sem=recv_sems.at[7 * t + k],
                device_id=to, device_id_type=MESH)

        for t in range(n):
            rows = ins[t].shape[0]
            step = 64 if rows % 64 == 0 else rows
            for r0 in range(0, rows, step):
                stage[t][r0:r0 + step, :] = ins[t][r0:r0 + step, :].astype(stage[t].dtype)

        local, sent = [], []
        for t in range(n):
            mine = pltpu.make_async_copy(stage[t], outs[t].at[blk(*me)], local_sems.at[t])
            mine.start()
            local.append(mine)
            first = [copy(t, 0, me, sib, src=stage[t])]
            first += [copy(t, 1 + j, me, (*chip, c), src=stage[t]) for j, chip in enumerate(chips)]
            for cp in first:
                cp.start()
            sent += first
        for t in range(n):
            for j, chip in enumerate(chips):
                copy(t, 1 + j, (*chip, c), me).wait_recv()
                fwd = copy(t, 4 + j, (*chip, c), sib)
                fwd.start()
                sent.append(fwd)
        for t in range(n):
            copy(t, 0, sib, me).wait_recv()
            for j, chip in enumerate(chips):
                copy(t, 4 + j, (*chip, 1 - c), me).wait_recv()
        for cp in sent:
            cp.wait_send()
        for cp in local:
            cp.wait()

    return pl.pallas_call(
        body, name="all_gather_weights",
        in_specs=[VMEM] * n, out_specs=[ANY] * n,
        out_shape=[jax.ShapeDtypeStruct((NDEV,) + s.shape, dt) for s, dt in zip(shards, out_dtypes)],
        scratch_shapes=[pltpu.VMEM(s.shape, dt) for s, dt in zip(shards, out_dtypes)]
        + [pltpu.SemaphoreType.DMA((7 * n,)), pltpu.SemaphoreType.DMA((7 * n,)), pltpu.SemaphoreType.DMA((n,))],
        compiler_params=_cp(),
    )(*shards)


def rs_sibling(grads):
    n = len(grads)

    def body(*refs):
        ins, outs = refs[:n], refs[n:2 * n]
        send_sems, recv_sems = refs[2 * n:]
        x, y, c, _ = _place()
        copies = []
        for t in range(n):
            for k in range(4):
                cp = pltpu.make_async_remote_copy(
                    src_ref=ins[t].at[2 * k + 1 - c], dst_ref=outs[t].at[k],
                    send_sem=send_sems.at[4 * t + k], recv_sem=recv_sems.at[4 * t + k],
                    device_id=(x, y, 1 - c), device_id_type=MESH)
                cp.start()
                copies.append(cp)
        for cp in copies:
            cp.wait()

    return pl.pallas_call(
        body, name="rs_sibling", in_specs=[ANY] * n, out_specs=[ANY] * n,
        out_shape=[jax.ShapeDtypeStruct((4,) + g.shape[1:], F32) for g in grads],
        scratch_shapes=[pltpu.SemaphoreType.DMA((4 * n,)), pltpu.SemaphoreType.DMA((4 * n,))],
    )(*grads)


def _row_tiles(rows):
    return 4 if rows % 64 == 0 and rows >= 512 else (2 if rows % 32 == 0 and rows >= 256 else 1)


def chip_sum(name, grad, recv, c_idx, chip_idx):
    _, R, C = grad.shape
    nt = _row_tiles(R)
    tr = R // nt

    def body(s_ref, g_ref, r_ref, p_ref, own_ref):
        k = pl.program_id(1)
        tot = g_ref[0] + r_ref[0]
        p_ref[0] = tot.astype(BF16)

        @pl.when(k == s_ref[1])
        def _():
            own_ref[...] = tot

    grid_spec = pltpu.PrefetchScalarGridSpec(
        num_scalar_prefetch=1, grid=(nt, 4),
        in_specs=[pl.BlockSpec((1, tr, C), lambda i, k, s: (2 * k + s[0], i, 0)),
                  pl.BlockSpec((1, tr, C), lambda i, k, s: (k, i, 0))],
        out_specs=[pl.BlockSpec((1, tr, C), lambda i, k, s: (k, i, 0)),
                   pl.BlockSpec((tr, C), lambda i, k, s: (i, 0))])
    return pl.pallas_call(
        body, name=name, grid_spec=grid_spec,
        out_shape=[jax.ShapeDtypeStruct((4, R, C), BF16), jax.ShapeDtypeStruct((R, C), F32)],
        compiler_params=_cp(dimension_semantics=("arbitrary", "arbitrary")),
    )(jnp.stack([c_idx, chip_idx]), grad, recv)


def rs_chips(parts):
    n = len(parts)

    def body(*refs):
        ins, outs = refs[:n], refs[n:2 * n]
        send_sems, recv_sems = refs[2 * n:]
        x, y, c, chips = _place()
        copies = []
        for t in range(n):
            for j, (cx, cy) in enumerate(chips):
                cp = pltpu.make_async_remote_copy(
                    src_ref=ins[t].at[2 * cx + cy], dst_ref=outs[t].at[j],
                    send_sem=send_sems.at[3 * t + j], recv_sem=recv_sems.at[3 * t + j],
                    device_id=(cx, cy, c), device_id_type=MESH)
                cp.start()
                copies.append(cp)
        for cp in copies:
            cp.wait()

    return pl.pallas_call(
        body, name="rs_chips", in_specs=[ANY] * n, out_specs=[ANY] * n,
        out_shape=[jax.ShapeDtypeStruct((3,) + p.shape[1:], BF16) for p in parts],
        scratch_shapes=[pltpu.SemaphoreType.DMA((3 * n,)), pltpu.SemaphoreType.DMA((3 * n,))],
    )(*parts)


def _adamw(w, g, m, v):
    m2 = ADAM_B1 * m + (1.0 - ADAM_B1) * g
    v2 = ADAM_B2 * v + (1.0 - ADAM_B2) * (g * g)
    m_hat = m2 / (1.0 - ADAM_B1 ** ADAM_STEP)
    v_hat = v2 / (1.0 - ADAM_B2 ** ADAM_STEP)
    delta = -ADAM_LR * (m_hat / (jnp.sqrt(v_hat) + ADAM_EPS) + ADAM_WD * w)
    return delta, m2, v2


def shard_adam(name, own, recv, w, m, v):
    R, C = own.shape
    nt = _row_tiles(R)
    tr = R // nt

    def body(o_ref, r_ref, w_ref, m_ref, v_ref, g_ref, d_ref, nm_ref, nv_ref):
        g = o_ref[...] + r_ref[0].astype(F32) + r_ref[1].astype(F32) + r_ref[2].astype(F32)
        delta, m2, v2 = _adamw(w_ref[...], g, m_ref[...], v_ref[...])
        g_ref[...] = g
        d_ref[...] = delta
        nm_ref[...] = m2
        nv_ref[...] = v2

    tile = pl.BlockSpec((tr, C), lambda i: (i, 0))
    return pl.pallas_call(
        body, name=name, grid=(nt,),
        in_specs=[tile, pl.BlockSpec((3, tr, C), lambda i: (0, i, 0)), tile, tile, tile],
        out_specs=[tile] * 4, out_shape=[jax.ShapeDtypeStruct((R, C), F32)] * 4,
        compiler_params=_cp(dimension_semantics=("arbitrary",)),
    )(own, recv, w, m, v)


ROW_N1, ROW_N2, ROW_BG, ROW_QN, ROW_KN, ROW_CB, ROW_LW, ROW_LB, ROW_CW = 0, 1, 2, 4, 5, 6, 7, 8, 9
PACK_ROWS = 40
SMALL = ("norm1_w", "norm2_w", "b_gate", "q_norm_w", "k_norm_w", "conv_b", "conv_ln_w", "conv_ln_b", "conv_w")


def small_sync_adam(g, w, m, v):
    ns = len(SMALL)

    def body(*refs):
        gi = dict(zip(SMALL, refs[:ns]))
        wi = dict(zip(SMALL, refs[ns:2 * ns]))
        mi = dict(zip(SMALL, refs[2 * ns:3 * ns]))
        vi = dict(zip(SMALL, refs[3 * ns:4 * ns]))
        outs = refs[4 * ns:8 * ns]
        pack, recv, tot, send_sems, recv_sems = refs[8 * ns:]
        x, y, c, _ = _place()
        me = 4 * x + 2 * y + c

        pack[...] = jnp.zeros_like(pack)
        pack[ROW_N1:ROW_N1 + 1, :] = gi["norm1_w"][...]
        pack[ROW_N2:ROW_N2 + 1, :] = gi["norm2_w"][...]
        pack[ROW_BG:ROW_BG + 2, :] = gi["b_gate"][...]
        pack[ROW_QN:ROW_QN + 1, 0:HD] = gi["q_norm_w"][...]
        pack[ROW_KN:ROW_KN + 1, 0:HD] = gi["k_norm_w"][...]
        pack[ROW_CB:ROW_CB + 1, 0:CC] = gi["conv_b"][...]
        pack[ROW_LW:ROW_LW + 1, 0:CC] = gi["conv_ln_w"][...]
        pack[ROW_LB:ROW_LB + 1, 0:CC] = gi["conv_ln_b"][...]
        pack[ROW_CW:ROW_CW + KW, 0:CC] = gi["conv_w"][...]

        copies = []
        for k in range(1, NDEV):
            peer = (x ^ (k >> 2), y ^ ((k >> 1) & 1), c ^ (k & 1))
            cp = pltpu.make_async_remote_copy(
                src_ref=pack, dst_ref=recv.at[me], send_sem=send_sems.at[k - 1], recv_sem=recv_sems.at[k - 1],
                device_id=peer, device_id_type=MESH)
            cp.start()
            copies.append(cp)
        recv[me] = pack[...]
        for cp in copies:
            cp.wait()
        acc = recv[0]
        for p in range(1, NDEV):
            acc = acc + recv[p]
        tot[...] = acc

        def shard_grad(name):
            if name == "b_gate":
                return tot[ROW_BG:ROW_BG + 2, pl.ds(pl.multiple_of(me * LANES, LANES), LANES)]
            if name == "conv_w":
                win = tot[ROW_CW:ROW_CW + KW, pl.ds(pl.multiple_of((me // 2) * LANES, LANES), LANES)]
                return jnp.where(me % 2 == 1, win[:, HD:LANES], win[:, 0:HD])
            row = {"norm1_w": ROW_N1, "norm2_w": ROW_N2, "q_norm_w": ROW_QN, "k_norm_w": ROW_KN,
                   "conv_b": ROW_CB, "conv_ln_w": ROW_LW, "conv_ln_b": ROW_LB}[name]
            return tot[row:row + 1, 0:wi[name].shape[1]]

        for i, name in enumerate(SMALL):
            gr = shard_grad(name)
            delta, m2, v2 = _adamw(wi[name][...], gr, mi[name][...], vi[name][...])
            outs[4 * i][...] = gr
            outs[4 * i + 1][...] = delta
            outs[4 * i + 2][...] = m2
            outs[4 * i + 3][...] = v2

    out_shape = []
    for name in SMALL:
        out_shape += [jax.ShapeDtypeStruct(w[name].shape, F32)] * 4
    args = [g[k] for k in SMALL] + [w[k] for k in SMALL] + [m[k] for k in SMALL] + [v[k] for k in SMALL]
    res = pl.pallas_call(
        body, name="small_sync_adam", in_specs=[VMEM] * len(args), out_specs=[VMEM] * len(out_shape),
        out_shape=out_shape,
        scratch_shapes=[pltpu.VMEM((PACK_ROWS, D), F32), pltpu.VMEM((NDEV, PACK_ROWS, D), F32),
                        pltpu.VMEM((PACK_ROWS, D), F32),
                        pltpu.SemaphoreType.DMA((NDEV - 1,)), pltpu.SemaphoreType.DMA((NDEV - 1,))],
    )(*args)
    return {name: tuple(res[4 * i:4 * i + 4]) for i, name in enumerate(SMALL)}


MATS = ("w_in", "w_o_attn", "w_pw_conv", "w_out", "w_ffn_in", "w_ffn_out")
COL_SHARDED = ("w_in", "w_o_attn", "w_pw_conv", "w_ffn_in")
WEIGHTS = ("norm1_w", "w_in", "b_gate", "q_norm_w", "k_norm_w", "w_o_attn", "conv_w", "conv_b", "conv_ln_w",
           "conv_ln_b", "w_pw_conv", "w_out", "norm2_w", "w_ffn_in", "w_ffn_out")


def _from_blocks(name, blocks):
    n, R, C = blocks.shape
    if name in COL_SHARDED or name in ("b_gate", "conv_w"):
        return blocks.transpose(1, 0, 2).reshape(R, n * C)
    return blocks.reshape(n * R, C)


def _to_blocks(name, full):
    if name in COL_SHARDED:
        R, C = full.shape
        return full.reshape(R, NDEV, C // NDEV).transpose(1, 0, 2)
    return full.reshape(NDEV, full.shape[0] // NDEV, full.shape[1])


def kernel(x, positions, norm1_w, w_in, b_gate, q_norm_w, k_norm_w, w_o_attn, conv_w, conv_b, conv_ln_w, conv_ln_b, w_pw_conv, w_out, norm2_w, w_ffn_in, w_ffn_out, loss_target, m_norm1_w, m_w_in, m_b_gate, m_q_norm_w, m_k_norm_w, m_w_o_attn, m_conv_w, m_conv_b, m_conv_ln_w, m_conv_ln_b, m_w_pw_conv, m_w_out, m_norm2_w, m_w_ffn_in, m_w_ffn_out, v_norm1_w, v_w_in, v_b_gate, v_q_norm_w, v_k_norm_w, v_w_o_attn, v_conv_w, v_conv_b, v_conv_ln_w, v_conv_ln_b, v_w_pw_conv, v_w_out, v_norm2_w, v_w_ffn_in, v_w_ffn_out):
    w = dict(norm1_w=norm1_w, w_in=w_in, b_gate=b_gate, q_norm_w=q_norm_w, k_norm_w=k_norm_w, w_o_attn=w_o_attn,
             conv_w=conv_w, conv_b=conv_b, conv_ln_w=conv_ln_w, conv_ln_b=conv_ln_b, w_pw_conv=w_pw_conv,
             w_out=w_out, norm2_w=norm2_w, w_ffn_in=w_ffn_in, w_ffn_out=w_ffn_out)
    m = dict(norm1_w=m_norm1_w, w_in=m_w_in, b_gate=m_b_gate, q_norm_w=m_q_norm_w, k_norm_w=m_k_norm_w,
             w_o_attn=m_w_o_attn, conv_w=m_conv_w, conv_b=m_conv_b, conv_ln_w=m_conv_ln_w,
             conv_ln_b=m_conv_ln_b, w_pw_conv=m_w_pw_conv, w_out=m_w_out, norm2_w=m_norm2_w,
             w_ffn_in=m_w_ffn_in, w_ffn_out=m_w_ffn_out)
    v = dict(norm1_w=v_norm1_w, w_in=v_w_in, b_gate=v_b_gate, q_norm_w=v_q_norm_w, k_norm_w=v_k_norm_w,
             w_o_attn=v_w_o_attn, conv_w=v_conv_w, conv_b=v_conv_b, conv_ln_w=v_conv_ln_w,
             conv_ln_b=v_conv_ln_b, w_pw_conv=v_w_pw_conv, w_out=v_w_out, norm2_w=v_norm2_w,
             w_ffn_in=v_w_ffn_in, w_ffn_out=v_w_ffn_out)
    two_d = lambda t: {k: (a[0] if a.ndim == 3 else a) for k, a in t.items()}
    w, m, v = two_d(w), two_d(m), two_d(v)

    gathered_names = MATS + ("b_gate", "conv_w")
    gathered = all_gather([w[k] for k in gathered_names], [BF16] * len(MATS) + [F32, F32])
    p = {k: _from_blocks(k, blocks) for k, blocks in zip(gathered_names, gathered)}
    for k in ("norm1_w", "q_norm_w", "k_norm_w", "conv_b", "conv_ln_w", "conv_ln_b", "norm2_w"):
        p[k] = w[k]

    sq, grad_x, g = local_step(x[0], positions.reshape(S, 1), loss_target[0], p)
    loss = lax.psum((0.5 / D) * jnp.sum(sq), ("x", "y", "c"))

    c_idx = lax.axis_index("c").astype(jnp.int32)
    chip_idx = (2 * lax.axis_index("x") + lax.axis_index("y")).astype(jnp.int32)
    blocks = [_to_blocks(k, g[k]) for k in MATS]
    from_sibling = rs_sibling(blocks)
    parts, owns = zip(*[chip_sum("chip_sum_" + k, b, r, c_idx, chip_idx)
                        for k, b, r in zip(MATS, blocks, from_sibling)])
    from_chips = rs_chips(list(parts))
    res = {k: tuple(shard_adam("adam_" + k, own, r, w[k], m[k], v[k]))
           for k, own, r in zip(MATS, owns, from_chips)}
    res.update(small_sync_adam(g, w, m, v))

    def shaped(name, a):
        return a.reshape((1,) + a.shape) if name in MATS or name in ("b_gate", "conv_w") else a

    outs = [loss, grad_x.reshape(1, S, D)]
    for i in range(4):
        outs += [shaped(k, res[k][i]) for k in WEIGHTS]
    return tuple(outs)
```

```python
import functools

import numpy as np
import jax
import jax.numpy as jnp
from jax import lax
from jax.experimental import pallas as pl
from jax.experimental.pallas import tpu as pltpu

F32 = jnp.float32
BF16 = jnp.bfloat16

S = 2048
D = 1024
HD = 64
QKV = 1536
CC = 512
KW = 31
FF = 2816
INW = 7680
OFF_Q, OFF_K, OFF_V, OFF_CA, OFF_CB, OFF_GA, OFF_GB = 0, 1536, 3072, 4608, 5120, 5632, 6656
DILATIONS = (1, 4, 16)
HALF_SPAN = 64
EPS = 1e-6
NEG_INF = -1e30
ROPE_THETA = 500000.0
ROT_DIM = 16

ADAM_LR = 0.001
ADAM_B1 = 0.9
ADAM_B2 = 0.999
ADAM_EPS = 1e-08
ADAM_WD = 0.01
ADAM_STEP = 10

NDEV = 8
LANES = 128
TM = 256
TQ = 128
VMEM_LIMIT = 56 * 1024 * 1024
MESH = pl.DeviceIdType.MESH


def _cp(**kw):
    return pltpu.CompilerParams(vmem_limit_bytes=VMEM_LIMIT, **kw)


def _row(width, col=0, tm=TM):
    return pl.BlockSpec((tm, width), lambda i: (i, col))


def _res(shape):
    nd = len(shape)
    return pl.BlockSpec(shape, lambda *_: (0,) * nd, pipeline_mode=pl.Buffered(1))


def _dot(a, b):
    return jnp.dot(a, b, preferred_element_type=F32)


def _dot_nt(a, b):
    return lax.dot_general(a, b, (((1,), (1,)), ((), ())), preferred_element_type=F32)


def _dot_tn(a, b):
    return lax.dot_general(a, b, (((0,), (0,)), ((), ())), preferred_element_type=F32)


def _sigmoid(x):
    return jax.nn.sigmoid(x)


def _dsilu(x, sg):
    return sg * (1.0 + x * (1.0 - sg))


def _inv_freq_lanes():
    inv = np.float32(ROPE_THETA) ** (-np.arange(0, ROT_DIM, 2, dtype=np.float32) / np.float32(ROT_DIM))
    lane = np.arange(LANES) % HD
    out = np.where(lane < ROT_DIM, inv[lane % (ROT_DIM // 2)], 0.0).astype(np.float32)
    return jnp.asarray(out.reshape(1, LANES))


def rope_tables(pos_col):
    def body(p_ref, f_ref, c_ref, s1_ref, s2_ref):
        ang = p_ref[...].astype(F32) * f_ref[...]
        lane = lax.broadcasted_iota(jnp.int32, ang.shape, 1) % HD
        cs = jnp.cos(ang)
        sn = jnp.sin(ang)
        c_ref[...] = jnp.where(lane < ROT_DIM, cs, 1.0)
        s1_ref[...] = jnp.where(lane < ROT_DIM // 2, -sn, 0.0)
        s2_ref[...] = jnp.where(lane < ROT_DIM // 2, 0.0, jnp.where(lane < ROT_DIM, sn, 0.0))

    sds = jax.ShapeDtypeStruct((S, LANES), F32)
    return pl.pallas_call(
        body, name="rope_tables", grid=(S // TM,),
        in_specs=[_row(1), pl.BlockSpec((1, LANES), lambda i: (0, 0))],
        out_specs=[_row(LANES)] * 3, out_shape=[sds] * 3,
    )(pos_col, _inv_freq_lanes())


def _rope(v, c, s1, s2):
    return v * c + pltpu.roll(v, LANES - 8, axis=1) * s1 + pltpu.roll(v, 8, axis=1) * s2


def _rope_t(d, c, s1, s2):
    return d * c - pltpu.roll(d, LANES - 8, axis=1) * s1 - pltpu.roll(d, 8, axis=1) * s2


def _head_mat():
    r = lax.broadcasted_iota(jnp.int32, (LANES, LANES), 0) // HD
    c = lax.broadcasted_iota(jnp.int32, (LANES, LANES), 1) // HD
    return jnp.where(r == c, 1.0 / HD, 0.0).astype(BF16)


def _head_mean(t, e):
    hi = t.astype(BF16)
    rest = (t - hi.astype(F32)).astype(BF16)
    return _dot(hi, e) + _dot(rest, e)


def in_proj(x, norm_w, w_in):
    nchunk = 5
    cw = INW // nchunk

    def body(x_ref, nw_ref, w_ref, h_ref, p_ref):
        xv = x_ref[...]
        r = lax.rsqrt(jnp.mean(xv * xv, axis=-1, keepdims=True) + EPS)
        h = (xv * r * nw_ref[...]).astype(BF16)
        h_ref[...] = h
        for j in range(nchunk):
            p_ref[:, j * cw:(j + 1) * cw] = _dot(h, w_ref[:, j * cw:(j + 1) * cw])

    return pl.pallas_call(
        body, name="in_proj", grid=(S // TM,),
        in_specs=[_row(D), _res((1, D)), _res((D, INW))],
        out_specs=[_row(D), _row(INW)],
        out_shape=[jax.ShapeDtypeStruct((S, D), BF16), jax.ShapeDtypeStruct((S, INW), F32)],
        compiler_params=_cp(dimension_semantics=("arbitrary",)),
    )(x, norm_w, w_in)


def _qk_specs():
    nb = QKV // LANES
    return [pl.BlockSpec((S, LANES), functools.partial(lambda hp, g, o: (0, o + g * 4 + hp), o=o))
            for o in (OFF_Q // LANES, OFF_K // LANES, OFF_V // LANES)]


def _tab_specs():
    return [pl.BlockSpec((S, LANES), lambda hp, g: (0, 0), pipeline_mode=pl.Buffered(1))] * 3


def _vec_spec():
    return pl.BlockSpec((1, LANES), lambda hp, g: (0, 0))


def _sub_rows(r, d, start, n):
    if d == 1:
        return pl.ds(start, n)
    return pl.ds(r + d * start, n, stride=d)


def _band_window(i, L):
    W = min(2 * TQ, L)
    q0 = pl.multiple_of(i * TQ, TQ)
    k0 = pl.multiple_of(jnp.clip(q0 - HALF_SPAN, 0, L - W), HALF_SPAN)
    qpos = q0 + (lax.broadcasted_iota(jnp.int32, (2 * TQ, W), 0) & (TQ - 1))
    kpos = k0 + lax.broadcasted_iota(jnp.int32, (2 * TQ, W), 1)
    valid = jnp.abs(qpos - kpos) <= HALF_SPAN
    return W, q0, k0, valid


def _stack_heads(t, lo):
    z = jnp.zeros_like(t)
    return jnp.concatenate([jnp.where(lo, t, z), jnp.where(lo, z, t)], axis=0)


def _unstack_heads(t2, lo):
    return jnp.where(lo, t2[0:TQ], t2[TQ:2 * TQ])


CHAINS = 4


def _interleave(d):
    ru = min(d, CHAINS)
    return ru, min(CHAINS // ru, S // d // TQ)


def _for_blocks(n, fn):
    if n == 1:
        fn(0)
    else:
        def it(j, _):
            fn(j)
            return 0
        lax.fori_loop(0, n, it, 0)


def attn_fwd(proj, tabs, qw2, kw2):
    CH = 256

    def body(q_ref, k_ref, v_ref, c_ref, s1_ref, s2_ref, qw_ref, kw_ref, at_ref, ls_ref,
             qs, ks, vs, osub, lsub, onat, lnat):
        g = pl.program_id(1)
        lo = lax.broadcasted_iota(jnp.int32, (1, LANES), 1) < HD
        e = _head_mat()

        def prep(t, w, c, s1, s2):
            r = lax.rsqrt(_head_mean(t * t, e) + EPS)
            return _rope(t * r * w, c, s1, s2)

        def group(gi, d):
            L = S // d

            ru, nb = _interleave(d)

            def stage(r, off):
                for c0 in range(0, L, CH):
                    n = min(CH, L)
                    rows = _sub_rows(r, d, c0, n)
                    c, s1, s2 = c_ref[rows, :], s1_ref[rows, :], s2_ref[rows, :]
                    dst = pl.ds(off + c0, n)
                    qs[dst, :] = (prep(q_ref[rows, :], qw_ref[...], c, s1, s2) * (HD ** -0.5)).astype(BF16)
                    ks[dst, :] = prep(k_ref[rows, :], kw_ref[...], c, s1, s2).astype(BF16)
                    vs[dst, :] = v_ref[rows, :].astype(BF16)

            def one(off, i):
                W, q0, k0, valid = _band_window(i, L)
                q2 = _stack_heads(qs[pl.ds(off + q0, TQ), :], lo)
                sc = jnp.where(valid, _dot_nt(q2, ks[pl.ds(off + k0, W), :]), NEG_INF)
                m = jnp.max(sc, axis=-1, keepdims=True)
                p = jnp.exp(sc - m)
                den = jnp.sum(p, axis=-1, keepdims=True)
                o2 = _dot(p.astype(BF16), vs[pl.ds(off + k0, W), :]) / den
                l2 = jnp.broadcast_to(m + jnp.log(den), (2 * TQ, LANES))
                osub[pl.ds(off + q0, TQ), :] = _unstack_heads(o2, lo)
                lsub[pl.ds(off + q0, TQ), :] = _unstack_heads(l2, lo)

            def unstage(r, off):
                for c0 in range(0, L, CH):
                    n = min(CH, L)
                    rows = _sub_rows(r, d, c0, n)
                    onat[gi, rows, :] = osub[pl.ds(off + c0, n), :]
                    lnat[gi, rows, :] = lsub[pl.ds(off + c0, n), :]

            def step(t, _):
                for u in range(ru):
                    stage(t * ru + u, u * L)
                _for_blocks(L // TQ // nb, lambda j: [one(u * L, j * nb + b) for u in range(ru) for b in range(nb)])
                for u in range(ru):
                    unstage(t * ru + u, u * L)
                return 0

            lax.fori_loop(0, d // ru, step, 0)

        for gi, d in enumerate(DILATIONS):
            pl.when(g == gi)(functools.partial(group, gi, d))

        @pl.when(g == len(DILATIONS) - 1)
        def _():
            def mix(i, _):
                rows = pl.ds(pl.multiple_of(i * CH, CH), CH)
                l0, l1, l2 = lnat[0, rows, :], lnat[1, rows, :], lnat[2, rows, :]
                m = jnp.maximum(jnp.maximum(l0, l1), l2)
                e0, e1, e2 = jnp.exp(l0 - m), jnp.exp(l1 - m), jnp.exp(l2 - m)
                den = e0 + e1 + e2
                a = (e0 * onat[0, rows, :] + e1 * onat[1, rows, :] + e2 * onat[2, rows, :]) / den
                at_ref[rows, :] = a.astype(BF16)
                ls_ref[rows, :] = m + jnp.log(den)
                return 0

            lax.fori_loop(0, S // CH, mix, 0)

    out_spec = pl.BlockSpec((S, LANES), lambda hp, g: (0, hp))
    return pl.pallas_call(
        body, name="attn_fwd", grid=(4, 3),
        in_specs=_qk_specs() + _tab_specs() + [_vec_spec(), _vec_spec()],
        out_specs=[out_spec, out_spec],
        out_shape=[jax.ShapeDtypeStruct((S, CC), BF16), jax.ShapeDtypeStruct((S, CC), F32)],
        scratch_shapes=[pltpu.VMEM((S, LANES), BF16)] * 3 + [pltpu.VMEM((S, LANES), F32)] * 2
        + [pltpu.VMEM((3, S, LANES), F32)] * 2,
        compiler_params=_cp(dimension_semantics=("arbitrary", "arbitrary")),
    )(proj, proj, proj, *tabs, qw2, kw2)


def attn_bwd(proj, tabs, qw2, kw2, d_attn, attn, lse):
    CH = 256

    def body(q_ref, k_ref, v_ref, c_ref, s1_ref, s2_ref, qw_ref, kw_ref, do_ref, at_ref, ls_ref,
             dq_ref, dk_ref, dv_ref, gqw_ref, gkw_ref,
             qs, ks, vs, dos, dsub, lsub, dqs, dks, dvs, dnat, dqn, dkn, dvn):
        hp, g = pl.program_id(0), pl.program_id(1)
        lo = lax.broadcasted_iota(jnp.int32, (1, LANES), 1) < HD
        e = _head_mat()

        @pl.when((hp == 0) & (g == 0))
        def _():
            gqw_ref[...] = jnp.zeros_like(gqw_ref)
            gkw_ref[...] = jnp.zeros_like(gkw_ref)

        def dsum(i, _):
            rows = pl.ds(pl.multiple_of(i * CH, CH), CH)
            dnat[rows, :] = _head_mean(do_ref[rows, :] * at_ref[rows, :].astype(F32), e) * float(HD)
            return 0

        lax.fori_loop(0, S // CH, dsum, 0)

        def group(d):
            L = S // d

            ru, nb = _interleave(d)

            def stage(r, off):
                for c0 in range(0, L, CH):
                    n = min(CH, L)
                    rows = _sub_rows(r, d, c0, n)
                    c, s1, s2 = c_ref[rows, :], s1_ref[rows, :], s2_ref[rows, :]
                    dst = pl.ds(off + c0, n)
                    qv, kv = q_ref[rows, :], k_ref[rows, :]
                    rq = lax.rsqrt(_head_mean(qv * qv, e) + EPS)
                    rk = lax.rsqrt(_head_mean(kv * kv, e) + EPS)
                    qs[dst, :] = (_rope(qv * rq * qw_ref[...], c, s1, s2) * (HD ** -0.5)).astype(BF16)
                    ks[dst, :] = _rope(kv * rk * kw_ref[...], c, s1, s2).astype(BF16)
                    vs[dst, :] = v_ref[rows, :].astype(BF16)
                    dos[dst, :] = do_ref[rows, :].astype(BF16)
                    dsub[dst, :] = dnat[rows, :]
                    lsub[dst, :] = ls_ref[rows, :]
                    dks[dst, :] = jnp.zeros((n, LANES), F32)
                    dvs[dst, :] = jnp.zeros((n, LANES), F32)

            def one(off, i):
                W, q0, k0, valid = _band_window(i, L)
                qrows, krows = pl.ds(off + q0, TQ), pl.ds(off + k0, W)
                q2 = _stack_heads(qs[qrows, :], lo)
                do2 = _stack_heads(dos[qrows, :], lo)
                kk, vv = ks[krows, :], vs[krows, :]
                lse_b, dd_b = lsub[qrows, :], dsub[qrows, :]
                lse2 = jnp.concatenate([lse_b[:, 0:1], lse_b[:, HD:HD + 1]], axis=0)
                dd2 = jnp.concatenate([dd_b[:, 0:1], dd_b[:, HD:HD + 1]], axis=0)
                sc = jnp.where(valid, _dot_nt(q2, kk), NEG_INF)
                p = jnp.exp(sc - lse2)
                ds = (p * (_dot_nt(do2, vv) - dd2)).astype(BF16)
                dqs[qrows, :] = _unstack_heads(_dot(ds, kk), lo)
                dks[krows, :] = dks[krows, :] + _dot_tn(ds, q2)
                dvs[krows, :] = dvs[krows, :] + _dot_tn(p.astype(BF16), do2)

            def unstage(r, off):
                gq = jnp.zeros((1, LANES), F32)
                gk = jnp.zeros((1, LANES), F32)
                for c0 in range(0, L, CH):
                    n = min(CH, L)
                    rows = _sub_rows(r, d, c0, n)
                    src = pl.ds(off + c0, n)
                    c, s1, s2 = c_ref[rows, :], s1_ref[rows, :], s2_ref[rows, :]
                    for (raw_ref, w_ref, gsub, scale, nat) in (
                            (q_ref, qw_ref, dqs, HD ** -0.5, dqn), (k_ref, kw_ref, dks, 1.0, dkn)):
                        t = raw_ref[rows, :]
                        rr = lax.rsqrt(_head_mean(t * t, e) + EPS)
                        tn = t * rr
                        dy = _rope_t(gsub[src, :] * scale, c, s1, s2)
                        gw = jnp.sum(dy * tn, axis=0, keepdims=True)
                        if raw_ref is q_ref:
                            gq = gq + gw
                        else:
                            gk = gk + gw
                        dtn = dy * w_ref[...]
                        nat[rows, :] = rr * (dtn - tn * _head_mean(dtn * tn, e))
                    dvn[rows, :] = dvs[src, :]
                gqw_ref[0:1, :] = gqw_ref[0:1, :] + gq
                gkw_ref[0:1, :] = gkw_ref[0:1, :] + gk

            def step(t, _):
                for u in range(ru):
                    stage(t * ru + u, u * L)
                _for_blocks(L // TQ // nb, lambda j: [one(u * L, j * nb + b) for u in range(ru) for b in range(nb)])
                for u in range(ru):
                    unstage(t * ru + u, u * L)
                return 0

            lax.fori_loop(0, d // ru, step, 0)

        for gi, d in enumerate(DILATIONS):
            pl.when(g == gi)(functools.partial(group, d))

        def emit(i, _):
            rows = pl.ds(pl.multiple_of(i * CH, CH), CH)
            dq_ref[rows, :] = dqn[rows, :].astype(BF16)
            dk_ref[rows, :] = dkn[rows, :].astype(BF16)
            dv_ref[rows, :] = dvn[rows, :].astype(BF16)
            return 0

        lax.fori_loop(0, S // CH, emit, 0)

    nat_spec = pl.BlockSpec((S, LANES), lambda hp, g: (0, hp))
    out_spec = pl.BlockSpec((S, LANES), lambda hp, g: (0, g * 4 + hp))
    acc_spec = pl.BlockSpec((8, LANES), lambda hp, g: (0, 0))
    return pl.pallas_call(
        body, name="attn_bwd", grid=(4, 3),
        in_specs=_qk_specs() + _tab_specs() + [_vec_spec(), _vec_spec(), nat_spec, nat_spec, nat_spec],
        out_specs=[out_spec] * 3 + [acc_spec] * 2,
        out_shape=[jax.ShapeDtypeStruct((S, QKV), BF16)] * 3 + [jax.ShapeDtypeStruct((8, LANES), F32)] * 2,
        scratch_shapes=[pltpu.VMEM((S, LANES), BF16)] * 4 + [pltpu.VMEM((S, LANES), F32)] * 9,
        compiler_params=_cp(dimension_semantics=("arbitrary", "arbitrary")),
    )(proj, proj, proj, *tabs, qw2, kw2, d_attn, attn, lse)


PADR = 16
CT = 128


def _conv_specs():
    return [pl.BlockSpec((S, CC), lambda i: (0, OFF_CA // CC)), pl.BlockSpec((S, CC), lambda i: (0, OFF_CB // CC))]


NCB = CC // LANES


def _pad_zero(pad):
    for cb in range(NCB):
        pad[cb, 0:PADR, :] = jnp.zeros((PADR, LANES), F32)
        pad[cb, PADR + S:PADR + S + PADR, :] = jnp.zeros((PADR, LANES), F32)


def _pad_store(pad, row0, n, val):
    for cb in range(NCB):
        pad[cb, pl.ds(pl.multiple_of(row0 + PADR, 8), n), :] = val[:, cb * LANES:(cb + 1) * LANES]


def _taps(pad_ref, cb, s0, weights):
    acc = jnp.zeros((CT, LANES), F32)
    for k in range(KW):
        acc = acc + weights[k] * pad_ref[cb, pl.ds(s0 + k + 1, CT), :]
    return acc


def conv_fwd(proj, conv_w, conv_b, ln_w, ln_b):
    def body(a_ref, b_ref, w_ref, cb_ref, lw_ref, lb_ref, c_ref, u3_ref, upad):
        _pad_zero(upad)

        def glu(i, _):
            rows = pl.ds(pl.multiple_of(i * TM, TM), TM)
            _pad_store(upad, i * TM, TM, a_ref[rows, :] * _sigmoid(b_ref[rows, :]))
            return 0

        lax.fori_loop(0, S // TM, glu, 0)

        def chunk(i, _):
            s0 = pl.multiple_of(i * CT, CT)
            for cb in range(CC // LANES):
                cols = slice(cb * LANES, (cb + 1) * LANES)
                w = [w_ref[k:k + 1, cols] for k in range(KW)]
                c_ref[pl.ds(s0, CT), cols] = _taps(upad, cb, s0, w) + cb_ref[:, cols]
            cv = c_ref[pl.ds(s0, CT), :]
            mu = jnp.mean(cv, axis=-1, keepdims=True)
            xc = cv - mu
            rstd = lax.rsqrt(jnp.mean(xc * xc, axis=-1, keepdims=True) + EPS)
            yl = xc * rstd * lw_ref[...] + lb_ref[...]
            u3_ref[pl.ds(s0, CT), :] = (yl * _sigmoid(yl)).astype(BF16)
            return 0

        lax.fori_loop(0, S // CT, chunk, 0)

    vec = pl.BlockSpec((1, CC), lambda i: (0, 0))
    full = pl.BlockSpec((S, CC), lambda i: (0, 0))
    return pl.pallas_call(
        body, name="conv_fwd", grid=(1,),
        in_specs=_conv_specs() + [pl.BlockSpec((KW, CC), lambda i: (0, 0)), vec, vec, vec],
        out_specs=[full, full],
        out_shape=[jax.ShapeDtypeStruct((S, CC), F32), jax.ShapeDtypeStruct((S, CC), BF16)],
        scratch_shapes=[pltpu.VMEM((NCB, S + 2 * PADR, LANES), F32)],
        compiler_params=_cp(dimension_semantics=("arbitrary",)),
    )(proj, proj, conv_w, conv_b, ln_w, ln_b)


def conv_bwd(proj, cpre, d_u3, conv_w, conv_w_rev, ln_w, ln_b):
    def body(a_ref, b_ref, c_ref, du3_ref, w_ref, wr_ref, lw_ref, lb_ref,
             dc_ref, gw_ref, gcb_ref, glw_ref, glb_ref, upad, dpad):
        _pad_zero(upad)
        _pad_zero(dpad)
        gw_ref[...] = jnp.zeros_like(gw_ref)

        def ln_bwd(i, carry):
            gcb, glw, glb = carry
            rows = pl.ds(pl.multiple_of(i * TM, TM), TM)
            _pad_store(upad, i * TM, TM, a_ref[rows, :] * _sigmoid(b_ref[rows, :]))
            cv = c_ref[rows, :]
            mu = jnp.mean(cv, axis=-1, keepdims=True)
            xc = cv - mu
            rstd = lax.rsqrt(jnp.mean(xc * xc, axis=-1, keepdims=True) + EPS)
            xh = xc * rstd
            yl = xh * lw_ref[...] + lb_ref[...]
            dyl = du3_ref[rows, :] * _dsilu(yl, _sigmoid(yl))
            dxh = dyl * lw_ref[...]
            dcv = rstd * (dxh - jnp.mean(dxh, axis=-1, keepdims=True)
                          - xh * jnp.mean(dxh * xh, axis=-1, keepdims=True))
            _pad_store(dpad, i * TM, TM, dcv)
            return (gcb + jnp.sum(dcv, axis=0, keepdims=True),
                    glw + jnp.sum(dyl * xh, axis=0, keepdims=True),
                    glb + jnp.sum(dyl, axis=0, keepdims=True))

        z = jnp.zeros((1, CC), F32)
        gcb, glw, glb = lax.fori_loop(0, S // TM, ln_bwd, (z, z, z))
        gcb_ref[...] = gcb
        glw_ref[...] = glw
        glb_ref[...] = glb

        def chunk(i, _):
            s0 = pl.multiple_of(i * CT, CT)
            for cb in range(CC // LANES):
                cols = slice(cb * LANES, (cb + 1) * LANES)
                wr = [wr_ref[k:k + 1, cols] for k in range(KW)]
                du = _taps(dpad, cb, s0, wr)
                dcv = dpad[cb, pl.ds(s0 + PADR, CT), :]
                for k in range(KW):
                    gw_ref[k:k + 1, cols] = gw_ref[k:k + 1, cols] + jnp.sum(
                        upad[cb, pl.ds(s0 + k + 1, CT), :] * dcv, axis=0, keepdims=True)
                av = a_ref[pl.ds(s0, CT), cols]
                sb = _sigmoid(b_ref[pl.ds(s0, CT), cols])
                dc_ref[pl.ds(s0, CT), cols] = (du * sb).astype(BF16)
                dc_ref[pl.ds(s0, CT), slice(CC + cb * LANES, CC + (cb + 1) * LANES)] = (
                    du * av * sb * (1.0 - sb)).astype(BF16)
            return 0

        lax.fori_loop(0, S // CT, chunk, 0)

    vec = pl.BlockSpec((1, CC), lambda i: (0, 0))
    full = pl.BlockSpec((S, CC), lambda i: (0, 0))
    wsp = pl.BlockSpec((KW, CC), lambda i: (0, 0))
    return pl.pallas_call(
        body, name="conv_bwd", grid=(1,),
        in_specs=_conv_specs() + [full, full, wsp, wsp, vec, vec],
        out_specs=[pl.BlockSpec((S, 2 * CC), lambda i: (0, 0)), wsp, vec, vec, vec],
        out_shape=[jax.ShapeDtypeStruct((S, 2 * CC), BF16), jax.ShapeDtypeStruct((KW, CC), F32)]
        + [jax.ShapeDtypeStruct((1, CC), F32)] * 3,
        scratch_shapes=[pltpu.VMEM((NCB, S + 2 * PADR, LANES), F32)] * 2,
        compiler_params=_cp(dimension_semantics=("arbitrary",)),
    )(proj, proj, cpre, d_u3, conv_w, conv_w_rev, ln_w, ln_b)


def _gate_specs():
    return [_row(CC, col=OFF_GA // CC + j) for j in range(4)]


def _gates(g_refs, bg_ref):
    ga = _sigmoid(jnp.concatenate([g_refs[0][...], g_refs[1][...]], axis=1) + bg_ref[0:1, :])
    gb = _sigmoid(jnp.concatenate([g_refs[2][...], g_refs[3][...]], axis=1) + bg_ref[1:2, :])
    return ga, gb


def mix_out(x, proj, b_gate, attn, u3, w_o, w_pw, w_out):
    def body(x_ref, g0, g1, g2, g3, bg_ref, at_ref, u3_ref, wo_ref, wp_ref, wout_ref,
             x1_ref, z_ref, ya_ref, yb_ref):
        ga, gb = _gates((g0, g1, g2, g3), bg_ref)
        ya = _dot(at_ref[...], wo_ref[...])
        yb = _dot(u3_ref[...], wp_ref[...])
        z = (ga * ya + gb * yb).astype(BF16)
        ya_ref[...] = ya.astype(BF16)
        yb_ref[...] = yb.astype(BF16)
        z_ref[...] = z
        x1_ref[...] = x_ref[...] + _dot(z, wout_ref[...])

    return pl.pallas_call(
        body, name="mix_out", grid=(S // TM,),
        in_specs=[_row(D)] + _gate_specs() + [_res((2, D)), _row(CC), _row(CC),
                                              _res((CC, D)), _res((CC, D)), _res((D, D))],
        out_specs=[_row(D)] * 4,
        out_shape=[jax.ShapeDtypeStruct((S, D), F32)] + [jax.ShapeDtypeStruct((S, D), BF16)] * 3,
        compiler_params=_cp(dimension_semantics=("arbitrary",)),
    )(x, proj, proj, proj, proj, b_gate, attn, u3, w_o, w_pw, w_out)


def out_bwd(d_x1b, proj, b_gate, ya, yb, w_o, w_pw, w_out):
    def body(dx_ref, g0, g1, g2, g3, bg_ref, ya_ref, yb_ref, wo_ref, wp_ref, wout_ref,
             dya_ref, dyb_ref, dgl_ref, dat_ref, du3_ref, gbg_ref):
        @pl.when(pl.program_id(0) == 0)
        def _():
            gbg_ref[...] = jnp.zeros_like(gbg_ref)

        ga, gb = _gates((g0, g1, g2, g3), bg_ref)
        dz = _dot_nt(dx_ref[...], wout_ref[...])
        dya = (dz * ga).astype(BF16)
        dyb = (dz * gb).astype(BF16)
        dgla = dz * ya_ref[...].astype(F32) * ga * (1.0 - ga)
        dglb = dz * yb_ref[...].astype(F32) * gb * (1.0 - gb)
        dya_ref[...] = dya
        dyb_ref[...] = dyb
        dgl_ref[:, 0:D] = dgla.astype(BF16)
        dgl_ref[:, D:2 * D] = dglb.astype(BF16)
        gbg_ref[0:1, :] = gbg_ref[0:1, :] + jnp.sum(dgla, axis=0, keepdims=True)
        gbg_ref[1:2, :] = gbg_ref[1:2, :] + jnp.sum(dglb, axis=0, keepdims=True)
        dat_ref[...] = _dot_nt(dya, wo_ref[...])
        du3_ref[...] = _dot_nt(dyb, wp_ref[...])

    return pl.pallas_call(
        body, name="out_bwd", grid=(S // TM,),
        in_specs=[_row(D)] + _gate_specs() + [_res((2, D)), _row(D), _row(D),
                                              _res((CC, D)), _res((CC, D)), _res((D, D))],
        out_specs=[_row(D), _row(D), _row(2 * D), _row(CC), _row(CC), pl.BlockSpec((2, D), lambda i: (0, 0))],
        out_shape=[jax.ShapeDtypeStruct((S, D), BF16)] * 2 + [jax.ShapeDtypeStruct((S, 2 * D), BF16)]
        + [jax.ShapeDtypeStruct((S, CC), F32)] * 2 + [jax.ShapeDtypeStruct((2, D), F32)],
        compiler_params=_cp(dimension_semantics=("arbitrary",)),
    )(d_x1b, proj, proj, proj, proj, b_gate, ya, yb, w_o, w_pw, w_out)


def ffn_in(x1, norm_w, w_ffn_in):
    half = FF // 2

    def body(x_ref, nw_ref, w_ref, h_ref, gu_ref, f_ref):
        xv = x_ref[...]
        r = lax.rsqrt(jnp.mean(xv * xv, axis=-1, keepdims=True) + EPS)
        h = (xv * r * nw_ref[...]).astype(BF16)
        h_ref[...] = h
        for j in range(2):
            gt = _dot(h, w_ref[:, j * half:(j + 1) * half])
            up = _dot(h, w_ref[:, FF + j * half:FF + (j + 1) * half])
            gu_ref[:, j * half:(j + 1) * half] = gt.astype(BF16)
            gu_ref[:, FF + j * half:FF + (j + 1) * half] = up.astype(BF16)
            f_ref[:, j * half:(j + 1) * half] = (gt * _sigmoid(gt) * up).astype(BF16)

    return pl.pallas_call(
        body, name="ffn_in", grid=(S // TM,),
        in_specs=[_row(D), _res((1, D)), _res((D, 2 * FF))],
        out_specs=[_row(D), _row(2 * FF), _row(FF)],
        out_shape=[jax.ShapeDtypeStruct((S, D), BF16), jax.ShapeDtypeStruct((S, 2 * FF), BF16),
                   jax.ShapeDtypeStruct((S, FF), BF16)],
        compiler_params=_cp(dimension_semantics=("arbitrary",)),
    )(x1, norm_w, w_ffn_in)


def ffn_out_loss(x1, f, w_ffn_out, target):
    def body(x_ref, f_ref, w_ref, t_ref, dy_ref, dyb_ref, sq_ref):
        @pl.when(pl.program_id(0) == 0)
        def _():
            sq_ref[...] = jnp.zeros_like(sq_ref)

        diff = x_ref[...] + _dot(f_ref[...], w_ref[...]) - t_ref[...]
        dy = diff * (1.0 / D)
        dy_ref[...] = dy
        dyb_ref[...] = dy.astype(BF16)
        sq_ref[...] = sq_ref[...] + jnp.sum((diff * diff).reshape(TM // 8, 8, D), axis=0)

    return pl.pallas_call(
        body, name="ffn_out_loss", grid=(S // TM,),
        in_specs=[_row(D), _row(FF), _res((FF, D)), _row(D)],
        out_specs=[_row(D), _row(D), pl.BlockSpec((8, D), lambda i: (0, 0))],
        out_shape=[jax.ShapeDtypeStruct((S, D), F32), jax.ShapeDtypeStruct((S, D), BF16),
                   jax.ShapeDtypeStruct((8, D), F32)],
        compiler_params=_cp(dimension_semantics=("arbitrary",)),
    )(x1, f, w_ffn_out, target)


def _rms_bwd(xv, nw, dh):
    r = lax.rsqrt(jnp.mean(xv * xv, axis=-1, keepdims=True) + EPS)
    xn = xv * r
    dxn = dh * nw
    dx = r * (dxn - xn * jnp.mean(dxn * xn, axis=-1, keepdims=True))
    return dx, dh * xn


def ffn_bwd(dy, dyb, gu, x1, norm_w, w_ffn_in, w_ffn_out):
    def body(dy_ref, dyb_ref, gu_ref, x_ref, nw_ref, wi_ref, wo_ref, dgu_ref, dx_ref, dxb_ref, gn_ref):
        @pl.when(pl.program_id(0) == 0)
        def _():
            gn_ref[...] = jnp.zeros_like(gn_ref)

        df = _dot_nt(dyb_ref[...], wo_ref[...])
        gt = gu_ref[:, 0:FF].astype(F32)
        up = gu_ref[:, FF:2 * FF].astype(F32)
        sg = _sigmoid(gt)
        dgt = (df * up * _dsilu(gt, sg)).astype(BF16)
        dup = (df * gt * sg).astype(BF16)
        dgu_ref[:, 0:FF] = dgt
        dgu_ref[:, FF:2 * FF] = dup
        dh = _dot_nt(dgt, wi_ref[:, 0:FF]) + _dot_nt(dup, wi_ref[:, FF:2 * FF])
        dxn, gw = _rms_bwd(x_ref[...], nw_ref[...], dh)
        dx = dy_ref[...] + dxn
        dx_ref[...] = dx
        dxb_ref[...] = dx.astype(BF16)
        gn_ref[...] = gn_ref[...] + jnp.sum(gw, axis=0, keepdims=True)

    return pl.pallas_call(
        body, name="ffn_bwd", grid=(S // TM,),
        in_specs=[_row(D), _row(D), _row(2 * FF), _row(D), _res((1, D)), _res((D, 2 * FF)), _res((FF, D))],
        out_specs=[_row(2 * FF), _row(D), _row(D), pl.BlockSpec((1, D), lambda i: (0, 0))],
        out_shape=[jax.ShapeDtypeStruct((S, 2 * FF), BF16), jax.ShapeDtypeStruct((S, D), F32),
                   jax.ShapeDtypeStruct((S, D), BF16), jax.ShapeDtypeStruct((1, D), F32)],
        compiler_params=_cp(dimension_semantics=("arbitrary",)),
    )(dy, dyb, gu, x1, norm_w, w_ffn_in, w_ffn_out)


def in_bwd(d_q, d_k, d_v, d_conv, d_gl, w_in, x, d_x1, norm_w):
    segs = ((OFF_Q, QKV), (OFF_K, QKV), (OFF_V, QKV), (OFF_CA, 2 * CC), (OFF_GA, 2 * D))

    def body(dq_ref, dk_ref, dv_ref, dc_ref, dg_ref, w_ref, x_ref, dx1_ref, nw_ref, gx_ref, gn_ref):
        @pl.when(pl.program_id(0) == 0)
        def _():
            gn_ref[...] = jnp.zeros_like(gn_ref)

        dh = jnp.zeros((TM, D), F32)
        for ref, (off, width) in zip((dq_ref, dk_ref, dv_ref, dc_ref, dg_ref), segs):
            dh = dh + _dot_nt(ref[...], w_ref[:, off:off + width])
        dxn, gw = _rms_bwd(x_ref[...], nw_ref[...], dh)
        gx_ref[...] = dx1_ref[...] + dxn
        gn_ref[...] = gn_ref[...] + jnp.sum(gw, axis=0, keepdims=True)

    return pl.pallas_call(
        body, name="in_bwd", grid=(S // TM,),
        in_specs=[_row(QKV)] * 3 + [_row(2 * CC), _row(2 * D), _res((D, INW)), _row(D), _row(D), _res((1, D))],
        out_specs=[_row(D), pl.BlockSpec((1, D), lambda i: (0, 0))],
        out_shape=[jax.ShapeDtypeStruct((S, D), F32), jax.ShapeDtypeStruct((1, D), F32)],
        compiler_params=_cp(dimension_semantics=("arbitrary",)),
    )(d_q, d_k, d_v, d_conv, d_gl, w_in, x, d_x1, norm_w)


def mm_tn(name, a, b, tm, tn):
    M, N = a.shape[1], b.shape[1]

    def body(a_ref, b_ref, o_ref):
        o_ref[...] = _dot_tn(a_ref[...], b_ref[...])

    return pl.pallas_call(
        body, name=name, grid=(M // tm, N // tn),
        in_specs=[pl.BlockSpec((S, tm), lambda i, j: (0, i)), pl.BlockSpec((S, tn), lambda i, j: (0, j))],
        out_specs=pl.BlockSpec((tm, tn), lambda i, j: (i, j)),
        out_shape=jax.ShapeDtypeStruct((M, N), F32),
        compiler_params=_cp(dimension_semantics=("arbitrary", "arbitrary")),
    )(a, b)


def local_step(x, pos_col, target, p):
    tabs = rope_tables(pos_col)
    qw2 = jnp.tile(p["q_norm_w"], (1, 2))
    kw2 = jnp.tile(p["k_norm_w"], (1, 2))

    h, proj = in_proj(x, p["norm1_w"], p["w_in"])
    attn, lse = attn_fwd(proj, tabs, qw2, kw2)
    cpre, u3 = conv_fwd(proj, p["conv_w"], p["conv_b"], p["conv_ln_w"], p["conv_ln_b"])
    x1, z, ya, yb = mix_out(x, proj, p["b_gate"], attn, u3, p["w_o_attn"], p["w_pw_conv"], p["w_out"])
    h2, gu, f = ffn_in(x1, p["norm2_w"], p["w_ffn_in"])
    dy, dyb, sq = ffn_out_loss(x1, f, p["w_ffn_out"], target)

    g = {}
    g["w_ffn_out"] = mm_tn("gw_ffn_out", f, dyb, FF // 2, D)
    d_gu, d_x1, d_x1b, g["norm2_w"] = ffn_bwd(dy, dyb, gu, x1, p["norm2_w"], p["w_ffn_in"], p["w_ffn_out"])
    g["w_ffn_in"] = mm_tn("gw_ffn_in", h2, d_gu, D // 2, FF // 2)
    g["w_out"] = mm_tn("gw_out", z, d_x1b, D // 2, D)
    d_ya, d_yb, d_gl, d_attn, d_u3, g["b_gate"] = out_bwd(
        d_x1b, proj, p["b_gate"], ya, yb, p["w_o_attn"], p["w_pw_conv"], p["w_out"])
    g["w_o_attn"] = mm_tn("gw_o_attn", attn, d_ya, CC, D)
    g["w_pw_conv"] = mm_tn("gw_pw_conv", u3, d_yb, CC, D)
    d_conv, g["conv_w"], g["conv_b"], g["conv_ln_w"], g["conv_ln_b"] = conv_bwd(
        proj, cpre, d_u3, p["conv_w"], p["conv_w"][::-1], p["conv_ln_w"], p["conv_ln_b"])
    d_q, d_k, d_v, gqw, gkw = attn_bwd(proj, tabs, qw2, kw2, d_attn, attn, lse)
    g["q_norm_w"] = gqw[0:1, 0:HD] + gqw[0:1, HD:LANES]
    g["k_norm_w"] = gkw[0:1, 0:HD] + gkw[0:1, HD:LANES]
    g["w_in"] = jnp.concatenate([
        mm_tn("gw_in_q", h, d_q, D // 2, QKV), mm_tn("gw_in_k", h, d_k, D // 2, QKV),
        mm_tn("gw_in_v", h, d_v, D // 2, QKV), mm_tn("gw_in_c", h, d_conv, D // 2, 2 * CC),
        mm_tn("gw_in_g", h, d_gl, D // 2, 2 * D)], axis=1)
    grad_x, g["norm1_w"] = in_bwd(d_q, d_k, d_v, d_conv, d_gl, p["w_in"], x, d_x1, p["norm1_w"])
    return sq, grad_x, g


ANY = pl.BlockSpec(memory_space=pl.ANY)
VMEM = pl.BlockSpec(memory_space=pltpu.VMEM)


def _place():
    x, y, c = lax.axis_index("x"), lax.axis_index("y"), lax.axis_index("c")
    chips = [(1 - x, y), (x, 1 - y), (1 - x, 1 - y)]
    return x, y, c, chips


def all_gather(shards, out_dtypes):
    n = len(shards)

    def body(*refs):
        ins, outs, stage = refs[:n], refs[n:2 * n], refs[2 * n:3 * n]
        send_sems, recv_sems, local_sems = refs[3 * n:]
        x, y, c, chips = _place()
        me, sib = (x, y, c), (x, y, 1 - c)

        def blk(px, py, pc):
            return 4 * px + 2 * py + pc

        def copy(t, k, block, to, src=None):
            dst = outs[t].at[blk(*block)]
            return pltpu.make_async_remote_copy(
                src_ref=dst if src is None else src, dst_ref=dst,
                send_sem=send_sems.at[7 * t + k], recv_sem=recv_sems.at[7 * t + k],
                device_id=to, device_id_type=MESH)

        for t in range(n):
            rows = ins[t].shape[0]
            step = 64 if rows % 64 == 0 else rows
            for r0 in range(0, rows, step):
                stage[t][r0:r0 + step, :] = ins[t][r0:r0 + step, :].astype(stage[t].dtype)

        local, sent = [], []
        for t in range(n):
            mine = pltpu.make_async_copy(stage[t], outs[t].at[blk(*me)], local_sems.at[t])
            mine.start()
            local.append(mine)
            first = [copy(t, 0, me, sib, src=stage[t])]
            first += [copy(t, 1 + j, me, (*chip, c), src=stage[t]) for j, chip in enumerate(chips)]
            for cp in first:
                cp.start()
            sent += first
        for t in range(n):
            for j, chip in enumerate(chips):
                copy(t, 1 + j, (*chip, c), me).wait_recv()
                fwd = copy(t, 4 + j, (*chip, c), sib)
                fwd.start()
                sent.append(fwd)
        for t in range(n):
            copy(t, 0, sib, me).wait_recv()
            for j, chip in enumerate(chips):
                copy(t, 4 + j, (*chip, 1 - c), me).wait_recv()
        for cp in sent:
            cp.wait_send()
        for cp in local:
            cp.wait()

    return pl.pallas_call(
        body, name="all_gather_weights",
        in_specs=[VMEM] * n, out_specs=[ANY] * n,
        out_shape=[jax.ShapeDtypeStruct((NDEV,) + s.shape, dt) for s, dt in zip(shards, out_dtypes)],
        scratch_shapes=[pltpu.VMEM(s.shape, dt) for s, dt in zip(shards, out_dtypes)]
        + [pltpu.SemaphoreType.DMA((7 * n,)), pltpu.SemaphoreType.DMA((7 * n,)), pltpu.SemaphoreType.DMA((n,))],
        compiler_params=_cp(),
    )(*shards)


def rs_sibling(grads):
    n = len(grads)

    def body(*refs):
        ins, outs = refs[:n], refs[n:2 * n]
        send_sems, recv_sems = refs[2 * n:]
        x, y, c, _ = _place()
        copies = []
        for t in range(n):
            for k in range(4):
                cp = pltpu.make_async_remote_copy(
                    src_ref=ins[t].at[2 * k + 1 - c], dst_ref=outs[t].at[k],
                    send_sem=send_sems.at[4 * t + k], recv_sem=recv_sems.at[4 * t + k],
                    device_id=(x, y, 1 - c), device_id_type=MESH)
                cp.start()
                copies.append(cp)
        for cp in copies:
            cp.wait()

    return pl.pallas_call(
        body, name="rs_sibling", in_specs=[ANY] * n, out_specs=[ANY] * n,
        out_shape=[jax.ShapeDtypeStruct((4,) + g.shape[1:], F32) for g in grads],
        scratch_shapes=[pltpu.SemaphoreType.DMA((4 * n,)), pltpu.SemaphoreType.DMA((4 * n,))],
    )(*grads)


def _row_tiles(rows):
    return 4 if rows % 64 == 0 and rows >= 512 else (2 if rows % 32 == 0 and rows >= 256 else 1)


def chip_sum(name, grad, recv, c_idx, chip_idx):
    _, R, C = grad.shape
    nt = _row_tiles(R)
    tr = R // nt

    def body(s_ref, g_ref, r_ref, p_ref, own_ref):
        k = pl.program_id(1)
        tot = g_ref[0] + r_ref[0]
        p_ref[0] = tot.astype(BF16)

        @pl.when(k == s_ref[1])
        def _():
            own_ref[...] = tot

    grid_spec = pltpu.PrefetchScalarGridSpec(
        num_scalar_prefetch=1, grid=(nt, 4),
        in_specs=[pl.BlockSpec((1, tr, C), lambda i, k, s: (2 * k + s[0], i, 0)),
                  pl.BlockSpec((1, tr, C), lambda i, k, s: (k, i, 0))],
        out_specs=[pl.BlockSpec((1, tr, C), lambda i, k, s: (k, i, 0)),
                   pl.BlockSpec((tr, C), lambda i, k, s: (i, 0))])
    return pl.pallas_call(
        body, name=name, grid_spec=grid_spec,
        out_shape=[jax.ShapeDtypeStruct((4, R, C), BF16), jax.ShapeDtypeStruct((R, C), F32)],
        compiler_params=_cp(dimension_semantics=("arbitrary", "arbitrary")),
    )(jnp.stack([c_idx, chip_idx]), grad, recv)


def rs_chips(parts):
    n = len(parts)

    def body(*refs):
        ins, outs = refs[:n], refs[n:2 * n]
        send_sems, recv_sems = refs[2 * n:]
        x, y, c, chips = _place()
        copies = []
        for t in range(n):
            for j, (cx, cy) in enumerate(chips):
                cp = pltpu.make_async_remote_copy(
                    src_ref=ins[t].at[2 * cx + cy], dst_ref=outs[t].at[j],
                    send_sem=send_sems.at[3 * t + j], recv_sem=recv_sems.at[3 * t + j],
                    device_id=(cx, cy, c), device_id_type=MESH)
                cp.start()
                copies.append(cp)
        for cp in copies:
            cp.wait()

    return pl.pallas_call(
        body, name="rs_chips", in_specs=[ANY] * n, out_specs=[ANY] * n,
        out_shape=[jax.ShapeDtypeStruct((3,) + p.shape[1:], BF16) for p in parts],
        scratch_shapes=[pltpu.SemaphoreType.DMA((3 * n,)), pltpu.SemaphoreType.DMA((3 * n,))],
    )(*parts)


def _adamw(w, g, m, v):
    m2 = ADAM_B1 * m + (1.0 - ADAM_B1) * g
    v2 = ADAM_B2 * v + (1.0 - ADAM_B2) * (g * g)
    m_hat = m2 / (1.0 - ADAM_B1 ** ADAM_STEP)
    v_hat = v2 / (1.0 - ADAM_B2 ** ADAM_STEP)
    delta = -ADAM_LR * (m_hat / (jnp.sqrt(v_hat) + ADAM_EPS) + ADAM_WD * w)
    return delta, m2, v2


def shard_adam(name, own, recv, w, m, v):
    R, C = own.shape
    nt = _row_tiles(R)
    tr = R // nt

    def body(o_ref, r_ref, w_ref, m_ref, v_ref, g_ref, d_ref, nm_ref, nv_ref):
        g = o_ref[...] + r_ref[0].astype(F32) + r_ref[1].astype(F32) + r_ref[2].astype(F32)
        delta, m2, v2 = _adamw(w_ref[...], g, m_ref[...], v_ref[...])
        g_ref[...] = g
        d_ref[...] = delta
        nm_ref[...] = m2
        nv_ref[...] = v2

    tile = pl.BlockSpec((tr, C), lambda i: (i, 0))
    return pl.pallas_call(
        body, name=name, grid=(nt,),
        in_specs=[tile, pl.BlockSpec((3, tr, C), lambda i: (0, i, 0)), tile, tile, tile],
        out_specs=[tile] * 4, out_shape=[jax.ShapeDtypeStruct((R, C), F32)] * 4,
        compiler_params=_cp(dimension_semantics=("arbitrary",)),
    )(own, recv, w, m, v)


ROW_N1, ROW_N2, ROW_BG, ROW_QN, ROW_KN, ROW_CB, ROW_LW, ROW_LB, ROW_CW = 0, 1, 2, 4, 5, 6, 7, 8, 9
PACK_ROWS = 40
SMALL = ("norm1_w", "norm2_w", "b_gate", "q_norm_w", "k_norm_w", "conv_b", "conv_ln_w", "conv_ln_b", "conv_w")


def small_sync_adam(g, w, m, v):
    ns = len(SMALL)

    def body(*refs):
        gi = dict(zip(SMALL, refs[:ns]))
        wi = dict(zip(SMALL, refs[ns:2 * ns]))
        mi = dict(zip(SMALL, refs[2 * ns:3 * ns]))
        vi = dict(zip(SMALL, refs[3 * ns:4 * ns]))
        outs = refs[4 * ns:8 * ns]
        pack, recv, tot, send_sems, recv_sems = refs[8 * ns:]
        x, y, c, _ = _place()
        me = 4 * x + 2 * y + c

        pack[...] = jnp.zeros_like(pack)
        pack[ROW_N1:ROW_N1 + 1, :] = gi["norm1_w"][...]
        pack[ROW_N2:ROW_N2 + 1, :] = gi["norm2_w"][...]
        pack[ROW_BG:ROW_BG + 2, :] = gi["b_gate"][...]
        pack[ROW_QN:ROW_QN + 1, 0:HD] = gi["q_norm_w"][...]
        pack[ROW_KN:ROW_KN + 1, 0:HD] = gi["k_norm_w"][...]
        pack[ROW_CB:ROW_CB + 1, 0:CC] = gi["conv_b"][...]
        pack[ROW_LW:ROW_LW + 1, 0:CC] = gi["conv_ln_w"][...]
        pack[ROW_LB:ROW_LB + 1, 0:CC] = gi["conv_ln_b"][...]
        pack[ROW_CW:ROW_CW + KW, 0:CC] = gi["conv_w"][...]

        copies = []
        for k in range(1, NDEV):
            peer = (x ^ (k >> 2), y ^ ((k >> 1) & 1), c ^ (k & 1))
            cp = pltpu.make_async_remote_copy(
                src_ref=pack, dst_ref=recv.at[me], send_sem=send_sems.at[k - 1], recv_sem=recv_sems.at[k - 1],
                device_id=peer, device_id_type=MESH)
            cp.start()
            copies.append(cp)
        recv[me] = pack[...]
        for cp in copies:
            cp.wait()
        acc = recv[0]
        for p in range(1, NDEV):
            acc = acc + recv[p]
        tot[...] = acc

        def shard_grad(name):
            if name == "b_gate":
                return tot[ROW_BG:ROW_BG + 2, pl.ds(pl.multiple_of(me * LANES, LANES), LANES)]
            if name == "conv_w":
                win = tot[ROW_CW:ROW_CW + KW, pl.ds(pl.multiple_of((me // 2) * LANES, LANES), LANES)]
                return jnp.where(me % 2 == 1, win[:, HD:LANES], win[:, 0:HD])
            row = {"norm1_w": ROW_N1, "norm2_w": ROW_N2, "q_norm_w": ROW_QN, "k_norm_w": ROW_KN,
                   "conv_b": ROW_CB, "conv_ln_w": ROW_LW, "conv_ln_b": ROW_LB}[name]
            return tot[row:row + 1, 0:wi[name].shape[1]]

        for i, name in enumerate(SMALL):
            gr = shard_grad(name)
            delta, m2, v2 = _adamw(wi[name][...], gr, mi[name][...], vi[name][...])
            outs[4 * i][...] = gr
            outs[4 * i + 1][...] = delta
            outs[4 * i + 2][...] = m2
            outs[4 * i + 3][...] = v2

    out_shape = []
    for name in SMALL:
        out_shape += [jax.ShapeDtypeStruct(w[name].shape, F32)] * 4
    args = [g[k] for k in SMALL] + [w[k] for k in SMALL] + [m[k] for k in SMALL] + [v[k] for k in SMALL]
    res = pl.pallas_call(
        body, name="small_sync_adam", in_specs=[VMEM] * len(args), out_specs=[VMEM] * len(out_shape),
        out_shape=out_shape,
        scratch_shapes=[pltpu.VMEM((PACK_ROWS, D), F32), pltpu.VMEM((NDEV, PACK_ROWS, D), F32),
                        pltpu.VMEM((PACK_ROWS, D), F32),
                        pltpu.SemaphoreType.DMA((NDEV - 1,)), pltpu.SemaphoreType.DMA((NDEV - 1,))],
    )(*args)
    return {name: tuple(res[4 * i:4 * i + 4]) for i, name in enumerate(SMALL)}


MATS = ("w_in", "w_o_attn", "w_pw_conv", "w_out", "w_ffn_in", "w_ffn_out")
COL_SHARDED = ("w_in", "w_o_attn", "w_pw_conv", "w_ffn_in")
WEIGHTS = ("norm1_w", "w_in", "b_gate", "q_norm_w", "k_norm_w", "w_o_attn", "conv_w", "conv_b", "conv_ln_w",
           "conv_ln_b", "w_pw_conv", "w_out", "norm2_w", "w_ffn_in", "w_ffn_out")


def _from_blocks(name, blocks):
    n, R, C = blocks.shape
    if name in COL_SHARDED or name in ("b_gate", "conv_w"):
        return blocks.transpose(1, 0, 2).reshape(R, n * C)
    return blocks.reshape(n * R, C)


def _to_blocks(name, full):
    if name in COL_SHARDED:
        R, C = full.shape
        return full.reshape(R, NDEV, C // NDEV).transpose(1, 0, 2)
    return full.reshape(NDEV, full.shape[0] // NDEV, full.shape[1])


def kernel(x, positions, norm1_w, w_in, b_gate, q_norm_w, k_norm_w, w_o_attn, conv_w, conv_b, conv_ln_w, conv_ln_b, w_pw_conv, w_out, norm2_w, w_ffn_in, w_ffn_out, loss_target, m_norm1_w, m_w_in, m_b_gate, m_q_norm_w, m_k_norm_w, m_w_o_attn, m_conv_w, m_conv_b, m_conv_ln_w, m_conv_ln_b, m_w_pw_conv, m_w_out, m_norm2_w, m_w_ffn_in, m_w_ffn_out, v_norm1_w, v_w_in, v_b_gate, v_q_norm_w, v_k_norm_w, v_w_o_attn, v_conv_w, v_conv_b, v_conv_ln_w, v_conv_ln_b, v_w_pw_conv, v_w_out, v_norm2_w, v_w_ffn_in, v_w_ffn_out):
    w = dict(norm1_w=norm1_w, w_in=w_in, b_gate=b_gate, q_norm_w=q_norm_w, k_norm_w=k_norm_w, w_o_attn=w_o_attn,
             conv_w=conv_w, conv_b=conv_b, conv_ln_w=conv_ln_w, conv_ln_b=conv_ln_b, w_pw_conv=w_pw_conv,
             w_out=w_out, norm2_w=norm2_w, w_ffn_in=w_ffn_in, w_ffn_out=w_ffn_out)
    m = dict(norm1_w=m_norm1_w, w_in=m_w_in, b_gate=m_b_gate, q_norm_w=m_q_norm_w, k_norm_w=m_k_norm_w,
             w_o_attn=m_w_o_attn, conv_w=m_conv_w, conv_b=m_conv_b, conv_ln_w=m_conv_ln_w,
             conv_ln_b=m_conv_ln_b, w_pw_conv=m_w_pw_conv, w_out=m_w_out, norm2_w=m_norm2_w,
             w_ffn_in=m_w_ffn_in, w_ffn_out=m_w_ffn_out)
    v = dict(norm1_w=v_norm1_w, w_in=v_w_in, b_gate=v_b_gate, q_norm_w=v_q_norm_w, k_norm_w=v_k_norm_w,
             w_o_attn=v_w_o_attn, conv_w=v_conv_w, conv_b=v_conv_b, conv_ln_w=v_conv_ln_w,
             conv_ln_b=v_conv_ln_b, w_pw_conv=v_w_pw_conv, w_out=v_w_out, norm2_w=v_norm2_w,
             w_ffn_in=v_w_ffn_in, w_ffn_out=v_w_ffn_out)
    two_d = lambda t: {k: (a[0] if a.ndim == 3 else a) for k, a in t.items()}
    w, m, v = two_d(w), two_d(m), two_d(v)

    gathered_names = MATS + ("b_gate", "conv_w")
    gathered = all_gather([w[k] for k in gathered_names], [BF16] * len(MATS) + [F32, F32])
    p = {k: _from_blocks(k, blocks) for k, blocks in zip(gathered_names, gathered)}
    for k in ("norm1_w", "q_norm_w", "k_norm_w", "conv_b", "conv_ln_w", "conv_ln_b", "norm2_w"):
        p[k] = w[k]

    sq, grad_x, g = local_step(x[0], positions.reshape(S, 1), loss_target[0], p)
    loss = lax.psum((0.5 / D) * jnp.sum(sq), ("x", "y", "c"))

    c_idx = lax.axis_index("c").astype(jnp.int32)
    chip_idx = (2 * lax.axis_index("x") + lax.axis_index("y")).astype(jnp.int32)
    blocks = [_to_blocks(k, g[k]) for k in MATS]
    from_sibling = rs_sibling(blocks)
    parts, owns = zip(*[chip_sum("chip_sum_" + k, b, r, c_idx, chip_idx)
                        for k, b, r in zip(MATS, blocks, from_sibling)])
    from_chips = rs_chips(list(parts))
    res = {k: tuple(shard_adam("adam_" + k, own, r, w[k], m[k], v[k]))
           for k, own, r in zip(MATS, owns, from_chips)}
    res.update(small_sync_adam(g, w, m, v))

    def shaped(name, a):
        return a.reshape((1,) + a.shape) if name in MATS or name in ("b_gate", "conv_w") else a

    outs = [loss, grad_x.reshape(1, S, D)]
    for i in range(4):
        outs += [shaped(k, res[k][i]) for k in WEIGHTS]
    return tuple(outs)
```

```python
import functools
from typing import Callable, NamedTuple

import numpy as np
import jax
import jax.numpy as jnp
from jax import lax
from jax.experimental import pallas as pl
from jax.experimental.pallas import tpu as pltpu

F32 = jnp.float32
BF16 = jnp.bfloat16

S = 2048
D = 1024
HD = 64
QKV = 1536
CC = 512
KW = 31
FF = 2816
INW = 7680
OFF_Q, OFF_K, OFF_V, OFF_CA, OFF_CB, OFF_GA, OFF_GB = 0, 1536, 3072, 4608, 5120, 5632, 6656
DILATIONS = (1, 4, 16)
HALF_SPAN = 64
EPS = 1e-6
NEG_INF = -1e30
ROPE_THETA = 500000.0
ROT_DIM = 16

ADAM_LR = 0.001
ADAM_B1 = 0.9
ADAM_B2 = 0.999
ADAM_EPS = 1e-08
ADAM_WD = 0.01
ADAM_STEP = 10

NDEV = 8
LANES = 128
TM = 256
TQ = 128
VMEM_LIMIT = 56 * 1024 * 1024
MESH = pl.DeviceIdType.MESH


def _cp(**kw):
    return pltpu.CompilerParams(vmem_limit_bytes=VMEM_LIMIT, **kw)


def _row(width, col=0, tm=TM):
    return pl.BlockSpec((tm, width), lambda i: (i, col))


def _res(shape):
    nd = len(shape)
    return pl.BlockSpec(shape, lambda *_: (0,) * nd, pipeline_mode=pl.Buffered(1))


def _dot(a, b):
    return jnp.dot(a, b, preferred_element_type=F32)


def _dot_nt(a, b):
    return lax.dot_general(a, b, (((1,), (1,)), ((), ())), preferred_element_type=F32)


def _dot_tn(a, b):
    return lax.dot_general(a, b, (((0,), (0,)), ((), ())), preferred_element_type=F32)


def _sigmoid(x):
    return jax.nn.sigmoid(x)


def _dsilu(x, sg):
    return sg * (1.0 + x * (1.0 - sg))


ANY = pl.BlockSpec(memory_space=pl.ANY)
VMEM = pl.BlockSpec(memory_space=pltpu.VMEM)


class Side(NamedTuple):
    args: tuple
    in_specs: tuple
    out_shape: tuple
    scratch: tuple
    start: Callable
    finish: Callable


def _call(body, sides=(), *, name, grid, in_specs, out_specs, out_shape, scratch_shapes=(), args):
    ni, no, ns = len(in_specs), len(out_specs), len(scratch_shapes)
    cnt = [(len(s.args), len(s.out_shape), len(s.scratch)) for s in sides]

    def take(refs, pos, n):
        return refs[pos:pos + n], pos + n

    def full(*refs):
        m_in, pos = take(refs, 0, ni)
        s_in = []
        for a, _, _ in cnt:
            r, pos = take(refs, pos, a)
            s_in.append(r)
        m_out, pos = take(refs, pos, no)
        s_out = []
        for _, o, _ in cnt:
            r, pos = take(refs, pos, o)
            s_out.append(r)
        m_scr, pos = take(refs, pos, ns)
        s_scr = []
        for _, _, c in cnt:
            r, pos = take(refs, pos, c)
            s_scr.append(r)
        if sides:
            first = functools.reduce(jnp.logical_and, [pl.program_id(d) == 0 for d in range(len(grid))])
            last = functools.reduce(jnp.logical_and, [pl.program_id(d) == g - 1 for d, g in enumerate(grid)])

            @pl.when(first)
            def _():
                for s, a, o, c in zip(sides, s_in, s_out, s_scr):
                    s.start(a, o, c)

        body(*m_in, *m_out, *m_scr)
        if sides:
            @pl.when(last)
            def _():
                for s, a, o, c in zip(sides, s_in, s_out, s_scr):
                    s.finish(a, o, c)

    res = pl.pallas_call(
        full, name=name, grid=grid,
        in_specs=list(in_specs) + [sp for s in sides for sp in s.in_specs],
        out_specs=list(out_specs) + [ANY for s in sides for _ in s.out_shape],
        out_shape=list(out_shape) + [o for s in sides for o in s.out_shape],
        scratch_shapes=list(scratch_shapes) + [c for s in sides for c in s.scratch],
        compiler_params=_cp(dimension_semantics=("arbitrary",) * len(grid)),
    )(*args, *[a for s in sides for a in s.args])
    res = list(res)
    if not sides:
        return res
    outs, pos = take(res, 0, no)
    side_outs = []
    for _, o, _ in cnt:
        r, pos = take(res, pos, o)
        side_outs.append(r)
    return outs, side_outs


def _inv_freq_lanes():
    inv = np.float32(ROPE_THETA) ** (-np.arange(0, ROT_DIM, 2, dtype=np.float32) / np.float32(ROT_DIM))
    lane = np.arange(LANES) % HD
    out = np.where(lane < ROT_DIM, inv[lane % (ROT_DIM // 2)], 0.0).astype(np.float32)
    return jnp.asarray(out.reshape(1, LANES))


def rope_tables(pos_col):
    def body(p_ref, f_ref, c_ref, s1_ref, s2_ref):
        ang = p_ref[...].astype(F32) * f_ref[...]
        lane = lax.broadcasted_iota(jnp.int32, ang.shape, 1) % HD
        cs = jnp.cos(ang)
        sn = jnp.sin(ang)
        c_ref[...] = jnp.where(lane < ROT_DIM, cs, 1.0)
        s1_ref[...] = jnp.where(lane < ROT_DIM // 2, -sn, 0.0)
        s2_ref[...] = jnp.where(lane < ROT_DIM // 2, 0.0, jnp.where(lane < ROT_DIM, sn, 0.0))

    sds = jax.ShapeDtypeStruct((S, LANES), F32)
    return pl.pallas_call(
        body, name="rope_tables", grid=(S // TM,),
        in_specs=[_row(1), pl.BlockSpec((1, LANES), lambda i: (0, 0))],
        out_specs=[_row(LANES)] * 3, out_shape=[sds] * 3,
    )(pos_col, _inv_freq_lanes())


def _rope(v, c, s1, s2):
    return v * c + pltpu.roll(v, LANES - 8, axis=1) * s1 + pltpu.roll(v, 8, axis=1) * s2


def _rope_t(d, c, s1, s2):
    return d * c - pltpu.roll(d, LANES - 8, axis=1) * s1 - pltpu.roll(d, 8, axis=1) * s2


def _head_mat():
    r = lax.broadcasted_iota(jnp.int32, (LANES, LANES), 0) // HD
    c = lax.broadcasted_iota(jnp.int32, (LANES, LANES), 1) // HD
    return jnp.where(r == c, 1.0 / HD, 0.0).astype(BF16)


def _head_mean(t, e):
    hi = t.astype(BF16)
    rest = (t - hi.astype(F32)).astype(BF16)
    return _dot(hi, e) + _dot(rest, e)


def in_proj(x, norm_w, w_in, sides=()):
    nchunk = 5
    cw = INW // nchunk

    def body(x_ref, nw_ref, w_ref, h_ref, p_ref):
        xv = x_ref[...]
        r = lax.rsqrt(jnp.mean(xv * xv, axis=-1, keepdims=True) + EPS)
        h = (xv * r * nw_ref[...]).astype(BF16)
        h_ref[...] = h
        for j in range(nchunk):
            p_ref[:, j * cw:(j + 1) * cw] = _dot(h, w_ref[:, j * cw:(j + 1) * cw])

    return _call(
        body, sides, name="in_proj", grid=(S // TM,),
        in_specs=[_row(D), _res((1, D)), _res((D, INW))],
        out_specs=[_row(D), _row(INW)],
        out_shape=[jax.ShapeDtypeStruct((S, D), BF16), jax.ShapeDtypeStruct((S, INW), F32)],
        args=(x, norm_w, w_in))


def _qk_specs():
    nb = QKV // LANES
    return [pl.BlockSpec((S, LANES), functools.partial(lambda hp, g, o: (0, o + g * 4 + hp), o=o))
            for o in (OFF_Q // LANES, OFF_K // LANES, OFF_V // LANES)]


def _tab_specs():
    return [pl.BlockSpec((S, LANES), lambda hp, g: (0, 0), pipeline_mode=pl.Buffered(1))] * 3


def _vec_spec():
    return pl.BlockSpec((1, LANES), lambda hp, g: (0, 0))


def _sub_rows(r, d, start, n):
    if d == 1:
        return pl.ds(start, n)
    return pl.ds(r + d * start, n, stride=d)


def _band_window(i, L):
    W = min(2 * TQ, L)
    q0 = pl.multiple_of(i * TQ, TQ)
    k0 = pl.multiple_of(jnp.clip(q0 - HALF_SPAN, 0, L - W), HALF_SPAN)
    qpos = q0 + (lax.broadcasted_iota(jnp.int32, (2 * TQ, W), 0) & (TQ - 1))
    kpos = k0 + lax.broadcasted_iota(jnp.int32, (2 * TQ, W), 1)
    valid = jnp.abs(qpos - kpos) <= HALF_SPAN
    return W, q0, k0, valid


def _stack_heads(t, lo):
    z = jnp.zeros_like(t)
    return jnp.concatenate([jnp.where(lo, t, z), jnp.where(lo, z, t)], axis=0)


def _unstack_heads(t2, lo):
    return jnp.where(lo, t2[0:TQ], t2[TQ:2 * TQ])


CHAINS = 4


def _interleave(d):
    ru = min(d, CHAINS)
    return ru, min(CHAINS // ru, S // d // TQ)


def _for_blocks(n, fn):
    if n == 1:
        fn(0)
    else:
        def it(j, _):
            fn(j)
            return 0
        lax.fori_loop(0, n, it, 0)


def attn_fwd(proj, tabs, qw2, kw2, sides=()):
    CH = 256

    def body(q_ref, k_ref, v_ref, c_ref, s1_ref, s2_ref, qw_ref, kw_ref, at_ref, ls_ref,
             qs, ks, vs, osub, lsub, onat, lnat):
        g = pl.program_id(1)
        lo = lax.broadcasted_iota(jnp.int32, (1, LANES), 1) < HD
        e = _head_mat()

        def prep(t, w, c, s1, s2):
            r = lax.rsqrt(_head_mean(t * t, e) + EPS)
            return _rope(t * r * w, c, s1, s2)

        def group(gi, d):
            L = S // d

            ru, nb = _interleave(d)

            def stage(r, off):
                for c0 in range(0, L, CH):
                    n = min(CH, L)
                    rows = _sub_rows(r, d, c0, n)
                    c, s1, s2 = c_ref[rows, :], s1_ref[rows, :], s2_ref[rows, :]
                    dst = pl.ds(off + c0, n)
                    qs[dst, :] = (prep(q_ref[rows, :], qw_ref[...], c, s1, s2) * (HD ** -0.5)).astype(BF16)
                    ks[dst, :] = prep(k_ref[rows, :], kw_ref[...], c, s1, s2).astype(BF16)
                    vs[dst, :] = v_ref[rows, :].astype(BF16)

            def one(off, i):
                W, q0, k0, valid = _band_window(i, L)
                q2 = _stack_heads(qs[pl.ds(off + q0, TQ), :], lo)
                sc = jnp.where(valid, _dot_nt(q2, ks[pl.ds(off + k0, W), :]), NEG_INF)
                m = jnp.max(sc, axis=-1, keepdims=True)
                p = jnp.exp(sc - m)
                den = jnp.sum(p, axis=-1, keepdims=True)
                o2 = _dot(p.astype(BF16), vs[pl.ds(off + k0, W), :]) / den
                l2 = jnp.broadcast_to(m + jnp.log(den), (2 * TQ, LANES))
                osub[pl.ds(off + q0, TQ), :] = _unstack_heads(o2, lo)
                lsub[pl.ds(off + q0, TQ), :] = _unstack_heads(l2, lo)

            def unstage(r, off):
                for c0 in range(0, L, CH):
                    n = min(CH, L)
                    rows = _sub_rows(r, d, c0, n)
                    onat[gi, rows, :] = osub[pl.ds(off + c0, n), :]
                    lnat[gi, rows, :] = lsub[pl.ds(off + c0, n), :]

            def step(t, _):
                for u in range(ru):
                    stage(t * ru + u, u * L)
                _for_blocks(L // TQ // nb, lambda j: [one(u * L, j * nb + b) for u in range(ru) for b in range(nb)])
                for u in range(ru):
                    unstage(t * ru + u, u * L)
                return 0

            lax.fori_loop(0, d // ru, step, 0)

        for gi, d in enumerate(DILATIONS):
            pl.when(g == gi)(functools.partial(group, gi, d))

        @pl.when(g == len(DILATIONS) - 1)
        def _():
            def mix(i, _):
                rows = pl.ds(pl.multiple_of(i * CH, CH), CH)
                l0, l1, l2 = lnat[0, rows, :], lnat[1, rows, :], lnat[2, rows, :]
                m = jnp.maximum(jnp.maximum(l0, l1), l2)
                e0, e1, e2 = jnp.exp(l0 - m), jnp.exp(l1 - m), jnp.exp(l2 - m)
                den = e0 + e1 + e2
                a = (e0 * onat[0, rows, :] + e1 * onat[1, rows, :] + e2 * onat[2, rows, :]) / den
                at_ref[rows, :] = a.astype(BF16)
                ls_ref[rows, :] = m + jnp.log(den)
                return 0

            lax.fori_loop(0, S // CH, mix, 0)

    out_spec = pl.BlockSpec((S, LANES), lambda hp, g: (0, hp))
    return _call(
        body, sides, name="attn_fwd", grid=(4, 3),
        in_specs=_qk_specs() + _tab_specs() + [_vec_spec(), _vec_spec()],
        out_specs=[out_spec, out_spec],
        out_shape=[jax.ShapeDtypeStruct((S, CC), BF16), jax.ShapeDtypeStruct((S, CC), F32)],
        scratch_shapes=[pltpu.VMEM((S, LANES), BF16)] * 3 + [pltpu.VMEM((S, LANES), F32)] * 2
        + [pltpu.VMEM((3, S, LANES), F32)] * 2,
        args=(proj, proj, proj, *tabs, qw2, kw2))


def attn_bwd(proj, tabs, qw2, kw2, d_attn, attn, lse, sides=()):
    CH = 256

    def body(q_ref, k_ref, v_ref, c_ref, s1_ref, s2_ref, qw_ref, kw_ref, do_ref, at_ref, ls_ref,
             dq_ref, dk_ref, dv_ref, gqw_ref, gkw_ref,
             qs, ks, vs, dos, dsub, lsub, dqs, dks, dvs, dnat, dqn, dkn, dvn):
        hp, g = pl.program_id(0), pl.program_id(1)
        lo = lax.broadcasted_iota(jnp.int32, (1, LANES), 1) < HD
        e = _head_mat()

        @pl.when((hp == 0) & (g == 0))
        def _():
            gqw_ref[...] = jnp.zeros_like(gqw_ref)
            gkw_ref[...] = jnp.zeros_like(gkw_ref)

        def dsum(i, _):
            rows = pl.ds(pl.multiple_of(i * CH, CH), CH)
            dnat[rows, :] = _head_mean(do_ref[rows, :] * at_ref[rows, :].astype(F32), e) * float(HD)
            return 0

        lax.fori_loop(0, S // CH, dsum, 0)

        def group(d):
            L = S // d

            ru, nb = _interleave(d)

            def stage(r, off):
                for c0 in range(0, L, CH):
                    n = min(CH, L)
                    rows = _sub_rows(r, d, c0, n)
                    c, s1, s2 = c_ref[rows, :], s1_ref[rows, :], s2_ref[rows, :]
                    dst = pl.ds(off + c0, n)
                    qv, kv = q_ref[rows, :], k_ref[rows, :]
                    rq = lax.rsqrt(_head_mean(qv * qv, e) + EPS)
                    rk = lax.rsqrt(_head_mean(kv * kv, e) + EPS)
                    qs[dst, :] = (_rope(qv * rq * qw_ref[...], c, s1, s2) * (HD ** -0.5)).astype(BF16)
                    ks[dst, :] = _rope(kv * rk * kw_ref[...], c, s1, s2).astype(BF16)
                    vs[dst, :] = v_ref[rows, :].astype(BF16)
                    dos[dst, :] = do_ref[rows, :].astype(BF16)
                    dsub[dst, :] = dnat[rows, :]
                    lsub[dst, :] = ls_ref[rows, :]
                    dks[dst, :] = jnp.zeros((n, LANES), F32)
                    dvs[dst, :] = jnp.zeros((n, LANES), F32)

            def one(off, i):
                W, q0, k0, valid = _band_window(i, L)
                qrows, krows = pl.ds(off + q0, TQ), pl.ds(off + k0, W)
                q2 = _stack_heads(qs[qrows, :], lo)
                do2 = _stack_heads(dos[qrows, :], lo)
                kk, vv = ks[krows, :], vs[krows, :]
                lse_b, dd_b = lsub[qrows, :], dsub[qrows, :]
                lse2 = jnp.concatenate([lse_b[:, 0:1], lse_b[:, HD:HD + 1]], axis=0)
                dd2 = jnp.concatenate([dd_b[:, 0:1], dd_b[:, HD:HD + 1]], axis=0)
                sc = jnp.where(valid, _dot_nt(q2, kk), NEG_INF)
                p = jnp.exp(sc - lse2)
                ds = (p * (_dot_nt(do2, vv) - dd2)).astype(BF16)
                dqs[qrows, :] = _unstack_heads(_dot(ds, kk), lo)
                dks[krows, :] = dks[krows, :] + _dot_tn(ds, q2)
                dvs[krows, :] = dvs[krows, :] + _dot_tn(p.astype(BF16), do2)

            def unstage(r, off):
                gq = jnp.zeros((1, LANES), F32)
                gk = jnp.zeros((1, LANES), F32)
                for c0 in range(0, L, CH):
                    n = min(CH, L)
                    rows = _sub_rows(r, d, c0, n)
                    src = pl.ds(off + c0, n)
                    c, s1, s2 = c_ref[rows, :], s1_ref[rows, :], s2_ref[rows, :]
                    for (raw_ref, w_ref, gsub, scale, nat) in (
                            (q_ref, qw_ref, dqs, HD ** -0.5, dqn), (k_ref, kw_ref, dks, 1.0, dkn)):
                        t = raw_ref[rows, :]
                        rr = lax.rsqrt(_head_mean(t * t, e) + EPS)
                        tn = t * rr
                        dy = _rope_t(gsub[src, :] * scale, c, s1, s2)
                        gw = jnp.sum(dy * tn, axis=0, keepdims=True)
                        if raw_ref is q_ref:
                            gq = gq + gw
                        else:
                            gk = gk + gw
                        dtn = dy * w_ref[...]
                        nat[rows, :] = rr * (dtn - tn * _head_mean(dtn * tn, e))
                    dvn[rows, :] = dvs[src, :]
                gqw_ref[0:1, :] = gqw_ref[0:1, :] + gq
                gkw_ref[0:1, :] = gkw_ref[0:1, :] + gk

            def step(t, _):
                for u in range(ru):
                    stage(t * ru + u, u * L)
                _for_blocks(L // TQ // nb, lambda j: [one(u * L, j * nb + b) for u in range(ru) for b in range(nb)])
                for u in range(ru):
                    unstage(t * ru + u, u * L)
                return 0

            lax.fori_loop(0, d // ru, step, 0)

        for gi, d in enumerate(DILATIONS):
            pl.when(g == gi)(functools.partial(group, d))

        def emit(i, _):
            rows = pl.ds(pl.multiple_of(i * CH, CH), CH)
            dq_ref[rows, :] = dqn[rows, :].astype(BF16)
            dk_ref[rows, :] = dkn[rows, :].astype(BF16)
            dv_ref[rows, :] = dvn[rows, :].astype(BF16)
            return 0

        lax.fori_loop(0, S // CH, emit, 0)

    nat_spec = pl.BlockSpec((S, LANES), lambda hp, g: (0, hp))
    out_spec = pl.BlockSpec((S, LANES), lambda hp, g: (0, g * 4 + hp))
    acc_spec = pl.BlockSpec((8, LANES), lambda hp, g: (0, 0))
    return _call(
        body, sides, name="attn_bwd", grid=(4, 3),
        in_specs=_qk_specs() + _tab_specs() + [_vec_spec(), _vec_spec(), nat_spec, nat_spec, nat_spec],
        out_specs=[out_spec] * 3 + [acc_spec] * 2,
        out_shape=[jax.ShapeDtypeStruct((S, QKV), BF16)] * 3 + [jax.ShapeDtypeStruct((8, LANES), F32)] * 2,
        scratch_shapes=[pltpu.VMEM((S, LANES), BF16)] * 4 + [pltpu.VMEM((S, LANES), F32)] * 9,
        args=(proj, proj, proj, *tabs, qw2, kw2, d_attn, attn, lse))


PADR = 16
CT = 128


def _conv_specs():
    return [pl.BlockSpec((S, CC), lambda i: (0, OFF_CA // CC)), pl.BlockSpec((S, CC), lambda i: (0, OFF_CB // CC))]


NCB = CC // LANES


def _pad_zero(pad):
    for cb in range(NCB):
        pad[cb, 0:PADR, :] = jnp.zeros((PADR, LANES), F32)
        pad[cb, PADR + S:PADR + S + PADR, :] = jnp.zeros((PADR, LANES), F32)


def _pad_store(pad, row0, n, val):
    for cb in range(NCB):
        pad[cb, pl.ds(pl.multiple_of(row0 + PADR, 8), n), :] = val[:, cb * LANES:(cb + 1) * LANES]


def _taps(pad_ref, cb, s0, weights):
    acc = jnp.zeros((CT, LANES), F32)
    for k in range(KW):
        acc = acc + weights[k] * pad_ref[cb, pl.ds(s0 + k + 1, CT), :]
    return acc


def conv_fwd(proj, conv_w, conv_b, ln_w, ln_b):
    def body(a_ref, b_ref, w_ref, cb_ref, lw_ref, lb_ref, c_ref, u3_ref, upad):
        _pad_zero(upad)

        def glu(i, _):
            rows = pl.ds(pl.multiple_of(i * TM, TM), TM)
            _pad_store(upad, i * TM, TM, a_ref[rows, :] * _sigmoid(b_ref[rows, :]))
            return 0

        lax.fori_loop(0, S // TM, glu, 0)

        def chunk(i, _):
            s0 = pl.multiple_of(i * CT, CT)
            for cb in range(CC // LANES):
                cols = slice(cb * LANES, (cb + 1) * LANES)
                w = [w_ref[k:k + 1, cols] for k in range(KW)]
                c_ref[pl.ds(s0, CT), cols] = _taps(upad, cb, s0, w) + cb_ref[:, cols]
            cv = c_ref[pl.ds(s0, CT), :]
            mu = jnp.mean(cv, axis=-1, keepdims=True)
            xc = cv - mu
            rstd = lax.rsqrt(jnp.mean(xc * xc, axis=-1, keepdims=True) + EPS)
            yl = xc * rstd * lw_ref[...] + lb_ref[...]
            u3_ref[pl.ds(s0, CT), :] = (yl * _sigmoid(yl)).astype(BF16)
            return 0

        lax.fori_loop(0, S // CT, chunk, 0)

    vec = pl.BlockSpec((1, CC), lambda i: (0, 0))
    full = pl.BlockSpec((S, CC), lambda i: (0, 0))
    return pl.pallas_call(
        body, name="conv_fwd", grid=(1,),
        in_specs=_conv_specs() + [pl.BlockSpec((KW, CC), lambda i: (0, 0)), vec, vec, vec],
        out_specs=[full, full],
        out_shape=[jax.ShapeDtypeStruct((S, CC), F32), jax.ShapeDtypeStruct((S, CC), BF16)],
        scratch_shapes=[pltpu.VMEM((NCB, S + 2 * PADR, LANES), F32)],
        compiler_params=_cp(dimension_semantics=("arbitrary",)),
    )(proj, proj, conv_w, conv_b, ln_w, ln_b)


def conv_bwd(proj, cpre, d_u3, conv_w, conv_w_rev, ln_w, ln_b, sides=()):
    def body(a_ref, b_ref, c_ref, du3_ref, w_ref, wr_ref, lw_ref, lb_ref,
             dc_ref, gw_ref, gcb_ref, glw_ref, glb_ref, upad, dpad):
        _pad_zero(upad)
        _pad_zero(dpad)
        gw_ref[...] = jnp.zeros_like(gw_ref)

        def ln_bwd(i, carry):
            gcb, glw, glb = carry
            rows = pl.ds(pl.multiple_of(i * TM, TM), TM)
            _pad_store(upad, i * TM, TM, a_ref[rows, :] * _sigmoid(b_ref[rows, :]))
            cv = c_ref[rows, :]
            mu = jnp.mean(cv, axis=-1, keepdims=True)
            xc = cv - mu
            rstd = lax.rsqrt(jnp.mean(xc * xc, axis=-1, keepdims=True) + EPS)
            xh = xc * rstd
            yl = xh * lw_ref[...] + lb_ref[...]
            dyl = du3_ref[rows, :] * _dsilu(yl, _sigmoid(yl))
            dxh = dyl * lw_ref[...]
            dcv = rstd * (dxh - jnp.mean(dxh, axis=-1, keepdims=True)
                          - xh * jnp.mean(dxh * xh, axis=-1, keepdims=True))
            _pad_store(dpad, i * TM, TM, dcv)
            return (gcb + jnp.sum(dcv, axis=0, keepdims=True),
                    glw + jnp.sum(dyl * xh, axis=0, keepdims=True),
                    glb + jnp.sum(dyl, axis=0, keepdims=True))

        z = jnp.zeros((1, CC), F32)
        gcb, glw, glb = lax.fori_loop(0, S // TM, ln_bwd, (z, z, z))
        gcb_ref[...] = gcb
        glw_ref[...] = glw
        glb_ref[...] = glb

        def chunk(i, _):
            s0 = pl.multiple_of(i * CT, CT)
            for cb in range(CC // LANES):
                cols = slice(cb * LANES, (cb + 1) * LANES)
                wr = [wr_ref[k:k + 1, cols] for k in range(KW)]
                du = _taps(dpad, cb, s0, wr)
                dcv = dpad[cb, pl.ds(s0 + PADR, CT), :]
                for k in range(KW):
                    gw_ref[k:k + 1, cols] = gw_ref[k:k + 1, cols] + jnp.sum(
                        upad[cb, pl.ds(s0 + k + 1, CT), :] * dcv, axis=0, keepdims=True)
                av = a_ref[pl.ds(s0, CT), cols]
                sb = _sigmoid(b_ref[pl.ds(s0, CT), cols])
                dc_ref[pl.ds(s0, CT), cols] = (du * sb).astype(BF16)
                dc_ref[pl.ds(s0, CT), slice(CC + cb * LANES, CC + (cb + 1) * LANES)] = (
                    du * av * sb * (1.0 - sb)).astype(BF16)
            return 0

        lax.fori_loop(0, S // CT, chunk, 0)

    vec = pl.BlockSpec((1, CC), lambda i: (0, 0))
    full = pl.BlockSpec((S, CC), lambda i: (0, 0))
    wsp = pl.BlockSpec((KW, CC), lambda i: (0, 0))
    return _call(
        body, sides, name="conv_bwd", grid=(1,),
        in_specs=_conv_specs() + [full, full, wsp, wsp, vec, vec],
        out_specs=[pl.BlockSpec((S, 2 * CC), lambda i: (0, 0)), wsp, vec, vec, vec],
        out_shape=[jax.ShapeDtypeStruct((S, 2 * CC), BF16), jax.ShapeDtypeStruct((KW, CC), F32)]
        + [jax.ShapeDtypeStruct((1, CC), F32)] * 3,
        scratch_shapes=[pltpu.VMEM((NCB, S + 2 * PADR, LANES), F32)] * 2,
        args=(proj, proj, cpre, d_u3, conv_w, conv_w_rev, ln_w, ln_b))


def _gate_specs():
    return [_row(CC, col=OFF_GA // CC + j) for j in range(4)]


def _gates(g_refs, bg_ref):
    ga = _sigmoid(jnp.concatenate([g_refs[0][...], g_refs[1][...]], axis=1) + bg_ref[0:1, :])
    gb = _sigmoid(jnp.concatenate([g_refs[2][...], g_refs[3][...]], axis=1) + bg_ref[1:2, :])
    return ga, gb


def mix_out(x, proj, b_gate, attn, u3, w_o, w_pw, w_out):
    def body(x_ref, g0, g1, g2, g3, bg_ref, at_ref, u3_ref, wo_ref, wp_ref, wout_ref,
             x1_ref, z_ref, ya_ref, yb_ref):
        ga, gb = _gates((g0, g1, g2, g3), bg_ref)
        ya = _dot(at_ref[...], wo_ref[...])
        yb = _dot(u3_ref[...], wp_ref[...])
        z = (ga * ya + gb * yb).astype(BF16)
        ya_ref[...] = ya.astype(BF16)
        yb_ref[...] = yb.astype(BF16)
        z_ref[...] = z
        x1_ref[...] = x_ref[...] + _dot(z, wout_ref[...])

    return pl.pallas_call(
        body, name="mix_out", grid=(S // TM,),
        in_specs=[_row(D)] + _gate_specs() + [_res((2, D)), _row(CC), _row(CC),
                                              _res((CC, D)), _res((CC, D)), _res((D, D))],
        out_specs=[_row(D)] * 4,
        out_shape=[jax.ShapeDtypeStruct((S, D), F32)] + [jax.ShapeDtypeStruct((S, D), BF16)] * 3,
        compiler_params=_cp(dimension_semantics=("arbitrary",)),
    )(x, proj, proj, proj, proj, b_gate, attn, u3, w_o, w_pw, w_out)


def out_bwd(d_x1b, proj, b_gate, ya, yb, w_o, w_pw, w_out, sides=()):
    def body(dx_ref, g0, g1, g2, g3, bg_ref, ya_ref, yb_ref, wo_ref, wp_ref, wout_ref,
             dya_ref, dyb_ref, dgl_ref, dat_ref, du3_ref, gbg_ref):
        @pl.when(pl.program_id(0) == 0)
        def _():
            gbg_ref[...] = jnp.zeros_like(gbg_ref)

        ga, gb = _gates((g0, g1, g2, g3), bg_ref)
        dz = _dot_nt(dx_ref[...], wout_ref[...])
        dya = (dz * ga).astype(BF16)
        dyb = (dz * gb).astype(BF16)
        dgla = dz * ya_ref[...].astype(F32) * ga * (1.0 - ga)
        dglb = dz * yb_ref[...].astype(F32) * gb * (1.0 - gb)
        dya_ref[...] = dya
        dyb_ref[...] = dyb
        dgl_ref[:, 0:D] = dgla.astype(BF16)
        dgl_ref[:, D:2 * D] = dglb.astype(BF16)
        gbg_ref[0:1, :] = gbg_ref[0:1, :] + jnp.sum(dgla, axis=0, keepdims=True)
        gbg_ref[1:2, :] = gbg_ref[1:2, :] + jnp.sum(dglb, axis=0, keepdims=True)
        dat_ref[...] = _dot_nt(dya, wo_ref[...])
        du3_ref[...] = _dot_nt(dyb, wp_ref[...])

    return _call(
        body, sides, name="out_bwd", grid=(S // TM,),
        in_specs=[_row(D)] + _gate_specs() + [_res((2, D)), _row(D), _row(D),
                                              _res((CC, D)), _res((CC, D)), _res((D, D))],
        out_specs=[_row(D), _row(D), _row(2 * D), _row(CC), _row(CC), pl.BlockSpec((2, D), lambda i: (0, 0))],
        out_shape=[jax.ShapeDtypeStruct((S, D), BF16)] * 2 + [jax.ShapeDtypeStruct((S, 2 * D), BF16)]
        + [jax.ShapeDtypeStruct((S, CC), F32)] * 2 + [jax.ShapeDtypeStruct((2, D), F32)],
        args=(d_x1b, proj, proj, proj, proj, b_gate, ya, yb, w_o, w_pw, w_out))


def ffn_in(x1, norm_w, w_ffn_in):
    half = FF // 2

    def body(x_ref, nw_ref, w_ref, h_ref, gu_ref, f_ref):
        xv = x_ref[...]
        r = lax.rsqrt(jnp.mean(xv * xv, axis=-1, keepdims=True) + EPS)
        h = (xv * r * nw_ref[...]).astype(BF16)
        h_ref[...] = h
        for j in range(2):
            gt = _dot(h, w_ref[:, j * half:(j + 1) * half])
            up = _dot(h, w_ref[:, FF + j * half:FF + (j + 1) * half])
            gu_ref[:, j * half:(j + 1) * half] = gt.astype(BF16)
            gu_ref[:, FF + j * half:FF + (j + 1) * half] = up.astype(BF16)
            f_ref[:, j * half:(j + 1) * half] = (gt * _sigmoid(gt) * up).astype(BF16)

    return pl.pallas_call(
        body, name="ffn_in", grid=(S // TM,),
        in_specs=[_row(D), _res((1, D)), _res((D, 2 * FF))],
        out_specs=[_row(D), _row(2 * FF), _row(FF)],
        out_shape=[jax.ShapeDtypeStruct((S, D), BF16), jax.ShapeDtypeStruct((S, 2 * FF), BF16),
                   jax.ShapeDtypeStruct((S, FF), BF16)],
        compiler_params=_cp(dimension_semantics=("arbitrary",)),
    )(x1, norm_w, w_ffn_in)


def ffn_out_loss(x1, f, w_ffn_out, target):
    def body(x_ref, f_ref, w_ref, t_ref, dy_ref, dyb_ref, sq_ref):
        @pl.when(pl.program_id(0) == 0)
        def _():
            sq_ref[...] = jnp.zeros_like(sq_ref)

        diff = x_ref[...] + _dot(f_ref[...], w_ref[...]) - t_ref[...]
        dy = diff * (1.0 / D)
        dy_ref[...] = dy
        dyb_ref[...] = dy.astype(BF16)
        sq_ref[...] = sq_ref[...] + jnp.sum((diff * diff).reshape(TM // 8, 8, D), axis=0)

    return pl.pallas_call(
        body, name="ffn_out_loss", grid=(S // TM,),
        in_specs=[_row(D), _row(FF), _res((FF, D)), _row(D)],
        out_specs=[_row(D), _row(D), pl.BlockSpec((8, D), lambda i: (0, 0))],
        out_shape=[jax.ShapeDtypeStruct((S, D), F32), jax.ShapeDtypeStruct((S, D), BF16),
                   jax.ShapeDtypeStruct((8, D), F32)],
        compiler_params=_cp(dimension_semantics=("arbitrary",)),
    )(x1, f, w_ffn_out, target)


def _rms_bwd(xv, nw, dh):
    r = lax.rsqrt(jnp.mean(xv * xv, axis=-1, keepdims=True) + EPS)
    xn = xv * r
    dxn = dh * nw
    dx = r * (dxn - xn * jnp.mean(dxn * xn, axis=-1, keepdims=True))
    return dx, dh * xn


def ffn_bwd(dy, dyb, gu, x1, norm_w, w_ffn_in, w_ffn_out, sides=()):
    def body(dy_ref, dyb_ref, gu_ref, x_ref, nw_ref, wi_ref, wo_ref, dgu_ref, dx_ref, dxb_ref, gn_ref):
        @pl.when(pl.program_id(0) == 0)
        def _():
            gn_ref[...] = jnp.zeros_like(gn_ref)

        df = _dot_nt(dyb_ref[...], wo_ref[...])
        gt = gu_ref[:, 0:FF].astype(F32)
        up = gu_ref[:, FF:2 * FF].astype(F32)
        sg = _sigmoid(gt)
        dgt = (df * up * _dsilu(gt, sg)).astype(BF16)
        dup = (df * gt * sg).astype(BF16)
        dgu_ref[:, 0:FF] = dgt
        dgu_ref[:, FF:2 * FF] = dup
        dh = _dot_nt(dgt, wi_ref[:, 0:FF]) + _dot_nt(dup, wi_ref[:, FF:2 * FF])
        dxn, gw = _rms_bwd(x_ref[...], nw_ref[...], dh)
        dx = dy_ref[...] + dxn
        dx_ref[...] = dx
        dxb_ref[...] = dx.astype(BF16)
        gn_ref[...] = gn_ref[...] + jnp.sum(gw, axis=0, keepdims=True)

    return _call(
        body, sides, name="ffn_bwd", grid=(S // TM,),
        in_specs=[_row(D), _row(D), _row(2 * FF), _row(D), _res((1, D)), _res((D, 2 * FF)), _res((FF, D))],
        out_specs=[_row(2 * FF), _row(D), _row(D), pl.BlockSpec((1, D), lambda i: (0, 0))],
        out_shape=[jax.ShapeDtypeStruct((S, 2 * FF), BF16), jax.ShapeDtypeStruct((S, D), F32),
                   jax.ShapeDtypeStruct((S, D), BF16), jax.ShapeDtypeStruct((1, D), F32)],
        args=(dy, dyb, gu, x1, norm_w, w_ffn_in, w_ffn_out))


def in_bwd(d_q, d_k, d_v, d_conv, d_gl, w_in, x, d_x1, norm_w, sides=()):
    segs = ((OFF_Q, QKV), (OFF_K, QKV), (OFF_V, QKV), (OFF_CA, 2 * CC), (OFF_GA, 2 * D))

    def body(dq_ref, dk_ref, dv_ref, dc_ref, dg_ref, w_ref, x_ref, dx1_ref, nw_ref, gx_ref, gn_ref):
        @pl.when(pl.program_id(0) == 0)
        def _():
            gn_ref[...] = jnp.zeros_like(gn_ref)

        dh = jnp.zeros((TM, D), F32)
        for ref, (off, width) in zip((dq_ref, dk_ref, dv_ref, dc_ref, dg_ref), segs):
            dh = dh + _dot_nt(ref[...], w_ref[:, off:off + width])
        dxn, gw = _rms_bwd(x_ref[...], nw_ref[...], dh)
        gx_ref[...] = dx1_ref[...] + dxn
        gn_ref[...] = gn_ref[...] + jnp.sum(gw, axis=0, keepdims=True)

    return _call(
        body, sides, name="in_bwd", grid=(S // TM,),
        in_specs=[_row(QKV)] * 3 + [_row(2 * CC), _row(2 * D), _res((D, INW)), _row(D), _row(D), _res((1, D))],
        out_specs=[_row(D), pl.BlockSpec((1, D), lambda i: (0, 0))],
        out_shape=[jax.ShapeDtypeStruct((S, D), F32), jax.ShapeDtypeStruct((1, D), F32)],
        args=(d_q, d_k, d_v, d_conv, d_gl, w_in, x, d_x1, norm_w))


def mm_tn(name, a, b, tm, tn, sides=()):
    M, N = a.shape[1], b.shape[1]

    def body(a_ref, b_ref, o_ref):
        o_ref[...] = _dot_tn(a_ref[...], b_ref[...])

    res = _call(
        body, sides, name=name, grid=(M // tm, N // tn),
        in_specs=[pl.BlockSpec((S, tm), lambda i, j: (0, i)), pl.BlockSpec((S, tn), lambda i, j: (0, j))],
        out_specs=[pl.BlockSpec((tm, tn), lambda i, j: (i, j))],
        out_shape=[jax.ShapeDtypeStruct((M, N), F32)],
        args=(a, b))
    return (res[0][0], res[1]) if sides else res[0]


GW_IN_TN = 512


def gw_in(h, d_segs, sides=()):
    tm, tn = D // 2, GW_IN_TN
    starts, t0 = [], 0
    for seg in d_segs:
        starts.append(t0)
        t0 += seg.shape[1] // tn
    ntiles = [seg.shape[1] // tn for seg in d_segs]

    def body(a_ref, *refs):
        b_refs, o_ref = refs[:-1], refs[-1]
        n = pl.program_id(1)
        for b_ref, st, nt in zip(b_refs, starts, ntiles):
            @pl.when((n >= st) & (n < st + nt))
            def _(b_ref=b_ref):
                o_ref[...] = _dot_tn(a_ref[...], b_ref[...])

    def seg_spec(st, nt):
        return pl.BlockSpec((S, tn), lambda i, n: (0, jnp.clip(n - st, 0, nt - 1)))

    res = _call(
        body, sides, name="gw_in", grid=(D // tm, INW // tn),
        in_specs=[pl.BlockSpec((S, tm), lambda i, n: (0, i))] + [seg_spec(st, nt) for st, nt in zip(starts, ntiles)],
        out_specs=[pl.BlockSpec((tm, tn), lambda i, n: (i, n))],
        out_shape=[jax.ShapeDtypeStruct((D, INW), F32)],
        args=(h, *d_segs))
    return (res[0][0], res[1]) if sides else res[0]


def _place():
    x, y, c = lax.axis_index("x"), lax.axis_index("y"), lax.axis_index("c")
    chips = [(1 - x, y), (x, 1 - y), (1 - x, 1 - y)]
    return x, y, c, chips


def _sems(n):
    return pltpu.SemaphoreType.DMA((n,))


def _remote(src, dst, send, recv, k, to):
    return pltpu.make_async_remote_copy(src_ref=src, dst_ref=dst, send_sem=send.at[k], recv_sem=recv.at[k],
                                        device_id=to, device_id_type=MESH)


def _cast_rows(dst, src, cols=slice(None)):
    rows = src.shape[0]
    step = 128 if rows % 128 == 0 else rows
    for r0 in range(0, rows, step):
        dst[r0:r0 + step, cols] = src[r0:r0 + step, :].astype(dst.dtype)


def comm_only(name, sides):
    def body():
        pass

    return _call(body, sides, name=name, grid=(1,), in_specs=[], out_specs=[], out_shape=[], args=())[1]


def ag_blocks(shard, dtype):
    R, W = shard.shape

    def copy(outs, scr, k, block, to, src=None):
        dst = outs[0].at[block]
        return _remote(dst if src is None else src, dst, scr[1], scr[2], k, to)

    def local(outs, scr, me):
        return pltpu.make_async_copy(scr[0], outs[0].at[me], scr[3].at[0])

    def start(ins, outs, scr):
        x, y, c, chips = _place()
        me = 4 * x + 2 * y + c
        _cast_rows(scr[0], ins[0])
        local(outs, scr, me).start()
        copy(outs, scr, 0, me, (x, y, 1 - c), src=scr[0]).start()
        for j, (cx, cy) in enumerate(chips):
            copy(outs, scr, 1 + j, me, (cx, cy, c), src=scr[0]).start()

    def finish(ins, outs, scr):
        x, y, c, chips = _place()
        me, sib = 4 * x + 2 * y + c, (x, y, 1 - c)
        passed = []
        for j, (cx, cy) in enumerate(chips):
            theirs = 4 * cx + 2 * cy + c
            copy(outs, scr, 1 + j, theirs, (x, y, c)).wait_recv()
            fwd = copy(outs, scr, 4 + j, theirs, sib)
            fwd.start()
            passed.append(fwd)
        copy(outs, scr, 0, 4 * x + 2 * y + 1 - c, (x, y, c)).wait_recv()
        for j, (cx, cy) in enumerate(chips):
            copy(outs, scr, 4 + j, 4 * cx + 2 * cy + 1 - c, (x, y, c)).wait_recv()
        copy(outs, scr, 0, me, sib, src=scr[0]).wait_send()
        for j, (cx, cy) in enumerate(chips):
            copy(outs, scr, 1 + j, me, (cx, cy, c), src=scr[0]).wait_send()
        for fwd in passed:
            fwd.wait_send()
        local(outs, scr, me).wait()

    return Side((shard,), (VMEM,), (jax.ShapeDtypeStruct((NDEV, R, W), dtype),),
                (pltpu.VMEM((R, W), dtype), _sems(7), _sems(7), _sems(1)), start, finish)


def ag_cols(shard):
    K, C = shard.shape
    half, w2 = K // 2, 2 * C

    def win(out, rows_c, chip):
        return out.at[pl.ds(pl.multiple_of(rows_c * half, 16), half), pl.ds(pl.multiple_of(chip * w2, LANES), w2)]

    def ici(outs, scr, j, to, c, k):
        slab, send, recv = scr[2], scr[5], scr[6]
        return _remote(slab.at[pl.ds(pl.multiple_of(c * half, 16), half), :], win(outs[0], c, k), send, recv, j, to)

    def local(outs, scr, k):
        return pltpu.make_async_copy(scr[2], outs[0].at[:, pl.ds(pl.multiple_of(k * w2, LANES), w2)], scr[7].at[0])

    def start(ins, outs, scr):
        stage, inbox, slab, xs, xr = scr[:5]
        x, y, c, chips = _place()
        k = 2 * x + y
        _cast_rows(stage, ins[0])
        swap = _remote(stage, inbox, xs, xr, 0, (x, y, 1 - c))
        swap.start()
        for cc in range(2):
            @pl.when(c == cc)
            def _(cc=cc):
                _cast_rows(slab, stage, slice(cc * C, (cc + 1) * C))
        swap.wait()
        for cc in range(2):
            @pl.when(c == cc)
            def _(cc=cc):
                _cast_rows(slab, inbox, slice((1 - cc) * C, (2 - cc) * C))
        local(outs, scr, k).start()
        for j, (cx, cy) in enumerate(chips):
            ici(outs, scr, j, (cx, cy, c), c, k).start()

    def finish(ins, outs, scr):
        send, recv = scr[5], scr[6]
        x, y, c, chips = _place()
        k, sib = 2 * x + y, (x, y, 1 - c)
        passed = []
        for j, (cx, cy) in enumerate(chips):
            w = win(outs[0], c, 2 * cx + cy)
            _remote(w, w, send, recv, j, sib).wait_recv()
            fwd = _remote(w, w, send, recv, 3 + j, sib)
            fwd.start()
            passed.append(fwd)
        for j, (cx, cy) in enumerate(chips):
            w = win(outs[0], 1 - c, 2 * cx + cy)
            _remote(w, w, send, recv, 3 + j, sib).wait_recv()
        for j, (cx, cy) in enumerate(chips):
            ici(outs, scr, j, (cx, cy, c), c, k).wait_send()
        for fwd in passed:
            fwd.wait_send()
        local(outs, scr, k).wait()

    return Side((shard,), (VMEM,), (jax.ShapeDtypeStruct((K, NDEV * C), BF16),),
                (pltpu.VMEM((K, C), BF16), pltpu.VMEM((K, C), BF16), pltpu.VMEM((K, w2), BF16),
                 _sems(1), _sems(1), _sems(6), _sems(6), _sems(1)), start, finish)


def copies_side(args, out_shape, n_copies, plan):
    def copies(ins, outs, scr):
        return [_remote(s_, d_, scr[0], scr[1], i, to) for i, (s_, d_, to) in enumerate(plan(ins, outs))]

    def start(ins, outs, scr):
        for cp in copies(ins, outs, scr):
            cp.start()

    def finish(ins, outs, scr):
        for cp in copies(ins, outs, scr):
            cp.wait()

    return Side(tuple(args), (ANY,) * len(args), tuple(out_shape), (_sems(n_copies), _sems(n_copies)), start, finish)


def rs_to_sibling(grads):
    out_shape = [jax.ShapeDtypeStruct((4,) + g.shape[1:] if kind == "rows" else (g.shape[0] // 2, g.shape[1]), F32)
                 for kind, g in grads]

    def plan(ins, outs):
        x, y, c, _ = _place()
        sib, res = (x, y, 1 - c), []
        for (kind, _), g, r in zip(grads, ins, outs):
            if kind == "rows":
                res += [(g.at[2 * k + 1 - c], r.at[k], sib) for k in range(4)]
            else:
                half = g.shape[0] // 2
                res.append((g.at[pl.ds(pl.multiple_of((1 - c) * half, 8), half), :], r, sib))
        return res

    return copies_side([g for _, g in grads], out_shape, sum(4 if kind == "rows" else 1 for kind, _ in grads), plan)


def rs_to_chips(parts):
    out_shape = [jax.ShapeDtypeStruct((3,) + p.shape[1:] if kind == "rows" else (3, p.shape[0], p.shape[1] // 4), BF16)
                 for kind, p in parts]

    def plan(ins, outs):
        x, y, c, chips = _place()
        res = []
        for (kind, _), p, r in zip(parts, ins, outs):
            for j, (cx, cy) in enumerate(chips):
                if kind == "rows":
                    src = p.at[2 * cx + cy]
                else:
                    w2 = p.shape[1] // 4
                    src = p.at[:, pl.ds(pl.multiple_of((2 * cx + cy) * w2, LANES), w2)]
                res.append((src, r.at[j], (cx, cy, c)))
        return res

    return copies_side([p for _, p in parts], out_shape, 3 * len(parts), plan)


def rs_swap_halves(theirs):
    def plan(ins, outs):
        x, y, c, _ = _place()
        return [(t, r, (x, y, 1 - c)) for t, r in zip(ins, outs)]

    return copies_side(theirs, [jax.ShapeDtypeStruct(t.shape, F32) for t in theirs], len(theirs), plan)


def _row_tiles(rows):
    return 4 if rows % 64 == 0 and rows >= 512 else (2 if rows % 32 == 0 and rows >= 256 else 1)


def chip_sum(name, grad, recv, c_idx, chip_idx):
    _, R, C = grad.shape
    nt = _row_tiles(R)
    tr = R // nt

    def body(s_ref, g_ref, r_ref, p_ref, own_ref):
        k = pl.program_id(1)
        tot = g_ref[0] + r_ref[0]
        p_ref[0] = tot.astype(BF16)

        @pl.when(k == s_ref[1])
        def _():
            own_ref[...] = tot

    grid_spec = pltpu.PrefetchScalarGridSpec(
        num_scalar_prefetch=1, grid=(nt, 4),
        in_specs=[pl.BlockSpec((1, tr, C), lambda i, k, s: (2 * k + s[0], i, 0)),
                  pl.BlockSpec((1, tr, C), lambda i, k, s: (k, i, 0))],
        out_specs=[pl.BlockSpec((1, tr, C), lambda i, k, s: (k, i, 0)),
                   pl.BlockSpec((tr, C), lambda i, k, s: (i, 0))])
    return pl.pallas_call(
        body, name=name, grid_spec=grid_spec,
        out_shape=[jax.ShapeDtypeStruct((4, R, C), BF16), jax.ShapeDtypeStruct((R, C), F32)],
        compiler_params=_cp(dimension_semantics=("arbitrary", "arbitrary")),
    )(jnp.stack([c_idx, chip_idx]), grad, recv)


def _half_tiles(half):
    return 2 if half >= 512 else 1


def chip_sum_cols(name, grad, recv, c_idx, chip_idx):
    K, W = grad.shape
    half, w2 = K // 2, W // 4
    nt = _half_tiles(half)
    tr = half // nt

    def body(s_ref, g_ref, r_ref, p_ref, own_ref):
        tot = g_ref[...] + r_ref[...]
        p_ref[...] = tot.astype(BF16)

        @pl.when(pl.program_id(1) == s_ref[1])
        def _():
            own_ref[...] = tot

    grid_spec = pltpu.PrefetchScalarGridSpec(
        num_scalar_prefetch=1, grid=(nt, 4),
        in_specs=[pl.BlockSpec((tr, w2), lambda i, k, s: (s[0] * nt + i, k)),
                  pl.BlockSpec((tr, w2), lambda i, k, s: (i, k))],
        out_specs=[pl.BlockSpec((tr, w2), lambda i, k, s: (i, k)),
                   pl.BlockSpec((tr, w2), lambda i, k, s: (i, 0))])
    return pl.pallas_call(
        body, name=name, grid_spec=grid_spec,
        out_shape=[jax.ShapeDtypeStruct((half, W), BF16), jax.ShapeDtypeStruct((half, w2), F32)],
        compiler_params=_cp(dimension_semantics=("arbitrary", "arbitrary")),
    )(jnp.stack([c_idx, chip_idx]), grad, recv)


def col_final(name, own, recv, c_idx):
    half, w2 = own.shape
    C = w2 // 2
    nt = _half_tiles(half)
    tr = half // nt

    def body(s_ref, o_ref, r_ref, mine_ref, theirs_ref, t_ref):
        t_ref[...] = o_ref[...] + r_ref[0].astype(F32) + r_ref[1].astype(F32) + r_ref[2].astype(F32)
        for cc in range(2):
            @pl.when(s_ref[0] == cc)
            def _(cc=cc):
                mine_ref[...] = t_ref[:, cc * C:(cc + 1) * C]
                theirs_ref[...] = t_ref[:, (1 - cc) * C:(2 - cc) * C]

    grid_spec = pltpu.PrefetchScalarGridSpec(
        num_scalar_prefetch=1, grid=(nt,),
        in_specs=[pl.BlockSpec((tr, w2), lambda i, s: (i, 0)), pl.BlockSpec((3, tr, w2), lambda i, s: (0, i, 0))],
        out_specs=[pl.BlockSpec((tr, C), lambda i, s: (i, 0))] * 2,
        scratch_shapes=[pltpu.VMEM((tr, w2), F32)])
    return pl.pallas_call(
        body, name=name, grid_spec=grid_spec, out_shape=[jax.ShapeDtypeStruct((half, C), F32)] * 2,
        compiler_params=_cp(dimension_semantics=("arbitrary",)),
    )(jnp.stack([c_idx]), own, recv)


def _adamw(w, g, m, v):
    m2 = ADAM_B1 * m + (1.0 - ADAM_B1) * g
    v2 = ADAM_B2 * v + (1.0 - ADAM_B2) * (g * g)
    m_hat = m2 / (1.0 - ADAM_B1 ** ADAM_STEP)
    v_hat = v2 / (1.0 - ADAM_B2 ** ADAM_STEP)
    delta = -ADAM_LR * (m_hat / (jnp.sqrt(v_hat) + ADAM_EPS) + ADAM_WD * w)
    return delta, m2, v2


def shard_adam(name, own, recv, w, m, v):
    R, C = own.shape
    nt = _row_tiles(R)
    tr = R // nt

    def body(o_ref, r_ref, w_ref, m_ref, v_ref, g_ref, d_ref, nm_ref, nv_ref):
        g = o_ref[...] + r_ref[0].astype(F32) + r_ref[1].astype(F32) + r_ref[2].astype(F32)
        delta, m2, v2 = _adamw(w_ref[...], g, m_ref[...], v_ref[...])
        g_ref[...] = g
        d_ref[...] = delta
        nm_ref[...] = m2
        nv_ref[...] = v2

    tile = pl.BlockSpec((tr, C), lambda i: (i, 0))
    return pl.pallas_call(
        body, name=name, grid=(nt,),
        in_specs=[tile, pl.BlockSpec((3, tr, C), lambda i: (0, i, 0)), tile, tile, tile],
        out_specs=[tile] * 4, out_shape=[jax.ShapeDtypeStruct((R, C), F32)] * 4,
        compiler_params=_cp(dimension_semantics=("arbitrary",)),
    )(own, recv, w, m, v)


def adam_cols(name, mine, recv, w, m, v, c_idx):
    half, C = mine.shape
    nt = _half_tiles(half)
    tr = half // nt

    def body(s_ref, a_ref, b_ref, w_ref, m_ref, v_ref, g_ref, d_ref, nm_ref, nv_ref):
        g = jnp.where(pl.program_id(0) == s_ref[0], a_ref[...], b_ref[...])
        delta, m2, v2 = _adamw(w_ref[...], g, m_ref[...], v_ref[...])
        g_ref[...] = g
        d_ref[...] = delta
        nm_ref[...] = m2
        nv_ref[...] = v2

    part = pl.BlockSpec((tr, C), lambda hh, i, s: (i, 0))
    tile = pl.BlockSpec((tr, C), lambda hh, i, s: (hh * nt + i, 0))
    grid_spec = pltpu.PrefetchScalarGridSpec(
        num_scalar_prefetch=1, grid=(2, nt), in_specs=[part, part, tile, tile, tile], out_specs=[tile] * 4)
    return pl.pallas_call(
        body, name=name, grid_spec=grid_spec, out_shape=[jax.ShapeDtypeStruct((2 * half, C), F32)] * 4,
        compiler_params=_cp(dimension_semantics=("arbitrary", "arbitrary")),
    )(jnp.stack([c_idx]), mine, recv, w, m, v)


ROW_N1, ROW_N2, ROW_BG, ROW_QN, ROW_KN, ROW_CB, ROW_LW, ROW_LB, ROW_CW = 0, 1, 2, 4, 5, 6, 7, 8, 9
PACK_ROWS = 40
SMALL = ("norm1_w", "norm2_w", "b_gate", "q_norm_w", "k_norm_w", "conv_b", "conv_ln_w", "conv_ln_b", "conv_w")


def small_sync_adam(g, w, m, v):
    ns = len(SMALL)

    def body(*refs):
        gi = dict(zip(SMALL, refs[:ns]))
        wi = dict(zip(SMALL, refs[ns:2 * ns]))
        mi = dict(zip(SMALL, refs[2 * ns:3 * ns]))
        vi = dict(zip(SMALL, refs[3 * ns:4 * ns]))
        outs = refs[4 * ns:8 * ns]
        pack, recv, tot, send_sems, recv_sems = refs[8 * ns:]
        x, y, c, _ = _place()
        me = 4 * x + 2 * y + c

        pack[...] = jnp.zeros_like(pack)
        pack[ROW_N1:ROW_N1 + 1, :] = gi["norm1_w"][...]
        pack[ROW_N2:ROW_N2 + 1, :] = gi["norm2_w"][...]
        pack[ROW_BG:ROW_BG + 2, :] = gi["b_gate"][...]
        pack[ROW_QN:ROW_QN + 1, 0:HD] = gi["q_norm_w"][...]
        pack[ROW_KN:ROW_KN + 1, 0:HD] = gi["k_norm_w"][...]
        pack[ROW_CB:ROW_CB + 1, 0:CC] = gi["conv_b"][...]
        pack[ROW_LW:ROW_LW + 1, 0:CC] = gi["conv_ln_w"][...]
        pack[ROW_LB:ROW_LB + 1, 0:CC] = gi["conv_ln_b"][...]
        pack[ROW_CW:ROW_CW + KW, 0:CC] = gi["conv_w"][...]

        copies = []
        for k in range(1, NDEV):
            peer = (x ^ (k >> 2), y ^ ((k >> 1) & 1), c ^ (k & 1))
            cp = pltpu.make_async_remote_copy(
                src_ref=pack, dst_ref=recv.at[me], send_sem=send_sems.at[k - 1], recv_sem=recv_sems.at[k - 1],
                device_id=peer, device_id_type=MESH)
            cp.start()
            copies.append(cp)
        recv[me] = pack[...]
        for cp in copies:
            cp.wait()
        acc = recv[0]
        for p in range(1, NDEV):
            acc = acc + recv[p]
        tot[...] = acc

        def shard_grad(name):
            if name == "b_gate":
                return tot[ROW_BG:ROW_BG + 2, pl.ds(pl.multiple_of(me * LANES, LANES), LANES)]
            if name == "conv_w":
                win = tot[ROW_CW:ROW_CW + KW, pl.ds(pl.multiple_of((me // 2) * LANES, LANES), LANES)]
                return jnp.where(me % 2 == 1, win[:, HD:LANES], win[:, 0:HD])
            row = {"norm1_w": ROW_N1, "norm2_w": ROW_N2, "q_norm_w": ROW_QN, "k_norm_w": ROW_KN,
                   "conv_b": ROW_CB, "conv_ln_w": ROW_LW, "conv_ln_b": ROW_LB}[name]
            return tot[row:row + 1, 0:wi[name].shape[1]]

        for i, name in enumerate(SMALL):
            gr = shard_grad(name)
            delta, m2, v2 = _adamw(wi[name][...], gr, mi[name][...], vi[name][...])
            outs[4 * i][...] = gr
            outs[4 * i + 1][...] = delta
            outs[4 * i + 2][...] = m2
            outs[4 * i + 3][...] = v2

    out_shape = []
    for name in SMALL:
        out_shape += [jax.ShapeDtypeStruct(w[name].shape, F32)] * 4
    args = [g[k] for k in SMALL] + [w[k] for k in SMALL] + [m[k] for k in SMALL] + [v[k] for k in SMALL]
    res = pl.pallas_call(
        body, name="small_sync_adam", in_specs=[VMEM] * len(args), out_specs=[VMEM] * len(out_shape),
        out_shape=out_shape,
        scratch_shapes=[pltpu.VMEM((PACK_ROWS, D), F32), pltpu.VMEM((NDEV, PACK_ROWS, D), F32),
                        pltpu.VMEM((PACK_ROWS, D), F32),
                        pltpu.SemaphoreType.DMA((NDEV - 1,)), pltpu.SemaphoreType.DMA((NDEV - 1,))],
    )(*args)
    return {name: tuple(res[4 * i:4 * i + 4]) for i, name in enumerate(SMALL)}


MATS = ("w_in", "w_o_attn", "w_pw_conv", "w_out", "w_ffn_in", "w_ffn_out")
WEIGHTS = ("norm1_w", "w_in", "b_gate", "q_norm_w", "k_norm_w", "w_o_attn", "conv_w", "conv_b", "conv_ln_w",
           "conv_ln_b", "w_pw_conv", "w_out", "norm2_w", "w_ffn_in", "w_ffn_out")


def _blocks_to_cols(blocks):
    n, R, C = blocks.shape
    return blocks.transpose(1, 0, 2).reshape(R, n * C)


def kernel(x, positions, norm1_w, w_in, b_gate, q_norm_w, k_norm_w, w_o_attn, conv_w, conv_b, conv_ln_w, conv_ln_b, w_pw_conv, w_out, norm2_w, w_ffn_in, w_ffn_out, loss_target, m_norm1_w, m_w_in, m_b_gate, m_q_norm_w, m_k_norm_w, m_w_o_attn, m_conv_w, m_conv_b, m_conv_ln_w, m_conv_ln_b, m_w_pw_conv, m_w_out, m_norm2_w, m_w_ffn_in, m_w_ffn_out, v_norm1_w, v_w_in, v_b_gate, v_q_norm_w, v_k_norm_w, v_w_o_attn, v_conv_w, v_conv_b, v_conv_ln_w, v_conv_ln_b, v_w_pw_conv, v_w_out, v_norm2_w, v_w_ffn_in, v_w_ffn_out):
    w = dict(norm1_w=norm1_w, w_in=w_in, b_gate=b_gate, q_norm_w=q_norm_w, k_norm_w=k_norm_w, w_o_attn=w_o_attn,
             conv_w=conv_w, conv_b=conv_b, conv_ln_w=conv_ln_w, conv_ln_b=conv_ln_b, w_pw_conv=w_pw_conv,
             w_out=w_out, norm2_w=norm2_w, w_ffn_in=w_ffn_in, w_ffn_out=w_ffn_out)
    m = dict(norm1_w=m_norm1_w, w_in=m_w_in, b_gate=m_b_gate, q_norm_w=m_q_norm_w, k_norm_w=m_k_norm_w,
             w_o_attn=m_w_o_attn, conv_w=m_conv_w, conv_b=m_conv_b, conv_ln_w=m_conv_ln_w,
             conv_ln_b=m_conv_ln_b, w_pw_conv=m_w_pw_conv, w_out=m_w_out, norm2_w=m_norm2_w,
             w_ffn_in=m_w_ffn_in, w_ffn_out=m_w_ffn_out)
    v = dict(norm1_w=v_norm1_w, w_in=v_w_in, b_gate=v_b_gate, q_norm_w=v_q_norm_w, k_norm_w=v_k_norm_w,
             w_o_attn=v_w_o_attn, conv_w=v_conv_w, conv_b=v_conv_b, conv_ln_w=v_conv_ln_w,
             conv_ln_b=v_conv_ln_b, w_pw_conv=v_w_pw_conv, w_out=v_w_out, norm2_w=v_norm2_w,
             w_ffn_in=v_w_ffn_in, w_ffn_out=v_w_ffn_out)
    two_d = lambda t: {k: (a[0] if a.ndim == 3 else a) for k, a in t.items()}
    w, m, v = two_d(w), two_d(m), two_d(v)

    x2, target = x[0], loss_target[0]
    c_idx = lax.axis_index("c").astype(jnp.int32)
    chip_idx = (2 * lax.axis_index("x") + lax.axis_index("y")).astype(jnp.int32)
    tabs = rope_tables(positions.reshape(S, 1))
    qw2 = jnp.tile(w["q_norm_w"], (1, 2))
    kw2 = jnp.tile(w["k_norm_w"], (1, 2))

    (w_in_f,), (bg_blocks,), (cw_blocks,) = comm_only(
        "gather_first", (ag_cols(w["w_in"]), ag_blocks(w["b_gate"], F32), ag_blocks(w["conv_w"], F32)))
    b_gate_f, conv_w_f = _blocks_to_cols(bg_blocks), _blocks_to_cols(cw_blocks)
    (h, proj), ((w_o_f,), (w_pw_f,), (w_out_blocks,)) = in_proj(
        x2, w["norm1_w"], w_in_f, sides=(ag_cols(w["w_o_attn"]), ag_cols(w["w_pw_conv"]), ag_blocks(w["w_out"], BF16)))
    w_out_f = w_out_blocks.reshape(D, D)
    (attn, lse), ((w_ffn_in_f,), (w_ffn_out_blocks,)) = attn_fwd(
        proj, tabs, qw2, kw2, sides=(ag_cols(w["w_ffn_in"]), ag_blocks(w["w_ffn_out"], BF16)))
    w_ffn_out_f = w_ffn_out_blocks.reshape(FF, D)
    cpre, u3 = conv_fwd(proj, conv_w_f, w["conv_b"], w["conv_ln_w"], w["conv_ln_b"])
    x1, z, ya, yb = mix_out(x2, proj, b_gate_f, attn, u3, w_o_f, w_pw_f, w_out_f)
    h2, gu, f = ffn_in(x1, w["norm2_w"], w_ffn_in_f)
    dy, dyb, sq = ffn_out_loss(x1, f, w_ffn_out_f, target)
    loss = lax.psum((0.5 / D) * jnp.sum(sq), ("x", "y", "c"))

    g = {}
    g_ffn_out = mm_tn("gw_ffn_out", f, dyb, FF // 2, D).reshape(NDEV, FF // NDEV, D)
    (d_gu, d_x1, d_x1b, g["norm2_w"]), ((ra_ffn_out,),) = ffn_bwd(
        dy, dyb, gu, x1, w["norm2_w"], w_ffn_in_f, w_ffn_out_f, sides=(rs_to_sibling([("rows", g_ffn_out)]),))
    pb_ffn_out, own_ffn_out = chip_sum("chip_sum_w_ffn_out", g_ffn_out, ra_ffn_out, c_idx, chip_idx)
    g_ffn_in, ((rb_ffn_out,),) = mm_tn("gw_ffn_in", h2, d_gu, D // 2, FF // 2,
                                       sides=(rs_to_chips([("rows", pb_ffn_out)]),))
    g_out = mm_tn("gw_out", z, d_x1b, D // 2, D).reshape(NDEV, D // NDEV, D)
    (d_ya, d_yb, d_gl, d_attn, d_u3, g["b_gate"]), ((ra_ffn_in,),) = out_bwd(
        d_x1b, proj, b_gate_f, ya, yb, w_o_f, w_pw_f, w_out_f, sides=(rs_to_sibling([("cols", g_ffn_in)]),))
    pb_ffn_in, own_ffn_in = chip_sum_cols("chip_sum_w_ffn_in", g_ffn_in, ra_ffn_in, c_idx, chip_idx)
    g_w_o = mm_tn("gw_o_attn", attn, d_ya, CC, D)
    g_w_pw = mm_tn("gw_pw_conv", u3, d_yb, CC, D)
    (d_conv, g["conv_w"], g["conv_b"], g["conv_ln_w"], g["conv_ln_b"]), ((ra_out, ra_w_o, ra_w_pw),) = conv_bwd(
        proj, cpre, d_u3, conv_w_f, conv_w_f[::-1], w["conv_ln_w"], w["conv_ln_b"],
        sides=(rs_to_sibling([("rows", g_out), ("cols", g_w_o), ("cols", g_w_pw)]),))
    pb_out, own_out = chip_sum("chip_sum_w_out", g_out, ra_out, c_idx, chip_idx)
    pb_w_o, own_w_o = chip_sum_cols("chip_sum_w_o_attn", g_w_o, ra_w_o, c_idx, chip_idx)
    pb_w_pw, own_w_pw = chip_sum_cols("chip_sum_w_pw_conv", g_w_pw, ra_w_pw, c_idx, chip_idx)
    (d_q, d_k, d_v, gqw, gkw), ((rb_ffn_in, rb_out, rb_w_o, rb_w_pw),) = attn_bwd(
        proj, tabs, qw2, kw2, d_attn, attn, lse,
        sides=(rs_to_chips([("cols", pb_ffn_in), ("rows", pb_out), ("cols", pb_w_o), ("cols", pb_w_pw)]),))
    g["q_norm_w"] = gqw[0:1, 0:HD] + gqw[0:1, HD:LANES]
    g["k_norm_w"] = gkw[0:1, 0:HD] + gkw[0:1, HD:LANES]
    mine_ffn_in, theirs_ffn_in = col_final("col_final_w_ffn_in", own_ffn_in, rb_ffn_in, c_idx)
    mine_w_o, theirs_w_o = col_final("col_final_w_o_attn", own_w_o, rb_w_o, c_idx)
    mine_w_pw, theirs_w_pw = col_final("col_final_w_pw_conv", own_w_pw, rb_w_pw, c_idx)
    g_w_in, ((rc_ffn_in, rc_w_o, rc_w_pw),) = gw_in(
        h, (d_q, d_k, d_v, d_conv, d_gl), sides=(rs_swap_halves([theirs_ffn_in, theirs_w_o, theirs_w_pw]),))
    (grad_x, g["norm1_w"]), ((ra_w_in,),) = in_bwd(
        d_q, d_k, d_v, d_conv, d_gl, w_in_f, x2, d_x1, w["norm1_w"], sides=(rs_to_sibling([("cols", g_w_in)]),))
    pb_w_in, own_w_in = chip_sum_cols("chip_sum_w_in", g_w_in, ra_w_in, c_idx, chip_idx)
    ((rb_w_in,),) = comm_only("rs_chips_w_in", (rs_to_chips([("cols", pb_w_in)]),))
    mine_w_in, theirs_w_in = col_final("col_final_w_in", own_w_in, rb_w_in, c_idx)
    ((rc_w_in,),) = comm_only("rs_swap_w_in", (rs_swap_halves([theirs_w_in]),))

    res = {
        "w_in": adam_cols("adam_w_in", mine_w_in, rc_w_in, w["w_in"], m["w_in"], v["w_in"], c_idx),
        "w_ffn_in": adam_cols("adam_w_ffn_in", mine_ffn_in, rc_ffn_in, w["w_ffn_in"], m["w_ffn_in"], v["w_ffn_in"], c_idx),
        "w_o_attn": adam_cols("adam_w_o_attn", mine_w_o, rc_w_o, w["w_o_attn"], m["w_o_attn"], v["w_o_attn"], c_idx),
        "w_pw_conv": adam_cols("adam_w_pw_conv", mine_w_pw, rc_w_pw, w["w_pw_conv"], m["w_pw_conv"], v["w_pw_conv"], c_idx),
        "w_out": shard_adam("adam_w_out", own_out, rb_out, w["w_out"], m["w_out"], v["w_out"]),
        "w_ffn_out": shard_adam("adam_w_ffn_out", own_ffn_out, rb_ffn_out, w["w_ffn_out"], m["w_ffn_out"], v["w_ffn_out"]),
    }
    res = {k: tuple(r) for k, r in res.items()}
    res.update(small_sync_adam(g, w, m, v))

    def shaped(name, a):
        return a.reshape((1,) + a.shape) if name in MATS or name in ("b_gate", "conv_w") else a

    outs = [loss, grad_x.reshape(1, S, D)]
    for i in range(4):
        outs += [shaped(k, res[k][i]) for k in WEIGHTS]
    return tuple(outs)
```

```python
import functools
from typing import Callable, NamedTuple

import numpy as np
import jax
import jax.numpy as jnp
from jax import lax
from jax.experimental import pallas as pl
from jax.experimental.pallas import tpu as pltpu

F32 = jnp.float32
BF16 = jnp.bfloat16

S = 2048
D = 1024
HD = 64
QKV = 1536
CC = 512
KW = 31
FF = 2816
INW = 7680
OFF_Q, OFF_K, OFF_V, OFF_CA, OFF_CB, OFF_GA, OFF_GB = 0, 1536, 3072, 4608, 5120, 5632, 6656
DILATIONS = (1, 4, 16)
HALF_SPAN = 64
EPS = 1e-6
NEG_INF = -1e30
ROPE_THETA = 500000.0
ROT_DIM = 16

ADAM_LR = 0.001
ADAM_B1 = 0.9
ADAM_B2 = 0.999
ADAM_EPS = 1e-08
ADAM_WD = 0.01
ADAM_STEP = 10

NDEV = 8
LANES = 128
TM = 256
TQ = 128
VMEM_LIMIT = 56 * 1024 * 1024
MESH = pl.DeviceIdType.MESH


def _cp(**kw):
    return pltpu.CompilerParams(vmem_limit_bytes=VMEM_LIMIT, **kw)


def _row(width, col=0, tm=TM):
    return pl.BlockSpec((tm, width), lambda i: (i, col))


def _res(shape):
    nd = len(shape)
    return pl.BlockSpec(shape, lambda *_: (0,) * nd, pipeline_mode=pl.Buffered(1))


def _dot(a, b):
    return jnp.dot(a, b, preferred_element_type=F32)


def _dot_nt(a, b):
    return lax.dot_general(a, b, (((1,), (1,)), ((), ())), preferred_element_type=F32)


def _dot_tn(a, b):
    return lax.dot_general(a, b, (((0,), (0,)), ((), ())), preferred_element_type=F32)


def _sigmoid(x):
    return jax.nn.sigmoid(x)


def _dsilu(x, sg):
    return sg * (1.0 + x * (1.0 - sg))


ANY = pl.BlockSpec(memory_space=pl.ANY)
VMEM = pl.BlockSpec(memory_space=pltpu.VMEM)


class Side(NamedTuple):
    args: tuple
    in_specs: tuple
    out_shape: tuple
    scratch: tuple
    start: Callable
    finish: Callable


def _call(body, sides=(), *, name, grid, in_specs, out_specs, out_shape, scratch_shapes=(), args):
    ni, no, ns = len(in_specs), len(out_specs), len(scratch_shapes)
    cnt = [(len(s.args), len(s.out_shape), len(s.scratch)) for s in sides]

    def take(refs, pos, n):
        return refs[pos:pos + n], pos + n

    def full(*refs):
        m_in, pos = take(refs, 0, ni)
        s_in = []
        for a, _, _ in cnt:
            r, pos = take(refs, pos, a)
            s_in.append(r)
        m_out, pos = take(refs, pos, no)
        s_out = []
        for _, o, _ in cnt:
            r, pos = take(refs, pos, o)
            s_out.append(r)
        m_scr, pos = take(refs, pos, ns)
        s_scr = []
        for _, _, c in cnt:
            r, pos = take(refs, pos, c)
            s_scr.append(r)
        if sides:
            first = functools.reduce(jnp.logical_and, [pl.program_id(d) == 0 for d in range(len(grid))])
            last = functools.reduce(jnp.logical_and, [pl.program_id(d) == g - 1 for d, g in enumerate(grid)])

            @pl.when(first)
            def _():
                for s, a, o, c in zip(sides, s_in, s_out, s_scr):
                    s.start(a, o, c)

        body(*m_in, *m_out, *m_scr)
        if sides:
            @pl.when(last)
            def _():
                for s, a, o, c in zip(sides, s_in, s_out, s_scr):
                    s.finish(a, o, c)

    res = pl.pallas_call(
        full, name=name, grid=grid,
        in_specs=list(in_specs) + [sp for s in sides for sp in s.in_specs],
        out_specs=list(out_specs) + [ANY for s in sides for _ in s.out_shape],
        out_shape=list(out_shape) + [o for s in sides for o in s.out_shape],
        scratch_shapes=list(scratch_shapes) + [c for s in sides for c in s.scratch],
        compiler_params=_cp(dimension_semantics=("arbitrary",) * len(grid)),
    )(*args, *[a for s in sides for a in s.args])
    res = list(res)
    if not sides:
        return res
    outs, pos = take(res, 0, no)
    side_outs = []
    for _, o, _ in cnt:
        r, pos = take(res, pos, o)
        side_outs.append(r)
    return outs, side_outs


def _inv_freq_lanes():
    inv = np.float32(ROPE_THETA) ** (-np.arange(0, ROT_DIM, 2, dtype=np.float32) / np.float32(ROT_DIM))
    lane = np.arange(LANES) % HD
    out = np.where(lane < ROT_DIM, inv[lane % (ROT_DIM // 2)], 0.0).astype(np.float32)
    return jnp.asarray(out.reshape(1, LANES))


def rope_tables(pos_col):
    def body(p_ref, f_ref, c_ref, s1_ref, s2_ref):
        ang = p_ref[...].astype(F32) * f_ref[...]
        lane = lax.broadcasted_iota(jnp.int32, ang.shape, 1) % HD
        cs = jnp.cos(ang)
        sn = jnp.sin(ang)
        c_ref[...] = jnp.where(lane < ROT_DIM, cs, 1.0)
        s1_ref[...] = jnp.where(lane < ROT_DIM // 2, -sn, 0.0)
        s2_ref[...] = jnp.where(lane < ROT_DIM // 2, 0.0, jnp.where(lane < ROT_DIM, sn, 0.0))

    sds = jax.ShapeDtypeStruct((S, LANES), F32)
    return pl.pallas_call(
        body, name="rope_tables", grid=(S // TM,),
        in_specs=[_row(1), pl.BlockSpec((1, LANES), lambda i: (0, 0))],
        out_specs=[_row(LANES)] * 3, out_shape=[sds] * 3,
    )(pos_col, _inv_freq_lanes())


def _rope(v, c, s1, s2):
    return v * c + pltpu.roll(v, LANES - 8, axis=1) * s1 + pltpu.roll(v, 8, axis=1) * s2


def _rope_t(d, c, s1, s2):
    return d * c - pltpu.roll(d, LANES - 8, axis=1) * s1 - pltpu.roll(d, 8, axis=1) * s2


def _head_mat():
    r = lax.broadcasted_iota(jnp.int32, (LANES, LANES), 0) // HD
    c = lax.broadcasted_iota(jnp.int32, (LANES, LANES), 1) // HD
    return jnp.where(r == c, 1.0 / HD, 0.0).astype(BF16)


def _head_mean(t, e):
    hi = t.astype(BF16)
    rest = (t - hi.astype(F32)).astype(BF16)
    return _dot(hi, e) + _dot(rest, e)


def in_proj(x, norm_w, w_in, sides=()):
    nchunk = 5
    cw = INW // nchunk

    def body(x_ref, nw_ref, w_ref, h_ref, p_ref):
        xv = x_ref[...]
        r = lax.rsqrt(jnp.mean(xv * xv, axis=-1, keepdims=True) + EPS)
        h = (xv * r * nw_ref[...]).astype(BF16)
        h_ref[...] = h
        for j in range(nchunk):
            p_ref[:, j * cw:(j + 1) * cw] = _dot_nt(h, w_ref[j * cw:(j + 1) * cw, :])

    return _call(
        body, sides, name="in_proj", grid=(S // TM,),
        in_specs=[_row(D), _res((1, D)), _res((INW, D))],
        out_specs=[_row(D), _row(INW)],
        out_shape=[jax.ShapeDtypeStruct((S, D), BF16), jax.ShapeDtypeStruct((S, INW), F32)],
        args=(x, norm_w, w_in))


def _qk_specs():
    nb = QKV // LANES
    return [pl.BlockSpec((S, LANES), functools.partial(lambda hp, g, o: (0, o + g * 4 + hp), o=o))
            for o in (OFF_Q // LANES, OFF_K // LANES, OFF_V // LANES)]


def _tab_specs():
    return [pl.BlockSpec((S, LANES), lambda hp, g: (0, 0), pipeline_mode=pl.Buffered(1))] * 3


def _vec_spec():
    return pl.BlockSpec((1, LANES), lambda hp, g: (0, 0))


def _sub_rows(r, d, start, n):
    if d == 1:
        return pl.ds(start, n)
    return pl.ds(r + d * start, n, stride=d)


def _band_window(i, L):
    W = min(2 * TQ, L)
    q0 = pl.multiple_of(i * TQ, TQ)
    k0 = pl.multiple_of(jnp.clip(q0 - HALF_SPAN, 0, L - W), HALF_SPAN)
    qpos = q0 + (lax.broadcasted_iota(jnp.int32, (2 * TQ, W), 0) & (TQ - 1))
    kpos = k0 + lax.broadcasted_iota(jnp.int32, (2 * TQ, W), 1)
    valid = jnp.abs(qpos - kpos) <= HALF_SPAN
    return W, q0, k0, valid


def _stack_heads(t, lo):
    z = jnp.zeros_like(t)
    return jnp.concatenate([jnp.where(lo, t, z), jnp.where(lo, z, t)], axis=0)


def _unstack_heads(t2, lo):
    return jnp.where(lo, t2[0:TQ], t2[TQ:2 * TQ])


CHAINS = 4


def _interleave(d):
    ru = min(d, CHAINS)
    return ru, min(CHAINS // ru, S // d // TQ)


def _for_blocks(n, fn):
    if n == 1:
        fn(0)
    else:
        def it(j, _):
            fn(j)
            return 0
        lax.fori_loop(0, n, it, 0)


def attn_fwd(proj, tabs, qw2, kw2, sides=()):
    CH = 256

    def body(q_ref, k_ref, v_ref, c_ref, s1_ref, s2_ref, qw_ref, kw_ref, at_ref, ls_ref,
             qs, ks, vs, osub, lsub, onat, lnat):
        g = pl.program_id(1)
        lo = lax.broadcasted_iota(jnp.int32, (1, LANES), 1) < HD
        e = _head_mat()

        def prep(t, w, c, s1, s2):
            r = lax.rsqrt(_head_mean(t * t, e) + EPS)
            return _rope(t * r * w, c, s1, s2)

        def group(gi, d):
            L = S // d

            ru, nb = _interleave(d)

            def stage(r, off):
                for c0 in range(0, L, CH):
                    n = min(CH, L)
                    rows = _sub_rows(r, d, c0, n)
                    c, s1, s2 = c_ref[rows, :], s1_ref[rows, :], s2_ref[rows, :]
                    dst = pl.ds(off + c0, n)
                    qs[dst, :] = (prep(q_ref[rows, :], qw_ref[...], c, s1, s2) * (HD ** -0.5)).astype(BF16)
                    ks[dst, :] = prep(k_ref[rows, :], kw_ref[...], c, s1, s2).astype(BF16)
                    vs[dst, :] = v_ref[rows, :].astype(BF16)

            def one(off, i):
                W, q0, k0, valid = _band_window(i, L)
                q2 = _stack_heads(qs[pl.ds(off + q0, TQ), :], lo)
                sc = jnp.where(valid, _dot_nt(q2, ks[pl.ds(off + k0, W), :]), NEG_INF)
                m = jnp.max(sc, axis=-1, keepdims=True)
                p = jnp.exp(sc - m)
                den = jnp.sum(p, axis=-1, keepdims=True)
                o2 = _dot(p.astype(BF16), vs[pl.ds(off + k0, W), :]) / den
                l2 = jnp.broadcast_to(m + jnp.log(den), (2 * TQ, LANES))
                osub[pl.ds(off + q0, TQ), :] = _unstack_heads(o2, lo)
                lsub[pl.ds(off + q0, TQ), :] = _unstack_heads(l2, lo)

            def unstage(r, off):
                for c0 in range(0, L, CH):
                    n = min(CH, L)
                    rows = _sub_rows(r, d, c0, n)
                    onat[gi, rows, :] = osub[pl.ds(off + c0, n), :]
                    lnat[gi, rows, :] = lsub[pl.ds(off + c0, n), :]

            def step(t, _):
                for u in range(ru):
                    stage(t * ru + u, u * L)
                _for_blocks(L // TQ // nb, lambda j: [one(u * L, j * nb + b) for u in range(ru) for b in range(nb)])
                for u in range(ru):
                    unstage(t * ru + u, u * L)
                return 0

            lax.fori_loop(0, d // ru, step, 0)

        for gi, d in enumerate(DILATIONS):
            pl.when(g == gi)(functools.partial(group, gi, d))

        @pl.when(g == len(DILATIONS) - 1)
        def _():
            def mix(i, _):
                rows = pl.ds(pl.multiple_of(i * CH, CH), CH)
                l0, l1, l2 = lnat[0, rows, :], lnat[1, rows, :], lnat[2, rows, :]
                m = jnp.maximum(jnp.maximum(l0, l1), l2)
                e0, e1, e2 = jnp.exp(l0 - m), jnp.exp(l1 - m), jnp.exp(l2 - m)
                den = e0 + e1 + e2
                a = (e0 * onat[0, rows, :] + e1 * onat[1, rows, :] + e2 * onat[2, rows, :]) / den
                at_ref[rows, :] = a.astype(BF16)
                ls_ref[rows, :] = m + jnp.log(den)
                return 0

            lax.fori_loop(0, S // CH, mix, 0)

    out_spec = pl.BlockSpec((S, LANES), lambda hp, g: (0, hp))
    return _call(
        body, sides, name="attn_fwd", grid=(4, 3),
        in_specs=_qk_specs() + _tab_specs() + [_vec_spec(), _vec_spec()],
        out_specs=[out_spec, out_spec],
        out_shape=[jax.ShapeDtypeStruct((S, CC), BF16), jax.ShapeDtypeStruct((S, CC), F32)],
        scratch_shapes=[pltpu.VMEM((S, LANES), BF16)] * 3 + [pltpu.VMEM((S, LANES), F32)] * 2
        + [pltpu.VMEM((3, S, LANES), F32)] * 2,
        args=(proj, proj, proj, *tabs, qw2, kw2))


def attn_bwd(proj, tabs, qw2, kw2, d_attn, attn, lse, sides=()):
    CH = 256

    def body(q_ref, k_ref, v_ref, c_ref, s1_ref, s2_ref, qw_ref, kw_ref, do_ref, at_ref, ls_ref,
             dq_ref, dk_ref, dv_ref, gqw_ref, gkw_ref,
             qs, ks, vs, dos, dsub, lsub, dqs, dks, dvs, dnat, dqn, dkn, dvn):
        hp, g = pl.program_id(0), pl.program_id(1)
        lo = lax.broadcasted_iota(jnp.int32, (1, LANES), 1) < HD
        e = _head_mat()

        @pl.when((hp == 0) & (g == 0))
        def _():
            gqw_ref[...] = jnp.zeros_like(gqw_ref)
            gkw_ref[...] = jnp.zeros_like(gkw_ref)

        def dsum(i, _):
            rows = pl.ds(pl.multiple_of(i * CH, CH), CH)
            dnat[rows, :] = _head_mean(do_ref[rows, :] * at_ref[rows, :].astype(F32), e) * float(HD)
            return 0

        lax.fori_loop(0, S // CH, dsum, 0)

        def group(d):
            L = S // d

            ru, nb = _interleave(d)

            def stage(r, off):
                for c0 in range(0, L, CH):
                    n = min(CH, L)
                    rows = _sub_rows(r, d, c0, n)
                    c, s1, s2 = c_ref[rows, :], s1_ref[rows, :], s2_ref[rows, :]
                    dst = pl.ds(off + c0, n)
                    qv, kv = q_ref[rows, :], k_ref[rows, :]
                    rq = lax.rsqrt(_head_mean(qv * qv, e) + EPS)
                    rk = lax.rsqrt(_head_mean(kv * kv, e) + EPS)
                    qs[dst, :] = (_rope(qv * rq * qw_ref[...], c, s1, s2) * (HD ** -0.5)).astype(BF16)
                    ks[dst, :] = _rope(kv * rk * kw_ref[...], c, s1, s2).astype(BF16)
                    vs[dst, :] = v_ref[rows, :].astype(BF16)
                    dos[dst, :] = do_ref[rows, :].astype(BF16)
                    dsub[dst, :] = dnat[rows, :]
                    lsub[dst, :] = ls_ref[rows, :]
                    dks[dst, :] = jnp.zeros((n, LANES), F32)
                    dvs[dst, :] = jnp.zeros((n, LANES), F32)

            def one(off, i):
                W, q0, k0, valid = _band_window(i, L)
                qrows, krows = pl.ds(off + q0, TQ), pl.ds(off + k0, W)
                q2 = _stack_heads(qs[qrows, :], lo)
                do2 = _stack_heads(dos[qrows, :], lo)
                kk, vv = ks[krows, :], vs[krows, :]
                lse_b, dd_b = lsub[qrows, :], dsub[qrows, :]
                lse2 = jnp.concatenate([lse_b[:, 0:1], lse_b[:, HD:HD + 1]], axis=0)
                dd2 = jnp.concatenate([dd_b[:, 0:1], dd_b[:, HD:HD + 1]], axis=0)
                sc = jnp.where(valid, _dot_nt(q2, kk), NEG_INF)
                p = jnp.exp(sc - lse2)
                ds = (p * (_dot_nt(do2, vv) - dd2)).astype(BF16)
                dqs[qrows, :] = _unstack_heads(_dot(ds, kk), lo)
                dks[krows, :] = dks[krows, :] + _dot_tn(ds, q2)
                dvs[krows, :] = dvs[krows, :] + _dot_tn(p.astype(BF16), do2)

            def unstage(r, off):
                gq = jnp.zeros((1, LANES), F32)
                gk = jnp.zeros((1, LANES), F32)
                for c0 in range(0, L, CH):
                    n = min(CH, L)
                    rows = _sub_rows(r, d, c0, n)
                    src = pl.ds(off + c0, n)
                    c, s1, s2 = c_ref[rows, :], s1_ref[rows, :], s2_ref[rows, :]
                    for (raw_ref, w_ref, gsub, scale, nat) in (
                            (q_ref, qw_ref, dqs, HD ** -0.5, dqn), (k_ref, kw_ref, dks, 1.0, dkn)):
                        t = raw_ref[rows, :]
                        rr = lax.rsqrt(_head_mean(t * t, e) + EPS)
                        tn = t * rr
                        dy = _rope_t(gsub[src, :] * scale, c, s1, s2)
                        gw = jnp.sum(dy * tn, axis=0, keepdims=True)
                        if raw_ref is q_ref:
                            gq = gq + gw
                        else:
                            gk = gk + gw
                        dtn = dy * w_ref[...]
                        nat[rows, :] = rr * (dtn - tn * _head_mean(dtn * tn, e))
                    dvn[rows, :] = dvs[src, :]
                gqw_ref[0:1, :] = gqw_ref[0:1, :] + gq
                gkw_ref[0:1, :] = gkw_ref[0:1, :] + gk

            def step(t, _):
                for u in range(ru):
                    stage(t * ru + u, u * L)
                _for_blocks(L // TQ // nb, lambda j: [one(u * L, j * nb + b) for u in range(ru) for b in range(nb)])
                for u in range(ru):
                    unstage(t * ru + u, u * L)
                return 0

            lax.fori_loop(0, d // ru, step, 0)

        for gi, d in enumerate(DILATIONS):
            pl.when(g == gi)(functools.partial(group, d))

        def emit(i, _):
            rows = pl.ds(pl.multiple_of(i * CH, CH), CH)
            dq_ref[rows, :] = dqn[rows, :].astype(BF16)
            dk_ref[rows, :] = dkn[rows, :].astype(BF16)
            dv_ref[rows, :] = dvn[rows, :].astype(BF16)
            return 0

        lax.fori_loop(0, S // CH, emit, 0)

    nat_spec = pl.BlockSpec((S, LANES), lambda hp, g: (0, hp))
    out_spec = pl.BlockSpec((S, LANES), lambda hp, g: (0, g * 4 + hp))
    acc_spec = pl.BlockSpec((8, LANES), lambda hp, g: (0, 0))
    return _call(
        body, sides, name="attn_bwd", grid=(4, 3),
        in_specs=_qk_specs() + _tab_specs() + [_vec_spec(), _vec_spec(), nat_spec, nat_spec, nat_spec],
        out_specs=[out_spec] * 3 + [acc_spec] * 2,
        out_shape=[jax.ShapeDtypeStruct((S, QKV), BF16)] * 3 + [jax.ShapeDtypeStruct((8, LANES), F32)] * 2,
        scratch_shapes=[pltpu.VMEM((S, LANES), BF16)] * 4 + [pltpu.VMEM((S, LANES), F32)] * 9,
        args=(proj, proj, proj, *tabs, qw2, kw2, d_attn, attn, lse))


PADR = 16
CT = 128


def _conv_specs():
    return [pl.BlockSpec((S, CC), lambda i: (0, OFF_CA // CC)), pl.BlockSpec((S, CC), lambda i: (0, OFF_CB // CC))]


NCB = CC // LANES


def _pad_zero(pad):
    for cb in range(NCB):
        pad[cb, 0:PADR, :] = jnp.zeros((PADR, LANES), F32)
        pad[cb, PADR + S:PADR + S + PADR, :] = jnp.zeros((PADR, LANES), F32)


def _pad_store(pad, row0, n, val):
    for cb in range(NCB):
        pad[cb, pl.ds(pl.multiple_of(row0 + PADR, 8), n), :] = val[:, cb * LANES:(cb + 1) * LANES]


def _taps(pad_ref, cb, s0, weights):
    acc = jnp.zeros((CT, LANES), F32)
    for k in range(KW):
        acc = acc + weights[k] * pad_ref[cb, pl.ds(s0 + k + 1, CT), :]
    return acc


def conv_fwd(proj, conv_w, conv_b, ln_w, ln_b):
    def body(a_ref, b_ref, w_ref, cb_ref, lw_ref, lb_ref, c_ref, u3_ref, upad):
        _pad_zero(upad)

        def glu(i, _):
            rows = pl.ds(pl.multiple_of(i * TM, TM), TM)
            _pad_store(upad, i * TM, TM, a_ref[rows, :] * _sigmoid(b_ref[rows, :]))
            return 0

        lax.fori_loop(0, S // TM, glu, 0)

        def chunk(i, _):
            s0 = pl.multiple_of(i * CT, CT)
            for cb in range(CC // LANES):
                cols = slice(cb * LANES, (cb + 1) * LANES)
                w = [w_ref[k:k + 1, cols] for k in range(KW)]
                c_ref[pl.ds(s0, CT), cols] = _taps(upad, cb, s0, w) + cb_ref[:, cols]
            cv = c_ref[pl.ds(s0, CT), :]
            mu = jnp.mean(cv, axis=-1, keepdims=True)
            xc = cv - mu
            rstd = lax.rsqrt(jnp.mean(xc * xc, axis=-1, keepdims=True) + EPS)
            yl = xc * rstd * lw_ref[...] + lb_ref[...]
            u3_ref[pl.ds(s0, CT), :] = (yl * _sigmoid(yl)).astype(BF16)
            return 0

        lax.fori_loop(0, S // CT, chunk, 0)

    vec = pl.BlockSpec((1, CC), lambda i: (0, 0))
    full = pl.BlockSpec((S, CC), lambda i: (0, 0))
    return pl.pallas_call(
        body, name="conv_fwd", grid=(1,),
        in_specs=_conv_specs() + [pl.BlockSpec((KW, CC), lambda i: (0, 0)), vec, vec, vec],
        out_specs=[full, full],
        out_shape=[jax.ShapeDtypeStruct((S, CC), F32), jax.ShapeDtypeStruct((S, CC), BF16)],
        scratch_shapes=[pltpu.VMEM((NCB, S + 2 * PADR, LANES), F32)],
        compiler_params=_cp(dimension_semantics=("arbitrary",)),
    )(proj, proj, conv_w, conv_b, ln_w, ln_b)


def conv_bwd(proj, cpre, d_u3, conv_w, conv_w_rev, ln_w, ln_b, sides=()):
    def body(a_ref, b_ref, c_ref, du3_ref, w_ref, wr_ref, lw_ref, lb_ref,
             dc_ref, gw_ref, gcb_ref, glw_ref, glb_ref, upad, dpad):
        _pad_zero(upad)
        _pad_zero(dpad)
        gw_ref[...] = jnp.zeros_like(gw_ref)

        def ln_bwd(i, carry):
            gcb, glw, glb = carry
            rows = pl.ds(pl.multiple_of(i * TM, TM), TM)
            _pad_store(upad, i * TM, TM, a_ref[rows, :] * _sigmoid(b_ref[rows, :]))
            cv = c_ref[rows, :]
            mu = jnp.mean(cv, axis=-1, keepdims=True)
            xc = cv - mu
            rstd = lax.rsqrt(jnp.mean(xc * xc, axis=-1, keepdims=True) + EPS)
            xh = xc * rstd
            yl = xh * lw_ref[...] + lb_ref[...]
            dyl = du3_ref[rows, :] * _dsilu(yl, _sigmoid(yl))
            dxh = dyl * lw_ref[...]
            dcv = rstd * (dxh - jnp.mean(dxh, axis=-1, keepdims=True)
                          - xh * jnp.mean(dxh * xh, axis=-1, keepdims=True))
            _pad_store(dpad, i * TM, TM, dcv)
            return (gcb + jnp.sum(dcv, axis=0, keepdims=True),
                    glw + jnp.sum(dyl * xh, axis=0, keepdims=True),
                    glb + jnp.sum(dyl, axis=0, keepdims=True))

        z = jnp.zeros((1, CC), F32)
        gcb, glw, glb = lax.fori_loop(0, S // TM, ln_bwd, (z, z, z))
        gcb_ref[...] = gcb
        glw_ref[...] = glw
        glb_ref[...] = glb

        def chunk(i, _):
            s0 = pl.multiple_of(i * CT, CT)
            for cb in range(CC // LANES):
                cols = slice(cb * LANES, (cb + 1) * LANES)
                wr = [wr_ref[k:k + 1, cols] for k in range(KW)]
                du = _taps(dpad, cb, s0, wr)
                dcv = dpad[cb, pl.ds(s0 + PADR, CT), :]
                for k in range(KW):
                    gw_ref[k:k + 1, cols] = gw_ref[k:k + 1, cols] + jnp.sum(
                        upad[cb, pl.ds(s0 + k + 1, CT), :] * dcv, axis=0, keepdims=True)
                av = a_ref[pl.ds(s0, CT), cols]
                sb = _sigmoid(b_ref[pl.ds(s0, CT), cols])
                dc_ref[pl.ds(s0, CT), cols] = (du * sb).astype(BF16)
                dc_ref[pl.ds(s0, CT), slice(CC + cb * LANES, CC + (cb + 1) * LANES)] = (
                    du * av * sb * (1.0 - sb)).astype(BF16)
            return 0

        lax.fori_loop(0, S // CT, chunk, 0)

    vec = pl.BlockSpec((1, CC), lambda i: (0, 0))
    full = pl.BlockSpec((S, CC), lambda i: (0, 0))
    wsp = pl.BlockSpec((KW, CC), lambda i: (0, 0))
    return _call(
        body, sides, name="conv_bwd", grid=(1,),
        in_specs=_conv_specs() + [full, full, wsp, wsp, vec, vec],
        out_specs=[pl.BlockSpec((S, 2 * CC), lambda i: (0, 0)), wsp, vec, vec, vec],
        out_shape=[jax.ShapeDtypeStruct((S, 2 * CC), BF16), jax.ShapeDtypeStruct((KW, CC), F32)]
        + [jax.ShapeDtypeStruct((1, CC), F32)] * 3,
        scratch_shapes=[pltpu.VMEM((NCB, S + 2 * PADR, LANES), F32)] * 2,
        args=(proj, proj, cpre, d_u3, conv_w, conv_w_rev, ln_w, ln_b))


def _gate_specs():
    return [_row(CC, col=OFF_GA // CC + j) for j in range(4)]


def _gates(g_refs, bg_ref):
    ga = _sigmoid(jnp.concatenate([g_refs[0][...], g_refs[1][...]], axis=1) + bg_ref[0:1, :])
    gb = _sigmoid(jnp.concatenate([g_refs[2][...], g_refs[3][...]], axis=1) + bg_ref[1:2, :])
    return ga, gb


def mix_out(x, proj, b_gate, attn, u3, w_o, w_pw, w_out):
    def body(x_ref, g0, g1, g2, g3, bg_ref, at_ref, u3_ref, wo_ref, wp_ref, wout_ref,
             x1_ref, z_ref, ya_ref, yb_ref):
        ga, gb = _gates((g0, g1, g2, g3), bg_ref)
        ya = _dot(at_ref[...], wo_ref[...])
        yb = _dot(u3_ref[...], wp_ref[...])
        z = (ga * ya + gb * yb).astype(BF16)
        ya_ref[...] = ya.astype(BF16)
        yb_ref[...] = yb.astype(BF16)
        z_ref[...] = z
        x1_ref[...] = x_ref[...] + _dot(z, wout_ref[...])

    return pl.pallas_call(
        body, name="mix_out", grid=(S // TM,),
        in_specs=[_row(D)] + _gate_specs() + [_res((2, D)), _row(CC), _row(CC),
                                              _res((CC, D)), _res((CC, D)), _res((D, D))],
        out_specs=[_row(D)] * 4,
        out_shape=[jax.ShapeDtypeStruct((S, D), F32)] + [jax.ShapeDtypeStruct((S, D), BF16)] * 3,
        compiler_params=_cp(dimension_semantics=("arbitrary",)),
    )(x, proj, proj, proj, proj, b_gate, attn, u3, w_o, w_pw, w_out)


def out_bwd(d_x1b, proj, b_gate, ya, yb, w_o, w_pw, w_out, sides=()):
    def body(dx_ref, g0, g1, g2, g3, bg_ref, ya_ref, yb_ref, wo_ref, wp_ref, wout_ref,
             dya_ref, dyb_ref, dgl_ref, dat_ref, du3_ref, gbg_ref):
        @pl.when(pl.program_id(0) == 0)
        def _():
            gbg_ref[...] = jnp.zeros_like(gbg_ref)

        ga, gb = _gates((g0, g1, g2, g3), bg_ref)
        dz = _dot_nt(dx_ref[...], wout_ref[...])
        dya = (dz * ga).astype(BF16)
        dyb = (dz * gb).astype(BF16)
        dgla = dz * ya_ref[...].astype(F32) * ga * (1.0 - ga)
        dglb = dz * yb_ref[...].astype(F32) * gb * (1.0 - gb)
        dya_ref[...] = dya
        dyb_ref[...] = dyb
        dgl_ref[:, 0:D] = dgla.astype(BF16)
        dgl_ref[:, D:2 * D] = dglb.astype(BF16)
        gbg_ref[0:1, :] = gbg_ref[0:1, :] + jnp.sum(dgla, axis=0, keepdims=True)
        gbg_ref[1:2, :] = gbg_ref[1:2, :] + jnp.sum(dglb, axis=0, keepdims=True)
        dat_ref[...] = _dot_nt(dya, wo_ref[...])
        du3_ref[...] = _dot_nt(dyb, wp_ref[...])

    return _call(
        body, sides, name="out_bwd", grid=(S // TM,),
        in_specs=[_row(D)] + _gate_specs() + [_res((2, D)), _row(D), _row(D),
                                              _res((CC, D)), _res((CC, D)), _res((D, D))],
        out_specs=[_row(D), _row(D), _row(2 * D), _row(CC), _row(CC), pl.BlockSpec((2, D), lambda i: (0, 0))],
        out_shape=[jax.ShapeDtypeStruct((S, D), BF16)] * 2 + [jax.ShapeDtypeStruct((S, 2 * D), BF16)]
        + [jax.ShapeDtypeStruct((S, CC), F32)] * 2 + [jax.ShapeDtypeStruct((2, D), F32)],
        args=(d_x1b, proj, proj, proj, proj, b_gate, ya, yb, w_o, w_pw, w_out))


def ffn_in(x1, norm_w, w_ffn_in, sides=()):
    half = FF // 2

    def body(x_ref, nw_ref, w_ref, h_ref, gu_ref, f_ref):
        xv = x_ref[...]
        r = lax.rsqrt(jnp.mean(xv * xv, axis=-1, keepdims=True) + EPS)
        h = (xv * r * nw_ref[...]).astype(BF16)
        h_ref[...] = h
        for j in range(2):
            gt = _dot_nt(h, w_ref[j * half:(j + 1) * half, :])
            up = _dot_nt(h, w_ref[FF + j * half:FF + (j + 1) * half, :])
            gu_ref[:, j * half:(j + 1) * half] = gt.astype(BF16)
            gu_ref[:, FF + j * half:FF + (j + 1) * half] = up.astype(BF16)
            f_ref[:, j * half:(j + 1) * half] = (gt * _sigmoid(gt) * up).astype(BF16)

    return _call(
        body, sides, name="ffn_in", grid=(S // TM,),
        in_specs=[_row(D), _res((1, D)), _res((2 * FF, D))],
        out_specs=[_row(D), _row(2 * FF), _row(FF)],
        out_shape=[jax.ShapeDtypeStruct((S, D), BF16), jax.ShapeDtypeStruct((S, 2 * FF), BF16),
                   jax.ShapeDtypeStruct((S, FF), BF16)],
        args=(x1, norm_w, w_ffn_in))


def ffn_out_loss(x1, f, w_ffn_out, target):
    def body(x_ref, f_ref, w_ref, t_ref, dy_ref, dyb_ref, sq_ref):
        @pl.when(pl.program_id(0) == 0)
        def _():
            sq_ref[...] = jnp.zeros_like(sq_ref)

        diff = x_ref[...] + _dot(f_ref[...], w_ref[...]) - t_ref[...]
        dy = diff * (1.0 / D)
        dy_ref[...] = dy
        dyb_ref[...] = dy.astype(BF16)
        sq_ref[...] = sq_ref[...] + jnp.sum((diff * diff).reshape(TM // 8, 8, D), axis=0)

    return pl.pallas_call(
        body, name="ffn_out_loss", grid=(S // TM,),
        in_specs=[_row(D), _row(FF), _res((FF, D)), _row(D)],
        out_specs=[_row(D), _row(D), pl.BlockSpec((8, D), lambda i: (0, 0))],
        out_shape=[jax.ShapeDtypeStruct((S, D), F32), jax.ShapeDtypeStruct((S, D), BF16),
                   jax.ShapeDtypeStruct((8, D), F32)],
        compiler_params=_cp(dimension_semantics=("arbitrary",)),
    )(x1, f, w_ffn_out, target)


def _rms_bwd(xv, nw, dh):
    r = lax.rsqrt(jnp.mean(xv * xv, axis=-1, keepdims=True) + EPS)
    xn = xv * r
    dxn = dh * nw
    dx = r * (dxn - xn * jnp.mean(dxn * xn, axis=-1, keepdims=True))
    return dx, dh * xn


def ffn_bwd(dy, dyb, gu, x1, norm_w, w_ffn_in, w_ffn_out, sides=()):
    def body(dy_ref, dyb_ref, gu_ref, x_ref, nw_ref, wi_ref, wo_ref, dgu_ref, dx_ref, dxb_ref, gn_ref):
        @pl.when(pl.program_id(0) == 0)
        def _():
            gn_ref[...] = jnp.zeros_like(gn_ref)

        df = _dot_nt(dyb_ref[...], wo_ref[...])
        gt = gu_ref[:, 0:FF].astype(F32)
        up = gu_ref[:, FF:2 * FF].astype(F32)
        sg = _sigmoid(gt)
        dgt = (df * up * _dsilu(gt, sg)).astype(BF16)
        dup = (df * gt * sg).astype(BF16)
        dgu_ref[:, 0:FF] = dgt
        dgu_ref[:, FF:2 * FF] = dup
        dh = _dot(dgt, wi_ref[0:FF, :]) + _dot(dup, wi_ref[FF:2 * FF, :])
        dxn, gw = _rms_bwd(x_ref[...], nw_ref[...], dh)
        dx = dy_ref[...] + dxn
        dx_ref[...] = dx
        dxb_ref[...] = dx.astype(BF16)
        gn_ref[...] = gn_ref[...] + jnp.sum(gw, axis=0, keepdims=True)

    return _call(
        body, sides, name="ffn_bwd", grid=(S // TM,),
        in_specs=[_row(D), _row(D), _row(2 * FF), _row(D), _res((1, D)), _res((2 * FF, D)), _res((FF, D))],
        out_specs=[_row(2 * FF), _row(D), _row(D), pl.BlockSpec((1, D), lambda i: (0, 0))],
        out_shape=[jax.ShapeDtypeStruct((S, 2 * FF), BF16), jax.ShapeDtypeStruct((S, D), F32),
                   jax.ShapeDtypeStruct((S, D), BF16), jax.ShapeDtypeStruct((1, D), F32)],
        args=(dy, dyb, gu, x1, norm_w, w_ffn_in, w_ffn_out))


def in_bwd(d_q, d_k, d_v, d_conv, d_gl, w_in, x, d_x1, norm_w, sides=()):
    segs = ((OFF_Q, QKV), (OFF_K, QKV), (OFF_V, QKV), (OFF_CA, 2 * CC), (OFF_GA, 2 * D))

    def body(dq_ref, dk_ref, dv_ref, dc_ref, dg_ref, w_ref, x_ref, dx1_ref, nw_ref, gx_ref, gn_ref):
        @pl.when(pl.program_id(0) == 0)
        def _():
            gn_ref[...] = jnp.zeros_like(gn_ref)

        dh = jnp.zeros((TM, D), F32)
        for ref, (off, width) in zip((dq_ref, dk_ref, dv_ref, dc_ref, dg_ref), segs):
            dh = dh + _dot(ref[...], w_ref[off:off + width, :])
        dxn, gw = _rms_bwd(x_ref[...], nw_ref[...], dh)
        gx_ref[...] = dx1_ref[...] + dxn
        gn_ref[...] = gn_ref[...] + jnp.sum(gw, axis=0, keepdims=True)

    return _call(
        body, sides, name="in_bwd", grid=(S // TM,),
        in_specs=[_row(QKV)] * 3 + [_row(2 * CC), _row(2 * D), _res((INW, D)), _row(D), _row(D), _res((1, D))],
        out_specs=[_row(D), pl.BlockSpec((1, D), lambda i: (0, 0))],
        out_shape=[jax.ShapeDtypeStruct((S, D), F32), jax.ShapeDtypeStruct((1, D), F32)],
        args=(d_q, d_k, d_v, d_conv, d_gl, w_in, x, d_x1, norm_w))


def mm_tn(name, a, b, tm, tn, sides=()):
    M, N = a.shape[1], b.shape[1]

    def body(a_ref, b_ref, o_ref):
        o_ref[...] = _dot_tn(a_ref[...], b_ref[...])

    res = _call(
        body, sides, name=name, grid=(M // tm, N // tn),
        in_specs=[pl.BlockSpec((S, tm), lambda i, j: (0, i)), pl.BlockSpec((S, tn), lambda i, j: (0, j))],
        out_specs=[pl.BlockSpec((tm, tn), lambda i, j: (i, j))],
        out_shape=[jax.ShapeDtypeStruct((M, N), F32)],
        args=(a, b))
    return (res[0][0], res[1]) if sides else res[0]


GW_IN_TN = 512


def gw_in_t(name, h, d_segs, col_half, sides=()):
    tn, hw = GW_IN_TN, D // 2
    starts, t0 = [], 0
    for seg in d_segs:
        starts.append(t0)
        t0 += seg.shape[1] // tn
    ntiles = [seg.shape[1] // tn for seg in d_segs]

    def body(h_ref, *refs):
        a_refs, o_ref = refs[:-1], refs[-1]
        n = pl.program_id(0)
        for a_ref, st, nt in zip(a_refs, starts, ntiles):
            @pl.when((n >= st) & (n < st + nt))
            def _(a_ref=a_ref):
                o_ref[...] = _dot_tn(a_ref[...], h_ref[...])

    def seg_spec(st, nt):
        return pl.BlockSpec((S, tn), lambda n: (0, jnp.clip(n - st, 0, nt - 1)))

    res = _call(
        body, sides, name=name, grid=(INW // tn,),
        in_specs=[pl.BlockSpec((S, hw), lambda n: (0, col_half))] + [seg_spec(st, nt) for st, nt in zip(starts, ntiles)],
        out_specs=[pl.BlockSpec((tn, hw), lambda n: (n, 0))],
        out_shape=[jax.ShapeDtypeStruct((INW, hw), F32)],
        args=(h, *d_segs))
    return (res[0][0], res[1]) if sides else res[0]


def _place():
    x, y, c = lax.axis_index("x"), lax.axis_index("y"), lax.axis_index("c")
    chips = [(1 - x, y), (x, 1 - y), (1 - x, 1 - y)]
    return x, y, c, chips


def _sems(n):
    return pltpu.SemaphoreType.DMA((n,))


def _remote(src, dst, send, recv, k, to):
    return pltpu.make_async_remote_copy(src_ref=src, dst_ref=dst, send_sem=send.at[k], recv_sem=recv.at[k],
                                        device_id=to, device_id_type=MESH)


def _cast_rows(dst, src, cols=slice(None)):
    rows = src.shape[0]
    step = next((s for s in (128, 64, 32, 16) if rows % s == 0), rows)
    for r0 in range(0, rows, step):
        dst[r0:r0 + step, cols] = src[r0:r0 + step, :].astype(dst.dtype)


def comm_only(name, sides):
    def body():
        pass

    return _call(body, sides, name=name, grid=(1,), in_specs=[], out_specs=[], out_shape=[], args=())[1]


def ag_blocks(shard, dtype):
    R, W = shard.shape

    def copy(outs, scr, k, block, to, src=None):
        dst = outs[0].at[block]
        return _remote(dst if src is None else src, dst, scr[1], scr[2], k, to)

    def local(outs, scr, me):
        return pltpu.make_async_copy(scr[0], outs[0].at[me], scr[3].at[0])

    def start(ins, outs, scr):
        x, y, c, chips = _place()
        me = 4 * x + 2 * y + c
        _cast_rows(scr[0], ins[0])
        local(outs, scr, me).start()
        copy(outs, scr, 0, me, (x, y, 1 - c), src=scr[0]).start()
        for j, (cx, cy) in enumerate(chips):
            copy(outs, scr, 1 + j, me, (cx, cy, c), src=scr[0]).start()

    def finish(ins, outs, scr):
        x, y, c, chips = _place()
        me, sib = 4 * x + 2 * y + c, (x, y, 1 - c)
        passed = []
        for j, (cx, cy) in enumerate(chips):
            theirs = 4 * cx + 2 * cy + c
            copy(outs, scr, 1 + j, theirs, (x, y, c)).wait_recv()
            fwd = copy(outs, scr, 4 + j, theirs, sib)
            fwd.start()
            passed.append(fwd)
        copy(outs, scr, 0, 4 * x + 2 * y + 1 - c, (x, y, c)).wait_recv()
        for j, (cx, cy) in enumerate(chips):
            copy(outs, scr, 4 + j, 4 * cx + 2 * cy + 1 - c, (x, y, c)).wait_recv()
        copy(outs, scr, 0, me, sib, src=scr[0]).wait_send()
        for j, (cx, cy) in enumerate(chips):
            copy(outs, scr, 1 + j, me, (cx, cy, c), src=scr[0]).wait_send()
        for fwd in passed:
            fwd.wait_send()
        local(outs, scr, me).wait()

    return Side((shard,), (VMEM,), (jax.ShapeDtypeStruct((NDEV, R, W), dtype),),
                (pltpu.VMEM((R, W), dtype), _sems(7), _sems(7), _sems(1)), start, finish)


def ag_cols(shard):
    K, C = shard.shape
    half, w2 = K // 2, 2 * C

    def win(out, rows_c, chip):
        return out.at[pl.ds(pl.multiple_of(rows_c * half, 16), half), pl.ds(pl.multiple_of(chip * w2, LANES), w2)]

    def ici(outs, scr, j, to, c, k):
        slab, send, recv = scr[2], scr[5], scr[6]
        return _remote(slab.at[pl.ds(pl.multiple_of(c * half, 16), half), :], win(outs[0], c, k), send, recv, j, to)

    def local(outs, scr, k):
        return pltpu.make_async_copy(scr[2], outs[0].at[:, pl.ds(pl.multiple_of(k * w2, LANES), w2)], scr[7].at[0])

    def start(ins, outs, scr):
        stage, inbox, slab, xs, xr = scr[:5]
        x, y, c, chips = _place()
        k = 2 * x + y
        _cast_rows(stage, ins[0])
        swap = _remote(stage, inbox, xs, xr, 0, (x, y, 1 - c))
        swap.start()
        for cc in range(2):
            @pl.when(c == cc)
            def _(cc=cc):
                _cast_rows(slab, stage, slice(cc * C, (cc + 1) * C))
        swap.wait()
        for cc in range(2):
            @pl.when(c == cc)
            def _(cc=cc):
                _cast_rows(slab, inbox, slice((1 - cc) * C, (2 - cc) * C))
        local(outs, scr, k).start()
        for j, (cx, cy) in enumerate(chips):
            ici(outs, scr, j, (cx, cy, c), c, k).start()

    def finish(ins, outs, scr):
        send, recv = scr[5], scr[6]
        x, y, c, chips = _place()
        k, sib = 2 * x + y, (x, y, 1 - c)
        passed = []
        for j, (cx, cy) in enumerate(chips):
            w = win(outs[0], c, 2 * cx + cy)
            _remote(w, w, send, recv, j, sib).wait_recv()
            fwd = _remote(w, w, send, recv, 3 + j, sib)
            fwd.start()
            passed.append(fwd)
        for j, (cx, cy) in enumerate(chips):
            w = win(outs[0], 1 - c, 2 * cx + cy)
            _remote(w, w, send, recv, 3 + j, sib).wait_recv()
        for j, (cx, cy) in enumerate(chips):
            ici(outs, scr, j, (cx, cy, c), c, k).wait_send()
        for fwd in passed:
            fwd.wait_send()
        local(outs, scr, k).wait()

    return Side((shard,), (VMEM,), (jax.ShapeDtypeStruct((K, NDEV * C), BF16),),
                (pltpu.VMEM((K, C), BF16), pltpu.VMEM((K, C), BF16), pltpu.VMEM((K, w2), BF16),
                 _sems(1), _sems(1), _sems(6), _sems(6), _sems(1)), start, finish)


def copies_side(args, out_shape, n_copies, plan):
    def copies(ins, outs, scr):
        return [_remote(s_, d_, scr[0], scr[1], i, to) for i, (s_, d_, to) in enumerate(plan(ins, outs))]

    def start(ins, outs, scr):
        for cp in copies(ins, outs, scr):
            cp.start()

    def finish(ins, outs, scr):
        for cp in copies(ins, outs, scr):
            cp.wait()

    return Side(tuple(args), (ANY,) * len(args), tuple(out_shape), (_sems(n_copies), _sems(n_copies)), start, finish)


def rs_to_sibling(grads):
    out_shape = [jax.ShapeDtypeStruct((4,) + g.shape[1:] if kind == "rows" else (g.shape[0] // 2, g.shape[1]), F32)
                 for kind, g in grads]

    def plan(ins, outs):
        x, y, c, _ = _place()
        sib, res = (x, y, 1 - c), []
        for (kind, _), g, r in zip(grads, ins, outs):
            if kind == "rows":
                res += [(g.at[2 * k + 1 - c], r.at[k], sib) for k in range(4)]
            else:
                half = g.shape[0] // 2
                res.append((g.at[pl.ds(pl.multiple_of((1 - c) * half, 8), half), :], r, sib))
        return res

    return copies_side([g for _, g in grads], out_shape, sum(4 if kind == "rows" else 1 for kind, _ in grads), plan)


def rs_to_chips(parts):
    out_shape = [jax.ShapeDtypeStruct((3,) + p.shape[1:] if kind == "rows" else (3, p.shape[0], p.shape[1] // 4), BF16)
                 for kind, p in parts]

    def plan(ins, outs):
        x, y, c, chips = _place()
        res = []
        for (kind, _), p, r in zip(parts, ins, outs):
            for j, (cx, cy) in enumerate(chips):
                if kind == "rows":
                    src = p.at[2 * cx + cy]
                else:
                    w2 = p.shape[1] // 4
                    src = p.at[:, pl.ds(pl.multiple_of((2 * cx + cy) * w2, LANES), w2)]
                res.append((src, r.at[j], (cx, cy, c)))
        return res

    return copies_side([p for _, p in parts], out_shape, 3 * len(parts), plan)


def rs_swap_halves(theirs):
    def plan(ins, outs):
        x, y, c, _ = _place()
        return [(t, r, (x, y, 1 - c)) for t, r in zip(ins, outs)]

    return copies_side(theirs, [jax.ShapeDtypeStruct(t.shape, F32) for t in theirs], len(theirs), plan)


def _row_tiles(rows):
    return 4 if rows % 64 == 0 and rows >= 512 else (2 if rows % 32 == 0 and rows >= 256 else 1)


def chip_sum(name, grad, recv, c_idx, chip_idx):
    _, R, C = grad.shape
    nt = _row_tiles(R)
    tr = R // nt

    def body(s_ref, g_ref, r_ref, p_ref, own_ref):
        k = pl.program_id(1)
        tot = g_ref[0] + r_ref[0]
        p_ref[0] = tot.astype(BF16)

        @pl.when(k == s_ref[1])
        def _():
            own_ref[...] = tot

    grid_spec = pltpu.PrefetchScalarGridSpec(
        num_scalar_prefetch=1, grid=(nt, 4),
        in_specs=[pl.BlockSpec((1, tr, C), lambda i, k, s: (2 * k + s[0], i, 0)),
                  pl.BlockSpec((1, tr, C), lambda i, k, s: (k, i, 0))],
        out_specs=[pl.BlockSpec((1, tr, C), lambda i, k, s: (k, i, 0)),
                   pl.BlockSpec((tr, C), lambda i, k, s: (i, 0))])
    return pl.pallas_call(
        body, name=name, grid_spec=grid_spec,
        out_shape=[jax.ShapeDtypeStruct((4, R, C), BF16), jax.ShapeDtypeStruct((R, C), F32)],
        compiler_params=_cp(dimension_semantics=("arbitrary", "arbitrary")),
    )(jnp.stack([c_idx, chip_idx]), grad, recv)


def _half_tiles(half):
    return 2 if half >= 512 else 1


def chip_sum_cols(name, grad, recv, c_idx, chip_idx):
    K, W = grad.shape
    half, w2 = K // 2, W // 4
    nt = _half_tiles(half)
    tr = half // nt

    def body(s_ref, g_ref, r_ref, p_ref, own_ref):
        tot = g_ref[...] + r_ref[...]
        p_ref[...] = tot.astype(BF16)

        @pl.when(pl.program_id(1) == s_ref[1])
        def _():
            own_ref[...] = tot

    grid_spec = pltpu.PrefetchScalarGridSpec(
        num_scalar_prefetch=1, grid=(nt, 4),
        in_specs=[pl.BlockSpec((tr, w2), lambda i, k, s: (s[0] * nt + i, k)),
                  pl.BlockSpec((tr, w2), lambda i, k, s: (i, k))],
        out_specs=[pl.BlockSpec((tr, w2), lambda i, k, s: (i, k)),
                   pl.BlockSpec((tr, w2), lambda i, k, s: (i, 0))])
    return pl.pallas_call(
        body, name=name, grid_spec=grid_spec,
        out_shape=[jax.ShapeDtypeStruct((half, W), BF16), jax.ShapeDtypeStruct((half, w2), F32)],
        compiler_params=_cp(dimension_semantics=("arbitrary", "arbitrary")),
    )(jnp.stack([c_idx, chip_idx]), grad, recv)


def col_final(name, own, recv, c_idx):
    half, w2 = own.shape
    C = w2 // 2
    nt = _half_tiles(half)
    tr = half // nt

    def body(s_ref, o_ref, r_ref, mine_ref, theirs_ref, t_ref):
        t_ref[...] = o_ref[...] + r_ref[0].astype(F32) + r_ref[1].astype(F32) + r_ref[2].astype(F32)
        for cc in range(2):
            @pl.when(s_ref[0] == cc)
            def _(cc=cc):
                mine_ref[...] = t_ref[:, cc * C:(cc + 1) * C]
                theirs_ref[...] = t_ref[:, (1 - cc) * C:(2 - cc) * C]

    grid_spec = pltpu.PrefetchScalarGridSpec(
        num_scalar_prefetch=1, grid=(nt,),
        in_specs=[pl.BlockSpec((tr, w2), lambda i, s: (i, 0)), pl.BlockSpec((3, tr, w2), lambda i, s: (0, i, 0))],
        out_specs=[pl.BlockSpec((tr, C), lambda i, s: (i, 0))] * 2,
        scratch_shapes=[pltpu.VMEM((tr, w2), F32)])
    return pl.pallas_call(
        body, name=name, grid_spec=grid_spec, out_shape=[jax.ShapeDtypeStruct((half, C), F32)] * 2,
        compiler_params=_cp(dimension_semantics=("arbitrary",)),
    )(jnp.stack([c_idx]), own, recv)


def _adamw(w, g, m, v):
    m2 = ADAM_B1 * m + (1.0 - ADAM_B1) * g
    v2 = ADAM_B2 * v + (1.0 - ADAM_B2) * (g * g)
    m_hat = m2 / (1.0 - ADAM_B1 ** ADAM_STEP)
    v_hat = v2 / (1.0 - ADAM_B2 ** ADAM_STEP)
    delta = -ADAM_LR * (m_hat / (jnp.sqrt(v_hat) + ADAM_EPS) + ADAM_WD * w)
    return delta, m2, v2


def shard_adam(name, owns, recvs, w, m, v):
    n = len(owns)
    R, Cp = owns[0].shape
    nt = _row_tiles(R)
    tr = R // nt

    def body(*refs):
        o_refs, r_refs = refs[:n], refs[n:2 * n]
        w_ref, m_ref, v_ref, g_ref, d_ref, nm_ref, nv_ref = refs[2 * n:]
        g = None
        for k in range(n):
            gk = o_refs[k][...] + r_refs[k][0].astype(F32) + r_refs[k][1].astype(F32) + r_refs[k][2].astype(F32)
            g = gk if g is None else jnp.where(pl.program_id(0) == k, gk, g)
        delta, m2, v2 = _adamw(w_ref[...], g, m_ref[...], v_ref[...])
        g_ref[...] = g
        d_ref[...] = delta
        nm_ref[...] = m2
        nv_ref[...] = v2

    part = pl.BlockSpec((tr, Cp), lambda k, i: (i, 0))
    part3 = pl.BlockSpec((3, tr, Cp), lambda k, i: (0, i, 0))
    tile = pl.BlockSpec((tr, Cp), lambda k, i: (i, k))
    return pl.pallas_call(
        body, name=name, grid=(n, nt),
        in_specs=[part] * n + [part3] * n + [tile, tile, tile],
        out_specs=[tile] * 4, out_shape=[jax.ShapeDtypeStruct((R, n * Cp), F32)] * 4,
        compiler_params=_cp(dimension_semantics=("arbitrary", "arbitrary")),
    )(*owns, *recvs, w, m, v)


def adam_cols(name, mine, recv, w, m, v, c_idx):
    half, C = mine.shape
    nt = _half_tiles(half)
    tr = half // nt

    def body(s_ref, a_ref, b_ref, w_ref, m_ref, v_ref, g_ref, d_ref, nm_ref, nv_ref):
        g = jnp.where(pl.program_id(0) == s_ref[0], a_ref[...], b_ref[...])
        delta, m2, v2 = _adamw(w_ref[...], g, m_ref[...], v_ref[...])
        g_ref[...] = g
        d_ref[...] = delta
        nm_ref[...] = m2
        nv_ref[...] = v2

    part = pl.BlockSpec((tr, C), lambda hh, i, s: (i, 0))
    tile = pl.BlockSpec((tr, C), lambda hh, i, s: (hh * nt + i, 0))
    grid_spec = pltpu.PrefetchScalarGridSpec(
        num_scalar_prefetch=1, grid=(2, nt), in_specs=[part, part, tile, tile, tile], out_specs=[tile] * 4)
    return pl.pallas_call(
        body, name=name, grid_spec=grid_spec, out_shape=[jax.ShapeDtypeStruct((2 * half, C), F32)] * 4,
        compiler_params=_cp(dimension_semantics=("arbitrary", "arbitrary")),
    )(jnp.stack([c_idx]), mine, recv, w, m, v)


ROW_N1, ROW_N2, ROW_BG, ROW_QN, ROW_KN, ROW_CB, ROW_LW, ROW_LB, ROW_CW = 0, 1, 2, 4, 5, 6, 7, 8, 9
PACK_ROWS = 40
SMALL = ("norm1_w", "norm2_w", "b_gate", "q_norm_w", "k_norm_w", "conv_b", "conv_ln_w", "conv_ln_b", "conv_w")


def small_sync_adam(g, w, m, v):
    ns = len(SMALL)

    def body(*refs):
        gi = dict(zip(SMALL, refs[:ns]))
        wi = dict(zip(SMALL, refs[ns:2 * ns]))
        mi = dict(zip(SMALL, refs[2 * ns:3 * ns]))
        vi = dict(zip(SMALL, refs[3 * ns:4 * ns]))
        outs = refs[4 * ns:8 * ns]
        pack, recv, tot, send_sems, recv_sems = refs[8 * ns:]
        x, y, c, _ = _place()
        me = 4 * x + 2 * y + c

        pack[...] = jnp.zeros_like(pack)
        pack[ROW_N1:ROW_N1 + 1, :] = gi["norm1_w"][...]
        pack[ROW_N2:ROW_N2 + 1, :] = gi["norm2_w"][...]
        pack[ROW_BG:ROW_BG + 2, :] = gi["b_gate"][...]
        pack[ROW_QN:ROW_QN + 1, 0:HD] = gi["q_norm_w"][...]
        pack[ROW_KN:ROW_KN + 1, 0:HD] = gi["k_norm_w"][...]
        pack[ROW_CB:ROW_CB + 1, 0:CC] = gi["conv_b"][...]
        pack[ROW_LW:ROW_LW + 1, 0:CC] = gi["conv_ln_w"][...]
        pack[ROW_LB:ROW_LB + 1, 0:CC] = gi["conv_ln_b"][...]
        pack[ROW_CW:ROW_CW + KW, 0:CC] = gi["conv_w"][...]

        copies = []
        for k in range(1, NDEV):
            peer = (x ^ (k >> 2), y ^ ((k >> 1) & 1), c ^ (k & 1))
            cp = pltpu.make_async_remote_copy(
                src_ref=pack, dst_ref=recv.at[me], send_sem=send_sems.at[k - 1], recv_sem=recv_sems.at[k - 1],
                device_id=peer, device_id_type=MESH)
            cp.start()
            copies.append(cp)
        recv[me] = pack[...]
        for cp in copies:
            cp.wait()
        acc = recv[0]
        for p in range(1, NDEV):
            acc = acc + recv[p]
        tot[...] = acc

        def shard_grad(name):
            if name == "b_gate":
                return tot[ROW_BG:ROW_BG + 2, pl.ds(pl.multiple_of(me * LANES, LANES), LANES)]
            if name == "conv_w":
                win = tot[ROW_CW:ROW_CW + KW, pl.ds(pl.multiple_of((me // 2) * LANES, LANES), LANES)]
                return jnp.where(me % 2 == 1, win[:, HD:LANES], win[:, 0:HD])
            row = {"norm1_w": ROW_N1, "norm2_w": ROW_N2, "q_norm_w": ROW_QN, "k_norm_w": ROW_KN,
                   "conv_b": ROW_CB, "conv_ln_w": ROW_LW, "conv_ln_b": ROW_LB}[name]
            return tot[row:row + 1, 0:wi[name].shape[1]]

        for i, name in enumerate(SMALL):
            gr = shard_grad(name)
            delta, m2, v2 = _adamw(wi[name][...], gr, mi[name][...], vi[name][...])
            outs[4 * i][...] = gr
            outs[4 * i + 1][...] = delta
            outs[4 * i + 2][...] = m2
            outs[4 * i + 3][...] = v2

    out_shape = []
    for name in SMALL:
        out_shape += [jax.ShapeDtypeStruct(w[name].shape, F32)] * 4
    args = [g[k] for k in SMALL] + [w[k] for k in SMALL] + [m[k] for k in SMALL] + [v[k] for k in SMALL]
    res = pl.pallas_call(
        body, name="small_sync_adam", in_specs=[VMEM] * len(args), out_specs=[VMEM] * len(out_shape),
        out_shape=out_shape,
        scratch_shapes=[pltpu.VMEM((PACK_ROWS, D), F32), pltpu.VMEM((NDEV, PACK_ROWS, D), F32),
                        pltpu.VMEM((PACK_ROWS, D), F32),
                        pltpu.SemaphoreType.DMA((NDEV - 1,)), pltpu.SemaphoreType.DMA((NDEV - 1,))],
    )(*args)
    return {name: tuple(res[4 * i:4 * i + 4]) for i, name in enumerate(SMALL)}


MATS = ("w_in", "w_o_attn", "w_pw_conv", "w_out", "w_ffn_in", "w_ffn_out")
TRANSPOSED = ("w_in", "w_ffn_in")
WEIGHTS = ("norm1_w", "w_in", "b_gate", "q_norm_w", "k_norm_w", "w_o_attn", "conv_w", "conv_b", "conv_ln_w",
           "conv_ln_b", "w_pw_conv", "w_out", "norm2_w", "w_ffn_in", "w_ffn_out")


def _blocks_to_cols(blocks):
    n, R, C = blocks.shape
    return blocks.transpose(1, 0, 2).reshape(R, n * C)


def kernel(x, positions, norm1_w, w_in, b_gate, q_norm_w, k_norm_w, w_o_attn, conv_w, conv_b, conv_ln_w, conv_ln_b, w_pw_conv, w_out, norm2_w, w_ffn_in, w_ffn_out, loss_target, m_norm1_w, m_w_in, m_b_gate, m_q_norm_w, m_k_norm_w, m_w_o_attn, m_conv_w, m_conv_b, m_conv_ln_w, m_conv_ln_b, m_w_pw_conv, m_w_out, m_norm2_w, m_w_ffn_in, m_w_ffn_out, v_norm1_w, v_w_in, v_b_gate, v_q_norm_w, v_k_norm_w, v_w_o_attn, v_conv_w, v_conv_b, v_conv_ln_w, v_conv_ln_b, v_w_pw_conv, v_w_out, v_norm2_w, v_w_ffn_in, v_w_ffn_out):
    w = dict(norm1_w=norm1_w, w_in=w_in, b_gate=b_gate, q_norm_w=q_norm_w, k_norm_w=k_norm_w, w_o_attn=w_o_attn,
             conv_w=conv_w, conv_b=conv_b, conv_ln_w=conv_ln_w, conv_ln_b=conv_ln_b, w_pw_conv=w_pw_conv,
             w_out=w_out, norm2_w=norm2_w, w_ffn_in=w_ffn_in, w_ffn_out=w_ffn_out)
    m = dict(norm1_w=m_norm1_w, w_in=m_w_in, b_gate=m_b_gate, q_norm_w=m_q_norm_w, k_norm_w=m_k_norm_w,
             w_o_attn=m_w_o_attn, conv_w=m_conv_w, conv_b=m_conv_b, conv_ln_w=m_conv_ln_w,
             conv_ln_b=m_conv_ln_b, w_pw_conv=m_w_pw_conv, w_out=m_w_out, norm2_w=m_norm2_w,
             w_ffn_in=m_w_ffn_in, w_ffn_out=m_w_ffn_out)
    v = dict(norm1_w=v_norm1_w, w_in=v_w_in, b_gate=v_b_gate, q_norm_w=v_q_norm_w, k_norm_w=v_k_norm_w,
             w_o_attn=v_w_o_attn, conv_w=v_conv_w, conv_b=v_conv_b, conv_ln_w=v_conv_ln_w,
             conv_ln_b=v_conv_ln_b, w_pw_conv=v_w_pw_conv, w_out=v_w_out, norm2_w=v_norm2_w,
             w_ffn_in=v_w_ffn_in, w_ffn_out=v_w_ffn_out)
    def two_d(t):
        t = {k: (a[0] if a.ndim == 3 else a) for k, a in t.items()}
        return {k: (a.T if k in TRANSPOSED else a) for k, a in t.items()}

    w, m, v = two_d(w), two_d(m), two_d(v)

    x2, target = x[0], loss_target[0]
    c_idx = lax.axis_index("c").astype(jnp.int32)
    chip_idx = (2 * lax.axis_index("x") + lax.axis_index("y")).astype(jnp.int32)
    tabs = rope_tables(positions.reshape(S, 1))
    qw2 = jnp.tile(w["q_norm_w"], (1, 2))
    kw2 = jnp.tile(w["k_norm_w"], (1, 2))

    (w_in_blocks,), (bg_blocks,), (cw_blocks,) = comm_only(
        "gather_first", (ag_blocks(w["w_in"], BF16), ag_blocks(w["b_gate"], F32), ag_blocks(w["conv_w"], F32)))
    w_in_t = w_in_blocks.reshape(INW, D)
    b_gate_f, conv_w_f = _blocks_to_cols(bg_blocks), _blocks_to_cols(cw_blocks)
    (h, proj), ((w_o_f,), (w_pw_f,), (w_out_blocks,)) = in_proj(
        x2, w["norm1_w"], w_in_t, sides=(ag_cols(w["w_o_attn"]), ag_cols(w["w_pw_conv"]), ag_blocks(w["w_out"], BF16)))
    w_out_f = w_out_blocks.reshape(D, D)
    (attn, lse), ((w_ffn_in_blocks,),) = attn_fwd(proj, tabs, qw2, kw2, sides=(ag_blocks(w["w_ffn_in"], BF16),))
    w_ffn_in_t = w_ffn_in_blocks.reshape(2 * FF, D)
    cpre, u3 = conv_fwd(proj, conv_w_f, w["conv_b"], w["conv_ln_w"], w["conv_ln_b"])
    x1, z, ya, yb = mix_out(x2, proj, b_gate_f, attn, u3, w_o_f, w_pw_f, w_out_f)
    (h2, gu, f), ((w_ffn_out_blocks,),) = ffn_in(x1, w["norm2_w"], w_ffn_in_t, sides=(ag_blocks(w["w_ffn_out"], BF16),))
    w_ffn_out_f = w_ffn_out_blocks.reshape(FF, D)
    dy, dyb, sq = ffn_out_loss(x1, f, w_ffn_out_f, target)
    loss = lax.psum((0.5 / D) * jnp.sum(sq), ("x", "y", "c"))

    g = {}
    g_ffn_out = mm_tn("gw_ffn_out", f, dyb, FF // 2, D).reshape(NDEV, FF // NDEV, D)
    (d_gu, d_x1, d_x1b, g["norm2_w"]), ((ra_ffn_out,),) = ffn_bwd(
        dy, dyb, gu, x1, w["norm2_w"], w_ffn_in_t, w_ffn_out_f, sides=(rs_to_sibling([("rows", g_ffn_out)]),))
    pb_ffn_out, own_ffn_out = chip_sum("chip_sum_w_ffn_out", g_ffn_out, ra_ffn_out, c_idx, chip_idx)
    g_ffn_in, ((rb_ffn_out,),) = mm_tn("gw_ffn_in", d_gu, h2, FF // 2, D,
                                       sides=(rs_to_chips([("rows", pb_ffn_out)]),))
    g_ffn_in = g_ffn_in.reshape(NDEV, 2 * FF // NDEV, D)
    g_out = mm_tn("gw_out", z, d_x1b, D // 2, D).reshape(NDEV, D // NDEV, D)
    (d_ya, d_yb, d_gl, d_attn, d_u3, g["b_gate"]), ((ra_ffn_in,),) = out_bwd(
        d_x1b, proj, b_gate_f, ya, yb, w_o_f, w_pw_f, w_out_f, sides=(rs_to_sibling([("rows", g_ffn_in)]),))
    pb_ffn_in, own_ffn_in = chip_sum("chip_sum_w_ffn_in", g_ffn_in, ra_ffn_in, c_idx, chip_idx)
    g_w_o = mm_tn("gw_o_attn", attn, d_ya, CC, D)
    g_w_pw = mm_tn("gw_pw_conv", u3, d_yb, CC, D)
    (d_conv, g["conv_w"], g["conv_b"], g["conv_ln_w"], g["conv_ln_b"]), ((ra_out, ra_w_o, ra_w_pw),) = conv_bwd(
        proj, cpre, d_u3, conv_w_f, conv_w_f[::-1], w["conv_ln_w"], w["conv_ln_b"],
        sides=(rs_to_sibling([("rows", g_out), ("cols", g_w_o), ("cols", g_w_pw)]),))
    pb_out, own_out = chip_sum("chip_sum_w_out", g_out, ra_out, c_idx, chip_idx)
    pb_w_o, own_w_o = chip_sum_cols("chip_sum_w_o_attn", g_w_o, ra_w_o, c_idx, chip_idx)
    pb_w_pw, own_w_pw = chip_sum_cols("chip_sum_w_pw_conv", g_w_pw, ra_w_pw, c_idx, chip_idx)
    (d_q, d_k, d_v, gqw, gkw), ((rb_ffn_in, rb_out, rb_w_o, rb_w_pw),) = attn_bwd(
        proj, tabs, qw2, kw2, d_attn, attn, lse,
        sides=(rs_to_chips([("rows", pb_ffn_in), ("rows", pb_out), ("cols", pb_w_o), ("cols", pb_w_pw)]),))
    g["q_norm_w"] = gqw[0:1, 0:HD] + gqw[0:1, HD:LANES]
    g["k_norm_w"] = gkw[0:1, 0:HD] + gkw[0:1, HD:LANES]
    mine_w_o, theirs_w_o = col_final("col_final_w_o_attn", own_w_o, rb_w_o, c_idx)
    mine_w_pw, theirs_w_pw = col_final("col_final_w_pw_conv", own_w_pw, rb_w_pw, c_idx)
    d_segs = (d_q, d_k, d_v, d_conv, d_gl)
    g_w_in_a, ((rc_w_o, rc_w_pw),) = gw_in_t("gw_in_a", h, d_segs, 0,
                                             sides=(rs_swap_halves([theirs_w_o, theirs_w_pw]),))
    g_w_in_a = g_w_in_a.reshape(NDEV, INW // NDEV, D // 2)
    g_w_in_b, ((ra_w_in_a,),) = gw_in_t("gw_in_b", h, d_segs, 1, sides=(rs_to_sibling([("rows", g_w_in_a)]),))
    g_w_in_b = g_w_in_b.reshape(NDEV, INW // NDEV, D // 2)
    pb_w_in_a, own_w_in_a = chip_sum("chip_sum_w_in_a", g_w_in_a, ra_w_in_a, c_idx, chip_idx)
    (grad_x, g["norm1_w"]), ((rb_w_in_a,), (ra_w_in_b,)) = in_bwd(
        d_q, d_k, d_v, d_conv, d_gl, w_in_t, x2, d_x1, w["norm1_w"],
        sides=(rs_to_chips([("rows", pb_w_in_a)]), rs_to_sibling([("rows", g_w_in_b)])))
    pb_w_in_b, own_w_in_b = chip_sum("chip_sum_w_in_b", g_w_in_b, ra_w_in_b, c_idx, chip_idx)
    ((rb_w_in_b,),) = comm_only("rs_chips_w_in_b", (rs_to_chips([("rows", pb_w_in_b)]),))

    res = {
        "w_in": shard_adam("adam_w_in", [own_w_in_a, own_w_in_b], [rb_w_in_a, rb_w_in_b],
                           w["w_in"], m["w_in"], v["w_in"]),
        "w_ffn_in": shard_adam("adam_w_ffn_in", [own_ffn_in], [rb_ffn_in], w["w_ffn_in"], m["w_ffn_in"], v["w_ffn_in"]),
        "w_o_attn": adam_cols("adam_w_o_attn", mine_w_o, rc_w_o, w["w_o_attn"], m["w_o_attn"], v["w_o_attn"], c_idx),
        "w_pw_conv": adam_cols("adam_w_pw_conv", mine_w_pw, rc_w_pw, w["w_pw_conv"], m["w_pw_conv"], v["w_pw_conv"], c_idx),
        "w_out": shard_adam("adam_w_out", [own_out], [rb_out], w["w_out"], m["w_out"], v["w_out"]),
        "w_ffn_out": shard_adam("adam_w_ffn_out", [own_ffn_out], [rb_ffn_out],
                                w["w_ffn_out"], m["w_ffn_out"], v["w_ffn_out"]),
    }
    res = {k: tuple(a.T if k in TRANSPOSED else a for a in r) for k, r in res.items()}
    res.update(small_sync_adam(g, w, m, v))

    def shaped(name, a):
        return a.reshape((1,) + a.shape) if name in MATS or name in ("b_gate", "conv_w") else a

    outs = [loss, grad_x.reshape(1, S, D)]
    for i in range(4):
        outs += [shaped(k, res[k][i]) for k in WEIGHTS]
    return tuple(outs)
```

```python
import functools
from typing import Callable, NamedTuple, Optional

import numpy as np
import jax
import jax.numpy as jnp
from jax import lax
from jax.experimental import pallas as pl
from jax.experimental.pallas import tpu as pltpu

F32 = jnp.float32
BF16 = jnp.bfloat16

S = 2048
D = 1024
HD = 64
QKV = 1536
CC = 512
KW = 31
FF = 2816
INW = 7680
OFF_Q, OFF_K, OFF_V, OFF_CA, OFF_CB, OFF_GA, OFF_GB = 0, 1536, 3072, 4608, 5120, 5632, 6656
DILATIONS = (1, 4, 16)
HALF_SPAN = 64
EPS = 1e-6
NEG_INF = -1e30
ROPE_THETA = 500000.0
ROT_DIM = 16

ADAM_LR = 0.001
ADAM_B1 = 0.9
ADAM_B2 = 0.999
ADAM_EPS = 1e-08
ADAM_WD = 0.01
ADAM_STEP = 10

NDEV = 8
LANES = 128
TM = 256
TQ = 128
VMEM_LIMIT = 56 * 1024 * 1024
MESH = pl.DeviceIdType.MESH


def _cp(**kw):
    return pltpu.CompilerParams(vmem_limit_bytes=VMEM_LIMIT, **kw)


def _row(width, col=0, tm=TM):
    return pl.BlockSpec((tm, width), lambda i: (i, col))


def _res(shape):
    nd = len(shape)
    return pl.BlockSpec(shape, lambda *_: (0,) * nd, pipeline_mode=pl.Buffered(1))


def _dot(a, b):
    return jnp.dot(a, b, preferred_element_type=F32)


def _dot_nt(a, b):
    return lax.dot_general(a, b, (((1,), (1,)), ((), ())), preferred_element_type=F32)


def _dot_tn(a, b):
    return lax.dot_general(a, b, (((0,), (0,)), ((), ())), preferred_element_type=F32)


def _sigmoid(x):
    return jax.nn.sigmoid(x)


def _dsilu(x, sg):
    return sg * (1.0 + x * (1.0 - sg))


ANY = pl.BlockSpec(memory_space=pl.ANY)
VMEM = pl.BlockSpec(memory_space=pltpu.VMEM)


class Side(NamedTuple):
    args: tuple
    in_specs: tuple
    out_shape: tuple
    scratch: tuple
    start: Callable
    finish: Callable
    mid: Optional[Callable] = None


def _call(body, sides=(), *, name, grid, in_specs, out_specs, out_shape, scratch_shapes=(), args):
    ni, no, ns = len(in_specs), len(out_specs), len(scratch_shapes)
    cnt = [(len(s.args), len(s.out_shape), len(s.scratch)) for s in sides]

    def take(refs, pos, n):
        return refs[pos:pos + n], pos + n

    def full(*refs):
        m_in, pos = take(refs, 0, ni)
        s_in = []
        for a, _, _ in cnt:
            r, pos = take(refs, pos, a)
            s_in.append(r)
        m_out, pos = take(refs, pos, no)
        s_out = []
        for _, o, _ in cnt:
            r, pos = take(refs, pos, o)
            s_out.append(r)
        m_scr, pos = take(refs, pos, ns)
        s_scr = []
        for _, _, c in cnt:
            r, pos = take(refs, pos, c)
            s_scr.append(r)
        if sides:
            first = functools.reduce(jnp.logical_and, [pl.program_id(d) == 0 for d in range(len(grid))])
            last = functools.reduce(jnp.logical_and, [pl.program_id(d) == g - 1 for d, g in enumerate(grid)])

            @pl.when(first)
            def _():
                for s, a, o, c in zip(sides, s_in, s_out, s_scr):
                    s.start(a, o, c)

            steps = int(np.prod(grid))
            mid_step = (2 * steps) // 3
            if steps > 1 and any(s.mid is not None for s in sides):
                step = functools.reduce(lambda acc, d: acc * grid[d] + pl.program_id(d), range(len(grid)), 0)

                @pl.when(step == mid_step)
                def _():
                    for s, a, o, c in zip(sides, s_in, s_out, s_scr):
                        if s.mid is not None:
                            s.mid(a, o, c)

        body(*m_in, *m_out, *m_scr)
        if sides:
            @pl.when(last)
            def _():
                for s, a, o, c in zip(sides, s_in, s_out, s_scr):
                    if s.mid is not None and steps == 1:
                        s.mid(a, o, c)
                    s.finish(a, o, c)

    res = pl.pallas_call(
        full, name=name, grid=grid,
        in_specs=list(in_specs) + [sp for s in sides for sp in s.in_specs],
        out_specs=list(out_specs) + [ANY for s in sides for _ in s.out_shape],
        out_shape=list(out_shape) + [o for s in sides for o in s.out_shape],
        scratch_shapes=list(scratch_shapes) + [c for s in sides for c in s.scratch],
        compiler_params=_cp(dimension_semantics=("arbitrary",) * len(grid)),
    )(*args, *[a for s in sides for a in s.args])
    res = list(res)
    if not sides:
        return res
    outs, pos = take(res, 0, no)
    side_outs = []
    for _, o, _ in cnt:
        r, pos = take(res, pos, o)
        side_outs.append(r)
    return outs, side_outs


def _inv_freq_lanes():
    inv = np.float32(ROPE_THETA) ** (-np.arange(0, ROT_DIM, 2, dtype=np.float32) / np.float32(ROT_DIM))
    lane = np.arange(LANES) % HD
    out = np.where(lane < ROT_DIM, inv[lane % (ROT_DIM // 2)], 0.0).astype(np.float32)
    return jnp.asarray(out.reshape(1, LANES))


def rope_tables(pos_col):
    def body(p_ref, f_ref, c_ref, s1_ref, s2_ref):
        ang = p_ref[...].astype(F32) * f_ref[...]
        lane = lax.broadcasted_iota(jnp.int32, ang.shape, 1) % HD
        cs = jnp.cos(ang)
        sn = jnp.sin(ang)
        c_ref[...] = jnp.where(lane < ROT_DIM, cs, 1.0)
        s1_ref[...] = jnp.where(lane < ROT_DIM // 2, -sn, 0.0)
        s2_ref[...] = jnp.where(lane < ROT_DIM // 2, 0.0, jnp.where(lane < ROT_DIM, sn, 0.0))

    sds = jax.ShapeDtypeStruct((S, LANES), F32)
    return pl.pallas_call(
        body, name="rope_tables", grid=(S // TM,),
        in_specs=[_row(1), pl.BlockSpec((1, LANES), lambda i: (0, 0))],
        out_specs=[_row(LANES)] * 3, out_shape=[sds] * 3,
    )(pos_col, _inv_freq_lanes())


def _rope(v, c, s1, s2):
    return v * c + pltpu.roll(v, LANES - 8, axis=1) * s1 + pltpu.roll(v, 8, axis=1) * s2


def _rope_t(d, c, s1, s2):
    return d * c - pltpu.roll(d, LANES - 8, axis=1) * s1 - pltpu.roll(d, 8, axis=1) * s2


def _head_mat():
    r = lax.broadcasted_iota(jnp.int32, (LANES, LANES), 0) // HD
    c = lax.broadcasted_iota(jnp.int32, (LANES, LANES), 1) // HD
    return jnp.where(r == c, 1.0 / HD, 0.0).astype(BF16)


def _head_mean(t, e):
    hi = t.astype(BF16)
    rest = (t - hi.astype(F32)).astype(BF16)
    return _dot(hi, e) + _dot(rest, e)


def in_proj(x, norm_w, w_in, sides=()):
    nchunk = 5
    cw = INW // nchunk

    def body(x_ref, nw_ref, w_ref, h_ref, p_ref):
        xv = x_ref[...]
        r = lax.rsqrt(jnp.mean(xv * xv, axis=-1, keepdims=True) + EPS)
        h = (xv * r * nw_ref[...]).astype(BF16)
        h_ref[...] = h
        for j in range(nchunk):
            p_ref[:, j * cw:(j + 1) * cw] = _dot_nt(h, w_ref[j * cw:(j + 1) * cw, :])

    return _call(
        body, sides, name="in_proj", grid=(S // TM,),
        in_specs=[_row(D), _res((1, D)), _res((INW, D))],
        out_specs=[_row(D), _row(INW)],
        out_shape=[jax.ShapeDtypeStruct((S, D), BF16), jax.ShapeDtypeStruct((S, INW), F32)],
        args=(x, norm_w, w_in))


def _qk_specs():
    nb = QKV // LANES
    return [pl.BlockSpec((S, LANES), functools.partial(lambda hp, g, o: (0, o + g * 4 + hp), o=o))
            for o in (OFF_Q // LANES, OFF_K // LANES, OFF_V // LANES)]


def _tab_specs():
    return [pl.BlockSpec((S, LANES), lambda hp, g: (0, 0), pipeline_mode=pl.Buffered(1))] * 3


def _vec_spec():
    return pl.BlockSpec((1, LANES), lambda hp, g: (0, 0))


def _sub_rows(r, d, start, n):
    if d == 1:
        return pl.ds(start, n)
    return pl.ds(r + d * start, n, stride=d)


def _band_window(i, L):
    W = min(2 * TQ, L)
    q0 = pl.multiple_of(i * TQ, TQ)
    k0 = pl.multiple_of(jnp.clip(q0 - HALF_SPAN, 0, L - W), HALF_SPAN)
    qpos = q0 + (lax.broadcasted_iota(jnp.int32, (2 * TQ, W), 0) & (TQ - 1))
    kpos = k0 + lax.broadcasted_iota(jnp.int32, (2 * TQ, W), 1)
    valid = jnp.abs(qpos - kpos) <= HALF_SPAN
    return W, q0, k0, valid


def _stack_heads(t, lo):
    z = jnp.zeros_like(t)
    return jnp.concatenate([jnp.where(lo, t, z), jnp.where(lo, z, t)], axis=0)


def _unstack_heads(t2, lo):
    return jnp.where(lo, t2[0:TQ], t2[TQ:2 * TQ])


CHAINS = 4


def _interleave(d):
    ru = min(d, CHAINS)
    return ru, min(CHAINS // ru, S // d // TQ)


def _for_blocks(n, fn):
    if n == 1:
        fn(0)
    else:
        def it(j, _):
            fn(j)
            return 0
        lax.fori_loop(0, n, it, 0)


def attn_fwd(proj, tabs, qw2, kw2, sides=()):
    CH = 256

    def body(q_ref, k_ref, v_ref, c_ref, s1_ref, s2_ref, qw_ref, kw_ref, at_ref, ls_ref,
             qs, ks, vs, osub, lsub, onat, lnat):
        g = pl.program_id(1)
        lo = lax.broadcasted_iota(jnp.int32, (1, LANES), 1) < HD
        e = _head_mat()

        def prep(t, w, c, s1, s2):
            r = lax.rsqrt(_head_mean(t * t, e) + EPS)
            return _rope(t * r * w, c, s1, s2)

        def group(gi, d):
            L = S // d

            ru, nb = _interleave(d)

            def stage(r, off):
                for c0 in range(0, L, CH):
                    n = min(CH, L)
                    rows = _sub_rows(r, d, c0, n)
                    c, s1, s2 = c_ref[rows, :], s1_ref[rows, :], s2_ref[rows, :]
                    dst = pl.ds(off + c0, n)
                    qs[dst, :] = (prep(q_ref[rows, :], qw_ref[...], c, s1, s2) * (HD ** -0.5)).astype(BF16)
                    ks[dst, :] = prep(k_ref[rows, :], kw_ref[...], c, s1, s2).astype(BF16)
                    vs[dst, :] = v_ref[rows, :].astype(BF16)

            def one(off, i):
                W, q0, k0, valid = _band_window(i, L)
                q2 = _stack_heads(qs[pl.ds(off + q0, TQ), :], lo)
                sc = jnp.where(valid, _dot_nt(q2, ks[pl.ds(off + k0, W), :]), NEG_INF)
                m = jnp.max(sc, axis=-1, keepdims=True)
                p = jnp.exp(sc - m)
                den = jnp.sum(p, axis=-1, keepdims=True)
                o2 = _dot(p.astype(BF16), vs[pl.ds(off + k0, W), :]) / den
                l2 = jnp.broadcast_to(m + jnp.log(den), (2 * TQ, LANES))
                osub[pl.ds(off + q0, TQ), :] = _unstack_heads(o2, lo)
                lsub[pl.ds(off + q0, TQ), :] = _unstack_heads(l2, lo)

            def unstage(r, off):
                for c0 in range(0, L, CH):
                    n = min(CH, L)
                    rows = _sub_rows(r, d, c0, n)
                    onat[gi, rows, :] = osub[pl.ds(off + c0, n), :]
                    lnat[gi, rows, :] = lsub[pl.ds(off + c0, n), :]

            def step(t, _):
                for u in range(ru):
                    stage(t * ru + u, u * L)
                _for_blocks(L // TQ // nb, lambda j: [one(u * L, j * nb + b) for u in range(ru) for b in range(nb)])
                for u in range(ru):
                    unstage(t * ru + u, u * L)
                return 0

            lax.fori_loop(0, d // ru, step, 0)

        for gi, d in enumerate(DILATIONS):
            pl.when(g == gi)(functools.partial(group, gi, d))

        @pl.when(g == len(DILATIONS) - 1)
        def _():
            def mix(i, _):
                rows = pl.ds(pl.multiple_of(i * CH, CH), CH)
                l0, l1, l2 = lnat[0, rows, :], lnat[1, rows, :], lnat[2, rows, :]
                m = jnp.maximum(jnp.maximum(l0, l1), l2)
                e0, e1, e2 = jnp.exp(l0 - m), jnp.exp(l1 - m), jnp.exp(l2 - m)
                den = e0 + e1 + e2
                a = (e0 * onat[0, rows, :] + e1 * onat[1, rows, :] + e2 * onat[2, rows, :]) / den
                at_ref[rows, :] = a.astype(BF16)
                ls_ref[rows, :] = m + jnp.log(den)
                return 0

            lax.fori_loop(0, S // CH, mix, 0)

    out_spec = pl.BlockSpec((S, LANES), lambda hp, g: (0, hp))
    return _call(
        body, sides, name="attn_fwd", grid=(4, 3),
        in_specs=_qk_specs() + _tab_specs() + [_vec_spec(), _vec_spec()],
        out_specs=[out_spec, out_spec],
        out_shape=[jax.ShapeDtypeStruct((S, CC), BF16), jax.ShapeDtypeStruct((S, CC), F32)],
        scratch_shapes=[pltpu.VMEM((S, LANES), BF16)] * 3 + [pltpu.VMEM((S, LANES), F32)] * 2
        + [pltpu.VMEM((3, S, LANES), F32)] * 2,
        args=(proj, proj, proj, *tabs, qw2, kw2))


def attn_bwd(proj, tabs, qw2, kw2, d_attn, attn, lse, sides=()):
    CH = 256

    def body(q_ref, k_ref, v_ref, c_ref, s1_ref, s2_ref, qw_ref, kw_ref, do_ref, at_ref, ls_ref,
             dq_ref, dk_ref, dv_ref, gqw_ref, gkw_ref,
             qs, ks, vs, dos, dsub, lsub, dqs, dks, dvs, dnat, dqn, dkn, dvn):
        hp, g = pl.program_id(0), pl.program_id(1)
        lo = lax.broadcasted_iota(jnp.int32, (1, LANES), 1) < HD
        e = _head_mat()

        @pl.when((hp == 0) & (g == 0))
        def _():
            gqw_ref[...] = jnp.zeros_like(gqw_ref)
            gkw_ref[...] = jnp.zeros_like(gkw_ref)

        def dsum(i, _):
            rows = pl.ds(pl.multiple_of(i * CH, CH), CH)
            dnat[rows, :] = _head_mean(do_ref[rows, :] * at_ref[rows, :].astype(F32), e) * float(HD)
            return 0

        lax.fori_loop(0, S // CH, dsum, 0)

        def group(d):
            L = S // d

            ru, nb = _interleave(d)

            def stage(r, off):
                for c0 in range(0, L, CH):
                    n = min(CH, L)
                    rows = _sub_rows(r, d, c0, n)
                    c, s1, s2 = c_ref[rows, :], s1_ref[rows, :], s2_ref[rows, :]
                    dst = pl.ds(off + c0, n)
                    qv, kv = q_ref[rows, :], k_ref[rows, :]
                    rq = lax.rsqrt(_head_mean(qv * qv, e) + EPS)
                    rk = lax.rsqrt(_head_mean(kv * kv, e) + EPS)
                    qs[dst, :] = (_rope(qv * rq * qw_ref[...], c, s1, s2) * (HD ** -0.5)).astype(BF16)
                    ks[dst, :] = _rope(kv * rk * kw_ref[...], c, s1, s2).astype(BF16)
                    vs[dst, :] = v_ref[rows, :].astype(BF16)
                    dos[dst, :] = do_ref[rows, :].astype(BF16)
                    dsub[dst, :] = dnat[rows, :]
                    lsub[dst, :] = ls_ref[rows, :]
                    dks[dst, :] = jnp.zeros((n, LANES), F32)
                    dvs[dst, :] = jnp.zeros((n, LANES), F32)

            def one(off, i):
                W, q0, k0, valid = _band_window(i, L)
                qrows, krows = pl.ds(off + q0, TQ), pl.ds(off + k0, W)
                q2 = _stack_heads(qs[qrows, :], lo)
                do2 = _stack_heads(dos[qrows, :], lo)
                kk, vv = ks[krows, :], vs[krows, :]
                lse_b, dd_b = lsub[qrows, :], dsub[qrows, :]
                lse2 = jnp.concatenate([lse_b[:, 0:1], lse_b[:, HD:HD + 1]], axis=0)
                dd2 = jnp.concatenate([dd_b[:, 0:1], dd_b[:, HD:HD + 1]], axis=0)
                sc = jnp.where(valid, _dot_nt(q2, kk), NEG_INF)
                p = jnp.exp(sc - lse2)
                ds = (p * (_dot_nt(do2, vv) - dd2)).astype(BF16)
                dqs[qrows, :] = _unstack_heads(_dot(ds, kk), lo)
                dks[krows, :] = dks[krows, :] + _dot_tn(ds, q2)
                dvs[krows, :] = dvs[krows, :] + _dot_tn(p.astype(BF16), do2)

            def unstage(r, off):
                gq = jnp.zeros((1, LANES), F32)
                gk = jnp.zeros((1, LANES), F32)
                for c0 in range(0, L, CH):
                    n = min(CH, L)
                    rows = _sub_rows(r, d, c0, n)
                    src = pl.ds(off + c0, n)
                    c, s1, s2 = c_ref[rows, :], s1_ref[rows, :], s2_ref[rows, :]
                    for (raw_ref, w_ref, gsub, scale, nat) in (
                            (q_ref, qw_ref, dqs, HD ** -0.5, dqn), (k_ref, kw_ref, dks, 1.0, dkn)):
                        t = raw_ref[rows, :]
                        rr = lax.rsqrt(_head_mean(t * t, e) + EPS)
                        tn = t * rr
                        dy = _rope_t(gsub[src, :] * scale, c, s1, s2)
                        gw = jnp.sum(dy * tn, axis=0, keepdims=True)
                        if raw_ref is q_ref:
                            gq = gq + gw
                        else:
                            gk = gk + gw
                        dtn = dy * w_ref[...]
                        nat[rows, :] = rr * (dtn - tn * _head_mean(dtn * tn, e))
                    dvn[rows, :] = dvs[src, :]
                gqw_ref[0:1, :] = gqw_ref[0:1, :] + gq
                gkw_ref[0:1, :] = gkw_ref[0:1, :] + gk

            def step(t, _):
                for u in range(ru):
                    stage(t * ru + u, u * L)
                _for_blocks(L // TQ // nb, lambda j: [one(u * L, j * nb + b) for u in range(ru) for b in range(nb)])
                for u in range(ru):
                    unstage(t * ru + u, u * L)
                return 0

            lax.fori_loop(0, d // ru, step, 0)

        for gi, d in enumerate(DILATIONS):
            pl.when(g == gi)(functools.partial(group, d))

        def emit(i, _):
            rows = pl.ds(pl.multiple_of(i * CH, CH), CH)
            dq_ref[rows, :] = dqn[rows, :].astype(BF16)
            dk_ref[rows, :] = dkn[rows, :].astype(BF16)
            dv_ref[rows, :] = dvn[rows, :].astype(BF16)
            return 0

        lax.fori_loop(0, S // CH, emit, 0)

    nat_spec = pl.BlockSpec((S, LANES), lambda hp, g: (0, hp))
    out_spec = pl.BlockSpec((S, LANES), lambda hp, g: (0, g * 4 + hp))
    acc_spec = pl.BlockSpec((8, LANES), lambda hp, g: (0, 0))
    return _call(
        body, sides, name="attn_bwd", grid=(4, 3),
        in_specs=_qk_specs() + _tab_specs() + [_vec_spec(), _vec_spec(), nat_spec, nat_spec, nat_spec],
        out_specs=[out_spec] * 3 + [acc_spec] * 2,
        out_shape=[jax.ShapeDtypeStruct((S, QKV), BF16)] * 3 + [jax.ShapeDtypeStruct((8, LANES), F32)] * 2,
        scratch_shapes=[pltpu.VMEM((S, LANES), BF16)] * 4 + [pltpu.VMEM((S, LANES), F32)] * 9,
        args=(proj, proj, proj, *tabs, qw2, kw2, d_attn, attn, lse))


PADR = 16
CT = 128


def _conv_specs():
    return [pl.BlockSpec((S, CC), lambda i: (0, OFF_CA // CC)), pl.BlockSpec((S, CC), lambda i: (0, OFF_CB // CC))]


NCB = CC // LANES


def _pad_zero(pad):
    for cb in range(NCB):
        pad[cb, 0:PADR, :] = jnp.zeros((PADR, LANES), F32)
        pad[cb, PADR + S:PADR + S + PADR, :] = jnp.zeros((PADR, LANES), F32)


def _pad_store(pad, row0, n, val):
    for cb in range(NCB):
        pad[cb, pl.ds(pl.multiple_of(row0 + PADR, 8), n), :] = val[:, cb * LANES:(cb + 1) * LANES]


def _taps(pad_ref, cb, s0, weights):
    acc = jnp.zeros((CT, LANES), F32)
    for k in range(KW):
        acc = acc + weights[k] * pad_ref[cb, pl.ds(s0 + k + 1, CT), :]
    return acc


def conv_fwd(proj, conv_w, conv_b, ln_w, ln_b):
    def body(a_ref, b_ref, w_ref, cb_ref, lw_ref, lb_ref, c_ref, u3_ref, upad):
        _pad_zero(upad)

        def glu(i, _):
            rows = pl.ds(pl.multiple_of(i * TM, TM), TM)
            _pad_store(upad, i * TM, TM, a_ref[rows, :] * _sigmoid(b_ref[rows, :]))
            return 0

        lax.fori_loop(0, S // TM, glu, 0)

        def chunk(i, _):
            s0 = pl.multiple_of(i * CT, CT)
            for cb in range(CC // LANES):
                cols = slice(cb * LANES, (cb + 1) * LANES)
                w = [w_ref[k:k + 1, cols] for k in range(KW)]
                c_ref[pl.ds(s0, CT), cols] = _taps(upad, cb, s0, w) + cb_ref[:, cols]
            cv = c_ref[pl.ds(s0, CT), :]
            mu = jnp.mean(cv, axis=-1, keepdims=True)
            xc = cv - mu
            rstd = lax.rsqrt(jnp.mean(xc * xc, axis=-1, keepdims=True) + EPS)
            yl = xc * rstd * lw_ref[...] + lb_ref[...]
            u3_ref[pl.ds(s0, CT), :] = (yl * _sigmoid(yl)).astype(BF16)
            return 0

        lax.fori_loop(0, S // CT, chunk, 0)

    vec = pl.BlockSpec((1, CC), lambda i: (0, 0))
    full = pl.BlockSpec((S, CC), lambda i: (0, 0))
    return pl.pallas_call(
        body, name="conv_fwd", grid=(1,),
        in_specs=_conv_specs() + [pl.BlockSpec((KW, CC), lambda i: (0, 0)), vec, vec, vec],
        out_specs=[full, full],
        out_shape=[jax.ShapeDtypeStruct((S, CC), F32), jax.ShapeDtypeStruct((S, CC), BF16)],
        scratch_shapes=[pltpu.VMEM((NCB, S + 2 * PADR, LANES), F32)],
        compiler_params=_cp(dimension_semantics=("arbitrary",)),
    )(proj, proj, conv_w, conv_b, ln_w, ln_b)


def conv_bwd(proj, cpre, d_u3, conv_w, conv_w_rev, ln_w, ln_b, sides=()):
    def body(a_ref, b_ref, c_ref, du3_ref, w_ref, wr_ref, lw_ref, lb_ref,
             dc_ref, gw_ref, gcb_ref, glw_ref, glb_ref, upad, dpad):
        _pad_zero(upad)
        _pad_zero(dpad)
        gw_ref[...] = jnp.zeros_like(gw_ref)

        def ln_bwd(i, carry):
            gcb, glw, glb = carry
            rows = pl.ds(pl.multiple_of(i * TM, TM), TM)
            _pad_store(upad, i * TM, TM, a_ref[rows, :] * _sigmoid(b_ref[rows, :]))
            cv = c_ref[rows, :]
            mu = jnp.mean(cv, axis=-1, keepdims=True)
            xc = cv - mu
            rstd = lax.rsqrt(jnp.mean(xc * xc, axis=-1, keepdims=True) + EPS)
            xh = xc * rstd
            yl = xh * lw_ref[...] + lb_ref[...]
            dyl = du3_ref[rows, :] * _dsilu(yl, _sigmoid(yl))
            dxh = dyl * lw_ref[...]
            dcv = rstd * (dxh - jnp.mean(dxh, axis=-1, keepdims=True)
                          - xh * jnp.mean(dxh * xh, axis=-1, keepdims=True))
            _pad_store(dpad, i * TM, TM, dcv)
            return (gcb + jnp.sum(dcv, axis=0, keepdims=True),
                    glw + jnp.sum(dyl * xh, axis=0, keepdims=True),
                    glb + jnp.sum(dyl, axis=0, keepdims=True))

        z = jnp.zeros((1, CC), F32)
        gcb, glw, glb = lax.fori_loop(0, S // TM, ln_bwd, (z, z, z))
        gcb_ref[...] = gcb
        glw_ref[...] = glw
        glb_ref[...] = glb

        def chunk(i, _):
            s0 = pl.multiple_of(i * CT, CT)
            for cb in range(CC // LANES):
                cols = slice(cb * LANES, (cb + 1) * LANES)
                wr = [wr_ref[k:k + 1, cols] for k in range(KW)]
                du = _taps(dpad, cb, s0, wr)
                dcv = dpad[cb, pl.ds(s0 + PADR, CT), :]
                for k in range(KW):
                    gw_ref[k:k + 1, cols] = gw_ref[k:k + 1, cols] + jnp.sum(
                        upad[cb, pl.ds(s0 + k + 1, CT), :] * dcv, axis=0, keepdims=True)
                av = a_ref[pl.ds(s0, CT), cols]
                sb = _sigmoid(b_ref[pl.ds(s0, CT), cols])
                dc_ref[pl.ds(s0, CT), cols] = (du * sb).astype(BF16)
                dc_ref[pl.ds(s0, CT), slice(CC + cb * LANES, CC + (cb + 1) * LANES)] = (
                    du * av * sb * (1.0 - sb)).astype(BF16)
            return 0

        lax.fori_loop(0, S // CT, chunk, 0)

    vec = pl.BlockSpec((1, CC), lambda i: (0, 0))
    full = pl.BlockSpec((S, CC), lambda i: (0, 0))
    wsp = pl.BlockSpec((KW, CC), lambda i: (0, 0))
    return _call(
        body, sides, name="conv_bwd", grid=(1,),
        in_specs=_conv_specs() + [full, full, wsp, wsp, vec, vec],
        out_specs=[pl.BlockSpec((S, 2 * CC), lambda i: (0, 0)), wsp, vec, vec, vec],
        out_shape=[jax.ShapeDtypeStruct((S, 2 * CC), BF16), jax.ShapeDtypeStruct((KW, CC), F32)]
        + [jax.ShapeDtypeStruct((1, CC), F32)] * 3,
        scratch_shapes=[pltpu.VMEM((NCB, S + 2 * PADR, LANES), F32)] * 2,
        args=(proj, proj, cpre, d_u3, conv_w, conv_w_rev, ln_w, ln_b))


def _gate_specs():
    return [_row(CC, col=OFF_GA // CC + j) for j in range(4)]


def _gates(g_refs, bg_ref):
    ga = _sigmoid(jnp.concatenate([g_refs[0][...], g_refs[1][...]], axis=1) + bg_ref[0:1, :])
    gb = _sigmoid(jnp.concatenate([g_refs[2][...], g_refs[3][...]], axis=1) + bg_ref[1:2, :])
    return ga, gb


def mix_out(x, proj, b_gate, attn, u3, w_o, w_pw, w_out):
    def body(x_ref, g0, g1, g2, g3, bg_ref, at_ref, u3_ref, wo_ref, wp_ref, wout_ref,
             x1_ref, z_ref, ya_ref, yb_ref):
        ga, gb = _gates((g0, g1, g2, g3), bg_ref)
        ya = _dot(at_ref[...], wo_ref[...])
        yb = _dot(u3_ref[...], wp_ref[...])
        z = (ga * ya + gb * yb).astype(BF16)
        ya_ref[...] = ya.astype(BF16)
        yb_ref[...] = yb.astype(BF16)
        z_ref[...] = z
        x1_ref[...] = x_ref[...] + _dot(z, wout_ref[...])

    return pl.pallas_call(
        body, name="mix_out", grid=(S // TM,),
        in_specs=[_row(D)] + _gate_specs() + [_res((2, D)), _row(CC), _row(CC),
                                              _res((CC, D)), _res((CC, D)), _res((D, D))],
        out_specs=[_row(D)] * 4,
        out_shape=[jax.ShapeDtypeStruct((S, D), F32)] + [jax.ShapeDtypeStruct((S, D), BF16)] * 3,
        compiler_params=_cp(dimension_semantics=("arbitrary",)),
    )(x, proj, proj, proj, proj, b_gate, attn, u3, w_o, w_pw, w_out)


def out_bwd(d_x1b, proj, b_gate, ya, yb, w_o, w_pw, w_out, sides=()):
    def body(dx_ref, g0, g1, g2, g3, bg_ref, ya_ref, yb_ref, wo_ref, wp_ref, wout_ref,
             dya_ref, dyb_ref, dgl_ref, dat_ref, du3_ref, gbg_ref):
        @pl.when(pl.program_id(0) == 0)
        def _():
            gbg_ref[...] = jnp.zeros_like(gbg_ref)

        ga, gb = _gates((g0, g1, g2, g3), bg_ref)
        dz = _dot_nt(dx_ref[...], wout_ref[...])
        dya = (dz * ga).astype(BF16)
        dyb = (dz * gb).astype(BF16)
        dgla = dz * ya_ref[...].astype(F32) * ga * (1.0 - ga)
        dglb = dz * yb_ref[...].astype(F32) * gb * (1.0 - gb)
        dya_ref[...] = dya
        dyb_ref[...] = dyb
        dgl_ref[:, 0:D] = dgla.astype(BF16)
        dgl_ref[:, D:2 * D] = dglb.astype(BF16)
        gbg_ref[0:1, :] = gbg_ref[0:1, :] + jnp.sum(dgla, axis=0, keepdims=True)
        gbg_ref[1:2, :] = gbg_ref[1:2, :] + jnp.sum(dglb, axis=0, keepdims=True)
        dat_ref[...] = _dot_nt(dya, wo_ref[...])
        du3_ref[...] = _dot_nt(dyb, wp_ref[...])

    return _call(
        body, sides, name="out_bwd", grid=(S // TM,),
        in_specs=[_row(D)] + _gate_specs() + [_res((2, D)), _row(D), _row(D),
                                              _res((CC, D)), _res((CC, D)), _res((D, D))],
        out_specs=[_row(D), _row(D), _row(2 * D), _row(CC), _row(CC), pl.BlockSpec((2, D), lambda i: (0, 0))],
        out_shape=[jax.ShapeDtypeStruct((S, D), BF16)] * 2 + [jax.ShapeDtypeStruct((S, 2 * D), BF16)]
        + [jax.ShapeDtypeStruct((S, CC), F32)] * 2 + [jax.ShapeDtypeStruct((2, D), F32)],
        args=(d_x1b, proj, proj, proj, proj, b_gate, ya, yb, w_o, w_pw, w_out))


def ffn_in(x1, norm_w, w_ffn_in, sides=()):
    half = FF // 2

    def body(x_ref, nw_ref, w_ref, h_ref, gu_ref, f_ref):
        xv = x_ref[...]
        r = lax.rsqrt(jnp.mean(xv * xv, axis=-1, keepdims=True) + EPS)
        h = (xv * r * nw_ref[...]).astype(BF16)
        h_ref[...] = h
        for j in range(2):
            gt = _dot_nt(h, w_ref[j * half:(j + 1) * half, :])
            up = _dot_nt(h, w_ref[FF + j * half:FF + (j + 1) * half, :])
            gu_ref[:, j * half:(j + 1) * half] = gt.astype(BF16)
            gu_ref[:, FF + j * half:FF + (j + 1) * half] = up.astype(BF16)
            f_ref[:, j * half:(j + 1) * half] = (gt * _sigmoid(gt) * up).astype(BF16)

    return _call(
        body, sides, name="ffn_in", grid=(S // TM,),
        in_specs=[_row(D), _res((1, D)), _res((2 * FF, D))],
        out_specs=[_row(D), _row(2 * FF), _row(FF)],
        out_shape=[jax.ShapeDtypeStruct((S, D), BF16), jax.ShapeDtypeStruct((S, 2 * FF), BF16),
                   jax.ShapeDtypeStruct((S, FF), BF16)],
        args=(x1, norm_w, w_ffn_in))


def ffn_out_loss(x1, f, w_ffn_out, target):
    def body(x_ref, f_ref, w_ref, t_ref, dy_ref, dyb_ref, sq_ref):
        @pl.when(pl.program_id(0) == 0)
        def _():
            sq_ref[...] = jnp.zeros_like(sq_ref)

        diff = x_ref[...] + _dot(f_ref[...], w_ref[...]) - t_ref[...]
        dy = diff * (1.0 / D)
        dy_ref[...] = dy
        dyb_ref[...] = dy.astype(BF16)
        sq_ref[...] = sq_ref[...] + jnp.sum((diff * diff).reshape(TM // 8, 8, D), axis=0)

    return pl.pallas_call(
        body, name="ffn_out_loss", grid=(S // TM,),
        in_specs=[_row(D), _row(FF), _res((FF, D)), _row(D)],
        out_specs=[_row(D), _row(D), pl.BlockSpec((8, D), lambda i: (0, 0))],
        out_shape=[jax.ShapeDtypeStruct((S, D), F32), jax.ShapeDtypeStruct((S, D), BF16),
                   jax.ShapeDtypeStruct((8, D), F32)],
        compiler_params=_cp(dimension_semantics=("arbitrary",)),
    )(x1, f, w_ffn_out, target)


def _rms_bwd(xv, nw, dh):
    r = lax.rsqrt(jnp.mean(xv * xv, axis=-1, keepdims=True) + EPS)
    xn = xv * r
    dxn = dh * nw
    dx = r * (dxn - xn * jnp.mean(dxn * xn, axis=-1, keepdims=True))
    return dx, dh * xn


def ffn_bwd(dy, dyb, gu, x1, norm_w, w_ffn_in, w_ffn_out, sides=()):
    def body(dy_ref, dyb_ref, gu_ref, x_ref, nw_ref, wi_ref, wo_ref, dgu_ref, dx_ref, dxb_ref, gn_ref):
        @pl.when(pl.program_id(0) == 0)
        def _():
            gn_ref[...] = jnp.zeros_like(gn_ref)

        df = _dot_nt(dyb_ref[...], wo_ref[...])
        gt = gu_ref[:, 0:FF].astype(F32)
        up = gu_ref[:, FF:2 * FF].astype(F32)
        sg = _sigmoid(gt)
        dgt = (df * up * _dsilu(gt, sg)).astype(BF16)
        dup = (df * gt * sg).astype(BF16)
        dgu_ref[:, 0:FF] = dgt
        dgu_ref[:, FF:2 * FF] = dup
        dh = _dot(dgt, wi_ref[0:FF, :]) + _dot(dup, wi_ref[FF:2 * FF, :])
        dxn, gw = _rms_bwd(x_ref[...], nw_ref[...], dh)
        dx = dy_ref[...] + dxn
        dx_ref[...] = dx
        dxb_ref[...] = dx.astype(BF16)
        gn_ref[...] = gn_ref[...] + jnp.sum(gw, axis=0, keepdims=True)

    return _call(
        body, sides, name="ffn_bwd", grid=(S // TM,),
        in_specs=[_row(D), _row(D), _row(2 * FF), _row(D), _res((1, D)), _res((2 * FF, D)), _res((FF, D))],
        out_specs=[_row(2 * FF), _row(D), _row(D), pl.BlockSpec((1, D), lambda i: (0, 0))],
        out_shape=[jax.ShapeDtypeStruct((S, 2 * FF), BF16), jax.ShapeDtypeStruct((S, D), F32),
                   jax.ShapeDtypeStruct((S, D), BF16), jax.ShapeDtypeStruct((1, D), F32)],
        args=(dy, dyb, gu, x1, norm_w, w_ffn_in, w_ffn_out))


def in_bwd(d_q, d_k, d_v, d_conv, d_gl, w_in, x, d_x1, norm_w, sides=()):
    segs = ((OFF_Q, QKV), (OFF_K, QKV), (OFF_V, QKV), (OFF_CA, 2 * CC), (OFF_GA, 2 * D))

    def body(dq_ref, dk_ref, dv_ref, dc_ref, dg_ref, w_ref, x_ref, dx1_ref, nw_ref, gx_ref, gn_ref):
        @pl.when(pl.program_id(0) == 0)
        def _():
            gn_ref[...] = jnp.zeros_like(gn_ref)

        dh = jnp.zeros((TM, D), F32)
        for ref, (off, width) in zip((dq_ref, dk_ref, dv_ref, dc_ref, dg_ref), segs):
            dh = dh + _dot(ref[...], w_ref[off:off + width, :])
        dxn, gw = _rms_bwd(x_ref[...], nw_ref[...], dh)
        gx_ref[...] = dx1_ref[...] + dxn
        gn_ref[...] = gn_ref[...] + jnp.sum(gw, axis=0, keepdims=True)

    return _call(
        body, sides, name="in_bwd", grid=(S // TM,),
        in_specs=[_row(QKV)] * 3 + [_row(2 * CC), _row(2 * D), _res((INW, D)), _row(D), _row(D), _res((1, D))],
        out_specs=[_row(D), pl.BlockSpec((1, D), lambda i: (0, 0))],
        out_shape=[jax.ShapeDtypeStruct((S, D), F32), jax.ShapeDtypeStruct((1, D), F32)],
        args=(d_q, d_k, d_v, d_conv, d_gl, w_in, x, d_x1, norm_w))


def mm_tn(name, a, b, tm, tn, sides=()):
    M, N = a.shape[1], b.shape[1]

    def body(a_ref, b_ref, o_ref):
        o_ref[...] = _dot_tn(a_ref[...], b_ref[...])

    res = _call(
        body, sides, name=name, grid=(M // tm, N // tn),
        in_specs=[pl.BlockSpec((S, tm), lambda i, j: (0, i)), pl.BlockSpec((S, tn), lambda i, j: (0, j))],
        out_specs=[pl.BlockSpec((tm, tn), lambda i, j: (i, j))],
        out_shape=[jax.ShapeDtypeStruct((M, N), F32)],
        args=(a, b))
    return (res[0][0], res[1]) if sides else res[0]


GW_IN_TN = 512


def gw_in_t(name, h, d_segs, col_half, sides=()):
    tn, hw = GW_IN_TN, D // 2
    starts, t0 = [], 0
    for seg in d_segs:
        starts.append(t0)
        t0 += seg.shape[1] // tn
    ntiles = [seg.shape[1] // tn for seg in d_segs]

    def body(h_ref, *refs):
        a_refs, o_ref = refs[:-1], refs[-1]
        n = pl.program_id(0)
        for a_ref, st, nt in zip(a_refs, starts, ntiles):
            @pl.when((n >= st) & (n < st + nt))
            def _(a_ref=a_ref):
                o_ref[...] = _dot_tn(a_ref[...], h_ref[...])

    def seg_spec(st, nt):
        return pl.BlockSpec((S, tn), lambda n: (0, jnp.clip(n - st, 0, nt - 1)))

    res = _call(
        body, sides, name=name, grid=(INW // tn,),
        in_specs=[pl.BlockSpec((S, hw), lambda n: (0, col_half))] + [seg_spec(st, nt) for st, nt in zip(starts, ntiles)],
        out_specs=[pl.BlockSpec((tn, hw), lambda n: (n, 0))],
        out_shape=[jax.ShapeDtypeStruct((INW, hw), F32)],
        args=(h, *d_segs))
    return (res[0][0], res[1]) if sides else res[0]


def _place():
    x, y, c = lax.axis_index("x"), lax.axis_index("y"), lax.axis_index("c")
    chips = [(1 - x, y), (x, 1 - y), (1 - x, 1 - y)]
    return x, y, c, chips


def _sems(n):
    return pltpu.SemaphoreType.DMA((n,))


def _remote(src, dst, send, recv, k, to):
    return pltpu.make_async_remote_copy(src_ref=src, dst_ref=dst, send_sem=send.at[k], recv_sem=recv.at[k],
                                        device_id=to, device_id_type=MESH)


def _cast_rows(dst, src, cols=slice(None)):
    rows = src.shape[0]
    step = next((s for s in (128, 64, 32, 16) if rows % s == 0), rows)
    for r0 in range(0, rows, step):
        dst[r0:r0 + step, cols] = src[r0:r0 + step, :].astype(dst.dtype)


def comm_only(name, sides):
    def body():
        pass

    return _call(body, sides, name=name, grid=(1,), in_specs=[], out_specs=[], out_shape=[], args=())[1]


def ag_blocks(shard, dtype):
    R, W = shard.shape

    def copy(outs, scr, k, block, to, src=None):
        dst = outs[0].at[block]
        return _remote(dst if src is None else src, dst, scr[1], scr[2], k, to)

    def local(outs, scr, me):
        return pltpu.make_async_copy(scr[0], outs[0].at[me], scr[3].at[0])

    def start(ins, outs, scr):
        x, y, c, chips = _place()
        me = 4 * x + 2 * y + c
        _cast_rows(scr[0], ins[0])
        local(outs, scr, me).start()
        copy(outs, scr, 0, me, (x, y, 1 - c), src=scr[0]).start()
        for j, (cx, cy) in enumerate(chips):
            copy(outs, scr, 1 + j, me, (cx, cy, c), src=scr[0]).start()

    def finish(ins, outs, scr):
        x, y, c, chips = _place()
        me, sib = 4 * x + 2 * y + c, (x, y, 1 - c)
        passed = []
        for j, (cx, cy) in enumerate(chips):
            theirs = 4 * cx + 2 * cy + c
            copy(outs, scr, 1 + j, theirs, (x, y, c)).wait_recv()
            fwd = copy(outs, scr, 4 + j, theirs, sib)
            fwd.start()
            passed.append(fwd)
        copy(outs, scr, 0, 4 * x + 2 * y + 1 - c, (x, y, c)).wait_recv()
        for j, (cx, cy) in enumerate(chips):
            copy(outs, scr, 4 + j, 4 * cx + 2 * cy + 1 - c, (x, y, c)).wait_recv()
        copy(outs, scr, 0, me, sib, src=scr[0]).wait_send()
        for j, (cx, cy) in enumerate(chips):
            copy(outs, scr, 1 + j, me, (cx, cy, c), src=scr[0]).wait_send()
        for fwd in passed:
            fwd.wait_send()
        local(outs, scr, me).wait()

    return Side((shard,), (VMEM,), (jax.ShapeDtypeStruct((NDEV, R, W), dtype),),
                (pltpu.VMEM((R, W), dtype), _sems(7), _sems(7), _sems(1)), start, finish)


def ag_blocks_relay(shard, dtype):
    R, W = shard.shape
    half = R // 2

    def copy(outs, scr, k, block, to, src=None, rows=None):
        dst = outs[0].at[block] if rows is None else outs[0].at[block, pl.ds(rows * half, half), :]
        return _remote(dst if src is None else src, dst, scr[1], scr[2], k, to)

    def local(outs, scr, me):
        return pltpu.make_async_copy(scr[0], outs[0].at[me], scr[3].at[0])

    def own(outs, scr):
        x, y, c, _ = _place()
        me = 4 * x + 2 * y + c
        return [copy(outs, scr, k, me, to, src=scr[0])
                for k, to in enumerate([(x, y, 1 - c), (1 - x, y, c), (x, 1 - y, c)])]

    def start(ins, outs, scr):
        x, y, c, _ = _place()
        _cast_rows(scr[0], ins[0])
        local(outs, scr, 4 * x + 2 * y + c).start()
        for cp in own(outs, scr):
            cp.start()

    def passed_on(outs, scr):
        x, y, c, _ = _place()
        sib, xn, yn = (x, y, 1 - c), (1 - x, y, c), (x, 1 - y, c)
        b_xn, b_yn, b_dg = 4 * (1 - x) + 2 * y + c, 4 * x + 2 * (1 - y) + c, 4 * (1 - x) + 2 * (1 - y) + c
        near = [copy(outs, scr, 5, b_xn, yn, rows=0), copy(outs, scr, 3, b_xn, sib),
                copy(outs, scr, 6, b_yn, xn, rows=1), copy(outs, scr, 4, b_yn, sib)]
        far = [copy(outs, scr, 7, b_dg, sib, rows=0), copy(outs, scr, 8, b_dg, sib, rows=1)]
        return (b_xn, b_yn, b_dg), near, far

    def mid(ins, outs, scr):
        x, y, c, _ = _place()
        (b_xn, b_yn, _), near, _ = passed_on(outs, scr)
        copy(outs, scr, 1, b_xn, (x, y, c)).wait_recv()
        near[0].start()
        near[1].start()
        copy(outs, scr, 2, b_yn, (x, y, c)).wait_recv()
        near[2].start()
        near[3].start()

    def finish(ins, outs, scr):
        x, y, c, _ = _place()
        here = (x, y, c)
        (b_xn, b_yn, b_dg), near, far = passed_on(outs, scr)
        copy(outs, scr, 5, b_dg, here, rows=0).wait_recv()
        far[0].start()
        copy(outs, scr, 6, b_dg, here, rows=1).wait_recv()
        far[1].start()
        flip = 1 - 2 * c
        copy(outs, scr, 0, 4 * x + 2 * y + 1 - c, here).wait_recv()
        copy(outs, scr, 3, b_xn + flip, here).wait_recv()
        copy(outs, scr, 4, b_yn + flip, here).wait_recv()
        copy(outs, scr, 7, b_dg + flip, here, rows=0).wait_recv()
        copy(outs, scr, 8, b_dg + flip, here, rows=1).wait_recv()
        for cp in own(outs, scr) + near + far:
            cp.wait_send()
        local(outs, scr, 4 * x + 2 * y + c).wait()

    return Side((shard,), (VMEM,), (jax.ShapeDtypeStruct((NDEV, R, W), dtype),),
                (pltpu.VMEM((R, W), dtype), _sems(9), _sems(9), _sems(1)), start, finish, mid)


def ag_cols(shard):
    K, C = shard.shape
    half, w2 = K // 2, 2 * C

    def win(out, rows_c, chip):
        return out.at[pl.ds(pl.multiple_of(rows_c * half, 16), half), pl.ds(pl.multiple_of(chip * w2, LANES), w2)]

    def ici(outs, scr, j, to, c, k):
        slab, send, recv = scr[2], scr[5], scr[6]
        return _remote(slab.at[pl.ds(pl.multiple_of(c * half, 16), half), :], win(outs[0], c, k), send, recv, j, to)

    def local(outs, scr, k):
        return pltpu.make_async_copy(scr[2], outs[0].at[:, pl.ds(pl.multiple_of(k * w2, LANES), w2)], scr[7].at[0])

    def start(ins, outs, scr):
        stage, inbox, slab, xs, xr = scr[:5]
        x, y, c, chips = _place()
        k = 2 * x + y
        _cast_rows(stage, ins[0])
        swap = _remote(stage, inbox, xs, xr, 0, (x, y, 1 - c))
        swap.start()
        for cc in range(2):
            @pl.when(c == cc)
            def _(cc=cc):
                _cast_rows(slab, stage, slice(cc * C, (cc + 1) * C))
        swap.wait()
        for cc in range(2):
            @pl.when(c == cc)
            def _(cc=cc):
                _cast_rows(slab, inbox, slice((1 - cc) * C, (2 - cc) * C))
        local(outs, scr, k).start()
        for j, (cx, cy) in enumerate(chips):
            ici(outs, scr, j, (cx, cy, c), c, k).start()

    def finish(ins, outs, scr):
        send, recv = scr[5], scr[6]
        x, y, c, chips = _place()
        k, sib = 2 * x + y, (x, y, 1 - c)
        passed = []
        for j, (cx, cy) in enumerate(chips):
            w = win(outs[0], c, 2 * cx + cy)
            _remote(w, w, send, recv, j, sib).wait_recv()
            fwd = _remote(w, w, send, recv, 3 + j, sib)
            fwd.start()
            passed.append(fwd)
        for j, (cx, cy) in enumerate(chips):
            w = win(outs[0], 1 - c, 2 * cx + cy)
            _remote(w, w, send, recv, 3 + j, sib).wait_recv()
        for j, (cx, cy) in enumerate(chips):
            ici(outs, scr, j, (cx, cy, c), c, k).wait_send()
        for fwd in passed:
            fwd.wait_send()
        local(outs, scr, k).wait()

    return Side((shard,), (VMEM,), (jax.ShapeDtypeStruct((K, NDEV * C), BF16),),
                (pltpu.VMEM((K, C), BF16), pltpu.VMEM((K, C), BF16), pltpu.VMEM((K, w2), BF16),
                 _sems(1), _sems(1), _sems(6), _sems(6), _sems(1)), start, finish)


def copies_side(args, out_shape, n_copies, plan):
    def copies(ins, outs, scr):
        return [_remote(s_, d_, scr[0], scr[1], i, to) for i, (s_, d_, to) in enumerate(plan(ins, outs))]

    def start(ins, outs, scr):
        for cp in copies(ins, outs, scr):
            cp.start()

    def finish(ins, outs, scr):
        for cp in copies(ins, outs, scr):
            cp.wait()

    return Side(tuple(args), (ANY,) * len(args), tuple(out_shape), (_sems(n_copies), _sems(n_copies)), start, finish)


def rs_to_sibling(grads):
    out_shape = [jax.ShapeDtypeStruct((4,) + g.shape[1:] if kind == "rows" else (g.shape[0] // 2, g.shape[1]), F32)
                 for kind, g in grads]

    def plan(ins, outs):
        x, y, c, _ = _place()
        sib, res = (x, y, 1 - c), []
        for (kind, _), g, r in zip(grads, ins, outs):
            if kind == "rows":
                res += [(g.at[2 * k + 1 - c], r.at[k], sib) for k in range(4)]
            else:
                half = g.shape[0] // 2
                res.append((g.at[pl.ds(pl.multiple_of((1 - c) * half, 8), half), :], r, sib))
        return res

    return copies_side([g for _, g in grads], out_shape, sum(4 if kind == "rows" else 1 for kind, _ in grads), plan)


def rs_to_chips(parts):
    out_shape = [jax.ShapeDtypeStruct((3,) + p.shape[1:] if kind == "rows" else (3, p.shape[0], p.shape[1] // 4), BF16)
                 for kind, p in parts]

    def plan(ins, outs):
        x, y, c, chips = _place()
        res = []
        for (kind, _), p, r in zip(parts, ins, outs):
            for j, (cx, cy) in enumerate(chips):
                if kind == "rows":
                    src = p.at[2 * cx + cy]
                else:
                    w2 = p.shape[1] // 4
                    src = p.at[:, pl.ds(pl.multiple_of((2 * cx + cy) * w2, LANES), w2)]
                res.append((src, r.at[j], (cx, cy, c)))
        return res

    return copies_side([p for _, p in parts], out_shape, 3 * len(parts), plan)


def rs_swap_halves(theirs):
    def plan(ins, outs):
        x, y, c, _ = _place()
        return [(t, r, (x, y, 1 - c)) for t, r in zip(ins, outs)]

    return copies_side(theirs, [jax.ShapeDtypeStruct(t.shape, F32) for t in theirs], len(theirs), plan)


def _row_tiles(rows):
    return 2 if rows % 32 == 0 and rows >= 512 else 1


def chip_sum(name, grad, recv, c_idx, chip_idx):
    _, R, C = grad.shape
    nt = 1
    tr = R // nt

    def body(s_ref, g_ref, r_ref, p_ref, own_ref):
        k = pl.program_id(1)
        tot = g_ref[0] + r_ref[0]
        p_ref[0] = tot.astype(BF16)

        @pl.when(k == s_ref[1])
        def _():
            own_ref[...] = tot

    grid_spec = pltpu.PrefetchScalarGridSpec(
        num_scalar_prefetch=1, grid=(nt, 4),
        in_specs=[pl.BlockSpec((1, tr, C), lambda i, k, s: (2 * k + s[0], i, 0)),
                  pl.BlockSpec((1, tr, C), lambda i, k, s: (k, i, 0))],
        out_specs=[pl.BlockSpec((1, tr, C), lambda i, k, s: (k, i, 0)),
                   pl.BlockSpec((tr, C), lambda i, k, s: (i, 0))])
    return pl.pallas_call(
        body, name=name, grid_spec=grid_spec,
        out_shape=[jax.ShapeDtypeStruct((4, R, C), BF16), jax.ShapeDtypeStruct((R, C), F32)],
        compiler_params=_cp(dimension_semantics=("arbitrary", "arbitrary")),
    )(jnp.stack([c_idx, chip_idx]), grad, recv)


def _half_tiles(half):
    return 2 if half >= 512 else 1


def chip_sum_cols(name, grad, recv, c_idx, chip_idx):
    K, W = grad.shape
    half, w2 = K // 2, W // 4
    nt = _half_tiles(half)
    tr = half // nt

    def body(s_ref, g_ref, r_ref, p_ref, own_ref):
        tot = g_ref[...] + r_ref[...]
        p_ref[...] = tot.astype(BF16)

        @pl.when(pl.program_id(1) == s_ref[1])
        def _():
            own_ref[...] = tot

    grid_spec = pltpu.PrefetchScalarGridSpec(
        num_scalar_prefetch=1, grid=(nt, 4),
        in_specs=[pl.BlockSpec((tr, w2), lambda i, k, s: (s[0] * nt + i, k)),
                  pl.BlockSpec((tr, w2), lambda i, k, s: (i, k))],
        out_specs=[pl.BlockSpec((tr, w2), lambda i, k, s: (i, k)),
                   pl.BlockSpec((tr, w2), lambda i, k, s: (i, 0))])
    return pl.pallas_call(
        body, name=name, grid_spec=grid_spec,
        out_shape=[jax.ShapeDtypeStruct((half, W), BF16), jax.ShapeDtypeStruct((half, w2), F32)],
        compiler_params=_cp(dimension_semantics=("arbitrary", "arbitrary")),
    )(jnp.stack([c_idx, chip_idx]), grad, recv)


def col_final(name, own, recv, c_idx):
    half, w2 = own.shape
    C = w2 // 2
    nt = _half_tiles(half)
    tr = half // nt

    def body(s_ref, o_ref, r_ref, mine_ref, theirs_ref, t_ref):
        t_ref[...] = o_ref[...] + r_ref[0].astype(F32) + r_ref[1].astype(F32) + r_ref[2].astype(F32)
        for cc in range(2):
            @pl.when(s_ref[0] == cc)
            def _(cc=cc):
                mine_ref[...] = t_ref[:, cc * C:(cc + 1) * C]
                theirs_ref[...] = t_ref[:, (1 - cc) * C:(2 - cc) * C]

    grid_spec = pltpu.PrefetchScalarGridSpec(
        num_scalar_prefetch=1, grid=(nt,),
        in_specs=[pl.BlockSpec((tr, w2), lambda i, s: (i, 0)), pl.BlockSpec((3, tr, w2), lambda i, s: (0, i, 0))],
        out_specs=[pl.BlockSpec((tr, C), lambda i, s: (i, 0))] * 2,
        scratch_shapes=[pltpu.VMEM((tr, w2), F32)])
    return pl.pallas_call(
        body, name=name, grid_spec=grid_spec, out_shape=[jax.ShapeDtypeStruct((half, C), F32)] * 2,
        compiler_params=_cp(dimension_semantics=("arbitrary",)),
    )(jnp.stack([c_idx]), own, recv)


def _adamw(w, g, m, v):
    m2 = ADAM_B1 * m + (1.0 - ADAM_B1) * g
    v2 = ADAM_B2 * v + (1.0 - ADAM_B2) * (g * g)
    m_hat = m2 / (1.0 - ADAM_B1 ** ADAM_STEP)
    v_hat = v2 / (1.0 - ADAM_B2 ** ADAM_STEP)
    delta = -ADAM_LR * (m_hat / (jnp.sqrt(v_hat) + ADAM_EPS) + ADAM_WD * w)
    return delta, m2, v2


def shard_adam(name, owns, recvs, w, m, v):
    n = len(owns)
    R, Cp = owns[0].shape
    nt = _row_tiles(R)
    tr = R // nt

    def body(*refs):
        o_refs, r_refs = refs[:n], refs[n:2 * n]
        w_ref, m_ref, v_ref, g_ref, d_ref, nm_ref, nv_ref = refs[2 * n:]
        g = None
        for k in range(n):
            gk = o_refs[k][...] + r_refs[k][0].astype(F32) + r_refs[k][1].astype(F32) + r_refs[k][2].astype(F32)
            g = gk if g is None else jnp.where(pl.program_id(0) == k, gk, g)
        delta, m2, v2 = _adamw(w_ref[...], g, m_ref[...], v_ref[...])
        g_ref[...] = g
        d_ref[...] = delta
        nm_ref[...] = m2
        nv_ref[...] = v2

    part = pl.BlockSpec((tr, Cp), lambda k, i: (i, 0))
    part3 = pl.BlockSpec((3, tr, Cp), lambda k, i: (0, i, 0))
    tile = pl.BlockSpec((tr, Cp), lambda k, i: (i, k))
    return pl.pallas_call(
        body, name=name, grid=(n, nt),
        in_specs=[part] * n + [part3] * n + [tile, tile, tile],
        out_specs=[tile] * 4, out_shape=[jax.ShapeDtypeStruct((R, n * Cp), F32)] * 4,
        compiler_params=_cp(dimension_semantics=("arbitrary", "arbitrary")),
    )(*owns, *recvs, w, m, v)


def adam_cols(name, mine, recv, w, m, v, c_idx):
    half, C = mine.shape
    nt = _half_tiles(half)
    tr = half // nt

    def body(s_ref, a_ref, b_ref, w_ref, m_ref, v_ref, g_ref, d_ref, nm_ref, nv_ref):
        g = jnp.where(pl.program_id(0) == s_ref[0], a_ref[...], b_ref[...])
        delta, m2, v2 = _adamw(w_ref[...], g, m_ref[...], v_ref[...])
        g_ref[...] = g
        d_ref[...] = delta
        nm_ref[...] = m2
        nv_ref[...] = v2

    part = pl.BlockSpec((tr, C), lambda hh, i, s: (i, 0))
    tile = pl.BlockSpec((tr, C), lambda hh, i, s: (hh * nt + i, 0))
    grid_spec = pltpu.PrefetchScalarGridSpec(
        num_scalar_prefetch=1, grid=(2, nt), in_specs=[part, part, tile, tile, tile], out_specs=[tile] * 4)
    return pl.pallas_call(
        body, name=name, grid_spec=grid_spec, out_shape=[jax.ShapeDtypeStruct((2 * half, C), F32)] * 4,
        compiler_params=_cp(dimension_semantics=("arbitrary", "arbitrary")),
    )(jnp.stack([c_idx]), mine, recv, w, m, v)


ROW_N1, ROW_N2, ROW_BG, ROW_QN, ROW_KN, ROW_CB, ROW_LW, ROW_LB, ROW_CW = 0, 1, 2, 4, 5, 6, 7, 8, 9
PACK_ROWS = 40
SMALL = ("norm1_w", "norm2_w", "b_gate", "q_norm_w", "k_norm_w", "conv_b", "conv_ln_w", "conv_ln_b", "conv_w")


def small_sync_adam(g, w, m, v, sides=()):
    ns = len(SMALL)

    def body(*refs):
        gi = dict(zip(SMALL, refs[:ns]))
        wi = dict(zip(SMALL, refs[ns:2 * ns]))
        mi = dict(zip(SMALL, refs[2 * ns:3 * ns]))
        vi = dict(zip(SMALL, refs[3 * ns:4 * ns]))
        outs = refs[4 * ns:8 * ns]
        pack, recv, tot, send_sems, recv_sems = refs[8 * ns:]
        x, y, c, _ = _place()
        me = 4 * x + 2 * y + c

        pack[...] = jnp.zeros_like(pack)
        pack[ROW_N1:ROW_N1 + 1, :] = gi["norm1_w"][...]
        pack[ROW_N2:ROW_N2 + 1, :] = gi["norm2_w"][...]
        pack[ROW_BG:ROW_BG + 2, :] = gi["b_gate"][...]
        pack[ROW_QN:ROW_QN + 1, 0:HD] = gi["q_norm_w"][...]
        pack[ROW_KN:ROW_KN + 1, 0:HD] = gi["k_norm_w"][...]
        pack[ROW_CB:ROW_CB + 1, 0:CC] = gi["conv_b"][...]
        pack[ROW_LW:ROW_LW + 1, 0:CC] = gi["conv_ln_w"][...]
        pack[ROW_LB:ROW_LB + 1, 0:CC] = gi["conv_ln_b"][...]
        pack[ROW_CW:ROW_CW + KW, 0:CC] = gi["conv_w"][...]

        copies = []
        for k in range(1, NDEV):
            peer = (x ^ (k >> 2), y ^ ((k >> 1) & 1), c ^ (k & 1))
            cp = pltpu.make_async_remote_copy(
                src_ref=pack, dst_ref=recv.at[me], send_sem=send_sems.at[k - 1], recv_sem=recv_sems.at[k - 1],
                device_id=peer, device_id_type=MESH)
            cp.start()
            copies.append(cp)
        recv[me] = pack[...]
        for cp in copies:
            cp.wait()
        acc = recv[0]
        for p in range(1, NDEV):
            acc = acc + recv[p]
        tot[...] = acc

        def shard_grad(name):
            if name == "b_gate":
                return tot[ROW_BG:ROW_BG + 2, pl.ds(pl.multiple_of(me * LANES, LANES), LANES)]
            if name == "conv_w":
                win = tot[ROW_CW:ROW_CW + KW, pl.ds(pl.multiple_of((me // 2) * LANES, LANES), LANES)]
                return jnp.where(me % 2 == 1, win[:, HD:LANES], win[:, 0:HD])
            row = {"norm1_w": ROW_N1, "norm2_w": ROW_N2, "q_norm_w": ROW_QN, "k_norm_w": ROW_KN,
                   "conv_b": ROW_CB, "conv_ln_w": ROW_LW, "conv_ln_b": ROW_LB}[name]
            return tot[row:row + 1, 0:wi[name].shape[1]]

        for i, name in enumerate(SMALL):
            gr = shard_grad(name)
            delta, m2, v2 = _adamw(wi[name][...], gr, mi[name][...], vi[name][...])
            outs[4 * i][...] = gr
            outs[4 * i + 1][...] = delta
            outs[4 * i + 2][...] = m2
            outs[4 * i + 3][...] = v2

    out_shape = []
    for name in SMALL:
        out_shape += [jax.ShapeDtypeStruct(w[name].shape, F32)] * 4
    args = [g[k] for k in SMALL] + [w[k] for k in SMALL] + [m[k] for k in SMALL] + [v[k] for k in SMALL]
    res = _call(
        body, sides, name="small_sync_adam", grid=(1,), in_specs=[VMEM] * len(args),
        out_specs=[VMEM] * len(out_shape), out_shape=out_shape,
        scratch_shapes=[pltpu.VMEM((PACK_ROWS, D), F32), pltpu.VMEM((NDEV, PACK_ROWS, D), F32),
                        pltpu.VMEM((PACK_ROWS, D), F32), _sems(NDEV - 1), _sems(NDEV - 1)],
        args=args)
    res, side_outs = res if sides else (res, None)
    out = {name: tuple(res[4 * i:4 * i + 4]) for i, name in enumerate(SMALL)}
    return (out, side_outs) if sides else out


MATS = ("w_in", "w_o_attn", "w_pw_conv", "w_out", "w_ffn_in", "w_ffn_out")
TRANSPOSED = ("w_in", "w_ffn_in")
WEIGHTS = ("norm1_w", "w_in", "b_gate", "q_norm_w", "k_norm_w", "w_o_attn", "conv_w", "conv_b", "conv_ln_w",
           "conv_ln_b", "w_pw_conv", "w_out", "norm2_w", "w_ffn_in", "w_ffn_out")


def _blocks_to_cols(blocks):
    n, R, C = blocks.shape
    return blocks.transpose(1, 0, 2).reshape(R, n * C)


def kernel(x, positions, norm1_w, w_in, b_gate, q_norm_w, k_norm_w, w_o_attn, conv_w, conv_b, conv_ln_w, conv_ln_b, w_pw_conv, w_out, norm2_w, w_ffn_in, w_ffn_out, loss_target, m_norm1_w, m_w_in, m_b_gate, m_q_norm_w, m_k_norm_w, m_w_o_attn, m_conv_w, m_conv_b, m_conv_ln_w, m_conv_ln_b, m_w_pw_conv, m_w_out, m_norm2_w, m_w_ffn_in, m_w_ffn_out, v_norm1_w, v_w_in, v_b_gate, v_q_norm_w, v_k_norm_w, v_w_o_attn, v_conv_w, v_conv_b, v_conv_ln_w, v_conv_ln_b, v_w_pw_conv, v_w_out, v_norm2_w, v_w_ffn_in, v_w_ffn_out):
    w = dict(norm1_w=norm1_w, w_in=w_in, b_gate=b_gate, q_norm_w=q_norm_w, k_norm_w=k_norm_w, w_o_attn=w_o_attn,
             conv_w=conv_w, conv_b=conv_b, conv_ln_w=conv_ln_w, conv_ln_b=conv_ln_b, w_pw_conv=w_pw_conv,
             w_out=w_out, norm2_w=norm2_w, w_ffn_in=w_ffn_in, w_ffn_out=w_ffn_out)
    m = dict(norm1_w=m_norm1_w, w_in=m_w_in, b_gate=m_b_gate, q_norm_w=m_q_norm_w, k_norm_w=m_k_norm_w,
             w_o_attn=m_w_o_attn, conv_w=m_conv_w, conv_b=m_conv_b, conv_ln_w=m_conv_ln_w,
             conv_ln_b=m_conv_ln_b, w_pw_conv=m_w_pw_conv, w_out=m_w_out, norm2_w=m_norm2_w,
             w_ffn_in=m_w_ffn_in, w_ffn_out=m_w_ffn_out)
    v = dict(norm1_w=v_norm1_w, w_in=v_w_in, b_gate=v_b_gate, q_norm_w=v_q_norm_w, k_norm_w=v_k_norm_w,
             w_o_attn=v_w_o_attn, conv_w=v_conv_w, conv_b=v_conv_b, conv_ln_w=v_conv_ln_w,
             conv_ln_b=v_conv_ln_b, w_pw_conv=v_w_pw_conv, w_out=v_w_out, norm2_w=v_norm2_w,
             w_ffn_in=v_w_ffn_in, w_ffn_out=v_w_ffn_out)
    def two_d(t):
        t = {k: (a[0] if a.ndim == 3 else a) for k, a in t.items()}
        return {k: (a.T if k in TRANSPOSED else a) for k, a in t.items()}

    w, m, v = two_d(w), two_d(m), two_d(v)

    x2, target = x[0], loss_target[0]
    c_idx = lax.axis_index("c").astype(jnp.int32)
    chip_idx = (2 * lax.axis_index("x") + lax.axis_index("y")).astype(jnp.int32)
    tabs = rope_tables(positions.reshape(S, 1))
    qw2 = jnp.tile(w["q_norm_w"], (1, 2))
    kw2 = jnp.tile(w["k_norm_w"], (1, 2))

    (w_in_blocks,), (bg_blocks,), (cw_blocks,) = comm_only(
        "gather_first", (ag_blocks_relay(w["w_in"], BF16), ag_blocks(w["b_gate"], F32), ag_blocks(w["conv_w"], F32)))
    w_in_t = w_in_blocks.reshape(INW, D)
    b_gate_f, conv_w_f = _blocks_to_cols(bg_blocks), _blocks_to_cols(cw_blocks)
    (h, proj), ((w_o_f,), (w_pw_f,), (w_out_blocks,)) = in_proj(
        x2, w["norm1_w"], w_in_t, sides=(ag_cols(w["w_o_attn"]), ag_cols(w["w_pw_conv"]), ag_blocks_relay(w["w_out"], BF16)))
    w_out_f = w_out_blocks.reshape(D, D)
    (attn, lse), ((w_ffn_in_blocks,),) = attn_fwd(proj, tabs, qw2, kw2, sides=(ag_blocks_relay(w["w_ffn_in"], BF16),))
    w_ffn_in_t = w_ffn_in_blocks.reshape(2 * FF, D)
    cpre, u3 = conv_fwd(proj, conv_w_f, w["conv_b"], w["conv_ln_w"], w["conv_ln_b"])
    x1, z, ya, yb = mix_out(x2, proj, b_gate_f, attn, u3, w_o_f, w_pw_f, w_out_f)
    (h2, gu, f), ((w_ffn_out_blocks,),) = ffn_in(x1, w["norm2_w"], w_ffn_in_t, sides=(ag_blocks_relay(w["w_ffn_out"], BF16),))
    w_ffn_out_f = w_ffn_out_blocks.reshape(FF, D)
    dy, dyb, sq = ffn_out_loss(x1, f, w_ffn_out_f, target)
    loss = lax.psum((0.5 / D) * jnp.sum(sq), ("x", "y", "c"))

    g = {}
    g_ffn_out = mm_tn("gw_ffn_out", f, dyb, FF // 2, D).reshape(NDEV, FF // NDEV, D)
    (d_gu, d_x1, d_x1b, g["norm2_w"]), ((ra_ffn_out,),) = ffn_bwd(
        dy, dyb, gu, x1, w["norm2_w"], w_ffn_in_t, w_ffn_out_f, sides=(rs_to_sibling([("rows", g_ffn_out)]),))
    pb_ffn_out, own_ffn_out = chip_sum("chip_sum_w_ffn_out", g_ffn_out, ra_ffn_out, c_idx, chip_idx)
    g_ffn_in, ((rb_ffn_out,),) = mm_tn("gw_ffn_in", d_gu, h2, FF // 2, D,
                                       sides=(rs_to_chips([("rows", pb_ffn_out)]),))
    g_ffn_in = g_ffn_in.reshape(NDEV, 2 * FF // NDEV, D)
    g_out = mm_tn("gw_out", z, d_x1b, D // 2, D).reshape(NDEV, D // NDEV, D)
    (d_ya, d_yb, d_gl, d_attn, d_u3, g["b_gate"]), ((ra_ffn_in,),) = out_bwd(
        d_x1b, proj, b_gate_f, ya, yb, w_o_f, w_pw_f, w_out_f, sides=(rs_to_sibling([("rows", g_ffn_in)]),))
    pb_ffn_in, own_ffn_in = chip_sum("chip_sum_w_ffn_in", g_ffn_in, ra_ffn_in, c_idx, chip_idx)
    g_w_o = mm_tn("gw_o_attn", attn, d_ya, CC, D)
    g_w_pw = mm_tn("gw_pw_conv", u3, d_yb, CC, D)
    (d_conv, g["conv_w"], g["conv_b"], g["conv_ln_w"], g["conv_ln_b"]), ((ra_out, ra_w_o, ra_w_pw),) = conv_bwd(
        proj, cpre, d_u3, conv_w_f, conv_w_f[::-1], w["conv_ln_w"], w["conv_ln_b"],
        sides=(rs_to_sibling([("rows", g_out), ("cols", g_w_o), ("cols", g_w_pw)]),))
    pb_out, own_out = chip_sum("chip_sum_w_out", g_out, ra_out, c_idx, chip_idx)
    pb_w_o, own_w_o = chip_sum_cols("chip_sum_w_o_attn", g_w_o, ra_w_o, c_idx, chip_idx)
    pb_w_pw, own_w_pw = chip_sum_cols("chip_sum_w_pw_conv", g_w_pw, ra_w_pw, c_idx, chip_idx)
    (d_q, d_k, d_v, gqw, gkw), ((rb_ffn_in, rb_out, rb_w_o, rb_w_pw),) = attn_bwd(
        proj, tabs, qw2, kw2, d_attn, attn, lse,
        sides=(rs_to_chips([("rows", pb_ffn_in), ("rows", pb_out), ("cols", pb_w_o), ("cols", pb_w_pw)]),))
    g["q_norm_w"] = gqw[0:1, 0:HD] + gqw[0:1, HD:LANES]
    g["k_norm_w"] = gkw[0:1, 0:HD] + gkw[0:1, HD:LANES]
    mine_w_o, theirs_w_o = col_final("col_final_w_o_attn", own_w_o, rb_w_o, c_idx)
    mine_w_pw, theirs_w_pw = col_final("col_final_w_pw_conv", own_w_pw, rb_w_pw, c_idx)
    d_segs = (d_q, d_k, d_v, d_conv, d_gl)
    g_w_in_a, ((rc_w_o, rc_w_pw),) = gw_in_t("gw_in_a", h, d_segs, 0,
                                             sides=(rs_swap_halves([theirs_w_o, theirs_w_pw]),))
    g_w_in_a = g_w_in_a.reshape(NDEV, INW // NDEV, D // 2)
    g_w_in_b, ((ra_w_in_a,),) = gw_in_t("gw_in_b", h, d_segs, 1, sides=(rs_to_sibling([("rows", g_w_in_a)]),))
    g_w_in_b = g_w_in_b.reshape(NDEV, INW // NDEV, D // 2)
    pb_w_in_a, own_w_in_a = chip_sum("chip_sum_w_in_a", g_w_in_a, ra_w_in_a, c_idx, chip_idx)
    (grad_x, g["norm1_w"]), ((rb_w_in_a,), (ra_w_in_b,)) = in_bwd(
        d_q, d_k, d_v, d_conv, d_gl, w_in_t, x2, d_x1, w["norm1_w"],
        sides=(rs_to_chips([("rows", pb_w_in_a)]), rs_to_sibling([("rows", g_w_in_b)])))
    pb_w_in_b, own_w_in_b = chip_sum("chip_sum_w_in_b", g_w_in_b, ra_w_in_b, c_idx, chip_idx)
    small, ((rb_w_in_b,),) = small_sync_adam(g, w, m, v, sides=(rs_to_chips([("rows", pb_w_in_b)]),))

    res = {
        "w_in": shard_adam("adam_w_in", [own_w_in_a, own_w_in_b], [rb_w_in_a, rb_w_in_b],
                           w["w_in"], m["w_in"], v["w_in"]),
        "w_ffn_in": shard_adam("adam_w_ffn_in", [own_ffn_in], [rb_ffn_in], w["w_ffn_in"], m["w_ffn_in"], v["w_ffn_in"]),
        "w_o_attn": adam_cols("adam_w_o_attn", mine_w_o, rc_w_o, w["w_o_attn"], m["w_o_attn"], v["w_o_attn"], c_idx),
        "w_pw_conv": adam_cols("adam_w_pw_conv", mine_w_pw, rc_w_pw, w["w_pw_conv"], m["w_pw_conv"], v["w_pw_conv"], c_idx),
        "w_out": shard_adam("adam_w_out", [own_out], [rb_out], w["w_out"], m["w_out"], v["w_out"]),
        "w_ffn_out": shard_adam("adam_w_ffn_out", [own_ffn_out], [rb_ffn_out],
                                w["w_ffn_out"], m["w_ffn_out"], v["w_ffn_out"]),
    }
    res = {k: tuple(a.T if k in TRANSPOSED else a for a in r) for k, r in res.items()}
    res.update(small)

    def shaped(name, a):
        return a.reshape((1,) + a.shape) if name in MATS or name in ("b_gate", "conv_w") else a

    outs = [loss, grad_x.reshape(1, S, D)]
    for i in range(4):
        outs += [shaped(k, res[k][i]) for k in WEIGHTS]
    return tuple(outs)
```

```python
import functools
from typing import Callable, NamedTuple, Optional

import numpy as np
import jax
import jax.numpy as jnp
from jax import lax
from jax.experimental import pallas as pl
from jax.experimental.pallas import tpu as pltpu

F32 = jnp.float32
BF16 = jnp.bfloat16

S = 2048
D = 1024
HD = 64
QKV = 1536
CC = 512
KW = 31
FF = 2816
INW = 7680
OFF_Q, OFF_K, OFF_V, OFF_CA, OFF_CB, OFF_GA, OFF_GB = 0, 1536, 3072, 4608, 5120, 5632, 6656
DILATIONS = (1, 4, 16)
HALF_SPAN = 64
EPS = 1e-6
NEG_INF = -1e30
ROPE_THETA = 500000.0
ROT_DIM = 16

ADAM_LR = 0.001
ADAM_B1 = 0.9
ADAM_B2 = 0.999
ADAM_EPS = 1e-08
ADAM_WD = 0.01
ADAM_STEP = 10

NDEV = 8
LANES = 128
TM = 256
TQ = 128
VMEM_LIMIT = 56 * 1024 * 1024
MESH = pl.DeviceIdType.MESH


def _cp(**kw):
    return pltpu.CompilerParams(vmem_limit_bytes=VMEM_LIMIT, **kw)


def _row(width, col=0, tm=TM):
    return pl.BlockSpec((tm, width), lambda i: (i, col))


def _res(shape):
    nd = len(shape)
    return pl.BlockSpec(shape, lambda *_: (0,) * nd, pipeline_mode=pl.Buffered(1))


def _dot(a, b):
    return jnp.dot(a, b, preferred_element_type=F32)


def _dot_nt(a, b):
    return lax.dot_general(a, b, (((1,), (1,)), ((), ())), preferred_element_type=F32)


def _dot_tn(a, b):
    return lax.dot_general(a, b, (((0,), (0,)), ((), ())), preferred_element_type=F32)


def _sigmoid(x):
    return jax.nn.sigmoid(x)


def _dsilu(x, sg):
    return sg * (1.0 + x * (1.0 - sg))


ANY = pl.BlockSpec(memory_space=pl.ANY)
VMEM = pl.BlockSpec(memory_space=pltpu.VMEM)


class Side(NamedTuple):
    args: tuple
    in_specs: tuple
    out_shape: tuple
    scratch: tuple
    start: Callable
    finish: Callable
    mid: Optional[Callable] = None


def _call(body, sides=(), *, name, grid, in_specs, out_specs, out_shape, scratch_shapes=(), args):
    ni, no, ns = len(in_specs), len(out_specs), len(scratch_shapes)
    cnt = [(len(s.args), len(s.out_shape), len(s.scratch)) for s in sides]

    def take(refs, pos, n):
        return refs[pos:pos + n], pos + n

    def full(*refs):
        m_in, pos = take(refs, 0, ni)
        s_in = []
        for a, _, _ in cnt:
            r, pos = take(refs, pos, a)
            s_in.append(r)
        m_out, pos = take(refs, pos, no)
        s_out = []
        for _, o, _ in cnt:
            r, pos = take(refs, pos, o)
            s_out.append(r)
        m_scr, pos = take(refs, pos, ns)
        s_scr = []
        for _, _, c in cnt:
            r, pos = take(refs, pos, c)
            s_scr.append(r)
        if sides:
            first = functools.reduce(jnp.logical_and, [pl.program_id(d) == 0 for d in range(len(grid))])
            last = functools.reduce(jnp.logical_and, [pl.program_id(d) == g - 1 for d, g in enumerate(grid)])

            @pl.when(first)
            def _():
                for s, a, o, c in zip(sides, s_in, s_out, s_scr):
                    s.start(a, o, c)

            steps = int(np.prod(grid))
            mid_step = (2 * steps) // 3
            if steps > 1 and any(s.mid is not None for s in sides):
                step = functools.reduce(lambda acc, d: acc * grid[d] + pl.program_id(d), range(len(grid)), 0)

                @pl.when(step == mid_step)
                def _():
                    for s, a, o, c in zip(sides, s_in, s_out, s_scr):
                        if s.mid is not None:
                            s.mid(a, o, c)

        body(*m_in, *m_out, *m_scr)
        if sides:
            @pl.when(last)
            def _():
                for s, a, o, c in zip(sides, s_in, s_out, s_scr):
                    if s.mid is not None and steps == 1:
                        s.mid(a, o, c)
                    s.finish(a, o, c)

    res = pl.pallas_call(
        full, name=name, grid=grid,
        in_specs=list(in_specs) + [sp for s in sides for sp in s.in_specs],
        out_specs=list(out_specs) + [ANY for s in sides for _ in s.out_shape],
        out_shape=list(out_shape) + [o for s in sides for o in s.out_shape],
        scratch_shapes=list(scratch_shapes) + [c for s in sides for c in s.scratch],
        compiler_params=_cp(dimension_semantics=("arbitrary",) * len(grid)),
    )(*args, *[a for s in sides for a in s.args])
    res = list(res)
    if not sides:
        return res
    outs, pos = take(res, 0, no)
    side_outs = []
    for _, o, _ in cnt:
        r, pos = take(res, pos, o)
        side_outs.append(r)
    return outs, side_outs


def _inv_freq_lanes():
    inv = np.float32(ROPE_THETA) ** (-np.arange(0, ROT_DIM, 2, dtype=np.float32) / np.float32(ROT_DIM))
    lane = np.arange(LANES) % HD
    out = np.where(lane < ROT_DIM, inv[lane % (ROT_DIM // 2)], 0.0).astype(np.float32)
    return jnp.asarray(out.reshape(1, LANES))


def rope_tables(pos_col):
    def body(p_ref, f_ref, c_ref, s1_ref, s2_ref):
        ang = p_ref[...].astype(F32) * f_ref[...]
        lane = lax.broadcasted_iota(jnp.int32, ang.shape, 1) % HD
        cs = jnp.cos(ang)
        sn = jnp.sin(ang)
        c_ref[...] = jnp.where(lane < ROT_DIM, cs, 1.0)
        s1_ref[...] = jnp.where(lane < ROT_DIM // 2, -sn, 0.0)
        s2_ref[...] = jnp.where(lane < ROT_DIM // 2, 0.0, jnp.where(lane < ROT_DIM, sn, 0.0))

    sds = jax.ShapeDtypeStruct((S, LANES), F32)
    return pl.pallas_call(
        body, name="rope_tables", grid=(S // TM,),
        in_specs=[_row(1), pl.BlockSpec((1, LANES), lambda i: (0, 0))],
        out_specs=[_row(LANES)] * 3, out_shape=[sds] * 3,
    )(pos_col, _inv_freq_lanes())


def _rope(v, c, s1, s2):
    return v * c + pltpu.roll(v, LANES - 8, axis=1) * s1 + pltpu.roll(v, 8, axis=1) * s2


def _rope_t(d, c, s1, s2):
    return d * c - pltpu.roll(d, LANES - 8, axis=1) * s1 - pltpu.roll(d, 8, axis=1) * s2


def _head_mat():
    r = lax.broadcasted_iota(jnp.int32, (LANES, LANES), 0) // HD
    c = lax.broadcasted_iota(jnp.int32, (LANES, LANES), 1) // HD
    return jnp.where(r == c, 1.0 / HD, 0.0).astype(BF16)


def _head_mean(t, e):
    hi = t.astype(BF16)
    rest = (t - hi.astype(F32)).astype(BF16)
    return _dot(hi, e) + _dot(rest, e)


def in_proj(x, norm_w, w_in, sides=()):
    nchunk = 5
    cw = INW // nchunk

    def body(x_ref, nw_ref, w_ref, h_ref, p_ref):
        xv = x_ref[...]
        r = lax.rsqrt(jnp.mean(xv * xv, axis=-1, keepdims=True) + EPS)
        h = (xv * r * nw_ref[...]).astype(BF16)
        h_ref[...] = h
        for j in range(nchunk):
            p_ref[:, j * cw:(j + 1) * cw] = _dot_nt(h, w_ref[j * cw:(j + 1) * cw, :])

    return _call(
        body, sides, name="in_proj", grid=(S // TM,),
        in_specs=[_row(D), _res((1, D)), _res((INW, D))],
        out_specs=[_row(D), _row(INW)],
        out_shape=[jax.ShapeDtypeStruct((S, D), BF16), jax.ShapeDtypeStruct((S, INW), F32)],
        args=(x, norm_w, w_in))


def _qk_specs():
    nb = QKV // LANES
    return [pl.BlockSpec((S, LANES), functools.partial(lambda hp, g, o: (0, o + g * 4 + hp), o=o))
            for o in (OFF_Q // LANES, OFF_K // LANES, OFF_V // LANES)]


def _tab_specs():
    return [pl.BlockSpec((S, LANES), lambda hp, g: (0, 0), pipeline_mode=pl.Buffered(1))] * 3


def _vec_spec():
    return pl.BlockSpec((1, LANES), lambda hp, g: (0, 0))


def _sub_rows(r, d, start, n):
    if d == 1:
        return pl.ds(start, n)
    return pl.ds(r + d * start, n, stride=d)


def _band_window(i, L):
    W = min(TQ + 2 * HALF_SPAN, L)
    q0 = pl.multiple_of(i * TQ, TQ)
    k0 = pl.multiple_of(jnp.clip(q0 - HALF_SPAN, 0, L - W), HALF_SPAN)
    qpos = q0 + (lax.broadcasted_iota(jnp.int32, (2 * TQ, W), 0) & (TQ - 1))
    kpos = k0 + lax.broadcasted_iota(jnp.int32, (2 * TQ, W), 1)
    valid = jnp.abs(qpos - kpos) <= HALF_SPAN
    return W, q0, k0, valid


def _stack_heads(t, lo):
    z = jnp.zeros_like(t)
    return jnp.concatenate([jnp.where(lo, t, z), jnp.where(lo, z, t)], axis=0)


def _unstack_heads(t2, lo):
    return jnp.where(lo, t2[0:TQ], t2[TQ:2 * TQ])


CHAINS = 4


def _interleave(d):
    ru = min(d, CHAINS)
    return ru, min(CHAINS // ru, S // d // TQ)


def _for_blocks(n, fn):
    if n == 1:
        fn(0)
    else:
        def it(j, _):
            fn(j)
            return 0
        lax.fori_loop(0, n, it, 0)


def attn_fwd(proj, tabs, qw2, kw2, sides=()):
    CH = 256

    def body(q_ref, k_ref, v_ref, c_ref, s1_ref, s2_ref, qw_ref, kw_ref, at_ref, ls_ref,
             qs, ks, vs, osub, lsub, onat, lnat, qn, kn):
        g = pl.program_id(1)
        lo = lax.broadcasted_iota(jnp.int32, (1, LANES), 1) < HD
        e = _head_mat()

        def prep(i, _):
            rows = pl.ds(pl.multiple_of(i * CH, CH), CH)
            c, s1, s2 = c_ref[rows, :], s1_ref[rows, :], s2_ref[rows, :]
            for t_ref, w_ref, out, scale in ((q_ref, qw_ref, qn, HD ** -0.5), (k_ref, kw_ref, kn, 1.0)):
                t = t_ref[rows, :]
                r = lax.rsqrt(_head_mean(t * t, e) + EPS)
                out[rows, :] = _rope(t * r * w_ref[...], c, s1, s2) * scale
            return 0

        lax.fori_loop(0, S // CH, prep, 0, unroll=4)

        def group(gi, d):
            L = S // d

            ru, nb = _interleave(d)

            def stage(r, off):
                for c0 in range(0, L, CH):
                    n = min(CH, L)
                    rows = _sub_rows(r, d, c0, n)
                    dst = pl.ds(off + c0, n)
                    qs[dst, :] = qn[rows, :].astype(BF16)
                    ks[dst, :] = kn[rows, :].astype(BF16)
                    vs[dst, :] = v_ref[rows, :].astype(BF16)

            def one(off, i):
                W, q0, k0, valid = _band_window(i, L)
                q2 = _stack_heads(qs[pl.ds(off + q0, TQ), :], lo)
                sc = jnp.where(valid, _dot_nt(q2, ks[pl.ds(off + k0, W), :]), NEG_INF)
                m = jnp.max(sc, axis=-1, keepdims=True)
                p = jnp.exp(sc - m)
                den = jnp.sum(p, axis=-1, keepdims=True)
                o2 = _dot(p.astype(BF16), vs[pl.ds(off + k0, W), :]) / den
                l2 = jnp.broadcast_to(m + jnp.log(den), (2 * TQ, LANES))
                osub[pl.ds(off + q0, TQ), :] = _unstack_heads(o2, lo)
                lsub[pl.ds(off + q0, TQ), :] = _unstack_heads(l2, lo)

            def unstage(r, off):
                for c0 in range(0, L, CH):
                    n = min(CH, L)
                    rows = _sub_rows(r, d, c0, n)
                    onat[gi, rows, :] = osub[pl.ds(off + c0, n), :]
                    lnat[gi, rows, :] = lsub[pl.ds(off + c0, n), :]

            def step(t, _):
                for u in range(ru):
                    stage(t * ru + u, u * L)
                _for_blocks(L // TQ // nb, lambda j: [one(u * L, j * nb + b) for u in range(ru) for b in range(nb)])
                for u in range(ru):
                    unstage(t * ru + u, u * L)
                return 0

            lax.fori_loop(0, d // ru, step, 0)

        for gi, d in enumerate(DILATIONS):
            pl.when(g == gi)(functools.partial(group, gi, d))

        @pl.when(g == len(DILATIONS) - 1)
        def _():
            def mix(i, _):
                rows = pl.ds(pl.multiple_of(i * CH, CH), CH)
                l0, l1, l2 = lnat[0, rows, :], lnat[1, rows, :], lnat[2, rows, :]
                m = jnp.maximum(jnp.maximum(l0, l1), l2)
                e0, e1, e2 = jnp.exp(l0 - m), jnp.exp(l1 - m), jnp.exp(l2 - m)
                den = e0 + e1 + e2
                a = (e0 * onat[0, rows, :] + e1 * onat[1, rows, :] + e2 * onat[2, rows, :]) / den
                at_ref[rows, :] = a.astype(BF16)
                ls_ref[rows, :] = m + jnp.log(den)
                return 0

            lax.fori_loop(0, S // CH, mix, 0)

    out_spec = pl.BlockSpec((S, LANES), lambda hp, g: (0, hp))
    return _call(
        body, sides, name="attn_fwd", grid=(4, 3),
        in_specs=_qk_specs() + _tab_specs() + [_vec_spec(), _vec_spec()],
        out_specs=[out_spec, out_spec],
        out_shape=[jax.ShapeDtypeStruct((S, CC), BF16), jax.ShapeDtypeStruct((S, CC), F32)],
        scratch_shapes=[pltpu.VMEM((S, LANES), BF16)] * 3 + [pltpu.VMEM((S, LANES), F32)] * 2
        + [pltpu.VMEM((3, S, LANES), F32)] * 2 + [pltpu.VMEM((S, LANES), F32)] * 2,
        args=(proj, proj, proj, *tabs, qw2, kw2))


def attn_bwd(proj, tabs, qw2, kw2, d_attn, attn, lse, sides=()):
    CH = 256

    def body(q_ref, k_ref, v_ref, c_ref, s1_ref, s2_ref, qw_ref, kw_ref, do_ref, at_ref, ls_ref,
             dq_ref, dk_ref, dv_ref, gqw_ref, gkw_ref,
             qs, ks, vs, dos, dsub, lsub, dqs, dks, dvs, dnat, qx, kx, dvn, tnq, tnk, rrq, rrk):
        hp, g = pl.program_id(0), pl.program_id(1)
        lo = lax.broadcasted_iota(jnp.int32, (1, LANES), 1) < HD
        e = _head_mat()
        both = ((q_ref, qw_ref, qx, tnq, rrq, HD ** -0.5), (k_ref, kw_ref, kx, tnk, rrk, 1.0))

        @pl.when((hp == 0) & (g == 0))
        def _():
            gqw_ref[...] = jnp.zeros_like(gqw_ref)
            gkw_ref[...] = jnp.zeros_like(gkw_ref)

        def prep(i, _):
            rows = pl.ds(pl.multiple_of(i * CH, CH), CH)
            dnat[rows, :] = _head_mean(do_ref[rows, :] * at_ref[rows, :].astype(F32), e) * float(HD)
            c, s1, s2 = c_ref[rows, :], s1_ref[rows, :], s2_ref[rows, :]
            for t_ref, w_ref, x, tn_s, rr_s, scale in both:
                t = t_ref[rows, :]
                rr = lax.rsqrt(_head_mean(t * t, e) + EPS)
                tn = t * rr
                rr_s[rows, :] = rr
                tn_s[rows, :] = tn
                x[rows, :] = _rope(tn * w_ref[...], c, s1, s2) * scale
            return 0

        lax.fori_loop(0, S // CH, prep, 0, unroll=4)

        def group(d):
            L = S // d

            ru, nb = _interleave(d)

            def stage(r, off):
                for c0 in range(0, L, CH):
                    n = min(CH, L)
                    rows = _sub_rows(r, d, c0, n)
                    dst = pl.ds(off + c0, n)
                    qs[dst, :] = qx[rows, :].astype(BF16)
                    ks[dst, :] = kx[rows, :].astype(BF16)
                    vs[dst, :] = v_ref[rows, :].astype(BF16)
                    dos[dst, :] = do_ref[rows, :].astype(BF16)
                    dsub[dst, :] = dnat[rows, :]
                    lsub[dst, :] = ls_ref[rows, :]
                    dks[dst, :] = jnp.zeros((n, LANES), F32)
                    dvs[dst, :] = jnp.zeros((n, LANES), F32)

            def one(off, i):
                W, q0, k0, valid = _band_window(i, L)
                qrows, krows = pl.ds(off + q0, TQ), pl.ds(off + k0, W)
                q2 = _stack_heads(qs[qrows, :], lo)
                do2 = _stack_heads(dos[qrows, :], lo)
                kk, vv = ks[krows, :], vs[krows, :]
                lse_b, dd_b = lsub[qrows, :], dsub[qrows, :]
                lse2 = jnp.concatenate([lse_b[:, 0:1], lse_b[:, HD:HD + 1]], axis=0)
                dd2 = jnp.concatenate([dd_b[:, 0:1], dd_b[:, HD:HD + 1]], axis=0)
                sc = jnp.where(valid, _dot_nt(q2, kk), NEG_INF)
                p = jnp.exp(sc - lse2)
                ds = (p * (_dot_nt(do2, vv) - dd2)).astype(BF16)
                dqs[qrows, :] = _unstack_heads(_dot(ds, kk), lo)
                dks[krows, :] = dks[krows, :] + _dot_tn(ds, q2)
                dvs[krows, :] = dvs[krows, :] + _dot_tn(p.astype(BF16), do2)

            def unstage(r, off):
                for c0 in range(0, L, CH):
                    n = min(CH, L)
                    rows = _sub_rows(r, d, c0, n)
                    src = pl.ds(off + c0, n)
                    qx[rows, :] = dqs[src, :]
                    kx[rows, :] = dks[src, :]
                    dvn[rows, :] = dvs[src, :]

            def step(t, _):
                for u in range(ru):
                    stage(t * ru + u, u * L)
                _for_blocks(L // TQ // nb, lambda j: [one(u * L, j * nb + b) for u in range(ru) for b in range(nb)])
                for u in range(ru):
                    unstage(t * ru + u, u * L)
                return 0

            lax.fori_loop(0, d // ru, step, 0)

        for gi, d in enumerate(DILATIONS):
            pl.when(g == gi)(functools.partial(group, d))

        def emit(i, _):
            rows = pl.ds(pl.multiple_of(i * CH, CH), CH)
            c, s1, s2 = c_ref[rows, :], s1_ref[rows, :], s2_ref[rows, :]
            for (_, w_ref, x, tn_s, rr_s, scale), out, gw_ref in zip(both, (dq_ref, dk_ref), (gqw_ref, gkw_ref)):
                tn = tn_s[rows, :]
                dy = _rope_t(x[rows, :] * scale, c, s1, s2)
                gw_ref[0:1, :] = gw_ref[0:1, :] + jnp.sum(dy * tn, axis=0, keepdims=True)
                dtn = dy * w_ref[...]
                out[rows, :] = (rr_s[rows, :] * (dtn - tn * _head_mean(dtn * tn, e))).astype(BF16)
            dv_ref[rows, :] = dvn[rows, :].astype(BF16)
            return 0

        lax.fori_loop(0, S // CH, emit, 0, unroll=4)

    nat_spec = pl.BlockSpec((S, LANES), lambda hp, g: (0, hp))
    out_spec = pl.BlockSpec((S, LANES), lambda hp, g: (0, g * 4 + hp))
    acc_spec = pl.BlockSpec((8, LANES), lambda hp, g: (0, 0))
    return _call(
        body, sides, name="attn_bwd", grid=(4, 3),
        in_specs=_qk_specs() + _tab_specs() + [_vec_spec(), _vec_spec(), nat_spec, nat_spec, nat_spec],
        out_specs=[out_spec] * 3 + [acc_spec] * 2,
        out_shape=[jax.ShapeDtypeStruct((S, QKV), BF16)] * 3 + [jax.ShapeDtypeStruct((8, LANES), F32)] * 2,
        scratch_shapes=[pltpu.VMEM((S, LANES), BF16)] * 4 + [pltpu.VMEM((S, LANES), F32)] * 13,
        args=(proj, proj, proj, *tabs, qw2, kw2, d_attn, attn, lse))


PADR = 16
CT = 128


def _conv_specs():
    return [pl.BlockSpec((S, CC), lambda i: (0, OFF_CA // CC)), pl.BlockSpec((S, CC), lambda i: (0, OFF_CB // CC))]


NCB = CC // LANES


def _pad_zero(pad):
    for cb in range(NCB):
        pad[cb, 0:PADR, :] = jnp.zeros((PADR, LANES), F32)
        pad[cb, PADR + S:PADR + S + PADR, :] = jnp.zeros((PADR, LANES), F32)


def _pad_store(pad, row0, n, val):
    for cb in range(NCB):
        pad[cb, pl.ds(pl.multiple_of(row0 + PADR, 8), n), :] = val[:, cb * LANES:(cb + 1) * LANES]


def _taps(pad_ref, cb, s0, weights):
    acc = jnp.zeros((CT, LANES), F32)
    for k in range(KW):
        acc = acc + weights[k] * pad_ref[cb, pl.ds(s0 + k + 1, CT), :]
    return acc


def conv_fwd(proj, conv_w, conv_b, ln_w, ln_b):
    def body(a_ref, b_ref, w_ref, cb_ref, lw_ref, lb_ref, c_ref, u3_ref, upad):
        _pad_zero(upad)

        def glu(i, _):
            rows = pl.ds(pl.multiple_of(i * TM, TM), TM)
            _pad_store(upad, i * TM, TM, a_ref[rows, :] * _sigmoid(b_ref[rows, :]))
            return 0

        lax.fori_loop(0, S // TM, glu, 0)

        def chunk(i, _):
            s0 = pl.multiple_of(i * CT, CT)
            for cb in range(CC // LANES):
                cols = slice(cb * LANES, (cb + 1) * LANES)
                w = [w_ref[k:k + 1, cols] for k in range(KW)]
                c_ref[pl.ds(s0, CT), cols] = _taps(upad, cb, s0, w) + cb_ref[:, cols]
            cv = c_ref[pl.ds(s0, CT), :]
            mu = jnp.mean(cv, axis=-1, keepdims=True)
            xc = cv - mu
            rstd = lax.rsqrt(jnp.mean(xc * xc, axis=-1, keepdims=True) + EPS)
            yl = xc * rstd * lw_ref[...] + lb_ref[...]
            u3_ref[pl.ds(s0, CT), :] = (yl * _sigmoid(yl)).astype(BF16)
            return 0

        lax.fori_loop(0, S // CT, chunk, 0)

    vec = pl.BlockSpec((1, CC), lambda i: (0, 0))
    full = pl.BlockSpec((S, CC), lambda i: (0, 0))
    return pl.pallas_call(
        body, name="conv_fwd", grid=(1,),
        in_specs=_conv_specs() + [pl.BlockSpec((KW, CC), lambda i: (0, 0)), vec, vec, vec],
        out_specs=[full, full],
        out_shape=[jax.ShapeDtypeStruct((S, CC), F32), jax.ShapeDtypeStruct((S, CC), BF16)],
        scratch_shapes=[pltpu.VMEM((NCB, S + 2 * PADR, LANES), F32)],
        compiler_params=_cp(dimension_semantics=("arbitrary",)),
    )(proj, proj, conv_w, conv_b, ln_w, ln_b)


def conv_bwd(proj, cpre, d_u3, conv_w, conv_w_rev, ln_w, ln_b, sides=()):
    def body(a_ref, b_ref, c_ref, du3_ref, w_ref, wr_ref, lw_ref, lb_ref,
             dc_ref, gw_ref, gcb_ref, glw_ref, glb_ref, upad, dpad):
        _pad_zero(upad)
        _pad_zero(dpad)
        gw_ref[...] = jnp.zeros_like(gw_ref)

        def ln_bwd(i, carry):
            gcb, glw, glb = carry
            rows = pl.ds(pl.multiple_of(i * TM, TM), TM)
            _pad_store(upad, i * TM, TM, a_ref[rows, :] * _sigmoid(b_ref[rows, :]))
            cv = c_ref[rows, :]
            mu = jnp.mean(cv, axis=-1, keepdims=True)
            xc = cv - mu
            rstd = lax.rsqrt(jnp.mean(xc * xc, axis=-1, keepdims=True) + EPS)
            xh = xc * rstd
            yl = xh * lw_ref[...] + lb_ref[...]
            dyl = du3_ref[rows, :] * _dsilu(yl, _sigmoid(yl))
            dxh = dyl * lw_ref[...]
            dcv = rstd * (dxh - jnp.mean(dxh, axis=-1, keepdims=True)
                          - xh * jnp.mean(dxh * xh, axis=-1, keepdims=True))
            _pad_store(dpad, i * TM, TM, dcv)
            return (gcb + jnp.sum(dcv, axis=0, keepdims=True),
                    glw + jnp.sum(dyl * xh, axis=0, keepdims=True),
                    glb + jnp.sum(dyl, axis=0, keepdims=True))

        z = jnp.zeros((1, CC), F32)
        gcb, glw, glb = lax.fori_loop(0, S // TM, ln_bwd, (z, z, z))
        gcb_ref[...] = gcb
        glw_ref[...] = glw
        glb_ref[...] = glb

        def chunk(i, _):
            s0 = pl.multiple_of(i * CT, CT)
            for cb in range(CC // LANES):
                cols = slice(cb * LANES, (cb + 1) * LANES)
                wr = [wr_ref[k:k + 1, cols] for k in range(KW)]
                du = _taps(dpad, cb, s0, wr)
                dcv = dpad[cb, pl.ds(s0 + PADR, CT), :]
                for k in range(KW):
                    gw_ref[k:k + 1, cols] = gw_ref[k:k + 1, cols] + jnp.sum(
                        upad[cb, pl.ds(s0 + k + 1, CT), :] * dcv, axis=0, keepdims=True)
                av = a_ref[pl.ds(s0, CT), cols]
                sb = _sigmoid(b_ref[pl.ds(s0, CT), cols])
                dc_ref[pl.ds(s0, CT), cols] = (du * sb).astype(BF16)
                dc_ref[pl.ds(s0, CT), slice(CC + cb * LANES, CC + (cb + 1) * LANES)] = (
                    du * av * sb * (1.0 - sb)).astype(BF16)
            return 0

        lax.fori_loop(0, S // CT, chunk, 0)

    vec = pl.BlockSpec((1, CC), lambda i: (0, 0))
    full = pl.BlockSpec((S, CC), lambda i: (0, 0))
    wsp = pl.BlockSpec((KW, CC), lambda i: (0, 0))
    return _call(
        body, sides, name="conv_bwd", grid=(1,),
        in_specs=_conv_specs() + [full, full, wsp, wsp, vec, vec],
        out_specs=[pl.BlockSpec((S, 2 * CC), lambda i: (0, 0)), wsp, vec, vec, vec],
        out_shape=[jax.ShapeDtypeStruct((S, 2 * CC), BF16), jax.ShapeDtypeStruct((KW, CC), F32)]
        + [jax.ShapeDtypeStruct((1, CC), F32)] * 3,
        scratch_shapes=[pltpu.VMEM((NCB, S + 2 * PADR, LANES), F32)] * 2,
        args=(proj, proj, cpre, d_u3, conv_w, conv_w_rev, ln_w, ln_b))


def _gate_specs():
    return [_row(CC, col=OFF_GA // CC + j) for j in range(4)]


def _gates(g_refs, bg_ref):
    ga = _sigmoid(jnp.concatenate([g_refs[0][...], g_refs[1][...]], axis=1) + bg_ref[0:1, :])
    gb = _sigmoid(jnp.concatenate([g_refs[2][...], g_refs[3][...]], axis=1) + bg_ref[1:2, :])
    return ga, gb


def mix_out(x, proj, b_gate, attn, u3, w_o, w_pw, w_out):
    def body(x_ref, g0, g1, g2, g3, bg_ref, at_ref, u3_ref, wo_ref, wp_ref, wout_ref,
             x1_ref, z_ref, ya_ref, yb_ref):
        ga, gb = _gates((g0, g1, g2, g3), bg_ref)
        ya = _dot(at_ref[...], wo_ref[...])
        yb = _dot(u3_ref[...], wp_ref[...])
        z = (ga * ya + gb * yb).astype(BF16)
        ya_ref[...] = ya.astype(BF16)
        yb_ref[...] = yb.astype(BF16)
        z_ref[...] = z
        x1_ref[...] = x_ref[...] + _dot(z, wout_ref[...])

    return pl.pallas_call(
        body, name="mix_out", grid=(S // TM,),
        in_specs=[_row(D)] + _gate_specs() + [_res((2, D)), _row(CC), _row(CC),
                                              _res((CC, D)), _res((CC, D)), _res((D, D))],
        out_specs=[_row(D)] * 4,
        out_shape=[jax.ShapeDtypeStruct((S, D), F32)] + [jax.ShapeDtypeStruct((S, D), BF16)] * 3,
        compiler_params=_cp(dimension_semantics=("arbitrary",)),
    )(x, proj, proj, proj, proj, b_gate, attn, u3, w_o, w_pw, w_out)


def out_bwd(d_x1b, proj, b_gate, ya, yb, w_o, w_pw, w_out, sides=()):
    def body(dx_ref, g0, g1, g2, g3, bg_ref, ya_ref, yb_ref, wo_ref, wp_ref, wout_ref,
             dya_ref, dyb_ref, dgl_ref, dat_ref, du3_ref, gbg_ref):
        @pl.when(pl.program_id(0) == 0)
        def _():
            gbg_ref[...] = jnp.zeros_like(gbg_ref)

        ga, gb = _gates((g0, g1, g2, g3), bg_ref)
        dz = _dot_nt(dx_ref[...], wout_ref[...])
        dya = (dz * ga).astype(BF16)
        dyb = (dz * gb).astype(BF16)
        dgla = dz * ya_ref[...].astype(F32) * ga * (1.0 - ga)
        dglb = dz * yb_ref[...].astype(F32) * gb * (1.0 - gb)
        dya_ref[...] = dya
        dyb_ref[...] = dyb
        dgl_ref[:, 0:D] = dgla.astype(BF16)
        dgl_ref[:, D:2 * D] = dglb.astype(BF16)
        gbg_ref[0:1, :] = gbg_ref[0:1, :] + jnp.sum(dgla, axis=0, keepdims=True)
        gbg_ref[1:2, :] = gbg_ref[1:2, :] + jnp.sum(dglb, axis=0, keepdims=True)
        dat_ref[...] = _dot_nt(dya, wo_ref[...])
        du3_ref[...] = _dot_nt(dyb, wp_ref[...])

    return _call(
        body, sides, name="out_bwd", grid=(S // TM,),
        in_specs=[_row(D)] + _gate_specs() + [_res((2, D)), _row(D), _row(D),
                                              _res((CC, D)), _res((CC, D)), _res((D, D))],
        out_specs=[_row(D), _row(D), _row(2 * D), _row(CC), _row(CC), pl.BlockSpec((2, D), lambda i: (0, 0))],
        out_shape=[jax.ShapeDtypeStruct((S, D), BF16)] * 2 + [jax.ShapeDtypeStruct((S, 2 * D), BF16)]
        + [jax.ShapeDtypeStruct((S, CC), F32)] * 2 + [jax.ShapeDtypeStruct((2, D), F32)],
        args=(d_x1b, proj, proj, proj, proj, b_gate, ya, yb, w_o, w_pw, w_out))


def ffn_in(x1, norm_w, w_ffn_in, sides=()):
    half = FF // 2

    def body(x_ref, nw_ref, w_ref, h_ref, gu_ref, f_ref):
        xv = x_ref[...]
        r = lax.rsqrt(jnp.mean(xv * xv, axis=-1, keepdims=True) + EPS)
        h = (xv * r * nw_ref[...]).astype(BF16)
        h_ref[...] = h
        for j in range(2):
            gt = _dot_nt(h, w_ref[j * half:(j + 1) * half, :])
            up = _dot_nt(h, w_ref[FF + j * half:FF + (j + 1) * half, :])
            gu_ref[:, j * half:(j + 1) * half] = gt.astype(BF16)
            gu_ref[:, FF + j * half:FF + (j + 1) * half] = up.astype(BF16)
            f_ref[:, j * half:(j + 1) * half] = (gt * _sigmoid(gt) * up).astype(BF16)

    return _call(
        body, sides, name="ffn_in", grid=(S // TM,),
        in_specs=[_row(D), _res((1, D)), _res((2 * FF, D))],
        out_specs=[_row(D), _row(2 * FF), _row(FF)],
        out_shape=[jax.ShapeDtypeStruct((S, D), BF16), jax.ShapeDtypeStruct((S, 2 * FF), BF16),
                   jax.ShapeDtypeStruct((S, FF), BF16)],
        args=(x1, norm_w, w_ffn_in))


def ffn_out_loss(x1, f, w_ffn_out, target):
    def body(x_ref, f_ref, w_ref, t_ref, dy_ref, dyb_ref, sq_ref):
        @pl.when(pl.program_id(0) == 0)
        def _():
            sq_ref[...] = jnp.zeros_like(sq_ref)

        diff = x_ref[...] + _dot(f_ref[...], w_ref[...]) - t_ref[...]
        dy = diff * (1.0 / D)
        dy_ref[...] = dy
        dyb_ref[...] = dy.astype(BF16)
        sq_ref[...] = sq_ref[...] + jnp.sum((diff * diff).reshape(TM // 8, 8, D), axis=0)

    return pl.pallas_call(
        body, name="ffn_out_loss", grid=(S // TM,),
        in_specs=[_row(D), _row(FF), _res((FF, D)), _row(D)],
        out_specs=[_row(D), _row(D), pl.BlockSpec((8, D), lambda i: (0, 0))],
        out_shape=[jax.ShapeDtypeStruct((S, D), F32), jax.ShapeDtypeStruct((S, D), BF16),
                   jax.ShapeDtypeStruct((8, D), F32)],
        compiler_params=_cp(dimension_semantics=("arbitrary",)),
    )(x1, f, w_ffn_out, target)


def _rms_bwd(xv, nw, dh):
    r = lax.rsqrt(jnp.mean(xv * xv, axis=-1, keepdims=True) + EPS)
    xn = xv * r
    dxn = dh * nw
    dx = r * (dxn - xn * jnp.mean(dxn * xn, axis=-1, keepdims=True))
    return dx, dh * xn


def ffn_bwd(dy, dyb, gu, x1, norm_w, w_ffn_in, w_ffn_out, sides=()):
    def body(dy_ref, dyb_ref, gu_ref, x_ref, nw_ref, wi_ref, wo_ref, dgu_ref, dx_ref, dxb_ref, gn_ref):
        @pl.when(pl.program_id(0) == 0)
        def _():
            gn_ref[...] = jnp.zeros_like(gn_ref)

        df = _dot_nt(dyb_ref[...], wo_ref[...])
        gt = gu_ref[:, 0:FF].astype(F32)
        up = gu_ref[:, FF:2 * FF].astype(F32)
        sg = _sigmoid(gt)
        dgt = (df * up * _dsilu(gt, sg)).astype(BF16)
        dup = (df * gt * sg).astype(BF16)
        dgu_ref[:, 0:FF] = dgt
        dgu_ref[:, FF:2 * FF] = dup
        dh = _dot(dgt, wi_ref[0:FF, :]) + _dot(dup, wi_ref[FF:2 * FF, :])
        dxn, gw = _rms_bwd(x_ref[...], nw_ref[...], dh)
        dx = dy_ref[...] + dxn
        dx_ref[...] = dx
        dxb_ref[...] = dx.astype(BF16)
        gn_ref[...] = gn_ref[...] + jnp.sum(gw, axis=0, keepdims=True)

    return _call(
        body, sides, name="ffn_bwd", grid=(S // TM,),
        in_specs=[_row(D), _row(D), _row(2 * FF), _row(D), _res((1, D)), _res((2 * FF, D)), _res((FF, D))],
        out_specs=[_row(2 * FF), _row(D), _row(D), pl.BlockSpec((1, D), lambda i: (0, 0))],
        out_shape=[jax.ShapeDtypeStruct((S, 2 * FF), BF16), jax.ShapeDtypeStruct((S, D), F32),
                   jax.ShapeDtypeStruct((S, D), BF16), jax.ShapeDtypeStruct((1, D), F32)],
        args=(dy, dyb, gu, x1, norm_w, w_ffn_in, w_ffn_out))


def in_bwd(d_q, d_k, d_v, d_conv, d_gl, w_in, x, d_x1, norm_w, sides=()):
    segs = ((OFF_Q, QKV), (OFF_K, QKV), (OFF_V, QKV), (OFF_CA, 2 * CC), (OFF_GA, 2 * D))

    def body(dq_ref, dk_ref, dv_ref, dc_ref, dg_ref, w_ref, x_ref, dx1_ref, nw_ref, gx_ref, gn_ref):
        @pl.when(pl.program_id(0) == 0)
        def _():
            gn_ref[...] = jnp.zeros_like(gn_ref)

        dh = jnp.zeros((TM, D), F32)
        for ref, (off, width) in zip((dq_ref, dk_ref, dv_ref, dc_ref, dg_ref), segs):
            dh = dh + _dot(ref[...], w_ref[off:off + width, :])
        dxn, gw = _rms_bwd(x_ref[...], nw_ref[...], dh)
        gx_ref[...] = dx1_ref[...] + dxn
        gn_ref[...] = gn_ref[...] + jnp.sum(gw, axis=0, keepdims=True)

    return _call(
        body, sides, name="in_bwd", grid=(S // TM,),
        in_specs=[_row(QKV)] * 3 + [_row(2 * CC), _row(2 * D), _res((INW, D)), _row(D), _row(D), _res((1, D))],
        out_specs=[_row(D), pl.BlockSpec((1, D), lambda i: (0, 0))],
        out_shape=[jax.ShapeDtypeStruct((S, D), F32), jax.ShapeDtypeStruct((1, D), F32)],
        args=(d_q, d_k, d_v, d_conv, d_gl, w_in, x, d_x1, norm_w))


def mm_tn(name, a, b, tm, tn, sides=()):
    M, N = a.shape[1], b.shape[1]

    def body(a_ref, b_ref, o_ref):
        o_ref[...] = _dot_tn(a_ref[...], b_ref[...])

    res = _call(
        body, sides, name=name, grid=(M // tm, N // tn),
        in_specs=[pl.BlockSpec((S, tm), lambda i, j: (0, i)), pl.BlockSpec((S, tn), lambda i, j: (0, j))],
        out_specs=[pl.BlockSpec((tm, tn), lambda i, j: (i, j))],
        out_shape=[jax.ShapeDtypeStruct((M, N), F32)],
        args=(a, b))
    return (res[0][0], res[1]) if sides else res[0]


GW_IN_TN = 512


def gw_in_t(name, h, d_segs, col_half, sides=()):
    tn, hw = GW_IN_TN, D // 2
    starts, t0 = [], 0
    for seg in d_segs:
        starts.append(t0)
        t0 += seg.shape[1] // tn
    ntiles = [seg.shape[1] // tn for seg in d_segs]

    def body(h_ref, *refs):
        a_refs, o_ref = refs[:-1], refs[-1]
        n = pl.program_id(0)
        for a_ref, st, nt in zip(a_refs, starts, ntiles):
            @pl.when((n >= st) & (n < st + nt))
            def _(a_ref=a_ref):
                o_ref[...] = _dot_tn(a_ref[...], h_ref[...])

    def seg_spec(st, nt):
        return pl.BlockSpec((S, tn), lambda n: (0, jnp.clip(n - st, 0, nt - 1)))

    res = _call(
        body, sides, name=name, grid=(INW // tn,),
        in_specs=[pl.BlockSpec((S, hw), lambda n: (0, col_half))] + [seg_spec(st, nt) for st, nt in zip(starts, ntiles)],
        out_specs=[pl.BlockSpec((tn, hw), lambda n: (n, 0))],
        out_shape=[jax.ShapeDtypeStruct((INW, hw), F32)],
        args=(h, *d_segs))
    return (res[0][0], res[1]) if sides else res[0]


def _place():
    x, y, c = lax.axis_index("x"), lax.axis_index("y"), lax.axis_index("c")
    chips = [(1 - x, y), (x, 1 - y), (1 - x, 1 - y)]
    return x, y, c, chips


def _sems(n):
    return pltpu.SemaphoreType.DMA((n,))


def _remote(src, dst, send, recv, k, to):
    return pltpu.make_async_remote_copy(src_ref=src, dst_ref=dst, send_sem=send.at[k], recv_sem=recv.at[k],
                                        device_id=to, device_id_type=MESH)


def _cast_rows(dst, src, cols=slice(None)):
    rows = src.shape[0]
    step = next((s for s in (128, 64, 32, 16) if rows % s == 0), rows)
    for r0 in range(0, rows, step):
        dst[r0:r0 + step, cols] = src[r0:r0 + step, :].astype(dst.dtype)


def comm_only(name, sides):
    def body():
        pass

    return _call(body, sides, name=name, grid=(1,), in_specs=[], out_specs=[], out_shape=[], args=())[1]


def ag_blocks(shard, dtype):
    R, W = shard.shape

    def copy(outs, scr, k, block, to, src=None):
        dst = outs[0].at[block]
        return _remote(dst if src is None else src, dst, scr[1], scr[2], k, to)

    def local(outs, scr, me):
        return pltpu.make_async_copy(scr[0], outs[0].at[me], scr[3].at[0])

    def start(ins, outs, scr):
        x, y, c, chips = _place()
        me = 4 * x + 2 * y + c
        _cast_rows(scr[0], ins[0])
        local(outs, scr, me).start()
        copy(outs, scr, 0, me, (x, y, 1 - c), src=scr[0]).start()
        for j, (cx, cy) in enumerate(chips):
            copy(outs, scr, 1 + j, me, (cx, cy, c), src=scr[0]).start()

    def finish(ins, outs, scr):
        x, y, c, chips = _place()
        me, sib = 4 * x + 2 * y + c, (x, y, 1 - c)
        passed = []
        for j, (cx, cy) in enumerate(chips):
            theirs = 4 * cx + 2 * cy + c
            copy(outs, scr, 1 + j, theirs, (x, y, c)).wait_recv()
            fwd = copy(outs, scr, 4 + j, theirs, sib)
            fwd.start()
            passed.append(fwd)
        copy(outs, scr, 0, 4 * x + 2 * y + 1 - c, (x, y, c)).wait_recv()
        for j, (cx, cy) in enumerate(chips):
            copy(outs, scr, 4 + j, 4 * cx + 2 * cy + 1 - c, (x, y, c)).wait_recv()
        copy(outs, scr, 0, me, sib, src=scr[0]).wait_send()
        for j, (cx, cy) in enumerate(chips):
            copy(outs, scr, 1 + j, me, (cx, cy, c), src=scr[0]).wait_send()
        for fwd in passed:
            fwd.wait_send()
        local(outs, scr, me).wait()

    return Side((shard,), (VMEM,), (jax.ShapeDtypeStruct((NDEV, R, W), dtype),),
                (pltpu.VMEM((R, W), dtype), _sems(7), _sems(7), _sems(1)), start, finish)


def ag_blocks_relay(shard, dtype):
    R, W = shard.shape
    half = R // 2

    def copy(outs, scr, k, block, to, src=None, rows=None):
        dst = outs[0].at[block] if rows is None else outs[0].at[block, pl.ds(rows * half, half), :]
        return _remote(dst if src is None else src, dst, scr[1], scr[2], k, to)

    def local(outs, scr, me):
        return pltpu.make_async_copy(scr[0], outs[0].at[me], scr[3].at[0])

    def own(outs, scr):
        x, y, c, _ = _place()
        me = 4 * x + 2 * y + c
        return [copy(outs, scr, k, me, to, src=scr[0])
                for k, to in enumerate([(x, y, 1 - c), (1 - x, y, c), (x, 1 - y, c)])]

    def start(ins, outs, scr):
        x, y, c, _ = _place()
        _cast_rows(scr[0], ins[0])
        local(outs, scr, 4 * x + 2 * y + c).start()
        for cp in own(outs, scr):
            cp.start()

    def passed_on(outs, scr):
        x, y, c, _ = _place()
        sib, xn, yn = (x, y, 1 - c), (1 - x, y, c), (x, 1 - y, c)
        b_xn, b_yn, b_dg = 4 * (1 - x) + 2 * y + c, 4 * x + 2 * (1 - y) + c, 4 * (1 - x) + 2 * (1 - y) + c
        near = [copy(outs, scr, 5, b_xn, yn, rows=0), copy(outs, scr, 3, b_xn, sib),
                copy(outs, scr, 6, b_yn, xn, rows=1), copy(outs, scr, 4, b_yn, sib)]
        far = [copy(outs, scr, 7, b_dg, sib, rows=0), copy(outs, scr, 8, b_dg, sib, rows=1)]
        return (b_xn, b_yn, b_dg), near, far

    def mid(ins, outs, scr):
        x, y, c, _ = _place()
        (b_xn, b_yn, _), near, _ = passed_on(outs, scr)
        copy(outs, scr, 1, b_xn, (x, y, c)).wait_recv()
        near[0].start()
        near[1].start()
        copy(outs, scr, 2, b_yn, (x, y, c)).wait_recv()
        near[2].start()
        near[3].start()

    def finish(ins, outs, scr):
        x, y, c, _ = _place()
        here = (x, y, c)
        (b_xn, b_yn, b_dg), near, far = passed_on(outs, scr)
        copy(outs, scr, 5, b_dg, here, rows=0).wait_recv()
        far[0].start()
        copy(outs, scr, 6, b_dg, here, rows=1).wait_recv()
        far[1].start()
        flip = 1 - 2 * c
        copy(outs, scr, 0, 4 * x + 2 * y + 1 - c, here).wait_recv()
        copy(outs, scr, 3, b_xn + flip, here).wait_recv()
        copy(outs, scr, 4, b_yn + flip, here).wait_recv()
        copy(outs, scr, 7, b_dg + flip, here, rows=0).wait_recv()
        copy(outs, scr, 8, b_dg + flip, here, rows=1).wait_recv()
        for cp in own(outs, scr) + near + far:
            cp.wait_send()
        local(outs, scr, 4 * x + 2 * y + c).wait()

    return Side((shard,), (VMEM,), (jax.ShapeDtypeStruct((NDEV, R, W), dtype),),
                (pltpu.VMEM((R, W), dtype), _sems(9), _sems(9), _sems(1)), start, finish, mid)


def ag_cols(shard):
    K, C = shard.shape
    half, w2 = K // 2, 2 * C

    def win(out, rows_c, chip):
        return out.at[pl.ds(pl.multiple_of(rows_c * half, 16), half), pl.ds(pl.multiple_of(chip * w2, LANES), w2)]

    def ici(outs, scr, j, to, c, k):
        slab, send, recv = scr[2], scr[5], scr[6]
        return _remote(slab.at[pl.ds(pl.multiple_of(c * half, 16), half), :], win(outs[0], c, k), send, recv, j, to)

    def local(outs, scr, k):
        return pltpu.make_async_copy(scr[2], outs[0].at[:, pl.ds(pl.multiple_of(k * w2, LANES), w2)], scr[7].at[0])

    def start(ins, outs, scr):
        stage, inbox, slab, xs, xr = scr[:5]
        x, y, c, chips = _place()
        k = 2 * x + y
        _cast_rows(stage, ins[0])
        swap = _remote(stage, inbox, xs, xr, 0, (x, y, 1 - c))
        swap.start()
        for cc in range(2):
            @pl.when(c == cc)
            def _(cc=cc):
                _cast_rows(slab, stage, slice(cc * C, (cc + 1) * C))
        swap.wait()
        for cc in range(2):
            @pl.when(c == cc)
            def _(cc=cc):
                _cast_rows(slab, inbox, slice((1 - cc) * C, (2 - cc) * C))
        local(outs, scr, k).start()
        for j, (cx, cy) in enumerate(chips):
            ici(outs, scr, j, (cx, cy, c), c, k).start()

    def finish(ins, outs, scr):
        send, recv = scr[5], scr[6]
        x, y, c, chips = _place()
        k, sib = 2 * x + y, (x, y, 1 - c)
        passed = []
        for j, (cx, cy) in enumerate(chips):
            w = win(outs[0], c, 2 * cx + cy)
            _remote(w, w, send, recv, j, sib).wait_recv()
            fwd = _remote(w, w, send, recv, 3 + j, sib)
            fwd.start()
            passed.append(fwd)
        for j, (cx, cy) in enumerate(chips):
            w = win(outs[0], 1 - c, 2 * cx + cy)
            _remote(w, w, send, recv, 3 + j, sib).wait_recv()
        for j, (cx, cy) in enumerate(chips):
            ici(outs, scr, j, (cx, cy, c), c, k).wait_send()
        for fwd in passed:
            fwd.wait_send()
        local(outs, scr, k).wait()

    return Side((shard,), (VMEM,), (jax.ShapeDtypeStruct((K, NDEV * C), BF16),),
                (pltpu.VMEM((K, C), BF16), pltpu.VMEM((K, C), BF16), pltpu.VMEM((K, w2), BF16),
                 _sems(1), _sems(1), _sems(6), _sems(6), _sems(1)), start, finish)


def copies_side(args, out_shape, n_copies, plan):
    def copies(ins, outs, scr):
        return [_remote(s_, d_, scr[0], scr[1], i, to) for i, (s_, d_, to) in enumerate(plan(ins, outs))]

    def start(ins, outs, scr):
        for cp in copies(ins, outs, scr):
            cp.start()

    def finish(ins, outs, scr):
        for cp in copies(ins, outs, scr):
            cp.wait()

    return Side(tuple(args), (ANY,) * len(args), tuple(out_shape), (_sems(n_copies), _sems(n_copies)), start, finish)


def rs_to_sibling(grads):
    out_shape = [jax.ShapeDtypeStruct((4,) + g.shape[1:] if kind == "rows" else (g.shape[0] // 2, g.shape[1]), F32)
                 for kind, g in grads]

    def plan(ins, outs):
        x, y, c, _ = _place()
        sib, res = (x, y, 1 - c), []
        for (kind, _), g, r in zip(grads, ins, outs):
            if kind == "rows":
                res += [(g.at[2 * k + 1 - c], r.at[k], sib) for k in range(4)]
            else:
                half = g.shape[0] // 2
                res.append((g.at[pl.ds(pl.multiple_of((1 - c) * half, 8), half), :], r, sib))
        return res

    return copies_side([g for _, g in grads], out_shape, sum(4 if kind == "rows" else 1 for kind, _ in grads), plan)


def rs_to_chips(parts):
    out_shape = [jax.ShapeDtypeStruct((3,) + p.shape[1:] if kind == "rows" else (3, p.shape[0], p.shape[1] // 4), BF16)
                 for kind, p in parts]

    def plan(ins, outs):
        x, y, c, chips = _place()
        res = []
        for (kind, _), p, r in zip(parts, ins, outs):
            for j, (cx, cy) in enumerate(chips):
                if kind == "rows":
                    src = p.at[2 * cx + cy]
                else:
                    w2 = p.shape[1] // 4
                    src = p.at[:, pl.ds(pl.multiple_of((2 * cx + cy) * w2, LANES), w2)]
                res.append((src, r.at[j], (cx, cy, c)))
        return res

    return copies_side([p for _, p in parts], out_shape, 3 * len(parts), plan)


def rs_swap_halves(theirs):
    def plan(ins, outs):
        x, y, c, _ = _place()
        return [(t, r, (x, y, 1 - c)) for t, r in zip(ins, outs)]

    return copies_side(theirs, [jax.ShapeDtypeStruct(t.shape, F32) for t in theirs], len(theirs), plan)


def _row_tiles(rows):
    return 2 if rows % 32 == 0 and rows >= 512 else 1


def chip_sum(name, grad, recv, c_idx, chip_idx):
    _, R, C = grad.shape
    nt = 1
    tr = R // nt

    def body(s_ref, g_ref, r_ref, p_ref, own_ref):
        k = pl.program_id(1)
        tot = g_ref[0] + r_ref[0]
        p_ref[0] = tot.astype(BF16)

        @pl.when(k == s_ref[1])
        def _():
            own_ref[...] = tot

    grid_spec = pltpu.PrefetchScalarGridSpec(
        num_scalar_prefetch=1, grid=(nt, 4),
        in_specs=[pl.BlockSpec((1, tr, C), lambda i, k, s: (2 * k + s[0], i, 0)),
                  pl.BlockSpec((1, tr, C), lambda i, k, s: (k, i, 0))],
        out_specs=[pl.BlockSpec((1, tr, C), lambda i, k, s: (k, i, 0)),
                   pl.BlockSpec((tr, C), lambda i, k, s: (i, 0))])
    return pl.pallas_call(
        body, name=name, grid_spec=grid_spec,
        out_shape=[jax.ShapeDtypeStruct((4, R, C), BF16), jax.ShapeDtypeStruct((R, C), F32)],
        compiler_params=_cp(dimension_semantics=("arbitrary", "arbitrary")),
    )(jnp.stack([c_idx, chip_idx]), grad, recv)


def _half_tiles(half):
    return 2 if half >= 512 else 1


def chip_sum_cols(name, grad, recv, c_idx, chip_idx):
    K, W = grad.shape
    half, w2 = K // 2, W // 4
    nt = _half_tiles(half)
    tr = half // nt

    def body(s_ref, g_ref, r_ref, p_ref, own_ref):
        tot = g_ref[...] + r_ref[...]
        p_ref[...] = tot.astype(BF16)

        @pl.when(pl.program_id(1) == s_ref[1])
        def _():
            own_ref[...] = tot

    grid_spec = pltpu.PrefetchScalarGridSpec(
        num_scalar_prefetch=1, grid=(nt, 4),
        in_specs=[pl.BlockSpec((tr, w2), lambda i, k, s: (s[0] * nt + i, k)),
                  pl.BlockSpec((tr, w2), lambda i, k, s: (i, k))],
        out_specs=[pl.BlockSpec((tr, w2), lambda i, k, s: (i, k)),
                   pl.BlockSpec((tr, w2), lambda i, k, s: (i, 0))])
    return pl.pallas_call(
        body, name=name, grid_spec=grid_spec,
        out_shape=[jax.ShapeDtypeStruct((half, W), BF16), jax.ShapeDtypeStruct((half, w2), F32)],
        compiler_params=_cp(dimension_semantics=("arbitrary", "arbitrary")),
    )(jnp.stack([c_idx, chip_idx]), grad, recv)


def col_final(name, own, recv, c_idx):
    half, w2 = own.shape
    C = w2 // 2
    nt = _half_tiles(half)
    tr = half // nt

    def body(s_ref, o_ref, r_ref, mine_ref, theirs_ref, t_ref):
        t_ref[...] = o_ref[...] + r_ref[0].astype(F32) + r_ref[1].astype(F32) + r_ref[2].astype(F32)
        for cc in range(2):
            @pl.when(s_ref[0] == cc)
            def _(cc=cc):
                mine_ref[...] = t_ref[:, cc * C:(cc + 1) * C]
                theirs_ref[...] = t_ref[:, (1 - cc) * C:(2 - cc) * C]

    grid_spec = pltpu.PrefetchScalarGridSpec(
        num_scalar_prefetch=1, grid=(nt,),
        in_specs=[pl.BlockSpec((tr, w2), lambda i, s: (i, 0)), pl.BlockSpec((3, tr, w2), lambda i, s: (0, i, 0))],
        out_specs=[pl.BlockSpec((tr, C), lambda i, s: (i, 0))] * 2,
        scratch_shapes=[pltpu.VMEM((tr, w2), F32)])
    return pl.pallas_call(
        body, name=name, grid_spec=grid_spec, out_shape=[jax.ShapeDtypeStruct((half, C), F32)] * 2,
        compiler_params=_cp(dimension_semantics=("arbitrary",)),
    )(jnp.stack([c_idx]), own, recv)


def _adamw(w, g, m, v):
    m2 = ADAM_B1 * m + (1.0 - ADAM_B1) * g
    v2 = ADAM_B2 * v + (1.0 - ADAM_B2) * (g * g)
    m_hat = m2 / (1.0 - ADAM_B1 ** ADAM_STEP)
    v_hat = v2 / (1.0 - ADAM_B2 ** ADAM_STEP)
    delta = -ADAM_LR * (m_hat / (jnp.sqrt(v_hat) + ADAM_EPS) + ADAM_WD * w)
    return delta, m2, v2


def shard_adam(name, owns, recvs, w, m, v):
    n = len(owns)
    R, Cp = owns[0].shape
    nt = _row_tiles(R)
    tr = R // nt

    def body(*refs):
        o_refs, r_refs = refs[:n], refs[n:2 * n]
        w_ref, m_ref, v_ref, g_ref, d_ref, nm_ref, nv_ref = refs[2 * n:]
        g = None
        for k in range(n):
            gk = o_refs[k][...] + r_refs[k][0].astype(F32) + r_refs[k][1].astype(F32) + r_refs[k][2].astype(F32)
            g = gk if g is None else jnp.where(pl.program_id(0) == k, gk, g)
        delta, m2, v2 = _adamw(w_ref[...], g, m_ref[...], v_ref[...])
        g_ref[...] = g
        d_ref[...] = delta
        nm_ref[...] = m2
        nv_ref[...] = v2

    part = pl.BlockSpec((tr, Cp), lambda k, i: (i, 0))
    part3 = pl.BlockSpec((3, tr, Cp), lambda k, i: (0, i, 0))
    tile = pl.BlockSpec((tr, Cp), lambda k, i: (i, k))
    return pl.pallas_call(
        body, name=name, grid=(n, nt),
        in_specs=[part] * n + [part3] * n + [tile, tile, tile],
        out_specs=[tile] * 4, out_shape=[jax.ShapeDtypeStruct((R, n * Cp), F32)] * 4,
        compiler_params=_cp(dimension_semantics=("arbitrary", "arbitrary")),
    )(*owns, *recvs, w, m, v)


def adam_cols(name, mine, recv, w, m, v, c_idx):
    half, C = mine.shape
    nt = _half_tiles(half)
    tr = half // nt

    def body(s_ref, a_ref, b_ref, w_ref, m_ref, v_ref, g_ref, d_ref, nm_ref, nv_ref):
        g = jnp.where(pl.program_id(0) == s_ref[0], a_ref[...], b_ref[...])
        delta, m2, v2 = _adamw(w_ref[...], g, m_ref[...], v_ref[...])
        g_ref[...] = g
        d_ref[...] = delta
        nm_ref[...] = m2
        nv_ref[...] = v2

    part = pl.BlockSpec((tr, C), lambda hh, i, s: (i, 0))
    tile = pl.BlockSpec((tr, C), lambda hh, i, s: (hh * nt + i, 0))
    grid_spec = pltpu.PrefetchScalarGridSpec(
        num_scalar_prefetch=1, grid=(2, nt), in_specs=[part, part, tile, tile, tile], out_specs=[tile] * 4)
    return pl.pallas_call(
        body, name=name, grid_spec=grid_spec, out_shape=[jax.ShapeDtypeStruct((2 * half, C), F32)] * 4,
        compiler_params=_cp(dimension_semantics=("arbitrary", "arbitrary")),
    )(jnp.stack([c_idx]), mine, recv, w, m, v)


ROW_N1, ROW_N2, ROW_BG, ROW_QN, ROW_KN, ROW_CB, ROW_LW, ROW_LB, ROW_CW = 0, 1, 2, 4, 5, 6, 7, 8, 9
PACK_ROWS = 40
SMALL = ("norm1_w", "norm2_w", "b_gate", "q_norm_w", "k_norm_w", "conv_b", "conv_ln_w", "conv_ln_b", "conv_w")


def small_sync_adam(g, w, m, v, sides=()):
    ns = len(SMALL)

    def body(*refs):
        gi = dict(zip(SMALL, refs[:ns]))
        wi = dict(zip(SMALL, refs[ns:2 * ns]))
        mi = dict(zip(SMALL, refs[2 * ns:3 * ns]))
        vi = dict(zip(SMALL, refs[3 * ns:4 * ns]))
        outs = refs[4 * ns:8 * ns]
        pack, recv, tot, send_sems, recv_sems = refs[8 * ns:]
        x, y, c, _ = _place()
        me = 4 * x + 2 * y + c

        pack[...] = jnp.zeros_like(pack)
        pack[ROW_N1:ROW_N1 + 1, :] = gi["norm1_w"][...]
        pack[ROW_N2:ROW_N2 + 1, :] = gi["norm2_w"][...]
        pack[ROW_BG:ROW_BG + 2, :] = gi["b_gate"][...]
        pack[ROW_QN:ROW_QN + 1, 0:HD] = gi["q_norm_w"][...]
        pack[ROW_KN:ROW_KN + 1, 0:HD] = gi["k_norm_w"][...]
        pack[ROW_CB:ROW_CB + 1, 0:CC] = gi["conv_b"][...]
        pack[ROW_LW:ROW_LW + 1, 0:CC] = gi["conv_ln_w"][...]
        pack[ROW_LB:ROW_LB + 1, 0:CC] = gi["conv_ln_b"][...]
        pack[ROW_CW:ROW_CW + KW, 0:CC] = gi["conv_w"][...]

        copies = []
        for k in range(1, NDEV):
            peer = (x ^ (k >> 2), y ^ ((k >> 1) & 1), c ^ (k & 1))
            cp = pltpu.make_async_remote_copy(
                src_ref=pack, dst_ref=recv.at[me], send_sem=send_sems.at[k - 1], recv_sem=recv_sems.at[k - 1],
                device_id=peer, device_id_type=MESH)
            cp.start()
            copies.append(cp)
        recv[me] = pack[...]
        for cp in copies:
            cp.wait()
        acc = recv[0]
        for p in range(1, NDEV):
            acc = acc + recv[p]
        tot[...] = acc

        def shard_grad(name):
            if name == "b_gate":
                return tot[ROW_BG:ROW_BG + 2, pl.ds(pl.multiple_of(me * LANES, LANES), LANES)]
            if name == "conv_w":
                win = tot[ROW_CW:ROW_CW + KW, pl.ds(pl.multiple_of((me // 2) * LANES, LANES), LANES)]
                return jnp.where(me % 2 == 1, win[:, HD:LANES], win[:, 0:HD])
            row = {"norm1_w": ROW_N1, "norm2_w": ROW_N2, "q_norm_w": ROW_QN, "k_norm_w": ROW_KN,
                   "conv_b": ROW_CB, "conv_ln_w": ROW_LW, "conv_ln_b": ROW_LB}[name]
            return tot[row:row + 1, 0:wi[name].shape[1]]

        for i, name in enumerate(SMALL):
            gr = shard_grad(name)
            delta, m2, v2 = _adamw(wi[name][...], gr, mi[name][...], vi[name][...])
            outs[4 * i][...] = gr
            outs[4 * i + 1][...] = delta
            outs[4 * i + 2][...] = m2
            outs[4 * i + 3][...] = v2

    out_shape = []
    for name in SMALL:
        out_shape += [jax.ShapeDtypeStruct(w[name].shape, F32)] * 4
    args = [g[k] for k in SMALL] + [w[k] for k in SMALL] + [m[k] for k in SMALL] + [v[k] for k in SMALL]
    res = _call(
        body, sides, name="small_sync_adam", grid=(1,), in_specs=[VMEM] * len(args),
        out_specs=[VMEM] * len(out_shape), out_shape=out_shape,
        scratch_shapes=[pltpu.VMEM((PACK_ROWS, D), F32), pltpu.VMEM((NDEV, PACK_ROWS, D), F32),
                        pltpu.VMEM((PACK_ROWS, D), F32), _sems(NDEV - 1), _sems(NDEV - 1)],
        args=args)
    res, side_outs = res if sides else (res, None)
    out = {name: tuple(res[4 * i:4 * i + 4]) for i, name in enumerate(SMALL)}
    return (out, side_outs) if sides else out


MATS = ("w_in", "w_o_attn", "w_pw_conv", "w_out", "w_ffn_in", "w_ffn_out")
TRANSPOSED = ("w_in", "w_ffn_in")
WEIGHTS = ("norm1_w", "w_in", "b_gate", "q_norm_w", "k_norm_w", "w_o_attn", "conv_w", "conv_b", "conv_ln_w",
           "conv_ln_b", "w_pw_conv", "w_out", "norm2_w", "w_ffn_in", "w_ffn_out")


def _blocks_to_cols(blocks):
    n, R, C = blocks.shape
    return blocks.transpose(1, 0, 2).reshape(R, n * C)


def kernel(x, positions, norm1_w, w_in, b_gate, q_norm_w, k_norm_w, w_o_attn, conv_w, conv_b, conv_ln_w, conv_ln_b, w_pw_conv, w_out, norm2_w, w_ffn_in, w_ffn_out, loss_target, m_norm1_w, m_w_in, m_b_gate, m_q_norm_w, m_k_norm_w, m_w_o_attn, m_conv_w, m_conv_b, m_conv_ln_w, m_conv_ln_b, m_w_pw_conv, m_w_out, m_norm2_w, m_w_ffn_in, m_w_ffn_out, v_norm1_w, v_w_in, v_b_gate, v_q_norm_w, v_k_norm_w, v_w_o_attn, v_conv_w, v_conv_b, v_conv_ln_w, v_conv_ln_b, v_w_pw_conv, v_w_out, v_norm2_w, v_w_ffn_in, v_w_ffn_out):
    w = dict(norm1_w=norm1_w, w_in=w_in, b_gate=b_gate, q_norm_w=q_norm_w, k_norm_w=k_norm_w, w_o_attn=w_o_attn,
             conv_w=conv_w, conv_b=conv_b, conv_ln_w=conv_ln_w, conv_ln_b=conv_ln_b, w_pw_conv=w_pw_conv,
             w_out=w_out, norm2_w=norm2_w, w_ffn_in=w_ffn_in, w_ffn_out=w_ffn_out)
    m = dict(norm1_w=m_norm1_w, w_in=m_w_in, b_gate=m_b_gate, q_norm_w=m_q_norm_w, k_norm_w=m_k_norm_w,
             w_o_attn=m_w_o_attn, conv_w=m_conv_w, conv_b=m_conv_b, conv_ln_w=m_conv_ln_w,
             conv_ln_b=m_conv_ln_b, w_pw_conv=m_w_pw_conv, w_out=m_w_out, norm2_w=m_norm2_w,
             w_ffn_in=m_w_ffn_in, w_ffn_out=m_w_ffn_out)
    v = dict(norm1_w=v_norm1_w, w_in=v_w_in, b_gate=v_b_gate, q_norm_w=v_q_norm_w, k_norm_w=v_k_norm_w,
             w_o_attn=v_w_o_attn, conv_w=v_conv_w, conv_b=v_conv_b, conv_ln_w=v_conv_ln_w,
             conv_ln_b=v_conv_ln_b, w_pw_conv=v_w_pw_conv, w_out=v_w_out, norm2_w=v_norm2_w,
             w_ffn_in=v_w_ffn_in, w_ffn_out=v_w_ffn_out)
    def two_d(t):
        t = {k: (a[0] if a.ndim == 3 else a) for k, a in t.items()}
        return {k: (a.T if k in TRANSPOSED else a) for k, a in t.items()}

    w, m, v = two_d(w), two_d(m), two_d(v)

    x2, target = x[0], loss_target[0]
    c_idx = lax.axis_index("c").astype(jnp.int32)
    chip_idx = (2 * lax.axis_index("x") + lax.axis_index("y")).astype(jnp.int32)
    tabs = rope_tables(positions.reshape(S, 1))
    qw2 = jnp.tile(w["q_norm_w"], (1, 2))
    kw2 = jnp.tile(w["k_norm_w"], (1, 2))

    (w_in_blocks,), (bg_blocks,), (cw_blocks,) = comm_only(
        "gather_first", (ag_blocks_relay(w["w_in"], BF16), ag_blocks(w["b_gate"], F32), ag_blocks(w["conv_w"], F32)))
    w_in_t = w_in_blocks.reshape(INW, D)
    b_gate_f, conv_w_f = _blocks_to_cols(bg_blocks), _blocks_to_cols(cw_blocks)
    (h, proj), ((w_o_f,), (w_pw_f,), (w_out_blocks,)) = in_proj(
        x2, w["norm1_w"], w_in_t, sides=(ag_cols(w["w_o_attn"]), ag_cols(w["w_pw_conv"]), ag_blocks_relay(w["w_out"], BF16)))
    w_out_f = w_out_blocks.reshape(D, D)
    (attn, lse), ((w_ffn_in_blocks,),) = attn_fwd(proj, tabs, qw2, kw2, sides=(ag_blocks_relay(w["w_ffn_in"], BF16),))
    w_ffn_in_t = w_ffn_in_blocks.reshape(2 * FF, D)
    cpre, u3 = conv_fwd(proj, conv_w_f, w["conv_b"], w["conv_ln_w"], w["conv_ln_b"])
    x1, z, ya, yb = mix_out(x2, proj, b_gate_f, attn, u3, w_o_f, w_pw_f, w_out_f)
    (h2, gu, f), ((w_ffn_out_blocks,),) = ffn_in(x1, w["norm2_w"], w_ffn_in_t, sides=(ag_blocks_relay(w["w_ffn_out"], BF16),))
    w_ffn_out_f = w_ffn_out_blocks.reshape(FF, D)
    dy, dyb, sq = ffn_out_loss(x1, f, w_ffn_out_f, target)
    loss = lax.psum((0.5 / D) * jnp.sum(sq), ("x", "y", "c"))

    g = {}
    g_ffn_out = mm_tn("gw_ffn_out", f, dyb, FF // 2, D).reshape(NDEV, FF // NDEV, D)
    (d_gu, d_x1, d_x1b, g["norm2_w"]), ((ra_ffn_out,),) = ffn_bwd(
        dy, dyb, gu, x1, w["norm2_w"], w_ffn_in_t, w_ffn_out_f, sides=(rs_to_sibling([("rows", g_ffn_out)]),))
    pb_ffn_out, own_ffn_out = chip_sum("chip_sum_w_ffn_out", g_ffn_out, ra_ffn_out, c_idx, chip_idx)
    g_ffn_in, ((rb_ffn_out,),) = mm_tn("gw_ffn_in", d_gu, h2, FF // 2, D,
                                       sides=(rs_to_chips([("rows", pb_ffn_out)]),))
    g_ffn_in = g_ffn_in.reshape(NDEV, 2 * FF // NDEV, D)
    g_out = mm_tn("gw_out", z, d_x1b, D // 2, D).reshape(NDEV, D // NDEV, D)
    (d_ya, d_yb, d_gl, d_attn, d_u3, g["b_gate"]), ((ra_ffn_in,),) = out_bwd(
        d_x1b, proj, b_gate_f, ya, yb, w_o_f, w_pw_f, w_out_f, sides=(rs_to_sibling([("rows", g_ffn_in)]),))
    pb_ffn_in, own_ffn_in = chip_sum("chip_sum_w_ffn_in", g_ffn_in, ra_ffn_in, c_idx, chip_idx)
    g_w_o = mm_tn("gw_o_attn", attn, d_ya, CC, D)
    g_w_pw = mm_tn("gw_pw_conv", u3, d_yb, CC, D)
    (d_conv, g["conv_w"], g["conv_b"], g["conv_ln_w"], g["conv_ln_b"]), ((ra_out, ra_w_o, ra_w_pw),) = conv_bwd(
        proj, cpre, d_u3, conv_w_f, conv_w_f[::-1], w["conv_ln_w"], w["conv_ln_b"],
        sides=(rs_to_sibling([("rows", g_out), ("cols", g_w_o), ("cols", g_w_pw)]),))
    pb_out, own_out = chip_sum("chip_sum_w_out", g_out, ra_out, c_idx, chip_idx)
    pb_w_o, own_w_o = chip_sum_cols("chip_sum_w_o_attn", g_w_o, ra_w_o, c_idx, chip_idx)
    pb_w_pw, own_w_pw = chip_sum_cols("chip_sum_w_pw_conv", g_w_pw, ra_w_pw, c_idx, chip_idx)
    (d_q, d_k, d_v, gqw, gkw), ((rb_ffn_in, rb_out, rb_w_o, rb_w_pw),) = attn_bwd(
        proj, tabs, qw2, kw2, d_attn, attn, lse,
        sides=(rs_to_chips([("rows", pb_ffn_in), ("rows", pb_out), ("cols", pb_w_o), ("cols", pb_w_pw)]),))
    g["q_norm_w"] = gqw[0:1, 0:HD] + gqw[0:1, HD:LANES]
    g["k_norm_w"] = gkw[0:1, 0:HD] + gkw[0:1, HD:LANES]
    mine_w_o, theirs_w_o = col_final("col_final_w_o_attn", own_w_o, rb_w_o, c_idx)
    mine_w_pw, theirs_w_pw = col_final("col_final_w_pw_conv", own_w_pw, rb_w_pw, c_idx)
    d_segs = (d_q, d_k, d_v, d_conv, d_gl)
    g_w_in_a, ((rc_w_o, rc_w_pw),) = gw_in_t("gw_in_a", h, d_segs, 0,
                                             sides=(rs_swap_halves([theirs_w_o, theirs_w_pw]),))
    g_w_in_a = g_w_in_a.reshape(NDEV, INW // NDEV, D // 2)
    g_w_in_b, ((ra_w_in_a,),) = gw_in_t("gw_in_b", h, d_segs, 1, sides=(rs_to_sibling([("rows", g_w_in_a)]),))
    g_w_in_b = g_w_in_b.reshape(NDEV, INW // NDEV, D // 2)
    pb_w_in_a, own_w_in_a = chip_sum("chip_sum_w_in_a", g_w_in_a, ra_w_in_a, c_idx, chip_idx)
    (grad_x, g["norm1_w"]), ((rb_w_in_a,), (ra_w_in_b,)) = in_bwd(
        d_q, d_k, d_v, d_conv, d_gl, w_in_t, x2, d_x1, w["norm1_w"],
        sides=(rs_to_chips([("rows", pb_w_in_a)]), rs_to_sibling([("rows", g_w_in_b)])))
    pb_w_in_b, own_w_in_b = chip_sum("chip_sum_w_in_b", g_w_in_b, ra_w_in_b, c_idx, chip_idx)
    small, ((rb_w_in_b,),) = small_sync_adam(g, w, m, v, sides=(rs_to_chips([("rows", pb_w_in_b)]),))

    res = {
        "w_in": shard_adam("adam_w_in", [own_w_in_a, own_w_in_b], [rb_w_in_a, rb_w_in_b],
                           w["w_in"], m["w_in"], v["w_in"]),
        "w_ffn_in": shard_adam("adam_w_ffn_in", [own_ffn_in], [rb_ffn_in], w["w_ffn_in"], m["w_ffn_in"], v["w_ffn_in"]),
        "w_o_attn": adam_cols("adam_w_o_attn", mine_w_o, rc_w_o, w["w_o_attn"], m["w_o_attn"], v["w_o_attn"], c_idx),
        "w_pw_conv": adam_cols("adam_w_pw_conv", mine_w_pw, rc_w_pw, w["w_pw_conv"], m["w_pw_conv"], v["w_pw_conv"], c_idx),
        "w_out": shard_adam("adam_w_out", [own_out], [rb_out], w["w_out"], m["w_out"], v["w_out"]),
        "w_ffn_out": shard_adam("adam_w_ffn_out", [own_ffn_out], [rb_ffn_out],
                                w["w_ffn_out"], m["w_ffn_out"], v["w_ffn_out"]),
    }
    res = {k: tuple(a.T if k in TRANSPOSED else a for a in r) for k, r in res.items()}
    res.update(small)

    def shaped(name, a):
        return a.reshape((1,) + a.shape) if name in MATS or name in ("b_gate", "conv_w") else a

    outs = [loss, grad_x.reshape(1, S, D)]
    for i in range(4):
        outs += [shaped(k, res[k][i]) for k in WEIGHTS]
    return tuple(outs)
```

```python
import functools
from typing import Callable, NamedTuple, Optional

import numpy as np
import jax
import jax.numpy as jnp
from jax import lax
from jax.experimental import pallas as pl
from jax.experimental.pallas import tpu as pltpu

F32 = jnp.float32
BF16 = jnp.bfloat16

S = 2048
D = 1024
HD = 64
QKV = 1536
CC = 512
KW = 31
FF = 2816
INW = 7680
OFF_Q, OFF_K, OFF_V, OFF_CA, OFF_CB, OFF_GA, OFF_GB = 0, 1536, 3072, 4608, 5120, 5632, 6656
DILATIONS = (1, 4, 16)
HALF_SPAN = 64
EPS = 1e-6
NEG_INF = -1e30
ROPE_THETA = 500000.0
ROT_DIM = 16

ADAM_LR = 0.001
ADAM_B1 = 0.9
ADAM_B2 = 0.999
ADAM_EPS = 1e-08
ADAM_WD = 0.01
ADAM_STEP = 10

NDEV = 8
LANES = 128
TM = 256
TQ = 128
VMEM_LIMIT = 56 * 1024 * 1024
MESH = pl.DeviceIdType.MESH


def _cp(**kw):
    return pltpu.CompilerParams(vmem_limit_bytes=VMEM_LIMIT, **kw)


def _row(width, col=0, tm=TM):
    return pl.BlockSpec((tm, width), lambda i: (i, col))


def _res(shape):
    nd = len(shape)
    return pl.BlockSpec(shape, lambda *_: (0,) * nd, pipeline_mode=pl.Buffered(1))


def _dot(a, b):
    return jnp.dot(a, b, preferred_element_type=F32)


def _dot_nt(a, b):
    return lax.dot_general(a, b, (((1,), (1,)), ((), ())), preferred_element_type=F32)


def _dot_tn(a, b):
    return lax.dot_general(a, b, (((0,), (0,)), ((), ())), preferred_element_type=F32)


def _sigmoid(x):
    return jax.nn.sigmoid(x)


def _dsilu(x, sg):
    return sg * (1.0 + x * (1.0 - sg))


ANY = pl.BlockSpec(memory_space=pl.ANY)
VMEM = pl.BlockSpec(memory_space=pltpu.VMEM)


class Side(NamedTuple):
    args: tuple
    in_specs: tuple
    out_shape: tuple
    scratch: tuple
    start: Callable
    finish: Callable
    mid: Optional[Callable] = None


def _call(body, sides=(), *, name, grid, in_specs, out_specs, out_shape, scratch_shapes=(), args):
    ni, no, ns = len(in_specs), len(out_specs), len(scratch_shapes)
    cnt = [(len(s.args), len(s.out_shape), len(s.scratch)) for s in sides]

    def take(refs, pos, n):
        return refs[pos:pos + n], pos + n

    def full(*refs):
        m_in, pos = take(refs, 0, ni)
        s_in = []
        for a, _, _ in cnt:
            r, pos = take(refs, pos, a)
            s_in.append(r)
        m_out, pos = take(refs, pos, no)
        s_out = []
        for _, o, _ in cnt:
            r, pos = take(refs, pos, o)
            s_out.append(r)
        m_scr, pos = take(refs, pos, ns)
        s_scr = []
        for _, _, c in cnt:
            r, pos = take(refs, pos, c)
            s_scr.append(r)
        if sides:
            first = functools.reduce(jnp.logical_and, [pl.program_id(d) == 0 for d in range(len(grid))])
            last = functools.reduce(jnp.logical_and, [pl.program_id(d) == g - 1 for d, g in enumerate(grid)])

            @pl.when(first)
            def _():
                for s, a, o, c in zip(sides, s_in, s_out, s_scr):
                    s.start(a, o, c)

            steps = int(np.prod(grid))
            mid_step = (2 * steps) // 3
            if steps > 1 and any(s.mid is not None for s in sides):
                step = functools.reduce(lambda acc, d: acc * grid[d] + pl.program_id(d), range(len(grid)), 0)

                @pl.when(step == mid_step)
                def _():
                    for s, a, o, c in zip(sides, s_in, s_out, s_scr):
                        if s.mid is not None:
                            s.mid(a, o, c)

        body(*m_in, *m_out, *m_scr)
        if sides:
            @pl.when(last)
            def _():
                for s, a, o, c in zip(sides, s_in, s_out, s_scr):
                    if s.mid is not None and steps == 1:
                        s.mid(a, o, c)
                    s.finish(a, o, c)

    res = pl.pallas_call(
        full, name=name, grid=grid,
        in_specs=list(in_specs) + [sp for s in sides for sp in s.in_specs],
        out_specs=list(out_specs) + [ANY for s in sides for _ in s.out_shape],
        out_shape=list(out_shape) + [o for s in sides for o in s.out_shape],
        scratch_shapes=list(scratch_shapes) + [c for s in sides for c in s.scratch],
        compiler_params=_cp(dimension_semantics=("arbitrary",) * len(grid)),
    )(*args, *[a for s in sides for a in s.args])
    res = list(res)
    if not sides:
        return res
    outs, pos = take(res, 0, no)
    side_outs = []
    for _, o, _ in cnt:
        r, pos = take(res, pos, o)
        side_outs.append(r)
    return outs, side_outs


def _inv_freq_lanes():
    inv = np.float32(ROPE_THETA) ** (-np.arange(0, ROT_DIM, 2, dtype=np.float32) / np.float32(ROT_DIM))
    lane = np.arange(LANES) % HD
    out = np.where(lane < ROT_DIM, inv[lane % (ROT_DIM // 2)], 0.0).astype(np.float32)
    return jnp.asarray(out.reshape(1, LANES))


def rope_tables(pos_col, sides=()):
    def body(p_ref, f_ref, c_ref, s1_ref, s2_ref):
        ang = p_ref[...].astype(F32) * f_ref[...]
        lane = lax.broadcasted_iota(jnp.int32, ang.shape, 1) % HD
        cs = jnp.cos(ang)
        sn = jnp.sin(ang)
        c_ref[...] = jnp.where(lane < ROT_DIM, cs, 1.0)
        s1_ref[...] = jnp.where(lane < ROT_DIM // 2, -sn, 0.0)
        s2_ref[...] = jnp.where(lane < ROT_DIM // 2, 0.0, jnp.where(lane < ROT_DIM, sn, 0.0))

    sds = jax.ShapeDtypeStruct((S, LANES), F32)
    return _call(
        body, sides, name="rope_tables", grid=(S // TM,),
        in_specs=[_row(1), pl.BlockSpec((1, LANES), lambda i: (0, 0))],
        out_specs=[_row(LANES)] * 3, out_shape=[sds] * 3,
        args=(pos_col, _inv_freq_lanes()))


def _rope(v, c, s1, s2):
    return v * c + pltpu.roll(v, LANES - 8, axis=1) * s1 + pltpu.roll(v, 8, axis=1) * s2


def _rope_t(d, c, s1, s2):
    return d * c - pltpu.roll(d, LANES - 8, axis=1) * s1 - pltpu.roll(d, 8, axis=1) * s2


def _head_mat():
    r = lax.broadcasted_iota(jnp.int32, (LANES, LANES), 0) // HD
    c = lax.broadcasted_iota(jnp.int32, (LANES, LANES), 1) // HD
    return jnp.where(r == c, 1.0 / HD, 0.0).astype(BF16)


def _head_mean(t, e):
    hi = t.astype(BF16)
    rest = (t - hi.astype(F32)).astype(BF16)
    return _dot(hi, e) + _dot(rest, e)


def in_proj(x, norm_w, w_in, sides=()):
    nchunk = 5
    cw = INW // nchunk

    def body(x_ref, nw_ref, w_ref, h_ref, p_ref):
        xv = x_ref[...]
        r = lax.rsqrt(jnp.mean(xv * xv, axis=-1, keepdims=True) + EPS)
        h = (xv * r * nw_ref[...]).astype(BF16)
        h_ref[...] = h
        for j in range(nchunk):
            p_ref[:, j * cw:(j + 1) * cw] = _dot_nt(h, w_ref[j * cw:(j + 1) * cw, :])

    return _call(
        body, sides, name="in_proj", grid=(S // TM,),
        in_specs=[_row(D), _res((1, D)), _res((INW, D))],
        out_specs=[_row(D), _row(INW)],
        out_shape=[jax.ShapeDtypeStruct((S, D), BF16), jax.ShapeDtypeStruct((S, INW), F32)],
        args=(x, norm_w, w_in))


def _qk_specs():
    nb = QKV // LANES
    return [pl.BlockSpec((S, LANES), functools.partial(lambda hp, g, o: (0, o + g * 4 + hp), o=o))
            for o in (OFF_Q // LANES, OFF_K // LANES, OFF_V // LANES)]


def _tab_specs():
    return [pl.BlockSpec((S, LANES), lambda hp, g: (0, 0), pipeline_mode=pl.Buffered(1))] * 3


def _vec_spec():
    return pl.BlockSpec((1, LANES), lambda hp, g: (0, 0))


def _sub_rows(r, d, start, n):
    if d == 1:
        return pl.ds(start, n)
    return pl.ds(r + d * start, n, stride=d)


def _band_window(i, L):
    W = min(TQ + 2 * HALF_SPAN, L)
    q0 = pl.multiple_of(i * TQ, TQ)
    k0 = pl.multiple_of(jnp.clip(q0 - HALF_SPAN, 0, L - W), HALF_SPAN)
    qpos = q0 + (lax.broadcasted_iota(jnp.int32, (2 * TQ, W), 0) & (TQ - 1))
    kpos = k0 + lax.broadcasted_iota(jnp.int32, (2 * TQ, W), 1)
    valid = jnp.abs(qpos - kpos) <= HALF_SPAN
    return W, q0, k0, valid


def _stack_heads(t, lo):
    z = jnp.zeros_like(t)
    return jnp.concatenate([jnp.where(lo, t, z), jnp.where(lo, z, t)], axis=0)


def _unstack_heads(t2, lo):
    return jnp.where(lo, t2[0:TQ], t2[TQ:2 * TQ])


CHAINS = 4


def _interleave(d):
    ru = min(d, CHAINS)
    return ru, min(CHAINS // ru, S // d // TQ)


def _for_blocks(n, fn):
    if n == 1:
        fn(0)
    else:
        def it(j, _):
            fn(j)
            return 0
        lax.fori_loop(0, n, it, 0)


def attn_fwd(proj, tabs, qw2, kw2, sides=()):
    CH = 256

    def body(q_ref, k_ref, v_ref, c_ref, s1_ref, s2_ref, qw_ref, kw_ref, at_ref, ls_ref,
             qs, ks, vs, osub, lsub, onat, lnat, qn, kn):
        g = pl.program_id(1)
        lo = lax.broadcasted_iota(jnp.int32, (1, LANES), 1) < HD
        e = _head_mat()

        def prep(i, _):
            rows = pl.ds(pl.multiple_of(i * CH, CH), CH)
            c, s1, s2 = c_ref[rows, :], s1_ref[rows, :], s2_ref[rows, :]
            for t_ref, w_ref, out, scale in ((q_ref, qw_ref, qn, HD ** -0.5), (k_ref, kw_ref, kn, 1.0)):
                t = t_ref[rows, :]
                r = lax.rsqrt(_head_mean(t * t, e) + EPS)
                out[rows, :] = _rope(t * r * w_ref[...], c, s1, s2) * scale
            return 0

        lax.fori_loop(0, S // CH, prep, 0, unroll=4)

        def group(gi, d):
            L = S // d

            ru, nb = _interleave(d)

            def stage(r, off):
                for c0 in range(0, L, CH):
                    n = min(CH, L)
                    rows = _sub_rows(r, d, c0, n)
                    dst = pl.ds(off + c0, n)
                    qs[dst, :] = qn[rows, :].astype(BF16)
                    ks[dst, :] = kn[rows, :].astype(BF16)
                    vs[dst, :] = v_ref[rows, :].astype(BF16)

            def one(off, i):
                W, q0, k0, valid = _band_window(i, L)
                q2 = _stack_heads(qs[pl.ds(off + q0, TQ), :], lo)
                sc = jnp.where(valid, _dot_nt(q2, ks[pl.ds(off + k0, W), :]), NEG_INF)
                m = jnp.max(sc, axis=-1, keepdims=True)
                p = jnp.exp(sc - m)
                den = jnp.sum(p, axis=-1, keepdims=True)
                o2 = _dot(p.astype(BF16), vs[pl.ds(off + k0, W), :]) / den
                l2 = jnp.broadcast_to(m + jnp.log(den), (2 * TQ, LANES))
                osub[pl.ds(off + q0, TQ), :] = _unstack_heads(o2, lo)
                lsub[pl.ds(off + q0, TQ), :] = _unstack_heads(l2, lo)

            def unstage(r, off):
                for c0 in range(0, L, CH):
                    n = min(CH, L)
                    rows = _sub_rows(r, d, c0, n)
                    onat[gi, rows, :] = osub[pl.ds(off + c0, n), :]
                    lnat[gi, rows, :] = lsub[pl.ds(off + c0, n), :]

            def step(t, _):
                for u in range(ru):
                    stage(t * ru + u, u * L)
                _for_blocks(L // TQ // nb, lambda j: [one(u * L, j * nb + b) for u in range(ru) for b in range(nb)])
                for u in range(ru):
                    unstage(t * ru + u, u * L)
                return 0

            lax.fori_loop(0, d // ru, step, 0)

        for gi, d in enumerate(DILATIONS):
            pl.when(g == gi)(functools.partial(group, gi, d))

        @pl.when(g == len(DILATIONS) - 1)
        def _():
            def mix(i, _):
                rows = pl.ds(pl.multiple_of(i * CH, CH), CH)
                l0, l1, l2 = lnat[0, rows, :], lnat[1, rows, :], lnat[2, rows, :]
                m = jnp.maximum(jnp.maximum(l0, l1), l2)
                e0, e1, e2 = jnp.exp(l0 - m), jnp.exp(l1 - m), jnp.exp(l2 - m)
                den = e0 + e1 + e2
                a = (e0 * onat[0, rows, :] + e1 * onat[1, rows, :] + e2 * onat[2, rows, :]) / den
                at_ref[rows, :] = a.astype(BF16)
                ls_ref[rows, :] = m + jnp.log(den)
                return 0

            lax.fori_loop(0, S // CH, mix, 0)

    out_spec = pl.BlockSpec((S, LANES), lambda hp, g: (0, hp))
    return _call(
        body, sides, name="attn_fwd", grid=(4, 3),
        in_specs=_qk_specs() + _tab_specs() + [_vec_spec(), _vec_spec()],
        out_specs=[out_spec, out_spec],
        out_shape=[jax.ShapeDtypeStruct((S, CC), BF16), jax.ShapeDtypeStruct((S, CC), F32)],
        scratch_shapes=[pltpu.VMEM((S, LANES), BF16)] * 3 + [pltpu.VMEM((S, LANES), F32)] * 2
        + [pltpu.VMEM((3, S, LANES), F32)] * 2 + [pltpu.VMEM((S, LANES), F32)] * 2,
        args=(proj, proj, proj, *tabs, qw2, kw2))


def attn_bwd(proj, tabs, qw2, kw2, d_attn, attn, lse, sides=()):
    CH = 256

    def body(q_ref, k_ref, v_ref, c_ref, s1_ref, s2_ref, qw_ref, kw_ref, do_ref, at_ref, ls_ref,
             dq_ref, dk_ref, dv_ref, gqw_ref, gkw_ref,
             qs, ks, vs, dos, dsub, lsub, dqs, dks, dvs, dnat, qx, kx, dvn, tnq, tnk, rrq, rrk):
        hp, g = pl.program_id(0), pl.program_id(1)
        lo = lax.broadcasted_iota(jnp.int32, (1, LANES), 1) < HD
        e = _head_mat()
        both = ((q_ref, qw_ref, qx, tnq, rrq, HD ** -0.5), (k_ref, kw_ref, kx, tnk, rrk, 1.0))

        @pl.when((hp == 0) & (g == 0))
        def _():
            gqw_ref[...] = jnp.zeros_like(gqw_ref)
            gkw_ref[...] = jnp.zeros_like(gkw_ref)

        def prep(i, _):
            rows = pl.ds(pl.multiple_of(i * CH, CH), CH)
            dnat[rows, :] = _head_mean(do_ref[rows, :] * at_ref[rows, :].astype(F32), e) * float(HD)
            c, s1, s2 = c_ref[rows, :], s1_ref[rows, :], s2_ref[rows, :]
            for t_ref, w_ref, x, tn_s, rr_s, scale in both:
                t = t_ref[rows, :]
                rr = lax.rsqrt(_head_mean(t * t, e) + EPS)
                tn = t * rr
                rr_s[rows, :] = rr
                tn_s[rows, :] = tn
                x[rows, :] = _rope(tn * w_ref[...], c, s1, s2) * scale
            return 0

        lax.fori_loop(0, S // CH, prep, 0, unroll=4)

        def group(d):
            L = S // d

            ru, nb = _interleave(d)

            def stage(r, off):
                for c0 in range(0, L, CH):
                    n = min(CH, L)
                    rows = _sub_rows(r, d, c0, n)
                    dst = pl.ds(off + c0, n)
                    qs[dst, :] = qx[rows, :].astype(BF16)
                    ks[dst, :] = kx[rows, :].astype(BF16)
                    vs[dst, :] = v_ref[rows, :].astype(BF16)
                    dos[dst, :] = do_ref[rows, :].astype(BF16)
                    dsub[dst, :] = dnat[rows, :]
                    lsub[dst, :] = ls_ref[rows, :]
                    dks[dst, :] = jnp.zeros((n, LANES), F32)
                    dvs[dst, :] = jnp.zeros((n, LANES), F32)

            def one(off, i):
                W, q0, k0, valid = _band_window(i, L)
                qrows, krows = pl.ds(off + q0, TQ), pl.ds(off + k0, W)
                q2 = _stack_heads(qs[qrows, :], lo)
                do2 = _stack_heads(dos[qrows, :], lo)
                kk, vv = ks[krows, :], vs[krows, :]
                lse_b, dd_b = lsub[qrows, :], dsub[qrows, :]
                lse2 = jnp.concatenate([lse_b[:, 0:1], lse_b[:, HD:HD + 1]], axis=0)
                dd2 = jnp.concatenate([dd_b[:, 0:1], dd_b[:, HD:HD + 1]], axis=0)
                sc = jnp.where(valid, _dot_nt(q2, kk), NEG_INF)
                p = jnp.exp(sc - lse2)
                ds = (p * (_dot_nt(do2, vv) - dd2)).astype(BF16)
                dqs[qrows, :] = _unstack_heads(_dot(ds, kk), lo)
                dks[krows, :] = dks[krows, :] + _dot_tn(ds, q2)
                dvs[krows, :] = dvs[krows, :] + _dot_tn(p.astype(BF16), do2)

            def unstage(r, off):
                for c0 in range(0, L, CH):
                    n = min(CH, L)
                    rows = _sub_rows(r, d, c0, n)
                    src = pl.ds(off + c0, n)
                    qx[rows, :] = dqs[src, :]
                    kx[rows, :] = dks[src, :]
                    dvn[rows, :] = dvs[src, :]

            def step(t, _):
                for u in range(ru):
                    stage(t * ru + u, u * L)
                _for_blocks(L // TQ // nb, lambda j: [one(u * L, j * nb + b) for u in range(ru) for b in range(nb)])
                for u in range(ru):
                    unstage(t * ru + u, u * L)
                return 0

            lax.fori_loop(0, d // ru, step, 0)

        for gi, d in enumerate(DILATIONS):
            pl.when(g == gi)(functools.partial(group, d))

        def emit(i, _):
            rows = pl.ds(pl.multiple_of(i * CH, CH), CH)
            c, s1, s2 = c_ref[rows, :], s1_ref[rows, :], s2_ref[rows, :]
            for (_, w_ref, x, tn_s, rr_s, scale), out, gw_ref in zip(both, (dq_ref, dk_ref), (gqw_ref, gkw_ref)):
                tn = tn_s[rows, :]
                dy = _rope_t(x[rows, :] * scale, c, s1, s2)
                gw_ref[0:1, :] = gw_ref[0:1, :] + jnp.sum(dy * tn, axis=0, keepdims=True)
                dtn = dy * w_ref[...]
                out[rows, :] = (rr_s[rows, :] * (dtn - tn * _head_mean(dtn * tn, e))).astype(BF16)
            dv_ref[rows, :] = dvn[rows, :].astype(BF16)
            return 0

        lax.fori_loop(0, S // CH, emit, 0, unroll=4)

    nat_spec = pl.BlockSpec((S, LANES), lambda hp, g: (0, hp))
    out_spec = pl.BlockSpec((S, LANES), lambda hp, g: (0, g * 4 + hp))
    acc_spec = pl.BlockSpec((8, LANES), lambda hp, g: (0, 0))
    return _call(
        body, sides, name="attn_bwd", grid=(4, 3),
        in_specs=_qk_specs() + _tab_specs() + [_vec_spec(), _vec_spec(), nat_spec, nat_spec, nat_spec],
        out_specs=[out_spec] * 3 + [acc_spec] * 2,
        out_shape=[jax.ShapeDtypeStruct((S, QKV), BF16)] * 3 + [jax.ShapeDtypeStruct((8, LANES), F32)] * 2,
        scratch_shapes=[pltpu.VMEM((S, LANES), BF16)] * 4 + [pltpu.VMEM((S, LANES), F32)] * 13,
        args=(proj, proj, proj, *tabs, qw2, kw2, d_attn, attn, lse))


PADR = 16
CT = 128


def _conv_specs():
    return [pl.BlockSpec((S, CC), lambda i: (0, OFF_CA // CC)), pl.BlockSpec((S, CC), lambda i: (0, OFF_CB // CC))]


NCB = CC // LANES


def _pad_zero(pad):
    for cb in range(NCB):
        pad[cb, 0:PADR, :] = jnp.zeros((PADR, LANES), F32)
        pad[cb, PADR + S:PADR + S + PADR, :] = jnp.zeros((PADR, LANES), F32)


def _pad_store(pad, row0, n, val):
    for cb in range(NCB):
        pad[cb, pl.ds(pl.multiple_of(row0 + PADR, 8), n), :] = val[:, cb * LANES:(cb + 1) * LANES]


def _taps(pad_ref, cb, s0, weights):
    acc = jnp.zeros((CT, LANES), F32)
    for k in range(KW):
        acc = acc + weights[k] * pad_ref[cb, pl.ds(s0 + k + 1, CT), :]
    return acc


def conv_fwd(proj, conv_w, conv_b, ln_w, ln_b):
    def body(a_ref, b_ref, w_ref, cb_ref, lw_ref, lb_ref, c_ref, u3_ref, upad):
        _pad_zero(upad)

        def glu(i, _):
            rows = pl.ds(pl.multiple_of(i * TM, TM), TM)
            _pad_store(upad, i * TM, TM, a_ref[rows, :] * _sigmoid(b_ref[rows, :]))
            return 0

        lax.fori_loop(0, S // TM, glu, 0)

        def chunk(i, _):
            s0 = pl.multiple_of(i * CT, CT)
            for cb in range(CC // LANES):
                cols = slice(cb * LANES, (cb + 1) * LANES)
                w = [w_ref[k:k + 1, cols] for k in range(KW)]
                c_ref[pl.ds(s0, CT), cols] = _taps(upad, cb, s0, w) + cb_ref[:, cols]
            cv = c_ref[pl.ds(s0, CT), :]
            mu = jnp.mean(cv, axis=-1, keepdims=True)
            xc = cv - mu
            rstd = lax.rsqrt(jnp.mean(xc * xc, axis=-1, keepdims=True) + EPS)
            yl = xc * rstd * lw_ref[...] + lb_ref[...]
            u3_ref[pl.ds(s0, CT), :] = (yl * _sigmoid(yl)).astype(BF16)
            return 0

        lax.fori_loop(0, S // CT, chunk, 0)

    vec = pl.BlockSpec((1, CC), lambda i: (0, 0))
    full = pl.BlockSpec((S, CC), lambda i: (0, 0))
    return pl.pallas_call(
        body, name="conv_fwd", grid=(1,),
        in_specs=_conv_specs() + [pl.BlockSpec((KW, CC), lambda i: (0, 0)), vec, vec, vec],
        out_specs=[full, full],
        out_shape=[jax.ShapeDtypeStruct((S, CC), F32), jax.ShapeDtypeStruct((S, CC), BF16)],
        scratch_shapes=[pltpu.VMEM((NCB, S + 2 * PADR, LANES), F32)],
        compiler_params=_cp(dimension_semantics=("arbitrary",)),
    )(proj, proj, conv_w, conv_b, ln_w, ln_b)


def conv_bwd(proj, cpre, d_u3, conv_w, conv_w_rev, ln_w, ln_b, sides=()):
    def body(a_ref, b_ref, c_ref, du3_ref, w_ref, wr_ref, lw_ref, lb_ref,
             dc_ref, gw_ref, gcb_ref, glw_ref, glb_ref, upad, dpad):
        _pad_zero(upad)
        _pad_zero(dpad)
        gw_ref[...] = jnp.zeros_like(gw_ref)

        def ln_bwd(i, carry):
            gcb, glw, glb = carry
            rows = pl.ds(pl.multiple_of(i * TM, TM), TM)
            _pad_store(upad, i * TM, TM, a_ref[rows, :] * _sigmoid(b_ref[rows, :]))
            cv = c_ref[rows, :]
            mu = jnp.mean(cv, axis=-1, keepdims=True)
            xc = cv - mu
            rstd = lax.rsqrt(jnp.mean(xc * xc, axis=-1, keepdims=True) + EPS)
            xh = xc * rstd
            yl = xh * lw_ref[...] + lb_ref[...]
            dyl = du3_ref[rows, :] * _dsilu(yl, _sigmoid(yl))
            dxh = dyl * lw_ref[...]
            dcv = rstd * (dxh - jnp.mean(dxh, axis=-1, keepdims=True)
                          - xh * jnp.mean(dxh * xh, axis=-1, keepdims=True))
            _pad_store(dpad, i * TM, TM, dcv)
            return (gcb + jnp.sum(dcv, axis=0, keepdims=True),
                    glw + jnp.sum(dyl * xh, axis=0, keepdims=True),
                    glb + jnp.sum(dyl, axis=0, keepdims=True))

        z = jnp.zeros((1, CC), F32)
        gcb, glw, glb = lax.fori_loop(0, S // TM, ln_bwd, (z, z, z))
        gcb_ref[...] = gcb
        glw_ref[...] = glw
        glb_ref[...] = glb

        def chunk(i, _):
            s0 = pl.multiple_of(i * CT, CT)
            for cb in range(CC // LANES):
                cols = slice(cb * LANES, (cb + 1) * LANES)
                wr = [wr_ref[k:k + 1, cols] for k in range(KW)]
                du = _taps(dpad, cb, s0, wr)
                dcv = dpad[cb, pl.ds(s0 + PADR, CT), :]
                for k in range(KW):
                    gw_ref[k:k + 1, cols] = gw_ref[k:k + 1, cols] + jnp.sum(
                        upad[cb, pl.ds(s0 + k + 1, CT), :] * dcv, axis=0, keepdims=True)
                av = a_ref[pl.ds(s0, CT), cols]
                sb = _sigmoid(b_ref[pl.ds(s0, CT), cols])
                dc_ref[pl.ds(s0, CT), cols] = (du * sb).astype(BF16)
                dc_ref[pl.ds(s0, CT), slice(CC + cb * LANES, CC + (cb + 1) * LANES)] = (
                    du * av * sb * (1.0 - sb)).astype(BF16)
            return 0

        lax.fori_loop(0, S // CT, chunk, 0)

    vec = pl.BlockSpec((1, CC), lambda i: (0, 0))
    full = pl.BlockSpec((S, CC), lambda i: (0, 0))
    wsp = pl.BlockSpec((KW, CC), lambda i: (0, 0))
    return _call(
        body, sides, name="conv_bwd", grid=(1,),
        in_specs=_conv_specs() + [full, full, wsp, wsp, vec, vec],
        out_specs=[pl.BlockSpec((S, 2 * CC), lambda i: (0, 0)), wsp, vec, vec, vec],
        out_shape=[jax.ShapeDtypeStruct((S, 2 * CC), BF16), jax.ShapeDtypeStruct((KW, CC), F32)]
        + [jax.ShapeDtypeStruct((1, CC), F32)] * 3,
        scratch_shapes=[pltpu.VMEM((NCB, S + 2 * PADR, LANES), F32)] * 2,
        args=(proj, proj, cpre, d_u3, conv_w, conv_w_rev, ln_w, ln_b))


def _gate_specs():
    return [_row(CC, col=OFF_GA // CC + j) for j in range(4)]


def _gates(g_refs, bg_ref):
    ga = _sigmoid(jnp.concatenate([g_refs[0][...], g_refs[1][...]], axis=1) + bg_ref[0:1, :])
    gb = _sigmoid(jnp.concatenate([g_refs[2][...], g_refs[3][...]], axis=1) + bg_ref[1:2, :])
    return ga, gb


def mix_out(x, proj, b_gate, attn, u3, w_o, w_pw, w_out):
    def body(x_ref, g0, g1, g2, g3, bg_ref, at_ref, u3_ref, wo_ref, wp_ref, wout_ref,
             x1_ref, z_ref, ya_ref, yb_ref):
        ga, gb = _gates((g0, g1, g2, g3), bg_ref)
        ya = _dot(at_ref[...], wo_ref[...])
        yb = _dot(u3_ref[...], wp_ref[...])
        z = (ga * ya + gb * yb).astype(BF16)
        ya_ref[...] = ya.astype(BF16)
        yb_ref[...] = yb.astype(BF16)
        z_ref[...] = z
        x1_ref[...] = x_ref[...] + _dot(z, wout_ref[...])

    return pl.pallas_call(
        body, name="mix_out", grid=(S // TM,),
        in_specs=[_row(D)] + _gate_specs() + [_res((2, D)), _row(CC), _row(CC),
                                              _res((CC, D)), _res((CC, D)), _res((D, D))],
        out_specs=[_row(D)] * 4,
        out_shape=[jax.ShapeDtypeStruct((S, D), F32)] + [jax.ShapeDtypeStruct((S, D), BF16)] * 3,
        compiler_params=_cp(dimension_semantics=("arbitrary",)),
    )(x, proj, proj, proj, proj, b_gate, attn, u3, w_o, w_pw, w_out)


def out_bwd(d_x1b, proj, b_gate, ya, yb, w_o, w_pw, w_out, sides=()):
    def body(dx_ref, g0, g1, g2, g3, bg_ref, ya_ref, yb_ref, wo_ref, wp_ref, wout_ref,
             dya_ref, dyb_ref, dgl_ref, dat_ref, du3_ref, gbg_ref):
        @pl.when(pl.program_id(0) == 0)
        def _():
            gbg_ref[...] = jnp.zeros_like(gbg_ref)

        ga, gb = _gates((g0, g1, g2, g3), bg_ref)
        dz = _dot_nt(dx_ref[...], wout_ref[...])
        dya = (dz * ga).astype(BF16)
        dyb = (dz * gb).astype(BF16)
        dgla = dz * ya_ref[...].astype(F32) * ga * (1.0 - ga)
        dglb = dz * yb_ref[...].astype(F32) * gb * (1.0 - gb)
        dya_ref[...] = dya
        dyb_ref[...] = dyb
        dgl_ref[:, 0:D] = dgla.astype(BF16)
        dgl_ref[:, D:2 * D] = dglb.astype(BF16)
        gbg_ref[0:1, :] = gbg_ref[0:1, :] + jnp.sum(dgla, axis=0, keepdims=True)
        gbg_ref[1:2, :] = gbg_ref[1:2, :] + jnp.sum(dglb, axis=0, keepdims=True)
        dat_ref[...] = _dot_nt(dya, wo_ref[...])
        du3_ref[...] = _dot_nt(dyb, wp_ref[...])

    return _call(
        body, sides, name="out_bwd", grid=(S // TM,),
        in_specs=[_row(D)] + _gate_specs() + [_res((2, D)), _row(D), _row(D),
                                              _res((CC, D)), _res((CC, D)), _res((D, D))],
        out_specs=[_row(D), _row(D), _row(2 * D), _row(CC), _row(CC), pl.BlockSpec((2, D), lambda i: (0, 0))],
        out_shape=[jax.ShapeDtypeStruct((S, D), BF16)] * 2 + [jax.ShapeDtypeStruct((S, 2 * D), BF16)]
        + [jax.ShapeDtypeStruct((S, CC), F32)] * 2 + [jax.ShapeDtypeStruct((2, D), F32)],
        args=(d_x1b, proj, proj, proj, proj, b_gate, ya, yb, w_o, w_pw, w_out))


def ffn_in(x1, norm_w, w_ffn_in, sides=()):
    half = FF // 2

    def body(x_ref, nw_ref, w_ref, h_ref, gu_ref, f_ref):
        xv = x_ref[...]
        r = lax.rsqrt(jnp.mean(xv * xv, axis=-1, keepdims=True) + EPS)
        h = (xv * r * nw_ref[...]).astype(BF16)
        h_ref[...] = h
        for j in range(2):
            gt = _dot_nt(h, w_ref[j * half:(j + 1) * half, :])
            up = _dot_nt(h, w_ref[FF + j * half:FF + (j + 1) * half, :])
            gu_ref[:, j * half:(j + 1) * half] = gt.astype(BF16)
            gu_ref[:, FF + j * half:FF + (j + 1) * half] = up.astype(BF16)
            f_ref[:, j * half:(j + 1) * half] = (gt * _sigmoid(gt) * up).astype(BF16)

    return _call(
        body, sides, name="ffn_in", grid=(S // TM,),
        in_specs=[_row(D), _res((1, D)), _res((2 * FF, D))],
        out_specs=[_row(D), _row(2 * FF), _row(FF)],
        out_shape=[jax.ShapeDtypeStruct((S, D), BF16), jax.ShapeDtypeStruct((S, 2 * FF), BF16),
                   jax.ShapeDtypeStruct((S, FF), BF16)],
        args=(x1, norm_w, w_ffn_in))


def ffn_out_loss(x1, f, w_ffn_out, target):
    def body(x_ref, f_ref, w_ref, t_ref, dy_ref, dyb_ref, sq_ref):
        @pl.when(pl.program_id(0) == 0)
        def _():
            sq_ref[...] = jnp.zeros_like(sq_ref)

        diff = x_ref[...] + _dot(f_ref[...], w_ref[...]) - t_ref[...]
        dy = diff * (1.0 / D)
        dy_ref[...] = dy
        dyb_ref[...] = dy.astype(BF16)
        sq_ref[...] = sq_ref[...] + jnp.sum((diff * diff).reshape(TM // 8, 8, D), axis=0)

    return pl.pallas_call(
        body, name="ffn_out_loss", grid=(S // TM,),
        in_specs=[_row(D), _row(FF), _res((FF, D)), _row(D)],
        out_specs=[_row(D), _row(D), pl.BlockSpec((8, D), lambda i: (0, 0))],
        out_shape=[jax.ShapeDtypeStruct((S, D), F32), jax.ShapeDtypeStruct((S, D), BF16),
                   jax.ShapeDtypeStruct((8, D), F32)],
        compiler_params=_cp(dimension_semantics=("arbitrary",)),
    )(x1, f, w_ffn_out, target)


def _rms_bwd(xv, nw, dh):
    r = lax.rsqrt(jnp.mean(xv * xv, axis=-1, keepdims=True) + EPS)
    xn = xv * r
    dxn = dh * nw
    dx = r * (dxn - xn * jnp.mean(dxn * xn, axis=-1, keepdims=True))
    return dx, dh * xn


def ffn_bwd(dy, dyb, gu, x1, norm_w, w_ffn_in, w_ffn_out, sides=()):
    def body(dy_ref, dyb_ref, gu_ref, x_ref, nw_ref, wi_ref, wo_ref, dgu_ref, dx_ref, dxb_ref, gn_ref):
        @pl.when(pl.program_id(0) == 0)
        def _():
            gn_ref[...] = jnp.zeros_like(gn_ref)

        df = _dot_nt(dyb_ref[...], wo_ref[...])
        gt = gu_ref[:, 0:FF].astype(F32)
        up = gu_ref[:, FF:2 * FF].astype(F32)
        sg = _sigmoid(gt)
        dgt = (df * up * _dsilu(gt, sg)).astype(BF16)
        dup = (df * gt * sg).astype(BF16)
        dgu_ref[:, 0:FF] = dgt
        dgu_ref[:, FF:2 * FF] = dup
        dh = _dot(dgt, wi_ref[0:FF, :]) + _dot(dup, wi_ref[FF:2 * FF, :])
        dxn, gw = _rms_bwd(x_ref[...], nw_ref[...], dh)
        dx = dy_ref[...] + dxn
        dx_ref[...] = dx
        dxb_ref[...] = dx.astype(BF16)
        gn_ref[...] = gn_ref[...] + jnp.sum(gw, axis=0, keepdims=True)

    return _call(
        body, sides, name="ffn_bwd", grid=(S // TM,),
        in_specs=[_row(D), _row(D), _row(2 * FF), _row(D), _res((1, D)), _res((2 * FF, D)), _res((FF, D))],
        out_specs=[_row(2 * FF), _row(D), _row(D), pl.BlockSpec((1, D), lambda i: (0, 0))],
        out_shape=[jax.ShapeDtypeStruct((S, 2 * FF), BF16), jax.ShapeDtypeStruct((S, D), F32),
                   jax.ShapeDtypeStruct((S, D), BF16), jax.ShapeDtypeStruct((1, D), F32)],
        args=(dy, dyb, gu, x1, norm_w, w_ffn_in, w_ffn_out))


def in_bwd(d_q, d_k, d_v, d_conv, d_gl, w_in, x, d_x1, norm_w, sides=()):
    segs = ((OFF_Q, QKV), (OFF_K, QKV), (OFF_V, QKV), (OFF_CA, 2 * CC), (OFF_GA, 2 * D))

    def body(dq_ref, dk_ref, dv_ref, dc_ref, dg_ref, w_ref, x_ref, dx1_ref, nw_ref, gx_ref, gn_ref):
        @pl.when(pl.program_id(0) == 0)
        def _():
            gn_ref[...] = jnp.zeros_like(gn_ref)

        dh = jnp.zeros((TM, D), F32)
        for ref, (off, width) in zip((dq_ref, dk_ref, dv_ref, dc_ref, dg_ref), segs):
            dh = dh + _dot(ref[...], w_ref[off:off + width, :])
        dxn, gw = _rms_bwd(x_ref[...], nw_ref[...], dh)
        gx_ref[...] = dx1_ref[...] + dxn
        gn_ref[...] = gn_ref[...] + jnp.sum(gw, axis=0, keepdims=True)

    return _call(
        body, sides, name="in_bwd", grid=(S // TM,),
        in_specs=[_row(QKV)] * 3 + [_row(2 * CC), _row(2 * D), _res((INW, D)), _row(D), _row(D), _res((1, D))],
        out_specs=[_row(D), pl.BlockSpec((1, D), lambda i: (0, 0))],
        out_shape=[jax.ShapeDtypeStruct((S, D), F32), jax.ShapeDtypeStruct((1, D), F32)],
        args=(d_q, d_k, d_v, d_conv, d_gl, w_in, x, d_x1, norm_w))


def mm_tn(name, a, b, tm, tn, sides=()):
    M, N = a.shape[1], b.shape[1]

    def body(a_ref, b_ref, o_ref):
        o_ref[...] = _dot_tn(a_ref[...], b_ref[...])

    res = _call(
        body, sides, name=name, grid=(M // tm, N // tn),
        in_specs=[pl.BlockSpec((S, tm), lambda i, j: (0, i)), pl.BlockSpec((S, tn), lambda i, j: (0, j))],
        out_specs=[pl.BlockSpec((tm, tn), lambda i, j: (i, j))],
        out_shape=[jax.ShapeDtypeStruct((M, N), F32)],
        args=(a, b))
    return (res[0][0], res[1]) if sides else res[0]


GW_IN_TN = 512
GW_IN_PARTS = 4


def gw_in_t(name, h, d_segs, col_half, sides=()):
    tn, hw = GW_IN_TN, D // GW_IN_PARTS
    starts, t0 = [], 0
    for seg in d_segs:
        starts.append(t0)
        t0 += seg.shape[1] // tn
    ntiles = [seg.shape[1] // tn for seg in d_segs]

    def body(h_ref, *refs):
        a_refs, o_ref = refs[:-1], refs[-1]
        n = pl.program_id(0)
        for a_ref, st, nt in zip(a_refs, starts, ntiles):
            @pl.when((n >= st) & (n < st + nt))
            def _(a_ref=a_ref):
                o_ref[...] = _dot_tn(a_ref[...], h_ref[...])

    def seg_spec(st, nt):
        return pl.BlockSpec((S, tn), lambda n: (0, jnp.clip(n - st, 0, nt - 1)))

    res = _call(
        body, sides, name=name, grid=(INW // tn,),
        in_specs=[pl.BlockSpec((S, hw), lambda n: (0, col_half))] + [seg_spec(st, nt) for st, nt in zip(starts, ntiles)],
        out_specs=[pl.BlockSpec((tn, hw), lambda n: (n, 0))],
        out_shape=[jax.ShapeDtypeStruct((INW, hw), F32)],
        args=(h, *d_segs))
    return (res[0][0], res[1]) if sides else res[0]


def _place():
    x, y, c = lax.axis_index("x"), lax.axis_index("y"), lax.axis_index("c")
    chips = [(1 - x, y), (x, 1 - y), (1 - x, 1 - y)]
    return x, y, c, chips


def _sems(n):
    return pltpu.SemaphoreType.DMA((n,))


def _remote(src, dst, send, recv, k, to):
    return pltpu.make_async_remote_copy(src_ref=src, dst_ref=dst, send_sem=send.at[k], recv_sem=recv.at[k],
                                        device_id=to, device_id_type=MESH)


def _cast_rows(dst, src, cols=slice(None)):
    rows = src.shape[0]
    step = next((s for s in (128, 64, 32, 16) if rows % s == 0), rows)
    for r0 in range(0, rows, step):
        dst[r0:r0 + step, cols] = src[r0:r0 + step, :].astype(dst.dtype)


def comm_only(name, sides):
    def body():
        pass

    return _call(body, sides, name=name, grid=(1,), in_specs=[], out_specs=[], out_shape=[], args=())[1]


def ag_blocks(shard, dtype):
    R, W = shard.shape

    def copy(outs, scr, k, block, to, src=None):
        dst = outs[0].at[block]
        return _remote(dst if src is None else src, dst, scr[1], scr[2], k, to)

    def local(outs, scr, me):
        return pltpu.make_async_copy(scr[0], outs[0].at[me], scr[3].at[0])

    def start(ins, outs, scr):
        x, y, c, chips = _place()
        me = 4 * x + 2 * y + c
        _cast_rows(scr[0], ins[0])
        local(outs, scr, me).start()
        copy(outs, scr, 0, me, (x, y, 1 - c), src=scr[0]).start()
        for j, (cx, cy) in enumerate(chips):
            copy(outs, scr, 1 + j, me, (cx, cy, c), src=scr[0]).start()

    def finish(ins, outs, scr):
        x, y, c, chips = _place()
        me, sib = 4 * x + 2 * y + c, (x, y, 1 - c)
        passed = []
        for j, (cx, cy) in enumerate(chips):
            theirs = 4 * cx + 2 * cy + c
            copy(outs, scr, 1 + j, theirs, (x, y, c)).wait_recv()
            fwd = copy(outs, scr, 4 + j, theirs, sib)
            fwd.start()
            passed.append(fwd)
        copy(outs, scr, 0, 4 * x + 2 * y + 1 - c, (x, y, c)).wait_recv()
        for j, (cx, cy) in enumerate(chips):
            copy(outs, scr, 4 + j, 4 * cx + 2 * cy + 1 - c, (x, y, c)).wait_recv()
        copy(outs, scr, 0, me, sib, src=scr[0]).wait_send()
        for j, (cx, cy) in enumerate(chips):
            copy(outs, scr, 1 + j, me, (cx, cy, c), src=scr[0]).wait_send()
        for fwd in passed:
            fwd.wait_send()
        local(outs, scr, me).wait()

    return Side((shard,), (VMEM,), (jax.ShapeDtypeStruct((NDEV, R, W), dtype),),
                (pltpu.VMEM((R, W), dtype), _sems(7), _sems(7), _sems(1)), start, finish)


def ag_blocks_relay(shard, dtype):
    R, W = shard.shape
    half = R // 2

    def copy(outs, scr, k, block, to, src=None, rows=None):
        dst = outs[0].at[block] if rows is None else outs[0].at[block, pl.ds(rows * half, half), :]
        return _remote(dst if src is None else src, dst, scr[1], scr[2], k, to)

    def local(outs, scr, me):
        return pltpu.make_async_copy(scr[0], outs[0].at[me], scr[3].at[0])

    def own(outs, scr):
        x, y, c, _ = _place()
        me = 4 * x + 2 * y + c
        return [copy(outs, scr, k, me, to, src=scr[0])
                for k, to in enumerate([(x, y, 1 - c), (1 - x, y, c), (x, 1 - y, c)])]

    def start(ins, outs, scr):
        x, y, c, _ = _place()
        _cast_rows(scr[0], ins[0])
        local(outs, scr, 4 * x + 2 * y + c).start()
        for cp in own(outs, scr):
            cp.start()

    def passed_on(outs, scr):
        x, y, c, _ = _place()
        sib, xn, yn = (x, y, 1 - c), (1 - x, y, c), (x, 1 - y, c)
        b_xn, b_yn, b_dg = 4 * (1 - x) + 2 * y + c, 4 * x + 2 * (1 - y) + c, 4 * (1 - x) + 2 * (1 - y) + c
        near = [copy(outs, scr, 5, b_xn, yn, rows=0), copy(outs, scr, 3, b_xn, sib),
                copy(outs, scr, 6, b_yn, xn, rows=1), copy(outs, scr, 4, b_yn, sib)]
        far = [copy(outs, scr, 7, b_dg, sib, rows=0), copy(outs, scr, 8, b_dg, sib, rows=1)]
        return (b_xn, b_yn, b_dg), near, far

    def mid(ins, outs, scr):
        x, y, c, _ = _place()
        (b_xn, b_yn, _), near, _ = passed_on(outs, scr)
        copy(outs, scr, 1, b_xn, (x, y, c)).wait_recv()
        near[0].start()
        near[1].start()
        copy(outs, scr, 2, b_yn, (x, y, c)).wait_recv()
        near[2].start()
        near[3].start()

    def finish(ins, outs, scr):
        x, y, c, _ = _place()
        here = (x, y, c)
        (b_xn, b_yn, b_dg), near, far = passed_on(outs, scr)
        copy(outs, scr, 5, b_dg, here, rows=0).wait_recv()
        far[0].start()
        copy(outs, scr, 6, b_dg, here, rows=1).wait_recv()
        far[1].start()
        flip = 1 - 2 * c
        copy(outs, scr, 0, 4 * x + 2 * y + 1 - c, here).wait_recv()
        copy(outs, scr, 3, b_xn + flip, here).wait_recv()
        copy(outs, scr, 4, b_yn + flip, here).wait_recv()
        copy(outs, scr, 7, b_dg + flip, here, rows=0).wait_recv()
        copy(outs, scr, 8, b_dg + flip, here, rows=1).wait_recv()
        for cp in own(outs, scr) + near + far:
            cp.wait_send()
        local(outs, scr, 4 * x + 2 * y + c).wait()

    return Side((shard,), (VMEM,), (jax.ShapeDtypeStruct((NDEV, R, W), dtype),),
                (pltpu.VMEM((R, W), dtype), _sems(9), _sems(9), _sems(1)), start, finish, mid)


def ag_cols(shard):
    K, C = shard.shape
    half, w2 = K // 2, 2 * C

    def win(out, rows_c, chip):
        return out.at[pl.ds(pl.multiple_of(rows_c * half, 16), half), pl.ds(pl.multiple_of(chip * w2, LANES), w2)]

    def ici(outs, scr, j, to, c, k):
        slab, send, recv = scr[2], scr[5], scr[6]
        return _remote(slab.at[pl.ds(pl.multiple_of(c * half, 16), half), :], win(outs[0], c, k), send, recv, j, to)

    def local(outs, scr, k):
        return pltpu.make_async_copy(scr[2], outs[0].at[:, pl.ds(pl.multiple_of(k * w2, LANES), w2)], scr[7].at[0])

    def start(ins, outs, scr):
        stage, inbox, slab, xs, xr = scr[:5]
        x, y, c, chips = _place()
        k = 2 * x + y
        _cast_rows(stage, ins[0])
        swap = _remote(stage, inbox, xs, xr, 0, (x, y, 1 - c))
        swap.start()
        for cc in range(2):
            @pl.when(c == cc)
            def _(cc=cc):
                _cast_rows(slab, stage, slice(cc * C, (cc + 1) * C))
        swap.wait()
        for cc in range(2):
            @pl.when(c == cc)
            def _(cc=cc):
                _cast_rows(slab, inbox, slice((1 - cc) * C, (2 - cc) * C))
        local(outs, scr, k).start()
        for j, (cx, cy) in enumerate(chips):
            ici(outs, scr, j, (cx, cy, c), c, k).start()

    def finish(ins, outs, scr):
        send, recv = scr[5], scr[6]
        x, y, c, chips = _place()
        k, sib = 2 * x + y, (x, y, 1 - c)
        passed = []
        for j, (cx, cy) in enumerate(chips):
            w = win(outs[0], c, 2 * cx + cy)
            _remote(w, w, send, recv, j, sib).wait_recv()
            fwd = _remote(w, w, send, recv, 3 + j, sib)
            fwd.start()
            passed.append(fwd)
        for j, (cx, cy) in enumerate(chips):
            w = win(outs[0], 1 - c, 2 * cx + cy)
            _remote(w, w, send, recv, 3 + j, sib).wait_recv()
        for j, (cx, cy) in enumerate(chips):
            ici(outs, scr, j, (cx, cy, c), c, k).wait_send()
        for fwd in passed:
            fwd.wait_send()
        local(outs, scr, k).wait()

    return Side((shard,), (VMEM,), (jax.ShapeDtypeStruct((K, NDEV * C), BF16),),
                (pltpu.VMEM((K, C), BF16), pltpu.VMEM((K, C), BF16), pltpu.VMEM((K, w2), BF16),
                 _sems(1), _sems(1), _sems(6), _sems(6), _sems(1)), start, finish)


def copies_side(args, out_shape, n_copies, plan):
    def copies(ins, outs, scr):
        return [_remote(s_, d_, scr[0], scr[1], i, to) for i, (s_, d_, to) in enumerate(plan(ins, outs))]

    def start(ins, outs, scr):
        for cp in copies(ins, outs, scr):
            cp.start()

    def finish(ins, outs, scr):
        for cp in copies(ins, outs, scr):
            cp.wait()

    return Side(tuple(args), (ANY,) * len(args), tuple(out_shape), (_sems(n_copies), _sems(n_copies)), start, finish)


def rs_to_sibling(grads):
    out_shape = [jax.ShapeDtypeStruct((4,) + g.shape[1:] if kind == "rows" else (g.shape[0] // 2, g.shape[1]), F32)
                 for kind, g in grads]

    def plan(ins, outs):
        x, y, c, _ = _place()
        sib, res = (x, y, 1 - c), []
        for (kind, _), g, r in zip(grads, ins, outs):
            if kind == "rows":
                res += [(g.at[2 * k + 1 - c], r.at[k], sib) for k in range(4)]
            else:
                half = g.shape[0] // 2
                res.append((g.at[pl.ds(pl.multiple_of((1 - c) * half, 8), half), :], r, sib))
        return res

    return copies_side([g for _, g in grads], out_shape, sum(4 if kind == "rows" else 1 for kind, _ in grads), plan)


def rs_to_chips(parts):
    out_shape = [jax.ShapeDtypeStruct((3,) + p.shape[1:] if kind == "rows" else (3, p.shape[0], p.shape[1] // 4), BF16)
                 for kind, p in parts]

    def plan(ins, outs):
        x, y, c, chips = _place()
        res = []
        for (kind, _), p, r in zip(parts, ins, outs):
            for j, (cx, cy) in enumerate(chips):
                if kind == "rows":
                    src = p.at[2 * cx + cy]
                else:
                    w2 = p.shape[1] // 4
                    src = p.at[:, pl.ds(pl.multiple_of((2 * cx + cy) * w2, LANES), w2)]
                res.append((src, r.at[j], (cx, cy, c)))
        return res

    return copies_side([p for _, p in parts], out_shape, 3 * len(parts), plan)


def rs_swap_halves(theirs):
    def plan(ins, outs):
        x, y, c, _ = _place()
        return [(t, r, (x, y, 1 - c)) for t, r in zip(ins, outs)]

    return copies_side(theirs, [jax.ShapeDtypeStruct(t.shape, F32) for t in theirs], len(theirs), plan)


def _row_tiles(rows):
    return 2 if rows % 32 == 0 and rows >= 512 else 1


def chip_sum(name, grad, recv, c_idx, chip_idx):
    _, R, C = grad.shape
    nt = 1
    tr = R // nt

    def body(s_ref, g_ref, r_ref, p_ref, own_ref):
        k = pl.program_id(1)
        tot = g_ref[0] + r_ref[0]
        p_ref[0] = tot.astype(BF16)

        @pl.when(k == s_ref[1])
        def _():
            own_ref[...] = tot

    grid_spec = pltpu.PrefetchScalarGridSpec(
        num_scalar_prefetch=1, grid=(nt, 4),
        in_specs=[pl.BlockSpec((1, tr, C), lambda i, k, s: (2 * k + s[0], i, 0)),
                  pl.BlockSpec((1, tr, C), lambda i, k, s: (k, i, 0))],
        out_specs=[pl.BlockSpec((1, tr, C), lambda i, k, s: (k, i, 0)),
                   pl.BlockSpec((tr, C), lambda i, k, s: (i, 0))])
    return pl.pallas_call(
        body, name=name, grid_spec=grid_spec,
        out_shape=[jax.ShapeDtypeStruct((4, R, C), BF16), jax.ShapeDtypeStruct((R, C), F32)],
        compiler_params=_cp(dimension_semantics=("arbitrary", "arbitrary")),
    )(jnp.stack([c_idx, chip_idx]), grad, recv)


def _half_tiles(half):
    return 2 if half >= 512 else 1


def chip_sum_cols(name, grad, recv, c_idx, chip_idx):
    K, W = grad.shape
    half, w2 = K // 2, W // 4
    nt = _half_tiles(half)
    tr = half // nt

    def body(s_ref, g_ref, r_ref, p_ref, own_ref):
        tot = g_ref[...] + r_ref[...]
        p_ref[...] = tot.astype(BF16)

        @pl.when(pl.program_id(1) == s_ref[1])
        def _():
            own_ref[...] = tot

    grid_spec = pltpu.PrefetchScalarGridSpec(
        num_scalar_prefetch=1, grid=(nt, 4),
        in_specs=[pl.BlockSpec((tr, w2), lambda i, k, s: (s[0] * nt + i, k)),
                  pl.BlockSpec((tr, w2), lambda i, k, s: (i, k))],
        out_specs=[pl.BlockSpec((tr, w2), lambda i, k, s: (i, k)),
                   pl.BlockSpec((tr, w2), lambda i, k, s: (i, 0))])
    return pl.pallas_call(
        body, name=name, grid_spec=grid_spec,
        out_shape=[jax.ShapeDtypeStruct((half, W), BF16), jax.ShapeDtypeStruct((half, w2), F32)],
        compiler_params=_cp(dimension_semantics=("arbitrary", "arbitrary")),
    )(jnp.stack([c_idx, chip_idx]), grad, recv)


def col_final(name, own, recv, c_idx):
    half, w2 = own.shape
    C = w2 // 2
    nt = _half_tiles(half)
    tr = half // nt

    def body(s_ref, o_ref, r_ref, mine_ref, theirs_ref, t_ref):
        t_ref[...] = o_ref[...] + r_ref[0].astype(F32) + r_ref[1].astype(F32) + r_ref[2].astype(F32)
        for cc in range(2):
            @pl.when(s_ref[0] == cc)
            def _(cc=cc):
                mine_ref[...] = t_ref[:, cc * C:(cc + 1) * C]
                theirs_ref[...] = t_ref[:, (1 - cc) * C:(2 - cc) * C]

    grid_spec = pltpu.PrefetchScalarGridSpec(
        num_scalar_prefetch=1, grid=(nt,),
        in_specs=[pl.BlockSpec((tr, w2), lambda i, s: (i, 0)), pl.BlockSpec((3, tr, w2), lambda i, s: (0, i, 0))],
        out_specs=[pl.BlockSpec((tr, C), lambda i, s: (i, 0))] * 2,
        scratch_shapes=[pltpu.VMEM((tr, w2), F32)])
    return pl.pallas_call(
        body, name=name, grid_spec=grid_spec, out_shape=[jax.ShapeDtypeStruct((half, C), F32)] * 2,
        compiler_params=_cp(dimension_semantics=("arbitrary",)),
    )(jnp.stack([c_idx]), own, recv)


def _adamw(w, g, m, v):
    m2 = ADAM_B1 * m + (1.0 - ADAM_B1) * g
    v2 = ADAM_B2 * v + (1.0 - ADAM_B2) * (g * g)
    m_hat = m2 / (1.0 - ADAM_B1 ** ADAM_STEP)
    v_hat = v2 / (1.0 - ADAM_B2 ** ADAM_STEP)
    delta = -ADAM_LR * (m_hat / (jnp.sqrt(v_hat) + ADAM_EPS) + ADAM_WD * w)
    return delta, m2, v2


def shard_adam(name, owns, recvs, w, m, v):
    n = len(owns)
    R, Cp = owns[0].shape
    nt = _row_tiles(R)
    tr = R // nt

    def body(*refs):
        o_refs, r_refs = refs[:n], refs[n:2 * n]
        w_ref, m_ref, v_ref, g_ref, d_ref, nm_ref, nv_ref = refs[2 * n:]
        g = None
        for k in range(n):
            gk = o_refs[k][...] + r_refs[k][0].astype(F32) + r_refs[k][1].astype(F32) + r_refs[k][2].astype(F32)
            g = gk if g is None else jnp.where(pl.program_id(0) == k, gk, g)
        delta, m2, v2 = _adamw(w_ref[...], g, m_ref[...], v_ref[...])
        g_ref[...] = g
        d_ref[...] = delta
        nm_ref[...] = m2
        nv_ref[...] = v2

    part = pl.BlockSpec((tr, Cp), lambda k, i: (i, 0))
    part3 = pl.BlockSpec((3, tr, Cp), lambda k, i: (0, i, 0))
    tile = pl.BlockSpec((tr, Cp), lambda k, i: (i, k))
    return pl.pallas_call(
        body, name=name, grid=(n, nt),
        in_specs=[part] * n + [part3] * n + [tile, tile, tile],
        out_specs=[tile] * 4, out_shape=[jax.ShapeDtypeStruct((R, n * Cp), F32)] * 4,
        compiler_params=_cp(dimension_semantics=("arbitrary", "arbitrary")),
    )(*owns, *recvs, w, m, v)


def adam_cols(name, mine, recv, w, m, v, c_idx):
    half, C = mine.shape
    nt = _half_tiles(half)
    tr = half // nt

    def body(s_ref, a_ref, b_ref, w_ref, m_ref, v_ref, g_ref, d_ref, nm_ref, nv_ref):
        g = jnp.where(pl.program_id(0) == s_ref[0], a_ref[...], b_ref[...])
        delta, m2, v2 = _adamw(w_ref[...], g, m_ref[...], v_ref[...])
        g_ref[...] = g
        d_ref[...] = delta
        nm_ref[...] = m2
        nv_ref[...] = v2

    part = pl.BlockSpec((tr, C), lambda hh, i, s: (i, 0))
    tile = pl.BlockSpec((tr, C), lambda hh, i, s: (hh * nt + i, 0))
    grid_spec = pltpu.PrefetchScalarGridSpec(
        num_scalar_prefetch=1, grid=(2, nt), in_specs=[part, part, tile, tile, tile], out_specs=[tile] * 4)
    return pl.pallas_call(
        body, name=name, grid_spec=grid_spec, out_shape=[jax.ShapeDtypeStruct((2 * half, C), F32)] * 4,
        compiler_params=_cp(dimension_semantics=("arbitrary", "arbitrary")),
    )(jnp.stack([c_idx]), mine, recv, w, m, v)


ROW_N1, ROW_N2, ROW_BG, ROW_QN, ROW_KN, ROW_CB, ROW_LW, ROW_LB, ROW_CW = 0, 1, 2, 4, 5, 6, 7, 8, 9
PACK_ROWS = 40
SMALL = ("norm1_w", "norm2_w", "b_gate", "q_norm_w", "k_norm_w", "conv_b", "conv_ln_w", "conv_ln_b", "conv_w")


def small_sync_adam(g, w, m, v, sq, sides=()):
    ns = len(SMALL)

    def body(*refs):
        gi = dict(zip(SMALL, refs[:ns]))
        wi = dict(zip(SMALL, refs[ns:2 * ns]))
        mi = dict(zip(SMALL, refs[2 * ns:3 * ns]))
        vi = dict(zip(SMALL, refs[3 * ns:4 * ns]))
        sq_ref = refs[4 * ns]
        outs = refs[4 * ns + 1:8 * ns + 1]
        loss_ref = refs[8 * ns + 1]
        pack, recv, tot, send_sems, recv_sems = refs[8 * ns + 2:]
        x, y, c, _ = _place()
        me = 4 * x + 2 * y + c

        pack[...] = jnp.zeros_like(pack)
        pack[ROW_KN:ROW_KN + 1, LANES:2 * LANES] = jnp.full((1, LANES), (0.5 / D) * jnp.sum(sq_ref[...]), F32)
        pack[ROW_N1:ROW_N1 + 1, :] = gi["norm1_w"][...]
        pack[ROW_N2:ROW_N2 + 1, :] = gi["norm2_w"][...]
        pack[ROW_BG:ROW_BG + 2, :] = gi["b_gate"][...]
        pack[ROW_QN:ROW_QN + 1, 0:HD] = gi["q_norm_w"][...]
        pack[ROW_KN:ROW_KN + 1, 0:HD] = gi["k_norm_w"][...]
        pack[ROW_CB:ROW_CB + 1, 0:CC] = gi["conv_b"][...]
        pack[ROW_LW:ROW_LW + 1, 0:CC] = gi["conv_ln_w"][...]
        pack[ROW_LB:ROW_LB + 1, 0:CC] = gi["conv_ln_b"][...]
        pack[ROW_CW:ROW_CW + KW, 0:CC] = gi["conv_w"][...]

        copies = []
        for k in range(1, NDEV):
            peer = (x ^ (k >> 2), y ^ ((k >> 1) & 1), c ^ (k & 1))
            cp = pltpu.make_async_remote_copy(
                src_ref=pack, dst_ref=recv.at[me], send_sem=send_sems.at[k - 1], recv_sem=recv_sems.at[k - 1],
                device_id=peer, device_id_type=MESH)
            cp.start()
            copies.append(cp)
        recv[me] = pack[...]
        for cp in copies:
            cp.wait()
        acc = recv[0]
        for p in range(1, NDEV):
            acc = acc + recv[p]
        tot[...] = acc

        def shard_grad(name):
            if name == "b_gate":
                return tot[ROW_BG:ROW_BG + 2, pl.ds(pl.multiple_of(me * LANES, LANES), LANES)]
            if name == "conv_w":
                win = tot[ROW_CW:ROW_CW + KW, pl.ds(pl.multiple_of((me // 2) * LANES, LANES), LANES)]
                return jnp.where(me % 2 == 1, win[:, HD:LANES], win[:, 0:HD])
            row = {"norm1_w": ROW_N1, "norm2_w": ROW_N2, "q_norm_w": ROW_QN, "k_norm_w": ROW_KN,
                   "conv_b": ROW_CB, "conv_ln_w": ROW_LW, "conv_ln_b": ROW_LB}[name]
            return tot[row:row + 1, 0:wi[name].shape[1]]

        for i, name in enumerate(SMALL):
            gr = shard_grad(name)
            delta, m2, v2 = _adamw(wi[name][...], gr, mi[name][...], vi[name][...])
            outs[4 * i][...] = gr
            outs[4 * i + 1][...] = delta
            outs[4 * i + 2][...] = m2
            outs[4 * i + 3][...] = v2
        loss_ref[...] = tot[ROW_KN:ROW_KN + 1, LANES:2 * LANES]

    out_shape = []
    for name in SMALL:
        out_shape += [jax.ShapeDtypeStruct(w[name].shape, F32)] * 4
    out_shape.append(jax.ShapeDtypeStruct((1, LANES), F32))
    args = [g[k] for k in SMALL] + [w[k] for k in SMALL] + [m[k] for k in SMALL] + [v[k] for k in SMALL] + [sq]
    res = _call(
        body, sides, name="small_sync_adam", grid=(1,), in_specs=[VMEM] * len(args),
        out_specs=[VMEM] * len(out_shape), out_shape=out_shape,
        scratch_shapes=[pltpu.VMEM((PACK_ROWS, D), F32), pltpu.VMEM((NDEV, PACK_ROWS, D), F32),
                        pltpu.VMEM((PACK_ROWS, D), F32), _sems(NDEV - 1), _sems(NDEV - 1)],
        args=args)
    res, side_outs = res if sides else (res, None)
    out = {name: tuple(res[4 * i:4 * i + 4]) for i, name in enumerate(SMALL)}
    loss = res[4 * ns][0, 0]
    return (out, loss, side_outs) if sides else (out, loss)


MATS = ("w_in", "w_o_attn", "w_pw_conv", "w_out", "w_ffn_in", "w_ffn_out")
TRANSPOSED = ("w_in", "w_ffn_in")
WEIGHTS = ("norm1_w", "w_in", "b_gate", "q_norm_w", "k_norm_w", "w_o_attn", "conv_w", "conv_b", "conv_ln_w",
           "conv_ln_b", "w_pw_conv", "w_out", "norm2_w", "w_ffn_in", "w_ffn_out")


def _blocks_to_cols(blocks):
    n, R, C = blocks.shape
    return blocks.transpose(1, 0, 2).reshape(R, n * C)


def kernel(x, positions, norm1_w, w_in, b_gate, q_norm_w, k_norm_w, w_o_attn, conv_w, conv_b, conv_ln_w, conv_ln_b, w_pw_conv, w_out, norm2_w, w_ffn_in, w_ffn_out, loss_target, m_norm1_w, m_w_in, m_b_gate, m_q_norm_w, m_k_norm_w, m_w_o_attn, m_conv_w, m_conv_b, m_conv_ln_w, m_conv_ln_b, m_w_pw_conv, m_w_out, m_norm2_w, m_w_ffn_in, m_w_ffn_out, v_norm1_w, v_w_in, v_b_gate, v_q_norm_w, v_k_norm_w, v_w_o_attn, v_conv_w, v_conv_b, v_conv_ln_w, v_conv_ln_b, v_w_pw_conv, v_w_out, v_norm2_w, v_w_ffn_in, v_w_ffn_out):
    w = dict(norm1_w=norm1_w, w_in=w_in, b_gate=b_gate, q_norm_w=q_norm_w, k_norm_w=k_norm_w, w_o_attn=w_o_attn,
             conv_w=conv_w, conv_b=conv_b, conv_ln_w=conv_ln_w, conv_ln_b=conv_ln_b, w_pw_conv=w_pw_conv,
             w_out=w_out, norm2_w=norm2_w, w_ffn_in=w_ffn_in, w_ffn_out=w_ffn_out)
    m = dict(norm1_w=m_norm1_w, w_in=m_w_in, b_gate=m_b_gate, q_norm_w=m_q_norm_w, k_norm_w=m_k_norm_w,
             w_o_attn=m_w_o_attn, conv_w=m_conv_w, conv_b=m_conv_b, conv_ln_w=m_conv_ln_w,
             conv_ln_b=m_conv_ln_b, w_pw_conv=m_w_pw_conv, w_out=m_w_out, norm2_w=m_norm2_w,
             w_ffn_in=m_w_ffn_in, w_ffn_out=m_w_ffn_out)
    v = dict(norm1_w=v_norm1_w, w_in=v_w_in, b_gate=v_b_gate, q_norm_w=v_q_norm_w, k_norm_w=v_k_norm_w,
             w_o_attn=v_w_o_attn, conv_w=v_conv_w, conv_b=v_conv_b, conv_ln_w=v_conv_ln_w,
             conv_ln_b=v_conv_ln_b, w_pw_conv=v_w_pw_conv, w_out=v_w_out, norm2_w=v_norm2_w,
             w_ffn_in=v_w_ffn_in, w_ffn_out=v_w_ffn_out)
    def two_d(t):
        t = {k: (a[0] if a.ndim == 3 else a) for k, a in t.items()}
        return {k: (a.T if k in TRANSPOSED else a) for k, a in t.items()}

    w, m, v = two_d(w), two_d(m), two_d(v)

    x2, target = x[0], loss_target[0]
    c_idx = lax.axis_index("c").astype(jnp.int32)
    chip_idx = (2 * lax.axis_index("x") + lax.axis_index("y")).astype(jnp.int32)
    qw2 = jnp.tile(w["q_norm_w"], (1, 2))
    kw2 = jnp.tile(w["k_norm_w"], (1, 2))

    tabs, ((w_in_blocks,), (bg_blocks,), (cw_blocks,)) = rope_tables(
        positions.reshape(S, 1),
        sides=(ag_blocks_relay(w["w_in"], BF16), ag_blocks(w["b_gate"], F32), ag_blocks(w["conv_w"], F32)))
    w_in_t = w_in_blocks.reshape(INW, D)
    b_gate_f, conv_w_f = _blocks_to_cols(bg_blocks), _blocks_to_cols(cw_blocks)
    (h, proj), ((w_o_f,), (w_pw_f,), (w_out_blocks,)) = in_proj(
        x2, w["norm1_w"], w_in_t, sides=(ag_cols(w["w_o_attn"]), ag_cols(w["w_pw_conv"]), ag_blocks_relay(w["w_out"], BF16)))
    w_out_f = w_out_blocks.reshape(D, D)
    (attn, lse), ((w_ffn_in_blocks,),) = attn_fwd(proj, tabs, qw2, kw2, sides=(ag_blocks_relay(w["w_ffn_in"], BF16),))
    w_ffn_in_t = w_ffn_in_blocks.reshape(2 * FF, D)
    cpre, u3 = conv_fwd(proj, conv_w_f, w["conv_b"], w["conv_ln_w"], w["conv_ln_b"])
    x1, z, ya, yb = mix_out(x2, proj, b_gate_f, attn, u3, w_o_f, w_pw_f, w_out_f)
    (h2, gu, f), ((w_ffn_out_blocks,),) = ffn_in(x1, w["norm2_w"], w_ffn_in_t, sides=(ag_blocks_relay(w["w_ffn_out"], BF16),))
    w_ffn_out_f = w_ffn_out_blocks.reshape(FF, D)
    dy, dyb, sq = ffn_out_loss(x1, f, w_ffn_out_f, target)

    g = {}
    g_ffn_out = mm_tn("gw_ffn_out", f, dyb, FF // 2, D).reshape(NDEV, FF // NDEV, D)
    (d_gu, d_x1, d_x1b, g["norm2_w"]), ((ra_ffn_out,),) = ffn_bwd(
        dy, dyb, gu, x1, w["norm2_w"], w_ffn_in_t, w_ffn_out_f, sides=(rs_to_sibling([("rows", g_ffn_out)]),))
    pb_ffn_out, own_ffn_out = chip_sum("chip_sum_w_ffn_out", g_ffn_out, ra_ffn_out, c_idx, chip_idx)
    g_ffn_in, ((rb_ffn_out,),) = mm_tn("gw_ffn_in", d_gu, h2, FF // 2, D,
                                       sides=(rs_to_chips([("rows", pb_ffn_out)]),))
    g_ffn_in = g_ffn_in.reshape(NDEV, 2 * FF // NDEV, D)
    g_out = mm_tn("gw_out", z, d_x1b, D // 2, D).reshape(NDEV, D // NDEV, D)
    (d_ya, d_yb, d_gl, d_attn, d_u3, g["b_gate"]), ((ra_ffn_in,),) = out_bwd(
        d_x1b, proj, b_gate_f, ya, yb, w_o_f, w_pw_f, w_out_f, sides=(rs_to_sibling([("rows", g_ffn_in)]),))
    pb_ffn_in, own_ffn_in = chip_sum("chip_sum_w_ffn_in", g_ffn_in, ra_ffn_in, c_idx, chip_idx)
    g_w_o = mm_tn("gw_o_attn", attn, d_ya, CC, D)
    g_w_pw = mm_tn("gw_pw_conv", u3, d_yb, CC, D)
    (d_conv, g["conv_w"], g["conv_b"], g["conv_ln_w"], g["conv_ln_b"]), ((ra_out, ra_w_o, ra_w_pw),) = conv_bwd(
        proj, cpre, d_u3, conv_w_f, conv_w_f[::-1], w["conv_ln_w"], w["conv_ln_b"],
        sides=(rs_to_sibling([("rows", g_out), ("cols", g_w_o), ("cols", g_w_pw)]),))
    pb_out, own_out = chip_sum("chip_sum_w_out", g_out, ra_out, c_idx, chip_idx)
    pb_w_o, own_w_o = chip_sum_cols("chip_sum_w_o_attn", g_w_o, ra_w_o, c_idx, chip_idx)
    pb_w_pw, own_w_pw = chip_sum_cols("chip_sum_w_pw_conv", g_w_pw, ra_w_pw, c_idx, chip_idx)
    (d_q, d_k, d_v, gqw, gkw), ((rb_ffn_in, rb_out, rb_w_o, rb_w_pw),) = attn_bwd(
        proj, tabs, qw2, kw2, d_attn, attn, lse,
        sides=(rs_to_chips([("rows", pb_ffn_in), ("rows", pb_out), ("cols", pb_w_o), ("cols", pb_w_pw)]),))
    g["q_norm_w"] = gqw[0:1, 0:HD] + gqw[0:1, HD:LANES]
    g["k_norm_w"] = gkw[0:1, 0:HD] + gkw[0:1, HD:LANES]
    mine_w_o, theirs_w_o = col_final("col_final_w_o_attn", own_w_o, rb_w_o, c_idx)
    mine_w_pw, theirs_w_pw = col_final("col_final_w_pw_conv", own_w_pw, rb_w_pw, c_idx)
    d_segs = (d_q, d_k, d_v, d_conv, d_gl)
    parts, to_sibling, to_chips, owns, from_chips = [], None, None, [], []
    for k in range(GW_IN_PARTS):
        sides = [rs_swap_halves([theirs_w_o, theirs_w_pw])] if k == 0 else []
        sides += [s for s in (to_chips, to_sibling) if s is not None]
        part, outs = gw_in_t("gw_in_%d" % k, h, d_segs, k, sides=tuple(sides))
        if k == 0:
            (rc_w_o, rc_w_pw), outs = outs[0], outs[1:]
        outs = list(outs)
        if to_chips is not None:
            from_chips.append(outs.pop(0)[0])
        if to_sibling is not None:
            pb, own = chip_sum("chip_sum_w_in_%d" % (k - 1), parts[-1], outs.pop(0)[0], c_idx, chip_idx)
            owns.append(own)
            to_chips = rs_to_chips([("rows", pb)])
        else:
            to_chips = None
        parts.append(part.reshape(NDEV, INW // NDEV, D // GW_IN_PARTS))
        to_sibling = rs_to_sibling([("rows", parts[-1])])
    (grad_x, g["norm1_w"]), ((rb_prev,), (ra_last,)) = in_bwd(
        d_q, d_k, d_v, d_conv, d_gl, w_in_t, x2, d_x1, w["norm1_w"], sides=(to_chips, to_sibling))
    from_chips.append(rb_prev)
    pb, own = chip_sum("chip_sum_w_in_%d" % (GW_IN_PARTS - 1), parts[-1], ra_last, c_idx, chip_idx)
    owns.append(own)
    small, loss, ((rb_last,),) = small_sync_adam(g, w, m, v, sq, sides=(rs_to_chips([("rows", pb)]),))
    from_chips.append(rb_last)

    res = {
        "w_in": shard_adam("adam_w_in", owns, from_chips, w["w_in"], m["w_in"], v["w_in"]),
        "w_ffn_in": shard_adam("adam_w_ffn_in", [own_ffn_in], [rb_ffn_in], w["w_ffn_in"], m["w_ffn_in"], v["w_ffn_in"]),
        "w_o_attn": adam_cols("adam_w_o_attn", mine_w_o, rc_w_o, w["w_o_attn"], m["w_o_attn"], v["w_o_attn"], c_idx),
        "w_pw_conv": adam_cols("adam_w_pw_conv", mine_w_pw, rc_w_pw, w["w_pw_conv"], m["w_pw_conv"], v["w_pw_conv"], c_idx),
        "w_out": shard_adam("adam_w_out", [own_out], [rb_out], w["w_out"], m["w_out"], v["w_out"]),
        "w_ffn_out": shard_adam("adam_w_ffn_out", [own_ffn_out], [rb_ffn_out],
                                w["w_ffn_out"], m["w_ffn_out"], v["w_ffn_out"]),
    }
    res = {k: tuple(a.T if k in TRANSPOSED else a for a in r) for k, r in res.items()}
    res.update(small)

    def shaped(name, a):
        return a.reshape((1,) + a.shape) if name in MATS or name in ("b_gate", "conv_w") else a

    outs = [loss, grad_x.reshape(1, S, D)]
    for i in range(4):
        outs += [shaped(k, res[k][i]) for k in WEIGHTS]
    return tuple(outs)
```

```python
import functools
from typing import Callable, NamedTuple, Optional

import numpy as np
import jax
import jax.numpy as jnp
from jax import lax
from jax.experimental import pallas as pl
from jax.experimental.pallas import tpu as pltpu

F32 = jnp.float32
BF16 = jnp.bfloat16

S = 2048
D = 1024
HD = 64
QKV = 1536
CC = 512
KW = 31
FF = 2816
INW = 7680
OFF_Q, OFF_K, OFF_V, OFF_CA, OFF_CB, OFF_GA, OFF_GB = 0, 1536, 3072, 4608, 5120, 5632, 6656
DILATIONS = (1, 4, 16)
HALF_SPAN = 64
EPS = 1e-6
NEG_INF = -1e30
ROPE_THETA = 500000.0
ROT_DIM = 16

ADAM_LR = 0.001
ADAM_B1 = 0.9
ADAM_B2 = 0.999
ADAM_EPS = 1e-08
ADAM_WD = 0.01
ADAM_STEP = 10

NDEV = 8
LANES = 128
TM = 256
TQ = 128
VMEM_LIMIT = 56 * 1024 * 1024
MESH = pl.DeviceIdType.MESH


def _cp(**kw):
    return pltpu.CompilerParams(vmem_limit_bytes=VMEM_LIMIT, **kw)


def _row(width, col=0, tm=TM):
    return pl.BlockSpec((tm, width), lambda i: (i, col))


def _res(shape):
    nd = len(shape)
    return pl.BlockSpec(shape, lambda *_: (0,) * nd, pipeline_mode=pl.Buffered(1))


def _dot(a, b):
    return jnp.dot(a, b, preferred_element_type=F32)


def _dot_nt(a, b):
    return lax.dot_general(a, b, (((1,), (1,)), ((), ())), preferred_element_type=F32)


def _dot_tn(a, b):
    return lax.dot_general(a, b, (((0,), (0,)), ((), ())), preferred_element_type=F32)


def _sigmoid(x):
    return jax.nn.sigmoid(x)


def _dsilu(x, sg):
    return sg * (1.0 + x * (1.0 - sg))


ANY = pl.BlockSpec(memory_space=pl.ANY)
VMEM = pl.BlockSpec(memory_space=pltpu.VMEM)


class Side(NamedTuple):
    args: tuple
    in_specs: tuple
    out_shape: tuple
    scratch: tuple
    start: Callable
    finish: Callable
    mid: Optional[Callable] = None


def _call(body, sides=(), *, name, grid, in_specs, out_specs, out_shape, scratch_shapes=(), args):
    ni, no, ns = len(in_specs), len(out_specs), len(scratch_shapes)
    cnt = [(len(s.args), len(s.out_shape), len(s.scratch)) for s in sides]

    def take(refs, pos, n):
        return refs[pos:pos + n], pos + n

    def full(*refs):
        m_in, pos = take(refs, 0, ni)
        s_in = []
        for a, _, _ in cnt:
            r, pos = take(refs, pos, a)
            s_in.append(r)
        m_out, pos = take(refs, pos, no)
        s_out = []
        for _, o, _ in cnt:
            r, pos = take(refs, pos, o)
            s_out.append(r)
        m_scr, pos = take(refs, pos, ns)
        s_scr = []
        for _, _, c in cnt:
            r, pos = take(refs, pos, c)
            s_scr.append(r)
        if sides:
            first = functools.reduce(jnp.logical_and, [pl.program_id(d) == 0 for d in range(len(grid))])
            last = functools.reduce(jnp.logical_and, [pl.program_id(d) == g - 1 for d, g in enumerate(grid)])

            @pl.when(first)
            def _():
                for s, a, o, c in zip(sides, s_in, s_out, s_scr):
                    s.start(a, o, c)

            steps = int(np.prod(grid))
            mid_step = (2 * steps) // 3
            if steps > 1 and any(s.mid is not None for s in sides):
                step = functools.reduce(lambda acc, d: acc * grid[d] + pl.program_id(d), range(len(grid)), 0)

                @pl.when(step == mid_step)
                def _():
                    for s, a, o, c in zip(sides, s_in, s_out, s_scr):
                        if s.mid is not None:
                            s.mid(a, o, c)

        body(*m_in, *m_out, *m_scr)
        if sides:
            @pl.when(last)
            def _():
                for s, a, o, c in zip(sides, s_in, s_out, s_scr):
                    if s.mid is not None and steps == 1:
                        s.mid(a, o, c)
                    s.finish(a, o, c)

    res = pl.pallas_call(
        full, name=name, grid=grid,
        in_specs=list(in_specs) + [sp for s in sides for sp in s.in_specs],
        out_specs=list(out_specs) + [ANY for s in sides for _ in s.out_shape],
        out_shape=list(out_shape) + [o for s in sides for o in s.out_shape],
        scratch_shapes=list(scratch_shapes) + [c for s in sides for c in s.scratch],
        compiler_params=_cp(dimension_semantics=("arbitrary",) * len(grid)),
    )(*args, *[a for s in sides for a in s.args])
    res = list(res)
    if not sides:
        return res
    outs, pos = take(res, 0, no)
    side_outs = []
    for _, o, _ in cnt:
        r, pos = take(res, pos, o)
        side_outs.append(r)
    return outs, side_outs


def _inv_freq_lanes():
    inv = np.float32(ROPE_THETA) ** (-np.arange(0, ROT_DIM, 2, dtype=np.float32) / np.float32(ROT_DIM))
    lane = np.arange(LANES) % HD
    out = np.where(lane < ROT_DIM, inv[lane % (ROT_DIM // 2)], 0.0).astype(np.float32)
    return jnp.asarray(out.reshape(1, LANES))


def rope_tables(pos_col, sides=()):
    def body(p_ref, f_ref, c_ref, s1_ref, s2_ref):
        ang = p_ref[...].astype(F32) * f_ref[...]
        lane = lax.broadcasted_iota(jnp.int32, ang.shape, 1) % HD
        cs = jnp.cos(ang)
        sn = jnp.sin(ang)
        c_ref[...] = jnp.where(lane < ROT_DIM, cs, 1.0)
        s1_ref[...] = jnp.where(lane < ROT_DIM // 2, -sn, 0.0)
        s2_ref[...] = jnp.where(lane < ROT_DIM // 2, 0.0, jnp.where(lane < ROT_DIM, sn, 0.0))

    sds = jax.ShapeDtypeStruct((S, LANES), F32)
    return _call(
        body, sides, name="rope_tables", grid=(S // TM,),
        in_specs=[_row(1), pl.BlockSpec((1, LANES), lambda i: (0, 0))],
        out_specs=[_row(LANES)] * 3, out_shape=[sds] * 3,
        args=(pos_col, _inv_freq_lanes()))


def _rope(v, c, s1, s2):
    return v * c + pltpu.roll(v, LANES - 8, axis=1) * s1 + pltpu.roll(v, 8, axis=1) * s2


def _rope_t(d, c, s1, s2):
    return d * c - pltpu.roll(d, LANES - 8, axis=1) * s1 - pltpu.roll(d, 8, axis=1) * s2


def _head_mat():
    r = lax.broadcasted_iota(jnp.int32, (LANES, LANES), 0) // HD
    c = lax.broadcasted_iota(jnp.int32, (LANES, LANES), 1) // HD
    return jnp.where(r == c, 1.0 / HD, 0.0).astype(BF16)


def _head_mean(t, e):
    hi = t.astype(BF16)
    rest = (t - hi.astype(F32)).astype(BF16)
    return _dot(hi, e) + _dot(rest, e)


def in_proj(x, norm_w, w_in, sides=()):
    nchunk = 5
    cw = INW // nchunk

    def body(x_ref, nw_ref, w_ref, ht_ref, p_ref):
        xv = x_ref[...]
        r = lax.rsqrt(jnp.mean(xv * xv, axis=-1, keepdims=True) + EPS)
        hf = xv * r * nw_ref[...]
        ht_ref[...] = hf.T.astype(BF16)
        h = hf.astype(BF16)
        for j in range(nchunk):
            p_ref[:, j * cw:(j + 1) * cw] = _dot_nt(h, w_ref[j * cw:(j + 1) * cw, :])

    return _call(
        body, sides, name="in_proj", grid=(S // TM,),
        in_specs=[_row(D), _res((1, D)), _res((INW, D))],
        out_specs=[pl.BlockSpec((D, TM), lambda i: (0, i)), _row(INW)],
        out_shape=[jax.ShapeDtypeStruct((D, S), BF16), jax.ShapeDtypeStruct((S, INW), F32)],
        args=(x, norm_w, w_in))


def _qk_specs():
    nb = QKV // LANES
    return [pl.BlockSpec((S, LANES), functools.partial(lambda hp, g, o: (0, o + g * 4 + hp), o=o))
            for o in (OFF_Q // LANES, OFF_K // LANES, OFF_V // LANES)]


def _tab_specs():
    return [pl.BlockSpec((S, LANES), lambda hp, g: (0, 0), pipeline_mode=pl.Buffered(1))] * 3


def _vec_spec():
    return pl.BlockSpec((1, LANES), lambda hp, g: (0, 0))


def _sub_rows(r, d, start, n):
    if d == 1:
        return pl.ds(start, n)
    return pl.ds(r + d * start, n, stride=d)


def _band_window(i, L):
    W = min(TQ + 2 * HALF_SPAN, L)
    q0 = pl.multiple_of(i * TQ, TQ)
    k0 = pl.multiple_of(jnp.clip(q0 - HALF_SPAN, 0, L - W), HALF_SPAN)
    qpos = q0 + (lax.broadcasted_iota(jnp.int32, (2 * TQ, W), 0) & (TQ - 1))
    kpos = k0 + lax.broadcasted_iota(jnp.int32, (2 * TQ, W), 1)
    valid = jnp.abs(qpos - kpos) <= HALF_SPAN
    return W, q0, k0, valid


def _stack_heads(t, lo):
    z = jnp.zeros_like(t)
    return jnp.concatenate([jnp.where(lo, t, z), jnp.where(lo, z, t)], axis=0)


def _unstack_heads(t2, lo):
    return jnp.where(lo, t2[0:TQ], t2[TQ:2 * TQ])


CHAINS = 4


def _interleave(d):
    ru = min(d, CHAINS)
    return ru, min(CHAINS // ru, S // d // TQ)


def _for_blocks(n, fn):
    if n == 1:
        fn(0)
    else:
        def it(j, _):
            fn(j)
            return 0
        lax.fori_loop(0, n, it, 0)


def attn_fwd(proj, tabs, qw2, kw2, sides=()):
    CH = 256

    def body(q_ref, k_ref, v_ref, c_ref, s1_ref, s2_ref, qw_ref, kw_ref, at_ref, ls_ref,
             qs, ks, vs, osub, lsub, onat, lnat, qn, kn):
        g = pl.program_id(1)
        lo = lax.broadcasted_iota(jnp.int32, (1, LANES), 1) < HD
        e = _head_mat()

        def prep(i, _):
            rows = pl.ds(pl.multiple_of(i * CH, CH), CH)
            c, s1, s2 = c_ref[rows, :], s1_ref[rows, :], s2_ref[rows, :]
            for t_ref, w_ref, out, scale in ((q_ref, qw_ref, qn, HD ** -0.5), (k_ref, kw_ref, kn, 1.0)):
                t = t_ref[rows, :]
                r = lax.rsqrt(_head_mean(t * t, e) + EPS)
                out[rows, :] = _rope(t * r * w_ref[...], c, s1, s2) * scale
            return 0

        lax.fori_loop(0, S // CH, prep, 0, unroll=4)

        def group(gi, d):
            L = S // d

            ru, nb = _interleave(d)

            def stage(r, off):
                for c0 in range(0, L, CH):
                    n = min(CH, L)
                    rows = _sub_rows(r, d, c0, n)
                    dst = pl.ds(off + c0, n)
                    qs[dst, :] = qn[rows, :].astype(BF16)
                    ks[dst, :] = kn[rows, :].astype(BF16)
                    vs[dst, :] = v_ref[rows, :].astype(BF16)

            def one(off, i):
                W, q0, k0, valid = _band_window(i, L)
                q2 = _stack_heads(qs[pl.ds(off + q0, TQ), :], lo)
                sc = jnp.where(valid, _dot_nt(q2, ks[pl.ds(off + k0, W), :]), NEG_INF)
                m = jnp.max(sc, axis=-1, keepdims=True)
                p = jnp.exp(sc - m)
                den = jnp.sum(p, axis=-1, keepdims=True)
                o2 = _dot(p.astype(BF16), vs[pl.ds(off + k0, W), :]) / den
                l2 = jnp.broadcast_to(m + jnp.log(den), (2 * TQ, LANES))
                osub[pl.ds(off + q0, TQ), :] = _unstack_heads(o2, lo)
                lsub[pl.ds(off + q0, TQ), :] = _unstack_heads(l2, lo)

            def unstage(r, off):
                for c0 in range(0, L, CH):
                    n = min(CH, L)
                    rows = _sub_rows(r, d, c0, n)
                    onat[gi, rows, :] = osub[pl.ds(off + c0, n), :]
                    lnat[gi, rows, :] = lsub[pl.ds(off + c0, n), :]

            def step(t, _):
                for u in range(ru):
                    stage(t * ru + u, u * L)
                _for_blocks(L // TQ // nb, lambda j: [one(u * L, j * nb + b) for u in range(ru) for b in range(nb)])
                for u in range(ru):
                    unstage(t * ru + u, u * L)
                return 0

            lax.fori_loop(0, d // ru, step, 0)

        for gi, d in enumerate(DILATIONS):
            pl.when(g == gi)(functools.partial(group, gi, d))

        @pl.when(g == len(DILATIONS) - 1)
        def _():
            def mix(i, _):
                rows = pl.ds(pl.multiple_of(i * CH, CH), CH)
                l0, l1, l2 = lnat[0, rows, :], lnat[1, rows, :], lnat[2, rows, :]
                m = jnp.maximum(jnp.maximum(l0, l1), l2)
                e0, e1, e2 = jnp.exp(l0 - m), jnp.exp(l1 - m), jnp.exp(l2 - m)
                den = e0 + e1 + e2
                a = (e0 * onat[0, rows, :] + e1 * onat[1, rows, :] + e2 * onat[2, rows, :]) / den
                at_ref[rows, :] = a.astype(BF16)
                ls_ref[rows, :] = m + jnp.log(den)
                return 0

            lax.fori_loop(0, S // CH, mix, 0)

    out_spec = pl.BlockSpec((S, LANES), lambda hp, g: (0, hp))
    return _call(
        body, sides, name="attn_fwd", grid=(4, 3),
        in_specs=_qk_specs() + _tab_specs() + [_vec_spec(), _vec_spec()],
        out_specs=[out_spec, out_spec],
        out_shape=[jax.ShapeDtypeStruct((S, CC), BF16), jax.ShapeDtypeStruct((S, CC), F32)],
        scratch_shapes=[pltpu.VMEM((S, LANES), BF16)] * 3 + [pltpu.VMEM((S, LANES), F32)] * 2
        + [pltpu.VMEM((3, S, LANES), F32)] * 2 + [pltpu.VMEM((S, LANES), F32)] * 2,
        args=(proj, proj, proj, *tabs, qw2, kw2))


def attn_bwd(proj, tabs, qw2, kw2, d_attn, attn, lse, sides=()):
    CH = 256

    def body(q_ref, k_ref, v_ref, c_ref, s1_ref, s2_ref, qw_ref, kw_ref, do_ref, at_ref, ls_ref,
             dq_ref, dk_ref, dv_ref, gqw_ref, gkw_ref,
             qs, ks, vs, dos, dsub, lsub, dqs, dks, dvs, dnat, qx, kx, dvn, tnq, tnk, rrq, rrk):
        hp, g = pl.program_id(0), pl.program_id(1)
        lo = lax.broadcasted_iota(jnp.int32, (1, LANES), 1) < HD
        e = _head_mat()
        both = ((q_ref, qw_ref, qx, tnq, rrq, HD ** -0.5), (k_ref, kw_ref, kx, tnk, rrk, 1.0))

        @pl.when((hp == 0) & (g == 0))
        def _():
            gqw_ref[...] = jnp.zeros_like(gqw_ref)
            gkw_ref[...] = jnp.zeros_like(gkw_ref)

        def prep(i, _):
            rows = pl.ds(pl.multiple_of(i * CH, CH), CH)
            dnat[rows, :] = _head_mean(do_ref[rows, :] * at_ref[rows, :].astype(F32), e) * float(HD)
            c, s1, s2 = c_ref[rows, :], s1_ref[rows, :], s2_ref[rows, :]
            for t_ref, w_ref, x, tn_s, rr_s, scale in both:
                t = t_ref[rows, :]
                rr = lax.rsqrt(_head_mean(t * t, e) + EPS)
                tn = t * rr
                rr_s[rows, :] = rr
                tn_s[rows, :] = tn
                x[rows, :] = _rope(tn * w_ref[...], c, s1, s2) * scale
            return 0

        lax.fori_loop(0, S // CH, prep, 0, unroll=4)

        def group(d):
            L = S // d

            ru, nb = _interleave(d)

            def stage(r, off):
                for c0 in range(0, L, CH):
                    n = min(CH, L)
                    rows = _sub_rows(r, d, c0, n)
                    dst = pl.ds(off + c0, n)
                    qs[dst, :] = qx[rows, :].astype(BF16)
                    ks[dst, :] = kx[rows, :].astype(BF16)
                    vs[dst, :] = v_ref[rows, :].astype(BF16)
                    dos[dst, :] = do_ref[rows, :].astype(BF16)
                    dsub[dst, :] = dnat[rows, :]
                    lsub[dst, :] = ls_ref[rows, :]
                    dks[dst, :] = jnp.zeros((n, LANES), F32)
                    dvs[dst, :] = jnp.zeros((n, LANES), F32)

            def one(off, i):
                W, q0, k0, valid = _band_window(i, L)
                qrows, krows = pl.ds(off + q0, TQ), pl.ds(off + k0, W)
                q2 = _stack_heads(qs[qrows, :], lo)
                do2 = _stack_heads(dos[qrows, :], lo)
                kk, vv = ks[krows, :], vs[krows, :]
                lse_b, dd_b = lsub[qrows, :], dsub[qrows, :]
                lse2 = jnp.concatenate([lse_b[:, 0:1], lse_b[:, HD:HD + 1]], axis=0)
                dd2 = jnp.concatenate([dd_b[:, 0:1], dd_b[:, HD:HD + 1]], axis=0)
                sc = jnp.where(valid, _dot_nt(q2, kk), NEG_INF)
                p = jnp.exp(sc - lse2)
                ds = (p * (_dot_nt(do2, vv) - dd2)).astype(BF16)
                dqs[qrows, :] = _unstack_heads(_dot(ds, kk), lo)
                dks[krows, :] = dks[krows, :] + _dot_tn(ds, q2)
                dvs[krows, :] = dvs[krows, :] + _dot_tn(p.astype(BF16), do2)

            def unstage(r, off):
                for c0 in range(0, L, CH):
                    n = min(CH, L)
                    rows = _sub_rows(r, d, c0, n)
                    src = pl.ds(off + c0, n)
                    qx[rows, :] = dqs[src, :]
                    kx[rows, :] = dks[src, :]
                    dvn[rows, :] = dvs[src, :]

            def step(t, _):
                for u in range(ru):
                    stage(t * ru + u, u * L)
                _for_blocks(L // TQ // nb, lambda j: [one(u * L, j * nb + b) for u in range(ru) for b in range(nb)])
                for u in range(ru):
                    unstage(t * ru + u, u * L)
                return 0

            lax.fori_loop(0, d // ru, step, 0)

        for gi, d in enumerate(DILATIONS):
            pl.when(g == gi)(functools.partial(group, d))

        def emit(i, _):
            rows = pl.ds(pl.multiple_of(i * CH, CH), CH)
            c, s1, s2 = c_ref[rows, :], s1_ref[rows, :], s2_ref[rows, :]
            for (_, w_ref, x, tn_s, rr_s, scale), out, gw_ref in zip(both, (dq_ref, dk_ref), (gqw_ref, gkw_ref)):
                tn = tn_s[rows, :]
                dy = _rope_t(x[rows, :] * scale, c, s1, s2)
                gw_ref[0:1, :] = gw_ref[0:1, :] + jnp.sum(dy * tn, axis=0, keepdims=True)
                dtn = dy * w_ref[...]
                out[rows, :] = (rr_s[rows, :] * (dtn - tn * _head_mean(dtn * tn, e))).astype(BF16)
            dv_ref[rows, :] = dvn[rows, :].astype(BF16)
            return 0

        lax.fori_loop(0, S // CH, emit, 0, unroll=4)

    nat_spec = pl.BlockSpec((S, LANES), lambda hp, g: (0, hp))
    out_spec = pl.BlockSpec((S, LANES), lambda hp, g: (0, g * 4 + hp))
    acc_spec = pl.BlockSpec((8, LANES), lambda hp, g: (0, 0))
    return _call(
        body, sides, name="attn_bwd", grid=(4, 3),
        in_specs=_qk_specs() + _tab_specs() + [_vec_spec(), _vec_spec(), nat_spec, nat_spec, nat_spec],
        out_specs=[out_spec] * 3 + [acc_spec] * 2,
        out_shape=[jax.ShapeDtypeStruct((S, QKV), BF16)] * 3 + [jax.ShapeDtypeStruct((8, LANES), F32)] * 2,
        scratch_shapes=[pltpu.VMEM((S, LANES), BF16)] * 4 + [pltpu.VMEM((S, LANES), F32)] * 13,
        args=(proj, proj, proj, *tabs, qw2, kw2, d_attn, attn, lse))


PADR = 16
CT = 128


def _conv_specs():
    return [pl.BlockSpec((S, CC), lambda i: (0, OFF_CA // CC)), pl.BlockSpec((S, CC), lambda i: (0, OFF_CB // CC))]


NCB = CC // LANES


def _pad_zero(pad):
    for cb in range(NCB):
        pad[cb, 0:PADR, :] = jnp.zeros((PADR, LANES), F32)
        pad[cb, PADR + S:PADR + S + PADR, :] = jnp.zeros((PADR, LANES), F32)


def _pad_store(pad, row0, n, val):
    for cb in range(NCB):
        pad[cb, pl.ds(pl.multiple_of(row0 + PADR, 8), n), :] = val[:, cb * LANES:(cb + 1) * LANES]


def _taps(pad_ref, cb, s0, weights):
    acc = jnp.zeros((CT, LANES), F32)
    for k in range(KW):
        acc = acc + weights[k] * pad_ref[cb, pl.ds(s0 + k + 1, CT), :]
    return acc


def conv_fwd(proj, conv_w, conv_b, ln_w, ln_b):
    def body(a_ref, b_ref, w_ref, cb_ref, lw_ref, lb_ref, c_ref, u3_ref, upad):
        _pad_zero(upad)

        def glu(i, _):
            rows = pl.ds(pl.multiple_of(i * TM, TM), TM)
            _pad_store(upad, i * TM, TM, a_ref[rows, :] * _sigmoid(b_ref[rows, :]))
            return 0

        lax.fori_loop(0, S // TM, glu, 0)

        def chunk(i, _):
            s0 = pl.multiple_of(i * CT, CT)
            for cb in range(CC // LANES):
                cols = slice(cb * LANES, (cb + 1) * LANES)
                w = [w_ref[k:k + 1, cols] for k in range(KW)]
                c_ref[pl.ds(s0, CT), cols] = _taps(upad, cb, s0, w) + cb_ref[:, cols]
            cv = c_ref[pl.ds(s0, CT), :]
            mu = jnp.mean(cv, axis=-1, keepdims=True)
            xc = cv - mu
            rstd = lax.rsqrt(jnp.mean(xc * xc, axis=-1, keepdims=True) + EPS)
            yl = xc * rstd * lw_ref[...] + lb_ref[...]
            u3_ref[pl.ds(s0, CT), :] = (yl * _sigmoid(yl)).astype(BF16)
            return 0

        lax.fori_loop(0, S // CT, chunk, 0)

    vec = pl.BlockSpec((1, CC), lambda i: (0, 0))
    full = pl.BlockSpec((S, CC), lambda i: (0, 0))
    return pl.pallas_call(
        body, name="conv_fwd", grid=(1,),
        in_specs=_conv_specs() + [pl.BlockSpec((KW, CC), lambda i: (0, 0)), vec, vec, vec],
        out_specs=[full, full],
        out_shape=[jax.ShapeDtypeStruct((S, CC), F32), jax.ShapeDtypeStruct((S, CC), BF16)],
        scratch_shapes=[pltpu.VMEM((NCB, S + 2 * PADR, LANES), F32)],
        compiler_params=_cp(dimension_semantics=("arbitrary",)),
    )(proj, proj, conv_w, conv_b, ln_w, ln_b)


def conv_bwd(proj, cpre, d_u3, conv_w, conv_w_rev, ln_w, ln_b, sides=()):
    def body(a_ref, b_ref, c_ref, du3_ref, w_ref, wr_ref, lw_ref, lb_ref,
             dc_ref, gw_ref, gcb_ref, glw_ref, glb_ref, upad, dpad):
        _pad_zero(upad)
        _pad_zero(dpad)
        gw_ref[...] = jnp.zeros_like(gw_ref)

        def ln_bwd(i, carry):
            gcb, glw, glb = carry
            rows = pl.ds(pl.multiple_of(i * TM, TM), TM)
            _pad_store(upad, i * TM, TM, a_ref[rows, :] * _sigmoid(b_ref[rows, :]))
            cv = c_ref[rows, :]
            mu = jnp.mean(cv, axis=-1, keepdims=True)
            xc = cv - mu
            rstd = lax.rsqrt(jnp.mean(xc * xc, axis=-1, keepdims=True) + EPS)
            xh = xc * rstd
            yl = xh * lw_ref[...] + lb_ref[...]
            dyl = du3_ref[rows, :] * _dsilu(yl, _sigmoid(yl))
            dxh = dyl * lw_ref[...]
            dcv = rstd * (dxh - jnp.mean(dxh, axis=-1, keepdims=True)
                          - xh * jnp.mean(dxh * xh, axis=-1, keepdims=True))
            _pad_store(dpad, i * TM, TM, dcv)
            return (gcb + jnp.sum(dcv, axis=0, keepdims=True),
                    glw + jnp.sum(dyl * xh, axis=0, keepdims=True),
                    glb + jnp.sum(dyl, axis=0, keepdims=True))

        z = jnp.zeros((1, CC), F32)
        gcb, glw, glb = lax.fori_loop(0, S // TM, ln_bwd, (z, z, z))
        gcb_ref[...] = gcb
        glw_ref[...] = glw
        glb_ref[...] = glb

        def chunk(i, _):
            s0 = pl.multiple_of(i * CT, CT)
            for cb in range(CC // LANES):
                cols = slice(cb * LANES, (cb + 1) * LANES)
                wr = [wr_ref[k:k + 1, cols] for k in range(KW)]
                du = _taps(dpad, cb, s0, wr)
                dcv = dpad[cb, pl.ds(s0 + PADR, CT), :]
                for k in range(KW):
                    gw_ref[k:k + 1, cols] = gw_ref[k:k + 1, cols] + jnp.sum(
                        upad[cb, pl.ds(s0 + k + 1, CT), :] * dcv, axis=0, keepdims=True)
                av = a_ref[pl.ds(s0, CT), cols]
                sb = _sigmoid(b_ref[pl.ds(s0, CT), cols])
                dc_ref[pl.ds(s0, CT), cols] = (du * sb).astype(BF16)
                dc_ref[pl.ds(s0, CT), slice(CC + cb * LANES, CC + (cb + 1) * LANES)] = (
                    du * av * sb * (1.0 - sb)).astype(BF16)
            return 0

        lax.fori_loop(0, S // CT, chunk, 0)

    vec = pl.BlockSpec((1, CC), lambda i: (0, 0))
    full = pl.BlockSpec((S, CC), lambda i: (0, 0))
    wsp = pl.BlockSpec((KW, CC), lambda i: (0, 0))
    return _call(
        body, sides, name="conv_bwd", grid=(1,),
        in_specs=_conv_specs() + [full, full, wsp, wsp, vec, vec],
        out_specs=[pl.BlockSpec((S, 2 * CC), lambda i: (0, 0)), wsp, vec, vec, vec],
        out_shape=[jax.ShapeDtypeStruct((S, 2 * CC), BF16), jax.ShapeDtypeStruct((KW, CC), F32)]
        + [jax.ShapeDtypeStruct((1, CC), F32)] * 3,
        scratch_shapes=[pltpu.VMEM((NCB, S + 2 * PADR, LANES), F32)] * 2,
        args=(proj, proj, cpre, d_u3, conv_w, conv_w_rev, ln_w, ln_b))


def _gate_specs():
    return [_row(CC, col=OFF_GA // CC + j) for j in range(4)]


def _gates(g_refs, bg_ref):
    ga = _sigmoid(jnp.concatenate([g_refs[0][...], g_refs[1][...]], axis=1) + bg_ref[0:1, :])
    gb = _sigmoid(jnp.concatenate([g_refs[2][...], g_refs[3][...]], axis=1) + bg_ref[1:2, :])
    return ga, gb


def mix_out(x, proj, b_gate, attn, u3, w_o, w_pw, w_out):
    def body(x_ref, g0, g1, g2, g3, bg_ref, at_ref, u3_ref, wo_ref, wp_ref, wout_ref,
             x1_ref, z_ref, ya_ref, yb_ref):
        ga, gb = _gates((g0, g1, g2, g3), bg_ref)
        ya = _dot(at_ref[...], wo_ref[...])
        yb = _dot(u3_ref[...], wp_ref[...])
        z = (ga * ya + gb * yb).astype(BF16)
        ya_ref[...] = ya.astype(BF16)
        yb_ref[...] = yb.astype(BF16)
        z_ref[...] = z
        x1_ref[...] = x_ref[...] + _dot(z, wout_ref[...])

    return pl.pallas_call(
        body, name="mix_out", grid=(S // TM,),
        in_specs=[_row(D)] + _gate_specs() + [_res((2, D)), _row(CC), _row(CC),
                                              _res((CC, D)), _res((CC, D)), _res((D, D))],
        out_specs=[_row(D)] * 4,
        out_shape=[jax.ShapeDtypeStruct((S, D), F32)] + [jax.ShapeDtypeStruct((S, D), BF16)] * 3,
        compiler_params=_cp(dimension_semantics=("arbitrary",)),
    )(x, proj, proj, proj, proj, b_gate, attn, u3, w_o, w_pw, w_out)


def out_bwd(d_x1b, proj, b_gate, ya, yb, w_o, w_pw, w_out, sides=()):
    def body(dx_ref, g0, g1, g2, g3, bg_ref, ya_ref, yb_ref, wo_ref, wp_ref, wout_ref,
             dya_ref, dyb_ref, dgl_ref, dat_ref, du3_ref, gbg_ref):
        @pl.when(pl.program_id(0) == 0)
        def _():
            gbg_ref[...] = jnp.zeros_like(gbg_ref)

        ga, gb = _gates((g0, g1, g2, g3), bg_ref)
        dz = _dot_nt(dx_ref[...], wout_ref[...])
        dya = (dz * ga).astype(BF16)
        dyb = (dz * gb).astype(BF16)
        dgla = dz * ya_ref[...].astype(F32) * ga * (1.0 - ga)
        dglb = dz * yb_ref[...].astype(F32) * gb * (1.0 - gb)
        dya_ref[...] = dya
        dyb_ref[...] = dyb
        dgl_ref[:, 0:D] = dgla.astype(BF16)
        dgl_ref[:, D:2 * D] = dglb.astype(BF16)
        gbg_ref[0:1, :] = gbg_ref[0:1, :] + jnp.sum(dgla, axis=0, keepdims=True)
        gbg_ref[1:2, :] = gbg_ref[1:2, :] + jnp.sum(dglb, axis=0, keepdims=True)
        dat_ref[...] = _dot_nt(dya, wo_ref[...])
        du3_ref[...] = _dot_nt(dyb, wp_ref[...])

    return _call(
        body, sides, name="out_bwd", grid=(S // TM,),
        in_specs=[_row(D)] + _gate_specs() + [_res((2, D)), _row(D), _row(D),
                                              _res((CC, D)), _res((CC, D)), _res((D, D))],
        out_specs=[_row(D), _row(D), _row(2 * D), _row(CC), _row(CC), pl.BlockSpec((2, D), lambda i: (0, 0))],
        out_shape=[jax.ShapeDtypeStruct((S, D), BF16)] * 2 + [jax.ShapeDtypeStruct((S, 2 * D), BF16)]
        + [jax.ShapeDtypeStruct((S, CC), F32)] * 2 + [jax.ShapeDtypeStruct((2, D), F32)],
        args=(d_x1b, proj, proj, proj, proj, b_gate, ya, yb, w_o, w_pw, w_out))


def ffn_in(x1, norm_w, w_ffn_in, sides=()):
    half = FF // 2

    def body(x_ref, nw_ref, w_ref, h_ref, gu_ref, f_ref):
        xv = x_ref[...]
        r = lax.rsqrt(jnp.mean(xv * xv, axis=-1, keepdims=True) + EPS)
        h = (xv * r * nw_ref[...]).astype(BF16)
        h_ref[...] = h
        for j in range(2):
            gt = _dot_nt(h, w_ref[j * half:(j + 1) * half, :])
            up = _dot_nt(h, w_ref[FF + j * half:FF + (j + 1) * half, :])
            gu_ref[:, j * half:(j + 1) * half] = gt.astype(BF16)
            gu_ref[:, FF + j * half:FF + (j + 1) * half] = up.astype(BF16)
            f_ref[:, j * half:(j + 1) * half] = (gt * _sigmoid(gt) * up).astype(BF16)

    return _call(
        body, sides, name="ffn_in", grid=(S // TM,),
        in_specs=[_row(D), _res((1, D)), _res((2 * FF, D))],
        out_specs=[_row(D), _row(2 * FF), _row(FF)],
        out_shape=[jax.ShapeDtypeStruct((S, D), BF16), jax.ShapeDtypeStruct((S, 2 * FF), BF16),
                   jax.ShapeDtypeStruct((S, FF), BF16)],
        args=(x1, norm_w, w_ffn_in))


def ffn_out_loss(x1, f, w_ffn_out, target):
    def body(x_ref, f_ref, w_ref, t_ref, dy_ref, dyb_ref, sq_ref):
        @pl.when(pl.program_id(0) == 0)
        def _():
            sq_ref[...] = jnp.zeros_like(sq_ref)

        diff = x_ref[...] + _dot(f_ref[...], w_ref[...]) - t_ref[...]
        dy = diff * (1.0 / D)
        dy_ref[...] = dy
        dyb_ref[...] = dy.astype(BF16)
        sq_ref[...] = sq_ref[...] + jnp.sum((diff * diff).reshape(TM // 8, 8, D), axis=0)

    return pl.pallas_call(
        body, name="ffn_out_loss", grid=(S // TM,),
        in_specs=[_row(D), _row(FF), _res((FF, D)), _row(D)],
        out_specs=[_row(D), _row(D), pl.BlockSpec((8, D), lambda i: (0, 0))],
        out_shape=[jax.ShapeDtypeStruct((S, D), F32), jax.ShapeDtypeStruct((S, D), BF16),
                   jax.ShapeDtypeStruct((8, D), F32)],
        compiler_params=_cp(dimension_semantics=("arbitrary",)),
    )(x1, f, w_ffn_out, target)


def _rms_bwd(xv, nw, dh):
    r = lax.rsqrt(jnp.mean(xv * xv, axis=-1, keepdims=True) + EPS)
    xn = xv * r
    dxn = dh * nw
    dx = r * (dxn - xn * jnp.mean(dxn * xn, axis=-1, keepdims=True))
    return dx, dh * xn


def ffn_bwd(dy, dyb, gu, x1, norm_w, w_ffn_in, w_ffn_out, sides=()):
    def body(dy_ref, dyb_ref, gu_ref, x_ref, nw_ref, wi_ref, wo_ref, dgu_ref, dx_ref, dxb_ref, gn_ref):
        @pl.when(pl.program_id(0) == 0)
        def _():
            gn_ref[...] = jnp.zeros_like(gn_ref)

        df = _dot_nt(dyb_ref[...], wo_ref[...])
        gt = gu_ref[:, 0:FF].astype(F32)
        up = gu_ref[:, FF:2 * FF].astype(F32)
        sg = _sigmoid(gt)
        dgt = (df * up * _dsilu(gt, sg)).astype(BF16)
        dup = (df * gt * sg).astype(BF16)
        dgu_ref[:, 0:FF] = dgt
        dgu_ref[:, FF:2 * FF] = dup
        dh = _dot(dgt, wi_ref[0:FF, :]) + _dot(dup, wi_ref[FF:2 * FF, :])
        dxn, gw = _rms_bwd(x_ref[...], nw_ref[...], dh)
        dx = dy_ref[...] + dxn
        dx_ref[...] = dx
        dxb_ref[...] = dx.astype(BF16)
        gn_ref[...] = gn_ref[...] + jnp.sum(gw, axis=0, keepdims=True)

    return _call(
        body, sides, name="ffn_bwd", grid=(S // TM,),
        in_specs=[_row(D), _row(D), _row(2 * FF), _row(D), _res((1, D)), _res((2 * FF, D)), _res((FF, D))],
        out_specs=[_row(2 * FF), _row(D), _row(D), pl.BlockSpec((1, D), lambda i: (0, 0))],
        out_shape=[jax.ShapeDtypeStruct((S, 2 * FF), BF16), jax.ShapeDtypeStruct((S, D), F32),
                   jax.ShapeDtypeStruct((S, D), BF16), jax.ShapeDtypeStruct((1, D), F32)],
        args=(dy, dyb, gu, x1, norm_w, w_ffn_in, w_ffn_out))


def in_bwd(d_q, d_k, d_v, d_conv, d_gl, w_in, x, d_x1, norm_w, sides=()):
    segs = ((OFF_Q, QKV), (OFF_K, QKV), (OFF_V, QKV), (OFF_CA, 2 * CC), (OFF_GA, 2 * D))

    def body(dq_ref, dk_ref, dv_ref, dc_ref, dg_ref, w_ref, x_ref, dx1_ref, nw_ref, gx_ref, gn_ref):
        @pl.when(pl.program_id(0) == 0)
        def _():
            gn_ref[...] = jnp.zeros_like(gn_ref)

        dh = jnp.zeros((TM, D), F32)
        for ref, (off, width) in zip((dq_ref, dk_ref, dv_ref, dc_ref, dg_ref), segs):
            dh = dh + _dot(ref[...], w_ref[off:off + width, :])
        dxn, gw = _rms_bwd(x_ref[...], nw_ref[...], dh)
        gx_ref[...] = dx1_ref[...] + dxn
        gn_ref[...] = gn_ref[...] + jnp.sum(gw, axis=0, keepdims=True)

    return _call(
        body, sides, name="in_bwd", grid=(S // TM,),
        in_specs=[_row(QKV)] * 3 + [_row(2 * CC), _row(2 * D), _res((INW, D)), _row(D), _row(D), _res((1, D))],
        out_specs=[_row(D), pl.BlockSpec((1, D), lambda i: (0, 0))],
        out_shape=[jax.ShapeDtypeStruct((S, D), F32), jax.ShapeDtypeStruct((1, D), F32)],
        args=(d_q, d_k, d_v, d_conv, d_gl, w_in, x, d_x1, norm_w))


def mm_tn(name, a, b, tm, tn, sides=()):
    M, N = a.shape[1], b.shape[1]

    def body(a_ref, b_ref, o_ref):
        o_ref[...] = _dot_tn(a_ref[...], b_ref[...])

    res = _call(
        body, sides, name=name, grid=(M // tm, N // tn),
        in_specs=[pl.BlockSpec((S, tm), lambda i, j: (0, i)), pl.BlockSpec((S, tn), lambda i, j: (0, j))],
        out_specs=[pl.BlockSpec((tm, tn), lambda i, j: (i, j))],
        out_shape=[jax.ShapeDtypeStruct((M, N), F32)],
        args=(a, b))
    return (res[0][0], res[1]) if sides else res[0]


GW_IN_TN = 512
GW_IN_PARTS = 4


def gw_in_t(name, ht, d_segs, col_half, sides=()):
    tn, hw = GW_IN_TN, D // GW_IN_PARTS
    starts, t0 = [], 0
    for seg in d_segs:
        starts.append(t0)
        t0 += seg.shape[1] // tn
    ntiles = [seg.shape[1] // tn for seg in d_segs]

    def body(h_ref, *refs):
        a_refs, o_ref = refs[:-1], refs[-1]
        n = pl.program_id(0)
        for a_ref, st, nt in zip(a_refs, starts, ntiles):
            @pl.when((n >= st) & (n < st + nt))
            def _(a_ref=a_ref):
                o_ref[...] = _dot(h_ref[...], a_ref[...]).T

    def seg_spec(st, nt):
        return pl.BlockSpec((S, tn), lambda n: (0, jnp.clip(n - st, 0, nt - 1)))

    res = _call(
        body, sides, name=name, grid=(INW // tn,),
        in_specs=[pl.BlockSpec((hw, S), lambda n: (col_half, 0))] + [seg_spec(st, nt) for st, nt in zip(starts, ntiles)],
        out_specs=[pl.BlockSpec((tn, hw), lambda n: (n, 0))],
        out_shape=[jax.ShapeDtypeStruct((INW, hw), F32)],
        args=(ht, *d_segs))
    return (res[0][0], res[1]) if sides else res[0]


def _place():
    x, y, c = lax.axis_index("x"), lax.axis_index("y"), lax.axis_index("c")
    chips = [(1 - x, y), (x, 1 - y), (1 - x, 1 - y)]
    return x, y, c, chips


def _sems(n):
    return pltpu.SemaphoreType.DMA((n,))


def _remote(src, dst, send, recv, k, to):
    return pltpu.make_async_remote_copy(src_ref=src, dst_ref=dst, send_sem=send.at[k], recv_sem=recv.at[k],
                                        device_id=to, device_id_type=MESH)


def _cast_rows(dst, src, cols=slice(None)):
    rows = src.shape[0]
    step = next((s for s in (128, 64, 32, 16) if rows % s == 0), rows)
    for r0 in range(0, rows, step):
        dst[r0:r0 + step, cols] = src[r0:r0 + step, :].astype(dst.dtype)


def comm_only(name, sides):
    def body():
        pass

    return _call(body, sides, name=name, grid=(1,), in_specs=[], out_specs=[], out_shape=[], args=())[1]


def ag_blocks(shard, dtype):
    R, W = shard.shape

    def copy(outs, scr, k, block, to, src=None):
        dst = outs[0].at[block]
        return _remote(dst if src is None else src, dst, scr[1], scr[2], k, to)

    def local(outs, scr, me):
        return pltpu.make_async_copy(scr[0], outs[0].at[me], scr[3].at[0])

    def start(ins, outs, scr):
        x, y, c, chips = _place()
        me = 4 * x + 2 * y + c
        _cast_rows(scr[0], ins[0])
        local(outs, scr, me).start()
        copy(outs, scr, 0, me, (x, y, 1 - c), src=scr[0]).start()
        for j, (cx, cy) in enumerate(chips):
            copy(outs, scr, 1 + j, me, (cx, cy, c), src=scr[0]).start()

    def finish(ins, outs, scr):
        x, y, c, chips = _place()
        me, sib = 4 * x + 2 * y + c, (x, y, 1 - c)
        passed = []
        for j, (cx, cy) in enumerate(chips):
            theirs = 4 * cx + 2 * cy + c
            copy(outs, scr, 1 + j, theirs, (x, y, c)).wait_recv()
            fwd = copy(outs, scr, 4 + j, theirs, sib)
            fwd.start()
            passed.append(fwd)
        copy(outs, scr, 0, 4 * x + 2 * y + 1 - c, (x, y, c)).wait_recv()
        for j, (cx, cy) in enumerate(chips):
            copy(outs, scr, 4 + j, 4 * cx + 2 * cy + 1 - c, (x, y, c)).wait_recv()
        copy(outs, scr, 0, me, sib, src=scr[0]).wait_send()
        for j, (cx, cy) in enumerate(chips):
            copy(outs, scr, 1 + j, me, (cx, cy, c), src=scr[0]).wait_send()
        for fwd in passed:
            fwd.wait_send()
        local(outs, scr, me).wait()

    return Side((shard,), (VMEM,), (jax.ShapeDtypeStruct((NDEV, R, W), dtype),),
                (pltpu.VMEM((R, W), dtype), _sems(7), _sems(7), _sems(1)), start, finish)


def ag_blocks_relay(shard, dtype):
    R, W = shard.shape
    half = R // 2

    def copy(outs, scr, k, block, to, src=None, rows=None):
        dst = outs[0].at[block] if rows is None else outs[0].at[block, pl.ds(rows * half, half), :]
        return _remote(dst if src is None else src, dst, scr[1], scr[2], k, to)

    def local(outs, scr, me):
        return pltpu.make_async_copy(scr[0], outs[0].at[me], scr[3].at[0])

    def own(outs, scr):
        x, y, c, _ = _place()
        me = 4 * x + 2 * y + c
        return [copy(outs, scr, k, me, to, src=scr[0])
                for k, to in enumerate([(x, y, 1 - c), (1 - x, y, c), (x, 1 - y, c)])]

    def start(ins, outs, scr):
        x, y, c, _ = _place()
        _cast_rows(scr[0], ins[0])
        local(outs, scr, 4 * x + 2 * y + c).start()
        for cp in own(outs, scr):
            cp.start()

    def passed_on(outs, scr):
        x, y, c, _ = _place()
        sib, xn, yn = (x, y, 1 - c), (1 - x, y, c), (x, 1 - y, c)
        b_xn, b_yn, b_dg = 4 * (1 - x) + 2 * y + c, 4 * x + 2 * (1 - y) + c, 4 * (1 - x) + 2 * (1 - y) + c
        near = [copy(outs, scr, 5, b_xn, yn, rows=0), copy(outs, scr, 3, b_xn, sib),
                copy(outs, scr, 6, b_yn, xn, rows=1), copy(outs, scr, 4, b_yn, sib)]
        far = [copy(outs, scr, 7, b_dg, sib, rows=0), copy(outs, scr, 8, b_dg, sib, rows=1)]
        return (b_xn, b_yn, b_dg), near, far

    def mid(ins, outs, scr):
        x, y, c, _ = _place()
        (b_xn, b_yn, _), near, _ = passed_on(outs, scr)
        copy(outs, scr, 1, b_xn, (x, y, c)).wait_recv()
        near[0].start()
        near[1].start()
        copy(outs, scr, 2, b_yn, (x, y, c)).wait_recv()
        near[2].start()
        near[3].start()

    def finish(ins, outs, scr):
        x, y, c, _ = _place()
        here = (x, y, c)
        (b_xn, b_yn, b_dg), near, far = passed_on(outs, scr)
        copy(outs, scr, 5, b_dg, here, rows=0).wait_recv()
        far[0].start()
        copy(outs, scr, 6, b_dg, here, rows=1).wait_recv()
        far[1].start()
        flip = 1 - 2 * c
        copy(outs, scr, 0, 4 * x + 2 * y + 1 - c, here).wait_recv()
        copy(outs, scr, 3, b_xn + flip, here).wait_recv()
        copy(outs, scr, 4, b_yn + flip, here).wait_recv()
        copy(outs, scr, 7, b_dg + flip, here, rows=0).wait_recv()
        copy(outs, scr, 8, b_dg + flip, here, rows=1).wait_recv()
        for cp in own(outs, scr) + near + far:
            cp.wait_send()
        local(outs, scr, 4 * x + 2 * y + c).wait()

    return Side((shard,), (VMEM,), (jax.ShapeDtypeStruct((NDEV, R, W), dtype),),
                (pltpu.VMEM((R, W), dtype), _sems(9), _sems(9), _sems(1)), start, finish, mid)


def ag_cols(shard):
    K, C = shard.shape
    half, w2 = K // 2, 2 * C

    def win(out, rows_c, chip):
        return out.at[pl.ds(pl.multiple_of(rows_c * half, 16), half), pl.ds(pl.multiple_of(chip * w2, LANES), w2)]

    def ici(outs, scr, j, to, c, k):
        slab, send, recv = scr[2], scr[5], scr[6]
        return _remote(slab.at[pl.ds(pl.multiple_of(c * half, 16), half), :], win(outs[0], c, k), send, recv, j, to)

    def local(outs, scr, k):
        return pltpu.make_async_copy(scr[2], outs[0].at[:, pl.ds(pl.multiple_of(k * w2, LANES), w2)], scr[7].at[0])

    def start(ins, outs, scr):
        stage, inbox, slab, xs, xr = scr[:5]
        x, y, c, chips = _place()
        k = 2 * x + y
        _cast_rows(stage, ins[0])
        swap = _remote(stage, inbox, xs, xr, 0, (x, y, 1 - c))
        swap.start()
        for cc in range(2):
            @pl.when(c == cc)
            def _(cc=cc):
                _cast_rows(slab, stage, slice(cc * C, (cc + 1) * C))
        swap.wait()
        for cc in range(2):
            @pl.when(c == cc)
            def _(cc=cc):
                _cast_rows(slab, inbox, slice((1 - cc) * C, (2 - cc) * C))
        local(outs, scr, k).start()
        for j, (cx, cy) in enumerate(chips):
            ici(outs, scr, j, (cx, cy, c), c, k).start()

    def finish(ins, outs, scr):
        send, recv = scr[5], scr[6]
        x, y, c, chips = _place()
        k, sib = 2 * x + y, (x, y, 1 - c)
        passed = []
        for j, (cx, cy) in enumerate(chips):
            w = win(outs[0], c, 2 * cx + cy)
            _remote(w, w, send, recv, j, sib).wait_recv()
            fwd = _remote(w, w, send, recv, 3 + j, sib)
            fwd.start()
            passed.append(fwd)
        for j, (cx, cy) in enumerate(chips):
            w = win(outs[0], 1 - c, 2 * cx + cy)
            _remote(w, w, send, recv, 3 + j, sib).wait_recv()
        for j, (cx, cy) in enumerate(chips):
            ici(outs, scr, j, (cx, cy, c), c, k).wait_send()
        for fwd in passed:
            fwd.wait_send()
        local(outs, scr, k).wait()

    return Side((shard,), (VMEM,), (jax.ShapeDtypeStruct((K, NDEV * C), BF16),),
                (pltpu.VMEM((K, C), BF16), pltpu.VMEM((K, C), BF16), pltpu.VMEM((K, w2), BF16),
                 _sems(1), _sems(1), _sems(6), _sems(6), _sems(1)), start, finish)


def copies_side(args, out_shape, n_copies, plan):
    def copies(ins, outs, scr):
        return [_remote(s_, d_, scr[0], scr[1], i, to) for i, (s_, d_, to) in enumerate(plan(ins, outs))]

    def start(ins, outs, scr):
        for cp in copies(ins, outs, scr):
            cp.start()

    def finish(ins, outs, scr):
        for cp in copies(ins, outs, scr):
            cp.wait()

    return Side(tuple(args), (ANY,) * len(args), tuple(out_shape), (_sems(n_copies), _sems(n_copies)), start, finish)


def rs_to_sibling(grads):
    out_shape = [jax.ShapeDtypeStruct((4,) + g.shape[1:] if kind == "rows" else (g.shape[0] // 2, g.shape[1]), F32)
                 for kind, g in grads]

    def plan(ins, outs):
        x, y, c, _ = _place()
        sib, res = (x, y, 1 - c), []
        for (kind, _), g, r in zip(grads, ins, outs):
            if kind == "rows":
                res += [(g.at[2 * k + 1 - c], r.at[k], sib) for k in range(4)]
            else:
                half = g.shape[0] // 2
                res.append((g.at[pl.ds(pl.multiple_of((1 - c) * half, 8), half), :], r, sib))
        return res

    return copies_side([g for _, g in grads], out_shape, sum(4 if kind == "rows" else 1 for kind, _ in grads), plan)


def rs_to_chips(parts):
    out_shape = [jax.ShapeDtypeStruct((3,) + p.shape[1:] if kind == "rows" else (3, p.shape[0], p.shape[1] // 4), BF16)
                 for kind, p in parts]

    def plan(ins, outs):
        x, y, c, chips = _place()
        res = []
        for (kind, _), p, r in zip(parts, ins, outs):
            for j, (cx, cy) in enumerate(chips):
                if kind == "rows":
                    src = p.at[2 * cx + cy]
                else:
                    w2 = p.shape[1] // 4
                    src = p.at[:, pl.ds(pl.multiple_of((2 * cx + cy) * w2, LANES), w2)]
                res.append((src, r.at[j], (cx, cy, c)))
        return res

    return copies_side([p for _, p in parts], out_shape, 3 * len(parts), plan)


def rs_swap_halves(theirs):
    def plan(ins, outs):
        x, y, c, _ = _place()
        return [(t, r, (x, y, 1 - c)) for t, r in zip(ins, outs)]

    return copies_side(theirs, [jax.ShapeDtypeStruct(t.shape, F32) for t in theirs], len(theirs), plan)


def _row_tiles(rows):
    return 2 if rows % 32 == 0 and rows >= 512 else 1


def chip_sum(name, grad, recv, c_idx, chip_idx):
    _, R, C = grad.shape
    nt = 1
    tr = R // nt

    def body(s_ref, g_ref, r_ref, p_ref, own_ref):
        k = pl.program_id(1)
        tot = g_ref[0] + r_ref[0]
        p_ref[0] = tot.astype(BF16)

        @pl.when(k == s_ref[1])
        def _():
            own_ref[...] = tot

    grid_spec = pltpu.PrefetchScalarGridSpec(
        num_scalar_prefetch=1, grid=(nt, 4),
        in_specs=[pl.BlockSpec((1, tr, C), lambda i, k, s: (2 * k + s[0], i, 0)),
                  pl.BlockSpec((1, tr, C), lambda i, k, s: (k, i, 0))],
        out_specs=[pl.BlockSpec((1, tr, C), lambda i, k, s: (k, i, 0)),
                   pl.BlockSpec((tr, C), lambda i, k, s: (i, 0))])
    return pl.pallas_call(
        body, name=name, grid_spec=grid_spec,
        out_shape=[jax.ShapeDtypeStruct((4, R, C), BF16), jax.ShapeDtypeStruct((R, C), F32)],
        compiler_params=_cp(dimension_semantics=("arbitrary", "arbitrary")),
    )(jnp.stack([c_idx, chip_idx]), grad, recv)


def _half_tiles(half):
    return 2 if half >= 512 else 1


def chip_sum_cols(name, grad, recv, c_idx, chip_idx):
    K, W = grad.shape
    half, w2 = K // 2, W // 4
    nt = _half_tiles(half)
    tr = half // nt

    def body(s_ref, g_ref, r_ref, p_ref, own_ref):
        tot = g_ref[...] + r_ref[...]
        p_ref[...] = tot.astype(BF16)

        @pl.when(pl.program_id(1) == s_ref[1])
        def _():
            own_ref[...] = tot

    grid_spec = pltpu.PrefetchScalarGridSpec(
        num_scalar_prefetch=1, grid=(nt, 4),
        in_specs=[pl.BlockSpec((tr, w2), lambda i, k, s: (s[0] * nt + i, k)),
                  pl.BlockSpec((tr, w2), lambda i, k, s: (i, k))],
        out_specs=[pl.BlockSpec((tr, w2), lambda i, k, s: (i, k)),
                   pl.BlockSpec((tr, w2), lambda i, k, s: (i, 0))])
    return pl.pallas_call(
        body, name=name, grid_spec=grid_spec,
        out_shape=[jax.ShapeDtypeStruct((half, W), BF16), jax.ShapeDtypeStruct((half, w2), F32)],
        compiler_params=_cp(dimension_semantics=("arbitrary", "arbitrary")),
    )(jnp.stack([c_idx, chip_idx]), grad, recv)


def col_final(name, own, recv, c_idx):
    half, w2 = own.shape
    C = w2 // 2
    nt = _half_tiles(half)
    tr = half // nt

    def body(s_ref, o_ref, r_ref, mine_ref, theirs_ref, t_ref):
        t_ref[...] = o_ref[...] + r_ref[0].astype(F32) + r_ref[1].astype(F32) + r_ref[2].astype(F32)
        for cc in range(2):
            @pl.when(s_ref[0] == cc)
            def _(cc=cc):
                mine_ref[...] = t_ref[:, cc * C:(cc + 1) * C]
                theirs_ref[...] = t_ref[:, (1 - cc) * C:(2 - cc) * C]

    grid_spec = pltpu.PrefetchScalarGridSpec(
        num_scalar_prefetch=1, grid=(nt,),
        in_specs=[pl.BlockSpec((tr, w2), lambda i, s: (i, 0)), pl.BlockSpec((3, tr, w2), lambda i, s: (0, i, 0))],
        out_specs=[pl.BlockSpec((tr, C), lambda i, s: (i, 0))] * 2,
        scratch_shapes=[pltpu.VMEM((tr, w2), F32)])
    return pl.pallas_call(
        body, name=name, grid_spec=grid_spec, out_shape=[jax.ShapeDtypeStruct((half, C), F32)] * 2,
        compiler_params=_cp(dimension_semantics=("arbitrary",)),
    )(jnp.stack([c_idx]), own, recv)


def _adamw(w, g, m, v):
    m2 = ADAM_B1 * m + (1.0 - ADAM_B1) * g
    v2 = ADAM_B2 * v + (1.0 - ADAM_B2) * (g * g)
    m_hat = m2 / (1.0 - ADAM_B1 ** ADAM_STEP)
    v_hat = v2 / (1.0 - ADAM_B2 ** ADAM_STEP)
    delta = -ADAM_LR * (m_hat / (jnp.sqrt(v_hat) + ADAM_EPS) + ADAM_WD * w)
    return delta, m2, v2


def shard_adam(name, owns, recvs, w, m, v):
    n = len(owns)
    R, Cp = owns[0].shape
    nt = _row_tiles(R)
    tr = R // nt

    def body(*refs):
        o_refs, r_refs = refs[:n], refs[n:2 * n]
        w_ref, m_ref, v_ref, g_ref, d_ref, nm_ref, nv_ref = refs[2 * n:]
        g = None
        for k in range(n):
            gk = o_refs[k][...] + r_refs[k][0].astype(F32) + r_refs[k][1].astype(F32) + r_refs[k][2].astype(F32)
            g = gk if g is None else jnp.where(pl.program_id(0) == k, gk, g)
        delta, m2, v2 = _adamw(w_ref[...], g, m_ref[...], v_ref[...])
        g_ref[...] = g
        d_ref[...] = delta
        nm_ref[...] = m2
        nv_ref[...] = v2

    part = pl.BlockSpec((tr, Cp), lambda k, i: (i, 0))
    part3 = pl.BlockSpec((3, tr, Cp), lambda k, i: (0, i, 0))
    tile = pl.BlockSpec((tr, Cp), lambda k, i: (i, k))
    return pl.pallas_call(
        body, name=name, grid=(n, nt),
        in_specs=[part] * n + [part3] * n + [tile, tile, tile],
        out_specs=[tile] * 4, out_shape=[jax.ShapeDtypeStruct((R, n * Cp), F32)] * 4,
        compiler_params=_cp(dimension_semantics=("arbitrary", "arbitrary")),
    )(*owns, *recvs, w, m, v)


def adam_cols(name, mine, recv, w, m, v, c_idx):
    half, C = mine.shape
    nt = _half_tiles(half)
    tr = half // nt

    def body(s_ref, a_ref, b_ref, w_ref, m_ref, v_ref, g_ref, d_ref, nm_ref, nv_ref):
        g = jnp.where(pl.program_id(0) == s_ref[0], a_ref[...], b_ref[...])
        delta, m2, v2 = _adamw(w_ref[...], g, m_ref[...], v_ref[...])
        g_ref[...] = g
        d_ref[...] = delta
        nm_ref[...] = m2
        nv_ref[...] = v2

    part = pl.BlockSpec((tr, C), lambda hh, i, s: (i, 0))
    tile = pl.BlockSpec((tr, C), lambda hh, i, s: (hh * nt + i, 0))
    grid_spec = pltpu.PrefetchScalarGridSpec(
        num_scalar_prefetch=1, grid=(2, nt), in_specs=[part, part, tile, tile, tile], out_specs=[tile] * 4)
    return pl.pallas_call(
        body, name=name, grid_spec=grid_spec, out_shape=[jax.ShapeDtypeStruct((2 * half, C), F32)] * 4,
        compiler_params=_cp(dimension_semantics=("arbitrary", "arbitrary")),
    )(jnp.stack([c_idx]), mine, recv, w, m, v)


ROW_N1, ROW_N2, ROW_BG, ROW_QN, ROW_KN, ROW_CB, ROW_LW, ROW_LB, ROW_CW = 0, 1, 2, 4, 5, 6, 7, 8, 9
PACK_ROWS = 40
SMALL = ("norm1_w", "norm2_w", "b_gate", "q_norm_w", "k_norm_w", "conv_b", "conv_ln_w", "conv_ln_b", "conv_w")


def small_sync_adam(g, w, m, v, sq, sides=()):
    ns = len(SMALL)

    def body(*refs):
        gi = dict(zip(SMALL, refs[:ns]))
        wi = dict(zip(SMALL, refs[ns:2 * ns]))
        mi = dict(zip(SMALL, refs[2 * ns:3 * ns]))
        vi = dict(zip(SMALL, refs[3 * ns:4 * ns]))
        sq_ref = refs[4 * ns]
        outs = refs[4 * ns + 1:8 * ns + 1]
        loss_ref = refs[8 * ns + 1]
        pack, recv, tot, send_sems, recv_sems = refs[8 * ns + 2:]
        x, y, c, _ = _place()
        me = 4 * x + 2 * y + c

        pack[...] = jnp.zeros_like(pack)
        pack[ROW_KN:ROW_KN + 1, LANES:2 * LANES] = jnp.full((1, LANES), (0.5 / D) * jnp.sum(sq_ref[...]), F32)
        pack[ROW_N1:ROW_N1 + 1, :] = gi["norm1_w"][...]
        pack[ROW_N2:ROW_N2 + 1, :] = gi["norm2_w"][...]
        pack[ROW_BG:ROW_BG + 2, :] = gi["b_gate"][...]
        pack[ROW_QN:ROW_QN + 1, 0:HD] = gi["q_norm_w"][...]
        pack[ROW_KN:ROW_KN + 1, 0:HD] = gi["k_norm_w"][...]
        pack[ROW_CB:ROW_CB + 1, 0:CC] = gi["conv_b"][...]
        pack[ROW_LW:ROW_LW + 1, 0:CC] = gi["conv_ln_w"][...]
        pack[ROW_LB:ROW_LB + 1, 0:CC] = gi["conv_ln_b"][...]
        pack[ROW_CW:ROW_CW + KW, 0:CC] = gi["conv_w"][...]

        copies = []
        for k in range(1, NDEV):
            peer = (x ^ (k >> 2), y ^ ((k >> 1) & 1), c ^ (k & 1))
            cp = pltpu.make_async_remote_copy(
                src_ref=pack, dst_ref=recv.at[me], send_sem=send_sems.at[k - 1], recv_sem=recv_sems.at[k - 1],
                device_id=peer, device_id_type=MESH)
            cp.start()
            copies.append(cp)
        recv[me] = pack[...]
        for cp in copies:
            cp.wait()
        acc = recv[0]
        for p in range(1, NDEV):
            acc = acc + recv[p]
        tot[...] = acc

        def shard_grad(name):
            if name == "b_gate":
                return tot[ROW_BG:ROW_BG + 2, pl.ds(pl.multiple_of(me * LANES, LANES), LANES)]
            if name == "conv_w":
                win = tot[ROW_CW:ROW_CW + KW, pl.ds(pl.multiple_of((me // 2) * LANES, LANES), LANES)]
                return jnp.where(me % 2 == 1, win[:, HD:LANES], win[:, 0:HD])
            row = {"norm1_w": ROW_N1, "norm2_w": ROW_N2, "q_norm_w": ROW_QN, "k_norm_w": ROW_KN,
                   "conv_b": ROW_CB, "conv_ln_w": ROW_LW, "conv_ln_b": ROW_LB}[name]
            return tot[row:row + 1, 0:wi[name].shape[1]]

        for i, name in enumerate(SMALL):
            gr = shard_grad(name)
            delta, m2, v2 = _adamw(wi[name][...], gr, mi[name][...], vi[name][...])
            outs[4 * i][...] = gr
            outs[4 * i + 1][...] = delta
            outs[4 * i + 2][...] = m2
            outs[4 * i + 3][...] = v2
        loss_ref[...] = tot[ROW_KN:ROW_KN + 1, LANES:2 * LANES]

    out_shape = []
    for name in SMALL:
        out_shape += [jax.ShapeDtypeStruct(w[name].shape, F32)] * 4
    out_shape.append(jax.ShapeDtypeStruct((1, LANES), F32))
    args = [g[k] for k in SMALL] + [w[k] for k in SMALL] + [m[k] for k in SMALL] + [v[k] for k in SMALL] + [sq]
    res = _call(
        body, sides, name="small_sync_adam", grid=(1,), in_specs=[VMEM] * len(args),
        out_specs=[VMEM] * len(out_shape), out_shape=out_shape,
        scratch_shapes=[pltpu.VMEM((PACK_ROWS, D), F32), pltpu.VMEM((NDEV, PACK_ROWS, D), F32),
                        pltpu.VMEM((PACK_ROWS, D), F32), _sems(NDEV - 1), _sems(NDEV - 1)],
        args=args)
    res, side_outs = res if sides else (res, None)
    out = {name: tuple(res[4 * i:4 * i + 4]) for i, name in enumerate(SMALL)}
    loss = res[4 * ns][0, 0]
    return (out, loss, side_outs) if sides else (out, loss)


MATS = ("w_in", "w_o_attn", "w_pw_conv", "w_out", "w_ffn_in", "w_ffn_out")
TRANSPOSED = ("w_in", "w_ffn_in")
WEIGHTS = ("norm1_w", "w_in", "b_gate", "q_norm_w", "k_norm_w", "w_o_attn", "conv_w", "conv_b", "conv_ln_w",
           "conv_ln_b", "w_pw_conv", "w_out", "norm2_w", "w_ffn_in", "w_ffn_out")


def _blocks_to_cols(blocks):
    n, R, C = blocks.shape
    return blocks.transpose(1, 0, 2).reshape(R, n * C)


def kernel(x, positions, norm1_w, w_in, b_gate, q_norm_w, k_norm_w, w_o_attn, conv_w, conv_b, conv_ln_w, conv_ln_b, w_pw_conv, w_out, norm2_w, w_ffn_in, w_ffn_out, loss_target, m_norm1_w, m_w_in, m_b_gate, m_q_norm_w, m_k_norm_w, m_w_o_attn, m_conv_w, m_conv_b, m_conv_ln_w, m_conv_ln_b, m_w_pw_conv, m_w_out, m_norm2_w, m_w_ffn_in, m_w_ffn_out, v_norm1_w, v_w_in, v_b_gate, v_q_norm_w, v_k_norm_w, v_w_o_attn, v_conv_w, v_conv_b, v_conv_ln_w, v_conv_ln_b, v_w_pw_conv, v_w_out, v_norm2_w, v_w_ffn_in, v_w_ffn_out):
    w = dict(norm1_w=norm1_w, w_in=w_in, b_gate=b_gate, q_norm_w=q_norm_w, k_norm_w=k_norm_w, w_o_attn=w_o_attn,
             conv_w=conv_w, conv_b=conv_b, conv_ln_w=conv_ln_w, conv_ln_b=conv_ln_b, w_pw_conv=w_pw_conv,
             w_out=w_out, norm2_w=norm2_w, w_ffn_in=w_ffn_in, w_ffn_out=w_ffn_out)
    m = dict(norm1_w=m_norm1_w, w_in=m_w_in, b_gate=m_b_gate, q_norm_w=m_q_norm_w, k_norm_w=m_k_norm_w,
             w_o_attn=m_w_o_attn, conv_w=m_conv_w, conv_b=m_conv_b, conv_ln_w=m_conv_ln_w,
             conv_ln_b=m_conv_ln_b, w_pw_conv=m_w_pw_conv, w_out=m_w_out, norm2_w=m_norm2_w,
             w_ffn_in=m_w_ffn_in, w_ffn_out=m_w_ffn_out)
    v = dict(norm1_w=v_norm1_w, w_in=v_w_in, b_gate=v_b_gate, q_norm_w=v_q_norm_w, k_norm_w=v_k_norm_w,
             w_o_attn=v_w_o_attn, conv_w=v_conv_w, conv_b=v_conv_b, conv_ln_w=v_conv_ln_w,
             conv_ln_b=v_conv_ln_b, w_pw_conv=v_w_pw_conv, w_out=v_w_out, norm2_w=v_norm2_w,
             w_ffn_in=v_w_ffn_in, w_ffn_out=v_w_ffn_out)
    def two_d(t):
        t = {k: (a[0] if a.ndim == 3 else a) for k, a in t.items()}
        return {k: (a.T if k in TRANSPOSED else a) for k, a in t.items()}

    w, m, v = two_d(w), two_d(m), two_d(v)

    x2, target = x[0], loss_target[0]
    c_idx = lax.axis_index("c").astype(jnp.int32)
    chip_idx = (2 * lax.axis_index("x") + lax.axis_index("y")).astype(jnp.int32)
    qw2 = jnp.tile(w["q_norm_w"], (1, 2))
    kw2 = jnp.tile(w["k_norm_w"], (1, 2))

    tabs, ((w_in_blocks,), (bg_blocks,), (cw_blocks,)) = rope_tables(
        positions.reshape(S, 1),
        sides=(ag_blocks_relay(w["w_in"], BF16), ag_blocks(w["b_gate"], F32), ag_blocks(w["conv_w"], F32)))
    w_in_t = w_in_blocks.reshape(INW, D)
    b_gate_f, conv_w_f = _blocks_to_cols(bg_blocks), _blocks_to_cols(cw_blocks)
    (h_t, proj), ((w_o_f,), (w_pw_f,), (w_out_blocks,)) = in_proj(
        x2, w["norm1_w"], w_in_t, sides=(ag_cols(w["w_o_attn"]), ag_cols(w["w_pw_conv"]), ag_blocks_relay(w["w_out"], BF16)))
    w_out_f = w_out_blocks.reshape(D, D)
    (attn, lse), ((w_ffn_in_blocks,),) = attn_fwd(proj, tabs, qw2, kw2, sides=(ag_blocks_relay(w["w_ffn_in"], BF16),))
    w_ffn_in_t = w_ffn_in_blocks.reshape(2 * FF, D)
    cpre, u3 = conv_fwd(proj, conv_w_f, w["conv_b"], w["conv_ln_w"], w["conv_ln_b"])
    x1, z, ya, yb = mix_out(x2, proj, b_gate_f, attn, u3, w_o_f, w_pw_f, w_out_f)
    (h2, gu, f), ((w_ffn_out_blocks,),) = ffn_in(x1, w["norm2_w"], w_ffn_in_t, sides=(ag_blocks_relay(w["w_ffn_out"], BF16),))
    w_ffn_out_f = w_ffn_out_blocks.reshape(FF, D)
    dy, dyb, sq = ffn_out_loss(x1, f, w_ffn_out_f, target)

    g = {}
    g_ffn_out = mm_tn("gw_ffn_out", f, dyb, FF // 2, D).reshape(NDEV, FF // NDEV, D)
    (d_gu, d_x1, d_x1b, g["norm2_w"]), ((ra_ffn_out,),) = ffn_bwd(
        dy, dyb, gu, x1, w["norm2_w"], w_ffn_in_t, w_ffn_out_f, sides=(rs_to_sibling([("rows", g_ffn_out)]),))
    pb_ffn_out, own_ffn_out = chip_sum("chip_sum_w_ffn_out", g_ffn_out, ra_ffn_out, c_idx, chip_idx)
    g_ffn_in, ((rb_ffn_out,),) = mm_tn("gw_ffn_in", d_gu, h2, FF // 2, D,
                                       sides=(rs_to_chips([("rows", pb_ffn_out)]),))
    g_ffn_in = g_ffn_in.reshape(NDEV, 2 * FF // NDEV, D)
    g_out = mm_tn("gw_out", z, d_x1b, D // 2, D).reshape(NDEV, D // NDEV, D)
    (d_ya, d_yb, d_gl, d_attn, d_u3, g["b_gate"]), ((ra_ffn_in,),) = out_bwd(
        d_x1b, proj, b_gate_f, ya, yb, w_o_f, w_pw_f, w_out_f, sides=(rs_to_sibling([("rows", g_ffn_in)]),))
    pb_ffn_in, own_ffn_in = chip_sum("chip_sum_w_ffn_in", g_ffn_in, ra_ffn_in, c_idx, chip_idx)
    g_w_o = mm_tn("gw_o_attn", attn, d_ya, CC, D)
    g_w_pw = mm_tn("gw_pw_conv", u3, d_yb, CC, D)
    (d_conv, g["conv_w"], g["conv_b"], g["conv_ln_w"], g["conv_ln_b"]), ((ra_out, ra_w_o, ra_w_pw),) = conv_bwd(
        proj, cpre, d_u3, conv_w_f, conv_w_f[::-1], w["conv_ln_w"], w["conv_ln_b"],
        sides=(rs_to_sibling([("rows", g_out), ("cols", g_w_o), ("cols", g_w_pw)]),))
    pb_out, own_out = chip_sum("chip_sum_w_out", g_out, ra_out, c_idx, chip_idx)
    pb_w_o, own_w_o = chip_sum_cols("chip_sum_w_o_attn", g_w_o, ra_w_o, c_idx, chip_idx)
    pb_w_pw, own_w_pw = chip_sum_cols("chip_sum_w_pw_conv", g_w_pw, ra_w_pw, c_idx, chip_idx)
    (d_q, d_k, d_v, gqw, gkw), ((rb_ffn_in, rb_out, rb_w_o, rb_w_pw),) = attn_bwd(
        proj, tabs, qw2, kw2, d_attn, attn, lse,
        sides=(rs_to_chips([("rows", pb_ffn_in), ("rows", pb_out), ("cols", pb_w_o), ("cols", pb_w_pw)]),))
    g["q_norm_w"] = gqw[0:1, 0:HD] + gqw[0:1, HD:LANES]
    g["k_norm_w"] = gkw[0:1, 0:HD] + gkw[0:1, HD:LANES]
    mine_w_o, theirs_w_o = col_final("col_final_w_o_attn", own_w_o, rb_w_o, c_idx)
    mine_w_pw, theirs_w_pw = col_final("col_final_w_pw_conv", own_w_pw, rb_w_pw, c_idx)
    d_segs = (d_q, d_k, d_v, d_conv, d_gl)
    parts, to_sibling, to_chips, owns, from_chips = [], None, None, [], []
    for k in range(GW_IN_PARTS):
        sides = [rs_swap_halves([theirs_w_o, theirs_w_pw])] if k == 0 else []
        sides += [s for s in (to_chips, to_sibling) if s is not None]
        part, outs = gw_in_t("gw_in_%d" % k, h_t, d_segs, k, sides=tuple(sides))
        if k == 0:
            (rc_w_o, rc_w_pw), outs = outs[0], outs[1:]
        outs = list(outs)
        if to_chips is not None:
            from_chips.append(outs.pop(0)[0])
        if to_sibling is not None:
            pb, own = chip_sum("chip_sum_w_in_%d" % (k - 1), parts[-1], outs.pop(0)[0], c_idx, chip_idx)
            owns.append(own)
            to_chips = rs_to_chips([("rows", pb)])
        else:
            to_chips = None
        parts.append(part.reshape(NDEV, INW // NDEV, D // GW_IN_PARTS))
        to_sibling = rs_to_sibling([("rows", parts[-1])])
    (grad_x, g["norm1_w"]), ((rb_prev,), (ra_last,)) = in_bwd(
        d_q, d_k, d_v, d_conv, d_gl, w_in_t, x2, d_x1, w["norm1_w"], sides=(to_chips, to_sibling))
    from_chips.append(rb_prev)
    pb, own = chip_sum("chip_sum_w_in_%d" % (GW_IN_PARTS - 1), parts[-1], ra_last, c_idx, chip_idx)
    owns.append(own)
    small, loss, ((rb_last,),) = small_sync_adam(g, w, m, v, sq, sides=(rs_to_chips([("rows", pb)]),))
    from_chips.append(rb_last)

    res = {
        "w_in": shard_adam("adam_w_in", owns, from_chips, w["w_in"], m["w_in"], v["w_in"]),
        "w_ffn_in": shard_adam("adam_w_ffn_in", [own_ffn_in], [rb_ffn_in], w["w_ffn_in"], m["w_ffn_in"], v["w_ffn_in"]),
        "w_o_attn": adam_cols("adam_w_o_attn", mine_w_o, rc_w_o, w["w_o_attn"], m["w_o_attn"], v["w_o_attn"], c_idx),
        "w_pw_conv": adam_cols("adam_w_pw_conv", mine_w_pw, rc_w_pw, w["w_pw_conv"], m["w_pw_conv"], v["w_pw_conv"], c_idx),
        "w_out": shard_adam("adam_w_out", [own_out], [rb_out], w["w_out"], m["w_out"], v["w_out"]),
        "w_ffn_out": shard_adam("adam_w_ffn_out", [own_ffn_out], [rb_ffn_out],
                                w["w_ffn_out"], m["w_ffn_out"], v["w_ffn_out"]),
    }
    res = {k: tuple(a.T if k in TRANSPOSED else a for a in r) for k, r in res.items()}
    res.update(small)

    def shaped(name, a):
        return a.reshape((1,) + a.shape) if name in MATS or name in ("b_gate", "conv_w") else a

    outs = [loss, grad_x.reshape(1, S, D)]
    for i in range(4):
        outs += [shaped(k, res[k][i]) for k in WEIGHTS]
    return tuple(outs)
```

```python
import functools
from typing import Callable, NamedTuple, Optional

import numpy as np
import jax
import jax.numpy as jnp
from jax import lax
from jax.experimental import pallas as pl
from jax.experimental.pallas import tpu as pltpu

F32 = jnp.float32
BF16 = jnp.bfloat16

S = 2048
D = 1024
HD = 64
QKV = 1536
CC = 512
KW = 31
FF = 2816
INW = 7680
OFF_Q, OFF_K, OFF_V, OFF_CA, OFF_CB, OFF_GA, OFF_GB = 0, 1536, 3072, 4608, 5120, 5632, 6656
DILATIONS = (1, 4, 16)
HALF_SPAN = 64
EPS = 1e-6
NEG_INF = -1e30
ROPE_THETA = 500000.0
ROT_DIM = 16

ADAM_LR = 0.001
ADAM_B1 = 0.9
ADAM_B2 = 0.999
ADAM_EPS = 1e-08
ADAM_WD = 0.01
ADAM_STEP = 10

NDEV = 8
LANES = 128
TM = 256
TQ = 128
VMEM_LIMIT = 56 * 1024 * 1024
MESH = pl.DeviceIdType.MESH


def _cp(**kw):
    return pltpu.CompilerParams(vmem_limit_bytes=VMEM_LIMIT, **kw)


def _row(width, col=0, tm=TM):
    return pl.BlockSpec((tm, width), lambda i: (i, col))


PLANE = 512


def _planes(width, tm=TM):
    return pl.BlockSpec((width // PLANE, tm, PLANE), lambda i: (0, i, 0))


def _res(shape):
    nd = len(shape)
    return pl.BlockSpec(shape, lambda *_: (0,) * nd, pipeline_mode=pl.Buffered(1))


def _dot(a, b):
    return jnp.dot(a, b, preferred_element_type=F32)


def _dot_nt(a, b):
    return lax.dot_general(a, b, (((1,), (1,)), ((), ())), preferred_element_type=F32)


def _dot_tn(a, b):
    return lax.dot_general(a, b, (((0,), (0,)), ((), ())), preferred_element_type=F32)


def _sigmoid(x):
    return jax.nn.sigmoid(x)


def _dsilu(x, sg):
    return sg * (1.0 + x * (1.0 - sg))


ANY = pl.BlockSpec(memory_space=pl.ANY)
VMEM = pl.BlockSpec(memory_space=pltpu.VMEM)


class Side(NamedTuple):
    args: tuple
    in_specs: tuple
    out_shape: tuple
    scratch: tuple
    start: Callable
    finish: Callable
    mid: Optional[Callable] = None


def _call(body, sides=(), *, name, grid, in_specs, out_specs, out_shape, scratch_shapes=(), args):
    ni, no, ns = len(in_specs), len(out_specs), len(scratch_shapes)
    cnt = [(len(s.args), len(s.out_shape), len(s.scratch)) for s in sides]

    def take(refs, pos, n):
        return refs[pos:pos + n], pos + n

    def full(*refs):
        m_in, pos = take(refs, 0, ni)
        s_in = []
        for a, _, _ in cnt:
            r, pos = take(refs, pos, a)
            s_in.append(r)
        m_out, pos = take(refs, pos, no)
        s_out = []
        for _, o, _ in cnt:
            r, pos = take(refs, pos, o)
            s_out.append(r)
        m_scr, pos = take(refs, pos, ns)
        s_scr = []
        for _, _, c in cnt:
            r, pos = take(refs, pos, c)
            s_scr.append(r)
        if sides:
            first = functools.reduce(jnp.logical_and, [pl.program_id(d) == 0 for d in range(len(grid))])
            last = functools.reduce(jnp.logical_and, [pl.program_id(d) == g - 1 for d, g in enumerate(grid)])

            @pl.when(first)
            def _():
                for s, a, o, c in zip(sides, s_in, s_out, s_scr):
                    s.start(a, o, c)

            steps = int(np.prod(grid))
            mid_step = (2 * steps) // 3
            if steps > 1 and any(s.mid is not None for s in sides):
                step = functools.reduce(lambda acc, d: acc * grid[d] + pl.program_id(d), range(len(grid)), 0)

                @pl.when(step == mid_step)
                def _():
                    for s, a, o, c in zip(sides, s_in, s_out, s_scr):
                        if s.mid is not None:
                            s.mid(a, o, c)

        body(*m_in, *m_out, *m_scr)
        if sides:
            @pl.when(last)
            def _():
                for s, a, o, c in zip(sides, s_in, s_out, s_scr):
                    if s.mid is not None and steps == 1:
                        s.mid(a, o, c)
                    s.finish(a, o, c)

    res = pl.pallas_call(
        full, name=name, grid=grid,
        in_specs=list(in_specs) + [sp for s in sides for sp in s.in_specs],
        out_specs=list(out_specs) + [ANY for s in sides for _ in s.out_shape],
        out_shape=list(out_shape) + [o for s in sides for o in s.out_shape],
        scratch_shapes=list(scratch_shapes) + [c for s in sides for c in s.scratch],
        compiler_params=_cp(dimension_semantics=("arbitrary",) * len(grid)),
    )(*args, *[a for s in sides for a in s.args])
    res = list(res)
    if not sides:
        return res
    outs, pos = take(res, 0, no)
    side_outs = []
    for _, o, _ in cnt:
        r, pos = take(res, pos, o)
        side_outs.append(r)
    return outs, side_outs


def _inv_freq_lanes():
    inv = np.float32(ROPE_THETA) ** (-np.arange(0, ROT_DIM, 2, dtype=np.float32) / np.float32(ROT_DIM))
    lane = np.arange(LANES) % HD
    out = np.where(lane < ROT_DIM, inv[lane % (ROT_DIM // 2)], 0.0).astype(np.float32)
    return jnp.asarray(out.reshape(1, LANES))


def rope_tables(pos_col, sides=()):
    def body(p_ref, f_ref, c_ref, s1_ref, s2_ref):
        ang = p_ref[...].astype(F32) * f_ref[...]
        lane = lax.broadcasted_iota(jnp.int32, ang.shape, 1) % HD
        cs = jnp.cos(ang)
        sn = jnp.sin(ang)
        c_ref[...] = jnp.where(lane < ROT_DIM, cs, 1.0)
        s1_ref[...] = jnp.where(lane < ROT_DIM // 2, -sn, 0.0)
        s2_ref[...] = jnp.where(lane < ROT_DIM // 2, 0.0, jnp.where(lane < ROT_DIM, sn, 0.0))

    sds = jax.ShapeDtypeStruct((S, LANES), F32)
    return _call(
        body, sides, name="rope_tables", grid=(S // TM,),
        in_specs=[_row(1), pl.BlockSpec((1, LANES), lambda i: (0, 0))],
        out_specs=[_row(LANES)] * 3, out_shape=[sds] * 3,
        args=(pos_col, _inv_freq_lanes()))


def _rope(v, c, s1, s2):
    return v * c + pltpu.roll(v, LANES - 8, axis=1) * s1 + pltpu.roll(v, 8, axis=1) * s2


def _rope_t(d, c, s1, s2):
    return d * c - pltpu.roll(d, LANES - 8, axis=1) * s1 - pltpu.roll(d, 8, axis=1) * s2


def _head_mat():
    r = lax.broadcasted_iota(jnp.int32, (LANES, LANES), 0) // HD
    c = lax.broadcasted_iota(jnp.int32, (LANES, LANES), 1) // HD
    return jnp.where(r == c, 1.0 / HD, 0.0).astype(BF16)


def _head_mean(t, e):
    hi = t.astype(BF16)
    rest = (t - hi.astype(F32)).astype(BF16)
    return _dot(hi, e) + _dot(rest, e)


def in_proj(x, norm_w, w_in, sides=()):
    nchunk = 5
    cw = INW // nchunk

    def body(x_ref, nw_ref, w_ref, ht_ref, p_ref):
        xv = x_ref[...]
        r = lax.rsqrt(jnp.mean(xv * xv, axis=-1, keepdims=True) + EPS)
        hf = xv * r * nw_ref[...]
        ht_ref[...] = hf.T.astype(BF16)
        h = hf.astype(BF16)
        for j in range(nchunk):
            p_ref[:, j * cw:(j + 1) * cw] = _dot_nt(h, w_ref[j * cw:(j + 1) * cw, :])

    return _call(
        body, sides, name="in_proj", grid=(S // TM,),
        in_specs=[_row(D), _res((1, D)), _res((INW, D))],
        out_specs=[pl.BlockSpec((D, TM), lambda i: (0, i)), _row(INW)],
        out_shape=[jax.ShapeDtypeStruct((D, S), BF16), jax.ShapeDtypeStruct((S, INW), F32)],
        args=(x, norm_w, w_in))


def _qk_specs():
    nb = QKV // LANES
    return [pl.BlockSpec((S, LANES), functools.partial(lambda hp, g, o: (0, o + g * 4 + hp), o=o))
            for o in (OFF_Q // LANES, OFF_K // LANES, OFF_V // LANES)]


def _tab_specs():
    return [pl.BlockSpec((S, LANES), lambda hp, g: (0, 0), pipeline_mode=pl.Buffered(1))] * 3


def _vec_spec():
    return pl.BlockSpec((1, LANES), lambda hp, g: (0, 0))


def _sub_rows(r, d, start, n):
    if d == 1:
        return pl.ds(start, n)
    return pl.ds(r + d * start, n, stride=d)


def _band_window(i, L):
    W = min(TQ + 2 * HALF_SPAN, L)
    q0 = pl.multiple_of(i * TQ, TQ)
    k0 = pl.multiple_of(jnp.clip(q0 - HALF_SPAN, 0, L - W), HALF_SPAN)
    qpos = q0 + (lax.broadcasted_iota(jnp.int32, (2 * TQ, W), 0) & (TQ - 1))
    kpos = k0 + lax.broadcasted_iota(jnp.int32, (2 * TQ, W), 1)
    valid = jnp.abs(qpos - kpos) <= HALF_SPAN
    return W, q0, k0, valid


def _stack_heads(t, lo):
    z = jnp.zeros_like(t)
    return jnp.concatenate([jnp.where(lo, t, z), jnp.where(lo, z, t)], axis=0)


def _unstack_heads(t2, lo):
    return jnp.where(lo, t2[0:TQ], t2[TQ:2 * TQ])


CHAINS = 4


def _interleave(d):
    ru = min(d, CHAINS)
    return ru, min(CHAINS // ru, S // d // TQ)


def _for_blocks(n, fn):
    if n == 1:
        fn(0)
    else:
        def it(j, _):
            fn(j)
            return 0
        lax.fori_loop(0, n, it, 0)


def attn_fwd(proj, tabs, qw2, kw2, sides=()):
    CH = 256

    def body(q_ref, k_ref, v_ref, c_ref, s1_ref, s2_ref, qw_ref, kw_ref, at_ref, ls_ref,
             qs, ks, vs, osub, lsub, onat, lnat, qn, kn):
        g = pl.program_id(1)
        lo = lax.broadcasted_iota(jnp.int32, (1, LANES), 1) < HD
        e = _head_mat()

        def prep(i, _):
            rows = pl.ds(pl.multiple_of(i * CH, CH), CH)
            c, s1, s2 = c_ref[rows, :], s1_ref[rows, :], s2_ref[rows, :]
            for t_ref, w_ref, out, scale in ((q_ref, qw_ref, qn, HD ** -0.5), (k_ref, kw_ref, kn, 1.0)):
                t = t_ref[rows, :]
                r = lax.rsqrt(_head_mean(t * t, e) + EPS)
                out[rows, :] = _rope(t * r * w_ref[...], c, s1, s2) * scale
            return 0

        lax.fori_loop(0, S // CH, prep, 0, unroll=4)

        def group(gi, d):
            L = S // d

            ru, nb = _interleave(d)

            def stage(r, off):
                for c0 in range(0, L, CH):
                    n = min(CH, L)
                    rows = _sub_rows(r, d, c0, n)
                    dst = pl.ds(off + c0, n)
                    qs[dst, :] = qn[rows, :].astype(BF16)
                    ks[dst, :] = kn[rows, :].astype(BF16)
                    vs[dst, :] = v_ref[rows, :].astype(BF16)

            def one(off, i):
                W, q0, k0, valid = _band_window(i, L)
                q2 = _stack_heads(qs[pl.ds(off + q0, TQ), :], lo)
                sc = jnp.where(valid, _dot_nt(q2, ks[pl.ds(off + k0, W), :]), NEG_INF)
                m = jnp.max(sc, axis=-1, keepdims=True)
                p = jnp.exp(sc - m)
                den = jnp.sum(p, axis=-1, keepdims=True)
                o2 = _dot(p.astype(BF16), vs[pl.ds(off + k0, W), :]) / den
                l2 = jnp.broadcast_to(m + jnp.log(den), (2 * TQ, LANES))
                osub[pl.ds(off + q0, TQ), :] = _unstack_heads(o2, lo)
                lsub[pl.ds(off + q0, TQ), :] = _unstack_heads(l2, lo)

            def unstage(r, off):
                for c0 in range(0, L, CH):
                    n = min(CH, L)
                    rows = _sub_rows(r, d, c0, n)
                    onat[gi, rows, :] = osub[pl.ds(off + c0, n), :]
                    lnat[gi, rows, :] = lsub[pl.ds(off + c0, n), :]

            def step(t, _):
                for u in range(ru):
                    stage(t * ru + u, u * L)
                _for_blocks(L // TQ // nb, lambda j: [one(u * L, j * nb + b) for u in range(ru) for b in range(nb)])
                for u in range(ru):
                    unstage(t * ru + u, u * L)
                return 0

            lax.fori_loop(0, d // ru, step, 0)

        for gi, d in enumerate(DILATIONS):
            pl.when(g == gi)(functools.partial(group, gi, d))

        @pl.when(g == len(DILATIONS) - 1)
        def _():
            def mix(i, _):
                rows = pl.ds(pl.multiple_of(i * CH, CH), CH)
                l0, l1, l2 = lnat[0, rows, :], lnat[1, rows, :], lnat[2, rows, :]
                m = jnp.maximum(jnp.maximum(l0, l1), l2)
                e0, e1, e2 = jnp.exp(l0 - m), jnp.exp(l1 - m), jnp.exp(l2 - m)
                den = e0 + e1 + e2
                a = (e0 * onat[0, rows, :] + e1 * onat[1, rows, :] + e2 * onat[2, rows, :]) / den
                at_ref[rows, :] = a.astype(BF16)
                ls_ref[rows, :] = m + jnp.log(den)
                return 0

            lax.fori_loop(0, S // CH, mix, 0)

    out_spec = pl.BlockSpec((S, LANES), lambda hp, g: (0, hp))
    return _call(
        body, sides, name="attn_fwd", grid=(4, 3),
        in_specs=_qk_specs() + _tab_specs() + [_vec_spec(), _vec_spec()],
        out_specs=[out_spec, out_spec],
        out_shape=[jax.ShapeDtypeStruct((S, CC), BF16), jax.ShapeDtypeStruct((S, CC), F32)],
        scratch_shapes=[pltpu.VMEM((S, LANES), BF16)] * 3 + [pltpu.VMEM((S, LANES), F32)] * 2
        + [pltpu.VMEM((3, S, LANES), F32)] * 2 + [pltpu.VMEM((S, LANES), F32)] * 2,
        args=(proj, proj, proj, *tabs, qw2, kw2))


def attn_bwd(proj, tabs, qw2, kw2, d_attn, attn, lse, sides=()):
    CH = 256

    def body(q_ref, k_ref, v_ref, c_ref, s1_ref, s2_ref, qw_ref, kw_ref, do_ref, at_ref, ls_ref,
             dq_ref, dk_ref, dv_ref, gqw_ref, gkw_ref,
             qs, ks, vs, dos, dsub, lsub, dqs, dks, dvs, dnat, qx, kx, dvn, tnq, tnk, rrq, rrk):
        hp, g = pl.program_id(0), pl.program_id(1)
        lo = lax.broadcasted_iota(jnp.int32, (1, LANES), 1) < HD
        e = _head_mat()
        both = ((q_ref, qw_ref, qx, tnq, rrq, HD ** -0.5), (k_ref, kw_ref, kx, tnk, rrk, 1.0))

        @pl.when((hp == 0) & (g == 0))
        def _():
            gqw_ref[...] = jnp.zeros_like(gqw_ref)
            gkw_ref[...] = jnp.zeros_like(gkw_ref)

        def prep(i, _):
            rows = pl.ds(pl.multiple_of(i * CH, CH), CH)
            dnat[rows, :] = _head_mean(do_ref[rows, :] * at_ref[rows, :].astype(F32), e) * float(HD)
            c, s1, s2 = c_ref[rows, :], s1_ref[rows, :], s2_ref[rows, :]
            for t_ref, w_ref, x, tn_s, rr_s, scale in both:
                t = t_ref[rows, :]
                rr = lax.rsqrt(_head_mean(t * t, e) + EPS)
                tn = t * rr
                rr_s[rows, :] = rr
                tn_s[rows, :] = tn
                x[rows, :] = _rope(tn * w_ref[...], c, s1, s2) * scale
            return 0

        lax.fori_loop(0, S // CH, prep, 0, unroll=4)

        def group(d):
            L = S // d

            ru, nb = _interleave(d)

            def stage(r, off):
                for c0 in range(0, L, CH):
                    n = min(CH, L)
                    rows = _sub_rows(r, d, c0, n)
                    dst = pl.ds(off + c0, n)
                    qs[dst, :] = qx[rows, :].astype(BF16)
                    ks[dst, :] = kx[rows, :].astype(BF16)
                    vs[dst, :] = v_ref[rows, :].astype(BF16)
                    dos[dst, :] = do_ref[rows, :].astype(BF16)
                    dsub[dst, :] = dnat[rows, :]
                    lsub[dst, :] = ls_ref[rows, :]
                    dks[dst, :] = jnp.zeros((n, LANES), F32)
                    dvs[dst, :] = jnp.zeros((n, LANES), F32)

            def one(off, i):
                W, q0, k0, valid = _band_window(i, L)
                qrows, krows = pl.ds(off + q0, TQ), pl.ds(off + k0, W)
                q2 = _stack_heads(qs[qrows, :], lo)
                do2 = _stack_heads(dos[qrows, :], lo)
                kk, vv = ks[krows, :], vs[krows, :]
                lse_b, dd_b = lsub[qrows, :], dsub[qrows, :]
                lse2 = jnp.concatenate([lse_b[:, 0:1], lse_b[:, HD:HD + 1]], axis=0)
                dd2 = jnp.concatenate([dd_b[:, 0:1], dd_b[:, HD:HD + 1]], axis=0)
                sc = jnp.where(valid, _dot_nt(q2, kk), NEG_INF)
                p = jnp.exp(sc - lse2)
                ds = (p * (_dot_nt(do2, vv) - dd2)).astype(BF16)
                dqs[qrows, :] = _unstack_heads(_dot(ds, kk), lo)
                dks[krows, :] = dks[krows, :] + _dot_tn(ds, q2)
                dvs[krows, :] = dvs[krows, :] + _dot_tn(p.astype(BF16), do2)

            def unstage(r, off):
                for c0 in range(0, L, CH):
                    n = min(CH, L)
                    rows = _sub_rows(r, d, c0, n)
                    src = pl.ds(off + c0, n)
                    qx[rows, :] = dqs[src, :]
                    kx[rows, :] = dks[src, :]
                    dvn[rows, :] = dvs[src, :]

            def step(t, _):
                for u in range(ru):
                    stage(t * ru + u, u * L)
                _for_blocks(L // TQ // nb, lambda j: [one(u * L, j * nb + b) for u in range(ru) for b in range(nb)])
                for u in range(ru):
                    unstage(t * ru + u, u * L)
                return 0

            lax.fori_loop(0, d // ru, step, 0)

        for gi, d in enumerate(DILATIONS):
            pl.when(g == gi)(functools.partial(group, d))

        def emit(i, _):
            rows = pl.ds(pl.multiple_of(i * CH, CH), CH)
            c, s1, s2 = c_ref[rows, :], s1_ref[rows, :], s2_ref[rows, :]
            for (_, w_ref, x, tn_s, rr_s, scale), out, gw_ref in zip(both, (dq_ref, dk_ref), (gqw_ref, gkw_ref)):
                tn = tn_s[rows, :]
                dy = _rope_t(x[rows, :] * scale, c, s1, s2)
                gw_ref[0:1, :] = gw_ref[0:1, :] + jnp.sum(dy * tn, axis=0, keepdims=True)
                dtn = dy * w_ref[...]
                out[rows, :] = (rr_s[rows, :] * (dtn - tn * _head_mean(dtn * tn, e))).astype(BF16)
            dv_ref[rows, :] = dvn[rows, :].astype(BF16)
            return 0

        lax.fori_loop(0, S // CH, emit, 0, unroll=4)

    nat_spec = pl.BlockSpec((S, LANES), lambda hp, g: (0, hp))
    out_spec = pl.BlockSpec((None, S, LANES), lambda hp, g: (g, 0, hp))
    acc_spec = pl.BlockSpec((8, LANES), lambda hp, g: (0, 0))
    return _call(
        body, sides, name="attn_bwd", grid=(4, 3),
        in_specs=_qk_specs() + _tab_specs() + [_vec_spec(), _vec_spec(), nat_spec, nat_spec, nat_spec],
        out_specs=[out_spec] * 3 + [acc_spec] * 2,
        out_shape=[jax.ShapeDtypeStruct((QKV // PLANE, S, PLANE), BF16)] * 3 + [jax.ShapeDtypeStruct((8, LANES), F32)] * 2,
        scratch_shapes=[pltpu.VMEM((S, LANES), BF16)] * 4 + [pltpu.VMEM((S, LANES), F32)] * 13,
        args=(proj, proj, proj, *tabs, qw2, kw2, d_attn, attn, lse))


PADR = 16
CT = 128


def _conv_specs():
    return [pl.BlockSpec((S, CC), lambda i: (0, OFF_CA // CC)), pl.BlockSpec((S, CC), lambda i: (0, OFF_CB // CC))]


NCB = CC // LANES


def _pad_zero(pad):
    for cb in range(NCB):
        pad[cb, 0:PADR, :] = jnp.zeros((PADR, LANES), F32)
        pad[cb, PADR + S:PADR + S + PADR, :] = jnp.zeros((PADR, LANES), F32)


def _pad_store(pad, row0, n, val):
    for cb in range(NCB):
        pad[cb, pl.ds(pl.multiple_of(row0 + PADR, 8), n), :] = val[:, cb * LANES:(cb + 1) * LANES]


def _taps(pad_ref, cb, s0, weights):
    acc = jnp.zeros((CT, LANES), F32)
    for k in range(KW):
        acc = acc + weights[k] * pad_ref[cb, pl.ds(s0 + k + 1, CT), :]
    return acc


def conv_fwd(proj, conv_w, conv_b, ln_w, ln_b):
    def body(a_ref, b_ref, w_ref, cb_ref, lw_ref, lb_ref, c_ref, u3_ref, upad):
        _pad_zero(upad)

        def glu(i, _):
            rows = pl.ds(pl.multiple_of(i * TM, TM), TM)
            _pad_store(upad, i * TM, TM, a_ref[rows, :] * _sigmoid(b_ref[rows, :]))
            return 0

        lax.fori_loop(0, S // TM, glu, 0)

        def chunk(i, _):
            s0 = pl.multiple_of(i * CT, CT)
            for cb in range(CC // LANES):
                cols = slice(cb * LANES, (cb + 1) * LANES)
                w = [w_ref[k:k + 1, cols] for k in range(KW)]
                c_ref[pl.ds(s0, CT), cols] = _taps(upad, cb, s0, w) + cb_ref[:, cols]
            cv = c_ref[pl.ds(s0, CT), :]
            mu = jnp.mean(cv, axis=-1, keepdims=True)
            xc = cv - mu
            rstd = lax.rsqrt(jnp.mean(xc * xc, axis=-1, keepdims=True) + EPS)
            yl = xc * rstd * lw_ref[...] + lb_ref[...]
            u3_ref[pl.ds(s0, CT), :] = (yl * _sigmoid(yl)).astype(BF16)
            return 0

        lax.fori_loop(0, S // CT, chunk, 0)

    vec = pl.BlockSpec((1, CC), lambda i: (0, 0))
    full = pl.BlockSpec((S, CC), lambda i: (0, 0))
    return pl.pallas_call(
        body, name="conv_fwd", grid=(1,),
        in_specs=_conv_specs() + [pl.BlockSpec((KW, CC), lambda i: (0, 0)), vec, vec, vec],
        out_specs=[full, full],
        out_shape=[jax.ShapeDtypeStruct((S, CC), F32), jax.ShapeDtypeStruct((S, CC), BF16)],
        scratch_shapes=[pltpu.VMEM((NCB, S + 2 * PADR, LANES), F32)],
        compiler_params=_cp(dimension_semantics=("arbitrary",)),
    )(proj, proj, conv_w, conv_b, ln_w, ln_b)


def conv_bwd(proj, cpre, d_u3, conv_w, conv_w_rev, ln_w, ln_b, sides=()):
    def body(a_ref, b_ref, c_ref, du3_ref, w_ref, wr_ref, lw_ref, lb_ref,
             dc_ref, gw_ref, gcb_ref, glw_ref, glb_ref, upad, dpad):
        _pad_zero(upad)
        _pad_zero(dpad)
        gw_ref[...] = jnp.zeros_like(gw_ref)

        def ln_bwd(i, carry):
            gcb, glw, glb = carry
            rows = pl.ds(pl.multiple_of(i * TM, TM), TM)
            _pad_store(upad, i * TM, TM, a_ref[rows, :] * _sigmoid(b_ref[rows, :]))
            cv = c_ref[rows, :]
            mu = jnp.mean(cv, axis=-1, keepdims=True)
            xc = cv - mu
            rstd = lax.rsqrt(jnp.mean(xc * xc, axis=-1, keepdims=True) + EPS)
            xh = xc * rstd
            yl = xh * lw_ref[...] + lb_ref[...]
            dyl = du3_ref[rows, :] * _dsilu(yl, _sigmoid(yl))
            dxh = dyl * lw_ref[...]
            dcv = rstd * (dxh - jnp.mean(dxh, axis=-1, keepdims=True)
                          - xh * jnp.mean(dxh * xh, axis=-1, keepdims=True))
            _pad_store(dpad, i * TM, TM, dcv)
            return (gcb + jnp.sum(dcv, axis=0, keepdims=True),
                    glw + jnp.sum(dyl * xh, axis=0, keepdims=True),
                    glb + jnp.sum(dyl, axis=0, keepdims=True))

        z = jnp.zeros((1, CC), F32)
        gcb, glw, glb = lax.fori_loop(0, S // TM, ln_bwd, (z, z, z))
        gcb_ref[...] = gcb
        glw_ref[...] = glw
        glb_ref[...] = glb

        def chunk(i, _):
            s0 = pl.multiple_of(i * CT, CT)
            for cb in range(CC // LANES):
                cols = slice(cb * LANES, (cb + 1) * LANES)
                wr = [wr_ref[k:k + 1, cols] for k in range(KW)]
                du = _taps(dpad, cb, s0, wr)
                dcv = dpad[cb, pl.ds(s0 + PADR, CT), :]
                for k in range(KW):
                    gw_ref[k:k + 1, cols] = gw_ref[k:k + 1, cols] + jnp.sum(
                        upad[cb, pl.ds(s0 + k + 1, CT), :] * dcv, axis=0, keepdims=True)
                av = a_ref[pl.ds(s0, CT), cols]
                sb = _sigmoid(b_ref[pl.ds(s0, CT), cols])
                dc_ref[0, pl.ds(s0, CT), cols] = (du * sb).astype(BF16)
                dc_ref[1, pl.ds(s0, CT), cols] = (du * av * sb * (1.0 - sb)).astype(BF16)
            return 0

        lax.fori_loop(0, S // CT, chunk, 0)

    vec = pl.BlockSpec((1, CC), lambda i: (0, 0))
    full = pl.BlockSpec((S, CC), lambda i: (0, 0))
    wsp = pl.BlockSpec((KW, CC), lambda i: (0, 0))
    return _call(
        body, sides, name="conv_bwd", grid=(1,),
        in_specs=_conv_specs() + [full, full, wsp, wsp, vec, vec],
        out_specs=[pl.BlockSpec((2, S, CC), lambda i: (0, 0, 0)), wsp, vec, vec, vec],
        out_shape=[jax.ShapeDtypeStruct((2, S, CC), BF16), jax.ShapeDtypeStruct((KW, CC), F32)]
        + [jax.ShapeDtypeStruct((1, CC), F32)] * 3,
        scratch_shapes=[pltpu.VMEM((NCB, S + 2 * PADR, LANES), F32)] * 2,
        args=(proj, proj, cpre, d_u3, conv_w, conv_w_rev, ln_w, ln_b))


def _gate_specs():
    return [_row(CC, col=OFF_GA // CC + j) for j in range(4)]


def _gates(g_refs, bg_ref):
    ga = _sigmoid(jnp.concatenate([g_refs[0][...], g_refs[1][...]], axis=1) + bg_ref[0:1, :])
    gb = _sigmoid(jnp.concatenate([g_refs[2][...], g_refs[3][...]], axis=1) + bg_ref[1:2, :])
    return ga, gb


def mix_out(x, proj, b_gate, attn, u3, w_o, w_pw, w_out):
    def body(x_ref, g0, g1, g2, g3, bg_ref, at_ref, u3_ref, wo_ref, wp_ref, wout_ref,
             x1_ref, z_ref, ya_ref, yb_ref):
        ga, gb = _gates((g0, g1, g2, g3), bg_ref)
        ya = _dot(at_ref[...], wo_ref[...])
        yb = _dot(u3_ref[...], wp_ref[...])
        z = (ga * ya + gb * yb).astype(BF16)
        ya_ref[...] = ya.astype(BF16)
        yb_ref[...] = yb.astype(BF16)
        z_ref[...] = z
        x1_ref[...] = x_ref[...] + _dot(z, wout_ref[...])

    return pl.pallas_call(
        body, name="mix_out", grid=(S // TM,),
        in_specs=[_row(D)] + _gate_specs() + [_res((2, D)), _row(CC), _row(CC),
                                              _res((CC, D)), _res((CC, D)), _res((D, D))],
        out_specs=[_row(D)] * 4,
        out_shape=[jax.ShapeDtypeStruct((S, D), F32)] + [jax.ShapeDtypeStruct((S, D), BF16)] * 3,
        compiler_params=_cp(dimension_semantics=("arbitrary",)),
    )(x, proj, proj, proj, proj, b_gate, attn, u3, w_o, w_pw, w_out)


def out_bwd(d_x1b, proj, b_gate, ya, yb, w_o, w_pw, w_out, sides=()):
    def body(dx_ref, g0, g1, g2, g3, bg_ref, ya_ref, yb_ref, wo_ref, wp_ref, wout_ref,
             dya_ref, dyb_ref, dgl_ref, dat_ref, du3_ref, gbg_ref):
        @pl.when(pl.program_id(0) == 0)
        def _():
            gbg_ref[...] = jnp.zeros_like(gbg_ref)

        ga, gb = _gates((g0, g1, g2, g3), bg_ref)
        dz = _dot_nt(dx_ref[...], wout_ref[...])
        dya = (dz * ga).astype(BF16)
        dyb = (dz * gb).astype(BF16)
        dgla = dz * ya_ref[...].astype(F32) * ga * (1.0 - ga)
        dglb = dz * yb_ref[...].astype(F32) * gb * (1.0 - gb)
        dya_ref[...] = dya
        dyb_ref[...] = dyb
        for j in range(2):
            dgl_ref[j] = dgla[:, j * PLANE:(j + 1) * PLANE].astype(BF16)
            dgl_ref[2 + j] = dglb[:, j * PLANE:(j + 1) * PLANE].astype(BF16)
        gbg_ref[0:1, :] = gbg_ref[0:1, :] + jnp.sum(dgla, axis=0, keepdims=True)
        gbg_ref[1:2, :] = gbg_ref[1:2, :] + jnp.sum(dglb, axis=0, keepdims=True)
        dat_ref[...] = _dot_nt(dya, wo_ref[...])
        du3_ref[...] = _dot_nt(dyb, wp_ref[...])

    return _call(
        body, sides, name="out_bwd", grid=(S // TM,),
        in_specs=[_row(D)] + _gate_specs() + [_res((2, D)), _row(D), _row(D),
                                              _res((CC, D)), _res((CC, D)), _res((D, D))],
        out_specs=[_row(D), _row(D), _planes(2 * D), _row(CC), _row(CC), pl.BlockSpec((2, D), lambda i: (0, 0))],
        out_shape=[jax.ShapeDtypeStruct((S, D), BF16)] * 2 + [jax.ShapeDtypeStruct((2 * D // PLANE, S, PLANE), BF16)]
        + [jax.ShapeDtypeStruct((S, CC), F32)] * 2 + [jax.ShapeDtypeStruct((2, D), F32)],
        args=(d_x1b, proj, proj, proj, proj, b_gate, ya, yb, w_o, w_pw, w_out))


def ffn_in(x1, norm_w, w_ffn_in, sides=()):
    half = FF // 2

    def body(x_ref, nw_ref, w_ref, h_ref, gu_ref, f_ref):
        xv = x_ref[...]
        r = lax.rsqrt(jnp.mean(xv * xv, axis=-1, keepdims=True) + EPS)
        h = (xv * r * nw_ref[...]).astype(BF16)
        h_ref[...] = h
        for j in range(2):
            gt = _dot_nt(h, w_ref[j * half:(j + 1) * half, :])
            up = _dot_nt(h, w_ref[FF + j * half:FF + (j + 1) * half, :])
            gu_ref[:, j * half:(j + 1) * half] = gt.astype(BF16)
            gu_ref[:, FF + j * half:FF + (j + 1) * half] = up.astype(BF16)
            f_ref[:, j * half:(j + 1) * half] = (gt * _sigmoid(gt) * up).astype(BF16)

    return _call(
        body, sides, name="ffn_in", grid=(S // TM,),
        in_specs=[_row(D), _res((1, D)), _res((2 * FF, D))],
        out_specs=[_row(D), _row(2 * FF), _row(FF)],
        out_shape=[jax.ShapeDtypeStruct((S, D), BF16), jax.ShapeDtypeStruct((S, 2 * FF), BF16),
                   jax.ShapeDtypeStruct((S, FF), BF16)],
        args=(x1, norm_w, w_ffn_in))


def ffn_out_loss(x1, f, w_ffn_out, target):
    def body(x_ref, f_ref, w_ref, t_ref, dy_ref, dyb_ref, sq_ref):
        @pl.when(pl.program_id(0) == 0)
        def _():
            sq_ref[...] = jnp.zeros_like(sq_ref)

        diff = x_ref[...] + _dot(f_ref[...], w_ref[...]) - t_ref[...]
        dy = diff * (1.0 / D)
        dy_ref[...] = dy
        dyb_ref[...] = dy.astype(BF16)
        sq_ref[...] = sq_ref[...] + jnp.sum((diff * diff).reshape(TM // 8, 8, D), axis=0)

    return pl.pallas_call(
        body, name="ffn_out_loss", grid=(S // TM,),
        in_specs=[_row(D), _row(FF), _res((FF, D)), _row(D)],
        out_specs=[_row(D), _row(D), pl.BlockSpec((8, D), lambda i: (0, 0))],
        out_shape=[jax.ShapeDtypeStruct((S, D), F32), jax.ShapeDtypeStruct((S, D), BF16),
                   jax.ShapeDtypeStruct((8, D), F32)],
        compiler_params=_cp(dimension_semantics=("arbitrary",)),
    )(x1, f, w_ffn_out, target)


def _rms_bwd(xv, nw, dh):
    r = lax.rsqrt(jnp.mean(xv * xv, axis=-1, keepdims=True) + EPS)
    xn = xv * r
    dxn = dh * nw
    dx = r * (dxn - xn * jnp.mean(dxn * xn, axis=-1, keepdims=True))
    return dx, dh * xn


def ffn_bwd(dy, dyb, gu, x1, norm_w, w_ffn_in, w_ffn_out, sides=()):
    def body(dy_ref, dyb_ref, gu_ref, x_ref, nw_ref, wi_ref, wo_ref, dgu_ref, dx_ref, dxb_ref, gn_ref):
        @pl.when(pl.program_id(0) == 0)
        def _():
            gn_ref[...] = jnp.zeros_like(gn_ref)

        df = _dot_nt(dyb_ref[...], wo_ref[...])
        gt = gu_ref[:, 0:FF].astype(F32)
        up = gu_ref[:, FF:2 * FF].astype(F32)
        sg = _sigmoid(gt)
        dgt = (df * up * _dsilu(gt, sg)).astype(BF16)
        dup = (df * gt * sg).astype(BF16)
        dgu_ref[:, 0:FF] = dgt
        dgu_ref[:, FF:2 * FF] = dup
        dh = _dot(dgt, wi_ref[0:FF, :]) + _dot(dup, wi_ref[FF:2 * FF, :])
        dxn, gw = _rms_bwd(x_ref[...], nw_ref[...], dh)
        dx = dy_ref[...] + dxn
        dx_ref[...] = dx
        dxb_ref[...] = dx.astype(BF16)
        gn_ref[...] = gn_ref[...] + jnp.sum(gw, axis=0, keepdims=True)

    return _call(
        body, sides, name="ffn_bwd", grid=(S // TM,),
        in_specs=[_row(D), _row(D), _row(2 * FF), _row(D), _res((1, D)), _res((2 * FF, D)), _res((FF, D))],
        out_specs=[_row(2 * FF), _row(D), _row(D), pl.BlockSpec((1, D), lambda i: (0, 0))],
        out_shape=[jax.ShapeDtypeStruct((S, 2 * FF), BF16), jax.ShapeDtypeStruct((S, D), F32),
                   jax.ShapeDtypeStruct((S, D), BF16), jax.ShapeDtypeStruct((1, D), F32)],
        args=(dy, dyb, gu, x1, norm_w, w_ffn_in, w_ffn_out))


def in_bwd(d_q, d_k, d_v, d_conv, d_gl, w_in, x, d_x1, norm_w, sides=()):
    segs = ((OFF_Q, QKV), (OFF_K, QKV), (OFF_V, QKV), (OFF_CA, 2 * CC), (OFF_GA, 2 * D))

    def body(dq_ref, dk_ref, dv_ref, dc_ref, dg_ref, w_ref, x_ref, dx1_ref, nw_ref, gx_ref, gn_ref):
        @pl.when(pl.program_id(0) == 0)
        def _():
            gn_ref[...] = jnp.zeros_like(gn_ref)

        dh = jnp.zeros((TM, D), F32)
        for ref, (off, width) in zip((dq_ref, dk_ref, dv_ref, dc_ref, dg_ref), segs):
            for j in range(width // PLANE):
                dh = dh + _dot(ref[j], w_ref[off + j * PLANE:off + (j + 1) * PLANE, :])
        dxn, gw = _rms_bwd(x_ref[...], nw_ref[...], dh)
        gx_ref[...] = dx1_ref[...] + dxn
        gn_ref[...] = gn_ref[...] + jnp.sum(gw, axis=0, keepdims=True)

    return _call(
        body, sides, name="in_bwd", grid=(S // TM,),
        in_specs=[_planes(QKV)] * 3 + [_planes(2 * CC), _planes(2 * D), _res((INW, D)), _row(D), _row(D), _res((1, D))],
        out_specs=[_row(D), pl.BlockSpec((1, D), lambda i: (0, 0))],
        out_shape=[jax.ShapeDtypeStruct((S, D), F32), jax.ShapeDtypeStruct((1, D), F32)],
        args=(d_q, d_k, d_v, d_conv, d_gl, w_in, x, d_x1, norm_w))


def mm_tn(name, a, b, tm, tn, sides=()):
    M, N = a.shape[1], b.shape[1]

    def body(a_ref, b_ref, o_ref):
        o_ref[...] = _dot_tn(a_ref[...], b_ref[...])

    res = _call(
        body, sides, name=name, grid=(M // tm, N // tn),
        in_specs=[pl.BlockSpec((S, tm), lambda i, j: (0, i)), pl.BlockSpec((S, tn), lambda i, j: (0, j))],
        out_specs=[pl.BlockSpec((tm, tn), lambda i, j: (i, j))],
        out_shape=[jax.ShapeDtypeStruct((M, N), F32)],
        args=(a, b))
    return (res[0][0], res[1]) if sides else res[0]


GW_IN_TN = PLANE
GW_IN_PARTS = 4


def gw_in_t(name, ht, d_segs, col_half, sides=()):
    tn, hw = GW_IN_TN, D // GW_IN_PARTS
    starts, t0 = [], 0
    for seg in d_segs:
        starts.append(t0)
        t0 += seg.shape[0]
    ntiles = [seg.shape[0] for seg in d_segs]

    def body(h_ref, *refs):
        a_refs, o_ref = refs[:-1], refs[-1]
        n = pl.program_id(0)
        for a_ref, st, nt in zip(a_refs, starts, ntiles):
            @pl.when((n >= st) & (n < st + nt))
            def _(a_ref=a_ref):
                o_ref[...] = _dot(h_ref[...], a_ref[...]).T

    def seg_spec(st, nt):
        return pl.BlockSpec((None, S, tn), lambda n: (jnp.clip(n - st, 0, nt - 1), 0, 0))

    res = _call(
        body, sides, name=name, grid=(INW // tn,),
        in_specs=[pl.BlockSpec((hw, S), lambda n: (col_half, 0))] + [seg_spec(st, nt) for st, nt in zip(starts, ntiles)],
        out_specs=[pl.BlockSpec((tn, hw), lambda n: (n, 0))],
        out_shape=[jax.ShapeDtypeStruct((INW, hw), F32)],
        args=(ht, *d_segs))
    return (res[0][0], res[1]) if sides else res[0]


def _place():
    x, y, c = lax.axis_index("x"), lax.axis_index("y"), lax.axis_index("c")
    chips = [(1 - x, y), (x, 1 - y), (1 - x, 1 - y)]
    return x, y, c, chips


def _sems(n):
    return pltpu.SemaphoreType.DMA((n,))


def _remote(src, dst, send, recv, k, to):
    return pltpu.make_async_remote_copy(src_ref=src, dst_ref=dst, send_sem=send.at[k], recv_sem=recv.at[k],
                                        device_id=to, device_id_type=MESH)


def _cast_rows(dst, src, cols=slice(None)):
    rows = src.shape[0]
    step = next((s for s in (128, 64, 32, 16) if rows % s == 0), rows)
    for r0 in range(0, rows, step):
        dst[r0:r0 + step, cols] = src[r0:r0 + step, :].astype(dst.dtype)


def comm_only(name, sides):
    def body():
        pass

    return _call(body, sides, name=name, grid=(1,), in_specs=[], out_specs=[], out_shape=[], args=())[1]


def ag_blocks(shard, dtype):
    R, W = shard.shape

    def copy(outs, scr, k, block, to, src=None):
        dst = outs[0].at[block]
        return _remote(dst if src is None else src, dst, scr[1], scr[2], k, to)

    def local(outs, scr, me):
        return pltpu.make_async_copy(scr[0], outs[0].at[me], scr[3].at[0])

    def start(ins, outs, scr):
        x, y, c, chips = _place()
        me = 4 * x + 2 * y + c
        _cast_rows(scr[0], ins[0])
        local(outs, scr, me).start()
        copy(outs, scr, 0, me, (x, y, 1 - c), src=scr[0]).start()
        for j, (cx, cy) in enumerate(chips):
            copy(outs, scr, 1 + j, me, (cx, cy, c), src=scr[0]).start()

    def finish(ins, outs, scr):
        x, y, c, chips = _place()
        me, sib = 4 * x + 2 * y + c, (x, y, 1 - c)
        passed = []
        for j, (cx, cy) in enumerate(chips):
            theirs = 4 * cx + 2 * cy + c
            copy(outs, scr, 1 + j, theirs, (x, y, c)).wait_recv()
            fwd = copy(outs, scr, 4 + j, theirs, sib)
            fwd.start()
            passed.append(fwd)
        copy(outs, scr, 0, 4 * x + 2 * y + 1 - c, (x, y, c)).wait_recv()
        for j, (cx, cy) in enumerate(chips):
            copy(outs, scr, 4 + j, 4 * cx + 2 * cy + 1 - c, (x, y, c)).wait_recv()
        copy(outs, scr, 0, me, sib, src=scr[0]).wait_send()
        for j, (cx, cy) in enumerate(chips):
            copy(outs, scr, 1 + j, me, (cx, cy, c), src=scr[0]).wait_send()
        for fwd in passed:
            fwd.wait_send()
        local(outs, scr, me).wait()

    return Side((shard,), (VMEM,), (jax.ShapeDtypeStruct((NDEV, R, W), dtype),),
                (pltpu.VMEM((R, W), dtype), _sems(7), _sems(7), _sems(1)), start, finish)


def ag_blocks_relay(shard, dtype):
    R, W = shard.shape
    half = R // 2

    def copy(outs, scr, k, block, to, src=None, rows=None):
        dst = outs[0].at[block] if rows is None else outs[0].at[block, pl.ds(rows * half, half), :]
        return _remote(dst if src is None else src, dst, scr[1], scr[2], k, to)

    def local(outs, scr, me):
        return pltpu.make_async_copy(scr[0], outs[0].at[me], scr[3].at[0])

    def own(outs, scr):
        x, y, c, _ = _place()
        me = 4 * x + 2 * y + c
        return [copy(outs, scr, k, me, to, src=scr[0])
                for k, to in enumerate([(x, y, 1 - c), (1 - x, y, c), (x, 1 - y, c)])]

    def start(ins, outs, scr):
        x, y, c, _ = _place()
        _cast_rows(scr[0], ins[0])
        local(outs, scr, 4 * x + 2 * y + c).start()
        for cp in own(outs, scr):
            cp.start()

    def passed_on(outs, scr):
        x, y, c, _ = _place()
        sib, xn, yn = (x, y, 1 - c), (1 - x, y, c), (x, 1 - y, c)
        b_xn, b_yn, b_dg = 4 * (1 - x) + 2 * y + c, 4 * x + 2 * (1 - y) + c, 4 * (1 - x) + 2 * (1 - y) + c
        near = [copy(outs, scr, 5, b_xn, yn, rows=0), copy(outs, scr, 3, b_xn, sib),
                copy(outs, scr, 6, b_yn, xn, rows=1), copy(outs, scr, 4, b_yn, sib)]
        far = [copy(outs, scr, 7, b_dg, sib, rows=0), copy(outs, scr, 8, b_dg, sib, rows=1)]
        return (b_xn, b_yn, b_dg), near, far

    def mid(ins, outs, scr):
        x, y, c, _ = _place()
        (b_xn, b_yn, _), near, _ = passed_on(outs, scr)
        copy(outs, scr, 1, b_xn, (x, y, c)).wait_recv()
        near[0].start()
        near[1].start()
        copy(outs, scr, 2, b_yn, (x, y, c)).wait_recv()
        near[2].start()
        near[3].start()

    def finish(ins, outs, scr):
        x, y, c, _ = _place()
        here = (x, y, c)
        (b_xn, b_yn, b_dg), near, far = passed_on(outs, scr)
        copy(outs, scr, 5, b_dg, here, rows=0).wait_recv()
        far[0].start()
        copy(outs, scr, 6, b_dg, here, rows=1).wait_recv()
        far[1].start()
        flip = 1 - 2 * c
        copy(outs, scr, 0, 4 * x + 2 * y + 1 - c, here).wait_recv()
        copy(outs, scr, 3, b_xn + flip, here).wait_recv()
        copy(outs, scr, 4, b_yn + flip, here).wait_recv()
        copy(outs, scr, 7, b_dg + flip, here, rows=0).wait_recv()
        copy(outs, scr, 8, b_dg + flip, here, rows=1).wait_recv()
        for cp in own(outs, scr) + near + far:
            cp.wait_send()
        local(outs, scr, 4 * x + 2 * y + c).wait()

    return Side((shard,), (VMEM,), (jax.ShapeDtypeStruct((NDEV, R, W), dtype),),
                (pltpu.VMEM((R, W), dtype), _sems(9), _sems(9), _sems(1)), start, finish, mid)


def ag_cols(shard):
    K, C = shard.shape
    half, w2 = K // 2, 2 * C

    def win(out, rows_c, chip):
        return out.at[pl.ds(pl.multiple_of(rows_c * half, 16), half), pl.ds(pl.multiple_of(chip * w2, LANES), w2)]

    def ici(outs, scr, j, to, c, k):
        slab, send, recv = scr[2], scr[5], scr[6]
        return _remote(slab.at[pl.ds(pl.multiple_of(c * half, 16), half), :], win(outs[0], c, k), send, recv, j, to)

    def local(outs, scr, k):
        return pltpu.make_async_copy(scr[2], outs[0].at[:, pl.ds(pl.multiple_of(k * w2, LANES), w2)], scr[7].at[0])

    def start(ins, outs, scr):
        stage, inbox, slab, xs, xr = scr[:5]
        x, y, c, chips = _place()
        k = 2 * x + y
        _cast_rows(stage, ins[0])
        swap = _remote(stage, inbox, xs, xr, 0, (x, y, 1 - c))
        swap.start()
        for cc in range(2):
            @pl.when(c == cc)
            def _(cc=cc):
                _cast_rows(slab, stage, slice(cc * C, (cc + 1) * C))
        swap.wait()
        for cc in range(2):
            @pl.when(c == cc)
            def _(cc=cc):
                _cast_rows(slab, inbox, slice((1 - cc) * C, (2 - cc) * C))
        local(outs, scr, k).start()
        for j, (cx, cy) in enumerate(chips):
            ici(outs, scr, j, (cx, cy, c), c, k).start()

    def finish(ins, outs, scr):
        send, recv = scr[5], scr[6]
        x, y, c, chips = _place()
        k, sib = 2 * x + y, (x, y, 1 - c)
        passed = []
        for j, (cx, cy) in enumerate(chips):
            w = win(outs[0], c, 2 * cx + cy)
            _remote(w, w, send, recv, j, sib).wait_recv()
            fwd = _remote(w, w, send, recv, 3 + j, sib)
            fwd.start()
            passed.append(fwd)
        for j, (cx, cy) in enumerate(chips):
            w = win(outs[0], 1 - c, 2 * cx + cy)
            _remote(w, w, send, recv, 3 + j, sib).wait_recv()
        for j, (cx, cy) in enumerate(chips):
            ici(outs, scr, j, (cx, cy, c), c, k).wait_send()
        for fwd in passed:
            fwd.wait_send()
        local(outs, scr, k).wait()

    return Side((shard,), (VMEM,), (jax.ShapeDtypeStruct((K, NDEV * C), BF16),),
                (pltpu.VMEM((K, C), BF16), pltpu.VMEM((K, C), BF16), pltpu.VMEM((K, w2), BF16),
                 _sems(1), _sems(1), _sems(6), _sems(6), _sems(1)), start, finish)


def copies_side(args, out_shape, n_copies, plan):
    def copies(ins, outs, scr):
        return [_remote(s_, d_, scr[0], scr[1], i, to) for i, (s_, d_, to) in enumerate(plan(ins, outs))]

    def start(ins, outs, scr):
        for cp in copies(ins, outs, scr):
            cp.start()

    def finish(ins, outs, scr):
        for cp in copies(ins, outs, scr):
            cp.wait()

    return Side(tuple(args), (ANY,) * len(args), tuple(out_shape), (_sems(n_copies), _sems(n_copies)), start, finish)


def rs_to_sibling(grads):
    out_shape = [jax.ShapeDtypeStruct((4,) + g.shape[1:] if kind == "rows" else (g.shape[0] // 2, g.shape[1]), F32)
                 for kind, g in grads]

    def plan(ins, outs):
        x, y, c, _ = _place()
        sib, res = (x, y, 1 - c), []
        for (kind, _), g, r in zip(grads, ins, outs):
            if kind == "rows":
                res += [(g.at[2 * k + 1 - c], r.at[k], sib) for k in range(4)]
            else:
                half = g.shape[0] // 2
                res.append((g.at[pl.ds(pl.multiple_of((1 - c) * half, 8), half), :], r, sib))
        return res

    return copies_side([g for _, g in grads], out_shape, sum(4 if kind == "rows" else 1 for kind, _ in grads), plan)


def rs_to_chips(parts):
    out_shape = [jax.ShapeDtypeStruct((3,) + p.shape[1:] if kind == "rows" else (3, p.shape[0], p.shape[1] // 4), BF16)
                 for kind, p in parts]

    def plan(ins, outs):
        x, y, c, chips = _place()
        res = []
        for (kind, _), p, r in zip(parts, ins, outs):
            for j, (cx, cy) in enumerate(chips):
                if kind == "rows":
                    src = p.at[2 * cx + cy]
                else:
                    w2 = p.shape[1] // 4
                    src = p.at[:, pl.ds(pl.multiple_of((2 * cx + cy) * w2, LANES), w2)]
                res.append((src, r.at[j], (cx, cy, c)))
        return res

    return copies_side([p for _, p in parts], out_shape, 3 * len(parts), plan)


def rs_swap_halves(theirs):
    def plan(ins, outs):
        x, y, c, _ = _place()
        return [(t, r, (x, y, 1 - c)) for t, r in zip(ins, outs)]

    return copies_side(theirs, [jax.ShapeDtypeStruct(t.shape, F32) for t in theirs], len(theirs), plan)


def _row_tiles(rows):
    return 2 if rows % 32 == 0 and rows >= 512 else 1


def chip_sum(name, grad, recv, c_idx, chip_idx):
    _, R, C = grad.shape
    nt = 1
    tr = R // nt

    def body(s_ref, g_ref, r_ref, p_ref, own_ref):
        k = pl.program_id(1)
        tot = g_ref[0] + r_ref[0]
        p_ref[0] = tot.astype(BF16)

        @pl.when(k == s_ref[1])
        def _():
            own_ref[...] = tot

    grid_spec = pltpu.PrefetchScalarGridSpec(
        num_scalar_prefetch=1, grid=(nt, 4),
        in_specs=[pl.BlockSpec((1, tr, C), lambda i, k, s: (2 * k + s[0], i, 0)),
                  pl.BlockSpec((1, tr, C), lambda i, k, s: (k, i, 0))],
        out_specs=[pl.BlockSpec((1, tr, C), lambda i, k, s: (k, i, 0)),
                   pl.BlockSpec((tr, C), lambda i, k, s: (i, 0))])
    return pl.pallas_call(
        body, name=name, grid_spec=grid_spec,
        out_shape=[jax.ShapeDtypeStruct((4, R, C), BF16), jax.ShapeDtypeStruct((R, C), F32)],
        compiler_params=_cp(dimension_semantics=("arbitrary", "arbitrary")),
    )(jnp.stack([c_idx, chip_idx]), grad, recv)


def _half_tiles(half):
    return 2 if half >= 512 else 1


def chip_sum_cols(name, grad, recv, c_idx, chip_idx):
    K, W = grad.shape
    half, w2 = K // 2, W // 4
    nt = _half_tiles(half)
    tr = half // nt

    def body(s_ref, g_ref, r_ref, p_ref, own_ref):
        tot = g_ref[...] + r_ref[...]
        p_ref[...] = tot.astype(BF16)

        @pl.when(pl.program_id(1) == s_ref[1])
        def _():
            own_ref[...] = tot

    grid_spec = pltpu.PrefetchScalarGridSpec(
        num_scalar_prefetch=1, grid=(nt, 4),
        in_specs=[pl.BlockSpec((tr, w2), lambda i, k, s: (s[0] * nt + i, k)),
                  pl.BlockSpec((tr, w2), lambda i, k, s: (i, k))],
        out_specs=[pl.BlockSpec((tr, w2), lambda i, k, s: (i, k)),
                   pl.BlockSpec((tr, w2), lambda i, k, s: (i, 0))])
    return pl.pallas_call(
        body, name=name, grid_spec=grid_spec,
        out_shape=[jax.ShapeDtypeStruct((half, W), BF16), jax.ShapeDtypeStruct((half, w2), F32)],
        compiler_params=_cp(dimension_semantics=("arbitrary", "arbitrary")),
    )(jnp.stack([c_idx, chip_idx]), grad, recv)


def col_final(name, own, recv, c_idx):
    half, w2 = own.shape
    C = w2 // 2
    nt = _half_tiles(half)
    tr = half // nt

    def body(s_ref, o_ref, r_ref, mine_ref, theirs_ref, t_ref):
        t_ref[...] = o_ref[...] + r_ref[0].astype(F32) + r_ref[1].astype(F32) + r_ref[2].astype(F32)
        for cc in range(2):
            @pl.when(s_ref[0] == cc)
            def _(cc=cc):
                mine_ref[...] = t_ref[:, cc * C:(cc + 1) * C]
                theirs_ref[...] = t_ref[:, (1 - cc) * C:(2 - cc) * C]

    grid_spec = pltpu.PrefetchScalarGridSpec(
        num_scalar_prefetch=1, grid=(nt,),
        in_specs=[pl.BlockSpec((tr, w2), lambda i, s: (i, 0)), pl.BlockSpec((3, tr, w2), lambda i, s: (0, i, 0))],
        out_specs=[pl.BlockSpec((tr, C), lambda i, s: (i, 0))] * 2,
        scratch_shapes=[pltpu.VMEM((tr, w2), F32)])
    return pl.pallas_call(
        body, name=name, grid_spec=grid_spec, out_shape=[jax.ShapeDtypeStruct((half, C), F32)] * 2,
        compiler_params=_cp(dimension_semantics=("arbitrary",)),
    )(jnp.stack([c_idx]), own, recv)


def _adamw(w, g, m, v):
    m2 = ADAM_B1 * m + (1.0 - ADAM_B1) * g
    v2 = ADAM_B2 * v + (1.0 - ADAM_B2) * (g * g)
    m_hat = m2 / (1.0 - ADAM_B1 ** ADAM_STEP)
    v_hat = v2 / (1.0 - ADAM_B2 ** ADAM_STEP)
    delta = -ADAM_LR * (m_hat / (jnp.sqrt(v_hat) + ADAM_EPS) + ADAM_WD * w)
    return delta, m2, v2


def shard_adam(name, owns, recvs, w, m, v):
    n = len(owns)
    R, Cp = owns[0].shape
    nt = _row_tiles(R)
    tr = R // nt

    def body(*refs):
        o_refs, r_refs = refs[:n], refs[n:2 * n]
        w_ref, m_ref, v_ref, g_ref, d_ref, nm_ref, nv_ref = refs[2 * n:]
        g = None
        for k in range(n):
            gk = o_refs[k][...] + r_refs[k][0].astype(F32) + r_refs[k][1].astype(F32) + r_refs[k][2].astype(F32)
            g = gk if g is None else jnp.where(pl.program_id(0) == k, gk, g)
        delta, m2, v2 = _adamw(w_ref[...], g, m_ref[...], v_ref[...])
        g_ref[...] = g
        d_ref[...] = delta
        nm_ref[...] = m2
        nv_ref[...] = v2

    part = pl.BlockSpec((tr, Cp), lambda k, i: (i, 0))
    part3 = pl.BlockSpec((3, tr, Cp), lambda k, i: (0, i, 0))
    tile = pl.BlockSpec((tr, Cp), lambda k, i: (i, k))
    return pl.pallas_call(
        body, name=name, grid=(n, nt),
        in_specs=[part] * n + [part3] * n + [tile, tile, tile],
        out_specs=[tile] * 4, out_shape=[jax.ShapeDtypeStruct((R, n * Cp), F32)] * 4,
        compiler_params=_cp(dimension_semantics=("arbitrary", "arbitrary")),
    )(*owns, *recvs, w, m, v)


def adam_cols(name, mine, recv, w, m, v, c_idx):
    half, C = mine.shape
    nt = _half_tiles(half)
    tr = half // nt

    def body(s_ref, a_ref, b_ref, w_ref, m_ref, v_ref, g_ref, d_ref, nm_ref, nv_ref):
        g = jnp.where(pl.program_id(0) == s_ref[0], a_ref[...], b_ref[...])
        delta, m2, v2 = _adamw(w_ref[...], g, m_ref[...], v_ref[...])
        g_ref[...] = g
        d_ref[...] = delta
        nm_ref[...] = m2
        nv_ref[...] = v2

    part = pl.BlockSpec((tr, C), lambda hh, i, s: (i, 0))
    tile = pl.BlockSpec((tr, C), lambda hh, i, s: (hh * nt + i, 0))
    grid_spec = pltpu.PrefetchScalarGridSpec(
        num_scalar_prefetch=1, grid=(2, nt), in_specs=[part, part, tile, tile, tile], out_specs=[tile] * 4)
    return pl.pallas_call(
        body, name=name, grid_spec=grid_spec, out_shape=[jax.ShapeDtypeStruct((2 * half, C), F32)] * 4,
        compiler_params=_cp(dimension_semantics=("arbitrary", "arbitrary")),
    )(jnp.stack([c_idx]), mine, recv, w, m, v)


ROW_N1, ROW_N2, ROW_BG, ROW_QN, ROW_KN, ROW_CB, ROW_LW, ROW_LB, ROW_CW = 0, 1, 2, 4, 5, 6, 7, 8, 9
PACK_ROWS = 40
SMALL = ("norm1_w", "norm2_w", "b_gate", "q_norm_w", "k_norm_w", "conv_b", "conv_ln_w", "conv_ln_b", "conv_w")


def small_sync_adam(g, w, m, v, sq, sides=()):
    ns = len(SMALL)

    def body(*refs):
        gi = dict(zip(SMALL, refs[:ns]))
        wi = dict(zip(SMALL, refs[ns:2 * ns]))
        mi = dict(zip(SMALL, refs[2 * ns:3 * ns]))
        vi = dict(zip(SMALL, refs[3 * ns:4 * ns]))
        sq_ref = refs[4 * ns]
        outs = refs[4 * ns + 1:8 * ns + 1]
        loss_ref = refs[8 * ns + 1]
        pack, recv, tot, send_sems, recv_sems = refs[8 * ns + 2:]
        x, y, c, _ = _place()
        me = 4 * x + 2 * y + c

        pack[...] = jnp.zeros_like(pack)
        pack[ROW_KN:ROW_KN + 1, LANES:2 * LANES] = jnp.full((1, LANES), (0.5 / D) * jnp.sum(sq_ref[...]), F32)
        pack[ROW_N1:ROW_N1 + 1, :] = gi["norm1_w"][...]
        pack[ROW_N2:ROW_N2 + 1, :] = gi["norm2_w"][...]
        pack[ROW_BG:ROW_BG + 2, :] = gi["b_gate"][...]
        pack[ROW_QN:ROW_QN + 1, 0:HD] = gi["q_norm_w"][...]
        pack[ROW_KN:ROW_KN + 1, 0:HD] = gi["k_norm_w"][...]
        pack[ROW_CB:ROW_CB + 1, 0:CC] = gi["conv_b"][...]
        pack[ROW_LW:ROW_LW + 1, 0:CC] = gi["conv_ln_w"][...]
        pack[ROW_LB:ROW_LB + 1, 0:CC] = gi["conv_ln_b"][...]
        pack[ROW_CW:ROW_CW + KW, 0:CC] = gi["conv_w"][...]

        copies = []
        for k in range(1, NDEV):
            peer = (x ^ (k >> 2), y ^ ((k >> 1) & 1), c ^ (k & 1))
            cp = pltpu.make_async_remote_copy(
                src_ref=pack, dst_ref=recv.at[me], send_sem=send_sems.at[k - 1], recv_sem=recv_sems.at[k - 1],
                device_id=peer, device_id_type=MESH)
            cp.start()
            copies.append(cp)
        recv[me] = pack[...]
        for cp in copies:
            cp.wait()
        acc = recv[0]
        for p in range(1, NDEV):
            acc = acc + recv[p]
        tot[...] = acc

        def shard_grad(name):
            if name == "b_gate":
                return tot[ROW_BG:ROW_BG + 2, pl.ds(pl.multiple_of(me * LANES, LANES), LANES)]
            if name == "conv_w":
                win = tot[ROW_CW:ROW_CW + KW, pl.ds(pl.multiple_of((me // 2) * LANES, LANES), LANES)]
                return jnp.where(me % 2 == 1, win[:, HD:LANES], win[:, 0:HD])
            row = {"norm1_w": ROW_N1, "norm2_w": ROW_N2, "q_norm_w": ROW_QN, "k_norm_w": ROW_KN,
                   "conv_b": ROW_CB, "conv_ln_w": ROW_LW, "conv_ln_b": ROW_LB}[name]
            return tot[row:row + 1, 0:wi[name].shape[1]]

        for i, name in enumerate(SMALL):
            gr = shard_grad(name)
            delta, m2, v2 = _adamw(wi[name][...], gr, mi[name][...], vi[name][...])
            outs[4 * i][...] = gr
            outs[4 * i + 1][...] = delta
            outs[4 * i + 2][...] = m2
            outs[4 * i + 3][...] = v2
        loss_ref[...] = tot[ROW_KN:ROW_KN + 1, LANES:2 * LANES]

    out_shape = []
    for name in SMALL:
        out_shape += [jax.ShapeDtypeStruct(w[name].shape, F32)] * 4
    out_shape.append(jax.ShapeDtypeStruct((1, LANES), F32))
    args = [g[k] for k in SMALL] + [w[k] for k in SMALL] + [m[k] for k in SMALL] + [v[k] for k in SMALL] + [sq]
    res = _call(
        body, sides, name="small_sync_adam", grid=(1,), in_specs=[VMEM] * len(args),
        out_specs=[VMEM] * len(out_shape), out_shape=out_shape,
        scratch_shapes=[pltpu.VMEM((PACK_ROWS, D), F32), pltpu.VMEM((NDEV, PACK_ROWS, D), F32),
                        pltpu.VMEM((PACK_ROWS, D), F32), _sems(NDEV - 1), _sems(NDEV - 1)],
        args=args)
    res, side_outs = res if sides else (res, None)
    out = {name: tuple(res[4 * i:4 * i + 4]) for i, name in enumerate(SMALL)}
    loss = res[4 * ns][0, 0]
    return (out, loss, side_outs) if sides else (out, loss)


MATS = ("w_in", "w_o_attn", "w_pw_conv", "w_out", "w_ffn_in", "w_ffn_out")
TRANSPOSED = ("w_in", "w_ffn_in")
WEIGHTS = ("norm1_w", "w_in", "b_gate", "q_norm_w", "k_norm_w", "w_o_attn", "conv_w", "conv_b", "conv_ln_w",
           "conv_ln_b", "w_pw_conv", "w_out", "norm2_w", "w_ffn_in", "w_ffn_out")


def _blocks_to_cols(blocks):
    n, R, C = blocks.shape
    return blocks.transpose(1, 0, 2).reshape(R, n * C)


def kernel(x, positions, norm1_w, w_in, b_gate, q_norm_w, k_norm_w, w_o_attn, conv_w, conv_b, conv_ln_w, conv_ln_b, w_pw_conv, w_out, norm2_w, w_ffn_in, w_ffn_out, loss_target, m_norm1_w, m_w_in, m_b_gate, m_q_norm_w, m_k_norm_w, m_w_o_attn, m_conv_w, m_conv_b, m_conv_ln_w, m_conv_ln_b, m_w_pw_conv, m_w_out, m_norm2_w, m_w_ffn_in, m_w_ffn_out, v_norm1_w, v_w_in, v_b_gate, v_q_norm_w, v_k_norm_w, v_w_o_attn, v_conv_w, v_conv_b, v_conv_ln_w, v_conv_ln_b, v_w_pw_conv, v_w_out, v_norm2_w, v_w_ffn_in, v_w_ffn_out):
    w = dict(norm1_w=norm1_w, w_in=w_in, b_gate=b_gate, q_norm_w=q_norm_w, k_norm_w=k_norm_w, w_o_attn=w_o_attn,
             conv_w=conv_w, conv_b=conv_b, conv_ln_w=conv_ln_w, conv_ln_b=conv_ln_b, w_pw_conv=w_pw_conv,
             w_out=w_out, norm2_w=norm2_w, w_ffn_in=w_ffn_in, w_ffn_out=w_ffn_out)
    m = dict(norm1_w=m_norm1_w, w_in=m_w_in, b_gate=m_b_gate, q_norm_w=m_q_norm_w, k_norm_w=m_k_norm_w,
             w_o_attn=m_w_o_attn, conv_w=m_conv_w, conv_b=m_conv_b, conv_ln_w=m_conv_ln_w,
             conv_ln_b=m_conv_ln_b, w_pw_conv=m_w_pw_conv, w_out=m_w_out, norm2_w=m_norm2_w,
             w_ffn_in=m_w_ffn_in, w_ffn_out=m_w_ffn_out)
    v = dict(norm1_w=v_norm1_w, w_in=v_w_in, b_gate=v_b_gate, q_norm_w=v_q_norm_w, k_norm_w=v_k_norm_w,
             w_o_attn=v_w_o_attn, conv_w=v_conv_w, conv_b=v_conv_b, conv_ln_w=v_conv_ln_w,
             conv_ln_b=v_conv_ln_b, w_pw_conv=v_w_pw_conv, w_out=v_w_out, norm2_w=v_norm2_w,
             w_ffn_in=v_w_ffn_in, w_ffn_out=v_w_ffn_out)
    def two_d(t):
        t = {k: (a[0] if a.ndim == 3 else a) for k, a in t.items()}
        return {k: (a.T if k in TRANSPOSED else a) for k, a in t.items()}

    w, m, v = two_d(w), two_d(m), two_d(v)

    x2, target = x[0], loss_target[0]
    c_idx = lax.axis_index("c").astype(jnp.int32)
    chip_idx = (2 * lax.axis_index("x") + lax.axis_index("y")).astype(jnp.int32)
    qw2 = jnp.tile(w["q_norm_w"], (1, 2))
    kw2 = jnp.tile(w["k_norm_w"], (1, 2))

    tabs, ((w_in_blocks,), (bg_blocks,), (cw_blocks,)) = rope_tables(
        positions.reshape(S, 1),
        sides=(ag_blocks_relay(w["w_in"], BF16), ag_blocks(w["b_gate"], F32), ag_blocks(w["conv_w"], F32)))
    w_in_t = w_in_blocks.reshape(INW, D)
    b_gate_f, conv_w_f = _blocks_to_cols(bg_blocks), _blocks_to_cols(cw_blocks)
    (h_t, proj), ((w_o_f,), (w_pw_f,), (w_out_blocks,)) = in_proj(
        x2, w["norm1_w"], w_in_t, sides=(ag_cols(w["w_o_attn"]), ag_cols(w["w_pw_conv"]), ag_blocks_relay(w["w_out"], BF16)))
    w_out_f = w_out_blocks.reshape(D, D)
    (attn, lse), ((w_ffn_in_blocks,),) = attn_fwd(proj, tabs, qw2, kw2, sides=(ag_blocks_relay(w["w_ffn_in"], BF16),))
    w_ffn_in_t = w_ffn_in_blocks.reshape(2 * FF, D)
    cpre, u3 = conv_fwd(proj, conv_w_f, w["conv_b"], w["conv_ln_w"], w["conv_ln_b"])
    x1, z, ya, yb = mix_out(x2, proj, b_gate_f, attn, u3, w_o_f, w_pw_f, w_out_f)
    (h2, gu, f), ((w_ffn_out_blocks,),) = ffn_in(x1, w["norm2_w"], w_ffn_in_t, sides=(ag_blocks_relay(w["w_ffn_out"], BF16),))
    w_ffn_out_f = w_ffn_out_blocks.reshape(FF, D)
    dy, dyb, sq = ffn_out_loss(x1, f, w_ffn_out_f, target)

    g = {}
    g_ffn_out = mm_tn("gw_ffn_out", f, dyb, FF // 2, D).reshape(NDEV, FF // NDEV, D)
    (d_gu, d_x1, d_x1b, g["norm2_w"]), ((ra_ffn_out,),) = ffn_bwd(
        dy, dyb, gu, x1, w["norm2_w"], w_ffn_in_t, w_ffn_out_f, sides=(rs_to_sibling([("rows", g_ffn_out)]),))
    pb_ffn_out, own_ffn_out = chip_sum("chip_sum_w_ffn_out", g_ffn_out, ra_ffn_out, c_idx, chip_idx)
    g_ffn_in, ((rb_ffn_out,),) = mm_tn("gw_ffn_in", d_gu, h2, FF // 2, D,
                                       sides=(rs_to_chips([("rows", pb_ffn_out)]),))
    g_ffn_in = g_ffn_in.reshape(NDEV, 2 * FF // NDEV, D)
    g_out = mm_tn("gw_out", z, d_x1b, D // 2, D).reshape(NDEV, D // NDEV, D)
    (d_ya, d_yb, d_gl, d_attn, d_u3, g["b_gate"]), ((ra_ffn_in,),) = out_bwd(
        d_x1b, proj, b_gate_f, ya, yb, w_o_f, w_pw_f, w_out_f, sides=(rs_to_sibling([("rows", g_ffn_in)]),))
    pb_ffn_in, own_ffn_in = chip_sum("chip_sum_w_ffn_in", g_ffn_in, ra_ffn_in, c_idx, chip_idx)
    g_w_o = mm_tn("gw_o_attn", attn, d_ya, CC, D)
    g_w_pw = mm_tn("gw_pw_conv", u3, d_yb, CC, D)
    (d_conv, g["conv_w"], g["conv_b"], g["conv_ln_w"], g["conv_ln_b"]), ((ra_out, ra_w_o, ra_w_pw),) = conv_bwd(
        proj, cpre, d_u3, conv_w_f, conv_w_f[::-1], w["conv_ln_w"], w["conv_ln_b"],
        sides=(rs_to_sibling([("rows", g_out), ("cols", g_w_o), ("cols", g_w_pw)]),))
    pb_out, own_out = chip_sum("chip_sum_w_out", g_out, ra_out, c_idx, chip_idx)
    pb_w_o, own_w_o = chip_sum_cols("chip_sum_w_o_attn", g_w_o, ra_w_o, c_idx, chip_idx)
    pb_w_pw, own_w_pw = chip_sum_cols("chip_sum_w_pw_conv", g_w_pw, ra_w_pw, c_idx, chip_idx)
    (d_q, d_k, d_v, gqw, gkw), ((rb_ffn_in, rb_out, rb_w_o, rb_w_pw),) = attn_bwd(
        proj, tabs, qw2, kw2, d_attn, attn, lse,
        sides=(rs_to_chips([("rows", pb_ffn_in), ("rows", pb_out), ("cols", pb_w_o), ("cols", pb_w_pw)]),))
    g["q_norm_w"] = gqw[0:1, 0:HD] + gqw[0:1, HD:LANES]
    g["k_norm_w"] = gkw[0:1, 0:HD] + gkw[0:1, HD:LANES]
    mine_w_o, theirs_w_o = col_final("col_final_w_o_attn", own_w_o, rb_w_o, c_idx)
    mine_w_pw, theirs_w_pw = col_final("col_final_w_pw_conv", own_w_pw, rb_w_pw, c_idx)
    d_segs = (d_q, d_k, d_v, d_conv, d_gl)
    parts, to_sibling, to_chips, owns, from_chips = [], None, None, [], []
    for k in range(GW_IN_PARTS):
        sides = [rs_swap_halves([theirs_w_o, theirs_w_pw])] if k == 0 else []
        sides += [s for s in (to_chips, to_sibling) if s is not None]
        part, outs = gw_in_t("gw_in_%d" % k, h_t, d_segs, k, sides=tuple(sides))
        if k == 0:
            (rc_w_o, rc_w_pw), outs = outs[0], outs[1:]
        outs = list(outs)
        if to_chips is not None:
            from_chips.append(outs.pop(0)[0])
        if to_sibling is not None:
            pb, own = chip_sum("chip_sum_w_in_%d" % (k - 1), parts[-1], outs.pop(0)[0], c_idx, chip_idx)
            owns.append(own)
            to_chips = rs_to_chips([("rows", pb)])
        else:
            to_chips = None
        parts.append(part.reshape(NDEV, INW // NDEV, D // GW_IN_PARTS))
        to_sibling = rs_to_sibling([("rows", parts[-1])])
    (grad_x, g["norm1_w"]), ((rb_prev,), (ra_last,)) = in_bwd(
        d_q, d_k, d_v, d_conv, d_gl, w_in_t, x2, d_x1, w["norm1_w"], sides=(to_chips, to_sibling))
    from_chips.append(rb_prev)
    pb, own = chip_sum("chip_sum_w_in_%d" % (GW_IN_PARTS - 1), parts[-1], ra_last, c_idx, chip_idx)
    owns.append(own)
    small, loss, ((rb_last,),) = small_sync_adam(g, w, m, v, sq, sides=(rs_to_chips([("rows", pb)]),))
    from_chips.append(rb_last)

    res = {
        "w_in": shard_adam("adam_w_in", owns, from_chips, w["w_in"], m["w_in"], v["w_in"]),
        "w_ffn_in": shard_adam("adam_w_ffn_in", [own_ffn_in], [rb_ffn_in], w["w_ffn_in"], m["w_ffn_in"], v["w_ffn_in"]),
        "w_o_attn": adam_cols("adam_w_o_attn", mine_w_o, rc_w_o, w["w_o_attn"], m["w_o_attn"], v["w_o_attn"], c_idx),
        "w_pw_conv": adam_cols("adam_w_pw_conv", mine_w_pw, rc_w_pw, w["w_pw_conv"], m["w_pw_conv"], v["w_pw_conv"], c_idx),
        "w_out": shard_adam("adam_w_out", [own_out], [rb_out], w["w_out"], m["w_out"], v["w_out"]),
        "w_ffn_out": shard_adam("adam_w_ffn_out", [own_ffn_out], [rb_ffn_out],
                                w["w_ffn_out"], m["w_ffn_out"], v["w_ffn_out"]),
    }
    res = {k: tuple(a.T if k in TRANSPOSED else a for a in r) for k, r in res.items()}
    res.update(small)

    def shaped(name, a):
        return a.reshape((1,) + a.shape) if name in MATS or name in ("b_gate", "conv_w") else a

    outs = [loss, grad_x.reshape(1, S, D)]
    for i in range(4):
        outs += [shaped(k, res[k][i]) for k in WEIGHTS]
    return tuple(outs)
```

```python
import functools
from typing import Callable, NamedTuple, Optional

import numpy as np
import jax
import jax.numpy as jnp
from jax import lax
from jax.experimental import pallas as pl
from jax.experimental.pallas import tpu as pltpu

F32 = jnp.float32
BF16 = jnp.bfloat16

S = 2048
D = 1024
HD = 64
QKV = 1536
CC = 512
KW = 31
FF = 2816
INW = 7680
OFF_Q, OFF_K, OFF_V, OFF_CA, OFF_CB, OFF_GA, OFF_GB = 0, 1536, 3072, 4608, 5120, 5632, 6656
DILATIONS = (1, 4, 16)
HALF_SPAN = 64
EPS = 1e-6
NEG_INF = -1e30
ROPE_THETA = 500000.0
ROT_DIM = 16

ADAM_LR = 0.001
ADAM_B1 = 0.9
ADAM_B2 = 0.999
ADAM_EPS = 1e-08
ADAM_WD = 0.01
ADAM_STEP = 10

NDEV = 8
LANES = 128
TM = 256
TQ = 128
VMEM_LIMIT = 56 * 1024 * 1024
MESH = pl.DeviceIdType.MESH


def _cp(**kw):
    return pltpu.CompilerParams(vmem_limit_bytes=VMEM_LIMIT, **kw)


def _row(width, col=0, tm=TM):
    return pl.BlockSpec((tm, width), lambda i: (i, col))


PLANE = 512


def _planes(width, tm=TM):
    return pl.BlockSpec((width // PLANE, tm, PLANE), lambda i: (0, i, 0))


def _res(shape):
    nd = len(shape)
    return pl.BlockSpec(shape, lambda *_: (0,) * nd, pipeline_mode=pl.Buffered(1))


def _dot(a, b):
    return jnp.dot(a, b, preferred_element_type=F32)


def _dot_nt(a, b):
    return lax.dot_general(a, b, (((1,), (1,)), ((), ())), preferred_element_type=F32)


def _dot_tn(a, b):
    return lax.dot_general(a, b, (((0,), (0,)), ((), ())), preferred_element_type=F32)


def _sigmoid(x):
    return jax.nn.sigmoid(x)


def _dsilu(x, sg):
    return sg * (1.0 + x * (1.0 - sg))


ANY = pl.BlockSpec(memory_space=pl.ANY)
VMEM = pl.BlockSpec(memory_space=pltpu.VMEM)


class Side(NamedTuple):
    args: tuple
    in_specs: tuple
    out_shape: tuple
    scratch: tuple
    start: Callable
    finish: Callable
    mid: Optional[Callable] = None


def _call(body, sides=(), *, name, grid, in_specs, out_specs, out_shape, scratch_shapes=(), args):
    ni, no, ns = len(in_specs), len(out_specs), len(scratch_shapes)
    cnt = [(len(s.args), len(s.out_shape), len(s.scratch)) for s in sides]

    def take(refs, pos, n):
        return refs[pos:pos + n], pos + n

    def full(*refs):
        m_in, pos = take(refs, 0, ni)
        s_in = []
        for a, _, _ in cnt:
            r, pos = take(refs, pos, a)
            s_in.append(r)
        m_out, pos = take(refs, pos, no)
        s_out = []
        for _, o, _ in cnt:
            r, pos = take(refs, pos, o)
            s_out.append(r)
        m_scr, pos = take(refs, pos, ns)
        s_scr = []
        for _, _, c in cnt:
            r, pos = take(refs, pos, c)
            s_scr.append(r)
        if sides:
            first = functools.reduce(jnp.logical_and, [pl.program_id(d) == 0 for d in range(len(grid))])
            last = functools.reduce(jnp.logical_and, [pl.program_id(d) == g - 1 for d, g in enumerate(grid)])

            @pl.when(first)
            def _():
                for s, a, o, c in zip(sides, s_in, s_out, s_scr):
                    s.start(a, o, c)

            steps = int(np.prod(grid))
            mid_step = (2 * steps) // 3
            if steps > 1 and any(s.mid is not None for s in sides):
                step = functools.reduce(lambda acc, d: acc * grid[d] + pl.program_id(d), range(len(grid)), 0)

                @pl.when(step == mid_step)
                def _():
                    for s, a, o, c in zip(sides, s_in, s_out, s_scr):
                        if s.mid is not None:
                            s.mid(a, o, c)

        body(*m_in, *m_out, *m_scr)
        if sides:
            @pl.when(last)
            def _():
                for s, a, o, c in zip(sides, s_in, s_out, s_scr):
                    if s.mid is not None and steps == 1:
                        s.mid(a, o, c)
                    s.finish(a, o, c)

    res = pl.pallas_call(
        full, name=name, grid=grid,
        in_specs=list(in_specs) + [sp for s in sides for sp in s.in_specs],
        out_specs=list(out_specs) + [ANY for s in sides for _ in s.out_shape],
        out_shape=list(out_shape) + [o for s in sides for o in s.out_shape],
        scratch_shapes=list(scratch_shapes) + [c for s in sides for c in s.scratch],
        compiler_params=_cp(dimension_semantics=("arbitrary",) * len(grid)),
    )(*args, *[a for s in sides for a in s.args])
    res = list(res)
    if not sides:
        return res
    outs, pos = take(res, 0, no)
    side_outs = []
    for _, o, _ in cnt:
        r, pos = take(res, pos, o)
        side_outs.append(r)
    return outs, side_outs


def _inv_freq_lanes():
    inv = np.float32(ROPE_THETA) ** (-np.arange(0, ROT_DIM, 2, dtype=np.float32) / np.float32(ROT_DIM))
    lane = np.arange(LANES) % HD
    out = np.where(lane < ROT_DIM, inv[lane % (ROT_DIM // 2)], 0.0).astype(np.float32)
    return jnp.asarray(out.reshape(1, LANES))


def rope_tables(pos_col, sides=()):
    def body(p_ref, f_ref, c_ref, s1_ref, s2_ref):
        ang = p_ref[...].astype(F32) * f_ref[...]
        lane = lax.broadcasted_iota(jnp.int32, ang.shape, 1) % HD
        cs = jnp.cos(ang)
        sn = jnp.sin(ang)
        c_ref[...] = jnp.where(lane < ROT_DIM, cs, 1.0)
        s1_ref[...] = jnp.where(lane < ROT_DIM // 2, -sn, 0.0)
        s2_ref[...] = jnp.where(lane < ROT_DIM // 2, 0.0, jnp.where(lane < ROT_DIM, sn, 0.0))

    sds = jax.ShapeDtypeStruct((S, LANES), F32)
    return _call(
        body, sides, name="rope_tables", grid=(S // TM,),
        in_specs=[_row(1), pl.BlockSpec((1, LANES), lambda i: (0, 0))],
        out_specs=[_row(LANES)] * 3, out_shape=[sds] * 3,
        args=(pos_col, _inv_freq_lanes()))


def _rope(v, c, s1, s2):
    return v * c + pltpu.roll(v, LANES - 8, axis=1) * s1 + pltpu.roll(v, 8, axis=1) * s2


def _rope_t(d, c, s1, s2):
    return d * c - pltpu.roll(d, LANES - 8, axis=1) * s1 - pltpu.roll(d, 8, axis=1) * s2


def _head_mat():
    r = lax.broadcasted_iota(jnp.int32, (LANES, LANES), 0) // HD
    c = lax.broadcasted_iota(jnp.int32, (LANES, LANES), 1) // HD
    return jnp.where(r == c, 1.0 / HD, 0.0).astype(BF16)


def _head_mean(t, e):
    hi = t.astype(BF16)
    rest = (t - hi.astype(F32)).astype(BF16)
    return _dot(hi, e) + _dot(rest, e)


def in_proj(x, norm_w, w_in, sides=()):
    nchunk = 5
    cw = INW // nchunk

    def body(x_ref, nw_ref, w_ref, ht_ref, p_ref):
        xv = x_ref[...]
        r = lax.rsqrt(jnp.mean(xv * xv, axis=-1, keepdims=True) + EPS)
        hf = xv * r * nw_ref[...]
        ht_ref[...] = hf.T.astype(BF16)
        h = hf.astype(BF16)
        for j in range(nchunk):
            p_ref[:, j * cw:(j + 1) * cw] = _dot_nt(h, w_ref[j * cw:(j + 1) * cw, :])

    return _call(
        body, sides, name="in_proj", grid=(S // TM,),
        in_specs=[_row(D), _res((1, D)), _res((INW, D))],
        out_specs=[pl.BlockSpec((D, TM), lambda i: (0, i)), _row(INW)],
        out_shape=[jax.ShapeDtypeStruct((D, S), BF16), jax.ShapeDtypeStruct((S, INW), F32)],
        args=(x, norm_w, w_in))


def _qk_specs():
    nb = QKV // LANES
    return [pl.BlockSpec((S, LANES), functools.partial(lambda hp, g, o: (0, o + g * 4 + hp), o=o))
            for o in (OFF_Q // LANES, OFF_K // LANES, OFF_V // LANES)]


def _tab_specs():
    return [pl.BlockSpec((S, LANES), lambda hp, g: (0, 0), pipeline_mode=pl.Buffered(1))] * 3


def _vec_spec():
    return pl.BlockSpec((1, LANES), lambda hp, g: (0, 0))


def _sub_rows(r, d, start, n):
    if d == 1:
        return pl.ds(start, n)
    return pl.ds(r + d * start, n, stride=d)


def _band_window(i, L):
    W = min(TQ + 2 * HALF_SPAN, L)
    q0 = pl.multiple_of(i * TQ, TQ)
    k0 = pl.multiple_of(jnp.clip(q0 - HALF_SPAN, 0, L - W), HALF_SPAN)
    qpos = q0 + (lax.broadcasted_iota(jnp.int32, (2 * TQ, W), 0) & (TQ - 1))
    kpos = k0 + lax.broadcasted_iota(jnp.int32, (2 * TQ, W), 1)
    valid = jnp.abs(qpos - kpos) <= HALF_SPAN
    return W, q0, k0, valid


def _stack_heads(t, lo):
    z = jnp.zeros_like(t)
    return jnp.concatenate([jnp.where(lo, t, z), jnp.where(lo, z, t)], axis=0)


def _unstack_heads(t2, lo):
    return jnp.where(lo, t2[0:TQ], t2[TQ:2 * TQ])


CHAINS = 4


def _interleave(d):
    ru = min(d, CHAINS)
    return ru, min(CHAINS // ru, S // d // TQ)


def _for_blocks(n, fn):
    if n == 1:
        fn(0)
    else:
        def it(j, _):
            fn(j)
            return 0
        lax.fori_loop(0, n, it, 0)


def attn_fwd(proj, tabs, qw2, kw2, sides=()):
    CH = 256

    def body(q_ref, k_ref, v_ref, c_ref, s1_ref, s2_ref, qw_ref, kw_ref, at_ref, ls_ref,
             qs, ks, vs, osub, lsub, onat, lnat, qn, kn):
        g = pl.program_id(1)
        lo = lax.broadcasted_iota(jnp.int32, (1, LANES), 1) < HD
        e = _head_mat()

        def prep(i, _):
            rows = pl.ds(pl.multiple_of(i * CH, CH), CH)
            c, s1, s2 = c_ref[rows, :], s1_ref[rows, :], s2_ref[rows, :]
            for t_ref, w_ref, out, scale in ((q_ref, qw_ref, qn, HD ** -0.5), (k_ref, kw_ref, kn, 1.0)):
                t = t_ref[rows, :]
                r = lax.rsqrt(_head_mean(t * t, e) + EPS)
                out[rows, :] = _rope(t * r * w_ref[...], c, s1, s2) * scale
            return 0

        lax.fori_loop(0, S // CH, prep, 0, unroll=4)

        def group(gi, d):
            L = S // d

            ru, nb = _interleave(d)

            def stage(r, off):
                for c0 in range(0, L, CH):
                    n = min(CH, L)
                    rows = _sub_rows(r, d, c0, n)
                    dst = pl.ds(off + c0, n)
                    qs[dst, :] = qn[rows, :].astype(BF16)
                    ks[dst, :] = kn[rows, :].astype(BF16)
                    vs[dst, :] = v_ref[rows, :].astype(BF16)

            def one(off, i):
                W, q0, k0, valid = _band_window(i, L)
                q2 = _stack_heads(qs[pl.ds(off + q0, TQ), :], lo)
                sc = jnp.where(valid, _dot_nt(q2, ks[pl.ds(off + k0, W), :]), NEG_INF)
                m = jnp.max(sc, axis=-1, keepdims=True)
                p = jnp.exp(sc - m)
                den = jnp.sum(p, axis=-1, keepdims=True)
                o2 = _dot(p.astype(BF16), vs[pl.ds(off + k0, W), :]) / den
                l2 = jnp.broadcast_to(m + jnp.log(den), (2 * TQ, LANES))
                osub[pl.ds(off + q0, TQ), :] = _unstack_heads(o2, lo)
                lsub[pl.ds(off + q0, TQ), :] = _unstack_heads(l2, lo)

            def unstage(r, off):
                for c0 in range(0, L, CH):
                    n = min(CH, L)
                    rows = _sub_rows(r, d, c0, n)
                    onat[gi, rows, :] = osub[pl.ds(off + c0, n), :]
                    lnat[gi, rows, :] = lsub[pl.ds(off + c0, n), :]

            def step(t, _):
                for u in range(ru):
                    stage(t * ru + u, u * L)
                _for_blocks(L // TQ // nb, lambda j: [one(u * L, j * nb + b) for u in range(ru) for b in range(nb)])
                for u in range(ru):
                    unstage(t * ru + u, u * L)
                return 0

            lax.fori_loop(0, d // ru, step, 0)

        for gi, d in enumerate(DILATIONS):
            pl.when(g == gi)(functools.partial(group, gi, d))

        @pl.when(g == len(DILATIONS) - 1)
        def _():
            def mix(i, _):
                rows = pl.ds(pl.multiple_of(i * CH, CH), CH)
                l0, l1, l2 = lnat[0, rows, :], lnat[1, rows, :], lnat[2, rows, :]
                m = jnp.maximum(jnp.maximum(l0, l1), l2)
                e0, e1, e2 = jnp.exp(l0 - m), jnp.exp(l1 - m), jnp.exp(l2 - m)
                den = e0 + e1 + e2
                a = (e0 * onat[0, rows, :] + e1 * onat[1, rows, :] + e2 * onat[2, rows, :]) / den
                at_ref[rows, :] = a.astype(BF16)
                ls_ref[rows, :] = m + jnp.log(den)
                return 0

            lax.fori_loop(0, S // CH, mix, 0)

    out_spec = pl.BlockSpec((S, LANES), lambda hp, g: (0, hp))
    return _call(
        body, sides, name="attn_fwd", grid=(4, 3),
        in_specs=_qk_specs() + _tab_specs() + [_vec_spec(), _vec_spec()],
        out_specs=[out_spec, out_spec],
        out_shape=[jax.ShapeDtypeStruct((S, CC), BF16), jax.ShapeDtypeStruct((S, CC), F32)],
        scratch_shapes=[pltpu.VMEM((S, LANES), BF16)] * 3 + [pltpu.VMEM((S, LANES), F32)] * 2
        + [pltpu.VMEM((3, S, LANES), F32)] * 2 + [pltpu.VMEM((S, LANES), F32)] * 2,
        args=(proj, proj, proj, *tabs, qw2, kw2))


def attn_bwd(proj, tabs, qw2, kw2, d_attn, attn, lse, sides=()):
    CH = 256

    def body(q_ref, k_ref, v_ref, c_ref, s1_ref, s2_ref, qw_ref, kw_ref, do_ref, at_ref, ls_ref,
             dq_ref, dk_ref, dv_ref, gqw_ref, gkw_ref,
             qs, ks, vs, dos, dsub, lsub, dqs, dks, dvs, dnat, qx, kx, dvn, tnq, tnk, rrq, rrk):
        hp, g = pl.program_id(0), pl.program_id(1)
        lo = lax.broadcasted_iota(jnp.int32, (1, LANES), 1) < HD
        e = _head_mat()
        both = ((q_ref, qw_ref, qx, tnq, rrq, HD ** -0.5), (k_ref, kw_ref, kx, tnk, rrk, 1.0))

        @pl.when((hp == 0) & (g == 0))
        def _():
            gqw_ref[...] = jnp.zeros_like(gqw_ref)
            gkw_ref[...] = jnp.zeros_like(gkw_ref)

        def prep(i, _):
            rows = pl.ds(pl.multiple_of(i * CH, CH), CH)
            dnat[rows, :] = _head_mean(do_ref[rows, :] * at_ref[rows, :].astype(F32), e) * float(HD)
            c, s1, s2 = c_ref[rows, :], s1_ref[rows, :], s2_ref[rows, :]
            for t_ref, w_ref, x, tn_s, rr_s, scale in both:
                t = t_ref[rows, :]
                rr = lax.rsqrt(_head_mean(t * t, e) + EPS)
                tn = t * rr
                rr_s[rows, :] = rr
                tn_s[rows, :] = tn
                x[rows, :] = _rope(tn * w_ref[...], c, s1, s2) * scale
            return 0

        lax.fori_loop(0, S // CH, prep, 0, unroll=4)

        def group(d):
            L = S // d

            ru, nb = _interleave(d)

            def stage(r, off):
                for c0 in range(0, L, CH):
                    n = min(CH, L)
                    rows = _sub_rows(r, d, c0, n)
                    dst = pl.ds(off + c0, n)
                    qs[dst, :] = qx[rows, :].astype(BF16)
                    ks[dst, :] = kx[rows, :].astype(BF16)
                    vs[dst, :] = v_ref[rows, :].astype(BF16)
                    dos[dst, :] = do_ref[rows, :].astype(BF16)
                    dsub[dst, :] = dnat[rows, :]
                    lsub[dst, :] = ls_ref[rows, :]
                    dks[dst, :] = jnp.zeros((n, LANES), F32)
                    dvs[dst, :] = jnp.zeros((n, LANES), F32)

            def one(off, i):
                W, q0, k0, valid = _band_window(i, L)
                qrows, krows = pl.ds(off + q0, TQ), pl.ds(off + k0, W)
                q2 = _stack_heads(qs[qrows, :], lo)
                do2 = _stack_heads(dos[qrows, :], lo)
                kk, vv = ks[krows, :], vs[krows, :]
                lse_b, dd_b = lsub[qrows, :], dsub[qrows, :]
                lse2 = jnp.concatenate([lse_b[:, 0:1], lse_b[:, HD:HD + 1]], axis=0)
                dd2 = jnp.concatenate([dd_b[:, 0:1], dd_b[:, HD:HD + 1]], axis=0)
                sc = jnp.where(valid, _dot_nt(q2, kk), NEG_INF)
                p = jnp.exp(sc - lse2)
                ds = (p * (_dot_nt(do2, vv) - dd2)).astype(BF16)
                dqs[qrows, :] = _unstack_heads(_dot(ds, kk), lo)
                dks[krows, :] = dks[krows, :] + _dot_tn(ds, q2)
                dvs[krows, :] = dvs[krows, :] + _dot_tn(p.astype(BF16), do2)

            def unstage(r, off):
                for c0 in range(0, L, CH):
                    n = min(CH, L)
                    rows = _sub_rows(r, d, c0, n)
                    src = pl.ds(off + c0, n)
                    qx[rows, :] = dqs[src, :]
                    kx[rows, :] = dks[src, :]
                    dvn[rows, :] = dvs[src, :]

            def step(t, _):
                for u in range(ru):
                    stage(t * ru + u, u * L)
                _for_blocks(L // TQ // nb, lambda j: [one(u * L, j * nb + b) for u in range(ru) for b in range(nb)])
                for u in range(ru):
                    unstage(t * ru + u, u * L)
                return 0

            lax.fori_loop(0, d // ru, step, 0)

        for gi, d in enumerate(DILATIONS):
            pl.when(g == gi)(functools.partial(group, d))

        def emit(i, _):
            rows = pl.ds(pl.multiple_of(i * CH, CH), CH)
            c, s1, s2 = c_ref[rows, :], s1_ref[rows, :], s2_ref[rows, :]
            for (_, w_ref, x, tn_s, rr_s, scale), out, gw_ref in zip(both, (dq_ref, dk_ref), (gqw_ref, gkw_ref)):
                tn = tn_s[rows, :]
                dy = _rope_t(x[rows, :] * scale, c, s1, s2)
                gw_ref[0:1, :] = gw_ref[0:1, :] + jnp.sum(dy * tn, axis=0, keepdims=True)
                dtn = dy * w_ref[...]
                out[rows, :] = (rr_s[rows, :] * (dtn - tn * _head_mean(dtn * tn, e))).astype(BF16)
            dv_ref[rows, :] = dvn[rows, :].astype(BF16)
            return 0

        lax.fori_loop(0, S // CH, emit, 0, unroll=4)

    nat_spec = pl.BlockSpec((S, LANES), lambda hp, g: (0, hp))
    out_spec = pl.BlockSpec((None, S, LANES), lambda hp, g: (g, 0, hp))
    acc_spec = pl.BlockSpec((8, LANES), lambda hp, g: (0, 0))
    return _call(
        body, sides, name="attn_bwd", grid=(4, 3),
        in_specs=_qk_specs() + _tab_specs() + [_vec_spec(), _vec_spec(), nat_spec, nat_spec, nat_spec],
        out_specs=[out_spec] * 3 + [acc_spec] * 2,
        out_shape=[jax.ShapeDtypeStruct((QKV // PLANE, S, PLANE), BF16)] * 3 + [jax.ShapeDtypeStruct((8, LANES), F32)] * 2,
        scratch_shapes=[pltpu.VMEM((S, LANES), BF16)] * 4 + [pltpu.VMEM((S, LANES), F32)] * 13,
        args=(proj, proj, proj, *tabs, qw2, kw2, d_attn, attn, lse))


PADR = 16
CT = 128


def _conv_specs():
    return [pl.BlockSpec((S, CC), lambda i: (0, OFF_CA // CC)), pl.BlockSpec((S, CC), lambda i: (0, OFF_CB // CC))]


NCB = CC // LANES


def _pad_zero(pad):
    for cb in range(NCB):
        pad[cb, 0:PADR, :] = jnp.zeros((PADR, LANES), F32)
        pad[cb, PADR + S:PADR + S + PADR, :] = jnp.zeros((PADR, LANES), F32)


def _pad_store(pad, row0, n, val):
    for cb in range(NCB):
        pad[cb, pl.ds(pl.multiple_of(row0 + PADR, 8), n), :] = val[:, cb * LANES:(cb + 1) * LANES]


def _taps(pad_ref, cb, s0, weights):
    acc = jnp.zeros((CT, LANES), F32)
    for k in range(KW):
        acc = acc + weights[k] * pad_ref[cb, pl.ds(s0 + k + 1, CT), :]
    return acc


def conv_fwd(proj, conv_w, conv_b, ln_w, ln_b):
    def body(a_ref, b_ref, w_ref, cb_ref, lw_ref, lb_ref, c_ref, u3_ref, upad):
        _pad_zero(upad)

        def glu(i, _):
            rows = pl.ds(pl.multiple_of(i * TM, TM), TM)
            _pad_store(upad, i * TM, TM, a_ref[rows, :] * _sigmoid(b_ref[rows, :]))
            return 0

        lax.fori_loop(0, S // TM, glu, 0)

        def chunk(i, _):
            s0 = pl.multiple_of(i * CT, CT)
            for cb in range(CC // LANES):
                cols = slice(cb * LANES, (cb + 1) * LANES)
                w = [w_ref[k:k + 1, cols] for k in range(KW)]
                c_ref[pl.ds(s0, CT), cols] = _taps(upad, cb, s0, w) + cb_ref[:, cols]
            cv = c_ref[pl.ds(s0, CT), :]
            mu = jnp.mean(cv, axis=-1, keepdims=True)
            xc = cv - mu
            rstd = lax.rsqrt(jnp.mean(xc * xc, axis=-1, keepdims=True) + EPS)
            yl = xc * rstd * lw_ref[...] + lb_ref[...]
            u3_ref[pl.ds(s0, CT), :] = (yl * _sigmoid(yl)).astype(BF16)
            return 0

        lax.fori_loop(0, S // CT, chunk, 0)

    vec = pl.BlockSpec((1, CC), lambda i: (0, 0))
    full = pl.BlockSpec((S, CC), lambda i: (0, 0))
    return pl.pallas_call(
        body, name="conv_fwd", grid=(1,),
        in_specs=_conv_specs() + [pl.BlockSpec((KW, CC), lambda i: (0, 0)), vec, vec, vec],
        out_specs=[full, full],
        out_shape=[jax.ShapeDtypeStruct((S, CC), F32), jax.ShapeDtypeStruct((S, CC), BF16)],
        scratch_shapes=[pltpu.VMEM((NCB, S + 2 * PADR, LANES), F32)],
        compiler_params=_cp(dimension_semantics=("arbitrary",)),
    )(proj, proj, conv_w, conv_b, ln_w, ln_b)


def conv_bwd(proj, cpre, d_u3, conv_w, conv_w_rev, ln_w, ln_b, sides=()):
    def body(a_ref, b_ref, c_ref, du3_ref, w_ref, wr_ref, lw_ref, lb_ref,
             dc_ref, gw_ref, gcb_ref, glw_ref, glb_ref, upad, dpad):
        _pad_zero(upad)
        _pad_zero(dpad)
        gw_ref[...] = jnp.zeros_like(gw_ref)

        def ln_bwd(i, carry):
            gcb, glw, glb = carry
            rows = pl.ds(pl.multiple_of(i * TM, TM), TM)
            _pad_store(upad, i * TM, TM, a_ref[rows, :] * _sigmoid(b_ref[rows, :]))
            cv = c_ref[rows, :]
            mu = jnp.mean(cv, axis=-1, keepdims=True)
            xc = cv - mu
            rstd = lax.rsqrt(jnp.mean(xc * xc, axis=-1, keepdims=True) + EPS)
            xh = xc * rstd
            yl = xh * lw_ref[...] + lb_ref[...]
            dyl = du3_ref[rows, :] * _dsilu(yl, _sigmoid(yl))
            dxh = dyl * lw_ref[...]
            dcv = rstd * (dxh - jnp.mean(dxh, axis=-1, keepdims=True)
                          - xh * jnp.mean(dxh * xh, axis=-1, keepdims=True))
            _pad_store(dpad, i * TM, TM, dcv)
            return (gcb + jnp.sum(dcv, axis=0, keepdims=True),
                    glw + jnp.sum(dyl * xh, axis=0, keepdims=True),
                    glb + jnp.sum(dyl, axis=0, keepdims=True))

        z = jnp.zeros((1, CC), F32)
        gcb, glw, glb = lax.fori_loop(0, S // TM, ln_bwd, (z, z, z))
        gcb_ref[...] = gcb
        glw_ref[...] = glw
        glb_ref[...] = glb

        def chunk(i, _):
            s0 = pl.multiple_of(i * CT, CT)
            for cb in range(CC // LANES):
                cols = slice(cb * LANES, (cb + 1) * LANES)
                wr = [wr_ref[k:k + 1, cols] for k in range(KW)]
                du = _taps(dpad, cb, s0, wr)
                dcv = dpad[cb, pl.ds(s0 + PADR, CT), :]
                for k in range(KW):
                    gw_ref[k:k + 1, cols] = gw_ref[k:k + 1, cols] + jnp.sum(
                        upad[cb, pl.ds(s0 + k + 1, CT), :] * dcv, axis=0, keepdims=True)
                av = a_ref[pl.ds(s0, CT), cols]
                sb = _sigmoid(b_ref[pl.ds(s0, CT), cols])
                dc_ref[0, pl.ds(s0, CT), cols] = (du * sb).astype(BF16)
                dc_ref[1, pl.ds(s0, CT), cols] = (du * av * sb * (1.0 - sb)).astype(BF16)
            return 0

        lax.fori_loop(0, S // CT, chunk, 0)

    vec = pl.BlockSpec((1, CC), lambda i: (0, 0))
    full = pl.BlockSpec((S, CC), lambda i: (0, 0))
    wsp = pl.BlockSpec((KW, CC), lambda i: (0, 0))
    return _call(
        body, sides, name="conv_bwd", grid=(1,),
        in_specs=_conv_specs() + [full, full, wsp, wsp, vec, vec],
        out_specs=[pl.BlockSpec((2, S, CC), lambda i: (0, 0, 0)), wsp, vec, vec, vec],
        out_shape=[jax.ShapeDtypeStruct((2, S, CC), BF16), jax.ShapeDtypeStruct((KW, CC), F32)]
        + [jax.ShapeDtypeStruct((1, CC), F32)] * 3,
        scratch_shapes=[pltpu.VMEM((NCB, S + 2 * PADR, LANES), F32)] * 2,
        args=(proj, proj, cpre, d_u3, conv_w, conv_w_rev, ln_w, ln_b))


def _gate_specs():
    return [_row(CC, col=OFF_GA // CC + j) for j in range(4)]


def _gates(g_refs, bg_ref):
    ga = _sigmoid(jnp.concatenate([g_refs[0][...], g_refs[1][...]], axis=1) + bg_ref[0:1, :])
    gb = _sigmoid(jnp.concatenate([g_refs[2][...], g_refs[3][...]], axis=1) + bg_ref[1:2, :])
    return ga, gb


def mix_out(x, proj, b_gate, attn, u3, w_o, w_pw, w_out):
    def body(x_ref, g0, g1, g2, g3, bg_ref, at_ref, u3_ref, wo_ref, wp_ref, wout_ref,
             x1_ref, z_ref, ya_ref, yb_ref):
        ga, gb = _gates((g0, g1, g2, g3), bg_ref)
        ya = _dot(at_ref[...], wo_ref[...])
        yb = _dot(u3_ref[...], wp_ref[...])
        z = (ga * ya + gb * yb).astype(BF16)
        ya_ref[...] = ya.astype(BF16)
        yb_ref[...] = yb.astype(BF16)
        z_ref[...] = z
        x1_ref[...] = x_ref[...] + _dot(z, wout_ref[...])

    return pl.pallas_call(
        body, name="mix_out", grid=(S // TM,),
        in_specs=[_row(D)] + _gate_specs() + [_res((2, D)), _row(CC), _row(CC),
                                              _res((CC, D)), _res((CC, D)), _res((D, D))],
        out_specs=[_row(D)] * 4,
        out_shape=[jax.ShapeDtypeStruct((S, D), F32)] + [jax.ShapeDtypeStruct((S, D), BF16)] * 3,
        compiler_params=_cp(dimension_semantics=("arbitrary",)),
    )(x, proj, proj, proj, proj, b_gate, attn, u3, w_o, w_pw, w_out)


def out_bwd(d_x1b, proj, b_gate, ya, yb, w_o, w_pw, w_out, sides=()):
    def body(dx_ref, g0, g1, g2, g3, bg_ref, ya_ref, yb_ref, wo_ref, wp_ref, wout_ref,
             dya_ref, dyb_ref, dgl_ref, dat_ref, du3_ref, gbg_ref):
        @pl.when(pl.program_id(0) == 0)
        def _():
            gbg_ref[...] = jnp.zeros_like(gbg_ref)

        ga, gb = _gates((g0, g1, g2, g3), bg_ref)
        dz = _dot_nt(dx_ref[...], wout_ref[...])
        dya = (dz * ga).astype(BF16)
        dyb = (dz * gb).astype(BF16)
        dgla = dz * ya_ref[...].astype(F32) * ga * (1.0 - ga)
        dglb = dz * yb_ref[...].astype(F32) * gb * (1.0 - gb)
        dya_ref[...] = dya
        dyb_ref[...] = dyb
        for j in range(2):
            dgl_ref[j] = dgla[:, j * PLANE:(j + 1) * PLANE].astype(BF16)
            dgl_ref[2 + j] = dglb[:, j * PLANE:(j + 1) * PLANE].astype(BF16)
        gbg_ref[0:1, :] = gbg_ref[0:1, :] + jnp.sum(dgla, axis=0, keepdims=True)
        gbg_ref[1:2, :] = gbg_ref[1:2, :] + jnp.sum(dglb, axis=0, keepdims=True)
        dat_ref[...] = _dot_nt(dya, wo_ref[...])
        du3_ref[...] = _dot_nt(dyb, wp_ref[...])

    return _call(
        body, sides, name="out_bwd", grid=(S // TM,),
        in_specs=[_row(D)] + _gate_specs() + [_res((2, D)), _row(D), _row(D),
                                              _res((CC, D)), _res((CC, D)), _res((D, D))],
        out_specs=[_row(D), _row(D), _planes(2 * D), _row(CC), _row(CC), pl.BlockSpec((2, D), lambda i: (0, 0))],
        out_shape=[jax.ShapeDtypeStruct((S, D), BF16)] * 2 + [jax.ShapeDtypeStruct((2 * D // PLANE, S, PLANE), BF16)]
        + [jax.ShapeDtypeStruct((S, CC), F32)] * 2 + [jax.ShapeDtypeStruct((2, D), F32)],
        args=(d_x1b, proj, proj, proj, proj, b_gate, ya, yb, w_o, w_pw, w_out))


def ffn_in(x1, norm_w, w_ffn_in, sides=()):
    half = FF // 2

    def body(x_ref, nw_ref, w_ref, h_ref, gu_ref, f_ref):
        xv = x_ref[...]
        r = lax.rsqrt(jnp.mean(xv * xv, axis=-1, keepdims=True) + EPS)
        h = (xv * r * nw_ref[...]).astype(BF16)
        h_ref[...] = h
        for j in range(2):
            gt = _dot_nt(h, w_ref[j * half:(j + 1) * half, :])
            up = _dot_nt(h, w_ref[FF + j * half:FF + (j + 1) * half, :])
            gu_ref[:, j * half:(j + 1) * half] = gt.astype(BF16)
            gu_ref[:, FF + j * half:FF + (j + 1) * half] = up.astype(BF16)
            f_ref[:, j * half:(j + 1) * half] = (gt * _sigmoid(gt) * up).astype(BF16)

    return _call(
        body, sides, name="ffn_in", grid=(S // TM,),
        in_specs=[_row(D), _res((1, D)), _res((2 * FF, D))],
        out_specs=[_row(D), _row(2 * FF), _row(FF)],
        out_shape=[jax.ShapeDtypeStruct((S, D), BF16), jax.ShapeDtypeStruct((S, 2 * FF), BF16),
                   jax.ShapeDtypeStruct((S, FF), BF16)],
        args=(x1, norm_w, w_ffn_in))


def ffn_out_loss(x1, f, w_ffn_out, target):
    def body(x_ref, f_ref, w_ref, t_ref, dy_ref, dyb_ref, sq_ref):
        @pl.when(pl.program_id(0) == 0)
        def _():
            sq_ref[...] = jnp.zeros_like(sq_ref)

        diff = x_ref[...] + _dot(f_ref[...], w_ref[...]) - t_ref[...]
        dy = diff * (1.0 / D)
        dy_ref[...] = dy
        dyb_ref[...] = dy.astype(BF16)
        sq_ref[...] = sq_ref[...] + jnp.sum((diff * diff).reshape(TM // 8, 8, D), axis=0)

    return pl.pallas_call(
        body, name="ffn_out_loss", grid=(S // TM,),
        in_specs=[_row(D), _row(FF), _res((FF, D)), _row(D)],
        out_specs=[_row(D), _row(D), pl.BlockSpec((8, D), lambda i: (0, 0))],
        out_shape=[jax.ShapeDtypeStruct((S, D), F32), jax.ShapeDtypeStruct((S, D), BF16),
                   jax.ShapeDtypeStruct((8, D), F32)],
        compiler_params=_cp(dimension_semantics=("arbitrary",)),
    )(x1, f, w_ffn_out, target)


def _rms_bwd(xv, nw, dh):
    r = lax.rsqrt(jnp.mean(xv * xv, axis=-1, keepdims=True) + EPS)
    xn = xv * r
    dxn = dh * nw
    dx = r * (dxn - xn * jnp.mean(dxn * xn, axis=-1, keepdims=True))
    return dx, dh * xn


def ffn_bwd(dy, dyb, gu, x1, norm_w, w_ffn_in, w_ffn_out, sides=()):
    def body(dy_ref, dyb_ref, gu_ref, x_ref, nw_ref, wi_ref, wo_ref, dgu_ref, dx_ref, dxb_ref, gn_ref):
        @pl.when(pl.program_id(0) == 0)
        def _():
            gn_ref[...] = jnp.zeros_like(gn_ref)

        df = _dot_nt(dyb_ref[...], wo_ref[...])
        gt = gu_ref[:, 0:FF].astype(F32)
        up = gu_ref[:, FF:2 * FF].astype(F32)
        sg = _sigmoid(gt)
        dgt = (df * up * _dsilu(gt, sg)).astype(BF16)
        dup = (df * gt * sg).astype(BF16)
        dgu_ref[:, 0:FF] = dgt
        dgu_ref[:, FF:2 * FF] = dup
        dh = _dot(dgt, wi_ref[0:FF, :]) + _dot(dup, wi_ref[FF:2 * FF, :])
        dxn, gw = _rms_bwd(x_ref[...], nw_ref[...], dh)
        dx = dy_ref[...] + dxn
        dx_ref[...] = dx
        dxb_ref[...] = dx.astype(BF16)
        gn_ref[...] = gn_ref[...] + jnp.sum(gw, axis=0, keepdims=True)

    return _call(
        body, sides, name="ffn_bwd", grid=(S // TM,),
        in_specs=[_row(D), _row(D), _row(2 * FF), _row(D), _res((1, D)), _res((2 * FF, D)), _res((FF, D))],
        out_specs=[_row(2 * FF), _row(D), _row(D), pl.BlockSpec((1, D), lambda i: (0, 0))],
        out_shape=[jax.ShapeDtypeStruct((S, 2 * FF), BF16), jax.ShapeDtypeStruct((S, D), F32),
                   jax.ShapeDtypeStruct((S, D), BF16), jax.ShapeDtypeStruct((1, D), F32)],
        args=(dy, dyb, gu, x1, norm_w, w_ffn_in, w_ffn_out))


def in_bwd(d_q, d_k, d_v, d_conv, d_gl, w_in, x, d_x1, norm_w, sides=()):
    segs = ((OFF_Q, QKV), (OFF_K, QKV), (OFF_V, QKV), (OFF_CA, 2 * CC), (OFF_GA, 2 * D))

    def body(dq_ref, dk_ref, dv_ref, dc_ref, dg_ref, w_ref, x_ref, dx1_ref, nw_ref, gx_ref, gn_ref):
        @pl.when(pl.program_id(0) == 0)
        def _():
            gn_ref[...] = jnp.zeros_like(gn_ref)

        dh = jnp.zeros((TM, D), F32)
        for ref, (off, width) in zip((dq_ref, dk_ref, dv_ref, dc_ref, dg_ref), segs):
            for j in range(width // PLANE):
                dh = dh + _dot(ref[j], w_ref[off + j * PLANE:off + (j + 1) * PLANE, :])
        dxn, gw = _rms_bwd(x_ref[...], nw_ref[...], dh)
        gx_ref[...] = dx1_ref[...] + dxn
        gn_ref[...] = gn_ref[...] + jnp.sum(gw, axis=0, keepdims=True)

    return _call(
        body, sides, name="in_bwd", grid=(S // TM,),
        in_specs=[_planes(QKV)] * 3 + [_planes(2 * CC), _planes(2 * D), _res((INW, D)), _row(D), _row(D), _res((1, D))],
        out_specs=[_row(D), pl.BlockSpec((1, D), lambda i: (0, 0))],
        out_shape=[jax.ShapeDtypeStruct((S, D), F32), jax.ShapeDtypeStruct((1, D), F32)],
        args=(d_q, d_k, d_v, d_conv, d_gl, w_in, x, d_x1, norm_w))


def mm_tn(name, a, b, tm, tn, sides=()):
    M, N = a.shape[1], b.shape[1]

    def body(a_ref, b_ref, o_ref):
        o_ref[...] = _dot_tn(a_ref[...], b_ref[...])

    res = _call(
        body, sides, name=name, grid=(M // tm, N // tn),
        in_specs=[pl.BlockSpec((S, tm), lambda i, j: (0, i)), pl.BlockSpec((S, tn), lambda i, j: (0, j))],
        out_specs=[pl.BlockSpec((tm, tn), lambda i, j: (i, j))],
        out_shape=[jax.ShapeDtypeStruct((M, N), F32)],
        args=(a, b))
    return (res[0][0], res[1]) if sides else res[0]


GW_IN_TN = PLANE
GW_IN_PARTS = 2


def gw_in_t(name, ht, d_segs, col_half, sides=()):
    tn, hw = GW_IN_TN, D // GW_IN_PARTS
    starts, t0 = [], 0
    for seg in d_segs:
        starts.append(t0)
        t0 += seg.shape[0]
    ntiles = [seg.shape[0] for seg in d_segs]

    def body(h_ref, *refs):
        a_refs, o_ref = refs[:-1], refs[-1]
        n = pl.program_id(0)
        for a_ref, st, nt in zip(a_refs, starts, ntiles):
            @pl.when((n >= st) & (n < st + nt))
            def _(a_ref=a_ref):
                o_ref[...] = _dot(h_ref[...], a_ref[...]).T

    def seg_spec(st, nt):
        return pl.BlockSpec((None, S, tn), lambda n: (jnp.clip(n - st, 0, nt - 1), 0, 0))

    res = _call(
        body, sides, name=name, grid=(INW // tn,),
        in_specs=[pl.BlockSpec((hw, S), lambda n: (col_half, 0))] + [seg_spec(st, nt) for st, nt in zip(starts, ntiles)],
        out_specs=[pl.BlockSpec((tn, hw), lambda n: (n, 0))],
        out_shape=[jax.ShapeDtypeStruct((INW, hw), F32)],
        args=(ht, *d_segs))
    return (res[0][0], res[1]) if sides else res[0]


def _place():
    x, y, c = lax.axis_index("x"), lax.axis_index("y"), lax.axis_index("c")
    chips = [(1 - x, y), (x, 1 - y), (1 - x, 1 - y)]
    return x, y, c, chips


def _sems(n):
    return pltpu.SemaphoreType.DMA((n,))


def _remote(src, dst, send, recv, k, to):
    return pltpu.make_async_remote_copy(src_ref=src, dst_ref=dst, send_sem=send.at[k], recv_sem=recv.at[k],
                                        device_id=to, device_id_type=MESH)


def _cast_rows(dst, src, cols=slice(None)):
    rows = src.shape[0]
    step = next((s for s in (128, 64, 32, 16) if rows % s == 0), rows)
    for r0 in range(0, rows, step):
        dst[r0:r0 + step, cols] = src[r0:r0 + step, :].astype(dst.dtype)


def comm_only(name, sides):
    def body():
        pass

    return _call(body, sides, name=name, grid=(1,), in_specs=[], out_specs=[], out_shape=[], args=())[1]


def ag_blocks(shard, dtype):
    R, W = shard.shape

    def copy(outs, scr, k, block, to, src=None):
        dst = outs[0].at[block]
        return _remote(dst if src is None else src, dst, scr[1], scr[2], k, to)

    def local(outs, scr, me):
        return pltpu.make_async_copy(scr[0], outs[0].at[me], scr[3].at[0])

    def start(ins, outs, scr):
        x, y, c, chips = _place()
        me = 4 * x + 2 * y + c
        _cast_rows(scr[0], ins[0])
        local(outs, scr, me).start()
        copy(outs, scr, 0, me, (x, y, 1 - c), src=scr[0]).start()
        for j, (cx, cy) in enumerate(chips):
            copy(outs, scr, 1 + j, me, (cx, cy, c), src=scr[0]).start()

    def finish(ins, outs, scr):
        x, y, c, chips = _place()
        me, sib = 4 * x + 2 * y + c, (x, y, 1 - c)
        passed = []
        for j, (cx, cy) in enumerate(chips):
            theirs = 4 * cx + 2 * cy + c
            copy(outs, scr, 1 + j, theirs, (x, y, c)).wait_recv()
            fwd = copy(outs, scr, 4 + j, theirs, sib)
            fwd.start()
            passed.append(fwd)
        copy(outs, scr, 0, 4 * x + 2 * y + 1 - c, (x, y, c)).wait_recv()
        for j, (cx, cy) in enumerate(chips):
            copy(outs, scr, 4 + j, 4 * cx + 2 * cy + 1 - c, (x, y, c)).wait_recv()
        copy(outs, scr, 0, me, sib, src=scr[0]).wait_send()
        for j, (cx, cy) in enumerate(chips):
            copy(outs, scr, 1 + j, me, (cx, cy, c), src=scr[0]).wait_send()
        for fwd in passed:
            fwd.wait_send()
        local(outs, scr, me).wait()

    return Side((shard,), (VMEM,), (jax.ShapeDtypeStruct((NDEV, R, W), dtype),),
                (pltpu.VMEM((R, W), dtype), _sems(7), _sems(7), _sems(1)), start, finish)


def ag_blocks_relay(shard, dtype):
    R, W = shard.shape
    half = R // 2

    def copy(outs, scr, k, block, to, src=None, rows=None):
        dst = outs[0].at[block] if rows is None else outs[0].at[block, pl.ds(rows * half, half), :]
        return _remote(dst if src is None else src, dst, scr[1], scr[2], k, to)

    def local(outs, scr, me):
        return pltpu.make_async_copy(scr[0], outs[0].at[me], scr[3].at[0])

    def own(outs, scr):
        x, y, c, _ = _place()
        me = 4 * x + 2 * y + c
        return [copy(outs, scr, k, me, to, src=scr[0])
                for k, to in enumerate([(x, y, 1 - c), (1 - x, y, c), (x, 1 - y, c)])]

    def start(ins, outs, scr):
        x, y, c, _ = _place()
        _cast_rows(scr[0], ins[0])
        local(outs, scr, 4 * x + 2 * y + c).start()
        for cp in own(outs, scr):
            cp.start()

    def passed_on(outs, scr):
        x, y, c, _ = _place()
        sib, xn, yn = (x, y, 1 - c), (1 - x, y, c), (x, 1 - y, c)
        b_xn, b_yn, b_dg = 4 * (1 - x) + 2 * y + c, 4 * x + 2 * (1 - y) + c, 4 * (1 - x) + 2 * (1 - y) + c
        near = [copy(outs, scr, 5, b_xn, yn, rows=0), copy(outs, scr, 3, b_xn, sib),
                copy(outs, scr, 6, b_yn, xn, rows=1), copy(outs, scr, 4, b_yn, sib)]
        far = [copy(outs, scr, 7, b_dg, sib, rows=0), copy(outs, scr, 8, b_dg, sib, rows=1)]
        return (b_xn, b_yn, b_dg), near, far

    def mid(ins, outs, scr):
        x, y, c, _ = _place()
        (b_xn, b_yn, _), near, _ = passed_on(outs, scr)
        copy(outs, scr, 1, b_xn, (x, y, c)).wait_recv()
        near[0].start()
        near[1].start()
        copy(outs, scr, 2, b_yn, (x, y, c)).wait_recv()
        near[2].start()
        near[3].start()

    def finish(ins, outs, scr):
        x, y, c, _ = _place()
        here = (x, y, c)
        (b_xn, b_yn, b_dg), near, far = passed_on(outs, scr)
        copy(outs, scr, 5, b_dg, here, rows=0).wait_recv()
        far[0].start()
        copy(outs, scr, 6, b_dg, here, rows=1).wait_recv()
        far[1].start()
        flip = 1 - 2 * c
        copy(outs, scr, 0, 4 * x + 2 * y + 1 - c, here).wait_recv()
        copy(outs, scr, 3, b_xn + flip, here).wait_recv()
        copy(outs, scr, 4, b_yn + flip, here).wait_recv()
        copy(outs, scr, 7, b_dg + flip, here, rows=0).wait_recv()
        copy(outs, scr, 8, b_dg + flip, here, rows=1).wait_recv()
        for cp in own(outs, scr) + near + far:
            cp.wait_send()
        local(outs, scr, 4 * x + 2 * y + c).wait()

    return Side((shard,), (VMEM,), (jax.ShapeDtypeStruct((NDEV, R, W), dtype),),
                (pltpu.VMEM((R, W), dtype), _sems(9), _sems(9), _sems(1)), start, finish, mid)


def ag_cols(shard):
    K, C = shard.shape
    half, w2 = K // 2, 2 * C

    def win(out, rows_c, chip):
        return out.at[pl.ds(pl.multiple_of(rows_c * half, 16), half), pl.ds(pl.multiple_of(chip * w2, LANES), w2)]

    def ici(outs, scr, j, to, c, k):
        slab, send, recv = scr[2], scr[5], scr[6]
        return _remote(slab.at[pl.ds(pl.multiple_of(c * half, 16), half), :], win(outs[0], c, k), send, recv, j, to)

    def local(outs, scr, k):
        return pltpu.make_async_copy(scr[2], outs[0].at[:, pl.ds(pl.multiple_of(k * w2, LANES), w2)], scr[7].at[0])

    def start(ins, outs, scr):
        stage, inbox, slab, xs, xr = scr[:5]
        x, y, c, chips = _place()
        k = 2 * x + y
        _cast_rows(stage, ins[0])
        swap = _remote(stage, inbox, xs, xr, 0, (x, y, 1 - c))
        swap.start()
        for cc in range(2):
            @pl.when(c == cc)
            def _(cc=cc):
                _cast_rows(slab, stage, slice(cc * C, (cc + 1) * C))
        swap.wait()
        for cc in range(2):
            @pl.when(c == cc)
            def _(cc=cc):
                _cast_rows(slab, inbox, slice((1 - cc) * C, (2 - cc) * C))
        local(outs, scr, k).start()
        for j, (cx, cy) in enumerate(chips):
            ici(outs, scr, j, (cx, cy, c), c, k).start()

    def finish(ins, outs, scr):
        send, recv = scr[5], scr[6]
        x, y, c, chips = _place()
        k, sib = 2 * x + y, (x, y, 1 - c)
        passed = []
        for j, (cx, cy) in enumerate(chips):
            w = win(outs[0], c, 2 * cx + cy)
            _remote(w, w, send, recv, j, sib).wait_recv()
            fwd = _remote(w, w, send, recv, 3 + j, sib)
            fwd.start()
            passed.append(fwd)
        for j, (cx, cy) in enumerate(chips):
            w = win(outs[0], 1 - c, 2 * cx + cy)
            _remote(w, w, send, recv, 3 + j, sib).wait_recv()
        for j, (cx, cy) in enumerate(chips):
            ici(outs, scr, j, (cx, cy, c), c, k).wait_send()
        for fwd in passed:
            fwd.wait_send()
        local(outs, scr, k).wait()

    return Side((shard,), (VMEM,), (jax.ShapeDtypeStruct((K, NDEV * C), BF16),),
                (pltpu.VMEM((K, C), BF16), pltpu.VMEM((K, C), BF16), pltpu.VMEM((K, w2), BF16),
                 _sems(1), _sems(1), _sems(6), _sems(6), _sems(1)), start, finish)


def copies_side(args, out_shape, n_copies, plan):
    def copies(ins, outs, scr):
        return [_remote(s_, d_, scr[0], scr[1], i, to) for i, (s_, d_, to) in enumerate(plan(ins, outs))]

    def start(ins, outs, scr):
        for cp in copies(ins, outs, scr):
            cp.start()

    def finish(ins, outs, scr):
        for cp in copies(ins, outs, scr):
            cp.wait()

    return Side(tuple(args), (ANY,) * len(args), tuple(out_shape), (_sems(n_copies), _sems(n_copies)), start, finish)


def rs_to_sibling(grads):
    out_shape = [jax.ShapeDtypeStruct((4,) + g.shape[1:] if kind == "rows" else (g.shape[0] // 2, g.shape[1]), F32)
                 for kind, g in grads]

    def plan(ins, outs):
        x, y, c, _ = _place()
        sib, res = (x, y, 1 - c), []
        for (kind, _), g, r in zip(grads, ins, outs):
            if kind == "rows":
                res += [(g.at[2 * k + 1 - c], r.at[k], sib) for k in range(4)]
            else:
                half = g.shape[0] // 2
                res.append((g.at[pl.ds(pl.multiple_of((1 - c) * half, 8), half), :], r, sib))
        return res

    return copies_side([g for _, g in grads], out_shape, sum(4 if kind == "rows" else 1 for kind, _ in grads), plan)


def rs_to_chips(parts):
    out_shape = [jax.ShapeDtypeStruct((3,) + p.shape[1:] if kind == "rows" else (3, p.shape[0], p.shape[1] // 4), BF16)
                 for kind, p in parts]

    def plan(ins, outs):
        x, y, c, chips = _place()
        res = []
        for (kind, _), p, r in zip(parts, ins, outs):
            for j, (cx, cy) in enumerate(chips):
                if kind == "rows":
                    src = p.at[2 * cx + cy]
                else:
                    w2 = p.shape[1] // 4
                    src = p.at[:, pl.ds(pl.multiple_of((2 * cx + cy) * w2, LANES), w2)]
                res.append((src, r.at[j], (cx, cy, c)))
        return res

    return copies_side([p for _, p in parts], out_shape, 3 * len(parts), plan)


def rs_swap_halves(theirs):
    def plan(ins, outs):
        x, y, c, _ = _place()
        return [(t, r, (x, y, 1 - c)) for t, r in zip(ins, outs)]

    return copies_side(theirs, [jax.ShapeDtypeStruct(t.shape, F32) for t in theirs], len(theirs), plan)


def _row_tiles(rows):
    return 2 if rows % 32 == 0 and rows >= 512 else 1


def chip_sum(name, grad, recv, c_idx, chip_idx):
    _, R, C = grad.shape
    nt = 1
    tr = R // nt

    def body(s_ref, g_ref, r_ref, p_ref, own_ref):
        k = pl.program_id(1)
        tot = g_ref[0] + r_ref[0]
        p_ref[0] = tot.astype(BF16)

        @pl.when(k == s_ref[1])
        def _():
            own_ref[...] = tot

    grid_spec = pltpu.PrefetchScalarGridSpec(
        num_scalar_prefetch=1, grid=(nt, 4),
        in_specs=[pl.BlockSpec((1, tr, C), lambda i, k, s: (2 * k + s[0], i, 0)),
                  pl.BlockSpec((1, tr, C), lambda i, k, s: (k, i, 0))],
        out_specs=[pl.BlockSpec((1, tr, C), lambda i, k, s: (k, i, 0)),
                   pl.BlockSpec((tr, C), lambda i, k, s: (i, 0))])
    return pl.pallas_call(
        body, name=name, grid_spec=grid_spec,
        out_shape=[jax.ShapeDtypeStruct((4, R, C), BF16), jax.ShapeDtypeStruct((R, C), F32)],
        compiler_params=_cp(dimension_semantics=("arbitrary", "arbitrary")),
    )(jnp.stack([c_idx, chip_idx]), grad, recv)


def _half_tiles(half):
    return 2 if half >= 512 else 1


def chip_sum_cols(name, grad, recv, c_idx, chip_idx):
    K, W = grad.shape
    half, w2 = K // 2, W // 4
    nt = _half_tiles(half)
    tr = half // nt

    def body(s_ref, g_ref, r_ref, p_ref, own_ref):
        tot = g_ref[...] + r_ref[...]
        p_ref[...] = tot.astype(BF16)

        @pl.when(pl.program_id(1) == s_ref[1])
        def _():
            own_ref[...] = tot

    grid_spec = pltpu.PrefetchScalarGridSpec(
        num_scalar_prefetch=1, grid=(nt, 4),
        in_specs=[pl.BlockSpec((tr, w2), lambda i, k, s: (s[0] * nt + i, k)),
                  pl.BlockSpec((tr, w2), lambda i, k, s: (i, k))],
        out_specs=[pl.BlockSpec((tr, w2), lambda i, k, s: (i, k)),
                   pl.BlockSpec((tr, w2), lambda i, k, s: (i, 0))])
    return pl.pallas_call(
        body, name=name, grid_spec=grid_spec,
        out_shape=[jax.ShapeDtypeStruct((half, W), BF16), jax.ShapeDtypeStruct((half, w2), F32)],
        compiler_params=_cp(dimension_semantics=("arbitrary", "arbitrary")),
    )(jnp.stack([c_idx, chip_idx]), grad, recv)


def col_final(name, own, recv, c_idx):
    half, w2 = own.shape
    C = w2 // 2
    nt = _half_tiles(half)
    tr = half // nt

    def body(s_ref, o_ref, r_ref, mine_ref, theirs_ref, t_ref):
        t_ref[...] = o_ref[...] + r_ref[0].astype(F32) + r_ref[1].astype(F32) + r_ref[2].astype(F32)
        for cc in range(2):
            @pl.when(s_ref[0] == cc)
            def _(cc=cc):
                mine_ref[...] = t_ref[:, cc * C:(cc + 1) * C]
                theirs_ref[...] = t_ref[:, (1 - cc) * C:(2 - cc) * C]

    grid_spec = pltpu.PrefetchScalarGridSpec(
        num_scalar_prefetch=1, grid=(nt,),
        in_specs=[pl.BlockSpec((tr, w2), lambda i, s: (i, 0)), pl.BlockSpec((3, tr, w2), lambda i, s: (0, i, 0))],
        out_specs=[pl.BlockSpec((tr, C), lambda i, s: (i, 0))] * 2,
        scratch_shapes=[pltpu.VMEM((tr, w2), F32)])
    return pl.pallas_call(
        body, name=name, grid_spec=grid_spec, out_shape=[jax.ShapeDtypeStruct((half, C), F32)] * 2,
        compiler_params=_cp(dimension_semantics=("arbitrary",)),
    )(jnp.stack([c_idx]), own, recv)


def _adamw(w, g, m, v):
    m2 = ADAM_B1 * m + (1.0 - ADAM_B1) * g
    v2 = ADAM_B2 * v + (1.0 - ADAM_B2) * (g * g)
    m_hat = m2 / (1.0 - ADAM_B1 ** ADAM_STEP)
    v_hat = v2 / (1.0 - ADAM_B2 ** ADAM_STEP)
    delta = -ADAM_LR * (m_hat / (jnp.sqrt(v_hat) + ADAM_EPS) + ADAM_WD * w)
    return delta, m2, v2


def shard_adam(name, owns, recvs, w, m, v):
    n = len(owns)
    R, Cp = owns[0].shape
    nt = _row_tiles(R)
    tr = R // nt

    def body(*refs):
        o_refs, r_refs = refs[:n], refs[n:2 * n]
        w_ref, m_ref, v_ref, g_ref, d_ref, nm_ref, nv_ref = refs[2 * n:]
        g = None
        for k in range(n):
            gk = o_refs[k][...] + r_refs[k][0].astype(F32) + r_refs[k][1].astype(F32) + r_refs[k][2].astype(F32)
            g = gk if g is None else jnp.where(pl.program_id(0) == k, gk, g)
        delta, m2, v2 = _adamw(w_ref[...], g, m_ref[...], v_ref[...])
        g_ref[...] = g
        d_ref[...] = delta
        nm_ref[...] = m2
        nv_ref[...] = v2

    part = pl.BlockSpec((tr, Cp), lambda k, i: (i, 0))
    part3 = pl.BlockSpec((3, tr, Cp), lambda k, i: (0, i, 0))
    tile = pl.BlockSpec((tr, Cp), lambda k, i: (i, k))
    return pl.pallas_call(
        body, name=name, grid=(n, nt),
        in_specs=[part] * n + [part3] * n + [tile, tile, tile],
        out_specs=[tile] * 4, out_shape=[jax.ShapeDtypeStruct((R, n * Cp), F32)] * 4,
        compiler_params=_cp(dimension_semantics=("arbitrary", "arbitrary")),
    )(*owns, *recvs, w, m, v)


def adam_cols(name, mine, recv, w, m, v, c_idx):
    half, C = mine.shape
    nt = _half_tiles(half)
    tr = half // nt

    def body(s_ref, a_ref, b_ref, w_ref, m_ref, v_ref, g_ref, d_ref, nm_ref, nv_ref):
        g = jnp.where(pl.program_id(0) == s_ref[0], a_ref[...], b_ref[...])
        delta, m2, v2 = _adamw(w_ref[...], g, m_ref[...], v_ref[...])
        g_ref[...] = g
        d_ref[...] = delta
        nm_ref[...] = m2
        nv_ref[...] = v2

    part = pl.BlockSpec((tr, C), lambda hh, i, s: (i, 0))
    tile = pl.BlockSpec((tr, C), lambda hh, i, s: (hh * nt + i, 0))
    grid_spec = pltpu.PrefetchScalarGridSpec(
        num_scalar_prefetch=1, grid=(2, nt), in_specs=[part, part, tile, tile, tile], out_specs=[tile] * 4)
    return pl.pallas_call(
        body, name=name, grid_spec=grid_spec, out_shape=[jax.ShapeDtypeStruct((2 * half, C), F32)] * 4,
        compiler_params=_cp(dimension_semantics=("arbitrary", "arbitrary")),
    )(jnp.stack([c_idx]), mine, recv, w, m, v)


ROW_N1, ROW_N2, ROW_BG, ROW_QN, ROW_KN, ROW_CB, ROW_LW, ROW_LB, ROW_CW = 0, 1, 2, 4, 5, 6, 7, 8, 9
PACK_ROWS = 40
SMALL = ("norm1_w", "norm2_w", "b_gate", "q_norm_w", "k_norm_w", "conv_b", "conv_ln_w", "conv_ln_b", "conv_w")


def small_sync_adam(g, w, m, v, sq, sides=()):
    ns = len(SMALL)

    def body(*refs):
        gi = dict(zip(SMALL, refs[:ns]))
        wi = dict(zip(SMALL, refs[ns:2 * ns]))
        mi = dict(zip(SMALL, refs[2 * ns:3 * ns]))
        vi = dict(zip(SMALL, refs[3 * ns:4 * ns]))
        sq_ref = refs[4 * ns]
        outs = refs[4 * ns + 1:8 * ns + 1]
        loss_ref = refs[8 * ns + 1]
        pack, recv, tot, send_sems, recv_sems = refs[8 * ns + 2:]
        x, y, c, _ = _place()
        me = 4 * x + 2 * y + c

        pack[...] = jnp.zeros_like(pack)
        pack[ROW_KN:ROW_KN + 1, LANES:2 * LANES] = jnp.full((1, LANES), (0.5 / D) * jnp.sum(sq_ref[...]), F32)
        pack[ROW_N1:ROW_N1 + 1, :] = gi["norm1_w"][...]
        pack[ROW_N2:ROW_N2 + 1, :] = gi["norm2_w"][...]
        pack[ROW_BG:ROW_BG + 2, :] = gi["b_gate"][...]
        pack[ROW_QN:ROW_QN + 1, 0:HD] = gi["q_norm_w"][...]
        pack[ROW_KN:ROW_KN + 1, 0:HD] = gi["k_norm_w"][...]
        pack[ROW_CB:ROW_CB + 1, 0:CC] = gi["conv_b"][...]
        pack[ROW_LW:ROW_LW + 1, 0:CC] = gi["conv_ln_w"][...]
        pack[ROW_LB:ROW_LB + 1, 0:CC] = gi["conv_ln_b"][...]
        pack[ROW_CW:ROW_CW + KW, 0:CC] = gi["conv_w"][...]

        copies = []
        for k in range(1, NDEV):
            peer = (x ^ (k >> 2), y ^ ((k >> 1) & 1), c ^ (k & 1))
            cp = pltpu.make_async_remote_copy(
                src_ref=pack, dst_ref=recv.at[me], send_sem=send_sems.at[k - 1], recv_sem=recv_sems.at[k - 1],
                device_id=peer, device_id_type=MESH)
            cp.start()
            copies.append(cp)
        recv[me] = pack[...]
        for cp in copies:
            cp.wait()
        acc = recv[0]
        for p in range(1, NDEV):
            acc = acc + recv[p]
        tot[...] = acc

        def shard_grad(name):
            if name == "b_gate":
                return tot[ROW_BG:ROW_BG + 2, pl.ds(pl.multiple_of(me * LANES, LANES), LANES)]
            if name == "conv_w":
                win = tot[ROW_CW:ROW_CW + KW, pl.ds(pl.multiple_of((me // 2) * LANES, LANES), LANES)]
                return jnp.where(me % 2 == 1, win[:, HD:LANES], win[:, 0:HD])
            row = {"norm1_w": ROW_N1, "norm2_w": ROW_N2, "q_norm_w": ROW_QN, "k_norm_w": ROW_KN,
                   "conv_b": ROW_CB, "conv_ln_w": ROW_LW, "conv_ln_b": ROW_LB}[name]
            return tot[row:row + 1, 0:wi[name].shape[1]]

        for i, name in enumerate(SMALL):
            gr = shard_grad(name)
            delta, m2, v2 = _adamw(wi[name][...], gr, mi[name][...], vi[name][...])
            outs[4 * i][...] = gr
            outs[4 * i + 1][...] = delta
            outs[4 * i + 2][...] = m2
            outs[4 * i + 3][...] = v2
        loss_ref[...] = tot[ROW_KN:ROW_KN + 1, LANES:2 * LANES]

    out_shape = []
    for name in SMALL:
        out_shape += [jax.ShapeDtypeStruct(w[name].shape, F32)] * 4
    out_shape.append(jax.ShapeDtypeStruct((1, LANES), F32))
    args = [g[k] for k in SMALL] + [w[k] for k in SMALL] + [m[k] for k in SMALL] + [v[k] for k in SMALL] + [sq]
    res = _call(
        body, sides, name="small_sync_adam", grid=(1,), in_specs=[VMEM] * len(args),
        out_specs=[VMEM] * len(out_shape), out_shape=out_shape,
        scratch_shapes=[pltpu.VMEM((PACK_ROWS, D), F32), pltpu.VMEM((NDEV, PACK_ROWS, D), F32),
                        pltpu.VMEM((PACK_ROWS, D), F32), _sems(NDEV - 1), _sems(NDEV - 1)],
        args=args)
    res, side_outs = res if sides else (res, None)
    out = {name: tuple(res[4 * i:4 * i + 4]) for i, name in enumerate(SMALL)}
    loss = res[4 * ns][0, 0]
    return (out, loss, side_outs) if sides else (out, loss)


MATS = ("w_in", "w_o_attn", "w_pw_conv", "w_out", "w_ffn_in", "w_ffn_out")
TRANSPOSED = ("w_in", "w_ffn_in")
WEIGHTS = ("norm1_w", "w_in", "b_gate", "q_norm_w", "k_norm_w", "w_o_attn", "conv_w", "conv_b", "conv_ln_w",
           "conv_ln_b", "w_pw_conv", "w_out", "norm2_w", "w_ffn_in", "w_ffn_out")


def _blocks_to_cols(blocks):
    n, R, C = blocks.shape
    return blocks.transpose(1, 0, 2).reshape(R, n * C)


def kernel(x, positions, norm1_w, w_in, b_gate, q_norm_w, k_norm_w, w_o_attn, conv_w, conv_b, conv_ln_w, conv_ln_b, w_pw_conv, w_out, norm2_w, w_ffn_in, w_ffn_out, loss_target, m_norm1_w, m_w_in, m_b_gate, m_q_norm_w, m_k_norm_w, m_w_o_attn, m_conv_w, m_conv_b, m_conv_ln_w, m_conv_ln_b, m_w_pw_conv, m_w_out, m_norm2_w, m_w_ffn_in, m_w_ffn_out, v_norm1_w, v_w_in, v_b_gate, v_q_norm_w, v_k_norm_w, v_w_o_attn, v_conv_w, v_conv_b, v_conv_ln_w, v_conv_ln_b, v_w_pw_conv, v_w_out, v_norm2_w, v_w_ffn_in, v_w_ffn_out):
    w = dict(norm1_w=norm1_w, w_in=w_in, b_gate=b_gate, q_norm_w=q_norm_w, k_norm_w=k_norm_w, w_o_attn=w_o_attn,
             conv_w=conv_w, conv_b=conv_b, conv_ln_w=conv_ln_w, conv_ln_b=conv_ln_b, w_pw_conv=w_pw_conv,
             w_out=w_out, norm2_w=norm2_w, w_ffn_in=w_ffn_in, w_ffn_out=w_ffn_out)
    m = dict(norm1_w=m_norm1_w, w_in=m_w_in, b_gate=m_b_gate, q_norm_w=m_q_norm_w, k_norm_w=m_k_norm_w,
             w_o_attn=m_w_o_attn, conv_w=m_conv_w, conv_b=m_conv_b, conv_ln_w=m_conv_ln_w,
             conv_ln_b=m_conv_ln_b, w_pw_conv=m_w_pw_conv, w_out=m_w_out, norm2_w=m_norm2_w,
             w_ffn_in=m_w_ffn_in, w_ffn_out=m_w_ffn_out)
    v = dict(norm1_w=v_norm1_w, w_in=v_w_in, b_gate=v_b_gate, q_norm_w=v_q_norm_w, k_norm_w=v_k_norm_w,
             w_o_attn=v_w_o_attn, conv_w=v_conv_w, conv_b=v_conv_b, conv_ln_w=v_conv_ln_w,
             conv_ln_b=v_conv_ln_b, w_pw_conv=v_w_pw_conv, w_out=v_w_out, norm2_w=v_norm2_w,
             w_ffn_in=v_w_ffn_in, w_ffn_out=v_w_ffn_out)
    def two_d(t):
        t = {k: (a[0] if a.ndim == 3 else a) for k, a in t.items()}
        return {k: (a.T if k in TRANSPOSED else a) for k, a in t.items()}

    w, m, v = two_d(w), two_d(m), two_d(v)

    x2, target = x[0], loss_target[0]
    c_idx = lax.axis_index("c").astype(jnp.int32)
    chip_idx = (2 * lax.axis_index("x") + lax.axis_index("y")).astype(jnp.int32)
    qw2 = jnp.tile(w["q_norm_w"], (1, 2))
    kw2 = jnp.tile(w["k_norm_w"], (1, 2))

    tabs, ((w_in_blocks,), (bg_blocks,), (cw_blocks,)) = rope_tables(
        positions.reshape(S, 1),
        sides=(ag_blocks_relay(w["w_in"], BF16), ag_blocks(w["b_gate"], F32), ag_blocks(w["conv_w"], F32)))
    w_in_t = w_in_blocks.reshape(INW, D)
    b_gate_f, conv_w_f = _blocks_to_cols(bg_blocks), _blocks_to_cols(cw_blocks)
    (h_t, proj), ((w_o_f,), (w_pw_f,), (w_out_blocks,)) = in_proj(
        x2, w["norm1_w"], w_in_t, sides=(ag_cols(w["w_o_attn"]), ag_cols(w["w_pw_conv"]), ag_blocks_relay(w["w_out"], BF16)))
    w_out_f = w_out_blocks.reshape(D, D)
    (attn, lse), ((w_ffn_in_blocks,),) = attn_fwd(proj, tabs, qw2, kw2, sides=(ag_blocks_relay(w["w_ffn_in"], BF16),))
    w_ffn_in_t = w_ffn_in_blocks.reshape(2 * FF, D)
    cpre, u3 = conv_fwd(proj, conv_w_f, w["conv_b"], w["conv_ln_w"], w["conv_ln_b"])
    x1, z, ya, yb = mix_out(x2, proj, b_gate_f, attn, u3, w_o_f, w_pw_f, w_out_f)
    (h2, gu, f), ((w_ffn_out_blocks,),) = ffn_in(x1, w["norm2_w"], w_ffn_in_t, sides=(ag_blocks_relay(w["w_ffn_out"], BF16),))
    w_ffn_out_f = w_ffn_out_blocks.reshape(FF, D)
    dy, dyb, sq = ffn_out_loss(x1, f, w_ffn_out_f, target)

    g = {}
    g_ffn_out = mm_tn("gw_ffn_out", f, dyb, FF // 2, D).reshape(NDEV, FF // NDEV, D)
    (d_gu, d_x1, d_x1b, g["norm2_w"]), ((ra_ffn_out,),) = ffn_bwd(
        dy, dyb, gu, x1, w["norm2_w"], w_ffn_in_t, w_ffn_out_f, sides=(rs_to_sibling([("rows", g_ffn_out)]),))
    pb_ffn_out, own_ffn_out = chip_sum("chip_sum_w_ffn_out", g_ffn_out, ra_ffn_out, c_idx, chip_idx)
    g_ffn_in, ((rb_ffn_out,),) = mm_tn("gw_ffn_in", d_gu, h2, FF // 2, D,
                                       sides=(rs_to_chips([("rows", pb_ffn_out)]),))
    g_ffn_in = g_ffn_in.reshape(NDEV, 2 * FF // NDEV, D)
    g_out = mm_tn("gw_out", z, d_x1b, D // 2, D).reshape(NDEV, D // NDEV, D)
    (d_ya, d_yb, d_gl, d_attn, d_u3, g["b_gate"]), ((ra_ffn_in,),) = out_bwd(
        d_x1b, proj, b_gate_f, ya, yb, w_o_f, w_pw_f, w_out_f, sides=(rs_to_sibling([("rows", g_ffn_in)]),))
    pb_ffn_in, own_ffn_in = chip_sum("chip_sum_w_ffn_in", g_ffn_in, ra_ffn_in, c_idx, chip_idx)
    g_w_o = mm_tn("gw_o_attn", attn, d_ya, CC, D)
    g_w_pw = mm_tn("gw_pw_conv", u3, d_yb, CC, D)
    (d_conv, g["conv_w"], g["conv_b"], g["conv_ln_w"], g["conv_ln_b"]), ((ra_out, ra_w_o, ra_w_pw),) = conv_bwd(
        proj, cpre, d_u3, conv_w_f, conv_w_f[::-1], w["conv_ln_w"], w["conv_ln_b"],
        sides=(rs_to_sibling([("rows", g_out), ("cols", g_w_o), ("cols", g_w_pw)]),))
    pb_out, own_out = chip_sum("chip_sum_w_out", g_out, ra_out, c_idx, chip_idx)
    pb_w_o, own_w_o = chip_sum_cols("chip_sum_w_o_attn", g_w_o, ra_w_o, c_idx, chip_idx)
    pb_w_pw, own_w_pw = chip_sum_cols("chip_sum_w_pw_conv", g_w_pw, ra_w_pw, c_idx, chip_idx)
    (d_q, d_k, d_v, gqw, gkw), ((rb_ffn_in, rb_out, rb_w_o, rb_w_pw),) = attn_bwd(
        proj, tabs, qw2, kw2, d_attn, attn, lse,
        sides=(rs_to_chips([("rows", pb_ffn_in), ("rows", pb_out), ("cols", pb_w_o), ("cols", pb_w_pw)]),))
    g["q_norm_w"] = gqw[0:1, 0:HD] + gqw[0:1, HD:LANES]
    g["k_norm_w"] = gkw[0:1, 0:HD] + gkw[0:1, HD:LANES]
    mine_w_o, theirs_w_o = col_final("col_final_w_o_attn", own_w_o, rb_w_o, c_idx)
    mine_w_pw, theirs_w_pw = col_final("col_final_w_pw_conv", own_w_pw, rb_w_pw, c_idx)
    d_segs = (d_q, d_k, d_v, d_conv, d_gl)
    parts, to_sibling, to_chips, owns, from_chips = [], None, None, [], []
    for k in range(GW_IN_PARTS):
        sides = [rs_swap_halves([theirs_w_o, theirs_w_pw])] if k == 0 else []
        sides += [s for s in (to_chips, to_sibling) if s is not None]
        part, outs = gw_in_t("gw_in_%d" % k, h_t, d_segs, k, sides=tuple(sides))
        if k == 0:
            (rc_w_o, rc_w_pw), outs = outs[0], outs[1:]
        outs = list(outs)
        if to_chips is not None:
            from_chips.append(outs.pop(0)[0])
        if to_sibling is not None:
            pb, own = chip_sum("chip_sum_w_in_%d" % (k - 1), parts[-1], outs.pop(0)[0], c_idx, chip_idx)
            owns.append(own)
            to_chips = rs_to_chips([("rows", pb)])
        else:
            to_chips = None
        parts.append(part.reshape(NDEV, INW // NDEV, D // GW_IN_PARTS))
        to_sibling = rs_to_sibling([("rows", parts[-1])])
    (grad_x, g["norm1_w"]), ((rb_prev,), (ra_last,)) = in_bwd(
        d_q, d_k, d_v, d_conv, d_gl, w_in_t, x2, d_x1, w["norm1_w"], sides=(to_chips, to_sibling))
    from_chips.append(rb_prev)
    pb, own = chip_sum("chip_sum_w_in_%d" % (GW_IN_PARTS - 1), parts[-1], ra_last, c_idx, chip_idx)
    owns.append(own)
    small, loss, ((rb_last,),) = small_sync_adam(g, w, m, v, sq, sides=(rs_to_chips([("rows", pb)]),))
    from_chips.append(rb_last)

    res = {
        "w_in": shard_adam("adam_w_in", owns, from_chips, w["w_in"], m["w_in"], v["w_in"]),
        "w_ffn_in": shard_adam("adam_w_ffn_in", [own_ffn_in], [rb_ffn_in], w["w_ffn_in"], m["w_ffn_in"], v["w_ffn_in"]),
        "w_o_attn": adam_cols("adam_w_o_attn", mine_w_o, rc_w_o, w["w_o_attn"], m["w_o_attn"], v["w_o_attn"], c_idx),
        "w_pw_conv": adam_cols("adam_w_pw_conv", mine_w_pw, rc_w_pw, w["w_pw_conv"], m["w_pw_conv"], v["w_pw_conv"], c_idx),
        "w_out": shard_adam("adam_w_out", [own_out], [rb_out], w["w_out"], m["w_out"], v["w_out"]),
        "w_ffn_out": shard_adam("adam_w_ffn_out", [own_ffn_out], [rb_ffn_out],
                                w["w_ffn_out"], m["w_ffn_out"], v["w_ffn_out"]),
    }
    res = {k: tuple(a.T if k in TRANSPOSED else a for a in r) for k, r in res.items()}
    res.update(small)

    def shaped(name, a):
        return a.reshape((1,) + a.shape) if name in MATS or name in ("b_gate", "conv_w") else a

    outs = [loss, grad_x.reshape(1, S, D)]
    for i in range(4):
        outs += [shaped(k, res[k][i]) for k in WEIGHTS]
    return tuple(outs)
```

```python
import functools
from typing import Callable, NamedTuple, Optional

import numpy as np
import jax
import jax.numpy as jnp
from jax import lax
from jax.experimental import pallas as pl
from jax.experimental.pallas import tpu as pltpu

F32 = jnp.float32
BF16 = jnp.bfloat16

S = 2048
D = 1024
HD = 64
QKV = 1536
CC = 512
KW = 31
FF = 2816
INW = 7680
OFF_Q, OFF_K, OFF_V, OFF_CA, OFF_CB, OFF_GA, OFF_GB = 0, 1536, 3072, 4608, 5120, 5632, 6656
DILATIONS = (1, 4, 16)
HALF_SPAN = 64
EPS = 1e-6
NEG_INF = -1e30
ROPE_THETA = 500000.0
ROT_DIM = 16

ADAM_LR = 0.001
ADAM_B1 = 0.9
ADAM_B2 = 0.999
ADAM_EPS = 1e-08
ADAM_WD = 0.01
ADAM_STEP = 10

NDEV = 8
LANES = 128
TM = 256
TQ = 128
VMEM_LIMIT = 56 * 1024 * 1024
MESH = pl.DeviceIdType.MESH


def _cp(**kw):
    return pltpu.CompilerParams(vmem_limit_bytes=VMEM_LIMIT, **kw)


def _row(width, col=0, tm=TM):
    return pl.BlockSpec((tm, width), lambda i: (i, col))


PLANE = 512


def _planes(width, tm=TM):
    return pl.BlockSpec((width // PLANE, tm, PLANE), lambda i: (0, i, 0))


def _res(shape):
    nd = len(shape)
    return pl.BlockSpec(shape, lambda *_: (0,) * nd, pipeline_mode=pl.Buffered(1))


def _dot(a, b):
    return jnp.dot(a, b, preferred_element_type=F32)


def _dot_nt(a, b):
    return lax.dot_general(a, b, (((1,), (1,)), ((), ())), preferred_element_type=F32)


def _dot_tn(a, b):
    return lax.dot_general(a, b, (((0,), (0,)), ((), ())), preferred_element_type=F32)


def _sigmoid(x):
    return jax.nn.sigmoid(x)


def _dsilu(x, sg):
    return sg * (1.0 + x * (1.0 - sg))


ANY = pl.BlockSpec(memory_space=pl.ANY)
VMEM = pl.BlockSpec(memory_space=pltpu.VMEM)


class Side(NamedTuple):
    args: tuple
    in_specs: tuple
    out_shape: tuple
    scratch: tuple
    start: Callable
    finish: Callable
    mid: Optional[Callable] = None
    peers: str = ""


BARRIER_IDS = {"s": 0, "dxy": 1, "dsxy": 2, "sxy": 3}


def _peer_barrier(peers):
    x, y, c = lax.axis_index("x"), lax.axis_index("y"), lax.axis_index("c")
    where = {"s": (x, y, 1 - c), "x": (1 - x, y, c), "y": (x, 1 - y, c), "d": (1 - x, 1 - y, c)}
    barrier = pltpu.get_barrier_semaphore()
    for p in peers:
        pl.semaphore_signal(barrier, inc=1, device_id=where[p], device_id_type=MESH)
    pl.semaphore_wait(barrier, len(peers))


def _call(body, sides=(), *, name, grid, in_specs, out_specs, out_shape, scratch_shapes=(), args, own_comm=False):
    ni, no, ns = len(in_specs), len(out_specs), len(scratch_shapes)
    cnt = [(len(s.args), len(s.out_shape), len(s.scratch)) for s in sides]
    peers = "".join(sorted(set("".join(s.peers for s in sides))))
    if own_comm or not sides or any(not s.peers for s in sides):
        peers = ""

    def take(refs, pos, n):
        return refs[pos:pos + n], pos + n

    def full(*refs):
        m_in, pos = take(refs, 0, ni)
        s_in = []
        for a, _, _ in cnt:
            r, pos = take(refs, pos, a)
            s_in.append(r)
        m_out, pos = take(refs, pos, no)
        s_out = []
        for _, o, _ in cnt:
            r, pos = take(refs, pos, o)
            s_out.append(r)
        m_scr, pos = take(refs, pos, ns)
        s_scr = []
        for _, _, c in cnt:
            r, pos = take(refs, pos, c)
            s_scr.append(r)
        if sides:
            first = functools.reduce(jnp.logical_and, [pl.program_id(d) == 0 for d in range(len(grid))])
            last = functools.reduce(jnp.logical_and, [pl.program_id(d) == g - 1 for d, g in enumerate(grid)])

            @pl.when(first)
            def _():
                if peers:
                    _peer_barrier(peers)
                for s, a, o, c in zip(sides, s_in, s_out, s_scr):
                    s.start(a, o, c)

            steps = int(np.prod(grid))
            mid_step = (2 * steps) // 3
            if steps > 1 and any(s.mid is not None for s in sides):
                step = functools.reduce(lambda acc, d: acc * grid[d] + pl.program_id(d), range(len(grid)), 0)

                @pl.when(step == mid_step)
                def _():
                    for s, a, o, c in zip(sides, s_in, s_out, s_scr):
                        if s.mid is not None:
                            s.mid(a, o, c)

        body(*m_in, *m_out, *m_scr)
        if sides:
            @pl.when(last)
            def _():
                for s, a, o, c in zip(sides, s_in, s_out, s_scr):
                    if s.mid is not None and steps == 1:
                        s.mid(a, o, c)
                    s.finish(a, o, c)

    res = pl.pallas_call(
        full, name=name, grid=grid,
        in_specs=list(in_specs) + [sp for s in sides for sp in s.in_specs],
        out_specs=list(out_specs) + [ANY for s in sides for _ in s.out_shape],
        out_shape=list(out_shape) + [o for s in sides for o in s.out_shape],
        scratch_shapes=list(scratch_shapes) + [c for s in sides for c in s.scratch],
        compiler_params=_cp(dimension_semantics=("arbitrary",) * len(grid),
                            **({"collective_id": BARRIER_IDS[peers]} if peers else {})),
    )(*args, *[a for s in sides for a in s.args])
    res = list(res)
    if not sides:
        return res
    outs, pos = take(res, 0, no)
    side_outs = []
    for _, o, _ in cnt:
        r, pos = take(res, pos, o)
        side_outs.append(r)
    return outs, side_outs


def _inv_freq_lanes():
    inv = np.float32(ROPE_THETA) ** (-np.arange(0, ROT_DIM, 2, dtype=np.float32) / np.float32(ROT_DIM))
    lane = np.arange(LANES) % HD
    out = np.where(lane < ROT_DIM, inv[lane % (ROT_DIM // 2)], 0.0).astype(np.float32)
    return jnp.asarray(out.reshape(1, LANES))


def rope_tables(pos_col, sides=()):
    def body(p_ref, f_ref, c_ref, s1_ref, s2_ref):
        ang = p_ref[...].astype(F32) * f_ref[...]
        lane = lax.broadcasted_iota(jnp.int32, ang.shape, 1) % HD
        cs = jnp.cos(ang)
        sn = jnp.sin(ang)
        c_ref[...] = jnp.where(lane < ROT_DIM, cs, 1.0)
        s1_ref[...] = jnp.where(lane < ROT_DIM // 2, -sn, 0.0)
        s2_ref[...] = jnp.where(lane < ROT_DIM // 2, 0.0, jnp.where(lane < ROT_DIM, sn, 0.0))

    sds = jax.ShapeDtypeStruct((S, LANES), F32)
    return _call(
        body, sides, name="rope_tables", grid=(S // TM,),
        in_specs=[_row(1), pl.BlockSpec((1, LANES), lambda i: (0, 0))],
        out_specs=[_row(LANES)] * 3, out_shape=[sds] * 3,
        args=(pos_col, _inv_freq_lanes()))


def _rope(v, c, s1, s2):
    return v * c + pltpu.roll(v, LANES - 8, axis=1) * s1 + pltpu.roll(v, 8, axis=1) * s2


def _rope_t(d, c, s1, s2):
    return d * c - pltpu.roll(d, LANES - 8, axis=1) * s1 - pltpu.roll(d, 8, axis=1) * s2


def _head_mat():
    r = lax.broadcasted_iota(jnp.int32, (LANES, LANES), 0) // HD
    c = lax.broadcasted_iota(jnp.int32, (LANES, LANES), 1) // HD
    return jnp.where(r == c, 1.0 / HD, 0.0).astype(BF16)


def _head_mean(t, e):
    hi = t.astype(BF16)
    rest = (t - hi.astype(F32)).astype(BF16)
    return _dot(hi, e) + _dot(rest, e)


def in_proj(x, norm_w, w_in, sides=()):
    nchunk = 5
    cw = INW // nchunk

    def body(x_ref, nw_ref, w_ref, ht_ref, p_ref):
        xv = x_ref[...]
        r = lax.rsqrt(jnp.mean(xv * xv, axis=-1, keepdims=True) + EPS)
        hf = xv * r * nw_ref[...]
        ht_ref[...] = hf.T.astype(BF16)
        h = hf.astype(BF16)
        for j in range(nchunk):
            p_ref[:, j * cw:(j + 1) * cw] = _dot_nt(h, w_ref[j * cw:(j + 1) * cw, :])

    return _call(
        body, sides, name="in_proj", grid=(S // TM,),
        in_specs=[_row(D), _res((1, D)), _res((INW, D))],
        out_specs=[pl.BlockSpec((D, TM), lambda i: (0, i)), _row(INW)],
        out_shape=[jax.ShapeDtypeStruct((D, S), BF16), jax.ShapeDtypeStruct((S, INW), F32)],
        args=(x, norm_w, w_in))


def _qk_specs():
    nb = QKV // LANES
    return [pl.BlockSpec((S, LANES), functools.partial(lambda hp, g, o: (0, o + g * 4 + hp), o=o))
            for o in (OFF_Q // LANES, OFF_K // LANES, OFF_V // LANES)]


def _tab_specs():
    return [pl.BlockSpec((S, LANES), lambda hp, g: (0, 0), pipeline_mode=pl.Buffered(1))] * 3


def _vec_spec():
    return pl.BlockSpec((1, LANES), lambda hp, g: (0, 0))


def _sub_rows(r, d, start, n):
    if d == 1:
        return pl.ds(start, n)
    return pl.ds(r + d * start, n, stride=d)


def _band_window(i, L):
    W = min(TQ + 2 * HALF_SPAN, L)
    q0 = pl.multiple_of(i * TQ, TQ)
    k0 = pl.multiple_of(jnp.clip(q0 - HALF_SPAN, 0, L - W), HALF_SPAN)
    qpos = q0 + (lax.broadcasted_iota(jnp.int32, (2 * TQ, W), 0) & (TQ - 1))
    kpos = k0 + lax.broadcasted_iota(jnp.int32, (2 * TQ, W), 1)
    valid = jnp.abs(qpos - kpos) <= HALF_SPAN
    return W, q0, k0, valid


def _stack_heads(t, lo):
    z = jnp.zeros_like(t)
    return jnp.concatenate([jnp.where(lo, t, z), jnp.where(lo, z, t)], axis=0)


def _unstack_heads(t2, lo):
    return jnp.where(lo, t2[0:TQ], t2[TQ:2 * TQ])


CHAINS = 8


def _interleave(d):
    ru = min(d, CHAINS)
    return ru, min(CHAINS // ru, S // d // TQ)


def _for_blocks(n, fn):
    if n == 1:
        fn(0)
    else:
        def it(j, _):
            fn(j)
            return 0
        lax.fori_loop(0, n, it, 0)


def attn_fwd(proj, tabs, qw2, kw2, sides=()):
    CH = 256

    def body(q_ref, k_ref, v_ref, c_ref, s1_ref, s2_ref, qw_ref, kw_ref, at_ref, ls_ref,
             qs, ks, vs, osub, lsub, onat, lnat, qn, kn):
        g = pl.program_id(1)
        lo = lax.broadcasted_iota(jnp.int32, (1, LANES), 1) < HD
        e = _head_mat()

        def prep(i, _):
            rows = pl.ds(pl.multiple_of(i * CH, CH), CH)
            c, s1, s2 = c_ref[rows, :], s1_ref[rows, :], s2_ref[rows, :]
            for t_ref, w_ref, out, scale in ((q_ref, qw_ref, qn, HD ** -0.5), (k_ref, kw_ref, kn, 1.0)):
                t = t_ref[rows, :]
                r = lax.rsqrt(_head_mean(t * t, e) + EPS)
                out[rows, :] = _rope(t * r * w_ref[...], c, s1, s2) * scale
            return 0

        lax.fori_loop(0, S // CH, prep, 0, unroll=4)

        def group(gi, d):
            L = S // d

            ru, nb = _interleave(d)

            def stage(r, off):
                for c0 in range(0, L, CH):
                    n = min(CH, L)
                    rows = _sub_rows(r, d, c0, n)
                    dst = pl.ds(off + c0, n)
                    qs[dst, :] = qn[rows, :].astype(BF16)
                    ks[dst, :] = kn[rows, :].astype(BF16)
                    vs[dst, :] = v_ref[rows, :].astype(BF16)

            def one(off, i):
                W, q0, k0, valid = _band_window(i, L)
                q2 = _stack_heads(qs[pl.ds(off + q0, TQ), :], lo)
                sc = jnp.where(valid, _dot_nt(q2, ks[pl.ds(off + k0, W), :]), NEG_INF)
                m = jnp.max(sc, axis=-1, keepdims=True)
                p = jnp.exp(sc - m)
                den = jnp.sum(p, axis=-1, keepdims=True)
                o2 = _dot(p.astype(BF16), vs[pl.ds(off + k0, W), :]) / den
                l2 = jnp.broadcast_to(m + jnp.log(den), (2 * TQ, LANES))
                osub[pl.ds(off + q0, TQ), :] = _unstack_heads(o2, lo)
                lsub[pl.ds(off + q0, TQ), :] = _unstack_heads(l2, lo)

            def unstage(r, off):
                for c0 in range(0, L, CH):
                    n = min(CH, L)
                    rows = _sub_rows(r, d, c0, n)
                    onat[gi, rows, :] = osub[pl.ds(off + c0, n), :]
                    lnat[gi, rows, :] = lsub[pl.ds(off + c0, n), :]

            def step(t, _):
                for u in range(ru):
                    stage(t * ru + u, u * L)
                _for_blocks(L // TQ // nb, lambda j: [one(u * L, j * nb + b) for u in range(ru) for b in range(nb)])
                for u in range(ru):
                    unstage(t * ru + u, u * L)
                return 0

            lax.fori_loop(0, d // ru, step, 0)

        for gi, d in enumerate(DILATIONS):
            pl.when(g == gi)(functools.partial(group, gi, d))

        @pl.when(g == len(DILATIONS) - 1)
        def _():
            def mix(i, _):
                rows = pl.ds(pl.multiple_of(i * CH, CH), CH)
                l0, l1, l2 = lnat[0, rows, :], lnat[1, rows, :], lnat[2, rows, :]
                m = jnp.maximum(jnp.maximum(l0, l1), l2)
                e0, e1, e2 = jnp.exp(l0 - m), jnp.exp(l1 - m), jnp.exp(l2 - m)
                den = e0 + e1 + e2
                a = (e0 * onat[0, rows, :] + e1 * onat[1, rows, :] + e2 * onat[2, rows, :]) / den
                at_ref[rows, :] = a.astype(BF16)
                ls_ref[rows, :] = m + jnp.log(den)
                return 0

            lax.fori_loop(0, S // CH, mix, 0)

    out_spec = pl.BlockSpec((S, LANES), lambda hp, g: (0, hp))
    return _call(
        body, sides, name="attn_fwd", grid=(4, 3),
        in_specs=_qk_specs() + _tab_specs() + [_vec_spec(), _vec_spec()],
        out_specs=[out_spec, out_spec],
        out_shape=[jax.ShapeDtypeStruct((S, CC), BF16), jax.ShapeDtypeStruct((S, CC), F32)],
        scratch_shapes=[pltpu.VMEM((S, LANES), BF16)] * 3 + [pltpu.VMEM((S, LANES), F32)] * 2
        + [pltpu.VMEM((3, S, LANES), F32)] * 2 + [pltpu.VMEM((S, LANES), F32)] * 2,
        args=(proj, proj, proj, *tabs, qw2, kw2))


def attn_bwd(proj, tabs, qw2, kw2, d_attn, attn, lse, sides=()):
    CH = 256

    def body(q_ref, k_ref, v_ref, c_ref, s1_ref, s2_ref, qw_ref, kw_ref, do_ref, at_ref, ls_ref,
             dq_ref, dk_ref, dv_ref, gqw_ref, gkw_ref,
             qs, ks, vs, dos, dsub, lsub, dqs, dks, dvs, dnat, qx, kx, dvn, tnq, tnk, rrq, rrk):
        hp, g = pl.program_id(0), pl.program_id(1)
        lo = lax.broadcasted_iota(jnp.int32, (1, LANES), 1) < HD
        e = _head_mat()
        both = ((q_ref, qw_ref, qx, tnq, rrq, HD ** -0.5), (k_ref, kw_ref, kx, tnk, rrk, 1.0))

        @pl.when((hp == 0) & (g == 0))
        def _():
            gqw_ref[...] = jnp.zeros_like(gqw_ref)
            gkw_ref[...] = jnp.zeros_like(gkw_ref)

        def prep(i, _):
            rows = pl.ds(pl.multiple_of(i * CH, CH), CH)
            dnat[rows, :] = _head_mean(do_ref[rows, :] * at_ref[rows, :].astype(F32), e) * float(HD)
            c, s1, s2 = c_ref[rows, :], s1_ref[rows, :], s2_ref[rows, :]
            for t_ref, w_ref, x, tn_s, rr_s, scale in both:
                t = t_ref[rows, :]
                rr = lax.rsqrt(_head_mean(t * t, e) + EPS)
                tn = t * rr
                rr_s[rows, :] = rr
                tn_s[rows, :] = tn
                x[rows, :] = _rope(tn * w_ref[...], c, s1, s2) * scale
            return 0

        lax.fori_loop(0, S // CH, prep, 0, unroll=4)

        def group(d):
            L = S // d

            ru, nb = _interleave(d)

            def stage(r, off):
                for c0 in range(0, L, CH):
                    n = min(CH, L)
                    rows = _sub_rows(r, d, c0, n)
                    dst = pl.ds(off + c0, n)
                    qs[dst, :] = qx[rows, :].astype(BF16)
                    ks[dst, :] = kx[rows, :].astype(BF16)
                    vs[dst, :] = v_ref[rows, :].astype(BF16)
                    dos[dst, :] = do_ref[rows, :].astype(BF16)
                    dsub[dst, :] = dnat[rows, :]
                    lsub[dst, :] = ls_ref[rows, :]
                    dks[dst, :] = jnp.zeros((n, LANES), F32)
                    dvs[dst, :] = jnp.zeros((n, LANES), F32)

            def one(off, i):
                W, q0, k0, valid = _band_window(i, L)
                qrows, krows = pl.ds(off + q0, TQ), pl.ds(off + k0, W)
                q2 = _stack_heads(qs[qrows, :], lo)
                do2 = _stack_heads(dos[qrows, :], lo)
                kk, vv = ks[krows, :], vs[krows, :]
                lse_b, dd_b = lsub[qrows, :], dsub[qrows, :]
                lse2 = jnp.concatenate([lse_b[:, 0:1], lse_b[:, HD:HD + 1]], axis=0)
                dd2 = jnp.concatenate([dd_b[:, 0:1], dd_b[:, HD:HD + 1]], axis=0)
                sc = jnp.where(valid, _dot_nt(q2, kk), NEG_INF)
                p = jnp.exp(sc - lse2)
                ds = (p * (_dot_nt(do2, vv) - dd2)).astype(BF16)
                dqs[qrows, :] = _unstack_heads(_dot(ds, kk), lo)
                dks[krows, :] = dks[krows, :] + _dot_tn(ds, q2)
                dvs[krows, :] = dvs[krows, :] + _dot_tn(p.astype(BF16), do2)

            def unstage(r, off):
                for c0 in range(0, L, CH):
                    n = min(CH, L)
                    rows = _sub_rows(r, d, c0, n)
                    src = pl.ds(off + c0, n)
                    qx[rows, :] = dqs[src, :]
                    kx[rows, :] = dks[src, :]
                    dvn[rows, :] = dvs[src, :]

            def step(t, _):
                for u in range(ru):
                    stage(t * ru + u, u * L)
                _for_blocks(L // TQ // nb, lambda j: [one(u * L, j * nb + b) for u in range(ru) for b in range(nb)])
                for u in range(ru):
                    unstage(t * ru + u, u * L)
                return 0

            lax.fori_loop(0, d // ru, step, 0)

        for gi, d in enumerate(DILATIONS):
            pl.when(g == gi)(functools.partial(group, d))

        def emit(i, _):
            rows = pl.ds(pl.multiple_of(i * CH, CH), CH)
            c, s1, s2 = c_ref[rows, :], s1_ref[rows, :], s2_ref[rows, :]
            for (_, w_ref, x, tn_s, rr_s, scale), out, gw_ref in zip(both, (dq_ref, dk_ref), (gqw_ref, gkw_ref)):
                tn = tn_s[rows, :]
                dy = _rope_t(x[rows, :] * scale, c, s1, s2)
                gw_ref[0:1, :] = gw_ref[0:1, :] + jnp.sum(dy * tn, axis=0, keepdims=True)
                dtn = dy * w_ref[...]
                out[rows, :] = (rr_s[rows, :] * (dtn - tn * _head_mean(dtn * tn, e))).astype(BF16)
            dv_ref[rows, :] = dvn[rows, :].astype(BF16)
            return 0

        lax.fori_loop(0, S // CH, emit, 0, unroll=4)

    nat_spec = pl.BlockSpec((S, LANES), lambda hp, g: (0, hp))
    out_spec = pl.BlockSpec((None, S, LANES), lambda hp, g: (g, 0, hp))
    acc_spec = pl.BlockSpec((8, LANES), lambda hp, g: (0, 0))
    return _call(
        body, sides, name="attn_bwd", grid=(4, 3),
        in_specs=_qk_specs() + _tab_specs() + [_vec_spec(), _vec_spec(), nat_spec, nat_spec, nat_spec],
        out_specs=[out_spec] * 3 + [acc_spec] * 2,
        out_shape=[jax.ShapeDtypeStruct((QKV // PLANE, S, PLANE), BF16)] * 3 + [jax.ShapeDtypeStruct((8, LANES), F32)] * 2,
        scratch_shapes=[pltpu.VMEM((S, LANES), BF16)] * 4 + [pltpu.VMEM((S, LANES), F32)] * 13,
        args=(proj, proj, proj, *tabs, qw2, kw2, d_attn, attn, lse))


PADR = 16
CT = 128


def _conv_specs():
    return [pl.BlockSpec((S, CC), lambda i: (0, OFF_CA // CC)), pl.BlockSpec((S, CC), lambda i: (0, OFF_CB // CC))]


NCB = CC // LANES


def _pad_zero(pad):
    for cb in range(NCB):
        pad[cb, 0:PADR, :] = jnp.zeros((PADR, LANES), F32)
        pad[cb, PADR + S:PADR + S + PADR, :] = jnp.zeros((PADR, LANES), F32)


def _pad_store(pad, row0, n, val):
    for cb in range(NCB):
        pad[cb, pl.ds(pl.multiple_of(row0 + PADR, 8), n), :] = val[:, cb * LANES:(cb + 1) * LANES]


def _taps(pad_ref, cb, s0, weights):
    acc = jnp.zeros((CT, LANES), F32)
    for k in range(KW):
        acc = acc + weights[k] * pad_ref[cb, pl.ds(s0 + k + 1, CT), :]
    return acc


def conv_fwd(proj, conv_w, conv_b, ln_w, ln_b, sides=()):
    def body(a_ref, b_ref, w_ref, cb_ref, lw_ref, lb_ref, c_ref, u3_ref, upad):
        _pad_zero(upad)

        def glu(i, _):
            rows = pl.ds(pl.multiple_of(i * TM, TM), TM)
            _pad_store(upad, i * TM, TM, a_ref[rows, :] * _sigmoid(b_ref[rows, :]))
            return 0

        lax.fori_loop(0, S // TM, glu, 0)

        def chunk(i, _):
            s0 = pl.multiple_of(i * CT, CT)
            for cb in range(CC // LANES):
                cols = slice(cb * LANES, (cb + 1) * LANES)
                w = [w_ref[k:k + 1, cols] for k in range(KW)]
                c_ref[pl.ds(s0, CT), cols] = _taps(upad, cb, s0, w) + cb_ref[:, cols]
            cv = c_ref[pl.ds(s0, CT), :]
            mu = jnp.mean(cv, axis=-1, keepdims=True)
            xc = cv - mu
            rstd = lax.rsqrt(jnp.mean(xc * xc, axis=-1, keepdims=True) + EPS)
            yl = xc * rstd * lw_ref[...] + lb_ref[...]
            u3_ref[pl.ds(s0, CT), :] = (yl * _sigmoid(yl)).astype(BF16)
            return 0

        lax.fori_loop(0, S // CT, chunk, 0)

    vec = pl.BlockSpec((1, CC), lambda i: (0, 0))
    full = pl.BlockSpec((S, CC), lambda i: (0, 0))
    return _call(
        body, sides, name="conv_fwd", grid=(1,),
        in_specs=_conv_specs() + [pl.BlockSpec((KW, CC), lambda i: (0, 0)), vec, vec, vec],
        out_specs=[full, full],
        out_shape=[jax.ShapeDtypeStruct((S, CC), F32), jax.ShapeDtypeStruct((S, CC), BF16)],
        scratch_shapes=[pltpu.VMEM((NCB, S + 2 * PADR, LANES), F32)],
        args=(proj, proj, conv_w, conv_b, ln_w, ln_b))


def conv_bwd(proj, cpre, d_u3, conv_w, conv_w_rev, ln_w, ln_b, sides=()):
    def body(a_ref, b_ref, c_ref, du3_ref, w_ref, wr_ref, lw_ref, lb_ref,
             dc_ref, gw_ref, gcb_ref, glw_ref, glb_ref, upad, dpad):
        _pad_zero(upad)
        _pad_zero(dpad)
        gw_ref[...] = jnp.zeros_like(gw_ref)

        def ln_bwd(i, carry):
            gcb, glw, glb = carry
            rows = pl.ds(pl.multiple_of(i * TM, TM), TM)
            _pad_store(upad, i * TM, TM, a_ref[rows, :] * _sigmoid(b_ref[rows, :]))
            cv = c_ref[rows, :]
            mu = jnp.mean(cv, axis=-1, keepdims=True)
            xc = cv - mu
            rstd = lax.rsqrt(jnp.mean(xc * xc, axis=-1, keepdims=True) + EPS)
            xh = xc * rstd
            yl = xh * lw_ref[...] + lb_ref[...]
            dyl = du3_ref[rows, :] * _dsilu(yl, _sigmoid(yl))
            dxh = dyl * lw_ref[...]
            dcv = rstd * (dxh - jnp.mean(dxh, axis=-1, keepdims=True)
                          - xh * jnp.mean(dxh * xh, axis=-1, keepdims=True))
            _pad_store(dpad, i * TM, TM, dcv)
            return (gcb + jnp.sum(dcv, axis=0, keepdims=True),
                    glw + jnp.sum(dyl * xh, axis=0, keepdims=True),
                    glb + jnp.sum(dyl, axis=0, keepdims=True))

        z = jnp.zeros((1, CC), F32)
        gcb, glw, glb = lax.fori_loop(0, S // TM, ln_bwd, (z, z, z))
        gcb_ref[...] = gcb
        glw_ref[...] = glw
        glb_ref[...] = glb

        def chunk(i, _):
            s0 = pl.multiple_of(i * CT, CT)
            for cb in range(CC // LANES):
                cols = slice(cb * LANES, (cb + 1) * LANES)
                wr = [wr_ref[k:k + 1, cols] for k in range(KW)]
                du = _taps(dpad, cb, s0, wr)
                dcv = dpad[cb, pl.ds(s0 + PADR, CT), :]
                for k in range(KW):
                    gw_ref[k:k + 1, cols] = gw_ref[k:k + 1, cols] + jnp.sum(
                        upad[cb, pl.ds(s0 + k + 1, CT), :] * dcv, axis=0, keepdims=True)
                av = a_ref[pl.ds(s0, CT), cols]
                sb = _sigmoid(b_ref[pl.ds(s0, CT), cols])
                dc_ref[0, pl.ds(s0, CT), cols] = (du * sb).astype(BF16)
                dc_ref[1, pl.ds(s0, CT), cols] = (du * av * sb * (1.0 - sb)).astype(BF16)
            return 0

        lax.fori_loop(0, S // CT, chunk, 0)

    vec = pl.BlockSpec((1, CC), lambda i: (0, 0))
    full = pl.BlockSpec((S, CC), lambda i: (0, 0))
    wsp = pl.BlockSpec((KW, CC), lambda i: (0, 0))
    return _call(
        body, sides, name="conv_bwd", grid=(1,),
        in_specs=_conv_specs() + [full, full, wsp, wsp, vec, vec],
        out_specs=[pl.BlockSpec((2, S, CC), lambda i: (0, 0, 0)), wsp, vec, vec, vec],
        out_shape=[jax.ShapeDtypeStruct((2, S, CC), BF16), jax.ShapeDtypeStruct((KW, CC), F32)]
        + [jax.ShapeDtypeStruct((1, CC), F32)] * 3,
        scratch_shapes=[pltpu.VMEM((NCB, S + 2 * PADR, LANES), F32)] * 2,
        args=(proj, proj, cpre, d_u3, conv_w, conv_w_rev, ln_w, ln_b))


def _gate_specs():
    return [_row(CC, col=OFF_GA // CC + j) for j in range(4)]


def _gates(g_refs, bg_ref):
    ga = _sigmoid(jnp.concatenate([g_refs[0][...], g_refs[1][...]], axis=1) + bg_ref[0:1, :])
    gb = _sigmoid(jnp.concatenate([g_refs[2][...], g_refs[3][...]], axis=1) + bg_ref[1:2, :])
    return ga, gb


def mix_out(x, proj, b_gate, attn, u3, w_o, w_pw, w_out):
    def body(x_ref, g0, g1, g2, g3, bg_ref, at_ref, u3_ref, wo_ref, wp_ref, wout_ref,
             x1_ref, z_ref, ya_ref, yb_ref):
        ga, gb = _gates((g0, g1, g2, g3), bg_ref)
        ya = _dot(at_ref[...], wo_ref[...])
        yb = _dot(u3_ref[...], wp_ref[...])
        z = (ga * ya + gb * yb).astype(BF16)
        ya_ref[...] = ya.astype(BF16)
        yb_ref[...] = yb.astype(BF16)
        z_ref[...] = z
        x1_ref[...] = x_ref[...] + _dot(z, wout_ref[...])

    return pl.pallas_call(
        body, name="mix_out", grid=(S // TM,),
        in_specs=[_row(D)] + _gate_specs() + [_res((2, D)), _row(CC), _row(CC),
                                              _res((CC, D)), _res((CC, D)), _res((D, D))],
        out_specs=[_row(D)] * 4,
        out_shape=[jax.ShapeDtypeStruct((S, D), F32)] + [jax.ShapeDtypeStruct((S, D), BF16)] * 3,
        compiler_params=_cp(dimension_semantics=("arbitrary",)),
    )(x, proj, proj, proj, proj, b_gate, attn, u3, w_o, w_pw, w_out)


def out_bwd(d_x1b, proj, b_gate, ya, yb, w_o, w_pw, w_out, sides=()):
    def body(dx_ref, g0, g1, g2, g3, bg_ref, ya_ref, yb_ref, wo_ref, wp_ref, wout_ref,
             dya_ref, dyb_ref, dgl_ref, dat_ref, du3_ref, gbg_ref):
        @pl.when(pl.program_id(0) == 0)
        def _():
            gbg_ref[...] = jnp.zeros_like(gbg_ref)

        ga, gb = _gates((g0, g1, g2, g3), bg_ref)
        dz = _dot_nt(dx_ref[...], wout_ref[...])
        dya = (dz * ga).astype(BF16)
        dyb = (dz * gb).astype(BF16)
        dgla = dz * ya_ref[...].astype(F32) * ga * (1.0 - ga)
        dglb = dz * yb_ref[...].astype(F32) * gb * (1.0 - gb)
        dya_ref[...] = dya
        dyb_ref[...] = dyb
        for j in range(2):
            dgl_ref[j] = dgla[:, j * PLANE:(j + 1) * PLANE].astype(BF16)
            dgl_ref[2 + j] = dglb[:, j * PLANE:(j + 1) * PLANE].astype(BF16)
        gbg_ref[0:1, :] = gbg_ref[0:1, :] + jnp.sum(dgla, axis=0, keepdims=True)
        gbg_ref[1:2, :] = gbg_ref[1:2, :] + jnp.sum(dglb, axis=0, keepdims=True)
        dat_ref[...] = _dot_nt(dya, wo_ref[...])
        du3_ref[...] = _dot_nt(dyb, wp_ref[...])

    return _call(
        body, sides, name="out_bwd", grid=(S // TM,),
        in_specs=[_row(D)] + _gate_specs() + [_res((2, D)), _row(D), _row(D),
                                              _res((CC, D)), _res((CC, D)), _res((D, D))],
        out_specs=[_row(D), _row(D), _planes(2 * D), _row(CC), _row(CC), pl.BlockSpec((2, D), lambda i: (0, 0))],
        out_shape=[jax.ShapeDtypeStruct((S, D), BF16)] * 2 + [jax.ShapeDtypeStruct((2 * D // PLANE, S, PLANE), BF16)]
        + [jax.ShapeDtypeStruct((S, CC), F32)] * 2 + [jax.ShapeDtypeStruct((2, D), F32)],
        args=(d_x1b, proj, proj, proj, proj, b_gate, ya, yb, w_o, w_pw, w_out))


def ffn_in(x1, norm_w, w_ffn_in, sides=()):
    half = FF // 2

    def body(x_ref, nw_ref, w_ref, h_ref, gu_ref, f_ref):
        xv = x_ref[...]
        r = lax.rsqrt(jnp.mean(xv * xv, axis=-1, keepdims=True) + EPS)
        h = (xv * r * nw_ref[...]).astype(BF16)
        h_ref[...] = h
        for j in range(2):
            gt = _dot_nt(h, w_ref[j * half:(j + 1) * half, :])
            up = _dot_nt(h, w_ref[FF + j * half:FF + (j + 1) * half, :])
            gu_ref[:, j * half:(j + 1) * half] = gt.astype(BF16)
            gu_ref[:, FF + j * half:FF + (j + 1) * half] = up.astype(BF16)
            f_ref[:, j * half:(j + 1) * half] = (gt * _sigmoid(gt) * up).astype(BF16)

    return _call(
        body, sides, name="ffn_in", grid=(S // TM,),
        in_specs=[_row(D), _res((1, D)), _res((2 * FF, D))],
        out_specs=[_row(D), _row(2 * FF), _row(FF)],
        out_shape=[jax.ShapeDtypeStruct((S, D), BF16), jax.ShapeDtypeStruct((S, 2 * FF), BF16),
                   jax.ShapeDtypeStruct((S, FF), BF16)],
        args=(x1, norm_w, w_ffn_in))


def ffn_out_loss(x1, f, w_ffn_out, target):
    def body(x_ref, f_ref, w_ref, t_ref, dy_ref, dyb_ref, sq_ref):
        @pl.when(pl.program_id(0) == 0)
        def _():
            sq_ref[...] = jnp.zeros_like(sq_ref)

        diff = x_ref[...] + _dot(f_ref[...], w_ref[...]) - t_ref[...]
        dy = diff * (1.0 / D)
        dy_ref[...] = dy
        dyb_ref[...] = dy.astype(BF16)
        sq_ref[...] = sq_ref[...] + jnp.sum((diff * diff).reshape(TM // 8, 8, D), axis=0)

    return pl.pallas_call(
        body, name="ffn_out_loss", grid=(S // TM,),
        in_specs=[_row(D), _row(FF), _res((FF, D)), _row(D)],
        out_specs=[_row(D), _row(D), pl.BlockSpec((8, D), lambda i: (0, 0))],
        out_shape=[jax.ShapeDtypeStruct((S, D), F32), jax.ShapeDtypeStruct((S, D), BF16),
                   jax.ShapeDtypeStruct((8, D), F32)],
        compiler_params=_cp(dimension_semantics=("arbitrary",)),
    )(x1, f, w_ffn_out, target)


def _rms_bwd(xv, nw, dh):
    r = lax.rsqrt(jnp.mean(xv * xv, axis=-1, keepdims=True) + EPS)
    xn = xv * r
    dxn = dh * nw
    dx = r * (dxn - xn * jnp.mean(dxn * xn, axis=-1, keepdims=True))
    return dx, dh * xn


def ffn_bwd(dy, dyb, gu, x1, norm_w, w_ffn_in, w_ffn_out, sides=()):
    def body(dy_ref, dyb_ref, gu_ref, x_ref, nw_ref, wi_ref, wo_ref, dgu_ref, dx_ref, dxb_ref, gn_ref):
        @pl.when(pl.program_id(0) == 0)
        def _():
            gn_ref[...] = jnp.zeros_like(gn_ref)

        df = _dot_nt(dyb_ref[...], wo_ref[...])
        gt = gu_ref[:, 0:FF].astype(F32)
        up = gu_ref[:, FF:2 * FF].astype(F32)
        sg = _sigmoid(gt)
        dgt = (df * up * _dsilu(gt, sg)).astype(BF16)
        dup = (df * gt * sg).astype(BF16)
        dgu_ref[:, 0:FF] = dgt
        dgu_ref[:, FF:2 * FF] = dup
        dh = _dot(dgt, wi_ref[0:FF, :]) + _dot(dup, wi_ref[FF:2 * FF, :])
        dxn, gw = _rms_bwd(x_ref[...], nw_ref[...], dh)
        dx = dy_ref[...] + dxn
        dx_ref[...] = dx
        dxb_ref[...] = dx.astype(BF16)
        gn_ref[...] = gn_ref[...] + jnp.sum(gw, axis=0, keepdims=True)

    return _call(
        body, sides, name="ffn_bwd", grid=(S // TM,),
        in_specs=[_row(D), _row(D), _row(2 * FF), _row(D), _res((1, D)), _res((2 * FF, D)), _res((FF, D))],
        out_specs=[_row(2 * FF), _row(D), _row(D), pl.BlockSpec((1, D), lambda i: (0, 0))],
        out_shape=[jax.ShapeDtypeStruct((S, 2 * FF), BF16), jax.ShapeDtypeStruct((S, D), F32),
                   jax.ShapeDtypeStruct((S, D), BF16), jax.ShapeDtypeStruct((1, D), F32)],
        args=(dy, dyb, gu, x1, norm_w, w_ffn_in, w_ffn_out))


def in_bwd(d_q, d_k, d_v, d_conv, d_gl, w_in, x, d_x1, norm_w, sides=()):
    segs = ((OFF_Q, QKV), (OFF_K, QKV), (OFF_V, QKV), (OFF_CA, 2 * CC), (OFF_GA, 2 * D))

    def body(dq_ref, dk_ref, dv_ref, dc_ref, dg_ref, w_ref, x_ref, dx1_ref, nw_ref, gx_ref, gn_ref):
        @pl.when(pl.program_id(0) == 0)
        def _():
            gn_ref[...] = jnp.zeros_like(gn_ref)

        dh = jnp.zeros((TM, D), F32)
        for ref, (off, width) in zip((dq_ref, dk_ref, dv_ref, dc_ref, dg_ref), segs):
            for j in range(width // PLANE):
                dh = dh + _dot(ref[j], w_ref[off + j * PLANE:off + (j + 1) * PLANE, :])
        dxn, gw = _rms_bwd(x_ref[...], nw_ref[...], dh)
        gx_ref[...] = dx1_ref[...] + dxn
        gn_ref[...] = gn_ref[...] + jnp.sum(gw, axis=0, keepdims=True)

    return _call(
        body, sides, name="in_bwd", grid=(S // TM,),
        in_specs=[_planes(QKV)] * 3 + [_planes(2 * CC), _planes(2 * D), _res((INW, D)), _row(D), _row(D), _res((1, D))],
        out_specs=[_row(D), pl.BlockSpec((1, D), lambda i: (0, 0))],
        out_shape=[jax.ShapeDtypeStruct((S, D), F32), jax.ShapeDtypeStruct((1, D), F32)],
        args=(d_q, d_k, d_v, d_conv, d_gl, w_in, x, d_x1, norm_w))


def mm_tn(name, a, b, tm, tn, sides=()):
    M, N = a.shape[1], b.shape[1]

    def body(a_ref, b_ref, o_ref):
        o_ref[...] = _dot_tn(a_ref[...], b_ref[...])

    res = _call(
        body, sides, name=name, grid=(M // tm, N // tn),
        in_specs=[pl.BlockSpec((S, tm), lambda i, j: (0, i)), pl.BlockSpec((S, tn), lambda i, j: (0, j))],
        out_specs=[pl.BlockSpec((tm, tn), lambda i, j: (i, j))],
        out_shape=[jax.ShapeDtypeStruct((M, N), F32)],
        args=(a, b))
    return (res[0][0], res[1]) if sides else res[0]


GW_IN_TN = PLANE
GW_IN_PARTS = 2


def gw_in_t(name, ht, d_segs, col_half, sides=()):
    tn, hw = GW_IN_TN, D // GW_IN_PARTS
    starts, t0 = [], 0
    for seg in d_segs:
        starts.append(t0)
        t0 += seg.shape[0]
    ntiles = [seg.shape[0] for seg in d_segs]

    def body(h_ref, *refs):
        a_refs, o_ref = refs[:-1], refs[-1]
        n = pl.program_id(0)
        for a_ref, st, nt in zip(a_refs, starts, ntiles):
            @pl.when((n >= st) & (n < st + nt))
            def _(a_ref=a_ref):
                o_ref[...] = _dot(h_ref[...], a_ref[...]).T

    def seg_spec(st, nt):
        return pl.BlockSpec((None, S, tn), lambda n: (jnp.clip(n - st, 0, nt - 1), 0, 0))

    res = _call(
        body, sides, name=name, grid=(INW // tn,),
        in_specs=[pl.BlockSpec((hw, S), lambda n: (col_half, 0))] + [seg_spec(st, nt) for st, nt in zip(starts, ntiles)],
        out_specs=[pl.BlockSpec((tn, hw), lambda n: (n, 0))],
        out_shape=[jax.ShapeDtypeStruct((INW, hw), F32)],
        args=(ht, *d_segs))
    return (res[0][0], res[1]) if sides else res[0]


def _place():
    x, y, c = lax.axis_index("x"), lax.axis_index("y"), lax.axis_index("c")
    chips = [(1 - x, y), (x, 1 - y), (1 - x, 1 - y)]
    return x, y, c, chips


def _sems(n):
    return pltpu.SemaphoreType.DMA((n,))


def _remote(src, dst, send, recv, k, to):
    return pltpu.make_async_remote_copy(src_ref=src, dst_ref=dst, send_sem=send.at[k], recv_sem=recv.at[k],
                                        device_id=to, device_id_type=MESH)


def _cast_rows(dst, src, cols=slice(None)):
    rows = src.shape[0]
    step = next((s for s in (128, 64, 32, 16) if rows % s == 0), rows)
    for r0 in range(0, rows, step):
        dst[r0:r0 + step, cols] = src[r0:r0 + step, :].astype(dst.dtype)


def comm_only(name, sides):
    def body():
        pass

    return _call(body, sides, name=name, grid=(1,), in_specs=[], out_specs=[], out_shape=[], args=())[1]


def ag_blocks(shard, dtype):
    R, W = shard.shape

    def copy(outs, scr, k, block, to, src=None):
        dst = outs[0].at[block]
        return _remote(dst if src is None else src, dst, scr[1], scr[2], k, to)

    def local(outs, scr, me):
        return pltpu.make_async_copy(scr[0], outs[0].at[me], scr[3].at[0])

    def start(ins, outs, scr):
        x, y, c, chips = _place()
        me = 4 * x + 2 * y + c
        _cast_rows(scr[0], ins[0])
        local(outs, scr, me).start()
        copy(outs, scr, 0, me, (x, y, 1 - c), src=scr[0]).start()
        for j, (cx, cy) in enumerate(chips):
            copy(outs, scr, 1 + j, me, (cx, cy, c), src=scr[0]).start()

    def finish(ins, outs, scr):
        x, y, c, chips = _place()
        me, sib = 4 * x + 2 * y + c, (x, y, 1 - c)
        passed = []
        for j, (cx, cy) in enumerate(chips):
            theirs = 4 * cx + 2 * cy + c
            copy(outs, scr, 1 + j, theirs, (x, y, c)).wait_recv()
            fwd = copy(outs, scr, 4 + j, theirs, sib)
            fwd.start()
            passed.append(fwd)
        copy(outs, scr, 0, 4 * x + 2 * y + 1 - c, (x, y, c)).wait_recv()
        for j, (cx, cy) in enumerate(chips):
            copy(outs, scr, 4 + j, 4 * cx + 2 * cy + 1 - c, (x, y, c)).wait_recv()
        copy(outs, scr, 0, me, sib, src=scr[0]).wait_send()
        for j, (cx, cy) in enumerate(chips):
            copy(outs, scr, 1 + j, me, (cx, cy, c), src=scr[0]).wait_send()
        for fwd in passed:
            fwd.wait_send()
        local(outs, scr, me).wait()

    return Side((shard,), (VMEM,), (jax.ShapeDtypeStruct((NDEV, R, W), dtype),),
                (pltpu.VMEM((R, W), dtype), _sems(7), _sems(7), _sems(1)), start, finish, None, "dsxy")


def ag_blocks_relay(shard, dtype):
    R, W = shard.shape
    half = R // 2

    def copy(outs, scr, k, block, to, src=None, rows=None):
        dst = outs[0].at[block] if rows is None else outs[0].at[block, pl.ds(rows * half, half), :]
        return _remote(dst if src is None else src, dst, scr[1], scr[2], k, to)

    def local(outs, scr, me):
        return pltpu.make_async_copy(scr[0], outs[0].at[me], scr[3].at[0])

    def own(outs, scr):
        x, y, c, _ = _place()
        me = 4 * x + 2 * y + c
        return [copy(outs, scr, k, me, to, src=scr[0])
                for k, to in enumerate([(x, y, 1 - c), (1 - x, y, c), (x, 1 - y, c)])]

    def start(ins, outs, scr):
        x, y, c, _ = _place()
        _cast_rows(scr[0], ins[0])
        local(outs, scr, 4 * x + 2 * y + c).start()
        for cp in own(outs, scr):
            cp.start()

    def passed_on(outs, scr):
        x, y, c, _ = _place()
        sib, xn, yn = (x, y, 1 - c), (1 - x, y, c), (x, 1 - y, c)
        b_xn, b_yn, b_dg = 4 * (1 - x) + 2 * y + c, 4 * x + 2 * (1 - y) + c, 4 * (1 - x) + 2 * (1 - y) + c
        near = [copy(outs, scr, 5, b_xn, yn, rows=0), copy(outs, scr, 3, b_xn, sib),
                copy(outs, scr, 6, b_yn, xn, rows=1), copy(outs, scr, 4, b_yn, sib)]
        far = [copy(outs, scr, 7, b_dg, sib, rows=0), copy(outs, scr, 8, b_dg, sib, rows=1)]
        return (b_xn, b_yn, b_dg), near, far

    def mid(ins, outs, scr):
        x, y, c, _ = _place()
        (b_xn, b_yn, _), near, _ = passed_on(outs, scr)
        copy(outs, scr, 1, b_xn, (x, y, c)).wait_recv()
        near[0].start()
        near[1].start()
        copy(outs, scr, 2, b_yn, (x, y, c)).wait_recv()
        near[2].start()
        near[3].start()

    def finish(ins, outs, scr):
        x, y, c, _ = _place()
        here = (x, y, c)
        (b_xn, b_yn, b_dg), near, far = passed_on(outs, scr)
        copy(outs, scr, 5, b_dg, here, rows=0).wait_recv()
        far[0].start()
        copy(outs, scr, 6, b_dg, here, rows=1).wait_recv()
        far[1].start()
        flip = 1 - 2 * c
        copy(outs, scr, 0, 4 * x + 2 * y + 1 - c, here).wait_recv()
        copy(outs, scr, 3, b_xn + flip, here).wait_recv()
        copy(outs, scr, 4, b_yn + flip, here).wait_recv()
        copy(outs, scr, 7, b_dg + flip, here, rows=0).wait_recv()
        copy(outs, scr, 8, b_dg + flip, here, rows=1).wait_recv()
        for cp in own(outs, scr) + near + far:
            cp.wait_send()
        local(outs, scr, 4 * x + 2 * y + c).wait()

    return Side((shard,), (VMEM,), (jax.ShapeDtypeStruct((NDEV, R, W), dtype),),
                (pltpu.VMEM((R, W), dtype), _sems(9), _sems(9), _sems(1)), start, finish, mid, "sxy")


def ag_cols(shard):
    K, C = shard.shape
    half, w2 = K // 2, 2 * C

    def win(out, rows_c, chip):
        return out.at[pl.ds(pl.multiple_of(rows_c * half, 16), half), pl.ds(pl.multiple_of(chip * w2, LANES), w2)]

    def ici(outs, scr, j, to, c, k):
        slab, send, recv = scr[2], scr[5], scr[6]
        return _remote(slab.at[pl.ds(pl.multiple_of(c * half, 16), half), :], win(outs[0], c, k), send, recv, j, to)

    def local(outs, scr, k):
        return pltpu.make_async_copy(scr[2], outs[0].at[:, pl.ds(pl.multiple_of(k * w2, LANES), w2)], scr[7].at[0])

    def start(ins, outs, scr):
        stage, inbox, slab, xs, xr = scr[:5]
        x, y, c, chips = _place()
        k = 2 * x + y
        _cast_rows(stage, ins[0])
        swap = _remote(stage, inbox, xs, xr, 0, (x, y, 1 - c))
        swap.start()
        for cc in range(2):
            @pl.when(c == cc)
            def _(cc=cc):
                _cast_rows(slab, stage, slice(cc * C, (cc + 1) * C))
        swap.wait()
        for cc in range(2):
            @pl.when(c == cc)
            def _(cc=cc):
                _cast_rows(slab, inbox, slice((1 - cc) * C, (2 - cc) * C))
        local(outs, scr, k).start()
        for j, (cx, cy) in enumerate(chips):
            ici(outs, scr, j, (cx, cy, c), c, k).start()

    def finish(ins, outs, scr):
        send, recv = scr[5], scr[6]
        x, y, c, chips = _place()
        k, sib = 2 * x + y, (x, y, 1 - c)
        passed = []
        for j, (cx, cy) in enumerate(chips):
            w = win(outs[0], c, 2 * cx + cy)
            _remote(w, w, send, recv, j, sib).wait_recv()
            fwd = _remote(w, w, send, recv, 3 + j, sib)
            fwd.start()
            passed.append(fwd)
        for j, (cx, cy) in enumerate(chips):
            w = win(outs[0], 1 - c, 2 * cx + cy)
            _remote(w, w, send, recv, 3 + j, sib).wait_recv()
        for j, (cx, cy) in enumerate(chips):
            ici(outs, scr, j, (cx, cy, c), c, k).wait_send()
        for fwd in passed:
            fwd.wait_send()
        local(outs, scr, k).wait()

    return Side((shard,), (VMEM,), (jax.ShapeDtypeStruct((K, NDEV * C), BF16),),
                (pltpu.VMEM((K, C), BF16), pltpu.VMEM((K, C), BF16), pltpu.VMEM((K, w2), BF16),
                 _sems(1), _sems(1), _sems(6), _sems(6), _sems(1)), start, finish, None, "dsxy")


def copies_side(args, out_shape, n_copies, plan, peers):
    def copies(ins, outs, scr):
        return [_remote(s_, d_, scr[0], scr[1], i, to) for i, (s_, d_, to) in enumerate(plan(ins, outs))]

    def start(ins, outs, scr):
        for cp in copies(ins, outs, scr):
            cp.start()

    def finish(ins, outs, scr):
        for cp in copies(ins, outs, scr):
            cp.wait()

    return Side(tuple(args), (ANY,) * len(args), tuple(out_shape), (_sems(n_copies), _sems(n_copies)),
                start, finish, None, peers)


def rs_to_sibling(grads):
    out_shape = [jax.ShapeDtypeStruct((4,) + g.shape[1:] if kind == "rows" else (g.shape[0] // 2, g.shape[1]), F32)
                 for kind, g in grads]

    def plan(ins, outs):
        x, y, c, _ = _place()
        sib, res = (x, y, 1 - c), []
        for (kind, _), g, r in zip(grads, ins, outs):
            if kind == "rows":
                res += [(g.at[2 * k + 1 - c], r.at[k], sib) for k in range(4)]
            else:
                half = g.shape[0] // 2
                res.append((g.at[pl.ds(pl.multiple_of((1 - c) * half, 8), half), :], r, sib))
        return res

    return copies_side([g for _, g in grads], out_shape, sum(4 if kind == "rows" else 1 for kind, _ in grads), plan, "s")


def rs_to_chips(parts):
    out_shape = [jax.ShapeDtypeStruct((3,) + p.shape[1:] if kind == "rows" else (3, p.shape[0], p.shape[1] // 4), BF16)
                 for kind, p in parts]

    def plan(ins, outs):
        x, y, c, chips = _place()
        res = []
        for (kind, _), p, r in zip(parts, ins, outs):
            for j, (cx, cy) in enumerate(chips):
                if kind == "rows":
                    src = p.at[2 * cx + cy]
                else:
                    w2 = p.shape[1] // 4
                    src = p.at[:, pl.ds(pl.multiple_of((2 * cx + cy) * w2, LANES), w2)]
                res.append((src, r.at[j], (cx, cy, c)))
        return res

    return copies_side([p for _, p in parts], out_shape, 3 * len(parts), plan, "dxy")


def rs_swap_halves(theirs):
    def plan(ins, outs):
        x, y, c, _ = _place()
        return [(t, r, (x, y, 1 - c)) for t, r in zip(ins, outs)]

    return copies_side(theirs, [jax.ShapeDtypeStruct(t.shape, F32) for t in theirs], len(theirs), plan, "s")


def _row_tiles(rows):
    return 2 if rows % 32 == 0 and rows >= 512 else 1


def chip_sum(name, grad, recv, c_idx, chip_idx):
    _, R, C = grad.shape
    nt = 1
    tr = R // nt

    def body(s_ref, g_ref, r_ref, p_ref, own_ref):
        k = pl.program_id(1)
        tot = g_ref[0] + r_ref[0]
        p_ref[0] = tot.astype(BF16)

        @pl.when(k == s_ref[1])
        def _():
            own_ref[...] = tot

    grid_spec = pltpu.PrefetchScalarGridSpec(
        num_scalar_prefetch=1, grid=(nt, 4),
        in_specs=[pl.BlockSpec((1, tr, C), lambda i, k, s: (2 * k + s[0], i, 0)),
                  pl.BlockSpec((1, tr, C), lambda i, k, s: (k, i, 0))],
        out_specs=[pl.BlockSpec((1, tr, C), lambda i, k, s: (k, i, 0)),
                   pl.BlockSpec((tr, C), lambda i, k, s: (i, 0))])
    return pl.pallas_call(
        body, name=name, grid_spec=grid_spec,
        out_shape=[jax.ShapeDtypeStruct((4, R, C), BF16), jax.ShapeDtypeStruct((R, C), F32)],
        compiler_params=_cp(dimension_semantics=("arbitrary", "arbitrary")),
    )(jnp.stack([c_idx, chip_idx]), grad, recv)


def _half_tiles(half):
    return 2 if half >= 512 else 1


def chip_sum_cols(name, grad, recv, c_idx, chip_idx):
    K, W = grad.shape
    half, w2 = K // 2, W // 4
    nt = _half_tiles(half)
    tr = half // nt

    def body(s_ref, g_ref, r_ref, p_ref, own_ref):
        tot = g_ref[...] + r_ref[...]
        p_ref[...] = tot.astype(BF16)

        @pl.when(pl.program_id(1) == s_ref[1])
        def _():
            own_ref[...] = tot

    grid_spec = pltpu.PrefetchScalarGridSpec(
        num_scalar_prefetch=1, grid=(nt, 4),
        in_specs=[pl.BlockSpec((tr, w2), lambda i, k, s: (s[0] * nt + i, k)),
                  pl.BlockSpec((tr, w2), lambda i, k, s: (i, k))],
        out_specs=[pl.BlockSpec((tr, w2), lambda i, k, s: (i, k)),
                   pl.BlockSpec((tr, w2), lambda i, k, s: (i, 0))])
    return pl.pallas_call(
        body, name=name, grid_spec=grid_spec,
        out_shape=[jax.ShapeDtypeStruct((half, W), BF16), jax.ShapeDtypeStruct((half, w2), F32)],
        compiler_params=_cp(dimension_semantics=("arbitrary", "arbitrary")),
    )(jnp.stack([c_idx, chip_idx]), grad, recv)


def col_final(name, own, recv, c_idx):
    half, w2 = own.shape
    C = w2 // 2
    nt = _half_tiles(half)
    tr = half // nt

    def body(s_ref, o_ref, r_ref, mine_ref, theirs_ref, t_ref):
        t_ref[...] = o_ref[...] + r_ref[0].astype(F32) + r_ref[1].astype(F32) + r_ref[2].astype(F32)
        for cc in range(2):
            @pl.when(s_ref[0] == cc)
            def _(cc=cc):
                mine_ref[...] = t_ref[:, cc * C:(cc + 1) * C]
                theirs_ref[...] = t_ref[:, (1 - cc) * C:(2 - cc) * C]

    grid_spec = pltpu.PrefetchScalarGridSpec(
        num_scalar_prefetch=1, grid=(nt,),
        in_specs=[pl.BlockSpec((tr, w2), lambda i, s: (i, 0)), pl.BlockSpec((3, tr, w2), lambda i, s: (0, i, 0))],
        out_specs=[pl.BlockSpec((tr, C), lambda i, s: (i, 0))] * 2,
        scratch_shapes=[pltpu.VMEM((tr, w2), F32)])
    return pl.pallas_call(
        body, name=name, grid_spec=grid_spec, out_shape=[jax.ShapeDtypeStruct((half, C), F32)] * 2,
        compiler_params=_cp(dimension_semantics=("arbitrary",)),
    )(jnp.stack([c_idx]), own, recv)


def _adamw(w, g, m, v):
    m2 = ADAM_B1 * m + (1.0 - ADAM_B1) * g
    v2 = ADAM_B2 * v + (1.0 - ADAM_B2) * (g * g)
    m_hat = m2 / (1.0 - ADAM_B1 ** ADAM_STEP)
    v_hat = v2 / (1.0 - ADAM_B2 ** ADAM_STEP)
    delta = -ADAM_LR * (m_hat / (jnp.sqrt(v_hat) + ADAM_EPS) + ADAM_WD * w)
    return delta, m2, v2


def shard_adam(name, owns, recvs, w, m, v):
    n = len(owns)
    R, Cp = owns[0].shape
    nt = _row_tiles(R)
    tr = R // nt

    def body(*refs):
        o_refs, r_refs = refs[:n], refs[n:2 * n]
        w_ref, m_ref, v_ref, g_ref, d_ref, nm_ref, nv_ref = refs[2 * n:]
        g = None
        for k in range(n):
            gk = o_refs[k][...] + r_refs[k][0].astype(F32) + r_refs[k][1].astype(F32) + r_refs[k][2].astype(F32)
            g = gk if g is None else jnp.where(pl.program_id(0) == k, gk, g)
        delta, m2, v2 = _adamw(w_ref[...], g, m_ref[...], v_ref[...])
        g_ref[...] = g
        d_ref[...] = delta
        nm_ref[...] = m2
        nv_ref[...] = v2

    part = pl.BlockSpec((tr, Cp), lambda k, i: (i, 0))
    part3 = pl.BlockSpec((3, tr, Cp), lambda k, i: (0, i, 0))
    tile = pl.BlockSpec((tr, Cp), lambda k, i: (i, k))
    return pl.pallas_call(
        body, name=name, grid=(n, nt),
        in_specs=[part] * n + [part3] * n + [tile, tile, tile],
        out_specs=[tile] * 4, out_shape=[jax.ShapeDtypeStruct((R, n * Cp), F32)] * 4,
        compiler_params=_cp(dimension_semantics=("arbitrary", "arbitrary")),
    )(*owns, *recvs, w, m, v)


def adam_cols(name, mine, recv, w, m, v, c_idx):
    half, C = mine.shape
    nt = _half_tiles(half)
    tr = half // nt

    def body(s_ref, a_ref, b_ref, w_ref, m_ref, v_ref, g_ref, d_ref, nm_ref, nv_ref):
        g = jnp.where(pl.program_id(0) == s_ref[0], a_ref[...], b_ref[...])
        delta, m2, v2 = _adamw(w_ref[...], g, m_ref[...], v_ref[...])
        g_ref[...] = g
        d_ref[...] = delta
        nm_ref[...] = m2
        nv_ref[...] = v2

    part = pl.BlockSpec((tr, C), lambda hh, i, s: (i, 0))
    tile = pl.BlockSpec((tr, C), lambda hh, i, s: (hh * nt + i, 0))
    grid_spec = pltpu.PrefetchScalarGridSpec(
        num_scalar_prefetch=1, grid=(2, nt), in_specs=[part, part, tile, tile, tile], out_specs=[tile] * 4)
    return pl.pallas_call(
        body, name=name, grid_spec=grid_spec, out_shape=[jax.ShapeDtypeStruct((2 * half, C), F32)] * 4,
        compiler_params=_cp(dimension_semantics=("arbitrary", "arbitrary")),
    )(jnp.stack([c_idx]), mine, recv, w, m, v)


ROW_N1, ROW_N2, ROW_BG, ROW_QN, ROW_KN, ROW_CB, ROW_LW, ROW_LB, ROW_CW = 0, 1, 2, 4, 5, 6, 7, 8, 9
PACK_ROWS = 40
SMALL = ("norm1_w", "norm2_w", "b_gate", "q_norm_w", "k_norm_w", "conv_b", "conv_ln_w", "conv_ln_b", "conv_w")


def small_sync_adam(g, w, m, v, sq, sides=()):
    ns = len(SMALL)

    def body(*refs):
        gi = dict(zip(SMALL, refs[:ns]))
        wi = dict(zip(SMALL, refs[ns:2 * ns]))
        mi = dict(zip(SMALL, refs[2 * ns:3 * ns]))
        vi = dict(zip(SMALL, refs[3 * ns:4 * ns]))
        sq_ref = refs[4 * ns]
        outs = refs[4 * ns + 1:8 * ns + 1]
        loss_ref = refs[8 * ns + 1]
        pack, recv, tot, send_sems, recv_sems = refs[8 * ns + 2:]
        x, y, c, _ = _place()
        me = 4 * x + 2 * y + c

        pack[...] = jnp.zeros_like(pack)
        pack[ROW_KN:ROW_KN + 1, LANES:2 * LANES] = jnp.full((1, LANES), (0.5 / D) * jnp.sum(sq_ref[...]), F32)
        pack[ROW_N1:ROW_N1 + 1, :] = gi["norm1_w"][...]
        pack[ROW_N2:ROW_N2 + 1, :] = gi["norm2_w"][...]
        pack[ROW_BG:ROW_BG + 2, :] = gi["b_gate"][...]
        pack[ROW_QN:ROW_QN + 1, 0:HD] = gi["q_norm_w"][...]
        pack[ROW_KN:ROW_KN + 1, 0:HD] = gi["k_norm_w"][...]
        pack[ROW_CB:ROW_CB + 1, 0:CC] = gi["conv_b"][...]
        pack[ROW_LW:ROW_LW + 1, 0:CC] = gi["conv_ln_w"][...]
        pack[ROW_LB:ROW_LB + 1, 0:CC] = gi["conv_ln_b"][...]
        pack[ROW_CW:ROW_CW + KW, 0:CC] = gi["conv_w"][...]

        copies = []
        for k in range(1, NDEV):
            peer = (x ^ (k >> 2), y ^ ((k >> 1) & 1), c ^ (k & 1))
            cp = pltpu.make_async_remote_copy(
                src_ref=pack, dst_ref=recv.at[me], send_sem=send_sems.at[k - 1], recv_sem=recv_sems.at[k - 1],
                device_id=peer, device_id_type=MESH)
            cp.start()
            copies.append(cp)
        recv[me] = pack[...]
        for cp in copies:
            cp.wait()
        acc = recv[0]
        for p in range(1, NDEV):
            acc = acc + recv[p]
        tot[...] = acc

        def shard_grad(name):
            if name == "b_gate":
                return tot[ROW_BG:ROW_BG + 2, pl.ds(pl.multiple_of(me * LANES, LANES), LANES)]
            if name == "conv_w":
                win = tot[ROW_CW:ROW_CW + KW, pl.ds(pl.multiple_of((me // 2) * LANES, LANES), LANES)]
                return jnp.where(me % 2 == 1, win[:, HD:LANES], win[:, 0:HD])
            row = {"norm1_w": ROW_N1, "norm2_w": ROW_N2, "q_norm_w": ROW_QN, "k_norm_w": ROW_KN,
                   "conv_b": ROW_CB, "conv_ln_w": ROW_LW, "conv_ln_b": ROW_LB}[name]
            return tot[row:row + 1, 0:wi[name].shape[1]]

        for i, name in enumerate(SMALL):
            gr = shard_grad(name)
            delta, m2, v2 = _adamw(wi[name][...], gr, mi[name][...], vi[name][...])
            outs[4 * i][...] = gr
            outs[4 * i + 1][...] = delta
            outs[4 * i + 2][...] = m2
            outs[4 * i + 3][...] = v2
        loss_ref[...] = tot[ROW_KN:ROW_KN + 1, LANES:2 * LANES]

    out_shape = []
    for name in SMALL:
        out_shape += [jax.ShapeDtypeStruct(w[name].shape, F32)] * 4
    out_shape.append(jax.ShapeDtypeStruct((1, LANES), F32))
    args = [g[k] for k in SMALL] + [w[k] for k in SMALL] + [m[k] for k in SMALL] + [v[k] for k in SMALL] + [sq]
    res = _call(
        body, sides, name="small_sync_adam", grid=(1,), in_specs=[VMEM] * len(args),
        out_specs=[VMEM] * len(out_shape), out_shape=out_shape,
        scratch_shapes=[pltpu.VMEM((PACK_ROWS, D), F32), pltpu.VMEM((NDEV, PACK_ROWS, D), F32),
                        pltpu.VMEM((PACK_ROWS, D), F32), _sems(NDEV - 1), _sems(NDEV - 1)],
        args=args, own_comm=True)
    res, side_outs = res if sides else (res, None)
    out = {name: tuple(res[4 * i:4 * i + 4]) for i, name in enumerate(SMALL)}
    loss = res[4 * ns][0, 0]
    return (out, loss, side_outs) if sides else (out, loss)


MATS = ("w_in", "w_o_attn", "w_pw_conv", "w_out", "w_ffn_in", "w_ffn_out")
TRANSPOSED = ("w_in", "w_ffn_in")
WEIGHTS = ("norm1_w", "w_in", "b_gate", "q_norm_w", "k_norm_w", "w_o_attn", "conv_w", "conv_b", "conv_ln_w",
           "conv_ln_b", "w_pw_conv", "w_out", "norm2_w", "w_ffn_in", "w_ffn_out")


def _blocks_to_cols(blocks):
    n, R, C = blocks.shape
    return blocks.transpose(1, 0, 2).reshape(R, n * C)


def kernel(x, positions, norm1_w, w_in, b_gate, q_norm_w, k_norm_w, w_o_attn, conv_w, conv_b, conv_ln_w, conv_ln_b, w_pw_conv, w_out, norm2_w, w_ffn_in, w_ffn_out, loss_target, m_norm1_w, m_w_in, m_b_gate, m_q_norm_w, m_k_norm_w, m_w_o_attn, m_conv_w, m_conv_b, m_conv_ln_w, m_conv_ln_b, m_w_pw_conv, m_w_out, m_norm2_w, m_w_ffn_in, m_w_ffn_out, v_norm1_w, v_w_in, v_b_gate, v_q_norm_w, v_k_norm_w, v_w_o_attn, v_conv_w, v_conv_b, v_conv_ln_w, v_conv_ln_b, v_w_pw_conv, v_w_out, v_norm2_w, v_w_ffn_in, v_w_ffn_out):
    w = dict(norm1_w=norm1_w, w_in=w_in, b_gate=b_gate, q_norm_w=q_norm_w, k_norm_w=k_norm_w, w_o_attn=w_o_attn,
             conv_w=conv_w, conv_b=conv_b, conv_ln_w=conv_ln_w, conv_ln_b=conv_ln_b, w_pw_conv=w_pw_conv,
             w_out=w_out, norm2_w=norm2_w, w_ffn_in=w_ffn_in, w_ffn_out=w_ffn_out)
    m = dict(norm1_w=m_norm1_w, w_in=m_w_in, b_gate=m_b_gate, q_norm_w=m_q_norm_w, k_norm_w=m_k_norm_w,
             w_o_attn=m_w_o_attn, conv_w=m_conv_w, conv_b=m_conv_b, conv_ln_w=m_conv_ln_w,
             conv_ln_b=m_conv_ln_b, w_pw_conv=m_w_pw_conv, w_out=m_w_out, norm2_w=m_norm2_w,
             w_ffn_in=m_w_ffn_in, w_ffn_out=m_w_ffn_out)
    v = dict(norm1_w=v_norm1_w, w_in=v_w_in, b_gate=v_b_gate, q_norm_w=v_q_norm_w, k_norm_w=v_k_norm_w,
             w_o_attn=v_w_o_attn, conv_w=v_conv_w, conv_b=v_conv_b, conv_ln_w=v_conv_ln_w,
             conv_ln_b=v_conv_ln_b, w_pw_conv=v_w_pw_conv, w_out=v_w_out, norm2_w=v_norm2_w,
             w_ffn_in=v_w_ffn_in, w_ffn_out=v_w_ffn_out)
    def two_d(t):
        t = {k: (a[0] if a.ndim == 3 else a) for k, a in t.items()}
        return {k: (a.T if k in TRANSPOSED else a) for k, a in t.items()}

    w, m, v = two_d(w), two_d(m), two_d(v)

    x2, target = x[0], loss_target[0]
    c_idx = lax.axis_index("c").astype(jnp.int32)
    chip_idx = (2 * lax.axis_index("x") + lax.axis_index("y")).astype(jnp.int32)
    qw2 = jnp.tile(w["q_norm_w"], (1, 2))
    kw2 = jnp.tile(w["k_norm_w"], (1, 2))

    tabs, ((w_in_blocks,), (bg_blocks,), (cw_blocks,)) = rope_tables(
        positions.reshape(S, 1),
        sides=(ag_blocks_relay(w["w_in"], BF16), ag_blocks(w["b_gate"], F32), ag_blocks(w["conv_w"], F32)))
    w_in_t = w_in_blocks.reshape(INW, D)
    b_gate_f, conv_w_f = _blocks_to_cols(bg_blocks), _blocks_to_cols(cw_blocks)
    h_t, proj = in_proj(x2, w["norm1_w"], w_in_t)
    (attn, lse), ((w_ffn_in_blocks,),) = attn_fwd(proj, tabs, qw2, kw2, sides=(ag_blocks_relay(w["w_ffn_in"], BF16),))
    w_ffn_in_t = w_ffn_in_blocks.reshape(2 * FF, D)
    (cpre, u3), ((w_o_f,), (w_pw_f,), (w_out_blocks,)) = conv_fwd(
        proj, conv_w_f, w["conv_b"], w["conv_ln_w"], w["conv_ln_b"],
        sides=(ag_cols(w["w_o_attn"]), ag_cols(w["w_pw_conv"]), ag_blocks_relay(w["w_out"], BF16)))
    w_out_f = w_out_blocks.reshape(D, D)
    x1, z, ya, yb = mix_out(x2, proj, b_gate_f, attn, u3, w_o_f, w_pw_f, w_out_f)
    (h2, gu, f), ((w_ffn_out_blocks,),) = ffn_in(x1, w["norm2_w"], w_ffn_in_t, sides=(ag_blocks_relay(w["w_ffn_out"], BF16),))
    w_ffn_out_f = w_ffn_out_blocks.reshape(FF, D)
    dy, dyb, sq = ffn_out_loss(x1, f, w_ffn_out_f, target)

    g = {}
    g_ffn_out = mm_tn("gw_ffn_out", f, dyb, FF // 2, D).reshape(NDEV, FF // NDEV, D)
    (d_gu, d_x1, d_x1b, g["norm2_w"]), ((ra_ffn_out,),) = ffn_bwd(
        dy, dyb, gu, x1, w["norm2_w"], w_ffn_in_t, w_ffn_out_f, sides=(rs_to_sibling([("rows", g_ffn_out)]),))
    pb_ffn_out, own_ffn_out = chip_sum("chip_sum_w_ffn_out", g_ffn_out, ra_ffn_out, c_idx, chip_idx)
    g_ffn_in, ((rb_ffn_out,),) = mm_tn("gw_ffn_in", d_gu, h2, FF // 2, D,
                                       sides=(rs_to_chips([("rows", pb_ffn_out)]),))
    g_ffn_in = g_ffn_in.reshape(NDEV, 2 * FF // NDEV, D)
    g_out = mm_tn("gw_out", z, d_x1b, D // 2, D).reshape(NDEV, D // NDEV, D)
    (d_ya, d_yb, d_gl, d_attn, d_u3, g["b_gate"]), ((ra_ffn_in,),) = out_bwd(
        d_x1b, proj, b_gate_f, ya, yb, w_o_f, w_pw_f, w_out_f, sides=(rs_to_sibling([("rows", g_ffn_in)]),))
    pb_ffn_in, own_ffn_in = chip_sum("chip_sum_w_ffn_in", g_ffn_in, ra_ffn_in, c_idx, chip_idx)
    g_w_o = mm_tn("gw_o_attn", attn, d_ya, CC, D)
    g_w_pw = mm_tn("gw_pw_conv", u3, d_yb, CC, D)
    (d_conv, g["conv_w"], g["conv_b"], g["conv_ln_w"], g["conv_ln_b"]), ((ra_out, ra_w_o, ra_w_pw),) = conv_bwd(
        proj, cpre, d_u3, conv_w_f, conv_w_f[::-1], w["conv_ln_w"], w["conv_ln_b"],
        sides=(rs_to_sibling([("rows", g_out), ("cols", g_w_o), ("cols", g_w_pw)]),))
    pb_out, own_out = chip_sum("chip_sum_w_out", g_out, ra_out, c_idx, chip_idx)
    pb_w_o, own_w_o = chip_sum_cols("chip_sum_w_o_attn", g_w_o, ra_w_o, c_idx, chip_idx)
    pb_w_pw, own_w_pw = chip_sum_cols("chip_sum_w_pw_conv", g_w_pw, ra_w_pw, c_idx, chip_idx)
    (d_q, d_k, d_v, gqw, gkw), ((rb_ffn_in, rb_out, rb_w_o, rb_w_pw),) = attn_bwd(
        proj, tabs, qw2, kw2, d_attn, attn, lse,
        sides=(rs_to_chips([("rows", pb_ffn_in), ("rows", pb_out), ("cols", pb_w_o), ("cols", pb_w_pw)]),))
    g["q_norm_w"] = gqw[0:1, 0:HD] + gqw[0:1, HD:LANES]
    g["k_norm_w"] = gkw[0:1, 0:HD] + gkw[0:1, HD:LANES]
    mine_w_o, theirs_w_o = col_final("col_final_w_o_attn", own_w_o, rb_w_o, c_idx)
    mine_w_pw, theirs_w_pw = col_final("col_final_w_pw_conv", own_w_pw, rb_w_pw, c_idx)
    d_segs = (d_q, d_k, d_v, d_conv, d_gl)
    parts, to_sibling, to_chips, owns, from_chips = [], None, None, [], []
    for k in range(GW_IN_PARTS):
        sides = [rs_swap_halves([theirs_w_o, theirs_w_pw])] if k == 0 else []
        sides += [s for s in (to_chips, to_sibling) if s is not None]
        part, outs = gw_in_t("gw_in_%d" % k, h_t, d_segs, k, sides=tuple(sides))
        if k == 0:
            (rc_w_o, rc_w_pw), outs = outs[0], outs[1:]
        outs = list(outs)
        if to_chips is not None:
            from_chips.append(outs.pop(0)[0])
        if to_sibling is not None:
            pb, own = chip_sum("chip_sum_w_in_%d" % (k - 1), parts[-1], outs.pop(0)[0], c_idx, chip_idx)
            owns.append(own)
            to_chips = rs_to_chips([("rows", pb)])
        else:
            to_chips = None
        parts.append(part.reshape(NDEV, INW // NDEV, D // GW_IN_PARTS))
        to_sibling = rs_to_sibling([("rows", parts[-1])])
    (grad_x, g["norm1_w"]), ((rb_prev,), (ra_last,)) = in_bwd(
        d_q, d_k, d_v, d_conv, d_gl, w_in_t, x2, d_x1, w["norm1_w"], sides=(to_chips, to_sibling))
    from_chips.append(rb_prev)
    pb, own = chip_sum("chip_sum_w_in_%d" % (GW_IN_PARTS - 1), parts[-1], ra_last, c_idx, chip_idx)
    owns.append(own)
    small, loss, ((rb_last,),) = small_sync_adam(g, w, m, v, sq, sides=(rs_to_chips([("rows", pb)]),))
    from_chips.append(rb_last)

    res = {
        "w_in": shard_adam("adam_w_in", owns, from_chips, w["w_in"], m["w_in"], v["w_in"]),
        "w_ffn_in": shard_adam("adam_w_ffn_in", [own_ffn_in], [rb_ffn_in], w["w_ffn_in"], m["w_ffn_in"], v["w_ffn_in"]),
        "w_o_attn": adam_cols("adam_w_o_attn", mine_w_o, rc_w_o, w["w_o_attn"], m["w_o_attn"], v["w_o_attn"], c_idx),
        "w_pw_conv": adam_cols("adam_w_pw_conv", mine_w_pw, rc_w_pw, w["w_pw_conv"], m["w_pw_conv"], v["w_pw_conv"], c_idx),
        "w_out": shard_adam("adam_w_out", [own_out], [rb_out], w["w_out"], m["w_out"], v["w_out"]),
        "w_ffn_out": shard_adam("adam_w_ffn_out", [own_ffn_out], [rb_ffn_out],
                                w["w_ffn_out"], m["w_ffn_out"], v["w_ffn_out"]),
    }
    res = {k: tuple(a.T if k in TRANSPOSED else a for a in r) for k, r in res.items()}
    res.update(small)

    def shaped(name, a):
        return a.reshape((1,) + a.shape) if name in MATS or name in ("b_gate", "conv_w") else a

    outs = [loss, grad_x.reshape(1, S, D)]
    for i in range(4):
        outs += [shaped(k, res[k][i]) for k in WEIGHTS]
    return tuple(outs)
```

```python
import functools
from typing import Callable, NamedTuple, Optional

import numpy as np
import jax
import jax.numpy as jnp
from jax import lax
from jax.experimental import pallas as pl
from jax.experimental.pallas import tpu as pltpu

F32 = jnp.float32
BF16 = jnp.bfloat16

S = 2048
D = 1024
HD = 64
QKV = 1536
CC = 512
KW = 31
FF = 2816
INW = 7680
OFF_Q, OFF_K, OFF_V, OFF_CA, OFF_CB, OFF_GA, OFF_GB = 0, 1536, 3072, 4608, 5120, 5632, 6656
DILATIONS = (1, 4, 16)
HALF_SPAN = 64
EPS = 1e-6
NEG_INF = -1e30
ROPE_THETA = 500000.0
ROT_DIM = 16

ADAM_LR = 0.001
ADAM_B1 = 0.9
ADAM_B2 = 0.999
ADAM_EPS = 1e-08
ADAM_WD = 0.01
ADAM_STEP = 10

NDEV = 8
LANES = 128
TM = 256
TQ = 128
VMEM_LIMIT = 56 * 1024 * 1024
MESH = pl.DeviceIdType.MESH


def _cp(**kw):
    return pltpu.CompilerParams(vmem_limit_bytes=VMEM_LIMIT, **kw)


def _row(width, col=0, tm=TM):
    return pl.BlockSpec((tm, width), lambda i: (i, col))


PLANE = 512


def _planes(width, tm=TM):
    return pl.BlockSpec((width // PLANE, tm, PLANE), lambda i: (0, i, 0))


def _res(shape):
    nd = len(shape)
    return pl.BlockSpec(shape, lambda *_: (0,) * nd, pipeline_mode=pl.Buffered(1))


def _dot(a, b):
    return jnp.dot(a, b, preferred_element_type=F32)


def _dot_nt(a, b):
    return lax.dot_general(a, b, (((1,), (1,)), ((), ())), preferred_element_type=F32)


def _dot_tn(a, b):
    return lax.dot_general(a, b, (((0,), (0,)), ((), ())), preferred_element_type=F32)


def _sigmoid(x):
    return jax.nn.sigmoid(x)


def _dsilu(x, sg):
    return sg * (1.0 + x * (1.0 - sg))


ANY = pl.BlockSpec(memory_space=pl.ANY)
VMEM = pl.BlockSpec(memory_space=pltpu.VMEM)


class Side(NamedTuple):
    args: tuple
    in_specs: tuple
    out_shape: tuple
    scratch: tuple
    start: Callable
    finish: Callable
    mid: Optional[Callable] = None
    peers: str = ""


BARRIER_IDS = {"s": 0, "dxy": 1, "dsxy": 2, "sxy": 3}


def _peer_barrier(peers):
    x, y, c = lax.axis_index("x"), lax.axis_index("y"), lax.axis_index("c")
    where = {"s": (x, y, 1 - c), "x": (1 - x, y, c), "y": (x, 1 - y, c), "d": (1 - x, 1 - y, c)}
    barrier = pltpu.get_barrier_semaphore()
    for p in peers:
        pl.semaphore_signal(barrier, inc=1, device_id=where[p], device_id_type=MESH)
    pl.semaphore_wait(barrier, len(peers))


def _call(body, sides=(), *, name, grid, in_specs, out_specs, out_shape, scratch_shapes=(), args, own_comm=False):
    ni, no, ns = len(in_specs), len(out_specs), len(scratch_shapes)
    cnt = [(len(s.args), len(s.out_shape), len(s.scratch)) for s in sides]
    peers = "".join(sorted(set("".join(s.peers for s in sides))))
    if own_comm or not sides or any(not s.peers for s in sides):
        peers = ""

    def take(refs, pos, n):
        return refs[pos:pos + n], pos + n

    def full(*refs):
        m_in, pos = take(refs, 0, ni)
        s_in = []
        for a, _, _ in cnt:
            r, pos = take(refs, pos, a)
            s_in.append(r)
        m_out, pos = take(refs, pos, no)
        s_out = []
        for _, o, _ in cnt:
            r, pos = take(refs, pos, o)
            s_out.append(r)
        m_scr, pos = take(refs, pos, ns)
        s_scr = []
        for _, _, c in cnt:
            r, pos = take(refs, pos, c)
            s_scr.append(r)
        if sides:
            first = functools.reduce(jnp.logical_and, [pl.program_id(d) == 0 for d in range(len(grid))])
            last = functools.reduce(jnp.logical_and, [pl.program_id(d) == g - 1 for d, g in enumerate(grid)])

            @pl.when(first)
            def _():
                if peers:
                    _peer_barrier(peers)
                for s, a, o, c in zip(sides, s_in, s_out, s_scr):
                    s.start(a, o, c)

            steps = int(np.prod(grid))
            mid_step = (2 * steps) // 3
            if steps > 1 and any(s.mid is not None for s in sides):
                step = functools.reduce(lambda acc, d: acc * grid[d] + pl.program_id(d), range(len(grid)), 0)

                @pl.when(step == mid_step)
                def _():
                    for s, a, o, c in zip(sides, s_in, s_out, s_scr):
                        if s.mid is not None:
                            s.mid(a, o, c)

        body(*m_in, *m_out, *m_scr)
        if sides:
            @pl.when(last)
            def _():
                for s, a, o, c in zip(sides, s_in, s_out, s_scr):
                    if s.mid is not None and steps == 1:
                        s.mid(a, o, c)
                    s.finish(a, o, c)

    res = pl.pallas_call(
        full, name=name, grid=grid,
        in_specs=list(in_specs) + [sp for s in sides for sp in s.in_specs],
        out_specs=list(out_specs) + [ANY for s in sides for _ in s.out_shape],
        out_shape=list(out_shape) + [o for s in sides for o in s.out_shape],
        scratch_shapes=list(scratch_shapes) + [c for s in sides for c in s.scratch],
        compiler_params=_cp(dimension_semantics=("arbitrary",) * len(grid),
                            **({"collective_id": BARRIER_IDS[peers]} if peers else {})),
    )(*args, *[a for s in sides for a in s.args])
    res = list(res)
    if not sides:
        return res
    outs, pos = take(res, 0, no)
    side_outs = []
    for _, o, _ in cnt:
        r, pos = take(res, pos, o)
        side_outs.append(r)
    return outs, side_outs


def _inv_freq_lanes():
    inv = np.float32(ROPE_THETA) ** (-np.arange(0, ROT_DIM, 2, dtype=np.float32) / np.float32(ROT_DIM))
    lane = np.arange(LANES) % HD
    out = np.where(lane < ROT_DIM, inv[lane % (ROT_DIM // 2)], 0.0).astype(np.float32)
    return jnp.asarray(out.reshape(1, LANES))


def rope_tables(pos_col, sides=()):
    def body(p_ref, f_ref, c_ref, s1_ref, s2_ref):
        ang = p_ref[...].astype(F32) * f_ref[...]
        lane = lax.broadcasted_iota(jnp.int32, ang.shape, 1) % HD
        cs = jnp.cos(ang)
        sn = jnp.sin(ang)
        c_ref[...] = jnp.where(lane < ROT_DIM, cs, 1.0)
        s1_ref[...] = jnp.where(lane < ROT_DIM // 2, -sn, 0.0)
        s2_ref[...] = jnp.where(lane < ROT_DIM // 2, 0.0, jnp.where(lane < ROT_DIM, sn, 0.0))

    sds = jax.ShapeDtypeStruct((S, LANES), F32)
    return _call(
        body, sides, name="rope_tables", grid=(S // TM,),
        in_specs=[_row(1), pl.BlockSpec((1, LANES), lambda i: (0, 0))],
        out_specs=[_row(LANES)] * 3, out_shape=[sds] * 3,
        args=(pos_col, _inv_freq_lanes()))


def _rope(v, c, s1, s2):
    return v * c + pltpu.roll(v, LANES - 8, axis=1) * s1 + pltpu.roll(v, 8, axis=1) * s2


def _rope_t(d, c, s1, s2):
    return d * c - pltpu.roll(d, LANES - 8, axis=1) * s1 - pltpu.roll(d, 8, axis=1) * s2


def _head_mat():
    r = lax.broadcasted_iota(jnp.int32, (LANES, LANES), 0) // HD
    c = lax.broadcasted_iota(jnp.int32, (LANES, LANES), 1) // HD
    return jnp.where(r == c, 1.0 / HD, 0.0).astype(BF16)


def _head_mean(t, e):
    hi = t.astype(BF16)
    rest = (t - hi.astype(F32)).astype(BF16)
    return _dot(hi, e) + _dot(rest, e)


def in_proj(x, norm_w, w_in, sides=()):
    nchunk = 5
    cw = INW // nchunk

    def body(x_ref, nw_ref, w_ref, ht_ref, p_ref):
        xv = x_ref[...]
        r = lax.rsqrt(jnp.mean(xv * xv, axis=-1, keepdims=True) + EPS)
        hf = xv * r * nw_ref[...]
        ht_ref[...] = hf.T.astype(BF16)
        h = hf.astype(BF16)
        for j in range(nchunk):
            p_ref[:, j * cw:(j + 1) * cw] = _dot_nt(h, w_ref[j * cw:(j + 1) * cw, :])

    return _call(
        body, sides, name="in_proj", grid=(S // TM,),
        in_specs=[_row(D), _res((1, D)), _res((INW, D))],
        out_specs=[pl.BlockSpec((D, TM), lambda i: (0, i)), _row(INW)],
        out_shape=[jax.ShapeDtypeStruct((D, S), BF16), jax.ShapeDtypeStruct((S, INW), F32)],
        args=(x, norm_w, w_in))


def _qk_specs():
    nb = QKV // LANES
    return [pl.BlockSpec((S, LANES), functools.partial(lambda hp, g, o: (0, o + g * 4 + hp), o=o))
            for o in (OFF_Q // LANES, OFF_K // LANES, OFF_V // LANES)]


def _tab_specs():
    return [pl.BlockSpec((S, LANES), lambda hp, g: (0, 0), pipeline_mode=pl.Buffered(1))] * 3


def _vec_spec():
    return pl.BlockSpec((1, LANES), lambda hp, g: (0, 0))


def _sub_rows(r, d, start, n):
    if d == 1:
        return pl.ds(start, n)
    return pl.ds(r + d * start, n, stride=d)


def _band_window(i, L):
    W = min(TQ + 2 * HALF_SPAN, L)
    q0 = pl.multiple_of(i * TQ, TQ)
    k0 = pl.multiple_of(jnp.clip(q0 - HALF_SPAN, 0, L - W), HALF_SPAN)
    qpos = q0 + (lax.broadcasted_iota(jnp.int32, (2 * TQ, W), 0) & (TQ - 1))
    kpos = k0 + lax.broadcasted_iota(jnp.int32, (2 * TQ, W), 1)
    valid = jnp.abs(qpos - kpos) <= HALF_SPAN
    return W, q0, k0, valid


def _stack_heads(t, lo):
    z = jnp.zeros_like(t)
    return jnp.concatenate([jnp.where(lo, t, z), jnp.where(lo, z, t)], axis=0)


def _unstack_heads(t2, lo):
    return jnp.where(lo, t2[0:TQ], t2[TQ:2 * TQ])


CHAINS = 8


def _interleave(d):
    ru = min(d, CHAINS)
    return ru, min(CHAINS // ru, S // d // TQ)


def _for_blocks(n, fn):
    if n == 1:
        fn(0)
    else:
        def it(j, _):
            fn(j)
            return 0
        lax.fori_loop(0, n, it, 0)


def attn_fwd(proj, tabs, qw2, kw2, sides=()):
    CH = 256

    def body(q_ref, k_ref, v_ref, c_ref, s1_ref, s2_ref, qw_ref, kw_ref, at_ref, ls_ref,
             qs, ks, vs, osub, lsub, onat, lnat, qn, kn):
        g = pl.program_id(1)
        lo = lax.broadcasted_iota(jnp.int32, (1, LANES), 1) < HD
        e = _head_mat()

        def prep(i, _):
            rows = pl.ds(pl.multiple_of(i * CH, CH), CH)
            c, s1, s2 = c_ref[rows, :], s1_ref[rows, :], s2_ref[rows, :]
            for t_ref, w_ref, out, scale in ((q_ref, qw_ref, qn, HD ** -0.5), (k_ref, kw_ref, kn, 1.0)):
                t = t_ref[rows, :]
                r = lax.rsqrt(_head_mean(t * t, e) + EPS)
                out[rows, :] = _rope(t * r * w_ref[...], c, s1, s2) * scale
            return 0

        lax.fori_loop(0, S // CH, prep, 0, unroll=4)

        def group(gi, d):
            L = S // d

            ru, nb = _interleave(d)

            def stage(r, off):
                for c0 in range(0, L, CH):
                    n = min(CH, L)
                    rows = _sub_rows(r, d, c0, n)
                    dst = pl.ds(off + c0, n)
                    qs[dst, :] = qn[rows, :].astype(BF16)
                    ks[dst, :] = kn[rows, :].astype(BF16)
                    vs[dst, :] = v_ref[rows, :].astype(BF16)

            def one(off, i):
                W, q0, k0, valid = _band_window(i, L)
                q2 = _stack_heads(qs[pl.ds(off + q0, TQ), :], lo)
                sc = jnp.where(valid, _dot_nt(q2, ks[pl.ds(off + k0, W), :]), NEG_INF)
                m = jnp.max(sc, axis=-1, keepdims=True)
                p = jnp.exp(sc - m)
                den = jnp.sum(p, axis=-1, keepdims=True)
                o2 = _dot(p.astype(BF16), vs[pl.ds(off + k0, W), :]) / den
                l2 = jnp.broadcast_to(m + jnp.log(den), (2 * TQ, LANES))
                osub[pl.ds(off + q0, TQ), :] = _unstack_heads(o2, lo)
                lsub[pl.ds(off + q0, TQ), :] = _unstack_heads(l2, lo)

            def unstage(r, off):
                for c0 in range(0, L, CH):
                    n = min(CH, L)
                    rows = _sub_rows(r, d, c0, n)
                    onat[gi, rows, :] = osub[pl.ds(off + c0, n), :]
                    lnat[gi, rows, :] = lsub[pl.ds(off + c0, n), :]

            def step(t, _):
                for u in range(ru):
                    stage(t * ru + u, u * L)
                _for_blocks(L // TQ // nb, lambda j: [one(u * L, j * nb + b) for u in range(ru) for b in range(nb)])
                for u in range(ru):
                    unstage(t * ru + u, u * L)
                return 0

            lax.fori_loop(0, d // ru, step, 0)

        for gi, d in enumerate(DILATIONS):
            pl.when(g == gi)(functools.partial(group, gi, d))

        @pl.when(g == len(DILATIONS) - 1)
        def _():
            def mix(i, _):
                rows = pl.ds(pl.multiple_of(i * CH, CH), CH)
                l0, l1, l2 = lnat[0, rows, :], lnat[1, rows, :], lnat[2, rows, :]
                m = jnp.maximum(jnp.maximum(l0, l1), l2)
                e0, e1, e2 = jnp.exp(l0 - m), jnp.exp(l1 - m), jnp.exp(l2 - m)
                den = e0 + e1 + e2
                a = (e0 * onat[0, rows, :] + e1 * onat[1, rows, :] + e2 * onat[2, rows, :]) / den
                at_ref[rows, :] = a.astype(BF16)
                ls_ref[rows, :] = m + jnp.log(den)
                return 0

            lax.fori_loop(0, S // CH, mix, 0)

    out_spec = pl.BlockSpec((S, LANES), lambda hp, g: (0, hp))
    return _call(
        body, sides, name="attn_fwd", grid=(4, 3),
        in_specs=_qk_specs() + _tab_specs() + [_vec_spec(), _vec_spec()],
        out_specs=[out_spec, out_spec],
        out_shape=[jax.ShapeDtypeStruct((S, CC), BF16), jax.ShapeDtypeStruct((S, CC), F32)],
        scratch_shapes=[pltpu.VMEM((S, LANES), BF16)] * 3 + [pltpu.VMEM((S, LANES), F32)] * 2
        + [pltpu.VMEM((3, S, LANES), F32)] * 2 + [pltpu.VMEM((S, LANES), F32)] * 2,
        args=(proj, proj, proj, *tabs, qw2, kw2))


def attn_bwd(proj, tabs, qw2, kw2, d_attn, attn, lse, sides=()):
    CH = 256

    def body(q_ref, k_ref, v_ref, c_ref, s1_ref, s2_ref, qw_ref, kw_ref, do_ref, at_ref, ls_ref,
             dq_ref, dk_ref, dv_ref, gqw_ref, gkw_ref,
             qs, ks, vs, dos, dsub, lsub, dqs, dks, dvs, dnat, qx, kx, dvn, tnq, tnk, rrq, rrk):
        hp, g = pl.program_id(0), pl.program_id(1)
        lo = lax.broadcasted_iota(jnp.int32, (1, LANES), 1) < HD
        e = _head_mat()
        both = ((q_ref, qw_ref, qx, tnq, rrq, HD ** -0.5), (k_ref, kw_ref, kx, tnk, rrk, 1.0))

        @pl.when((hp == 0) & (g == 0))
        def _():
            gqw_ref[...] = jnp.zeros_like(gqw_ref)
            gkw_ref[...] = jnp.zeros_like(gkw_ref)

        def prep(i, _):
            rows = pl.ds(pl.multiple_of(i * CH, CH), CH)
            dnat[rows, :] = _head_mean(do_ref[rows, :] * at_ref[rows, :].astype(F32), e) * float(HD)
            c, s1, s2 = c_ref[rows, :], s1_ref[rows, :], s2_ref[rows, :]
            for t_ref, w_ref, x, tn_s, rr_s, scale in both:
                t = t_ref[rows, :]
                rr = lax.rsqrt(_head_mean(t * t, e) + EPS)
                tn = t * rr
                rr_s[rows, :] = rr
                tn_s[rows, :] = tn
                x[rows, :] = _rope(tn * w_ref[...], c, s1, s2) * scale
            return 0

        lax.fori_loop(0, S // CH, prep, 0, unroll=4)

        def group(d):
            L = S // d

            ru, nb = _interleave(d)

            def stage(r, off):
                for c0 in range(0, L, CH):
                    n = min(CH, L)
                    rows = _sub_rows(r, d, c0, n)
                    dst = pl.ds(off + c0, n)
                    qs[dst, :] = qx[rows, :].astype(BF16)
                    ks[dst, :] = kx[rows, :].astype(BF16)
                    vs[dst, :] = v_ref[rows, :].astype(BF16)
                    dos[dst, :] = do_ref[rows, :].astype(BF16)
                    dsub[dst, :] = dnat[rows, :]
                    lsub[dst, :] = ls_ref[rows, :]
                    dks[dst, :] = jnp.zeros((n, LANES), F32)
                    dvs[dst, :] = jnp.zeros((n, LANES), F32)

            def one(off, i):
                W, q0, k0, valid = _band_window(i, L)
                qrows, krows = pl.ds(off + q0, TQ), pl.ds(off + k0, W)
                q2 = _stack_heads(qs[qrows, :], lo)
                do2 = _stack_heads(dos[qrows, :], lo)
                kk, vv = ks[krows, :], vs[krows, :]
                lse_b, dd_b = lsub[qrows, :], dsub[qrows, :]
                lse2 = jnp.concatenate([lse_b[:, 0:1], lse_b[:, HD:HD + 1]], axis=0)
                dd2 = jnp.concatenate([dd_b[:, 0:1], dd_b[:, HD:HD + 1]], axis=0)
                sc = jnp.where(valid, _dot_nt(q2, kk), NEG_INF)
                p = jnp.exp(sc - lse2)
                ds = (p * (_dot_nt(do2, vv) - dd2)).astype(BF16)
                dqs[qrows, :] = _unstack_heads(_dot(ds, kk), lo)
                dks[krows, :] = dks[krows, :] + _dot_tn(ds, q2)
                dvs[krows, :] = dvs[krows, :] + _dot_tn(p.astype(BF16), do2)

            def unstage(r, off):
                for c0 in range(0, L, CH):
                    n = min(CH, L)
                    rows = _sub_rows(r, d, c0, n)
                    src = pl.ds(off + c0, n)
                    qx[rows, :] = dqs[src, :]
                    kx[rows, :] = dks[src, :]
                    dvn[rows, :] = dvs[src, :]

            def step(t, _):
                for u in range(ru):
                    stage(t * ru + u, u * L)
                _for_blocks(L // TQ // nb, lambda j: [one(u * L, j * nb + b) for u in range(ru) for b in range(nb)])
                for u in range(ru):
                    unstage(t * ru + u, u * L)
                return 0

            lax.fori_loop(0, d // ru, step, 0)

        for gi, d in enumerate(DILATIONS):
            pl.when(g == gi)(functools.partial(group, d))

        def emit(i, _):
            rows = pl.ds(pl.multiple_of(i * CH, CH), CH)
            c, s1, s2 = c_ref[rows, :], s1_ref[rows, :], s2_ref[rows, :]
            for (_, w_ref, x, tn_s, rr_s, scale), out, gw_ref in zip(both, (dq_ref, dk_ref), (gqw_ref, gkw_ref)):
                tn = tn_s[rows, :]
                dy = _rope_t(x[rows, :] * scale, c, s1, s2)
                gw_ref[0:1, :] = gw_ref[0:1, :] + jnp.sum(dy * tn, axis=0, keepdims=True)
                dtn = dy * w_ref[...]
                out[rows, :] = (rr_s[rows, :] * (dtn - tn * _head_mean(dtn * tn, e))).astype(BF16)
            dv_ref[rows, :] = dvn[rows, :].astype(BF16)
            return 0

        lax.fori_loop(0, S // CH, emit, 0, unroll=4)

    nat_spec = pl.BlockSpec((S, LANES), lambda hp, g: (0, hp))
    out_spec = pl.BlockSpec((None, S, LANES), lambda hp, g: (g, 0, hp))
    acc_spec = pl.BlockSpec((8, LANES), lambda hp, g: (0, 0))
    return _call(
        body, sides, name="attn_bwd", grid=(4, 3),
        in_specs=_qk_specs() + _tab_specs() + [_vec_spec(), _vec_spec(), nat_spec, nat_spec, nat_spec],
        out_specs=[out_spec] * 3 + [acc_spec] * 2,
        out_shape=[jax.ShapeDtypeStruct((QKV // PLANE, S, PLANE), BF16)] * 3 + [jax.ShapeDtypeStruct((8, LANES), F32)] * 2,
        scratch_shapes=[pltpu.VMEM((S, LANES), BF16)] * 4 + [pltpu.VMEM((S, LANES), F32)] * 13,
        args=(proj, proj, proj, *tabs, qw2, kw2, d_attn, attn, lse))


PADR = 16
CT = 128


def _conv_specs():
    return [pl.BlockSpec((S, CC), lambda i: (0, OFF_CA // CC)), pl.BlockSpec((S, CC), lambda i: (0, OFF_CB // CC))]


NCB = CC // LANES


def _pad_zero(pad):
    for cb in range(NCB):
        pad[cb, 0:PADR, :] = jnp.zeros((PADR, LANES), F32)
        pad[cb, PADR + S:PADR + S + PADR, :] = jnp.zeros((PADR, LANES), F32)


def _pad_store(pad, row0, n, val):
    for cb in range(NCB):
        pad[cb, pl.ds(pl.multiple_of(row0 + PADR, 8), n), :] = val[:, cb * LANES:(cb + 1) * LANES]


def _taps(pad_ref, cb, s0, weights):
    acc = jnp.zeros((CT, LANES), F32)
    for k in range(KW):
        acc = acc + weights[k] * pad_ref[cb, pl.ds(s0 + k + 1, CT), :]
    return acc


def conv_fwd(proj, conv_w, conv_b, ln_w, ln_b, sides=()):
    def body(a_ref, b_ref, w_ref, cb_ref, lw_ref, lb_ref, c_ref, u3_ref, upad):
        _pad_zero(upad)

        def glu(i, _):
            rows = pl.ds(pl.multiple_of(i * TM, TM), TM)
            _pad_store(upad, i * TM, TM, a_ref[rows, :] * _sigmoid(b_ref[rows, :]))
            return 0

        lax.fori_loop(0, S // TM, glu, 0)

        def chunk(i, _):
            s0 = pl.multiple_of(i * CT, CT)
            for cb in range(CC // LANES):
                cols = slice(cb * LANES, (cb + 1) * LANES)
                w = [w_ref[k:k + 1, cols] for k in range(KW)]
                c_ref[pl.ds(s0, CT), cols] = _taps(upad, cb, s0, w) + cb_ref[:, cols]
            cv = c_ref[pl.ds(s0, CT), :]
            mu = jnp.mean(cv, axis=-1, keepdims=True)
            xc = cv - mu
            rstd = lax.rsqrt(jnp.mean(xc * xc, axis=-1, keepdims=True) + EPS)
            yl = xc * rstd * lw_ref[...] + lb_ref[...]
            u3_ref[pl.ds(s0, CT), :] = (yl * _sigmoid(yl)).astype(BF16)
            return 0

        lax.fori_loop(0, S // CT, chunk, 0)

    vec = pl.BlockSpec((1, CC), lambda i: (0, 0))
    full = pl.BlockSpec((S, CC), lambda i: (0, 0))
    return _call(
        body, sides, name="conv_fwd", grid=(1,),
        in_specs=_conv_specs() + [pl.BlockSpec((KW, CC), lambda i: (0, 0)), vec, vec, vec],
        out_specs=[full, full],
        out_shape=[jax.ShapeDtypeStruct((S, CC), F32), jax.ShapeDtypeStruct((S, CC), BF16)],
        scratch_shapes=[pltpu.VMEM((NCB, S + 2 * PADR, LANES), F32)],
        args=(proj, proj, conv_w, conv_b, ln_w, ln_b))


def conv_bwd(proj, cpre, d_u3, conv_w, conv_w_rev, ln_w, ln_b, sides=()):
    def body(a_ref, b_ref, c_ref, du3_ref, w_ref, wr_ref, lw_ref, lb_ref,
             dc_ref, gw_ref, gcb_ref, glw_ref, glb_ref, upad, dpad):
        _pad_zero(upad)
        _pad_zero(dpad)
        gw_ref[...] = jnp.zeros_like(gw_ref)

        def ln_bwd(i, carry):
            gcb, glw, glb = carry
            rows = pl.ds(pl.multiple_of(i * TM, TM), TM)
            _pad_store(upad, i * TM, TM, a_ref[rows, :] * _sigmoid(b_ref[rows, :]))
            cv = c_ref[rows, :]
            mu = jnp.mean(cv, axis=-1, keepdims=True)
            xc = cv - mu
            rstd = lax.rsqrt(jnp.mean(xc * xc, axis=-1, keepdims=True) + EPS)
            xh = xc * rstd
            yl = xh * lw_ref[...] + lb_ref[...]
            dyl = du3_ref[rows, :] * _dsilu(yl, _sigmoid(yl))
            dxh = dyl * lw_ref[...]
            dcv = rstd * (dxh - jnp.mean(dxh, axis=-1, keepdims=True)
                          - xh * jnp.mean(dxh * xh, axis=-1, keepdims=True))
            _pad_store(dpad, i * TM, TM, dcv)
            return (gcb + jnp.sum(dcv, axis=0, keepdims=True),
                    glw + jnp.sum(dyl * xh, axis=0, keepdims=True),
                    glb + jnp.sum(dyl, axis=0, keepdims=True))

        z = jnp.zeros((1, CC), F32)
        gcb, glw, glb = lax.fori_loop(0, S // TM, ln_bwd, (z, z, z))
        gcb_ref[...] = gcb
        glw_ref[...] = glw
        glb_ref[...] = glb

        def chunk(i, _):
            s0 = pl.multiple_of(i * CT, CT)
            for cb in range(CC // LANES):
                cols = slice(cb * LANES, (cb + 1) * LANES)
                wr = [wr_ref[k:k + 1, cols] for k in range(KW)]
                du = _taps(dpad, cb, s0, wr)
                dcv = dpad[cb, pl.ds(s0 + PADR, CT), :]
                for k in range(KW):
                    gw_ref[k:k + 1, cols] = gw_ref[k:k + 1, cols] + jnp.sum(
                        upad[cb, pl.ds(s0 + k + 1, CT), :] * dcv, axis=0, keepdims=True)
                av = a_ref[pl.ds(s0, CT), cols]
                sb = _sigmoid(b_ref[pl.ds(s0, CT), cols])
                dc_ref[0, pl.ds(s0, CT), cols] = (du * sb).astype(BF16)
                dc_ref[1, pl.ds(s0, CT), cols] = (du * av * sb * (1.0 - sb)).astype(BF16)
            return 0

        lax.fori_loop(0, S // CT, chunk, 0)

    vec = pl.BlockSpec((1, CC), lambda i: (0, 0))
    full = pl.BlockSpec((S, CC), lambda i: (0, 0))
    wsp = pl.BlockSpec((KW, CC), lambda i: (0, 0))
    return _call(
        body, sides, name="conv_bwd", grid=(1,),
        in_specs=_conv_specs() + [full, full, wsp, wsp, vec, vec],
        out_specs=[pl.BlockSpec((2, S, CC), lambda i: (0, 0, 0)), wsp, vec, vec, vec],
        out_shape=[jax.ShapeDtypeStruct((2, S, CC), BF16), jax.ShapeDtypeStruct((KW, CC), F32)]
        + [jax.ShapeDtypeStruct((1, CC), F32)] * 3,
        scratch_shapes=[pltpu.VMEM((NCB, S + 2 * PADR, LANES), F32)] * 2,
        args=(proj, proj, cpre, d_u3, conv_w, conv_w_rev, ln_w, ln_b))


def _gate_specs():
    return [_row(CC, col=OFF_GA // CC + j) for j in range(4)]


def _gates(g_refs, bg_ref):
    ga = _sigmoid(jnp.concatenate([g_refs[0][...], g_refs[1][...]], axis=1) + bg_ref[0:1, :])
    gb = _sigmoid(jnp.concatenate([g_refs[2][...], g_refs[3][...]], axis=1) + bg_ref[1:2, :])
    return ga, gb


def mix_out(x, proj, b_gate, attn, u3, w_o, w_pw, w_out):
    def body(x_ref, g0, g1, g2, g3, bg_ref, at_ref, u3_ref, wo_ref, wp_ref, wout_ref,
             x1_ref, z_ref, ya_ref, yb_ref):
        ga, gb = _gates((g0, g1, g2, g3), bg_ref)
        ya = _dot(at_ref[...], wo_ref[...])
        yb = _dot(u3_ref[...], wp_ref[...])
        z = (ga * ya + gb * yb).astype(BF16)
        ya_ref[...] = ya.astype(BF16)
        yb_ref[...] = yb.astype(BF16)
        z_ref[...] = z
        x1_ref[...] = x_ref[...] + _dot(z, wout_ref[...])

    return pl.pallas_call(
        body, name="mix_out", grid=(S // TM,),
        in_specs=[_row(D)] + _gate_specs() + [_res((2, D)), _row(CC), _row(CC),
                                              _res((CC, D)), _res((CC, D)), _res((D, D))],
        out_specs=[_row(D)] * 4,
        out_shape=[jax.ShapeDtypeStruct((S, D), F32)] + [jax.ShapeDtypeStruct((S, D), BF16)] * 3,
        compiler_params=_cp(dimension_semantics=("arbitrary",)),
    )(x, proj, proj, proj, proj, b_gate, attn, u3, w_o, w_pw, w_out)


def out_bwd(d_x1b, proj, b_gate, ya, yb, w_o, w_pw, w_out, sides=()):
    def body(dx_ref, g0, g1, g2, g3, bg_ref, ya_ref, yb_ref, wo_ref, wp_ref, wout_ref,
             dya_ref, dyb_ref, dgl_ref, dat_ref, du3_ref, gbg_ref):
        @pl.when(pl.program_id(0) == 0)
        def _():
            gbg_ref[...] = jnp.zeros_like(gbg_ref)

        ga, gb = _gates((g0, g1, g2, g3), bg_ref)
        dz = _dot_nt(dx_ref[...], wout_ref[...])
        dya = (dz * ga).astype(BF16)
        dyb = (dz * gb).astype(BF16)
        dgla = dz * ya_ref[...].astype(F32) * ga * (1.0 - ga)
        dglb = dz * yb_ref[...].astype(F32) * gb * (1.0 - gb)
        dya_ref[...] = dya
        dyb_ref[...] = dyb
        for j in range(2):
            dgl_ref[j] = dgla[:, j * PLANE:(j + 1) * PLANE].astype(BF16)
            dgl_ref[2 + j] = dglb[:, j * PLANE:(j + 1) * PLANE].astype(BF16)
        gbg_ref[0:1, :] = gbg_ref[0:1, :] + jnp.sum(dgla, axis=0, keepdims=True)
        gbg_ref[1:2, :] = gbg_ref[1:2, :] + jnp.sum(dglb, axis=0, keepdims=True)
        dat_ref[...] = _dot_nt(dya, wo_ref[...])
        du3_ref[...] = _dot_nt(dyb, wp_ref[...])

    return _call(
        body, sides, name="out_bwd", grid=(S // TM,),
        in_specs=[_row(D)] + _gate_specs() + [_res((2, D)), _row(D), _row(D),
                                              _res((CC, D)), _res((CC, D)), _res((D, D))],
        out_specs=[_row(D), _row(D), _planes(2 * D), _row(CC), _row(CC), pl.BlockSpec((2, D), lambda i: (0, 0))],
        out_shape=[jax.ShapeDtypeStruct((S, D), BF16)] * 2 + [jax.ShapeDtypeStruct((2 * D // PLANE, S, PLANE), BF16)]
        + [jax.ShapeDtypeStruct((S, CC), F32)] * 2 + [jax.ShapeDtypeStruct((2, D), F32)],
        args=(d_x1b, proj, proj, proj, proj, b_gate, ya, yb, w_o, w_pw, w_out))


def ffn_in(x1, norm_w, w_ffn_in, sides=()):
    half = FF // 2

    def body(x_ref, nw_ref, w_ref, h_ref, gu_ref, f_ref):
        xv = x_ref[...]
        r = lax.rsqrt(jnp.mean(xv * xv, axis=-1, keepdims=True) + EPS)
        h = (xv * r * nw_ref[...]).astype(BF16)
        h_ref[...] = h
        for j in range(2):
            gt = _dot_nt(h, w_ref[j * half:(j + 1) * half, :])
            up = _dot_nt(h, w_ref[FF + j * half:FF + (j + 1) * half, :])
            gu_ref[:, j * half:(j + 1) * half] = gt.astype(BF16)
            gu_ref[:, FF + j * half:FF + (j + 1) * half] = up.astype(BF16)
            f_ref[:, j * half:(j + 1) * half] = (gt * _sigmoid(gt) * up).astype(BF16)

    return _call(
        body, sides, name="ffn_in", grid=(S // TM,),
        in_specs=[_row(D), _res((1, D)), _res((2 * FF, D))],
        out_specs=[_row(D), _row(2 * FF), _row(FF)],
        out_shape=[jax.ShapeDtypeStruct((S, D), BF16), jax.ShapeDtypeStruct((S, 2 * FF), BF16),
                   jax.ShapeDtypeStruct((S, FF), BF16)],
        args=(x1, norm_w, w_ffn_in))


def ffn_out_loss(x1, f, w_ffn_out, target):
    def body(x_ref, f_ref, w_ref, t_ref, dy_ref, dyb_ref, sq_ref):
        @pl.when(pl.program_id(0) == 0)
        def _():
            sq_ref[...] = jnp.zeros_like(sq_ref)

        diff = x_ref[...] + _dot(f_ref[...], w_ref[...]) - t_ref[...]
        dy = diff * (1.0 / D)
        dy_ref[...] = dy
        dyb_ref[...] = dy.astype(BF16)
        sq_ref[...] = sq_ref[...] + jnp.sum((diff * diff).reshape(TM // 8, 8, D), axis=0)

    return pl.pallas_call(
        body, name="ffn_out_loss", grid=(S // TM,),
        in_specs=[_row(D), _row(FF), _res((FF, D)), _row(D)],
        out_specs=[_row(D), _row(D), pl.BlockSpec((8, D), lambda i: (0, 0))],
        out_shape=[jax.ShapeDtypeStruct((S, D), F32), jax.ShapeDtypeStruct((S, D), BF16),
                   jax.ShapeDtypeStruct((8, D), F32)],
        compiler_params=_cp(dimension_semantics=("arbitrary",)),
    )(x1, f, w_ffn_out, target)


def _rms_bwd(xv, nw, dh):
    r = lax.rsqrt(jnp.mean(xv * xv, axis=-1, keepdims=True) + EPS)
    xn = xv * r
    dxn = dh * nw
    dx = r * (dxn - xn * jnp.mean(dxn * xn, axis=-1, keepdims=True))
    return dx, dh * xn


def ffn_bwd(dy, dyb, gu, x1, norm_w, w_ffn_in, w_ffn_out, sides=()):
    def body(dy_ref, dyb_ref, gu_ref, x_ref, nw_ref, wi_ref, wo_ref, dgu_ref, dx_ref, dxb_ref, gn_ref):
        @pl.when(pl.program_id(0) == 0)
        def _():
            gn_ref[...] = jnp.zeros_like(gn_ref)

        df = _dot_nt(dyb_ref[...], wo_ref[...])
        gt = gu_ref[:, 0:FF].astype(F32)
        up = gu_ref[:, FF:2 * FF].astype(F32)
        sg = _sigmoid(gt)
        dgt = (df * up * _dsilu(gt, sg)).astype(BF16)
        dup = (df * gt * sg).astype(BF16)
        dgu_ref[:, 0:FF] = dgt
        dgu_ref[:, FF:2 * FF] = dup
        dh = _dot(dgt, wi_ref[0:FF, :]) + _dot(dup, wi_ref[FF:2 * FF, :])
        dxn, gw = _rms_bwd(x_ref[...], nw_ref[...], dh)
        dx = dy_ref[...] + dxn
        dx_ref[...] = dx
        dxb_ref[...] = dx.astype(BF16)
        gn_ref[...] = gn_ref[...] + jnp.sum(gw, axis=0, keepdims=True)

    return _call(
        body, sides, name="ffn_bwd", grid=(S // TM,),
        in_specs=[_row(D), _row(D), _row(2 * FF), _row(D), _res((1, D)), _res((2 * FF, D)), _res((FF, D))],
        out_specs=[_row(2 * FF), _row(D), _row(D), pl.BlockSpec((1, D), lambda i: (0, 0))],
        out_shape=[jax.ShapeDtypeStruct((S, 2 * FF), BF16), jax.ShapeDtypeStruct((S, D), F32),
                   jax.ShapeDtypeStruct((S, D), BF16), jax.ShapeDtypeStruct((1, D), F32)],
        args=(dy, dyb, gu, x1, norm_w, w_ffn_in, w_ffn_out))


def in_bwd(d_q, d_k, d_v, d_conv, d_gl, w_in, x, d_x1, norm_w, sides=()):
    segs = ((OFF_Q, QKV), (OFF_K, QKV), (OFF_V, QKV), (OFF_CA, 2 * CC), (OFF_GA, 2 * D))

    def body(dq_ref, dk_ref, dv_ref, dc_ref, dg_ref, w_ref, x_ref, dx1_ref, nw_ref, gx_ref, gn_ref):
        @pl.when(pl.program_id(0) == 0)
        def _():
            gn_ref[...] = jnp.zeros_like(gn_ref)

        dh = jnp.zeros((TM, D), F32)
        for ref, (off, width) in zip((dq_ref, dk_ref, dv_ref, dc_ref, dg_ref), segs):
            for j in range(width // PLANE):
                dh = dh + _dot(ref[j], w_ref[off + j * PLANE:off + (j + 1) * PLANE, :])
        dxn, gw = _rms_bwd(x_ref[...], nw_ref[...], dh)
        gx_ref[...] = dx1_ref[...] + dxn
        gn_ref[...] = gn_ref[...] + jnp.sum(gw, axis=0, keepdims=True)

    return _call(
        body, sides, name="in_bwd", grid=(S // TM,),
        in_specs=[_planes(QKV)] * 3 + [_planes(2 * CC), _planes(2 * D), _res((INW, D)), _row(D), _row(D), _res((1, D))],
        out_specs=[_row(D), pl.BlockSpec((1, D), lambda i: (0, 0))],
        out_shape=[jax.ShapeDtypeStruct((S, D), F32), jax.ShapeDtypeStruct((1, D), F32)],
        args=(d_q, d_k, d_v, d_conv, d_gl, w_in, x, d_x1, norm_w))


def mm_tn(name, a, b, tm, tn, sides=()):
    M, N = a.shape[1], b.shape[1]

    def body(a_ref, b_ref, o_ref):
        o_ref[...] = _dot_tn(a_ref[...], b_ref[...])

    res = _call(
        body, sides, name=name, grid=(M // tm, N // tn),
        in_specs=[pl.BlockSpec((S, tm), lambda i, j: (0, i)), pl.BlockSpec((S, tn), lambda i, j: (0, j))],
        out_specs=[pl.BlockSpec((tm, tn), lambda i, j: (i, j))],
        out_shape=[jax.ShapeDtypeStruct((M, N), F32)],
        args=(a, b))
    return (res[0][0], res[1]) if sides else res[0]


GW_IN_TN = PLANE
GW_IN_SPLIT = (768, 256)


def gw_in_t(name, ht, d_segs, col0, hw, sides=()):
    tn = GW_IN_TN
    starts, t0 = [], 0
    for seg in d_segs:
        starts.append(t0)
        t0 += seg.shape[0]
    ntiles = [seg.shape[0] for seg in d_segs]

    def body(h_ref, *refs):
        a_refs, o_ref = refs[:-1], refs[-1]
        n = pl.program_id(0)
        for a_ref, st, nt in zip(a_refs, starts, ntiles):
            @pl.when((n >= st) & (n < st + nt))
            def _(a_ref=a_ref):
                o_ref[...] = _dot(h_ref[...], a_ref[...]).T

    def seg_spec(st, nt):
        return pl.BlockSpec((None, S, tn), lambda n: (jnp.clip(n - st, 0, nt - 1), 0, 0))

    res = _call(
        body, sides, name=name, grid=(INW // tn,),
        in_specs=[pl.BlockSpec((hw, S), lambda n: (col0 // hw, 0))] + [seg_spec(st, nt) for st, nt in zip(starts, ntiles)],
        out_specs=[pl.BlockSpec((tn, hw), lambda n: (n, 0))],
        out_shape=[jax.ShapeDtypeStruct((INW, hw), F32)],
        args=(ht, *d_segs))
    return (res[0][0], res[1]) if sides else res[0]


def _place():
    x, y, c = lax.axis_index("x"), lax.axis_index("y"), lax.axis_index("c")
    chips = [(1 - x, y), (x, 1 - y), (1 - x, 1 - y)]
    return x, y, c, chips


def _sems(n):
    return pltpu.SemaphoreType.DMA((n,))


def _remote(src, dst, send, recv, k, to):
    return pltpu.make_async_remote_copy(src_ref=src, dst_ref=dst, send_sem=send.at[k], recv_sem=recv.at[k],
                                        device_id=to, device_id_type=MESH)


def _cast_rows(dst, src, cols=slice(None)):
    rows = src.shape[0]
    step = next((s for s in (128, 64, 32, 16) if rows % s == 0), rows)
    for r0 in range(0, rows, step):
        dst[r0:r0 + step, cols] = src[r0:r0 + step, :].astype(dst.dtype)


def comm_only(name, sides):
    def body():
        pass

    return _call(body, sides, name=name, grid=(1,), in_specs=[], out_specs=[], out_shape=[], args=())[1]


def ag_blocks(shard, dtype):
    R, W = shard.shape

    def copy(outs, scr, k, block, to, src=None):
        dst = outs[0].at[block]
        return _remote(dst if src is None else src, dst, scr[1], scr[2], k, to)

    def local(outs, scr, me):
        return pltpu.make_async_copy(scr[0], outs[0].at[me], scr[3].at[0])

    def start(ins, outs, scr):
        x, y, c, chips = _place()
        me = 4 * x + 2 * y + c
        _cast_rows(scr[0], ins[0])
        local(outs, scr, me).start()
        copy(outs, scr, 0, me, (x, y, 1 - c), src=scr[0]).start()
        for j, (cx, cy) in enumerate(chips):
            copy(outs, scr, 1 + j, me, (cx, cy, c), src=scr[0]).start()

    def finish(ins, outs, scr):
        x, y, c, chips = _place()
        me, sib = 4 * x + 2 * y + c, (x, y, 1 - c)
        passed = []
        for j, (cx, cy) in enumerate(chips):
            theirs = 4 * cx + 2 * cy + c
            copy(outs, scr, 1 + j, theirs, (x, y, c)).wait_recv()
            fwd = copy(outs, scr, 4 + j, theirs, sib)
            fwd.start()
            passed.append(fwd)
        copy(outs, scr, 0, 4 * x + 2 * y + 1 - c, (x, y, c)).wait_recv()
        for j, (cx, cy) in enumerate(chips):
            copy(outs, scr, 4 + j, 4 * cx + 2 * cy + 1 - c, (x, y, c)).wait_recv()
        copy(outs, scr, 0, me, sib, src=scr[0]).wait_send()
        for j, (cx, cy) in enumerate(chips):
            copy(outs, scr, 1 + j, me, (cx, cy, c), src=scr[0]).wait_send()
        for fwd in passed:
            fwd.wait_send()
        local(outs, scr, me).wait()

    return Side((shard,), (VMEM,), (jax.ShapeDtypeStruct((NDEV, R, W), dtype),),
                (pltpu.VMEM((R, W), dtype), _sems(7), _sems(7), _sems(1)), start, finish, None, "dsxy")


def ag_blocks_relay(shard, dtype):
    R, W = shard.shape
    half = R // 2

    def copy(outs, scr, k, block, to, src=None, rows=None):
        dst = outs[0].at[block] if rows is None else outs[0].at[block, pl.ds(rows * half, half), :]
        return _remote(dst if src is None else src, dst, scr[1], scr[2], k, to)

    def local(outs, scr, me):
        return pltpu.make_async_copy(scr[0], outs[0].at[me], scr[3].at[0])

    def own(outs, scr):
        x, y, c, _ = _place()
        me = 4 * x + 2 * y + c
        return [copy(outs, scr, k, me, to, src=scr[0])
                for k, to in enumerate([(x, y, 1 - c), (1 - x, y, c), (x, 1 - y, c)])]

    def start(ins, outs, scr):
        x, y, c, _ = _place()
        _cast_rows(scr[0], ins[0])
        local(outs, scr, 4 * x + 2 * y + c).start()
        for cp in own(outs, scr):
            cp.start()

    def passed_on(outs, scr):
        x, y, c, _ = _place()
        sib, xn, yn = (x, y, 1 - c), (1 - x, y, c), (x, 1 - y, c)
        b_xn, b_yn, b_dg = 4 * (1 - x) + 2 * y + c, 4 * x + 2 * (1 - y) + c, 4 * (1 - x) + 2 * (1 - y) + c
        near = [copy(outs, scr, 5, b_xn, yn, rows=0), copy(outs, scr, 3, b_xn, sib),
                copy(outs, scr, 6, b_yn, xn, rows=1), copy(outs, scr, 4, b_yn, sib)]
        far = [copy(outs, scr, 7, b_dg, sib, rows=0), copy(outs, scr, 8, b_dg, sib, rows=1)]
        return (b_xn, b_yn, b_dg), near, far

    def mid(ins, outs, scr):
        x, y, c, _ = _place()
        (b_xn, b_yn, _), near, _ = passed_on(outs, scr)
        copy(outs, scr, 1, b_xn, (x, y, c)).wait_recv()
        near[0].start()
        near[1].start()
        copy(outs, scr, 2, b_yn, (x, y, c)).wait_recv()
        near[2].start()
        near[3].start()

    def finish(ins, outs, scr):
        x, y, c, _ = _place()
        here = (x, y, c)
        (b_xn, b_yn, b_dg), near, far = passed_on(outs, scr)
        copy(outs, scr, 5, b_dg, here, rows=0).wait_recv()
        far[0].start()
        copy(outs, scr, 6, b_dg, here, rows=1).wait_recv()
        far[1].start()
        flip = 1 - 2 * c
        copy(outs, scr, 0, 4 * x + 2 * y + 1 - c, here).wait_recv()
        copy(outs, scr, 3, b_xn + flip, here).wait_recv()
        copy(outs, scr, 4, b_yn + flip, here).wait_recv()
        copy(outs, scr, 7, b_dg + flip, here, rows=0).wait_recv()
        copy(outs, scr, 8, b_dg + flip, here, rows=1).wait_recv()
        for cp in own(outs, scr) + near + far:
            cp.wait_send()
        local(outs, scr, 4 * x + 2 * y + c).wait()

    return Side((shard,), (VMEM,), (jax.ShapeDtypeStruct((NDEV, R, W), dtype),),
                (pltpu.VMEM((R, W), dtype), _sems(9), _sems(9), _sems(1)), start, finish, mid, "sxy")


def ag_cols(shard):
    K, C = shard.shape
    half, w2 = K // 2, 2 * C

    def win(out, rows_c, chip):
        return out.at[pl.ds(pl.multiple_of(rows_c * half, 16), half), pl.ds(pl.multiple_of(chip * w2, LANES), w2)]

    def ici(outs, scr, j, to, c, k):
        slab, send, recv = scr[2], scr[5], scr[6]
        return _remote(slab.at[pl.ds(pl.multiple_of(c * half, 16), half), :], win(outs[0], c, k), send, recv, j, to)

    def local(outs, scr, k):
        return pltpu.make_async_copy(scr[2], outs[0].at[:, pl.ds(pl.multiple_of(k * w2, LANES), w2)], scr[7].at[0])

    def start(ins, outs, scr):
        stage, inbox, slab, xs, xr = scr[:5]
        x, y, c, chips = _place()
        k = 2 * x + y
        _cast_rows(stage, ins[0])
        swap = _remote(stage, inbox, xs, xr, 0, (x, y, 1 - c))
        swap.start()
        for cc in range(2):
            @pl.when(c == cc)
            def _(cc=cc):
                _cast_rows(slab, stage, slice(cc * C, (cc + 1) * C))
        swap.wait()
        for cc in range(2):
            @pl.when(c == cc)
            def _(cc=cc):
                _cast_rows(slab, inbox, slice((1 - cc) * C, (2 - cc) * C))
        local(outs, scr, k).start()
        for j, (cx, cy) in enumerate(chips):
            ici(outs, scr, j, (cx, cy, c), c, k).start()

    def finish(ins, outs, scr):
        send, recv = scr[5], scr[6]
        x, y, c, chips = _place()
        k, sib = 2 * x + y, (x, y, 1 - c)
        passed = []
        for j, (cx, cy) in enumerate(chips):
            w = win(outs[0], c, 2 * cx + cy)
            _remote(w, w, send, recv, j, sib).wait_recv()
            fwd = _remote(w, w, send, recv, 3 + j, sib)
            fwd.start()
            passed.append(fwd)
        for j, (cx, cy) in enumerate(chips):
            w = win(outs[0], 1 - c, 2 * cx + cy)
            _remote(w, w, send, recv, 3 + j, sib).wait_recv()
        for j, (cx, cy) in enumerate(chips):
            ici(outs, scr, j, (cx, cy, c), c, k).wait_send()
        for fwd in passed:
            fwd.wait_send()
        local(outs, scr, k).wait()

    return Side((shard,), (VMEM,), (jax.ShapeDtypeStruct((K, NDEV * C), BF16),),
                (pltpu.VMEM((K, C), BF16), pltpu.VMEM((K, C), BF16), pltpu.VMEM((K, w2), BF16),
                 _sems(1), _sems(1), _sems(6), _sems(6), _sems(1)), start, finish, None, "dsxy")


def copies_side(args, out_shape, n_copies, plan, peers):
    def copies(ins, outs, scr):
        return [_remote(s_, d_, scr[0], scr[1], i, to) for i, (s_, d_, to) in enumerate(plan(ins, outs))]

    def start(ins, outs, scr):
        for cp in copies(ins, outs, scr):
            cp.start()

    def finish(ins, outs, scr):
        for cp in copies(ins, outs, scr):
            cp.wait()

    return Side(tuple(args), (ANY,) * len(args), tuple(out_shape), (_sems(n_copies), _sems(n_copies)),
                start, finish, None, peers)


def rs_to_sibling(grads):
    out_shape = [jax.ShapeDtypeStruct((4,) + g.shape[1:] if kind == "rows" else (g.shape[0] // 2, g.shape[1]), F32)
                 for kind, g in grads]

    def plan(ins, outs):
        x, y, c, _ = _place()
        sib, res = (x, y, 1 - c), []
        for (kind, _), g, r in zip(grads, ins, outs):
            if kind == "rows":
                res += [(g.at[2 * k + 1 - c], r.at[k], sib) for k in range(4)]
            else:
                half = g.shape[0] // 2
                res.append((g.at[pl.ds(pl.multiple_of((1 - c) * half, 8), half), :], r, sib))
        return res

    return copies_side([g for _, g in grads], out_shape, sum(4 if kind == "rows" else 1 for kind, _ in grads), plan, "s")


def rs_to_chips(parts):
    out_shape = [jax.ShapeDtypeStruct((3,) + p.shape[1:] if kind == "rows" else (3, p.shape[0], p.shape[1] // 4), BF16)
                 for kind, p in parts]

    def plan(ins, outs):
        x, y, c, chips = _place()
        res = []
        for (kind, _), p, r in zip(parts, ins, outs):
            for j, (cx, cy) in enumerate(chips):
                if kind == "rows":
                    src = p.at[2 * cx + cy]
                else:
                    w2 = p.shape[1] // 4
                    src = p.at[:, pl.ds(pl.multiple_of((2 * cx + cy) * w2, LANES), w2)]
                res.append((src, r.at[j], (cx, cy, c)))
        return res

    return copies_side([p for _, p in parts], out_shape, 3 * len(parts), plan, "dxy")


def rs_swap_halves(theirs):
    def plan(ins, outs):
        x, y, c, _ = _place()
        return [(t, r, (x, y, 1 - c)) for t, r in zip(ins, outs)]

    return copies_side(theirs, [jax.ShapeDtypeStruct(t.shape, F32) for t in theirs], len(theirs), plan, "s")


def _row_tiles(rows):
    return 2 if rows % 32 == 0 and rows >= 512 else 1


def chip_sum(name, grad, recv, c_idx, chip_idx):
    _, R, C = grad.shape
    nt = 1
    tr = R // nt

    def body(s_ref, g_ref, r_ref, p_ref, own_ref):
        k = pl.program_id(1)
        tot = g_ref[0] + r_ref[0]
        p_ref[0] = tot.astype(BF16)

        @pl.when(k == s_ref[1])
        def _():
            own_ref[...] = tot

    grid_spec = pltpu.PrefetchScalarGridSpec(
        num_scalar_prefetch=1, grid=(nt, 4),
        in_specs=[pl.BlockSpec((1, tr, C), lambda i, k, s: (2 * k + s[0], i, 0)),
                  pl.BlockSpec((1, tr, C), lambda i, k, s: (k, i, 0))],
        out_specs=[pl.BlockSpec((1, tr, C), lambda i, k, s: (k, i, 0)),
                   pl.BlockSpec((tr, C), lambda i, k, s: (i, 0))])
    return pl.pallas_call(
        body, name=name, grid_spec=grid_spec,
        out_shape=[jax.ShapeDtypeStruct((4, R, C), BF16), jax.ShapeDtypeStruct((R, C), F32)],
        compiler_params=_cp(dimension_semantics=("arbitrary", "arbitrary")),
    )(jnp.stack([c_idx, chip_idx]), grad, recv)


def _half_tiles(half):
    return 2 if half >= 512 else 1


def chip_sum_cols(name, grad, recv, c_idx, chip_idx):
    K, W = grad.shape
    half, w2 = K // 2, W // 4
    nt = _half_tiles(half)
    tr = half // nt

    def body(s_ref, g_ref, r_ref, p_ref, own_ref):
        tot = g_ref[...] + r_ref[...]
        p_ref[...] = tot.astype(BF16)

        @pl.when(pl.program_id(1) == s_ref[1])
        def _():
            own_ref[...] = tot

    grid_spec = pltpu.PrefetchScalarGridSpec(
        num_scalar_prefetch=1, grid=(nt, 4),
        in_specs=[pl.BlockSpec((tr, w2), lambda i, k, s: (s[0] * nt + i, k)),
                  pl.BlockSpec((tr, w2), lambda i, k, s: (i, k))],
        out_specs=[pl.BlockSpec((tr, w2), lambda i, k, s: (i, k)),
                   pl.BlockSpec((tr, w2), lambda i, k, s: (i, 0))])
    return pl.pallas_call(
        body, name=name, grid_spec=grid_spec,
        out_shape=[jax.ShapeDtypeStruct((half, W), BF16), jax.ShapeDtypeStruct((half, w2), F32)],
        compiler_params=_cp(dimension_semantics=("arbitrary", "arbitrary")),
    )(jnp.stack([c_idx, chip_idx]), grad, recv)


def col_final(name, own, recv, c_idx):
    half, w2 = own.shape
    C = w2 // 2
    nt = _half_tiles(half)
    tr = half // nt

    def body(s_ref, o_ref, r_ref, mine_ref, theirs_ref, t_ref):
        t_ref[...] = o_ref[...] + r_ref[0].astype(F32) + r_ref[1].astype(F32) + r_ref[2].astype(F32)
        for cc in range(2):
            @pl.when(s_ref[0] == cc)
            def _(cc=cc):
                mine_ref[...] = t_ref[:, cc * C:(cc + 1) * C]
                theirs_ref[...] = t_ref[:, (1 - cc) * C:(2 - cc) * C]

    grid_spec = pltpu.PrefetchScalarGridSpec(
        num_scalar_prefetch=1, grid=(nt,),
        in_specs=[pl.BlockSpec((tr, w2), lambda i, s: (i, 0)), pl.BlockSpec((3, tr, w2), lambda i, s: (0, i, 0))],
        out_specs=[pl.BlockSpec((tr, C), lambda i, s: (i, 0))] * 2,
        scratch_shapes=[pltpu.VMEM((tr, w2), F32)])
    return pl.pallas_call(
        body, name=name, grid_spec=grid_spec, out_shape=[jax.ShapeDtypeStruct((half, C), F32)] * 2,
        compiler_params=_cp(dimension_semantics=("arbitrary",)),
    )(jnp.stack([c_idx]), own, recv)


def _adamw(w, g, m, v):
    m2 = ADAM_B1 * m + (1.0 - ADAM_B1) * g
    v2 = ADAM_B2 * v + (1.0 - ADAM_B2) * (g * g)
    m_hat = m2 / (1.0 - ADAM_B1 ** ADAM_STEP)
    v_hat = v2 / (1.0 - ADAM_B2 ** ADAM_STEP)
    delta = -ADAM_LR * (m_hat / (jnp.sqrt(v_hat) + ADAM_EPS) + ADAM_WD * w)
    return delta, m2, v2


def shard_adam(name, owns, recvs, w, m, v):
    n = len(owns)
    R = owns[0].shape[0]
    ct = min(o.shape[1] for o in owns)
    first = [sum(o.shape[1] for o in owns[:j]) // ct for j in range(n)]
    count = [o.shape[1] // ct for o in owns]
    nt = _row_tiles(R)
    tr = R // nt

    def body(*refs):
        o_refs, r_refs = refs[:n], refs[n:2 * n]
        w_ref, m_ref, v_ref, g_ref, d_ref, nm_ref, nv_ref = refs[2 * n:]
        g = None
        for j in range(n):
            gj = o_refs[j][...] + r_refs[j][0].astype(F32) + r_refs[j][1].astype(F32) + r_refs[j][2].astype(F32)
            g = gj if g is None else jnp.where(pl.program_id(0) >= first[j], gj, g)
        delta, m2, v2 = _adamw(w_ref[...], g, m_ref[...], v_ref[...])
        g_ref[...] = g
        d_ref[...] = delta
        nm_ref[...] = m2
        nv_ref[...] = v2

    def part(j):
        return pl.BlockSpec((tr, ct), lambda k, i: (i, jnp.clip(k - first[j], 0, count[j] - 1)))

    def part3(j):
        return pl.BlockSpec((3, tr, ct), lambda k, i: (0, i, jnp.clip(k - first[j], 0, count[j] - 1)))

    tile = pl.BlockSpec((tr, ct), lambda k, i: (i, k))
    return pl.pallas_call(
        body, name=name, grid=(sum(count), nt),
        in_specs=[part(j) for j in range(n)] + [part3(j) for j in range(n)] + [tile, tile, tile],
        out_specs=[tile] * 4, out_shape=[jax.ShapeDtypeStruct((R, sum(count) * ct), F32)] * 4,
        compiler_params=_cp(dimension_semantics=("arbitrary", "arbitrary")),
    )(*owns, *recvs, w, m, v)


def adam_cols(name, mine, recv, w, m, v, c_idx):
    half, C = mine.shape
    nt = _half_tiles(half)
    tr = half // nt

    def body(s_ref, a_ref, b_ref, w_ref, m_ref, v_ref, g_ref, d_ref, nm_ref, nv_ref):
        g = jnp.where(pl.program_id(0) == s_ref[0], a_ref[...], b_ref[...])
        delta, m2, v2 = _adamw(w_ref[...], g, m_ref[...], v_ref[...])
        g_ref[...] = g
        d_ref[...] = delta
        nm_ref[...] = m2
        nv_ref[...] = v2

    part = pl.BlockSpec((tr, C), lambda hh, i, s: (i, 0))
    tile = pl.BlockSpec((tr, C), lambda hh, i, s: (hh * nt + i, 0))
    grid_spec = pltpu.PrefetchScalarGridSpec(
        num_scalar_prefetch=1, grid=(2, nt), in_specs=[part, part, tile, tile, tile], out_specs=[tile] * 4)
    return pl.pallas_call(
        body, name=name, grid_spec=grid_spec, out_shape=[jax.ShapeDtypeStruct((2 * half, C), F32)] * 4,
        compiler_params=_cp(dimension_semantics=("arbitrary", "arbitrary")),
    )(jnp.stack([c_idx]), mine, recv, w, m, v)


ROW_N1, ROW_N2, ROW_BG, ROW_QN, ROW_KN, ROW_CB, ROW_LW, ROW_LB, ROW_CW = 0, 1, 2, 4, 5, 6, 7, 8, 9
PACK_ROWS = 40
SMALL = ("norm1_w", "norm2_w", "b_gate", "q_norm_w", "k_norm_w", "conv_b", "conv_ln_w", "conv_ln_b", "conv_w")


def small_sync_adam(g, w, m, v, sq, sides=()):
    ns = len(SMALL)

    def body(*refs):
        gi = dict(zip(SMALL, refs[:ns]))
        wi = dict(zip(SMALL, refs[ns:2 * ns]))
        mi = dict(zip(SMALL, refs[2 * ns:3 * ns]))
        vi = dict(zip(SMALL, refs[3 * ns:4 * ns]))
        sq_ref = refs[4 * ns]
        outs = refs[4 * ns + 1:8 * ns + 1]
        loss_ref = refs[8 * ns + 1]
        pack, recv, tot, send_sems, recv_sems = refs[8 * ns + 2:]
        x, y, c, _ = _place()
        me = 4 * x + 2 * y + c

        pack[...] = jnp.zeros_like(pack)
        pack[ROW_KN:ROW_KN + 1, LANES:2 * LANES] = jnp.full((1, LANES), (0.5 / D) * jnp.sum(sq_ref[...]), F32)
        pack[ROW_N1:ROW_N1 + 1, :] = gi["norm1_w"][...]
        pack[ROW_N2:ROW_N2 + 1, :] = gi["norm2_w"][...]
        pack[ROW_BG:ROW_BG + 2, :] = gi["b_gate"][...]
        pack[ROW_QN:ROW_QN + 1, 0:HD] = gi["q_norm_w"][...]
        pack[ROW_KN:ROW_KN + 1, 0:HD] = gi["k_norm_w"][...]
        pack[ROW_CB:ROW_CB + 1, 0:CC] = gi["conv_b"][...]
        pack[ROW_LW:ROW_LW + 1, 0:CC] = gi["conv_ln_w"][...]
        pack[ROW_LB:ROW_LB + 1, 0:CC] = gi["conv_ln_b"][...]
        pack[ROW_CW:ROW_CW + KW, 0:CC] = gi["conv_w"][...]

        copies = []
        for k in range(1, NDEV):
            peer = (x ^ (k >> 2), y ^ ((k >> 1) & 1), c ^ (k & 1))
            cp = pltpu.make_async_remote_copy(
                src_ref=pack, dst_ref=recv.at[me], send_sem=send_sems.at[k - 1], recv_sem=recv_sems.at[k - 1],
                device_id=peer, device_id_type=MESH)
            cp.start()
            copies.append(cp)
        recv[me] = pack[...]
        for cp in copies:
            cp.wait()
        acc = recv[0]
        for p in range(1, NDEV):
            acc = acc + recv[p]
        tot[...] = acc

        def shard_grad(name):
            if name == "b_gate":
                return tot[ROW_BG:ROW_BG + 2, pl.ds(pl.multiple_of(me * LANES, LANES), LANES)]
            if name == "conv_w":
                win = tot[ROW_CW:ROW_CW + KW, pl.ds(pl.multiple_of((me // 2) * LANES, LANES), LANES)]
                return jnp.where(me % 2 == 1, win[:, HD:LANES], win[:, 0:HD])
            row = {"norm1_w": ROW_N1, "norm2_w": ROW_N2, "q_norm_w": ROW_QN, "k_norm_w": ROW_KN,
                   "conv_b": ROW_CB, "conv_ln_w": ROW_LW, "conv_ln_b": ROW_LB}[name]
            return tot[row:row + 1, 0:wi[name].shape[1]]

        for i, name in enumerate(SMALL):
            gr = shard_grad(name)
            delta, m2, v2 = _adamw(wi[name][...], gr, mi[name][...], vi[name][...])
            outs[4 * i][...] = gr
            outs[4 * i + 1][...] = delta
            outs[4 * i + 2][...] = m2
            outs[4 * i + 3][...] = v2
        loss_ref[...] = tot[ROW_KN:ROW_KN + 1, LANES:2 * LANES]

    out_shape = []
    for name in SMALL:
        out_shape += [jax.ShapeDtypeStruct(w[name].shape, F32)] * 4
    out_shape.append(jax.ShapeDtypeStruct((1, LANES), F32))
    args = [g[k] for k in SMALL] + [w[k] for k in SMALL] + [m[k] for k in SMALL] + [v[k] for k in SMALL] + [sq]
    res = _call(
        body, sides, name="small_sync_adam", grid=(1,), in_specs=[VMEM] * len(args),
        out_specs=[VMEM] * len(out_shape), out_shape=out_shape,
        scratch_shapes=[pltpu.VMEM((PACK_ROWS, D), F32), pltpu.VMEM((NDEV, PACK_ROWS, D), F32),
                        pltpu.VMEM((PACK_ROWS, D), F32), _sems(NDEV - 1), _sems(NDEV - 1)],
        args=args, own_comm=True)
    res, side_outs = res if sides else (res, None)
    out = {name: tuple(res[4 * i:4 * i + 4]) for i, name in enumerate(SMALL)}
    loss = res[4 * ns][0, 0]
    return (out, loss, side_outs) if sides else (out, loss)


MATS = ("w_in", "w_o_attn", "w_pw_conv", "w_out", "w_ffn_in", "w_ffn_out")
TRANSPOSED = ("w_in", "w_ffn_in")
WEIGHTS = ("norm1_w", "w_in", "b_gate", "q_norm_w", "k_norm_w", "w_o_attn", "conv_w", "conv_b", "conv_ln_w",
           "conv_ln_b", "w_pw_conv", "w_out", "norm2_w", "w_ffn_in", "w_ffn_out")


def _blocks_to_cols(blocks):
    n, R, C = blocks.shape
    return blocks.transpose(1, 0, 2).reshape(R, n * C)


def kernel(x, positions, norm1_w, w_in, b_gate, q_norm_w, k_norm_w, w_o_attn, conv_w, conv_b, conv_ln_w, conv_ln_b, w_pw_conv, w_out, norm2_w, w_ffn_in, w_ffn_out, loss_target, m_norm1_w, m_w_in, m_b_gate, m_q_norm_w, m_k_norm_w, m_w_o_attn, m_conv_w, m_conv_b, m_conv_ln_w, m_conv_ln_b, m_w_pw_conv, m_w_out, m_norm2_w, m_w_ffn_in, m_w_ffn_out, v_norm1_w, v_w_in, v_b_gate, v_q_norm_w, v_k_norm_w, v_w_o_attn, v_conv_w, v_conv_b, v_conv_ln_w, v_conv_ln_b, v_w_pw_conv, v_w_out, v_norm2_w, v_w_ffn_in, v_w_ffn_out):
    w = dict(norm1_w=norm1_w, w_in=w_in, b_gate=b_gate, q_norm_w=q_norm_w, k_norm_w=k_norm_w, w_o_attn=w_o_attn,
             conv_w=conv_w, conv_b=conv_b, conv_ln_w=conv_ln_w, conv_ln_b=conv_ln_b, w_pw_conv=w_pw_conv,
             w_out=w_out, norm2_w=norm2_w, w_ffn_in=w_ffn_in, w_ffn_out=w_ffn_out)
    m = dict(norm1_w=m_norm1_w, w_in=m_w_in, b_gate=m_b_gate, q_norm_w=m_q_norm_w, k_norm_w=m_k_norm_w,
             w_o_attn=m_w_o_attn, conv_w=m_conv_w, conv_b=m_conv_b, conv_ln_w=m_conv_ln_w,
             conv_ln_b=m_conv_ln_b, w_pw_conv=m_w_pw_conv, w_out=m_w_out, norm2_w=m_norm2_w,
             w_ffn_in=m_w_ffn_in, w_ffn_out=m_w_ffn_out)
    v = dict(norm1_w=v_norm1_w, w_in=v_w_in, b_gate=v_b_gate, q_norm_w=v_q_norm_w, k_norm_w=v_k_norm_w,
             w_o_attn=v_w_o_attn, conv_w=v_conv_w, conv_b=v_conv_b, conv_ln_w=v_conv_ln_w,
             conv_ln_b=v_conv_ln_b, w_pw_conv=v_w_pw_conv, w_out=v_w_out, norm2_w=v_norm2_w,
             w_ffn_in=v_w_ffn_in, w_ffn_out=v_w_ffn_out)
    def two_d(t):
        t = {k: (a[0] if a.ndim == 3 else a) for k, a in t.items()}
        return {k: (a.T if k in TRANSPOSED else a) for k, a in t.items()}

    w, m, v = two_d(w), two_d(m), two_d(v)

    x2, target = x[0], loss_target[0]
    c_idx = lax.axis_index("c").astype(jnp.int32)
    chip_idx = (2 * lax.axis_index("x") + lax.axis_index("y")).astype(jnp.int32)
    qw2 = jnp.tile(w["q_norm_w"], (1, 2))
    kw2 = jnp.tile(w["k_norm_w"], (1, 2))

    tabs, ((w_in_blocks,), (bg_blocks,), (cw_blocks,)) = rope_tables(
        positions.reshape(S, 1),
        sides=(ag_blocks_relay(w["w_in"], BF16), ag_blocks(w["b_gate"], F32), ag_blocks(w["conv_w"], F32)))
    w_in_t = w_in_blocks.reshape(INW, D)
    b_gate_f, conv_w_f = _blocks_to_cols(bg_blocks), _blocks_to_cols(cw_blocks)
    (h_t, proj), ((w_o_f,), (w_pw_f,)) = in_proj(
        x2, w["norm1_w"], w_in_t, sides=(ag_cols(w["w_o_attn"]), ag_cols(w["w_pw_conv"])))
    (attn, lse), ((w_ffn_in_blocks,), (w_out_blocks,)) = attn_fwd(
        proj, tabs, qw2, kw2, sides=(ag_blocks_relay(w["w_ffn_in"], BF16), ag_blocks_relay(w["w_out"], BF16)))
    w_ffn_in_t = w_ffn_in_blocks.reshape(2 * FF, D)
    w_out_f = w_out_blocks.reshape(D, D)
    cpre, u3 = conv_fwd(proj, conv_w_f, w["conv_b"], w["conv_ln_w"], w["conv_ln_b"])
    x1, z, ya, yb = mix_out(x2, proj, b_gate_f, attn, u3, w_o_f, w_pw_f, w_out_f)
    (h2, gu, f), ((w_ffn_out_blocks,),) = ffn_in(x1, w["norm2_w"], w_ffn_in_t, sides=(ag_blocks_relay(w["w_ffn_out"], BF16),))
    w_ffn_out_f = w_ffn_out_blocks.reshape(FF, D)
    dy, dyb, sq = ffn_out_loss(x1, f, w_ffn_out_f, target)

    g = {}
    g_ffn_out = mm_tn("gw_ffn_out", f, dyb, FF // 2, D).reshape(NDEV, FF // NDEV, D)
    (d_gu, d_x1, d_x1b, g["norm2_w"]), ((ra_ffn_out,),) = ffn_bwd(
        dy, dyb, gu, x1, w["norm2_w"], w_ffn_in_t, w_ffn_out_f, sides=(rs_to_sibling([("rows", g_ffn_out)]),))
    pb_ffn_out, own_ffn_out = chip_sum("chip_sum_w_ffn_out", g_ffn_out, ra_ffn_out, c_idx, chip_idx)
    g_ffn_in, ((rb_ffn_out,),) = mm_tn("gw_ffn_in", d_gu, h2, FF // 2, D,
                                       sides=(rs_to_chips([("rows", pb_ffn_out)]),))
    g_ffn_in = g_ffn_in.reshape(NDEV, 2 * FF // NDEV, D)
    g_out = mm_tn("gw_out", z, d_x1b, D // 2, D).reshape(NDEV, D // NDEV, D)
    (d_ya, d_yb, d_gl, d_attn, d_u3, g["b_gate"]), ((ra_ffn_in,),) = out_bwd(
        d_x1b, proj, b_gate_f, ya, yb, w_o_f, w_pw_f, w_out_f, sides=(rs_to_sibling([("rows", g_ffn_in)]),))
    pb_ffn_in, own_ffn_in = chip_sum("chip_sum_w_ffn_in", g_ffn_in, ra_ffn_in, c_idx, chip_idx)
    g_w_o = mm_tn("gw_o_attn", attn, d_ya, CC, D)
    g_w_pw = mm_tn("gw_pw_conv", u3, d_yb, CC, D)
    (d_conv, g["conv_w"], g["conv_b"], g["conv_ln_w"], g["conv_ln_b"]), ((ra_out, ra_w_o, ra_w_pw),) = conv_bwd(
        proj, cpre, d_u3, conv_w_f, conv_w_f[::-1], w["conv_ln_w"], w["conv_ln_b"],
        sides=(rs_to_sibling([("rows", g_out), ("cols", g_w_o), ("cols", g_w_pw)]),))
    pb_out, own_out = chip_sum("chip_sum_w_out", g_out, ra_out, c_idx, chip_idx)
    pb_w_o, own_w_o = chip_sum_cols("chip_sum_w_o_attn", g_w_o, ra_w_o, c_idx, chip_idx)
    pb_w_pw, own_w_pw = chip_sum_cols("chip_sum_w_pw_conv", g_w_pw, ra_w_pw, c_idx, chip_idx)
    (d_q, d_k, d_v, gqw, gkw), ((rb_ffn_in, rb_out, rb_w_o, rb_w_pw),) = attn_bwd(
        proj, tabs, qw2, kw2, d_attn, attn, lse,
        sides=(rs_to_chips([("rows", pb_ffn_in), ("rows", pb_out), ("cols", pb_w_o), ("cols", pb_w_pw)]),))
    g["q_norm_w"] = gqw[0:1, 0:HD] + gqw[0:1, HD:LANES]
    g["k_norm_w"] = gkw[0:1, 0:HD] + gkw[0:1, HD:LANES]
    mine_w_o, theirs_w_o = col_final("col_final_w_o_attn", own_w_o, rb_w_o, c_idx)
    mine_w_pw, theirs_w_pw = col_final("col_final_w_pw_conv", own_w_pw, rb_w_pw, c_idx)
    d_segs = (d_q, d_k, d_v, d_conv, d_gl)
    parts, to_sibling, to_chips, owns, from_chips = [], None, None, [], []
    for k, hw in enumerate(GW_IN_SPLIT):
        sides = [rs_swap_halves([theirs_w_o, theirs_w_pw])] if k == 0 else []
        sides += [s for s in (to_chips, to_sibling) if s is not None]
        part, outs = gw_in_t("gw_in_%d" % k, h_t, d_segs, sum(GW_IN_SPLIT[:k]), hw, sides=tuple(sides))
        if k == 0:
            (rc_w_o, rc_w_pw), outs = outs[0], outs[1:]
        outs = list(outs)
        if to_chips is not None:
            from_chips.append(outs.pop(0)[0])
        if to_sibling is not None:
            pb, own = chip_sum("chip_sum_w_in_%d" % (k - 1), parts[-1], outs.pop(0)[0], c_idx, chip_idx)
            owns.append(own)
            to_chips = rs_to_chips([("rows", pb)])
        else:
            to_chips = None
        parts.append(part.reshape(NDEV, INW // NDEV, hw))
        to_sibling = rs_to_sibling([("rows", parts[-1])])
    (grad_x, g["norm1_w"]), ((rb_prev,), (ra_last,)) = in_bwd(
        d_q, d_k, d_v, d_conv, d_gl, w_in_t, x2, d_x1, w["norm1_w"], sides=(to_chips, to_sibling))
    from_chips.append(rb_prev)
    pb, own = chip_sum("chip_sum_w_in_%d" % (len(GW_IN_SPLIT) - 1), parts[-1], ra_last, c_idx, chip_idx)
    owns.append(own)
    small, loss, ((rb_last,),) = small_sync_adam(g, w, m, v, sq, sides=(rs_to_chips([("rows", pb)]),))
    from_chips.append(rb_last)

    res = {
        "w_in": shard_adam("adam_w_in", owns, from_chips, w["w_in"], m["w_in"], v["w_in"]),
        "w_ffn_in": shard_adam("adam_w_ffn_in", [own_ffn_in], [rb_ffn_in], w["w_ffn_in"], m["w_ffn_in"], v["w_ffn_in"]),
        "w_o_attn": adam_cols("adam_w_o_attn", mine_w_o, rc_w_o, w["w_o_attn"], m["w_o_attn"], v["w_o_attn"], c_idx),
        "w_pw_conv": adam_cols("adam_w_pw_conv", mine_w_pw, rc_w_pw, w["w_pw_conv"], m["w_pw_conv"], v["w_pw_conv"], c_idx),
        "w_out": shard_adam("adam_w_out", [own_out], [rb_out], w["w_out"], m["w_out"], v["w_out"]),
        "w_ffn_out": shard_adam("adam_w_ffn_out", [own_ffn_out], [rb_ffn_out],
                                w["w_ffn_out"], m["w_ffn_out"], v["w_ffn_out"]),
    }
    res = {k: tuple(a.T if k in TRANSPOSED else a for a in r) for k, r in res.items()}
    res.update(small)

    def shaped(name, a):
        return a.reshape((1,) + a.shape) if name in MATS or name in ("b_gate", "conv_w") else a

    outs = [loss, grad_x.reshape(1, S, D)]
    for i in range(4):
        outs += [shaped(k, res[k][i]) for k in WEIGHTS]
    return tuple(outs)
```

```python
import functools
from typing import Callable, NamedTuple, Optional

import numpy as np
import jax
import jax.numpy as jnp
from jax import lax
from jax.experimental import pallas as pl
from jax.experimental.pallas import tpu as pltpu

F32 = jnp.float32
BF16 = jnp.bfloat16

S = 2048
D = 1024
HD = 64
QKV = 1536
CC = 512
KW = 31
FF = 2816
INW = 7680
OFF_Q, OFF_K, OFF_V, OFF_CA, OFF_CB, OFF_GA, OFF_GB = 0, 1536, 3072, 4608, 5120, 5632, 6656
DILATIONS = (1, 4, 16)
HALF_SPAN = 64
EPS = 1e-6
NEG_INF = -1e30
ROPE_THETA = 500000.0
ROT_DIM = 16

ADAM_LR = 0.001
ADAM_B1 = 0.9
ADAM_B2 = 0.999
ADAM_EPS = 1e-08
ADAM_WD = 0.01
ADAM_STEP = 10

NDEV = 8
LANES = 128
TM = 256
TQ = 128
VMEM_LIMIT = 56 * 1024 * 1024
MESH = pl.DeviceIdType.MESH


def _cp(**kw):
    return pltpu.CompilerParams(vmem_limit_bytes=VMEM_LIMIT, **kw)


def _row(width, col=0, tm=TM):
    return pl.BlockSpec((tm, width), lambda i: (i, col))


PLANE = 512


def _planes(width, tm=TM):
    return pl.BlockSpec((width // PLANE, tm, PLANE), lambda i: (0, i, 0))


def _res(shape):
    nd = len(shape)
    return pl.BlockSpec(shape, lambda *_: (0,) * nd, pipeline_mode=pl.Buffered(1))


def _dot(a, b):
    return jnp.dot(a, b, preferred_element_type=F32)


def _dot_nt(a, b):
    return lax.dot_general(a, b, (((1,), (1,)), ((), ())), preferred_element_type=F32)


def _dot_tn(a, b):
    return lax.dot_general(a, b, (((0,), (0,)), ((), ())), preferred_element_type=F32)


def _sigmoid(x):
    return jax.nn.sigmoid(x)


def _dsilu(x, sg):
    return sg * (1.0 + x * (1.0 - sg))


ANY = pl.BlockSpec(memory_space=pl.ANY)
VMEM = pl.BlockSpec(memory_space=pltpu.VMEM)


class Side(NamedTuple):
    args: tuple
    in_specs: tuple
    out_shape: tuple
    scratch: tuple
    start: Callable
    finish: Callable
    mid: Optional[Callable] = None
    peers: str = ""


BARRIER_IDS = {"s": 0, "dxy": 1, "dsxy": 2, "sxy": 3}


def _peer_barrier(peers):
    x, y, c = lax.axis_index("x"), lax.axis_index("y"), lax.axis_index("c")
    where = {"s": (x, y, 1 - c), "x": (1 - x, y, c), "y": (x, 1 - y, c), "d": (1 - x, 1 - y, c)}
    barrier = pltpu.get_barrier_semaphore()
    for p in peers:
        pl.semaphore_signal(barrier, inc=1, device_id=where[p], device_id_type=MESH)
    pl.semaphore_wait(barrier, len(peers))


def _call(body, sides=(), *, name, grid, in_specs, out_specs, out_shape, scratch_shapes=(), args, own_comm=False):
    ni, no, ns = len(in_specs), len(out_specs), len(scratch_shapes)
    cnt = [(len(s.args), len(s.out_shape), len(s.scratch)) for s in sides]
    peers = "".join(sorted(set("".join(s.peers for s in sides))))
    if own_comm or not sides or any(not s.peers for s in sides):
        peers = ""

    def take(refs, pos, n):
        return refs[pos:pos + n], pos + n

    def full(*refs):
        m_in, pos = take(refs, 0, ni)
        s_in = []
        for a, _, _ in cnt:
            r, pos = take(refs, pos, a)
            s_in.append(r)
        m_out, pos = take(refs, pos, no)
        s_out = []
        for _, o, _ in cnt:
            r, pos = take(refs, pos, o)
            s_out.append(r)
        m_scr, pos = take(refs, pos, ns)
        s_scr = []
        for _, _, c in cnt:
            r, pos = take(refs, pos, c)
            s_scr.append(r)
        if sides:
            first = functools.reduce(jnp.logical_and, [pl.program_id(d) == 0 for d in range(len(grid))])
            last = functools.reduce(jnp.logical_and, [pl.program_id(d) == g - 1 for d, g in enumerate(grid)])

            @pl.when(first)
            def _():
                if peers:
                    _peer_barrier(peers)
                for s, a, o, c in zip(sides, s_in, s_out, s_scr):
                    s.start(a, o, c)

            steps = int(np.prod(grid))
            mid_step = (2 * steps) // 3
            if steps > 1 and any(s.mid is not None for s in sides):
                step = functools.reduce(lambda acc, d: acc * grid[d] + pl.program_id(d), range(len(grid)), 0)

                @pl.when(step == mid_step)
                def _():
                    for s, a, o, c in zip(sides, s_in, s_out, s_scr):
                        if s.mid is not None:
                            s.mid(a, o, c)

        body(*m_in, *m_out, *m_scr)
        if sides:
            @pl.when(last)
            def _():
                for s, a, o, c in zip(sides, s_in, s_out, s_scr):
                    if s.mid is not None and steps == 1:
                        s.mid(a, o, c)
                    s.finish(a, o, c)

    res = pl.pallas_call(
        full, name=name, grid=grid,
        in_specs=list(in_specs) + [sp for s in sides for sp in s.in_specs],
        out_specs=list(out_specs) + [ANY for s in sides for _ in s.out_shape],
        out_shape=list(out_shape) + [o for s in sides for o in s.out_shape],
        scratch_shapes=list(scratch_shapes) + [c for s in sides for c in s.scratch],
        compiler_params=_cp(dimension_semantics=("arbitrary",) * len(grid),
                            **({"collective_id": BARRIER_IDS[peers]} if peers else {})),
    )(*args, *[a for s in sides for a in s.args])
    res = list(res)
    if not sides:
        return res
    outs, pos = take(res, 0, no)
    side_outs = []
    for _, o, _ in cnt:
        r, pos = take(res, pos, o)
        side_outs.append(r)
    return outs, side_outs


def _inv_freq_lanes():
    inv = np.float32(ROPE_THETA) ** (-np.arange(0, ROT_DIM, 2, dtype=np.float32) / np.float32(ROT_DIM))
    lane = np.arange(LANES) % HD
    out = np.where(lane < ROT_DIM, inv[lane % (ROT_DIM // 2)], 0.0).astype(np.float32)
    return jnp.asarray(out.reshape(1, LANES))


def rope_tables(pos_col, sides=()):
    def body(p_ref, f_ref, c_ref, s1_ref, s2_ref):
        ang = p_ref[...].astype(F32) * f_ref[...]
        lane = lax.broadcasted_iota(jnp.int32, ang.shape, 1) % HD
        cs = jnp.cos(ang)
        sn = jnp.sin(ang)
        c_ref[...] = jnp.where(lane < ROT_DIM, cs, 1.0)
        s1_ref[...] = jnp.where(lane < ROT_DIM // 2, -sn, 0.0)
        s2_ref[...] = jnp.where(lane < ROT_DIM // 2, 0.0, jnp.where(lane < ROT_DIM, sn, 0.0))

    sds = jax.ShapeDtypeStruct((S, LANES), F32)
    return _call(
        body, sides, name="rope_tables", grid=(S // TM,),
        in_specs=[_row(1), pl.BlockSpec((1, LANES), lambda i: (0, 0))],
        out_specs=[_row(LANES)] * 3, out_shape=[sds] * 3,
        args=(pos_col, _inv_freq_lanes()))


def _rope(v, c, s1, s2):
    return v * c + pltpu.roll(v, LANES - 8, axis=1) * s1 + pltpu.roll(v, 8, axis=1) * s2


def _rope_t(d, c, s1, s2):
    return d * c - pltpu.roll(d, LANES - 8, axis=1) * s1 - pltpu.roll(d, 8, axis=1) * s2


def _head_mat():
    r = lax.broadcasted_iota(jnp.int32, (LANES, LANES), 0) // HD
    c = lax.broadcasted_iota(jnp.int32, (LANES, LANES), 1) // HD
    return jnp.where(r == c, 1.0 / HD, 0.0).astype(BF16)


def _head_mean(t, e):
    hi = t.astype(BF16)
    rest = (t - hi.astype(F32)).astype(BF16)
    return _dot(hi, e) + _dot(rest, e)


def in_proj_gather(x, norm_w, shard_t, chip_order):
    R = INW // NDEV
    half, nt = R // 2, S // TM

    def body(ord_ref, x_ref, nw_ref, sh_ref, ht_ref, p_ref, wfull_ref, wt, hs, send, recv, loc):
        kk, i = pl.program_id(0), pl.program_id(1)
        x, y, c, _ = _place()
        me, flip = 4 * x + 2 * y + c, 1 - 2 * c
        here, sib, xn, yn = (x, y, c), (x, y, 1 - c), (1 - x, y, c), (x, 1 - y, c)
        b_xn, b_yn, b_dg = 4 * (1 - x) + 2 * y + c, 4 * x + 2 * (1 - y) + c, 4 * (1 - x) + 2 * (1 - y) + c

        def cp(k, block, to, rows=None):
            dst = wt.at[block] if rows is None else wt.at[block, pl.ds(rows * half, half), :]
            return _remote(dst, dst, send, recv, k, to)

        def sends():
            return [cp(0, me, sib), cp(1, me, xn), cp(2, me, yn), cp(3, b_xn, sib), cp(4, b_yn, sib),
                    cp(5, b_xn, yn, rows=0), cp(6, b_yn, xn, rows=1), cp(7, b_dg, sib, rows=0), cp(8, b_dg, sib, rows=1)]

        @pl.when((kk == 0) & (i == 0))
        def _():
            _peer_barrier("sxy")
            _cast_rows(wt.at[me], sh_ref)
            for s_ in sends()[0:3]:
                s_.start()
            cp(0, me + flip, here).wait_recv()

        @pl.when((kk == 1) & (i == 0))
        def _():
            cp(1, b_xn, here).wait_recv()
            sends()[5].start()
            sends()[3].start()
            cp(3, b_xn + flip, here).wait_recv()

        @pl.when((kk == 2) & (i == 0))
        def _():
            cp(2, b_yn, here).wait_recv()
            sends()[6].start()
            sends()[4].start()
            cp(4, b_yn + flip, here).wait_recv()

        @pl.when((kk == 3) & (i == 0))
        def _():
            cp(5, b_dg, here, rows=0).wait_recv()
            sends()[7].start()
            cp(6, b_dg, here, rows=1).wait_recv()
            sends()[8].start()
            cp(7, b_dg + flip, here, rows=0).wait_recv()
            cp(8, b_dg + flip, here, rows=1).wait_recv()

        rows = pl.ds(pl.multiple_of(i * TM, TM), TM)

        @pl.when(kk == 0)
        def _():
            xv = x_ref[...]
            r = lax.rsqrt(jnp.mean(xv * xv, axis=-1, keepdims=True) + EPS)
            hf = xv * r * nw_ref[...]
            ht_ref[...] = hf.T.astype(BF16)
            hs[rows, :] = hf.astype(BF16)

        h = hs[rows, :]
        chip = ord_ref[kk]
        for cc in range(2):
            p_ref[:, cc * R:(cc + 1) * R] = _dot_nt(h, wt[2 * chip + cc])

        @pl.when((kk == 3) & (i == nt - 1))
        def _():
            for s_ in sends():
                s_.wait_send()
            keep = pltpu.make_async_copy(wt, wfull_ref, loc.at[0])
            keep.start()
            keep.wait()

    def first_pass(kk, i):
        return jnp.where(kk == 0, i, nt - 1)

    grid_spec = pltpu.PrefetchScalarGridSpec(
        num_scalar_prefetch=1, grid=(4, nt),
        in_specs=[pl.BlockSpec((TM, D), lambda kk, i, o: (first_pass(kk, i), 0)),
                  pl.BlockSpec((1, D), lambda kk, i, o: (0, 0)), VMEM],
        out_specs=[pl.BlockSpec((D, TM), lambda kk, i, o: (0, first_pass(kk, i))),
                   pl.BlockSpec((TM, 2 * R), lambda kk, i, o: (i, o[kk])), ANY],
        scratch_shapes=[pltpu.VMEM((NDEV, R, D), BF16), pltpu.VMEM((S, D), BF16), _sems(9), _sems(9), _sems(1)])
    return pl.pallas_call(
        body, name="in_proj_gather", grid_spec=grid_spec,
        out_shape=[jax.ShapeDtypeStruct((D, S), BF16), jax.ShapeDtypeStruct((S, INW), F32),
                   jax.ShapeDtypeStruct((NDEV, R, D), BF16)],
        compiler_params=_cp(dimension_semantics=("arbitrary", "arbitrary"), collective_id=BARRIER_IDS["sxy"]),
    )(chip_order, x, norm_w, shard_t)


def _qk_specs():
    nb = QKV // LANES
    return [pl.BlockSpec((S, LANES), functools.partial(lambda hp, g, o: (0, o + g * 4 + hp), o=o))
            for o in (OFF_Q // LANES, OFF_K // LANES, OFF_V // LANES)]


def _tab_specs():
    return [pl.BlockSpec((S, LANES), lambda hp, g: (0, 0), pipeline_mode=pl.Buffered(1))] * 3


def _vec_spec():
    return pl.BlockSpec((1, LANES), lambda hp, g: (0, 0))


def _sub_rows(r, d, start, n):
    if d == 1:
        return pl.ds(start, n)
    return pl.ds(r + d * start, n, stride=d)


def _band_window(i, L):
    W = min(TQ + 2 * HALF_SPAN, L)
    q0 = pl.multiple_of(i * TQ, TQ)
    k0 = pl.multiple_of(jnp.clip(q0 - HALF_SPAN, 0, L - W), HALF_SPAN)
    qpos = q0 + (lax.broadcasted_iota(jnp.int32, (2 * TQ, W), 0) & (TQ - 1))
    kpos = k0 + lax.broadcasted_iota(jnp.int32, (2 * TQ, W), 1)
    valid = jnp.abs(qpos - kpos) <= HALF_SPAN
    return W, q0, k0, valid


def _stack_heads(t, lo):
    z = jnp.zeros_like(t)
    return jnp.concatenate([jnp.where(lo, t, z), jnp.where(lo, z, t)], axis=0)


def _unstack_heads(t2, lo):
    return jnp.where(lo, t2[0:TQ], t2[TQ:2 * TQ])


CHAINS = 8


def _interleave(d):
    ru = min(d, CHAINS)
    return ru, min(CHAINS // ru, S // d // TQ)


def _for_blocks(n, fn):
    if n == 1:
        fn(0)
    else:
        def it(j, _):
            fn(j)
            return 0
        lax.fori_loop(0, n, it, 0)


def attn_fwd(proj, tabs, qw2, kw2, sides=()):
    CH = 256

    def body(q_ref, k_ref, v_ref, c_ref, s1_ref, s2_ref, qw_ref, kw_ref, at_ref, ls_ref,
             qs, ks, vs, osub, lsub, onat, lnat, qn, kn):
        g = pl.program_id(1)
        lo = lax.broadcasted_iota(jnp.int32, (1, LANES), 1) < HD
        e = _head_mat()

        def prep(i, _):
            rows = pl.ds(pl.multiple_of(i * CH, CH), CH)
            c, s1, s2 = c_ref[rows, :], s1_ref[rows, :], s2_ref[rows, :]
            for t_ref, w_ref, out, scale in ((q_ref, qw_ref, qn, HD ** -0.5), (k_ref, kw_ref, kn, 1.0)):
                t = t_ref[rows, :]
                r = lax.rsqrt(_head_mean(t * t, e) + EPS)
                out[rows, :] = _rope(t * r * w_ref[...], c, s1, s2) * scale
            return 0

        lax.fori_loop(0, S // CH, prep, 0, unroll=4)

        def group(gi, d):
            L = S // d

            ru, nb = _interleave(d)

            def stage(r, off):
                for c0 in range(0, L, CH):
                    n = min(CH, L)
                    rows = _sub_rows(r, d, c0, n)
                    dst = pl.ds(off + c0, n)
                    qs[dst, :] = qn[rows, :].astype(BF16)
                    ks[dst, :] = kn[rows, :].astype(BF16)
                    vs[dst, :] = v_ref[rows, :].astype(BF16)

            def one(off, i):
                W, q0, k0, valid = _band_window(i, L)
                q2 = _stack_heads(qs[pl.ds(off + q0, TQ), :], lo)
                sc = jnp.where(valid, _dot_nt(q2, ks[pl.ds(off + k0, W), :]), NEG_INF)
                m = jnp.max(sc, axis=-1, keepdims=True)
                p = jnp.exp(sc - m)
                den = jnp.sum(p, axis=-1, keepdims=True)
                o2 = _dot(p.astype(BF16), vs[pl.ds(off + k0, W), :]) / den
                l2 = jnp.broadcast_to(m + jnp.log(den), (2 * TQ, LANES))
                osub[pl.ds(off + q0, TQ), :] = _unstack_heads(o2, lo)
                lsub[pl.ds(off + q0, TQ), :] = _unstack_heads(l2, lo)

            def unstage(r, off):
                for c0 in range(0, L, CH):
                    n = min(CH, L)
                    rows = _sub_rows(r, d, c0, n)
                    onat[gi, rows, :] = osub[pl.ds(off + c0, n), :]
                    lnat[gi, rows, :] = lsub[pl.ds(off + c0, n), :]

            def step(t, _):
                for u in range(ru):
                    stage(t * ru + u, u * L)
                _for_blocks(L // TQ // nb, lambda j: [one(u * L, j * nb + b) for u in range(ru) for b in range(nb)])
                for u in range(ru):
                    unstage(t * ru + u, u * L)
                return 0

            lax.fori_loop(0, d // ru, step, 0)

        for gi, d in enumerate(DILATIONS):
            pl.when(g == gi)(functools.partial(group, gi, d))

        @pl.when(g == len(DILATIONS) - 1)
        def _():
            def mix(i, _):
                rows = pl.ds(pl.multiple_of(i * CH, CH), CH)
                l0, l1, l2 = lnat[0, rows, :], lnat[1, rows, :], lnat[2, rows, :]
                m = jnp.maximum(jnp.maximum(l0, l1), l2)
                e0, e1, e2 = jnp.exp(l0 - m), jnp.exp(l1 - m), jnp.exp(l2 - m)
                den = e0 + e1 + e2
                a = (e0 * onat[0, rows, :] + e1 * onat[1, rows, :] + e2 * onat[2, rows, :]) / den
                at_ref[rows, :] = a.astype(BF16)
                ls_ref[rows, :] = m + jnp.log(den)
                return 0

            lax.fori_loop(0, S // CH, mix, 0)

    out_spec = pl.BlockSpec((S, LANES), lambda hp, g: (0, hp))
    return _call(
        body, sides, name="attn_fwd", grid=(4, 3),
        in_specs=_qk_specs() + _tab_specs() + [_vec_spec(), _vec_spec()],
        out_specs=[out_spec, out_spec],
        out_shape=[jax.ShapeDtypeStruct((S, CC), BF16), jax.ShapeDtypeStruct((S, CC), F32)],
        scratch_shapes=[pltpu.VMEM((S, LANES), BF16)] * 3 + [pltpu.VMEM((S, LANES), F32)] * 2
        + [pltpu.VMEM((3, S, LANES), F32)] * 2 + [pltpu.VMEM((S, LANES), F32)] * 2,
        args=(proj, proj, proj, *tabs, qw2, kw2))


def attn_bwd(proj, tabs, qw2, kw2, d_attn, attn, lse, sides=()):
    CH = 256

    def body(q_ref, k_ref, v_ref, c_ref, s1_ref, s2_ref, qw_ref, kw_ref, do_ref, at_ref, ls_ref,
             dq_ref, dk_ref, dv_ref, gqw_ref, gkw_ref,
             qs, ks, vs, dos, dsub, lsub, dqs, dks, dvs, dnat, qx, kx, dvn, tnq, tnk, rrq, rrk):
        hp, g = pl.program_id(0), pl.program_id(1)
        lo = lax.broadcasted_iota(jnp.int32, (1, LANES), 1) < HD
        e = _head_mat()
        both = ((q_ref, qw_ref, qx, tnq, rrq, HD ** -0.5), (k_ref, kw_ref, kx, tnk, rrk, 1.0))

        @pl.when((hp == 0) & (g == 0))
        def _():
            gqw_ref[...] = jnp.zeros_like(gqw_ref)
            gkw_ref[...] = jnp.zeros_like(gkw_ref)

        def prep(i, _):
            rows = pl.ds(pl.multiple_of(i * CH, CH), CH)
            dnat[rows, :] = _head_mean(do_ref[rows, :] * at_ref[rows, :].astype(F32), e) * float(HD)
            c, s1, s2 = c_ref[rows, :], s1_ref[rows, :], s2_ref[rows, :]
            for t_ref, w_ref, x, tn_s, rr_s, scale in both:
                t = t_ref[rows, :]
                rr = lax.rsqrt(_head_mean(t * t, e) + EPS)
                tn = t * rr
                rr_s[rows, :] = rr
                tn_s[rows, :] = tn
                x[rows, :] = _rope(tn * w_ref[...], c, s1, s2) * scale
            return 0

        lax.fori_loop(0, S // CH, prep, 0, unroll=4)

        def group(d):
            L = S // d

            ru, nb = _interleave(d)

            def stage(r, off):
                for c0 in range(0, L, CH):
                    n = min(CH, L)
                    rows = _sub_rows(r, d, c0, n)
                    dst = pl.ds(off + c0, n)
                    qs[dst, :] = qx[rows, :].astype(BF16)
                    ks[dst, :] = kx[rows, :].astype(BF16)
                    vs[dst, :] = v_ref[rows, :].astype(BF16)
                    dos[dst, :] = do_ref[rows, :].astype(BF16)
                    dsub[dst, :] = dnat[rows, :]
                    lsub[dst, :] = ls_ref[rows, :]
                    dks[dst, :] = jnp.zeros((n, LANES), F32)
                    dvs[dst, :] = jnp.zeros((n, LANES), F32)

            def one(off, i):
                W, q0, k0, valid = _band_window(i, L)
                qrows, krows = pl.ds(off + q0, TQ), pl.ds(off + k0, W)
                q2 = _stack_heads(qs[qrows, :], lo)
                do2 = _stack_heads(dos[qrows, :], lo)
                kk, vv = ks[krows, :], vs[krows, :]
                lse_b, dd_b = lsub[qrows, :], dsub[qrows, :]
                lse2 = jnp.concatenate([lse_b[:, 0:1], lse_b[:, HD:HD + 1]], axis=0)
                dd2 = jnp.concatenate([dd_b[:, 0:1], dd_b[:, HD:HD + 1]], axis=0)
                sc = jnp.where(valid, _dot_nt(q2, kk), NEG_INF)
                p = jnp.exp(sc - lse2)
                ds = (p * (_dot_nt(do2, vv) - dd2)).astype(BF16)
                dqs[qrows, :] = _unstack_heads(_dot(ds, kk), lo)
                dks[krows, :] = dks[krows, :] + _dot_tn(ds, q2)
                dvs[krows, :] = dvs[krows, :] + _dot_tn(p.astype(BF16), do2)

            def unstage(r, off):
                for c0 in range(0, L, CH):
                    n = min(CH, L)
                    rows = _sub_rows(r, d, c0, n)
                    src = pl.ds(off + c0, n)
                    qx[rows, :] = dqs[src, :]
                    kx[rows, :] = dks[src, :]
                    dvn[rows, :] = dvs[src, :]

            def step(t, _):
                for u in range(ru):
                    stage(t * ru + u, u * L)
                _for_blocks(L // TQ // nb, lambda j: [one(u * L, j * nb + b) for u in range(ru) for b in range(nb)])
                for u in range(ru):
                    unstage(t * ru + u, u * L)
                return 0

            lax.fori_loop(0, d // ru, step, 0)

        for gi, d in enumerate(DILATIONS):
            pl.when(g == gi)(functools.partial(group, d))

        def emit(i, _):
            rows = pl.ds(pl.multiple_of(i * CH, CH), CH)
            c, s1, s2 = c_ref[rows, :], s1_ref[rows, :], s2_ref[rows, :]
            for (_, w_ref, x, tn_s, rr_s, scale), out, gw_ref in zip(both, (dq_ref, dk_ref), (gqw_ref, gkw_ref)):
                tn = tn_s[rows, :]
                dy = _rope_t(x[rows, :] * scale, c, s1, s2)
                gw_ref[0:1, :] = gw_ref[0:1, :] + jnp.sum(dy * tn, axis=0, keepdims=True)
                dtn = dy * w_ref[...]
                out[rows, :] = (rr_s[rows, :] * (dtn - tn * _head_mean(dtn * tn, e))).astype(BF16)
            dv_ref[rows, :] = dvn[rows, :].astype(BF16)
            return 0

        lax.fori_loop(0, S // CH, emit, 0, unroll=4)

    nat_spec = pl.BlockSpec((S, LANES), lambda hp, g: (0, hp))
    out_spec = pl.BlockSpec((None, S, LANES), lambda hp, g: (g, 0, hp))
    acc_spec = pl.BlockSpec((8, LANES), lambda hp, g: (0, 0))
    return _call(
        body, sides, name="attn_bwd", grid=(4, 3),
        in_specs=_qk_specs() + _tab_specs() + [_vec_spec(), _vec_spec(), nat_spec, nat_spec, nat_spec],
        out_specs=[out_spec] * 3 + [acc_spec] * 2,
        out_shape=[jax.ShapeDtypeStruct((QKV // PLANE, S, PLANE), BF16)] * 3 + [jax.ShapeDtypeStruct((8, LANES), F32)] * 2,
        scratch_shapes=[pltpu.VMEM((S, LANES), BF16)] * 4 + [pltpu.VMEM((S, LANES), F32)] * 13,
        args=(proj, proj, proj, *tabs, qw2, kw2, d_attn, attn, lse))


PADR = 16
CT = 128


def _conv_specs():
    return [pl.BlockSpec((S, CC), lambda i: (0, OFF_CA // CC)), pl.BlockSpec((S, CC), lambda i: (0, OFF_CB // CC))]


NCB = CC // LANES


def _pad_zero(pad):
    for cb in range(NCB):
        pad[cb, 0:PADR, :] = jnp.zeros((PADR, LANES), F32)
        pad[cb, PADR + S:PADR + S + PADR, :] = jnp.zeros((PADR, LANES), F32)


def _pad_store(pad, row0, n, val):
    for cb in range(NCB):
        pad[cb, pl.ds(pl.multiple_of(row0 + PADR, 8), n), :] = val[:, cb * LANES:(cb + 1) * LANES]


def _taps(pad_ref, cb, s0, weights):
    acc = jnp.zeros((CT, LANES), F32)
    for k in range(KW):
        acc = acc + weights[k] * pad_ref[cb, pl.ds(s0 + k + 1, CT), :]
    return acc


def conv_fwd(proj, conv_w, conv_b, ln_w, ln_b, sides=()):
    def body(a_ref, b_ref, w_ref, cb_ref, lw_ref, lb_ref, c_ref, u3_ref, upad):
        _pad_zero(upad)

        def glu(i, _):
            rows = pl.ds(pl.multiple_of(i * TM, TM), TM)
            _pad_store(upad, i * TM, TM, a_ref[rows, :] * _sigmoid(b_ref[rows, :]))
            return 0

        lax.fori_loop(0, S // TM, glu, 0)

        def chunk(i, _):
            s0 = pl.multiple_of(i * CT, CT)
            for cb in range(CC // LANES):
                cols = slice(cb * LANES, (cb + 1) * LANES)
                w = [w_ref[k:k + 1, cols] for k in range(KW)]
                c_ref[pl.ds(s0, CT), cols] = _taps(upad, cb, s0, w) + cb_ref[:, cols]
            cv = c_ref[pl.ds(s0, CT), :]
            mu = jnp.mean(cv, axis=-1, keepdims=True)
            xc = cv - mu
            rstd = lax.rsqrt(jnp.mean(xc * xc, axis=-1, keepdims=True) + EPS)
            yl = xc * rstd * lw_ref[...] + lb_ref[...]
            u3_ref[pl.ds(s0, CT), :] = (yl * _sigmoid(yl)).astype(BF16)
            return 0

        lax.fori_loop(0, S // CT, chunk, 0)

    vec = pl.BlockSpec((1, CC), lambda i: (0, 0))
    full = pl.BlockSpec((S, CC), lambda i: (0, 0))
    return _call(
        body, sides, name="conv_fwd", grid=(1,),
        in_specs=_conv_specs() + [pl.BlockSpec((KW, CC), lambda i: (0, 0)), vec, vec, vec],
        out_specs=[full, full],
        out_shape=[jax.ShapeDtypeStruct((S, CC), F32), jax.ShapeDtypeStruct((S, CC), BF16)],
        scratch_shapes=[pltpu.VMEM((NCB, S + 2 * PADR, LANES), F32)],
        args=(proj, proj, conv_w, conv_b, ln_w, ln_b))


def conv_bwd(proj, cpre, d_u3, conv_w, conv_w_rev, ln_w, ln_b, sides=()):
    def body(a_ref, b_ref, c_ref, du3_ref, w_ref, wr_ref, lw_ref, lb_ref,
             dc_ref, gw_ref, gcb_ref, glw_ref, glb_ref, upad, dpad):
        _pad_zero(upad)
        _pad_zero(dpad)
        gw_ref[...] = jnp.zeros_like(gw_ref)

        def ln_bwd(i, carry):
            gcb, glw, glb = carry
            rows = pl.ds(pl.multiple_of(i * TM, TM), TM)
            _pad_store(upad, i * TM, TM, a_ref[rows, :] * _sigmoid(b_ref[rows, :]))
            cv = c_ref[rows, :]
            mu = jnp.mean(cv, axis=-1, keepdims=True)
            xc = cv - mu
            rstd = lax.rsqrt(jnp.mean(xc * xc, axis=-1, keepdims=True) + EPS)
            xh = xc * rstd
            yl = xh * lw_ref[...] + lb_ref[...]
            dyl = du3_ref[rows, :] * _dsilu(yl, _sigmoid(yl))
            dxh = dyl * lw_ref[...]
            dcv = rstd * (dxh - jnp.mean(dxh, axis=-1, keepdims=True)
                          - xh * jnp.mean(dxh * xh, axis=-1, keepdims=True))
            _pad_store(dpad, i * TM, TM, dcv)
            return (gcb + jnp.sum(dcv, axis=0, keepdims=True),
                    glw + jnp.sum(dyl * xh, axis=0, keepdims=True),
                    glb + jnp.sum(dyl, axis=0, keepdims=True))

        z = jnp.zeros((1, CC), F32)
        gcb, glw, glb = lax.fori_loop(0, S // TM, ln_bwd, (z, z, z))
        gcb_ref[...] = gcb
        glw_ref[...] = glw
        glb_ref[...] = glb

        def chunk(i, _):
            s0 = pl.multiple_of(i * CT, CT)
            for cb in range(CC // LANES):
                cols = slice(cb * LANES, (cb + 1) * LANES)
                wr = [wr_ref[k:k + 1, cols] for k in range(KW)]
                du = _taps(dpad, cb, s0, wr)
                dcv = dpad[cb, pl.ds(s0 + PADR, CT), :]
                for k in range(KW):
                    gw_ref[k:k + 1, cols] = gw_ref[k:k + 1, cols] + jnp.sum(
                        upad[cb, pl.ds(s0 + k + 1, CT), :] * dcv, axis=0, keepdims=True)
                av = a_ref[pl.ds(s0, CT), cols]
                sb = _sigmoid(b_ref[pl.ds(s0, CT), cols])
                dc_ref[0, pl.ds(s0, CT), cols] = (du * sb).astype(BF16)
                dc_ref[1, pl.ds(s0, CT), cols] = (du * av * sb * (1.0 - sb)).astype(BF16)
            return 0

        lax.fori_loop(0, S // CT, chunk, 0)

    vec = pl.BlockSpec((1, CC), lambda i: (0, 0))
    full = pl.BlockSpec((S, CC), lambda i: (0, 0))
    wsp = pl.BlockSpec((KW, CC), lambda i: (0, 0))
    return _call(
        body, sides, name="conv_bwd", grid=(1,),
        in_specs=_conv_specs() + [full, full, wsp, wsp, vec, vec],
        out_specs=[pl.BlockSpec((2, S, CC), lambda i: (0, 0, 0)), wsp, vec, vec, vec],
        out_shape=[jax.ShapeDtypeStruct((2, S, CC), BF16), jax.ShapeDtypeStruct((KW, CC), F32)]
        + [jax.ShapeDtypeStruct((1, CC), F32)] * 3,
        scratch_shapes=[pltpu.VMEM((NCB, S + 2 * PADR, LANES), F32)] * 2,
        args=(proj, proj, cpre, d_u3, conv_w, conv_w_rev, ln_w, ln_b))


def _gate_specs():
    return [_row(CC, col=OFF_GA // CC + j) for j in range(4)]


def _gates(g_refs, bg_ref):
    ga = _sigmoid(jnp.concatenate([g_refs[0][...], g_refs[1][...]], axis=1) + bg_ref[0:1, :])
    gb = _sigmoid(jnp.concatenate([g_refs[2][...], g_refs[3][...]], axis=1) + bg_ref[1:2, :])
    return ga, gb


def mix_out(x, proj, b_gate, attn, u3, w_o, w_pw, w_out):
    def body(x_ref, g0, g1, g2, g3, bg_ref, at_ref, u3_ref, wo_ref, wp_ref, wout_ref,
             x1_ref, z_ref, ya_ref, yb_ref):
        ga, gb = _gates((g0, g1, g2, g3), bg_ref)
        ya = _dot(at_ref[...], wo_ref[...])
        yb = _dot(u3_ref[...], wp_ref[...])
        z = (ga * ya + gb * yb).astype(BF16)
        ya_ref[...] = ya.astype(BF16)
        yb_ref[...] = yb.astype(BF16)
        z_ref[...] = z
        x1_ref[...] = x_ref[...] + _dot(z, wout_ref[...])

    return pl.pallas_call(
        body, name="mix_out", grid=(S // TM,),
        in_specs=[_row(D)] + _gate_specs() + [_res((2, D)), _row(CC), _row(CC),
                                              _res((CC, D)), _res((CC, D)), _res((D, D))],
        out_specs=[_row(D)] * 4,
        out_shape=[jax.ShapeDtypeStruct((S, D), F32)] + [jax.ShapeDtypeStruct((S, D), BF16)] * 3,
        compiler_params=_cp(dimension_semantics=("arbitrary",)),
    )(x, proj, proj, proj, proj, b_gate, attn, u3, w_o, w_pw, w_out)


def out_bwd(d_x1b, proj, b_gate, ya, yb, w_o, w_pw, w_out, sides=()):
    def body(dx_ref, g0, g1, g2, g3, bg_ref, ya_ref, yb_ref, wo_ref, wp_ref, wout_ref,
             dya_ref, dyb_ref, dgl_ref, dat_ref, du3_ref, gbg_ref):
        @pl.when(pl.program_id(0) == 0)
        def _():
            gbg_ref[...] = jnp.zeros_like(gbg_ref)

        ga, gb = _gates((g0, g1, g2, g3), bg_ref)
        dz = _dot_nt(dx_ref[...], wout_ref[...])
        dya = (dz * ga).astype(BF16)
        dyb = (dz * gb).astype(BF16)
        dgla = dz * ya_ref[...].astype(F32) * ga * (1.0 - ga)
        dglb = dz * yb_ref[...].astype(F32) * gb * (1.0 - gb)
        dya_ref[...] = dya
        dyb_ref[...] = dyb
        for j in range(2):
            dgl_ref[j] = dgla[:, j * PLANE:(j + 1) * PLANE].astype(BF16)
            dgl_ref[2 + j] = dglb[:, j * PLANE:(j + 1) * PLANE].astype(BF16)
        gbg_ref[0:1, :] = gbg_ref[0:1, :] + jnp.sum(dgla, axis=0, keepdims=True)
        gbg_ref[1:2, :] = gbg_ref[1:2, :] + jnp.sum(dglb, axis=0, keepdims=True)
        dat_ref[...] = _dot_nt(dya, wo_ref[...])
        du3_ref[...] = _dot_nt(dyb, wp_ref[...])

    return _call(
        body, sides, name="out_bwd", grid=(S // TM,),
        in_specs=[_row(D)] + _gate_specs() + [_res((2, D)), _row(D), _row(D),
                                              _res((CC, D)), _res((CC, D)), _res((D, D))],
        out_specs=[_row(D), _row(D), _planes(2 * D), _row(CC), _row(CC), pl.BlockSpec((2, D), lambda i: (0, 0))],
        out_shape=[jax.ShapeDtypeStruct((S, D), BF16)] * 2 + [jax.ShapeDtypeStruct((2 * D // PLANE, S, PLANE), BF16)]
        + [jax.ShapeDtypeStruct((S, CC), F32)] * 2 + [jax.ShapeDtypeStruct((2, D), F32)],
        args=(d_x1b, proj, proj, proj, proj, b_gate, ya, yb, w_o, w_pw, w_out))


def ffn_in(x1, norm_w, w_ffn_in, sides=()):
    half = FF // 2

    def body(x_ref, nw_ref, w_ref, h_ref, gu_ref, f_ref):
        xv = x_ref[...]
        r = lax.rsqrt(jnp.mean(xv * xv, axis=-1, keepdims=True) + EPS)
        h = (xv * r * nw_ref[...]).astype(BF16)
        h_ref[...] = h
        for j in range(2):
            gt = _dot_nt(h, w_ref[j * half:(j + 1) * half, :])
            up = _dot_nt(h, w_ref[FF + j * half:FF + (j + 1) * half, :])
            gu_ref[:, j * half:(j + 1) * half] = gt.astype(BF16)
            gu_ref[:, FF + j * half:FF + (j + 1) * half] = up.astype(BF16)
            f_ref[:, j * half:(j + 1) * half] = (gt * _sigmoid(gt) * up).astype(BF16)

    return _call(
        body, sides, name="ffn_in", grid=(S // TM,),
        in_specs=[_row(D), _res((1, D)), _res((2 * FF, D))],
        out_specs=[_row(D), _row(2 * FF), _row(FF)],
        out_shape=[jax.ShapeDtypeStruct((S, D), BF16), jax.ShapeDtypeStruct((S, 2 * FF), BF16),
                   jax.ShapeDtypeStruct((S, FF), BF16)],
        args=(x1, norm_w, w_ffn_in))


def ffn_out_loss(x1, f, w_ffn_out, target):
    def body(x_ref, f_ref, w_ref, t_ref, dy_ref, dyb_ref, sq_ref):
        @pl.when(pl.program_id(0) == 0)
        def _():
            sq_ref[...] = jnp.zeros_like(sq_ref)

        diff = x_ref[...] + _dot(f_ref[...], w_ref[...]) - t_ref[...]
        dy = diff * (1.0 / D)
        dy_ref[...] = dy
        dyb_ref[...] = dy.astype(BF16)
        sq_ref[...] = sq_ref[...] + jnp.sum((diff * diff).reshape(TM // 8, 8, D), axis=0)

    return pl.pallas_call(
        body, name="ffn_out_loss", grid=(S // TM,),
        in_specs=[_row(D), _row(FF), _res((FF, D)), _row(D)],
        out_specs=[_row(D), _row(D), pl.BlockSpec((8, D), lambda i: (0, 0))],
        out_shape=[jax.ShapeDtypeStruct((S, D), F32), jax.ShapeDtypeStruct((S, D), BF16),
                   jax.ShapeDtypeStruct((8, D), F32)],
        compiler_params=_cp(dimension_semantics=("arbitrary",)),
    )(x1, f, w_ffn_out, target)


def _rms_bwd(xv, nw, dh):
    r = lax.rsqrt(jnp.mean(xv * xv, axis=-1, keepdims=True) + EPS)
    xn = xv * r
    dxn = dh * nw
    dx = r * (dxn - xn * jnp.mean(dxn * xn, axis=-1, keepdims=True))
    return dx, dh * xn


def ffn_bwd(dy, dyb, gu, x1, norm_w, w_ffn_in, w_ffn_out, sides=()):
    def body(dy_ref, dyb_ref, gu_ref, x_ref, nw_ref, wi_ref, wo_ref, dgu_ref, dx_ref, dxb_ref, gn_ref):
        @pl.when(pl.program_id(0) == 0)
        def _():
            gn_ref[...] = jnp.zeros_like(gn_ref)

        df = _dot_nt(dyb_ref[...], wo_ref[...])
        gt = gu_ref[:, 0:FF].astype(F32)
        up = gu_ref[:, FF:2 * FF].astype(F32)
        sg = _sigmoid(gt)
        dgt = (df * up * _dsilu(gt, sg)).astype(BF16)
        dup = (df * gt * sg).astype(BF16)
        dgu_ref[:, 0:FF] = dgt
        dgu_ref[:, FF:2 * FF] = dup
        dh = _dot(dgt, wi_ref[0:FF, :]) + _dot(dup, wi_ref[FF:2 * FF, :])
        dxn, gw = _rms_bwd(x_ref[...], nw_ref[...], dh)
        dx = dy_ref[...] + dxn
        dx_ref[...] = dx
        dxb_ref[...] = dx.astype(BF16)
        gn_ref[...] = gn_ref[...] + jnp.sum(gw, axis=0, keepdims=True)

    return _call(
        body, sides, name="ffn_bwd", grid=(S // TM,),
        in_specs=[_row(D), _row(D), _row(2 * FF), _row(D), _res((1, D)), _res((2 * FF, D)), _res((FF, D))],
        out_specs=[_row(2 * FF), _row(D), _row(D), pl.BlockSpec((1, D), lambda i: (0, 0))],
        out_shape=[jax.ShapeDtypeStruct((S, 2 * FF), BF16), jax.ShapeDtypeStruct((S, D), F32),
                   jax.ShapeDtypeStruct((S, D), BF16), jax.ShapeDtypeStruct((1, D), F32)],
        args=(dy, dyb, gu, x1, norm_w, w_ffn_in, w_ffn_out))


def in_bwd(d_q, d_k, d_v, d_conv, d_gl, w_in, x, d_x1, norm_w, sides=()):
    segs = ((OFF_Q, QKV), (OFF_K, QKV), (OFF_V, QKV), (OFF_CA, 2 * CC), (OFF_GA, 2 * D))

    def body(dq_ref, dk_ref, dv_ref, dc_ref, dg_ref, w_ref, x_ref, dx1_ref, nw_ref, gx_ref, gn_ref):
        @pl.when(pl.program_id(0) == 0)
        def _():
            gn_ref[...] = jnp.zeros_like(gn_ref)

        dh = jnp.zeros((TM, D), F32)
        for ref, (off, width) in zip((dq_ref, dk_ref, dv_ref, dc_ref, dg_ref), segs):
            for j in range(width // PLANE):
                dh = dh + _dot(ref[j], w_ref[off + j * PLANE:off + (j + 1) * PLANE, :])
        dxn, gw = _rms_bwd(x_ref[...], nw_ref[...], dh)
        gx_ref[...] = dx1_ref[...] + dxn
        gn_ref[...] = gn_ref[...] + jnp.sum(gw, axis=0, keepdims=True)

    return _call(
        body, sides, name="in_bwd", grid=(S // TM,),
        in_specs=[_planes(QKV)] * 3 + [_planes(2 * CC), _planes(2 * D), _res((INW, D)), _row(D), _row(D), _res((1, D))],
        out_specs=[_row(D), pl.BlockSpec((1, D), lambda i: (0, 0))],
        out_shape=[jax.ShapeDtypeStruct((S, D), F32), jax.ShapeDtypeStruct((1, D), F32)],
        args=(d_q, d_k, d_v, d_conv, d_gl, w_in, x, d_x1, norm_w))


def mm_tn(name, a, b, tm, tn, sides=()):
    M, N = a.shape[1], b.shape[1]

    def body(a_ref, b_ref, o_ref):
        o_ref[...] = _dot_tn(a_ref[...], b_ref[...])

    res = _call(
        body, sides, name=name, grid=(M // tm, N // tn),
        in_specs=[pl.BlockSpec((S, tm), lambda i, j: (0, i)), pl.BlockSpec((S, tn), lambda i, j: (0, j))],
        out_specs=[pl.BlockSpec((tm, tn), lambda i, j: (i, j))],
        out_shape=[jax.ShapeDtypeStruct((M, N), F32)],
        args=(a, b))
    return (res[0][0], res[1]) if sides else res[0]


GW_IN_TN = PLANE
GW_IN_SPLIT = (768, 256)


def gw_in_t(name, ht, d_segs, col0, hw, sides=()):
    tn = GW_IN_TN
    starts, t0 = [], 0
    for seg in d_segs:
        starts.append(t0)
        t0 += seg.shape[0]
    ntiles = [seg.shape[0] for seg in d_segs]

    def body(h_ref, *refs):
        a_refs, o_ref = refs[:-1], refs[-1]
        n = pl.program_id(0)
        for a_ref, st, nt in zip(a_refs, starts, ntiles):
            @pl.when((n >= st) & (n < st + nt))
            def _(a_ref=a_ref):
                o_ref[...] = _dot(h_ref[...], a_ref[...]).T

    def seg_spec(st, nt):
        return pl.BlockSpec((None, S, tn), lambda n: (jnp.clip(n - st, 0, nt - 1), 0, 0))

    res = _call(
        body, sides, name=name, grid=(INW // tn,),
        in_specs=[pl.BlockSpec((hw, S), lambda n: (col0 // hw, 0))] + [seg_spec(st, nt) for st, nt in zip(starts, ntiles)],
        out_specs=[pl.BlockSpec((tn, hw), lambda n: (n, 0))],
        out_shape=[jax.ShapeDtypeStruct((INW, hw), F32)],
        args=(ht, *d_segs))
    return (res[0][0], res[1]) if sides else res[0]


def _place():
    x, y, c = lax.axis_index("x"), lax.axis_index("y"), lax.axis_index("c")
    chips = [(1 - x, y), (x, 1 - y), (1 - x, 1 - y)]
    return x, y, c, chips


def _sems(n):
    return pltpu.SemaphoreType.DMA((n,))


def _remote(src, dst, send, recv, k, to):
    return pltpu.make_async_remote_copy(src_ref=src, dst_ref=dst, send_sem=send.at[k], recv_sem=recv.at[k],
                                        device_id=to, device_id_type=MESH)


def _cast_rows(dst, src, cols=slice(None)):
    rows = src.shape[0]
    step = next((s for s in (128, 64, 32, 16) if rows % s == 0), rows)
    for r0 in range(0, rows, step):
        dst[r0:r0 + step, cols] = src[r0:r0 + step, :].astype(dst.dtype)


def comm_only(name, sides):
    def body():
        pass

    return _call(body, sides, name=name, grid=(1,), in_specs=[], out_specs=[], out_shape=[], args=())[1]


def ag_blocks(shard, dtype):
    R, W = shard.shape

    def copy(outs, scr, k, block, to, src=None):
        dst = outs[0].at[block]
        return _remote(dst if src is None else src, dst, scr[1], scr[2], k, to)

    def local(outs, scr, me):
        return pltpu.make_async_copy(scr[0], outs[0].at[me], scr[3].at[0])

    def start(ins, outs, scr):
        x, y, c, chips = _place()
        me = 4 * x + 2 * y + c
        _cast_rows(scr[0], ins[0])
        local(outs, scr, me).start()
        copy(outs, scr, 0, me, (x, y, 1 - c), src=scr[0]).start()
        for j, (cx, cy) in enumerate(chips):
            copy(outs, scr, 1 + j, me, (cx, cy, c), src=scr[0]).start()

    def finish(ins, outs, scr):
        x, y, c, chips = _place()
        me, sib = 4 * x + 2 * y + c, (x, y, 1 - c)
        passed = []
        for j, (cx, cy) in enumerate(chips):
            theirs = 4 * cx + 2 * cy + c
            copy(outs, scr, 1 + j, theirs, (x, y, c)).wait_recv()
            fwd = copy(outs, scr, 4 + j, theirs, sib)
            fwd.start()
            passed.append(fwd)
        copy(outs, scr, 0, 4 * x + 2 * y + 1 - c, (x, y, c)).wait_recv()
        for j, (cx, cy) in enumerate(chips):
            copy(outs, scr, 4 + j, 4 * cx + 2 * cy + 1 - c, (x, y, c)).wait_recv()
        copy(outs, scr, 0, me, sib, src=scr[0]).wait_send()
        for j, (cx, cy) in enumerate(chips):
            copy(outs, scr, 1 + j, me, (cx, cy, c), src=scr[0]).wait_send()
        for fwd in passed:
            fwd.wait_send()
        local(outs, scr, me).wait()

    return Side((shard,), (VMEM,), (jax.ShapeDtypeStruct((NDEV, R, W), dtype),),
                (pltpu.VMEM((R, W), dtype), _sems(7), _sems(7), _sems(1)), start, finish, None, "dsxy")


def ag_blocks_relay(shard, dtype):
    R, W = shard.shape
    half = R // 2

    def copy(outs, scr, k, block, to, src=None, rows=None):
        dst = outs[0].at[block] if rows is None else outs[0].at[block, pl.ds(rows * half, half), :]
        return _remote(dst if src is None else src, dst, scr[1], scr[2], k, to)

    def local(outs, scr, me):
        return pltpu.make_async_copy(scr[0], outs[0].at[me], scr[3].at[0])

    def own(outs, scr):
        x, y, c, _ = _place()
        me = 4 * x + 2 * y + c
        return [copy(outs, scr, k, me, to, src=scr[0])
                for k, to in enumerate([(x, y, 1 - c), (1 - x, y, c), (x, 1 - y, c)])]

    def start(ins, outs, scr):
        x, y, c, _ = _place()
        _cast_rows(scr[0], ins[0])
        local(outs, scr, 4 * x + 2 * y + c).start()
        for cp in own(outs, scr):
            cp.start()

    def passed_on(outs, scr):
        x, y, c, _ = _place()
        sib, xn, yn = (x, y, 1 - c), (1 - x, y, c), (x, 1 - y, c)
        b_xn, b_yn, b_dg = 4 * (1 - x) + 2 * y + c, 4 * x + 2 * (1 - y) + c, 4 * (1 - x) + 2 * (1 - y) + c
        near = [copy(outs, scr, 5, b_xn, yn, rows=0), copy(outs, scr, 3, b_xn, sib),
                copy(outs, scr, 6, b_yn, xn, rows=1), copy(outs, scr, 4, b_yn, sib)]
        far = [copy(outs, scr, 7, b_dg, sib, rows=0), copy(outs, scr, 8, b_dg, sib, rows=1)]
        return (b_xn, b_yn, b_dg), near, far

    def mid(ins, outs, scr):
        x, y, c, _ = _place()
        (b_xn, b_yn, _), near, _ = passed_on(outs, scr)
        copy(outs, scr, 1, b_xn, (x, y, c)).wait_recv()
        near[0].start()
        near[1].start()
        copy(outs, scr, 2, b_yn, (x, y, c)).wait_recv()
        near[2].start()
        near[3].start()

    def finish(ins, outs, scr):
        x, y, c, _ = _place()
        here = (x, y, c)
        (b_xn, b_yn, b_dg), near, far = passed_on(outs, scr)
        copy(outs, scr, 5, b_dg, here, rows=0).wait_recv()
        far[0].start()
        copy(outs, scr, 6, b_dg, here, rows=1).wait_recv()
        far[1].start()
        flip = 1 - 2 * c
        copy(outs, scr, 0, 4 * x + 2 * y + 1 - c, here).wait_recv()
        copy(outs, scr, 3, b_xn + flip, here).wait_recv()
        copy(outs, scr, 4, b_yn + flip, here).wait_recv()
        copy(outs, scr, 7, b_dg + flip, here, rows=0).wait_recv()
        copy(outs, scr, 8, b_dg + flip, here, rows=1).wait_recv()
        for cp in own(outs, scr) + near + far:
            cp.wait_send()
        local(outs, scr, 4 * x + 2 * y + c).wait()

    return Side((shard,), (VMEM,), (jax.ShapeDtypeStruct((NDEV, R, W), dtype),),
                (pltpu.VMEM((R, W), dtype), _sems(9), _sems(9), _sems(1)), start, finish, mid, "sxy")


def ag_cols(shard):
    K, C = shard.shape
    half, w2 = K // 2, 2 * C

    def win(out, rows_c, chip):
        return out.at[pl.ds(pl.multiple_of(rows_c * half, 16), half), pl.ds(pl.multiple_of(chip * w2, LANES), w2)]

    def ici(outs, scr, j, to, c, k):
        slab, send, recv = scr[2], scr[5], scr[6]
        return _remote(slab.at[pl.ds(pl.multiple_of(c * half, 16), half), :], win(outs[0], c, k), send, recv, j, to)

    def local(outs, scr, k):
        return pltpu.make_async_copy(scr[2], outs[0].at[:, pl.ds(pl.multiple_of(k * w2, LANES), w2)], scr[7].at[0])

    def start(ins, outs, scr):
        stage, inbox, slab, xs, xr = scr[:5]
        x, y, c, chips = _place()
        k = 2 * x + y
        _cast_rows(stage, ins[0])
        swap = _remote(stage, inbox, xs, xr, 0, (x, y, 1 - c))
        swap.start()
        for cc in range(2):
            @pl.when(c == cc)
            def _(cc=cc):
                _cast_rows(slab, stage, slice(cc * C, (cc + 1) * C))
        swap.wait()
        for cc in range(2):
            @pl.when(c == cc)
            def _(cc=cc):
                _cast_rows(slab, inbox, slice((1 - cc) * C, (2 - cc) * C))
        local(outs, scr, k).start()
        for j, (cx, cy) in enumerate(chips):
            ici(outs, scr, j, (cx, cy, c), c, k).start()

    def finish(ins, outs, scr):
        send, recv = scr[5], scr[6]
        x, y, c, chips = _place()
        k, sib = 2 * x + y, (x, y, 1 - c)
        passed = []
        for j, (cx, cy) in enumerate(chips):
            w = win(outs[0], c, 2 * cx + cy)
            _remote(w, w, send, recv, j, sib).wait_recv()
            fwd = _remote(w, w, send, recv, 3 + j, sib)
            fwd.start()
            passed.append(fwd)
        for j, (cx, cy) in enumerate(chips):
            w = win(outs[0], 1 - c, 2 * cx + cy)
            _remote(w, w, send, recv, 3 + j, sib).wait_recv()
        for j, (cx, cy) in enumerate(chips):
            ici(outs, scr, j, (cx, cy, c), c, k).wait_send()
        for fwd in passed:
            fwd.wait_send()
        local(outs, scr, k).wait()

    return Side((shard,), (VMEM,), (jax.ShapeDtypeStruct((K, NDEV * C), BF16),),
                (pltpu.VMEM((K, C), BF16), pltpu.VMEM((K, C), BF16), pltpu.VMEM((K, w2), BF16),
                 _sems(1), _sems(1), _sems(6), _sems(6), _sems(1)), start, finish, None, "dsxy")


def copies_side(args, out_shape, n_copies, plan, peers):
    def copies(ins, outs, scr):
        return [_remote(s_, d_, scr[0], scr[1], i, to) for i, (s_, d_, to) in enumerate(plan(ins, outs))]

    def start(ins, outs, scr):
        for cp in copies(ins, outs, scr):
            cp.start()

    def finish(ins, outs, scr):
        for cp in copies(ins, outs, scr):
            cp.wait()

    return Side(tuple(args), (ANY,) * len(args), tuple(out_shape), (_sems(n_copies), _sems(n_copies)),
                start, finish, None, peers)


def rs_to_sibling(grads):
    out_shape = [jax.ShapeDtypeStruct((4,) + g.shape[1:] if kind == "rows" else (g.shape[0] // 2, g.shape[1]), F32)
                 for kind, g in grads]

    def plan(ins, outs):
        x, y, c, _ = _place()
        sib, res = (x, y, 1 - c), []
        for (kind, _), g, r in zip(grads, ins, outs):
            if kind == "rows":
                res += [(g.at[2 * k + 1 - c], r.at[k], sib) for k in range(4)]
            else:
                half = g.shape[0] // 2
                res.append((g.at[pl.ds(pl.multiple_of((1 - c) * half, 8), half), :], r, sib))
        return res

    return copies_side([g for _, g in grads], out_shape, sum(4 if kind == "rows" else 1 for kind, _ in grads), plan, "s")


def rs_to_chips(parts):
    out_shape = [jax.ShapeDtypeStruct((3,) + p.shape[1:] if kind == "rows" else (3, p.shape[0], p.shape[1] // 4), BF16)
                 for kind, p in parts]

    def plan(ins, outs):
        x, y, c, chips = _place()
        res = []
        for (kind, _), p, r in zip(parts, ins, outs):
            for j, (cx, cy) in enumerate(chips):
                if kind == "rows":
                    src = p.at[2 * cx + cy]
                else:
                    w2 = p.shape[1] // 4
                    src = p.at[:, pl.ds(pl.multiple_of((2 * cx + cy) * w2, LANES), w2)]
                res.append((src, r.at[j], (cx, cy, c)))
        return res

    return copies_side([p for _, p in parts], out_shape, 3 * len(parts), plan, "dxy")


def rs_swap_halves(theirs):
    def plan(ins, outs):
        x, y, c, _ = _place()
        return [(t, r, (x, y, 1 - c)) for t, r in zip(ins, outs)]

    return copies_side(theirs, [jax.ShapeDtypeStruct(t.shape, F32) for t in theirs], len(theirs), plan, "s")


def _row_tiles(rows):
    return 2 if rows % 32 == 0 and rows >= 512 else 1


def chip_sum(name, grad, recv, c_idx, chip_idx):
    _, R, C = grad.shape
    nt = 1
    tr = R // nt

    def body(s_ref, g_ref, r_ref, p_ref, own_ref):
        k = pl.program_id(1)
        tot = g_ref[0] + r_ref[0]
        p_ref[0] = tot.astype(BF16)

        @pl.when(k == s_ref[1])
        def _():
            own_ref[...] = tot

    grid_spec = pltpu.PrefetchScalarGridSpec(
        num_scalar_prefetch=1, grid=(nt, 4),
        in_specs=[pl.BlockSpec((1, tr, C), lambda i, k, s: (2 * k + s[0], i, 0)),
                  pl.BlockSpec((1, tr, C), lambda i, k, s: (k, i, 0))],
        out_specs=[pl.BlockSpec((1, tr, C), lambda i, k, s: (k, i, 0)),
                   pl.BlockSpec((tr, C), lambda i, k, s: (i, 0))])
    return pl.pallas_call(
        body, name=name, grid_spec=grid_spec,
        out_shape=[jax.ShapeDtypeStruct((4, R, C), BF16), jax.ShapeDtypeStruct((R, C), F32)],
        compiler_params=_cp(dimension_semantics=("arbitrary", "arbitrary")),
    )(jnp.stack([c_idx, chip_idx]), grad, recv)


def _half_tiles(half):
    return 2 if half >= 512 else 1


def chip_sum_cols(name, grad, recv, c_idx, chip_idx):
    K, W = grad.shape
    half, w2 = K // 2, W // 4
    nt = _half_tiles(half)
    tr = half // nt

    def body(s_ref, g_ref, r_ref, p_ref, own_ref):
        tot = g_ref[...] + r_ref[...]
        p_ref[...] = tot.astype(BF16)

        @pl.when(pl.program_id(1) == s_ref[1])
        def _():
            own_ref[...] = tot

    grid_spec = pltpu.PrefetchScalarGridSpec(
        num_scalar_prefetch=1, grid=(nt, 4),
        in_specs=[pl.BlockSpec((tr, w2), lambda i, k, s: (s[0] * nt + i, k)),
                  pl.BlockSpec((tr, w2), lambda i, k, s: (i, k))],
        out_specs=[pl.BlockSpec((tr, w2), lambda i, k, s: (i, k)),
                   pl.BlockSpec((tr, w2), lambda i, k, s: (i, 0))])
    return pl.pallas_call(
        body, name=name, grid_spec=grid_spec,
        out_shape=[jax.ShapeDtypeStruct((half, W), BF16), jax.ShapeDtypeStruct((half, w2), F32)],
        compiler_params=_cp(dimension_semantics=("arbitrary", "arbitrary")),
    )(jnp.stack([c_idx, chip_idx]), grad, recv)


def col_final(name, own, recv, c_idx):
    half, w2 = own.shape
    C = w2 // 2
    nt = _half_tiles(half)
    tr = half // nt

    def body(s_ref, o_ref, r_ref, mine_ref, theirs_ref, t_ref):
        t_ref[...] = o_ref[...] + r_ref[0].astype(F32) + r_ref[1].astype(F32) + r_ref[2].astype(F32)
        for cc in range(2):
            @pl.when(s_ref[0] == cc)
            def _(cc=cc):
                mine_ref[...] = t_ref[:, cc * C:(cc + 1) * C]
                theirs_ref[...] = t_ref[:, (1 - cc) * C:(2 - cc) * C]

    grid_spec = pltpu.PrefetchScalarGridSpec(
        num_scalar_prefetch=1, grid=(nt,),
        in_specs=[pl.BlockSpec((tr, w2), lambda i, s: (i, 0)), pl.BlockSpec((3, tr, w2), lambda i, s: (0, i, 0))],
        out_specs=[pl.BlockSpec((tr, C), lambda i, s: (i, 0))] * 2,
        scratch_shapes=[pltpu.VMEM((tr, w2), F32)])
    return pl.pallas_call(
        body, name=name, grid_spec=grid_spec, out_shape=[jax.ShapeDtypeStruct((half, C), F32)] * 2,
        compiler_params=_cp(dimension_semantics=("arbitrary",)),
    )(jnp.stack([c_idx]), own, recv)


def _adamw(w, g, m, v):
    m2 = ADAM_B1 * m + (1.0 - ADAM_B1) * g
    v2 = ADAM_B2 * v + (1.0 - ADAM_B2) * (g * g)
    m_hat = m2 / (1.0 - ADAM_B1 ** ADAM_STEP)
    v_hat = v2 / (1.0 - ADAM_B2 ** ADAM_STEP)
    delta = -ADAM_LR * (m_hat / (jnp.sqrt(v_hat) + ADAM_EPS) + ADAM_WD * w)
    return delta, m2, v2


def shard_adam(name, owns, recvs, w, m, v):
    n = len(owns)
    R = owns[0].shape[0]
    ct = min(o.shape[1] for o in owns)
    first = [sum(o.shape[1] for o in owns[:j]) // ct for j in range(n)]
    count = [o.shape[1] // ct for o in owns]
    nt = _row_tiles(R)
    tr = R // nt

    def body(*refs):
        o_refs, r_refs = refs[:n], refs[n:2 * n]
        w_ref, m_ref, v_ref, g_ref, d_ref, nm_ref, nv_ref = refs[2 * n:]
        g = None
        for j in range(n):
            gj = o_refs[j][...] + r_refs[j][0].astype(F32) + r_refs[j][1].astype(F32) + r_refs[j][2].astype(F32)
            g = gj if g is None else jnp.where(pl.program_id(0) >= first[j], gj, g)
        delta, m2, v2 = _adamw(w_ref[...], g, m_ref[...], v_ref[...])
        g_ref[...] = g
        d_ref[...] = delta
        nm_ref[...] = m2
        nv_ref[...] = v2

    def part(j):
        return pl.BlockSpec((tr, ct), lambda k, i: (i, jnp.clip(k - first[j], 0, count[j] - 1)))

    def part3(j):
        return pl.BlockSpec((3, tr, ct), lambda k, i: (0, i, jnp.clip(k - first[j], 0, count[j] - 1)))

    tile = pl.BlockSpec((tr, ct), lambda k, i: (i, k))
    return pl.pallas_call(
        body, name=name, grid=(sum(count), nt),
        in_specs=[part(j) for j in range(n)] + [part3(j) for j in range(n)] + [tile, tile, tile],
        out_specs=[tile] * 4, out_shape=[jax.ShapeDtypeStruct((R, sum(count) * ct), F32)] * 4,
        compiler_params=_cp(dimension_semantics=("arbitrary", "arbitrary")),
    )(*owns, *recvs, w, m, v)


def adam_cols(name, mine, recv, w, m, v, c_idx):
    half, C = mine.shape
    nt = _half_tiles(half)
    tr = half // nt

    def body(s_ref, a_ref, b_ref, w_ref, m_ref, v_ref, g_ref, d_ref, nm_ref, nv_ref):
        g = jnp.where(pl.program_id(0) == s_ref[0], a_ref[...], b_ref[...])
        delta, m2, v2 = _adamw(w_ref[...], g, m_ref[...], v_ref[...])
        g_ref[...] = g
        d_ref[...] = delta
        nm_ref[...] = m2
        nv_ref[...] = v2

    part = pl.BlockSpec((tr, C), lambda hh, i, s: (i, 0))
    tile = pl.BlockSpec((tr, C), lambda hh, i, s: (hh * nt + i, 0))
    grid_spec = pltpu.PrefetchScalarGridSpec(
        num_scalar_prefetch=1, grid=(2, nt), in_specs=[part, part, tile, tile, tile], out_specs=[tile] * 4)
    return pl.pallas_call(
        body, name=name, grid_spec=grid_spec, out_shape=[jax.ShapeDtypeStruct((2 * half, C), F32)] * 4,
        compiler_params=_cp(dimension_semantics=("arbitrary", "arbitrary")),
    )(jnp.stack([c_idx]), mine, recv, w, m, v)


ROW_N1, ROW_N2, ROW_BG, ROW_QN, ROW_KN, ROW_CB, ROW_LW, ROW_LB, ROW_CW = 0, 1, 2, 4, 5, 6, 7, 8, 9
PACK_ROWS = 40
SMALL = ("norm1_w", "norm2_w", "b_gate", "q_norm_w", "k_norm_w", "conv_b", "conv_ln_w", "conv_ln_b", "conv_w")


def small_sync_adam(g, w, m, v, sq, sides=()):
    ns = len(SMALL)

    def body(*refs):
        gi = dict(zip(SMALL, refs[:ns]))
        wi = dict(zip(SMALL, refs[ns:2 * ns]))
        mi = dict(zip(SMALL, refs[2 * ns:3 * ns]))
        vi = dict(zip(SMALL, refs[3 * ns:4 * ns]))
        sq_ref = refs[4 * ns]
        outs = refs[4 * ns + 1:8 * ns + 1]
        loss_ref = refs[8 * ns + 1]
        pack, recv, tot, send_sems, recv_sems = refs[8 * ns + 2:]
        x, y, c, _ = _place()
        me = 4 * x + 2 * y + c

        pack[...] = jnp.zeros_like(pack)
        pack[ROW_KN:ROW_KN + 1, LANES:2 * LANES] = jnp.full((1, LANES), (0.5 / D) * jnp.sum(sq_ref[...]), F32)
        pack[ROW_N1:ROW_N1 + 1, :] = gi["norm1_w"][...]
        pack[ROW_N2:ROW_N2 + 1, :] = gi["norm2_w"][...]
        pack[ROW_BG:ROW_BG + 2, :] = gi["b_gate"][...]
        pack[ROW_QN:ROW_QN + 1, 0:HD] = gi["q_norm_w"][...]
        pack[ROW_KN:ROW_KN + 1, 0:HD] = gi["k_norm_w"][...]
        pack[ROW_CB:ROW_CB + 1, 0:CC] = gi["conv_b"][...]
        pack[ROW_LW:ROW_LW + 1, 0:CC] = gi["conv_ln_w"][...]
        pack[ROW_LB:ROW_LB + 1, 0:CC] = gi["conv_ln_b"][...]
        pack[ROW_CW:ROW_CW + KW, 0:CC] = gi["conv_w"][...]

        copies = []
        for k in range(1, NDEV):
            peer = (x ^ (k >> 2), y ^ ((k >> 1) & 1), c ^ (k & 1))
            cp = pltpu.make_async_remote_copy(
                src_ref=pack, dst_ref=recv.at[me], send_sem=send_sems.at[k - 1], recv_sem=recv_sems.at[k - 1],
                device_id=peer, device_id_type=MESH)
            cp.start()
            copies.append(cp)
        recv[me] = pack[...]
        for cp in copies:
            cp.wait()
        acc = recv[0]
        for p in range(1, NDEV):
            acc = acc + recv[p]
        tot[...] = acc

        def shard_grad(name):
            if name == "b_gate":
                return tot[ROW_BG:ROW_BG + 2, pl.ds(pl.multiple_of(me * LANES, LANES), LANES)]
            if name == "conv_w":
                win = tot[ROW_CW:ROW_CW + KW, pl.ds(pl.multiple_of((me // 2) * LANES, LANES), LANES)]
                return jnp.where(me % 2 == 1, win[:, HD:LANES], win[:, 0:HD])
            row = {"norm1_w": ROW_N1, "norm2_w": ROW_N2, "q_norm_w": ROW_QN, "k_norm_w": ROW_KN,
                   "conv_b": ROW_CB, "conv_ln_w": ROW_LW, "conv_ln_b": ROW_LB}[name]
            return tot[row:row + 1, 0:wi[name].shape[1]]

        for i, name in enumerate(SMALL):
            gr = shard_grad(name)
            delta, m2, v2 = _adamw(wi[name][...], gr, mi[name][...], vi[name][...])
            outs[4 * i][...] = gr
            outs[4 * i + 1][...] = delta
            outs[4 * i + 2][...] = m2
            outs[4 * i + 3][...] = v2
        loss_ref[...] = tot[ROW_KN:ROW_KN + 1, LANES:2 * LANES]

    out_shape = []
    for name in SMALL:
        out_shape += [jax.ShapeDtypeStruct(w[name].shape, F32)] * 4
    out_shape.append(jax.ShapeDtypeStruct((1, LANES), F32))
    args = [g[k] for k in SMALL] + [w[k] for k in SMALL] + [m[k] for k in SMALL] + [v[k] for k in SMALL] + [sq]
    res = _call(
        body, sides, name="small_sync_adam", grid=(1,), in_specs=[VMEM] * len(args),
        out_specs=[VMEM] * len(out_shape), out_shape=out_shape,
        scratch_shapes=[pltpu.VMEM((PACK_ROWS, D), F32), pltpu.VMEM((NDEV, PACK_ROWS, D), F32),
                        pltpu.VMEM((PACK_ROWS, D), F32), _sems(NDEV - 1), _sems(NDEV - 1)],
        args=args, own_comm=True)
    res, side_outs = res if sides else (res, None)
    out = {name: tuple(res[4 * i:4 * i + 4]) for i, name in enumerate(SMALL)}
    loss = res[4 * ns][0, 0]
    return (out, loss, side_outs) if sides else (out, loss)


MATS = ("w_in", "w_o_attn", "w_pw_conv", "w_out", "w_ffn_in", "w_ffn_out")
TRANSPOSED = ("w_in", "w_ffn_in")
WEIGHTS = ("norm1_w", "w_in", "b_gate", "q_norm_w", "k_norm_w", "w_o_attn", "conv_w", "conv_b", "conv_ln_w",
           "conv_ln_b", "w_pw_conv", "w_out", "norm2_w", "w_ffn_in", "w_ffn_out")


def _blocks_to_cols(blocks):
    n, R, C = blocks.shape
    return blocks.transpose(1, 0, 2).reshape(R, n * C)


def kernel(x, positions, norm1_w, w_in, b_gate, q_norm_w, k_norm_w, w_o_attn, conv_w, conv_b, conv_ln_w, conv_ln_b, w_pw_conv, w_out, norm2_w, w_ffn_in, w_ffn_out, loss_target, m_norm1_w, m_w_in, m_b_gate, m_q_norm_w, m_k_norm_w, m_w_o_attn, m_conv_w, m_conv_b, m_conv_ln_w, m_conv_ln_b, m_w_pw_conv, m_w_out, m_norm2_w, m_w_ffn_in, m_w_ffn_out, v_norm1_w, v_w_in, v_b_gate, v_q_norm_w, v_k_norm_w, v_w_o_attn, v_conv_w, v_conv_b, v_conv_ln_w, v_conv_ln_b, v_w_pw_conv, v_w_out, v_norm2_w, v_w_ffn_in, v_w_ffn_out):
    w = dict(norm1_w=norm1_w, w_in=w_in, b_gate=b_gate, q_norm_w=q_norm_w, k_norm_w=k_norm_w, w_o_attn=w_o_attn,
             conv_w=conv_w, conv_b=conv_b, conv_ln_w=conv_ln_w, conv_ln_b=conv_ln_b, w_pw_conv=w_pw_conv,
             w_out=w_out, norm2_w=norm2_w, w_ffn_in=w_ffn_in, w_ffn_out=w_ffn_out)
    m = dict(norm1_w=m_norm1_w, w_in=m_w_in, b_gate=m_b_gate, q_norm_w=m_q_norm_w, k_norm_w=m_k_norm_w,
             w_o_attn=m_w_o_attn, conv_w=m_conv_w, conv_b=m_conv_b, conv_ln_w=m_conv_ln_w,
             conv_ln_b=m_conv_ln_b, w_pw_conv=m_w_pw_conv, w_out=m_w_out, norm2_w=m_norm2_w,
             w_ffn_in=m_w_ffn_in, w_ffn_out=m_w_ffn_out)
    v = dict(norm1_w=v_norm1_w, w_in=v_w_in, b_gate=v_b_gate, q_norm_w=v_q_norm_w, k_norm_w=v_k_norm_w,
             w_o_attn=v_w_o_attn, conv_w=v_conv_w, conv_b=v_conv_b, conv_ln_w=v_conv_ln_w,
             conv_ln_b=v_conv_ln_b, w_pw_conv=v_w_pw_conv, w_out=v_w_out, norm2_w=v_norm2_w,
             w_ffn_in=v_w_ffn_in, w_ffn_out=v_w_ffn_out)
    def two_d(t):
        t = {k: (a[0] if a.ndim == 3 else a) for k, a in t.items()}
        return {k: (a.T if k in TRANSPOSED else a) for k, a in t.items()}

    w, m, v = two_d(w), two_d(m), two_d(v)

    x2, target = x[0], loss_target[0]
    c_idx = lax.axis_index("c").astype(jnp.int32)
    chip_idx = (2 * lax.axis_index("x") + lax.axis_index("y")).astype(jnp.int32)
    qw2 = jnp.tile(w["q_norm_w"], (1, 2))
    kw2 = jnp.tile(w["k_norm_w"], (1, 2))

    tabs, ((bg_blocks,), (cw_blocks,)) = rope_tables(
        positions.reshape(S, 1), sides=(ag_blocks(w["b_gate"], F32), ag_blocks(w["conv_w"], F32)))
    b_gate_f, conv_w_f = _blocks_to_cols(bg_blocks), _blocks_to_cols(cw_blocks)
    ax, ay = lax.axis_index("x"), lax.axis_index("y")
    chip_order = jnp.stack([2 * ax + ay, 2 * (1 - ax) + ay, 2 * ax + 1 - ay, 2 * (1 - ax) + 1 - ay]).astype(jnp.int32)
    h_t, proj, w_in_blocks = in_proj_gather(x2, w["norm1_w"], w["w_in"], chip_order)
    w_in_t = w_in_blocks.reshape(INW, D)
    (attn, lse), ((w_ffn_in_blocks,), (w_out_blocks,)) = attn_fwd(
        proj, tabs, qw2, kw2, sides=(ag_blocks_relay(w["w_ffn_in"], BF16), ag_blocks_relay(w["w_out"], BF16)))
    w_ffn_in_t = w_ffn_in_blocks.reshape(2 * FF, D)
    w_out_f = w_out_blocks.reshape(D, D)
    (cpre, u3), ((w_o_f,), (w_pw_f,)) = conv_fwd(
        proj, conv_w_f, w["conv_b"], w["conv_ln_w"], w["conv_ln_b"],
        sides=(ag_cols(w["w_o_attn"]), ag_cols(w["w_pw_conv"])))
    x1, z, ya, yb = mix_out(x2, proj, b_gate_f, attn, u3, w_o_f, w_pw_f, w_out_f)
    (h2, gu, f), ((w_ffn_out_blocks,),) = ffn_in(x1, w["norm2_w"], w_ffn_in_t, sides=(ag_blocks_relay(w["w_ffn_out"], BF16),))
    w_ffn_out_f = w_ffn_out_blocks.reshape(FF, D)
    dy, dyb, sq = ffn_out_loss(x1, f, w_ffn_out_f, target)

    g = {}
    g_ffn_out = mm_tn("gw_ffn_out", f, dyb, FF // 2, D).reshape(NDEV, FF // NDEV, D)
    (d_gu, d_x1, d_x1b, g["norm2_w"]), ((ra_ffn_out,),) = ffn_bwd(
        dy, dyb, gu, x1, w["norm2_w"], w_ffn_in_t, w_ffn_out_f, sides=(rs_to_sibling([("rows", g_ffn_out)]),))
    pb_ffn_out, own_ffn_out = chip_sum("chip_sum_w_ffn_out", g_ffn_out, ra_ffn_out, c_idx, chip_idx)
    g_ffn_in, ((rb_ffn_out,),) = mm_tn("gw_ffn_in", d_gu, h2, FF // 2, D,
                                       sides=(rs_to_chips([("rows", pb_ffn_out)]),))
    g_ffn_in = g_ffn_in.reshape(NDEV, 2 * FF // NDEV, D)
    g_out = mm_tn("gw_out", z, d_x1b, D // 2, D).reshape(NDEV, D // NDEV, D)
    (d_ya, d_yb, d_gl, d_attn, d_u3, g["b_gate"]), ((ra_ffn_in,),) = out_bwd(
        d_x1b, proj, b_gate_f, ya, yb, w_o_f, w_pw_f, w_out_f, sides=(rs_to_sibling([("rows", g_ffn_in)]),))
    pb_ffn_in, own_ffn_in = chip_sum("chip_sum_w_ffn_in", g_ffn_in, ra_ffn_in, c_idx, chip_idx)
    g_w_o = mm_tn("gw_o_attn", attn, d_ya, CC, D)
    g_w_pw = mm_tn("gw_pw_conv", u3, d_yb, CC, D)
    (d_conv, g["conv_w"], g["conv_b"], g["conv_ln_w"], g["conv_ln_b"]), ((ra_out, ra_w_o, ra_w_pw),) = conv_bwd(
        proj, cpre, d_u3, conv_w_f, conv_w_f[::-1], w["conv_ln_w"], w["conv_ln_b"],
        sides=(rs_to_sibling([("rows", g_out), ("cols", g_w_o), ("cols", g_w_pw)]),))
    pb_out, own_out = chip_sum("chip_sum_w_out", g_out, ra_out, c_idx, chip_idx)
    pb_w_o, own_w_o = chip_sum_cols("chip_sum_w_o_attn", g_w_o, ra_w_o, c_idx, chip_idx)
    pb_w_pw, own_w_pw = chip_sum_cols("chip_sum_w_pw_conv", g_w_pw, ra_w_pw, c_idx, chip_idx)
    (d_q, d_k, d_v, gqw, gkw), ((rb_ffn_in, rb_out, rb_w_o, rb_w_pw),) = attn_bwd(
        proj, tabs, qw2, kw2, d_attn, attn, lse,
        sides=(rs_to_chips([("rows", pb_ffn_in), ("rows", pb_out), ("cols", pb_w_o), ("cols", pb_w_pw)]),))
    g["q_norm_w"] = gqw[0:1, 0:HD] + gqw[0:1, HD:LANES]
    g["k_norm_w"] = gkw[0:1, 0:HD] + gkw[0:1, HD:LANES]
    mine_w_o, theirs_w_o = col_final("col_final_w_o_attn", own_w_o, rb_w_o, c_idx)
    mine_w_pw, theirs_w_pw = col_final("col_final_w_pw_conv", own_w_pw, rb_w_pw, c_idx)
    d_segs = (d_q, d_k, d_v, d_conv, d_gl)
    parts, to_sibling, to_chips, owns, from_chips = [], None, None, [], []
    for k, hw in enumerate(GW_IN_SPLIT):
        sides = [rs_swap_halves([theirs_w_o, theirs_w_pw])] if k == 0 else []
        sides += [s for s in (to_chips, to_sibling) if s is not None]
        part, outs = gw_in_t("gw_in_%d" % k, h_t, d_segs, sum(GW_IN_SPLIT[:k]), hw, sides=tuple(sides))
        if k == 0:
            (rc_w_o, rc_w_pw), outs = outs[0], outs[1:]
        outs = list(outs)
        if to_chips is not None:
            from_chips.append(outs.pop(0)[0])
        if to_sibling is not None:
            pb, own = chip_sum("chip_sum_w_in_%d" % (k - 1), parts[-1], outs.pop(0)[0], c_idx, chip_idx)
            owns.append(own)
            to_chips = rs_to_chips([("rows", pb)])
        else:
            to_chips = None
        parts.append(part.reshape(NDEV, INW // NDEV, hw))
        to_sibling = rs_to_sibling([("rows", parts[-1])])
    (grad_x, g["norm1_w"]), ((rb_prev,), (ra_last,)) = in_bwd(
        d_q, d_k, d_v, d_conv, d_gl, w_in_t, x2, d_x1, w["norm1_w"], sides=(to_chips, to_sibling))
    from_chips.append(rb_prev)
    pb, own = chip_sum("chip_sum_w_in_%d" % (len(GW_IN_SPLIT) - 1), parts[-1], ra_last, c_idx, chip_idx)
    owns.append(own)
    small, loss, ((rb_last,),) = small_sync_adam(g, w, m, v, sq, sides=(rs_to_chips([("rows", pb)]),))
    from_chips.append(rb_last)

    res = {
        "w_in": shard_adam("adam_w_in", owns, from_chips, w["w_in"], m["w_in"], v["w_in"]),
        "w_ffn_in": shard_adam("adam_w_ffn_in", [own_ffn_in], [rb_ffn_in], w["w_ffn_in"], m["w_ffn_in"], v["w_ffn_in"]),
        "w_o_attn": adam_cols("adam_w_o_attn", mine_w_o, rc_w_o, w["w_o_attn"], m["w_o_attn"], v["w_o_attn"], c_idx),
        "w_pw_conv": adam_cols("adam_w_pw_conv", mine_w_pw, rc_w_pw, w["w_pw_conv"], m["w_pw_conv"], v["w_pw_conv"], c_idx),
        "w_out": shard_adam("adam_w_out", [own_out], [rb_out], w["w_out"], m["w_out"], v["w_out"]),
        "w_ffn_out": shard_adam("adam_w_ffn_out", [own_ffn_out], [rb_ffn_out],
                                w["w_ffn_out"], m["w_ffn_out"], v["w_ffn_out"]),
    }
    res = {k: tuple(a.T if k in TRANSPOSED else a for a in r) for k, r in res.items()}
    res.update(small)

    def shaped(name, a):
        return a.reshape((1,) + a.shape) if name in MATS or name in ("b_gate", "conv_w") else a

    outs = [loss, grad_x.reshape(1, S, D)]
    for i in range(4):
        outs += [shaped(k, res[k][i]) for k in WEIGHTS]
    return tuple(outs)
```

```python
import functools
from typing import Callable, NamedTuple, Optional

import numpy as np
import jax
import jax.numpy as jnp
from jax import lax
from jax.experimental import pallas as pl
from jax.experimental.pallas import tpu as pltpu

F32 = jnp.float32
BF16 = jnp.bfloat16

S = 2048
D = 1024
HD = 64
QKV = 1536
CC = 512
KW = 31
FF = 2816
INW = 7680
OFF_Q, OFF_K, OFF_V, OFF_CA, OFF_CB, OFF_GA, OFF_GB = 0, 1536, 3072, 4608, 5120, 5632, 6656
DILATIONS = (1, 4, 16)
HALF_SPAN = 64
EPS = 1e-6
NEG_INF = -1e30
ROPE_THETA = 500000.0
ROT_DIM = 16

ADAM_LR = 0.001
ADAM_B1 = 0.9
ADAM_B2 = 0.999
ADAM_EPS = 1e-08
ADAM_WD = 0.01
ADAM_STEP = 10

NDEV = 8
LANES = 128
TM = 256
TQ = 128
VMEM_LIMIT = 56 * 1024 * 1024
MESH = pl.DeviceIdType.MESH


def _cp(**kw):
    return pltpu.CompilerParams(vmem_limit_bytes=VMEM_LIMIT, **kw)


def _row(width, col=0, tm=TM):
    return pl.BlockSpec((tm, width), lambda i: (i, col))


PLANE = 512


def _planes(width, tm=TM):
    return pl.BlockSpec((width // PLANE, tm, PLANE), lambda i: (0, i, 0))


def _res(shape):
    nd = len(shape)
    return pl.BlockSpec(shape, lambda *_: (0,) * nd, pipeline_mode=pl.Buffered(1))


def _dot(a, b):
    return jnp.dot(a, b, preferred_element_type=F32)


def _dot_nt(a, b):
    return lax.dot_general(a, b, (((1,), (1,)), ((), ())), preferred_element_type=F32)


def _dot_tn(a, b):
    return lax.dot_general(a, b, (((0,), (0,)), ((), ())), preferred_element_type=F32)


def _sigmoid(x):
    return jax.nn.sigmoid(x)


def _dsilu(x, sg):
    return sg * (1.0 + x * (1.0 - sg))


ANY = pl.BlockSpec(memory_space=pl.ANY)
VMEM = pl.BlockSpec(memory_space=pltpu.VMEM)


class Side(NamedTuple):
    args: tuple
    in_specs: tuple
    out_shape: tuple
    scratch: tuple
    start: Callable
    finish: Callable
    mid: Optional[Callable] = None
    peers: str = ""


BARRIER_IDS = {"s": 0, "dxy": 1, "dsxy": 2, "sxy": 3}


def _peer_barrier(peers):
    x, y, c = lax.axis_index("x"), lax.axis_index("y"), lax.axis_index("c")
    where = {"s": (x, y, 1 - c), "x": (1 - x, y, c), "y": (x, 1 - y, c), "d": (1 - x, 1 - y, c)}
    barrier = pltpu.get_barrier_semaphore()
    for p in peers:
        pl.semaphore_signal(barrier, inc=1, device_id=where[p], device_id_type=MESH)
    pl.semaphore_wait(barrier, len(peers))


def _call(body, sides=(), *, name, grid, in_specs, out_specs, out_shape, scratch_shapes=(), args, own_comm=False):
    ni, no, ns = len(in_specs), len(out_specs), len(scratch_shapes)
    cnt = [(len(s.args), len(s.out_shape), len(s.scratch)) for s in sides]
    peers = "".join(sorted(set("".join(s.peers for s in sides))))
    if own_comm or not sides or any(not s.peers for s in sides):
        peers = ""

    def take(refs, pos, n):
        return refs[pos:pos + n], pos + n

    def full(*refs):
        m_in, pos = take(refs, 0, ni)
        s_in = []
        for a, _, _ in cnt:
            r, pos = take(refs, pos, a)
            s_in.append(r)
        m_out, pos = take(refs, pos, no)
        s_out = []
        for _, o, _ in cnt:
            r, pos = take(refs, pos, o)
            s_out.append(r)
        m_scr, pos = take(refs, pos, ns)
        s_scr = []
        for _, _, c in cnt:
            r, pos = take(refs, pos, c)
            s_scr.append(r)
        if sides:
            first = functools.reduce(jnp.logical_and, [pl.program_id(d) == 0 for d in range(len(grid))])
            last = functools.reduce(jnp.logical_and, [pl.program_id(d) == g - 1 for d, g in enumerate(grid)])

            @pl.when(first)
            def _():
                if peers:
                    _peer_barrier(peers)
                for s, a, o, c in zip(sides, s_in, s_out, s_scr):
                    s.start(a, o, c)

            steps = int(np.prod(grid))
            mid_step = (2 * steps) // 3
            if steps > 1 and any(s.mid is not None for s in sides):
                step = functools.reduce(lambda acc, d: acc * grid[d] + pl.program_id(d), range(len(grid)), 0)

                @pl.when(step == mid_step)
                def _():
                    for s, a, o, c in zip(sides, s_in, s_out, s_scr):
                        if s.mid is not None:
                            s.mid(a, o, c)

        body(*m_in, *m_out, *m_scr)
        if sides:
            @pl.when(last)
            def _():
                for s, a, o, c in zip(sides, s_in, s_out, s_scr):
                    if s.mid is not None and steps == 1:
                        s.mid(a, o, c)
                    s.finish(a, o, c)

    res = pl.pallas_call(
        full, name=name, grid=grid,
        in_specs=list(in_specs) + [sp for s in sides for sp in s.in_specs],
        out_specs=list(out_specs) + [ANY for s in sides for _ in s.out_shape],
        out_shape=list(out_shape) + [o for s in sides for o in s.out_shape],
        scratch_shapes=list(scratch_shapes) + [c for s in sides for c in s.scratch],
        compiler_params=_cp(dimension_semantics=("arbitrary",) * len(grid),
                            **({"collective_id": BARRIER_IDS[peers]} if peers else {})),
    )(*args, *[a for s in sides for a in s.args])
    res = list(res)
    if not sides:
        return res
    outs, pos = take(res, 0, no)
    side_outs = []
    for _, o, _ in cnt:
        r, pos = take(res, pos, o)
        side_outs.append(r)
    return outs, side_outs


def _inv_freq_lanes():
    inv = np.float32(ROPE_THETA) ** (-np.arange(0, ROT_DIM, 2, dtype=np.float32) / np.float32(ROT_DIM))
    lane = np.arange(LANES) % HD
    out = np.where(lane < ROT_DIM, inv[lane % (ROT_DIM // 2)], 0.0).astype(np.float32)
    return jnp.asarray(out.reshape(1, LANES))


def rope_tables(pos_col, sides=()):
    def body(p_ref, f_ref, c_ref, s1_ref, s2_ref):
        ang = p_ref[...].astype(F32) * f_ref[...]
        lane = lax.broadcasted_iota(jnp.int32, ang.shape, 1) % HD
        cs = jnp.cos(ang)
        sn = jnp.sin(ang)
        c_ref[...] = jnp.where(lane < ROT_DIM, cs, 1.0)
        s1_ref[...] = jnp.where(lane < ROT_DIM // 2, -sn, 0.0)
        s2_ref[...] = jnp.where(lane < ROT_DIM // 2, 0.0, jnp.where(lane < ROT_DIM, sn, 0.0))

    sds = jax.ShapeDtypeStruct((S, LANES), F32)
    return _call(
        body, sides, name="rope_tables", grid=(S // TM,),
        in_specs=[_row(1), pl.BlockSpec((1, LANES), lambda i: (0, 0))],
        out_specs=[_row(LANES)] * 3, out_shape=[sds] * 3,
        args=(pos_col, _inv_freq_lanes()))


def _rope(v, c, s1, s2):
    return v * c + pltpu.roll(v, LANES - 8, axis=1) * s1 + pltpu.roll(v, 8, axis=1) * s2


def _rope_t(d, c, s1, s2):
    return d * c - pltpu.roll(d, LANES - 8, axis=1) * s1 - pltpu.roll(d, 8, axis=1) * s2


def _head_mat():
    r = lax.broadcasted_iota(jnp.int32, (LANES, LANES), 0) // HD
    c = lax.broadcasted_iota(jnp.int32, (LANES, LANES), 1) // HD
    return jnp.where(r == c, 1.0 / HD, 0.0).astype(BF16)


def _head_mean(t, e):
    hi = t.astype(BF16)
    rest = (t - hi.astype(F32)).astype(BF16)
    return _dot(hi, e) + _dot(rest, e)


def in_proj_gather(x, norm_w, shard_t, chip_order):
    R = INW // NDEV
    half, nt = R // 2, S // TM

    def body(ord_ref, x_ref, nw_ref, sh_ref, ht_ref, p_ref, wfull_ref, wt, hs, send, recv, loc):
        kk, i = pl.program_id(0), pl.program_id(1)
        x, y, c, _ = _place()
        me, flip = 4 * x + 2 * y + c, 1 - 2 * c
        here, sib, xn, yn = (x, y, c), (x, y, 1 - c), (1 - x, y, c), (x, 1 - y, c)
        b_xn, b_yn, b_dg = 4 * (1 - x) + 2 * y + c, 4 * x + 2 * (1 - y) + c, 4 * (1 - x) + 2 * (1 - y) + c

        def cp(k, block, to, rows=None):
            dst = wt.at[block] if rows is None else wt.at[block, pl.ds(rows * half, half), :]
            return _remote(dst, dst, send, recv, k, to)

        def sends():
            return [cp(0, me, sib), cp(1, me, xn), cp(2, me, yn), cp(3, b_xn, sib), cp(4, b_yn, sib),
                    cp(5, b_xn, yn, rows=0), cp(6, b_yn, xn, rows=1), cp(7, b_dg, sib, rows=0), cp(8, b_dg, sib, rows=1)]

        def keep(j, blk0):
            pair = pl.ds(pl.multiple_of(blk0, 2), 2)
            return pltpu.make_async_copy(wt.at[pair], wfull_ref.at[pair], loc.at[j])

        @pl.when((kk == 0) & (i == 0))
        def _():
            _peer_barrier("sxy")
            _cast_rows(wt.at[me], sh_ref)
            for s_ in sends()[0:3]:
                s_.start()
            cp(0, me + flip, here).wait_recv()
            keep(0, me - c).start()

        @pl.when((kk == 1) & (i == 0))
        def _():
            cp(1, b_xn, here).wait_recv()
            sends()[5].start()
            sends()[3].start()
            cp(2, b_yn, here).wait_recv()
            sends()[6].start()
            sends()[4].start()
            cp(3, b_xn + flip, here).wait_recv()
            keep(1, b_xn - c).start()

        @pl.when((kk == 2) & (i == 0))
        def _():
            cp(4, b_yn + flip, here).wait_recv()
            keep(2, b_yn - c).start()

        @pl.when((kk == 3) & (i == 0))
        def _():
            cp(5, b_dg, here, rows=0).wait_recv()
            sends()[7].start()
            cp(6, b_dg, here, rows=1).wait_recv()
            sends()[8].start()
            cp(7, b_dg + flip, here, rows=0).wait_recv()
            cp(8, b_dg + flip, here, rows=1).wait_recv()
            keep(3, b_dg - c).start()

        rows = pl.ds(pl.multiple_of(i * TM, TM), TM)

        @pl.when(kk == 0)
        def _():
            xv = x_ref[...]
            r = lax.rsqrt(jnp.mean(xv * xv, axis=-1, keepdims=True) + EPS)
            hf = xv * r * nw_ref[...]
            ht_ref[...] = hf.T.astype(BF16)
            hs[rows, :] = hf.astype(BF16)

        h = hs[rows, :]
        chip = ord_ref[kk]
        for cc in range(2):
            p_ref[:, cc * R:(cc + 1) * R] = _dot_nt(h, wt[2 * chip + cc])

        @pl.when((kk == 3) & (i == nt - 1))
        def _():
            for s_ in sends():
                s_.wait_send()
            for j, blk in enumerate((me, b_xn, b_yn, b_dg)):
                keep(j, blk - c).wait()

    def first_pass(kk, i):
        return jnp.where(kk == 0, i, nt - 1)

    grid_spec = pltpu.PrefetchScalarGridSpec(
        num_scalar_prefetch=1, grid=(4, nt),
        in_specs=[pl.BlockSpec((TM, D), lambda kk, i, o: (first_pass(kk, i), 0)),
                  pl.BlockSpec((1, D), lambda kk, i, o: (0, 0)), VMEM],
        out_specs=[pl.BlockSpec((D, TM), lambda kk, i, o: (0, first_pass(kk, i))),
                   pl.BlockSpec((TM, 2 * R), lambda kk, i, o: (i, o[kk])), ANY],
        scratch_shapes=[pltpu.VMEM((NDEV, R, D), BF16), pltpu.VMEM((S, D), BF16), _sems(9), _sems(9), _sems(4)])
    return pl.pallas_call(
        body, name="in_proj_gather", grid_spec=grid_spec,
        out_shape=[jax.ShapeDtypeStruct((D, S), BF16), jax.ShapeDtypeStruct((S, INW), F32),
                   jax.ShapeDtypeStruct((NDEV, R, D), BF16)],
        compiler_params=_cp(dimension_semantics=("arbitrary", "arbitrary"), collective_id=BARRIER_IDS["sxy"]),
    )(chip_order, x, norm_w, shard_t)


def _qk_specs():
    nb = QKV // LANES
    return [pl.BlockSpec((S, LANES), functools.partial(lambda hp, g, o: (0, o + g * 4 + hp), o=o))
            for o in (OFF_Q // LANES, OFF_K // LANES, OFF_V // LANES)]


def _tab_specs():
    return [pl.BlockSpec((S, LANES), lambda hp, g: (0, 0), pipeline_mode=pl.Buffered(1))] * 3


def _vec_spec():
    return pl.BlockSpec((1, LANES), lambda hp, g: (0, 0))


def _sub_rows(r, d, start, n):
    if d == 1:
        return pl.ds(start, n)
    return pl.ds(r + d * start, n, stride=d)


def _band_window(i, L):
    W = min(TQ + 2 * HALF_SPAN, L)
    q0 = pl.multiple_of(i * TQ, TQ)
    k0 = pl.multiple_of(jnp.clip(q0 - HALF_SPAN, 0, L - W), HALF_SPAN)
    qpos = q0 + (lax.broadcasted_iota(jnp.int32, (2 * TQ, W), 0) & (TQ - 1))
    kpos = k0 + lax.broadcasted_iota(jnp.int32, (2 * TQ, W), 1)
    valid = jnp.abs(qpos - kpos) <= HALF_SPAN
    return W, q0, k0, valid


def _stack_heads(t, lo):
    z = jnp.zeros_like(t)
    return jnp.concatenate([jnp.where(lo, t, z), jnp.where(lo, z, t)], axis=0)


def _unstack_heads(t2, lo):
    return jnp.where(lo, t2[0:TQ], t2[TQ:2 * TQ])


CHAINS = 8


def _interleave(d):
    ru = min(d, CHAINS)
    return ru, min(CHAINS // ru, S // d // TQ)


def _for_blocks(n, fn):
    if n == 1:
        fn(0)
    else:
        def it(j, _):
            fn(j)
            return 0
        lax.fori_loop(0, n, it, 0)


def attn_fwd(proj, tabs, qw2, kw2, sides=()):
    CH = 256

    def body(q_ref, k_ref, v_ref, c_ref, s1_ref, s2_ref, qw_ref, kw_ref, at_ref, ls_ref,
             qs, ks, vs, osub, lsub, onat, lnat, qn, kn):
        g = pl.program_id(1)
        lo = lax.broadcasted_iota(jnp.int32, (1, LANES), 1) < HD
        e = _head_mat()

        def prep(i, _):
            rows = pl.ds(pl.multiple_of(i * CH, CH), CH)
            c, s1, s2 = c_ref[rows, :], s1_ref[rows, :], s2_ref[rows, :]
            for t_ref, w_ref, out, scale in ((q_ref, qw_ref, qn, HD ** -0.5), (k_ref, kw_ref, kn, 1.0)):
                t = t_ref[rows, :]
                r = lax.rsqrt(_head_mean(t * t, e) + EPS)
                out[rows, :] = _rope(t * r * w_ref[...], c, s1, s2) * scale
            return 0

        lax.fori_loop(0, S // CH, prep, 0, unroll=4)

        def group(gi, d):
            L = S // d

            ru, nb = _interleave(d)

            def stage(r, off):
                for c0 in range(0, L, CH):
                    n = min(CH, L)
                    rows = _sub_rows(r, d, c0, n)
                    dst = pl.ds(off + c0, n)
                    qs[dst, :] = qn[rows, :].astype(BF16)
                    ks[dst, :] = kn[rows, :].astype(BF16)
                    vs[dst, :] = v_ref[rows, :].astype(BF16)

            def one(off, i):
                W, q0, k0, valid = _band_window(i, L)
                q2 = _stack_heads(qs[pl.ds(off + q0, TQ), :], lo)
                sc = jnp.where(valid, _dot_nt(q2, ks[pl.ds(off + k0, W), :]), NEG_INF)
                m = jnp.max(sc, axis=-1, keepdims=True)
                p = jnp.exp(sc - m)
                den = jnp.sum(p, axis=-1, keepdims=True)
                o2 = _dot(p.astype(BF16), vs[pl.ds(off + k0, W), :]) / den
                l2 = jnp.broadcast_to(m + jnp.log(den), (2 * TQ, LANES))
                osub[pl.ds(off + q0, TQ), :] = _unstack_heads(o2, lo)
                lsub[pl.ds(off + q0, TQ), :] = _unstack_heads(l2, lo)

            def unstage(r, off):
                for c0 in range(0, L, CH):
                    n = min(CH, L)
                    rows = _sub_rows(r, d, c0, n)
                    onat[gi, rows, :] = osub[pl.ds(off + c0, n), :]
                    lnat[gi, rows, :] = lsub[pl.ds(off + c0, n), :]

            def step(t, _):
                for u in range(ru):
                    stage(t * ru + u, u * L)
                _for_blocks(L // TQ // nb, lambda j: [one(u * L, j * nb + b) for u in range(ru) for b in range(nb)])
                for u in range(ru):
                    unstage(t * ru + u, u * L)
                return 0

            lax.fori_loop(0, d // ru, step, 0)

        for gi, d in enumerate(DILATIONS):
            pl.when(g == gi)(functools.partial(group, gi, d))

        @pl.when(g == len(DILATIONS) - 1)
        def _():
            def mix(i, _):
                rows = pl.ds(pl.multiple_of(i * CH, CH), CH)
                l0, l1, l2 = lnat[0, rows, :], lnat[1, rows, :], lnat[2, rows, :]
                m = jnp.maximum(jnp.maximum(l0, l1), l2)
                e0, e1, e2 = jnp.exp(l0 - m), jnp.exp(l1 - m), jnp.exp(l2 - m)
                den = e0 + e1 + e2
                a = (e0 * onat[0, rows, :] + e1 * onat[1, rows, :] + e2 * onat[2, rows, :]) / den
                at_ref[rows, :] = a.astype(BF16)
                ls_ref[rows, :] = m + jnp.log(den)
                return 0

            lax.fori_loop(0, S // CH, mix, 0)

    out_spec = pl.BlockSpec((S, LANES), lambda hp, g: (0, hp))
    return _call(
        body, sides, name="attn_fwd", grid=(4, 3),
        in_specs=_qk_specs() + _tab_specs() + [_vec_spec(), _vec_spec()],
        out_specs=[out_spec, out_spec],
        out_shape=[jax.ShapeDtypeStruct((S, CC), BF16), jax.ShapeDtypeStruct((S, CC), F32)],
        scratch_shapes=[pltpu.VMEM((S, LANES), BF16)] * 3 + [pltpu.VMEM((S, LANES), F32)] * 2
        + [pltpu.VMEM((3, S, LANES), F32)] * 2 + [pltpu.VMEM((S, LANES), F32)] * 2,
        args=(proj, proj, proj, *tabs, qw2, kw2))


def attn_bwd(proj, tabs, qw2, kw2, d_attn, attn, lse, sides=()):
    CH = 256

    def body(q_ref, k_ref, v_ref, c_ref, s1_ref, s2_ref, qw_ref, kw_ref, do_ref, at_ref, ls_ref,
             dq_ref, dk_ref, dv_ref, gqw_ref, gkw_ref,
             qs, ks, vs, dos, dsub, lsub, dqs, dks, dvs, dnat, qx, kx, dvn, tnq, tnk, rrq, rrk):
        hp, g = pl.program_id(0), pl.program_id(1)
        lo = lax.broadcasted_iota(jnp.int32, (1, LANES), 1) < HD
        e = _head_mat()
        both = ((q_ref, qw_ref, qx, tnq, rrq, HD ** -0.5), (k_ref, kw_ref, kx, tnk, rrk, 1.0))

        @pl.when((hp == 0) & (g == 0))
        def _():
            gqw_ref[...] = jnp.zeros_like(gqw_ref)
            gkw_ref[...] = jnp.zeros_like(gkw_ref)

        def prep(i, _):
            rows = pl.ds(pl.multiple_of(i * CH, CH), CH)
            dnat[rows, :] = _head_mean(do_ref[rows, :] * at_ref[rows, :].astype(F32), e) * float(HD)
            c, s1, s2 = c_ref[rows, :], s1_ref[rows, :], s2_ref[rows, :]
            for t_ref, w_ref, x, tn_s, rr_s, scale in both:
                t = t_ref[rows, :]
                rr = lax.rsqrt(_head_mean(t * t, e) + EPS)
                tn = t * rr
                rr_s[rows, :] = rr
                tn_s[rows, :] = tn
                x[rows, :] = _rope(tn * w_ref[...], c, s1, s2) * scale
            return 0

        lax.fori_loop(0, S // CH, prep, 0, unroll=4)

        def group(d):
            L = S // d

            ru, nb = _interleave(d)

            def stage(r, off):
                for c0 in range(0, L, CH):
                    n = min(CH, L)
                    rows = _sub_rows(r, d, c0, n)
                    dst = pl.ds(off + c0, n)
                    qs[dst, :] = qx[rows, :].astype(BF16)
                    ks[dst, :] = kx[rows, :].astype(BF16)
                    vs[dst, :] = v_ref[rows, :].astype(BF16)
                    dos[dst, :] = do_ref[rows, :].astype(BF16)
                    dsub[dst, :] = dnat[rows, :]
                    lsub[dst, :] = ls_ref[rows, :]
                    dks[dst, :] = jnp.zeros((n, LANES), F32)
                    dvs[dst, :] = jnp.zeros((n, LANES), F32)

            def one(off, i):
                W, q0, k0, valid = _band_window(i, L)
                qrows, krows = pl.ds(off + q0, TQ), pl.ds(off + k0, W)
                q2 = _stack_heads(qs[qrows, :], lo)
                do2 = _stack_heads(dos[qrows, :], lo)
                kk, vv = ks[krows, :], vs[krows, :]
                lse_b, dd_b = lsub[qrows, :], dsub[qrows, :]
                lse2 = jnp.concatenate([lse_b[:, 0:1], lse_b[:, HD:HD + 1]], axis=0)
                dd2 = jnp.concatenate([dd_b[:, 0:1], dd_b[:, HD:HD + 1]], axis=0)
                sc = jnp.where(valid, _dot_nt(q2, kk), NEG_INF)
                p = jnp.exp(sc - lse2)
                ds = (p * (_dot_nt(do2, vv) - dd2)).astype(BF16)
                dqs[qrows, :] = _unstack_heads(_dot(ds, kk), lo)
                dks[krows, :] = dks[krows, :] + _dot_tn(ds, q2)
                dvs[krows, :] = dvs[krows, :] + _dot_tn(p.astype(BF16), do2)

            def unstage(r, off):
                for c0 in range(0, L, CH):
                    n = min(CH, L)
                    rows = _sub_rows(r, d, c0, n)
                    src = pl.ds(off + c0, n)
                    qx[rows, :] = dqs[src, :]
                    kx[rows, :] = dks[src, :]
                    dvn[rows, :] = dvs[src, :]

            def step(t, _):
                for u in range(ru):
                    stage(t * ru + u, u * L)
                _for_blocks(L // TQ // nb, lambda j: [one(u * L, j * nb + b) for u in range(ru) for b in range(nb)])
                for u in range(ru):
                    unstage(t * ru + u, u * L)
                return 0

            lax.fori_loop(0, d // ru, step, 0)

        for gi, d in enumerate(DILATIONS):
            pl.when(g == gi)(functools.partial(group, d))

        def emit(i, _):
            rows = pl.ds(pl.multiple_of(i * CH, CH), CH)
            c, s1, s2 = c_ref[rows, :], s1_ref[rows, :], s2_ref[rows, :]
            for (_, w_ref, x, tn_s, rr_s, scale), out, gw_ref in zip(both, (dq_ref, dk_ref), (gqw_ref, gkw_ref)):
                tn = tn_s[rows, :]
                dy = _rope_t(x[rows, :] * scale, c, s1, s2)
                gw_ref[0:1, :] = gw_ref[0:1, :] + jnp.sum(dy * tn, axis=0, keepdims=True)
                dtn = dy * w_ref[...]
                out[rows, :] = (rr_s[rows, :] * (dtn - tn * _head_mean(dtn * tn, e))).astype(BF16)
            dv_ref[rows, :] = dvn[rows, :].astype(BF16)
            return 0

        lax.fori_loop(0, S // CH, emit, 0, unroll=4)

    nat_spec = pl.BlockSpec((S, LANES), lambda hp, g: (0, hp))
    out_spec = pl.BlockSpec((None, S, LANES), lambda hp, g: (g, 0, hp))
    acc_spec = pl.BlockSpec((8, LANES), lambda hp, g: (0, 0))
    return _call(
        body, sides, name="attn_bwd", grid=(4, 3),
        in_specs=_qk_specs() + _tab_specs() + [_vec_spec(), _vec_spec(), nat_spec, nat_spec, nat_spec],
        out_specs=[out_spec] * 3 + [acc_spec] * 2,
        out_shape=[jax.ShapeDtypeStruct((QKV // PLANE, S, PLANE), BF16)] * 3 + [jax.ShapeDtypeStruct((8, LANES), F32)] * 2,
        scratch_shapes=[pltpu.VMEM((S, LANES), BF16)] * 4 + [pltpu.VMEM((S, LANES), F32)] * 13,
        args=(proj, proj, proj, *tabs, qw2, kw2, d_attn, attn, lse))


PADR = 16
CT = 128


def _conv_specs():
    return [pl.BlockSpec((S, CC), lambda i: (0, OFF_CA // CC)), pl.BlockSpec((S, CC), lambda i: (0, OFF_CB // CC))]


NCB = CC // LANES


def _pad_zero(pad):
    for cb in range(NCB):
        pad[cb, 0:PADR, :] = jnp.zeros((PADR, LANES), F32)
        pad[cb, PADR + S:PADR + S + PADR, :] = jnp.zeros((PADR, LANES), F32)


def _pad_store(pad, row0, n, val):
    for cb in range(NCB):
        pad[cb, pl.ds(pl.multiple_of(row0 + PADR, 8), n), :] = val[:, cb * LANES:(cb + 1) * LANES]


def _taps(pad_ref, cb, s0, weights):
    acc = jnp.zeros((CT, LANES), F32)
    for k in range(KW):
        acc = acc + weights[k] * pad_ref[cb, pl.ds(s0 + k + 1, CT), :]
    return acc


def conv_fwd(proj, conv_w, conv_b, ln_w, ln_b, sides=()):
    def body(a_ref, b_ref, w_ref, cb_ref, lw_ref, lb_ref, c_ref, u3_ref, upad):
        _pad_zero(upad)

        def glu(i, _):
            rows = pl.ds(pl.multiple_of(i * TM, TM), TM)
            _pad_store(upad, i * TM, TM, a_ref[rows, :] * _sigmoid(b_ref[rows, :]))
            return 0

        lax.fori_loop(0, S // TM, glu, 0)

        def chunk(i, _):
            s0 = pl.multiple_of(i * CT, CT)
            for cb in range(CC // LANES):
                cols = slice(cb * LANES, (cb + 1) * LANES)
                w = [w_ref[k:k + 1, cols] for k in range(KW)]
                c_ref[pl.ds(s0, CT), cols] = _taps(upad, cb, s0, w) + cb_ref[:, cols]
            cv = c_ref[pl.ds(s0, CT), :]
            mu = jnp.mean(cv, axis=-1, keepdims=True)
            xc = cv - mu
            rstd = lax.rsqrt(jnp.mean(xc * xc, axis=-1, keepdims=True) + EPS)
            yl = xc * rstd * lw_ref[...] + lb_ref[...]
            u3_ref[pl.ds(s0, CT), :] = (yl * _sigmoid(yl)).astype(BF16)
            return 0

        lax.fori_loop(0, S // CT, chunk, 0)

    vec = pl.BlockSpec((1, CC), lambda i: (0, 0))
    full = pl.BlockSpec((S, CC), lambda i: (0, 0))
    return _call(
        body, sides, name="conv_fwd", grid=(1,),
        in_specs=_conv_specs() + [pl.BlockSpec((KW, CC), lambda i: (0, 0)), vec, vec, vec],
        out_specs=[full, full],
        out_shape=[jax.ShapeDtypeStruct((S, CC), F32), jax.ShapeDtypeStruct((S, CC), BF16)],
        scratch_shapes=[pltpu.VMEM((NCB, S + 2 * PADR, LANES), F32)],
        args=(proj, proj, conv_w, conv_b, ln_w, ln_b))


def conv_bwd(proj, cpre, d_u3, conv_w, conv_w_rev, ln_w, ln_b, sides=()):
    def body(a_ref, b_ref, c_ref, du3_ref, w_ref, wr_ref, lw_ref, lb_ref,
             dc_ref, gw_ref, gcb_ref, glw_ref, glb_ref, upad, dpad):
        _pad_zero(upad)
        _pad_zero(dpad)
        gw_ref[...] = jnp.zeros_like(gw_ref)

        def ln_bwd(i, carry):
            gcb, glw, glb = carry
            rows = pl.ds(pl.multiple_of(i * TM, TM), TM)
            _pad_store(upad, i * TM, TM, a_ref[rows, :] * _sigmoid(b_ref[rows, :]))
            cv = c_ref[rows, :]
            mu = jnp.mean(cv, axis=-1, keepdims=True)
            xc = cv - mu
            rstd = lax.rsqrt(jnp.mean(xc * xc, axis=-1, keepdims=True) + EPS)
            xh = xc * rstd
            yl = xh * lw_ref[...] + lb_ref[...]
            dyl = du3_ref[rows, :] * _dsilu(yl, _sigmoid(yl))
            dxh = dyl * lw_ref[...]
            dcv = rstd * (dxh - jnp.mean(dxh, axis=-1, keepdims=True)
                          - xh * jnp.mean(dxh * xh, axis=-1, keepdims=True))
            _pad_store(dpad, i * TM, TM, dcv)
            return (gcb + jnp.sum(dcv, axis=0, keepdims=True),
                    glw + jnp.sum(dyl * xh, axis=0, keepdims=True),
                    glb + jnp.sum(dyl, axis=0, keepdims=True))

        z = jnp.zeros((1, CC), F32)
        gcb, glw, glb = lax.fori_loop(0, S // TM, ln_bwd, (z, z, z))
        gcb_ref[...] = gcb
        glw_ref[...] = glw
        glb_ref[...] = glb

        def chunk(i, _):
            s0 = pl.multiple_of(i * CT, CT)
            for cb in range(CC // LANES):
                cols = slice(cb * LANES, (cb + 1) * LANES)
                wr = [wr_ref[k:k + 1, cols] for k in range(KW)]
                du = _taps(dpad, cb, s0, wr)
                dcv = dpad[cb, pl.ds(s0 + PADR, CT), :]
                for k in range(KW):
                    gw_ref[k:k + 1, cols] = gw_ref[k:k + 1, cols] + jnp.sum(
                        upad[cb, pl.ds(s0 + k + 1, CT), :] * dcv, axis=0, keepdims=True)
                av = a_ref[pl.ds(s0, CT), cols]
                sb = _sigmoid(b_ref[pl.ds(s0, CT), cols])
                dc_ref[0, pl.ds(s0, CT), cols] = (du * sb).astype(BF16)
                dc_ref[1, pl.ds(s0, CT), cols] = (du * av * sb * (1.0 - sb)).astype(BF16)
            return 0

        lax.fori_loop(0, S // CT, chunk, 0)

    vec = pl.BlockSpec((1, CC), lambda i: (0, 0))
    full = pl.BlockSpec((S, CC), lambda i: (0, 0))
    wsp = pl.BlockSpec((KW, CC), lambda i: (0, 0))
    return _call(
        body, sides, name="conv_bwd", grid=(1,),
        in_specs=_conv_specs() + [full, full, wsp, wsp, vec, vec],
        out_specs=[pl.BlockSpec((2, S, CC), lambda i: (0, 0, 0)), wsp, vec, vec, vec],
        out_shape=[jax.ShapeDtypeStruct((2, S, CC), BF16), jax.ShapeDtypeStruct((KW, CC), F32)]
        + [jax.ShapeDtypeStruct((1, CC), F32)] * 3,
        scratch_shapes=[pltpu.VMEM((NCB, S + 2 * PADR, LANES), F32)] * 2,
        args=(proj, proj, cpre, d_u3, conv_w, conv_w_rev, ln_w, ln_b))


def _gate_specs():
    return [_row(CC, col=OFF_GA // CC + j) for j in range(4)]


def _gates(g_refs, bg_ref):
    ga = _sigmoid(jnp.concatenate([g_refs[0][...], g_refs[1][...]], axis=1) + bg_ref[0:1, :])
    gb = _sigmoid(jnp.concatenate([g_refs[2][...], g_refs[3][...]], axis=1) + bg_ref[1:2, :])
    return ga, gb


def mix_out(x, proj, b_gate, attn, u3, w_o, w_pw, w_out):
    def body(x_ref, g0, g1, g2, g3, bg_ref, at_ref, u3_ref, wo_ref, wp_ref, wout_ref,
             x1_ref, z_ref, ya_ref, yb_ref):
        ga, gb = _gates((g0, g1, g2, g3), bg_ref)
        ya = _dot(at_ref[...], wo_ref[...])
        yb = _dot(u3_ref[...], wp_ref[...])
        z = (ga * ya + gb * yb).astype(BF16)
        ya_ref[...] = ya.astype(BF16)
        yb_ref[...] = yb.astype(BF16)
        z_ref[...] = z
        x1_ref[...] = x_ref[...] + _dot(z, wout_ref[...])

    return pl.pallas_call(
        body, name="mix_out", grid=(S // TM,),
        in_specs=[_row(D)] + _gate_specs() + [_res((2, D)), _row(CC), _row(CC),
                                              _res((CC, D)), _res((CC, D)), _res((D, D))],
        out_specs=[_row(D)] * 4,
        out_shape=[jax.ShapeDtypeStruct((S, D), F32)] + [jax.ShapeDtypeStruct((S, D), BF16)] * 3,
        compiler_params=_cp(dimension_semantics=("arbitrary",)),
    )(x, proj, proj, proj, proj, b_gate, attn, u3, w_o, w_pw, w_out)


def out_bwd(d_x1b, proj, b_gate, ya, yb, w_o, w_pw, w_out, sides=()):
    def body(dx_ref, g0, g1, g2, g3, bg_ref, ya_ref, yb_ref, wo_ref, wp_ref, wout_ref,
             dya_ref, dyb_ref, dgl_ref, dat_ref, du3_ref, gbg_ref):
        @pl.when(pl.program_id(0) == 0)
        def _():
            gbg_ref[...] = jnp.zeros_like(gbg_ref)

        ga, gb = _gates((g0, g1, g2, g3), bg_ref)
        dz = _dot_nt(dx_ref[...], wout_ref[...])
        dya = (dz * ga).astype(BF16)
        dyb = (dz * gb).astype(BF16)
        dgla = dz * ya_ref[...].astype(F32) * ga * (1.0 - ga)
        dglb = dz * yb_ref[...].astype(F32) * gb * (1.0 - gb)
        dya_ref[...] = dya
        dyb_ref[...] = dyb
        for j in range(2):
            dgl_ref[j] = dgla[:, j * PLANE:(j + 1) * PLANE].astype(BF16)
            dgl_ref[2 + j] = dglb[:, j * PLANE:(j + 1) * PLANE].astype(BF16)
        gbg_ref[0:1, :] = gbg_ref[0:1, :] + jnp.sum(dgla, axis=0, keepdims=True)
        gbg_ref[1:2, :] = gbg_ref[1:2, :] + jnp.sum(dglb, axis=0, keepdims=True)
        dat_ref[...] = _dot_nt(dya, wo_ref[...])
        du3_ref[...] = _dot_nt(dyb, wp_ref[...])

    return _call(
        body, sides, name="out_bwd", grid=(S // TM,),
        in_specs=[_row(D)] + _gate_specs() + [_res((2, D)), _row(D), _row(D),
                                              _res((CC, D)), _res((CC, D)), _res((D, D))],
        out_specs=[_row(D), _row(D), _planes(2 * D), _row(CC), _row(CC), pl.BlockSpec((2, D), lambda i: (0, 0))],
        out_shape=[jax.ShapeDtypeStruct((S, D), BF16)] * 2 + [jax.ShapeDtypeStruct((2 * D // PLANE, S, PLANE), BF16)]
        + [jax.ShapeDtypeStruct((S, CC), F32)] * 2 + [jax.ShapeDtypeStruct((2, D), F32)],
        args=(d_x1b, proj, proj, proj, proj, b_gate, ya, yb, w_o, w_pw, w_out))


def ffn_in(x1, norm_w, w_ffn_in, sides=()):
    half = FF // 2

    def body(x_ref, nw_ref, w_ref, h_ref, gu_ref, f_ref):
        xv = x_ref[...]
        r = lax.rsqrt(jnp.mean(xv * xv, axis=-1, keepdims=True) + EPS)
        h = (xv * r * nw_ref[...]).astype(BF16)
        h_ref[...] = h
        for j in range(2):
            gt = _dot_nt(h, w_ref[j * half:(j + 1) * half, :])
            up = _dot_nt(h, w_ref[FF + j * half:FF + (j + 1) * half, :])
            gu_ref[:, j * half:(j + 1) * half] = gt.astype(BF16)
            gu_ref[:, FF + j * half:FF + (j + 1) * half] = up.astype(BF16)
            f_ref[:, j * half:(j + 1) * half] = (gt * _sigmoid(gt) * up).astype(BF16)

    return _call(
        body, sides, name="ffn_in", grid=(S // TM,),
        in_specs=[_row(D), _res((1, D)), _res((2 * FF, D))],
        out_specs=[_row(D), _row(2 * FF), _row(FF)],
        out_shape=[jax.ShapeDtypeStruct((S, D), BF16), jax.ShapeDtypeStruct((S, 2 * FF), BF16),
                   jax.ShapeDtypeStruct((S, FF), BF16)],
        args=(x1, norm_w, w_ffn_in))


def ffn_out_loss(x1, f, w_ffn_out, target):
    def body(x_ref, f_ref, w_ref, t_ref, dy_ref, dyb_ref, sq_ref):
        @pl.when(pl.program_id(0) == 0)
        def _():
            sq_ref[...] = jnp.zeros_like(sq_ref)

        diff = x_ref[...] + _dot(f_ref[...], w_ref[...]) - t_ref[...]
        dy = diff * (1.0 / D)
        dy_ref[...] = dy
        dyb_ref[...] = dy.astype(BF16)
        sq_ref[...] = sq_ref[...] + jnp.sum((diff * diff).reshape(TM // 8, 8, D), axis=0)

    return pl.pallas_call(
        body, name="ffn_out_loss", grid=(S // TM,),
        in_specs=[_row(D), _row(FF), _res((FF, D)), _row(D)],
        out_specs=[_row(D), _row(D), pl.BlockSpec((8, D), lambda i: (0, 0))],
        out_shape=[jax.ShapeDtypeStruct((S, D), F32), jax.ShapeDtypeStruct((S, D), BF16),
                   jax.ShapeDtypeStruct((8, D), F32)],
        compiler_params=_cp(dimension_semantics=("arbitrary",)),
    )(x1, f, w_ffn_out, target)


def _rms_bwd(xv, nw, dh):
    r = lax.rsqrt(jnp.mean(xv * xv, axis=-1, keepdims=True) + EPS)
    xn = xv * r
    dxn = dh * nw
    dx = r * (dxn - xn * jnp.mean(dxn * xn, axis=-1, keepdims=True))
    return dx, dh * xn


def ffn_bwd(dy, dyb, gu, x1, norm_w, w_ffn_in, w_ffn_out, sides=()):
    def body(dy_ref, dyb_ref, gu_ref, x_ref, nw_ref, wi_ref, wo_ref, dgu_ref, dx_ref, dxb_ref, gn_ref):
        @pl.when(pl.program_id(0) == 0)
        def _():
            gn_ref[...] = jnp.zeros_like(gn_ref)

        df = _dot_nt(dyb_ref[...], wo_ref[...])
        gt = gu_ref[:, 0:FF].astype(F32)
        up = gu_ref[:, FF:2 * FF].astype(F32)
        sg = _sigmoid(gt)
        dgt = (df * up * _dsilu(gt, sg)).astype(BF16)
        dup = (df * gt * sg).astype(BF16)
        dgu_ref[:, 0:FF] = dgt
        dgu_ref[:, FF:2 * FF] = dup
        dh = _dot(dgt, wi_ref[0:FF, :]) + _dot(dup, wi_ref[FF:2 * FF, :])
        dxn, gw = _rms_bwd(x_ref[...], nw_ref[...], dh)
        dx = dy_ref[...] + dxn
        dx_ref[...] = dx
        dxb_ref[...] = dx.astype(BF16)
        gn_ref[...] = gn_ref[...] + jnp.sum(gw, axis=0, keepdims=True)

    return _call(
        body, sides, name="ffn_bwd", grid=(S // TM,),
        in_specs=[_row(D), _row(D), _row(2 * FF), _row(D), _res((1, D)), _res((2 * FF, D)), _res((FF, D))],
        out_specs=[_row(2 * FF), _row(D), _row(D), pl.BlockSpec((1, D), lambda i: (0, 0))],
        out_shape=[jax.ShapeDtypeStruct((S, 2 * FF), BF16), jax.ShapeDtypeStruct((S, D), F32),
                   jax.ShapeDtypeStruct((S, D), BF16), jax.ShapeDtypeStruct((1, D), F32)],
        args=(dy, dyb, gu, x1, norm_w, w_ffn_in, w_ffn_out))


def in_bwd(d_q, d_k, d_v, d_conv, d_gl, w_in, x, d_x1, norm_w, sides=()):
    segs = ((OFF_Q, QKV), (OFF_K, QKV), (OFF_V, QKV), (OFF_CA, 2 * CC), (OFF_GA, 2 * D))

    def body(dq_ref, dk_ref, dv_ref, dc_ref, dg_ref, w_ref, x_ref, dx1_ref, nw_ref, gx_ref, gn_ref):
        @pl.when(pl.program_id(0) == 0)
        def _():
            gn_ref[...] = jnp.zeros_like(gn_ref)

        dh = jnp.zeros((TM, D), F32)
        for ref, (off, width) in zip((dq_ref, dk_ref, dv_ref, dc_ref, dg_ref), segs):
            for j in range(width // PLANE):
                dh = dh + _dot(ref[j], w_ref[off + j * PLANE:off + (j + 1) * PLANE, :])
        dxn, gw = _rms_bwd(x_ref[...], nw_ref[...], dh)
        gx_ref[...] = dx1_ref[...] + dxn
        gn_ref[...] = gn_ref[...] + jnp.sum(gw, axis=0, keepdims=True)

    return _call(
        body, sides, name="in_bwd", grid=(S // TM,),
        in_specs=[_planes(QKV)] * 3 + [_planes(2 * CC), _planes(2 * D), _res((INW, D)), _row(D), _row(D), _res((1, D))],
        out_specs=[_row(D), pl.BlockSpec((1, D), lambda i: (0, 0))],
        out_shape=[jax.ShapeDtypeStruct((S, D), F32), jax.ShapeDtypeStruct((1, D), F32)],
        args=(d_q, d_k, d_v, d_conv, d_gl, w_in, x, d_x1, norm_w))


def mm_tn(name, a, b, tm, tn, sides=()):
    M, N = a.shape[1], b.shape[1]

    def body(a_ref, b_ref, o_ref):
        o_ref[...] = _dot_tn(a_ref[...], b_ref[...])

    res = _call(
        body, sides, name=name, grid=(M // tm, N // tn),
        in_specs=[pl.BlockSpec((S, tm), lambda i, j: (0, i)), pl.BlockSpec((S, tn), lambda i, j: (0, j))],
        out_specs=[pl.BlockSpec((tm, tn), lambda i, j: (i, j))],
        out_shape=[jax.ShapeDtypeStruct((M, N), F32)],
        args=(a, b))
    return (res[0][0], res[1]) if sides else res[0]


GW_IN_TN = PLANE
GW_IN_SPLIT = (768, 256)


def gw_in_t(name, ht, d_segs, col0, hw, sides=()):
    tn = GW_IN_TN
    starts, t0 = [], 0
    for seg in d_segs:
        starts.append(t0)
        t0 += seg.shape[0]
    ntiles = [seg.shape[0] for seg in d_segs]

    def body(h_ref, *refs):
        a_refs, o_ref = refs[:-1], refs[-1]
        n = pl.program_id(0)
        for a_ref, st, nt in zip(a_refs, starts, ntiles):
            @pl.when((n >= st) & (n < st + nt))
            def _(a_ref=a_ref):
                o_ref[...] = _dot(h_ref[...], a_ref[...]).T

    def seg_spec(st, nt):
        return pl.BlockSpec((None, S, tn), lambda n: (jnp.clip(n - st, 0, nt - 1), 0, 0))

    res = _call(
        body, sides, name=name, grid=(INW // tn,),
        in_specs=[pl.BlockSpec((hw, S), lambda n: (col0 // hw, 0))] + [seg_spec(st, nt) for st, nt in zip(starts, ntiles)],
        out_specs=[pl.BlockSpec((tn, hw), lambda n: (n, 0))],
        out_shape=[jax.ShapeDtypeStruct((INW, hw), F32)],
        args=(ht, *d_segs))
    return (res[0][0], res[1]) if sides else res[0]


def _place():
    x, y, c = lax.axis_index("x"), lax.axis_index("y"), lax.axis_index("c")
    chips = [(1 - x, y), (x, 1 - y), (1 - x, 1 - y)]
    return x, y, c, chips


def _sems(n):
    return pltpu.SemaphoreType.DMA((n,))


def _remote(src, dst, send, recv, k, to):
    return pltpu.make_async_remote_copy(src_ref=src, dst_ref=dst, send_sem=send.at[k], recv_sem=recv.at[k],
                                        device_id=to, device_id_type=MESH)


def _cast_rows(dst, src, cols=slice(None)):
    rows = src.shape[0]
    step = next((s for s in (128, 64, 32, 16) if rows % s == 0), rows)
    for r0 in range(0, rows, step):
        dst[r0:r0 + step, cols] = src[r0:r0 + step, :].astype(dst.dtype)


def comm_only(name, sides):
    def body():
        pass

    return _call(body, sides, name=name, grid=(1,), in_specs=[], out_specs=[], out_shape=[], args=())[1]


def ag_blocks(shard, dtype):
    R, W = shard.shape

    def copy(outs, scr, k, block, to, src=None):
        dst = outs[0].at[block]
        return _remote(dst if src is None else src, dst, scr[1], scr[2], k, to)

    def local(outs, scr, me):
        return pltpu.make_async_copy(scr[0], outs[0].at[me], scr[3].at[0])

    def start(ins, outs, scr):
        x, y, c, chips = _place()
        me = 4 * x + 2 * y + c
        _cast_rows(scr[0], ins[0])
        local(outs, scr, me).start()
        copy(outs, scr, 0, me, (x, y, 1 - c), src=scr[0]).start()
        for j, (cx, cy) in enumerate(chips):
            copy(outs, scr, 1 + j, me, (cx, cy, c), src=scr[0]).start()

    def finish(ins, outs, scr):
        x, y, c, chips = _place()
        me, sib = 4 * x + 2 * y + c, (x, y, 1 - c)
        passed = []
        for j, (cx, cy) in enumerate(chips):
            theirs = 4 * cx + 2 * cy + c
            copy(outs, scr, 1 + j, theirs, (x, y, c)).wait_recv()
            fwd = copy(outs, scr, 4 + j, theirs, sib)
            fwd.start()
            passed.append(fwd)
        copy(outs, scr, 0, 4 * x + 2 * y + 1 - c, (x, y, c)).wait_recv()
        for j, (cx, cy) in enumerate(chips):
            copy(outs, scr, 4 + j, 4 * cx + 2 * cy + 1 - c, (x, y, c)).wait_recv()
        copy(outs, scr, 0, me, sib, src=scr[0]).wait_send()
        for j, (cx, cy) in enumerate(chips):
            copy(outs, scr, 1 + j, me, (cx, cy, c), src=scr[0]).wait_send()
        for fwd in passed:
            fwd.wait_send()
        local(outs, scr, me).wait()

    return Side((shard,), (VMEM,), (jax.ShapeDtypeStruct((NDEV, R, W), dtype),),
                (pltpu.VMEM((R, W), dtype), _sems(7), _sems(7), _sems(1)), start, finish, None, "dsxy")


def ag_blocks_relay(shard, dtype):
    R, W = shard.shape
    half = R // 2

    def copy(outs, scr, k, block, to, src=None, rows=None):
        dst = outs[0].at[block] if rows is None else outs[0].at[block, pl.ds(rows * half, half), :]
        return _remote(dst if src is None else src, dst, scr[1], scr[2], k, to)

    def local(outs, scr, me):
        return pltpu.make_async_copy(scr[0], outs[0].at[me], scr[3].at[0])

    def own(outs, scr):
        x, y, c, _ = _place()
        me = 4 * x + 2 * y + c
        return [copy(outs, scr, k, me, to, src=scr[0])
                for k, to in enumerate([(x, y, 1 - c), (1 - x, y, c), (x, 1 - y, c)])]

    def start(ins, outs, scr):
        x, y, c, _ = _place()
        _cast_rows(scr[0], ins[0])
        local(outs, scr, 4 * x + 2 * y + c).start()
        for cp in own(outs, scr):
            cp.start()

    def passed_on(outs, scr):
        x, y, c, _ = _place()
        sib, xn, yn = (x, y, 1 - c), (1 - x, y, c), (x, 1 - y, c)
        b_xn, b_yn, b_dg = 4 * (1 - x) + 2 * y + c, 4 * x + 2 * (1 - y) + c, 4 * (1 - x) + 2 * (1 - y) + c
        near = [copy(outs, scr, 5, b_xn, yn, rows=0), copy(outs, scr, 3, b_xn, sib),
                copy(outs, scr, 6, b_yn, xn, rows=1), copy(outs, scr, 4, b_yn, sib)]
        far = [copy(outs, scr, 7, b_dg, sib, rows=0), copy(outs, scr, 8, b_dg, sib, rows=1)]
        return (b_xn, b_yn, b_dg), near, far

    def mid(ins, outs, scr):
        x, y, c, _ = _place()
        (b_xn, b_yn, _), near, _ = passed_on(outs, scr)
        copy(outs, scr, 1, b_xn, (x, y, c)).wait_recv()
        near[0].start()
        near[1].start()
        copy(outs, scr, 2, b_yn, (x, y, c)).wait_recv()
        near[2].start()
        near[3].start()

    def finish(ins, outs, scr):
        x, y, c, _ = _place()
        here = (x, y, c)
        (b_xn, b_yn, b_dg), near, far = passed_on(outs, scr)
        copy(outs, scr, 5, b_dg, here, rows=0).wait_recv()
        far[0].start()
        copy(outs, scr, 6, b_dg, here, rows=1).wait_recv()
        far[1].start()
        flip = 1 - 2 * c
        copy(outs, scr, 0, 4 * x + 2 * y + 1 - c, here).wait_recv()
        copy(outs, scr, 3, b_xn + flip, here).wait_recv()
        copy(outs, scr, 4, b_yn + flip, here).wait_recv()
        copy(outs, scr, 7, b_dg + flip, here, rows=0).wait_recv()
        copy(outs, scr, 8, b_dg + flip, here, rows=1).wait_recv()
        for cp in own(outs, scr) + near + far:
            cp.wait_send()
        local(outs, scr, 4 * x + 2 * y + c).wait()

    return Side((shard,), (VMEM,), (jax.ShapeDtypeStruct((NDEV, R, W), dtype),),
                (pltpu.VMEM((R, W), dtype), _sems(9), _sems(9), _sems(1)), start, finish, mid, "sxy")


def ag_cols(shard):
    K, C = shard.shape
    half, w2 = K // 2, 2 * C

    def win(out, rows_c, chip):
        return out.at[pl.ds(pl.multiple_of(rows_c * half, 16), half), pl.ds(pl.multiple_of(chip * w2, LANES), w2)]

    def ici(outs, scr, j, to, c, k):
        slab, send, recv = scr[2], scr[5], scr[6]
        return _remote(slab.at[pl.ds(pl.multiple_of(c * half, 16), half), :], win(outs[0], c, k), send, recv, j, to)

    def local(outs, scr, k):
        return pltpu.make_async_copy(scr[2], outs[0].at[:, pl.ds(pl.multiple_of(k * w2, LANES), w2)], scr[7].at[0])

    def start(ins, outs, scr):
        stage, inbox, slab, xs, xr = scr[:5]
        x, y, c, chips = _place()
        k = 2 * x + y
        _cast_rows(stage, ins[0])
        swap = _remote(stage, inbox, xs, xr, 0, (x, y, 1 - c))
        swap.start()
        for cc in range(2):
            @pl.when(c == cc)
            def _(cc=cc):
                _cast_rows(slab, stage, slice(cc * C, (cc + 1) * C))
        swap.wait()
        for cc in range(2):
            @pl.when(c == cc)
            def _(cc=cc):
                _cast_rows(slab, inbox, slice((1 - cc) * C, (2 - cc) * C))
        local(outs, scr, k).start()
        for j, (cx, cy) in enumerate(chips):
            ici(outs, scr, j, (cx, cy, c), c, k).start()

    def finish(ins, outs, scr):
        send, recv = scr[5], scr[6]
        x, y, c, chips = _place()
        k, sib = 2 * x + y, (x, y, 1 - c)
        passed = []
        for j, (cx, cy) in enumerate(chips):
            w = win(outs[0], c, 2 * cx + cy)
            _remote(w, w, send, recv, j, sib).wait_recv()
            fwd = _remote(w, w, send, recv, 3 + j, sib)
            fwd.start()
            passed.append(fwd)
        for j, (cx, cy) in enumerate(chips):
            w = win(outs[0], 1 - c, 2 * cx + cy)
            _remote(w, w, send, recv, 3 + j, sib).wait_recv()
        for j, (cx, cy) in enumerate(chips):
            ici(outs, scr, j, (cx, cy, c), c, k).wait_send()
        for fwd in passed:
            fwd.wait_send()
        local(outs, scr, k).wait()

    return Side((shard,), (VMEM,), (jax.ShapeDtypeStruct((K, NDEV * C), BF16),),
                (pltpu.VMEM((K, C), BF16), pltpu.VMEM((K, C), BF16), pltpu.VMEM((K, w2), BF16),
                 _sems(1), _sems(1), _sems(6), _sems(6), _sems(1)), start, finish, None, "dsxy")


def copies_side(args, out_shape, n_copies, plan, peers):
    def copies(ins, outs, scr):
        return [_remote(s_, d_, scr[0], scr[1], i, to) for i, (s_, d_, to) in enumerate(plan(ins, outs))]

    def start(ins, outs, scr):
        for cp in copies(ins, outs, scr):
            cp.start()

    def finish(ins, outs, scr):
        for cp in copies(ins, outs, scr):
            cp.wait()

    return Side(tuple(args), (ANY,) * len(args), tuple(out_shape), (_sems(n_copies), _sems(n_copies)),
                start, finish, None, peers)


def rs_to_sibling(grads):
    out_shape = [jax.ShapeDtypeStruct((4,) + g.shape[1:] if kind == "rows" else (g.shape[0] // 2, g.shape[1]), F32)
                 for kind, g in grads]

    def plan(ins, outs):
        x, y, c, _ = _place()
        sib, res = (x, y, 1 - c), []
        for (kind, _), g, r in zip(grads, ins, outs):
            if kind == "rows":
                res += [(g.at[2 * k + 1 - c], r.at[k], sib) for k in range(4)]
            else:
                half = g.shape[0] // 2
                res.append((g.at[pl.ds(pl.multiple_of((1 - c) * half, 8), half), :], r, sib))
        return res

    return copies_side([g for _, g in grads], out_shape, sum(4 if kind == "rows" else 1 for kind, _ in grads), plan, "s")


def rs_to_chips(parts):
    out_shape = [jax.ShapeDtypeStruct((3,) + p.shape[1:] if kind == "rows" else (3, p.shape[0], p.shape[1] // 4), BF16)
                 for kind, p in parts]

    def plan(ins, outs):
        x, y, c, chips = _place()
        res = []
        for (kind, _), p, r in zip(parts, ins, outs):
            for j, (cx, cy) in enumerate(chips):
                if kind == "rows":
                    src = p.at[2 * cx + cy]
                else:
                    w2 = p.shape[1] // 4
                    src = p.at[:, pl.ds(pl.multiple_of((2 * cx + cy) * w2, LANES), w2)]
                res.append((src, r.at[j], (cx, cy, c)))
        return res

    return copies_side([p for _, p in parts], out_shape, 3 * len(parts), plan, "dxy")


def rs_swap_halves(theirs):
    def plan(ins, outs):
        x, y, c, _ = _place()
        return [(t, r, (x, y, 1 - c)) for t, r in zip(ins, outs)]

    return copies_side(theirs, [jax.ShapeDtypeStruct(t.shape, F32) for t in theirs], len(theirs), plan, "s")


def _row_tiles(rows):
    return 2 if rows % 32 == 0 and rows >= 512 else 1


def chip_sum(name, grad, recv, c_idx, chip_idx):
    _, R, C = grad.shape
    nt = 1
    tr = R // nt

    def body(s_ref, g_ref, r_ref, p_ref, own_ref):
        k = pl.program_id(1)
        tot = g_ref[0] + r_ref[0]
        p_ref[0] = tot.astype(BF16)

        @pl.when(k == s_ref[1])
        def _():
            own_ref[...] = tot

    grid_spec = pltpu.PrefetchScalarGridSpec(
        num_scalar_prefetch=1, grid=(nt, 4),
        in_specs=[pl.BlockSpec((1, tr, C), lambda i, k, s: (2 * k + s[0], i, 0)),
                  pl.BlockSpec((1, tr, C), lambda i, k, s: (k, i, 0))],
        out_specs=[pl.BlockSpec((1, tr, C), lambda i, k, s: (k, i, 0)),
                   pl.BlockSpec((tr, C), lambda i, k, s: (i, 0))])
    return pl.pallas_call(
        body, name=name, grid_spec=grid_spec,
        out_shape=[jax.ShapeDtypeStruct((4, R, C), BF16), jax.ShapeDtypeStruct((R, C), F32)],
        compiler_params=_cp(dimension_semantics=("arbitrary", "arbitrary")),
    )(jnp.stack([c_idx, chip_idx]), grad, recv)


def _half_tiles(half):
    return 2 if half >= 512 else 1


def chip_sum_cols(name, grad, recv, c_idx, chip_idx):
    K, W = grad.shape
    half, w2 = K // 2, W // 4
    nt = _half_tiles(half)
    tr = half // nt

    def body(s_ref, g_ref, r_ref, p_ref, own_ref):
        tot = g_ref[...] + r_ref[...]
        p_ref[...] = tot.astype(BF16)

        @pl.when(pl.program_id(1) == s_ref[1])
        def _():
            own_ref[...] = tot

    grid_spec = pltpu.PrefetchScalarGridSpec(
        num_scalar_prefetch=1, grid=(nt, 4),
        in_specs=[pl.BlockSpec((tr, w2), lambda i, k, s: (s[0] * nt + i, k)),
                  pl.BlockSpec((tr, w2), lambda i, k, s: (i, k))],
        out_specs=[pl.BlockSpec((tr, w2), lambda i, k, s: (i, k)),
                   pl.BlockSpec((tr, w2), lambda i, k, s: (i, 0))])
    return pl.pallas_call(
        body, name=name, grid_spec=grid_spec,
        out_shape=[jax.ShapeDtypeStruct((half, W), BF16), jax.ShapeDtypeStruct((half, w2), F32)],
        compiler_params=_cp(dimension_semantics=("arbitrary", "arbitrary")),
    )(jnp.stack([c_idx, chip_idx]), grad, recv)


def col_final(name, own, recv, c_idx):
    half, w2 = own.shape
    C = w2 // 2
    nt = _half_tiles(half)
    tr = half // nt

    def body(s_ref, o_ref, r_ref, mine_ref, theirs_ref, t_ref):
        t_ref[...] = o_ref[...] + r_ref[0].astype(F32) + r_ref[1].astype(F32) + r_ref[2].astype(F32)
        for cc in range(2):
            @pl.when(s_ref[0] == cc)
            def _(cc=cc):
                mine_ref[...] = t_ref[:, cc * C:(cc + 1) * C]
                theirs_ref[...] = t_ref[:, (1 - cc) * C:(2 - cc) * C]

    grid_spec = pltpu.PrefetchScalarGridSpec(
        num_scalar_prefetch=1, grid=(nt,),
        in_specs=[pl.BlockSpec((tr, w2), lambda i, s: (i, 0)), pl.BlockSpec((3, tr, w2), lambda i, s: (0, i, 0))],
        out_specs=[pl.BlockSpec((tr, C), lambda i, s: (i, 0))] * 2,
        scratch_shapes=[pltpu.VMEM((tr, w2), F32)])
    return pl.pallas_call(
        body, name=name, grid_spec=grid_spec, out_shape=[jax.ShapeDtypeStruct((half, C), F32)] * 2,
        compiler_params=_cp(dimension_semantics=("arbitrary",)),
    )(jnp.stack([c_idx]), own, recv)


def _adamw(w, g, m, v):
    m2 = ADAM_B1 * m + (1.0 - ADAM_B1) * g
    v2 = ADAM_B2 * v + (1.0 - ADAM_B2) * (g * g)
    m_hat = m2 / (1.0 - ADAM_B1 ** ADAM_STEP)
    v_hat = v2 / (1.0 - ADAM_B2 ** ADAM_STEP)
    delta = -ADAM_LR * (m_hat / (jnp.sqrt(v_hat) + ADAM_EPS) + ADAM_WD * w)
    return delta, m2, v2


def shard_adam(name, owns, recvs, w, m, v):
    n = len(owns)
    R = owns[0].shape[0]
    ct = min(o.shape[1] for o in owns)
    first = [sum(o.shape[1] for o in owns[:j]) // ct for j in range(n)]
    count = [o.shape[1] // ct for o in owns]
    nt = _row_tiles(R)
    tr = R // nt

    def body(*refs):
        o_refs, r_refs = refs[:n], refs[n:2 * n]
        w_ref, m_ref, v_ref, g_ref, d_ref, nm_ref, nv_ref = refs[2 * n:]
        g = None
        for j in range(n):
            gj = o_refs[j][...] + r_refs[j][0].astype(F32) + r_refs[j][1].astype(F32) + r_refs[j][2].astype(F32)
            g = gj if g is None else jnp.where(pl.program_id(0) >= first[j], gj, g)
        delta, m2, v2 = _adamw(w_ref[...], g, m_ref[...], v_ref[...])
        g_ref[...] = g
        d_ref[...] = delta
        nm_ref[...] = m2
        nv_ref[...] = v2

    def part(j):
        return pl.BlockSpec((tr, ct), lambda k, i: (i, jnp.clip(k - first[j], 0, count[j] - 1)))

    def part3(j):
        return pl.BlockSpec((3, tr, ct), lambda k, i: (0, i, jnp.clip(k - first[j], 0, count[j] - 1)))

    tile = pl.BlockSpec((tr, ct), lambda k, i: (i, k))
    return pl.pallas_call(
        body, name=name, grid=(sum(count), nt),
        in_specs=[part(j) for j in range(n)] + [part3(j) for j in range(n)] + [tile, tile, tile],
        out_specs=[tile] * 4, out_shape=[jax.ShapeDtypeStruct((R, sum(count) * ct), F32)] * 4,
        compiler_params=_cp(dimension_semantics=("arbitrary", "arbitrary")),
    )(*owns, *recvs, w, m, v)


def adam_cols(name, mine, recv, w, m, v, c_idx):
    half, C = mine.shape
    nt = _half_tiles(half)
    tr = half // nt

    def body(s_ref, a_ref, b_ref, w_ref, m_ref, v_ref, g_ref, d_ref, nm_ref, nv_ref):
        g = jnp.where(pl.program_id(0) == s_ref[0], a_ref[...], b_ref[...])
        delta, m2, v2 = _adamw(w_ref[...], g, m_ref[...], v_ref[...])
        g_ref[...] = g
        d_ref[...] = delta
        nm_ref[...] = m2
        nv_ref[...] = v2

    part = pl.BlockSpec((tr, C), lambda hh, i, s: (i, 0))
    tile = pl.BlockSpec((tr, C), lambda hh, i, s: (hh * nt + i, 0))
    grid_spec = pltpu.PrefetchScalarGridSpec(
        num_scalar_prefetch=1, grid=(2, nt), in_specs=[part, part, tile, tile, tile], out_specs=[tile] * 4)
    return pl.pallas_call(
        body, name=name, grid_spec=grid_spec, out_shape=[jax.ShapeDtypeStruct((2 * half, C), F32)] * 4,
        compiler_params=_cp(dimension_semantics=("arbitrary", "arbitrary")),
    )(jnp.stack([c_idx]), mine, recv, w, m, v)


ROW_N1, ROW_N2, ROW_BG, ROW_QN, ROW_KN, ROW_CB, ROW_LW, ROW_LB, ROW_CW = 0, 1, 2, 4, 5, 6, 7, 8, 9
PACK_ROWS = 40
SMALL = ("norm1_w", "norm2_w", "b_gate", "q_norm_w", "k_norm_w", "conv_b", "conv_ln_w", "conv_ln_b", "conv_w")


def small_sync_adam(g, w, m, v, sq, sides=()):
    ns = len(SMALL)

    def body(*refs):
        gi = dict(zip(SMALL, refs[:ns]))
        wi = dict(zip(SMALL, refs[ns:2 * ns]))
        mi = dict(zip(SMALL, refs[2 * ns:3 * ns]))
        vi = dict(zip(SMALL, refs[3 * ns:4 * ns]))
        sq_ref = refs[4 * ns]
        outs = refs[4 * ns + 1:8 * ns + 1]
        loss_ref = refs[8 * ns + 1]
        pack, recv, tot, send_sems, recv_sems = refs[8 * ns + 2:]
        x, y, c, _ = _place()
        me = 4 * x + 2 * y + c

        pack[...] = jnp.zeros_like(pack)
        pack[ROW_KN:ROW_KN + 1, LANES:2 * LANES] = jnp.full((1, LANES), (0.5 / D) * jnp.sum(sq_ref[...]), F32)
        pack[ROW_N1:ROW_N1 + 1, :] = gi["norm1_w"][...]
        pack[ROW_N2:ROW_N2 + 1, :] = gi["norm2_w"][...]
        pack[ROW_BG:ROW_BG + 2, :] = gi["b_gate"][...]
        pack[ROW_QN:ROW_QN + 1, 0:HD] = gi["q_norm_w"][...]
        pack[ROW_KN:ROW_KN + 1, 0:HD] = gi["k_norm_w"][...]
        pack[ROW_CB:ROW_CB + 1, 0:CC] = gi["conv_b"][...]
        pack[ROW_LW:ROW_LW + 1, 0:CC] = gi["conv_ln_w"][...]
        pack[ROW_LB:ROW_LB + 1, 0:CC] = gi["conv_ln_b"][...]
        pack[ROW_CW:ROW_CW + KW, 0:CC] = gi["conv_w"][...]

        copies = []
        for k in range(1, NDEV):
            peer = (x ^ (k >> 2), y ^ ((k >> 1) & 1), c ^ (k & 1))
            cp = pltpu.make_async_remote_copy(
                src_ref=pack, dst_ref=recv.at[me], send_sem=send_sems.at[k - 1], recv_sem=recv_sems.at[k - 1],
                device_id=peer, device_id_type=MESH)
            cp.start()
            copies.append(cp)
        recv[me] = pack[...]
        for cp in copies:
            cp.wait()
        acc = recv[0]
        for p in range(1, NDEV):
            acc = acc + recv[p]
        tot[...] = acc

        def shard_grad(name):
            if name == "b_gate":
                return tot[ROW_BG:ROW_BG + 2, pl.ds(pl.multiple_of(me * LANES, LANES), LANES)]
            if name == "conv_w":
                win = tot[ROW_CW:ROW_CW + KW, pl.ds(pl.multiple_of((me // 2) * LANES, LANES), LANES)]
                return jnp.where(me % 2 == 1, win[:, HD:LANES], win[:, 0:HD])
            row = {"norm1_w": ROW_N1, "norm2_w": ROW_N2, "q_norm_w": ROW_QN, "k_norm_w": ROW_KN,
                   "conv_b": ROW_CB, "conv_ln_w": ROW_LW, "conv_ln_b": ROW_LB}[name]
            return tot[row:row + 1, 0:wi[name].shape[1]]

        for i, name in enumerate(SMALL):
            gr = shard_grad(name)
            delta, m2, v2 = _adamw(wi[name][...], gr, mi[name][...], vi[name][...])
            outs[4 * i][...] = gr
            outs[4 * i + 1][...] = delta
            outs[4 * i + 2][...] = m2
            outs[4 * i + 3][...] = v2
        loss_ref[...] = tot[ROW_KN:ROW_KN + 1, LANES:2 * LANES]

    out_shape = []
    for name in SMALL:
        out_shape += [jax.ShapeDtypeStruct(w[name].shape, F32)] * 4
    out_shape.append(jax.ShapeDtypeStruct((1, LANES), F32))
    args = [g[k] for k in SMALL] + [w[k] for k in SMALL] + [m[k] for k in SMALL] + [v[k] for k in SMALL] + [sq]
    res = _call(
        body, sides, name="small_sync_adam", grid=(1,), in_specs=[VMEM] * len(args),
        out_specs=[VMEM] * len(out_shape), out_shape=out_shape,
        scratch_shapes=[pltpu.VMEM((PACK_ROWS, D), F32), pltpu.VMEM((NDEV, PACK_ROWS, D), F32),
                        pltpu.VMEM((PACK_ROWS, D), F32), _sems(NDEV - 1), _sems(NDEV - 1)],
        args=args, own_comm=True)
    res, side_outs = res if sides else (res, None)
    out = {name: tuple(res[4 * i:4 * i + 4]) for i, name in enumerate(SMALL)}
    loss = res[4 * ns][0, 0]
    return (out, loss, side_outs) if sides else (out, loss)


MATS = ("w_in", "w_o_attn", "w_pw_conv", "w_out", "w_ffn_in", "w_ffn_out")
TRANSPOSED = ("w_in", "w_ffn_in")
WEIGHTS = ("norm1_w", "w_in", "b_gate", "q_norm_w", "k_norm_w", "w_o_attn", "conv_w", "conv_b", "conv_ln_w",
           "conv_ln_b", "w_pw_conv", "w_out", "norm2_w", "w_ffn_in", "w_ffn_out")


def _blocks_to_cols(blocks):
    n, R, C = blocks.shape
    return blocks.transpose(1, 0, 2).reshape(R, n * C)


def kernel(x, positions, norm1_w, w_in, b_gate, q_norm_w, k_norm_w, w_o_attn, conv_w, conv_b, conv_ln_w, conv_ln_b, w_pw_conv, w_out, norm2_w, w_ffn_in, w_ffn_out, loss_target, m_norm1_w, m_w_in, m_b_gate, m_q_norm_w, m_k_norm_w, m_w_o_attn, m_conv_w, m_conv_b, m_conv_ln_w, m_conv_ln_b, m_w_pw_conv, m_w_out, m_norm2_w, m_w_ffn_in, m_w_ffn_out, v_norm1_w, v_w_in, v_b_gate, v_q_norm_w, v_k_norm_w, v_w_o_attn, v_conv_w, v_conv_b, v_conv_ln_w, v_conv_ln_b, v_w_pw_conv, v_w_out, v_norm2_w, v_w_ffn_in, v_w_ffn_out):
    w = dict(norm1_w=norm1_w, w_in=w_in, b_gate=b_gate, q_norm_w=q_norm_w, k_norm_w=k_norm_w, w_o_attn=w_o_attn,
             conv_w=conv_w, conv_b=conv_b, conv_ln_w=conv_ln_w, conv_ln_b=conv_ln_b, w_pw_conv=w_pw_conv,
             w_out=w_out, norm2_w=norm2_w, w_ffn_in=w_ffn_in, w_ffn_out=w_ffn_out)
    m = dict(norm1_w=m_norm1_w, w_in=m_w_in, b_gate=m_b_gate, q_norm_w=m_q_norm_w, k_norm_w=m_k_norm_w,
             w_o_attn=m_w_o_attn, conv_w=m_conv_w, conv_b=m_conv_b, conv_ln_w=m_conv_ln_w,
             conv_ln_b=m_conv_ln_b, w_pw_conv=m_w_pw_conv, w_out=m_w_out, norm2_w=m_norm2_w,
             w_ffn_in=m_w_ffn_in, w_ffn_out=m_w_ffn_out)
    v = dict(norm1_w=v_norm1_w, w_in=v_w_in, b_gate=v_b_gate, q_norm_w=v_q_norm_w, k_norm_w=v_k_norm_w,
             w_o_attn=v_w_o_attn, conv_w=v_conv_w, conv_b=v_conv_b, conv_ln_w=v_conv_ln_w,
             conv_ln_b=v_conv_ln_b, w_pw_conv=v_w_pw_conv, w_out=v_w_out, norm2_w=v_norm2_w,
             w_ffn_in=v_w_ffn_in, w_ffn_out=v_w_ffn_out)
    def two_d(t):
        t = {k: (a[0] if a.ndim == 3 else a) for k, a in t.items()}
        return {k: (a.T if k in TRANSPOSED else a) for k, a in t.items()}

    w, m, v = two_d(w), two_d(m), two_d(v)

    x2, target = x[0], loss_target[0]
    c_idx = lax.axis_index("c").astype(jnp.int32)
    chip_idx = (2 * lax.axis_index("x") + lax.axis_index("y")).astype(jnp.int32)
    qw2 = jnp.tile(w["q_norm_w"], (1, 2))
    kw2 = jnp.tile(w["k_norm_w"], (1, 2))

    tabs, ((bg_blocks,), (cw_blocks,)) = rope_tables(
        positions.reshape(S, 1), sides=(ag_blocks(w["b_gate"], F32), ag_blocks(w["conv_w"], F32)))
    b_gate_f, conv_w_f = _blocks_to_cols(bg_blocks), _blocks_to_cols(cw_blocks)
    ax, ay = lax.axis_index("x"), lax.axis_index("y")
    chip_order = jnp.stack([2 * ax + ay, 2 * (1 - ax) + ay, 2 * ax + 1 - ay, 2 * (1 - ax) + 1 - ay]).astype(jnp.int32)
    h_t, proj, w_in_blocks = in_proj_gather(x2, w["norm1_w"], w["w_in"], chip_order)
    w_in_t = w_in_blocks.reshape(INW, D)
    (attn, lse), ((w_ffn_in_blocks,), (w_out_blocks,)) = attn_fwd(
        proj, tabs, qw2, kw2, sides=(ag_blocks_relay(w["w_ffn_in"], BF16), ag_blocks_relay(w["w_out"], BF16)))
    w_ffn_in_t = w_ffn_in_blocks.reshape(2 * FF, D)
    w_out_f = w_out_blocks.reshape(D, D)
    (cpre, u3), ((w_o_f,), (w_pw_f,)) = conv_fwd(
        proj, conv_w_f, w["conv_b"], w["conv_ln_w"], w["conv_ln_b"],
        sides=(ag_cols(w["w_o_attn"]), ag_cols(w["w_pw_conv"])))
    x1, z, ya, yb = mix_out(x2, proj, b_gate_f, attn, u3, w_o_f, w_pw_f, w_out_f)
    (h2, gu, f), ((w_ffn_out_blocks,),) = ffn_in(x1, w["norm2_w"], w_ffn_in_t, sides=(ag_blocks_relay(w["w_ffn_out"], BF16),))
    w_ffn_out_f = w_ffn_out_blocks.reshape(FF, D)
    dy, dyb, sq = ffn_out_loss(x1, f, w_ffn_out_f, target)

    g = {}
    g_ffn_out = mm_tn("gw_ffn_out", f, dyb, FF // 2, D).reshape(NDEV, FF // NDEV, D)
    (d_gu, d_x1, d_x1b, g["norm2_w"]), ((ra_ffn_out,),) = ffn_bwd(
        dy, dyb, gu, x1, w["norm2_w"], w_ffn_in_t, w_ffn_out_f, sides=(rs_to_sibling([("rows", g_ffn_out)]),))
    pb_ffn_out, own_ffn_out = chip_sum("chip_sum_w_ffn_out", g_ffn_out, ra_ffn_out, c_idx, chip_idx)
    g_ffn_in, ((rb_ffn_out,),) = mm_tn("gw_ffn_in", d_gu, h2, FF // 2, D,
                                       sides=(rs_to_chips([("rows", pb_ffn_out)]),))
    g_ffn_in = g_ffn_in.reshape(NDEV, 2 * FF // NDEV, D)
    g_out = mm_tn("gw_out", z, d_x1b, D // 2, D).reshape(NDEV, D // NDEV, D)
    (d_ya, d_yb, d_gl, d_attn, d_u3, g["b_gate"]), ((ra_ffn_in,),) = out_bwd(
        d_x1b, proj, b_gate_f, ya, yb, w_o_f, w_pw_f, w_out_f, sides=(rs_to_sibling([("rows", g_ffn_in)]),))
    pb_ffn_in, own_ffn_in = chip_sum("chip_sum_w_ffn_in", g_ffn_in, ra_ffn_in, c_idx, chip_idx)
    g_w_o = mm_tn("gw_o_attn", attn, d_ya, CC, D)
    g_w_pw = mm_tn("gw_pw_conv", u3, d_yb, CC, D)
    (d_conv, g["conv_w"], g["conv_b"], g["conv_ln_w"], g["conv_ln_b"]), ((ra_out, ra_w_o, ra_w_pw),) = conv_bwd(
        proj, cpre, d_u3, conv_w_f, conv_w_f[::-1], w["conv_ln_w"], w["conv_ln_b"],
        sides=(rs_to_sibling([("rows", g_out), ("cols", g_w_o), ("cols", g_w_pw)]),))
    pb_out, own_out = chip_sum("chip_sum_w_out", g_out, ra_out, c_idx, chip_idx)
    pb_w_o, own_w_o = chip_sum_cols("chip_sum_w_o_attn", g_w_o, ra_w_o, c_idx, chip_idx)
    pb_w_pw, own_w_pw = chip_sum_cols("chip_sum_w_pw_conv", g_w_pw, ra_w_pw, c_idx, chip_idx)
    (d_q, d_k, d_v, gqw, gkw), ((rb_ffn_in, rb_out, rb_w_o, rb_w_pw),) = attn_bwd(
        proj, tabs, qw2, kw2, d_attn, attn, lse,
        sides=(rs_to_chips([("rows", pb_ffn_in), ("rows", pb_out), ("cols", pb_w_o), ("cols", pb_w_pw)]),))
    g["q_norm_w"] = gqw[0:1, 0:HD] + gqw[0:1, HD:LANES]
    g["k_norm_w"] = gkw[0:1, 0:HD] + gkw[0:1, HD:LANES]
    mine_w_o, theirs_w_o = col_final("col_final_w_o_attn", own_w_o, rb_w_o, c_idx)
    mine_w_pw, theirs_w_pw = col_final("col_final_w_pw_conv", own_w_pw, rb_w_pw, c_idx)
    d_segs = (d_q, d_k, d_v, d_conv, d_gl)
    parts, to_sibling, to_chips, owns, from_chips = [], None, None, [], []
    for k, hw in enumerate(GW_IN_SPLIT):
        sides = [rs_swap_halves([theirs_w_o, theirs_w_pw])] if k == 0 else []
        sides += [s for s in (to_chips, to_sibling) if s is not None]
        part, outs = gw_in_t("gw_in_%d" % k, h_t, d_segs, sum(GW_IN_SPLIT[:k]), hw, sides=tuple(sides))
        if k == 0:
            (rc_w_o, rc_w_pw), outs = outs[0], outs[1:]
        outs = list(outs)
        if to_chips is not None:
            from_chips.append(outs.pop(0)[0])
        if to_sibling is not None:
            pb, own = chip_sum("chip_sum_w_in_%d" % (k - 1), parts[-1], outs.pop(0)[0], c_idx, chip_idx)
            owns.append(own)
            to_chips = rs_to_chips([("rows", pb)])
        else:
            to_chips = None
        parts.append(part.reshape(NDEV, INW // NDEV, hw))
        to_sibling = rs_to_sibling([("rows", parts[-1])])
    (grad_x, g["norm1_w"]), ((rb_prev,), (ra_last,)) = in_bwd(
        d_q, d_k, d_v, d_conv, d_gl, w_in_t, x2, d_x1, w["norm1_w"], sides=(to_chips, to_sibling))
    from_chips.append(rb_prev)
    pb, own = chip_sum("chip_sum_w_in_%d" % (len(GW_IN_SPLIT) - 1), parts[-1], ra_last, c_idx, chip_idx)
    owns.append(own)
    small, loss, ((rb_last,),) = small_sync_adam(g, w, m, v, sq, sides=(rs_to_chips([("rows", pb)]),))
    from_chips.append(rb_last)

    res = {
        "w_in": shard_adam("adam_w_in", owns, from_chips, w["w_in"], m["w_in"], v["w_in"]),
        "w_ffn_in": shard_adam("adam_w_ffn_in", [own_ffn_in], [rb_ffn_in], w["w_ffn_in"], m["w_ffn_in"], v["w_ffn_in"]),
        "w_o_attn": adam_cols("adam_w_o_attn", mine_w_o, rc_w_o, w["w_o_attn"], m["w_o_attn"], v["w_o_attn"], c_idx),
        "w_pw_conv": adam_cols("adam_w_pw_conv", mine_w_pw, rc_w_pw, w["w_pw_conv"], m["w_pw_conv"], v["w_pw_conv"], c_idx),
        "w_out": shard_adam("adam_w_out", [own_out], [rb_out], w["w_out"], m["w_out"], v["w_out"]),
        "w_ffn_out": shard_adam("adam_w_ffn_out", [own_ffn_out], [rb_ffn_out],
                                w["w_ffn_out"], m["w_ffn_out"], v["w_ffn_out"]),
    }
    res = {k: tuple(a.T if k in TRANSPOSED else a for a in r) for k, r in res.items()}
    res.update(small)

    def shaped(name, a):
        return a.reshape((1,) + a.shape) if name in MATS or name in ("b_gate", "conv_w") else a

    outs = [loss, grad_x.reshape(1, S, D)]
    for i in range(4):
        outs += [shaped(k, res[k][i]) for k in WEIGHTS]
    return tuple(outs)
```

```python
import functools
from typing import Callable, NamedTuple, Optional

import numpy as np
import jax
import jax.numpy as jnp
from jax import lax
from jax.experimental import pallas as pl
from jax.experimental.pallas import tpu as pltpu

F32 = jnp.float32
BF16 = jnp.bfloat16

S = 2048
D = 1024
HD = 64
QKV = 1536
CC = 512
KW = 31
FF = 2816
INW = 7680
OFF_Q, OFF_K, OFF_V, OFF_CA, OFF_CB, OFF_GA, OFF_GB = 0, 1536, 3072, 4608, 5120, 5632, 6656
DILATIONS = (1, 4, 16)
HALF_SPAN = 64
EPS = 1e-6
NEG_INF = -1e30
ROPE_THETA = 500000.0
ROT_DIM = 16

ADAM_LR = 0.001
ADAM_B1 = 0.9
ADAM_B2 = 0.999
ADAM_EPS = 1e-08
ADAM_WD = 0.01
ADAM_STEP = 10

NDEV = 8
LANES = 128
TM = 256
TQ = 128
VMEM_LIMIT = 56 * 1024 * 1024
MESH = pl.DeviceIdType.MESH


def _cp(**kw):
    return pltpu.CompilerParams(vmem_limit_bytes=VMEM_LIMIT, **kw)


def _row(width, col=0, tm=TM):
    return pl.BlockSpec((tm, width), lambda i: (i, col))


PLANE = 512


def _planes(width, tm=TM):
    return pl.BlockSpec((width // PLANE, tm, PLANE), lambda i: (0, i, 0))


def _res(shape):
    nd = len(shape)
    return pl.BlockSpec(shape, lambda *_: (0,) * nd, pipeline_mode=pl.Buffered(1))


def _dot(a, b):
    return jnp.dot(a, b, preferred_element_type=F32)


def _dot_nt(a, b):
    return lax.dot_general(a, b, (((1,), (1,)), ((), ())), preferred_element_type=F32)


def _dot_tn(a, b):
    return lax.dot_general(a, b, (((0,), (0,)), ((), ())), preferred_element_type=F32)


def _sigmoid(x):
    return jax.nn.sigmoid(x)


def _dsilu(x, sg):
    return sg * (1.0 + x * (1.0 - sg))


ANY = pl.BlockSpec(memory_space=pl.ANY)
VMEM = pl.BlockSpec(memory_space=pltpu.VMEM)


class Side(NamedTuple):
    args: tuple
    in_specs: tuple
    out_shape: tuple
    scratch: tuple
    start: Callable
    finish: Callable
    mid: Optional[Callable] = None
    peers: str = ""


BARRIER_IDS = {"s": 0, "dxy": 1, "dsxy": 2, "sxy": 3}


def _peer_barrier(peers):
    x, y, c = lax.axis_index("x"), lax.axis_index("y"), lax.axis_index("c")
    where = {"s": (x, y, 1 - c), "x": (1 - x, y, c), "y": (x, 1 - y, c), "d": (1 - x, 1 - y, c)}
    barrier = pltpu.get_barrier_semaphore()
    for p in peers:
        pl.semaphore_signal(barrier, inc=1, device_id=where[p], device_id_type=MESH)
    pl.semaphore_wait(barrier, len(peers))


def _call(body, sides=(), *, name, grid, in_specs, out_specs, out_shape, scratch_shapes=(), args, own_comm=False):
    ni, no, ns = len(in_specs), len(out_specs), len(scratch_shapes)
    cnt = [(len(s.args), len(s.out_shape), len(s.scratch)) for s in sides]
    peers = "".join(sorted(set("".join(s.peers for s in sides))))
    if own_comm or not sides or any(not s.peers for s in sides):
        peers = ""

    def take(refs, pos, n):
        return refs[pos:pos + n], pos + n

    def full(*refs):
        m_in, pos = take(refs, 0, ni)
        s_in = []
        for a, _, _ in cnt:
            r, pos = take(refs, pos, a)
            s_in.append(r)
        m_out, pos = take(refs, pos, no)
        s_out = []
        for _, o, _ in cnt:
            r, pos = take(refs, pos, o)
            s_out.append(r)
        m_scr, pos = take(refs, pos, ns)
        s_scr = []
        for _, _, c in cnt:
            r, pos = take(refs, pos, c)
            s_scr.append(r)
        if sides:
            first = functools.reduce(jnp.logical_and, [pl.program_id(d) == 0 for d in range(len(grid))])
            last = functools.reduce(jnp.logical_and, [pl.program_id(d) == g - 1 for d, g in enumerate(grid)])

            @pl.when(first)
            def _():
                if peers:
                    _peer_barrier(peers)
                for s, a, o, c in zip(sides, s_in, s_out, s_scr):
                    s.start(a, o, c)

            steps = int(np.prod(grid))
            mid_step = (2 * steps) // 3
            if steps > 1 and any(s.mid is not None for s in sides):
                step = functools.reduce(lambda acc, d: acc * grid[d] + pl.program_id(d), range(len(grid)), 0)

                @pl.when(step == mid_step)
                def _():
                    for s, a, o, c in zip(sides, s_in, s_out, s_scr):
                        if s.mid is not None:
                            s.mid(a, o, c)

        body(*m_in, *m_out, *m_scr)
        if sides:
            @pl.when(last)
            def _():
                for s, a, o, c in zip(sides, s_in, s_out, s_scr):
                    if s.mid is not None and steps == 1:
                        s.mid(a, o, c)
                    s.finish(a, o, c)

    res = pl.pallas_call(
        full, name=name, grid=grid,
        in_specs=list(in_specs) + [sp for s in sides for sp in s.in_specs],
        out_specs=list(out_specs) + [ANY for s in sides for _ in s.out_shape],
        out_shape=list(out_shape) + [o for s in sides for o in s.out_shape],
        scratch_shapes=list(scratch_shapes) + [c for s in sides for c in s.scratch],
        compiler_params=_cp(dimension_semantics=("arbitrary",) * len(grid),
                            **({"collective_id": BARRIER_IDS[peers]} if peers else {})),
    )(*args, *[a for s in sides for a in s.args])
    res = list(res)
    if not sides:
        return res
    outs, pos = take(res, 0, no)
    side_outs = []
    for _, o, _ in cnt:
        r, pos = take(res, pos, o)
        side_outs.append(r)
    return outs, side_outs


def _inv_freq_lanes():
    inv = np.float32(ROPE_THETA) ** (-np.arange(0, ROT_DIM, 2, dtype=np.float32) / np.float32(ROT_DIM))
    lane = np.arange(LANES) % HD
    out = np.where(lane < ROT_DIM, inv[lane % (ROT_DIM // 2)], 0.0).astype(np.float32)
    return jnp.asarray(out.reshape(1, LANES))


def _rope_tables(pos, inv_freq):
    ang = pos.astype(F32) * inv_freq
    lane = lax.broadcasted_iota(jnp.int32, ang.shape, 1) % HD
    cs = jnp.cos(ang)
    sn = jnp.sin(ang)
    return (jnp.where(lane < ROT_DIM, cs, 1.0), jnp.where(lane < ROT_DIM // 2, -sn, 0.0),
            jnp.where(lane < ROT_DIM // 2, 0.0, jnp.where(lane < ROT_DIM, sn, 0.0)))


def _rope(v, c, s1, s2):
    return v * c + pltpu.roll(v, LANES - 8, axis=1) * s1 + pltpu.roll(v, 8, axis=1) * s2


def _rope_t(d, c, s1, s2):
    return d * c - pltpu.roll(d, LANES - 8, axis=1) * s1 - pltpu.roll(d, 8, axis=1) * s2


def _head_mat():
    r = lax.broadcasted_iota(jnp.int32, (LANES, LANES), 0) // HD
    c = lax.broadcasted_iota(jnp.int32, (LANES, LANES), 1) // HD
    return jnp.where(r == c, 1.0 / HD, 0.0).astype(BF16)


def _head_mean(t, e):
    hi = t.astype(BF16)
    rest = (t - hi.astype(F32)).astype(BF16)
    return _dot(hi, e) + _dot(rest, e)


def in_proj_gather(x, norm_w, shard_t, chip_order, pos_col):
    R = INW // NDEV
    half, nt = R // 2, S // TM

    def body(ord_ref, x_ref, nw_ref, sh_ref, pos_ref, f_ref, ht_ref, p_ref, wfull_ref, c_ref, s1_ref, s2_ref,
             wt, hs, send, recv, loc):
        kk, i = pl.program_id(0), pl.program_id(1)
        x, y, c, _ = _place()
        me, flip = 4 * x + 2 * y + c, 1 - 2 * c
        here, sib, xn, yn = (x, y, c), (x, y, 1 - c), (1 - x, y, c), (x, 1 - y, c)
        b_xn, b_yn, b_dg = 4 * (1 - x) + 2 * y + c, 4 * x + 2 * (1 - y) + c, 4 * (1 - x) + 2 * (1 - y) + c

        def cp(k, block, to, rows=None):
            dst = wt.at[block] if rows is None else wt.at[block, pl.ds(rows * half, half), :]
            return _remote(dst, dst, send, recv, k, to)

        def sends():
            return [cp(0, me, sib), cp(1, me, xn), cp(2, me, yn), cp(3, b_xn, sib), cp(4, b_yn, sib),
                    cp(5, b_xn, yn, rows=0), cp(6, b_yn, xn, rows=1), cp(7, b_dg, sib, rows=0), cp(8, b_dg, sib, rows=1)]

        def keep(j, blk0):
            pair = pl.ds(pl.multiple_of(blk0, 2), 2)
            return pltpu.make_async_copy(wt.at[pair], wfull_ref.at[pair], loc.at[j])

        @pl.when((kk == 0) & (i == 0))
        def _():
            _peer_barrier("sxy")
            _cast_rows(wt.at[me], sh_ref)
            for s_ in sends()[0:3]:
                s_.start()

            def tables(j, _):
                chunk = pl.ds(pl.multiple_of(j * TM, TM), TM)
                c_ref[chunk, :], s1_ref[chunk, :], s2_ref[chunk, :] = _rope_tables(pos_ref[chunk, :], f_ref[...])
                return 0

            lax.fori_loop(0, nt, tables, 0)
            cp(0, me + flip, here).wait_recv()
            keep(0, me - c).start()

        @pl.when((kk == 1) & (i == 0))
        def _():
            cp(1, b_xn, here).wait_recv()
            sends()[5].start()
            sends()[3].start()
            cp(2, b_yn, here).wait_recv()
            sends()[6].start()
            sends()[4].start()
            cp(3, b_xn + flip, here).wait_recv()
            keep(1, b_xn - c).start()

        @pl.when((kk == 2) & (i == 0))
        def _():
            cp(4, b_yn + flip, here).wait_recv()
            keep(2, b_yn - c).start()

        @pl.when((kk == 3) & (i == 0))
        def _():
            cp(5, b_dg, here, rows=0).wait_recv()
            sends()[7].start()
            cp(6, b_dg, here, rows=1).wait_recv()
            sends()[8].start()
            cp(7, b_dg + flip, here, rows=0).wait_recv()
            cp(8, b_dg + flip, here, rows=1).wait_recv()
            keep(3, b_dg - c).start()

        rows = pl.ds(pl.multiple_of(i * TM, TM), TM)

        @pl.when(kk == 0)
        def _():
            xv = x_ref[...]
            r = lax.rsqrt(jnp.mean(xv * xv, axis=-1, keepdims=True) + EPS)
            hf = xv * r * nw_ref[...]
            ht_ref[...] = hf.T.astype(BF16)
            hs[rows, :] = hf.astype(BF16)

        h = hs[rows, :]
        chip = ord_ref[kk]
        for cc in range(2):
            p_ref[:, cc * R:(cc + 1) * R] = _dot_nt(h, wt[2 * chip + cc])

        @pl.when((kk == 3) & (i == nt - 1))
        def _():
            for s_ in sends():
                s_.wait_send()
            for j, blk in enumerate((me, b_xn, b_yn, b_dg)):
                keep(j, blk - c).wait()

    def first_pass(kk, i):
        return jnp.where(kk == 0, i, nt - 1)

    grid_spec = pltpu.PrefetchScalarGridSpec(
        num_scalar_prefetch=1, grid=(4, nt),
        in_specs=[pl.BlockSpec((TM, D), lambda kk, i, o: (first_pass(kk, i), 0)),
                  pl.BlockSpec((1, D), lambda kk, i, o: (0, 0)), VMEM, VMEM,
                  pl.BlockSpec((1, LANES), lambda kk, i, o: (0, 0))],
        out_specs=[pl.BlockSpec((D, TM), lambda kk, i, o: (0, first_pass(kk, i))),
                   pl.BlockSpec((TM, 2 * R), lambda kk, i, o: (i, o[kk])), ANY]
        + [pl.BlockSpec((S, LANES), lambda kk, i, o: (0, 0))] * 3,
        scratch_shapes=[pltpu.VMEM((NDEV, R, D), BF16), pltpu.VMEM((S, D), BF16), _sems(9), _sems(9), _sems(4)])
    res = pl.pallas_call(
        body, name="in_proj_gather", grid_spec=grid_spec,
        out_shape=[jax.ShapeDtypeStruct((D, S), BF16), jax.ShapeDtypeStruct((S, INW), F32),
                   jax.ShapeDtypeStruct((NDEV, R, D), BF16)] + [jax.ShapeDtypeStruct((S, LANES), F32)] * 3,
        compiler_params=_cp(dimension_semantics=("arbitrary", "arbitrary"), collective_id=BARRIER_IDS["sxy"]),
    )(chip_order, x, norm_w, shard_t, pos_col, _inv_freq_lanes())
    return res[0], res[1], res[2], tuple(res[3:])


def _qk_specs():
    nb = QKV // LANES
    return [pl.BlockSpec((S, LANES), functools.partial(lambda hp, g, o: (0, o + g * 4 + hp), o=o))
            for o in (OFF_Q // LANES, OFF_K // LANES, OFF_V // LANES)]


def _tab_specs():
    return [pl.BlockSpec((S, LANES), lambda hp, g: (0, 0), pipeline_mode=pl.Buffered(1))] * 3


def _vec_spec():
    return pl.BlockSpec((1, LANES), lambda hp, g: (0, 0))


def _sub_rows(r, d, start, n):
    if d == 1:
        return pl.ds(start, n)
    return pl.ds(r + d * start, n, stride=d)


def _band_window(i, L):
    W = min(TQ + 2 * HALF_SPAN, L)
    q0 = pl.multiple_of(i * TQ, TQ)
    k0 = pl.multiple_of(jnp.clip(q0 - HALF_SPAN, 0, L - W), HALF_SPAN)
    qpos = q0 + (lax.broadcasted_iota(jnp.int32, (2 * TQ, W), 0) & (TQ - 1))
    kpos = k0 + lax.broadcasted_iota(jnp.int32, (2 * TQ, W), 1)
    valid = jnp.abs(qpos - kpos) <= HALF_SPAN
    return W, q0, k0, valid


def _stack_heads(t, lo):
    z = jnp.zeros_like(t)
    return jnp.concatenate([jnp.where(lo, t, z), jnp.where(lo, z, t)], axis=0)


def _unstack_heads(t2, lo):
    return jnp.where(lo, t2[0:TQ], t2[TQ:2 * TQ])


CHAINS = 8


def _interleave(d):
    ru = min(d, CHAINS)
    return ru, min(CHAINS // ru, S // d // TQ)


def _for_blocks(n, fn):
    if n == 1:
        fn(0)
    else:
        def it(j, _):
            fn(j)
            return 0
        lax.fori_loop(0, n, it, 0)


def attn_fwd(proj, tabs, qw2, kw2, sides=()):
    CH = 256

    def body(q_ref, k_ref, v_ref, c_ref, s1_ref, s2_ref, qw_ref, kw_ref, at_ref, ls_ref,
             qs, ks, vs, osub, lsub, onat, lnat, qn, kn):
        g = pl.program_id(1)
        lo = lax.broadcasted_iota(jnp.int32, (1, LANES), 1) < HD
        e = _head_mat()

        def prep(i, _):
            rows = pl.ds(pl.multiple_of(i * CH, CH), CH)
            c, s1, s2 = c_ref[rows, :], s1_ref[rows, :], s2_ref[rows, :]
            for t_ref, w_ref, out, scale in ((q_ref, qw_ref, qn, HD ** -0.5), (k_ref, kw_ref, kn, 1.0)):
                t = t_ref[rows, :]
                r = lax.rsqrt(_head_mean(t * t, e) + EPS)
                out[rows, :] = _rope(t * r * w_ref[...], c, s1, s2) * scale
            return 0

        lax.fori_loop(0, S // CH, prep, 0, unroll=4)

        def group(gi, d):
            L = S // d

            ru, nb = _interleave(d)

            def stage(r, off):
                for c0 in range(0, L, CH):
                    n = min(CH, L)
                    rows = _sub_rows(r, d, c0, n)
                    dst = pl.ds(off + c0, n)
                    qs[dst, :] = qn[rows, :].astype(BF16)
                    ks[dst, :] = kn[rows, :].astype(BF16)
                    vs[dst, :] = v_ref[rows, :].astype(BF16)

            def one(off, i):
                W, q0, k0, valid = _band_window(i, L)
                q2 = _stack_heads(qs[pl.ds(off + q0, TQ), :], lo)
                sc = jnp.where(valid, _dot_nt(q2, ks[pl.ds(off + k0, W), :]), NEG_INF)
                m = jnp.max(sc, axis=-1, keepdims=True)
                p = jnp.exp(sc - m)
                den = jnp.sum(p, axis=-1, keepdims=True)
                o2 = _dot(p.astype(BF16), vs[pl.ds(off + k0, W), :]) / den
                l2 = jnp.broadcast_to(m + jnp.log(den), (2 * TQ, LANES))
                osub[pl.ds(off + q0, TQ), :] = _unstack_heads(o2, lo)
                lsub[pl.ds(off + q0, TQ), :] = _unstack_heads(l2, lo)

            def unstage(r, off):
                for c0 in range(0, L, CH):
                    n = min(CH, L)
                    rows = _sub_rows(r, d, c0, n)
                    onat[gi, rows, :] = osub[pl.ds(off + c0, n), :]
                    lnat[gi, rows, :] = lsub[pl.ds(off + c0, n), :]

            def step(t, _):
                for u in range(ru):
                    stage(t * ru + u, u * L)
                _for_blocks(L // TQ // nb, lambda j: [one(u * L, j * nb + b) for u in range(ru) for b in range(nb)])
                for u in range(ru):
                    unstage(t * ru + u, u * L)
                return 0

            lax.fori_loop(0, d // ru, step, 0)

        for gi, d in enumerate(DILATIONS):
            pl.when(g == gi)(functools.partial(group, gi, d))

        @pl.when(g == len(DILATIONS) - 1)
        def _():
            def mix(i, _):
                rows = pl.ds(pl.multiple_of(i * CH, CH), CH)
                l0, l1, l2 = lnat[0, rows, :], lnat[1, rows, :], lnat[2, rows, :]
                m = jnp.maximum(jnp.maximum(l0, l1), l2)
                e0, e1, e2 = jnp.exp(l0 - m), jnp.exp(l1 - m), jnp.exp(l2 - m)
                den = e0 + e1 + e2
                a = (e0 * onat[0, rows, :] + e1 * onat[1, rows, :] + e2 * onat[2, rows, :]) / den
                at_ref[rows, :] = a.astype(BF16)
                ls_ref[rows, :] = m + jnp.log(den)
                return 0

            lax.fori_loop(0, S // CH, mix, 0)

    out_spec = pl.BlockSpec((S, LANES), lambda hp, g: (0, hp))
    return _call(
        body, sides, name="attn_fwd", grid=(4, 3),
        in_specs=_qk_specs() + _tab_specs() + [_vec_spec(), _vec_spec()],
        out_specs=[out_spec, out_spec],
        out_shape=[jax.ShapeDtypeStruct((S, CC), BF16), jax.ShapeDtypeStruct((S, CC), F32)],
        scratch_shapes=[pltpu.VMEM((S, LANES), BF16)] * 3 + [pltpu.VMEM((S, LANES), F32)] * 2
        + [pltpu.VMEM((3, S, LANES), F32)] * 2 + [pltpu.VMEM((S, LANES), F32)] * 2,
        args=(proj, proj, proj, *tabs, qw2, kw2))


def attn_bwd(proj, tabs, qw2, kw2, d_attn, attn, lse, sides=()):
    CH = 256

    def body(q_ref, k_ref, v_ref, c_ref, s1_ref, s2_ref, qw_ref, kw_ref, do_ref, at_ref, ls_ref,
             dq_ref, dk_ref, dv_ref, gqw_ref, gkw_ref,
             qs, ks, vs, dos, dsub, lsub, dqs, dks, dvs, dnat, qx, kx, dvn, tnq, tnk, rrq, rrk):
        hp, g = pl.program_id(0), pl.program_id(1)
        lo = lax.broadcasted_iota(jnp.int32, (1, LANES), 1) < HD
        e = _head_mat()
        both = ((q_ref, qw_ref, qx, tnq, rrq, HD ** -0.5), (k_ref, kw_ref, kx, tnk, rrk, 1.0))

        @pl.when((hp == 0) & (g == 0))
        def _():
            gqw_ref[...] = jnp.zeros_like(gqw_ref)
            gkw_ref[...] = jnp.zeros_like(gkw_ref)

        def prep(i, _):
            rows = pl.ds(pl.multiple_of(i * CH, CH), CH)
            dnat[rows, :] = _head_mean(do_ref[rows, :] * at_ref[rows, :].astype(F32), e) * float(HD)
            c, s1, s2 = c_ref[rows, :], s1_ref[rows, :], s2_ref[rows, :]
            for t_ref, w_ref, x, tn_s, rr_s, scale in both:
                t = t_ref[rows, :]
                rr = lax.rsqrt(_head_mean(t * t, e) + EPS)
                tn = t * rr
                rr_s[rows, :] = rr
                tn_s[rows, :] = tn
                x[rows, :] = _rope(tn * w_ref[...], c, s1, s2) * scale
            return 0

        lax.fori_loop(0, S // CH, prep, 0, unroll=4)

        def group(d):
            L = S // d

            ru, nb = _interleave(d)

            def stage(r, off):
                for c0 in range(0, L, CH):
                    n = min(CH, L)
                    rows = _sub_rows(r, d, c0, n)
                    dst = pl.ds(off + c0, n)
                    qs[dst, :] = qx[rows, :].astype(BF16)
                    ks[dst, :] = kx[rows, :].astype(BF16)
                    vs[dst, :] = v_ref[rows, :].astype(BF16)
                    dos[dst, :] = do_ref[rows, :].astype(BF16)
                    dsub[dst, :] = dnat[rows, :]
                    lsub[dst, :] = ls_ref[rows, :]
                    dks[dst, :] = jnp.zeros((n, LANES), F32)
                    dvs[dst, :] = jnp.zeros((n, LANES), F32)

            def one(off, i):
                W, q0, k0, valid = _band_window(i, L)
                qrows, krows = pl.ds(off + q0, TQ), pl.ds(off + k0, W)
                q2 = _stack_heads(qs[qrows, :], lo)
                do2 = _stack_heads(dos[qrows, :], lo)
                kk, vv = ks[krows, :], vs[krows, :]
                lse_b, dd_b = lsub[qrows, :], dsub[qrows, :]
                lse2 = jnp.concatenate([lse_b[:, 0:1], lse_b[:, HD:HD + 1]], axis=0)
                dd2 = jnp.concatenate([dd_b[:, 0:1], dd_b[:, HD:HD + 1]], axis=0)
                sc = jnp.where(valid, _dot_nt(q2, kk), NEG_INF)
                p = jnp.exp(sc - lse2)
                ds = (p * (_dot_nt(do2, vv) - dd2)).astype(BF16)
                dqs[qrows, :] = _unstack_heads(_dot(ds, kk), lo)
                dks[krows, :] = dks[krows, :] + _dot_tn(ds, q2)
                dvs[krows, :] = dvs[krows, :] + _dot_tn(p.astype(BF16), do2)

            def unstage(r, off):
                for c0 in range(0, L, CH):
                    n = min(CH, L)
                    rows = _sub_rows(r, d, c0, n)
                    src = pl.ds(off + c0, n)
                    qx[rows, :] = dqs[src, :]
                    kx[rows, :] = dks[src, :]
                    dvn[rows, :] = dvs[src, :]

            def step(t, _):
                for u in range(ru):
                    stage(t * ru + u, u * L)
                _for_blocks(L // TQ // nb, lambda j: [one(u * L, j * nb + b) for u in range(ru) for b in range(nb)])
                for u in range(ru):
                    unstage(t * ru + u, u * L)
                return 0

            lax.fori_loop(0, d // ru, step, 0)

        for gi, d in enumerate(DILATIONS):
            pl.when(g == gi)(functools.partial(group, d))

        def emit(i, _):
            rows = pl.ds(pl.multiple_of(i * CH, CH), CH)
            c, s1, s2 = c_ref[rows, :], s1_ref[rows, :], s2_ref[rows, :]
            for (_, w_ref, x, tn_s, rr_s, scale), out, gw_ref in zip(both, (dq_ref, dk_ref), (gqw_ref, gkw_ref)):
                tn = tn_s[rows, :]
                dy = _rope_t(x[rows, :] * scale, c, s1, s2)
                gw_ref[0:1, :] = gw_ref[0:1, :] + jnp.sum(dy * tn, axis=0, keepdims=True)
                dtn = dy * w_ref[...]
                out[rows, :] = (rr_s[rows, :] * (dtn - tn * _head_mean(dtn * tn, e))).astype(BF16)
            dv_ref[rows, :] = dvn[rows, :].astype(BF16)
            return 0

        lax.fori_loop(0, S // CH, emit, 0, unroll=4)

    nat_spec = pl.BlockSpec((S, LANES), lambda hp, g: (0, hp))
    out_spec = pl.BlockSpec((None, S, LANES), lambda hp, g: (g, 0, hp))
    acc_spec = pl.BlockSpec((8, LANES), lambda hp, g: (0, 0))
    return _call(
        body, sides, name="attn_bwd", grid=(4, 3),
        in_specs=_qk_specs() + _tab_specs() + [_vec_spec(), _vec_spec(), nat_spec, nat_spec, nat_spec],
        out_specs=[out_spec] * 3 + [acc_spec] * 2,
        out_shape=[jax.ShapeDtypeStruct((QKV // PLANE, S, PLANE), BF16)] * 3 + [jax.ShapeDtypeStruct((8, LANES), F32)] * 2,
        scratch_shapes=[pltpu.VMEM((S, LANES), BF16)] * 4 + [pltpu.VMEM((S, LANES), F32)] * 13,
        args=(proj, proj, proj, *tabs, qw2, kw2, d_attn, attn, lse))


PADR = 16
CT = 128


def _conv_specs():
    return [pl.BlockSpec((S, CC), lambda i: (0, OFF_CA // CC)), pl.BlockSpec((S, CC), lambda i: (0, OFF_CB // CC))]


NCB = CC // LANES


def _pad_zero(pad):
    for cb in range(NCB):
        pad[cb, 0:PADR, :] = jnp.zeros((PADR, LANES), F32)
        pad[cb, PADR + S:PADR + S + PADR, :] = jnp.zeros((PADR, LANES), F32)


def _pad_store(pad, row0, n, val):
    for cb in range(NCB):
        pad[cb, pl.ds(pl.multiple_of(row0 + PADR, 8), n), :] = val[:, cb * LANES:(cb + 1) * LANES]


def _taps(pad_ref, cb, s0, weights):
    acc = jnp.zeros((CT, LANES), F32)
    for k in range(KW):
        acc = acc + weights[k] * pad_ref[cb, pl.ds(s0 + k + 1, CT), :]
    return acc


def conv_fwd(proj, conv_w, conv_b, ln_w, ln_b, sides=()):
    def body(a_ref, b_ref, w_ref, cb_ref, lw_ref, lb_ref, c_ref, u3_ref, upad):
        _pad_zero(upad)

        def glu(i, _):
            rows = pl.ds(pl.multiple_of(i * TM, TM), TM)
            _pad_store(upad, i * TM, TM, a_ref[rows, :] * _sigmoid(b_ref[rows, :]))
            return 0

        lax.fori_loop(0, S // TM, glu, 0)

        def chunk(i, _):
            s0 = pl.multiple_of(i * CT, CT)
            for cb in range(CC // LANES):
                cols = slice(cb * LANES, (cb + 1) * LANES)
                w = [w_ref[k:k + 1, cols] for k in range(KW)]
                c_ref[pl.ds(s0, CT), cols] = _taps(upad, cb, s0, w) + cb_ref[:, cols]
            cv = c_ref[pl.ds(s0, CT), :]
            mu = jnp.mean(cv, axis=-1, keepdims=True)
            xc = cv - mu
            rstd = lax.rsqrt(jnp.mean(xc * xc, axis=-1, keepdims=True) + EPS)
            yl = xc * rstd * lw_ref[...] + lb_ref[...]
            u3_ref[pl.ds(s0, CT), :] = (yl * _sigmoid(yl)).astype(BF16)
            return 0

        lax.fori_loop(0, S // CT, chunk, 0)

    vec = pl.BlockSpec((1, CC), lambda i: (0, 0))
    full = pl.BlockSpec((S, CC), lambda i: (0, 0))
    return _call(
        body, sides, name="conv_fwd", grid=(1,),
        in_specs=_conv_specs() + [pl.BlockSpec((KW, CC), lambda i: (0, 0)), vec, vec, vec],
        out_specs=[full, full],
        out_shape=[jax.ShapeDtypeStruct((S, CC), F32), jax.ShapeDtypeStruct((S, CC), BF16)],
        scratch_shapes=[pltpu.VMEM((NCB, S + 2 * PADR, LANES), F32)],
        args=(proj, proj, conv_w, conv_b, ln_w, ln_b))


def conv_bwd(proj, cpre, d_u3, conv_w, conv_w_rev, ln_w, ln_b, sides=()):
    def body(a_ref, b_ref, c_ref, du3_ref, w_ref, wr_ref, lw_ref, lb_ref,
             dc_ref, gw_ref, gcb_ref, glw_ref, glb_ref, upad, dpad):
        _pad_zero(upad)
        _pad_zero(dpad)
        gw_ref[...] = jnp.zeros_like(gw_ref)

        def ln_bwd(i, carry):
            gcb, glw, glb = carry
            rows = pl.ds(pl.multiple_of(i * TM, TM), TM)
            _pad_store(upad, i * TM, TM, a_ref[rows, :] * _sigmoid(b_ref[rows, :]))
            cv = c_ref[rows, :]
            mu = jnp.mean(cv, axis=-1, keepdims=True)
            xc = cv - mu
            rstd = lax.rsqrt(jnp.mean(xc * xc, axis=-1, keepdims=True) + EPS)
            xh = xc * rstd
            yl = xh * lw_ref[...] + lb_ref[...]
            dyl = du3_ref[rows, :] * _dsilu(yl, _sigmoid(yl))
            dxh = dyl * lw_ref[...]
            dcv = rstd * (dxh - jnp.mean(dxh, axis=-1, keepdims=True)
                          - xh * jnp.mean(dxh * xh, axis=-1, keepdims=True))
            _pad_store(dpad, i * TM, TM, dcv)
            return (gcb + jnp.sum(dcv, axis=0, keepdims=True),
                    glw + jnp.sum(dyl * xh, axis=0, keepdims=True),
                    glb + jnp.sum(dyl, axis=0, keepdims=True))

        z = jnp.zeros((1, CC), F32)
        gcb, glw, glb = lax.fori_loop(0, S // TM, ln_bwd, (z, z, z))
        gcb_ref[...] = gcb
        glw_ref[...] = glw
        glb_ref[...] = glb

        def chunk(i, _):
            s0 = pl.multiple_of(i * CT, CT)
            for cb in range(CC // LANES):
                cols = slice(cb * LANES, (cb + 1) * LANES)
                wr = [wr_ref[k:k + 1, cols] for k in range(KW)]
                du = _taps(dpad, cb, s0, wr)
                dcv = dpad[cb, pl.ds(s0 + PADR, CT), :]
                for k in range(KW):
                    gw_ref[k:k + 1, cols] = gw_ref[k:k + 1, cols] + jnp.sum(
                        upad[cb, pl.ds(s0 + k + 1, CT), :] * dcv, axis=0, keepdims=True)
                av = a_ref[pl.ds(s0, CT), cols]
                sb = _sigmoid(b_ref[pl.ds(s0, CT), cols])
                dc_ref[0, pl.ds(s0, CT), cols] = (du * sb).astype(BF16)
                dc_ref[1, pl.ds(s0, CT), cols] = (du * av * sb * (1.0 - sb)).astype(BF16)
            return 0

        lax.fori_loop(0, S // CT, chunk, 0)

    vec = pl.BlockSpec((1, CC), lambda i: (0, 0))
    full = pl.BlockSpec((S, CC), lambda i: (0, 0))
    wsp = pl.BlockSpec((KW, CC), lambda i: (0, 0))
    return _call(
        body, sides, name="conv_bwd", grid=(1,),
        in_specs=_conv_specs() + [full, full, wsp, wsp, vec, vec],
        out_specs=[pl.BlockSpec((2, S, CC), lambda i: (0, 0, 0)), wsp, vec, vec, vec],
        out_shape=[jax.ShapeDtypeStruct((2, S, CC), BF16), jax.ShapeDtypeStruct((KW, CC), F32)]
        + [jax.ShapeDtypeStruct((1, CC), F32)] * 3,
        scratch_shapes=[pltpu.VMEM((NCB, S + 2 * PADR, LANES), F32)] * 2,
        args=(proj, proj, cpre, d_u3, conv_w, conv_w_rev, ln_w, ln_b))


def _gate_specs():
    return [_row(CC, col=OFF_GA // CC + j) for j in range(4)]


def _gates(g_refs, bg_ref):
    ga = _sigmoid(jnp.concatenate([g_refs[0][...], g_refs[1][...]], axis=1) + bg_ref[0:1, :])
    gb = _sigmoid(jnp.concatenate([g_refs[2][...], g_refs[3][...]], axis=1) + bg_ref[1:2, :])
    return ga, gb


def mix_out(x, proj, b_gate, attn, u3, w_o, w_pw, w_out):
    def body(x_ref, g0, g1, g2, g3, bg_ref, at_ref, u3_ref, wo_ref, wp_ref, wout_ref,
             x1_ref, z_ref, ya_ref, yb_ref):
        ga, gb = _gates((g0, g1, g2, g3), bg_ref)
        ya = _dot(at_ref[...], wo_ref[...])
        yb = _dot(u3_ref[...], wp_ref[...])
        z = (ga * ya + gb * yb).astype(BF16)
        ya_ref[...] = ya.astype(BF16)
        yb_ref[...] = yb.astype(BF16)
        z_ref[...] = z
        x1_ref[...] = x_ref[...] + _dot(z, wout_ref[...])

    return pl.pallas_call(
        body, name="mix_out", grid=(S // TM,),
        in_specs=[_row(D)] + _gate_specs() + [_res((2, D)), _row(CC), _row(CC),
                                              _res((CC, D)), _res((CC, D)), _res((D, D))],
        out_specs=[_row(D)] * 4,
        out_shape=[jax.ShapeDtypeStruct((S, D), F32)] + [jax.ShapeDtypeStruct((S, D), BF16)] * 3,
        compiler_params=_cp(dimension_semantics=("arbitrary",)),
    )(x, proj, proj, proj, proj, b_gate, attn, u3, w_o, w_pw, w_out)


def out_bwd(d_x1b, proj, b_gate, ya, yb, w_o, w_pw, w_out, sides=()):
    def body(dx_ref, g0, g1, g2, g3, bg_ref, ya_ref, yb_ref, wo_ref, wp_ref, wout_ref,
             dya_ref, dyb_ref, dgl_ref, dat_ref, du3_ref, gbg_ref):
        @pl.when(pl.program_id(0) == 0)
        def _():
            gbg_ref[...] = jnp.zeros_like(gbg_ref)

        ga, gb = _gates((g0, g1, g2, g3), bg_ref)
        dz = _dot_nt(dx_ref[...], wout_ref[...])
        dya = (dz * ga).astype(BF16)
        dyb = (dz * gb).astype(BF16)
        dgla = dz * ya_ref[...].astype(F32) * ga * (1.0 - ga)
        dglb = dz * yb_ref[...].astype(F32) * gb * (1.0 - gb)
        dya_ref[...] = dya
        dyb_ref[...] = dyb
        for j in range(2):
            dgl_ref[j] = dgla[:, j * PLANE:(j + 1) * PLANE].astype(BF16)
            dgl_ref[2 + j] = dglb[:, j * PLANE:(j + 1) * PLANE].astype(BF16)
        gbg_ref[0:1, :] = gbg_ref[0:1, :] + jnp.sum(dgla, axis=0, keepdims=True)
        gbg_ref[1:2, :] = gbg_ref[1:2, :] + jnp.sum(dglb, axis=0, keepdims=True)
        dat_ref[...] = _dot_nt(dya, wo_ref[...])
        du3_ref[...] = _dot_nt(dyb, wp_ref[...])

    return _call(
        body, sides, name="out_bwd", grid=(S // TM,),
        in_specs=[_row(D)] + _gate_specs() + [_res((2, D)), _row(D), _row(D),
                                              _res((CC, D)), _res((CC, D)), _res((D, D))],
        out_specs=[_row(D), _row(D), _planes(2 * D), _row(CC), _row(CC), pl.BlockSpec((2, D), lambda i: (0, 0))],
        out_shape=[jax.ShapeDtypeStruct((S, D), BF16)] * 2 + [jax.ShapeDtypeStruct((2 * D // PLANE, S, PLANE), BF16)]
        + [jax.ShapeDtypeStruct((S, CC), F32)] * 2 + [jax.ShapeDtypeStruct((2, D), F32)],
        args=(d_x1b, proj, proj, proj, proj, b_gate, ya, yb, w_o, w_pw, w_out))


def ffn_in(x1, norm_w, w_ffn_in, sides=()):
    half = FF // 2

    def body(x_ref, nw_ref, w_ref, h_ref, gu_ref, f_ref):
        xv = x_ref[...]
        r = lax.rsqrt(jnp.mean(xv * xv, axis=-1, keepdims=True) + EPS)
        h = (xv * r * nw_ref[...]).astype(BF16)
        h_ref[...] = h
        for j in range(2):
            gt = _dot_nt(h, w_ref[j * half:(j + 1) * half, :])
            up = _dot_nt(h, w_ref[FF + j * half:FF + (j + 1) * half, :])
            gu_ref[:, j * half:(j + 1) * half] = gt.astype(BF16)
            gu_ref[:, FF + j * half:FF + (j + 1) * half] = up.astype(BF16)
            f_ref[:, j * half:(j + 1) * half] = (gt * _sigmoid(gt) * up).astype(BF16)

    return _call(
        body, sides, name="ffn_in", grid=(S // TM,),
        in_specs=[_row(D), _res((1, D)), _res((2 * FF, D))],
        out_specs=[_row(D), _row(2 * FF), _row(FF)],
        out_shape=[jax.ShapeDtypeStruct((S, D), BF16), jax.ShapeDtypeStruct((S, 2 * FF), BF16),
                   jax.ShapeDtypeStruct((S, FF), BF16)],
        args=(x1, norm_w, w_ffn_in))


def ffn_out_loss(x1, f, w_ffn_out, target):
    def body(x_ref, f_ref, w_ref, t_ref, dy_ref, dyb_ref, sq_ref):
        @pl.when(pl.program_id(0) == 0)
        def _():
            sq_ref[...] = jnp.zeros_like(sq_ref)

        diff = x_ref[...] + _dot(f_ref[...], w_ref[...]) - t_ref[...]
        dy = diff * (1.0 / D)
        dy_ref[...] = dy
        dyb_ref[...] = dy.astype(BF16)
        sq_ref[...] = sq_ref[...] + jnp.sum((diff * diff).reshape(TM // 8, 8, D), axis=0)

    return pl.pallas_call(
        body, name="ffn_out_loss", grid=(S // TM,),
        in_specs=[_row(D), _row(FF), _res((FF, D)), _row(D)],
        out_specs=[_row(D), _row(D), pl.BlockSpec((8, D), lambda i: (0, 0))],
        out_shape=[jax.ShapeDtypeStruct((S, D), F32), jax.ShapeDtypeStruct((S, D), BF16),
                   jax.ShapeDtypeStruct((8, D), F32)],
        compiler_params=_cp(dimension_semantics=("arbitrary",)),
    )(x1, f, w_ffn_out, target)


def _rms_bwd(xv, nw, dh):
    r = lax.rsqrt(jnp.mean(xv * xv, axis=-1, keepdims=True) + EPS)
    xn = xv * r
    dxn = dh * nw
    dx = r * (dxn - xn * jnp.mean(dxn * xn, axis=-1, keepdims=True))
    return dx, dh * xn


def ffn_bwd(dy, dyb, gu, x1, norm_w, w_ffn_in, w_ffn_out, sides=()):
    def body(dy_ref, dyb_ref, gu_ref, x_ref, nw_ref, wi_ref, wo_ref, dgu_ref, dx_ref, dxb_ref, gn_ref):
        @pl.when(pl.program_id(0) == 0)
        def _():
            gn_ref[...] = jnp.zeros_like(gn_ref)

        df = _dot_nt(dyb_ref[...], wo_ref[...])
        gt = gu_ref[:, 0:FF].astype(F32)
        up = gu_ref[:, FF:2 * FF].astype(F32)
        sg = _sigmoid(gt)
        dgt = (df * up * _dsilu(gt, sg)).astype(BF16)
        dup = (df * gt * sg).astype(BF16)
        dgu_ref[:, 0:FF] = dgt
        dgu_ref[:, FF:2 * FF] = dup
        dh = _dot(dgt, wi_ref[0:FF, :]) + _dot(dup, wi_ref[FF:2 * FF, :])
        dxn, gw = _rms_bwd(x_ref[...], nw_ref[...], dh)
        dx = dy_ref[...] + dxn
        dx_ref[...] = dx
        dxb_ref[...] = dx.astype(BF16)
        gn_ref[...] = gn_ref[...] + jnp.sum(gw, axis=0, keepdims=True)

    return _call(
        body, sides, name="ffn_bwd", grid=(S // TM,),
        in_specs=[_row(D), _row(D), _row(2 * FF), _row(D), _res((1, D)), _res((2 * FF, D)), _res((FF, D))],
        out_specs=[_row(2 * FF), _row(D), _row(D), pl.BlockSpec((1, D), lambda i: (0, 0))],
        out_shape=[jax.ShapeDtypeStruct((S, 2 * FF), BF16), jax.ShapeDtypeStruct((S, D), F32),
                   jax.ShapeDtypeStruct((S, D), BF16), jax.ShapeDtypeStruct((1, D), F32)],
        args=(dy, dyb, gu, x1, norm_w, w_ffn_in, w_ffn_out))


def in_bwd(d_q, d_k, d_v, d_conv, d_gl, w_in, x, d_x1, norm_w, sides=()):
    segs = ((OFF_Q, QKV), (OFF_K, QKV), (OFF_V, QKV), (OFF_CA, 2 * CC), (OFF_GA, 2 * D))

    def body(dq_ref, dk_ref, dv_ref, dc_ref, dg_ref, w_ref, x_ref, dx1_ref, nw_ref, gx_ref, gn_ref):
        @pl.when(pl.program_id(0) == 0)
        def _():
            gn_ref[...] = jnp.zeros_like(gn_ref)

        dh = jnp.zeros((TM, D), F32)
        for ref, (off, width) in zip((dq_ref, dk_ref, dv_ref, dc_ref, dg_ref), segs):
            for j in range(width // PLANE):
                dh = dh + _dot(ref[j], w_ref[off + j * PLANE:off + (j + 1) * PLANE, :])
        dxn, gw = _rms_bwd(x_ref[...], nw_ref[...], dh)
        gx_ref[...] = dx1_ref[...] + dxn
        gn_ref[...] = gn_ref[...] + jnp.sum(gw, axis=0, keepdims=True)

    return _call(
        body, sides, name="in_bwd", grid=(S // TM,),
        in_specs=[_planes(QKV)] * 3 + [_planes(2 * CC), _planes(2 * D), _res((INW, D)), _row(D), _row(D), _res((1, D))],
        out_specs=[_row(D), pl.BlockSpec((1, D), lambda i: (0, 0))],
        out_shape=[jax.ShapeDtypeStruct((S, D), F32), jax.ShapeDtypeStruct((1, D), F32)],
        args=(d_q, d_k, d_v, d_conv, d_gl, w_in, x, d_x1, norm_w))


def mm_tn(name, a, b, tm, tn, sides=()):
    M, N = a.shape[1], b.shape[1]

    def body(a_ref, b_ref, o_ref):
        o_ref[...] = _dot_tn(a_ref[...], b_ref[...])

    res = _call(
        body, sides, name=name, grid=(M // tm, N // tn),
        in_specs=[pl.BlockSpec((S, tm), lambda i, j: (0, i)), pl.BlockSpec((S, tn), lambda i, j: (0, j))],
        out_specs=[pl.BlockSpec((tm, tn), lambda i, j: (i, j))],
        out_shape=[jax.ShapeDtypeStruct((M, N), F32)],
        args=(a, b))
    return (res[0][0], res[1]) if sides else res[0]


GW_IN_TN = PLANE
GW_IN_SPLIT = (768, 256)


def gw_in_t(name, ht, d_segs, col0, hw, sides=()):
    tn = GW_IN_TN
    starts, t0 = [], 0
    for seg in d_segs:
        starts.append(t0)
        t0 += seg.shape[0]
    ntiles = [seg.shape[0] for seg in d_segs]

    def body(h_ref, *refs):
        a_refs, o_ref = refs[:-1], refs[-1]
        n = pl.program_id(0)
        for a_ref, st, nt in zip(a_refs, starts, ntiles):
            @pl.when((n >= st) & (n < st + nt))
            def _(a_ref=a_ref):
                o_ref[...] = _dot(h_ref[...], a_ref[...]).T

    def seg_spec(st, nt):
        return pl.BlockSpec((None, S, tn), lambda n: (jnp.clip(n - st, 0, nt - 1), 0, 0))

    res = _call(
        body, sides, name=name, grid=(INW // tn,),
        in_specs=[pl.BlockSpec((hw, S), lambda n: (col0 // hw, 0))] + [seg_spec(st, nt) for st, nt in zip(starts, ntiles)],
        out_specs=[pl.BlockSpec((tn, hw), lambda n: (n, 0))],
        out_shape=[jax.ShapeDtypeStruct((INW, hw), F32)],
        args=(ht, *d_segs))
    return (res[0][0], res[1]) if sides else res[0]


def _place():
    x, y, c = lax.axis_index("x"), lax.axis_index("y"), lax.axis_index("c")
    chips = [(1 - x, y), (x, 1 - y), (1 - x, 1 - y)]
    return x, y, c, chips


def _sems(n):
    return pltpu.SemaphoreType.DMA((n,))


def _remote(src, dst, send, recv, k, to):
    return pltpu.make_async_remote_copy(src_ref=src, dst_ref=dst, send_sem=send.at[k], recv_sem=recv.at[k],
                                        device_id=to, device_id_type=MESH)


def _cast_rows(dst, src, cols=slice(None)):
    rows = src.shape[0]
    step = next((s for s in (128, 64, 32, 16) if rows % s == 0), rows)
    for r0 in range(0, rows, step):
        dst[r0:r0 + step, cols] = src[r0:r0 + step, :].astype(dst.dtype)


def comm_only(name, sides):
    def body():
        pass

    return _call(body, sides, name=name, grid=(1,), in_specs=[], out_specs=[], out_shape=[], args=())[1]


def ag_blocks(shard, dtype):
    R, W = shard.shape

    def copy(outs, scr, k, block, to, src=None):
        dst = outs[0].at[block]
        return _remote(dst if src is None else src, dst, scr[1], scr[2], k, to)

    def local(outs, scr, me):
        return pltpu.make_async_copy(scr[0], outs[0].at[me], scr[3].at[0])

    def start(ins, outs, scr):
        x, y, c, chips = _place()
        me = 4 * x + 2 * y + c
        _cast_rows(scr[0], ins[0])
        local(outs, scr, me).start()
        copy(outs, scr, 0, me, (x, y, 1 - c), src=scr[0]).start()
        for j, (cx, cy) in enumerate(chips):
            copy(outs, scr, 1 + j, me, (cx, cy, c), src=scr[0]).start()

    def finish(ins, outs, scr):
        x, y, c, chips = _place()
        me, sib = 4 * x + 2 * y + c, (x, y, 1 - c)
        passed = []
        for j, (cx, cy) in enumerate(chips):
            theirs = 4 * cx + 2 * cy + c
            copy(outs, scr, 1 + j, theirs, (x, y, c)).wait_recv()
            fwd = copy(outs, scr, 4 + j, theirs, sib)
            fwd.start()
            passed.append(fwd)
        copy(outs, scr, 0, 4 * x + 2 * y + 1 - c, (x, y, c)).wait_recv()
        for j, (cx, cy) in enumerate(chips):
            copy(outs, scr, 4 + j, 4 * cx + 2 * cy + 1 - c, (x, y, c)).wait_recv()
        copy(outs, scr, 0, me, sib, src=scr[0]).wait_send()
        for j, (cx, cy) in enumerate(chips):
            copy(outs, scr, 1 + j, me, (cx, cy, c), src=scr[0]).wait_send()
        for fwd in passed:
            fwd.wait_send()
        local(outs, scr, me).wait()

    return Side((shard,), (VMEM,), (jax.ShapeDtypeStruct((NDEV, R, W), dtype),),
                (pltpu.VMEM((R, W), dtype), _sems(7), _sems(7), _sems(1)), start, finish, None, "dsxy")


def ag_blocks_relay(shard, dtype):
    R, W = shard.shape
    half = R // 2

    def copy(outs, scr, k, block, to, src=None, rows=None):
        dst = outs[0].at[block] if rows is None else outs[0].at[block, pl.ds(rows * half, half), :]
        return _remote(dst if src is None else src, dst, scr[1], scr[2], k, to)

    def local(outs, scr, me):
        return pltpu.make_async_copy(scr[0], outs[0].at[me], scr[3].at[0])

    def own(outs, scr):
        x, y, c, _ = _place()
        me = 4 * x + 2 * y + c
        return [copy(outs, scr, k, me, to, src=scr[0])
                for k, to in enumerate([(x, y, 1 - c), (1 - x, y, c), (x, 1 - y, c)])]

    def start(ins, outs, scr):
        x, y, c, _ = _place()
        _cast_rows(scr[0], ins[0])
        local(outs, scr, 4 * x + 2 * y + c).start()
        for cp in own(outs, scr):
            cp.start()

    def passed_on(outs, scr):
        x, y, c, _ = _place()
        sib, xn, yn = (x, y, 1 - c), (1 - x, y, c), (x, 1 - y, c)
        b_xn, b_yn, b_dg = 4 * (1 - x) + 2 * y + c, 4 * x + 2 * (1 - y) + c, 4 * (1 - x) + 2 * (1 - y) + c
        near = [copy(outs, scr, 5, b_xn, yn, rows=0), copy(outs, scr, 3, b_xn, sib),
                copy(outs, scr, 6, b_yn, xn, rows=1), copy(outs, scr, 4, b_yn, sib)]
        far = [copy(outs, scr, 7, b_dg, sib, rows=0), copy(outs, scr, 8, b_dg, sib, rows=1)]
        return (b_xn, b_yn, b_dg), near, far

    def mid(ins, outs, scr):
        x, y, c, _ = _place()
        (b_xn, b_yn, _), near, _ = passed_on(outs, scr)
        copy(outs, scr, 1, b_xn, (x, y, c)).wait_recv()
        near[0].start()
        near[1].start()
        copy(outs, scr, 2, b_yn, (x, y, c)).wait_recv()
        near[2].start()
        near[3].start()

    def finish(ins, outs, scr):
        x, y, c, _ = _place()
        here = (x, y, c)
        (b_xn, b_yn, b_dg), near, far = passed_on(outs, scr)
        copy(outs, scr, 5, b_dg, here, rows=0).wait_recv()
        far[0].start()
        copy(outs, scr, 6, b_dg, here, rows=1).wait_recv()
        far[1].start()
        flip = 1 - 2 * c
        copy(outs, scr, 0, 4 * x + 2 * y + 1 - c, here).wait_recv()
        copy(outs, scr, 3, b_xn + flip, here).wait_recv()
        copy(outs, scr, 4, b_yn + flip, here).wait_recv()
        copy(outs, scr, 7, b_dg + flip, here, rows=0).wait_recv()
        copy(outs, scr, 8, b_dg + flip, here, rows=1).wait_recv()
        for cp in own(outs, scr) + near + far:
            cp.wait_send()
        local(outs, scr, 4 * x + 2 * y + c).wait()

    return Side((shard,), (VMEM,), (jax.ShapeDtypeStruct((NDEV, R, W), dtype),),
                (pltpu.VMEM((R, W), dtype), _sems(9), _sems(9), _sems(1)), start, finish, mid, "sxy")


def ag_cols(shard):
    K, C = shard.shape
    half, w2 = K // 2, 2 * C

    def win(out, rows_c, chip):
        return out.at[pl.ds(pl.multiple_of(rows_c * half, 16), half), pl.ds(pl.multiple_of(chip * w2, LANES), w2)]

    def ici(outs, scr, j, to, c, k):
        slab, send, recv = scr[2], scr[5], scr[6]
        return _remote(slab.at[pl.ds(pl.multiple_of(c * half, 16), half), :], win(outs[0], c, k), send, recv, j, to)

    def local(outs, scr, k):
        return pltpu.make_async_copy(scr[2], outs[0].at[:, pl.ds(pl.multiple_of(k * w2, LANES), w2)], scr[7].at[0])

    def start(ins, outs, scr):
        stage, inbox, slab, xs, xr = scr[:5]
        x, y, c, chips = _place()
        k = 2 * x + y
        _cast_rows(stage, ins[0])
        swap = _remote(stage, inbox, xs, xr, 0, (x, y, 1 - c))
        swap.start()
        for cc in range(2):
            @pl.when(c == cc)
            def _(cc=cc):
                _cast_rows(slab, stage, slice(cc * C, (cc + 1) * C))
        swap.wait()
        for cc in range(2):
            @pl.when(c == cc)
            def _(cc=cc):
                _cast_rows(slab, inbox, slice((1 - cc) * C, (2 - cc) * C))
        local(outs, scr, k).start()
        for j, (cx, cy) in enumerate(chips):
            ici(outs, scr, j, (cx, cy, c), c, k).start()

    def finish(ins, outs, scr):
        send, recv = scr[5], scr[6]
        x, y, c, chips = _place()
        k, sib = 2 * x + y, (x, y, 1 - c)
        passed = []
        for j, (cx, cy) in enumerate(chips):
            w = win(outs[0], c, 2 * cx + cy)
            _remote(w, w, send, recv, j, sib).wait_recv()
            fwd = _remote(w, w, send, recv, 3 + j, sib)
            fwd.start()
            passed.append(fwd)
        for j, (cx, cy) in enumerate(chips):
            w = win(outs[0], 1 - c, 2 * cx + cy)
            _remote(w, w, send, recv, 3 + j, sib).wait_recv()
        for j, (cx, cy) in enumerate(chips):
            ici(outs, scr, j, (cx, cy, c), c, k).wait_send()
        for fwd in passed:
            fwd.wait_send()
        local(outs, scr, k).wait()

    return Side((shard,), (VMEM,), (jax.ShapeDtypeStruct((K, NDEV * C), BF16),),
                (pltpu.VMEM((K, C), BF16), pltpu.VMEM((K, C), BF16), pltpu.VMEM((K, w2), BF16),
                 _sems(1), _sems(1), _sems(6), _sems(6), _sems(1)), start, finish, None, "dsxy")


def copies_side(args, out_shape, n_copies, plan, peers):
    def copies(ins, outs, scr):
        return [_remote(s_, d_, scr[0], scr[1], i, to) for i, (s_, d_, to) in enumerate(plan(ins, outs))]

    def start(ins, outs, scr):
        for cp in copies(ins, outs, scr):
            cp.start()

    def finish(ins, outs, scr):
        for cp in copies(ins, outs, scr):
            cp.wait()

    return Side(tuple(args), (ANY,) * len(args), tuple(out_shape), (_sems(n_copies), _sems(n_copies)),
                start, finish, None, peers)


def rs_to_sibling(grads):
    out_shape = [jax.ShapeDtypeStruct((4,) + g.shape[1:] if kind == "rows" else (g.shape[0] // 2, g.shape[1]), F32)
                 for kind, g in grads]

    def plan(ins, outs):
        x, y, c, _ = _place()
        sib, res = (x, y, 1 - c), []
        for (kind, _), g, r in zip(grads, ins, outs):
            if kind == "rows":
                res += [(g.at[2 * k + 1 - c], r.at[k], sib) for k in range(4)]
            else:
                half = g.shape[0] // 2
                res.append((g.at[pl.ds(pl.multiple_of((1 - c) * half, 8), half), :], r, sib))
        return res

    return copies_side([g for _, g in grads], out_shape, sum(4 if kind == "rows" else 1 for kind, _ in grads), plan, "s")


def rs_to_chips(parts):
    out_shape = [jax.ShapeDtypeStruct((3,) + p.shape[1:] if kind == "rows" else (3, p.shape[0], p.shape[1] // 4), BF16)
                 for kind, p in parts]

    def plan(ins, outs):
        x, y, c, chips = _place()
        res = []
        for (kind, _), p, r in zip(parts, ins, outs):
            for j, (cx, cy) in enumerate(chips):
                if kind == "rows":
                    src = p.at[2 * cx + cy]
                else:
                    w2 = p.shape[1] // 4
                    src = p.at[:, pl.ds(pl.multiple_of((2 * cx + cy) * w2, LANES), w2)]
                res.append((src, r.at[j], (cx, cy, c)))
        return res

    return copies_side([p for _, p in parts], out_shape, 3 * len(parts), plan, "dxy")


def rs_swap_halves(theirs):
    def plan(ins, outs):
        x, y, c, _ = _place()
        return [(t, r, (x, y, 1 - c)) for t, r in zip(ins, outs)]

    return copies_side(theirs, [jax.ShapeDtypeStruct(t.shape, F32) for t in theirs], len(theirs), plan, "s")


def _row_tiles(rows):
    return 2 if rows % 32 == 0 and rows >= 512 else 1


def chip_sum(name, grad, recv, c_idx, chip_idx):
    _, R, C = grad.shape
    nt = 1
    tr = R // nt

    def body(s_ref, g_ref, r_ref, p_ref, own_ref):
        k = pl.program_id(1)
        tot = g_ref[0] + r_ref[0]
        p_ref[0] = tot.astype(BF16)

        @pl.when(k == s_ref[1])
        def _():
            own_ref[...] = tot

    grid_spec = pltpu.PrefetchScalarGridSpec(
        num_scalar_prefetch=1, grid=(nt, 4),
        in_specs=[pl.BlockSpec((1, tr, C), lambda i, k, s: (2 * k + s[0], i, 0)),
                  pl.BlockSpec((1, tr, C), lambda i, k, s: (k, i, 0))],
        out_specs=[pl.BlockSpec((1, tr, C), lambda i, k, s: (k, i, 0)),
                   pl.BlockSpec((tr, C), lambda i, k, s: (i, 0))])
    return pl.pallas_call(
        body, name=name, grid_spec=grid_spec,
        out_shape=[jax.ShapeDtypeStruct((4, R, C), BF16), jax.ShapeDtypeStruct((R, C), F32)],
        compiler_params=_cp(dimension_semantics=("arbitrary", "arbitrary")),
    )(jnp.stack([c_idx, chip_idx]), grad, recv)


def _half_tiles(half):
    return 2 if half >= 512 else 1


def chip_sum_cols(name, grad, recv, c_idx, chip_idx):
    K, W = grad.shape
    half, w2 = K // 2, W // 4
    nt = _half_tiles(half)
    tr = half // nt

    def body(s_ref, g_ref, r_ref, p_ref, own_ref):
        tot = g_ref[...] + r_ref[...]
        p_ref[...] = tot.astype(BF16)

        @pl.when(pl.program_id(1) == s_ref[1])
        def _():
            own_ref[...] = tot

    grid_spec = pltpu.PrefetchScalarGridSpec(
        num_scalar_prefetch=1, grid=(nt, 4),
        in_specs=[pl.BlockSpec((tr, w2), lambda i, k, s: (s[0] * nt + i, k)),
                  pl.BlockSpec((tr, w2), lambda i, k, s: (i, k))],
        out_specs=[pl.BlockSpec((tr, w2), lambda i, k, s: (i, k)),
                   pl.BlockSpec((tr, w2), lambda i, k, s: (i, 0))])
    return pl.pallas_call(
        body, name=name, grid_spec=grid_spec,
        out_shape=[jax.ShapeDtypeStruct((half, W), BF16), jax.ShapeDtypeStruct((half, w2), F32)],
        compiler_params=_cp(dimension_semantics=("arbitrary", "arbitrary")),
    )(jnp.stack([c_idx, chip_idx]), grad, recv)


def col_final(name, own, recv, c_idx):
    half, w2 = own.shape
    C = w2 // 2
    nt = _half_tiles(half)
    tr = half // nt

    def body(s_ref, o_ref, r_ref, mine_ref, theirs_ref, t_ref):
        t_ref[...] = o_ref[...] + r_ref[0].astype(F32) + r_ref[1].astype(F32) + r_ref[2].astype(F32)
        for cc in range(2):
            @pl.when(s_ref[0] == cc)
            def _(cc=cc):
                mine_ref[...] = t_ref[:, cc * C:(cc + 1) * C]
                theirs_ref[...] = t_ref[:, (1 - cc) * C:(2 - cc) * C]

    grid_spec = pltpu.PrefetchScalarGridSpec(
        num_scalar_prefetch=1, grid=(nt,),
        in_specs=[pl.BlockSpec((tr, w2), lambda i, s: (i, 0)), pl.BlockSpec((3, tr, w2), lambda i, s: (0, i, 0))],
        out_specs=[pl.BlockSpec((tr, C), lambda i, s: (i, 0))] * 2,
        scratch_shapes=[pltpu.VMEM((tr, w2), F32)])
    return pl.pallas_call(
        body, name=name, grid_spec=grid_spec, out_shape=[jax.ShapeDtypeStruct((half, C), F32)] * 2,
        compiler_params=_cp(dimension_semantics=("arbitrary",)),
    )(jnp.stack([c_idx]), own, recv)


def _adamw(w, g, m, v):
    m2 = ADAM_B1 * m + (1.0 - ADAM_B1) * g
    v2 = ADAM_B2 * v + (1.0 - ADAM_B2) * (g * g)
    m_hat = m2 / (1.0 - ADAM_B1 ** ADAM_STEP)
    v_hat = v2 / (1.0 - ADAM_B2 ** ADAM_STEP)
    delta = -ADAM_LR * (m_hat / (jnp.sqrt(v_hat) + ADAM_EPS) + ADAM_WD * w)
    return delta, m2, v2


def shard_adam(name, owns, recvs, w, m, v):
    n = len(owns)
    R = owns[0].shape[0]
    ct = min(o.shape[1] for o in owns)
    first = [sum(o.shape[1] for o in owns[:j]) // ct for j in range(n)]
    count = [o.shape[1] // ct for o in owns]
    nt = _row_tiles(R)
    tr = R // nt

    def body(*refs):
        o_refs, r_refs = refs[:n], refs[n:2 * n]
        w_ref, m_ref, v_ref, g_ref, d_ref, nm_ref, nv_ref = refs[2 * n:]
        g = None
        for j in range(n):
            gj = o_refs[j][...] + r_refs[j][0].astype(F32) + r_refs[j][1].astype(F32) + r_refs[j][2].astype(F32)
            g = gj if g is None else jnp.where(pl.program_id(0) >= first[j], gj, g)
        delta, m2, v2 = _adamw(w_ref[...], g, m_ref[...], v_ref[...])
        g_ref[...] = g
        d_ref[...] = delta
        nm_ref[...] = m2
        nv_ref[...] = v2

    def part(j):
        return pl.BlockSpec((tr, ct), lambda k, i: (i, jnp.clip(k - first[j], 0, count[j] - 1)))

    def part3(j):
        return pl.BlockSpec((3, tr, ct), lambda k, i: (0, i, jnp.clip(k - first[j], 0, count[j] - 1)))

    tile = pl.BlockSpec((tr, ct), lambda k, i: (i, k))
    return pl.pallas_call(
        body, name=name, grid=(sum(count), nt),
        in_specs=[part(j) for j in range(n)] + [part3(j) for j in range(n)] + [tile, tile, tile],
        out_specs=[tile] * 4, out_shape=[jax.ShapeDtypeStruct((R, sum(count) * ct), F32)] * 4,
        compiler_params=_cp(dimension_semantics=("arbitrary", "arbitrary")),
    )(*owns, *recvs, w, m, v)


def adam_cols(name, mine, recv, w, m, v, c_idx):
    half, C = mine.shape
    nt = _half_tiles(half)
    tr = half // nt

    def body(s_ref, a_ref, b_ref, w_ref, m_ref, v_ref, g_ref, d_ref, nm_ref, nv_ref):
        g = jnp.where(pl.program_id(0) == s_ref[0], a_ref[...], b_ref[...])
        delta, m2, v2 = _adamw(w_ref[...], g, m_ref[...], v_ref[...])
        g_ref[...] = g
        d_ref[...] = delta
        nm_ref[...] = m2
        nv_ref[...] = v2

    part = pl.BlockSpec((tr, C), lambda hh, i, s: (i, 0))
    tile = pl.BlockSpec((tr, C), lambda hh, i, s: (hh * nt + i, 0))
    grid_spec = pltpu.PrefetchScalarGridSpec(
        num_scalar_prefetch=1, grid=(2, nt), in_specs=[part, part, tile, tile, tile], out_specs=[tile] * 4)
    return pl.pallas_call(
        body, name=name, grid_spec=grid_spec, out_shape=[jax.ShapeDtypeStruct((2 * half, C), F32)] * 4,
        compiler_params=_cp(dimension_semantics=("arbitrary", "arbitrary")),
    )(jnp.stack([c_idx]), mine, recv, w, m, v)


ROW_N1, ROW_N2, ROW_BG, ROW_QN, ROW_KN, ROW_CB, ROW_LW, ROW_LB, ROW_CW = 0, 1, 2, 4, 5, 6, 7, 8, 9
PACK_ROWS = 40
SMALL = ("norm1_w", "norm2_w", "b_gate", "q_norm_w", "k_norm_w", "conv_b", "conv_ln_w", "conv_ln_b", "conv_w")


def small_sync_adam(g, w, m, v, sq, sides=()):
    ns = len(SMALL)

    def body(*refs):
        gi = dict(zip(SMALL, refs[:ns]))
        wi = dict(zip(SMALL, refs[ns:2 * ns]))
        mi = dict(zip(SMALL, refs[2 * ns:3 * ns]))
        vi = dict(zip(SMALL, refs[3 * ns:4 * ns]))
        sq_ref = refs[4 * ns]
        outs = refs[4 * ns + 1:8 * ns + 1]
        loss_ref = refs[8 * ns + 1]
        pack, recv, tot, send_sems, recv_sems = refs[8 * ns + 2:]
        x, y, c, _ = _place()
        me = 4 * x + 2 * y + c

        pack[...] = jnp.zeros_like(pack)
        pack[ROW_KN:ROW_KN + 1, LANES:2 * LANES] = jnp.full((1, LANES), (0.5 / D) * jnp.sum(sq_ref[...]), F32)
        pack[ROW_N1:ROW_N1 + 1, :] = gi["norm1_w"][...]
        pack[ROW_N2:ROW_N2 + 1, :] = gi["norm2_w"][...]
        pack[ROW_BG:ROW_BG + 2, :] = gi["b_gate"][...]
        pack[ROW_QN:ROW_QN + 1, 0:HD] = gi["q_norm_w"][...]
        pack[ROW_KN:ROW_KN + 1, 0:HD] = gi["k_norm_w"][...]
        pack[ROW_CB:ROW_CB + 1, 0:CC] = gi["conv_b"][...]
        pack[ROW_LW:ROW_LW + 1, 0:CC] = gi["conv_ln_w"][...]
        pack[ROW_LB:ROW_LB + 1, 0:CC] = gi["conv_ln_b"][...]
        pack[ROW_CW:ROW_CW + KW, 0:CC] = gi["conv_w"][...]

        copies = []
        for k in range(1, NDEV):
            peer = (x ^ (k >> 2), y ^ ((k >> 1) & 1), c ^ (k & 1))
            cp = pltpu.make_async_remote_copy(
                src_ref=pack, dst_ref=recv.at[me], send_sem=send_sems.at[k - 1], recv_sem=recv_sems.at[k - 1],
                device_id=peer, device_id_type=MESH)
            cp.start()
            copies.append(cp)
        recv[me] = pack[...]
        for cp in copies:
            cp.wait()
        acc = recv[0]
        for p in range(1, NDEV):
            acc = acc + recv[p]
        tot[...] = acc

        def shard_grad(name):
            if name == "b_gate":
                return tot[ROW_BG:ROW_BG + 2, pl.ds(pl.multiple_of(me * LANES, LANES), LANES)]
            if name == "conv_w":
                win = tot[ROW_CW:ROW_CW + KW, pl.ds(pl.multiple_of((me // 2) * LANES, LANES), LANES)]
                return jnp.where(me % 2 == 1, win[:, HD:LANES], win[:, 0:HD])
            row = {"norm1_w": ROW_N1, "norm2_w": ROW_N2, "q_norm_w": ROW_QN, "k_norm_w": ROW_KN,
                   "conv_b": ROW_CB, "conv_ln_w": ROW_LW, "conv_ln_b": ROW_LB}[name]
            return tot[row:row + 1, 0:wi[name].shape[1]]

        for i, name in enumerate(SMALL):
            gr = shard_grad(name)
            delta, m2, v2 = _adamw(wi[name][...], gr, mi[name][...], vi[name][...])
            outs[4 * i][...] = gr
            outs[4 * i + 1][...] = delta
            outs[4 * i + 2][...] = m2
            outs[4 * i + 3][...] = v2
        loss_ref[...] = tot[ROW_KN:ROW_KN + 1, LANES:2 * LANES]

    out_shape = []
    for name in SMALL:
        out_shape += [jax.ShapeDtypeStruct(w[name].shape, F32)] * 4
    out_shape.append(jax.ShapeDtypeStruct((1, LANES), F32))
    args = [g[k] for k in SMALL] + [w[k] for k in SMALL] + [m[k] for k in SMALL] + [v[k] for k in SMALL] + [sq]
    res = _call(
        body, sides, name="small_sync_adam", grid=(1,), in_specs=[VMEM] * len(args),
        out_specs=[VMEM] * len(out_shape), out_shape=out_shape,
        scratch_shapes=[pltpu.VMEM((PACK_ROWS, D), F32), pltpu.VMEM((NDEV, PACK_ROWS, D), F32),
                        pltpu.VMEM((PACK_ROWS, D), F32), _sems(NDEV - 1), _sems(NDEV - 1)],
        args=args, own_comm=True)
    res, side_outs = res if sides else (res, None)
    out = {name: tuple(res[4 * i:4 * i + 4]) for i, name in enumerate(SMALL)}
    loss = res[4 * ns][0, 0]
    return (out, loss, side_outs) if sides else (out, loss)


MATS = ("w_in", "w_o_attn", "w_pw_conv", "w_out", "w_ffn_in", "w_ffn_out")
TRANSPOSED = ("w_in", "w_ffn_in")
WEIGHTS = ("norm1_w", "w_in", "b_gate", "q_norm_w", "k_norm_w", "w_o_attn", "conv_w", "conv_b", "conv_ln_w",
           "conv_ln_b", "w_pw_conv", "w_out", "norm2_w", "w_ffn_in", "w_ffn_out")


def _blocks_to_cols(blocks):
    n, R, C = blocks.shape
    return blocks.transpose(1, 0, 2).reshape(R, n * C)


def kernel(x, positions, norm1_w, w_in, b_gate, q_norm_w, k_norm_w, w_o_attn, conv_w, conv_b, conv_ln_w, conv_ln_b, w_pw_conv, w_out, norm2_w, w_ffn_in, w_ffn_out, loss_target, m_norm1_w, m_w_in, m_b_gate, m_q_norm_w, m_k_norm_w, m_w_o_attn, m_conv_w, m_conv_b, m_conv_ln_w, m_conv_ln_b, m_w_pw_conv, m_w_out, m_norm2_w, m_w_ffn_in, m_w_ffn_out, v_norm1_w, v_w_in, v_b_gate, v_q_norm_w, v_k_norm_w, v_w_o_attn, v_conv_w, v_conv_b, v_conv_ln_w, v_conv_ln_b, v_w_pw_conv, v_w_out, v_norm2_w, v_w_ffn_in, v_w_ffn_out):
    w = dict(norm1_w=norm1_w, w_in=w_in, b_gate=b_gate, q_norm_w=q_norm_w, k_norm_w=k_norm_w, w_o_attn=w_o_attn,
             conv_w=conv_w, conv_b=conv_b, conv_ln_w=conv_ln_w, conv_ln_b=conv_ln_b, w_pw_conv=w_pw_conv,
             w_out=w_out, norm2_w=norm2_w, w_ffn_in=w_ffn_in, w_ffn_out=w_ffn_out)
    m = dict(norm1_w=m_norm1_w, w_in=m_w_in, b_gate=m_b_gate, q_norm_w=m_q_norm_w, k_norm_w=m_k_norm_w,
             w_o_attn=m_w_o_attn, conv_w=m_conv_w, conv_b=m_conv_b, conv_ln_w=m_conv_ln_w,
             conv_ln_b=m_conv_ln_b, w_pw_conv=m_w_pw_conv, w_out=m_w_out, norm2_w=m_norm2_w,
             w_ffn_in=m_w_ffn_in, w_ffn_out=m_w_ffn_out)
    v = dict(norm1_w=v_norm1_w, w_in=v_w_in, b_gate=v_b_gate, q_norm_w=v_q_norm_w, k_norm_w=v_k_norm_w,
             w_o_attn=v_w_o_attn, conv_w=v_conv_w, conv_b=v_conv_b, conv_ln_w=v_conv_ln_w,
             conv_ln_b=v_conv_ln_b, w_pw_conv=v_w_pw_conv, w_out=v_w_out, norm2_w=v_norm2_w,
             w_ffn_in=v_w_ffn_in, w_ffn_out=v_w_ffn_out)
    def two_d(t):
        t = {k: (a[0] if a.ndim == 3 else a) for k, a in t.items()}
        return {k: (a.T if k in TRANSPOSED else a) for k, a in t.items()}

    w, m, v = two_d(w), two_d(m), two_d(v)

    x2, target = x[0], loss_target[0]
    c_idx = lax.axis_index("c").astype(jnp.int32)
    chip_idx = (2 * lax.axis_index("x") + lax.axis_index("y")).astype(jnp.int32)
    qw2 = jnp.tile(w["q_norm_w"], (1, 2))
    kw2 = jnp.tile(w["k_norm_w"], (1, 2))

    ax, ay = lax.axis_index("x"), lax.axis_index("y")
    chip_order = jnp.stack([2 * ax + ay, 2 * (1 - ax) + ay, 2 * ax + 1 - ay, 2 * (1 - ax) + 1 - ay]).astype(jnp.int32)
    h_t, proj, w_in_blocks, tabs = in_proj_gather(x2, w["norm1_w"], w["w_in"], chip_order, positions.reshape(S, 1))
    w_in_t = w_in_blocks.reshape(INW, D)
    (attn, lse), ((w_ffn_in_blocks,), (w_out_blocks,), (bg_blocks,), (cw_blocks,)) = attn_fwd(
        proj, tabs, qw2, kw2, sides=(ag_blocks_relay(w["w_ffn_in"], BF16), ag_blocks_relay(w["w_out"], BF16),
                                     ag_blocks(w["b_gate"], F32), ag_blocks(w["conv_w"], F32)))
    w_ffn_in_t = w_ffn_in_blocks.reshape(2 * FF, D)
    w_out_f = w_out_blocks.reshape(D, D)
    b_gate_f, conv_w_f = _blocks_to_cols(bg_blocks), _blocks_to_cols(cw_blocks)
    (cpre, u3), ((w_o_f,), (w_pw_f,)) = conv_fwd(
        proj, conv_w_f, w["conv_b"], w["conv_ln_w"], w["conv_ln_b"],
        sides=(ag_cols(w["w_o_attn"]), ag_cols(w["w_pw_conv"])))
    x1, z, ya, yb = mix_out(x2, proj, b_gate_f, attn, u3, w_o_f, w_pw_f, w_out_f)
    (h2, gu, f), ((w_ffn_out_blocks,),) = ffn_in(x1, w["norm2_w"], w_ffn_in_t, sides=(ag_blocks_relay(w["w_ffn_out"], BF16),))
    w_ffn_out_f = w_ffn_out_blocks.reshape(FF, D)
    dy, dyb, sq = ffn_out_loss(x1, f, w_ffn_out_f, target)

    g = {}
    g_ffn_out = mm_tn("gw_ffn_out", f, dyb, FF // 2, D).reshape(NDEV, FF // NDEV, D)
    (d_gu, d_x1, d_x1b, g["norm2_w"]), ((ra_ffn_out,),) = ffn_bwd(
        dy, dyb, gu, x1, w["norm2_w"], w_ffn_in_t, w_ffn_out_f, sides=(rs_to_sibling([("rows", g_ffn_out)]),))
    pb_ffn_out, own_ffn_out = chip_sum("chip_sum_w_ffn_out", g_ffn_out, ra_ffn_out, c_idx, chip_idx)
    g_ffn_in, ((rb_ffn_out,),) = mm_tn("gw_ffn_in", d_gu, h2, FF // 2, D,
                                       sides=(rs_to_chips([("rows", pb_ffn_out)]),))
    g_ffn_in = g_ffn_in.reshape(NDEV, 2 * FF // NDEV, D)
    g_out = mm_tn("gw_out", z, d_x1b, D // 2, D).reshape(NDEV, D // NDEV, D)
    (d_ya, d_yb, d_gl, d_attn, d_u3, g["b_gate"]), ((ra_ffn_in,),) = out_bwd(
        d_x1b, proj, b_gate_f, ya, yb, w_o_f, w_pw_f, w_out_f, sides=(rs_to_sibling([("rows", g_ffn_in)]),))
    pb_ffn_in, own_ffn_in = chip_sum("chip_sum_w_ffn_in", g_ffn_in, ra_ffn_in, c_idx, chip_idx)
    g_w_o = mm_tn("gw_o_attn", attn, d_ya, CC, D)
    g_w_pw = mm_tn("gw_pw_conv", u3, d_yb, CC, D)
    (d_conv, g["conv_w"], g["conv_b"], g["conv_ln_w"], g["conv_ln_b"]), ((ra_out, ra_w_o, ra_w_pw),) = conv_bwd(
        proj, cpre, d_u3, conv_w_f, conv_w_f[::-1], w["conv_ln_w"], w["conv_ln_b"],
        sides=(rs_to_sibling([("rows", g_out), ("cols", g_w_o), ("cols", g_w_pw)]),))
    pb_out, own_out = chip_sum("chip_sum_w_out", g_out, ra_out, c_idx, chip_idx)
    pb_w_o, own_w_o = chip_sum_cols("chip_sum_w_o_attn", g_w_o, ra_w_o, c_idx, chip_idx)
    pb_w_pw, own_w_pw = chip_sum_cols("chip_sum_w_pw_conv", g_w_pw, ra_w_pw, c_idx, chip_idx)
    (d_q, d_k, d_v, gqw, gkw), ((rb_ffn_in, rb_out, rb_w_o, rb_w_pw),) = attn_bwd(
        proj, tabs, qw2, kw2, d_attn, attn, lse,
        sides=(rs_to_chips([("rows", pb_ffn_in), ("rows", pb_out), ("cols", pb_w_o), ("cols", pb_w_pw)]),))
    g["q_norm_w"] = gqw[0:1, 0:HD] + gqw[0:1, HD:LANES]
    g["k_norm_w"] = gkw[0:1, 0:HD] + gkw[0:1, HD:LANES]
    mine_w_o, theirs_w_o = col_final("col_final_w_o_attn", own_w_o, rb_w_o, c_idx)
    mine_w_pw, theirs_w_pw = col_final("col_final_w_pw_conv", own_w_pw, rb_w_pw, c_idx)
    d_segs = (d_q, d_k, d_v, d_conv, d_gl)
    parts, to_sibling, to_chips, owns, from_chips = [], None, None, [], []
    for k, hw in enumerate(GW_IN_SPLIT):
        sides = [rs_swap_halves([theirs_w_o, theirs_w_pw])] if k == 0 else []
        sides += [s for s in (to_chips, to_sibling) if s is not None]
        part, outs = gw_in_t("gw_in_%d" % k, h_t, d_segs, sum(GW_IN_SPLIT[:k]), hw, sides=tuple(sides))
        if k == 0:
            (rc_w_o, rc_w_pw), outs = outs[0], outs[1:]
        outs = list(outs)
        if to_chips is not None:
            from_chips.append(outs.pop(0)[0])
        if to_sibling is not None:
            pb, own = chip_sum("chip_sum_w_in_%d" % (k - 1), parts[-1], outs.pop(0)[0], c_idx, chip_idx)
            owns.append(own)
            to_chips = rs_to_chips([("rows", pb)])
        else:
            to_chips = None
        parts.append(part.reshape(NDEV, INW // NDEV, hw))
        to_sibling = rs_to_sibling([("rows", parts[-1])])
    (grad_x, g["norm1_w"]), ((rb_prev,), (ra_last,)) = in_bwd(
        d_q, d_k, d_v, d_conv, d_gl, w_in_t, x2, d_x1, w["norm1_w"], sides=(to_chips, to_sibling))
    from_chips.append(rb_prev)
    pb, own = chip_sum("chip_sum_w_in_%d" % (len(GW_IN_SPLIT) - 1), parts[-1], ra_last, c_idx, chip_idx)
    owns.append(own)
    small, loss, ((rb_last,),) = small_sync_adam(g, w, m, v, sq, sides=(rs_to_chips([("rows", pb)]),))
    from_chips.append(rb_last)

    res = {
        "w_in": shard_adam("adam_w_in", owns, from_chips, w["w_in"], m["w_in"], v["w_in"]),
        "w_ffn_in": shard_adam("adam_w_ffn_in", [own_ffn_in], [rb_ffn_in], w["w_ffn_in"], m["w_ffn_in"], v["w_ffn_in"]),
        "w_o_attn": adam_cols("adam_w_o_attn", mine_w_o, rc_w_o, w["w_o_attn"], m["w_o_attn"], v["w_o_attn"], c_idx),
        "w_pw_conv": adam_cols("adam_w_pw_conv", mine_w_pw, rc_w_pw, w["w_pw_conv"], m["w_pw_conv"], v["w_pw_conv"], c_idx),
        "w_out": shard_adam("adam_w_out", [own_out], [rb_out], w["w_out"], m["w_out"], v["w_out"]),
        "w_ffn_out": shard_adam("adam_w_ffn_out", [own_ffn_out], [rb_ffn_out],
                                w["w_ffn_out"], m["w_ffn_out"], v["w_ffn_out"]),
    }
    res = {k: tuple(a.T if k in TRANSPOSED else a for a in r) for k, r in res.items()}
    res.update(small)

    def shaped(name, a):
        return a.reshape((1,) + a.shape) if name in MATS or name in ("b_gate", "conv_w") else a

    outs = [loss, grad_x.reshape(1, S, D)]
    for i in range(4):
        outs += [shaped(k, res[k][i]) for k in WEIGHTS]
    return tuple(outs)
```

```python
import functools
from typing import Callable, NamedTuple, Optional

import numpy as np
import jax
import jax.numpy as jnp
from jax import lax
from jax.experimental import pallas as pl
from jax.experimental.pallas import tpu as pltpu

F32 = jnp.float32
BF16 = jnp.bfloat16

S = 2048
D = 1024
HD = 64
QKV = 1536
CC = 512
KW = 31
FF = 2816
INW = 7680
OFF_Q, OFF_K, OFF_V, OFF_CA, OFF_CB, OFF_GA, OFF_GB = 0, 1536, 3072, 4608, 5120, 5632, 6656
DILATIONS = (1, 4, 16)
HALF_SPAN = 64
EPS = 1e-6
NEG_INF = -1e30
ROPE_THETA = 500000.0
ROT_DIM = 16

ADAM_LR = 0.001
ADAM_B1 = 0.9
ADAM_B2 = 0.999
ADAM_EPS = 1e-08
ADAM_WD = 0.01
ADAM_STEP = 10

NDEV = 8
LANES = 128
TM = 256
TQ = 128
VMEM_LIMIT = 56 * 1024 * 1024
MESH = pl.DeviceIdType.MESH


def _cp(**kw):
    return pltpu.CompilerParams(vmem_limit_bytes=VMEM_LIMIT, **kw)


def _row(width, col=0, tm=TM):
    return pl.BlockSpec((tm, width), lambda i: (i, col))


PLANE = 512


def _planes(width, tm=TM):
    return pl.BlockSpec((width // PLANE, tm, PLANE), lambda i: (0, i, 0))


def _res(shape):
    nd = len(shape)
    return pl.BlockSpec(shape, lambda *_: (0,) * nd, pipeline_mode=pl.Buffered(1))


def _dot(a, b):
    return jnp.dot(a, b, preferred_element_type=F32)


def _dot_nt(a, b):
    return lax.dot_general(a, b, (((1,), (1,)), ((), ())), preferred_element_type=F32)


def _dot_tn(a, b):
    return lax.dot_general(a, b, (((0,), (0,)), ((), ())), preferred_element_type=F32)


def _sigmoid(x):
    return jax.nn.sigmoid(x)


def _dsilu(x, sg):
    return sg * (1.0 + x * (1.0 - sg))


ANY = pl.BlockSpec(memory_space=pl.ANY)
VMEM = pl.BlockSpec(memory_space=pltpu.VMEM)


class Side(NamedTuple):
    args: tuple
    in_specs: tuple
    out_shape: tuple
    scratch: tuple
    start: Callable
    finish: Callable
    mid: Optional[Callable] = None
    peers: str = ""


BARRIER_IDS = {"s": 0, "dxy": 1, "dsxy": 2, "sxy": 3}


def _peer_barrier(peers):
    x, y, c = lax.axis_index("x"), lax.axis_index("y"), lax.axis_index("c")
    where = {"s": (x, y, 1 - c), "x": (1 - x, y, c), "y": (x, 1 - y, c), "d": (1 - x, 1 - y, c)}
    barrier = pltpu.get_barrier_semaphore()
    for p in peers:
        pl.semaphore_signal(barrier, inc=1, device_id=where[p], device_id_type=MESH)
    pl.semaphore_wait(barrier, len(peers))


def _call(body, sides=(), *, name, grid, in_specs, out_specs, out_shape, scratch_shapes=(), args, own_comm=False):
    ni, no, ns = len(in_specs), len(out_specs), len(scratch_shapes)
    cnt = [(len(s.args), len(s.out_shape), len(s.scratch)) for s in sides]
    peers = "".join(sorted(set("".join(s.peers for s in sides))))
    if own_comm or not sides or any(not s.peers for s in sides):
        peers = ""

    def take(refs, pos, n):
        return refs[pos:pos + n], pos + n

    def full(*refs):
        m_in, pos = take(refs, 0, ni)
        s_in = []
        for a, _, _ in cnt:
            r, pos = take(refs, pos, a)
            s_in.append(r)
        m_out, pos = take(refs, pos, no)
        s_out = []
        for _, o, _ in cnt:
            r, pos = take(refs, pos, o)
            s_out.append(r)
        m_scr, pos = take(refs, pos, ns)
        s_scr = []
        for _, _, c in cnt:
            r, pos = take(refs, pos, c)
            s_scr.append(r)
        if sides:
            first = functools.reduce(jnp.logical_and, [pl.program_id(d) == 0 for d in range(len(grid))])
            last = functools.reduce(jnp.logical_and, [pl.program_id(d) == g - 1 for d, g in enumerate(grid)])

            @pl.when(first)
            def _():
                if peers:
                    _peer_barrier(peers)
                for s, a, o, c in zip(sides, s_in, s_out, s_scr):
                    s.start(a, o, c)

            steps = int(np.prod(grid))
            mid_step = (2 * steps) // 3
            if steps > 1 and any(s.mid is not None for s in sides):
                step = functools.reduce(lambda acc, d: acc * grid[d] + pl.program_id(d), range(len(grid)), 0)

                @pl.when(step == mid_step)
                def _():
                    for s, a, o, c in zip(sides, s_in, s_out, s_scr):
                        if s.mid is not None:
                            s.mid(a, o, c)

        body(*m_in, *m_out, *m_scr)
        if sides:
            @pl.when(last)
            def _():
                for s, a, o, c in zip(sides, s_in, s_out, s_scr):
                    if s.mid is not None and steps == 1:
                        s.mid(a, o, c)
                    s.finish(a, o, c)

    res = pl.pallas_call(
        full, name=name, grid=grid,
        in_specs=list(in_specs) + [sp for s in sides for sp in s.in_specs],
        out_specs=list(out_specs) + [ANY for s in sides for _ in s.out_shape],
        out_shape=list(out_shape) + [o for s in sides for o in s.out_shape],
        scratch_shapes=list(scratch_shapes) + [c for s in sides for c in s.scratch],
        compiler_params=_cp(dimension_semantics=("arbitrary",) * len(grid),
                            **({"collective_id": BARRIER_IDS[peers]} if peers else {})),
    )(*args, *[a for s in sides for a in s.args])
    res = list(res)
    if not sides:
        return res
    outs, pos = take(res, 0, no)
    side_outs = []
    for _, o, _ in cnt:
        r, pos = take(res, pos, o)
        side_outs.append(r)
    return outs, side_outs


def _inv_freq_lanes():
    inv = np.float32(ROPE_THETA) ** (-np.arange(0, ROT_DIM, 2, dtype=np.float32) / np.float32(ROT_DIM))
    lane = np.arange(LANES) % HD
    out = np.where(lane < ROT_DIM, inv[lane % (ROT_DIM // 2)], 0.0).astype(np.float32)
    return jnp.asarray(out.reshape(1, LANES))


def _rope_tables(pos, inv_freq):
    ang = pos.astype(F32) * inv_freq
    lane = lax.broadcasted_iota(jnp.int32, ang.shape, 1) % HD
    cs = jnp.cos(ang)
    sn = jnp.sin(ang)
    return (jnp.where(lane < ROT_DIM, cs, 1.0), jnp.where(lane < ROT_DIM // 2, -sn, 0.0),
            jnp.where(lane < ROT_DIM // 2, 0.0, jnp.where(lane < ROT_DIM, sn, 0.0)))


def _rope(v, c, s1, s2):
    return v * c + pltpu.roll(v, LANES - 8, axis=1) * s1 + pltpu.roll(v, 8, axis=1) * s2


def _rope_t(d, c, s1, s2):
    return d * c - pltpu.roll(d, LANES - 8, axis=1) * s1 - pltpu.roll(d, 8, axis=1) * s2


def _head_mat():
    r = lax.broadcasted_iota(jnp.int32, (LANES, LANES), 0) // HD
    c = lax.broadcasted_iota(jnp.int32, (LANES, LANES), 1) // HD
    return jnp.where(r == c, 1.0 / HD, 0.0).astype(BF16)


def _head_mean(t, e):
    hi = t.astype(BF16)
    rest = (t - hi.astype(F32)).astype(BF16)
    return _dot(hi, e) + _dot(rest, e)


def in_proj_gather(x, norm_w, shard_t, chip_order, pos_col):
    R = INW // NDEV
    half, nt = R // 2, S // TM

    def body(ord_ref, x_ref, nw_ref, sh_ref, pos_ref, f_ref, ht_ref, p_ref, wfull_ref, c_ref, s1_ref, s2_ref,
             wt, hs, send, recv, loc):
        kk, i = pl.program_id(0), pl.program_id(1)
        x, y, c, _ = _place()
        me, flip = 4 * x + 2 * y + c, 1 - 2 * c
        here, sib, xn, yn = (x, y, c), (x, y, 1 - c), (1 - x, y, c), (x, 1 - y, c)
        b_xn, b_yn, b_dg = 4 * (1 - x) + 2 * y + c, 4 * x + 2 * (1 - y) + c, 4 * (1 - x) + 2 * (1 - y) + c

        def cp(k, block, to, rows=None):
            dst = wt.at[block] if rows is None else wt.at[block, pl.ds(rows * half, half), :]
            return _remote(dst, dst, send, recv, k, to)

        def sends():
            return [cp(0, me, sib), cp(1, me, xn), cp(2, me, yn), cp(3, b_xn, sib), cp(4, b_yn, sib),
                    cp(5, b_xn, yn, rows=0), cp(6, b_yn, xn, rows=1), cp(7, b_dg, sib, rows=0), cp(8, b_dg, sib, rows=1)]

        def keep(j, blk0):
            pair = pl.ds(pl.multiple_of(blk0, 2), 2)
            return pltpu.make_async_copy(wt.at[pair], wfull_ref.at[pair], loc.at[j])

        @pl.when((kk == 0) & (i == 0))
        def _():
            _peer_barrier("sxy")
            _cast_rows(wt.at[me], sh_ref)
            for s_ in sends()[0:3]:
                s_.start()

            def tables(j, _):
                chunk = pl.ds(pl.multiple_of(j * TM, TM), TM)
                c_ref[chunk, :], s1_ref[chunk, :], s2_ref[chunk, :] = _rope_tables(pos_ref[chunk, :], f_ref[...])
                return 0

            lax.fori_loop(0, nt, tables, 0)
            cp(0, me + flip, here).wait_recv()
            keep(0, me - c).start()

        @pl.when((kk == 1) & (i == 0))
        def _():
            cp(1, b_xn, here).wait_recv()
            sends()[5].start()
            sends()[3].start()
            cp(2, b_yn, here).wait_recv()
            sends()[6].start()
            sends()[4].start()
            cp(3, b_xn + flip, here).wait_recv()
            keep(1, b_xn - c).start()

        @pl.when((kk == 2) & (i == 0))
        def _():
            cp(4, b_yn + flip, here).wait_recv()
            keep(2, b_yn - c).start()

        @pl.when((kk == 3) & (i == 0))
        def _():
            cp(5, b_dg, here, rows=0).wait_recv()
            sends()[7].start()
            cp(6, b_dg, here, rows=1).wait_recv()
            sends()[8].start()
            cp(7, b_dg + flip, here, rows=0).wait_recv()
            cp(8, b_dg + flip, here, rows=1).wait_recv()
            keep(3, b_dg - c).start()

        rows = pl.ds(pl.multiple_of(i * TM, TM), TM)

        @pl.when(kk == 0)
        def _():
            xv = x_ref[...]
            r = lax.rsqrt(jnp.mean(xv * xv, axis=-1, keepdims=True) + EPS)
            hf = xv * r * nw_ref[...]
            ht_ref[...] = hf.T.astype(BF16)
            hs[rows, :] = hf.astype(BF16)

        h = hs[rows, :]
        chip = ord_ref[kk]
        for cc in range(2):
            p_ref[:, cc * R:(cc + 1) * R] = _dot_nt(h, wt[2 * chip + cc])

        @pl.when((kk == 3) & (i == nt - 1))
        def _():
            for s_ in sends():
                s_.wait_send()
            for j, blk in enumerate((me, b_xn, b_yn, b_dg)):
                keep(j, blk - c).wait()

    def first_pass(kk, i):
        return jnp.where(kk == 0, i, nt - 1)

    grid_spec = pltpu.PrefetchScalarGridSpec(
        num_scalar_prefetch=1, grid=(4, nt),
        in_specs=[pl.BlockSpec((TM, D), lambda kk, i, o: (first_pass(kk, i), 0)),
                  pl.BlockSpec((1, D), lambda kk, i, o: (0, 0)), VMEM, VMEM,
                  pl.BlockSpec((1, LANES), lambda kk, i, o: (0, 0))],
        out_specs=[pl.BlockSpec((D, TM), lambda kk, i, o: (0, first_pass(kk, i))),
                   pl.BlockSpec((TM, 2 * R), lambda kk, i, o: (i, o[kk])), ANY]
        + [pl.BlockSpec((S, LANES), lambda kk, i, o: (0, 0))] * 3,
        scratch_shapes=[pltpu.VMEM((NDEV, R, D), BF16), pltpu.VMEM((S, D), BF16), _sems(9), _sems(9), _sems(4)])
    res = pl.pallas_call(
        body, name="in_proj_gather", grid_spec=grid_spec,
        out_shape=[jax.ShapeDtypeStruct((D, S), BF16), jax.ShapeDtypeStruct((S, INW), F32),
                   jax.ShapeDtypeStruct((NDEV, R, D), BF16)] + [jax.ShapeDtypeStruct((S, LANES), F32)] * 3,
        compiler_params=_cp(dimension_semantics=("arbitrary", "arbitrary"), collective_id=BARRIER_IDS["sxy"]),
    )(chip_order, x, norm_w, shard_t, pos_col, _inv_freq_lanes())
    return res[0], res[1], res[2], tuple(res[3:])


def _qk_specs():
    nb = QKV // LANES
    return [pl.BlockSpec((S, LANES), functools.partial(lambda hp, g, o: (0, o + g * 4 + hp), o=o))
            for o in (OFF_Q // LANES, OFF_K // LANES, OFF_V // LANES)]


def _tab_specs():
    return [pl.BlockSpec((S, LANES), lambda hp, g: (0, 0), pipeline_mode=pl.Buffered(1))] * 3


def _vec_spec():
    return pl.BlockSpec((1, LANES), lambda hp, g: (0, 0))


def _sub_rows(r, d, start, n):
    if d == 1:
        return pl.ds(start, n)
    return pl.ds(r + d * start, n, stride=d)


def _band_window(i, L):
    W = min(TQ + 2 * HALF_SPAN, L)
    q0 = pl.multiple_of(i * TQ, TQ)
    k0 = pl.multiple_of(jnp.clip(q0 - HALF_SPAN, 0, L - W), HALF_SPAN)
    qpos = q0 + (lax.broadcasted_iota(jnp.int32, (2 * TQ, W), 0) & (TQ - 1))
    kpos = k0 + lax.broadcasted_iota(jnp.int32, (2 * TQ, W), 1)
    valid = jnp.abs(qpos - kpos) <= HALF_SPAN
    return W, q0, k0, valid


def _stack_heads(t, lo):
    z = jnp.zeros_like(t)
    return jnp.concatenate([jnp.where(lo, t, z), jnp.where(lo, z, t)], axis=0)


def _unstack_heads(t2, lo):
    return jnp.where(lo, t2[0:TQ], t2[TQ:2 * TQ])


CHAINS = 8


def _interleave(d):
    ru = min(d, CHAINS)
    return ru, min(CHAINS // ru, S // d // TQ)


def _for_blocks(n, fn):
    if n == 1:
        fn(0)
    else:
        def it(j, _):
            fn(j)
            return 0
        lax.fori_loop(0, n, it, 0)


def attn_fwd(proj, tabs, qw2, kw2, sides=()):
    CH = 256

    def body(q_ref, k_ref, v_ref, c_ref, s1_ref, s2_ref, qw_ref, kw_ref, at_ref, ls_ref,
             qs, ks, vs, osub, lsub, onat, lnat, qn, kn):
        g = pl.program_id(1)
        lo = lax.broadcasted_iota(jnp.int32, (1, LANES), 1) < HD
        e = _head_mat()

        def prep(i, _):
            rows = pl.ds(pl.multiple_of(i * CH, CH), CH)
            c, s1, s2 = c_ref[rows, :], s1_ref[rows, :], s2_ref[rows, :]
            for t_ref, w_ref, out, scale in ((q_ref, qw_ref, qn, HD ** -0.5), (k_ref, kw_ref, kn, 1.0)):
                t = t_ref[rows, :]
                r = lax.rsqrt(_head_mean(t * t, e) + EPS)
                out[rows, :] = _rope(t * r * w_ref[...], c, s1, s2) * scale
            return 0

        lax.fori_loop(0, S // CH, prep, 0, unroll=4)

        def group(gi, d):
            L = S // d

            ru, nb = _interleave(d)

            def stage(r, off):
                for c0 in range(0, L, CH):
                    n = min(CH, L)
                    rows = _sub_rows(r, d, c0, n)
                    dst = pl.ds(off + c0, n)
                    qs[dst, :] = qn[rows, :].astype(BF16)
                    ks[dst, :] = kn[rows, :].astype(BF16)
                    vs[dst, :] = v_ref[rows, :].astype(BF16)

            def one(off, i):
                W, q0, k0, valid = _band_window(i, L)
                q2 = _stack_heads(qs[pl.ds(off + q0, TQ), :], lo)
                sc = jnp.where(valid, _dot_nt(q2, ks[pl.ds(off + k0, W), :]), NEG_INF)
                m = jnp.max(sc, axis=-1, keepdims=True)
                p = jnp.exp(sc - m)
                den = jnp.sum(p, axis=-1, keepdims=True)
                o2 = _dot(p.astype(BF16), vs[pl.ds(off + k0, W), :]) / den
                l2 = jnp.broadcast_to(m + jnp.log(den), (2 * TQ, LANES))
                osub[pl.ds(off + q0, TQ), :] = _unstack_heads(o2, lo)
                lsub[pl.ds(off + q0, TQ), :] = _unstack_heads(l2, lo)

            def unstage(r, off):
                for c0 in range(0, L, CH):
                    n = min(CH, L)
                    rows = _sub_rows(r, d, c0, n)
                    onat[gi, rows, :] = osub[pl.ds(off + c0, n), :]
                    lnat[gi, rows, :] = lsub[pl.ds(off + c0, n), :]

            def step(t, _):
                for u in range(ru):
                    stage(t * ru + u, u * L)
                _for_blocks(L // TQ // nb, lambda j: [one(u * L, j * nb + b) for u in range(ru) for b in range(nb)])
                for u in range(ru):
                    unstage(t * ru + u, u * L)
                return 0

            lax.fori_loop(0, d // ru, step, 0)

        for gi, d in enumerate(DILATIONS):
            pl.when(g == gi)(functools.partial(group, gi, d))

        @pl.when(g == len(DILATIONS) - 1)
        def _():
            def mix(i, _):
                rows = pl.ds(pl.multiple_of(i * CH, CH), CH)
                l0, l1, l2 = lnat[0, rows, :], lnat[1, rows, :], lnat[2, rows, :]
                m = jnp.maximum(jnp.maximum(l0, l1), l2)
                e0, e1, e2 = jnp.exp(l0 - m), jnp.exp(l1 - m), jnp.exp(l2 - m)
                den = e0 + e1 + e2
                a = (e0 * onat[0, rows, :] + e1 * onat[1, rows, :] + e2 * onat[2, rows, :]) / den
                at_ref[rows, :] = a.astype(BF16)
                ls_ref[rows, :] = m + jnp.log(den)
                return 0

            lax.fori_loop(0, S // CH, mix, 0)

    out_spec = pl.BlockSpec((S, LANES), lambda hp, g: (0, hp))
    return _call(
        body, sides, name="attn_fwd", grid=(4, 3),
        in_specs=_qk_specs() + _tab_specs() + [_vec_spec(), _vec_spec()],
        out_specs=[out_spec, out_spec],
        out_shape=[jax.ShapeDtypeStruct((S, CC), BF16), jax.ShapeDtypeStruct((S, CC), F32)],
        scratch_shapes=[pltpu.VMEM((S, LANES), BF16)] * 3 + [pltpu.VMEM((S, LANES), F32)] * 2
        + [pltpu.VMEM((3, S, LANES), F32)] * 2 + [pltpu.VMEM((S, LANES), F32)] * 2,
        args=(proj, proj, proj, *tabs, qw2, kw2))


def attn_bwd(proj, tabs, qw2, kw2, d_attn, attn, lse, sides=()):
    CH = 256

    def body(q_ref, k_ref, v_ref, c_ref, s1_ref, s2_ref, qw_ref, kw_ref, do_ref, at_ref, ls_ref,
             dq_ref, dk_ref, dv_ref, gqw_ref, gkw_ref,
             qs, ks, vs, dos, dsub, lsub, dqs, dks, dvs, dnat, qx, kx, dvn, tnq, tnk, rrq, rrk):
        hp, g = pl.program_id(0), pl.program_id(1)
        lo = lax.broadcasted_iota(jnp.int32, (1, LANES), 1) < HD
        e = _head_mat()
        both = ((q_ref, qw_ref, qx, tnq, rrq, HD ** -0.5), (k_ref, kw_ref, kx, tnk, rrk, 1.0))

        @pl.when((hp == 0) & (g == 0))
        def _():
            gqw_ref[...] = jnp.zeros_like(gqw_ref)
            gkw_ref[...] = jnp.zeros_like(gkw_ref)

        def prep(i, _):
            rows = pl.ds(pl.multiple_of(i * CH, CH), CH)
            dnat[rows, :] = _head_mean(do_ref[rows, :] * at_ref[rows, :].astype(F32), e) * float(HD)
            c, s1, s2 = c_ref[rows, :], s1_ref[rows, :], s2_ref[rows, :]
            for t_ref, w_ref, x, tn_s, rr_s, scale in both:
                t = t_ref[rows, :]
                rr = lax.rsqrt(_head_mean(t * t, e) + EPS)
                tn = t * rr
                rr_s[rows, :] = rr
                tn_s[rows, :] = tn
                x[rows, :] = _rope(tn * w_ref[...], c, s1, s2) * scale
            return 0

        lax.fori_loop(0, S // CH, prep, 0, unroll=4)

        def group(d):
            L = S // d

            ru, nb = _interleave(d)

            def stage(r, off):
                for c0 in range(0, L, CH):
                    n = min(CH, L)
                    rows = _sub_rows(r, d, c0, n)
                    dst = pl.ds(off + c0, n)
                    qs[dst, :] = qx[rows, :].astype(BF16)
                    ks[dst, :] = kx[rows, :].astype(BF16)
                    vs[dst, :] = v_ref[rows, :].astype(BF16)
                    dos[dst, :] = do_ref[rows, :].astype(BF16)
                    dsub[dst, :] = dnat[rows, :]
                    lsub[dst, :] = ls_ref[rows, :]
                    dks[dst, :] = jnp.zeros((n, LANES), F32)
                    dvs[dst, :] = jnp.zeros((n, LANES), F32)

            def one(off, i):
                W, q0, k0, valid = _band_window(i, L)
                qrows, krows = pl.ds(off + q0, TQ), pl.ds(off + k0, W)
                q2 = _stack_heads(qs[qrows, :], lo)
                do2 = _stack_heads(dos[qrows, :], lo)
                kk, vv = ks[krows, :], vs[krows, :]
                lse_b, dd_b = lsub[qrows, :], dsub[qrows, :]
                lse2 = jnp.concatenate([lse_b[:, 0:1], lse_b[:, HD:HD + 1]], axis=0)
                dd2 = jnp.concatenate([dd_b[:, 0:1], dd_b[:, HD:HD + 1]], axis=0)
                sc = jnp.where(valid, _dot_nt(q2, kk), NEG_INF)
                p = jnp.exp(sc - lse2)
                ds = (p * (_dot_nt(do2, vv) - dd2)).astype(BF16)
                dqs[qrows, :] = _unstack_heads(_dot(ds, kk), lo)
                dks[krows, :] = dks[krows, :] + _dot_tn(ds, q2)
                dvs[krows, :] = dvs[krows, :] + _dot_tn(p.astype(BF16), do2)

            def unstage(r, off):
                for c0 in range(0, L, CH):
                    n = min(CH, L)
                    rows = _sub_rows(r, d, c0, n)
                    src = pl.ds(off + c0, n)
                    qx[rows, :] = dqs[src, :]
                    kx[rows, :] = dks[src, :]
                    dvn[rows, :] = dvs[src, :]

            def step(t, _):
                for u in range(ru):
                    stage(t * ru + u, u * L)
                _for_blocks(L // TQ // nb, lambda j: [one(u * L, j * nb + b) for u in range(ru) for b in range(nb)])
                for u in range(ru):
                    unstage(t * ru + u, u * L)
                return 0

            lax.fori_loop(0, d // ru, step, 0)

        for gi, d in enumerate(DILATIONS):
            pl.when(g == gi)(functools.partial(group, d))

        def emit(i, _):
            rows = pl.ds(pl.multiple_of(i * CH, CH), CH)
            c, s1, s2 = c_ref[rows, :], s1_ref[rows, :], s2_ref[rows, :]
            for (_, w_ref, x, tn_s, rr_s, scale), out, gw_ref in zip(both, (dq_ref, dk_ref), (gqw_ref, gkw_ref)):
                tn = tn_s[rows, :]
                dy = _rope_t(x[rows, :] * scale, c, s1, s2)
                gw_ref[0:1, :] = gw_ref[0:1, :] + jnp.sum(dy * tn, axis=0, keepdims=True)
                dtn = dy * w_ref[...]
                out[rows, :] = (rr_s[rows, :] * (dtn - tn * _head_mean(dtn * tn, e))).astype(BF16)
            dv_ref[rows, :] = dvn[rows, :].astype(BF16)
            return 0

        lax.fori_loop(0, S // CH, emit, 0, unroll=4)

    nat_spec = pl.BlockSpec((S, LANES), lambda hp, g: (0, hp))
    out_spec = pl.BlockSpec((None, S, LANES), lambda hp, g: (g, 0, hp))
    acc_spec = pl.BlockSpec((8, LANES), lambda hp, g: (0, 0))
    return _call(
        body, sides, name="attn_bwd", grid=(4, 3),
        in_specs=_qk_specs() + _tab_specs() + [_vec_spec(), _vec_spec(), nat_spec, nat_spec, nat_spec],
        out_specs=[out_spec] * 3 + [acc_spec] * 2,
        out_shape=[jax.ShapeDtypeStruct((QKV // PLANE, S, PLANE), BF16)] * 3 + [jax.ShapeDtypeStruct((8, LANES), F32)] * 2,
        scratch_shapes=[pltpu.VMEM((S, LANES), BF16)] * 4 + [pltpu.VMEM((S, LANES), F32)] * 13,
        args=(proj, proj, proj, *tabs, qw2, kw2, d_attn, attn, lse))


PADR = 16
CT = 128


def _conv_specs():
    return [pl.BlockSpec((S, CC), lambda i: (0, OFF_CA // CC)), pl.BlockSpec((S, CC), lambda i: (0, OFF_CB // CC))]


NCB = CC // LANES


def _pad_zero(pad):
    for cb in range(NCB):
        pad[cb, 0:PADR, :] = jnp.zeros((PADR, LANES), F32)
        pad[cb, PADR + S:PADR + S + PADR, :] = jnp.zeros((PADR, LANES), F32)


def _pad_store(pad, row0, n, val):
    for cb in range(NCB):
        pad[cb, pl.ds(pl.multiple_of(row0 + PADR, 8), n), :] = val[:, cb * LANES:(cb + 1) * LANES]


def _taps(pad_ref, cb, s0, weights):
    acc = jnp.zeros((CT, LANES), F32)
    for k in range(KW):
        acc = acc + weights[k] * pad_ref[cb, pl.ds(s0 + k + 1, CT), :]
    return acc


def conv_fwd(proj, conv_w, conv_b, ln_w, ln_b, sides=()):
    def body(a_ref, b_ref, w_ref, cb_ref, lw_ref, lb_ref, c_ref, u3_ref, upad):
        _pad_zero(upad)

        def glu(i, _):
            rows = pl.ds(pl.multiple_of(i * TM, TM), TM)
            _pad_store(upad, i * TM, TM, a_ref[rows, :] * _sigmoid(b_ref[rows, :]))
            return 0

        lax.fori_loop(0, S // TM, glu, 0)

        def chunk(i, _):
            s0 = pl.multiple_of(i * CT, CT)
            for cb in range(CC // LANES):
                cols = slice(cb * LANES, (cb + 1) * LANES)
                w = [w_ref[k:k + 1, cols] for k in range(KW)]
                c_ref[pl.ds(s0, CT), cols] = _taps(upad, cb, s0, w) + cb_ref[:, cols]
            cv = c_ref[pl.ds(s0, CT), :]
            mu = jnp.mean(cv, axis=-1, keepdims=True)
            xc = cv - mu
            rstd = lax.rsqrt(jnp.mean(xc * xc, axis=-1, keepdims=True) + EPS)
            yl = xc * rstd * lw_ref[...] + lb_ref[...]
            u3_ref[pl.ds(s0, CT), :] = (yl * _sigmoid(yl)).astype(BF16)
            return 0

        lax.fori_loop(0, S // CT, chunk, 0)

    vec = pl.BlockSpec((1, CC), lambda i: (0, 0))
    full = pl.BlockSpec((S, CC), lambda i: (0, 0))
    return _call(
        body, sides, name="conv_fwd", grid=(1,),
        in_specs=_conv_specs() + [pl.BlockSpec((KW, CC), lambda i: (0, 0)), vec, vec, vec],
        out_specs=[full, full],
        out_shape=[jax.ShapeDtypeStruct((S, CC), F32), jax.ShapeDtypeStruct((S, CC), BF16)],
        scratch_shapes=[pltpu.VMEM((NCB, S + 2 * PADR, LANES), F32)],
        args=(proj, proj, conv_w, conv_b, ln_w, ln_b))


def conv_bwd(proj, cpre, d_u3, conv_w, conv_w_rev, ln_w, ln_b, sides=()):
    def body(a_ref, b_ref, c_ref, du3_ref, w_ref, wr_ref, lw_ref, lb_ref,
             dc_ref, gw_ref, gcb_ref, glw_ref, glb_ref, upad, dpad):
        _pad_zero(upad)
        _pad_zero(dpad)
        gw_ref[...] = jnp.zeros_like(gw_ref)

        def ln_bwd(i, carry):
            gcb, glw, glb = carry
            rows = pl.ds(pl.multiple_of(i * TM, TM), TM)
            _pad_store(upad, i * TM, TM, a_ref[rows, :] * _sigmoid(b_ref[rows, :]))
            cv = c_ref[rows, :]
            mu = jnp.mean(cv, axis=-1, keepdims=True)
            xc = cv - mu
            rstd = lax.rsqrt(jnp.mean(xc * xc, axis=-1, keepdims=True) + EPS)
            xh = xc * rstd
            yl = xh * lw_ref[...] + lb_ref[...]
            dyl = du3_ref[rows, :] * _dsilu(yl, _sigmoid(yl))
            dxh = dyl * lw_ref[...]
            dcv = rstd * (dxh - jnp.mean(dxh, axis=-1, keepdims=True)
                          - xh * jnp.mean(dxh * xh, axis=-1, keepdims=True))
            _pad_store(dpad, i * TM, TM, dcv)
            return (gcb + jnp.sum(dcv, axis=0, keepdims=True),
                    glw + jnp.sum(dyl * xh, axis=0, keepdims=True),
                    glb + jnp.sum(dyl, axis=0, keepdims=True))

        z = jnp.zeros((1, CC), F32)
        gcb, glw, glb = lax.fori_loop(0, S // TM, ln_bwd, (z, z, z))
        gcb_ref[...] = gcb
        glw_ref[...] = glw
        glb_ref[...] = glb

        def chunk(i, _):
            s0 = pl.multiple_of(i * CT, CT)
            for cb in range(CC // LANES):
                cols = slice(cb * LANES, (cb + 1) * LANES)
                wr = [wr_ref[k:k + 1, cols] for k in range(KW)]
                du = _taps(dpad, cb, s0, wr)
                dcv = dpad[cb, pl.ds(s0 + PADR, CT), :]
                for k in range(KW):
                    gw_ref[k:k + 1, cols] = gw_ref[k:k + 1, cols] + jnp.sum(
                        upad[cb, pl.ds(s0 + k + 1, CT), :] * dcv, axis=0, keepdims=True)
                av = a_ref[pl.ds(s0, CT), cols]
                sb = _sigmoid(b_ref[pl.ds(s0, CT), cols])
                dc_ref[0, pl.ds(s0, CT), cols] = (du * sb).astype(BF16)
                dc_ref[1, pl.ds(s0, CT), cols] = (du * av * sb * (1.0 - sb)).astype(BF16)
            return 0

        lax.fori_loop(0, S // CT, chunk, 0)

    vec = pl.BlockSpec((1, CC), lambda i: (0, 0))
    full = pl.BlockSpec((S, CC), lambda i: (0, 0))
    wsp = pl.BlockSpec((KW, CC), lambda i: (0, 0))
    return _call(
        body, sides, name="conv_bwd", grid=(1,),
        in_specs=_conv_specs() + [full, full, wsp, wsp, vec, vec],
        out_specs=[pl.BlockSpec((2, S, CC), lambda i: (0, 0, 0)), wsp, vec, vec, vec],
        out_shape=[jax.ShapeDtypeStruct((2, S, CC), BF16), jax.ShapeDtypeStruct((KW, CC), F32)]
        + [jax.ShapeDtypeStruct((1, CC), F32)] * 3,
        scratch_shapes=[pltpu.VMEM((NCB, S + 2 * PADR, LANES), F32)] * 2,
        args=(proj, proj, cpre, d_u3, conv_w, conv_w_rev, ln_w, ln_b))


def _gate_specs():
    return [_row(CC, col=OFF_GA // CC + j) for j in range(4)]


def _gates(g_refs, bg_ref):
    ga = _sigmoid(jnp.concatenate([g_refs[0][...], g_refs[1][...]], axis=1) + bg_ref[0:1, :])
    gb = _sigmoid(jnp.concatenate([g_refs[2][...], g_refs[3][...]], axis=1) + bg_ref[1:2, :])
    return ga, gb


def mix_out(x, proj, b_gate, attn, u3, w_o, w_pw, w_out):
    def body(x_ref, g0, g1, g2, g3, bg_ref, at_ref, u3_ref, wo_ref, wp_ref, wout_ref,
             x1_ref, z_ref, ya_ref, yb_ref):
        ga, gb = _gates((g0, g1, g2, g3), bg_ref)
        ya = _dot(at_ref[...], wo_ref[...])
        yb = _dot(u3_ref[...], wp_ref[...])
        z = (ga * ya + gb * yb).astype(BF16)
        ya_ref[...] = ya.astype(BF16)
        yb_ref[...] = yb.astype(BF16)
        z_ref[...] = z
        x1_ref[...] = x_ref[...] + _dot(z, wout_ref[...])

    return pl.pallas_call(
        body, name="mix_out", grid=(S // TM,),
        in_specs=[_row(D)] + _gate_specs() + [_res((2, D)), _row(CC), _row(CC),
                                              _res((CC, D)), _res((CC, D)), _res((D, D))],
        out_specs=[_row(D)] * 4,
        out_shape=[jax.ShapeDtypeStruct((S, D), F32)] + [jax.ShapeDtypeStruct((S, D), BF16)] * 3,
        compiler_params=_cp(dimension_semantics=("arbitrary",)),
    )(x, proj, proj, proj, proj, b_gate, attn, u3, w_o, w_pw, w_out)


def out_bwd(d_x1b, proj, b_gate, ya, yb, w_o, w_pw, w_out, sides=()):
    def body(dx_ref, g0, g1, g2, g3, bg_ref, ya_ref, yb_ref, wo_ref, wp_ref, wout_ref,
             dya_ref, dyb_ref, dgl_ref, dat_ref, du3_ref, gbg_ref):
        @pl.when(pl.program_id(0) == 0)
        def _():
            gbg_ref[...] = jnp.zeros_like(gbg_ref)

        ga, gb = _gates((g0, g1, g2, g3), bg_ref)
        dz = _dot_nt(dx_ref[...], wout_ref[...])
        dya = (dz * ga).astype(BF16)
        dyb = (dz * gb).astype(BF16)
        dgla = dz * ya_ref[...].astype(F32) * ga * (1.0 - ga)
        dglb = dz * yb_ref[...].astype(F32) * gb * (1.0 - gb)
        dya_ref[...] = dya
        dyb_ref[...] = dyb
        for j in range(2):
            dgl_ref[j] = dgla[:, j * PLANE:(j + 1) * PLANE].astype(BF16)
            dgl_ref[2 + j] = dglb[:, j * PLANE:(j + 1) * PLANE].astype(BF16)
        gbg_ref[0:1, :] = gbg_ref[0:1, :] + jnp.sum(dgla, axis=0, keepdims=True)
        gbg_ref[1:2, :] = gbg_ref[1:2, :] + jnp.sum(dglb, axis=0, keepdims=True)
        dat_ref[...] = _dot_nt(dya, wo_ref[...])
        du3_ref[...] = _dot_nt(dyb, wp_ref[...])

    return _call(
        body, sides, name="out_bwd", grid=(S // TM,),
        in_specs=[_row(D)] + _gate_specs() + [_res((2, D)), _row(D), _row(D),
                                              _res((CC, D)), _res((CC, D)), _res((D, D))],
        out_specs=[_row(D), _row(D), _planes(2 * D), _row(CC), _row(CC), pl.BlockSpec((2, D), lambda i: (0, 0))],
        out_shape=[jax.ShapeDtypeStruct((S, D), BF16)] * 2 + [jax.ShapeDtypeStruct((2 * D // PLANE, S, PLANE), BF16)]
        + [jax.ShapeDtypeStruct((S, CC), F32)] * 2 + [jax.ShapeDtypeStruct((2, D), F32)],
        args=(d_x1b, proj, proj, proj, proj, b_gate, ya, yb, w_o, w_pw, w_out))


def ffn_in(x1, norm_w, w_ffn_in, sides=()):
    half = FF // 2

    def body(x_ref, nw_ref, w_ref, h_ref, gu_ref, f_ref):
        xv = x_ref[...]
        r = lax.rsqrt(jnp.mean(xv * xv, axis=-1, keepdims=True) + EPS)
        h = (xv * r * nw_ref[...]).astype(BF16)
        h_ref[...] = h
        for j in range(2):
            gt = _dot_nt(h, w_ref[j * half:(j + 1) * half, :])
            up = _dot_nt(h, w_ref[FF + j * half:FF + (j + 1) * half, :])
            gu_ref[:, j * half:(j + 1) * half] = gt.astype(BF16)
            gu_ref[:, FF + j * half:FF + (j + 1) * half] = up.astype(BF16)
            f_ref[:, j * half:(j + 1) * half] = (gt * _sigmoid(gt) * up).astype(BF16)

    return _call(
        body, sides, name="ffn_in", grid=(S // TM,),
        in_specs=[_row(D), _res((1, D)), _res((2 * FF, D))],
        out_specs=[_row(D), _row(2 * FF), _row(FF)],
        out_shape=[jax.ShapeDtypeStruct((S, D), BF16), jax.ShapeDtypeStruct((S, 2 * FF), BF16),
                   jax.ShapeDtypeStruct((S, FF), BF16)],
        args=(x1, norm_w, w_ffn_in))


def ffn_out_loss(x1, f, w_ffn_out, target):
    def body(x_ref, f_ref, w_ref, t_ref, dy_ref, dyb_ref, sq_ref):
        @pl.when(pl.program_id(0) == 0)
        def _():
            sq_ref[...] = jnp.zeros_like(sq_ref)

        diff = x_ref[...] + _dot(f_ref[...], w_ref[...]) - t_ref[...]
        dy = diff * (1.0 / D)
        dy_ref[...] = dy
        dyb_ref[...] = dy.astype(BF16)
        sq_ref[...] = sq_ref[...] + jnp.sum((diff * diff).reshape(TM // 8, 8, D), axis=0)

    return pl.pallas_call(
        body, name="ffn_out_loss", grid=(S // TM,),
        in_specs=[_row(D), _row(FF), _res((FF, D)), _row(D)],
        out_specs=[_row(D), _row(D), pl.BlockSpec((8, D), lambda i: (0, 0))],
        out_shape=[jax.ShapeDtypeStruct((S, D), F32), jax.ShapeDtypeStruct((S, D), BF16),
                   jax.ShapeDtypeStruct((8, D), F32)],
        compiler_params=_cp(dimension_semantics=("arbitrary",)),
    )(x1, f, w_ffn_out, target)


def _rms_bwd(xv, nw, dh):
    r = lax.rsqrt(jnp.mean(xv * xv, axis=-1, keepdims=True) + EPS)
    xn = xv * r
    dxn = dh * nw
    dx = r * (dxn - xn * jnp.mean(dxn * xn, axis=-1, keepdims=True))
    return dx, dh * xn


def ffn_bwd(dy, dyb, gu, x1, norm_w, w_ffn_in, w_ffn_out, sides=()):
    def body(dy_ref, dyb_ref, gu_ref, x_ref, nw_ref, wi_ref, wo_ref, dgu_ref, dx_ref, dxb_ref, gn_ref):
        @pl.when(pl.program_id(0) == 0)
        def _():
            gn_ref[...] = jnp.zeros_like(gn_ref)

        df = _dot_nt(dyb_ref[...], wo_ref[...])
        gt = gu_ref[:, 0:FF].astype(F32)
        up = gu_ref[:, FF:2 * FF].astype(F32)
        sg = _sigmoid(gt)
        dgt = (df * up * _dsilu(gt, sg)).astype(BF16)
        dup = (df * gt * sg).astype(BF16)
        dgu_ref[:, 0:FF] = dgt
        dgu_ref[:, FF:2 * FF] = dup
        dh = _dot(dgt, wi_ref[0:FF, :]) + _dot(dup, wi_ref[FF:2 * FF, :])
        dxn, gw = _rms_bwd(x_ref[...], nw_ref[...], dh)
        dx = dy_ref[...] + dxn
        dx_ref[...] = dx
        dxb_ref[...] = dx.astype(BF16)
        gn_ref[...] = gn_ref[...] + jnp.sum(gw, axis=0, keepdims=True)

    return _call(
        body, sides, name="ffn_bwd", grid=(S // TM,),
        in_specs=[_row(D), _row(D), _row(2 * FF), _row(D), _res((1, D)), _res((2 * FF, D)), _res((FF, D))],
        out_specs=[_row(2 * FF), _row(D), _row(D), pl.BlockSpec((1, D), lambda i: (0, 0))],
        out_shape=[jax.ShapeDtypeStruct((S, 2 * FF), BF16), jax.ShapeDtypeStruct((S, D), F32),
                   jax.ShapeDtypeStruct((S, D), BF16), jax.ShapeDtypeStruct((1, D), F32)],
        args=(dy, dyb, gu, x1, norm_w, w_ffn_in, w_ffn_out))


def in_bwd(d_q, d_k, d_v, d_conv, d_gl, w_in, x, d_x1, norm_w, sides=()):
    segs = ((OFF_Q, QKV), (OFF_K, QKV), (OFF_V, QKV), (OFF_CA, 2 * CC), (OFF_GA, 2 * D))

    def body(dq_ref, dk_ref, dv_ref, dc_ref, dg_ref, w_ref, x_ref, dx1_ref, nw_ref, gx_ref, gn_ref):
        @pl.when(pl.program_id(0) == 0)
        def _():
            gn_ref[...] = jnp.zeros_like(gn_ref)

        dh = jnp.zeros((TM, D), F32)
        for ref, (off, width) in zip((dq_ref, dk_ref, dv_ref, dc_ref, dg_ref), segs):
            for j in range(width // PLANE):
                dh = dh + _dot(ref[j], w_ref[off + j * PLANE:off + (j + 1) * PLANE, :])
        dxn, gw = _rms_bwd(x_ref[...], nw_ref[...], dh)
        gx_ref[...] = dx1_ref[...] + dxn
        gn_ref[...] = gn_ref[...] + jnp.sum(gw, axis=0, keepdims=True)

    return _call(
        body, sides, name="in_bwd", grid=(S // TM,),
        in_specs=[_planes(QKV)] * 3 + [_planes(2 * CC), _planes(2 * D), _res((INW, D)), _row(D), _row(D), _res((1, D))],
        out_specs=[_row(D), pl.BlockSpec((1, D), lambda i: (0, 0))],
        out_shape=[jax.ShapeDtypeStruct((S, D), F32), jax.ShapeDtypeStruct((1, D), F32)],
        args=(d_q, d_k, d_v, d_conv, d_gl, w_in, x, d_x1, norm_w))


def mm_tn(name, a, b, tm, tn, sides=()):
    M, N = a.shape[1], b.shape[1]

    def body(a_ref, b_ref, o_ref):
        o_ref[...] = _dot_tn(a_ref[...], b_ref[...])

    res = _call(
        body, sides, name=name, grid=(M // tm, N // tn),
        in_specs=[pl.BlockSpec((S, tm), lambda i, j: (0, i)), pl.BlockSpec((S, tn), lambda i, j: (0, j))],
        out_specs=[pl.BlockSpec((tm, tn), lambda i, j: (i, j))],
        out_shape=[jax.ShapeDtypeStruct((M, N), F32)],
        args=(a, b))
    return (res[0][0], res[1]) if sides else res[0]


GW_IN_TN = PLANE
GW_IN_SPLIT = (768, 256)


def gw_in_t(name, ht, d_segs, col0, hw, sides=()):
    tn = GW_IN_TN
    starts, t0 = [], 0
    for seg in d_segs:
        starts.append(t0)
        t0 += seg.shape[0]
    ntiles = [seg.shape[0] for seg in d_segs]

    def body(h_ref, *refs):
        a_refs, o_ref = refs[:-1], refs[-1]
        n = pl.program_id(0)
        for a_ref, st, nt in zip(a_refs, starts, ntiles):
            @pl.when((n >= st) & (n < st + nt))
            def _(a_ref=a_ref):
                o_ref[...] = _dot(h_ref[...], a_ref[...]).T

    def seg_spec(st, nt):
        return pl.BlockSpec((None, S, tn), lambda n: (jnp.clip(n - st, 0, nt - 1), 0, 0))

    res = _call(
        body, sides, name=name, grid=(INW // tn,),
        in_specs=[pl.BlockSpec((hw, S), lambda n: (col0 // hw, 0))] + [seg_spec(st, nt) for st, nt in zip(starts, ntiles)],
        out_specs=[pl.BlockSpec((tn, hw), lambda n: (n, 0))],
        out_shape=[jax.ShapeDtypeStruct((INW, hw), F32)],
        args=(ht, *d_segs))
    return (res[0][0], res[1]) if sides else res[0]


def _place():
    x, y, c = lax.axis_index("x"), lax.axis_index("y"), lax.axis_index("c")
    chips = [(1 - x, y), (x, 1 - y), (1 - x, 1 - y)]
    return x, y, c, chips


def _sems(n):
    return pltpu.SemaphoreType.DMA((n,))


def _remote(src, dst, send, recv, k, to):
    return pltpu.make_async_remote_copy(src_ref=src, dst_ref=dst, send_sem=send.at[k], recv_sem=recv.at[k],
                                        device_id=to, device_id_type=MESH)


def _cast_rows(dst, src, cols=slice(None)):
    rows = src.shape[0]
    step = next((s for s in (128, 64, 32, 16) if rows % s == 0), rows)
    for r0 in range(0, rows, step):
        dst[r0:r0 + step, cols] = src[r0:r0 + step, :].astype(dst.dtype)


def own_result(name, a):
    def body(a_ref, o_ref):
        pass

    return pl.pallas_call(body, name=name, in_specs=[ANY], out_specs=ANY,
                          out_shape=jax.ShapeDtypeStruct(a.shape, a.dtype), input_output_aliases={0: 0})(a)


def comm_only(name, sides):
    def body():
        pass

    return _call(body, sides, name=name, grid=(1,), in_specs=[], out_specs=[], out_shape=[], args=())[1]


def ag_blocks(shard, dtype):
    R, W = shard.shape

    def copy(outs, scr, k, block, to, src=None):
        dst = outs[0].at[block]
        return _remote(dst if src is None else src, dst, scr[1], scr[2], k, to)

    def local(outs, scr, me):
        return pltpu.make_async_copy(scr[0], outs[0].at[me], scr[3].at[0])

    def start(ins, outs, scr):
        x, y, c, chips = _place()
        me = 4 * x + 2 * y + c
        _cast_rows(scr[0], ins[0])
        local(outs, scr, me).start()
        copy(outs, scr, 0, me, (x, y, 1 - c), src=scr[0]).start()
        for j, (cx, cy) in enumerate(chips):
            copy(outs, scr, 1 + j, me, (cx, cy, c), src=scr[0]).start()

    def finish(ins, outs, scr):
        x, y, c, chips = _place()
        me, sib = 4 * x + 2 * y + c, (x, y, 1 - c)
        passed = []
        for j, (cx, cy) in enumerate(chips):
            theirs = 4 * cx + 2 * cy + c
            copy(outs, scr, 1 + j, theirs, (x, y, c)).wait_recv()
            fwd = copy(outs, scr, 4 + j, theirs, sib)
            fwd.start()
            passed.append(fwd)
        copy(outs, scr, 0, 4 * x + 2 * y + 1 - c, (x, y, c)).wait_recv()
        for j, (cx, cy) in enumerate(chips):
            copy(outs, scr, 4 + j, 4 * cx + 2 * cy + 1 - c, (x, y, c)).wait_recv()
        copy(outs, scr, 0, me, sib, src=scr[0]).wait_send()
        for j, (cx, cy) in enumerate(chips):
            copy(outs, scr, 1 + j, me, (cx, cy, c), src=scr[0]).wait_send()
        for fwd in passed:
            fwd.wait_send()
        local(outs, scr, me).wait()

    return Side((shard,), (VMEM,), (jax.ShapeDtypeStruct((NDEV, R, W), dtype),),
                (pltpu.VMEM((R, W), dtype), _sems(7), _sems(7), _sems(1)), start, finish, None, "dsxy")


def ag_blocks_relay(shard, dtype):
    R, W = shard.shape
    half = R // 2

    def copy(outs, scr, k, block, to, src=None, rows=None):
        dst = outs[0].at[block] if rows is None else outs[0].at[block, pl.ds(rows * half, half), :]
        return _remote(dst if src is None else src, dst, scr[1], scr[2], k, to)

    def local(outs, scr, me):
        return pltpu.make_async_copy(scr[0], outs[0].at[me], scr[3].at[0])

    def own(outs, scr):
        x, y, c, _ = _place()
        me = 4 * x + 2 * y + c
        return [copy(outs, scr, k, me, to, src=scr[0])
                for k, to in enumerate([(x, y, 1 - c), (1 - x, y, c), (x, 1 - y, c)])]

    def start(ins, outs, scr):
        x, y, c, _ = _place()
        _cast_rows(scr[0], ins[0])
        local(outs, scr, 4 * x + 2 * y + c).start()
        for cp in own(outs, scr):
            cp.start()

    def passed_on(outs, scr):
        x, y, c, _ = _place()
        sib, xn, yn = (x, y, 1 - c), (1 - x, y, c), (x, 1 - y, c)
        b_xn, b_yn, b_dg = 4 * (1 - x) + 2 * y + c, 4 * x + 2 * (1 - y) + c, 4 * (1 - x) + 2 * (1 - y) + c
        near = [copy(outs, scr, 5, b_xn, yn, rows=0), copy(outs, scr, 3, b_xn, sib),
                copy(outs, scr, 6, b_yn, xn, rows=1), copy(outs, scr, 4, b_yn, sib)]
        far = [copy(outs, scr, 7, b_dg, sib, rows=0), copy(outs, scr, 8, b_dg, sib, rows=1)]
        return (b_xn, b_yn, b_dg), near, far

    def mid(ins, outs, scr):
        x, y, c, _ = _place()
        (b_xn, b_yn, _), near, _ = passed_on(outs, scr)
        copy(outs, scr, 1, b_xn, (x, y, c)).wait_recv()
        near[0].start()
        near[1].start()
        copy(outs, scr, 2, b_yn, (x, y, c)).wait_recv()
        near[2].start()
        near[3].start()

    def finish(ins, outs, scr):
        x, y, c, _ = _place()
        here = (x, y, c)
        (b_xn, b_yn, b_dg), near, far = passed_on(outs, scr)
        copy(outs, scr, 5, b_dg, here, rows=0).wait_recv()
        far[0].start()
        copy(outs, scr, 6, b_dg, here, rows=1).wait_recv()
        far[1].start()
        flip = 1 - 2 * c
        copy(outs, scr, 0, 4 * x + 2 * y + 1 - c, here).wait_recv()
        copy(outs, scr, 3, b_xn + flip, here).wait_recv()
        copy(outs, scr, 4, b_yn + flip, here).wait_recv()
        copy(outs, scr, 7, b_dg + flip, here, rows=0).wait_recv()
        copy(outs, scr, 8, b_dg + flip, here, rows=1).wait_recv()
        for cp in own(outs, scr) + near + far:
            cp.wait_send()
        local(outs, scr, 4 * x + 2 * y + c).wait()

    return Side((shard,), (VMEM,), (jax.ShapeDtypeStruct((NDEV, R, W), dtype),),
                (pltpu.VMEM((R, W), dtype), _sems(9), _sems(9), _sems(1)), start, finish, mid, "sxy")


def ag_cols(shard):
    K, C = shard.shape
    half, w2 = K // 2, 2 * C

    def win(out, rows_c, chip):
        return out.at[pl.ds(pl.multiple_of(rows_c * half, 16), half), pl.ds(pl.multiple_of(chip * w2, LANES), w2)]

    def ici(outs, scr, j, to, c, k):
        slab, send, recv = scr[2], scr[5], scr[6]
        return _remote(slab.at[pl.ds(pl.multiple_of(c * half, 16), half), :], win(outs[0], c, k), send, recv, j, to)

    def local(outs, scr, k):
        return pltpu.make_async_copy(scr[2], outs[0].at[:, pl.ds(pl.multiple_of(k * w2, LANES), w2)], scr[7].at[0])

    def start(ins, outs, scr):
        stage, inbox, slab, xs, xr = scr[:5]
        x, y, c, chips = _place()
        k = 2 * x + y
        _cast_rows(stage, ins[0])
        swap = _remote(stage, inbox, xs, xr, 0, (x, y, 1 - c))
        swap.start()
        for cc in range(2):
            @pl.when(c == cc)
            def _(cc=cc):
                _cast_rows(slab, stage, slice(cc * C, (cc + 1) * C))
        swap.wait()
        for cc in range(2):
            @pl.when(c == cc)
            def _(cc=cc):
                _cast_rows(slab, inbox, slice((1 - cc) * C, (2 - cc) * C))
        local(outs, scr, k).start()
        for j, (cx, cy) in enumerate(chips):
            ici(outs, scr, j, (cx, cy, c), c, k).start()

    def finish(ins, outs, scr):
        send, recv = scr[5], scr[6]
        x, y, c, chips = _place()
        k, sib = 2 * x + y, (x, y, 1 - c)
        passed = []
        for j, (cx, cy) in enumerate(chips):
            w = win(outs[0], c, 2 * cx + cy)
            _remote(w, w, send, recv, j, sib).wait_recv()
            fwd = _remote(w, w, send, recv, 3 + j, sib)
            fwd.start()
            passed.append(fwd)
        for j, (cx, cy) in enumerate(chips):
            w = win(outs[0], 1 - c, 2 * cx + cy)
            _remote(w, w, send, recv, 3 + j, sib).wait_recv()
        for j, (cx, cy) in enumerate(chips):
            ici(outs, scr, j, (cx, cy, c), c, k).wait_send()
        for fwd in passed:
            fwd.wait_send()
        local(outs, scr, k).wait()

    return Side((shard,), (VMEM,), (jax.ShapeDtypeStruct((K, NDEV * C), BF16),),
                (pltpu.VMEM((K, C), BF16), pltpu.VMEM((K, C), BF16), pltpu.VMEM((K, w2), BF16),
                 _sems(1), _sems(1), _sems(6), _sems(6), _sems(1)), start, finish, None, "dsxy")


def copies_side(args, out_shape, n_copies, plan, peers):
    def copies(ins, outs, scr):
        return [_remote(s_, d_, scr[0], scr[1], i, to) for i, (s_, d_, to) in enumerate(plan(ins, outs))]

    def start(ins, outs, scr):
        for cp in copies(ins, outs, scr):
            cp.start()

    def finish(ins, outs, scr):
        for cp in copies(ins, outs, scr):
            cp.wait()

    return Side(tuple(args), (ANY,) * len(args), tuple(out_shape), (_sems(n_copies), _sems(n_copies)),
                start, finish, None, peers)


def rs_to_sibling(grads):
    out_shape = [jax.ShapeDtypeStruct((4,) + g.shape[1:] if kind == "rows" else (g.shape[0] // 2, g.shape[1]), F32)
                 for kind, g in grads]

    def plan(ins, outs):
        x, y, c, _ = _place()
        sib, res = (x, y, 1 - c), []
        for (kind, _), g, r in zip(grads, ins, outs):
            if kind == "rows":
                res += [(g.at[2 * k + 1 - c], r.at[k], sib) for k in range(4)]
            else:
                half = g.shape[0] // 2
                res.append((g.at[pl.ds(pl.multiple_of((1 - c) * half, 8), half), :], r, sib))
        return res

    return copies_side([g for _, g in grads], out_shape, sum(4 if kind == "rows" else 1 for kind, _ in grads), plan, "s")


def rs_to_chips(parts):
    out_shape = [jax.ShapeDtypeStruct((3,) + p.shape[1:] if kind == "rows" else (3, p.shape[0], p.shape[1] // 4), BF16)
                 for kind, p in parts]

    def plan(ins, outs):
        x, y, c, chips = _place()
        res = []
        for (kind, _), p, r in zip(parts, ins, outs):
            for j, (cx, cy) in enumerate(chips):
                if kind == "rows":
                    src = p.at[2 * cx + cy]
                else:
                    w2 = p.shape[1] // 4
                    src = p.at[:, pl.ds(pl.multiple_of((2 * cx + cy) * w2, LANES), w2)]
                res.append((src, r.at[j], (cx, cy, c)))
        return res

    return copies_side([p for _, p in parts], out_shape, 3 * len(parts), plan, "dxy")


def rs_swap_halves(theirs):
    def plan(ins, outs):
        x, y, c, _ = _place()
        return [(t, r, (x, y, 1 - c)) for t, r in zip(ins, outs)]

    return copies_side(theirs, [jax.ShapeDtypeStruct(t.shape, F32) for t in theirs], len(theirs), plan, "s")


ADAM_TILE_BYTES = 3 * 512 * 1024


def _row_tiles(rows, width):
    return 2 if rows % 32 == 0 and rows * width * 4 > ADAM_TILE_BYTES else 1


def chip_sum(name, grad, recv, c_idx, chip_idx):
    _, R, C = grad.shape
    nt = 1
    tr = R // nt

    def body(s_ref, g_ref, r_ref, p_ref, own_ref):
        k = pl.program_id(1)
        tot = g_ref[0] + r_ref[0]
        p_ref[0] = tot.astype(BF16)

        @pl.when(k == s_ref[1])
        def _():
            own_ref[...] = tot

    grid_spec = pltpu.PrefetchScalarGridSpec(
        num_scalar_prefetch=1, grid=(nt, 4),
        in_specs=[pl.BlockSpec((1, tr, C), lambda i, k, s: (2 * k + s[0], i, 0)),
                  pl.BlockSpec((1, tr, C), lambda i, k, s: (k, i, 0))],
        out_specs=[pl.BlockSpec((1, tr, C), lambda i, k, s: (k, i, 0)),
                   pl.BlockSpec((tr, C), lambda i, k, s: (i, 0))])
    return pl.pallas_call(
        body, name=name, grid_spec=grid_spec,
        out_shape=[jax.ShapeDtypeStruct((4, R, C), BF16), jax.ShapeDtypeStruct((R, C), F32)],
        compiler_params=_cp(dimension_semantics=("arbitrary", "arbitrary")),
    )(jnp.stack([c_idx, chip_idx]), grad, recv)


def _half_tiles(half):
    return 2 if half >= 512 else 1


def chip_sum_cols(name, grad, recv, c_idx, chip_idx):
    K, W = grad.shape
    half, w2 = K // 2, W // 4
    nt = _half_tiles(half)
    tr = half // nt

    def body(s_ref, g_ref, r_ref, p_ref, own_ref):
        tot = g_ref[...] + r_ref[...]
        p_ref[...] = tot.astype(BF16)

        @pl.when(pl.program_id(1) == s_ref[1])
        def _():
            own_ref[...] = tot

    grid_spec = pltpu.PrefetchScalarGridSpec(
        num_scalar_prefetch=1, grid=(nt, 4),
        in_specs=[pl.BlockSpec((tr, w2), lambda i, k, s: (s[0] * nt + i, k)),
                  pl.BlockSpec((tr, w2), lambda i, k, s: (i, k))],
        out_specs=[pl.BlockSpec((tr, w2), lambda i, k, s: (i, k)),
                   pl.BlockSpec((tr, w2), lambda i, k, s: (i, 0))])
    return pl.pallas_call(
        body, name=name, grid_spec=grid_spec,
        out_shape=[jax.ShapeDtypeStruct((half, W), BF16), jax.ShapeDtypeStruct((half, w2), F32)],
        compiler_params=_cp(dimension_semantics=("arbitrary", "arbitrary")),
    )(jnp.stack([c_idx, chip_idx]), grad, recv)


def col_final(name, own, recv, c_idx):
    half, w2 = own.shape
    C = w2 // 2
    nt = _half_tiles(half)
    tr = half // nt

    def body(s_ref, o_ref, r_ref, mine_ref, theirs_ref, t_ref):
        t_ref[...] = o_ref[...] + r_ref[0].astype(F32) + r_ref[1].astype(F32) + r_ref[2].astype(F32)
        for cc in range(2):
            @pl.when(s_ref[0] == cc)
            def _(cc=cc):
                mine_ref[...] = t_ref[:, cc * C:(cc + 1) * C]
                theirs_ref[...] = t_ref[:, (1 - cc) * C:(2 - cc) * C]

    grid_spec = pltpu.PrefetchScalarGridSpec(
        num_scalar_prefetch=1, grid=(nt,),
        in_specs=[pl.BlockSpec((tr, w2), lambda i, s: (i, 0)), pl.BlockSpec((3, tr, w2), lambda i, s: (0, i, 0))],
        out_specs=[pl.BlockSpec((tr, C), lambda i, s: (i, 0))] * 2,
        scratch_shapes=[pltpu.VMEM((tr, w2), F32)])
    return pl.pallas_call(
        body, name=name, grid_spec=grid_spec, out_shape=[jax.ShapeDtypeStruct((half, C), F32)] * 2,
        compiler_params=_cp(dimension_semantics=("arbitrary",)),
    )(jnp.stack([c_idx]), own, recv)


def _adamw(w, g, m, v):
    m2 = ADAM_B1 * m + (1.0 - ADAM_B1) * g
    v2 = ADAM_B2 * v + (1.0 - ADAM_B2) * (g * g)
    m_hat = m2 / (1.0 - ADAM_B1 ** ADAM_STEP)
    v_hat = v2 / (1.0 - ADAM_B2 ** ADAM_STEP)
    delta = -ADAM_LR * (m_hat / (jnp.sqrt(v_hat) + ADAM_EPS) + ADAM_WD * w)
    return delta, m2, v2


def shard_adam(name, owns, recvs, w, m, v):
    n = len(owns)
    R = owns[0].shape[0]
    ct = min(o.shape[1] for o in owns)
    first = [sum(o.shape[1] for o in owns[:j]) // ct for j in range(n)]
    count = [o.shape[1] // ct for o in owns]
    nt = _row_tiles(R, ct)
    tr = R // nt

    def body(*refs):
        o_refs, r_refs = refs[:n], refs[n:2 * n]
        w_ref, m_ref, v_ref, g_ref, d_ref, nm_ref, nv_ref = refs[2 * n:]
        g = None
        for j in range(n):
            gj = o_refs[j][...] + r_refs[j][0].astype(F32) + r_refs[j][1].astype(F32) + r_refs[j][2].astype(F32)
            g = gj if g is None else jnp.where(pl.program_id(0) >= first[j], gj, g)
        delta, m2, v2 = _adamw(w_ref[...], g, m_ref[...], v_ref[...])
        g_ref[...] = g
        d_ref[...] = delta
        nm_ref[...] = m2
        nv_ref[...] = v2

    def part(j):
        return pl.BlockSpec((tr, ct), lambda k, i: (i, jnp.clip(k - first[j], 0, count[j] - 1)))

    def part3(j):
        return pl.BlockSpec((3, tr, ct), lambda k, i: (0, i, jnp.clip(k - first[j], 0, count[j] - 1)))

    tile = pl.BlockSpec((tr, ct), lambda k, i: (i, k))
    return pl.pallas_call(
        body, name=name, grid=(sum(count), nt),
        in_specs=[part(j) for j in range(n)] + [part3(j) for j in range(n)] + [tile, tile, tile],
        out_specs=[tile] * 4, out_shape=[jax.ShapeDtypeStruct((R, sum(count) * ct), F32)] * 4,
        compiler_params=_cp(dimension_semantics=("arbitrary", "arbitrary")),
    )(*owns, *recvs, w, m, v)


def adam_cols(name, mine, recv, w, m, v, c_idx):
    half, C = mine.shape
    nt = _half_tiles(half)
    tr = half // nt

    def body(s_ref, a_ref, b_ref, w_ref, m_ref, v_ref, g_ref, d_ref, nm_ref, nv_ref):
        g = jnp.where(pl.program_id(0) == s_ref[0], a_ref[...], b_ref[...])
        delta, m2, v2 = _adamw(w_ref[...], g, m_ref[...], v_ref[...])
        g_ref[...] = g
        d_ref[...] = delta
        nm_ref[...] = m2
        nv_ref[...] = v2

    part = pl.BlockSpec((tr, C), lambda hh, i, s: (i, 0))
    tile = pl.BlockSpec((tr, C), lambda hh, i, s: (hh * nt + i, 0))
    grid_spec = pltpu.PrefetchScalarGridSpec(
        num_scalar_prefetch=1, grid=(2, nt), in_specs=[part, part, tile, tile, tile], out_specs=[tile] * 4)
    return pl.pallas_call(
        body, name=name, grid_spec=grid_spec, out_shape=[jax.ShapeDtypeStruct((2 * half, C), F32)] * 4,
        compiler_params=_cp(dimension_semantics=("arbitrary", "arbitrary")),
    )(jnp.stack([c_idx]), mine, recv, w, m, v)


ROW_N1, ROW_N2, ROW_BG, ROW_QN, ROW_KN, ROW_CB, ROW_LW, ROW_LB, ROW_CW = 0, 1, 2, 4, 5, 6, 7, 8, 9
PACK_ROWS = 40
SMALL = ("norm1_w", "norm2_w", "b_gate", "q_norm_w", "k_norm_w", "conv_b", "conv_ln_w", "conv_ln_b", "conv_w")


def small_sync_adam(g, w, m, v, sq, sides=()):
    ns = len(SMALL)

    def body(*refs):
        gi = dict(zip(SMALL, refs[:ns]))
        wi = dict(zip(SMALL, refs[ns:2 * ns]))
        mi = dict(zip(SMALL, refs[2 * ns:3 * ns]))
        vi = dict(zip(SMALL, refs[3 * ns:4 * ns]))
        sq_ref = refs[4 * ns]
        outs = refs[4 * ns + 1:8 * ns + 1]
        loss_ref = refs[8 * ns + 1]
        pack, recv, tot, send_sems, recv_sems = refs[8 * ns + 2:]
        x, y, c, _ = _place()
        me = 4 * x + 2 * y + c

        pack[...] = jnp.zeros_like(pack)
        pack[ROW_KN:ROW_KN + 1, LANES:2 * LANES] = jnp.full((1, LANES), (0.5 / D) * jnp.sum(sq_ref[...]), F32)
        pack[ROW_N1:ROW_N1 + 1, :] = gi["norm1_w"][...]
        pack[ROW_N2:ROW_N2 + 1, :] = gi["norm2_w"][...]
        pack[ROW_BG:ROW_BG + 2, :] = gi["b_gate"][...]
        pack[ROW_QN:ROW_QN + 1, 0:HD] = gi["q_norm_w"][...]
        pack[ROW_KN:ROW_KN + 1, 0:HD] = gi["k_norm_w"][...]
        pack[ROW_CB:ROW_CB + 1, 0:CC] = gi["conv_b"][...]
        pack[ROW_LW:ROW_LW + 1, 0:CC] = gi["conv_ln_w"][...]
        pack[ROW_LB:ROW_LB + 1, 0:CC] = gi["conv_ln_b"][...]
        pack[ROW_CW:ROW_CW + KW, 0:CC] = gi["conv_w"][...]

        copies = []
        for k in range(1, NDEV):
            peer = (x ^ (k >> 2), y ^ ((k >> 1) & 1), c ^ (k & 1))
            cp = pltpu.make_async_remote_copy(
                src_ref=pack, dst_ref=recv.at[me], send_sem=send_sems.at[k - 1], recv_sem=recv_sems.at[k - 1],
                device_id=peer, device_id_type=MESH)
            cp.start()
            copies.append(cp)
        recv[me] = pack[...]
        for cp in copies:
            cp.wait()
        acc = recv[0]
        for p in range(1, NDEV):
            acc = acc + recv[p]
        tot[...] = acc

        def shard_grad(name):
            if name == "b_gate":
                return tot[ROW_BG:ROW_BG + 2, pl.ds(pl.multiple_of(me * LANES, LANES), LANES)]
            if name == "conv_w":
                win = tot[ROW_CW:ROW_CW + KW, pl.ds(pl.multiple_of((me // 2) * LANES, LANES), LANES)]
                return jnp.where(me % 2 == 1, win[:, HD:LANES], win[:, 0:HD])
            row = {"norm1_w": ROW_N1, "norm2_w": ROW_N2, "q_norm_w": ROW_QN, "k_norm_w": ROW_KN,
                   "conv_b": ROW_CB, "conv_ln_w": ROW_LW, "conv_ln_b": ROW_LB}[name]
            return tot[row:row + 1, 0:wi[name].shape[1]]

        for i, name in enumerate(SMALL):
            gr = shard_grad(name)
            delta, m2, v2 = _adamw(wi[name][...], gr, mi[name][...], vi[name][...])
            outs[4 * i][...] = gr
            outs[4 * i + 1][...] = delta
            outs[4 * i + 2][...] = m2
            outs[4 * i + 3][...] = v2
        loss_ref[...] = tot[ROW_KN:ROW_KN + 1, LANES:2 * LANES]

    out_shape = []
    for name in SMALL:
        out_shape += [jax.ShapeDtypeStruct(w[name].shape, F32)] * 4
    out_shape.append(jax.ShapeDtypeStruct((1, LANES), F32))
    args = [g[k] for k in SMALL] + [w[k] for k in SMALL] + [m[k] for k in SMALL] + [v[k] for k in SMALL] + [sq]
    res = _call(
        body, sides, name="small_sync_adam", grid=(1,), in_specs=[VMEM] * len(args),
        out_specs=[VMEM] * len(out_shape), out_shape=out_shape,
        scratch_shapes=[pltpu.VMEM((PACK_ROWS, D), F32), pltpu.VMEM((NDEV, PACK_ROWS, D), F32),
                        pltpu.VMEM((PACK_ROWS, D), F32), _sems(NDEV - 1), _sems(NDEV - 1)],
        args=args, own_comm=True)
    res, side_outs = res if sides else (res, None)
    out = {name: tuple(res[4 * i:4 * i + 4]) for i, name in enumerate(SMALL)}
    loss = res[4 * ns][0, 0]
    return (out, loss, side_outs) if sides else (out, loss)


MATS = ("w_in", "w_o_attn", "w_pw_conv", "w_out", "w_ffn_in", "w_ffn_out")
TRANSPOSED = ("w_in", "w_ffn_in")
WEIGHTS = ("norm1_w", "w_in", "b_gate", "q_norm_w", "k_norm_w", "w_o_attn", "conv_w", "conv_b", "conv_ln_w",
           "conv_ln_b", "w_pw_conv", "w_out", "norm2_w", "w_ffn_in", "w_ffn_out")


def _blocks_to_cols(blocks):
    n, R, C = blocks.shape
    return blocks.transpose(1, 0, 2).reshape(R, n * C)


def kernel(x, positions, norm1_w, w_in, b_gate, q_norm_w, k_norm_w, w_o_attn, conv_w, conv_b, conv_ln_w, conv_ln_b, w_pw_conv, w_out, norm2_w, w_ffn_in, w_ffn_out, loss_target, m_norm1_w, m_w_in, m_b_gate, m_q_norm_w, m_k_norm_w, m_w_o_attn, m_conv_w, m_conv_b, m_conv_ln_w, m_conv_ln_b, m_w_pw_conv, m_w_out, m_norm2_w, m_w_ffn_in, m_w_ffn_out, v_norm1_w, v_w_in, v_b_gate, v_q_norm_w, v_k_norm_w, v_w_o_attn, v_conv_w, v_conv_b, v_conv_ln_w, v_conv_ln_b, v_w_pw_conv, v_w_out, v_norm2_w, v_w_ffn_in, v_w_ffn_out):
    w = dict(norm1_w=norm1_w, w_in=w_in, b_gate=b_gate, q_norm_w=q_norm_w, k_norm_w=k_norm_w, w_o_attn=w_o_attn,
             conv_w=conv_w, conv_b=conv_b, conv_ln_w=conv_ln_w, conv_ln_b=conv_ln_b, w_pw_conv=w_pw_conv,
             w_out=w_out, norm2_w=norm2_w, w_ffn_in=w_ffn_in, w_ffn_out=w_ffn_out)
    m = dict(norm1_w=m_norm1_w, w_in=m_w_in, b_gate=m_b_gate, q_norm_w=m_q_norm_w, k_norm_w=m_k_norm_w,
             w_o_attn=m_w_o_attn, conv_w=m_conv_w, conv_b=m_conv_b, conv_ln_w=m_conv_ln_w,
             conv_ln_b=m_conv_ln_b, w_pw_conv=m_w_pw_conv, w_out=m_w_out, norm2_w=m_norm2_w,
             w_ffn_in=m_w_ffn_in, w_ffn_out=m_w_ffn_out)
    v = dict(norm1_w=v_norm1_w, w_in=v_w_in, b_gate=v_b_gate, q_norm_w=v_q_norm_w, k_norm_w=v_k_norm_w,
             w_o_attn=v_w_o_attn, conv_w=v_conv_w, conv_b=v_conv_b, conv_ln_w=v_conv_ln_w,
             conv_ln_b=v_conv_ln_b, w_pw_conv=v_w_pw_conv, w_out=v_w_out, norm2_w=v_norm2_w,
             w_ffn_in=v_w_ffn_in, w_ffn_out=v_w_ffn_out)
    def two_d(t):
        t = {k: (a[0] if a.ndim == 3 else a) for k, a in t.items()}
        return {k: (a.T if k in TRANSPOSED else a) for k, a in t.items()}

    w, m, v = two_d(w), two_d(m), two_d(v)

    x2, target = x[0], loss_target[0]
    c_idx = lax.axis_index("c").astype(jnp.int32)
    chip_idx = (2 * lax.axis_index("x") + lax.axis_index("y")).astype(jnp.int32)
    qw2 = jnp.tile(w["q_norm_w"], (1, 2))
    kw2 = jnp.tile(w["k_norm_w"], (1, 2))

    ax, ay = lax.axis_index("x"), lax.axis_index("y")
    chip_order = jnp.stack([2 * ax + ay, 2 * (1 - ax) + ay, 2 * ax + 1 - ay, 2 * (1 - ax) + 1 - ay]).astype(jnp.int32)
    h_t, proj, w_in_blocks, tabs = in_proj_gather(x2, w["norm1_w"], w["w_in"], chip_order, positions.reshape(S, 1))
    w_in_t = w_in_blocks.reshape(INW, D)
    (attn, lse), ((w_ffn_in_blocks,), (w_out_blocks,), (bg_blocks,), (cw_blocks,)) = attn_fwd(
        proj, tabs, qw2, kw2, sides=(ag_blocks_relay(w["w_ffn_in"], BF16), ag_blocks_relay(w["w_out"], BF16),
                                     ag_blocks(w["b_gate"], F32), ag_blocks(w["conv_w"], F32)))
    w_ffn_in_t = w_ffn_in_blocks.reshape(2 * FF, D)
    w_out_f = w_out_blocks.reshape(D, D)
    b_gate_f, conv_w_f = _blocks_to_cols(bg_blocks), _blocks_to_cols(cw_blocks)
    (cpre, u3), ((w_o_f,), (w_pw_f,)) = conv_fwd(
        proj, conv_w_f, w["conv_b"], w["conv_ln_w"], w["conv_ln_b"],
        sides=(ag_cols(w["w_o_attn"]), ag_cols(w["w_pw_conv"])))
    x1, z, ya, yb = mix_out(x2, proj, b_gate_f, attn, u3, w_o_f, w_pw_f, w_out_f)
    (h2, gu, f), ((w_ffn_out_blocks,),) = ffn_in(x1, w["norm2_w"], w_ffn_in_t, sides=(ag_blocks_relay(w["w_ffn_out"], BF16),))
    w_ffn_out_f = w_ffn_out_blocks.reshape(FF, D)
    dy, dyb, sq = ffn_out_loss(x1, f, w_ffn_out_f, target)

    g = {}
    g_ffn_out = mm_tn("gw_ffn_out", f, dyb, FF // 2, D).reshape(NDEV, FF // NDEV, D)
    (d_gu, d_x1, d_x1b, g["norm2_w"]), ((ra_ffn_out,),) = ffn_bwd(
        dy, dyb, gu, x1, w["norm2_w"], w_ffn_in_t, w_ffn_out_f, sides=(rs_to_sibling([("rows", g_ffn_out)]),))
    pb_ffn_out, own_ffn_out = chip_sum("chip_sum_w_ffn_out", g_ffn_out, ra_ffn_out, c_idx, chip_idx)
    g_ffn_in = mm_tn("gw_ffn_in", d_gu, h2, FF // 2, D).reshape(NDEV, 2 * FF // NDEV, D)
    g_out = mm_tn("gw_out", z, d_x1b, D // 2, D).reshape(NDEV, D // NDEV, D)
    (d_ya, d_yb, d_gl, d_attn, d_u3, g["b_gate"]), ((ra_ffn_in,),) = out_bwd(
        d_x1b, proj, b_gate_f, ya, yb, w_o_f, w_pw_f, w_out_f, sides=(rs_to_sibling([("rows", g_ffn_in)]),))
    pb_ffn_in, own_ffn_in = chip_sum("chip_sum_w_ffn_in", g_ffn_in, ra_ffn_in, c_idx, chip_idx)
    g_w_o = mm_tn("gw_o_attn", attn, d_ya, CC, D)
    g_w_pw = mm_tn("gw_pw_conv", u3, d_yb, CC, D)
    (d_conv, g["conv_w"], g["conv_b"], g["conv_ln_w"], g["conv_ln_b"]), ((ra_out, ra_w_o, ra_w_pw),) = conv_bwd(
        proj, cpre, d_u3, conv_w_f, conv_w_f[::-1], w["conv_ln_w"], w["conv_ln_b"],
        sides=(rs_to_sibling([("rows", g_out), ("cols", g_w_o), ("cols", g_w_pw)]),))
    pb_out, own_out = chip_sum("chip_sum_w_out", g_out, ra_out, c_idx, chip_idx)
    pb_w_o, own_w_o = chip_sum_cols("chip_sum_w_o_attn", g_w_o, ra_w_o, c_idx, chip_idx)
    pb_w_pw, own_w_pw = chip_sum_cols("chip_sum_w_pw_conv", g_w_pw, ra_w_pw, c_idx, chip_idx)
    (d_q, d_k, d_v, gqw, gkw), ((rb_ffn_out, rb_ffn_in, rb_out, rb_w_o, rb_w_pw),) = attn_bwd(
        proj, tabs, qw2, kw2, d_attn, attn, lse,
        sides=(rs_to_chips([("rows", pb_ffn_out), ("rows", pb_ffn_in), ("rows", pb_out),
                            ("cols", pb_w_o), ("cols", pb_w_pw)]),))
    g["q_norm_w"] = gqw[0:1, 0:HD] + gqw[0:1, HD:LANES]
    g["k_norm_w"] = gkw[0:1, 0:HD] + gkw[0:1, HD:LANES]
    mine_w_o, theirs_w_o = col_final("col_final_w_o_attn", own_w_o, rb_w_o, c_idx)
    mine_w_pw, theirs_w_pw = col_final("col_final_w_pw_conv", own_w_pw, rb_w_pw, c_idx)
    d_segs = (d_q, d_k, d_v, d_conv, d_gl)
    parts, to_sibling, to_chips, owns, from_chips = [], None, None, [], []
    for k, hw in enumerate(GW_IN_SPLIT):
        sides = [rs_swap_halves([theirs_w_o, theirs_w_pw])] if k == 0 else []
        sides += [s for s in (to_chips, to_sibling) if s is not None]
        part, outs = gw_in_t("gw_in_%d" % k, h_t, d_segs, sum(GW_IN_SPLIT[:k]), hw, sides=tuple(sides))
        if k == 0:
            (rc_w_o, rc_w_pw), outs = outs[0], outs[1:]
        outs = list(outs)
        if to_chips is not None:
            from_chips.append(outs.pop(0)[0])
        if to_sibling is not None:
            pb, own = chip_sum("chip_sum_w_in_%d" % (k - 1), parts[-1], outs.pop(0)[0], c_idx, chip_idx)
            owns.append(own)
            to_chips = rs_to_chips([("rows", pb)])
        else:
            to_chips = None
        parts.append(part.reshape(NDEV, INW // NDEV, hw))
        to_sibling = rs_to_sibling([("rows", parts[-1])])
    (grad_x, g["norm1_w"]), ((rb_prev,), (ra_last,)) = in_bwd(
        d_q, d_k, d_v, d_conv, d_gl, w_in_t, x2, d_x1, w["norm1_w"], sides=(to_chips, to_sibling))
    from_chips.append(rb_prev)
    pb, own = chip_sum("chip_sum_w_in_%d" % (len(GW_IN_SPLIT) - 1), parts[-1], ra_last, c_idx, chip_idx)
    owns.append(own)
    small, loss, ((rb_last,),) = small_sync_adam(g, w, m, v, sq, sides=(rs_to_chips([("rows", pb)]),))
    from_chips.append(rb_last)

    res = {
        "w_in": shard_adam("adam_w_in", owns, from_chips, w["w_in"], m["w_in"], v["w_in"]),
        "w_ffn_in": shard_adam("adam_w_ffn_in", [own_ffn_in], [rb_ffn_in], w["w_ffn_in"], m["w_ffn_in"], v["w_ffn_in"]),
        "w_o_attn": adam_cols("adam_w_o_attn", mine_w_o, rc_w_o, w["w_o_attn"], m["w_o_attn"], v["w_o_attn"], c_idx),
        "w_pw_conv": adam_cols("adam_w_pw_conv", mine_w_pw, rc_w_pw, w["w_pw_conv"], m["w_pw_conv"], v["w_pw_conv"], c_idx),
        "w_out": shard_adam("adam_w_out", [own_out], [rb_out], w["w_out"], m["w_out"], v["w_out"]),
        "w_ffn_out": shard_adam("adam_w_ffn_out", [own_ffn_out], [rb_ffn_out],
                                w["w_ffn_out"], m["w_ffn_out"], v["w_ffn_out"]),
    }
    res = {k: tuple(a.T if k in TRANSPOSED else a for a in r) for k, r in res.items()}
    res.update(small)

    def shaped(name, a):
        return a.reshape((1,) + a.shape) if name in MATS or name in ("b_gate", "conv_w") else a

    outs = [loss, own_result("grad_x_result", grad_x).reshape(1, S, D)]
    for i in range(4):
        outs += [shaped(k, res[k][i]) for k in WEIGHTS]
    return tuple(outs)
```

```python
import functools
from typing import Callable, NamedTuple, Optional

import numpy as np
import jax
import jax.numpy as jnp
from jax import lax
from jax.experimental import pallas as pl
from jax.experimental.pallas import tpu as pltpu

F32 = jnp.float32
BF16 = jnp.bfloat16

S = 2048
D = 1024
HD = 64
QKV = 1536
CC = 512
KW = 31
FF = 2816
INW = 7680
OFF_Q, OFF_K, OFF_V, OFF_CA, OFF_CB, OFF_GA, OFF_GB = 0, 1536, 3072, 4608, 5120, 5632, 6656
DILATIONS = (1, 4, 16)
HALF_SPAN = 64
EPS = 1e-6
NEG_INF = -1e30
ROPE_THETA = 500000.0
ROT_DIM = 16

ADAM_LR = 0.001
ADAM_B1 = 0.9
ADAM_B2 = 0.999
ADAM_EPS = 1e-08
ADAM_WD = 0.01
ADAM_STEP = 10

NDEV = 8
LANES = 128
TM = 256
TQ = 128
VMEM_LIMIT = 56 * 1024 * 1024
MESH = pl.DeviceIdType.MESH


def _cp(**kw):
    return pltpu.CompilerParams(vmem_limit_bytes=VMEM_LIMIT, **kw)


def _row(width, col=0, tm=TM):
    return pl.BlockSpec((tm, width), lambda i: (i, col))


PLANE = 512


def _planes(width, tm=TM):
    return pl.BlockSpec((width // PLANE, tm, PLANE), lambda i: (0, i, 0))


def _res(shape):
    nd = len(shape)
    return pl.BlockSpec(shape, lambda *_: (0,) * nd, pipeline_mode=pl.Buffered(1))


def _dot(a, b):
    return jnp.dot(a, b, preferred_element_type=F32)


def _dot_nt(a, b):
    return lax.dot_general(a, b, (((1,), (1,)), ((), ())), preferred_element_type=F32)


def _dot_tn(a, b):
    return lax.dot_general(a, b, (((0,), (0,)), ((), ())), preferred_element_type=F32)


def _sigmoid(x):
    return jax.nn.sigmoid(x)


def _dsilu(x, sg):
    return sg * (1.0 + x * (1.0 - sg))


ANY = pl.BlockSpec(memory_space=pl.ANY)
VMEM = pl.BlockSpec(memory_space=pltpu.VMEM)


class Side(NamedTuple):
    args: tuple
    in_specs: tuple
    out_shape: tuple
    scratch: tuple
    start: Callable
    finish: Callable
    mid: Optional[Callable] = None
    peers: str = ""


BARRIER_IDS = {"s": 0, "dxy": 1, "dsxy": 2, "sxy": 3, "xy": 4}


def _peer_barrier(peers):
    x, y, c = lax.axis_index("x"), lax.axis_index("y"), lax.axis_index("c")
    where = {"s": (x, y, 1 - c), "x": (1 - x, y, c), "y": (x, 1 - y, c), "d": (1 - x, 1 - y, c)}
    barrier = pltpu.get_barrier_semaphore()
    for p in peers:
        pl.semaphore_signal(barrier, inc=1, device_id=where[p], device_id_type=MESH)
    pl.semaphore_wait(barrier, len(peers))


def _call(body, sides=(), *, name, grid, in_specs, out_specs, out_shape, scratch_shapes=(), args, own_comm=False):
    ni, no, ns = len(in_specs), len(out_specs), len(scratch_shapes)
    cnt = [(len(s.args), len(s.out_shape), len(s.scratch)) for s in sides]
    peers = "".join(sorted(set("".join(s.peers for s in sides))))
    if own_comm or not sides or any(not s.peers for s in sides):
        peers = ""

    def take(refs, pos, n):
        return refs[pos:pos + n], pos + n

    def full(*refs):
        m_in, pos = take(refs, 0, ni)
        s_in = []
        for a, _, _ in cnt:
            r, pos = take(refs, pos, a)
            s_in.append(r)
        m_out, pos = take(refs, pos, no)
        s_out = []
        for _, o, _ in cnt:
            r, pos = take(refs, pos, o)
            s_out.append(r)
        m_scr, pos = take(refs, pos, ns)
        s_scr = []
        for _, _, c in cnt:
            r, pos = take(refs, pos, c)
            s_scr.append(r)
        if sides:
            first = functools.reduce(jnp.logical_and, [pl.program_id(d) == 0 for d in range(len(grid))])
            last = functools.reduce(jnp.logical_and, [pl.program_id(d) == g - 1 for d, g in enumerate(grid)])

            @pl.when(first)
            def _():
                if peers:
                    _peer_barrier(peers)
                for s, a, o, c in zip(sides, s_in, s_out, s_scr):
                    s.start(a, o, c)

            steps = int(np.prod(grid))
            mid_step = (2 * steps) // 3
            if steps > 1 and any(s.mid is not None for s in sides):
                step = functools.reduce(lambda acc, d: acc * grid[d] + pl.program_id(d), range(len(grid)), 0)

                @pl.when(step == mid_step)
                def _():
                    for s, a, o, c in zip(sides, s_in, s_out, s_scr):
                        if s.mid is not None:
                            s.mid(a, o, c)

        body(*m_in, *m_out, *m_scr)
        if sides:
            @pl.when(last)
            def _():
                for s, a, o, c in zip(sides, s_in, s_out, s_scr):
                    if s.mid is not None and steps == 1:
                        s.mid(a, o, c)
                    s.finish(a, o, c)

    res = pl.pallas_call(
        full, name=name, grid=grid,
        in_specs=list(in_specs) + [sp for s in sides for sp in s.in_specs],
        out_specs=list(out_specs) + [ANY for s in sides for _ in s.out_shape],
        out_shape=list(out_shape) + [o for s in sides for o in s.out_shape],
        scratch_shapes=list(scratch_shapes) + [c for s in sides for c in s.scratch],
        compiler_params=_cp(dimension_semantics=("arbitrary",) * len(grid),
                            **({"collective_id": BARRIER_IDS[peers]} if peers else {})),
    )(*args, *[a for s in sides for a in s.args])
    res = list(res)
    if not sides:
        return res
    outs, pos = take(res, 0, no)
    side_outs = []
    for _, o, _ in cnt:
        r, pos = take(res, pos, o)
        side_outs.append(r)
    return outs, side_outs


def _inv_freq_lanes():
    inv = np.float32(ROPE_THETA) ** (-np.arange(0, ROT_DIM, 2, dtype=np.float32) / np.float32(ROT_DIM))
    lane = np.arange(LANES) % HD
    out = np.where(lane < ROT_DIM, inv[lane % (ROT_DIM // 2)], 0.0).astype(np.float32)
    return jnp.asarray(out.reshape(1, LANES))


def _rope_tables(pos, inv_freq):
    ang = pos.astype(F32) * inv_freq
    lane = lax.broadcasted_iota(jnp.int32, ang.shape, 1) % HD
    cs = jnp.cos(ang)
    sn = jnp.sin(ang)
    return (jnp.where(lane < ROT_DIM, cs, 1.0), jnp.where(lane < ROT_DIM // 2, -sn, 0.0),
            jnp.where(lane < ROT_DIM // 2, 0.0, jnp.where(lane < ROT_DIM, sn, 0.0)))


def _rope(v, c, s1, s2):
    return v * c + pltpu.roll(v, LANES - 8, axis=1) * s1 + pltpu.roll(v, 8, axis=1) * s2


def _rope_t(d, c, s1, s2):
    return d * c - pltpu.roll(d, LANES - 8, axis=1) * s1 - pltpu.roll(d, 8, axis=1) * s2


def _head_mat():
    r = lax.broadcasted_iota(jnp.int32, (LANES, LANES), 0) // HD
    c = lax.broadcasted_iota(jnp.int32, (LANES, LANES), 1) // HD
    return jnp.where(r == c, 1.0 / HD, 0.0).astype(BF16)


def _head_mean(t, e):
    hi = t.astype(BF16)
    rest = (t - hi.astype(F32)).astype(BF16)
    return _dot(hi, e) + _dot(rest, e)


def in_proj_gather(x, norm_w, shard_t, chip_order, pos_col):
    R = INW // NDEV
    half, nt = R // 2, S // TM

    def body(ord_ref, x_ref, nw_ref, sh_ref, pos_ref, f_ref, ht_ref, p_ref, wfull_ref, c_ref, s1_ref, s2_ref,
             wt, hs, send, recv, loc):
        kk, i = pl.program_id(0), pl.program_id(1)
        x, y, c, _ = _place()
        me, flip = 4 * x + 2 * y + c, 1 - 2 * c
        here, sib, xn, yn = (x, y, c), (x, y, 1 - c), (1 - x, y, c), (x, 1 - y, c)
        b_xn, b_yn, b_dg = 4 * (1 - x) + 2 * y + c, 4 * x + 2 * (1 - y) + c, 4 * (1 - x) + 2 * (1 - y) + c

        def cp(k, block, to, rows=None):
            dst = wt.at[block] if rows is None else wt.at[block, pl.ds(rows * half, half), :]
            return _remote(dst, dst, send, recv, k, to)

        def sends():
            return [cp(0, me, sib), cp(1, me, xn), cp(2, me, yn), cp(3, b_xn, sib), cp(4, b_yn, sib),
                    cp(5, b_xn, yn, rows=0), cp(6, b_yn, xn, rows=1), cp(7, b_dg, sib, rows=0), cp(8, b_dg, sib, rows=1)]

        def keep(j, blk0):
            pair = pl.ds(pl.multiple_of(blk0, 2), 2)
            return pltpu.make_async_copy(wt.at[pair], wfull_ref.at[pair], loc.at[j])

        @pl.when((kk == 0) & (i == 0))
        def _():
            _peer_barrier("sxy")
            _cast_rows(wt.at[me], sh_ref)
            for s_ in sends()[0:3]:
                s_.start()

            def tables(j, _):
                chunk = pl.ds(pl.multiple_of(j * TM, TM), TM)
                c_ref[chunk, :], s1_ref[chunk, :], s2_ref[chunk, :] = _rope_tables(pos_ref[chunk, :], f_ref[...])
                return 0

            lax.fori_loop(0, nt, tables, 0)
            cp(0, me + flip, here).wait_recv()
            keep(0, me - c).start()

        @pl.when((kk == 1) & (i == 0))
        def _():
            cp(1, b_xn, here).wait_recv()
            sends()[5].start()
            sends()[3].start()
            cp(2, b_yn, here).wait_recv()
            sends()[6].start()
            sends()[4].start()
            cp(3, b_xn + flip, here).wait_recv()
            keep(1, b_xn - c).start()

        @pl.when((kk == 2) & (i == 0))
        def _():
            cp(4, b_yn + flip, here).wait_recv()
            keep(2, b_yn - c).start()

        @pl.when((kk == 3) & (i == 0))
        def _():
            cp(5, b_dg, here, rows=0).wait_recv()
            sends()[7].start()
            cp(6, b_dg, here, rows=1).wait_recv()
            sends()[8].start()
            cp(7, b_dg + flip, here, rows=0).wait_recv()
            cp(8, b_dg + flip, here, rows=1).wait_recv()
            keep(3, b_dg - c).start()

        rows = pl.ds(pl.multiple_of(i * TM, TM), TM)

        @pl.when(kk == 0)
        def _():
            xv = x_ref[...]
            r = lax.rsqrt(jnp.mean(xv * xv, axis=-1, keepdims=True) + EPS)
            hf = xv * r * nw_ref[...]
            ht_ref[...] = hf.T.astype(BF16)
            hs[rows, :] = hf.astype(BF16)

        h = hs[rows, :]
        chip = ord_ref[kk]
        for cc in range(2):
            p_ref[:, cc * R:(cc + 1) * R] = _dot_nt(h, wt[2 * chip + cc])

        @pl.when((kk == 3) & (i == nt - 1))
        def _():
            for s_ in sends():
                s_.wait_send()
            for j, blk in enumerate((me, b_xn, b_yn, b_dg)):
                keep(j, blk - c).wait()

    def first_pass(kk, i):
        return jnp.where(kk == 0, i, nt - 1)

    grid_spec = pltpu.PrefetchScalarGridSpec(
        num_scalar_prefetch=1, grid=(4, nt),
        in_specs=[pl.BlockSpec((TM, D), lambda kk, i, o: (first_pass(kk, i), 0)),
                  pl.BlockSpec((1, D), lambda kk, i, o: (0, 0)), VMEM, VMEM,
                  pl.BlockSpec((1, LANES), lambda kk, i, o: (0, 0))],
        out_specs=[pl.BlockSpec((D, TM), lambda kk, i, o: (0, first_pass(kk, i))),
                   pl.BlockSpec((TM, 2 * R), lambda kk, i, o: (i, o[kk])), ANY]
        + [pl.BlockSpec((S, LANES), lambda kk, i, o: (0, 0))] * 3,
        scratch_shapes=[pltpu.VMEM((NDEV, R, D), BF16), pltpu.VMEM((S, D), BF16), _sems(9), _sems(9), _sems(4)])
    res = pl.pallas_call(
        body, name="in_proj_gather", grid_spec=grid_spec,
        out_shape=[jax.ShapeDtypeStruct((D, S), BF16), jax.ShapeDtypeStruct((S, INW), F32),
                   jax.ShapeDtypeStruct((NDEV, R, D), BF16)] + [jax.ShapeDtypeStruct((S, LANES), F32)] * 3,
        compiler_params=_cp(dimension_semantics=("arbitrary", "arbitrary"), collective_id=BARRIER_IDS["sxy"]),
    )(chip_order, x, norm_w, shard_t, pos_col, _inv_freq_lanes())
    return res[0], res[1], res[2], tuple(res[3:])


def _qk_specs():
    nb = QKV // LANES
    return [pl.BlockSpec((S, LANES), functools.partial(lambda hp, g, o: (0, o + g * 4 + hp), o=o))
            for o in (OFF_Q // LANES, OFF_K // LANES, OFF_V // LANES)]


def _tab_specs():
    return [pl.BlockSpec((S, LANES), lambda hp, g: (0, 0), pipeline_mode=pl.Buffered(1))] * 3


def _vec_spec():
    return pl.BlockSpec((1, LANES), lambda hp, g: (0, 0))


def _sub_rows(r, d, start, n):
    if d == 1:
        return pl.ds(start, n)
    return pl.ds(r + d * start, n, stride=d)


def _band_window(i, L):
    W = min(TQ + 2 * HALF_SPAN, L)
    q0 = pl.multiple_of(i * TQ, TQ)
    k0 = pl.multiple_of(jnp.clip(q0 - HALF_SPAN, 0, L - W), HALF_SPAN)
    qpos = q0 + (lax.broadcasted_iota(jnp.int32, (2 * TQ, W), 0) & (TQ - 1))
    kpos = k0 + lax.broadcasted_iota(jnp.int32, (2 * TQ, W), 1)
    valid = jnp.abs(qpos - kpos) <= HALF_SPAN
    return W, q0, k0, valid


def _stack_heads(t, lo):
    z = jnp.zeros_like(t)
    return jnp.concatenate([jnp.where(lo, t, z), jnp.where(lo, z, t)], axis=0)


def _unstack_heads(t2, lo):
    return jnp.where(lo, t2[0:TQ], t2[TQ:2 * TQ])


CHAINS = 8


def _interleave(d):
    ru = min(d, CHAINS)
    return ru, min(CHAINS // ru, S // d // TQ)


def _for_blocks(n, fn):
    if n == 1:
        fn(0)
    else:
        def it(j, _):
            fn(j)
            return 0
        lax.fori_loop(0, n, it, 0)


def attn_fwd(proj, tabs, qw2, kw2, sides=()):
    CH = 256

    def body(q_ref, k_ref, v_ref, c_ref, s1_ref, s2_ref, qw_ref, kw_ref, at_ref, ls_ref,
             qs, ks, vs, osub, lsub, onat, lnat, qn, kn):
        g = pl.program_id(1)
        lo = lax.broadcasted_iota(jnp.int32, (1, LANES), 1) < HD
        e = _head_mat()

        def prep(i, _):
            rows = pl.ds(pl.multiple_of(i * CH, CH), CH)
            c, s1, s2 = c_ref[rows, :], s1_ref[rows, :], s2_ref[rows, :]
            for t_ref, w_ref, out, scale in ((q_ref, qw_ref, qn, HD ** -0.5), (k_ref, kw_ref, kn, 1.0)):
                t = t_ref[rows, :]
                r = lax.rsqrt(_head_mean(t * t, e) + EPS)
                out[rows, :] = _rope(t * r * w_ref[...], c, s1, s2) * scale
            return 0

        lax.fori_loop(0, S // CH, prep, 0, unroll=4)

        def group(gi, d):
            L = S // d

            ru, nb = _interleave(d)

            def stage(r, off):
                for c0 in range(0, L, CH):
                    n = min(CH, L)
                    rows = _sub_rows(r, d, c0, n)
                    dst = pl.ds(off + c0, n)
                    qs[dst, :] = qn[rows, :].astype(BF16)
                    ks[dst, :] = kn[rows, :].astype(BF16)
                    vs[dst, :] = v_ref[rows, :].astype(BF16)

            def one(off, i):
                W, q0, k0, valid = _band_window(i, L)
                q2 = _stack_heads(qs[pl.ds(off + q0, TQ), :], lo)
                sc = jnp.where(valid, _dot_nt(q2, ks[pl.ds(off + k0, W), :]), NEG_INF)
                m = jnp.max(sc, axis=-1, keepdims=True)
                p = jnp.exp(sc - m)
                den = jnp.sum(p, axis=-1, keepdims=True)
                o2 = _dot(p.astype(BF16), vs[pl.ds(off + k0, W), :]) / den
                l2 = jnp.broadcast_to(m + jnp.log(den), (2 * TQ, LANES))
                osub[pl.ds(off + q0, TQ), :] = _unstack_heads(o2, lo)
                lsub[pl.ds(off + q0, TQ), :] = _unstack_heads(l2, lo)

            def unstage(r, off):
                for c0 in range(0, L, CH):
                    n = min(CH, L)
                    rows = _sub_rows(r, d, c0, n)
                    onat[gi, rows, :] = osub[pl.ds(off + c0, n), :]
                    lnat[gi, rows, :] = lsub[pl.ds(off + c0, n), :]

            def step(t, _):
                for u in range(ru):
                    stage(t * ru + u, u * L)
                _for_blocks(L // TQ // nb, lambda j: [one(u * L, j * nb + b) for u in range(ru) for b in range(nb)])
                for u in range(ru):
                    unstage(t * ru + u, u * L)
                return 0

            lax.fori_loop(0, d // ru, step, 0)

        for gi, d in enumerate(DILATIONS):
            pl.when(g == gi)(functools.partial(group, gi, d))

        @pl.when(g == len(DILATIONS) - 1)
        def _():
            def mix(i, _):
                rows = pl.ds(pl.multiple_of(i * CH, CH), CH)
                l0, l1, l2 = lnat[0, rows, :], lnat[1, rows, :], lnat[2, rows, :]
                m = jnp.maximum(jnp.maximum(l0, l1), l2)
                e0, e1, e2 = jnp.exp(l0 - m), jnp.exp(l1 - m), jnp.exp(l2 - m)
                den = e0 + e1 + e2
                a = (e0 * onat[0, rows, :] + e1 * onat[1, rows, :] + e2 * onat[2, rows, :]) / den
                at_ref[rows, :] = a.astype(BF16)
                ls_ref[rows, :] = m + jnp.log(den)
                return 0

            lax.fori_loop(0, S // CH, mix, 0)

    out_spec = pl.BlockSpec((S, LANES), lambda hp, g: (0, hp))
    return _call(
        body, sides, name="attn_fwd", grid=(4, 3),
        in_specs=_qk_specs() + _tab_specs() + [_vec_spec(), _vec_spec()],
        out_specs=[out_spec, out_spec],
        out_shape=[jax.ShapeDtypeStruct((S, CC), BF16), jax.ShapeDtypeStruct((S, CC), F32)],
        scratch_shapes=[pltpu.VMEM((S, LANES), BF16)] * 3 + [pltpu.VMEM((S, LANES), F32)] * 2
        + [pltpu.VMEM((3, S, LANES), F32)] * 2 + [pltpu.VMEM((S, LANES), F32)] * 2,
        args=(proj, proj, proj, *tabs, qw2, kw2))


def attn_bwd(proj, tabs, qw2, kw2, d_attn, attn, lse, sides=()):
    CH = 256

    def body(q_ref, k_ref, v_ref, c_ref, s1_ref, s2_ref, qw_ref, kw_ref, do_ref, at_ref, ls_ref,
             dq_ref, dk_ref, dv_ref, gqw_ref, gkw_ref,
             qs, ks, vs, dos, dsub, lsub, dqs, dks, dvs, dnat, qx, kx, dvn, tnq, tnk, rrq, rrk):
        hp, g = pl.program_id(0), pl.program_id(1)
        lo = lax.broadcasted_iota(jnp.int32, (1, LANES), 1) < HD
        e = _head_mat()
        both = ((q_ref, qw_ref, qx, tnq, rrq, HD ** -0.5), (k_ref, kw_ref, kx, tnk, rrk, 1.0))

        @pl.when((hp == 0) & (g == 0))
        def _():
            gqw_ref[...] = jnp.zeros_like(gqw_ref)
            gkw_ref[...] = jnp.zeros_like(gkw_ref)

        def prep(i, _):
            rows = pl.ds(pl.multiple_of(i * CH, CH), CH)
            dnat[rows, :] = _head_mean(do_ref[rows, :] * at_ref[rows, :].astype(F32), e) * float(HD)
            c, s1, s2 = c_ref[rows, :], s1_ref[rows, :], s2_ref[rows, :]
            for t_ref, w_ref, x, tn_s, rr_s, scale in both:
                t = t_ref[rows, :]
                rr = lax.rsqrt(_head_mean(t * t, e) + EPS)
                tn = t * rr
                rr_s[rows, :] = rr
                tn_s[rows, :] = tn
                x[rows, :] = _rope(tn * w_ref[...], c, s1, s2) * scale
            return 0

        lax.fori_loop(0, S // CH, prep, 0, unroll=4)

        def group(d):
            L = S // d

            ru, nb = _interleave(d)

            def stage(r, off):
                for c0 in range(0, L, CH):
                    n = min(CH, L)
                    rows = _sub_rows(r, d, c0, n)
                    dst = pl.ds(off + c0, n)
                    qs[dst, :] = qx[rows, :].astype(BF16)
                    ks[dst, :] = kx[rows, :].astype(BF16)
                    vs[dst, :] = v_ref[rows, :].astype(BF16)
                    dos[dst, :] = do_ref[rows, :].astype(BF16)
                    dsub[dst, :] = dnat[rows, :]
                    lsub[dst, :] = ls_ref[rows, :]
                    dks[dst, :] = jnp.zeros((n, LANES), F32)
                    dvs[dst, :] = jnp.zeros((n, LANES), F32)

            def one(off, i):
                W, q0, k0, valid = _band_window(i, L)
                qrows, krows = pl.ds(off + q0, TQ), pl.ds(off + k0, W)
                q2 = _stack_heads(qs[qrows, :], lo)
                do2 = _stack_heads(dos[qrows, :], lo)
                kk, vv = ks[krows, :], vs[krows, :]
                lse_b, dd_b = lsub[qrows, :], dsub[qrows, :]
                lse2 = jnp.concatenate([lse_b[:, 0:1], lse_b[:, HD:HD + 1]], axis=0)
                dd2 = jnp.concatenate([dd_b[:, 0:1], dd_b[:, HD:HD + 1]], axis=0)
                sc = jnp.where(valid, _dot_nt(q2, kk), NEG_INF)
                p = jnp.exp(sc - lse2)
                ds = (p * (_dot_nt(do2, vv) - dd2)).astype(BF16)
                dqs[qrows, :] = _unstack_heads(_dot(ds, kk), lo)
                dks[krows, :] = dks[krows, :] + _dot_tn(ds, q2)
                dvs[krows, :] = dvs[krows, :] + _dot_tn(p.astype(BF16), do2)

            def unstage(r, off):
                for c0 in range(0, L, CH):
                    n = min(CH, L)
                    rows = _sub_rows(r, d, c0, n)
                    src = pl.ds(off + c0, n)
                    qx[rows, :] = dqs[src, :]
                    kx[rows, :] = dks[src, :]
                    dvn[rows, :] = dvs[src, :]

            def step(t, _):
                for u in range(ru):
                    stage(t * ru + u, u * L)
                _for_blocks(L // TQ // nb, lambda j: [one(u * L, j * nb + b) for u in range(ru) for b in range(nb)])
                for u in range(ru):
                    unstage(t * ru + u, u * L)
                return 0

            lax.fori_loop(0, d // ru, step, 0)

        for gi, d in enumerate(DILATIONS):
            pl.when(g == gi)(functools.partial(group, d))

        def emit(i, _):
            rows = pl.ds(pl.multiple_of(i * CH, CH), CH)
            c, s1, s2 = c_ref[rows, :], s1_ref[rows, :], s2_ref[rows, :]
            for (_, w_ref, x, tn_s, rr_s, scale), out, gw_ref in zip(both, (dq_ref, dk_ref), (gqw_ref, gkw_ref)):
                tn = tn_s[rows, :]
                dy = _rope_t(x[rows, :] * scale, c, s1, s2)
                gw_ref[0:1, :] = gw_ref[0:1, :] + jnp.sum(dy * tn, axis=0, keepdims=True)
                dtn = dy * w_ref[...]
                out[rows, :] = (rr_s[rows, :] * (dtn - tn * _head_mean(dtn * tn, e))).astype(BF16)
            dv_ref[rows, :] = dvn[rows, :].astype(BF16)
            return 0

        lax.fori_loop(0, S // CH, emit, 0, unroll=4)

    nat_spec = pl.BlockSpec((S, LANES), lambda hp, g: (0, hp))
    out_spec = pl.BlockSpec((None, S, LANES), lambda hp, g: (g, 0, hp))
    acc_spec = pl.BlockSpec((8, LANES), lambda hp, g: (0, 0))
    return _call(
        body, sides, name="attn_bwd", grid=(4, 3),
        in_specs=_qk_specs() + _tab_specs() + [_vec_spec(), _vec_spec(), nat_spec, nat_spec, nat_spec],
        out_specs=[out_spec] * 3 + [acc_spec] * 2,
        out_shape=[jax.ShapeDtypeStruct((QKV // PLANE, S, PLANE), BF16)] * 3 + [jax.ShapeDtypeStruct((8, LANES), F32)] * 2,
        scratch_shapes=[pltpu.VMEM((S, LANES), BF16)] * 4 + [pltpu.VMEM((S, LANES), F32)] * 13,
        args=(proj, proj, proj, *tabs, qw2, kw2, d_attn, attn, lse))


PADR = 16
CT = 128


def _conv_specs():
    return [pl.BlockSpec((S, CC), lambda i: (0, OFF_CA // CC)), pl.BlockSpec((S, CC), lambda i: (0, OFF_CB // CC))]


NCB = CC // LANES


def _pad_zero(pad):
    for cb in range(NCB):
        pad[cb, 0:PADR, :] = jnp.zeros((PADR, LANES), F32)
        pad[cb, PADR + S:PADR + S + PADR, :] = jnp.zeros((PADR, LANES), F32)


def _pad_store(pad, row0, n, val):
    for cb in range(NCB):
        pad[cb, pl.ds(pl.multiple_of(row0 + PADR, 8), n), :] = val[:, cb * LANES:(cb + 1) * LANES]


def _taps(pad_ref, cb, s0, weights):
    acc = jnp.zeros((CT, LANES), F32)
    for k in range(KW):
        acc = acc + weights[k] * pad_ref[cb, pl.ds(s0 + k + 1, CT), :]
    return acc


def conv_fwd(proj, conv_w, conv_b, ln_w, ln_b, sides=()):
    def body(a_ref, b_ref, w_ref, cb_ref, lw_ref, lb_ref, c_ref, u3_ref, upad):
        _pad_zero(upad)

        def glu(i, _):
            rows = pl.ds(pl.multiple_of(i * TM, TM), TM)
            _pad_store(upad, i * TM, TM, a_ref[rows, :] * _sigmoid(b_ref[rows, :]))
            return 0

        lax.fori_loop(0, S // TM, glu, 0)

        def chunk(i, _):
            s0 = pl.multiple_of(i * CT, CT)
            for cb in range(CC // LANES):
                cols = slice(cb * LANES, (cb + 1) * LANES)
                w = [w_ref[k:k + 1, cols] for k in range(KW)]
                c_ref[pl.ds(s0, CT), cols] = _taps(upad, cb, s0, w) + cb_ref[:, cols]
            cv = c_ref[pl.ds(s0, CT), :]
            mu = jnp.mean(cv, axis=-1, keepdims=True)
            xc = cv - mu
            rstd = lax.rsqrt(jnp.mean(xc * xc, axis=-1, keepdims=True) + EPS)
            yl = xc * rstd * lw_ref[...] + lb_ref[...]
            u3_ref[pl.ds(s0, CT), :] = (yl * _sigmoid(yl)).astype(BF16)
            return 0

        lax.fori_loop(0, S // CT, chunk, 0)

    vec = pl.BlockSpec((1, CC), lambda i: (0, 0))
    full = pl.BlockSpec((S, CC), lambda i: (0, 0))
    return _call(
        body, sides, name="conv_fwd", grid=(1,),
        in_specs=_conv_specs() + [pl.BlockSpec((KW, CC), lambda i: (0, 0)), vec, vec, vec],
        out_specs=[full, full],
        out_shape=[jax.ShapeDtypeStruct((S, CC), F32), jax.ShapeDtypeStruct((S, CC), BF16)],
        scratch_shapes=[pltpu.VMEM((NCB, S + 2 * PADR, LANES), F32)],
        args=(proj, proj, conv_w, conv_b, ln_w, ln_b))


def conv_bwd(proj, cpre, d_u3, conv_w, conv_w_rev, ln_w, ln_b, sides=()):
    def body(a_ref, b_ref, c_ref, du3_ref, w_ref, wr_ref, lw_ref, lb_ref,
             dc_ref, gw_ref, gcb_ref, glw_ref, glb_ref, upad, dpad):
        _pad_zero(upad)
        _pad_zero(dpad)
        gw_ref[...] = jnp.zeros_like(gw_ref)

        def ln_bwd(i, carry):
            gcb, glw, glb = carry
            rows = pl.ds(pl.multiple_of(i * TM, TM), TM)
            _pad_store(upad, i * TM, TM, a_ref[rows, :] * _sigmoid(b_ref[rows, :]))
            cv = c_ref[rows, :]
            mu = jnp.mean(cv, axis=-1, keepdims=True)
            xc = cv - mu
            rstd = lax.rsqrt(jnp.mean(xc * xc, axis=-1, keepdims=True) + EPS)
            xh = xc * rstd
            yl = xh * lw_ref[...] + lb_ref[...]
            dyl = du3_ref[rows, :] * _dsilu(yl, _sigmoid(yl))
            dxh = dyl * lw_ref[...]
            dcv = rstd * (dxh - jnp.mean(dxh, axis=-1, keepdims=True)
                          - xh * jnp.mean(dxh * xh, axis=-1, keepdims=True))
            _pad_store(dpad, i * TM, TM, dcv)
            return (gcb + jnp.sum(dcv, axis=0, keepdims=True),
                    glw + jnp.sum(dyl * xh, axis=0, keepdims=True),
                    glb + jnp.sum(dyl, axis=0, keepdims=True))

        z = jnp.zeros((1, CC), F32)
        gcb, glw, glb = lax.fori_loop(0, S // TM, ln_bwd, (z, z, z))
        gcb_ref[...] = gcb
        glw_ref[...] = glw
        glb_ref[...] = glb

        def chunk(i, _):
            s0 = pl.multiple_of(i * CT, CT)
            for cb in range(CC // LANES):
                cols = slice(cb * LANES, (cb + 1) * LANES)
                wr = [wr_ref[k:k + 1, cols] for k in range(KW)]
                du = _taps(dpad, cb, s0, wr)
                dcv = dpad[cb, pl.ds(s0 + PADR, CT), :]
                for k in range(KW):
                    gw_ref[k:k + 1, cols] = gw_ref[k:k + 1, cols] + jnp.sum(
                        upad[cb, pl.ds(s0 + k + 1, CT), :] * dcv, axis=0, keepdims=True)
                av = a_ref[pl.ds(s0, CT), cols]
                sb = _sigmoid(b_ref[pl.ds(s0, CT), cols])
                dc_ref[0, pl.ds(s0, CT), cols] = (du * sb).astype(BF16)
                dc_ref[1, pl.ds(s0, CT), cols] = (du * av * sb * (1.0 - sb)).astype(BF16)
            return 0

        lax.fori_loop(0, S // CT, chunk, 0)

    vec = pl.BlockSpec((1, CC), lambda i: (0, 0))
    full = pl.BlockSpec((S, CC), lambda i: (0, 0))
    wsp = pl.BlockSpec((KW, CC), lambda i: (0, 0))
    return _call(
        body, sides, name="conv_bwd", grid=(1,),
        in_specs=_conv_specs() + [full, full, wsp, wsp, vec, vec],
        out_specs=[pl.BlockSpec((2, S, CC), lambda i: (0, 0, 0)), wsp, vec, vec, vec],
        out_shape=[jax.ShapeDtypeStruct((2, S, CC), BF16), jax.ShapeDtypeStruct((KW, CC), F32)]
        + [jax.ShapeDtypeStruct((1, CC), F32)] * 3,
        scratch_shapes=[pltpu.VMEM((NCB, S + 2 * PADR, LANES), F32)] * 2,
        args=(proj, proj, cpre, d_u3, conv_w, conv_w_rev, ln_w, ln_b))


def _gate_specs():
    return [_row(CC, col=OFF_GA // CC + j) for j in range(4)]


def _gates(g_refs, bg_ref):
    ga = _sigmoid(jnp.concatenate([g_refs[0][...], g_refs[1][...]], axis=1) + bg_ref[0:1, :])
    gb = _sigmoid(jnp.concatenate([g_refs[2][...], g_refs[3][...]], axis=1) + bg_ref[1:2, :])
    return ga, gb


def mix_out(x, proj, b_gate, attn, u3, w_o, w_pw, w_out):
    def body(x_ref, g0, g1, g2, g3, bg_ref, at_ref, u3_ref, wo_ref, wp_ref, wout_ref,
             x1_ref, z_ref, ya_ref, yb_ref):
        ga, gb = _gates((g0, g1, g2, g3), bg_ref)
        ya = _dot(at_ref[...], wo_ref[...])
        yb = _dot(u3_ref[...], wp_ref[...])
        z = (ga * ya + gb * yb).astype(BF16)
        ya_ref[...] = ya.astype(BF16)
        yb_ref[...] = yb.astype(BF16)
        z_ref[...] = z
        x1_ref[...] = x_ref[...] + _dot(z, wout_ref[...])

    return pl.pallas_call(
        body, name="mix_out", grid=(S // TM,),
        in_specs=[_row(D)] + _gate_specs() + [_res((2, D)), _row(CC), _row(CC),
                                              _res((CC, D)), _res((CC, D)), _res((D, D))],
        out_specs=[_row(D)] * 4,
        out_shape=[jax.ShapeDtypeStruct((S, D), F32)] + [jax.ShapeDtypeStruct((S, D), BF16)] * 3,
        compiler_params=_cp(dimension_semantics=("arbitrary",)),
    )(x, proj, proj, proj, proj, b_gate, attn, u3, w_o, w_pw, w_out)


def out_bwd(d_x1b, proj, b_gate, ya, yb, w_o, w_pw, w_out, sides=()):
    def body(dx_ref, g0, g1, g2, g3, bg_ref, ya_ref, yb_ref, wo_ref, wp_ref, wout_ref,
             dya_ref, dyb_ref, dgl_ref, dat_ref, du3_ref, gbg_ref):
        @pl.when(pl.program_id(0) == 0)
        def _():
            gbg_ref[...] = jnp.zeros_like(gbg_ref)

        ga, gb = _gates((g0, g1, g2, g3), bg_ref)
        dz = _dot_nt(dx_ref[...], wout_ref[...])
        dya = (dz * ga).astype(BF16)
        dyb = (dz * gb).astype(BF16)
        dgla = dz * ya_ref[...].astype(F32) * ga * (1.0 - ga)
        dglb = dz * yb_ref[...].astype(F32) * gb * (1.0 - gb)
        dya_ref[...] = dya
        dyb_ref[...] = dyb
        for j in range(2):
            dgl_ref[j] = dgla[:, j * PLANE:(j + 1) * PLANE].astype(BF16)
            dgl_ref[2 + j] = dglb[:, j * PLANE:(j + 1) * PLANE].astype(BF16)
        gbg_ref[0:1, :] = gbg_ref[0:1, :] + jnp.sum(dgla, axis=0, keepdims=True)
        gbg_ref[1:2, :] = gbg_ref[1:2, :] + jnp.sum(dglb, axis=0, keepdims=True)
        dat_ref[...] = _dot_nt(dya, wo_ref[...])
        du3_ref[...] = _dot_nt(dyb, wp_ref[...])

    return _call(
        body, sides, name="out_bwd", grid=(S // TM,),
        in_specs=[_row(D)] + _gate_specs() + [_res((2, D)), _row(D), _row(D),
                                              _res((CC, D)), _res((CC, D)), _res((D, D))],
        out_specs=[_row(D), _row(D), _planes(2 * D), _row(CC), _row(CC), pl.BlockSpec((2, D), lambda i: (0, 0))],
        out_shape=[jax.ShapeDtypeStruct((S, D), BF16)] * 2 + [jax.ShapeDtypeStruct((2 * D // PLANE, S, PLANE), BF16)]
        + [jax.ShapeDtypeStruct((S, CC), F32)] * 2 + [jax.ShapeDtypeStruct((2, D), F32)],
        args=(d_x1b, proj, proj, proj, proj, b_gate, ya, yb, w_o, w_pw, w_out))


def ffn_in(x1, norm_w, w_ffn_in, sides=()):
    half = FF // 2

    def body(x_ref, nw_ref, w_ref, h_ref, gu_ref, f_ref):
        xv = x_ref[...]
        r = lax.rsqrt(jnp.mean(xv * xv, axis=-1, keepdims=True) + EPS)
        h = (xv * r * nw_ref[...]).astype(BF16)
        h_ref[...] = h
        for j in range(2):
            gt = _dot_nt(h, w_ref[j * half:(j + 1) * half, :])
            up = _dot_nt(h, w_ref[FF + j * half:FF + (j + 1) * half, :])
            gu_ref[:, j * half:(j + 1) * half] = gt.astype(BF16)
            gu_ref[:, FF + j * half:FF + (j + 1) * half] = up.astype(BF16)
            f_ref[:, j * half:(j + 1) * half] = (gt * _sigmoid(gt) * up).astype(BF16)

    return _call(
        body, sides, name="ffn_in", grid=(S // TM,),
        in_specs=[_row(D), _res((1, D)), _res((2 * FF, D))],
        out_specs=[_row(D), _row(2 * FF), _row(FF)],
        out_shape=[jax.ShapeDtypeStruct((S, D), BF16), jax.ShapeDtypeStruct((S, 2 * FF), BF16),
                   jax.ShapeDtypeStruct((S, FF), BF16)],
        args=(x1, norm_w, w_ffn_in))


def ffn_out_loss(x1, f, w_ffn_out, target):
    def body(x_ref, f_ref, w_ref, t_ref, dy_ref, dyb_ref, sq_ref):
        @pl.when(pl.program_id(0) == 0)
        def _():
            sq_ref[...] = jnp.zeros_like(sq_ref)

        diff = x_ref[...] + _dot(f_ref[...], w_ref[...]) - t_ref[...]
        dy = diff * (1.0 / D)
        dy_ref[...] = dy
        dyb_ref[...] = dy.astype(BF16)
        sq_ref[...] = sq_ref[...] + jnp.sum((diff * diff).reshape(TM // 8, 8, D), axis=0)

    return pl.pallas_call(
        body, name="ffn_out_loss", grid=(S // TM,),
        in_specs=[_row(D), _row(FF), _res((FF, D)), _row(D)],
        out_specs=[_row(D), _row(D), pl.BlockSpec((8, D), lambda i: (0, 0))],
        out_shape=[jax.ShapeDtypeStruct((S, D), F32), jax.ShapeDtypeStruct((S, D), BF16),
                   jax.ShapeDtypeStruct((8, D), F32)],
        compiler_params=_cp(dimension_semantics=("arbitrary",)),
    )(x1, f, w_ffn_out, target)


def _rms_bwd(xv, nw, dh):
    r = lax.rsqrt(jnp.mean(xv * xv, axis=-1, keepdims=True) + EPS)
    xn = xv * r
    dxn = dh * nw
    dx = r * (dxn - xn * jnp.mean(dxn * xn, axis=-1, keepdims=True))
    return dx, dh * xn


def ffn_bwd(dy, dyb, gu, x1, norm_w, w_ffn_in, w_ffn_out, sides=()):
    def body(dy_ref, dyb_ref, gu_ref, x_ref, nw_ref, wi_ref, wo_ref, dgu_ref, dx_ref, dxb_ref, gn_ref):
        @pl.when(pl.program_id(0) == 0)
        def _():
            gn_ref[...] = jnp.zeros_like(gn_ref)

        df = _dot_nt(dyb_ref[...], wo_ref[...])
        gt = gu_ref[:, 0:FF].astype(F32)
        up = gu_ref[:, FF:2 * FF].astype(F32)
        sg = _sigmoid(gt)
        dgt = (df * up * _dsilu(gt, sg)).astype(BF16)
        dup = (df * gt * sg).astype(BF16)
        dgu_ref[:, 0:FF] = dgt
        dgu_ref[:, FF:2 * FF] = dup
        dh = _dot(dgt, wi_ref[0:FF, :]) + _dot(dup, wi_ref[FF:2 * FF, :])
        dxn, gw = _rms_bwd(x_ref[...], nw_ref[...], dh)
        dx = dy_ref[...] + dxn
        dx_ref[...] = dx
        dxb_ref[...] = dx.astype(BF16)
        gn_ref[...] = gn_ref[...] + jnp.sum(gw, axis=0, keepdims=True)

    return _call(
        body, sides, name="ffn_bwd", grid=(S // TM,),
        in_specs=[_row(D), _row(D), _row(2 * FF), _row(D), _res((1, D)), _res((2 * FF, D)), _res((FF, D))],
        out_specs=[_row(2 * FF), _row(D), _row(D), pl.BlockSpec((1, D), lambda i: (0, 0))],
        out_shape=[jax.ShapeDtypeStruct((S, 2 * FF), BF16), jax.ShapeDtypeStruct((S, D), F32),
                   jax.ShapeDtypeStruct((S, D), BF16), jax.ShapeDtypeStruct((1, D), F32)],
        args=(dy, dyb, gu, x1, norm_w, w_ffn_in, w_ffn_out))


def in_bwd(d_q, d_k, d_v, d_conv, d_gl, w_in, x, d_x1, norm_w, sides=()):
    segs = ((OFF_Q, QKV), (OFF_K, QKV), (OFF_V, QKV), (OFF_CA, 2 * CC), (OFF_GA, 2 * D))

    def body(dq_ref, dk_ref, dv_ref, dc_ref, dg_ref, w_ref, x_ref, dx1_ref, nw_ref, gx_ref, gn_ref):
        @pl.when(pl.program_id(0) == 0)
        def _():
            gn_ref[...] = jnp.zeros_like(gn_ref)

        dh = jnp.zeros((TM, D), F32)
        for ref, (off, width) in zip((dq_ref, dk_ref, dv_ref, dc_ref, dg_ref), segs):
            for j in range(width // PLANE):
                dh = dh + _dot(ref[j], w_ref[off + j * PLANE:off + (j + 1) * PLANE, :])
        dxn, gw = _rms_bwd(x_ref[...], nw_ref[...], dh)
        gx_ref[...] = dx1_ref[...] + dxn
        gn_ref[...] = gn_ref[...] + jnp.sum(gw, axis=0, keepdims=True)

    return _call(
        body, sides, name="in_bwd", grid=(S // TM,),
        in_specs=[_planes(QKV)] * 3 + [_planes(2 * CC), _planes(2 * D), _res((INW, D)), _row(D), _row(D), _res((1, D))],
        out_specs=[_row(D), pl.BlockSpec((1, D), lambda i: (0, 0))],
        out_shape=[jax.ShapeDtypeStruct((S, D), F32), jax.ShapeDtypeStruct((1, D), F32)],
        args=(d_q, d_k, d_v, d_conv, d_gl, w_in, x, d_x1, norm_w))


def mm_tn(name, a, b, tm, tn, sides=()):
    M, N = a.shape[1], b.shape[1]

    def body(a_ref, b_ref, o_ref):
        o_ref[...] = _dot_tn(a_ref[...], b_ref[...])

    res = _call(
        body, sides, name=name, grid=(M // tm, N // tn),
        in_specs=[pl.BlockSpec((S, tm), lambda i, j: (0, i)), pl.BlockSpec((S, tn), lambda i, j: (0, j))],
        out_specs=[pl.BlockSpec((tm, tn), lambda i, j: (i, j))],
        out_shape=[jax.ShapeDtypeStruct((M, N), F32)],
        args=(a, b))
    return (res[0][0], res[1]) if sides else res[0]


GW_IN_TN = PLANE
GW_IN_SPLIT = (768, 256)


def gw_in_t(name, ht, d_segs, col0, hw, sides=()):
    tn = GW_IN_TN
    starts, t0 = [], 0
    for seg in d_segs:
        starts.append(t0)
        t0 += seg.shape[0]
    ntiles = [seg.shape[0] for seg in d_segs]

    def body(h_ref, *refs):
        a_refs, o_ref = refs[:-1], refs[-1]
        n = pl.program_id(0)
        for a_ref, st, nt in zip(a_refs, starts, ntiles):
            @pl.when((n >= st) & (n < st + nt))
            def _(a_ref=a_ref):
                o_ref[...] = _dot(h_ref[...], a_ref[...]).T

    def seg_spec(st, nt):
        return pl.BlockSpec((None, S, tn), lambda n: (jnp.clip(n - st, 0, nt - 1), 0, 0))

    res = _call(
        body, sides, name=name, grid=(INW // tn,),
        in_specs=[pl.BlockSpec((hw, S), lambda n: (col0 // hw, 0))] + [seg_spec(st, nt) for st, nt in zip(starts, ntiles)],
        out_specs=[pl.BlockSpec((tn, hw), lambda n: (n, 0))],
        out_shape=[jax.ShapeDtypeStruct((INW, hw), F32)],
        args=(ht, *d_segs))
    return (res[0][0], res[1]) if sides else res[0]


def _place():
    x, y, c = lax.axis_index("x"), lax.axis_index("y"), lax.axis_index("c")
    chips = [(1 - x, y), (x, 1 - y), (1 - x, 1 - y)]
    return x, y, c, chips


def _sems(n):
    return pltpu.SemaphoreType.DMA((n,))


def _remote(src, dst, send, recv, k, to):
    return pltpu.make_async_remote_copy(src_ref=src, dst_ref=dst, send_sem=send.at[k], recv_sem=recv.at[k],
                                        device_id=to, device_id_type=MESH)


def _cast_rows(dst, src, cols=slice(None)):
    rows = src.shape[0]
    step = next((s for s in (128, 64, 32, 16) if rows % s == 0), rows)
    for r0 in range(0, rows, step):
        dst[r0:r0 + step, cols] = src[r0:r0 + step, :].astype(dst.dtype)


def comm_only(name, sides):
    def body():
        pass

    return _call(body, sides, name=name, grid=(1,), in_specs=[], out_specs=[], out_shape=[], args=())[1]


def ag_blocks(shard, dtype):
    R, W = shard.shape

    def copy(outs, scr, k, block, to, src=None):
        dst = outs[0].at[block]
        return _remote(dst if src is None else src, dst, scr[1], scr[2], k, to)

    def local(outs, scr, me):
        return pltpu.make_async_copy(scr[0], outs[0].at[me], scr[3].at[0])

    def start(ins, outs, scr):
        x, y, c, chips = _place()
        me = 4 * x + 2 * y + c
        _cast_rows(scr[0], ins[0])
        local(outs, scr, me).start()
        copy(outs, scr, 0, me, (x, y, 1 - c), src=scr[0]).start()
        for j, (cx, cy) in enumerate(chips):
            copy(outs, scr, 1 + j, me, (cx, cy, c), src=scr[0]).start()

    def finish(ins, outs, scr):
        x, y, c, chips = _place()
        me, sib = 4 * x + 2 * y + c, (x, y, 1 - c)
        passed = []
        for j, (cx, cy) in enumerate(chips):
            theirs = 4 * cx + 2 * cy + c
            copy(outs, scr, 1 + j, theirs, (x, y, c)).wait_recv()
            fwd = copy(outs, scr, 4 + j, theirs, sib)
            fwd.start()
            passed.append(fwd)
        copy(outs, scr, 0, 4 * x + 2 * y + 1 - c, (x, y, c)).wait_recv()
        for j, (cx, cy) in enumerate(chips):
            copy(outs, scr, 4 + j, 4 * cx + 2 * cy + 1 - c, (x, y, c)).wait_recv()
        copy(outs, scr, 0, me, sib, src=scr[0]).wait_send()
        for j, (cx, cy) in enumerate(chips):
            copy(outs, scr, 1 + j, me, (cx, cy, c), src=scr[0]).wait_send()
        for fwd in passed:
            fwd.wait_send()
        local(outs, scr, me).wait()

    return Side((shard,), (VMEM,), (jax.ShapeDtypeStruct((NDEV, R, W), dtype),),
                (pltpu.VMEM((R, W), dtype), _sems(7), _sems(7), _sems(1)), start, finish, None, "dsxy")


def ag_blocks_relay(shard, dtype):
    R, W = shard.shape
    half = R // 2

    def copy(outs, scr, k, block, to, src=None, rows=None):
        dst = outs[0].at[block] if rows is None else outs[0].at[block, pl.ds(rows * half, half), :]
        return _remote(dst if src is None else src, dst, scr[1], scr[2], k, to)

    def local(outs, scr, me):
        return pltpu.make_async_copy(scr[0], outs[0].at[me], scr[3].at[0])

    def own(outs, scr):
        x, y, c, _ = _place()
        me = 4 * x + 2 * y + c
        return [copy(outs, scr, k, me, to, src=scr[0])
                for k, to in enumerate([(x, y, 1 - c), (1 - x, y, c), (x, 1 - y, c)])]

    def start(ins, outs, scr):
        x, y, c, _ = _place()
        _cast_rows(scr[0], ins[0])
        local(outs, scr, 4 * x + 2 * y + c).start()
        for cp in own(outs, scr):
            cp.start()

    def passed_on(outs, scr):
        x, y, c, _ = _place()
        sib, xn, yn = (x, y, 1 - c), (1 - x, y, c), (x, 1 - y, c)
        b_xn, b_yn, b_dg = 4 * (1 - x) + 2 * y + c, 4 * x + 2 * (1 - y) + c, 4 * (1 - x) + 2 * (1 - y) + c
        near = [copy(outs, scr, 5, b_xn, yn, rows=0), copy(outs, scr, 3, b_xn, sib),
                copy(outs, scr, 6, b_yn, xn, rows=1), copy(outs, scr, 4, b_yn, sib)]
        far = [copy(outs, scr, 7, b_dg, sib, rows=0), copy(outs, scr, 8, b_dg, sib, rows=1)]
        return (b_xn, b_yn, b_dg), near, far

    def mid(ins, outs, scr):
        x, y, c, _ = _place()
        (b_xn, b_yn, _), near, _ = passed_on(outs, scr)
        copy(outs, scr, 1, b_xn, (x, y, c)).wait_recv()
        near[0].start()
        near[1].start()
        copy(outs, scr, 2, b_yn, (x, y, c)).wait_recv()
        near[2].start()
        near[3].start()

    def finish(ins, outs, scr):
        x, y, c, _ = _place()
        here = (x, y, c)
        (b_xn, b_yn, b_dg), near, far = passed_on(outs, scr)
        copy(outs, scr, 5, b_dg, here, rows=0).wait_recv()
        far[0].start()
        copy(outs, scr, 6, b_dg, here, rows=1).wait_recv()
        far[1].start()
        flip = 1 - 2 * c
        copy(outs, scr, 0, 4 * x + 2 * y + 1 - c, here).wait_recv()
        copy(outs, scr, 3, b_xn + flip, here).wait_recv()
        copy(outs, scr, 4, b_yn + flip, here).wait_recv()
        copy(outs, scr, 7, b_dg + flip, here, rows=0).wait_recv()
        copy(outs, scr, 8, b_dg + flip, here, rows=1).wait_recv()
        for cp in own(outs, scr) + near + far:
            cp.wait_send()
        local(outs, scr, 4 * x + 2 * y + c).wait()

    return Side((shard,), (VMEM,), (jax.ShapeDtypeStruct((NDEV, R, W), dtype),),
                (pltpu.VMEM((R, W), dtype), _sems(9), _sems(9), _sems(1)), start, finish, mid, "sxy")


def ag_cols(shard):
    K, C = shard.shape
    half, w2 = K // 2, 2 * C

    def win(out, rows_c, chip):
        return out.at[pl.ds(pl.multiple_of(rows_c * half, 16), half), pl.ds(pl.multiple_of(chip * w2, LANES), w2)]

    def ici(outs, scr, j, to, c, k):
        slab, send, recv = scr[2], scr[5], scr[6]
        return _remote(slab.at[pl.ds(pl.multiple_of(c * half, 16), half), :], win(outs[0], c, k), send, recv, j, to)

    def local(outs, scr, k):
        return pltpu.make_async_copy(scr[2], outs[0].at[:, pl.ds(pl.multiple_of(k * w2, LANES), w2)], scr[7].at[0])

    def start(ins, outs, scr):
        stage, inbox, slab, xs, xr = scr[:5]
        x, y, c, chips = _place()
        k = 2 * x + y
        _cast_rows(stage, ins[0])
        swap = _remote(stage, inbox, xs, xr, 0, (x, y, 1 - c))
        swap.start()
        for cc in range(2):
            @pl.when(c == cc)
            def _(cc=cc):
                _cast_rows(slab, stage, slice(cc * C, (cc + 1) * C))
        swap.wait()
        for cc in range(2):
            @pl.when(c == cc)
            def _(cc=cc):
                _cast_rows(slab, inbox, slice((1 - cc) * C, (2 - cc) * C))
        local(outs, scr, k).start()
        for j, (cx, cy) in enumerate(chips):
            ici(outs, scr, j, (cx, cy, c), c, k).start()

    def finish(ins, outs, scr):
        send, recv = scr[5], scr[6]
        x, y, c, chips = _place()
        k, sib = 2 * x + y, (x, y, 1 - c)
        passed = []
        for j, (cx, cy) in enumerate(chips):
            w = win(outs[0], c, 2 * cx + cy)
            _remote(w, w, send, recv, j, sib).wait_recv()
            fwd = _remote(w, w, send, recv, 3 + j, sib)
            fwd.start()
            passed.append(fwd)
        for j, (cx, cy) in enumerate(chips):
            w = win(outs[0], 1 - c, 2 * cx + cy)
            _remote(w, w, send, recv, 3 + j, sib).wait_recv()
        for j, (cx, cy) in enumerate(chips):
            ici(outs, scr, j, (cx, cy, c), c, k).wait_send()
        for fwd in passed:
            fwd.wait_send()
        local(outs, scr, k).wait()

    return Side((shard,), (VMEM,), (jax.ShapeDtypeStruct((K, NDEV * C), BF16),),
                (pltpu.VMEM((K, C), BF16), pltpu.VMEM((K, C), BF16), pltpu.VMEM((K, w2), BF16),
                 _sems(1), _sems(1), _sems(6), _sems(6), _sems(1)), start, finish, None, "dsxy")


def copies_side(args, out_shape, n_copies, plan, peers):
    def copies(ins, outs, scr):
        return [_remote(s_, d_, scr[0], scr[1], i, to) for i, (s_, d_, to) in enumerate(plan(ins, outs))]

    def start(ins, outs, scr):
        for cp in copies(ins, outs, scr):
            cp.start()

    def finish(ins, outs, scr):
        for cp in copies(ins, outs, scr):
            cp.wait()

    return Side(tuple(args), (ANY,) * len(args), tuple(out_shape), (_sems(n_copies), _sems(n_copies)),
                start, finish, None, peers)


def rs_to_sibling(grads):
    out_shape = [jax.ShapeDtypeStruct((4,) + g.shape[1:] if kind == "rows" else (g.shape[0] // 2, g.shape[1]), F32)
                 for kind, g in grads]

    def plan(ins, outs):
        x, y, c, _ = _place()
        sib, res = (x, y, 1 - c), []
        for (kind, _), g, r in zip(grads, ins, outs):
            if kind == "rows":
                res += [(g.at[2 * k + 1 - c], r.at[k], sib) for k in range(4)]
            else:
                half = g.shape[0] // 2
                res.append((g.at[pl.ds(pl.multiple_of((1 - c) * half, 8), half), :], r, sib))
        return res

    return copies_side([g for _, g in grads], out_shape, sum(4 if kind == "rows" else 1 for kind, _ in grads), plan, "s")


def rs_to_chips(parts):
    out_shape = [jax.ShapeDtypeStruct((3,) + p.shape[1:] if kind == "rows" else (3, p.shape[0], p.shape[1] // 4), BF16)
                 for kind, p in parts]

    def plan(ins, outs):
        x, y, c, chips = _place()
        res = []
        for (kind, _), p, r in zip(parts, ins, outs):
            for j, (cx, cy) in enumerate(chips):
                if kind == "rows":
                    src = p.at[2 * cx + cy]
                else:
                    w2 = p.shape[1] // 4
                    src = p.at[:, pl.ds(pl.multiple_of((2 * cx + cy) * w2, LANES), w2)]
                res.append((src, r.at[j], (cx, cy, c)))
        return res

    return copies_side([p for _, p in parts], out_shape, 3 * len(parts), plan, "dxy")


def rs_to_chips_combined(part):
    _, R, W = part.shape
    half = R // 2
    top, bot = pl.ds(0, half), pl.ds(half, half)

    def copies(ins, outs, scr):
        p, r = ins[0], outs[0]
        loc_a, loc_b, in_x, in_y, comb_a, comb_b, send, recv, loc = scr
        x, y, c, _ = _place()
        xn, yn = (1 - x, y, c), (x, 1 - y, c)
        k_xn, k_yn, k_dg = 2 * (1 - x) + y, 2 * x + 1 - y, 2 * (1 - x) + 1 - y
        direct = [_remote(p.at[k_xn, top, :], r.at[0, top, :], send, recv, 0, xn),
                  _remote(p.at[k_yn, bot, :], r.at[1, bot, :], send, recv, 1, yn),
                  _remote(p.at[k_dg, top, :], in_x, send, recv, 2, xn),
                  _remote(p.at[k_dg, bot, :], in_y, send, recv, 3, yn)]
        combined = [_remote(comb_a, r.at[1, top, :], send, recv, 4, yn),
                    _remote(comb_b, r.at[0, bot, :], send, recv, 5, xn)]
        local = [pltpu.make_async_copy(p.at[k_yn, top, :], loc_a, loc.at[0]),
                 pltpu.make_async_copy(p.at[k_xn, bot, :], loc_b, loc.at[1])]
        return direct, combined, local

    def start(ins, outs, scr):
        direct, _, local = copies(ins, outs, scr)
        for cp in local + direct:
            cp.start()

    def mid(ins, outs, scr):
        loc_a, loc_b, in_x, in_y, comb_a, comb_b = scr[:6]
        direct, combined, local = copies(ins, outs, scr)
        for mine, arrival, inbox, out, nxt in ((local[0], direct[2], in_x, comb_a, combined[0]),
                                               (local[1], direct[3], in_y, comb_b, combined[1])):
            mine.wait()
            arrival.wait_recv()
            src = loc_a if out is comb_a else loc_b
            out[...] = (src[...].astype(F32) + inbox[...].astype(F32)).astype(BF16)
            nxt.start()

    def finish(ins, outs, scr):
        direct, combined, _ = copies(ins, outs, scr)
        direct[0].wait_recv()
        direct[1].wait_recv()
        combined[0].wait_recv()
        combined[1].wait_recv()
        for cp in direct + combined:
            cp.wait_send()

    buf = pltpu.VMEM((half, W), BF16)
    return Side((part,), (ANY,), (jax.ShapeDtypeStruct((2, R, W), BF16),),
                (buf, buf, buf, buf, buf, buf, _sems(6), _sems(6), _sems(2)), start, finish, mid, "xy")


def rs_swap_halves(theirs):
    def plan(ins, outs):
        x, y, c, _ = _place()
        return [(t, r, (x, y, 1 - c)) for t, r in zip(ins, outs)]

    return copies_side(theirs, [jax.ShapeDtypeStruct(t.shape, F32) for t in theirs], len(theirs), plan, "s")


ADAM_TILE_BYTES = 3 * 512 * 1024


def _row_tiles(rows, width):
    return 2 if rows % 32 == 0 and rows * width * 4 > ADAM_TILE_BYTES else 1


def chip_sum(name, grad, recv, c_idx, chip_idx):
    _, R, C = grad.shape
    nt = 1
    tr = R // nt

    def body(s_ref, g_ref, r_ref, p_ref, own_ref):
        k = pl.program_id(1)
        tot = g_ref[0] + r_ref[0]
        p_ref[0] = tot.astype(BF16)

        @pl.when(k == s_ref[1])
        def _():
            own_ref[...] = tot

    grid_spec = pltpu.PrefetchScalarGridSpec(
        num_scalar_prefetch=1, grid=(nt, 4),
        in_specs=[pl.BlockSpec((1, tr, C), lambda i, k, s: (2 * k + s[0], i, 0)),
                  pl.BlockSpec((1, tr, C), lambda i, k, s: (k, i, 0))],
        out_specs=[pl.BlockSpec((1, tr, C), lambda i, k, s: (k, i, 0)),
                   pl.BlockSpec((tr, C), lambda i, k, s: (i, 0))])
    return pl.pallas_call(
        body, name=name, grid_spec=grid_spec,
        out_shape=[jax.ShapeDtypeStruct((4, R, C), BF16), jax.ShapeDtypeStruct((R, C), F32)],
        compiler_params=_cp(dimension_semantics=("arbitrary", "arbitrary")),
    )(jnp.stack([c_idx, chip_idx]), grad, recv)


def _half_tiles(half):
    return 2 if half >= 512 else 1


def chip_sum_cols(name, grad, recv, c_idx, chip_idx):
    K, W = grad.shape
    half, w2 = K // 2, W // 4
    nt = _half_tiles(half)
    tr = half // nt

    def body(s_ref, g_ref, r_ref, p_ref, own_ref):
        tot = g_ref[...] + r_ref[...]
        p_ref[...] = tot.astype(BF16)

        @pl.when(pl.program_id(1) == s_ref[1])
        def _():
            own_ref[...] = tot

    grid_spec = pltpu.PrefetchScalarGridSpec(
        num_scalar_prefetch=1, grid=(nt, 4),
        in_specs=[pl.BlockSpec((tr, w2), lambda i, k, s: (s[0] * nt + i, k)),
                  pl.BlockSpec((tr, w2), lambda i, k, s: (i, k))],
        out_specs=[pl.BlockSpec((tr, w2), lambda i, k, s: (i, k)),
                   pl.BlockSpec((tr, w2), lambda i, k, s: (i, 0))])
    return pl.pallas_call(
        body, name=name, grid_spec=grid_spec,
        out_shape=[jax.ShapeDtypeStruct((half, W), BF16), jax.ShapeDtypeStruct((half, w2), F32)],
        compiler_params=_cp(dimension_semantics=("arbitrary", "arbitrary")),
    )(jnp.stack([c_idx, chip_idx]), grad, recv)


def col_final(name, own, recv, c_idx):
    half, w2 = own.shape
    C = w2 // 2
    nt = _half_tiles(half)
    tr = half // nt

    def body(s_ref, o_ref, r_ref, mine_ref, theirs_ref, t_ref):
        t_ref[...] = o_ref[...] + r_ref[0].astype(F32) + r_ref[1].astype(F32) + r_ref[2].astype(F32)
        for cc in range(2):
            @pl.when(s_ref[0] == cc)
            def _(cc=cc):
                mine_ref[...] = t_ref[:, cc * C:(cc + 1) * C]
                theirs_ref[...] = t_ref[:, (1 - cc) * C:(2 - cc) * C]

    grid_spec = pltpu.PrefetchScalarGridSpec(
        num_scalar_prefetch=1, grid=(nt,),
        in_specs=[pl.BlockSpec((tr, w2), lambda i, s: (i, 0)), pl.BlockSpec((3, tr, w2), lambda i, s: (0, i, 0))],
        out_specs=[pl.BlockSpec((tr, C), lambda i, s: (i, 0))] * 2,
        scratch_shapes=[pltpu.VMEM((tr, w2), F32)])
    return pl.pallas_call(
        body, name=name, grid_spec=grid_spec, out_shape=[jax.ShapeDtypeStruct((half, C), F32)] * 2,
        compiler_params=_cp(dimension_semantics=("arbitrary",)),
    )(jnp.stack([c_idx]), own, recv)


def _adamw(w, g, m, v):
    m2 = ADAM_B1 * m + (1.0 - ADAM_B1) * g
    v2 = ADAM_B2 * v + (1.0 - ADAM_B2) * (g * g)
    m_hat = m2 / (1.0 - ADAM_B1 ** ADAM_STEP)
    v_hat = v2 / (1.0 - ADAM_B2 ** ADAM_STEP)
    delta = -ADAM_LR * (m_hat / (jnp.sqrt(v_hat) + ADAM_EPS) + ADAM_WD * w)
    return delta, m2, v2


def shard_adam(name, owns, recvs, w, m, v):
    n = len(owns)
    R = owns[0].shape[0]
    ct = min(o.shape[1] for o in owns)
    first = [sum(o.shape[1] for o in owns[:j]) // ct for j in range(n)]
    count = [o.shape[1] // ct for o in owns]
    nt = _row_tiles(R, ct)
    tr = R // nt

    def body(*refs):
        o_refs, r_refs = refs[:n], refs[n:2 * n]
        w_ref, m_ref, v_ref, g_ref, d_ref, nm_ref, nv_ref = refs[2 * n:]
        g = None
        for j in range(n):
            gj = o_refs[j][...]
            for q in range(recvs[j].shape[0]):
                gj = gj + r_refs[j][q].astype(F32)
            g = gj if g is None else jnp.where(pl.program_id(0) >= first[j], gj, g)
        delta, m2, v2 = _adamw(w_ref[...], g, m_ref[...], v_ref[...])
        g_ref[...] = g
        d_ref[...] = delta
        nm_ref[...] = m2
        nv_ref[...] = v2

    def part(j):
        return pl.BlockSpec((tr, ct), lambda k, i: (i, jnp.clip(k - first[j], 0, count[j] - 1)))

    def part3(j):
        return pl.BlockSpec((recvs[j].shape[0], tr, ct), lambda k, i: (0, i, jnp.clip(k - first[j], 0, count[j] - 1)))

    tile = pl.BlockSpec((tr, ct), lambda k, i: (i, k))
    return pl.pallas_call(
        body, name=name, grid=(sum(count), nt),
        in_specs=[part(j) for j in range(n)] + [part3(j) for j in range(n)] + [tile, tile, tile],
        out_specs=[tile] * 4, out_shape=[jax.ShapeDtypeStruct((R, sum(count) * ct), F32)] * 4,
        compiler_params=_cp(dimension_semantics=("arbitrary", "arbitrary")),
    )(*owns, *recvs, w, m, v)


def adam_cols(name, mine, recv, w, m, v, c_idx):
    half, C = mine.shape
    nt = _half_tiles(half)
    tr = half // nt

    def body(s_ref, a_ref, b_ref, w_ref, m_ref, v_ref, g_ref, d_ref, nm_ref, nv_ref):
        g = jnp.where(pl.program_id(0) == s_ref[0], a_ref[...], b_ref[...])
        delta, m2, v2 = _adamw(w_ref[...], g, m_ref[...], v_ref[...])
        g_ref[...] = g
        d_ref[...] = delta
        nm_ref[...] = m2
        nv_ref[...] = v2

    part = pl.BlockSpec((tr, C), lambda hh, i, s: (i, 0))
    tile = pl.BlockSpec((tr, C), lambda hh, i, s: (hh * nt + i, 0))
    grid_spec = pltpu.PrefetchScalarGridSpec(
        num_scalar_prefetch=1, grid=(2, nt), in_specs=[part, part, tile, tile, tile], out_specs=[tile] * 4)
    return pl.pallas_call(
        body, name=name, grid_spec=grid_spec, out_shape=[jax.ShapeDtypeStruct((2 * half, C), F32)] * 4,
        compiler_params=_cp(dimension_semantics=("arbitrary", "arbitrary")),
    )(jnp.stack([c_idx]), mine, recv, w, m, v)


ROW_N1, ROW_N2, ROW_BG, ROW_QN, ROW_KN, ROW_CB, ROW_LW, ROW_LB, ROW_CW = 0, 1, 2, 4, 5, 6, 7, 8, 9
PACK_ROWS = 40
SMALL = ("norm1_w", "norm2_w", "b_gate", "q_norm_w", "k_norm_w", "conv_b", "conv_ln_w", "conv_ln_b", "conv_w")


def small_sync_adam(g, w, m, v, sq, sides=()):
    ns = len(SMALL)

    def body(*refs):
        gi = dict(zip(SMALL, refs[:ns]))
        wi = dict(zip(SMALL, refs[ns:2 * ns]))
        mi = dict(zip(SMALL, refs[2 * ns:3 * ns]))
        vi = dict(zip(SMALL, refs[3 * ns:4 * ns]))
        sq_ref = refs[4 * ns]
        outs = refs[4 * ns + 1:8 * ns + 1]
        loss_ref = refs[8 * ns + 1]
        pack, recv, tot, send_sems, recv_sems = refs[8 * ns + 2:]
        x, y, c, _ = _place()
        me = 4 * x + 2 * y + c

        pack[...] = jnp.zeros_like(pack)
        pack[ROW_KN:ROW_KN + 1, LANES:2 * LANES] = jnp.full((1, LANES), (0.5 / D) * jnp.sum(sq_ref[...]), F32)
        pack[ROW_N1:ROW_N1 + 1, :] = gi["norm1_w"][...]
        pack[ROW_N2:ROW_N2 + 1, :] = gi["norm2_w"][...]
        pack[ROW_BG:ROW_BG + 2, :] = gi["b_gate"][...]
        pack[ROW_QN:ROW_QN + 1, 0:HD] = gi["q_norm_w"][...]
        pack[ROW_KN:ROW_KN + 1, 0:HD] = gi["k_norm_w"][...]
        pack[ROW_CB:ROW_CB + 1, 0:CC] = gi["conv_b"][...]
        pack[ROW_LW:ROW_LW + 1, 0:CC] = gi["conv_ln_w"][...]
        pack[ROW_LB:ROW_LB + 1, 0:CC] = gi["conv_ln_b"][...]
        pack[ROW_CW:ROW_CW + KW, 0:CC] = gi["conv_w"][...]

        copies = []
        for k in range(1, NDEV):
            peer = (x ^ (k >> 2), y ^ ((k >> 1) & 1), c ^ (k & 1))
            cp = pltpu.make_async_remote_copy(
                src_ref=pack, dst_ref=recv.at[me], send_sem=send_sems.at[k - 1], recv_sem=recv_sems.at[k - 1],
                device_id=peer, device_id_type=MESH)
            cp.start()
            copies.append(cp)
        recv[me] = pack[...]
        for cp in copies:
            cp.wait()
        acc = recv[0]
        for p in range(1, NDEV):
            acc = acc + recv[p]
        tot[...] = acc

        def shard_grad(name):
            if name == "b_gate":
                return tot[ROW_BG:ROW_BG + 2, pl.ds(pl.multiple_of(me * LANES, LANES), LANES)]
            if name == "conv_w":
                win = tot[ROW_CW:ROW_CW + KW, pl.ds(pl.multiple_of((me // 2) * LANES, LANES), LANES)]
                return jnp.where(me % 2 == 1, win[:, HD:LANES], win[:, 0:HD])
            row = {"norm1_w": ROW_N1, "norm2_w": ROW_N2, "q_norm_w": ROW_QN, "k_norm_w": ROW_KN,
                   "conv_b": ROW_CB, "conv_ln_w": ROW_LW, "conv_ln_b": ROW_LB}[name]
            return tot[row:row + 1, 0:wi[name].shape[1]]

        for i, name in enumerate(SMALL):
            gr = shard_grad(name)
            delta, m2, v2 = _adamw(wi[name][...], gr, mi[name][...], vi[name][...])
            outs[4 * i][...] = gr
            outs[4 * i + 1][...] = delta
            outs[4 * i + 2][...] = m2
            outs[4 * i + 3][...] = v2
        loss_ref[...] = tot[ROW_KN:ROW_KN + 1, LANES:2 * LANES]

    out_shape = []
    for name in SMALL:
        out_shape += [jax.ShapeDtypeStruct(w[name].shape, F32)] * 4
    out_shape.append(jax.ShapeDtypeStruct((1, LANES), F32))
    args = [g[k] for k in SMALL] + [w[k] for k in SMALL] + [m[k] for k in SMALL] + [v[k] for k in SMALL] + [sq]
    res = _call(
        body, sides, name="small_sync_adam", grid=(1,), in_specs=[VMEM] * len(args),
        out_specs=[VMEM] * len(out_shape), out_shape=out_shape,
        scratch_shapes=[pltpu.VMEM((PACK_ROWS, D), F32), pltpu.VMEM((NDEV, PACK_ROWS, D), F32),
                        pltpu.VMEM((PACK_ROWS, D), F32), _sems(NDEV - 1), _sems(NDEV - 1)],
        args=args, own_comm=True)
    res, side_outs = res if sides else (res, None)
    out = {name: tuple(res[4 * i:4 * i + 4]) for i, name in enumerate(SMALL)}
    loss = res[4 * ns][0, 0]
    return (out, loss, side_outs) if sides else (out, loss)


MATS = ("w_in", "w_o_attn", "w_pw_conv", "w_out", "w_ffn_in", "w_ffn_out")
TRANSPOSED = ("w_in", "w_ffn_in")
WEIGHTS = ("norm1_w", "w_in", "b_gate", "q_norm_w", "k_norm_w", "w_o_attn", "conv_w", "conv_b", "conv_ln_w",
           "conv_ln_b", "w_pw_conv", "w_out", "norm2_w", "w_ffn_in", "w_ffn_out")


def _blocks_to_cols(blocks):
    n, R, C = blocks.shape
    return blocks.transpose(1, 0, 2).reshape(R, n * C)


def kernel(x, positions, norm1_w, w_in, b_gate, q_norm_w, k_norm_w, w_o_attn, conv_w, conv_b, conv_ln_w, conv_ln_b, w_pw_conv, w_out, norm2_w, w_ffn_in, w_ffn_out, loss_target, m_norm1_w, m_w_in, m_b_gate, m_q_norm_w, m_k_norm_w, m_w_o_attn, m_conv_w, m_conv_b, m_conv_ln_w, m_conv_ln_b, m_w_pw_conv, m_w_out, m_norm2_w, m_w_ffn_in, m_w_ffn_out, v_norm1_w, v_w_in, v_b_gate, v_q_norm_w, v_k_norm_w, v_w_o_attn, v_conv_w, v_conv_b, v_conv_ln_w, v_conv_ln_b, v_w_pw_conv, v_w_out, v_norm2_w, v_w_ffn_in, v_w_ffn_out):
    w = dict(norm1_w=norm1_w, w_in=w_in, b_gate=b_gate, q_norm_w=q_norm_w, k_norm_w=k_norm_w, w_o_attn=w_o_attn,
             conv_w=conv_w, conv_b=conv_b, conv_ln_w=conv_ln_w, conv_ln_b=conv_ln_b, w_pw_conv=w_pw_conv,
             w_out=w_out, norm2_w=norm2_w, w_ffn_in=w_ffn_in, w_ffn_out=w_ffn_out)
    m = dict(norm1_w=m_norm1_w, w_in=m_w_in, b_gate=m_b_gate, q_norm_w=m_q_norm_w, k_norm_w=m_k_norm_w,
             w_o_attn=m_w_o_attn, conv_w=m_conv_w, conv_b=m_conv_b, conv_ln_w=m_conv_ln_w,
             conv_ln_b=m_conv_ln_b, w_pw_conv=m_w_pw_conv, w_out=m_w_out, norm2_w=m_norm2_w,
             w_ffn_in=m_w_ffn_in, w_ffn_out=m_w_ffn_out)
    v = dict(norm1_w=v_norm1_w, w_in=v_w_in, b_gate=v_b_gate, q_norm_w=v_q_norm_w, k_norm_w=v_k_norm_w,
             w_o_attn=v_w_o_attn, conv_w=v_conv_w, conv_b=v_conv_b, conv_ln_w=v_conv_ln_w,
             conv_ln_b=v_conv_ln_b, w_pw_conv=v_w_pw_conv, w_out=v_w_out, norm2_w=v_norm2_w,
             w_ffn_in=v_w_ffn_in, w_ffn_out=v_w_ffn_out)
    def two_d(t):
        t = {k: (a[0] if a.ndim == 3 else a) for k, a in t.items()}
        return {k: (a.T if k in TRANSPOSED else a) for k, a in t.items()}

    w, m, v = two_d(w), two_d(m), two_d(v)

    x2, target = x[0], loss_target[0]
    c_idx = lax.axis_index("c").astype(jnp.int32)
    chip_idx = (2 * lax.axis_index("x") + lax.axis_index("y")).astype(jnp.int32)
    qw2 = jnp.tile(w["q_norm_w"], (1, 2))
    kw2 = jnp.tile(w["k_norm_w"], (1, 2))

    ax, ay = lax.axis_index("x"), lax.axis_index("y")
    chip_order = jnp.stack([2 * ax + ay, 2 * (1 - ax) + ay, 2 * ax + 1 - ay, 2 * (1 - ax) + 1 - ay]).astype(jnp.int32)
    h_t, proj, w_in_blocks, tabs = in_proj_gather(x2, w["norm1_w"], w["w_in"], chip_order, positions.reshape(S, 1))
    w_in_t = w_in_blocks.reshape(INW, D)
    (attn, lse), ((w_ffn_in_blocks,), (w_out_blocks,), (bg_blocks,), (cw_blocks,)) = attn_fwd(
        proj, tabs, qw2, kw2, sides=(ag_blocks_relay(w["w_ffn_in"], BF16), ag_blocks_relay(w["w_out"], BF16),
                                     ag_blocks(w["b_gate"], F32), ag_blocks(w["conv_w"], F32)))
    w_ffn_in_t = w_ffn_in_blocks.reshape(2 * FF, D)
    w_out_f = w_out_blocks.reshape(D, D)
    b_gate_f, conv_w_f = _blocks_to_cols(bg_blocks), _blocks_to_cols(cw_blocks)
    (cpre, u3), ((w_o_f,), (w_pw_f,)) = conv_fwd(
        proj, conv_w_f, w["conv_b"], w["conv_ln_w"], w["conv_ln_b"],
        sides=(ag_cols(w["w_o_attn"]), ag_cols(w["w_pw_conv"])))
    x1, z, ya, yb = mix_out(x2, proj, b_gate_f, attn, u3, w_o_f, w_pw_f, w_out_f)
    (h2, gu, f), ((w_ffn_out_blocks,),) = ffn_in(x1, w["norm2_w"], w_ffn_in_t, sides=(ag_blocks_relay(w["w_ffn_out"], BF16),))
    w_ffn_out_f = w_ffn_out_blocks.reshape(FF, D)
    dy, dyb, sq = ffn_out_loss(x1, f, w_ffn_out_f, target)

    g = {}
    g_ffn_out = mm_tn("gw_ffn_out", f, dyb, FF // 2, D).reshape(NDEV, FF // NDEV, D)
    (d_gu, d_x1, d_x1b, g["norm2_w"]), ((ra_ffn_out,),) = ffn_bwd(
        dy, dyb, gu, x1, w["norm2_w"], w_ffn_in_t, w_ffn_out_f, sides=(rs_to_sibling([("rows", g_ffn_out)]),))
    pb_ffn_out, own_ffn_out = chip_sum("chip_sum_w_ffn_out", g_ffn_out, ra_ffn_out, c_idx, chip_idx)
    g_ffn_in = mm_tn("gw_ffn_in", d_gu, h2, FF // 2, D).reshape(NDEV, 2 * FF // NDEV, D)
    g_out = mm_tn("gw_out", z, d_x1b, D // 2, D).reshape(NDEV, D // NDEV, D)
    (d_ya, d_yb, d_gl, d_attn, d_u3, g["b_gate"]), ((ra_ffn_in,),) = out_bwd(
        d_x1b, proj, b_gate_f, ya, yb, w_o_f, w_pw_f, w_out_f, sides=(rs_to_sibling([("rows", g_ffn_in)]),))
    pb_ffn_in, own_ffn_in = chip_sum("chip_sum_w_ffn_in", g_ffn_in, ra_ffn_in, c_idx, chip_idx)
    g_w_o = mm_tn("gw_o_attn", attn, d_ya, CC, D)
    g_w_pw = mm_tn("gw_pw_conv", u3, d_yb, CC, D)
    (d_conv, g["conv_w"], g["conv_b"], g["conv_ln_w"], g["conv_ln_b"]), ((ra_out, ra_w_o, ra_w_pw),) = conv_bwd(
        proj, cpre, d_u3, conv_w_f, conv_w_f[::-1], w["conv_ln_w"], w["conv_ln_b"],
        sides=(rs_to_sibling([("rows", g_out), ("cols", g_w_o), ("cols", g_w_pw)]),))
    pb_out, own_out = chip_sum("chip_sum_w_out", g_out, ra_out, c_idx, chip_idx)
    pb_w_o, own_w_o = chip_sum_cols("chip_sum_w_o_attn", g_w_o, ra_w_o, c_idx, chip_idx)
    pb_w_pw, own_w_pw = chip_sum_cols("chip_sum_w_pw_conv", g_w_pw, ra_w_pw, c_idx, chip_idx)
    (d_q, d_k, d_v, gqw, gkw), ((rb_ffn_out, rb_ffn_in, rb_out, rb_w_o, rb_w_pw),) = attn_bwd(
        proj, tabs, qw2, kw2, d_attn, attn, lse,
        sides=(rs_to_chips([("rows", pb_ffn_out), ("rows", pb_ffn_in), ("rows", pb_out),
                            ("cols", pb_w_o), ("cols", pb_w_pw)]),))
    g["q_norm_w"] = gqw[0:1, 0:HD] + gqw[0:1, HD:LANES]
    g["k_norm_w"] = gkw[0:1, 0:HD] + gkw[0:1, HD:LANES]
    mine_w_o, theirs_w_o = col_final("col_final_w_o_attn", own_w_o, rb_w_o, c_idx)
    mine_w_pw, theirs_w_pw = col_final("col_final_w_pw_conv", own_w_pw, rb_w_pw, c_idx)
    d_segs = (d_q, d_k, d_v, d_conv, d_gl)
    parts, to_sibling, to_chips, owns, from_chips = [], None, None, [], []
    for k, hw in enumerate(GW_IN_SPLIT):
        sides = [rs_swap_halves([theirs_w_o, theirs_w_pw])] if k == 0 else []
        sides += [s for s in (to_chips, to_sibling) if s is not None]
        part, outs = gw_in_t("gw_in_%d" % k, h_t, d_segs, sum(GW_IN_SPLIT[:k]), hw, sides=tuple(sides))
        if k == 0:
            (rc_w_o, rc_w_pw), outs = outs[0], outs[1:]
        outs = list(outs)
        if to_chips is not None:
            from_chips.append(outs.pop(0)[0])
        if to_sibling is not None:
            pb, own = chip_sum("chip_sum_w_in_%d" % (k - 1), parts[-1], outs.pop(0)[0], c_idx, chip_idx)
            owns.append(own)
            to_chips = rs_to_chips_combined(pb)
        else:
            to_chips = None
        parts.append(part.reshape(NDEV, INW // NDEV, hw))
        to_sibling = rs_to_sibling([("rows", parts[-1])])
    (grad_x, g["norm1_w"]), ((rb_prev,), (ra_last,)) = in_bwd(
        d_q, d_k, d_v, d_conv, d_gl, w_in_t, x2, d_x1, w["norm1_w"], sides=(to_chips, to_sibling))
    from_chips.append(rb_prev)
    pb, own = chip_sum("chip_sum_w_in_%d" % (len(GW_IN_SPLIT) - 1), parts[-1], ra_last, c_idx, chip_idx)
    owns.append(own)
    small, loss, ((rb_last,),) = small_sync_adam(g, w, m, v, sq, sides=(rs_to_chips_combined(pb),))
    from_chips.append(rb_last)

    res = {
        "w_in": shard_adam("adam_w_in", owns, from_chips, w["w_in"], m["w_in"], v["w_in"]),
        "w_ffn_in": shard_adam("adam_w_ffn_in", [own_ffn_in], [rb_ffn_in], w["w_ffn_in"], m["w_ffn_in"], v["w_ffn_in"]),
        "w_o_attn": adam_cols("adam_w_o_attn", mine_w_o, rc_w_o, w["w_o_attn"], m["w_o_attn"], v["w_o_attn"], c_idx),
        "w_pw_conv": adam_cols("adam_w_pw_conv", mine_w_pw, rc_w_pw, w["w_pw_conv"], m["w_pw_conv"], v["w_pw_conv"], c_idx),
        "w_out": shard_adam("adam_w_out", [own_out], [rb_out], w["w_out"], m["w_out"], v["w_out"]),
        "w_ffn_out": shard_adam("adam_w_ffn_out", [own_ffn_out], [rb_ffn_out],
                                w["w_ffn_out"], m["w_ffn_out"], v["w_ffn_out"]),
    }
    res = {k: tuple(a.T if k in TRANSPOSED else a for a in r) for k, r in res.items()}
    res.update(small)

    def shaped(name, a):
        return a.reshape((1,) + a.shape) if name in MATS or name in ("b_gate", "conv_w") else a

    outs = [loss, grad_x.reshape(1, S, D)]
    for i in range(4):
        outs += [shaped(k, res[k][i]) for k in WEIGHTS]
    return tuple(outs)
```

```python
import functools
from typing import Callable, NamedTuple, Optional

import numpy as np
import jax
import jax.numpy as jnp
from jax import lax
from jax.experimental import pallas as pl
from jax.experimental.pallas import tpu as pltpu

F32 = jnp.float32
BF16 = jnp.bfloat16

S = 2048
D = 1024
HD = 64
QKV = 1536
CC = 512
KW = 31
FF = 2816
INW = 7680
OFF_Q, OFF_K, OFF_V, OFF_CA, OFF_CB, OFF_GA, OFF_GB = 0, 1536, 3072, 4608, 5120, 5632, 6656
DILATIONS = (1, 4, 16)
HALF_SPAN = 64
EPS = 1e-6
NEG_INF = -1e30
ROPE_THETA = 500000.0
ROT_DIM = 16

ADAM_LR = 0.001
ADAM_B1 = 0.9
ADAM_B2 = 0.999
ADAM_EPS = 1e-08
ADAM_WD = 0.01
ADAM_STEP = 10

NDEV = 8
LANES = 128
TM = 256
TQ = 128
VMEM_LIMIT = 56 * 1024 * 1024
MESH = pl.DeviceIdType.MESH


def _cp(**kw):
    return pltpu.CompilerParams(vmem_limit_bytes=VMEM_LIMIT, **kw)


def _row(width, col=0, tm=TM):
    return pl.BlockSpec((tm, width), lambda i: (i, col))


PLANE = 512


def _planes(width, tm=TM):
    return pl.BlockSpec((width // PLANE, tm, PLANE), lambda i: (0, i, 0))


def _res(shape):
    nd = len(shape)
    return pl.BlockSpec(shape, lambda *_: (0,) * nd, pipeline_mode=pl.Buffered(1))


def _dot(a, b):
    return jnp.dot(a, b, preferred_element_type=F32)


def _dot_nt(a, b):
    return lax.dot_general(a, b, (((1,), (1,)), ((), ())), preferred_element_type=F32)


def _dot_tn(a, b):
    return lax.dot_general(a, b, (((0,), (0,)), ((), ())), preferred_element_type=F32)


def _sigmoid(x):
    return jax.nn.sigmoid(x)


def _dsilu(x, sg):
    return sg * (1.0 + x * (1.0 - sg))


ANY = pl.BlockSpec(memory_space=pl.ANY)
VMEM = pl.BlockSpec(memory_space=pltpu.VMEM)


class Side(NamedTuple):
    args: tuple
    in_specs: tuple
    out_shape: tuple
    scratch: tuple
    start: Callable
    finish: Callable
    mid: Optional[Callable] = None
    peers: str = ""


BARRIER_IDS = {"s": 0, "dxy": 1, "dsxy": 2, "sxy": 3, "xy": 4}


def _peer_barrier(peers):
    x, y, c = lax.axis_index("x"), lax.axis_index("y"), lax.axis_index("c")
    where = {"s": (x, y, 1 - c), "x": (1 - x, y, c), "y": (x, 1 - y, c), "d": (1 - x, 1 - y, c)}
    barrier = pltpu.get_barrier_semaphore()
    for p in peers:
        pl.semaphore_signal(barrier, inc=1, device_id=where[p], device_id_type=MESH)
    pl.semaphore_wait(barrier, len(peers))


def _call(body, sides=(), *, name, grid, in_specs, out_specs, out_shape, scratch_shapes=(), args, own_comm=False):
    ni, no, ns = len(in_specs), len(out_specs), len(scratch_shapes)
    cnt = [(len(s.args), len(s.out_shape), len(s.scratch)) for s in sides]
    peers = "".join(sorted(set("".join(s.peers for s in sides))))
    if own_comm or not sides or any(not s.peers for s in sides):
        peers = ""

    def take(refs, pos, n):
        return refs[pos:pos + n], pos + n

    def full(*refs):
        m_in, pos = take(refs, 0, ni)
        s_in = []
        for a, _, _ in cnt:
            r, pos = take(refs, pos, a)
            s_in.append(r)
        m_out, pos = take(refs, pos, no)
        s_out = []
        for _, o, _ in cnt:
            r, pos = take(refs, pos, o)
            s_out.append(r)
        m_scr, pos = take(refs, pos, ns)
        s_scr = []
        for _, _, c in cnt:
            r, pos = take(refs, pos, c)
            s_scr.append(r)
        if sides:
            first = functools.reduce(jnp.logical_and, [pl.program_id(d) == 0 for d in range(len(grid))])
            last = functools.reduce(jnp.logical_and, [pl.program_id(d) == g - 1 for d, g in enumerate(grid)])

            @pl.when(first)
            def _():
                if peers:
                    _peer_barrier(peers)
                for s, a, o, c in zip(sides, s_in, s_out, s_scr):
                    s.start(a, o, c)

            steps = int(np.prod(grid))
            mid_step = (2 * steps) // 3
            if steps > 1 and any(s.mid is not None for s in sides):
                step = functools.reduce(lambda acc, d: acc * grid[d] + pl.program_id(d), range(len(grid)), 0)

                @pl.when(step == mid_step)
                def _():
                    for s, a, o, c in zip(sides, s_in, s_out, s_scr):
                        if s.mid is not None:
                            s.mid(a, o, c)

        body(*m_in, *m_out, *m_scr)
        if sides:
            @pl.when(last)
            def _():
                for s, a, o, c in zip(sides, s_in, s_out, s_scr):
                    if s.mid is not None and steps == 1:
                        s.mid(a, o, c)
                    s.finish(a, o, c)

    res = pl.pallas_call(
        full, name=name, grid=grid,
        in_specs=list(in_specs) + [sp for s in sides for sp in s.in_specs],
        out_specs=list(out_specs) + [ANY for s in sides for _ in s.out_shape],
        out_shape=list(out_shape) + [o for s in sides for o in s.out_shape],
        scratch_shapes=list(scratch_shapes) + [c for s in sides for c in s.scratch],
        compiler_params=_cp(dimension_semantics=("arbitrary",) * len(grid),
                            **({"collective_id": BARRIER_IDS[peers]} if peers else {})),
    )(*args, *[a for s in sides for a in s.args])
    res = list(res)
    if not sides:
        return res
    outs, pos = take(res, 0, no)
    side_outs = []
    for _, o, _ in cnt:
        r, pos = take(res, pos, o)
        side_outs.append(r)
    return outs, side_outs


def _inv_freq_lanes():
    inv = np.float32(ROPE_THETA) ** (-np.arange(0, ROT_DIM, 2, dtype=np.float32) / np.float32(ROT_DIM))
    lane = np.arange(LANES) % HD
    out = np.where(lane < ROT_DIM, inv[lane % (ROT_DIM // 2)], 0.0).astype(np.float32)
    return jnp.asarray(out.reshape(1, LANES))


def _rope_tables(pos, inv_freq):
    ang = pos.astype(F32) * inv_freq
    lane = lax.broadcasted_iota(jnp.int32, ang.shape, 1) % HD
    cs = jnp.cos(ang)
    sn = jnp.sin(ang)
    return (jnp.where(lane < ROT_DIM, cs, 1.0), jnp.where(lane < ROT_DIM // 2, -sn, 0.0),
            jnp.where(lane < ROT_DIM // 2, 0.0, jnp.where(lane < ROT_DIM, sn, 0.0)))


def _rope(v, c, s1, s2):
    return v * c + pltpu.roll(v, LANES - 8, axis=1) * s1 + pltpu.roll(v, 8, axis=1) * s2


def _rope_t(d, c, s1, s2):
    return d * c - pltpu.roll(d, LANES - 8, axis=1) * s1 - pltpu.roll(d, 8, axis=1) * s2


def _head_mat():
    r = lax.broadcasted_iota(jnp.int32, (LANES, LANES), 0) // HD
    c = lax.broadcasted_iota(jnp.int32, (LANES, LANES), 1) // HD
    return jnp.where(r == c, 1.0 / HD, 0.0).astype(BF16)


def _head_mean(t, e):
    hi = t.astype(BF16)
    rest = (t - hi.astype(F32)).astype(BF16)
    return _dot(hi, e) + _dot(rest, e)


def in_proj_gather(x, norm_w, shard_t, chip_order, pos_col):
    R = INW // NDEV
    half, nt = R // 2, S // TM

    def body(ord_ref, x_ref, nw_ref, sh_ref, pos_ref, f_ref, ht_ref, p_ref, wfull_ref, c_ref, s1_ref, s2_ref,
             wt, hs, send, recv, loc):
        kk, i = pl.program_id(0), pl.program_id(1)
        x, y, c, _ = _place()
        me, flip = 4 * x + 2 * y + c, 1 - 2 * c
        here, sib, xn, yn = (x, y, c), (x, y, 1 - c), (1 - x, y, c), (x, 1 - y, c)
        b_xn, b_yn, b_dg = 4 * (1 - x) + 2 * y + c, 4 * x + 2 * (1 - y) + c, 4 * (1 - x) + 2 * (1 - y) + c

        def cp(k, block, to, rows=None):
            dst = wt.at[block] if rows is None else wt.at[block, pl.ds(rows * half, half), :]
            return _remote(dst, dst, send, recv, k, to)

        def sends():
            return [cp(0, me, sib), cp(1, me, xn), cp(2, me, yn), cp(3, b_xn, sib), cp(4, b_yn, sib),
                    cp(5, b_xn, yn, rows=0), cp(6, b_yn, xn, rows=1), cp(7, b_dg, sib, rows=0), cp(8, b_dg, sib, rows=1)]

        def keep(j, blk0):
            pair = pl.ds(pl.multiple_of(blk0, 2), 2)
            return pltpu.make_async_copy(wt.at[pair], wfull_ref.at[pair], loc.at[j])

        @pl.when((kk == 0) & (i == 0))
        def _():
            _peer_barrier("sxy")
            _cast_rows(wt.at[me], sh_ref)
            for s_ in sends()[0:3]:
                s_.start()

            def tables(j, _):
                chunk = pl.ds(pl.multiple_of(j * TM, TM), TM)
                c_ref[chunk, :], s1_ref[chunk, :], s2_ref[chunk, :] = _rope_tables(pos_ref[chunk, :], f_ref[...])
                return 0

            lax.fori_loop(0, nt, tables, 0)
            cp(0, me + flip, here).wait_recv()
            keep(0, me - c).start()

        @pl.when((kk == 1) & (i == 0))
        def _():
            cp(1, b_xn, here).wait_recv()
            sends()[5].start()
            sends()[3].start()
            cp(2, b_yn, here).wait_recv()
            sends()[6].start()
            sends()[4].start()
            cp(3, b_xn + flip, here).wait_recv()
            keep(1, b_xn - c).start()

        @pl.when((kk == 2) & (i == 0))
        def _():
            cp(4, b_yn + flip, here).wait_recv()
            keep(2, b_yn - c).start()

        @pl.when((kk == 3) & (i == 0))
        def _():
            cp(5, b_dg, here, rows=0).wait_recv()
            sends()[7].start()
            cp(6, b_dg, here, rows=1).wait_recv()
            sends()[8].start()
            cp(7, b_dg + flip, here, rows=0).wait_recv()
            cp(8, b_dg + flip, here, rows=1).wait_recv()
            keep(3, b_dg - c).start()

        rows = pl.ds(pl.multiple_of(i * TM, TM), TM)

        @pl.when(kk == 0)
        def _():
            xv = x_ref[...]
            r = lax.rsqrt(jnp.mean(xv * xv, axis=-1, keepdims=True) + EPS)
            hf = xv * r * nw_ref[...]
            ht_ref[...] = hf.T.astype(BF16)
            hs[rows, :] = hf.astype(BF16)

        h = hs[rows, :]
        chip = ord_ref[kk]
        for cc in range(2):
            p_ref[:, cc * R:(cc + 1) * R] = _dot_nt(h, wt[2 * chip + cc])

        @pl.when((kk == 3) & (i == nt - 1))
        def _():
            for s_ in sends():
                s_.wait_send()
            for j, blk in enumerate((me, b_xn, b_yn, b_dg)):
                keep(j, blk - c).wait()

    def first_pass(kk, i):
        return jnp.where(kk == 0, i, nt - 1)

    grid_spec = pltpu.PrefetchScalarGridSpec(
        num_scalar_prefetch=1, grid=(4, nt),
        in_specs=[pl.BlockSpec((TM, D), lambda kk, i, o: (first_pass(kk, i), 0)),
                  pl.BlockSpec((1, D), lambda kk, i, o: (0, 0)), VMEM, VMEM,
                  pl.BlockSpec((1, LANES), lambda kk, i, o: (0, 0))],
        out_specs=[pl.BlockSpec((D, TM), lambda kk, i, o: (0, first_pass(kk, i))),
                   pl.BlockSpec((TM, 2 * R), lambda kk, i, o: (i, o[kk])), ANY]
        + [pl.BlockSpec((S, LANES), lambda kk, i, o: (0, 0))] * 3,
        scratch_shapes=[pltpu.VMEM((NDEV, R, D), BF16), pltpu.VMEM((S, D), BF16), _sems(9), _sems(9), _sems(4)])
    res = pl.pallas_call(
        body, name="in_proj_gather", grid_spec=grid_spec,
        out_shape=[jax.ShapeDtypeStruct((D, S), BF16), jax.ShapeDtypeStruct((S, INW), F32),
                   jax.ShapeDtypeStruct((NDEV, R, D), BF16)] + [jax.ShapeDtypeStruct((S, LANES), F32)] * 3,
        compiler_params=_cp(dimension_semantics=("arbitrary", "arbitrary"), collective_id=BARRIER_IDS["sxy"]),
    )(chip_order, x, norm_w, shard_t, pos_col, _inv_freq_lanes())
    return res[0], res[1], res[2], tuple(res[3:])


def _qk_specs():
    nb = QKV // LANES
    return [pl.BlockSpec((S, LANES), functools.partial(lambda hp, g, o: (0, o + g * 4 + hp), o=o))
            for o in (OFF_Q // LANES, OFF_K // LANES, OFF_V // LANES)]


def _tab_specs():
    return [pl.BlockSpec((S, LANES), lambda hp, g: (0, 0), pipeline_mode=pl.Buffered(1))] * 3


def _vec_spec():
    return pl.BlockSpec((1, LANES), lambda hp, g: (0, 0))


def _sub_rows(r, d, start, n):
    if d == 1:
        return pl.ds(start, n)
    return pl.ds(r + d * start, n, stride=d)


def _band_window(i, L):
    W = min(TQ + 2 * HALF_SPAN, L)
    q0 = pl.multiple_of(i * TQ, TQ)
    k0 = pl.multiple_of(jnp.clip(q0 - HALF_SPAN, 0, L - W), HALF_SPAN)
    qpos = q0 + (lax.broadcasted_iota(jnp.int32, (2 * TQ, W), 0) & (TQ - 1))
    kpos = k0 + lax.broadcasted_iota(jnp.int32, (2 * TQ, W), 1)
    valid = jnp.abs(qpos - kpos) <= HALF_SPAN
    return W, q0, k0, valid


def _stack_heads(t, lo):
    z = jnp.zeros_like(t)
    return jnp.concatenate([jnp.where(lo, t, z), jnp.where(lo, z, t)], axis=0)


def _unstack_heads(t2, lo):
    return jnp.where(lo, t2[0:TQ], t2[TQ:2 * TQ])


CHAINS = 8


def _interleave(d):
    ru = min(d, CHAINS)
    return ru, min(CHAINS // ru, S // d // TQ)


def _for_blocks(n, fn):
    if n == 1:
        fn(0)
    else:
        def it(j, _):
            fn(j)
            return 0
        lax.fori_loop(0, n, it, 0)


def attn_fwd(proj, tabs, qw2, kw2, sides=()):
    CH = 256

    def body(q_ref, k_ref, v_ref, c_ref, s1_ref, s2_ref, qw_ref, kw_ref, at_ref, ls_ref,
             qs, ks, vs, osub, lsub, onat, lnat, qn, kn):
        g = pl.program_id(1)
        lo = lax.broadcasted_iota(jnp.int32, (1, LANES), 1) < HD
        e = _head_mat()

        def prep(i, _):
            rows = pl.ds(pl.multiple_of(i * CH, CH), CH)
            c, s1, s2 = c_ref[rows, :], s1_ref[rows, :], s2_ref[rows, :]
            for t_ref, w_ref, out, scale in ((q_ref, qw_ref, qn, HD ** -0.5), (k_ref, kw_ref, kn, 1.0)):
                t = t_ref[rows, :]
                r = lax.rsqrt(_head_mean(t * t, e) + EPS)
                out[rows, :] = _rope(t * r * w_ref[...], c, s1, s2) * scale
            return 0

        lax.fori_loop(0, S // CH, prep, 0, unroll=4)

        def group(gi, d):
            L = S // d

            ru, nb = _interleave(d)

            def stage(r, off):
                for c0 in range(0, L, CH):
                    n = min(CH, L)
                    rows = _sub_rows(r, d, c0, n)
                    dst = pl.ds(off + c0, n)
                    qs[dst, :] = qn[rows, :].astype(BF16)
                    ks[dst, :] = kn[rows, :].astype(BF16)
                    vs[dst, :] = v_ref[rows, :].astype(BF16)

            def one(off, i):
                W, q0, k0, valid = _band_window(i, L)
                q2 = _stack_heads(qs[pl.ds(off + q0, TQ), :], lo)
                sc = jnp.where(valid, _dot_nt(q2, ks[pl.ds(off + k0, W), :]), NEG_INF)
                m = jnp.max(sc, axis=-1, keepdims=True)
                p = jnp.exp(sc - m)
                den = jnp.sum(p, axis=-1, keepdims=True)
                o2 = _dot(p.astype(BF16), vs[pl.ds(off + k0, W), :]) / den
                l2 = jnp.broadcast_to(m + jnp.log(den), (2 * TQ, LANES))
                osub[pl.ds(off + q0, TQ), :] = _unstack_heads(o2, lo)
                lsub[pl.ds(off + q0, TQ), :] = _unstack_heads(l2, lo)

            def unstage(r, off):
                for c0 in range(0, L, CH):
                    n = min(CH, L)
                    rows = _sub_rows(r, d, c0, n)
                    onat[gi, rows, :] = osub[pl.ds(off + c0, n), :]
                    lnat[gi, rows, :] = lsub[pl.ds(off + c0, n), :]

            def step(t, _):
                for u in range(ru):
                    stage(t * ru + u, u * L)
                _for_blocks(L // TQ // nb, lambda j: [one(u * L, j * nb + b) for u in range(ru) for b in range(nb)])
                for u in range(ru):
                    unstage(t * ru + u, u * L)
                return 0

            lax.fori_loop(0, d // ru, step, 0)

        for gi, d in enumerate(DILATIONS):
            pl.when(g == gi)(functools.partial(group, gi, d))

        @pl.when(g == len(DILATIONS) - 1)
        def _():
            def mix(i, _):
                rows = pl.ds(pl.multiple_of(i * CH, CH), CH)
                l0, l1, l2 = lnat[0, rows, :], lnat[1, rows, :], lnat[2, rows, :]
                m = jnp.maximum(jnp.maximum(l0, l1), l2)
                e0, e1, e2 = jnp.exp(l0 - m), jnp.exp(l1 - m), jnp.exp(l2 - m)
                den = e0 + e1 + e2
                a = (e0 * onat[0, rows, :] + e1 * onat[1, rows, :] + e2 * onat[2, rows, :]) / den
                at_ref[rows, :] = a.astype(BF16)
                ls_ref[rows, :] = m + jnp.log(den)
                return 0

            lax.fori_loop(0, S // CH, mix, 0)

    out_spec = pl.BlockSpec((S, LANES), lambda hp, g: (0, hp))
    return _call(
        body, sides, name="attn_fwd", grid=(4, 3),
        in_specs=_qk_specs() + _tab_specs() + [_vec_spec(), _vec_spec()],
        out_specs=[out_spec, out_spec],
        out_shape=[jax.ShapeDtypeStruct((S, CC), BF16), jax.ShapeDtypeStruct((S, CC), F32)],
        scratch_shapes=[pltpu.VMEM((S, LANES), BF16)] * 3 + [pltpu.VMEM((S, LANES), F32)] * 2
        + [pltpu.VMEM((3, S, LANES), F32)] * 2 + [pltpu.VMEM((S, LANES), F32)] * 2,
        args=(proj, proj, proj, *tabs, qw2, kw2))


def attn_bwd(proj, tabs, qw2, kw2, d_attn, attn, lse, sides=()):
    CH = 256

    def body(q_ref, k_ref, v_ref, c_ref, s1_ref, s2_ref, qw_ref, kw_ref, do_ref, at_ref, ls_ref,
             dq_ref, dk_ref, dv_ref, gqw_ref, gkw_ref,
             qs, ks, vs, dos, dsub, lsub, dqs, dks, dvs, dnat, qx, kx, dvn, tnq, tnk, rrq, rrk):
        hp, g = pl.program_id(0), pl.program_id(1)
        lo = lax.broadcasted_iota(jnp.int32, (1, LANES), 1) < HD
        e = _head_mat()
        both = ((q_ref, qw_ref, qx, tnq, rrq, HD ** -0.5), (k_ref, kw_ref, kx, tnk, rrk, 1.0))

        @pl.when((hp == 0) & (g == 0))
        def _():
            gqw_ref[...] = jnp.zeros_like(gqw_ref)
            gkw_ref[...] = jnp.zeros_like(gkw_ref)

        def prep(i, _):
            rows = pl.ds(pl.multiple_of(i * CH, CH), CH)
            dnat[rows, :] = _head_mean(do_ref[rows, :] * at_ref[rows, :].astype(F32), e) * float(HD)
            c, s1, s2 = c_ref[rows, :], s1_ref[rows, :], s2_ref[rows, :]
            for t_ref, w_ref, x, tn_s, rr_s, scale in both:
                t = t_ref[rows, :]
                rr = lax.rsqrt(_head_mean(t * t, e) + EPS)
                tn = t * rr
                rr_s[rows, :] = rr
                tn_s[rows, :] = tn
                x[rows, :] = _rope(tn * w_ref[...], c, s1, s2) * scale
            return 0

        lax.fori_loop(0, S // CH, prep, 0, unroll=4)

        def group(d):
            L = S // d

            ru, nb = _interleave(d)

            def stage(r, off):
                for c0 in range(0, L, CH):
                    n = min(CH, L)
                    rows = _sub_rows(r, d, c0, n)
                    dst = pl.ds(off + c0, n)
                    qs[dst, :] = qx[rows, :].astype(BF16)
                    ks[dst, :] = kx[rows, :].astype(BF16)
                    vs[dst, :] = v_ref[rows, :].astype(BF16)
                    dos[dst, :] = do_ref[rows, :].astype(BF16)
                    dsub[dst, :] = dnat[rows, :]
                    lsub[dst, :] = ls_ref[rows, :]
                    dks[dst, :] = jnp.zeros((n, LANES), F32)
                    dvs[dst, :] = jnp.zeros((n, LANES), F32)

            def one(off, i):
                W, q0, k0, valid = _band_window(i, L)
                qrows, krows = pl.ds(off + q0, TQ), pl.ds(off + k0, W)
                q2 = _stack_heads(qs[qrows, :], lo)
                do2 = _stack_heads(dos[qrows, :], lo)
                kk, vv = ks[krows, :], vs[krows, :]
                lse_b, dd_b = lsub[qrows, :], dsub[qrows, :]
                lse2 = jnp.concatenate([lse_b[:, 0:1], lse_b[:, HD:HD + 1]], axis=0)
                dd2 = jnp.concatenate([dd_b[:, 0:1], dd_b[:, HD:HD + 1]], axis=0)
                sc = jnp.where(valid, _dot_nt(q2, kk), NEG_INF)
                p = jnp.exp(sc - lse2)
                ds = (p * (_dot_nt(do2, vv) - dd2)).astype(BF16)
                dqs[qrows, :] = _unstack_heads(_dot(ds, kk), lo)
                dks[krows, :] = dks[krows, :] + _dot_tn(ds, q2)
                dvs[krows, :] = dvs[krows, :] + _dot_tn(p.astype(BF16), do2)

            def unstage(r, off):
                for c0 in range(0, L, CH):
                    n = min(CH, L)
                    rows = _sub_rows(r, d, c0, n)
                    src = pl.ds(off + c0, n)
                    qx[rows, :] = dqs[src, :]
                    kx[rows, :] = dks[src, :]
                    dvn[rows, :] = dvs[src, :]

            def step(t, _):
                for u in range(ru):
                    stage(t * ru + u, u * L)
                _for_blocks(L // TQ // nb, lambda j: [one(u * L, j * nb + b) for u in range(ru) for b in range(nb)])
                for u in range(ru):
                    unstage(t * ru + u, u * L)
                return 0

            lax.fori_loop(0, d // ru, step, 0)

        for gi, d in enumerate(DILATIONS):
            pl.when(g == gi)(functools.partial(group, d))

        def emit(i, _):
            rows = pl.ds(pl.multiple_of(i * CH, CH), CH)
            c, s1, s2 = c_ref[rows, :], s1_ref[rows, :], s2_ref[rows, :]
            for (_, w_ref, x, tn_s, rr_s, scale), out, gw_ref in zip(both, (dq_ref, dk_ref), (gqw_ref, gkw_ref)):
                tn = tn_s[rows, :]
                dy = _rope_t(x[rows, :] * scale, c, s1, s2)
                gw_ref[0:1, :] = gw_ref[0:1, :] + jnp.sum(dy * tn, axis=0, keepdims=True)
                dtn = dy * w_ref[...]
                out[rows, :] = (rr_s[rows, :] * (dtn - tn * _head_mean(dtn * tn, e))).astype(BF16)
            dv_ref[rows, :] = dvn[rows, :].astype(BF16)
            return 0

        lax.fori_loop(0, S // CH, emit, 0, unroll=4)

    nat_spec = pl.BlockSpec((S, LANES), lambda hp, g: (0, hp))
    out_spec = pl.BlockSpec((None, S, LANES), lambda hp, g: (g, 0, hp))
    acc_spec = pl.BlockSpec((8, LANES), lambda hp, g: (0, 0))
    return _call(
        body, sides, name="attn_bwd", grid=(4, 3),
        in_specs=_qk_specs() + _tab_specs() + [_vec_spec(), _vec_spec(), nat_spec, nat_spec, nat_spec],
        out_specs=[out_spec] * 3 + [acc_spec] * 2,
        out_shape=[jax.ShapeDtypeStruct((QKV // PLANE, S, PLANE), BF16)] * 3 + [jax.ShapeDtypeStruct((8, LANES), F32)] * 2,
        scratch_shapes=[pltpu.VMEM((S, LANES), BF16)] * 4 + [pltpu.VMEM((S, LANES), F32)] * 13,
        args=(proj, proj, proj, *tabs, qw2, kw2, d_attn, attn, lse))


PADR = 16
CT = 128


def _conv_specs():
    return [pl.BlockSpec((S, CC), lambda i: (0, OFF_CA // CC)), pl.BlockSpec((S, CC), lambda i: (0, OFF_CB // CC))]


NCB = CC // LANES


def _pad_zero(pad):
    for cb in range(NCB):
        pad[cb, 0:PADR, :] = jnp.zeros((PADR, LANES), F32)
        pad[cb, PADR + S:PADR + S + PADR, :] = jnp.zeros((PADR, LANES), F32)


def _pad_store(pad, row0, n, val):
    for cb in range(NCB):
        pad[cb, pl.ds(pl.multiple_of(row0 + PADR, 8), n), :] = val[:, cb * LANES:(cb + 1) * LANES]


def _taps(pad_ref, cb, s0, weights):
    acc = jnp.zeros((CT, LANES), F32)
    for k in range(KW):
        acc = acc + weights[k] * pad_ref[cb, pl.ds(s0 + k + 1, CT), :]
    return acc


def conv_fwd(proj, conv_w, conv_b, ln_w, ln_b, sides=()):
    def body(a_ref, b_ref, w_ref, cb_ref, lw_ref, lb_ref, c_ref, u3_ref, upad):
        _pad_zero(upad)

        def glu(i, _):
            rows = pl.ds(pl.multiple_of(i * TM, TM), TM)
            _pad_store(upad, i * TM, TM, a_ref[rows, :] * _sigmoid(b_ref[rows, :]))
            return 0

        lax.fori_loop(0, S // TM, glu, 0)

        def chunk(i, _):
            s0 = pl.multiple_of(i * CT, CT)
            for cb in range(CC // LANES):
                cols = slice(cb * LANES, (cb + 1) * LANES)
                w = [w_ref[k:k + 1, cols] for k in range(KW)]
                c_ref[pl.ds(s0, CT), cols] = _taps(upad, cb, s0, w) + cb_ref[:, cols]
            cv = c_ref[pl.ds(s0, CT), :]
            mu = jnp.mean(cv, axis=-1, keepdims=True)
            xc = cv - mu
            rstd = lax.rsqrt(jnp.mean(xc * xc, axis=-1, keepdims=True) + EPS)
            yl = xc * rstd * lw_ref[...] + lb_ref[...]
            u3_ref[pl.ds(s0, CT), :] = (yl * _sigmoid(yl)).astype(BF16)
            return 0

        lax.fori_loop(0, S // CT, chunk, 0)

    vec = pl.BlockSpec((1, CC), lambda i: (0, 0))
    full = pl.BlockSpec((S, CC), lambda i: (0, 0))
    return _call(
        body, sides, name="conv_fwd", grid=(1,),
        in_specs=_conv_specs() + [pl.BlockSpec((KW, CC), lambda i: (0, 0)), vec, vec, vec],
        out_specs=[full, full],
        out_shape=[jax.ShapeDtypeStruct((S, CC), F32), jax.ShapeDtypeStruct((S, CC), BF16)],
        scratch_shapes=[pltpu.VMEM((NCB, S + 2 * PADR, LANES), F32)],
        args=(proj, proj, conv_w, conv_b, ln_w, ln_b))


def conv_bwd(proj, cpre, d_u3, conv_w, conv_w_rev, ln_w, ln_b, sides=()):
    def body(a_ref, b_ref, c_ref, du3_ref, w_ref, wr_ref, lw_ref, lb_ref,
             dc_ref, gw_ref, gcb_ref, glw_ref, glb_ref, upad, dpad):
        _pad_zero(upad)
        _pad_zero(dpad)
        gw_ref[...] = jnp.zeros_like(gw_ref)

        def ln_bwd(i, carry):
            gcb, glw, glb = carry
            rows = pl.ds(pl.multiple_of(i * TM, TM), TM)
            _pad_store(upad, i * TM, TM, a_ref[rows, :] * _sigmoid(b_ref[rows, :]))
            cv = c_ref[rows, :]
            mu = jnp.mean(cv, axis=-1, keepdims=True)
            xc = cv - mu
            rstd = lax.rsqrt(jnp.mean(xc * xc, axis=-1, keepdims=True) + EPS)
            xh = xc * rstd
            yl = xh * lw_ref[...] + lb_ref[...]
            dyl = du3_ref[rows, :] * _dsilu(yl, _sigmoid(yl))
            dxh = dyl * lw_ref[...]
            dcv = rstd * (dxh - jnp.mean(dxh, axis=-1, keepdims=True)
                          - xh * jnp.mean(dxh * xh, axis=-1, keepdims=True))
            _pad_store(dpad, i * TM, TM, dcv)
            return (gcb + jnp.sum(dcv, axis=0, keepdims=True),
                    glw + jnp.sum(dyl * xh, axis=0, keepdims=True),
                    glb + jnp.sum(dyl, axis=0, keepdims=True))

        z = jnp.zeros((1, CC), F32)
        gcb, glw, glb = lax.fori_loop(0, S // TM, ln_bwd, (z, z, z))
        gcb_ref[...] = gcb
        glw_ref[...] = glw
        glb_ref[...] = glb

        def chunk(i, _):
            s0 = pl.multiple_of(i * CT, CT)
            for cb in range(CC // LANES):
                cols = slice(cb * LANES, (cb + 1) * LANES)
                wr = [wr_ref[k:k + 1, cols] for k in range(KW)]
                du = _taps(dpad, cb, s0, wr)
                dcv = dpad[cb, pl.ds(s0 + PADR, CT), :]
                for k in range(KW):
                    gw_ref[k:k + 1, cols] = gw_ref[k:k + 1, cols] + jnp.sum(
                        upad[cb, pl.ds(s0 + k + 1, CT), :] * dcv, axis=0, keepdims=True)
                av = a_ref[pl.ds(s0, CT), cols]
                sb = _sigmoid(b_ref[pl.ds(s0, CT), cols])
                dc_ref[0, pl.ds(s0, CT), cols] = (du * sb).astype(BF16)
                dc_ref[1, pl.ds(s0, CT), cols] = (du * av * sb * (1.0 - sb)).astype(BF16)
            return 0

        lax.fori_loop(0, S // CT, chunk, 0)

    vec = pl.BlockSpec((1, CC), lambda i: (0, 0))
    full = pl.BlockSpec((S, CC), lambda i: (0, 0))
    wsp = pl.BlockSpec((KW, CC), lambda i: (0, 0))
    return _call(
        body, sides, name="conv_bwd", grid=(1,),
        in_specs=_conv_specs() + [full, full, wsp, wsp, vec, vec],
        out_specs=[pl.BlockSpec((2, S, CC), lambda i: (0, 0, 0)), wsp, vec, vec, vec],
        out_shape=[jax.ShapeDtypeStruct((2, S, CC), BF16), jax.ShapeDtypeStruct((KW, CC), F32)]
        + [jax.ShapeDtypeStruct((1, CC), F32)] * 3,
        scratch_shapes=[pltpu.VMEM((NCB, S + 2 * PADR, LANES), F32)] * 2,
        args=(proj, proj, cpre, d_u3, conv_w, conv_w_rev, ln_w, ln_b))


def _gate_specs():
    return [_row(CC, col=OFF_GA // CC + j) for j in range(4)]


def _gates(g_refs, bg_ref):
    ga = _sigmoid(jnp.concatenate([g_refs[0][...], g_refs[1][...]], axis=1) + bg_ref[0:1, :])
    gb = _sigmoid(jnp.concatenate([g_refs[2][...], g_refs[3][...]], axis=1) + bg_ref[1:2, :])
    return ga, gb


def mix_out(x, proj, b_gate, attn, u3, w_o, w_pw, w_out):
    def body(x_ref, g0, g1, g2, g3, bg_ref, at_ref, u3_ref, wo_ref, wp_ref, wout_ref,
             x1_ref, z_ref, ya_ref, yb_ref):
        ga, gb = _gates((g0, g1, g2, g3), bg_ref)
        ya = _dot(at_ref[...], wo_ref[...])
        yb = _dot(u3_ref[...], wp_ref[...])
        z = (ga * ya + gb * yb).astype(BF16)
        ya_ref[...] = ya.astype(BF16)
        yb_ref[...] = yb.astype(BF16)
        z_ref[...] = z
        x1_ref[...] = x_ref[...] + _dot(z, wout_ref[...])

    return pl.pallas_call(
        body, name="mix_out", grid=(S // TM,),
        in_specs=[_row(D)] + _gate_specs() + [_res((2, D)), _row(CC), _row(CC),
                                              _res((CC, D)), _res((CC, D)), _res((D, D))],
        out_specs=[_row(D)] * 4,
        out_shape=[jax.ShapeDtypeStruct((S, D), F32)] + [jax.ShapeDtypeStruct((S, D), BF16)] * 3,
        compiler_params=_cp(dimension_semantics=("arbitrary",)),
    )(x, proj, proj, proj, proj, b_gate, attn, u3, w_o, w_pw, w_out)


def out_bwd(d_x1b, proj, b_gate, ya, yb, w_o, w_pw, w_out, sides=()):
    def body(dx_ref, g0, g1, g2, g3, bg_ref, ya_ref, yb_ref, wo_ref, wp_ref, wout_ref,
             dya_ref, dyb_ref, dgl_ref, dat_ref, du3_ref, gbg_ref):
        @pl.when(pl.program_id(0) == 0)
        def _():
            gbg_ref[...] = jnp.zeros_like(gbg_ref)

        ga, gb = _gates((g0, g1, g2, g3), bg_ref)
        dz = _dot_nt(dx_ref[...], wout_ref[...])
        dya = (dz * ga).astype(BF16)
        dyb = (dz * gb).astype(BF16)
        dgla = dz * ya_ref[...].astype(F32) * ga * (1.0 - ga)
        dglb = dz * yb_ref[...].astype(F32) * gb * (1.0 - gb)
        dya_ref[...] = dya
        dyb_ref[...] = dyb
        for j in range(2):
            dgl_ref[j] = dgla[:, j * PLANE:(j + 1) * PLANE].astype(BF16)
            dgl_ref[2 + j] = dglb[:, j * PLANE:(j + 1) * PLANE].astype(BF16)
        gbg_ref[0:1, :] = gbg_ref[0:1, :] + jnp.sum(dgla, axis=0, keepdims=True)
        gbg_ref[1:2, :] = gbg_ref[1:2, :] + jnp.sum(dglb, axis=0, keepdims=True)
        dat_ref[...] = _dot_nt(dya, wo_ref[...])
        du3_ref[...] = _dot_nt(dyb, wp_ref[...])

    return _call(
        body, sides, name="out_bwd", grid=(S // TM,),
        in_specs=[_row(D)] + _gate_specs() + [_res((2, D)), _row(D), _row(D),
                                              _res((CC, D)), _res((CC, D)), _res((D, D))],
        out_specs=[_row(D), _row(D), _planes(2 * D), _row(CC), _row(CC), pl.BlockSpec((2, D), lambda i: (0, 0))],
        out_shape=[jax.ShapeDtypeStruct((S, D), BF16)] * 2 + [jax.ShapeDtypeStruct((2 * D // PLANE, S, PLANE), BF16)]
        + [jax.ShapeDtypeStruct((S, CC), F32)] * 2 + [jax.ShapeDtypeStruct((2, D), F32)],
        args=(d_x1b, proj, proj, proj, proj, b_gate, ya, yb, w_o, w_pw, w_out))


def ffn_in(x1, norm_w, w_ffn_in, sides=()):
    half = FF // 2

    def body(x_ref, nw_ref, w_ref, h_ref, gu_ref, f_ref):
        xv = x_ref[...]
        r = lax.rsqrt(jnp.mean(xv * xv, axis=-1, keepdims=True) + EPS)
        h = (xv * r * nw_ref[...]).astype(BF16)
        h_ref[...] = h
        for j in range(2):
            gt = _dot_nt(h, w_ref[j * half:(j + 1) * half, :])
            up = _dot_nt(h, w_ref[FF + j * half:FF + (j + 1) * half, :])
            gu_ref[:, j * half:(j + 1) * half] = gt.astype(BF16)
            gu_ref[:, FF + j * half:FF + (j + 1) * half] = up.astype(BF16)
            f_ref[:, j * half:(j + 1) * half] = (gt * _sigmoid(gt) * up).astype(BF16)

    return _call(
        body, sides, name="ffn_in", grid=(S // TM,),
        in_specs=[_row(D), _res((1, D)), _res((2 * FF, D))],
        out_specs=[_row(D), _row(2 * FF), _row(FF)],
        out_shape=[jax.ShapeDtypeStruct((S, D), BF16), jax.ShapeDtypeStruct((S, 2 * FF), BF16),
                   jax.ShapeDtypeStruct((S, FF), BF16)],
        args=(x1, norm_w, w_ffn_in))


def ffn_out_loss(x1, f, w_ffn_out, target):
    def body(x_ref, f_ref, w_ref, t_ref, dy_ref, dyb_ref, sq_ref):
        @pl.when(pl.program_id(0) == 0)
        def _():
            sq_ref[...] = jnp.zeros_like(sq_ref)

        diff = x_ref[...] + _dot(f_ref[...], w_ref[...]) - t_ref[...]
        dy = diff * (1.0 / D)
        dy_ref[...] = dy
        dyb_ref[...] = dy.astype(BF16)
        sq_ref[...] = sq_ref[...] + jnp.sum((diff * diff).reshape(TM // 8, 8, D), axis=0)

    return pl.pallas_call(
        body, name="ffn_out_loss", grid=(S // TM,),
        in_specs=[_row(D), _row(FF), _res((FF, D)), _row(D)],
        out_specs=[_row(D), _row(D), pl.BlockSpec((8, D), lambda i: (0, 0))],
        out_shape=[jax.ShapeDtypeStruct((S, D), F32), jax.ShapeDtypeStruct((S, D), BF16),
                   jax.ShapeDtypeStruct((8, D), F32)],
        compiler_params=_cp(dimension_semantics=("arbitrary",)),
    )(x1, f, w_ffn_out, target)


def _rms_bwd(xv, nw, dh):
    r = lax.rsqrt(jnp.mean(xv * xv, axis=-1, keepdims=True) + EPS)
    xn = xv * r
    dxn = dh * nw
    dx = r * (dxn - xn * jnp.mean(dxn * xn, axis=-1, keepdims=True))
    return dx, dh * xn


def ffn_bwd(dy, dyb, gu, x1, norm_w, w_ffn_in, w_ffn_out, sides=()):
    def body(dy_ref, dyb_ref, gu_ref, x_ref, nw_ref, wi_ref, wo_ref, dgu_ref, dx_ref, dxb_ref, gn_ref):
        @pl.when(pl.program_id(0) == 0)
        def _():
            gn_ref[...] = jnp.zeros_like(gn_ref)

        df = _dot_nt(dyb_ref[...], wo_ref[...])
        gt = gu_ref[:, 0:FF].astype(F32)
        up = gu_ref[:, FF:2 * FF].astype(F32)
        sg = _sigmoid(gt)
        dgt = (df * up * _dsilu(gt, sg)).astype(BF16)
        dup = (df * gt * sg).astype(BF16)
        dgu_ref[:, 0:FF] = dgt
        dgu_ref[:, FF:2 * FF] = dup
        dh = _dot(dgt, wi_ref[0:FF, :]) + _dot(dup, wi_ref[FF:2 * FF, :])
        dxn, gw = _rms_bwd(x_ref[...], nw_ref[...], dh)
        dx = dy_ref[...] + dxn
        dx_ref[...] = dx
        dxb_ref[...] = dx.astype(BF16)
        gn_ref[...] = gn_ref[...] + jnp.sum(gw, axis=0, keepdims=True)

    return _call(
        body, sides, name="ffn_bwd", grid=(S // TM,),
        in_specs=[_row(D), _row(D), _row(2 * FF), _row(D), _res((1, D)), _res((2 * FF, D)), _res((FF, D))],
        out_specs=[_row(2 * FF), _row(D), _row(D), pl.BlockSpec((1, D), lambda i: (0, 0))],
        out_shape=[jax.ShapeDtypeStruct((S, 2 * FF), BF16), jax.ShapeDtypeStruct((S, D), F32),
                   jax.ShapeDtypeStruct((S, D), BF16), jax.ShapeDtypeStruct((1, D), F32)],
        args=(dy, dyb, gu, x1, norm_w, w_ffn_in, w_ffn_out))


def in_bwd(d_q, d_k, d_v, d_conv, d_gl, w_in, x, d_x1, norm_w, sides=()):
    segs = ((OFF_Q, QKV), (OFF_K, QKV), (OFF_V, QKV), (OFF_CA, 2 * CC), (OFF_GA, 2 * D))

    def body(dq_ref, dk_ref, dv_ref, dc_ref, dg_ref, w_ref, x_ref, dx1_ref, nw_ref, gx_ref, gn_ref):
        @pl.when(pl.program_id(0) == 0)
        def _():
            gn_ref[...] = jnp.zeros_like(gn_ref)

        dh = jnp.zeros((TM, D), F32)
        for ref, (off, width) in zip((dq_ref, dk_ref, dv_ref, dc_ref, dg_ref), segs):
            for j in range(width // PLANE):
                dh = dh + _dot(ref[j], w_ref[off + j * PLANE:off + (j + 1) * PLANE, :])
        dxn, gw = _rms_bwd(x_ref[...], nw_ref[...], dh)
        gx_ref[...] = dx1_ref[...] + dxn
        gn_ref[...] = gn_ref[...] + jnp.sum(gw, axis=0, keepdims=True)

    return _call(
        body, sides, name="in_bwd", grid=(S // TM,),
        in_specs=[_planes(QKV)] * 3 + [_planes(2 * CC), _planes(2 * D), _res((INW, D)), _row(D), _row(D), _res((1, D))],
        out_specs=[_row(D), pl.BlockSpec((1, D), lambda i: (0, 0))],
        out_shape=[jax.ShapeDtypeStruct((S, D), F32), jax.ShapeDtypeStruct((1, D), F32)],
        args=(d_q, d_k, d_v, d_conv, d_gl, w_in, x, d_x1, norm_w))


def mm_tn(name, a, b, tm, tn, sides=()):
    M, N = a.shape[1], b.shape[1]

    def body(a_ref, b_ref, o_ref):
        o_ref[...] = _dot_tn(a_ref[...], b_ref[...])

    res = _call(
        body, sides, name=name, grid=(M // tm, N // tn),
        in_specs=[pl.BlockSpec((S, tm), lambda i, j: (0, i)), pl.BlockSpec((S, tn), lambda i, j: (0, j))],
        out_specs=[pl.BlockSpec((tm, tn), lambda i, j: (i, j))],
        out_shape=[jax.ShapeDtypeStruct((M, N), F32)],
        args=(a, b))
    return (res[0][0], res[1]) if sides else res[0]


GW_IN_TN = PLANE
GW_IN_SPLIT = (768, 256)


def gw_in_t(name, ht, d_segs, col0, hw, sides=()):
    tn = GW_IN_TN
    starts, t0 = [], 0
    for seg in d_segs:
        starts.append(t0)
        t0 += seg.shape[0]
    ntiles = [seg.shape[0] for seg in d_segs]

    def body(h_ref, *refs):
        a_refs, o_ref = refs[:-1], refs[-1]
        n = pl.program_id(0)
        for a_ref, st, nt in zip(a_refs, starts, ntiles):
            @pl.when((n >= st) & (n < st + nt))
            def _(a_ref=a_ref):
                o_ref[...] = _dot(h_ref[...], a_ref[...]).T

    def seg_spec(st, nt):
        return pl.BlockSpec((None, S, tn), lambda n: (jnp.clip(n - st, 0, nt - 1), 0, 0))

    res = _call(
        body, sides, name=name, grid=(INW // tn,),
        in_specs=[pl.BlockSpec((hw, S), lambda n: (col0 // hw, 0))] + [seg_spec(st, nt) for st, nt in zip(starts, ntiles)],
        out_specs=[pl.BlockSpec((tn, hw), lambda n: (n, 0))],
        out_shape=[jax.ShapeDtypeStruct((INW, hw), F32)],
        args=(ht, *d_segs))
    return (res[0][0], res[1]) if sides else res[0]


def _place():
    x, y, c = lax.axis_index("x"), lax.axis_index("y"), lax.axis_index("c")
    chips = [(1 - x, y), (x, 1 - y), (1 - x, 1 - y)]
    return x, y, c, chips


def _sems(n):
    return pltpu.SemaphoreType.DMA((n,))


def _remote(src, dst, send, recv, k, to):
    return pltpu.make_async_remote_copy(src_ref=src, dst_ref=dst, send_sem=send.at[k], recv_sem=recv.at[k],
                                        device_id=to, device_id_type=MESH)


def _cast_rows(dst, src, cols=slice(None)):
    rows = src.shape[0]
    step = next((s for s in (128, 64, 32, 16) if rows % s == 0), rows)
    for r0 in range(0, rows, step):
        dst[r0:r0 + step, cols] = src[r0:r0 + step, :].astype(dst.dtype)


def comm_only(name, sides):
    def body():
        pass

    return _call(body, sides, name=name, grid=(1,), in_specs=[], out_specs=[], out_shape=[], args=())[1]


def ag_blocks(shard, dtype):
    R, W = shard.shape

    def copy(outs, scr, k, block, to, src=None):
        dst = outs[0].at[block]
        return _remote(dst if src is None else src, dst, scr[1], scr[2], k, to)

    def local(outs, scr, me):
        return pltpu.make_async_copy(scr[0], outs[0].at[me], scr[3].at[0])

    def start(ins, outs, scr):
        x, y, c, chips = _place()
        me = 4 * x + 2 * y + c
        _cast_rows(scr[0], ins[0])
        local(outs, scr, me).start()
        copy(outs, scr, 0, me, (x, y, 1 - c), src=scr[0]).start()
        for j, (cx, cy) in enumerate(chips):
            copy(outs, scr, 1 + j, me, (cx, cy, c), src=scr[0]).start()

    def finish(ins, outs, scr):
        x, y, c, chips = _place()
        me, sib = 4 * x + 2 * y + c, (x, y, 1 - c)
        passed = []
        for j, (cx, cy) in enumerate(chips):
            theirs = 4 * cx + 2 * cy + c
            copy(outs, scr, 1 + j, theirs, (x, y, c)).wait_recv()
            fwd = copy(outs, scr, 4 + j, theirs, sib)
            fwd.start()
            passed.append(fwd)
        copy(outs, scr, 0, 4 * x + 2 * y + 1 - c, (x, y, c)).wait_recv()
        for j, (cx, cy) in enumerate(chips):
            copy(outs, scr, 4 + j, 4 * cx + 2 * cy + 1 - c, (x, y, c)).wait_recv()
        copy(outs, scr, 0, me, sib, src=scr[0]).wait_send()
        for j, (cx, cy) in enumerate(chips):
            copy(outs, scr, 1 + j, me, (cx, cy, c), src=scr[0]).wait_send()
        for fwd in passed:
            fwd.wait_send()
        local(outs, scr, me).wait()

    return Side((shard,), (VMEM,), (jax.ShapeDtypeStruct((NDEV, R, W), dtype),),
                (pltpu.VMEM((R, W), dtype), _sems(7), _sems(7), _sems(1)), start, finish, None, "dsxy")


def ag_blocks_relay(shard, dtype):
    R, W = shard.shape
    half = R // 2

    def copy(outs, scr, k, block, to, src=None, rows=None):
        dst = outs[0].at[block] if rows is None else outs[0].at[block, pl.ds(rows * half, half), :]
        return _remote(dst if src is None else src, dst, scr[1], scr[2], k, to)

    def local(outs, scr, me):
        return pltpu.make_async_copy(scr[0], outs[0].at[me], scr[3].at[0])

    def own(outs, scr):
        x, y, c, _ = _place()
        me = 4 * x + 2 * y + c
        return [copy(outs, scr, k, me, to, src=scr[0])
                for k, to in enumerate([(x, y, 1 - c), (1 - x, y, c), (x, 1 - y, c)])]

    def start(ins, outs, scr):
        x, y, c, _ = _place()
        _cast_rows(scr[0], ins[0])
        local(outs, scr, 4 * x + 2 * y + c).start()
        for cp in own(outs, scr):
            cp.start()

    def passed_on(outs, scr):
        x, y, c, _ = _place()
        sib, xn, yn = (x, y, 1 - c), (1 - x, y, c), (x, 1 - y, c)
        b_xn, b_yn, b_dg = 4 * (1 - x) + 2 * y + c, 4 * x + 2 * (1 - y) + c, 4 * (1 - x) + 2 * (1 - y) + c
        near = [copy(outs, scr, 5, b_xn, yn, rows=0), copy(outs, scr, 3, b_xn, sib),
                copy(outs, scr, 6, b_yn, xn, rows=1), copy(outs, scr, 4, b_yn, sib)]
        far = [copy(outs, scr, 7, b_dg, sib, rows=0), copy(outs, scr, 8, b_dg, sib, rows=1)]
        return (b_xn, b_yn, b_dg), near, far

    def mid(ins, outs, scr):
        x, y, c, _ = _place()
        (b_xn, b_yn, _), near, _ = passed_on(outs, scr)
        copy(outs, scr, 1, b_xn, (x, y, c)).wait_recv()
        near[0].start()
        near[1].start()
        copy(outs, scr, 2, b_yn, (x, y, c)).wait_recv()
        near[2].start()
        near[3].start()

    def finish(ins, outs, scr):
        x, y, c, _ = _place()
        here = (x, y, c)
        (b_xn, b_yn, b_dg), near, far = passed_on(outs, scr)
        copy(outs, scr, 5, b_dg, here, rows=0).wait_recv()
        far[0].start()
        copy(outs, scr, 6, b_dg, here, rows=1).wait_recv()
        far[1].start()
        flip = 1 - 2 * c
        copy(outs, scr, 0, 4 * x + 2 * y + 1 - c, here).wait_recv()
        copy(outs, scr, 3, b_xn + flip, here).wait_recv()
        copy(outs, scr, 4, b_yn + flip, here).wait_recv()
        copy(outs, scr, 7, b_dg + flip, here, rows=0).wait_recv()
        copy(outs, scr, 8, b_dg + flip, here, rows=1).wait_recv()
        for cp in own(outs, scr) + near + far:
            cp.wait_send()
        local(outs, scr, 4 * x + 2 * y + c).wait()

    return Side((shard,), (VMEM,), (jax.ShapeDtypeStruct((NDEV, R, W), dtype),),
                (pltpu.VMEM((R, W), dtype), _sems(9), _sems(9), _sems(1)), start, finish, mid, "sxy")


def ag_cols(shard):
    K, C = shard.shape
    half, w2 = K // 2, 2 * C

    def win(out, rows_c, chip):
        return out.at[pl.ds(pl.multiple_of(rows_c * half, 16), half), pl.ds(pl.multiple_of(chip * w2, LANES), w2)]

    def ici(outs, scr, j, to, c, k):
        slab, send, recv = scr[2], scr[5], scr[6]
        return _remote(slab.at[pl.ds(pl.multiple_of(c * half, 16), half), :], win(outs[0], c, k), send, recv, j, to)

    def local(outs, scr, k):
        return pltpu.make_async_copy(scr[2], outs[0].at[:, pl.ds(pl.multiple_of(k * w2, LANES), w2)], scr[7].at[0])

    def start(ins, outs, scr):
        stage, inbox, slab, xs, xr = scr[:5]
        x, y, c, chips = _place()
        k = 2 * x + y
        _cast_rows(stage, ins[0])
        swap = _remote(stage, inbox, xs, xr, 0, (x, y, 1 - c))
        swap.start()
        for cc in range(2):
            @pl.when(c == cc)
            def _(cc=cc):
                _cast_rows(slab, stage, slice(cc * C, (cc + 1) * C))
        swap.wait()
        for cc in range(2):
            @pl.when(c == cc)
            def _(cc=cc):
                _cast_rows(slab, inbox, slice((1 - cc) * C, (2 - cc) * C))
        local(outs, scr, k).start()
        for j, (cx, cy) in enumerate(chips):
            ici(outs, scr, j, (cx, cy, c), c, k).start()

    def finish(ins, outs, scr):
        send, recv = scr[5], scr[6]
        x, y, c, chips = _place()
        k, sib = 2 * x + y, (x, y, 1 - c)
        passed = []
        for j, (cx, cy) in enumerate(chips):
            w = win(outs[0], c, 2 * cx + cy)
            _remote(w, w, send, recv, j, sib).wait_recv()
            fwd = _remote(w, w, send, recv, 3 + j, sib)
            fwd.start()
            passed.append(fwd)
        for j, (cx, cy) in enumerate(chips):
            w = win(outs[0], 1 - c, 2 * cx + cy)
            _remote(w, w, send, recv, 3 + j, sib).wait_recv()
        for j, (cx, cy) in enumerate(chips):
            ici(outs, scr, j, (cx, cy, c), c, k).wait_send()
        for fwd in passed:
            fwd.wait_send()
        local(outs, scr, k).wait()

    return Side((shard,), (VMEM,), (jax.ShapeDtypeStruct((K, NDEV * C), BF16),),
                (pltpu.VMEM((K, C), BF16), pltpu.VMEM((K, C), BF16), pltpu.VMEM((K, w2), BF16),
                 _sems(1), _sems(1), _sems(6), _sems(6), _sems(1)), start, finish, None, "dsxy")


def copies_side(args, out_shape, n_copies, plan, peers):
    def copies(ins, outs, scr):
        return [_remote(s_, d_, scr[0], scr[1], i, to) for i, (s_, d_, to) in enumerate(plan(ins, outs))]

    def start(ins, outs, scr):
        for cp in copies(ins, outs, scr):
            cp.start()

    def finish(ins, outs, scr):
        for cp in copies(ins, outs, scr):
            cp.wait()

    return Side(tuple(args), (ANY,) * len(args), tuple(out_shape), (_sems(n_copies), _sems(n_copies)),
                start, finish, None, peers)


def rs_to_sibling(grads):
    out_shape = [jax.ShapeDtypeStruct((4,) + g.shape[1:] if kind == "rows" else (g.shape[0] // 2, g.shape[1]), F32)
                 for kind, g in grads]

    def plan(ins, outs):
        x, y, c, _ = _place()
        sib, res = (x, y, 1 - c), []
        for (kind, _), g, r in zip(grads, ins, outs):
            if kind == "rows":
                res += [(g.at[2 * k + 1 - c], r.at[k], sib) for k in range(4)]
            else:
                half = g.shape[0] // 2
                res.append((g.at[pl.ds(pl.multiple_of((1 - c) * half, 8), half), :], r, sib))
        return res

    return copies_side([g for _, g in grads], out_shape, sum(4 if kind == "rows" else 1 for kind, _ in grads), plan, "s")


def rs_to_chips(parts):
    out_shape = [jax.ShapeDtypeStruct((3,) + p.shape[1:] if kind == "rows" else (3, p.shape[0], p.shape[1] // 4), BF16)
                 for kind, p in parts]

    def plan(ins, outs):
        x, y, c, chips = _place()
        res = []
        for (kind, _), p, r in zip(parts, ins, outs):
            for j, (cx, cy) in enumerate(chips):
                if kind == "rows":
                    src = p.at[2 * cx + cy]
                else:
                    w2 = p.shape[1] // 4
                    src = p.at[:, pl.ds(pl.multiple_of((2 * cx + cy) * w2, LANES), w2)]
                res.append((src, r.at[j], (cx, cy, c)))
        return res

    return copies_side([p for _, p in parts], out_shape, 3 * len(parts), plan, "dxy")


def rs_to_chips_combined(part):
    _, R, W = part.shape
    half = R // 2
    top, bot = pl.ds(0, half), pl.ds(half, half)

    def copies(ins, outs, scr):
        p, r = ins[0], outs[0]
        loc_a, loc_b, in_x, in_y, comb_a, comb_b, send, recv, loc = scr
        x, y, c, _ = _place()
        xn, yn = (1 - x, y, c), (x, 1 - y, c)
        k_xn, k_yn, k_dg = 2 * (1 - x) + y, 2 * x + 1 - y, 2 * (1 - x) + 1 - y
        direct = [_remote(p.at[k_xn, top, :], r.at[0, top, :], send, recv, 0, xn),
                  _remote(p.at[k_yn, bot, :], r.at[1, bot, :], send, recv, 1, yn),
                  _remote(p.at[k_dg, top, :], in_x, send, recv, 2, xn),
                  _remote(p.at[k_dg, bot, :], in_y, send, recv, 3, yn)]
        combined = [_remote(comb_a, r.at[1, top, :], send, recv, 4, yn),
                    _remote(comb_b, r.at[0, bot, :], send, recv, 5, xn)]
        local = [pltpu.make_async_copy(p.at[k_yn, top, :], loc_a, loc.at[0]),
                 pltpu.make_async_copy(p.at[k_xn, bot, :], loc_b, loc.at[1])]
        return direct, combined, local

    def start(ins, outs, scr):
        direct, _, local = copies(ins, outs, scr)
        for cp in local + direct:
            cp.start()

    def mid(ins, outs, scr):
        loc_a, loc_b, in_x, in_y, comb_a, comb_b = scr[:6]
        direct, combined, local = copies(ins, outs, scr)
        for mine, arrival, inbox, out, nxt in ((local[0], direct[2], in_x, comb_a, combined[0]),
                                               (local[1], direct[3], in_y, comb_b, combined[1])):
            mine.wait()
            arrival.wait_recv()
            src = loc_a if out is comb_a else loc_b
            out[...] = (src[...].astype(F32) + inbox[...].astype(F32)).astype(BF16)
            nxt.start()

    def finish(ins, outs, scr):
        direct, combined, _ = copies(ins, outs, scr)
        direct[0].wait_recv()
        direct[1].wait_recv()
        combined[0].wait_recv()
        combined[1].wait_recv()
        for cp in direct + combined:
            cp.wait_send()

    buf = pltpu.VMEM((half, W), BF16)
    return Side((part,), (ANY,), (jax.ShapeDtypeStruct((2, R, W), BF16),),
                (buf, buf, buf, buf, buf, buf, _sems(6), _sems(6), _sems(2)), start, finish, mid, "xy")


def rs_swap_halves(theirs):
    def plan(ins, outs):
        x, y, c, _ = _place()
        return [(t, r, (x, y, 1 - c)) for t, r in zip(ins, outs)]

    return copies_side(theirs, [jax.ShapeDtypeStruct(t.shape, F32) for t in theirs], len(theirs), plan, "s")


ADAM_TILE_BYTES = 3 * 512 * 1024


def _row_tiles(rows, width):
    return 2 if rows % 32 == 0 and rows * width * 4 > ADAM_TILE_BYTES else 1


def chip_sum(name, grad, recv, c_idx, chip_idx):
    _, R, C = grad.shape
    nt = 1
    tr = R // nt

    def body(s_ref, g_ref, r_ref, p_ref, own_ref):
        k = pl.program_id(1)
        tot = g_ref[0] + r_ref[0]
        p_ref[0] = tot.astype(BF16)

        @pl.when(k == s_ref[1])
        def _():
            own_ref[...] = tot

    grid_spec = pltpu.PrefetchScalarGridSpec(
        num_scalar_prefetch=1, grid=(nt, 4),
        in_specs=[pl.BlockSpec((1, tr, C), lambda i, k, s: (2 * k + s[0], i, 0)),
                  pl.BlockSpec((1, tr, C), lambda i, k, s: (k, i, 0))],
        out_specs=[pl.BlockSpec((1, tr, C), lambda i, k, s: (k, i, 0)),
                   pl.BlockSpec((tr, C), lambda i, k, s: (i, 0))])
    return pl.pallas_call(
        body, name=name, grid_spec=grid_spec,
        out_shape=[jax.ShapeDtypeStruct((4, R, C), BF16), jax.ShapeDtypeStruct((R, C), F32)],
        compiler_params=_cp(dimension_semantics=("arbitrary", "arbitrary")),
    )(jnp.stack([c_idx, chip_idx]), grad, recv)


def _half_tiles(half):
    return 2 if half >= 512 else 1


def chip_sum_cols(name, grad, recv, c_idx, chip_idx):
    K, W = grad.shape
    half, w2 = K // 2, W // 4
    nt = _half_tiles(half)
    tr = half // nt

    def body(s_ref, g_ref, r_ref, p_ref, own_ref):
        tot = g_ref[...] + r_ref[...]
        p_ref[...] = tot.astype(BF16)

        @pl.when(pl.program_id(1) == s_ref[1])
        def _():
            own_ref[...] = tot

    grid_spec = pltpu.PrefetchScalarGridSpec(
        num_scalar_prefetch=1, grid=(nt, 4),
        in_specs=[pl.BlockSpec((tr, w2), lambda i, k, s: (s[0] * nt + i, k)),
                  pl.BlockSpec((tr, w2), lambda i, k, s: (i, k))],
        out_specs=[pl.BlockSpec((tr, w2), lambda i, k, s: (i, k)),
                   pl.BlockSpec((tr, w2), lambda i, k, s: (i, 0))])
    return pl.pallas_call(
        body, name=name, grid_spec=grid_spec,
        out_shape=[jax.ShapeDtypeStruct((half, W), BF16), jax.ShapeDtypeStruct((half, w2), F32)],
        compiler_params=_cp(dimension_semantics=("arbitrary", "arbitrary")),
    )(jnp.stack([c_idx, chip_idx]), grad, recv)


def col_final(name, own, recv, c_idx):
    half, w2 = own.shape
    C = w2 // 2
    nt = _half_tiles(half)
    tr = half // nt

    def body(s_ref, o_ref, r_ref, mine_ref, theirs_ref, t_ref):
        t_ref[...] = o_ref[...] + r_ref[0].astype(F32) + r_ref[1].astype(F32) + r_ref[2].astype(F32)
        for cc in range(2):
            @pl.when(s_ref[0] == cc)
            def _(cc=cc):
                mine_ref[...] = t_ref[:, cc * C:(cc + 1) * C]
                theirs_ref[...] = t_ref[:, (1 - cc) * C:(2 - cc) * C]

    grid_spec = pltpu.PrefetchScalarGridSpec(
        num_scalar_prefetch=1, grid=(nt,),
        in_specs=[pl.BlockSpec((tr, w2), lambda i, s: (i, 0)), pl.BlockSpec((3, tr, w2), lambda i, s: (0, i, 0))],
        out_specs=[pl.BlockSpec((tr, C), lambda i, s: (i, 0))] * 2,
        scratch_shapes=[pltpu.VMEM((tr, w2), F32)])
    return pl.pallas_call(
        body, name=name, grid_spec=grid_spec, out_shape=[jax.ShapeDtypeStruct((half, C), F32)] * 2,
        compiler_params=_cp(dimension_semantics=("arbitrary",)),
    )(jnp.stack([c_idx]), own, recv)


def _adamw(w, g, m, v):
    m2 = ADAM_B1 * m + (1.0 - ADAM_B1) * g
    v2 = ADAM_B2 * v + (1.0 - ADAM_B2) * (g * g)
    m_hat = m2 / (1.0 - ADAM_B1 ** ADAM_STEP)
    v_hat = v2 / (1.0 - ADAM_B2 ** ADAM_STEP)
    delta = -ADAM_LR * (m_hat / (jnp.sqrt(v_hat) + ADAM_EPS) + ADAM_WD * w)
    return delta, m2, v2


def shard_adam(name, owns, recvs, w, m, v):
    n = len(owns)
    R = owns[0].shape[0]
    ct = min(o.shape[1] for o in owns)
    first = [sum(o.shape[1] for o in owns[:j]) // ct for j in range(n)]
    count = [o.shape[1] // ct for o in owns]
    nt = _row_tiles(R, ct)
    tr = R // nt

    def body(*refs):
        o_refs, r_refs = refs[:n], refs[n:2 * n]
        w_ref, m_ref, v_ref, g_ref, d_ref, nm_ref, nv_ref = refs[2 * n:]
        g = None
        for j in range(n):
            gj = o_refs[j][...]
            for q in range(recvs[j].shape[0]):
                gj = gj + r_refs[j][q].astype(F32)
            g = gj if g is None else jnp.where(pl.program_id(0) >= first[j], gj, g)
        delta, m2, v2 = _adamw(w_ref[...], g, m_ref[...], v_ref[...])
        g_ref[...] = g
        d_ref[...] = delta
        nm_ref[...] = m2
        nv_ref[...] = v2

    def part(j):
        return pl.BlockSpec((tr, ct), lambda k, i: (i, jnp.clip(k - first[j], 0, count[j] - 1)))

    def part3(j):
        return pl.BlockSpec((recvs[j].shape[0], tr, ct), lambda k, i: (0, i, jnp.clip(k - first[j], 0, count[j] - 1)))

    tile = pl.BlockSpec((tr, ct), lambda k, i: (i, k))
    return pl.pallas_call(
        body, name=name, grid=(sum(count), nt),
        in_specs=[part(j) for j in range(n)] + [part3(j) for j in range(n)] + [tile, tile, tile],
        out_specs=[tile] * 4, out_shape=[jax.ShapeDtypeStruct((R, sum(count) * ct), F32)] * 4,
        compiler_params=_cp(dimension_semantics=("arbitrary", "arbitrary")),
    )(*owns, *recvs, w, m, v)


def adam_cols(name, mine, recv, w, m, v, c_idx):
    half, C = mine.shape
    nt = _half_tiles(half)
    tr = half // nt

    def body(s_ref, a_ref, b_ref, w_ref, m_ref, v_ref, g_ref, d_ref, nm_ref, nv_ref):
        g = jnp.where(pl.program_id(0) == s_ref[0], a_ref[...], b_ref[...])
        delta, m2, v2 = _adamw(w_ref[...], g, m_ref[...], v_ref[...])
        g_ref[...] = g
        d_ref[...] = delta
        nm_ref[...] = m2
        nv_ref[...] = v2

    part = pl.BlockSpec((tr, C), lambda hh, i, s: (i, 0))
    tile = pl.BlockSpec((tr, C), lambda hh, i, s: (hh * nt + i, 0))
    grid_spec = pltpu.PrefetchScalarGridSpec(
        num_scalar_prefetch=1, grid=(2, nt), in_specs=[part, part, tile, tile, tile], out_specs=[tile] * 4)
    return pl.pallas_call(
        body, name=name, grid_spec=grid_spec, out_shape=[jax.ShapeDtypeStruct((2 * half, C), F32)] * 4,
        compiler_params=_cp(dimension_semantics=("arbitrary", "arbitrary")),
    )(jnp.stack([c_idx]), mine, recv, w, m, v)


ROW_N1, ROW_N2, ROW_BG, ROW_QN, ROW_KN, ROW_CB, ROW_LW, ROW_LB, ROW_CW = 0, 1, 2, 4, 5, 6, 7, 8, 9
PACK_ROWS = 40
SMALL = ("norm1_w", "norm2_w", "b_gate", "q_norm_w", "k_norm_w", "conv_b", "conv_ln_w", "conv_ln_b", "conv_w")


def small_sync(g, sq, sides=()):
    ns = len(SMALL)

    def body(*refs):
        gi = dict(zip(SMALL, refs[:ns]))
        sq_ref, tot, pack, recv, send_sems, recv_sems = refs[ns:]
        x, y, c, _ = _place()
        me = 4 * x + 2 * y + c

        pack[...] = jnp.zeros_like(pack)
        pack[ROW_KN:ROW_KN + 1, LANES:2 * LANES] = jnp.full((1, LANES), (0.5 / D) * jnp.sum(sq_ref[...]), F32)
        pack[ROW_N1:ROW_N1 + 1, :] = gi["norm1_w"][...]
        pack[ROW_N2:ROW_N2 + 1, :] = gi["norm2_w"][...]
        pack[ROW_BG:ROW_BG + 2, :] = gi["b_gate"][...]
        pack[ROW_QN:ROW_QN + 1, 0:HD] = gi["q_norm_w"][...]
        pack[ROW_KN:ROW_KN + 1, 0:HD] = gi["k_norm_w"][...]
        pack[ROW_CB:ROW_CB + 1, 0:CC] = gi["conv_b"][...]
        pack[ROW_LW:ROW_LW + 1, 0:CC] = gi["conv_ln_w"][...]
        pack[ROW_LB:ROW_LB + 1, 0:CC] = gi["conv_ln_b"][...]
        pack[ROW_CW:ROW_CW + KW, 0:CC] = gi["conv_w"][...]

        copies = []
        for k in range(1, NDEV):
            peer = (x ^ (k >> 2), y ^ ((k >> 1) & 1), c ^ (k & 1))
            cp = pltpu.make_async_remote_copy(
                src_ref=pack, dst_ref=recv.at[me], send_sem=send_sems.at[k - 1], recv_sem=recv_sems.at[k - 1],
                device_id=peer, device_id_type=MESH)
            cp.start()
            copies.append(cp)
        recv[me] = pack[...]
        for cp in copies:
            cp.wait()
        acc = recv[0]
        for p in range(1, NDEV):
            acc = acc + recv[p]
        tot[...] = acc

    args = [g[k] for k in SMALL] + [sq]
    res = _call(
        body, sides, name="small_sync", grid=(1,), in_specs=[VMEM] * len(args), out_specs=[VMEM],
        out_shape=[jax.ShapeDtypeStruct((PACK_ROWS, D), F32)],
        scratch_shapes=[pltpu.VMEM((PACK_ROWS, D), F32), pltpu.VMEM((NDEV, PACK_ROWS, D), F32),
                        _sems(NDEV - 1), _sems(NDEV - 1)],
        args=args, own_comm=True)
    return (res[0][0], res[1]) if sides else res[0]


def small_adam(tot, w, m, v, me):
    ns = len(SMALL)

    def body(me_ref, tot, *refs):
        wi = dict(zip(SMALL, refs[:ns]))
        mi = dict(zip(SMALL, refs[ns:2 * ns]))
        vi = dict(zip(SMALL, refs[2 * ns:3 * ns]))
        outs = refs[3 * ns:7 * ns]
        loss_ref = refs[7 * ns]
        me = me_ref[0]

        def shard_grad(name):
            if name == "b_gate":
                return tot[ROW_BG:ROW_BG + 2, pl.ds(pl.multiple_of(me * LANES, LANES), LANES)]
            if name == "conv_w":
                win = tot[ROW_CW:ROW_CW + KW, pl.ds(pl.multiple_of((me // 2) * LANES, LANES), LANES)]
                return jnp.where(me % 2 == 1, win[:, HD:LANES], win[:, 0:HD])
            row = {"norm1_w": ROW_N1, "norm2_w": ROW_N2, "q_norm_w": ROW_QN, "k_norm_w": ROW_KN,
                   "conv_b": ROW_CB, "conv_ln_w": ROW_LW, "conv_ln_b": ROW_LB}[name]
            return tot[row:row + 1, 0:wi[name].shape[1]]

        for i, name in enumerate(SMALL):
            gr = shard_grad(name)
            delta, m2, v2 = _adamw(wi[name][...], gr, mi[name][...], vi[name][...])
            outs[4 * i][...] = gr
            outs[4 * i + 1][...] = delta
            outs[4 * i + 2][...] = m2
            outs[4 * i + 3][...] = v2
        loss_ref[...] = tot[ROW_KN:ROW_KN + 1, LANES:2 * LANES]

    out_shape = []
    for name in SMALL:
        out_shape += [jax.ShapeDtypeStruct(w[name].shape, F32)] * 4
    out_shape.append(jax.ShapeDtypeStruct((1, LANES), F32))
    args = [tot] + [w[k] for k in SMALL] + [m[k] for k in SMALL] + [v[k] for k in SMALL]
    grid_spec = pltpu.PrefetchScalarGridSpec(
        num_scalar_prefetch=1, grid=(1,), in_specs=[VMEM] * len(args), out_specs=[VMEM] * len(out_shape))
    res = pl.pallas_call(body, name="small_adam", grid_spec=grid_spec, out_shape=out_shape)(me, *args)
    out = {name: tuple(res[4 * i:4 * i + 4]) for i, name in enumerate(SMALL)}
    return out, res[4 * ns][0, 0]


MATS = ("w_in", "w_o_attn", "w_pw_conv", "w_out", "w_ffn_in", "w_ffn_out")
TRANSPOSED = ("w_in", "w_ffn_in")
WEIGHTS = ("norm1_w", "w_in", "b_gate", "q_norm_w", "k_norm_w", "w_o_attn", "conv_w", "conv_b", "conv_ln_w",
           "conv_ln_b", "w_pw_conv", "w_out", "norm2_w", "w_ffn_in", "w_ffn_out")


def _blocks_to_cols(blocks):
    n, R, C = blocks.shape
    return blocks.transpose(1, 0, 2).reshape(R, n * C)


def kernel(x, positions, norm1_w, w_in, b_gate, q_norm_w, k_norm_w, w_o_attn, conv_w, conv_b, conv_ln_w, conv_ln_b, w_pw_conv, w_out, norm2_w, w_ffn_in, w_ffn_out, loss_target, m_norm1_w, m_w_in, m_b_gate, m_q_norm_w, m_k_norm_w, m_w_o_attn, m_conv_w, m_conv_b, m_conv_ln_w, m_conv_ln_b, m_w_pw_conv, m_w_out, m_norm2_w, m_w_ffn_in, m_w_ffn_out, v_norm1_w, v_w_in, v_b_gate, v_q_norm_w, v_k_norm_w, v_w_o_attn, v_conv_w, v_conv_b, v_conv_ln_w, v_conv_ln_b, v_w_pw_conv, v_w_out, v_norm2_w, v_w_ffn_in, v_w_ffn_out):
    w = dict(norm1_w=norm1_w, w_in=w_in, b_gate=b_gate, q_norm_w=q_norm_w, k_norm_w=k_norm_w, w_o_attn=w_o_attn,
             conv_w=conv_w, conv_b=conv_b, conv_ln_w=conv_ln_w, conv_ln_b=conv_ln_b, w_pw_conv=w_pw_conv,
             w_out=w_out, norm2_w=norm2_w, w_ffn_in=w_ffn_in, w_ffn_out=w_ffn_out)
    m = dict(norm1_w=m_norm1_w, w_in=m_w_in, b_gate=m_b_gate, q_norm_w=m_q_norm_w, k_norm_w=m_k_norm_w,
             w_o_attn=m_w_o_attn, conv_w=m_conv_w, conv_b=m_conv_b, conv_ln_w=m_conv_ln_w,
             conv_ln_b=m_conv_ln_b, w_pw_conv=m_w_pw_conv, w_out=m_w_out, norm2_w=m_norm2_w,
             w_ffn_in=m_w_ffn_in, w_ffn_out=m_w_ffn_out)
    v = dict(norm1_w=v_norm1_w, w_in=v_w_in, b_gate=v_b_gate, q_norm_w=v_q_norm_w, k_norm_w=v_k_norm_w,
             w_o_attn=v_w_o_attn, conv_w=v_conv_w, conv_b=v_conv_b, conv_ln_w=v_conv_ln_w,
             conv_ln_b=v_conv_ln_b, w_pw_conv=v_w_pw_conv, w_out=v_w_out, norm2_w=v_norm2_w,
             w_ffn_in=v_w_ffn_in, w_ffn_out=v_w_ffn_out)
    def two_d(t):
        t = {k: (a[0] if a.ndim == 3 else a) for k, a in t.items()}
        return {k: (a.T if k in TRANSPOSED else a) for k, a in t.items()}

    w, m, v = two_d(w), two_d(m), two_d(v)

    x2, target = x[0], loss_target[0]
    c_idx = lax.axis_index("c").astype(jnp.int32)
    chip_idx = (2 * lax.axis_index("x") + lax.axis_index("y")).astype(jnp.int32)
    qw2 = jnp.tile(w["q_norm_w"], (1, 2))
    kw2 = jnp.tile(w["k_norm_w"], (1, 2))

    ax, ay = lax.axis_index("x"), lax.axis_index("y")
    chip_order = jnp.stack([2 * ax + ay, 2 * (1 - ax) + ay, 2 * ax + 1 - ay, 2 * (1 - ax) + 1 - ay]).astype(jnp.int32)
    h_t, proj, w_in_blocks, tabs = in_proj_gather(x2, w["norm1_w"], w["w_in"], chip_order, positions.reshape(S, 1))
    w_in_t = w_in_blocks.reshape(INW, D)
    (attn, lse), ((w_ffn_in_blocks,), (w_out_blocks,), (bg_blocks,), (cw_blocks,)) = attn_fwd(
        proj, tabs, qw2, kw2, sides=(ag_blocks_relay(w["w_ffn_in"], BF16), ag_blocks_relay(w["w_out"], BF16),
                                     ag_blocks(w["b_gate"], F32), ag_blocks(w["conv_w"], F32)))
    w_ffn_in_t = w_ffn_in_blocks.reshape(2 * FF, D)
    w_out_f = w_out_blocks.reshape(D, D)
    b_gate_f, conv_w_f = _blocks_to_cols(bg_blocks), _blocks_to_cols(cw_blocks)
    (cpre, u3), ((w_o_f,), (w_pw_f,)) = conv_fwd(
        proj, conv_w_f, w["conv_b"], w["conv_ln_w"], w["conv_ln_b"],
        sides=(ag_cols(w["w_o_attn"]), ag_cols(w["w_pw_conv"])))
    x1, z, ya, yb = mix_out(x2, proj, b_gate_f, attn, u3, w_o_f, w_pw_f, w_out_f)
    (h2, gu, f), ((w_ffn_out_blocks,),) = ffn_in(x1, w["norm2_w"], w_ffn_in_t, sides=(ag_blocks_relay(w["w_ffn_out"], BF16),))
    w_ffn_out_f = w_ffn_out_blocks.reshape(FF, D)
    dy, dyb, sq = ffn_out_loss(x1, f, w_ffn_out_f, target)

    g = {}
    g_ffn_out = mm_tn("gw_ffn_out", f, dyb, FF // 2, D).reshape(NDEV, FF // NDEV, D)
    (d_gu, d_x1, d_x1b, g["norm2_w"]), ((ra_ffn_out,),) = ffn_bwd(
        dy, dyb, gu, x1, w["norm2_w"], w_ffn_in_t, w_ffn_out_f, sides=(rs_to_sibling([("rows", g_ffn_out)]),))
    pb_ffn_out, own_ffn_out = chip_sum("chip_sum_w_ffn_out", g_ffn_out, ra_ffn_out, c_idx, chip_idx)
    g_ffn_in = mm_tn("gw_ffn_in", d_gu, h2, FF // 2, D).reshape(NDEV, 2 * FF // NDEV, D)
    g_out = mm_tn("gw_out", z, d_x1b, D // 2, D).reshape(NDEV, D // NDEV, D)
    (d_ya, d_yb, d_gl, d_attn, d_u3, g["b_gate"]), ((ra_ffn_in,),) = out_bwd(
        d_x1b, proj, b_gate_f, ya, yb, w_o_f, w_pw_f, w_out_f, sides=(rs_to_sibling([("rows", g_ffn_in)]),))
    pb_ffn_in, own_ffn_in = chip_sum("chip_sum_w_ffn_in", g_ffn_in, ra_ffn_in, c_idx, chip_idx)
    g_w_o = mm_tn("gw_o_attn", attn, d_ya, CC, D)
    g_w_pw = mm_tn("gw_pw_conv", u3, d_yb, CC, D)
    (d_conv, g["conv_w"], g["conv_b"], g["conv_ln_w"], g["conv_ln_b"]), ((ra_out, ra_w_o, ra_w_pw),) = conv_bwd(
        proj, cpre, d_u3, conv_w_f, conv_w_f[::-1], w["conv_ln_w"], w["conv_ln_b"],
        sides=(rs_to_sibling([("rows", g_out), ("cols", g_w_o), ("cols", g_w_pw)]),))
    pb_out, own_out = chip_sum("chip_sum_w_out", g_out, ra_out, c_idx, chip_idx)
    pb_w_o, own_w_o = chip_sum_cols("chip_sum_w_o_attn", g_w_o, ra_w_o, c_idx, chip_idx)
    pb_w_pw, own_w_pw = chip_sum_cols("chip_sum_w_pw_conv", g_w_pw, ra_w_pw, c_idx, chip_idx)
    (d_q, d_k, d_v, gqw, gkw), ((rb_ffn_out, rb_ffn_in, rb_out, rb_w_o, rb_w_pw),) = attn_bwd(
        proj, tabs, qw2, kw2, d_attn, attn, lse,
        sides=(rs_to_chips([("rows", pb_ffn_out), ("rows", pb_ffn_in), ("rows", pb_out),
                            ("cols", pb_w_o), ("cols", pb_w_pw)]),))
    g["q_norm_w"] = gqw[0:1, 0:HD] + gqw[0:1, HD:LANES]
    g["k_norm_w"] = gkw[0:1, 0:HD] + gkw[0:1, HD:LANES]
    mine_w_o, theirs_w_o = col_final("col_final_w_o_attn", own_w_o, rb_w_o, c_idx)
    mine_w_pw, theirs_w_pw = col_final("col_final_w_pw_conv", own_w_pw, rb_w_pw, c_idx)
    d_segs = (d_q, d_k, d_v, d_conv, d_gl)
    parts, to_sibling, to_chips, owns, from_chips = [], None, None, [], []
    for k, hw in enumerate(GW_IN_SPLIT):
        sides = [rs_swap_halves([theirs_w_o, theirs_w_pw])] if k == 0 else []
        sides += [s for s in (to_chips, to_sibling) if s is not None]
        part, outs = gw_in_t("gw_in_%d" % k, h_t, d_segs, sum(GW_IN_SPLIT[:k]), hw, sides=tuple(sides))
        if k == 0:
            (rc_w_o, rc_w_pw), outs = outs[0], outs[1:]
        outs = list(outs)
        if to_chips is not None:
            from_chips.append(outs.pop(0)[0])
        if to_sibling is not None:
            pb, own = chip_sum("chip_sum_w_in_%d" % (k - 1), parts[-1], outs.pop(0)[0], c_idx, chip_idx)
            owns.append(own)
            to_chips = rs_to_chips_combined(pb)
        else:
            to_chips = None
        parts.append(part.reshape(NDEV, INW // NDEV, hw))
        to_sibling = rs_to_sibling([("rows", parts[-1])])
    (grad_x, g["norm1_w"]), ((rb_prev,), (ra_last,)) = in_bwd(
        d_q, d_k, d_v, d_conv, d_gl, w_in_t, x2, d_x1, w["norm1_w"], sides=(to_chips, to_sibling))
    from_chips.append(rb_prev)
    pb, own = chip_sum("chip_sum_w_in_%d" % (len(GW_IN_SPLIT) - 1), parts[-1], ra_last, c_idx, chip_idx)
    owns.append(own)
    small_sums, ((rb_last,),) = small_sync(g, sq, sides=(rs_to_chips_combined(pb),))
    small, loss = small_adam(small_sums, w, m, v, (4 * ax + 2 * ay + c_idx).astype(jnp.int32).reshape(1))
    from_chips.append(rb_last)

    res = {
        "w_in": shard_adam("adam_w_in", owns, from_chips, w["w_in"], m["w_in"], v["w_in"]),
        "w_ffn_in": shard_adam("adam_w_ffn_in", [own_ffn_in], [rb_ffn_in], w["w_ffn_in"], m["w_ffn_in"], v["w_ffn_in"]),
        "w_o_attn": adam_cols("adam_w_o_attn", mine_w_o, rc_w_o, w["w_o_attn"], m["w_o_attn"], v["w_o_attn"], c_idx),
        "w_pw_conv": adam_cols("adam_w_pw_conv", mine_w_pw, rc_w_pw, w["w_pw_conv"], m["w_pw_conv"], v["w_pw_conv"], c_idx),
        "w_out": shard_adam("adam_w_out", [own_out], [rb_out], w["w_out"], m["w_out"], v["w_out"]),
        "w_ffn_out": shard_adam("adam_w_ffn_out", [own_ffn_out], [rb_ffn_out],
                                w["w_ffn_out"], m["w_ffn_out"], v["w_ffn_out"]),
    }
    res = {k: tuple(a.T if k in TRANSPOSED else a for a in r) for k, r in res.items()}
    res.update(small)

    def shaped(name, a):
        return a.reshape((1,) + a.shape) if name in MATS or name in ("b_gate", "conv_w") else a

    outs = [loss, grad_x.reshape(1, S, D)]
    for i in range(4):
        outs += [shaped(k, res[k][i]) for k in WEIGHTS]
    return tuple(outs)
```

```python
import functools
from typing import Callable, NamedTuple, Optional

import numpy as np
import jax
import jax.numpy as jnp
from jax import lax
from jax.experimental import pallas as pl
from jax.experimental.pallas import tpu as pltpu

F32 = jnp.float32
BF16 = jnp.bfloat16

S = 2048
D = 1024
HD = 64
QKV = 1536
CC = 512
KW = 31
FF = 2816
INW = 7680
OFF_Q, OFF_K, OFF_V, OFF_CA, OFF_CB, OFF_GA, OFF_GB = 0, 1536, 3072, 4608, 5120, 5632, 6656
DILATIONS = (1, 4, 16)
HALF_SPAN = 64
EPS = 1e-6
NEG_INF = -1e30
ROPE_THETA = 500000.0
ROT_DIM = 16

ADAM_LR = 0.001
ADAM_B1 = 0.9
ADAM_B2 = 0.999
ADAM_EPS = 1e-08
ADAM_WD = 0.01
ADAM_STEP = 10

NDEV = 8
LANES = 128
TM = 256
TQ = 128
VMEM_LIMIT = 56 * 1024 * 1024
MESH = pl.DeviceIdType.MESH


def _cp(**kw):
    return pltpu.CompilerParams(vmem_limit_bytes=VMEM_LIMIT, **kw)


def _row(width, col=0, tm=TM):
    return pl.BlockSpec((tm, width), lambda i: (i, col))


PLANE = 512


def _planes(width, tm=TM):
    return pl.BlockSpec((width // PLANE, tm, PLANE), lambda i: (0, i, 0))


def _res(shape):
    nd = len(shape)
    return pl.BlockSpec(shape, lambda *_: (0,) * nd, pipeline_mode=pl.Buffered(1))


def _dot(a, b):
    return jnp.dot(a, b, preferred_element_type=F32)


def _dot_nt(a, b):
    return lax.dot_general(a, b, (((1,), (1,)), ((), ())), preferred_element_type=F32)


def _dot_tn(a, b):
    return lax.dot_general(a, b, (((0,), (0,)), ((), ())), preferred_element_type=F32)


def _sigmoid(x):
    return jax.nn.sigmoid(x)


def _dsilu(x, sg):
    return sg * (1.0 + x * (1.0 - sg))


ANY = pl.BlockSpec(memory_space=pl.ANY)
VMEM = pl.BlockSpec(memory_space=pltpu.VMEM)


class Side(NamedTuple):
    args: tuple
    in_specs: tuple
    out_shape: tuple
    scratch: tuple
    start: Callable
    finish: Callable
    mid: Optional[Callable] = None
    peers: str = ""


BARRIER_IDS = {"s": 0, "dxy": 1, "dsxy": 2, "sxy": 3, "xy": 4}


def _peer_barrier(peers):
    x, y, c = lax.axis_index("x"), lax.axis_index("y"), lax.axis_index("c")
    where = {"s": (x, y, 1 - c), "x": (1 - x, y, c), "y": (x, 1 - y, c), "d": (1 - x, 1 - y, c)}
    barrier = pltpu.get_barrier_semaphore()
    for p in peers:
        pl.semaphore_signal(barrier, inc=1, device_id=where[p], device_id_type=MESH)
    pl.semaphore_wait(barrier, len(peers))


def _call(body, sides=(), *, name, grid, in_specs, out_specs, out_shape, scratch_shapes=(), args, own_comm=False):
    ni, no, ns = len(in_specs), len(out_specs), len(scratch_shapes)
    cnt = [(len(s.args), len(s.out_shape), len(s.scratch)) for s in sides]
    peers = "".join(sorted(set("".join(s.peers for s in sides))))
    if own_comm or not sides or any(not s.peers for s in sides):
        peers = ""

    def take(refs, pos, n):
        return refs[pos:pos + n], pos + n

    def full(*refs):
        m_in, pos = take(refs, 0, ni)
        s_in = []
        for a, _, _ in cnt:
            r, pos = take(refs, pos, a)
            s_in.append(r)
        m_out, pos = take(refs, pos, no)
        s_out = []
        for _, o, _ in cnt:
            r, pos = take(refs, pos, o)
            s_out.append(r)
        m_scr, pos = take(refs, pos, ns)
        s_scr = []
        for _, _, c in cnt:
            r, pos = take(refs, pos, c)
            s_scr.append(r)
        if sides:
            first = functools.reduce(jnp.logical_and, [pl.program_id(d) == 0 for d in range(len(grid))])
            last = functools.reduce(jnp.logical_and, [pl.program_id(d) == g - 1 for d, g in enumerate(grid)])

            @pl.when(first)
            def _():
                if peers:
                    _peer_barrier(peers)
                for s, a, o, c in zip(sides, s_in, s_out, s_scr):
                    s.start(a, o, c)

            steps = int(np.prod(grid))
            mid_step = (2 * steps) // 3
            if steps > 1 and any(s.mid is not None for s in sides):
                step = functools.reduce(lambda acc, d: acc * grid[d] + pl.program_id(d), range(len(grid)), 0)

                @pl.when(step == mid_step)
                def _():
                    for s, a, o, c in zip(sides, s_in, s_out, s_scr):
                        if s.mid is not None:
                            s.mid(a, o, c)

        body(*m_in, *m_out, *m_scr)
        if sides:
            @pl.when(last)
            def _():
                for s, a, o, c in zip(sides, s_in, s_out, s_scr):
                    if s.mid is not None and steps == 1:
                        s.mid(a, o, c)
                    s.finish(a, o, c)

    res = pl.pallas_call(
        full, name=name, grid=grid,
        in_specs=list(in_specs) + [sp for s in sides for sp in s.in_specs],
        out_specs=list(out_specs) + [ANY for s in sides for _ in s.out_shape],
        out_shape=list(out_shape) + [o for s in sides for o in s.out_shape],
        scratch_shapes=list(scratch_shapes) + [c for s in sides for c in s.scratch],
        compiler_params=_cp(dimension_semantics=("arbitrary",) * len(grid),
                            **({"collective_id": BARRIER_IDS[peers]} if peers else {})),
    )(*args, *[a for s in sides for a in s.args])
    res = list(res)
    if not sides:
        return res
    outs, pos = take(res, 0, no)
    side_outs = []
    for _, o, _ in cnt:
        r, pos = take(res, pos, o)
        side_outs.append(r)
    return outs, side_outs


def _inv_freq_lanes():
    inv = np.float32(ROPE_THETA) ** (-np.arange(0, ROT_DIM, 2, dtype=np.float32) / np.float32(ROT_DIM))
    lane = np.arange(LANES) % HD
    out = np.where(lane < ROT_DIM, inv[lane % (ROT_DIM // 2)], 0.0).astype(np.float32)
    return jnp.asarray(out.reshape(1, LANES))


def _rope_tables(pos, inv_freq):
    ang = pos.astype(F32) * inv_freq
    lane = lax.broadcasted_iota(jnp.int32, ang.shape, 1) % HD
    cs = jnp.cos(ang)
    sn = jnp.sin(ang)
    return (jnp.where(lane < ROT_DIM, cs, 1.0), jnp.where(lane < ROT_DIM // 2, -sn, 0.0),
            jnp.where(lane < ROT_DIM // 2, 0.0, jnp.where(lane < ROT_DIM, sn, 0.0)))


def _rope(v, c, s1, s2):
    return v * c + pltpu.roll(v, LANES - 8, axis=1) * s1 + pltpu.roll(v, 8, axis=1) * s2


def _rope_t(d, c, s1, s2):
    return d * c - pltpu.roll(d, LANES - 8, axis=1) * s1 - pltpu.roll(d, 8, axis=1) * s2


def _head_mat():
    r = lax.broadcasted_iota(jnp.int32, (LANES, LANES), 0) // HD
    c = lax.broadcasted_iota(jnp.int32, (LANES, LANES), 1) // HD
    return jnp.where(r == c, 1.0 / HD, 0.0).astype(BF16)


def _head_mean(t, e):
    hi = t.astype(BF16)
    rest = (t - hi.astype(F32)).astype(BF16)
    return _dot(hi, e) + _dot(rest, e)


def in_proj_gather(x, norm_w, shard_t, chip_order, pos_col):
    R = INW // NDEV
    half, nt = R // 2, S // TM

    def body(ord_ref, x_ref, nw_ref, sh_ref, pos_ref, f_ref, ht_ref, p_ref, wfull_ref, c_ref, s1_ref, s2_ref,
             wt, hs, send, recv, loc):
        kk, i = pl.program_id(0), pl.program_id(1)
        x, y, c, _ = _place()
        me, flip = 4 * x + 2 * y + c, 1 - 2 * c
        here, sib, xn, yn = (x, y, c), (x, y, 1 - c), (1 - x, y, c), (x, 1 - y, c)
        b_xn, b_yn, b_dg = 4 * (1 - x) + 2 * y + c, 4 * x + 2 * (1 - y) + c, 4 * (1 - x) + 2 * (1 - y) + c

        def cp(k, block, to, rows=None):
            dst = wt.at[block] if rows is None else wt.at[block, pl.ds(rows * half, half), :]
            return _remote(dst, dst, send, recv, k, to)

        def sends():
            return [cp(0, me, sib), cp(1, me, xn), cp(2, me, yn), cp(3, b_xn, sib), cp(4, b_yn, sib),
                    cp(5, b_xn, yn, rows=0), cp(6, b_yn, xn, rows=1), cp(7, b_dg, sib, rows=0), cp(8, b_dg, sib, rows=1)]

        def keep(j, blk0):
            pair = pl.ds(pl.multiple_of(blk0, 2), 2)
            return pltpu.make_async_copy(wt.at[pair], wfull_ref.at[pair], loc.at[j])

        @pl.when((kk == 0) & (i == 0))
        def _():
            _peer_barrier("sxy")
            _cast_rows(wt.at[me], sh_ref)
            for s_ in sends()[0:3]:
                s_.start()

            def tables(j, _):
                chunk = pl.ds(pl.multiple_of(j * TM, TM), TM)
                c_ref[chunk, :], s1_ref[chunk, :], s2_ref[chunk, :] = _rope_tables(pos_ref[chunk, :], f_ref[...])
                return 0

            lax.fori_loop(0, nt, tables, 0)
            cp(0, me + flip, here).wait_recv()
            keep(0, me - c).start()

        @pl.when((kk == 1) & (i == 0))
        def _():
            cp(1, b_xn, here).wait_recv()
            sends()[5].start()
            sends()[3].start()
            cp(2, b_yn, here).wait_recv()
            sends()[6].start()
            sends()[4].start()
            cp(3, b_xn + flip, here).wait_recv()
            keep(1, b_xn - c).start()

        @pl.when((kk == 2) & (i == 0))
        def _():
            cp(4, b_yn + flip, here).wait_recv()
            keep(2, b_yn - c).start()

        @pl.when((kk == 3) & (i == 0))
        def _():
            cp(5, b_dg, here, rows=0).wait_recv()
            sends()[7].start()
            cp(6, b_dg, here, rows=1).wait_recv()
            sends()[8].start()
            cp(7, b_dg + flip, here, rows=0).wait_recv()
            cp(8, b_dg + flip, here, rows=1).wait_recv()
            keep(3, b_dg - c).start()

        rows = pl.ds(pl.multiple_of(i * TM, TM), TM)

        @pl.when(kk == 0)
        def _():
            xv = x_ref[...]
            r = lax.rsqrt(jnp.mean(xv * xv, axis=-1, keepdims=True) + EPS)
            hf = xv * r * nw_ref[...]
            ht_ref[...] = hf.T.astype(BF16)
            hs[rows, :] = hf.astype(BF16)

        h = hs[rows, :]
        chip = ord_ref[kk]
        for cc in range(2):
            p_ref[:, cc * R:(cc + 1) * R] = _dot_nt(h, wt[2 * chip + cc])

        @pl.when((kk == 3) & (i == nt - 1))
        def _():
            for s_ in sends():
                s_.wait_send()
            for j, blk in enumerate((me, b_xn, b_yn, b_dg)):
                keep(j, blk - c).wait()

    def first_pass(kk, i):
        return jnp.where(kk == 0, i, nt - 1)

    grid_spec = pltpu.PrefetchScalarGridSpec(
        num_scalar_prefetch=1, grid=(4, nt),
        in_specs=[pl.BlockSpec((TM, D), lambda kk, i, o: (first_pass(kk, i), 0)),
                  pl.BlockSpec((1, D), lambda kk, i, o: (0, 0)), VMEM, VMEM,
                  pl.BlockSpec((1, LANES), lambda kk, i, o: (0, 0))],
        out_specs=[pl.BlockSpec((D, TM), lambda kk, i, o: (0, first_pass(kk, i))),
                   pl.BlockSpec((TM, 2 * R), lambda kk, i, o: (i, o[kk])), ANY]
        + [pl.BlockSpec((S, LANES), lambda kk, i, o: (0, 0))] * 3,
        scratch_shapes=[pltpu.VMEM((NDEV, R, D), BF16), pltpu.VMEM((S, D), BF16), _sems(9), _sems(9), _sems(4)])
    res = pl.pallas_call(
        body, name="in_proj_gather", grid_spec=grid_spec,
        out_shape=[jax.ShapeDtypeStruct((D, S), BF16), jax.ShapeDtypeStruct((S, INW), F32),
                   jax.ShapeDtypeStruct((NDEV, R, D), BF16)] + [jax.ShapeDtypeStruct((S, LANES), F32)] * 3,
        compiler_params=_cp(dimension_semantics=("arbitrary", "arbitrary"), collective_id=BARRIER_IDS["sxy"]),
    )(chip_order, x, norm_w, shard_t, pos_col, _inv_freq_lanes())
    return res[0], res[1], res[2], tuple(res[3:])


def _qk_specs():
    nb = QKV // LANES
    return [pl.BlockSpec((S, LANES), functools.partial(lambda hp, g, o: (0, o + g * 4 + hp), o=o))
            for o in (OFF_Q // LANES, OFF_K // LANES, OFF_V // LANES)]


def _tab_specs():
    return [pl.BlockSpec((S, LANES), lambda hp, g: (0, 0), pipeline_mode=pl.Buffered(1))] * 3


def _vec_spec():
    return pl.BlockSpec((1, LANES), lambda hp, g: (0, 0))


def _sub_rows(r, d, start, n):
    if d == 1:
        return pl.ds(start, n)
    return pl.ds(r + d * start, n, stride=d)


def _band_window(i, L):
    W = min(TQ + 2 * HALF_SPAN, L)
    q0 = pl.multiple_of(i * TQ, TQ)
    k0 = pl.multiple_of(jnp.clip(q0 - HALF_SPAN, 0, L - W), HALF_SPAN)
    qpos = q0 + (lax.broadcasted_iota(jnp.int32, (2 * TQ, W), 0) & (TQ - 1))
    kpos = k0 + lax.broadcasted_iota(jnp.int32, (2 * TQ, W), 1)
    valid = jnp.abs(qpos - kpos) <= HALF_SPAN
    return W, q0, k0, valid


def _stack_heads(t, lo):
    z = jnp.zeros_like(t)
    return jnp.concatenate([jnp.where(lo, t, z), jnp.where(lo, z, t)], axis=0)


def _unstack_heads(t2, lo):
    return jnp.where(lo, t2[0:TQ], t2[TQ:2 * TQ])


CHAINS = 8


def _interleave(d):
    ru = min(d, CHAINS)
    return ru, min(CHAINS // ru, S // d // TQ)


def _for_blocks(n, fn):
    if n == 1:
        fn(0)
    else:
        def it(j, _):
            fn(j)
            return 0
        lax.fori_loop(0, n, it, 0)


def attn_fwd(proj, tabs, qw2, kw2, sides=()):
    CH = 256

    def body(q_ref, k_ref, v_ref, c_ref, s1_ref, s2_ref, qw_ref, kw_ref, at_ref, ls_ref,
             qs, ks, vs, osub, lsub, onat, lnat, qn, kn):
        g = pl.program_id(1)
        lo = lax.broadcasted_iota(jnp.int32, (1, LANES), 1) < HD
        e = _head_mat()

        def prep(i, _):
            rows = pl.ds(pl.multiple_of(i * CH, CH), CH)
            c, s1, s2 = c_ref[rows, :], s1_ref[rows, :], s2_ref[rows, :]
            for t_ref, w_ref, out, scale in ((q_ref, qw_ref, qn, HD ** -0.5), (k_ref, kw_ref, kn, 1.0)):
                t = t_ref[rows, :]
                r = lax.rsqrt(_head_mean(t * t, e) + EPS)
                out[rows, :] = _rope(t * r * w_ref[...], c, s1, s2) * scale
            return 0

        lax.fori_loop(0, S // CH, prep, 0, unroll=4)

        def group(gi, d):
            L = S // d

            ru, nb = _interleave(d)

            def stage(r, off):
                for c0 in range(0, L, CH):
                    n = min(CH, L)
                    rows = _sub_rows(r, d, c0, n)
                    dst = pl.ds(off + c0, n)
                    qs[dst, :] = qn[rows, :].astype(BF16)
                    ks[dst, :] = kn[rows, :].astype(BF16)
                    vs[dst, :] = v_ref[rows, :].astype(BF16)

            def one(off, i):
                W, q0, k0, valid = _band_window(i, L)
                q2 = _stack_heads(qs[pl.ds(off + q0, TQ), :], lo)
                sc = jnp.where(valid, _dot_nt(q2, ks[pl.ds(off + k0, W), :]), NEG_INF)
                m = jnp.max(sc, axis=-1, keepdims=True)
                p = jnp.exp(sc - m)
                den = jnp.sum(p, axis=-1, keepdims=True)
                o2 = _dot(p.astype(BF16), vs[pl.ds(off + k0, W), :]) / den
                l2 = jnp.broadcast_to(m + jnp.log(den), (2 * TQ, LANES))
                osub[pl.ds(off + q0, TQ), :] = _unstack_heads(o2, lo)
                lsub[pl.ds(off + q0, TQ), :] = _unstack_heads(l2, lo)

            def unstage(r, off):
                for c0 in range(0, L, CH):
                    n = min(CH, L)
                    rows = _sub_rows(r, d, c0, n)
                    onat[gi, rows, :] = osub[pl.ds(off + c0, n), :]
                    lnat[gi, rows, :] = lsub[pl.ds(off + c0, n), :]

            def step(t, _):
                for u in range(ru):
                    stage(t * ru + u, u * L)
                _for_blocks(L // TQ // nb, lambda j: [one(u * L, j * nb + b) for u in range(ru) for b in range(nb)])
                for u in range(ru):
                    unstage(t * ru + u, u * L)
                return 0

            lax.fori_loop(0, d // ru, step, 0)

        for gi, d in enumerate(DILATIONS):
            pl.when(g == gi)(functools.partial(group, gi, d))

        @pl.when(g == len(DILATIONS) - 1)
        def _():
            def mix(i, _):
                rows = pl.ds(pl.multiple_of(i * CH, CH), CH)
                l0, l1, l2 = lnat[0, rows, :], lnat[1, rows, :], lnat[2, rows, :]
                m = jnp.maximum(jnp.maximum(l0, l1), l2)
                e0, e1, e2 = jnp.exp(l0 - m), jnp.exp(l1 - m), jnp.exp(l2 - m)
                den = e0 + e1 + e2
                a = (e0 * onat[0, rows, :] + e1 * onat[1, rows, :] + e2 * onat[2, rows, :]) / den
                at_ref[rows, :] = a.astype(BF16)
                ls_ref[rows, :] = m + jnp.log(den)
                return 0

            lax.fori_loop(0, S // CH, mix, 0)

    out_spec = pl.BlockSpec((S, LANES), lambda hp, g: (0, hp))
    return _call(
        body, sides, name="attn_fwd", grid=(4, 3),
        in_specs=_qk_specs() + _tab_specs() + [_vec_spec(), _vec_spec()],
        out_specs=[out_spec, out_spec],
        out_shape=[jax.ShapeDtypeStruct((S, CC), BF16), jax.ShapeDtypeStruct((S, CC), F32)],
        scratch_shapes=[pltpu.VMEM((S, LANES), BF16)] * 3 + [pltpu.VMEM((S, LANES), F32)] * 2
        + [pltpu.VMEM((3, S, LANES), F32)] * 2 + [pltpu.VMEM((S, LANES), F32)] * 2,
        args=(proj, proj, proj, *tabs, qw2, kw2))


def attn_bwd(proj, tabs, qw2, kw2, d_attn, attn, lse, sides=()):
    CH = 256

    def body(q_ref, k_ref, v_ref, c_ref, s1_ref, s2_ref, qw_ref, kw_ref, do_ref, at_ref, ls_ref,
             dq_ref, dk_ref, dv_ref, gqw_ref, gkw_ref,
             qs, ks, vs, dos, dsub, lsub, dqs, dks, dvs, dnat, qx, kx, dvn, tnq, tnk, rrq, rrk):
        hp, g = pl.program_id(0), pl.program_id(1)
        lo = lax.broadcasted_iota(jnp.int32, (1, LANES), 1) < HD
        e = _head_mat()
        both = ((q_ref, qw_ref, qx, tnq, rrq, HD ** -0.5), (k_ref, kw_ref, kx, tnk, rrk, 1.0))

        @pl.when((hp == 0) & (g == 0))
        def _():
            gqw_ref[...] = jnp.zeros_like(gqw_ref)
            gkw_ref[...] = jnp.zeros_like(gkw_ref)

        def prep(i, _):
            rows = pl.ds(pl.multiple_of(i * CH, CH), CH)
            dnat[rows, :] = _head_mean(do_ref[rows, :] * at_ref[rows, :].astype(F32), e) * float(HD)
            c, s1, s2 = c_ref[rows, :], s1_ref[rows, :], s2_ref[rows, :]
            for t_ref, w_ref, x, tn_s, rr_s, scale in both:
                t = t_ref[rows, :]
                rr = lax.rsqrt(_head_mean(t * t, e) + EPS)
                tn = t * rr
                rr_s[rows, :] = rr
                tn_s[rows, :] = tn
                x[rows, :] = _rope(tn * w_ref[...], c, s1, s2) * scale
            return 0

        lax.fori_loop(0, S // CH, prep, 0, unroll=4)

        def group(d):
            L = S // d

            ru, nb = _interleave(d)

            def stage(r, off):
                for c0 in range(0, L, CH):
                    n = min(CH, L)
                    rows = _sub_rows(r, d, c0, n)
                    dst = pl.ds(off + c0, n)
                    qs[dst, :] = qx[rows, :].astype(BF16)
                    ks[dst, :] = kx[rows, :].astype(BF16)
                    vs[dst, :] = v_ref[rows, :].astype(BF16)
                    dos[dst, :] = do_ref[rows, :].astype(BF16)
                    dsub[dst, :] = dnat[rows, :]
                    lsub[dst, :] = ls_ref[rows, :]
                    dks[dst, :] = jnp.zeros((n, LANES), F32)
                    dvs[dst, :] = jnp.zeros((n, LANES), F32)

            def one(off, i):
                W, q0, k0, valid = _band_window(i, L)
                qrows, krows = pl.ds(off + q0, TQ), pl.ds(off + k0, W)
                q2 = _stack_heads(qs[qrows, :], lo)
                do2 = _stack_heads(dos[qrows, :], lo)
                kk, vv = ks[krows, :], vs[krows, :]
                lse_b, dd_b = lsub[qrows, :], dsub[qrows, :]
                lse2 = jnp.concatenate([lse_b[:, 0:1], lse_b[:, HD:HD + 1]], axis=0)
                dd2 = jnp.concatenate([dd_b[:, 0:1], dd_b[:, HD:HD + 1]], axis=0)
                sc = jnp.where(valid, _dot_nt(q2, kk), NEG_INF)
                p = jnp.exp(sc - lse2)
                ds = (p * (_dot_nt(do2, vv) - dd2)).astype(BF16)
                dqs[qrows, :] = _unstack_heads(_dot(ds, kk), lo)
                dks[krows, :] = dks[krows, :] + _dot_tn(ds, q2)
                dvs[krows, :] = dvs[krows, :] + _dot_tn(p.astype(BF16), do2)

            def unstage(r, off):
                for c0 in range(0, L, CH):
                    n = min(CH, L)
                    rows = _sub_rows(r, d, c0, n)
                    src = pl.ds(off + c0, n)
                    qx[rows, :] = dqs[src, :]
                    kx[rows, :] = dks[src, :]
                    dvn[rows, :] = dvs[src, :]

            def step(t, _):
                for u in range(ru):
                    stage(t * ru + u, u * L)
                _for_blocks(L // TQ // nb, lambda j: [one(u * L, j * nb + b) for u in range(ru) for b in range(nb)])
                for u in range(ru):
                    unstage(t * ru + u, u * L)
                return 0

            lax.fori_loop(0, d // ru, step, 0)

        for gi, d in enumerate(DILATIONS):
            pl.when(g == gi)(functools.partial(group, d))

        def emit(i, _):
            rows = pl.ds(pl.multiple_of(i * CH, CH), CH)
            c, s1, s2 = c_ref[rows, :], s1_ref[rows, :], s2_ref[rows, :]
            for (_, w_ref, x, tn_s, rr_s, scale), out, gw_ref in zip(both, (dq_ref, dk_ref), (gqw_ref, gkw_ref)):
                tn = tn_s[rows, :]
                dy = _rope_t(x[rows, :] * scale, c, s1, s2)
                gw_ref[0:1, :] = gw_ref[0:1, :] + jnp.sum(dy * tn, axis=0, keepdims=True)
                dtn = dy * w_ref[...]
                out[rows, :] = (rr_s[rows, :] * (dtn - tn * _head_mean(dtn * tn, e))).astype(BF16)
            dv_ref[rows, :] = dvn[rows, :].astype(BF16)
            return 0

        lax.fori_loop(0, S // CH, emit, 0, unroll=4)

    nat_spec = pl.BlockSpec((S, LANES), lambda hp, g: (0, hp))
    out_spec = pl.BlockSpec((None, S, LANES), lambda hp, g: (g, 0, hp))
    acc_spec = pl.BlockSpec((8, LANES), lambda hp, g: (0, 0))
    return _call(
        body, sides, name="attn_bwd", grid=(4, 3),
        in_specs=_qk_specs() + _tab_specs() + [_vec_spec(), _vec_spec(), nat_spec, nat_spec, nat_spec],
        out_specs=[out_spec] * 3 + [acc_spec] * 2,
        out_shape=[jax.ShapeDtypeStruct((QKV // PLANE, S, PLANE), BF16)] * 3 + [jax.ShapeDtypeStruct((8, LANES), F32)] * 2,
        scratch_shapes=[pltpu.VMEM((S, LANES), BF16)] * 4 + [pltpu.VMEM((S, LANES), F32)] * 13,
        args=(proj, proj, proj, *tabs, qw2, kw2, d_attn, attn, lse))


PADR = 16
CT = 128


def _conv_specs():
    return [pl.BlockSpec((S, CC), lambda i: (0, OFF_CA // CC)), pl.BlockSpec((S, CC), lambda i: (0, OFF_CB // CC))]


NCB = CC // LANES


def _pad_zero(pad):
    for cb in range(NCB):
        pad[cb, 0:PADR, :] = jnp.zeros((PADR, LANES), F32)
        pad[cb, PADR + S:PADR + S + PADR, :] = jnp.zeros((PADR, LANES), F32)


def _pad_store(pad, row0, n, val):
    for cb in range(NCB):
        pad[cb, pl.ds(pl.multiple_of(row0 + PADR, 8), n), :] = val[:, cb * LANES:(cb + 1) * LANES]


def _taps(pad_ref, cb, s0, weights):
    acc = jnp.zeros((CT, LANES), F32)
    for k in range(KW):
        acc = acc + weights[k] * pad_ref[cb, pl.ds(s0 + k + 1, CT), :]
    return acc


def conv_fwd(proj, conv_w, conv_b, ln_w, ln_b, sides=()):
    def body(a_ref, b_ref, w_ref, cb_ref, lw_ref, lb_ref, c_ref, u3_ref, upad):
        _pad_zero(upad)

        def glu(i, _):
            rows = pl.ds(pl.multiple_of(i * TM, TM), TM)
            _pad_store(upad, i * TM, TM, a_ref[rows, :] * _sigmoid(b_ref[rows, :]))
            return 0

        lax.fori_loop(0, S // TM, glu, 0)

        def chunk(i, _):
            s0 = pl.multiple_of(i * CT, CT)
            for cb in range(CC // LANES):
                cols = slice(cb * LANES, (cb + 1) * LANES)
                w = [w_ref[k:k + 1, cols] for k in range(KW)]
                c_ref[pl.ds(s0, CT), cols] = _taps(upad, cb, s0, w) + cb_ref[:, cols]
            cv = c_ref[pl.ds(s0, CT), :]
            mu = jnp.mean(cv, axis=-1, keepdims=True)
            xc = cv - mu
            rstd = lax.rsqrt(jnp.mean(xc * xc, axis=-1, keepdims=True) + EPS)
            yl = xc * rstd * lw_ref[...] + lb_ref[...]
            u3_ref[pl.ds(s0, CT), :] = (yl * _sigmoid(yl)).astype(BF16)
            return 0

        lax.fori_loop(0, S // CT, chunk, 0)

    vec = pl.BlockSpec((1, CC), lambda i: (0, 0))
    full = pl.BlockSpec((S, CC), lambda i: (0, 0))
    return _call(
        body, sides, name="conv_fwd", grid=(1,),
        in_specs=_conv_specs() + [pl.BlockSpec((KW, CC), lambda i: (0, 0)), vec, vec, vec],
        out_specs=[full, full],
        out_shape=[jax.ShapeDtypeStruct((S, CC), F32), jax.ShapeDtypeStruct((S, CC), BF16)],
        scratch_shapes=[pltpu.VMEM((NCB, S + 2 * PADR, LANES), F32)],
        args=(proj, proj, conv_w, conv_b, ln_w, ln_b))


def conv_bwd(proj, cpre, d_u3, conv_w, conv_w_rev, ln_w, ln_b, sides=()):
    def body(a_ref, b_ref, c_ref, du3_ref, w_ref, wr_ref, lw_ref, lb_ref,
             dc_ref, gw_ref, gcb_ref, glw_ref, glb_ref, upad, dpad):
        _pad_zero(upad)
        _pad_zero(dpad)
        gw_ref[...] = jnp.zeros_like(gw_ref)

        def ln_bwd(i, carry):
            gcb, glw, glb = carry
            rows = pl.ds(pl.multiple_of(i * TM, TM), TM)
            _pad_store(upad, i * TM, TM, a_ref[rows, :] * _sigmoid(b_ref[rows, :]))
            cv = c_ref[rows, :]
            mu = jnp.mean(cv, axis=-1, keepdims=True)
            xc = cv - mu
            rstd = lax.rsqrt(jnp.mean(xc * xc, axis=-1, keepdims=True) + EPS)
            xh = xc * rstd
            yl = xh * lw_ref[...] + lb_ref[...]
            dyl = du3_ref[rows, :] * _dsilu(yl, _sigmoid(yl))
            dxh = dyl * lw_ref[...]
            dcv = rstd * (dxh - jnp.mean(dxh, axis=-1, keepdims=True)
                          - xh * jnp.mean(dxh * xh, axis=-1, keepdims=True))
            _pad_store(dpad, i * TM, TM, dcv)
            return (gcb + jnp.sum(dcv, axis=0, keepdims=True),
                    glw + jnp.sum(dyl * xh, axis=0, keepdims=True),
                    glb + jnp.sum(dyl, axis=0, keepdims=True))

        z = jnp.zeros((1, CC), F32)
        gcb, glw, glb = lax.fori_loop(0, S // TM, ln_bwd, (z, z, z))
        gcb_ref[...] = gcb
        glw_ref[...] = glw
        glb_ref[...] = glb

        def chunk(i, _):
            s0 = pl.multiple_of(i * CT, CT)
            for cb in range(CC // LANES):
                cols = slice(cb * LANES, (cb + 1) * LANES)
                wr = [wr_ref[k:k + 1, cols] for k in range(KW)]
                du = _taps(dpad, cb, s0, wr)
                dcv = dpad[cb, pl.ds(s0 + PADR, CT), :]
                for k in range(KW):
                    gw_ref[k:k + 1, cols] = gw_ref[k:k + 1, cols] + jnp.sum(
                        upad[cb, pl.ds(s0 + k + 1, CT), :] * dcv, axis=0, keepdims=True)
                av = a_ref[pl.ds(s0, CT), cols]
                sb = _sigmoid(b_ref[pl.ds(s0, CT), cols])
                dc_ref[0, pl.ds(s0, CT), cols] = (du * sb).astype(BF16)
                dc_ref[1, pl.ds(s0, CT), cols] = (du * av * sb * (1.0 - sb)).astype(BF16)
            return 0

        lax.fori_loop(0, S // CT, chunk, 0)

    vec = pl.BlockSpec((1, CC), lambda i: (0, 0))
    full = pl.BlockSpec((S, CC), lambda i: (0, 0))
    wsp = pl.BlockSpec((KW, CC), lambda i: (0, 0))
    return _call(
        body, sides, name="conv_bwd", grid=(1,),
        in_specs=_conv_specs() + [full, full, wsp, wsp, vec, vec],
        out_specs=[pl.BlockSpec((2, S, CC), lambda i: (0, 0, 0)), wsp, vec, vec, vec],
        out_shape=[jax.ShapeDtypeStruct((2, S, CC), BF16), jax.ShapeDtypeStruct((KW, CC), F32)]
        + [jax.ShapeDtypeStruct((1, CC), F32)] * 3,
        scratch_shapes=[pltpu.VMEM((NCB, S + 2 * PADR, LANES), F32)] * 2,
        args=(proj, proj, cpre, d_u3, conv_w, conv_w_rev, ln_w, ln_b))


def _gate_specs():
    return [_row(CC, col=OFF_GA // CC + j) for j in range(4)]


def _gates(g_refs, bg_ref):
    ga = _sigmoid(jnp.concatenate([g_refs[0][...], g_refs[1][...]], axis=1) + bg_ref[0:1, :])
    gb = _sigmoid(jnp.concatenate([g_refs[2][...], g_refs[3][...]], axis=1) + bg_ref[1:2, :])
    return ga, gb


def mix_out(x, proj, b_gate, attn, u3, w_o, w_pw, w_out):
    def body(x_ref, g0, g1, g2, g3, bg_ref, at_ref, u3_ref, wo_ref, wp_ref, wout_ref,
             x1_ref, z_ref, ya_ref, yb_ref):
        ga, gb = _gates((g0, g1, g2, g3), bg_ref)
        ya = _dot_nt(at_ref[...], wo_ref[...])
        yb = _dot_nt(u3_ref[...], wp_ref[...])
        z = (ga * ya + gb * yb).astype(BF16)
        ya_ref[...] = ya.astype(BF16)
        yb_ref[...] = yb.astype(BF16)
        z_ref[...] = z
        x1_ref[...] = x_ref[...] + _dot(z, wout_ref[...])

    return pl.pallas_call(
        body, name="mix_out", grid=(S // TM,),
        in_specs=[_row(D)] + _gate_specs() + [_res((2, D)), _row(CC), _row(CC),
                                              _res((D, CC)), _res((D, CC)), _res((D, D))],
        out_specs=[_row(D)] * 4,
        out_shape=[jax.ShapeDtypeStruct((S, D), F32)] + [jax.ShapeDtypeStruct((S, D), BF16)] * 3,
        compiler_params=_cp(dimension_semantics=("arbitrary",)),
    )(x, proj, proj, proj, proj, b_gate, attn, u3, w_o, w_pw, w_out)


def out_bwd(d_x1b, proj, b_gate, ya, yb, w_o, w_pw, w_out, sides=()):
    def body(dx_ref, g0, g1, g2, g3, bg_ref, ya_ref, yb_ref, wo_ref, wp_ref, wout_ref,
             dya_ref, dyb_ref, dgl_ref, dat_ref, du3_ref, gbg_ref):
        @pl.when(pl.program_id(0) == 0)
        def _():
            gbg_ref[...] = jnp.zeros_like(gbg_ref)

        ga, gb = _gates((g0, g1, g2, g3), bg_ref)
        dz = _dot_nt(dx_ref[...], wout_ref[...])
        dya = (dz * ga).astype(BF16)
        dyb = (dz * gb).astype(BF16)
        dgla = dz * ya_ref[...].astype(F32) * ga * (1.0 - ga)
        dglb = dz * yb_ref[...].astype(F32) * gb * (1.0 - gb)
        dya_ref[...] = dya
        dyb_ref[...] = dyb
        for j in range(2):
            dgl_ref[j] = dgla[:, j * PLANE:(j + 1) * PLANE].astype(BF16)
            dgl_ref[2 + j] = dglb[:, j * PLANE:(j + 1) * PLANE].astype(BF16)
        gbg_ref[0:1, :] = gbg_ref[0:1, :] + jnp.sum(dgla, axis=0, keepdims=True)
        gbg_ref[1:2, :] = gbg_ref[1:2, :] + jnp.sum(dglb, axis=0, keepdims=True)
        dat_ref[...] = _dot(dya, wo_ref[...])
        du3_ref[...] = _dot(dyb, wp_ref[...])

    return _call(
        body, sides, name="out_bwd", grid=(S // TM,),
        in_specs=[_row(D)] + _gate_specs() + [_res((2, D)), _row(D), _row(D),
                                              _res((D, CC)), _res((D, CC)), _res((D, D))],
        out_specs=[_row(D), _row(D), _planes(2 * D), _row(CC), _row(CC), pl.BlockSpec((2, D), lambda i: (0, 0))],
        out_shape=[jax.ShapeDtypeStruct((S, D), BF16)] * 2 + [jax.ShapeDtypeStruct((2 * D // PLANE, S, PLANE), BF16)]
        + [jax.ShapeDtypeStruct((S, CC), F32)] * 2 + [jax.ShapeDtypeStruct((2, D), F32)],
        args=(d_x1b, proj, proj, proj, proj, b_gate, ya, yb, w_o, w_pw, w_out))


def ffn_in(x1, norm_w, w_ffn_in, sides=()):
    half = FF // 2

    def body(x_ref, nw_ref, w_ref, h_ref, gu_ref, f_ref):
        xv = x_ref[...]
        r = lax.rsqrt(jnp.mean(xv * xv, axis=-1, keepdims=True) + EPS)
        h = (xv * r * nw_ref[...]).astype(BF16)
        h_ref[...] = h
        for j in range(2):
            gt = _dot_nt(h, w_ref[j * half:(j + 1) * half, :])
            up = _dot_nt(h, w_ref[FF + j * half:FF + (j + 1) * half, :])
            gu_ref[:, j * half:(j + 1) * half] = gt.astype(BF16)
            gu_ref[:, FF + j * half:FF + (j + 1) * half] = up.astype(BF16)
            f_ref[:, j * half:(j + 1) * half] = (gt * _sigmoid(gt) * up).astype(BF16)

    return _call(
        body, sides, name="ffn_in", grid=(S // TM,),
        in_specs=[_row(D), _res((1, D)), _res((2 * FF, D))],
        out_specs=[_row(D), _row(2 * FF), _row(FF)],
        out_shape=[jax.ShapeDtypeStruct((S, D), BF16), jax.ShapeDtypeStruct((S, 2 * FF), BF16),
                   jax.ShapeDtypeStruct((S, FF), BF16)],
        args=(x1, norm_w, w_ffn_in))


def ffn_out_loss(x1, f, w_ffn_out, target):
    def body(x_ref, f_ref, w_ref, t_ref, dy_ref, dyb_ref, sq_ref):
        @pl.when(pl.program_id(0) == 0)
        def _():
            sq_ref[...] = jnp.zeros_like(sq_ref)

        diff = x_ref[...] + _dot(f_ref[...], w_ref[...]) - t_ref[...]
        dy = diff * (1.0 / D)
        dy_ref[...] = dy
        dyb_ref[...] = dy.astype(BF16)
        sq_ref[...] = sq_ref[...] + jnp.sum((diff * diff).reshape(TM // 8, 8, D), axis=0)

    return pl.pallas_call(
        body, name="ffn_out_loss", grid=(S // TM,),
        in_specs=[_row(D), _row(FF), _res((FF, D)), _row(D)],
        out_specs=[_row(D), _row(D), pl.BlockSpec((8, D), lambda i: (0, 0))],
        out_shape=[jax.ShapeDtypeStruct((S, D), F32), jax.ShapeDtypeStruct((S, D), BF16),
                   jax.ShapeDtypeStruct((8, D), F32)],
        compiler_params=_cp(dimension_semantics=("arbitrary",)),
    )(x1, f, w_ffn_out, target)


def _rms_bwd(xv, nw, dh):
    r = lax.rsqrt(jnp.mean(xv * xv, axis=-1, keepdims=True) + EPS)
    xn = xv * r
    dxn = dh * nw
    dx = r * (dxn - xn * jnp.mean(dxn * xn, axis=-1, keepdims=True))
    return dx, dh * xn


def ffn_bwd(dy, dyb, gu, x1, norm_w, w_ffn_in, w_ffn_out, sides=()):
    def body(dy_ref, dyb_ref, gu_ref, x_ref, nw_ref, wi_ref, wo_ref, dgu_ref, dx_ref, dxb_ref, gn_ref):
        @pl.when(pl.program_id(0) == 0)
        def _():
            gn_ref[...] = jnp.zeros_like(gn_ref)

        df = _dot_nt(dyb_ref[...], wo_ref[...])
        gt = gu_ref[:, 0:FF].astype(F32)
        up = gu_ref[:, FF:2 * FF].astype(F32)
        sg = _sigmoid(gt)
        dgt = (df * up * _dsilu(gt, sg)).astype(BF16)
        dup = (df * gt * sg).astype(BF16)
        dgu_ref[:, 0:FF] = dgt
        dgu_ref[:, FF:2 * FF] = dup
        dh = _dot(dgt, wi_ref[0:FF, :]) + _dot(dup, wi_ref[FF:2 * FF, :])
        dxn, gw = _rms_bwd(x_ref[...], nw_ref[...], dh)
        dx = dy_ref[...] + dxn
        dx_ref[...] = dx
        dxb_ref[...] = dx.astype(BF16)
        gn_ref[...] = gn_ref[...] + jnp.sum(gw, axis=0, keepdims=True)

    return _call(
        body, sides, name="ffn_bwd", grid=(S // TM,),
        in_specs=[_row(D), _row(D), _row(2 * FF), _row(D), _res((1, D)), _res((2 * FF, D)), _res((FF, D))],
        out_specs=[_row(2 * FF), _row(D), _row(D), pl.BlockSpec((1, D), lambda i: (0, 0))],
        out_shape=[jax.ShapeDtypeStruct((S, 2 * FF), BF16), jax.ShapeDtypeStruct((S, D), F32),
                   jax.ShapeDtypeStruct((S, D), BF16), jax.ShapeDtypeStruct((1, D), F32)],
        args=(dy, dyb, gu, x1, norm_w, w_ffn_in, w_ffn_out))


def in_bwd(d_q, d_k, d_v, d_conv, d_gl, w_in, x, d_x1, norm_w, sides=()):
    segs = ((OFF_Q, QKV), (OFF_K, QKV), (OFF_V, QKV), (OFF_CA, 2 * CC), (OFF_GA, 2 * D))

    def body(dq_ref, dk_ref, dv_ref, dc_ref, dg_ref, w_ref, x_ref, dx1_ref, nw_ref, gx_ref, gn_ref):
        @pl.when(pl.program_id(0) == 0)
        def _():
            gn_ref[...] = jnp.zeros_like(gn_ref)

        dh = jnp.zeros((TM, D), F32)
        for ref, (off, width) in zip((dq_ref, dk_ref, dv_ref, dc_ref, dg_ref), segs):
            for j in range(width // PLANE):
                dh = dh + _dot(ref[j], w_ref[off + j * PLANE:off + (j + 1) * PLANE, :])
        dxn, gw = _rms_bwd(x_ref[...], nw_ref[...], dh)
        gx_ref[...] = dx1_ref[...] + dxn
        gn_ref[...] = gn_ref[...] + jnp.sum(gw, axis=0, keepdims=True)

    return _call(
        body, sides, name="in_bwd", grid=(S // TM,),
        in_specs=[_planes(QKV)] * 3 + [_planes(2 * CC), _planes(2 * D), _res((INW, D)), _row(D), _row(D), _res((1, D))],
        out_specs=[_row(D), pl.BlockSpec((1, D), lambda i: (0, 0))],
        out_shape=[jax.ShapeDtypeStruct((S, D), F32), jax.ShapeDtypeStruct((1, D), F32)],
        args=(d_q, d_k, d_v, d_conv, d_gl, w_in, x, d_x1, norm_w))


def mm_tn(name, a, b, tm, tn, sides=()):
    M, N = a.shape[1], b.shape[1]

    def body(a_ref, b_ref, o_ref):
        o_ref[...] = _dot_tn(a_ref[...], b_ref[...])

    res = _call(
        body, sides, name=name, grid=(M // tm, N // tn),
        in_specs=[pl.BlockSpec((S, tm), lambda i, j: (0, i)), pl.BlockSpec((S, tn), lambda i, j: (0, j))],
        out_specs=[pl.BlockSpec((tm, tn), lambda i, j: (i, j))],
        out_shape=[jax.ShapeDtypeStruct((M, N), F32)],
        args=(a, b))
    return (res[0][0], res[1]) if sides else res[0]


GW_IN_TN = PLANE
GW_IN_SPLIT = (768, 256)


def gw_in_t(name, ht, d_segs, col0, hw, sides=()):
    tn = GW_IN_TN
    starts, t0 = [], 0
    for seg in d_segs:
        starts.append(t0)
        t0 += seg.shape[0]
    ntiles = [seg.shape[0] for seg in d_segs]

    def body(h_ref, *refs):
        a_refs, o_ref = refs[:-1], refs[-1]
        n = pl.program_id(0)
        for a_ref, st, nt in zip(a_refs, starts, ntiles):
            @pl.when((n >= st) & (n < st + nt))
            def _(a_ref=a_ref):
                o_ref[...] = _dot(h_ref[...], a_ref[...]).T

    def seg_spec(st, nt):
        return pl.BlockSpec((None, S, tn), lambda n: (jnp.clip(n - st, 0, nt - 1), 0, 0))

    res = _call(
        body, sides, name=name, grid=(INW // tn,),
        in_specs=[pl.BlockSpec((hw, S), lambda n: (col0 // hw, 0))] + [seg_spec(st, nt) for st, nt in zip(starts, ntiles)],
        out_specs=[pl.BlockSpec((tn, hw), lambda n: (n, 0))],
        out_shape=[jax.ShapeDtypeStruct((INW, hw), F32)],
        args=(ht, *d_segs))
    return (res[0][0], res[1]) if sides else res[0]


def _place():
    x, y, c = lax.axis_index("x"), lax.axis_index("y"), lax.axis_index("c")
    chips = [(1 - x, y), (x, 1 - y), (1 - x, 1 - y)]
    return x, y, c, chips


def _sems(n):
    return pltpu.SemaphoreType.DMA((n,))


def _remote(src, dst, send, recv, k, to):
    return pltpu.make_async_remote_copy(src_ref=src, dst_ref=dst, send_sem=send.at[k], recv_sem=recv.at[k],
                                        device_id=to, device_id_type=MESH)


def _cast_rows(dst, src, cols=slice(None)):
    rows = src.shape[0]
    step = next((s for s in (128, 64, 32, 16) if rows % s == 0), rows)
    for r0 in range(0, rows, step):
        dst[r0:r0 + step, cols] = src[r0:r0 + step, :].astype(dst.dtype)


def comm_only(name, sides):
    def body():
        pass

    return _call(body, sides, name=name, grid=(1,), in_specs=[], out_specs=[], out_shape=[], args=())[1]


def ag_blocks(shard, dtype):
    R, W = shard.shape

    def copy(outs, scr, k, block, to, src=None):
        dst = outs[0].at[block]
        return _remote(dst if src is None else src, dst, scr[1], scr[2], k, to)

    def local(outs, scr, me):
        return pltpu.make_async_copy(scr[0], outs[0].at[me], scr[3].at[0])

    def start(ins, outs, scr):
        x, y, c, chips = _place()
        me = 4 * x + 2 * y + c
        _cast_rows(scr[0], ins[0])
        local(outs, scr, me).start()
        copy(outs, scr, 0, me, (x, y, 1 - c), src=scr[0]).start()
        for j, (cx, cy) in enumerate(chips):
            copy(outs, scr, 1 + j, me, (cx, cy, c), src=scr[0]).start()

    def finish(ins, outs, scr):
        x, y, c, chips = _place()
        me, sib = 4 * x + 2 * y + c, (x, y, 1 - c)
        passed = []
        for j, (cx, cy) in enumerate(chips):
            theirs = 4 * cx + 2 * cy + c
            copy(outs, scr, 1 + j, theirs, (x, y, c)).wait_recv()
            fwd = copy(outs, scr, 4 + j, theirs, sib)
            fwd.start()
            passed.append(fwd)
        copy(outs, scr, 0, 4 * x + 2 * y + 1 - c, (x, y, c)).wait_recv()
        for j, (cx, cy) in enumerate(chips):
            copy(outs, scr, 4 + j, 4 * cx + 2 * cy + 1 - c, (x, y, c)).wait_recv()
        copy(outs, scr, 0, me, sib, src=scr[0]).wait_send()
        for j, (cx, cy) in enumerate(chips):
            copy(outs, scr, 1 + j, me, (cx, cy, c), src=scr[0]).wait_send()
        for fwd in passed:
            fwd.wait_send()
        local(outs, scr, me).wait()

    return Side((shard,), (VMEM,), (jax.ShapeDtypeStruct((NDEV, R, W), dtype),),
                (pltpu.VMEM((R, W), dtype), _sems(7), _sems(7), _sems(1)), start, finish, None, "dsxy")


def ag_blocks_relay(shard, dtype, transpose=False):
    R, W = shard.shape[::-1] if transpose else shard.shape
    half = R // 2

    def copy(outs, scr, k, block, to, src=None, rows=None):
        dst = outs[0].at[block] if rows is None else outs[0].at[block, pl.ds(rows * half, half), :]
        return _remote(dst if src is None else src, dst, scr[1], scr[2], k, to)

    def local(outs, scr, me):
        return pltpu.make_async_copy(scr[0], outs[0].at[me], scr[3].at[0])

    def own(outs, scr):
        x, y, c, _ = _place()
        me = 4 * x + 2 * y + c
        return [copy(outs, scr, k, me, to, src=scr[0])
                for k, to in enumerate([(x, y, 1 - c), (1 - x, y, c), (x, 1 - y, c)])]

    def start(ins, outs, scr):
        x, y, c, _ = _place()
        if transpose:
            scr[0][...] = ins[0][...].T.astype(dtype)
        else:
            _cast_rows(scr[0], ins[0])
        local(outs, scr, 4 * x + 2 * y + c).start()
        for cp in own(outs, scr):
            cp.start()

    def passed_on(outs, scr):
        x, y, c, _ = _place()
        sib, xn, yn = (x, y, 1 - c), (1 - x, y, c), (x, 1 - y, c)
        b_xn, b_yn, b_dg = 4 * (1 - x) + 2 * y + c, 4 * x + 2 * (1 - y) + c, 4 * (1 - x) + 2 * (1 - y) + c
        near = [copy(outs, scr, 5, b_xn, yn, rows=0), copy(outs, scr, 3, b_xn, sib),
                copy(outs, scr, 6, b_yn, xn, rows=1), copy(outs, scr, 4, b_yn, sib)]
        far = [copy(outs, scr, 7, b_dg, sib, rows=0), copy(outs, scr, 8, b_dg, sib, rows=1)]
        return (b_xn, b_yn, b_dg), near, far

    def mid(ins, outs, scr):
        x, y, c, _ = _place()
        (b_xn, b_yn, _), near, _ = passed_on(outs, scr)
        copy(outs, scr, 1, b_xn, (x, y, c)).wait_recv()
        near[0].start()
        near[1].start()
        copy(outs, scr, 2, b_yn, (x, y, c)).wait_recv()
        near[2].start()
        near[3].start()

    def finish(ins, outs, scr):
        x, y, c, _ = _place()
        here = (x, y, c)
        (b_xn, b_yn, b_dg), near, far = passed_on(outs, scr)
        copy(outs, scr, 5, b_dg, here, rows=0).wait_recv()
        far[0].start()
        copy(outs, scr, 6, b_dg, here, rows=1).wait_recv()
        far[1].start()
        flip = 1 - 2 * c
        copy(outs, scr, 0, 4 * x + 2 * y + 1 - c, here).wait_recv()
        copy(outs, scr, 3, b_xn + flip, here).wait_recv()
        copy(outs, scr, 4, b_yn + flip, here).wait_recv()
        copy(outs, scr, 7, b_dg + flip, here, rows=0).wait_recv()
        copy(outs, scr, 8, b_dg + flip, here, rows=1).wait_recv()
        for cp in own(outs, scr) + near + far:
            cp.wait_send()
        local(outs, scr, 4 * x + 2 * y + c).wait()

    return Side((shard,), (VMEM,), (jax.ShapeDtypeStruct((NDEV, R, W), dtype),),
                (pltpu.VMEM((R, W), dtype), _sems(9), _sems(9), _sems(1)), start, finish, mid, "sxy")


def copies_side(args, out_shape, n_copies, plan, peers):
    def copies(ins, outs, scr):
        return [_remote(s_, d_, scr[0], scr[1], i, to) for i, (s_, d_, to) in enumerate(plan(ins, outs))]

    def start(ins, outs, scr):
        for cp in copies(ins, outs, scr):
            cp.start()

    def finish(ins, outs, scr):
        for cp in copies(ins, outs, scr):
            cp.wait()

    return Side(tuple(args), (ANY,) * len(args), tuple(out_shape), (_sems(n_copies), _sems(n_copies)),
                start, finish, None, peers)


def rs_to_sibling(grads):
    out_shape = [jax.ShapeDtypeStruct((4,) + g.shape[1:], F32) for g in grads]

    def plan(ins, outs):
        x, y, c, _ = _place()
        return [(g.at[2 * k + 1 - c], r.at[k], (x, y, 1 - c)) for g, r in zip(ins, outs) for k in range(4)]

    return copies_side(grads, out_shape, 4 * len(grads), plan, "s")


def rs_to_chips(parts):
    out_shape = [jax.ShapeDtypeStruct((3,) + p.shape[1:], BF16) for p in parts]

    def plan(ins, outs):
        x, y, c, chips = _place()
        return [(p.at[2 * cx + cy], r.at[j], (cx, cy, c))
                for p, r in zip(ins, outs) for j, (cx, cy) in enumerate(chips)]

    return copies_side(parts, out_shape, 3 * len(parts), plan, "dxy")


def rs_to_chips_combined(part):
    _, R, W = part.shape
    half = R // 2
    top, bot = pl.ds(0, half), pl.ds(half, half)

    def copies(ins, outs, scr):
        p, r = ins[0], outs[0]
        loc_a, loc_b, in_x, in_y, comb_a, comb_b, send, recv, loc = scr
        x, y, c, _ = _place()
        xn, yn = (1 - x, y, c), (x, 1 - y, c)
        k_xn, k_yn, k_dg = 2 * (1 - x) + y, 2 * x + 1 - y, 2 * (1 - x) + 1 - y
        direct = [_remote(p.at[k_xn, top, :], r.at[0, top, :], send, recv, 0, xn),
                  _remote(p.at[k_yn, bot, :], r.at[1, bot, :], send, recv, 1, yn),
                  _remote(p.at[k_dg, top, :], in_x, send, recv, 2, xn),
                  _remote(p.at[k_dg, bot, :], in_y, send, recv, 3, yn)]
        combined = [_remote(comb_a, r.at[1, top, :], send, recv, 4, yn),
                    _remote(comb_b, r.at[0, bot, :], send, recv, 5, xn)]
        local = [pltpu.make_async_copy(p.at[k_yn, top, :], loc_a, loc.at[0]),
                 pltpu.make_async_copy(p.at[k_xn, bot, :], loc_b, loc.at[1])]
        return direct, combined, local

    def start(ins, outs, scr):
        direct, _, local = copies(ins, outs, scr)
        for cp in local + direct:
            cp.start()

    def mid(ins, outs, scr):
        loc_a, loc_b, in_x, in_y, comb_a, comb_b = scr[:6]
        direct, combined, local = copies(ins, outs, scr)
        for mine, arrival, inbox, out, nxt in ((local[0], direct[2], in_x, comb_a, combined[0]),
                                               (local[1], direct[3], in_y, comb_b, combined[1])):
            mine.wait()
            arrival.wait_recv()
            src = loc_a if out is comb_a else loc_b
            out[...] = (src[...].astype(F32) + inbox[...].astype(F32)).astype(BF16)
            nxt.start()

    def finish(ins, outs, scr):
        direct, combined, _ = copies(ins, outs, scr)
        direct[0].wait_recv()
        direct[1].wait_recv()
        combined[0].wait_recv()
        combined[1].wait_recv()
        for cp in direct + combined:
            cp.wait_send()

    buf = pltpu.VMEM((half, W), BF16)
    return Side((part,), (ANY,), (jax.ShapeDtypeStruct((2, R, W), BF16),),
                (buf, buf, buf, buf, buf, buf, _sems(6), _sems(6), _sems(2)), start, finish, mid, "xy")


ADAM_TILE_BYTES = 3 * 512 * 1024


def _row_tiles(rows, width):
    return 2 if rows % 32 == 0 and rows * width * 4 > ADAM_TILE_BYTES else 1


def chip_sum(name, grad, recv, c_idx, chip_idx):
    _, R, C = grad.shape
    nt = 1
    tr = R // nt

    def body(s_ref, g_ref, r_ref, p_ref, own_ref):
        k = pl.program_id(1)
        tot = g_ref[0] + r_ref[0]
        p_ref[0] = tot.astype(BF16)

        @pl.when(k == s_ref[1])
        def _():
            own_ref[...] = tot

    grid_spec = pltpu.PrefetchScalarGridSpec(
        num_scalar_prefetch=1, grid=(nt, 4),
        in_specs=[pl.BlockSpec((1, tr, C), lambda i, k, s: (2 * k + s[0], i, 0)),
                  pl.BlockSpec((1, tr, C), lambda i, k, s: (k, i, 0))],
        out_specs=[pl.BlockSpec((1, tr, C), lambda i, k, s: (k, i, 0)),
                   pl.BlockSpec((tr, C), lambda i, k, s: (i, 0))])
    return pl.pallas_call(
        body, name=name, grid_spec=grid_spec,
        out_shape=[jax.ShapeDtypeStruct((4, R, C), BF16), jax.ShapeDtypeStruct((R, C), F32)],
        compiler_params=_cp(dimension_semantics=("arbitrary", "arbitrary")),
    )(jnp.stack([c_idx, chip_idx]), grad, recv)


def _adamw(w, g, m, v):
    m2 = ADAM_B1 * m + (1.0 - ADAM_B1) * g
    v2 = ADAM_B2 * v + (1.0 - ADAM_B2) * (g * g)
    m_hat = m2 / (1.0 - ADAM_B1 ** ADAM_STEP)
    v_hat = v2 / (1.0 - ADAM_B2 ** ADAM_STEP)
    delta = -ADAM_LR * (m_hat / (jnp.sqrt(v_hat) + ADAM_EPS) + ADAM_WD * w)
    return delta, m2, v2


def shard_adam(name, owns, recvs, w, m, v, io_t=False):
    n = len(owns)
    R = owns[0].shape[0]
    ct = min(o.shape[1] for o in owns)
    first = [sum(o.shape[1] for o in owns[:j]) // ct for j in range(n)]
    count = [o.shape[1] // ct for o in owns]
    nt = _row_tiles(R, ct)
    tr = R // nt

    def body(*refs):
        o_refs, r_refs = refs[:n], refs[n:2 * n]
        w_ref, m_ref, v_ref, g_ref, d_ref, nm_ref, nv_ref = refs[2 * n:]
        g = None
        for j in range(n):
            gj = o_refs[j][...]
            for q in range(recvs[j].shape[0]):
                gj = gj + r_refs[j][q].astype(F32)
            g = gj if g is None else jnp.where(pl.program_id(0) >= first[j], gj, g)
        t = (lambda a: a.T) if io_t else (lambda a: a)
        delta, m2, v2 = _adamw(t(w_ref[...]), g, t(m_ref[...]), t(v_ref[...]))
        g_ref[...] = t(g)
        d_ref[...] = t(delta)
        nm_ref[...] = t(m2)
        nv_ref[...] = t(v2)

    def part(j):
        return pl.BlockSpec((tr, ct), lambda k, i: (i, jnp.clip(k - first[j], 0, count[j] - 1)))

    def part3(j):
        return pl.BlockSpec((recvs[j].shape[0], tr, ct), lambda k, i: (0, i, jnp.clip(k - first[j], 0, count[j] - 1)))

    C = sum(count) * ct
    tile = pl.BlockSpec((ct, tr), lambda k, i: (k, i)) if io_t else pl.BlockSpec((tr, ct), lambda k, i: (i, k))
    return pl.pallas_call(
        body, name=name, grid=(sum(count), nt),
        in_specs=[part(j) for j in range(n)] + [part3(j) for j in range(n)] + [tile, tile, tile],
        out_specs=[tile] * 4, out_shape=[jax.ShapeDtypeStruct((C, R) if io_t else (R, C), F32)] * 4,
        compiler_params=_cp(dimension_semantics=("arbitrary", "arbitrary")),
    )(*owns, *recvs, w, m, v)


ROW_N1, ROW_N2, ROW_BG, ROW_QN, ROW_KN, ROW_CB, ROW_LW, ROW_LB, ROW_CW = 0, 1, 2, 4, 5, 6, 7, 8, 9
PACK_ROWS = 40
SMALL = ("norm1_w", "norm2_w", "b_gate", "q_norm_w", "k_norm_w", "conv_b", "conv_ln_w", "conv_ln_b", "conv_w")


def small_sync(g, sq, sides=()):
    ns = len(SMALL)

    def body(*refs):
        gi = dict(zip(SMALL, refs[:ns]))
        sq_ref, tot, pack, recv, send_sems, recv_sems = refs[ns:]
        x, y, c, _ = _place()
        me = 4 * x + 2 * y + c

        pack[...] = jnp.zeros_like(pack)
        pack[ROW_KN:ROW_KN + 1, LANES:2 * LANES] = jnp.full((1, LANES), (0.5 / D) * jnp.sum(sq_ref[...]), F32)
        pack[ROW_N1:ROW_N1 + 1, :] = gi["norm1_w"][...]
        pack[ROW_N2:ROW_N2 + 1, :] = gi["norm2_w"][...]
        pack[ROW_BG:ROW_BG + 2, :] = gi["b_gate"][...]
        pack[ROW_QN:ROW_QN + 1, 0:HD] = gi["q_norm_w"][...]
        pack[ROW_KN:ROW_KN + 1, 0:HD] = gi["k_norm_w"][...]
        pack[ROW_CB:ROW_CB + 1, 0:CC] = gi["conv_b"][...]
        pack[ROW_LW:ROW_LW + 1, 0:CC] = gi["conv_ln_w"][...]
        pack[ROW_LB:ROW_LB + 1, 0:CC] = gi["conv_ln_b"][...]
        pack[ROW_CW:ROW_CW + KW, 0:CC] = gi["conv_w"][...]

        copies = []
        for k in range(1, NDEV):
            peer = (x ^ (k >> 2), y ^ ((k >> 1) & 1), c ^ (k & 1))
            cp = pltpu.make_async_remote_copy(
                src_ref=pack, dst_ref=recv.at[me], send_sem=send_sems.at[k - 1], recv_sem=recv_sems.at[k - 1],
                device_id=peer, device_id_type=MESH)
            cp.start()
            copies.append(cp)
        recv[me] = pack[...]
        for cp in copies:
            cp.wait()
        acc = recv[0]
        for p in range(1, NDEV):
            acc = acc + recv[p]
        tot[...] = acc

    args = [g[k] for k in SMALL] + [sq]
    res = _call(
        body, sides, name="small_sync", grid=(1,), in_specs=[VMEM] * len(args), out_specs=[VMEM],
        out_shape=[jax.ShapeDtypeStruct((PACK_ROWS, D), F32)],
        scratch_shapes=[pltpu.VMEM((PACK_ROWS, D), F32), pltpu.VMEM((NDEV, PACK_ROWS, D), F32),
                        _sems(NDEV - 1), _sems(NDEV - 1)],
        args=args, own_comm=True)
    return (res[0][0], res[1]) if sides else res[0]


def small_adam(tot, w, m, v, me):
    ns = len(SMALL)

    def body(me_ref, tot, *refs):
        wi = dict(zip(SMALL, refs[:ns]))
        mi = dict(zip(SMALL, refs[ns:2 * ns]))
        vi = dict(zip(SMALL, refs[2 * ns:3 * ns]))
        outs = refs[3 * ns:7 * ns]
        loss_ref = refs[7 * ns]
        me = me_ref[0]

        def shard_grad(name):
            if name == "b_gate":
                return tot[ROW_BG:ROW_BG + 2, pl.ds(pl.multiple_of(me * LANES, LANES), LANES)]
            if name == "conv_w":
                win = tot[ROW_CW:ROW_CW + KW, pl.ds(pl.multiple_of((me // 2) * LANES, LANES), LANES)]
                return jnp.where(me % 2 == 1, win[:, HD:LANES], win[:, 0:HD])
            row = {"norm1_w": ROW_N1, "norm2_w": ROW_N2, "q_norm_w": ROW_QN, "k_norm_w": ROW_KN,
                   "conv_b": ROW_CB, "conv_ln_w": ROW_LW, "conv_ln_b": ROW_LB}[name]
            return tot[row:row + 1, 0:wi[name].shape[1]]

        for i, name in enumerate(SMALL):
            gr = shard_grad(name)
            delta, m2, v2 = _adamw(wi[name][...], gr, mi[name][...], vi[name][...])
            outs[4 * i][...] = gr
            outs[4 * i + 1][...] = delta
            outs[4 * i + 2][...] = m2
            outs[4 * i + 3][...] = v2
        loss_ref[...] = tot[ROW_KN:ROW_KN + 1, LANES:2 * LANES]

    out_shape = []
    for name in SMALL:
        out_shape += [jax.ShapeDtypeStruct(w[name].shape, F32)] * 4
    out_shape.append(jax.ShapeDtypeStruct((1, LANES), F32))
    args = [tot] + [w[k] for k in SMALL] + [m[k] for k in SMALL] + [v[k] for k in SMALL]
    grid_spec = pltpu.PrefetchScalarGridSpec(
        num_scalar_prefetch=1, grid=(1,), in_specs=[VMEM] * len(args), out_specs=[VMEM] * len(out_shape))
    res = pl.pallas_call(body, name="small_adam", grid_spec=grid_spec, out_shape=out_shape)(me, *args)
    out = {name: tuple(res[4 * i:4 * i + 4]) for i, name in enumerate(SMALL)}
    return out, res[4 * ns][0, 0]


MATS = ("w_in", "w_o_attn", "w_pw_conv", "w_out", "w_ffn_in", "w_ffn_out")
TRANSPOSED = ("w_in", "w_ffn_in")
WEIGHTS = ("norm1_w", "w_in", "b_gate", "q_norm_w", "k_norm_w", "w_o_attn", "conv_w", "conv_b", "conv_ln_w",
           "conv_ln_b", "w_pw_conv", "w_out", "norm2_w", "w_ffn_in", "w_ffn_out")


def _blocks_to_cols(blocks):
    n, R, C = blocks.shape
    return blocks.transpose(1, 0, 2).reshape(R, n * C)


def kernel(x, positions, norm1_w, w_in, b_gate, q_norm_w, k_norm_w, w_o_attn, conv_w, conv_b, conv_ln_w, conv_ln_b, w_pw_conv, w_out, norm2_w, w_ffn_in, w_ffn_out, loss_target, m_norm1_w, m_w_in, m_b_gate, m_q_norm_w, m_k_norm_w, m_w_o_attn, m_conv_w, m_conv_b, m_conv_ln_w, m_conv_ln_b, m_w_pw_conv, m_w_out, m_norm2_w, m_w_ffn_in, m_w_ffn_out, v_norm1_w, v_w_in, v_b_gate, v_q_norm_w, v_k_norm_w, v_w_o_attn, v_conv_w, v_conv_b, v_conv_ln_w, v_conv_ln_b, v_w_pw_conv, v_w_out, v_norm2_w, v_w_ffn_in, v_w_ffn_out):
    w = dict(norm1_w=norm1_w, w_in=w_in, b_gate=b_gate, q_norm_w=q_norm_w, k_norm_w=k_norm_w, w_o_attn=w_o_attn,
             conv_w=conv_w, conv_b=conv_b, conv_ln_w=conv_ln_w, conv_ln_b=conv_ln_b, w_pw_conv=w_pw_conv,
             w_out=w_out, norm2_w=norm2_w, w_ffn_in=w_ffn_in, w_ffn_out=w_ffn_out)
    m = dict(norm1_w=m_norm1_w, w_in=m_w_in, b_gate=m_b_gate, q_norm_w=m_q_norm_w, k_norm_w=m_k_norm_w,
             w_o_attn=m_w_o_attn, conv_w=m_conv_w, conv_b=m_conv_b, conv_ln_w=m_conv_ln_w,
             conv_ln_b=m_conv_ln_b, w_pw_conv=m_w_pw_conv, w_out=m_w_out, norm2_w=m_norm2_w,
             w_ffn_in=m_w_ffn_in, w_ffn_out=m_w_ffn_out)
    v = dict(norm1_w=v_norm1_w, w_in=v_w_in, b_gate=v_b_gate, q_norm_w=v_q_norm_w, k_norm_w=v_k_norm_w,
             w_o_attn=v_w_o_attn, conv_w=v_conv_w, conv_b=v_conv_b, conv_ln_w=v_conv_ln_w,
             conv_ln_b=v_conv_ln_b, w_pw_conv=v_w_pw_conv, w_out=v_w_out, norm2_w=v_norm2_w,
             w_ffn_in=v_w_ffn_in, w_ffn_out=v_w_ffn_out)
    def two_d(t):
        t = {k: (a[0] if a.ndim == 3 else a) for k, a in t.items()}
        return {k: (a.T if k in TRANSPOSED else a) for k, a in t.items()}

    w, m, v = two_d(w), two_d(m), two_d(v)

    x2, target = x[0], loss_target[0]
    c_idx = lax.axis_index("c").astype(jnp.int32)
    chip_idx = (2 * lax.axis_index("x") + lax.axis_index("y")).astype(jnp.int32)
    qw2 = jnp.tile(w["q_norm_w"], (1, 2))
    kw2 = jnp.tile(w["k_norm_w"], (1, 2))

    ax, ay = lax.axis_index("x"), lax.axis_index("y")
    chip_order = jnp.stack([2 * ax + ay, 2 * (1 - ax) + ay, 2 * ax + 1 - ay, 2 * (1 - ax) + 1 - ay]).astype(jnp.int32)
    h_t, proj, w_in_blocks, tabs = in_proj_gather(x2, w["norm1_w"], w["w_in"], chip_order, positions.reshape(S, 1))
    w_in_t = w_in_blocks.reshape(INW, D)
    (attn, lse), ((w_ffn_in_blocks,), (w_out_blocks,), (w_o_blocks,), (w_pw_blocks,), (bg_blocks,), (cw_blocks,)) = attn_fwd(
        proj, tabs, qw2, kw2, sides=(ag_blocks_relay(w["w_ffn_in"], BF16), ag_blocks_relay(w["w_out"], BF16),
                                     ag_blocks_relay(w["w_o_attn"], BF16, transpose=True),
                                     ag_blocks_relay(w["w_pw_conv"], BF16, transpose=True),
                                     ag_blocks(w["b_gate"], F32), ag_blocks(w["conv_w"], F32)))
    w_ffn_in_t = w_ffn_in_blocks.reshape(2 * FF, D)
    w_out_f = w_out_blocks.reshape(D, D)
    w_o_t, w_pw_t = w_o_blocks.reshape(D, CC), w_pw_blocks.reshape(D, CC)
    b_gate_f, conv_w_f = _blocks_to_cols(bg_blocks), _blocks_to_cols(cw_blocks)
    cpre, u3 = conv_fwd(proj, conv_w_f, w["conv_b"], w["conv_ln_w"], w["conv_ln_b"])
    x1, z, ya, yb = mix_out(x2, proj, b_gate_f, attn, u3, w_o_t, w_pw_t, w_out_f)
    (h2, gu, f), ((w_ffn_out_blocks,),) = ffn_in(x1, w["norm2_w"], w_ffn_in_t, sides=(ag_blocks_relay(w["w_ffn_out"], BF16),))
    w_ffn_out_f = w_ffn_out_blocks.reshape(FF, D)
    dy, dyb, sq = ffn_out_loss(x1, f, w_ffn_out_f, target)

    g = {}
    g_ffn_out = mm_tn("gw_ffn_out", f, dyb, FF // 2, D).reshape(NDEV, FF // NDEV, D)
    (d_gu, d_x1, d_x1b, g["norm2_w"]), ((ra_ffn_out,),) = ffn_bwd(
        dy, dyb, gu, x1, w["norm2_w"], w_ffn_in_t, w_ffn_out_f, sides=(rs_to_sibling([g_ffn_out]),))
    pb_ffn_out, own_ffn_out = chip_sum("chip_sum_w_ffn_out", g_ffn_out, ra_ffn_out, c_idx, chip_idx)
    g_ffn_in = mm_tn("gw_ffn_in", d_gu, h2, FF // 2, D).reshape(NDEV, 2 * FF // NDEV, D)
    g_out = mm_tn("gw_out", z, d_x1b, D // 2, D).reshape(NDEV, D // NDEV, D)
    (d_ya, d_yb, d_gl, d_attn, d_u3, g["b_gate"]), ((ra_ffn_in,),) = out_bwd(
        d_x1b, proj, b_gate_f, ya, yb, w_o_t, w_pw_t, w_out_f, sides=(rs_to_sibling([g_ffn_in]),))
    pb_ffn_in, own_ffn_in = chip_sum("chip_sum_w_ffn_in", g_ffn_in, ra_ffn_in, c_idx, chip_idx)
    g_w_o = mm_tn("gw_o_attn", d_ya, attn, D // 2, CC).reshape(NDEV, D // NDEV, CC)
    g_w_pw = mm_tn("gw_pw_conv", d_yb, u3, D // 2, CC).reshape(NDEV, D // NDEV, CC)
    (d_conv, g["conv_w"], g["conv_b"], g["conv_ln_w"], g["conv_ln_b"]), ((ra_out, ra_w_o, ra_w_pw),) = conv_bwd(
        proj, cpre, d_u3, conv_w_f, conv_w_f[::-1], w["conv_ln_w"], w["conv_ln_b"],
        sides=(rs_to_sibling([g_out, g_w_o, g_w_pw]),))
    pb_out, own_out = chip_sum("chip_sum_w_out", g_out, ra_out, c_idx, chip_idx)
    pb_w_o, own_w_o = chip_sum("chip_sum_w_o_attn", g_w_o, ra_w_o, c_idx, chip_idx)
    pb_w_pw, own_w_pw = chip_sum("chip_sum_w_pw_conv", g_w_pw, ra_w_pw, c_idx, chip_idx)
    (d_q, d_k, d_v, gqw, gkw), ((rb_ffn_out, rb_ffn_in, rb_out, rb_w_o, rb_w_pw),) = attn_bwd(
        proj, tabs, qw2, kw2, d_attn, attn, lse,
        sides=(rs_to_chips([pb_ffn_out, pb_ffn_in, pb_out, pb_w_o, pb_w_pw]),))
    g["q_norm_w"] = gqw[0:1, 0:HD] + gqw[0:1, HD:LANES]
    g["k_norm_w"] = gkw[0:1, 0:HD] + gkw[0:1, HD:LANES]
    d_segs = (d_q, d_k, d_v, d_conv, d_gl)
    parts, to_sibling, to_chips, owns, from_chips = [], None, None, [], []
    for k, hw in enumerate(GW_IN_SPLIT):
        sides = tuple(s for s in (to_chips, to_sibling) if s is not None)
        part = gw_in_t("gw_in_%d" % k, h_t, d_segs, sum(GW_IN_SPLIT[:k]), hw, sides=sides)
        part, outs = part if sides else (part, [])
        outs = list(outs)
        if to_chips is not None:
            from_chips.append(outs.pop(0)[0])
        if to_sibling is not None:
            pb, own = chip_sum("chip_sum_w_in_%d" % (k - 1), parts[-1], outs.pop(0)[0], c_idx, chip_idx)
            owns.append(own)
            to_chips = rs_to_chips_combined(pb)
        else:
            to_chips = None
        parts.append(part.reshape(NDEV, INW // NDEV, hw))
        to_sibling = rs_to_sibling([parts[-1]])
    (grad_x, g["norm1_w"]), ((rb_prev,), (ra_last,)) = in_bwd(
        d_q, d_k, d_v, d_conv, d_gl, w_in_t, x2, d_x1, w["norm1_w"], sides=(to_chips, to_sibling))
    from_chips.append(rb_prev)
    pb, own = chip_sum("chip_sum_w_in_%d" % (len(GW_IN_SPLIT) - 1), parts[-1], ra_last, c_idx, chip_idx)
    owns.append(own)
    small_sums, ((rb_last,),) = small_sync(g, sq, sides=(rs_to_chips_combined(pb),))
    small, loss = small_adam(small_sums, w, m, v, (4 * ax + 2 * ay + c_idx).astype(jnp.int32).reshape(1))
    from_chips.append(rb_last)

    res = {
        "w_in": shard_adam("adam_w_in", owns, from_chips, w["w_in"], m["w_in"], v["w_in"]),
        "w_ffn_in": shard_adam("adam_w_ffn_in", [own_ffn_in], [rb_ffn_in], w["w_ffn_in"], m["w_ffn_in"], v["w_ffn_in"]),
        "w_o_attn": shard_adam("adam_w_o_attn", [own_w_o], [rb_w_o], w["w_o_attn"], m["w_o_attn"], v["w_o_attn"], io_t=True),
        "w_pw_conv": shard_adam("adam_w_pw_conv", [own_w_pw], [rb_w_pw],
                                w["w_pw_conv"], m["w_pw_conv"], v["w_pw_conv"], io_t=True),
        "w_out": shard_adam("adam_w_out", [own_out], [rb_out], w["w_out"], m["w_out"], v["w_out"]),
        "w_ffn_out": shard_adam("adam_w_ffn_out", [own_ffn_out], [rb_ffn_out],
                                w["w_ffn_out"], m["w_ffn_out"], v["w_ffn_out"]),
    }
    res = {k: tuple(a.T if k in TRANSPOSED else a for a in r) for k, r in res.items()}
    res.update(small)

    def shaped(name, a):
        return a.reshape((1,) + a.shape) if name in MATS or name in ("b_gate", "conv_w") else a

    outs = [loss, grad_x.reshape(1, S, D)]
    for i in range(4):
        outs += [shaped(k, res[k][i]) for k in WEIGHTS]
    return tuple(outs)
```

```python
import functools
from typing import Callable, NamedTuple, Optional

import numpy as np
import jax
import jax.numpy as jnp
from jax import lax
from jax.experimental import pallas as pl
from jax.experimental.pallas import tpu as pltpu

F32 = jnp.float32
BF16 = jnp.bfloat16

S = 2048
D = 1024
HD = 64
QKV = 1536
CC = 512
KW = 31
FF = 2816
INW = 7680
OFF_Q, OFF_K, OFF_V, OFF_CA, OFF_CB, OFF_GA, OFF_GB = 0, 1536, 3072, 4608, 5120, 5632, 6656
DILATIONS = (1, 4, 16)
HALF_SPAN = 64
EPS = 1e-6
NEG_INF = -1e30
ROPE_THETA = 500000.0
ROT_DIM = 16

ADAM_LR = 0.001
ADAM_B1 = 0.9
ADAM_B2 = 0.999
ADAM_EPS = 1e-08
ADAM_WD = 0.01
ADAM_STEP = 10

NDEV = 8
LANES = 128
TM = 256
IN_PROJ_TM = 512
TQ = 128
VMEM_LIMIT = 56 * 1024 * 1024
MESH = pl.DeviceIdType.MESH


def _cp(**kw):
    return pltpu.CompilerParams(vmem_limit_bytes=VMEM_LIMIT, **kw)


def _row(width, col=0, tm=TM):
    return pl.BlockSpec((tm, width), lambda i: (i, col))


PLANE = 512


def _planes(width, tm=TM):
    return pl.BlockSpec((width // PLANE, tm, PLANE), lambda i: (0, i, 0))


def _res(shape):
    nd = len(shape)
    return pl.BlockSpec(shape, lambda *_: (0,) * nd, pipeline_mode=pl.Buffered(1))


def _dot(a, b):
    return jnp.dot(a, b, preferred_element_type=F32)


def _dot_nt(a, b):
    return lax.dot_general(a, b, (((1,), (1,)), ((), ())), preferred_element_type=F32)


def _dot_tn(a, b):
    return lax.dot_general(a, b, (((0,), (0,)), ((), ())), preferred_element_type=F32)


def _sigmoid(x):
    return jax.nn.sigmoid(x)


def _dsilu(x, sg):
    return sg * (1.0 + x * (1.0 - sg))


ANY = pl.BlockSpec(memory_space=pl.ANY)
VMEM = pl.BlockSpec(memory_space=pltpu.VMEM)


class Side(NamedTuple):
    args: tuple
    in_specs: tuple
    out_shape: tuple
    scratch: tuple
    start: Callable
    finish: Callable
    mid: Optional[Callable] = None
    peers: str = ""


BARRIER_IDS = {"s": 0, "dxy": 1, "dsxy": 2, "sxy": 3, "xy": 4}


def _peer_barrier(peers):
    x, y, c = lax.axis_index("x"), lax.axis_index("y"), lax.axis_index("c")
    where = {"s": (x, y, 1 - c), "x": (1 - x, y, c), "y": (x, 1 - y, c), "d": (1 - x, 1 - y, c)}
    barrier = pltpu.get_barrier_semaphore()
    for p in peers:
        pl.semaphore_signal(barrier, inc=1, device_id=where[p], device_id_type=MESH)
    pl.semaphore_wait(barrier, len(peers))


def _call(body, sides=(), *, name, grid, in_specs, out_specs, out_shape, scratch_shapes=(), args, own_comm=False):
    ni, no, ns = len(in_specs), len(out_specs), len(scratch_shapes)
    cnt = [(len(s.args), len(s.out_shape), len(s.scratch)) for s in sides]
    peers = "".join(sorted(set("".join(s.peers for s in sides))))
    if own_comm or not sides or any(not s.peers for s in sides):
        peers = ""

    def take(refs, pos, n):
        return refs[pos:pos + n], pos + n

    def full(*refs):
        m_in, pos = take(refs, 0, ni)
        s_in = []
        for a, _, _ in cnt:
            r, pos = take(refs, pos, a)
            s_in.append(r)
        m_out, pos = take(refs, pos, no)
        s_out = []
        for _, o, _ in cnt:
            r, pos = take(refs, pos, o)
            s_out.append(r)
        m_scr, pos = take(refs, pos, ns)
        s_scr = []
        for _, _, c in cnt:
            r, pos = take(refs, pos, c)
            s_scr.append(r)
        if sides:
            first = functools.reduce(jnp.logical_and, [pl.program_id(d) == 0 for d in range(len(grid))])
            last = functools.reduce(jnp.logical_and, [pl.program_id(d) == g - 1 for d, g in enumerate(grid)])

            @pl.when(first)
            def _():
                if peers:
                    _peer_barrier(peers)
                for s, a, o, c in zip(sides, s_in, s_out, s_scr):
                    s.start(a, o, c)

            steps = int(np.prod(grid))
            mid_step = (2 * steps) // 3
            if steps > 1 and any(s.mid is not None for s in sides):
                step = functools.reduce(lambda acc, d: acc * grid[d] + pl.program_id(d), range(len(grid)), 0)

                @pl.when(step == mid_step)
                def _():
                    for s, a, o, c in zip(sides, s_in, s_out, s_scr):
                        if s.mid is not None:
                            s.mid(a, o, c)

        body(*m_in, *m_out, *m_scr)
        if sides:
            @pl.when(last)
            def _():
                for s, a, o, c in zip(sides, s_in, s_out, s_scr):
                    if s.mid is not None and steps == 1:
                        s.mid(a, o, c)
                    s.finish(a, o, c)

    res = pl.pallas_call(
        full, name=name, grid=grid,
        in_specs=list(in_specs) + [sp for s in sides for sp in s.in_specs],
        out_specs=list(out_specs) + [ANY for s in sides for _ in s.out_shape],
        out_shape=list(out_shape) + [o for s in sides for o in s.out_shape],
        scratch_shapes=list(scratch_shapes) + [c for s in sides for c in s.scratch],
        compiler_params=_cp(dimension_semantics=("arbitrary",) * len(grid),
                            **({"collective_id": BARRIER_IDS[peers]} if peers else {})),
    )(*args, *[a for s in sides for a in s.args])
    res = list(res)
    if not sides:
        return res
    outs, pos = take(res, 0, no)
    side_outs = []
    for _, o, _ in cnt:
        r, pos = take(res, pos, o)
        side_outs.append(r)
    return outs, side_outs


def _inv_freq_lanes():
    inv = np.float32(ROPE_THETA) ** (-np.arange(0, ROT_DIM, 2, dtype=np.float32) / np.float32(ROT_DIM))
    lane = np.arange(LANES) % HD
    out = np.where(lane < ROT_DIM, inv[lane % (ROT_DIM // 2)], 0.0).astype(np.float32)
    return jnp.asarray(out.reshape(1, LANES))


def _rope_tables(pos, inv_freq):
    ang = pos.astype(F32) * inv_freq
    lane = lax.broadcasted_iota(jnp.int32, ang.shape, 1) % HD
    cs = jnp.cos(ang)
    sn = jnp.sin(ang)
    return (jnp.where(lane < ROT_DIM, cs, 1.0), jnp.where(lane < ROT_DIM // 2, -sn, 0.0),
            jnp.where(lane < ROT_DIM // 2, 0.0, jnp.where(lane < ROT_DIM, sn, 0.0)))


def _rope(v, c, s1, s2):
    return v * c + pltpu.roll(v, LANES - 8, axis=1) * s1 + pltpu.roll(v, 8, axis=1) * s2


def _rope_t(d, c, s1, s2):
    return d * c - pltpu.roll(d, LANES - 8, axis=1) * s1 - pltpu.roll(d, 8, axis=1) * s2


def _head_mat():
    r = lax.broadcasted_iota(jnp.int32, (LANES, LANES), 0) // HD
    c = lax.broadcasted_iota(jnp.int32, (LANES, LANES), 1) // HD
    return jnp.where(r == c, 1.0 / HD, 0.0).astype(BF16)


def _head_mean(t, e):
    hi = t.astype(BF16)
    rest = (t - hi.astype(F32)).astype(BF16)
    return _dot(hi, e) + _dot(rest, e)


def in_proj_gather(x, norm_w, shard_t, chip_order, pos_col):
    R = INW // NDEV
    tm = IN_PROJ_TM
    half, nt = R // 2, S // tm

    def body(ord_ref, x_ref, nw_ref, sh_ref, pos_ref, f_ref, ht_ref, p_ref, wfull_ref, c_ref, s1_ref, s2_ref,
             wt, hs, send, recv, loc):
        kk, i = pl.program_id(0), pl.program_id(1)
        x, y, c, _ = _place()
        me, flip = 4 * x + 2 * y + c, 1 - 2 * c
        here, sib, xn, yn = (x, y, c), (x, y, 1 - c), (1 - x, y, c), (x, 1 - y, c)
        b_xn, b_yn, b_dg = 4 * (1 - x) + 2 * y + c, 4 * x + 2 * (1 - y) + c, 4 * (1 - x) + 2 * (1 - y) + c

        def cp(k, block, to, rows=None):
            dst = wt.at[block] if rows is None else wt.at[block, pl.ds(rows * half, half), :]
            return _remote(dst, dst, send, recv, k, to)

        def sends():
            return [cp(0, me, sib), cp(1, me, xn), cp(2, me, yn), cp(3, b_xn, sib), cp(4, b_yn, sib),
                    cp(5, b_xn, yn, rows=0), cp(6, b_yn, xn, rows=1), cp(7, b_dg, sib, rows=0), cp(8, b_dg, sib, rows=1)]

        def keep(j, blk0):
            pair = pl.ds(pl.multiple_of(blk0, 2), 2)
            return pltpu.make_async_copy(wt.at[pair], wfull_ref.at[pair], loc.at[j])

        @pl.when((kk == 0) & (i == 0))
        def _():
            _peer_barrier("sxy")
            _cast_rows(wt.at[me], sh_ref)
            for s_ in sends()[0:3]:
                s_.start()

            def tables(j, _):
                chunk = pl.ds(pl.multiple_of(j * TM, TM), TM)
                c_ref[chunk, :], s1_ref[chunk, :], s2_ref[chunk, :] = _rope_tables(pos_ref[chunk, :], f_ref[...])
                return 0

            lax.fori_loop(0, S // TM, tables, 0)
            cp(0, me + flip, here).wait_recv()
            keep(0, me - c).start()

        @pl.when((kk == 1) & (i == 0))
        def _():
            cp(1, b_xn, here).wait_recv()
            sends()[5].start()
            sends()[3].start()
            cp(2, b_yn, here).wait_recv()
            sends()[6].start()
            sends()[4].start()
            cp(3, b_xn + flip, here).wait_recv()
            keep(1, b_xn - c).start()

        @pl.when((kk == 2) & (i == 0))
        def _():
            cp(4, b_yn + flip, here).wait_recv()
            keep(2, b_yn - c).start()

        @pl.when((kk == 3) & (i == 0))
        def _():
            cp(5, b_dg, here, rows=0).wait_recv()
            sends()[7].start()
            cp(6, b_dg, here, rows=1).wait_recv()
            sends()[8].start()
            cp(7, b_dg + flip, here, rows=0).wait_recv()
            cp(8, b_dg + flip, here, rows=1).wait_recv()
            keep(3, b_dg - c).start()

        rows = pl.ds(pl.multiple_of(i * tm, tm), tm)

        @pl.when(kk == 0)
        def _():
            xv = x_ref[...]
            r = lax.rsqrt(jnp.mean(xv * xv, axis=-1, keepdims=True) + EPS)
            hf = xv * r * nw_ref[...]
            ht_ref[...] = hf.T.astype(BF16)
            hs[rows, :] = hf.astype(BF16)

        h = hs[rows, :]
        chip = ord_ref[kk]
        for cc in range(2):
            p_ref[:, cc * R:(cc + 1) * R] = _dot_nt(h, wt[2 * chip + cc])

        @pl.when((kk == 3) & (i == nt - 1))
        def _():
            for s_ in sends():
                s_.wait_send()
            for j, blk in enumerate((me, b_xn, b_yn, b_dg)):
                keep(j, blk - c).wait()

    def first_pass(kk, i):
        return jnp.where(kk == 0, i, nt - 1)

    grid_spec = pltpu.PrefetchScalarGridSpec(
        num_scalar_prefetch=1, grid=(4, nt),
        in_specs=[pl.BlockSpec((tm, D), lambda kk, i, o: (first_pass(kk, i), 0)),
                  pl.BlockSpec((1, D), lambda kk, i, o: (0, 0)), VMEM, VMEM,
                  pl.BlockSpec((1, LANES), lambda kk, i, o: (0, 0))],
        out_specs=[pl.BlockSpec((D, tm), lambda kk, i, o: (0, first_pass(kk, i))),
                   pl.BlockSpec((tm, 2 * R), lambda kk, i, o: (i, o[kk])), ANY]
        + [pl.BlockSpec((S, LANES), lambda kk, i, o: (0, 0))] * 3,
        scratch_shapes=[pltpu.VMEM((NDEV, R, D), BF16), pltpu.VMEM((S, D), BF16), _sems(9), _sems(9), _sems(4)])
    res = pl.pallas_call(
        body, name="in_proj_gather", grid_spec=grid_spec,
        out_shape=[jax.ShapeDtypeStruct((D, S), BF16), jax.ShapeDtypeStruct((S, INW), F32),
                   jax.ShapeDtypeStruct((NDEV, R, D), BF16)] + [jax.ShapeDtypeStruct((S, LANES), F32)] * 3,
        compiler_params=_cp(dimension_semantics=("arbitrary", "arbitrary"), collective_id=BARRIER_IDS["sxy"]),
    )(chip_order, x, norm_w, shard_t, pos_col, _inv_freq_lanes())
    return res[0], res[1], res[2], tuple(res[3:])


def _qk_specs():
    nb = QKV // LANES
    return [pl.BlockSpec((S, LANES), functools.partial(lambda hp, g, o: (0, o + g * 4 + hp), o=o))
            for o in (OFF_Q // LANES, OFF_K // LANES, OFF_V // LANES)]


def _tab_specs():
    return [pl.BlockSpec((S, LANES), lambda hp, g: (0, 0), pipeline_mode=pl.Buffered(1))] * 3


def _vec_spec():
    return pl.BlockSpec((1, LANES), lambda hp, g: (0, 0))


def _sub_rows(r, d, start, n):
    if d == 1:
        return pl.ds(start, n)
    return pl.ds(r + d * start, n, stride=d)


def _band_window(i, L):
    W = min(TQ + 2 * HALF_SPAN, L)
    q0 = pl.multiple_of(i * TQ, TQ)
    k0 = pl.multiple_of(jnp.clip(q0 - HALF_SPAN, 0, L - W), HALF_SPAN)
    qpos = q0 + (lax.broadcasted_iota(jnp.int32, (2 * TQ, W), 0) & (TQ - 1))
    kpos = k0 + lax.broadcasted_iota(jnp.int32, (2 * TQ, W), 1)
    valid = jnp.abs(qpos - kpos) <= HALF_SPAN
    return W, q0, k0, valid


def _stack_heads(t, lo):
    z = jnp.zeros_like(t)
    return jnp.concatenate([jnp.where(lo, t, z), jnp.where(lo, z, t)], axis=0)


def _unstack_heads(t2, lo):
    return jnp.where(lo, t2[0:TQ], t2[TQ:2 * TQ])


CHAINS = 8


def _interleave(d):
    ru = min(d, CHAINS)
    return ru, min(CHAINS // ru, S // d // TQ)


def _for_blocks(n, fn):
    if n == 1:
        fn(0)
    else:
        def it(j, _):
            fn(j)
            return 0
        lax.fori_loop(0, n, it, 0)


def attn_fwd(proj, tabs, qw2, kw2, sides=()):
    CH = 256

    def body(q_ref, k_ref, v_ref, c_ref, s1_ref, s2_ref, qw_ref, kw_ref, at_ref, ls_ref,
             qs, ks, vs, osub, lsub, onat, lnat, qn, kn):
        g = pl.program_id(1)
        lo = lax.broadcasted_iota(jnp.int32, (1, LANES), 1) < HD
        e = _head_mat()

        def prep(i, _):
            rows = pl.ds(pl.multiple_of(i * CH, CH), CH)
            c, s1, s2 = c_ref[rows, :], s1_ref[rows, :], s2_ref[rows, :]
            for t_ref, w_ref, out, scale in ((q_ref, qw_ref, qn, HD ** -0.5), (k_ref, kw_ref, kn, 1.0)):
                t = t_ref[rows, :]
                r = lax.rsqrt(_head_mean(t * t, e) + EPS)
                out[rows, :] = _rope(t * r * w_ref[...], c, s1, s2) * scale
            return 0

        lax.fori_loop(0, S // CH, prep, 0, unroll=4)

        def group(gi, d):
            L = S // d

            ru, nb = _interleave(d)

            def stage(r, off):
                for c0 in range(0, L, CH):
                    n = min(CH, L)
                    rows = _sub_rows(r, d, c0, n)
                    dst = pl.ds(off + c0, n)
                    qs[dst, :] = qn[rows, :].astype(BF16)
                    ks[dst, :] = kn[rows, :].astype(BF16)
                    vs[dst, :] = v_ref[rows, :].astype(BF16)

            def one(off, i):
                W, q0, k0, valid = _band_window(i, L)
                q2 = _stack_heads(qs[pl.ds(off + q0, TQ), :], lo)
                sc = jnp.where(valid, _dot_nt(q2, ks[pl.ds(off + k0, W), :]), NEG_INF)
                m = jnp.max(sc, axis=-1, keepdims=True)
                p = jnp.exp(sc - m)
                den = jnp.sum(p, axis=-1, keepdims=True)
                o2 = _dot(p.astype(BF16), vs[pl.ds(off + k0, W), :]) / den
                l2 = jnp.broadcast_to(m + jnp.log(den), (2 * TQ, LANES))
                osub[pl.ds(off + q0, TQ), :] = _unstack_heads(o2, lo)
                lsub[pl.ds(off + q0, TQ), :] = _unstack_heads(l2, lo)

            def unstage(r, off):
                for c0 in range(0, L, CH):
                    n = min(CH, L)
                    rows = _sub_rows(r, d, c0, n)
                    onat[gi, rows, :] = osub[pl.ds(off + c0, n), :]
                    lnat[gi, rows, :] = lsub[pl.ds(off + c0, n), :]

            def step(t, _):
                for u in range(ru):
                    stage(t * ru + u, u * L)
                _for_blocks(L // TQ // nb, lambda j: [one(u * L, j * nb + b) for u in range(ru) for b in range(nb)])
                for u in range(ru):
                    unstage(t * ru + u, u * L)
                return 0

            lax.fori_loop(0, d // ru, step, 0)

        for gi, d in enumerate(DILATIONS):
            pl.when(g == gi)(functools.partial(group, gi, d))

        @pl.when(g == len(DILATIONS) - 1)
        def _():
            def mix(i, _):
                rows = pl.ds(pl.multiple_of(i * CH, CH), CH)
                l0, l1, l2 = lnat[0, rows, :], lnat[1, rows, :], lnat[2, rows, :]
                m = jnp.maximum(jnp.maximum(l0, l1), l2)
                e0, e1, e2 = jnp.exp(l0 - m), jnp.exp(l1 - m), jnp.exp(l2 - m)
                den = e0 + e1 + e2
                a = (e0 * onat[0, rows, :] + e1 * onat[1, rows, :] + e2 * onat[2, rows, :]) / den
                at_ref[rows, :] = a.astype(BF16)
                ls_ref[rows, :] = m + jnp.log(den)
                return 0

            lax.fori_loop(0, S // CH, mix, 0)

    out_spec = pl.BlockSpec((S, LANES), lambda hp, g: (0, hp))
    return _call(
        body, sides, name="attn_fwd", grid=(4, 3),
        in_specs=_qk_specs() + _tab_specs() + [_vec_spec(), _vec_spec()],
        out_specs=[out_spec, out_spec],
        out_shape=[jax.ShapeDtypeStruct((S, CC), BF16), jax.ShapeDtypeStruct((S, CC), F32)],
        scratch_shapes=[pltpu.VMEM((S, LANES), BF16)] * 3 + [pltpu.VMEM((S, LANES), F32)] * 2
        + [pltpu.VMEM((3, S, LANES), F32)] * 2 + [pltpu.VMEM((S, LANES), F32)] * 2,
        args=(proj, proj, proj, *tabs, qw2, kw2))


def attn_bwd(proj, tabs, qw2, kw2, d_attn, attn, lse, sides=()):
    CH = 256

    def body(q_ref, k_ref, v_ref, c_ref, s1_ref, s2_ref, qw_ref, kw_ref, do_ref, at_ref, ls_ref,
             dq_ref, dk_ref, dv_ref, gqw_ref, gkw_ref,
             qs, ks, vs, dos, dsub, lsub, dqs, dks, dvs, dnat, qx, kx, dvn, tnq, tnk, rrq, rrk):
        hp, g = pl.program_id(0), pl.program_id(1)
        lo = lax.broadcasted_iota(jnp.int32, (1, LANES), 1) < HD
        e = _head_mat()
        both = ((q_ref, qw_ref, qx, tnq, rrq, HD ** -0.5), (k_ref, kw_ref, kx, tnk, rrk, 1.0))

        @pl.when((hp == 0) & (g == 0))
        def _():
            gqw_ref[...] = jnp.zeros_like(gqw_ref)
            gkw_ref[...] = jnp.zeros_like(gkw_ref)

        def prep(i, _):
            rows = pl.ds(pl.multiple_of(i * CH, CH), CH)
            dnat[rows, :] = _head_mean(do_ref[rows, :] * at_ref[rows, :].astype(F32), e) * float(HD)
            c, s1, s2 = c_ref[rows, :], s1_ref[rows, :], s2_ref[rows, :]
            for t_ref, w_ref, x, tn_s, rr_s, scale in both:
                t = t_ref[rows, :]
                rr = lax.rsqrt(_head_mean(t * t, e) + EPS)
                tn = t * rr
                rr_s[rows, :] = rr
                tn_s[rows, :] = tn
                x[rows, :] = _rope(tn * w_ref[...], c, s1, s2) * scale
            return 0

        lax.fori_loop(0, S // CH, prep, 0, unroll=4)

        def group(d):
            L = S // d

            ru, nb = _interleave(d)

            def stage(r, off):
                for c0 in range(0, L, CH):
                    n = min(CH, L)
                    rows = _sub_rows(r, d, c0, n)
                    dst = pl.ds(off + c0, n)
                    qs[dst, :] = qx[rows, :].astype(BF16)
                    ks[dst, :] = kx[rows, :].astype(BF16)
                    vs[dst, :] = v_ref[rows, :].astype(BF16)
                    dos[dst, :] = do_ref[rows, :].astype(BF16)
                    dsub[dst, :] = dnat[rows, :]
                    lsub[dst, :] = ls_ref[rows, :]
                    dks[dst, :] = jnp.zeros((n, LANES), F32)
                    dvs[dst, :] = jnp.zeros((n, LANES), F32)

            def one(off, i):
                W, q0, k0, valid = _band_window(i, L)
                qrows, krows = pl.ds(off + q0, TQ), pl.ds(off + k0, W)
                q2 = _stack_heads(qs[qrows, :], lo)
                do2 = _stack_heads(dos[qrows, :], lo)
                kk, vv = ks[krows, :], vs[krows, :]
                lse_b, dd_b = lsub[qrows, :], dsub[qrows, :]
                lse2 = jnp.concatenate([lse_b[:, 0:1], lse_b[:, HD:HD + 1]], axis=0)
                dd2 = jnp.concatenate([dd_b[:, 0:1], dd_b[:, HD:HD + 1]], axis=0)
                sc = jnp.where(valid, _dot_nt(q2, kk), NEG_INF)
                p = jnp.exp(sc - lse2)
                ds = (p * (_dot_nt(do2, vv) - dd2)).astype(BF16)
                dqs[qrows, :] = _unstack_heads(_dot(ds, kk), lo)
                dks[krows, :] = dks[krows, :] + _dot_tn(ds, q2)
                dvs[krows, :] = dvs[krows, :] + _dot_tn(p.astype(BF16), do2)

            def unstage(r, off):
                for c0 in range(0, L, CH):
                    n = min(CH, L)
                    rows = _sub_rows(r, d, c0, n)
                    src = pl.ds(off + c0, n)
                    qx[rows, :] = dqs[src, :]
                    kx[rows, :] = dks[src, :]
                    dvn[rows, :] = dvs[src, :]

            def step(t, _):
                for u in range(ru):
                    stage(t * ru + u, u * L)
                _for_blocks(L // TQ // nb, lambda j: [one(u * L, j * nb + b) for u in range(ru) for b in range(nb)])
                for u in range(ru):
                    unstage(t * ru + u, u * L)
                return 0

            lax.fori_loop(0, d // ru, step, 0)

        for gi, d in enumerate(DILATIONS):
            pl.when(g == gi)(functools.partial(group, d))

        def emit(i, _):
            rows = pl.ds(pl.multiple_of(i * CH, CH), CH)
            c, s1, s2 = c_ref[rows, :], s1_ref[rows, :], s2_ref[rows, :]
            for (_, w_ref, x, tn_s, rr_s, scale), out, gw_ref in zip(both, (dq_ref, dk_ref), (gqw_ref, gkw_ref)):
                tn = tn_s[rows, :]
                dy = _rope_t(x[rows, :] * scale, c, s1, s2)
                gw_ref[0:1, :] = gw_ref[0:1, :] + jnp.sum(dy * tn, axis=0, keepdims=True)
                dtn = dy * w_ref[...]
                out[rows, :] = (rr_s[rows, :] * (dtn - tn * _head_mean(dtn * tn, e))).astype(BF16)
            dv_ref[rows, :] = dvn[rows, :].astype(BF16)
            return 0

        lax.fori_loop(0, S // CH, emit, 0, unroll=4)

    nat_spec = pl.BlockSpec((S, LANES), lambda hp, g: (0, hp))
    out_spec = pl.BlockSpec((None, S, LANES), lambda hp, g: (g, 0, hp))
    acc_spec = pl.BlockSpec((8, LANES), lambda hp, g: (0, 0))
    return _call(
        body, sides, name="attn_bwd", grid=(4, 3),
        in_specs=_qk_specs() + _tab_specs() + [_vec_spec(), _vec_spec(), nat_spec, nat_spec, nat_spec],
        out_specs=[out_spec] * 3 + [acc_spec] * 2,
        out_shape=[jax.ShapeDtypeStruct((QKV // PLANE, S, PLANE), BF16)] * 3 + [jax.ShapeDtypeStruct((8, LANES), F32)] * 2,
        scratch_shapes=[pltpu.VMEM((S, LANES), BF16)] * 4 + [pltpu.VMEM((S, LANES), F32)] * 13,
        args=(proj, proj, proj, *tabs, qw2, kw2, d_attn, attn, lse))


PADR = 16
CT = 128


def _conv_specs():
    return [pl.BlockSpec((S, CC), lambda i: (0, OFF_CA // CC)), pl.BlockSpec((S, CC), lambda i: (0, OFF_CB // CC))]


NCB = CC // LANES


def _pad_zero(pad):
    for cb in range(NCB):
        pad[cb, 0:PADR, :] = jnp.zeros((PADR, LANES), F32)
        pad[cb, PADR + S:PADR + S + PADR, :] = jnp.zeros((PADR, LANES), F32)


def _pad_store(pad, row0, n, val):
    for cb in range(NCB):
        pad[cb, pl.ds(pl.multiple_of(row0 + PADR, 8), n), :] = val[:, cb * LANES:(cb + 1) * LANES]


def _taps(pad_ref, cb, s0, weights):
    acc = jnp.zeros((CT, LANES), F32)
    for k in range(KW):
        acc = acc + weights[k] * pad_ref[cb, pl.ds(s0 + k + 1, CT), :]
    return acc


def conv_fwd(proj, conv_w, conv_b, ln_w, ln_b, sides=()):
    def body(a_ref, b_ref, w_ref, cb_ref, lw_ref, lb_ref, c_ref, u3_ref, upad):
        _pad_zero(upad)

        def glu(i, _):
            rows = pl.ds(pl.multiple_of(i * TM, TM), TM)
            _pad_store(upad, i * TM, TM, a_ref[rows, :] * _sigmoid(b_ref[rows, :]))
            return 0

        lax.fori_loop(0, S // TM, glu, 0)

        def chunk(i, _):
            s0 = pl.multiple_of(i * CT, CT)
            for cb in range(CC // LANES):
                cols = slice(cb * LANES, (cb + 1) * LANES)
                w = [w_ref[k:k + 1, cols] for k in range(KW)]
                c_ref[pl.ds(s0, CT), cols] = _taps(upad, cb, s0, w) + cb_ref[:, cols]
            cv = c_ref[pl.ds(s0, CT), :]
            mu = jnp.mean(cv, axis=-1, keepdims=True)
            xc = cv - mu
            rstd = lax.rsqrt(jnp.mean(xc * xc, axis=-1, keepdims=True) + EPS)
            yl = xc * rstd * lw_ref[...] + lb_ref[...]
            u3_ref[pl.ds(s0, CT), :] = (yl * _sigmoid(yl)).astype(BF16)
            return 0

        lax.fori_loop(0, S // CT, chunk, 0)

    vec = pl.BlockSpec((1, CC), lambda i: (0, 0))
    full = pl.BlockSpec((S, CC), lambda i: (0, 0))
    return _call(
        body, sides, name="conv_fwd", grid=(1,),
        in_specs=_conv_specs() + [pl.BlockSpec((KW, CC), lambda i: (0, 0)), vec, vec, vec],
        out_specs=[full, full],
        out_shape=[jax.ShapeDtypeStruct((S, CC), F32), jax.ShapeDtypeStruct((S, CC), BF16)],
        scratch_shapes=[pltpu.VMEM((NCB, S + 2 * PADR, LANES), F32)],
        args=(proj, proj, conv_w, conv_b, ln_w, ln_b))


def conv_bwd(proj, cpre, d_u3, conv_w, conv_w_rev, ln_w, ln_b, sides=()):
    def body(a_ref, b_ref, c_ref, du3_ref, w_ref, wr_ref, lw_ref, lb_ref,
             dc_ref, gw_ref, gcb_ref, glw_ref, glb_ref, upad, dpad):
        _pad_zero(upad)
        _pad_zero(dpad)
        gw_ref[...] = jnp.zeros_like(gw_ref)

        def ln_bwd(i, carry):
            gcb, glw, glb = carry
            rows = pl.ds(pl.multiple_of(i * TM, TM), TM)
            _pad_store(upad, i * TM, TM, a_ref[rows, :] * _sigmoid(b_ref[rows, :]))
            cv = c_ref[rows, :]
            mu = jnp.mean(cv, axis=-1, keepdims=True)
            xc = cv - mu
            rstd = lax.rsqrt(jnp.mean(xc * xc, axis=-1, keepdims=True) + EPS)
            xh = xc * rstd
            yl = xh * lw_ref[...] + lb_ref[...]
            dyl = du3_ref[rows, :] * _dsilu(yl, _sigmoid(yl))
            dxh = dyl * lw_ref[...]
            dcv = rstd * (dxh - jnp.mean(dxh, axis=-1, keepdims=True)
                          - xh * jnp.mean(dxh * xh, axis=-1, keepdims=True))
            _pad_store(dpad, i * TM, TM, dcv)
            return (gcb + jnp.sum(dcv, axis=0, keepdims=True),
                    glw + jnp.sum(dyl * xh, axis=0, keepdims=True),
                    glb + jnp.sum(dyl, axis=0, keepdims=True))

        z = jnp.zeros((1, CC), F32)
        gcb, glw, glb = lax.fori_loop(0, S // TM, ln_bwd, (z, z, z))
        gcb_ref[...] = gcb
        glw_ref[...] = glw
        glb_ref[...] = glb

        def chunk(i, _):
            s0 = pl.multiple_of(i * CT, CT)
            for cb in range(CC // LANES):
                cols = slice(cb * LANES, (cb + 1) * LANES)
                wr = [wr_ref[k:k + 1, cols] for k in range(KW)]
                du = _taps(dpad, cb, s0, wr)
                dcv = dpad[cb, pl.ds(s0 + PADR, CT), :]
                for k in range(KW):
                    gw_ref[k:k + 1, cols] = gw_ref[k:k + 1, cols] + jnp.sum(
                        upad[cb, pl.ds(s0 + k + 1, CT), :] * dcv, axis=0, keepdims=True)
                av = a_ref[pl.ds(s0, CT), cols]
                sb = _sigmoid(b_ref[pl.ds(s0, CT), cols])
                dc_ref[0, pl.ds(s0, CT), cols] = (du * sb).astype(BF16)
                dc_ref[1, pl.ds(s0, CT), cols] = (du * av * sb * (1.0 - sb)).astype(BF16)
            return 0

        lax.fori_loop(0, S // CT, chunk, 0)

    vec = pl.BlockSpec((1, CC), lambda i: (0, 0))
    full = pl.BlockSpec((S, CC), lambda i: (0, 0))
    wsp = pl.BlockSpec((KW, CC), lambda i: (0, 0))
    return _call(
        body, sides, name="conv_bwd", grid=(1,),
        in_specs=_conv_specs() + [full, full, wsp, wsp, vec, vec],
        out_specs=[pl.BlockSpec((2, S, CC), lambda i: (0, 0, 0)), wsp, vec, vec, vec],
        out_shape=[jax.ShapeDtypeStruct((2, S, CC), BF16), jax.ShapeDtypeStruct((KW, CC), F32)]
        + [jax.ShapeDtypeStruct((1, CC), F32)] * 3,
        scratch_shapes=[pltpu.VMEM((NCB, S + 2 * PADR, LANES), F32)] * 2,
        args=(proj, proj, cpre, d_u3, conv_w, conv_w_rev, ln_w, ln_b))


def _gate_specs():
    return [_row(CC, col=OFF_GA // CC + j) for j in range(4)]


def _gates(g_refs, bg_ref):
    ga = _sigmoid(jnp.concatenate([g_refs[0][...], g_refs[1][...]], axis=1) + bg_ref[0:1, :])
    gb = _sigmoid(jnp.concatenate([g_refs[2][...], g_refs[3][...]], axis=1) + bg_ref[1:2, :])
    return ga, gb


def mix_out(x, proj, b_gate, attn, u3, w_o, w_pw, w_out):
    def body(x_ref, g0, g1, g2, g3, bg_ref, at_ref, u3_ref, wo_ref, wp_ref, wout_ref,
             x1_ref, z_ref, ya_ref, yb_ref):
        ga, gb = _gates((g0, g1, g2, g3), bg_ref)
        ya = _dot_nt(at_ref[...], wo_ref[...])
        yb = _dot_nt(u3_ref[...], wp_ref[...])
        z = (ga * ya + gb * yb).astype(BF16)
        ya_ref[...] = ya.astype(BF16)
        yb_ref[...] = yb.astype(BF16)
        z_ref[...] = z
        x1_ref[...] = x_ref[...] + _dot(z, wout_ref[...])

    return pl.pallas_call(
        body, name="mix_out", grid=(S // TM,),
        in_specs=[_row(D)] + _gate_specs() + [_res((2, D)), _row(CC), _row(CC),
                                              _res((D, CC)), _res((D, CC)), _res((D, D))],
        out_specs=[_row(D)] * 4,
        out_shape=[jax.ShapeDtypeStruct((S, D), F32)] + [jax.ShapeDtypeStruct((S, D), BF16)] * 3,
        compiler_params=_cp(dimension_semantics=("arbitrary",)),
    )(x, proj, proj, proj, proj, b_gate, attn, u3, w_o, w_pw, w_out)


def out_bwd(d_x1b, proj, b_gate, ya, yb, w_o, w_pw, w_out, sides=()):
    def body(dx_ref, g0, g1, g2, g3, bg_ref, ya_ref, yb_ref, wo_ref, wp_ref, wout_ref,
             dya_ref, dyb_ref, dgl_ref, dat_ref, du3_ref, gbg_ref):
        @pl.when(pl.program_id(0) == 0)
        def _():
            gbg_ref[...] = jnp.zeros_like(gbg_ref)

        ga, gb = _gates((g0, g1, g2, g3), bg_ref)
        dz = _dot_nt(dx_ref[...], wout_ref[...])
        dya = (dz * ga).astype(BF16)
        dyb = (dz * gb).astype(BF16)
        dgla = dz * ya_ref[...].astype(F32) * ga * (1.0 - ga)
        dglb = dz * yb_ref[...].astype(F32) * gb * (1.0 - gb)
        dya_ref[...] = dya
        dyb_ref[...] = dyb
        for j in range(2):
            dgl_ref[j] = dgla[:, j * PLANE:(j + 1) * PLANE].astype(BF16)
            dgl_ref[2 + j] = dglb[:, j * PLANE:(j + 1) * PLANE].astype(BF16)
        gbg_ref[0:1, :] = gbg_ref[0:1, :] + jnp.sum(dgla, axis=0, keepdims=True)
        gbg_ref[1:2, :] = gbg_ref[1:2, :] + jnp.sum(dglb, axis=0, keepdims=True)
        dat_ref[...] = _dot(dya, wo_ref[...])
        du3_ref[...] = _dot(dyb, wp_ref[...])

    return _call(
        body, sides, name="out_bwd", grid=(S // TM,),
        in_specs=[_row(D)] + _gate_specs() + [_res((2, D)), _row(D), _row(D),
                                              _res((D, CC)), _res((D, CC)), _res((D, D))],
        out_specs=[_row(D), _row(D), _planes(2 * D), _row(CC), _row(CC), pl.BlockSpec((2, D), lambda i: (0, 0))],
        out_shape=[jax.ShapeDtypeStruct((S, D), BF16)] * 2 + [jax.ShapeDtypeStruct((2 * D // PLANE, S, PLANE), BF16)]
        + [jax.ShapeDtypeStruct((S, CC), F32)] * 2 + [jax.ShapeDtypeStruct((2, D), F32)],
        args=(d_x1b, proj, proj, proj, proj, b_gate, ya, yb, w_o, w_pw, w_out))


def ffn_in(x1, norm_w, w_ffn_in, sides=()):
    half = FF // 2

    def body(x_ref, nw_ref, w_ref, h_ref, gu_ref, f_ref):
        xv = x_ref[...]
        r = lax.rsqrt(jnp.mean(xv * xv, axis=-1, keepdims=True) + EPS)
        h = (xv * r * nw_ref[...]).astype(BF16)
        h_ref[...] = h
        for j in range(2):
            gt = _dot_nt(h, w_ref[j * half:(j + 1) * half, :])
            up = _dot_nt(h, w_ref[FF + j * half:FF + (j + 1) * half, :])
            gu_ref[:, j * half:(j + 1) * half] = gt.astype(BF16)
            gu_ref[:, FF + j * half:FF + (j + 1) * half] = up.astype(BF16)
            f_ref[:, j * half:(j + 1) * half] = (gt * _sigmoid(gt) * up).astype(BF16)

    return _call(
        body, sides, name="ffn_in", grid=(S // TM,),
        in_specs=[_row(D), _res((1, D)), _res((2 * FF, D))],
        out_specs=[_row(D), _row(2 * FF), _row(FF)],
        out_shape=[jax.ShapeDtypeStruct((S, D), BF16), jax.ShapeDtypeStruct((S, 2 * FF), BF16),
                   jax.ShapeDtypeStruct((S, FF), BF16)],
        args=(x1, norm_w, w_ffn_in))


def ffn_out_loss(x1, f, w_ffn_out, target):
    def body(x_ref, f_ref, w_ref, t_ref, dy_ref, dyb_ref, sq_ref):
        @pl.when(pl.program_id(0) == 0)
        def _():
            sq_ref[...] = jnp.zeros_like(sq_ref)

        diff = x_ref[...] + _dot(f_ref[...], w_ref[...]) - t_ref[...]
        dy = diff * (1.0 / D)
        dy_ref[...] = dy
        dyb_ref[...] = dy.astype(BF16)
        sq_ref[...] = sq_ref[...] + jnp.sum((diff * diff).reshape(TM // 8, 8, D), axis=0)

    return pl.pallas_call(
        body, name="ffn_out_loss", grid=(S // TM,),
        in_specs=[_row(D), _row(FF), _res((FF, D)), _row(D)],
        out_specs=[_row(D), _row(D), pl.BlockSpec((8, D), lambda i: (0, 0))],
        out_shape=[jax.ShapeDtypeStruct((S, D), F32), jax.ShapeDtypeStruct((S, D), BF16),
                   jax.ShapeDtypeStruct((8, D), F32)],
        compiler_params=_cp(dimension_semantics=("arbitrary",)),
    )(x1, f, w_ffn_out, target)


def _rms_bwd(xv, nw, dh):
    r = lax.rsqrt(jnp.mean(xv * xv, axis=-1, keepdims=True) + EPS)
    xn = xv * r
    dxn = dh * nw
    dx = r * (dxn - xn * jnp.mean(dxn * xn, axis=-1, keepdims=True))
    return dx, dh * xn


def ffn_bwd(dy, dyb, gu, x1, norm_w, w_ffn_in, w_ffn_out, sides=()):
    def body(dy_ref, dyb_ref, gu_ref, x_ref, nw_ref, wi_ref, wo_ref, dgu_ref, dx_ref, dxb_ref, gn_ref):
        @pl.when(pl.program_id(0) == 0)
        def _():
            gn_ref[...] = jnp.zeros_like(gn_ref)

        df = _dot_nt(dyb_ref[...], wo_ref[...])
        gt = gu_ref[:, 0:FF].astype(F32)
        up = gu_ref[:, FF:2 * FF].astype(F32)
        sg = _sigmoid(gt)
        dgt = (df * up * _dsilu(gt, sg)).astype(BF16)
        dup = (df * gt * sg).astype(BF16)
        dgu_ref[:, 0:FF] = dgt
        dgu_ref[:, FF:2 * FF] = dup
        dh = _dot(dgt, wi_ref[0:FF, :]) + _dot(dup, wi_ref[FF:2 * FF, :])
        dxn, gw = _rms_bwd(x_ref[...], nw_ref[...], dh)
        dx = dy_ref[...] + dxn
        dx_ref[...] = dx
        dxb_ref[...] = dx.astype(BF16)
        gn_ref[...] = gn_ref[...] + jnp.sum(gw, axis=0, keepdims=True)

    return _call(
        body, sides, name="ffn_bwd", grid=(S // TM,),
        in_specs=[_row(D), _row(D), _row(2 * FF), _row(D), _res((1, D)), _res((2 * FF, D)), _res((FF, D))],
        out_specs=[_row(2 * FF), _row(D), _row(D), pl.BlockSpec((1, D), lambda i: (0, 0))],
        out_shape=[jax.ShapeDtypeStruct((S, 2 * FF), BF16), jax.ShapeDtypeStruct((S, D), F32),
                   jax.ShapeDtypeStruct((S, D), BF16), jax.ShapeDtypeStruct((1, D), F32)],
        args=(dy, dyb, gu, x1, norm_w, w_ffn_in, w_ffn_out))


def in_bwd(d_q, d_k, d_v, d_conv, d_gl, w_in, x, d_x1, norm_w, sides=()):
    segs = ((OFF_Q, QKV), (OFF_K, QKV), (OFF_V, QKV), (OFF_CA, 2 * CC), (OFF_GA, 2 * D))

    def body(dq_ref, dk_ref, dv_ref, dc_ref, dg_ref, w_ref, x_ref, dx1_ref, nw_ref, gx_ref, gn_ref):
        @pl.when(pl.program_id(0) == 0)
        def _():
            gn_ref[...] = jnp.zeros_like(gn_ref)

        dh = jnp.zeros((TM, D), F32)
        for ref, (off, width) in zip((dq_ref, dk_ref, dv_ref, dc_ref, dg_ref), segs):
            for j in range(width // PLANE):
                dh = dh + _dot(ref[j], w_ref[off + j * PLANE:off + (j + 1) * PLANE, :])
        dxn, gw = _rms_bwd(x_ref[...], nw_ref[...], dh)
        gx_ref[...] = dx1_ref[...] + dxn
        gn_ref[...] = gn_ref[...] + jnp.sum(gw, axis=0, keepdims=True)

    return _call(
        body, sides, name="in_bwd", grid=(S // TM,),
        in_specs=[_planes(QKV)] * 3 + [_planes(2 * CC), _planes(2 * D), _res((INW, D)), _row(D), _row(D), _res((1, D))],
        out_specs=[_row(D), pl.BlockSpec((1, D), lambda i: (0, 0))],
        out_shape=[jax.ShapeDtypeStruct((S, D), F32), jax.ShapeDtypeStruct((1, D), F32)],
        args=(d_q, d_k, d_v, d_conv, d_gl, w_in, x, d_x1, norm_w))


def mm_tn(name, a, b, tm, tn, sides=()):
    M, N = a.shape[1], b.shape[1]

    def body(a_ref, b_ref, o_ref):
        o_ref[...] = _dot_tn(a_ref[...], b_ref[...])

    res = _call(
        body, sides, name=name, grid=(M // tm, N // tn),
        in_specs=[pl.BlockSpec((S, tm), lambda i, j: (0, i)), pl.BlockSpec((S, tn), lambda i, j: (0, j))],
        out_specs=[pl.BlockSpec((tm, tn), lambda i, j: (i, j))],
        out_shape=[jax.ShapeDtypeStruct((M, N), F32)],
        args=(a, b))
    return (res[0][0], res[1]) if sides else res[0]


GW_IN_TN = PLANE
GW_IN_SPLIT = (768, 256)


def gw_in_t(name, ht, d_segs, col0, hw, sides=()):
    tn = GW_IN_TN
    starts, t0 = [], 0
    for seg in d_segs:
        starts.append(t0)
        t0 += seg.shape[0]
    ntiles = [seg.shape[0] for seg in d_segs]

    def body(h_ref, *refs):
        a_refs, o_ref = refs[:-1], refs[-1]
        n = pl.program_id(0)
        for a_ref, st, nt in zip(a_refs, starts, ntiles):
            @pl.when((n >= st) & (n < st + nt))
            def _(a_ref=a_ref):
                o_ref[...] = _dot(h_ref[...], a_ref[...]).T

    def seg_spec(st, nt):
        return pl.BlockSpec((None, S, tn), lambda n: (jnp.clip(n - st, 0, nt - 1), 0, 0))

    res = _call(
        body, sides, name=name, grid=(INW // tn,),
        in_specs=[pl.BlockSpec((hw, S), lambda n: (col0 // hw, 0))] + [seg_spec(st, nt) for st, nt in zip(starts, ntiles)],
        out_specs=[pl.BlockSpec((tn, hw), lambda n: (n, 0))],
        out_shape=[jax.ShapeDtypeStruct((INW, hw), F32)],
        args=(ht, *d_segs))
    return (res[0][0], res[1]) if sides else res[0]


def _place():
    x, y, c = lax.axis_index("x"), lax.axis_index("y"), lax.axis_index("c")
    chips = [(1 - x, y), (x, 1 - y), (1 - x, 1 - y)]
    return x, y, c, chips


def _sems(n):
    return pltpu.SemaphoreType.DMA((n,))


def _remote(src, dst, send, recv, k, to):
    return pltpu.make_async_remote_copy(src_ref=src, dst_ref=dst, send_sem=send.at[k], recv_sem=recv.at[k],
                                        device_id=to, device_id_type=MESH)


def _cast_rows(dst, src, cols=slice(None)):
    rows = src.shape[0]
    step = next((s for s in (128, 64, 32, 16) if rows % s == 0), rows)
    for r0 in range(0, rows, step):
        dst[r0:r0 + step, cols] = src[r0:r0 + step, :].astype(dst.dtype)


def comm_only(name, sides):
    def body():
        pass

    return _call(body, sides, name=name, grid=(1,), in_specs=[], out_specs=[], out_shape=[], args=())[1]


def ag_blocks(shard, dtype):
    R, W = shard.shape

    def copy(outs, scr, k, block, to, src=None):
        dst = outs[0].at[block]
        return _remote(dst if src is None else src, dst, scr[1], scr[2], k, to)

    def local(outs, scr, me):
        return pltpu.make_async_copy(scr[0], outs[0].at[me], scr[3].at[0])

    def start(ins, outs, scr):
        x, y, c, chips = _place()
        me = 4 * x + 2 * y + c
        _cast_rows(scr[0], ins[0])
        local(outs, scr, me).start()
        copy(outs, scr, 0, me, (x, y, 1 - c), src=scr[0]).start()
        for j, (cx, cy) in enumerate(chips):
            copy(outs, scr, 1 + j, me, (cx, cy, c), src=scr[0]).start()

    def finish(ins, outs, scr):
        x, y, c, chips = _place()
        me, sib = 4 * x + 2 * y + c, (x, y, 1 - c)
        passed = []
        for j, (cx, cy) in enumerate(chips):
            theirs = 4 * cx + 2 * cy + c
            copy(outs, scr, 1 + j, theirs, (x, y, c)).wait_recv()
            fwd = copy(outs, scr, 4 + j, theirs, sib)
            fwd.start()
            passed.append(fwd)
        copy(outs, scr, 0, 4 * x + 2 * y + 1 - c, (x, y, c)).wait_recv()
        for j, (cx, cy) in enumerate(chips):
            copy(outs, scr, 4 + j, 4 * cx + 2 * cy + 1 - c, (x, y, c)).wait_recv()
        copy(outs, scr, 0, me, sib, src=scr[0]).wait_send()
        for j, (cx, cy) in enumerate(chips):
            copy(outs, scr, 1 + j, me, (cx, cy, c), src=scr[0]).wait_send()
        for fwd in passed:
            fwd.wait_send()
        local(outs, scr, me).wait()

    return Side((shard,), (VMEM,), (jax.ShapeDtypeStruct((NDEV, R, W), dtype),),
                (pltpu.VMEM((R, W), dtype), _sems(7), _sems(7), _sems(1)), start, finish, None, "dsxy")


def ag_blocks_relay(shard, dtype, transpose=False):
    R, W = shard.shape[::-1] if transpose else shard.shape
    half = R // 2

    def copy(outs, scr, k, block, to, src=None, rows=None):
        dst = outs[0].at[block] if rows is None else outs[0].at[block, pl.ds(rows * half, half), :]
        return _remote(dst if src is None else src, dst, scr[1], scr[2], k, to)

    def local(outs, scr, me):
        return pltpu.make_async_copy(scr[0], outs[0].at[me], scr[3].at[0])

    def own(outs, scr):
        x, y, c, _ = _place()
        me = 4 * x + 2 * y + c
        return [copy(outs, scr, k, me, to, src=scr[0])
                for k, to in enumerate([(x, y, 1 - c), (1 - x, y, c), (x, 1 - y, c)])]

    def start(ins, outs, scr):
        x, y, c, _ = _place()
        if transpose:
            scr[0][...] = ins[0][...].T.astype(dtype)
        else:
            _cast_rows(scr[0], ins[0])
        local(outs, scr, 4 * x + 2 * y + c).start()
        for cp in own(outs, scr):
            cp.start()

    def passed_on(outs, scr):
        x, y, c, _ = _place()
        sib, xn, yn = (x, y, 1 - c), (1 - x, y, c), (x, 1 - y, c)
        b_xn, b_yn, b_dg = 4 * (1 - x) + 2 * y + c, 4 * x + 2 * (1 - y) + c, 4 * (1 - x) + 2 * (1 - y) + c
        near = [copy(outs, scr, 5, b_xn, yn, rows=0), copy(outs, scr, 3, b_xn, sib),
                copy(outs, scr, 6, b_yn, xn, rows=1), copy(outs, scr, 4, b_yn, sib)]
        far = [copy(outs, scr, 7, b_dg, sib, rows=0), copy(outs, scr, 8, b_dg, sib, rows=1)]
        return (b_xn, b_yn, b_dg), near, far

    def mid(ins, outs, scr):
        x, y, c, _ = _place()
        (b_xn, b_yn, _), near, _ = passed_on(outs, scr)
        copy(outs, scr, 1, b_xn, (x, y, c)).wait_recv()
        near[0].start()
        near[1].start()
        copy(outs, scr, 2, b_yn, (x, y, c)).wait_recv()
        near[2].start()
        near[3].start()

    def finish(ins, outs, scr):
        x, y, c, _ = _place()
        here = (x, y, c)
        (b_xn, b_yn, b_dg), near, far = passed_on(outs, scr)
        copy(outs, scr, 5, b_dg, here, rows=0).wait_recv()
        far[0].start()
        copy(outs, scr, 6, b_dg, here, rows=1).wait_recv()
        far[1].start()
        flip = 1 - 2 * c
        copy(outs, scr, 0, 4 * x + 2 * y + 1 - c, here).wait_recv()
        copy(outs, scr, 3, b_xn + flip, here).wait_recv()
        copy(outs, scr, 4, b_yn + flip, here).wait_recv()
        copy(outs, scr, 7, b_dg + flip, here, rows=0).wait_recv()
        copy(outs, scr, 8, b_dg + flip, here, rows=1).wait_recv()
        for cp in own(outs, scr) + near + far:
            cp.wait_send()
        local(outs, scr, 4 * x + 2 * y + c).wait()

    return Side((shard,), (VMEM,), (jax.ShapeDtypeStruct((NDEV, R, W), dtype),),
                (pltpu.VMEM((R, W), dtype), _sems(9), _sems(9), _sems(1)), start, finish, mid, "sxy")


def copies_side(args, out_shape, n_copies, plan, peers):
    def copies(ins, outs, scr):
        return [_remote(s_, d_, scr[0], scr[1], i, to) for i, (s_, d_, to) in enumerate(plan(ins, outs))]

    def start(ins, outs, scr):
        for cp in copies(ins, outs, scr):
            cp.start()

    def finish(ins, outs, scr):
        for cp in copies(ins, outs, scr):
            cp.wait()

    return Side(tuple(args), (ANY,) * len(args), tuple(out_shape), (_sems(n_copies), _sems(n_copies)),
                start, finish, None, peers)


def rs_to_sibling(grads):
    out_shape = [jax.ShapeDtypeStruct((4,) + g.shape[1:], F32) for g in grads]

    def plan(ins, outs):
        x, y, c, _ = _place()
        return [(g.at[2 * k + 1 - c], r.at[k], (x, y, 1 - c)) for g, r in zip(ins, outs) for k in range(4)]

    return copies_side(grads, out_shape, 4 * len(grads), plan, "s")


def rs_to_chips(parts):
    out_shape = [jax.ShapeDtypeStruct((3,) + p.shape[1:], BF16) for p in parts]

    def plan(ins, outs):
        x, y, c, chips = _place()
        return [(p.at[2 * cx + cy], r.at[j], (cx, cy, c))
                for p, r in zip(ins, outs) for j, (cx, cy) in enumerate(chips)]

    return copies_side(parts, out_shape, 3 * len(parts), plan, "dxy")


def rs_to_chips_combined(part):
    _, R, W = part.shape
    half = R // 2
    top, bot = pl.ds(0, half), pl.ds(half, half)

    def copies(ins, outs, scr):
        p, r = ins[0], outs[0]
        loc_a, loc_b, in_x, in_y, comb_a, comb_b, send, recv, loc = scr
        x, y, c, _ = _place()
        xn, yn = (1 - x, y, c), (x, 1 - y, c)
        k_xn, k_yn, k_dg = 2 * (1 - x) + y, 2 * x + 1 - y, 2 * (1 - x) + 1 - y
        direct = [_remote(p.at[k_xn, top, :], r.at[0, top, :], send, recv, 0, xn),
                  _remote(p.at[k_yn, bot, :], r.at[1, bot, :], send, recv, 1, yn),
                  _remote(p.at[k_dg, top, :], in_x, send, recv, 2, xn),
                  _remote(p.at[k_dg, bot, :], in_y, send, recv, 3, yn)]
        combined = [_remote(comb_a, r.at[1, top, :], send, recv, 4, yn),
                    _remote(comb_b, r.at[0, bot, :], send, recv, 5, xn)]
        local = [pltpu.make_async_copy(p.at[k_yn, top, :], loc_a, loc.at[0]),
                 pltpu.make_async_copy(p.at[k_xn, bot, :], loc_b, loc.at[1])]
        return direct, combined, local

    def start(ins, outs, scr):
        direct, _, local = copies(ins, outs, scr)
        for cp in local + direct:
            cp.start()

    def mid(ins, outs, scr):
        loc_a, loc_b, in_x, in_y, comb_a, comb_b = scr[:6]
        direct, combined, local = copies(ins, outs, scr)
        for mine, arrival, inbox, out, nxt in ((local[0], direct[2], in_x, comb_a, combined[0]),
                                               (local[1], direct[3], in_y, comb_b, combined[1])):
            mine.wait()
            arrival.wait_recv()
            src = loc_a if out is comb_a else loc_b
            out[...] = (src[...].astype(F32) + inbox[...].astype(F32)).astype(BF16)
            nxt.start()

    def finish(ins, outs, scr):
        direct, combined, _ = copies(ins, outs, scr)
        direct[0].wait_recv()
        direct[1].wait_recv()
        combined[0].wait_recv()
        combined[1].wait_recv()
        for cp in direct + combined:
            cp.wait_send()

    buf = pltpu.VMEM((half, W), BF16)
    return Side((part,), (ANY,), (jax.ShapeDtypeStruct((2, R, W), BF16),),
                (buf, buf, buf, buf, buf, buf, _sems(6), _sems(6), _sems(2)), start, finish, mid, "xy")


ADAM_TILE_BYTES = 3 * 512 * 1024


def _row_tiles(rows, width):
    return 2 if rows % 32 == 0 and rows * width * 4 > ADAM_TILE_BYTES else 1


def chip_sum(name, grad, recv, c_idx, chip_idx):
    _, R, C = grad.shape
    nt = 1
    tr = R // nt

    def body(s_ref, g_ref, r_ref, p_ref, own_ref):
        k = pl.program_id(1)
        tot = g_ref[0] + r_ref[0]
        p_ref[0] = tot.astype(BF16)

        @pl.when(k == s_ref[1])
        def _():
            own_ref[...] = tot

    grid_spec = pltpu.PrefetchScalarGridSpec(
        num_scalar_prefetch=1, grid=(nt, 4),
        in_specs=[pl.BlockSpec((1, tr, C), lambda i, k, s: (2 * k + s[0], i, 0)),
                  pl.BlockSpec((1, tr, C), lambda i, k, s: (k, i, 0))],
        out_specs=[pl.BlockSpec((1, tr, C), lambda i, k, s: (k, i, 0)),
                   pl.BlockSpec((tr, C), lambda i, k, s: (i, 0))])
    return pl.pallas_call(
        body, name=name, grid_spec=grid_spec,
        out_shape=[jax.ShapeDtypeStruct((4, R, C), BF16), jax.ShapeDtypeStruct((R, C), F32)],
        compiler_params=_cp(dimension_semantics=("arbitrary", "arbitrary")),
    )(jnp.stack([c_idx, chip_idx]), grad, recv)


def _adamw(w, g, m, v):
    m2 = ADAM_B1 * m + (1.0 - ADAM_B1) * g
    v2 = ADAM_B2 * v + (1.0 - ADAM_B2) * (g * g)
    m_hat = m2 / (1.0 - ADAM_B1 ** ADAM_STEP)
    v_hat = v2 / (1.0 - ADAM_B2 ** ADAM_STEP)
    delta = -ADAM_LR * (m_hat / (jnp.sqrt(v_hat) + ADAM_EPS) + ADAM_WD * w)
    return delta, m2, v2


def shard_adam(name, owns, recvs, w, m, v, io_t=False):
    n = len(owns)
    R = owns[0].shape[0]
    ct = min(o.shape[1] for o in owns)
    first = [sum(o.shape[1] for o in owns[:j]) // ct for j in range(n)]
    count = [o.shape[1] // ct for o in owns]
    nt = _row_tiles(R, ct)
    tr = R // nt

    def body(*refs):
        o_refs, r_refs = refs[:n], refs[n:2 * n]
        w_ref, m_ref, v_ref, g_ref, d_ref, nm_ref, nv_ref = refs[2 * n:]
        g = None
        for j in range(n):
            gj = o_refs[j][...]
            for q in range(recvs[j].shape[0]):
                gj = gj + r_refs[j][q].astype(F32)
            g = gj if g is None else jnp.where(pl.program_id(0) >= first[j], gj, g)
        t = (lambda a: a.T) if io_t else (lambda a: a)
        delta, m2, v2 = _adamw(t(w_ref[...]), g, t(m_ref[...]), t(v_ref[...]))
        g_ref[...] = t(g)
        d_ref[...] = t(delta)
        nm_ref[...] = t(m2)
        nv_ref[...] = t(v2)

    def part(j):
        return pl.BlockSpec((tr, ct), lambda k, i: (i, jnp.clip(k - first[j], 0, count[j] - 1)))

    def part3(j):
        return pl.BlockSpec((recvs[j].shape[0], tr, ct), lambda k, i: (0, i, jnp.clip(k - first[j], 0, count[j] - 1)))

    C = sum(count) * ct
    tile = pl.BlockSpec((ct, tr), lambda k, i: (k, i)) if io_t else pl.BlockSpec((tr, ct), lambda k, i: (i, k))
    return pl.pallas_call(
        body, name=name, grid=(sum(count), nt),
        in_specs=[part(j) for j in range(n)] + [part3(j) for j in range(n)] + [tile, tile, tile],
        out_specs=[tile] * 4, out_shape=[jax.ShapeDtypeStruct((C, R) if io_t else (R, C), F32)] * 4,
        compiler_params=_cp(dimension_semantics=("arbitrary", "arbitrary")),
    )(*owns, *recvs, w, m, v)


ROW_N1, ROW_N2, ROW_BG, ROW_QN, ROW_KN, ROW_CB, ROW_LW, ROW_LB, ROW_CW = 0, 1, 2, 4, 5, 6, 7, 8, 9
PACK_ROWS = 40
SMALL = ("norm1_w", "norm2_w", "b_gate", "q_norm_w", "k_norm_w", "conv_b", "conv_ln_w", "conv_ln_b", "conv_w")


def small_sync(g, sq, sides=()):
    ns = len(SMALL)

    def body(*refs):
        gi = dict(zip(SMALL, refs[:ns]))
        sq_ref, tot, pack, recv, send_sems, recv_sems = refs[ns:]
        x, y, c, _ = _place()
        me = 4 * x + 2 * y + c

        pack[...] = jnp.zeros_like(pack)
        pack[ROW_KN:ROW_KN + 1, LANES:2 * LANES] = jnp.full((1, LANES), (0.5 / D) * jnp.sum(sq_ref[...]), F32)
        pack[ROW_N1:ROW_N1 + 1, :] = gi["norm1_w"][...]
        pack[ROW_N2:ROW_N2 + 1, :] = gi["norm2_w"][...]
        pack[ROW_BG:ROW_BG + 2, :] = gi["b_gate"][...]
        pack[ROW_QN:ROW_QN + 1, 0:HD] = gi["q_norm_w"][...]
        pack[ROW_KN:ROW_KN + 1, 0:HD] = gi["k_norm_w"][...]
        pack[ROW_CB:ROW_CB + 1, 0:CC] = gi["conv_b"][...]
        pack[ROW_LW:ROW_LW + 1, 0:CC] = gi["conv_ln_w"][...]
        pack[ROW_LB:ROW_LB + 1, 0:CC] = gi["conv_ln_b"][...]
        pack[ROW_CW:ROW_CW + KW, 0:CC] = gi["conv_w"][...]

        copies = []
        for k in range(1, NDEV):
            peer = (x ^ (k >> 2), y ^ ((k >> 1) & 1), c ^ (k & 1))
            cp = pltpu.make_async_remote_copy(
                src_ref=pack, dst_ref=recv.at[me], send_sem=send_sems.at[k - 1], recv_sem=recv_sems.at[k - 1],
                device_id=peer, device_id_type=MESH)
            cp.start()
            copies.append(cp)
        recv[me] = pack[...]
        for cp in copies:
            cp.wait()
        acc = recv[0]
        for p in range(1, NDEV):
            acc = acc + recv[p]
        tot[...] = acc

    args = [g[k] for k in SMALL] + [sq]
    res = _call(
        body, sides, name="small_sync", grid=(1,), in_specs=[VMEM] * len(args), out_specs=[VMEM],
        out_shape=[jax.ShapeDtypeStruct((PACK_ROWS, D), F32)],
        scratch_shapes=[pltpu.VMEM((PACK_ROWS, D), F32), pltpu.VMEM((NDEV, PACK_ROWS, D), F32),
                        _sems(NDEV - 1), _sems(NDEV - 1)],
        args=args, own_comm=True)
    return (res[0][0], res[1]) if sides else res[0]


def small_adam(tot, w, m, v, me):
    ns = len(SMALL)

    def body(me_ref, tot, *refs):
        wi = dict(zip(SMALL, refs[:ns]))
        mi = dict(zip(SMALL, refs[ns:2 * ns]))
        vi = dict(zip(SMALL, refs[2 * ns:3 * ns]))
        outs = refs[3 * ns:7 * ns]
        loss_ref = refs[7 * ns]
        me = me_ref[0]

        def shard_grad(name):
            if name == "b_gate":
                return tot[ROW_BG:ROW_BG + 2, pl.ds(pl.multiple_of(me * LANES, LANES), LANES)]
            if name == "conv_w":
                win = tot[ROW_CW:ROW_CW + KW, pl.ds(pl.multiple_of((me // 2) * LANES, LANES), LANES)]
                return jnp.where(me % 2 == 1, win[:, HD:LANES], win[:, 0:HD])
            row = {"norm1_w": ROW_N1, "norm2_w": ROW_N2, "q_norm_w": ROW_QN, "k_norm_w": ROW_KN,
                   "conv_b": ROW_CB, "conv_ln_w": ROW_LW, "conv_ln_b": ROW_LB}[name]
            return tot[row:row + 1, 0:wi[name].shape[1]]

        for i, name in enumerate(SMALL):
            gr = shard_grad(name)
            delta, m2, v2 = _adamw(wi[name][...], gr, mi[name][...], vi[name][...])
            outs[4 * i][...] = gr
            outs[4 * i + 1][...] = delta
            outs[4 * i + 2][...] = m2
            outs[4 * i + 3][...] = v2
        loss_ref[...] = tot[ROW_KN:ROW_KN + 1, LANES:2 * LANES]

    out_shape = []
    for name in SMALL:
        out_shape += [jax.ShapeDtypeStruct(w[name].shape, F32)] * 4
    out_shape.append(jax.ShapeDtypeStruct((1, LANES), F32))
    args = [tot] + [w[k] for k in SMALL] + [m[k] for k in SMALL] + [v[k] for k in SMALL]
    grid_spec = pltpu.PrefetchScalarGridSpec(
        num_scalar_prefetch=1, grid=(1,), in_specs=[VMEM] * len(args), out_specs=[VMEM] * len(out_shape))
    res = pl.pallas_call(body, name="small_adam", grid_spec=grid_spec, out_shape=out_shape)(me, *args)
    out = {name: tuple(res[4 * i:4 * i + 4]) for i, name in enumerate(SMALL)}
    return out, res[4 * ns][0, 0]


MATS = ("w_in", "w_o_attn", "w_pw_conv", "w_out", "w_ffn_in", "w_ffn_out")
TRANSPOSED = ("w_in", "w_ffn_in")
WEIGHTS = ("norm1_w", "w_in", "b_gate", "q_norm_w", "k_norm_w", "w_o_attn", "conv_w", "conv_b", "conv_ln_w",
           "conv_ln_b", "w_pw_conv", "w_out", "norm2_w", "w_ffn_in", "w_ffn_out")


def _blocks_to_cols(blocks):
    n, R, C = blocks.shape
    return blocks.transpose(1, 0, 2).reshape(R, n * C)


def kernel(x, positions, norm1_w, w_in, b_gate, q_norm_w, k_norm_w, w_o_attn, conv_w, conv_b, conv_ln_w, conv_ln_b, w_pw_conv, w_out, norm2_w, w_ffn_in, w_ffn_out, loss_target, m_norm1_w, m_w_in, m_b_gate, m_q_norm_w, m_k_norm_w, m_w_o_attn, m_conv_w, m_conv_b, m_conv_ln_w, m_conv_ln_b, m_w_pw_conv, m_w_out, m_norm2_w, m_w_ffn_in, m_w_ffn_out, v_norm1_w, v_w_in, v_b_gate, v_q_norm_w, v_k_norm_w, v_w_o_attn, v_conv_w, v_conv_b, v_conv_ln_w, v_conv_ln_b, v_w_pw_conv, v_w_out, v_norm2_w, v_w_ffn_in, v_w_ffn_out):
    w = dict(norm1_w=norm1_w, w_in=w_in, b_gate=b_gate, q_norm_w=q_norm_w, k_norm_w=k_norm_w, w_o_attn=w_o_attn,
             conv_w=conv_w, conv_b=conv_b, conv_ln_w=conv_ln_w, conv_ln_b=conv_ln_b, w_pw_conv=w_pw_conv,
             w_out=w_out, norm2_w=norm2_w, w_ffn_in=w_ffn_in, w_ffn_out=w_ffn_out)
    m = dict(norm1_w=m_norm1_w, w_in=m_w_in, b_gate=m_b_gate, q_norm_w=m_q_norm_w, k_norm_w=m_k_norm_w,
             w_o_attn=m_w_o_attn, conv_w=m_conv_w, conv_b=m_conv_b, conv_ln_w=m_conv_ln_w,
             conv_ln_b=m_conv_ln_b, w_pw_conv=m_w_pw_conv, w_out=m_w_out, norm2_w=m_norm2_w,
             w_ffn_in=m_w_ffn_in, w_ffn_out=m_w_ffn_out)
    v = dict(norm1_w=v_norm1_w, w_in=v_w_in, b_gate=v_b_gate, q_norm_w=v_q_norm_w, k_norm_w=v_k_norm_w,
             w_o_attn=v_w_o_attn, conv_w=v_conv_w, conv_b=v_conv_b, conv_ln_w=v_conv_ln_w,
             conv_ln_b=v_conv_ln_b, w_pw_conv=v_w_pw_conv, w_out=v_w_out, norm2_w=v_norm2_w,
             w_ffn_in=v_w_ffn_in, w_ffn_out=v_w_ffn_out)
    def two_d(t):
        t = {k: (a[0] if a.ndim == 3 else a) for k, a in t.items()}
        return {k: (a.T if k in TRANSPOSED else a) for k, a in t.items()}

    w, m, v = two_d(w), two_d(m), two_d(v)

    x2, target = x[0], loss_target[0]
    c_idx = lax.axis_index("c").astype(jnp.int32)
    chip_idx = (2 * lax.axis_index("x") + lax.axis_index("y")).astype(jnp.int32)
    qw2 = jnp.tile(w["q_norm_w"], (1, 2))
    kw2 = jnp.tile(w["k_norm_w"], (1, 2))

    ax, ay = lax.axis_index("x"), lax.axis_index("y")
    chip_order = jnp.stack([2 * ax + ay, 2 * (1 - ax) + ay, 2 * ax + 1 - ay, 2 * (1 - ax) + 1 - ay]).astype(jnp.int32)
    h_t, proj, w_in_blocks, tabs = in_proj_gather(x2, w["norm1_w"], w["w_in"], chip_order, positions.reshape(S, 1))
    w_in_t = w_in_blocks.reshape(INW, D)
    (attn, lse), ((w_ffn_in_blocks,), (w_out_blocks,), (w_o_blocks,), (w_pw_blocks,), (bg_blocks,), (cw_blocks,)) = attn_fwd(
        proj, tabs, qw2, kw2, sides=(ag_blocks_relay(w["w_ffn_in"], BF16), ag_blocks_relay(w["w_out"], BF16),
                                     ag_blocks_relay(w["w_o_attn"], BF16, transpose=True),
                                     ag_blocks_relay(w["w_pw_conv"], BF16, transpose=True),
                                     ag_blocks(w["b_gate"], F32), ag_blocks(w["conv_w"], F32)))
    w_ffn_in_t = w_ffn_in_blocks.reshape(2 * FF, D)
    w_out_f = w_out_blocks.reshape(D, D)
    w_o_t, w_pw_t = w_o_blocks.reshape(D, CC), w_pw_blocks.reshape(D, CC)
    b_gate_f, conv_w_f = _blocks_to_cols(bg_blocks), _blocks_to_cols(cw_blocks)
    cpre, u3 = conv_fwd(proj, conv_w_f, w["conv_b"], w["conv_ln_w"], w["conv_ln_b"])
    x1, z, ya, yb = mix_out(x2, proj, b_gate_f, attn, u3, w_o_t, w_pw_t, w_out_f)
    (h2, gu, f), ((w_ffn_out_blocks,),) = ffn_in(x1, w["norm2_w"], w_ffn_in_t, sides=(ag_blocks_relay(w["w_ffn_out"], BF16),))
    w_ffn_out_f = w_ffn_out_blocks.reshape(FF, D)
    dy, dyb, sq = ffn_out_loss(x1, f, w_ffn_out_f, target)

    g = {}
    g_ffn_out = mm_tn("gw_ffn_out", f, dyb, FF // 2, D).reshape(NDEV, FF // NDEV, D)
    (d_gu, d_x1, d_x1b, g["norm2_w"]), ((ra_ffn_out,),) = ffn_bwd(
        dy, dyb, gu, x1, w["norm2_w"], w_ffn_in_t, w_ffn_out_f, sides=(rs_to_sibling([g_ffn_out]),))
    pb_ffn_out, own_ffn_out = chip_sum("chip_sum_w_ffn_out", g_ffn_out, ra_ffn_out, c_idx, chip_idx)
    g_ffn_in = mm_tn("gw_ffn_in", d_gu, h2, FF // 2, D).reshape(NDEV, 2 * FF // NDEV, D)
    g_out = mm_tn("gw_out", z, d_x1b, D // 2, D).reshape(NDEV, D // NDEV, D)
    (d_ya, d_yb, d_gl, d_attn, d_u3, g["b_gate"]), ((ra_ffn_in,),) = out_bwd(
        d_x1b, proj, b_gate_f, ya, yb, w_o_t, w_pw_t, w_out_f, sides=(rs_to_sibling([g_ffn_in]),))
    pb_ffn_in, own_ffn_in = chip_sum("chip_sum_w_ffn_in", g_ffn_in, ra_ffn_in, c_idx, chip_idx)
    g_w_o = mm_tn("gw_o_attn", d_ya, attn, D // 2, CC).reshape(NDEV, D // NDEV, CC)
    g_w_pw = mm_tn("gw_pw_conv", d_yb, u3, D // 2, CC).reshape(NDEV, D // NDEV, CC)
    (d_conv, g["conv_w"], g["conv_b"], g["conv_ln_w"], g["conv_ln_b"]), ((ra_out, ra_w_o, ra_w_pw),) = conv_bwd(
        proj, cpre, d_u3, conv_w_f, conv_w_f[::-1], w["conv_ln_w"], w["conv_ln_b"],
        sides=(rs_to_sibling([g_out, g_w_o, g_w_pw]),))
    pb_out, own_out = chip_sum("chip_sum_w_out", g_out, ra_out, c_idx, chip_idx)
    pb_w_o, own_w_o = chip_sum("chip_sum_w_o_attn", g_w_o, ra_w_o, c_idx, chip_idx)
    pb_w_pw, own_w_pw = chip_sum("chip_sum_w_pw_conv", g_w_pw, ra_w_pw, c_idx, chip_idx)
    (d_q, d_k, d_v, gqw, gkw), ((rb_ffn_out, rb_ffn_in, rb_out, rb_w_o, rb_w_pw),) = attn_bwd(
        proj, tabs, qw2, kw2, d_attn, attn, lse,
        sides=(rs_to_chips([pb_ffn_out, pb_ffn_in, pb_out, pb_w_o, pb_w_pw]),))
    g["q_norm_w"] = gqw[0:1, 0:HD] + gqw[0:1, HD:LANES]
    g["k_norm_w"] = gkw[0:1, 0:HD] + gkw[0:1, HD:LANES]
    d_segs = (d_q, d_k, d_v, d_conv, d_gl)
    parts, to_sibling, to_chips, owns, from_chips = [], None, None, [], []
    for k, hw in enumerate(GW_IN_SPLIT):
        sides = tuple(s for s in (to_chips, to_sibling) if s is not None)
        part = gw_in_t("gw_in_%d" % k, h_t, d_segs, sum(GW_IN_SPLIT[:k]), hw, sides=sides)
        part, outs = part if sides else (part, [])
        outs = list(outs)
        if to_chips is not None:
            from_chips.append(outs.pop(0)[0])
        if to_sibling is not None:
            pb, own = chip_sum("chip_sum_w_in_%d" % (k - 1), parts[-1], outs.pop(0)[0], c_idx, chip_idx)
            owns.append(own)
            to_chips = rs_to_chips_combined(pb)
        else:
            to_chips = None
        parts.append(part.reshape(NDEV, INW // NDEV, hw))
        to_sibling = rs_to_sibling([parts[-1]])
    (grad_x, g["norm1_w"]), ((rb_prev,), (ra_last,)) = in_bwd(
        d_q, d_k, d_v, d_conv, d_gl, w_in_t, x2, d_x1, w["norm1_w"], sides=(to_chips, to_sibling))
    from_chips.append(rb_prev)
    pb, own = chip_sum("chip_sum_w_in_%d" % (len(GW_IN_SPLIT) - 1), parts[-1], ra_last, c_idx, chip_idx)
    owns.append(own)
    small_sums, ((rb_last,),) = small_sync(g, sq, sides=(rs_to_chips_combined(pb),))
    small, loss = small_adam(small_sums, w, m, v, (4 * ax + 2 * ay + c_idx).astype(jnp.int32).reshape(1))
    from_chips.append(rb_last)

    res = {
        "w_in": shard_adam("adam_w_in", owns, from_chips, w["w_in"], m["w_in"], v["w_in"]),
        "w_ffn_in": shard_adam("adam_w_ffn_in", [own_ffn_in], [rb_ffn_in], w["w_ffn_in"], m["w_ffn_in"], v["w_ffn_in"]),
        "w_o_attn": shard_adam("adam_w_o_attn", [own_w_o], [rb_w_o], w["w_o_attn"], m["w_o_attn"], v["w_o_attn"], io_t=True),
        "w_pw_conv": shard_adam("adam_w_pw_conv", [own_w_pw], [rb_w_pw],
                                w["w_pw_conv"], m["w_pw_conv"], v["w_pw_conv"], io_t=True),
        "w_out": shard_adam("adam_w_out", [own_out], [rb_out], w["w_out"], m["w_out"], v["w_out"]),
        "w_ffn_out": shard_adam("adam_w_ffn_out", [own_ffn_out], [rb_ffn_out],
                                w["w_ffn_out"], m["w_ffn_out"], v["w_ffn_out"]),
    }
    res = {k: tuple(a.T if k in TRANSPOSED else a for a in r) for k, r in res.items()}
    res.update(small)

    def shaped(name, a):
        return a.reshape((1,) + a.shape) if name in MATS or name in ("b_gate", "conv_w") else a

    outs = [loss, grad_x.reshape(1, S, D)]
    for i in range(4):
        outs += [shaped(k, res[k][i]) for k in WEIGHTS]
    return tuple(outs)
```

```python
import functools
from typing import Callable, NamedTuple, Optional

import numpy as np
import jax
import jax.numpy as jnp
from jax import lax
from jax.experimental import pallas as pl
from jax.experimental.pallas import tpu as pltpu

F32 = jnp.float32
BF16 = jnp.bfloat16

S = 2048
D = 1024
HD = 64
QKV = 1536
CC = 512
KW = 31
FF = 2816
INW = 7680
OFF_Q, OFF_K, OFF_V, OFF_CA, OFF_CB, OFF_GA, OFF_GB = 0, 1536, 3072, 4608, 5120, 5632, 6656
DILATIONS = (1, 4, 16)
HALF_SPAN = 64
EPS = 1e-6
NEG_INF = -1e30
ROPE_THETA = 500000.0
ROT_DIM = 16

ADAM_LR = 0.001
ADAM_B1 = 0.9
ADAM_B2 = 0.999
ADAM_EPS = 1e-08
ADAM_WD = 0.01
ADAM_STEP = 10

NDEV = 8
LANES = 128
TM = 256
IN_PROJ_TM = 512
TQ = 128
VMEM_LIMIT = 56 * 1024 * 1024
MESH = pl.DeviceIdType.MESH


def _cp(**kw):
    return pltpu.CompilerParams(vmem_limit_bytes=VMEM_LIMIT, **kw)


def _row(width, col=0, tm=TM):
    return pl.BlockSpec((tm, width), lambda i: (i, col))


PLANE = 512


def _planes(width, tm=TM):
    return pl.BlockSpec((width // PLANE, tm, PLANE), lambda i: (0, i, 0))


def _res(shape):
    nd = len(shape)
    return pl.BlockSpec(shape, lambda *_: (0,) * nd, pipeline_mode=pl.Buffered(1))


def _dot(a, b):
    return jnp.dot(a, b, preferred_element_type=F32)


def _dot_nt(a, b):
    return lax.dot_general(a, b, (((1,), (1,)), ((), ())), preferred_element_type=F32)


def _dot_tn(a, b):
    return lax.dot_general(a, b, (((0,), (0,)), ((), ())), preferred_element_type=F32)


def _sigmoid(x):
    return jax.nn.sigmoid(x)


def _dsilu(x, sg):
    return sg * (1.0 + x * (1.0 - sg))


ANY = pl.BlockSpec(memory_space=pl.ANY)
VMEM = pl.BlockSpec(memory_space=pltpu.VMEM)


class Side(NamedTuple):
    args: tuple
    in_specs: tuple
    out_shape: tuple
    scratch: tuple
    start: Callable
    finish: Callable
    mid: Optional[Callable] = None
    peers: str = ""


BARRIER_IDS = {"s": 0, "dxy": 1, "dsxy": 2, "sxy": 3, "xy": 4}


def _peer_barrier(peers):
    x, y, c = lax.axis_index("x"), lax.axis_index("y"), lax.axis_index("c")
    where = {"s": (x, y, 1 - c), "x": (1 - x, y, c), "y": (x, 1 - y, c), "d": (1 - x, 1 - y, c)}
    barrier = pltpu.get_barrier_semaphore()
    for p in peers:
        pl.semaphore_signal(barrier, inc=1, device_id=where[p], device_id_type=MESH)
    pl.semaphore_wait(barrier, len(peers))


def _call(body, sides=(), *, name, grid, in_specs, out_specs, out_shape, scratch_shapes=(), args, own_comm=False,
          tail=None):
    assert tail is None or int(np.prod(grid)) == 1
    ni, no, ns = len(in_specs), len(out_specs), len(scratch_shapes)
    cnt = [(len(s.args), len(s.out_shape), len(s.scratch)) for s in sides]
    peers = "".join(sorted(set("".join(s.peers for s in sides))))
    if own_comm or not sides or any(not s.peers for s in sides):
        peers = ""

    def take(refs, pos, n):
        return refs[pos:pos + n], pos + n

    def full(*refs):
        m_in, pos = take(refs, 0, ni)
        s_in = []
        for a, _, _ in cnt:
            r, pos = take(refs, pos, a)
            s_in.append(r)
        m_out, pos = take(refs, pos, no)
        s_out = []
        for _, o, _ in cnt:
            r, pos = take(refs, pos, o)
            s_out.append(r)
        m_scr, pos = take(refs, pos, ns)
        s_scr = []
        for _, _, c in cnt:
            r, pos = take(refs, pos, c)
            s_scr.append(r)
        if sides:
            first = functools.reduce(jnp.logical_and, [pl.program_id(d) == 0 for d in range(len(grid))])
            last = functools.reduce(jnp.logical_and, [pl.program_id(d) == g - 1 for d, g in enumerate(grid)])

            @pl.when(first)
            def _():
                if peers:
                    _peer_barrier(peers)
                for s, a, o, c in zip(sides, s_in, s_out, s_scr):
                    s.start(a, o, c)

            steps = int(np.prod(grid))
            mid_step = (2 * steps) // 3
            if steps > 1 and any(s.mid is not None for s in sides):
                step = functools.reduce(lambda acc, d: acc * grid[d] + pl.program_id(d), range(len(grid)), 0)

                @pl.when(step == mid_step)
                def _():
                    for s, a, o, c in zip(sides, s_in, s_out, s_scr):
                        if s.mid is not None:
                            s.mid(a, o, c)

        body(*m_in, *m_out, *m_scr)
        if sides:
            @pl.when(last)
            def _():
                for s, a, o, c in zip(sides, s_in, s_out, s_scr):
                    if s.mid is not None and steps == 1:
                        s.mid(a, o, c)
                if tail is not None:
                    tail(*m_in, *m_out, *m_scr)
                for s, a, o, c in zip(sides, s_in, s_out, s_scr):
                    s.finish(a, o, c)
        elif tail is not None:
            tail(*m_in, *m_out, *m_scr)

    res = pl.pallas_call(
        full, name=name, grid=grid,
        in_specs=list(in_specs) + [sp for s in sides for sp in s.in_specs],
        out_specs=list(out_specs) + [ANY for s in sides for _ in s.out_shape],
        out_shape=list(out_shape) + [o for s in sides for o in s.out_shape],
        scratch_shapes=list(scratch_shapes) + [c for s in sides for c in s.scratch],
        compiler_params=_cp(dimension_semantics=("arbitrary",) * len(grid),
                            **({"collective_id": BARRIER_IDS[peers]} if peers else {})),
    )(*args, *[a for s in sides for a in s.args])
    res = list(res)
    if not sides:
        return res
    outs, pos = take(res, 0, no)
    side_outs = []
    for _, o, _ in cnt:
        r, pos = take(res, pos, o)
        side_outs.append(r)
    return outs, side_outs


def _inv_freq_lanes():
    inv = np.float32(ROPE_THETA) ** (-np.arange(0, ROT_DIM, 2, dtype=np.float32) / np.float32(ROT_DIM))
    lane = np.arange(LANES) % HD
    out = np.where(lane < ROT_DIM, inv[lane % (ROT_DIM // 2)], 0.0).astype(np.float32)
    return jnp.asarray(out.reshape(1, LANES))


def _rope_tables(pos, inv_freq):
    ang = pos.astype(F32) * inv_freq
    lane = lax.broadcasted_iota(jnp.int32, ang.shape, 1) % HD
    cs = jnp.cos(ang)
    sn = jnp.sin(ang)
    return (jnp.where(lane < ROT_DIM, cs, 1.0), jnp.where(lane < ROT_DIM // 2, -sn, 0.0),
            jnp.where(lane < ROT_DIM // 2, 0.0, jnp.where(lane < ROT_DIM, sn, 0.0)))


def _rope(v, c, s1, s2):
    return v * c + pltpu.roll(v, LANES - 8, axis=1) * s1 + pltpu.roll(v, 8, axis=1) * s2


def _rope_t(d, c, s1, s2):
    return d * c - pltpu.roll(d, LANES - 8, axis=1) * s1 - pltpu.roll(d, 8, axis=1) * s2


def _head_mat():
    r = lax.broadcasted_iota(jnp.int32, (LANES, LANES), 0) // HD
    c = lax.broadcasted_iota(jnp.int32, (LANES, LANES), 1) // HD
    return jnp.where(r == c, 1.0 / HD, 0.0).astype(BF16)


def _head_mean(t, e):
    hi = t.astype(BF16)
    rest = (t - hi.astype(F32)).astype(BF16)
    return _dot(hi, e) + _dot(rest, e)


def in_proj_gather(x, norm_w, shard_t, chip_order, pos_col):
    R = INW // NDEV
    tm = IN_PROJ_TM
    half, nt = R // 2, S // tm

    def body(ord_ref, x_ref, nw_ref, sh_ref, pos_ref, f_ref, ht_ref, p_ref, wfull_ref, c_ref, s1_ref, s2_ref,
             wt, hs, send, recv, loc):
        kk, i = pl.program_id(0), pl.program_id(1)
        x, y, c, _ = _place()
        me, flip = 4 * x + 2 * y + c, 1 - 2 * c
        here, sib, xn, yn = (x, y, c), (x, y, 1 - c), (1 - x, y, c), (x, 1 - y, c)
        b_xn, b_yn, b_dg = 4 * (1 - x) + 2 * y + c, 4 * x + 2 * (1 - y) + c, 4 * (1 - x) + 2 * (1 - y) + c

        def cp(k, block, to, rows=None):
            dst = wt.at[block] if rows is None else wt.at[block, pl.ds(rows * half, half), :]
            return _remote(dst, dst, send, recv, k, to)

        def sends():
            return [cp(0, me, sib), cp(1, me, xn), cp(2, me, yn), cp(3, b_xn, sib), cp(4, b_yn, sib),
                    cp(5, b_xn, yn, rows=0), cp(6, b_yn, xn, rows=1), cp(7, b_dg, sib, rows=0), cp(8, b_dg, sib, rows=1)]

        def keep(j, blk0):
            pair = pl.ds(pl.multiple_of(blk0, 2), 2)
            return pltpu.make_async_copy(wt.at[pair], wfull_ref.at[pair], loc.at[j])

        @pl.when((kk == 0) & (i == 0))
        def _():
            _peer_barrier("sxy")
            _cast_rows(wt.at[me], sh_ref)
            for s_ in sends()[0:3]:
                s_.start()

            def tables(j, _):
                chunk = pl.ds(pl.multiple_of(j * TM, TM), TM)
                c_ref[chunk, :], s1_ref[chunk, :], s2_ref[chunk, :] = _rope_tables(pos_ref[chunk, :], f_ref[...])
                return 0

            lax.fori_loop(0, S // TM, tables, 0)
            cp(0, me + flip, here).wait_recv()
            keep(0, me - c).start()

        @pl.when((kk == 1) & (i == 0))
        def _():
            cp(1, b_xn, here).wait_recv()
            sends()[5].start()
            sends()[3].start()
            cp(2, b_yn, here).wait_recv()
            sends()[6].start()
            sends()[4].start()
            cp(3, b_xn + flip, here).wait_recv()
            keep(1, b_xn - c).start()

        @pl.when((kk == 2) & (i == 0))
        def _():
            cp(4, b_yn + flip, here).wait_recv()
            keep(2, b_yn - c).start()

        @pl.when((kk == 3) & (i == 0))
        def _():
            cp(5, b_dg, here, rows=0).wait_recv()
            sends()[7].start()
            cp(6, b_dg, here, rows=1).wait_recv()
            sends()[8].start()
            cp(7, b_dg + flip, here, rows=0).wait_recv()
            cp(8, b_dg + flip, here, rows=1).wait_recv()
            keep(3, b_dg - c).start()

        rows = pl.ds(pl.multiple_of(i * tm, tm), tm)

        @pl.when(kk == 0)
        def _():
            xv = x_ref[...]
            r = lax.rsqrt(jnp.mean(xv * xv, axis=-1, keepdims=True) + EPS)
            hf = xv * r * nw_ref[...]
            ht_ref[...] = hf.T.astype(BF16)
            hs[rows, :] = hf.astype(BF16)

        h = hs[rows, :]
        chip = ord_ref[kk]
        for cc in range(2):
            p_ref[:, cc * R:(cc + 1) * R] = _dot_nt(h, wt[2 * chip + cc])

        @pl.when((kk == 3) & (i == nt - 1))
        def _():
            for s_ in sends():
                s_.wait_send()
            for j, blk in enumerate((me, b_xn, b_yn, b_dg)):
                keep(j, blk - c).wait()

    def first_pass(kk, i):
        return jnp.where(kk == 0, i, nt - 1)

    grid_spec = pltpu.PrefetchScalarGridSpec(
        num_scalar_prefetch=1, grid=(4, nt),
        in_specs=[pl.BlockSpec((tm, D), lambda kk, i, o: (first_pass(kk, i), 0)),
                  pl.BlockSpec((1, D), lambda kk, i, o: (0, 0)), VMEM, VMEM,
                  pl.BlockSpec((1, LANES), lambda kk, i, o: (0, 0))],
        out_specs=[pl.BlockSpec((D, tm), lambda kk, i, o: (0, first_pass(kk, i))),
                   pl.BlockSpec((tm, 2 * R), lambda kk, i, o: (i, o[kk])), ANY]
        + [pl.BlockSpec((S, LANES), lambda kk, i, o: (0, 0))] * 3,
        scratch_shapes=[pltpu.VMEM((NDEV, R, D), BF16), pltpu.VMEM((S, D), BF16), _sems(9), _sems(9), _sems(4)])
    res = pl.pallas_call(
        body, name="in_proj_gather", grid_spec=grid_spec,
        out_shape=[jax.ShapeDtypeStruct((D, S), BF16), jax.ShapeDtypeStruct((S, INW), F32),
                   jax.ShapeDtypeStruct((NDEV, R, D), BF16)] + [jax.ShapeDtypeStruct((S, LANES), F32)] * 3,
        compiler_params=_cp(dimension_semantics=("arbitrary", "arbitrary"), collective_id=BARRIER_IDS["sxy"]),
    )(chip_order, x, norm_w, shard_t, pos_col, _inv_freq_lanes())
    return res[0], res[1], res[2], tuple(res[3:])


def _qk_specs():
    nb = QKV // LANES
    return [pl.BlockSpec((S, LANES), functools.partial(lambda hp, g, o: (0, o + g * 4 + hp), o=o))
            for o in (OFF_Q // LANES, OFF_K // LANES, OFF_V // LANES)]


def _tab_specs():
    return [pl.BlockSpec((S, LANES), lambda hp, g: (0, 0), pipeline_mode=pl.Buffered(1))] * 3


def _vec_spec():
    return pl.BlockSpec((1, LANES), lambda hp, g: (0, 0))


def _sub_rows(r, d, start, n):
    if d == 1:
        return pl.ds(start, n)
    return pl.ds(r + d * start, n, stride=d)


def _band_window(i, L):
    W = min(TQ + 2 * HALF_SPAN, L)
    q0 = pl.multiple_of(i * TQ, TQ)
    k0 = pl.multiple_of(jnp.clip(q0 - HALF_SPAN, 0, L - W), HALF_SPAN)
    qpos = q0 + (lax.broadcasted_iota(jnp.int32, (2 * TQ, W), 0) & (TQ - 1))
    kpos = k0 + lax.broadcasted_iota(jnp.int32, (2 * TQ, W), 1)
    valid = jnp.abs(qpos - kpos) <= HALF_SPAN
    return W, q0, k0, valid


def _stack_heads(t, lo):
    z = jnp.zeros_like(t)
    return jnp.concatenate([jnp.where(lo, t, z), jnp.where(lo, z, t)], axis=0)


def _unstack_heads(t2, lo):
    return jnp.where(lo, t2[0:TQ], t2[TQ:2 * TQ])


CHAINS = 8


def _interleave(d):
    ru = min(d, CHAINS)
    return ru, min(CHAINS // ru, S // d // TQ)


def _for_blocks(n, fn):
    if n == 1:
        fn(0)
    else:
        def it(j, _):
            fn(j)
            return 0
        lax.fori_loop(0, n, it, 0)


def attn_fwd(proj, tabs, qw2, kw2, sides=()):
    CH = 256

    def body(q_ref, k_ref, v_ref, c_ref, s1_ref, s2_ref, qw_ref, kw_ref, at_ref, ls_ref,
             qs, ks, vs, osub, lsub, onat, lnat, qn, kn):
        g = pl.program_id(1)
        lo = lax.broadcasted_iota(jnp.int32, (1, LANES), 1) < HD
        e = _head_mat()

        def prep(i, _):
            rows = pl.ds(pl.multiple_of(i * CH, CH), CH)
            c, s1, s2 = c_ref[rows, :], s1_ref[rows, :], s2_ref[rows, :]
            for t_ref, w_ref, out, scale in ((q_ref, qw_ref, qn, HD ** -0.5), (k_ref, kw_ref, kn, 1.0)):
                t = t_ref[rows, :]
                r = lax.rsqrt(_head_mean(t * t, e) + EPS)
                out[rows, :] = _rope(t * r * w_ref[...], c, s1, s2) * scale
            return 0

        lax.fori_loop(0, S // CH, prep, 0, unroll=4)

        def group(gi, d):
            L = S // d

            ru, nb = _interleave(d)

            def stage(r, off):
                for c0 in range(0, L, CH):
                    n = min(CH, L)
                    rows = _sub_rows(r, d, c0, n)
                    dst = pl.ds(off + c0, n)
                    qs[dst, :] = qn[rows, :].astype(BF16)
                    ks[dst, :] = kn[rows, :].astype(BF16)
                    vs[dst, :] = v_ref[rows, :].astype(BF16)

            def one(off, i):
                W, q0, k0, valid = _band_window(i, L)
                q2 = _stack_heads(qs[pl.ds(off + q0, TQ), :], lo)
                sc = jnp.where(valid, _dot_nt(q2, ks[pl.ds(off + k0, W), :]), NEG_INF)
                m = jnp.max(sc, axis=-1, keepdims=True)
                p = jnp.exp(sc - m)
                den = jnp.sum(p, axis=-1, keepdims=True)
                o2 = _dot(p.astype(BF16), vs[pl.ds(off + k0, W), :]) / den
                l2 = jnp.broadcast_to(m + jnp.log(den), (2 * TQ, LANES))
                osub[pl.ds(off + q0, TQ), :] = _unstack_heads(o2, lo)
                lsub[pl.ds(off + q0, TQ), :] = _unstack_heads(l2, lo)

            def unstage(r, off):
                for c0 in range(0, L, CH):
                    n = min(CH, L)
                    rows = _sub_rows(r, d, c0, n)
                    onat[gi, rows, :] = osub[pl.ds(off + c0, n), :]
                    lnat[gi, rows, :] = lsub[pl.ds(off + c0, n), :]

            def step(t, _):
                for u in range(ru):
                    stage(t * ru + u, u * L)
                _for_blocks(L // TQ // nb, lambda j: [one(u * L, j * nb + b) for u in range(ru) for b in range(nb)])
                for u in range(ru):
                    unstage(t * ru + u, u * L)
                return 0

            lax.fori_loop(0, d // ru, step, 0)

        for gi, d in enumerate(DILATIONS):
            pl.when(g == gi)(functools.partial(group, gi, d))

        @pl.when(g == len(DILATIONS) - 1)
        def _():
            def mix(i, _):
                rows = pl.ds(pl.multiple_of(i * CH, CH), CH)
                l0, l1, l2 = lnat[0, rows, :], lnat[1, rows, :], lnat[2, rows, :]
                m = jnp.maximum(jnp.maximum(l0, l1), l2)
                e0, e1, e2 = jnp.exp(l0 - m), jnp.exp(l1 - m), jnp.exp(l2 - m)
                den = e0 + e1 + e2
                a = (e0 * onat[0, rows, :] + e1 * onat[1, rows, :] + e2 * onat[2, rows, :]) / den
                at_ref[rows, :] = a.astype(BF16)
                ls_ref[rows, :] = m + jnp.log(den)
                return 0

            lax.fori_loop(0, S // CH, mix, 0)

    out_spec = pl.BlockSpec((S, LANES), lambda hp, g: (0, hp))
    return _call(
        body, sides, name="attn_fwd", grid=(4, 3),
        in_specs=_qk_specs() + _tab_specs() + [_vec_spec(), _vec_spec()],
        out_specs=[out_spec, out_spec],
        out_shape=[jax.ShapeDtypeStruct((S, CC), BF16), jax.ShapeDtypeStruct((S, CC), F32)],
        scratch_shapes=[pltpu.VMEM((S, LANES), BF16)] * 3 + [pltpu.VMEM((S, LANES), F32)] * 2
        + [pltpu.VMEM((3, S, LANES), F32)] * 2 + [pltpu.VMEM((S, LANES), F32)] * 2,
        args=(proj, proj, proj, *tabs, qw2, kw2))


def attn_bwd(proj, tabs, qw2, kw2, d_attn, attn, lse, sides=()):
    CH = 256

    def body(q_ref, k_ref, v_ref, c_ref, s1_ref, s2_ref, qw_ref, kw_ref, do_ref, at_ref, ls_ref,
             dq_ref, dk_ref, dv_ref, gqw_ref, gkw_ref,
             qs, ks, vs, dos, dsub, lsub, dqs, dks, dvs, dnat, qx, kx, dvn, tnq, tnk, rrq, rrk):
        hp, g = pl.program_id(0), pl.program_id(1)
        lo = lax.broadcasted_iota(jnp.int32, (1, LANES), 1) < HD
        e = _head_mat()
        both = ((q_ref, qw_ref, qx, tnq, rrq, HD ** -0.5), (k_ref, kw_ref, kx, tnk, rrk, 1.0))

        @pl.when((hp == 0) & (g == 0))
        def _():
            gqw_ref[...] = jnp.zeros_like(gqw_ref)
            gkw_ref[...] = jnp.zeros_like(gkw_ref)

        def prep(i, _):
            rows = pl.ds(pl.multiple_of(i * CH, CH), CH)
            dnat[rows, :] = _head_mean(do_ref[rows, :] * at_ref[rows, :].astype(F32), e) * float(HD)
            c, s1, s2 = c_ref[rows, :], s1_ref[rows, :], s2_ref[rows, :]
            for t_ref, w_ref, x, tn_s, rr_s, scale in both:
                t = t_ref[rows, :]
                rr = lax.rsqrt(_head_mean(t * t, e) + EPS)
                tn = t * rr
                rr_s[rows, :] = rr
                tn_s[rows, :] = tn
                x[rows, :] = _rope(tn * w_ref[...], c, s1, s2) * scale
            return 0

        lax.fori_loop(0, S // CH, prep, 0, unroll=4)

        def group(d):
            L = S // d

            ru, nb = _interleave(d)

            def stage(r, off):
                for c0 in range(0, L, CH):
                    n = min(CH, L)
                    rows = _sub_rows(r, d, c0, n)
                    dst = pl.ds(off + c0, n)
                    qs[dst, :] = qx[rows, :].astype(BF16)
                    ks[dst, :] = kx[rows, :].astype(BF16)
                    vs[dst, :] = v_ref[rows, :].astype(BF16)
                    dos[dst, :] = do_ref[rows, :].astype(BF16)
                    dsub[dst, :] = dnat[rows, :]
                    lsub[dst, :] = ls_ref[rows, :]
                    dks[dst, :] = jnp.zeros((n, LANES), F32)
                    dvs[dst, :] = jnp.zeros((n, LANES), F32)

            def one(off, i):
                W, q0, k0, valid = _band_window(i, L)
                qrows, krows = pl.ds(off + q0, TQ), pl.ds(off + k0, W)
                q2 = _stack_heads(qs[qrows, :], lo)
                do2 = _stack_heads(dos[qrows, :], lo)
                kk, vv = ks[krows, :], vs[krows, :]
                lse_b, dd_b = lsub[qrows, :], dsub[qrows, :]
                lse2 = jnp.concatenate([lse_b[:, 0:1], lse_b[:, HD:HD + 1]], axis=0)
                dd2 = jnp.concatenate([dd_b[:, 0:1], dd_b[:, HD:HD + 1]], axis=0)
                sc = jnp.where(valid, _dot_nt(q2, kk), NEG_INF)
                p = jnp.exp(sc - lse2)
                ds = (p * (_dot_nt(do2, vv) - dd2)).astype(BF16)
                dqs[qrows, :] = _unstack_heads(_dot(ds, kk), lo)
                dks[krows, :] = dks[krows, :] + _dot_tn(ds, q2)
                dvs[krows, :] = dvs[krows, :] + _dot_tn(p.astype(BF16), do2)

            def unstage(r, off):
                for c0 in range(0, L, CH):
                    n = min(CH, L)
                    rows = _sub_rows(r, d, c0, n)
                    src = pl.ds(off + c0, n)
                    qx[rows, :] = dqs[src, :]
                    kx[rows, :] = dks[src, :]
                    dvn[rows, :] = dvs[src, :]

            def step(t, _):
                for u in range(ru):
                    stage(t * ru + u, u * L)
                _for_blocks(L // TQ // nb, lambda j: [one(u * L, j * nb + b) for u in range(ru) for b in range(nb)])
                for u in range(ru):
                    unstage(t * ru + u, u * L)
                return 0

            lax.fori_loop(0, d // ru, step, 0)

        for gi, d in enumerate(DILATIONS):
            pl.when(g == gi)(functools.partial(group, d))

        def emit(i, _):
            rows = pl.ds(pl.multiple_of(i * CH, CH), CH)
            c, s1, s2 = c_ref[rows, :], s1_ref[rows, :], s2_ref[rows, :]
            for (_, w_ref, x, tn_s, rr_s, scale), out, gw_ref in zip(both, (dq_ref, dk_ref), (gqw_ref, gkw_ref)):
                tn = tn_s[rows, :]
                dy = _rope_t(x[rows, :] * scale, c, s1, s2)
                gw_ref[0:1, :] = gw_ref[0:1, :] + jnp.sum(dy * tn, axis=0, keepdims=True)
                dtn = dy * w_ref[...]
                out[rows, :] = (rr_s[rows, :] * (dtn - tn * _head_mean(dtn * tn, e))).astype(BF16)
            dv_ref[rows, :] = dvn[rows, :].astype(BF16)
            return 0

        lax.fori_loop(0, S // CH, emit, 0, unroll=4)

    nat_spec = pl.BlockSpec((S, LANES), lambda hp, g: (0, hp))
    out_spec = pl.BlockSpec((None, S, LANES), lambda hp, g: (g, 0, hp))
    acc_spec = pl.BlockSpec((8, LANES), lambda hp, g: (0, 0))
    return _call(
        body, sides, name="attn_bwd", grid=(4, 3),
        in_specs=_qk_specs() + _tab_specs() + [_vec_spec(), _vec_spec(), nat_spec, nat_spec, nat_spec],
        out_specs=[out_spec] * 3 + [acc_spec] * 2,
        out_shape=[jax.ShapeDtypeStruct((QKV // PLANE, S, PLANE), BF16)] * 3 + [jax.ShapeDtypeStruct((8, LANES), F32)] * 2,
        scratch_shapes=[pltpu.VMEM((S, LANES), BF16)] * 4 + [pltpu.VMEM((S, LANES), F32)] * 13,
        args=(proj, proj, proj, *tabs, qw2, kw2, d_attn, attn, lse))


PADR = 16
CT = 128


def _conv_specs():
    return [pl.BlockSpec((S, CC), lambda i: (0, OFF_CA // CC)), pl.BlockSpec((S, CC), lambda i: (0, OFF_CB // CC))]


NCB = CC // LANES


def _pad_zero(pad):
    for cb in range(NCB):
        pad[cb, 0:PADR, :] = jnp.zeros((PADR, LANES), F32)
        pad[cb, PADR + S:PADR + S + PADR, :] = jnp.zeros((PADR, LANES), F32)


def _pad_store(pad, row0, n, val):
    for cb in range(NCB):
        pad[cb, pl.ds(pl.multiple_of(row0 + PADR, 8), n), :] = val[:, cb * LANES:(cb + 1) * LANES]


def _taps(pad_ref, cb, s0, weights):
    acc = jnp.zeros((CT, LANES), F32)
    for k in range(KW):
        acc = acc + weights[k] * pad_ref[cb, pl.ds(s0 + k + 1, CT), :]
    return acc


def conv_fwd(proj, conv_w, conv_b, ln_w, ln_b, sides=()):
    def body(a_ref, b_ref, w_ref, cb_ref, lw_ref, lb_ref, c_ref, u3_ref, upad):
        _pad_zero(upad)

        def glu(i, _):
            rows = pl.ds(pl.multiple_of(i * TM, TM), TM)
            _pad_store(upad, i * TM, TM, a_ref[rows, :] * _sigmoid(b_ref[rows, :]))
            return 0

        lax.fori_loop(0, S // TM, glu, 0)

        def chunk(i, _):
            s0 = pl.multiple_of(i * CT, CT)
            for cb in range(CC // LANES):
                cols = slice(cb * LANES, (cb + 1) * LANES)
                w = [w_ref[k:k + 1, cols] for k in range(KW)]
                c_ref[pl.ds(s0, CT), cols] = _taps(upad, cb, s0, w) + cb_ref[:, cols]
            cv = c_ref[pl.ds(s0, CT), :]
            mu = jnp.mean(cv, axis=-1, keepdims=True)
            xc = cv - mu
            rstd = lax.rsqrt(jnp.mean(xc * xc, axis=-1, keepdims=True) + EPS)
            yl = xc * rstd * lw_ref[...] + lb_ref[...]
            u3_ref[pl.ds(s0, CT), :] = (yl * _sigmoid(yl)).astype(BF16)
            return 0

        lax.fori_loop(0, S // CT, chunk, 0)

    vec = pl.BlockSpec((1, CC), lambda i: (0, 0))
    full = pl.BlockSpec((S, CC), lambda i: (0, 0))
    return _call(
        body, sides, name="conv_fwd", grid=(1,),
        in_specs=_conv_specs() + [pl.BlockSpec((KW, CC), lambda i: (0, 0)), vec, vec, vec],
        out_specs=[full, full],
        out_shape=[jax.ShapeDtypeStruct((S, CC), F32), jax.ShapeDtypeStruct((S, CC), BF16)],
        scratch_shapes=[pltpu.VMEM((NCB, S + 2 * PADR, LANES), F32)],
        args=(proj, proj, conv_w, conv_b, ln_w, ln_b))


def conv_bwd(proj, cpre, d_u3, conv_w, conv_w_rev, ln_w, ln_b, sides=()):
    def body(a_ref, b_ref, c_ref, du3_ref, w_ref, wr_ref, lw_ref, lb_ref,
             dc_ref, gw_ref, gcb_ref, glw_ref, glb_ref, upad, dpad):
        _pad_zero(upad)
        _pad_zero(dpad)
        gw_ref[...] = jnp.zeros_like(gw_ref)

        def ln_bwd(i, carry):
            gcb, glw, glb = carry
            rows = pl.ds(pl.multiple_of(i * TM, TM), TM)
            _pad_store(upad, i * TM, TM, a_ref[rows, :] * _sigmoid(b_ref[rows, :]))
            cv = c_ref[rows, :]
            mu = jnp.mean(cv, axis=-1, keepdims=True)
            xc = cv - mu
            rstd = lax.rsqrt(jnp.mean(xc * xc, axis=-1, keepdims=True) + EPS)
            xh = xc * rstd
            yl = xh * lw_ref[...] + lb_ref[...]
            dyl = du3_ref[rows, :] * _dsilu(yl, _sigmoid(yl))
            dxh = dyl * lw_ref[...]
            dcv = rstd * (dxh - jnp.mean(dxh, axis=-1, keepdims=True)
                          - xh * jnp.mean(dxh * xh, axis=-1, keepdims=True))
            _pad_store(dpad, i * TM, TM, dcv)
            return (gcb + jnp.sum(dcv, axis=0, keepdims=True),
                    glw + jnp.sum(dyl * xh, axis=0, keepdims=True),
                    glb + jnp.sum(dyl, axis=0, keepdims=True))

        z = jnp.zeros((1, CC), F32)
        gcb, glw, glb = lax.fori_loop(0, S // TM, ln_bwd, (z, z, z))
        gcb_ref[...] = gcb
        glw_ref[...] = glw
        glb_ref[...] = glb

        def chunk(i, _):
            s0 = pl.multiple_of(i * CT, CT)
            for cb in range(CC // LANES):
                cols = slice(cb * LANES, (cb + 1) * LANES)
                wr = [wr_ref[k:k + 1, cols] for k in range(KW)]
                du = _taps(dpad, cb, s0, wr)
                dcv = dpad[cb, pl.ds(s0 + PADR, CT), :]
                for k in range(KW):
                    gw_ref[k:k + 1, cols] = gw_ref[k:k + 1, cols] + jnp.sum(
                        upad[cb, pl.ds(s0 + k + 1, CT), :] * dcv, axis=0, keepdims=True)
                av = a_ref[pl.ds(s0, CT), cols]
                sb = _sigmoid(b_ref[pl.ds(s0, CT), cols])
                dc_ref[0, pl.ds(s0, CT), cols] = (du * sb).astype(BF16)
                dc_ref[1, pl.ds(s0, CT), cols] = (du * av * sb * (1.0 - sb)).astype(BF16)
            return 0

        lax.fori_loop(0, S // CT, chunk, 0)

    vec = pl.BlockSpec((1, CC), lambda i: (0, 0))
    full = pl.BlockSpec((S, CC), lambda i: (0, 0))
    wsp = pl.BlockSpec((KW, CC), lambda i: (0, 0))
    return _call(
        body, sides, name="conv_bwd", grid=(1,),
        in_specs=_conv_specs() + [full, full, wsp, wsp, vec, vec],
        out_specs=[pl.BlockSpec((2, S, CC), lambda i: (0, 0, 0)), wsp, vec, vec, vec],
        out_shape=[jax.ShapeDtypeStruct((2, S, CC), BF16), jax.ShapeDtypeStruct((KW, CC), F32)]
        + [jax.ShapeDtypeStruct((1, CC), F32)] * 3,
        scratch_shapes=[pltpu.VMEM((NCB, S + 2 * PADR, LANES), F32)] * 2,
        args=(proj, proj, cpre, d_u3, conv_w, conv_w_rev, ln_w, ln_b))


def _gate_specs():
    return [_row(CC, col=OFF_GA // CC + j) for j in range(4)]


def _gates(g_refs, bg_ref):
    ga = _sigmoid(jnp.concatenate([g_refs[0][...], g_refs[1][...]], axis=1) + bg_ref[0:1, :])
    gb = _sigmoid(jnp.concatenate([g_refs[2][...], g_refs[3][...]], axis=1) + bg_ref[1:2, :])
    return ga, gb


def mix_out(x, proj, b_gate, attn, u3, w_o, w_pw, w_out):
    def body(x_ref, g0, g1, g2, g3, bg_ref, at_ref, u3_ref, wo_ref, wp_ref, wout_ref,
             x1_ref, z_ref, ya_ref, yb_ref):
        ga, gb = _gates((g0, g1, g2, g3), bg_ref)
        ya = _dot_nt(at_ref[...], wo_ref[...])
        yb = _dot_nt(u3_ref[...], wp_ref[...])
        z = (ga * ya + gb * yb).astype(BF16)
        ya_ref[...] = ya.astype(BF16)
        yb_ref[...] = yb.astype(BF16)
        z_ref[...] = z
        x1_ref[...] = x_ref[...] + _dot(z, wout_ref[...])

    return pl.pallas_call(
        body, name="mix_out", grid=(S // TM,),
        in_specs=[_row(D)] + _gate_specs() + [_res((2, D)), _row(CC), _row(CC),
                                              _res((D, CC)), _res((D, CC)), _res((D, D))],
        out_specs=[_row(D)] * 4,
        out_shape=[jax.ShapeDtypeStruct((S, D), F32)] + [jax.ShapeDtypeStruct((S, D), BF16)] * 3,
        compiler_params=_cp(dimension_semantics=("arbitrary",)),
    )(x, proj, proj, proj, proj, b_gate, attn, u3, w_o, w_pw, w_out)


def out_bwd(d_x1b, proj, b_gate, ya, yb, w_o, w_pw, w_out, sides=()):
    def body(dx_ref, g0, g1, g2, g3, bg_ref, ya_ref, yb_ref, wo_ref, wp_ref, wout_ref,
             dya_ref, dyb_ref, dgl_ref, dat_ref, du3_ref, gbg_ref):
        @pl.when(pl.program_id(0) == 0)
        def _():
            gbg_ref[...] = jnp.zeros_like(gbg_ref)

        ga, gb = _gates((g0, g1, g2, g3), bg_ref)
        dz = _dot_nt(dx_ref[...], wout_ref[...])
        dya = (dz * ga).astype(BF16)
        dyb = (dz * gb).astype(BF16)
        dgla = dz * ya_ref[...].astype(F32) * ga * (1.0 - ga)
        dglb = dz * yb_ref[...].astype(F32) * gb * (1.0 - gb)
        dya_ref[...] = dya
        dyb_ref[...] = dyb
        for j in range(2):
            dgl_ref[j] = dgla[:, j * PLANE:(j + 1) * PLANE].astype(BF16)
            dgl_ref[2 + j] = dglb[:, j * PLANE:(j + 1) * PLANE].astype(BF16)
        gbg_ref[0:1, :] = gbg_ref[0:1, :] + jnp.sum(dgla, axis=0, keepdims=True)
        gbg_ref[1:2, :] = gbg_ref[1:2, :] + jnp.sum(dglb, axis=0, keepdims=True)
        dat_ref[...] = _dot(dya, wo_ref[...])
        du3_ref[...] = _dot(dyb, wp_ref[...])

    return _call(
        body, sides, name="out_bwd", grid=(S // TM,),
        in_specs=[_row(D)] + _gate_specs() + [_res((2, D)), _row(D), _row(D),
                                              _res((D, CC)), _res((D, CC)), _res((D, D))],
        out_specs=[_row(D), _row(D), _planes(2 * D), _row(CC), _row(CC), pl.BlockSpec((2, D), lambda i: (0, 0))],
        out_shape=[jax.ShapeDtypeStruct((S, D), BF16)] * 2 + [jax.ShapeDtypeStruct((2 * D // PLANE, S, PLANE), BF16)]
        + [jax.ShapeDtypeStruct((S, CC), F32)] * 2 + [jax.ShapeDtypeStruct((2, D), F32)],
        args=(d_x1b, proj, proj, proj, proj, b_gate, ya, yb, w_o, w_pw, w_out))


def ffn_in(x1, norm_w, w_ffn_in, sides=()):
    half = FF // 2

    def body(x_ref, nw_ref, w_ref, h_ref, gu_ref, f_ref):
        xv = x_ref[...]
        r = lax.rsqrt(jnp.mean(xv * xv, axis=-1, keepdims=True) + EPS)
        h = (xv * r * nw_ref[...]).astype(BF16)
        h_ref[...] = h
        for j in range(2):
            gt = _dot_nt(h, w_ref[j * half:(j + 1) * half, :])
            up = _dot_nt(h, w_ref[FF + j * half:FF + (j + 1) * half, :])
            gu_ref[:, j * half:(j + 1) * half] = gt.astype(BF16)
            gu_ref[:, FF + j * half:FF + (j + 1) * half] = up.astype(BF16)
            f_ref[:, j * half:(j + 1) * half] = (gt * _sigmoid(gt) * up).astype(BF16)

    return _call(
        body, sides, name="ffn_in", grid=(S // TM,),
        in_specs=[_row(D), _res((1, D)), _res((2 * FF, D))],
        out_specs=[_row(D), _row(2 * FF), _row(FF)],
        out_shape=[jax.ShapeDtypeStruct((S, D), BF16), jax.ShapeDtypeStruct((S, 2 * FF), BF16),
                   jax.ShapeDtypeStruct((S, FF), BF16)],
        args=(x1, norm_w, w_ffn_in))


def ffn_out_loss(x1, f, w_ffn_out, target):
    def body(x_ref, f_ref, w_ref, t_ref, dy_ref, dyb_ref, sq_ref):
        @pl.when(pl.program_id(0) == 0)
        def _():
            sq_ref[...] = jnp.zeros_like(sq_ref)

        diff = x_ref[...] + _dot(f_ref[...], w_ref[...]) - t_ref[...]
        dy = diff * (1.0 / D)
        dy_ref[...] = dy
        dyb_ref[...] = dy.astype(BF16)
        sq_ref[...] = sq_ref[...] + jnp.sum((diff * diff).reshape(TM // 8, 8, D), axis=0)

    return pl.pallas_call(
        body, name="ffn_out_loss", grid=(S // TM,),
        in_specs=[_row(D), _row(FF), _res((FF, D)), _row(D)],
        out_specs=[_row(D), _row(D), pl.BlockSpec((8, D), lambda i: (0, 0))],
        out_shape=[jax.ShapeDtypeStruct((S, D), F32), jax.ShapeDtypeStruct((S, D), BF16),
                   jax.ShapeDtypeStruct((8, D), F32)],
        compiler_params=_cp(dimension_semantics=("arbitrary",)),
    )(x1, f, w_ffn_out, target)


def _rms_bwd(xv, nw, dh):
    r = lax.rsqrt(jnp.mean(xv * xv, axis=-1, keepdims=True) + EPS)
    xn = xv * r
    dxn = dh * nw
    dx = r * (dxn - xn * jnp.mean(dxn * xn, axis=-1, keepdims=True))
    return dx, dh * xn


def ffn_bwd(dy, dyb, gu, x1, norm_w, w_ffn_in, w_ffn_out, sides=()):
    def body(dy_ref, dyb_ref, gu_ref, x_ref, nw_ref, wi_ref, wo_ref, dgu_ref, dx_ref, dxb_ref, gn_ref):
        @pl.when(pl.program_id(0) == 0)
        def _():
            gn_ref[...] = jnp.zeros_like(gn_ref)

        df = _dot_nt(dyb_ref[...], wo_ref[...])
        gt = gu_ref[:, 0:FF].astype(F32)
        up = gu_ref[:, FF:2 * FF].astype(F32)
        sg = _sigmoid(gt)
        dgt = (df * up * _dsilu(gt, sg)).astype(BF16)
        dup = (df * gt * sg).astype(BF16)
        dgu_ref[:, 0:FF] = dgt
        dgu_ref[:, FF:2 * FF] = dup
        dh = _dot(dgt, wi_ref[0:FF, :]) + _dot(dup, wi_ref[FF:2 * FF, :])
        dxn, gw = _rms_bwd(x_ref[...], nw_ref[...], dh)
        dx = dy_ref[...] + dxn
        dx_ref[...] = dx
        dxb_ref[...] = dx.astype(BF16)
        gn_ref[...] = gn_ref[...] + jnp.sum(gw, axis=0, keepdims=True)

    return _call(
        body, sides, name="ffn_bwd", grid=(S // TM,),
        in_specs=[_row(D), _row(D), _row(2 * FF), _row(D), _res((1, D)), _res((2 * FF, D)), _res((FF, D))],
        out_specs=[_row(2 * FF), _row(D), _row(D), pl.BlockSpec((1, D), lambda i: (0, 0))],
        out_shape=[jax.ShapeDtypeStruct((S, 2 * FF), BF16), jax.ShapeDtypeStruct((S, D), F32),
                   jax.ShapeDtypeStruct((S, D), BF16), jax.ShapeDtypeStruct((1, D), F32)],
        args=(dy, dyb, gu, x1, norm_w, w_ffn_in, w_ffn_out))


def in_bwd(d_q, d_k, d_v, d_conv, d_gl, w_in, x, d_x1, norm_w, sides=()):
    segs = ((OFF_Q, QKV), (OFF_K, QKV), (OFF_V, QKV), (OFF_CA, 2 * CC), (OFF_GA, 2 * D))

    def body(dq_ref, dk_ref, dv_ref, dc_ref, dg_ref, w_ref, x_ref, dx1_ref, nw_ref, gx_ref, gn_ref):
        @pl.when(pl.program_id(0) == 0)
        def _():
            gn_ref[...] = jnp.zeros_like(gn_ref)

        dh = jnp.zeros((TM, D), F32)
        for ref, (off, width) in zip((dq_ref, dk_ref, dv_ref, dc_ref, dg_ref), segs):
            for j in range(width // PLANE):
                dh = dh + _dot(ref[j], w_ref[off + j * PLANE:off + (j + 1) * PLANE, :])
        dxn, gw = _rms_bwd(x_ref[...], nw_ref[...], dh)
        gx_ref[...] = dx1_ref[...] + dxn
        gn_ref[...] = gn_ref[...] + jnp.sum(gw, axis=0, keepdims=True)

    return _call(
        body, sides, name="in_bwd", grid=(S // TM,),
        in_specs=[_planes(QKV)] * 3 + [_planes(2 * CC), _planes(2 * D), _res((INW, D)), _row(D), _row(D), _res((1, D))],
        out_specs=[_row(D), pl.BlockSpec((1, D), lambda i: (0, 0))],
        out_shape=[jax.ShapeDtypeStruct((S, D), F32), jax.ShapeDtypeStruct((1, D), F32)],
        args=(d_q, d_k, d_v, d_conv, d_gl, w_in, x, d_x1, norm_w))


def mm_tn(name, a, b, tm, tn, sides=()):
    M, N = a.shape[1], b.shape[1]

    def body(a_ref, b_ref, o_ref):
        o_ref[...] = _dot_tn(a_ref[...], b_ref[...])

    res = _call(
        body, sides, name=name, grid=(M // tm, N // tn),
        in_specs=[pl.BlockSpec((S, tm), lambda i, j: (0, i)), pl.BlockSpec((S, tn), lambda i, j: (0, j))],
        out_specs=[pl.BlockSpec((tm, tn), lambda i, j: (i, j))],
        out_shape=[jax.ShapeDtypeStruct((M, N), F32)],
        args=(a, b))
    return (res[0][0], res[1]) if sides else res[0]


GW_IN_TN = PLANE
GW_IN_SPLIT = (768, 256)


def gw_in_t(name, ht, d_segs, col0, hw, sides=()):
    tn = GW_IN_TN
    starts, t0 = [], 0
    for seg in d_segs:
        starts.append(t0)
        t0 += seg.shape[0]
    ntiles = [seg.shape[0] for seg in d_segs]

    def body(h_ref, *refs):
        a_refs, o_ref = refs[:-1], refs[-1]
        n = pl.program_id(0)
        for a_ref, st, nt in zip(a_refs, starts, ntiles):
            @pl.when((n >= st) & (n < st + nt))
            def _(a_ref=a_ref):
                o_ref[...] = _dot(h_ref[...], a_ref[...]).T

    def seg_spec(st, nt):
        return pl.BlockSpec((None, S, tn), lambda n: (jnp.clip(n - st, 0, nt - 1), 0, 0))

    res = _call(
        body, sides, name=name, grid=(INW // tn,),
        in_specs=[pl.BlockSpec((hw, S), lambda n: (col0 // hw, 0))] + [seg_spec(st, nt) for st, nt in zip(starts, ntiles)],
        out_specs=[pl.BlockSpec((tn, hw), lambda n: (n, 0))],
        out_shape=[jax.ShapeDtypeStruct((INW, hw), F32)],
        args=(ht, *d_segs))
    return (res[0][0], res[1]) if sides else res[0]


def _place():
    x, y, c = lax.axis_index("x"), lax.axis_index("y"), lax.axis_index("c")
    chips = [(1 - x, y), (x, 1 - y), (1 - x, 1 - y)]
    return x, y, c, chips


def _sems(n):
    return pltpu.SemaphoreType.DMA((n,))


def _remote(src, dst, send, recv, k, to):
    return pltpu.make_async_remote_copy(src_ref=src, dst_ref=dst, send_sem=send.at[k], recv_sem=recv.at[k],
                                        device_id=to, device_id_type=MESH)


def _cast_rows(dst, src, cols=slice(None)):
    rows = src.shape[0]
    step = next((s for s in (128, 64, 32, 16) if rows % s == 0), rows)
    for r0 in range(0, rows, step):
        dst[r0:r0 + step, cols] = src[r0:r0 + step, :].astype(dst.dtype)


def comm_only(name, sides):
    def body():
        pass

    return _call(body, sides, name=name, grid=(1,), in_specs=[], out_specs=[], out_shape=[], args=())[1]


def ag_blocks(shard, dtype):
    R, W = shard.shape

    def copy(outs, scr, k, block, to, src=None):
        dst = outs[0].at[block]
        return _remote(dst if src is None else src, dst, scr[1], scr[2], k, to)

    def local(outs, scr, me):
        return pltpu.make_async_copy(scr[0], outs[0].at[me], scr[3].at[0])

    def start(ins, outs, scr):
        x, y, c, chips = _place()
        me = 4 * x + 2 * y + c
        _cast_rows(scr[0], ins[0])
        local(outs, scr, me).start()
        copy(outs, scr, 0, me, (x, y, 1 - c), src=scr[0]).start()
        for j, (cx, cy) in enumerate(chips):
            copy(outs, scr, 1 + j, me, (cx, cy, c), src=scr[0]).start()

    def finish(ins, outs, scr):
        x, y, c, chips = _place()
        me, sib = 4 * x + 2 * y + c, (x, y, 1 - c)
        passed = []
        for j, (cx, cy) in enumerate(chips):
            theirs = 4 * cx + 2 * cy + c
            copy(outs, scr, 1 + j, theirs, (x, y, c)).wait_recv()
            fwd = copy(outs, scr, 4 + j, theirs, sib)
            fwd.start()
            passed.append(fwd)
        copy(outs, scr, 0, 4 * x + 2 * y + 1 - c, (x, y, c)).wait_recv()
        for j, (cx, cy) in enumerate(chips):
            copy(outs, scr, 4 + j, 4 * cx + 2 * cy + 1 - c, (x, y, c)).wait_recv()
        copy(outs, scr, 0, me, sib, src=scr[0]).wait_send()
        for j, (cx, cy) in enumerate(chips):
            copy(outs, scr, 1 + j, me, (cx, cy, c), src=scr[0]).wait_send()
        for fwd in passed:
            fwd.wait_send()
        local(outs, scr, me).wait()

    return Side((shard,), (VMEM,), (jax.ShapeDtypeStruct((NDEV, R, W), dtype),),
                (pltpu.VMEM((R, W), dtype), _sems(7), _sems(7), _sems(1)), start, finish, None, "dsxy")


def ag_blocks_relay(shard, dtype, transpose=False):
    R, W = shard.shape[::-1] if transpose else shard.shape
    half = R // 2

    def copy(outs, scr, k, block, to, src=None, rows=None):
        dst = outs[0].at[block] if rows is None else outs[0].at[block, pl.ds(rows * half, half), :]
        return _remote(dst if src is None else src, dst, scr[1], scr[2], k, to)

    def local(outs, scr, me):
        return pltpu.make_async_copy(scr[0], outs[0].at[me], scr[3].at[0])

    def own(outs, scr):
        x, y, c, _ = _place()
        me = 4 * x + 2 * y + c
        return [copy(outs, scr, k, me, to, src=scr[0])
                for k, to in enumerate([(x, y, 1 - c), (1 - x, y, c), (x, 1 - y, c)])]

    def start(ins, outs, scr):
        x, y, c, _ = _place()
        if transpose:
            scr[0][...] = ins[0][...].T.astype(dtype)
        else:
            _cast_rows(scr[0], ins[0])
        local(outs, scr, 4 * x + 2 * y + c).start()
        for cp in own(outs, scr):
            cp.start()

    def passed_on(outs, scr):
        x, y, c, _ = _place()
        sib, xn, yn = (x, y, 1 - c), (1 - x, y, c), (x, 1 - y, c)
        b_xn, b_yn, b_dg = 4 * (1 - x) + 2 * y + c, 4 * x + 2 * (1 - y) + c, 4 * (1 - x) + 2 * (1 - y) + c
        near = [copy(outs, scr, 5, b_xn, yn, rows=0), copy(outs, scr, 3, b_xn, sib),
                copy(outs, scr, 6, b_yn, xn, rows=1), copy(outs, scr, 4, b_yn, sib)]
        far = [copy(outs, scr, 7, b_dg, sib, rows=0), copy(outs, scr, 8, b_dg, sib, rows=1)]
        return (b_xn, b_yn, b_dg), near, far

    def mid(ins, outs, scr):
        x, y, c, _ = _place()
        (b_xn, b_yn, _), near, _ = passed_on(outs, scr)
        copy(outs, scr, 1, b_xn, (x, y, c)).wait_recv()
        near[0].start()
        near[1].start()
        copy(outs, scr, 2, b_yn, (x, y, c)).wait_recv()
        near[2].start()
        near[3].start()

    def finish(ins, outs, scr):
        x, y, c, _ = _place()
        here = (x, y, c)
        (b_xn, b_yn, b_dg), near, far = passed_on(outs, scr)
        copy(outs, scr, 5, b_dg, here, rows=0).wait_recv()
        far[0].start()
        copy(outs, scr, 6, b_dg, here, rows=1).wait_recv()
        far[1].start()
        flip = 1 - 2 * c
        copy(outs, scr, 0, 4 * x + 2 * y + 1 - c, here).wait_recv()
        copy(outs, scr, 3, b_xn + flip, here).wait_recv()
        copy(outs, scr, 4, b_yn + flip, here).wait_recv()
        copy(outs, scr, 7, b_dg + flip, here, rows=0).wait_recv()
        copy(outs, scr, 8, b_dg + flip, here, rows=1).wait_recv()
        for cp in own(outs, scr) + near + far:
            cp.wait_send()
        local(outs, scr, 4 * x + 2 * y + c).wait()

    return Side((shard,), (VMEM,), (jax.ShapeDtypeStruct((NDEV, R, W), dtype),),
                (pltpu.VMEM((R, W), dtype), _sems(9), _sems(9), _sems(1)), start, finish, mid, "sxy")


def copies_side(args, out_shape, n_copies, plan, peers):
    def copies(ins, outs, scr):
        return [_remote(s_, d_, scr[0], scr[1], i, to) for i, (s_, d_, to) in enumerate(plan(ins, outs))]

    def start(ins, outs, scr):
        for cp in copies(ins, outs, scr):
            cp.start()

    def finish(ins, outs, scr):
        for cp in copies(ins, outs, scr):
            cp.wait()

    return Side(tuple(args), (ANY,) * len(args), tuple(out_shape), (_sems(n_copies), _sems(n_copies)),
                start, finish, None, peers)


def rs_to_sibling(grads):
    out_shape = [jax.ShapeDtypeStruct((4,) + g.shape[1:], F32) for g in grads]

    def plan(ins, outs):
        x, y, c, _ = _place()
        return [(g.at[2 * k + 1 - c], r.at[k], (x, y, 1 - c)) for g, r in zip(ins, outs) for k in range(4)]

    return copies_side(grads, out_shape, 4 * len(grads), plan, "s")


def rs_to_chips(parts):
    out_shape = [jax.ShapeDtypeStruct((3,) + p.shape[1:], BF16) for p in parts]

    def plan(ins, outs):
        x, y, c, chips = _place()
        return [(p.at[2 * cx + cy], r.at[j], (cx, cy, c))
                for p, r in zip(ins, outs) for j, (cx, cy) in enumerate(chips)]

    return copies_side(parts, out_shape, 3 * len(parts), plan, "dxy")


def rs_to_chips_combined(part):
    _, R, W = part.shape
    half = R // 2
    top, bot = pl.ds(0, half), pl.ds(half, half)

    def copies(ins, outs, scr):
        p, r = ins[0], outs[0]
        loc_a, loc_b, in_x, in_y, comb_a, comb_b, send, recv, loc = scr
        x, y, c, _ = _place()
        xn, yn = (1 - x, y, c), (x, 1 - y, c)
        k_xn, k_yn, k_dg = 2 * (1 - x) + y, 2 * x + 1 - y, 2 * (1 - x) + 1 - y
        direct = [_remote(p.at[k_xn, top, :], r.at[0, top, :], send, recv, 0, xn),
                  _remote(p.at[k_yn, bot, :], r.at[1, bot, :], send, recv, 1, yn),
                  _remote(p.at[k_dg, top, :], in_x, send, recv, 2, xn),
                  _remote(p.at[k_dg, bot, :], in_y, send, recv, 3, yn)]
        combined = [_remote(comb_a, r.at[1, top, :], send, recv, 4, yn),
                    _remote(comb_b, r.at[0, bot, :], send, recv, 5, xn)]
        local = [pltpu.make_async_copy(p.at[k_yn, top, :], loc_a, loc.at[0]),
                 pltpu.make_async_copy(p.at[k_xn, bot, :], loc_b, loc.at[1])]
        return direct, combined, local

    def start(ins, outs, scr):
        direct, _, local = copies(ins, outs, scr)
        for cp in local + direct:
            cp.start()

    def mid(ins, outs, scr):
        loc_a, loc_b, in_x, in_y, comb_a, comb_b = scr[:6]
        direct, combined, local = copies(ins, outs, scr)
        for mine, arrival, inbox, out, nxt in ((local[0], direct[2], in_x, comb_a, combined[0]),
                                               (local[1], direct[3], in_y, comb_b, combined[1])):
            mine.wait()
            arrival.wait_recv()
            src = loc_a if out is comb_a else loc_b
            out[...] = (src[...].astype(F32) + inbox[...].astype(F32)).astype(BF16)
            nxt.start()

    def finish(ins, outs, scr):
        direct, combined, _ = copies(ins, outs, scr)
        direct[0].wait_recv()
        direct[1].wait_recv()
        combined[0].wait_recv()
        combined[1].wait_recv()
        for cp in direct + combined:
            cp.wait_send()

    buf = pltpu.VMEM((half, W), BF16)
    return Side((part,), (ANY,), (jax.ShapeDtypeStruct((2, R, W), BF16),),
                (buf, buf, buf, buf, buf, buf, _sems(6), _sems(6), _sems(2)), start, finish, mid, "xy")


ADAM_TILE_BYTES = 3 * 512 * 1024


def _row_tiles(rows, width):
    return 2 if rows % 32 == 0 and rows * width * 4 > ADAM_TILE_BYTES else 1


def chip_sum(name, grad, recv, c_idx, chip_idx):
    _, R, C = grad.shape
    nt = 1
    tr = R // nt

    def body(s_ref, g_ref, r_ref, p_ref, own_ref):
        k = pl.program_id(1)
        tot = g_ref[0] + r_ref[0]
        p_ref[0] = tot.astype(BF16)

        @pl.when(k == s_ref[1])
        def _():
            own_ref[...] = tot

    grid_spec = pltpu.PrefetchScalarGridSpec(
        num_scalar_prefetch=1, grid=(nt, 4),
        in_specs=[pl.BlockSpec((1, tr, C), lambda i, k, s: (2 * k + s[0], i, 0)),
                  pl.BlockSpec((1, tr, C), lambda i, k, s: (k, i, 0))],
        out_specs=[pl.BlockSpec((1, tr, C), lambda i, k, s: (k, i, 0)),
                   pl.BlockSpec((tr, C), lambda i, k, s: (i, 0))])
    return pl.pallas_call(
        body, name=name, grid_spec=grid_spec,
        out_shape=[jax.ShapeDtypeStruct((4, R, C), BF16), jax.ShapeDtypeStruct((R, C), F32)],
        compiler_params=_cp(dimension_semantics=("arbitrary", "arbitrary")),
    )(jnp.stack([c_idx, chip_idx]), grad, recv)


def _adamw(w, g, m, v):
    m2 = ADAM_B1 * m + (1.0 - ADAM_B1) * g
    v2 = ADAM_B2 * v + (1.0 - ADAM_B2) * (g * g)
    m_hat = m2 / (1.0 - ADAM_B1 ** ADAM_STEP)
    v_hat = v2 / (1.0 - ADAM_B2 ** ADAM_STEP)
    delta = -ADAM_LR * (m_hat / (jnp.sqrt(v_hat) + ADAM_EPS) + ADAM_WD * w)
    return delta, m2, v2


def shard_adam(name, owns, recvs, w, m, v, io_t=False):
    n = len(owns)
    R = owns[0].shape[0]
    ct = min(o.shape[1] for o in owns)
    first = [sum(o.shape[1] for o in owns[:j]) // ct for j in range(n)]
    count = [o.shape[1] // ct for o in owns]
    nt = _row_tiles(R, ct)
    tr = R // nt

    def body(*refs):
        o_refs, r_refs = refs[:n], refs[n:2 * n]
        w_ref, m_ref, v_ref, g_ref, d_ref, nm_ref, nv_ref = refs[2 * n:]
        g = None
        for j in range(n):
            gj = o_refs[j][...]
            for q in range(recvs[j].shape[0]):
                gj = gj + r_refs[j][q].astype(F32)
            g = gj if g is None else jnp.where(pl.program_id(0) >= first[j], gj, g)
        t = (lambda a: a.T) if io_t else (lambda a: a)
        delta, m2, v2 = _adamw(t(w_ref[...]), g, t(m_ref[...]), t(v_ref[...]))
        g_ref[...] = t(g)
        d_ref[...] = t(delta)
        nm_ref[...] = t(m2)
        nv_ref[...] = t(v2)

    def part(j):
        return pl.BlockSpec((tr, ct), lambda k, i: (i, jnp.clip(k - first[j], 0, count[j] - 1)))

    def part3(j):
        return pl.BlockSpec((recvs[j].shape[0], tr, ct), lambda k, i: (0, i, jnp.clip(k - first[j], 0, count[j] - 1)))

    C = sum(count) * ct
    tile = pl.BlockSpec((ct, tr), lambda k, i: (k, i)) if io_t else pl.BlockSpec((tr, ct), lambda k, i: (i, k))
    return pl.pallas_call(
        body, name=name, grid=(sum(count), nt),
        in_specs=[part(j) for j in range(n)] + [part3(j) for j in range(n)] + [tile, tile, tile],
        out_specs=[tile] * 4, out_shape=[jax.ShapeDtypeStruct((C, R) if io_t else (R, C), F32)] * 4,
        compiler_params=_cp(dimension_semantics=("arbitrary", "arbitrary")),
    )(*owns, *recvs, w, m, v)


ROW_N1, ROW_N2, ROW_BG, ROW_QN, ROW_KN, ROW_CB, ROW_LW, ROW_LB, ROW_CW = 0, 1, 2, 4, 5, 6, 7, 8, 9
PACK_ROWS = 40
SMALL = ("norm1_w", "norm2_w", "b_gate", "q_norm_w", "k_norm_w", "conv_b", "conv_ln_w", "conv_ln_b", "conv_w")


def small_sync(g, sq, sides=()):
    ns = len(SMALL)

    def copies(refs):
        pack, recv, send_sems, recv_sems = refs[ns + 2:]
        x, y, c, _ = _place()
        return [pltpu.make_async_remote_copy(
            src_ref=pack, dst_ref=recv.at[4 * x + 2 * y + c], send_sem=send_sems.at[k - 1],
            recv_sem=recv_sems.at[k - 1], device_id=(x ^ (k >> 2), y ^ ((k >> 1) & 1), c ^ (k & 1)),
            device_id_type=MESH) for k in range(1, NDEV)]

    def body(*refs):
        gi = dict(zip(SMALL, refs[:ns]))
        sq_ref, tot, pack, recv, send_sems, recv_sems = refs[ns:]
        x, y, c, _ = _place()
        me = 4 * x + 2 * y + c

        pack[...] = jnp.zeros_like(pack)
        pack[ROW_KN:ROW_KN + 1, LANES:2 * LANES] = jnp.full((1, LANES), (0.5 / D) * jnp.sum(sq_ref[...]), F32)
        pack[ROW_N1:ROW_N1 + 1, :] = gi["norm1_w"][...]
        pack[ROW_N2:ROW_N2 + 1, :] = gi["norm2_w"][...]
        pack[ROW_BG:ROW_BG + 2, :] = gi["b_gate"][...]
        pack[ROW_QN:ROW_QN + 1, 0:HD] = gi["q_norm_w"][...]
        pack[ROW_KN:ROW_KN + 1, 0:HD] = gi["k_norm_w"][...]
        pack[ROW_CB:ROW_CB + 1, 0:CC] = gi["conv_b"][...]
        pack[ROW_LW:ROW_LW + 1, 0:CC] = gi["conv_ln_w"][...]
        pack[ROW_LB:ROW_LB + 1, 0:CC] = gi["conv_ln_b"][...]
        pack[ROW_CW:ROW_CW + KW, 0:CC] = gi["conv_w"][...]

        for cp in copies(refs):
            cp.start()
        recv[me] = pack[...]

    def tail(*refs):
        tot, recv = refs[ns + 1], refs[ns + 3]
        for cp in copies(refs):
            cp.wait()
        acc = recv[0]
        for p in range(1, NDEV):
            acc = acc + recv[p]
        tot[...] = acc

    args = [g[k] for k in SMALL] + [sq]
    res = _call(
        body, sides, name="small_sync", grid=(1,), in_specs=[VMEM] * len(args), out_specs=[VMEM],
        out_shape=[jax.ShapeDtypeStruct((PACK_ROWS, D), F32)],
        scratch_shapes=[pltpu.VMEM((PACK_ROWS, D), F32), pltpu.VMEM((NDEV, PACK_ROWS, D), F32),
                        _sems(NDEV - 1), _sems(NDEV - 1)],
        args=args, own_comm=True, tail=tail)
    return (res[0][0], res[1]) if sides else res[0]


def small_adam(tot, w, m, v, me):
    ns = len(SMALL)

    def body(me_ref, tot, *refs):
        wi = dict(zip(SMALL, refs[:ns]))
        mi = dict(zip(SMALL, refs[ns:2 * ns]))
        vi = dict(zip(SMALL, refs[2 * ns:3 * ns]))
        outs = refs[3 * ns:7 * ns]
        loss_ref = refs[7 * ns]
        me = me_ref[0]

        def shard_grad(name):
            if name == "b_gate":
                return tot[ROW_BG:ROW_BG + 2, pl.ds(pl.multiple_of(me * LANES, LANES), LANES)]
            if name == "conv_w":
                win = tot[ROW_CW:ROW_CW + KW, pl.ds(pl.multiple_of((me // 2) * LANES, LANES), LANES)]
                return jnp.where(me % 2 == 1, win[:, HD:LANES], win[:, 0:HD])
            row = {"norm1_w": ROW_N1, "norm2_w": ROW_N2, "q_norm_w": ROW_QN, "k_norm_w": ROW_KN,
                   "conv_b": ROW_CB, "conv_ln_w": ROW_LW, "conv_ln_b": ROW_LB}[name]
            return tot[row:row + 1, 0:wi[name].shape[1]]

        for i, name in enumerate(SMALL):
            gr = shard_grad(name)
            delta, m2, v2 = _adamw(wi[name][...], gr, mi[name][...], vi[name][...])
            outs[4 * i][...] = gr
            outs[4 * i + 1][...] = delta
            outs[4 * i + 2][...] = m2
            outs[4 * i + 3][...] = v2
        loss_ref[...] = tot[ROW_KN:ROW_KN + 1, LANES:2 * LANES]

    out_shape = []
    for name in SMALL:
        out_shape += [jax.ShapeDtypeStruct(w[name].shape, F32)] * 4
    out_shape.append(jax.ShapeDtypeStruct((1, LANES), F32))
    args = [tot] + [w[k] for k in SMALL] + [m[k] for k in SMALL] + [v[k] for k in SMALL]
    grid_spec = pltpu.PrefetchScalarGridSpec(
        num_scalar_prefetch=1, grid=(1,), in_specs=[VMEM] * len(args), out_specs=[VMEM] * len(out_shape))
    res = pl.pallas_call(body, name="small_adam", grid_spec=grid_spec, out_shape=out_shape)(me, *args)
    out = {name: tuple(res[4 * i:4 * i + 4]) for i, name in enumerate(SMALL)}
    return out, res[4 * ns][0, 0]


MATS = ("w_in", "w_o_attn", "w_pw_conv", "w_out", "w_ffn_in", "w_ffn_out")
TRANSPOSED = ("w_in", "w_ffn_in")
WEIGHTS = ("norm1_w", "w_in", "b_gate", "q_norm_w", "k_norm_w", "w_o_attn", "conv_w", "conv_b", "conv_ln_w",
           "conv_ln_b", "w_pw_conv", "w_out", "norm2_w", "w_ffn_in", "w_ffn_out")


def _blocks_to_cols(blocks):
    n, R, C = blocks.shape
    return blocks.transpose(1, 0, 2).reshape(R, n * C)


def kernel(x, positions, norm1_w, w_in, b_gate, q_norm_w, k_norm_w, w_o_attn, conv_w, conv_b, conv_ln_w, conv_ln_b, w_pw_conv, w_out, norm2_w, w_ffn_in, w_ffn_out, loss_target, m_norm1_w, m_w_in, m_b_gate, m_q_norm_w, m_k_norm_w, m_w_o_attn, m_conv_w, m_conv_b, m_conv_ln_w, m_conv_ln_b, m_w_pw_conv, m_w_out, m_norm2_w, m_w_ffn_in, m_w_ffn_out, v_norm1_w, v_w_in, v_b_gate, v_q_norm_w, v_k_norm_w, v_w_o_attn, v_conv_w, v_conv_b, v_conv_ln_w, v_conv_ln_b, v_w_pw_conv, v_w_out, v_norm2_w, v_w_ffn_in, v_w_ffn_out):
    w = dict(norm1_w=norm1_w, w_in=w_in, b_gate=b_gate, q_norm_w=q_norm_w, k_norm_w=k_norm_w, w_o_attn=w_o_attn,
             conv_w=conv_w, conv_b=conv_b, conv_ln_w=conv_ln_w, conv_ln_b=conv_ln_b, w_pw_conv=w_pw_conv,
             w_out=w_out, norm2_w=norm2_w, w_ffn_in=w_ffn_in, w_ffn_out=w_ffn_out)
    m = dict(norm1_w=m_norm1_w, w_in=m_w_in, b_gate=m_b_gate, q_norm_w=m_q_norm_w, k_norm_w=m_k_norm_w,
             w_o_attn=m_w_o_attn, conv_w=m_conv_w, conv_b=m_conv_b, conv_ln_w=m_conv_ln_w,
             conv_ln_b=m_conv_ln_b, w_pw_conv=m_w_pw_conv, w_out=m_w_out, norm2_w=m_norm2_w,
             w_ffn_in=m_w_ffn_in, w_ffn_out=m_w_ffn_out)
    v = dict(norm1_w=v_norm1_w, w_in=v_w_in, b_gate=v_b_gate, q_norm_w=v_q_norm_w, k_norm_w=v_k_norm_w,
             w_o_attn=v_w_o_attn, conv_w=v_conv_w, conv_b=v_conv_b, conv_ln_w=v_conv_ln_w,
             conv_ln_b=v_conv_ln_b, w_pw_conv=v_w_pw_conv, w_out=v_w_out, norm2_w=v_norm2_w,
             w_ffn_in=v_w_ffn_in, w_ffn_out=v_w_ffn_out)
    def two_d(t):
        t = {k: (a[0] if a.ndim == 3 else a) for k, a in t.items()}
        return {k: (a.T if k in TRANSPOSED else a) for k, a in t.items()}

    w, m, v = two_d(w), two_d(m), two_d(v)

    x2, target = x[0], loss_target[0]
    c_idx = lax.axis_index("c").astype(jnp.int32)
    chip_idx = (2 * lax.axis_index("x") + lax.axis_index("y")).astype(jnp.int32)
    qw2 = jnp.tile(w["q_norm_w"], (1, 2))
    kw2 = jnp.tile(w["k_norm_w"], (1, 2))

    ax, ay = lax.axis_index("x"), lax.axis_index("y")
    chip_order = jnp.stack([2 * ax + ay, 2 * (1 - ax) + ay, 2 * ax + 1 - ay, 2 * (1 - ax) + 1 - ay]).astype(jnp.int32)
    h_t, proj, w_in_blocks, tabs = in_proj_gather(x2, w["norm1_w"], w["w_in"], chip_order, positions.reshape(S, 1))
    w_in_t = w_in_blocks.reshape(INW, D)
    (attn, lse), ((w_ffn_in_blocks,), (w_out_blocks,), (w_o_blocks,), (w_pw_blocks,), (bg_blocks,), (cw_blocks,)) = attn_fwd(
        proj, tabs, qw2, kw2, sides=(ag_blocks_relay(w["w_ffn_in"], BF16), ag_blocks_relay(w["w_out"], BF16),
                                     ag_blocks_relay(w["w_o_attn"], BF16, transpose=True),
                                     ag_blocks_relay(w["w_pw_conv"], BF16, transpose=True),
                                     ag_blocks(w["b_gate"], F32), ag_blocks(w["conv_w"], F32)))
    w_ffn_in_t = w_ffn_in_blocks.reshape(2 * FF, D)
    w_out_f = w_out_blocks.reshape(D, D)
    w_o_t, w_pw_t = w_o_blocks.reshape(D, CC), w_pw_blocks.reshape(D, CC)
    b_gate_f, conv_w_f = _blocks_to_cols(bg_blocks), _blocks_to_cols(cw_blocks)
    cpre, u3 = conv_fwd(proj, conv_w_f, w["conv_b"], w["conv_ln_w"], w["conv_ln_b"])
    x1, z, ya, yb = mix_out(x2, proj, b_gate_f, attn, u3, w_o_t, w_pw_t, w_out_f)
    (h2, gu, f), ((w_ffn_out_blocks,),) = ffn_in(x1, w["norm2_w"], w_ffn_in_t, sides=(ag_blocks_relay(w["w_ffn_out"], BF16),))
    w_ffn_out_f = w_ffn_out_blocks.reshape(FF, D)
    dy, dyb, sq = ffn_out_loss(x1, f, w_ffn_out_f, target)

    g = {}
    g_ffn_out = mm_tn("gw_ffn_out", f, dyb, FF // 2, D).reshape(NDEV, FF // NDEV, D)
    (d_gu, d_x1, d_x1b, g["norm2_w"]), ((ra_ffn_out,),) = ffn_bwd(
        dy, dyb, gu, x1, w["norm2_w"], w_ffn_in_t, w_ffn_out_f, sides=(rs_to_sibling([g_ffn_out]),))
    pb_ffn_out, own_ffn_out = chip_sum("chip_sum_w_ffn_out", g_ffn_out, ra_ffn_out, c_idx, chip_idx)
    g_ffn_in = mm_tn("gw_ffn_in", d_gu, h2, FF // 2, D).reshape(NDEV, 2 * FF // NDEV, D)
    g_out = mm_tn("gw_out", z, d_x1b, D // 2, D).reshape(NDEV, D // NDEV, D)
    (d_ya, d_yb, d_gl, d_attn, d_u3, g["b_gate"]), ((ra_ffn_in,),) = out_bwd(
        d_x1b, proj, b_gate_f, ya, yb, w_o_t, w_pw_t, w_out_f, sides=(rs_to_sibling([g_ffn_in]),))
    pb_ffn_in, own_ffn_in = chip_sum("chip_sum_w_ffn_in", g_ffn_in, ra_ffn_in, c_idx, chip_idx)
    g_w_o = mm_tn("gw_o_attn", d_ya, attn, D // 2, CC).reshape(NDEV, D // NDEV, CC)
    g_w_pw = mm_tn("gw_pw_conv", d_yb, u3, D // 2, CC).reshape(NDEV, D // NDEV, CC)
    (d_conv, g["conv_w"], g["conv_b"], g["conv_ln_w"], g["conv_ln_b"]), ((ra_out, ra_w_o, ra_w_pw),) = conv_bwd(
        proj, cpre, d_u3, conv_w_f, conv_w_f[::-1], w["conv_ln_w"], w["conv_ln_b"],
        sides=(rs_to_sibling([g_out, g_w_o, g_w_pw]),))
    pb_out, own_out = chip_sum("chip_sum_w_out", g_out, ra_out, c_idx, chip_idx)
    pb_w_o, own_w_o = chip_sum("chip_sum_w_o_attn", g_w_o, ra_w_o, c_idx, chip_idx)
    pb_w_pw, own_w_pw = chip_sum("chip_sum_w_pw_conv", g_w_pw, ra_w_pw, c_idx, chip_idx)
    (d_q, d_k, d_v, gqw, gkw), ((rb_ffn_out, rb_ffn_in, rb_out, rb_w_o, rb_w_pw),) = attn_bwd(
        proj, tabs, qw2, kw2, d_attn, attn, lse,
        sides=(rs_to_chips([pb_ffn_out, pb_ffn_in, pb_out, pb_w_o, pb_w_pw]),))
    g["q_norm_w"] = gqw[0:1, 0:HD] + gqw[0:1, HD:LANES]
    g["k_norm_w"] = gkw[0:1, 0:HD] + gkw[0:1, HD:LANES]
    d_segs = (d_q, d_k, d_v, d_conv, d_gl)
    parts, to_sibling, to_chips, owns, from_chips = [], None, None, [], []
    for k, hw in enumerate(GW_IN_SPLIT):
        sides = tuple(s for s in (to_chips, to_sibling) if s is not None)
        part = gw_in_t("gw_in_%d" % k, h_t, d_segs, sum(GW_IN_SPLIT[:k]), hw, sides=sides)
        part, outs = part if sides else (part, [])
        outs = list(outs)
        if to_chips is not None:
            from_chips.append(outs.pop(0)[0])
        if to_sibling is not None:
            pb, own = chip_sum("chip_sum_w_in_%d" % (k - 1), parts[-1], outs.pop(0)[0], c_idx, chip_idx)
            owns.append(own)
            to_chips = rs_to_chips_combined(pb)
        else:
            to_chips = None
        parts.append(part.reshape(NDEV, INW // NDEV, hw))
        to_sibling = rs_to_sibling([parts[-1]])
    (grad_x, g["norm1_w"]), ((rb_prev,), (ra_last,)) = in_bwd(
        d_q, d_k, d_v, d_conv, d_gl, w_in_t, x2, d_x1, w["norm1_w"], sides=(to_chips, to_sibling))
    from_chips.append(rb_prev)
    pb, own = chip_sum("chip_sum_w_in_%d" % (len(GW_IN_SPLIT) - 1), parts[-1], ra_last, c_idx, chip_idx)
    owns.append(own)
    small_sums, ((rb_last,),) = small_sync(g, sq, sides=(rs_to_chips_combined(pb),))
    small, loss = small_adam(small_sums, w, m, v, (4 * ax + 2 * ay + c_idx).astype(jnp.int32).reshape(1))
    from_chips.append(rb_last)

    res = {
        "w_in": shard_adam("adam_w_in", owns, from_chips, w["w_in"], m["w_in"], v["w_in"]),
        "w_ffn_in": shard_adam("adam_w_ffn_in", [own_ffn_in], [rb_ffn_in], w["w_ffn_in"], m["w_ffn_in"], v["w_ffn_in"]),
        "w_o_attn": shard_adam("adam_w_o_attn", [own_w_o], [rb_w_o], w["w_o_attn"], m["w_o_attn"], v["w_o_attn"], io_t=True),
        "w_pw_conv": shard_adam("adam_w_pw_conv", [own_w_pw], [rb_w_pw],
                                w["w_pw_conv"], m["w_pw_conv"], v["w_pw_conv"], io_t=True),
        "w_out": shard_adam("adam_w_out", [own_out], [rb_out], w["w_out"], m["w_out"], v["w_out"]),
        "w_ffn_out": shard_adam("adam_w_ffn_out", [own_ffn_out], [rb_ffn_out],
                                w["w_ffn_out"], m["w_ffn_out"], v["w_ffn_out"]),
    }
    res = {k: tuple(a.T if k in TRANSPOSED else a for a in r) for k, r in res.items()}
    res.update(small)

    def shaped(name, a):
        return a.reshape((1,) + a.shape) if name in MATS or name in ("b_gate", "conv_w") else a

    outs = [loss, grad_x.reshape(1, S, D)]
    for i in range(4):
        outs += [shaped(k, res[k][i]) for k in WEIGHTS]
    return tuple(outs)
```

```python
import functools
from typing import Callable, NamedTuple, Optional

import numpy as np
import jax
import jax.numpy as jnp
from jax import lax
from jax.experimental import pallas as pl
from jax.experimental.pallas import tpu as pltpu

F32 = jnp.float32
BF16 = jnp.bfloat16

S = 2048
D = 1024
HD = 64
QKV = 1536
CC = 512
KW = 31
FF = 2816
INW = 7680
OFF_Q, OFF_K, OFF_V, OFF_CA, OFF_CB, OFF_GA, OFF_GB = 0, 1536, 3072, 4608, 5120, 5632, 6656
DILATIONS = (1, 4, 16)
HALF_SPAN = 64
EPS = 1e-6
NEG_INF = -1e30
ROPE_THETA = 500000.0
ROT_DIM = 16

ADAM_LR = 0.001
ADAM_B1 = 0.9
ADAM_B2 = 0.999
ADAM_EPS = 1e-08
ADAM_WD = 0.01
ADAM_STEP = 10

NDEV = 8
LANES = 128
TM = 256
IN_PROJ_TM = 512
TQ = 128
VMEM_LIMIT = 56 * 1024 * 1024
MESH = pl.DeviceIdType.MESH


def _cp(**kw):
    return pltpu.CompilerParams(vmem_limit_bytes=VMEM_LIMIT, **kw)


def _row(width, col=0, tm=TM):
    return pl.BlockSpec((tm, width), lambda i: (i, col))


PLANE = 512


def _planes(width, tm=TM):
    return pl.BlockSpec((width // PLANE, tm, PLANE), lambda i: (0, i, 0))


def _res(shape):
    nd = len(shape)
    return pl.BlockSpec(shape, lambda *_: (0,) * nd, pipeline_mode=pl.Buffered(1))


def _dot(a, b):
    return jnp.dot(a, b, preferred_element_type=F32)


def _dot_nt(a, b):
    return lax.dot_general(a, b, (((1,), (1,)), ((), ())), preferred_element_type=F32)


def _dot_tn(a, b):
    return lax.dot_general(a, b, (((0,), (0,)), ((), ())), preferred_element_type=F32)


def _sigmoid(x):
    return jax.nn.sigmoid(x)


def _dsilu(x, sg):
    return sg * (1.0 + x * (1.0 - sg))


ANY = pl.BlockSpec(memory_space=pl.ANY)
VMEM = pl.BlockSpec(memory_space=pltpu.VMEM)


class Side(NamedTuple):
    args: tuple
    in_specs: tuple
    out_shape: tuple
    scratch: tuple
    start: Callable
    finish: Callable
    mid: Optional[Callable] = None
    peers: str = ""


BARRIER_IDS = {"s": 0, "dxy": 1, "dsxy": 2, "sxy": 3, "xy": 4}


def _peer_barrier(peers):
    x, y, c = lax.axis_index("x"), lax.axis_index("y"), lax.axis_index("c")
    where = {"s": (x, y, 1 - c), "x": (1 - x, y, c), "y": (x, 1 - y, c), "d": (1 - x, 1 - y, c)}
    barrier = pltpu.get_barrier_semaphore()
    for p in peers:
        pl.semaphore_signal(barrier, inc=1, device_id=where[p], device_id_type=MESH)
    pl.semaphore_wait(barrier, len(peers))


def _call(body, sides=(), *, name, grid, in_specs, out_specs, out_shape, scratch_shapes=(), args, own_comm=False,
          tail=None):
    assert tail is None or int(np.prod(grid)) == 1
    ni, no, ns = len(in_specs), len(out_specs), len(scratch_shapes)
    cnt = [(len(s.args), len(s.out_shape), len(s.scratch)) for s in sides]
    peers = "".join(sorted(set("".join(s.peers for s in sides))))
    if own_comm or not sides or any(not s.peers for s in sides):
        peers = ""

    def take(refs, pos, n):
        return refs[pos:pos + n], pos + n

    def full(*refs):
        m_in, pos = take(refs, 0, ni)
        s_in = []
        for a, _, _ in cnt:
            r, pos = take(refs, pos, a)
            s_in.append(r)
        m_out, pos = take(refs, pos, no)
        s_out = []
        for _, o, _ in cnt:
            r, pos = take(refs, pos, o)
            s_out.append(r)
        m_scr, pos = take(refs, pos, ns)
        s_scr = []
        for _, _, c in cnt:
            r, pos = take(refs, pos, c)
            s_scr.append(r)
        if sides:
            first = functools.reduce(jnp.logical_and, [pl.program_id(d) == 0 for d in range(len(grid))])
            last = functools.reduce(jnp.logical_and, [pl.program_id(d) == g - 1 for d, g in enumerate(grid)])

            @pl.when(first)
            def _():
                if peers:
                    _peer_barrier(peers)
                for s, a, o, c in zip(sides, s_in, s_out, s_scr):
                    s.start(a, o, c)

            steps = int(np.prod(grid))
            mid_step = (2 * steps) // 3
            if steps > 1 and any(s.mid is not None for s in sides):
                step = functools.reduce(lambda acc, d: acc * grid[d] + pl.program_id(d), range(len(grid)), 0)

                @pl.when(step == mid_step)
                def _():
                    for s, a, o, c in zip(sides, s_in, s_out, s_scr):
                        if s.mid is not None:
                            s.mid(a, o, c)

        body(*m_in, *m_out, *m_scr)
        if sides:
            @pl.when(last)
            def _():
                for s, a, o, c in zip(sides, s_in, s_out, s_scr):
                    if s.mid is not None and steps == 1:
                        s.mid(a, o, c)
                if tail is not None:
                    tail(*m_in, *m_out, *m_scr)
                for s, a, o, c in zip(sides, s_in, s_out, s_scr):
                    s.finish(a, o, c)
        elif tail is not None:
            tail(*m_in, *m_out, *m_scr)

    res = pl.pallas_call(
        full, name=name, grid=grid,
        in_specs=list(in_specs) + [sp for s in sides for sp in s.in_specs],
        out_specs=list(out_specs) + [ANY for s in sides for _ in s.out_shape],
        out_shape=list(out_shape) + [o for s in sides for o in s.out_shape],
        scratch_shapes=list(scratch_shapes) + [c for s in sides for c in s.scratch],
        compiler_params=_cp(dimension_semantics=("arbitrary",) * len(grid),
                            **({"collective_id": BARRIER_IDS[peers]} if peers else {})),
    )(*args, *[a for s in sides for a in s.args])
    res = list(res)
    if not sides:
        return res
    outs, pos = take(res, 0, no)
    side_outs = []
    for _, o, _ in cnt:
        r, pos = take(res, pos, o)
        side_outs.append(r)
    return outs, side_outs


def _inv_freq_lanes():
    inv = np.float32(ROPE_THETA) ** (-np.arange(0, ROT_DIM, 2, dtype=np.float32) / np.float32(ROT_DIM))
    lane = np.arange(LANES) % HD
    out = np.where(lane < ROT_DIM, inv[lane % (ROT_DIM // 2)], 0.0).astype(np.float32)
    return jnp.asarray(out.reshape(1, LANES))


def _rope_tables(pos, inv_freq):
    ang = pos.astype(F32) * inv_freq
    lane = lax.broadcasted_iota(jnp.int32, ang.shape, 1) % HD
    cs = jnp.cos(ang)
    sn = jnp.sin(ang)
    return (jnp.where(lane < ROT_DIM, cs, 1.0), jnp.where(lane < ROT_DIM // 2, -sn, 0.0),
            jnp.where(lane < ROT_DIM // 2, 0.0, jnp.where(lane < ROT_DIM, sn, 0.0)))


def _rope(v, c, s1, s2):
    return v * c + pltpu.roll(v, LANES - 8, axis=1) * s1 + pltpu.roll(v, 8, axis=1) * s2


def _rope_t(d, c, s1, s2):
    return d * c - pltpu.roll(d, LANES - 8, axis=1) * s1 - pltpu.roll(d, 8, axis=1) * s2


def _head_mat():
    r = lax.broadcasted_iota(jnp.int32, (LANES, LANES), 0) // HD
    c = lax.broadcasted_iota(jnp.int32, (LANES, LANES), 1) // HD
    return jnp.where(r == c, 1.0 / HD, 0.0).astype(BF16)


def _head_mean(t, e):
    hi = t.astype(BF16)
    rest = (t - hi.astype(F32)).astype(BF16)
    return _dot(hi, e) + _dot(rest, e)


def in_proj_gather(x, norm_w, shard_t, chip_order, pos_col):
    R = INW // NDEV
    tm = IN_PROJ_TM
    half, nt = R // 2, S // tm

    def body(ord_ref, x_ref, nw_ref, sh_ref, pos_ref, f_ref, ht_ref, p_ref, wfull_ref, c_ref, s1_ref, s2_ref,
             wt, hs, send, recv, loc):
        kk, i = pl.program_id(0), pl.program_id(1)
        x, y, c, _ = _place()
        me, flip = 4 * x + 2 * y + c, 1 - 2 * c
        here, sib, xn, yn = (x, y, c), (x, y, 1 - c), (1 - x, y, c), (x, 1 - y, c)
        b_xn, b_yn, b_dg = 4 * (1 - x) + 2 * y + c, 4 * x + 2 * (1 - y) + c, 4 * (1 - x) + 2 * (1 - y) + c

        def cp(k, block, to, rows=None):
            dst = wt.at[block] if rows is None else wt.at[block, pl.ds(rows * half, half), :]
            return _remote(dst, dst, send, recv, k, to)

        def sends():
            return [cp(0, me, sib), cp(1, me, xn), cp(2, me, yn), cp(3, b_xn, sib), cp(4, b_yn, sib),
                    cp(5, b_xn, yn, rows=0), cp(6, b_yn, xn, rows=1), cp(7, b_dg, sib, rows=0), cp(8, b_dg, sib, rows=1)]

        def keep(j, blk0):
            pair = pl.ds(pl.multiple_of(blk0, 2), 2)
            return pltpu.make_async_copy(wt.at[pair], wfull_ref.at[pair], loc.at[j])

        @pl.when((kk == 0) & (i == 0))
        def _():
            _peer_barrier("sxy")
            _cast_rows(wt.at[me], sh_ref)
            for s_ in sends()[0:3]:
                s_.start()

            def tables(j, _):
                chunk = pl.ds(pl.multiple_of(j * TM, TM), TM)
                c_ref[chunk, :], s1_ref[chunk, :], s2_ref[chunk, :] = _rope_tables(pos_ref[chunk, :], f_ref[...])
                return 0

            lax.fori_loop(0, S // TM, tables, 0)
            cp(0, me + flip, here).wait_recv()
            keep(0, me - c).start()

        @pl.when((kk == 1) & (i == 0))
        def _():
            cp(1, b_xn, here).wait_recv()
            sends()[5].start()
            sends()[3].start()
            cp(2, b_yn, here).wait_recv()
            sends()[6].start()
            sends()[4].start()
            cp(3, b_xn + flip, here).wait_recv()
            keep(1, b_xn - c).start()

        @pl.when((kk == 2) & (i == 0))
        def _():
            cp(4, b_yn + flip, here).wait_recv()
            keep(2, b_yn - c).start()

        @pl.when((kk == 3) & (i == 0))
        def _():
            cp(5, b_dg, here, rows=0).wait_recv()
            sends()[7].start()
            cp(6, b_dg, here, rows=1).wait_recv()
            sends()[8].start()
            cp(7, b_dg + flip, here, rows=0).wait_recv()
            cp(8, b_dg + flip, here, rows=1).wait_recv()
            keep(3, b_dg - c).start()

        rows = pl.ds(pl.multiple_of(i * tm, tm), tm)

        @pl.when(kk == 0)
        def _():
            xv = x_ref[...]
            r = lax.rsqrt(jnp.mean(xv * xv, axis=-1, keepdims=True) + EPS)
            hf = xv * r * nw_ref[...]
            ht_ref[...] = hf.T.astype(BF16)
            hs[rows, :] = hf.astype(BF16)

        h = hs[rows, :]
        chip = ord_ref[kk]
        for cc in range(2):
            p_ref[:, cc * R:(cc + 1) * R] = _dot_nt(h, wt[2 * chip + cc])

        @pl.when((kk == 3) & (i == nt - 1))
        def _():
            for s_ in sends():
                s_.wait_send()
            for j, blk in enumerate((me, b_xn, b_yn, b_dg)):
                keep(j, blk - c).wait()

    def first_pass(kk, i):
        return jnp.where(kk == 0, i, nt - 1)

    grid_spec = pltpu.PrefetchScalarGridSpec(
        num_scalar_prefetch=1, grid=(4, nt),
        in_specs=[pl.BlockSpec((tm, D), lambda kk, i, o: (first_pass(kk, i), 0)),
                  pl.BlockSpec((1, D), lambda kk, i, o: (0, 0)), VMEM, VMEM,
                  pl.BlockSpec((1, LANES), lambda kk, i, o: (0, 0))],
        out_specs=[pl.BlockSpec((D, tm), lambda kk, i, o: (0, first_pass(kk, i))),
                   pl.BlockSpec((tm, 2 * R), lambda kk, i, o: (i, o[kk])), ANY]
        + [pl.BlockSpec((S, LANES), lambda kk, i, o: (0, 0))] * 3,
        scratch_shapes=[pltpu.VMEM((NDEV, R, D), BF16), pltpu.VMEM((S, D), BF16), _sems(9), _sems(9), _sems(4)])
    res = pl.pallas_call(
        body, name="in_proj_gather", grid_spec=grid_spec,
        out_shape=[jax.ShapeDtypeStruct((D, S), BF16), jax.ShapeDtypeStruct((S, INW), F32),
                   jax.ShapeDtypeStruct((NDEV, R, D), BF16)] + [jax.ShapeDtypeStruct((S, LANES), F32)] * 3,
        compiler_params=_cp(dimension_semantics=("arbitrary", "arbitrary"), collective_id=BARRIER_IDS["sxy"]),
    )(chip_order, x, norm_w, shard_t, pos_col, _inv_freq_lanes())
    return res[0], res[1], res[2], tuple(res[3:])


def _qk_specs():
    nb = QKV // LANES
    return [pl.BlockSpec((S, LANES), functools.partial(lambda hp, g, o: (0, o + g * 4 + hp), o=o))
            for o in (OFF_Q // LANES, OFF_K // LANES, OFF_V // LANES)]


def _tab_specs():
    return [pl.BlockSpec((S, LANES), lambda hp, g: (0, 0), pipeline_mode=pl.Buffered(1))] * 3


def _vec_spec():
    return pl.BlockSpec((1, LANES), lambda hp, g: (0, 0))


def _sub_rows(r, d, start, n):
    if d == 1:
        return pl.ds(start, n)
    return pl.ds(r + d * start, n, stride=d)


def _band_window(i, L):
    W = min(TQ + 2 * HALF_SPAN, L)
    q0 = pl.multiple_of(i * TQ, TQ)
    k0 = pl.multiple_of(jnp.clip(q0 - HALF_SPAN, 0, L - W), HALF_SPAN)
    qpos = q0 + (lax.broadcasted_iota(jnp.int32, (2 * TQ, W), 0) & (TQ - 1))
    kpos = k0 + lax.broadcasted_iota(jnp.int32, (2 * TQ, W), 1)
    valid = jnp.abs(qpos - kpos) <= HALF_SPAN
    return W, q0, k0, valid


def _stack_heads(t, lo):
    z = jnp.zeros_like(t)
    return jnp.concatenate([jnp.where(lo, t, z), jnp.where(lo, z, t)], axis=0)


def _unstack_heads(t2, lo):
    return jnp.where(lo, t2[0:TQ], t2[TQ:2 * TQ])


CHAINS = 8


def _interleave(d):
    ru = min(d, CHAINS)
    return ru, min(CHAINS // ru, S // d // TQ)


def _for_blocks(n, fn):
    if n == 1:
        fn(0)
    else:
        def it(j, _):
            fn(j)
            return 0
        lax.fori_loop(0, n, it, 0)


def attn_fwd(proj, tabs, qw2, kw2, sides=()):
    CH = 256

    def body(q_ref, k_ref, v_ref, c_ref, s1_ref, s2_ref, qw_ref, kw_ref, at_ref, ls_ref,
             qs, ks, vs, osub, lsub, onat, lnat, qn, kn):
        g = pl.program_id(1)
        lo = lax.broadcasted_iota(jnp.int32, (1, LANES), 1) < HD
        e = _head_mat()

        def prep(i, _):
            rows = pl.ds(pl.multiple_of(i * CH, CH), CH)
            c, s1, s2 = c_ref[rows, :], s1_ref[rows, :], s2_ref[rows, :]
            for t_ref, w_ref, out, scale in ((q_ref, qw_ref, qn, HD ** -0.5), (k_ref, kw_ref, kn, 1.0)):
                t = t_ref[rows, :]
                r = lax.rsqrt(_head_mean(t * t, e) + EPS)
                out[rows, :] = _rope(t * r * w_ref[...], c, s1, s2) * scale
            return 0

        lax.fori_loop(0, S // CH, prep, 0, unroll=4)

        def group(gi, d):
            L = S // d

            ru, nb = _interleave(d)

            def stage(r, off):
                for c0 in range(0, L, CH):
                    n = min(CH, L)
                    rows = _sub_rows(r, d, c0, n)
                    dst = pl.ds(off + c0, n)
                    qs[dst, :] = qn[rows, :].astype(BF16)
                    ks[dst, :] = kn[rows, :].astype(BF16)
                    vs[dst, :] = v_ref[rows, :].astype(BF16)

            def one(off, i):
                W, q0, k0, valid = _band_window(i, L)
                q2 = _stack_heads(qs[pl.ds(off + q0, TQ), :], lo)
                sc = jnp.where(valid, _dot_nt(q2, ks[pl.ds(off + k0, W), :]), NEG_INF)
                m = jnp.max(sc, axis=-1, keepdims=True)
                p = jnp.exp(sc - m)
                den = jnp.sum(p, axis=-1, keepdims=True)
                o2 = _dot(p.astype(BF16), vs[pl.ds(off + k0, W), :]) / den
                l2 = jnp.broadcast_to(m + jnp.log(den), (2 * TQ, LANES))
                osub[pl.ds(off + q0, TQ), :] = _unstack_heads(o2, lo)
                lsub[pl.ds(off + q0, TQ), :] = _unstack_heads(l2, lo)

            def unstage(r, off):
                for c0 in range(0, L, CH):
                    n = min(CH, L)
                    rows = _sub_rows(r, d, c0, n)
                    onat[gi, rows, :] = osub[pl.ds(off + c0, n), :]
                    lnat[gi, rows, :] = lsub[pl.ds(off + c0, n), :]

            def step(t, _):
                for u in range(ru):
                    stage(t * ru + u, u * L)
                _for_blocks(L // TQ // nb, lambda j: [one(u * L, j * nb + b) for u in range(ru) for b in range(nb)])
                for u in range(ru):
                    unstage(t * ru + u, u * L)
                return 0

            lax.fori_loop(0, d // ru, step, 0)

        for gi, d in enumerate(DILATIONS):
            pl.when(g == gi)(functools.partial(group, gi, d))

        @pl.when(g == len(DILATIONS) - 1)
        def _():
            def mix(i, _):
                rows = pl.ds(pl.multiple_of(i * CH, CH), CH)
                l0, l1, l2 = lnat[0, rows, :], lnat[1, rows, :], lnat[2, rows, :]
                m = jnp.maximum(jnp.maximum(l0, l1), l2)
                e0, e1, e2 = jnp.exp(l0 - m), jnp.exp(l1 - m), jnp.exp(l2 - m)
                den = e0 + e1 + e2
                a = (e0 * onat[0, rows, :] + e1 * onat[1, rows, :] + e2 * onat[2, rows, :]) / den
                at_ref[rows, :] = a.astype(BF16)
                ls_ref[rows, :] = m + jnp.log(den)
                return 0

            lax.fori_loop(0, S // CH, mix, 0)

    out_spec = pl.BlockSpec((S, LANES), lambda hp, g: (0, hp))
    return _call(
        body, sides, name="attn_fwd", grid=(4, 3),
        in_specs=_qk_specs() + _tab_specs() + [_vec_spec(), _vec_spec()],
        out_specs=[out_spec, out_spec],
        out_shape=[jax.ShapeDtypeStruct((S, CC), BF16), jax.ShapeDtypeStruct((S, CC), F32)],
        scratch_shapes=[pltpu.VMEM((S, LANES), BF16)] * 3 + [pltpu.VMEM((S, LANES), F32)] * 2
        + [pltpu.VMEM((3, S, LANES), F32)] * 2 + [pltpu.VMEM((S, LANES), F32)] * 2,
        args=(proj, proj, proj, *tabs, qw2, kw2))


def attn_bwd(proj, tabs, qw2, kw2, d_attn, attn, lse, sides=()):
    CH = 256

    def body(q_ref, k_ref, v_ref, c_ref, s1_ref, s2_ref, qw_ref, kw_ref, do_ref, at_ref, ls_ref,
             dq_ref, dk_ref, dv_ref, gqw_ref, gkw_ref,
             qs, ks, vs, dos, dsub, lsub, dqs, dks, dvs, dnat, qx, kx, dvn, tnq, tnk, rrq, rrk):
        hp, g = pl.program_id(0), pl.program_id(1)
        lo = lax.broadcasted_iota(jnp.int32, (1, LANES), 1) < HD
        e = _head_mat()
        both = ((q_ref, qw_ref, qx, tnq, rrq, HD ** -0.5), (k_ref, kw_ref, kx, tnk, rrk, 1.0))

        @pl.when((hp == 0) & (g == 0))
        def _():
            gqw_ref[...] = jnp.zeros_like(gqw_ref)
            gkw_ref[...] = jnp.zeros_like(gkw_ref)

        def prep(i, _):
            rows = pl.ds(pl.multiple_of(i * CH, CH), CH)
            dnat[rows, :] = _head_mean(do_ref[rows, :] * at_ref[rows, :].astype(F32), e) * float(HD)
            c, s1, s2 = c_ref[rows, :], s1_ref[rows, :], s2_ref[rows, :]
            for t_ref, w_ref, x, tn_s, rr_s, scale in both:
                t = t_ref[rows, :]
                rr = lax.rsqrt(_head_mean(t * t, e) + EPS)
                tn = t * rr
                rr_s[rows, :] = rr
                tn_s[rows, :] = tn
                x[rows, :] = _rope(tn * w_ref[...], c, s1, s2) * scale
            return 0

        lax.fori_loop(0, S // CH, prep, 0, unroll=4)

        def group(d):
            L = S // d

            ru, nb = _interleave(d)

            def stage(r, off):
                for c0 in range(0, L, CH):
                    n = min(CH, L)
                    rows = _sub_rows(r, d, c0, n)
                    dst = pl.ds(off + c0, n)
                    qs[dst, :] = qx[rows, :].astype(BF16)
                    ks[dst, :] = kx[rows, :].astype(BF16)
                    vs[dst, :] = v_ref[rows, :].astype(BF16)
                    dos[dst, :] = do_ref[rows, :].astype(BF16)
                    dsub[dst, :] = dnat[rows, :]
                    lsub[dst, :] = ls_ref[rows, :]
                    dks[dst, :] = jnp.zeros((n, LANES), F32)
                    dvs[dst, :] = jnp.zeros((n, LANES), F32)

            def one(off, i):
                W, q0, k0, valid = _band_window(i, L)
                qrows, krows = pl.ds(off + q0, TQ), pl.ds(off + k0, W)
                q2 = _stack_heads(qs[qrows, :], lo)
                do2 = _stack_heads(dos[qrows, :], lo)
                kk, vv = ks[krows, :], vs[krows, :]
                lse_b, dd_b = lsub[qrows, :], dsub[qrows, :]
                lse2 = jnp.concatenate([lse_b[:, 0:1], lse_b[:, HD:HD + 1]], axis=0)
                dd2 = jnp.concatenate([dd_b[:, 0:1], dd_b[:, HD:HD + 1]], axis=0)
                sc = jnp.where(valid, _dot_nt(q2, kk), NEG_INF)
                p = jnp.exp(sc - lse2)
                ds = (p * (_dot_nt(do2, vv) - dd2)).astype(BF16)
                dqs[qrows, :] = _unstack_heads(_dot(ds, kk), lo)
                dks[krows, :] = dks[krows, :] + _dot_tn(ds, q2)
                dvs[krows, :] = dvs[krows, :] + _dot_tn(p.astype(BF16), do2)

            def unstage(r, off):
                for c0 in range(0, L, CH):
                    n = min(CH, L)
                    rows = _sub_rows(r, d, c0, n)
                    src = pl.ds(off + c0, n)
                    qx[rows, :] = dqs[src, :]
                    kx[rows, :] = dks[src, :]
                    dvn[rows, :] = dvs[src, :]

            def step(t, _):
                for u in range(ru):
                    stage(t * ru + u, u * L)
                _for_blocks(L // TQ // nb, lambda j: [one(u * L, j * nb + b) for u in range(ru) for b in range(nb)])
                for u in range(ru):
                    unstage(t * ru + u, u * L)
                return 0

            lax.fori_loop(0, d // ru, step, 0)

        for gi, d in enumerate(DILATIONS):
            pl.when(g == gi)(functools.partial(group, d))

        def emit(i, _):
            rows = pl.ds(pl.multiple_of(i * CH, CH), CH)
            c, s1, s2 = c_ref[rows, :], s1_ref[rows, :], s2_ref[rows, :]
            for (_, w_ref, x, tn_s, rr_s, scale), out, gw_ref in zip(both, (dq_ref, dk_ref), (gqw_ref, gkw_ref)):
                tn = tn_s[rows, :]
                dy = _rope_t(x[rows, :] * scale, c, s1, s2)
                gw_ref[0:1, :] = gw_ref[0:1, :] + jnp.sum(dy * tn, axis=0, keepdims=True)
                dtn = dy * w_ref[...]
                out[rows, :] = (rr_s[rows, :] * (dtn - tn * _head_mean(dtn * tn, e))).astype(BF16)
            dv_ref[rows, :] = dvn[rows, :].astype(BF16)
            return 0

        lax.fori_loop(0, S // CH, emit, 0, unroll=4)

    nat_spec = pl.BlockSpec((S, LANES), lambda hp, g: (0, hp))
    out_spec = pl.BlockSpec((None, S, LANES), lambda hp, g: (g, 0, hp))
    acc_spec = pl.BlockSpec((8, LANES), lambda hp, g: (0, 0))
    return _call(
        body, sides, name="attn_bwd", grid=(4, 3),
        in_specs=_qk_specs() + _tab_specs() + [_vec_spec(), _vec_spec(), nat_spec, nat_spec, nat_spec],
        out_specs=[out_spec] * 3 + [acc_spec] * 2,
        out_shape=[jax.ShapeDtypeStruct((QKV // PLANE, S, PLANE), BF16)] * 3 + [jax.ShapeDtypeStruct((8, LANES), F32)] * 2,
        scratch_shapes=[pltpu.VMEM((S, LANES), BF16)] * 4 + [pltpu.VMEM((S, LANES), F32)] * 13,
        args=(proj, proj, proj, *tabs, qw2, kw2, d_attn, attn, lse))


PADR = 16
CT = 128


def _conv_specs():
    return [pl.BlockSpec((S, CC), lambda i: (0, OFF_CA // CC)), pl.BlockSpec((S, CC), lambda i: (0, OFF_CB // CC))]


NCB = CC // LANES


def _pad_zero(pad):
    for cb in range(NCB):
        pad[cb, 0:PADR, :] = jnp.zeros((PADR, LANES), F32)
        pad[cb, PADR + S:PADR + S + PADR, :] = jnp.zeros((PADR, LANES), F32)


def _pad_store(pad, row0, n, val):
    for cb in range(NCB):
        pad[cb, pl.ds(pl.multiple_of(row0 + PADR, 8), n), :] = val[:, cb * LANES:(cb + 1) * LANES]


def _taps(pad_ref, cb, s0, weights):
    acc = jnp.zeros((CT, LANES), F32)
    for k in range(KW):
        acc = acc + weights[k] * pad_ref[cb, pl.ds(s0 + k + 1, CT), :]
    return acc


def conv_fwd(proj, conv_w, conv_b, ln_w, ln_b, sides=()):
    def body(a_ref, b_ref, w_ref, cb_ref, lw_ref, lb_ref, c_ref, u3_ref, upad):
        _pad_zero(upad)

        def glu(i, _):
            rows = pl.ds(pl.multiple_of(i * TM, TM), TM)
            _pad_store(upad, i * TM, TM, a_ref[rows, :] * _sigmoid(b_ref[rows, :]))
            return 0

        lax.fori_loop(0, S // TM, glu, 0)

        def chunk(i, _):
            s0 = pl.multiple_of(i * CT, CT)
            for cb in range(CC // LANES):
                cols = slice(cb * LANES, (cb + 1) * LANES)
                w = [w_ref[k:k + 1, cols] for k in range(KW)]
                c_ref[pl.ds(s0, CT), cols] = _taps(upad, cb, s0, w) + cb_ref[:, cols]
            cv = c_ref[pl.ds(s0, CT), :]
            mu = jnp.mean(cv, axis=-1, keepdims=True)
            xc = cv - mu
            rstd = lax.rsqrt(jnp.mean(xc * xc, axis=-1, keepdims=True) + EPS)
            yl = xc * rstd * lw_ref[...] + lb_ref[...]
            u3_ref[pl.ds(s0, CT), :] = (yl * _sigmoid(yl)).astype(BF16)
            return 0

        lax.fori_loop(0, S // CT, chunk, 0)

    vec = pl.BlockSpec((1, CC), lambda i: (0, 0))
    full = pl.BlockSpec((S, CC), lambda i: (0, 0))
    return _call(
        body, sides, name="conv_fwd", grid=(1,),
        in_specs=_conv_specs() + [pl.BlockSpec((KW, CC), lambda i: (0, 0)), vec, vec, vec],
        out_specs=[full, full],
        out_shape=[jax.ShapeDtypeStruct((S, CC), F32), jax.ShapeDtypeStruct((S, CC), BF16)],
        scratch_shapes=[pltpu.VMEM((NCB, S + 2 * PADR, LANES), F32)],
        args=(proj, proj, conv_w, conv_b, ln_w, ln_b))


def conv_bwd(proj, cpre, d_u3, conv_w, conv_w_rev, ln_w, ln_b, sides=()):
    def body(a_ref, b_ref, c_ref, du3_ref, w_ref, wr_ref, lw_ref, lb_ref,
             dc_ref, gw_ref, gcb_ref, glw_ref, glb_ref, upad, dpad):
        _pad_zero(upad)
        _pad_zero(dpad)
        gw_ref[...] = jnp.zeros_like(gw_ref)

        def ln_bwd(i, carry):
            gcb, glw, glb = carry
            rows = pl.ds(pl.multiple_of(i * TM, TM), TM)
            _pad_store(upad, i * TM, TM, a_ref[rows, :] * _sigmoid(b_ref[rows, :]))
            cv = c_ref[rows, :]
            mu = jnp.mean(cv, axis=-1, keepdims=True)
            xc = cv - mu
            rstd = lax.rsqrt(jnp.mean(xc * xc, axis=-1, keepdims=True) + EPS)
            xh = xc * rstd
            yl = xh * lw_ref[...] + lb_ref[...]
            dyl = du3_ref[rows, :] * _dsilu(yl, _sigmoid(yl))
            dxh = dyl * lw_ref[...]
            dcv = rstd * (dxh - jnp.mean(dxh, axis=-1, keepdims=True)
                          - xh * jnp.mean(dxh * xh, axis=-1, keepdims=True))
            _pad_store(dpad, i * TM, TM, dcv)
            return (gcb + jnp.sum(dcv, axis=0, keepdims=True),
                    glw + jnp.sum(dyl * xh, axis=0, keepdims=True),
                    glb + jnp.sum(dyl, axis=0, keepdims=True))

        z = jnp.zeros((1, CC), F32)
        gcb, glw, glb = lax.fori_loop(0, S // TM, ln_bwd, (z, z, z))
        gcb_ref[...] = gcb
        glw_ref[...] = glw
        glb_ref[...] = glb

        def chunk(i, _):
            s0 = pl.multiple_of(i * CT, CT)
            for cb in range(CC // LANES):
                cols = slice(cb * LANES, (cb + 1) * LANES)
                wr = [wr_ref[k:k + 1, cols] for k in range(KW)]
                du = _taps(dpad, cb, s0, wr)
                dcv = dpad[cb, pl.ds(s0 + PADR, CT), :]
                for k in range(KW):
                    gw_ref[k:k + 1, cols] = gw_ref[k:k + 1, cols] + jnp.sum(
                        upad[cb, pl.ds(s0 + k + 1, CT), :] * dcv, axis=0, keepdims=True)
                av = a_ref[pl.ds(s0, CT), cols]
                sb = _sigmoid(b_ref[pl.ds(s0, CT), cols])
                dc_ref[0, pl.ds(s0, CT), cols] = (du * sb).astype(BF16)
                dc_ref[1, pl.ds(s0, CT), cols] = (du * av * sb * (1.0 - sb)).astype(BF16)
            return 0

        lax.fori_loop(0, S // CT, chunk, 0)

    vec = pl.BlockSpec((1, CC), lambda i: (0, 0))
    full = pl.BlockSpec((S, CC), lambda i: (0, 0))
    wsp = pl.BlockSpec((KW, CC), lambda i: (0, 0))
    return _call(
        body, sides, name="conv_bwd", grid=(1,),
        in_specs=_conv_specs() + [full, full, wsp, wsp, vec, vec],
        out_specs=[pl.BlockSpec((2, S, CC), lambda i: (0, 0, 0)), wsp, vec, vec, vec],
        out_shape=[jax.ShapeDtypeStruct((2, S, CC), BF16), jax.ShapeDtypeStruct((KW, CC), F32)]
        + [jax.ShapeDtypeStruct((1, CC), F32)] * 3,
        scratch_shapes=[pltpu.VMEM((NCB, S + 2 * PADR, LANES), F32)] * 2,
        args=(proj, proj, cpre, d_u3, conv_w, conv_w_rev, ln_w, ln_b))


def _gate_specs():
    return [_row(CC, col=OFF_GA // CC + j) for j in range(4)]


def _gates(g_refs, bg_ref):
    ga = _sigmoid(jnp.concatenate([g_refs[0][...], g_refs[1][...]], axis=1) + bg_ref[0:1, :])
    gb = _sigmoid(jnp.concatenate([g_refs[2][...], g_refs[3][...]], axis=1) + bg_ref[1:2, :])
    return ga, gb


def mix_out(x, proj, b_gate, attn, u3, w_o, w_pw, w_out):
    def body(x_ref, g0, g1, g2, g3, bg_ref, at_ref, u3_ref, wo_ref, wp_ref, wout_ref,
             x1_ref, z_ref, ya_ref, yb_ref):
        ga, gb = _gates((g0, g1, g2, g3), bg_ref)
        ya = _dot_nt(at_ref[...], wo_ref[...])
        yb = _dot_nt(u3_ref[...], wp_ref[...])
        z = (ga * ya + gb * yb).astype(BF16)
        ya_ref[...] = ya.astype(BF16)
        yb_ref[...] = yb.astype(BF16)
        z_ref[...] = z
        x1_ref[...] = x_ref[...] + _dot(z, wout_ref[...])

    return pl.pallas_call(
        body, name="mix_out", grid=(S // TM,),
        in_specs=[_row(D)] + _gate_specs() + [_res((2, D)), _row(CC), _row(CC),
                                              _res((D, CC)), _res((D, CC)), _res((D, D))],
        out_specs=[_row(D)] * 4,
        out_shape=[jax.ShapeDtypeStruct((S, D), F32)] + [jax.ShapeDtypeStruct((S, D), BF16)] * 3,
        compiler_params=_cp(dimension_semantics=("arbitrary",)),
    )(x, proj, proj, proj, proj, b_gate, attn, u3, w_o, w_pw, w_out)


def out_bwd(d_x1b, proj, b_gate, ya, yb, w_o, w_pw, w_out, sides=()):
    def body(dx_ref, g0, g1, g2, g3, bg_ref, ya_ref, yb_ref, wo_ref, wp_ref, wout_ref,
             dya_ref, dyb_ref, dgl_ref, dat_ref, du3_ref, gbg_ref):
        @pl.when(pl.program_id(0) == 0)
        def _():
            gbg_ref[...] = jnp.zeros_like(gbg_ref)

        ga, gb = _gates((g0, g1, g2, g3), bg_ref)
        dz = _dot_nt(dx_ref[...], wout_ref[...])
        dya = (dz * ga).astype(BF16)
        dyb = (dz * gb).astype(BF16)
        dgla = dz * ya_ref[...].astype(F32) * ga * (1.0 - ga)
        dglb = dz * yb_ref[...].astype(F32) * gb * (1.0 - gb)
        dya_ref[...] = dya
        dyb_ref[...] = dyb
        for j in range(2):
            dgl_ref[j] = dgla[:, j * PLANE:(j + 1) * PLANE].astype(BF16)
            dgl_ref[2 + j] = dglb[:, j * PLANE:(j + 1) * PLANE].astype(BF16)
        gbg_ref[0:1, :] = gbg_ref[0:1, :] + jnp.sum(dgla, axis=0, keepdims=True)
        gbg_ref[1:2, :] = gbg_ref[1:2, :] + jnp.sum(dglb, axis=0, keepdims=True)
        dat_ref[...] = _dot(dya, wo_ref[...])
        du3_ref[...] = _dot(dyb, wp_ref[...])

    return _call(
        body, sides, name="out_bwd", grid=(S // TM,),
        in_specs=[_row(D)] + _gate_specs() + [_res((2, D)), _row(D), _row(D),
                                              _res((D, CC)), _res((D, CC)), _res((D, D))],
        out_specs=[_row(D), _row(D), _planes(2 * D), _row(CC), _row(CC), pl.BlockSpec((2, D), lambda i: (0, 0))],
        out_shape=[jax.ShapeDtypeStruct((S, D), BF16)] * 2 + [jax.ShapeDtypeStruct((2 * D // PLANE, S, PLANE), BF16)]
        + [jax.ShapeDtypeStruct((S, CC), F32)] * 2 + [jax.ShapeDtypeStruct((2, D), F32)],
        args=(d_x1b, proj, proj, proj, proj, b_gate, ya, yb, w_o, w_pw, w_out))


def ffn_in(x1, norm_w, w_ffn_in, sides=()):
    half = FF // 2

    def body(x_ref, nw_ref, w_ref, h_ref, gu_ref, f_ref):
        xv = x_ref[...]
        r = lax.rsqrt(jnp.mean(xv * xv, axis=-1, keepdims=True) + EPS)
        h = (xv * r * nw_ref[...]).astype(BF16)
        h_ref[...] = h
        for j in range(2):
            gt = _dot_nt(h, w_ref[j * half:(j + 1) * half, :])
            up = _dot_nt(h, w_ref[FF + j * half:FF + (j + 1) * half, :])
            gu_ref[:, j * half:(j + 1) * half] = gt.astype(BF16)
            gu_ref[:, FF + j * half:FF + (j + 1) * half] = up.astype(BF16)
            f_ref[:, j * half:(j + 1) * half] = (gt * _sigmoid(gt) * up).astype(BF16)

    return _call(
        body, sides, name="ffn_in", grid=(S // TM,),
        in_specs=[_row(D), _res((1, D)), _res((2 * FF, D))],
        out_specs=[_row(D), _row(2 * FF), _row(FF)],
        out_shape=[jax.ShapeDtypeStruct((S, D), BF16), jax.ShapeDtypeStruct((S, 2 * FF), BF16),
                   jax.ShapeDtypeStruct((S, FF), BF16)],
        args=(x1, norm_w, w_ffn_in))


def ffn_out_loss(x1, f, w_ffn_out, target):
    def body(x_ref, f_ref, w_ref, t_ref, dy_ref, dyb_ref, sq_ref):
        @pl.when(pl.program_id(0) == 0)
        def _():
            sq_ref[...] = jnp.zeros_like(sq_ref)

        diff = x_ref[...] + _dot(f_ref[...], w_ref[...]) - t_ref[...]
        dy = diff * (1.0 / D)
        dy_ref[...] = dy
        dyb_ref[...] = dy.astype(BF16)
        sq_ref[...] = sq_ref[...] + jnp.sum((diff * diff).reshape(TM // 8, 8, D), axis=0)

    return pl.pallas_call(
        body, name="ffn_out_loss", grid=(S // TM,),
        in_specs=[_row(D), _row(FF), _res((FF, D)), _row(D)],
        out_specs=[_row(D), _row(D), pl.BlockSpec((8, D), lambda i: (0, 0))],
        out_shape=[jax.ShapeDtypeStruct((S, D), F32), jax.ShapeDtypeStruct((S, D), BF16),
                   jax.ShapeDtypeStruct((8, D), F32)],
        compiler_params=_cp(dimension_semantics=("arbitrary",)),
    )(x1, f, w_ffn_out, target)


def _rms_bwd(xv, nw, dh):
    r = lax.rsqrt(jnp.mean(xv * xv, axis=-1, keepdims=True) + EPS)
    xn = xv * r
    dxn = dh * nw
    dx = r * (dxn - xn * jnp.mean(dxn * xn, axis=-1, keepdims=True))
    return dx, dh * xn


def ffn_bwd(dy, dyb, gu, x1, norm_w, w_ffn_in, w_ffn_out, sides=()):
    def body(dy_ref, dyb_ref, gu_ref, x_ref, nw_ref, wi_ref, wo_ref, dgu_ref, dx_ref, dxb_ref, gn_ref):
        @pl.when(pl.program_id(0) == 0)
        def _():
            gn_ref[...] = jnp.zeros_like(gn_ref)

        df = _dot_nt(dyb_ref[...], wo_ref[...])
        gt = gu_ref[:, 0:FF].astype(F32)
        up = gu_ref[:, FF:2 * FF].astype(F32)
        sg = _sigmoid(gt)
        dgt = (df * up * _dsilu(gt, sg)).astype(BF16)
        dup = (df * gt * sg).astype(BF16)
        dgu_ref[:, 0:FF] = dgt
        dgu_ref[:, FF:2 * FF] = dup
        dh = _dot(dgt, wi_ref[0:FF, :]) + _dot(dup, wi_ref[FF:2 * FF, :])
        dxn, gw = _rms_bwd(x_ref[...], nw_ref[...], dh)
        dx = dy_ref[...] + dxn
        dx_ref[...] = dx
        dxb_ref[...] = dx.astype(BF16)
        gn_ref[...] = gn_ref[...] + jnp.sum(gw, axis=0, keepdims=True)

    return _call(
        body, sides, name="ffn_bwd", grid=(S // TM,),
        in_specs=[_row(D), _row(D), _row(2 * FF), _row(D), _res((1, D)), _res((2 * FF, D)), _res((FF, D))],
        out_specs=[_row(2 * FF), _row(D), _row(D), pl.BlockSpec((1, D), lambda i: (0, 0))],
        out_shape=[jax.ShapeDtypeStruct((S, 2 * FF), BF16), jax.ShapeDtypeStruct((S, D), F32),
                   jax.ShapeDtypeStruct((S, D), BF16), jax.ShapeDtypeStruct((1, D), F32)],
        args=(dy, dyb, gu, x1, norm_w, w_ffn_in, w_ffn_out))


def in_bwd(d_q, d_k, d_v, d_conv, d_gl, w_in, x, d_x1, norm_w, sides=()):
    segs = ((OFF_Q, QKV), (OFF_K, QKV), (OFF_V, QKV), (OFF_CA, 2 * CC), (OFF_GA, 2 * D))

    def body(dq_ref, dk_ref, dv_ref, dc_ref, dg_ref, w_ref, x_ref, dx1_ref, nw_ref, gx_ref, gn_ref):
        @pl.when(pl.program_id(0) == 0)
        def _():
            gn_ref[...] = jnp.zeros_like(gn_ref)

        dh = jnp.zeros((TM, D), F32)
        for ref, (off, width) in zip((dq_ref, dk_ref, dv_ref, dc_ref, dg_ref), segs):
            for j in range(width // PLANE):
                dh = dh + _dot(ref[j], w_ref[off + j * PLANE:off + (j + 1) * PLANE, :])
        dxn, gw = _rms_bwd(x_ref[...], nw_ref[...], dh)
        gx_ref[...] = dx1_ref[...] + dxn
        gn_ref[...] = gn_ref[...] + jnp.sum(gw, axis=0, keepdims=True)

    return _call(
        body, sides, name="in_bwd", grid=(S // TM,),
        in_specs=[_planes(QKV)] * 3 + [_planes(2 * CC), _planes(2 * D), _res((INW, D)), _row(D), _row(D), _res((1, D))],
        out_specs=[_row(D), pl.BlockSpec((1, D), lambda i: (0, 0))],
        out_shape=[jax.ShapeDtypeStruct((S, D), F32), jax.ShapeDtypeStruct((1, D), F32)],
        args=(d_q, d_k, d_v, d_conv, d_gl, w_in, x, d_x1, norm_w))


def mm_tn(name, a, b, tm, tn, sides=()):
    M, N = a.shape[1], b.shape[1]

    def body(a_ref, b_ref, o_ref):
        o_ref[...] = _dot_tn(a_ref[...], b_ref[...])

    res = _call(
        body, sides, name=name, grid=(M // tm, N // tn),
        in_specs=[pl.BlockSpec((S, tm), lambda i, j: (0, i)), pl.BlockSpec((S, tn), lambda i, j: (0, j))],
        out_specs=[pl.BlockSpec((tm, tn), lambda i, j: (i, j))],
        out_shape=[jax.ShapeDtypeStruct((M, N), F32)],
        args=(a, b))
    return (res[0][0], res[1]) if sides else res[0]


GW_IN_TN = PLANE
GW_IN_SPLIT = (768, 256)


def gw_in_t(name, ht, d_segs, col0, hw, sides=()):
    tn = GW_IN_TN
    starts, t0 = [], 0
    for seg in d_segs:
        starts.append(t0)
        t0 += seg.shape[0]
    ntiles = [seg.shape[0] for seg in d_segs]

    def body(h_ref, *refs):
        a_refs, o_ref = refs[:-1], refs[-1]
        n = pl.program_id(0)
        for a_ref, st, nt in zip(a_refs, starts, ntiles):
            @pl.when((n >= st) & (n < st + nt))
            def _(a_ref=a_ref):
                o_ref[...] = _dot(h_ref[...], a_ref[...]).T

    def seg_spec(st, nt):
        return pl.BlockSpec((None, S, tn), lambda n: (jnp.clip(n - st, 0, nt - 1), 0, 0))

    res = _call(
        body, sides, name=name, grid=(INW // tn,),
        in_specs=[pl.BlockSpec((hw, S), lambda n: (col0 // hw, 0))] + [seg_spec(st, nt) for st, nt in zip(starts, ntiles)],
        out_specs=[pl.BlockSpec((tn, hw), lambda n: (n, 0))],
        out_shape=[jax.ShapeDtypeStruct((INW, hw), F32)],
        args=(ht, *d_segs))
    return (res[0][0], res[1]) if sides else res[0]


def _place():
    x, y, c = lax.axis_index("x"), lax.axis_index("y"), lax.axis_index("c")
    chips = [(1 - x, y), (x, 1 - y), (1 - x, 1 - y)]
    return x, y, c, chips


def _sems(n):
    return pltpu.SemaphoreType.DMA((n,))


def _remote(src, dst, send, recv, k, to):
    return pltpu.make_async_remote_copy(src_ref=src, dst_ref=dst, send_sem=send.at[k], recv_sem=recv.at[k],
                                        device_id=to, device_id_type=MESH)


def _cast_rows(dst, src, cols=slice(None)):
    rows = src.shape[0]
    step = next((s for s in (128, 64, 32, 16) if rows % s == 0), rows)
    for r0 in range(0, rows, step):
        dst[r0:r0 + step, cols] = src[r0:r0 + step, :].astype(dst.dtype)


def comm_only(name, sides):
    def body():
        pass

    return _call(body, sides, name=name, grid=(1,), in_specs=[], out_specs=[], out_shape=[], args=())[1]


def ag_blocks(shard, dtype):
    R, W = shard.shape

    def copy(outs, scr, k, block, to, src=None):
        dst = outs[0].at[block]
        return _remote(dst if src is None else src, dst, scr[1], scr[2], k, to)

    def local(outs, scr, me):
        return pltpu.make_async_copy(scr[0], outs[0].at[me], scr[3].at[0])

    def start(ins, outs, scr):
        x, y, c, chips = _place()
        me = 4 * x + 2 * y + c
        _cast_rows(scr[0], ins[0])
        local(outs, scr, me).start()
        copy(outs, scr, 0, me, (x, y, 1 - c), src=scr[0]).start()
        for j, (cx, cy) in enumerate(chips):
            copy(outs, scr, 1 + j, me, (cx, cy, c), src=scr[0]).start()

    def finish(ins, outs, scr):
        x, y, c, chips = _place()
        me, sib = 4 * x + 2 * y + c, (x, y, 1 - c)
        passed = []
        for j, (cx, cy) in enumerate(chips):
            theirs = 4 * cx + 2 * cy + c
            copy(outs, scr, 1 + j, theirs, (x, y, c)).wait_recv()
            fwd = copy(outs, scr, 4 + j, theirs, sib)
            fwd.start()
            passed.append(fwd)
        copy(outs, scr, 0, 4 * x + 2 * y + 1 - c, (x, y, c)).wait_recv()
        for j, (cx, cy) in enumerate(chips):
            copy(outs, scr, 4 + j, 4 * cx + 2 * cy + 1 - c, (x, y, c)).wait_recv()
        copy(outs, scr, 0, me, sib, src=scr[0]).wait_send()
        for j, (cx, cy) in enumerate(chips):
            copy(outs, scr, 1 + j, me, (cx, cy, c), src=scr[0]).wait_send()
        for fwd in passed:
            fwd.wait_send()
        local(outs, scr, me).wait()

    return Side((shard,), (VMEM,), (jax.ShapeDtypeStruct((NDEV, R, W), dtype),),
                (pltpu.VMEM((R, W), dtype), _sems(7), _sems(7), _sems(1)), start, finish, None, "dsxy")


def ag_blocks_relay(shard, dtype, transpose=False):
    R, W = shard.shape[::-1] if transpose else shard.shape
    half = R // 2

    def copy(outs, scr, k, block, to, src=None, rows=None):
        dst = outs[0].at[block] if rows is None else outs[0].at[block, pl.ds(rows * half, half), :]
        return _remote(dst if src is None else src, dst, scr[1], scr[2], k, to)

    def local(outs, scr, me):
        return pltpu.make_async_copy(scr[0], outs[0].at[me], scr[3].at[0])

    def own(outs, scr):
        x, y, c, _ = _place()
        me = 4 * x + 2 * y + c
        return [copy(outs, scr, k, me, to, src=scr[0])
                for k, to in enumerate([(x, y, 1 - c), (1 - x, y, c), (x, 1 - y, c)])]

    def start(ins, outs, scr):
        x, y, c, _ = _place()
        if transpose:
            scr[0][...] = ins[0][...].T.astype(dtype)
        else:
            _cast_rows(scr[0], ins[0])
        local(outs, scr, 4 * x + 2 * y + c).start()
        for cp in own(outs, scr):
            cp.start()

    def passed_on(outs, scr):
        x, y, c, _ = _place()
        sib, xn, yn = (x, y, 1 - c), (1 - x, y, c), (x, 1 - y, c)
        b_xn, b_yn, b_dg = 4 * (1 - x) + 2 * y + c, 4 * x + 2 * (1 - y) + c, 4 * (1 - x) + 2 * (1 - y) + c
        near = [copy(outs, scr, 5, b_xn, yn, rows=0), copy(outs, scr, 3, b_xn, sib),
                copy(outs, scr, 6, b_yn, xn, rows=1), copy(outs, scr, 4, b_yn, sib)]
        far = [copy(outs, scr, 7, b_dg, sib, rows=0), copy(outs, scr, 8, b_dg, sib, rows=1)]
        return (b_xn, b_yn, b_dg), near, far

    def mid(ins, outs, scr):
        x, y, c, _ = _place()
        (b_xn, b_yn, _), near, _ = passed_on(outs, scr)
        copy(outs, scr, 1, b_xn, (x, y, c)).wait_recv()
        near[0].start()
        near[1].start()
        copy(outs, scr, 2, b_yn, (x, y, c)).wait_recv()
        near[2].start()
        near[3].start()

    def finish(ins, outs, scr):
        x, y, c, _ = _place()
        here = (x, y, c)
        (b_xn, b_yn, b_dg), near, far = passed_on(outs, scr)
        copy(outs, scr, 5, b_dg, here, rows=0).wait_recv()
        far[0].start()
        copy(outs, scr, 6, b_dg, here, rows=1).wait_recv()
        far[1].start()
        flip = 1 - 2 * c
        copy(outs, scr, 0, 4 * x + 2 * y + 1 - c, here).wait_recv()
        copy(outs, scr, 3, b_xn + flip, here).wait_recv()
        copy(outs, scr, 4, b_yn + flip, here).wait_recv()
        copy(outs, scr, 7, b_dg + flip, here, rows=0).wait_recv()
        copy(outs, scr, 8, b_dg + flip, here, rows=1).wait_recv()
        for cp in own(outs, scr) + near + far:
            cp.wait_send()
        local(outs, scr, 4 * x + 2 * y + c).wait()

    return Side((shard,), (VMEM,), (jax.ShapeDtypeStruct((NDEV, R, W), dtype),),
                (pltpu.VMEM((R, W), dtype), _sems(9), _sems(9), _sems(1)), start, finish, mid, "sxy")


def copies_side(args, out_shape, n_copies, plan, peers):
    def copies(ins, outs, scr):
        return [_remote(s_, d_, scr[0], scr[1], i, to) for i, (s_, d_, to) in enumerate(plan(ins, outs))]

    def start(ins, outs, scr):
        for cp in copies(ins, outs, scr):
            cp.start()

    def finish(ins, outs, scr):
        for cp in copies(ins, outs, scr):
            cp.wait()

    return Side(tuple(args), (ANY,) * len(args), tuple(out_shape), (_sems(n_copies), _sems(n_copies)),
                start, finish, None, peers)


def rs_to_sibling(grads):
    out_shape = [jax.ShapeDtypeStruct((4,) + g.shape[1:], F32) for g in grads]

    def plan(ins, outs):
        x, y, c, _ = _place()
        return [(g.at[2 * k + 1 - c], r.at[k], (x, y, 1 - c)) for g, r in zip(ins, outs) for k in range(4)]

    return copies_side(grads, out_shape, 4 * len(grads), plan, "s")


def rs_to_chips(parts):
    out_shape = [jax.ShapeDtypeStruct((3,) + p.shape[1:], BF16) for p in parts]

    def plan(ins, outs):
        x, y, c, chips = _place()
        return [(p.at[2 * cx + cy], r.at[j], (cx, cy, c))
                for p, r in zip(ins, outs) for j, (cx, cy) in enumerate(chips)]

    return copies_side(parts, out_shape, 3 * len(parts), plan, "dxy")


def rs_to_chips_combined(part):
    _, R, W = part.shape
    half = R // 2
    top, bot = pl.ds(0, half), pl.ds(half, half)

    def copies(ins, outs, scr):
        p, r = ins[0], outs[0]
        loc_a, loc_b, in_x, in_y, comb_a, comb_b, send, recv, loc = scr
        x, y, c, _ = _place()
        xn, yn = (1 - x, y, c), (x, 1 - y, c)
        k_xn, k_yn, k_dg = 2 * (1 - x) + y, 2 * x + 1 - y, 2 * (1 - x) + 1 - y
        direct = [_remote(p.at[k_xn, top, :], r.at[0, top, :], send, recv, 0, xn),
                  _remote(p.at[k_yn, bot, :], r.at[1, bot, :], send, recv, 1, yn),
                  _remote(p.at[k_dg, top, :], in_x, send, recv, 2, xn),
                  _remote(p.at[k_dg, bot, :], in_y, send, recv, 3, yn)]
        combined = [_remote(comb_a, r.at[1, top, :], send, recv, 4, yn),
                    _remote(comb_b, r.at[0, bot, :], send, recv, 5, xn)]
        local = [pltpu.make_async_copy(p.at[k_yn, top, :], loc_a, loc.at[0]),
                 pltpu.make_async_copy(p.at[k_xn, bot, :], loc_b, loc.at[1])]
        return direct, combined, local

    def start(ins, outs, scr):
        direct, _, local = copies(ins, outs, scr)
        for cp in local + direct:
            cp.start()

    def mid(ins, outs, scr):
        loc_a, loc_b, in_x, in_y, comb_a, comb_b = scr[:6]
        direct, combined, local = copies(ins, outs, scr)
        for mine, arrival, inbox, out, nxt in ((local[0], direct[2], in_x, comb_a, combined[0]),
                                               (local[1], direct[3], in_y, comb_b, combined[1])):
            mine.wait()
            arrival.wait_recv()
            src = loc_a if out is comb_a else loc_b
            out[...] = (src[...].astype(F32) + inbox[...].astype(F32)).astype(BF16)
            nxt.start()

    def finish(ins, outs, scr):
        direct, combined, _ = copies(ins, outs, scr)
        direct[0].wait_recv()
        direct[1].wait_recv()
        combined[0].wait_recv()
        combined[1].wait_recv()
        for cp in direct + combined:
            cp.wait_send()

    buf = pltpu.VMEM((half, W), BF16)
    return Side((part,), (ANY,), (jax.ShapeDtypeStruct((2, R, W), BF16),),
                (buf, buf, buf, buf, buf, buf, _sems(6), _sems(6), _sems(2)), start, finish, mid, "xy")


ADAM_TILE_BYTES = 3 * 512 * 1024


def _row_tiles(rows, width):
    return 2 if rows % 32 == 0 and rows * width * 4 > ADAM_TILE_BYTES else 1


def chip_sum(name, grad, recv, c_idx, chip_idx):
    _, R, C = grad.shape
    nt = 1
    tr = R // nt

    def body(s_ref, g_ref, r_ref, p_ref, own_ref):
        k = pl.program_id(1)
        tot = g_ref[0] + r_ref[0]
        p_ref[0] = tot.astype(BF16)

        @pl.when(k == s_ref[1])
        def _():
            own_ref[...] = tot

    grid_spec = pltpu.PrefetchScalarGridSpec(
        num_scalar_prefetch=1, grid=(nt, 4),
        in_specs=[pl.BlockSpec((1, tr, C), lambda i, k, s: (2 * k + s[0], i, 0)),
                  pl.BlockSpec((1, tr, C), lambda i, k, s: (k, i, 0))],
        out_specs=[pl.BlockSpec((1, tr, C), lambda i, k, s: (k, i, 0)),
                   pl.BlockSpec((tr, C), lambda i, k, s: (i, 0))])
    return pl.pallas_call(
        body, name=name, grid_spec=grid_spec,
        out_shape=[jax.ShapeDtypeStruct((4, R, C), BF16), jax.ShapeDtypeStruct((R, C), F32)],
        compiler_params=_cp(dimension_semantics=("arbitrary", "arbitrary")),
    )(jnp.stack([c_idx, chip_idx]), grad, recv)


def _adamw(w, g, m, v):
    m2 = ADAM_B1 * m + (1.0 - ADAM_B1) * g
    v2 = ADAM_B2 * v + (1.0 - ADAM_B2) * (g * g)
    m_hat = m2 / (1.0 - ADAM_B1 ** ADAM_STEP)
    v_hat = v2 / (1.0 - ADAM_B2 ** ADAM_STEP)
    delta = -ADAM_LR * (m_hat / (jnp.sqrt(v_hat) + ADAM_EPS) + ADAM_WD * w)
    return delta, m2, v2


def shard_adam(name, owns, recvs, w, m, v, io_t=False, sides=()):
    n = len(owns)
    R = owns[0].shape[0]
    ct = min(o.shape[1] for o in owns)
    first = [sum(o.shape[1] for o in owns[:j]) // ct for j in range(n)]
    count = [o.shape[1] // ct for o in owns]
    nt = _row_tiles(R, ct)
    tr = R // nt

    def body(*refs):
        o_refs, r_refs = refs[:n], refs[n:2 * n]
        w_ref, m_ref, v_ref, g_ref, d_ref, nm_ref, nv_ref = refs[2 * n:]
        g = None
        for j in range(n):
            gj = o_refs[j][...]
            for q in range(recvs[j].shape[0]):
                gj = gj + r_refs[j][q].astype(F32)
            g = gj if g is None else jnp.where(pl.program_id(0) >= first[j], gj, g)
        t = (lambda a: a.T) if io_t else (lambda a: a)
        delta, m2, v2 = _adamw(t(w_ref[...]), g, t(m_ref[...]), t(v_ref[...]))
        g_ref[...] = t(g)
        d_ref[...] = t(delta)
        nm_ref[...] = t(m2)
        nv_ref[...] = t(v2)

    def part(j):
        return pl.BlockSpec((tr, ct), lambda k, i: (i, jnp.clip(k - first[j], 0, count[j] - 1)))

    def part3(j):
        return pl.BlockSpec((recvs[j].shape[0], tr, ct), lambda k, i: (0, i, jnp.clip(k - first[j], 0, count[j] - 1)))

    C = sum(count) * ct
    tile = pl.BlockSpec((ct, tr), lambda k, i: (k, i)) if io_t else pl.BlockSpec((tr, ct), lambda k, i: (i, k))
    return _call(
        body, sides, name=name, grid=(sum(count), nt),
        in_specs=[part(j) for j in range(n)] + [part3(j) for j in range(n)] + [tile, tile, tile],
        out_specs=[tile] * 4, out_shape=[jax.ShapeDtypeStruct((C, R) if io_t else (R, C), F32)] * 4,
        args=[*owns, *recvs, w, m, v])


ROW_N1, ROW_N2, ROW_BG, ROW_QN, ROW_KN, ROW_CB, ROW_LW, ROW_LB, ROW_CW = 0, 1, 2, 4, 5, 6, 7, 8, 9
PACK_ROWS = 40
SMALL = ("norm1_w", "norm2_w", "b_gate", "q_norm_w", "k_norm_w", "conv_b", "conv_ln_w", "conv_ln_b", "conv_w")


def small_sync(g, sq, sides=()):
    ns = len(SMALL)

    def copies(refs):
        pack, recv, send_sems, recv_sems = refs[ns + 2:]
        x, y, c, _ = _place()
        return [pltpu.make_async_remote_copy(
            src_ref=pack, dst_ref=recv.at[4 * x + 2 * y + c], send_sem=send_sems.at[k - 1],
            recv_sem=recv_sems.at[k - 1], device_id=(x ^ (k >> 2), y ^ ((k >> 1) & 1), c ^ (k & 1)),
            device_id_type=MESH) for k in range(1, NDEV)]

    def body(*refs):
        gi = dict(zip(SMALL, refs[:ns]))
        sq_ref, tot, pack, recv, send_sems, recv_sems = refs[ns:]
        x, y, c, _ = _place()
        me = 4 * x + 2 * y + c

        pack[...] = jnp.zeros_like(pack)
        pack[ROW_KN:ROW_KN + 1, LANES:2 * LANES] = jnp.full((1, LANES), (0.5 / D) * jnp.sum(sq_ref[...]), F32)
        pack[ROW_N1:ROW_N1 + 1, :] = gi["norm1_w"][...]
        pack[ROW_N2:ROW_N2 + 1, :] = gi["norm2_w"][...]
        pack[ROW_BG:ROW_BG + 2, :] = gi["b_gate"][...]
        pack[ROW_QN:ROW_QN + 1, 0:HD] = gi["q_norm_w"][...]
        pack[ROW_KN:ROW_KN + 1, 0:HD] = gi["k_norm_w"][...]
        pack[ROW_CB:ROW_CB + 1, 0:CC] = gi["conv_b"][...]
        pack[ROW_LW:ROW_LW + 1, 0:CC] = gi["conv_ln_w"][...]
        pack[ROW_LB:ROW_LB + 1, 0:CC] = gi["conv_ln_b"][...]
        pack[ROW_CW:ROW_CW + KW, 0:CC] = gi["conv_w"][...]

        for cp in copies(refs):
            cp.start()
        recv[me] = pack[...]

    def tail(*refs):
        tot, recv = refs[ns + 1], refs[ns + 3]
        for cp in copies(refs):
            cp.wait()
        acc = recv[0]
        for p in range(1, NDEV):
            acc = acc + recv[p]
        tot[...] = acc

    args = [g[k] for k in SMALL] + [sq]
    res = _call(
        body, sides, name="small_sync", grid=(1,), in_specs=[VMEM] * len(args), out_specs=[VMEM],
        out_shape=[jax.ShapeDtypeStruct((PACK_ROWS, D), F32)],
        scratch_shapes=[pltpu.VMEM((PACK_ROWS, D), F32), pltpu.VMEM((NDEV, PACK_ROWS, D), F32),
                        _sems(NDEV - 1), _sems(NDEV - 1)],
        args=args, own_comm=True, tail=tail)
    return (res[0][0], res[1]) if sides else res[0]


def small_adam(tot, w, m, v, me):
    ns = len(SMALL)

    def body(me_ref, tot, *refs):
        wi = dict(zip(SMALL, refs[:ns]))
        mi = dict(zip(SMALL, refs[ns:2 * ns]))
        vi = dict(zip(SMALL, refs[2 * ns:3 * ns]))
        outs = refs[3 * ns:7 * ns]
        loss_ref = refs[7 * ns]
        me = me_ref[0]

        def shard_grad(name):
            if name == "b_gate":
                return tot[ROW_BG:ROW_BG + 2, pl.ds(pl.multiple_of(me * LANES, LANES), LANES)]
            if name == "conv_w":
                win = tot[ROW_CW:ROW_CW + KW, pl.ds(pl.multiple_of((me // 2) * LANES, LANES), LANES)]
                return jnp.where(me % 2 == 1, win[:, HD:LANES], win[:, 0:HD])
            row = {"norm1_w": ROW_N1, "norm2_w": ROW_N2, "q_norm_w": ROW_QN, "k_norm_w": ROW_KN,
                   "conv_b": ROW_CB, "conv_ln_w": ROW_LW, "conv_ln_b": ROW_LB}[name]
            return tot[row:row + 1, 0:wi[name].shape[1]]

        for i, name in enumerate(SMALL):
            gr = shard_grad(name)
            delta, m2, v2 = _adamw(wi[name][...], gr, mi[name][...], vi[name][...])
            outs[4 * i][...] = gr
            outs[4 * i + 1][...] = delta
            outs[4 * i + 2][...] = m2
            outs[4 * i + 3][...] = v2
        loss_ref[...] = tot[ROW_KN:ROW_KN + 1, LANES:2 * LANES]

    out_shape = []
    for name in SMALL:
        out_shape += [jax.ShapeDtypeStruct(w[name].shape, F32)] * 4
    out_shape.append(jax.ShapeDtypeStruct((1, LANES), F32))
    args = [tot] + [w[k] for k in SMALL] + [m[k] for k in SMALL] + [v[k] for k in SMALL]
    grid_spec = pltpu.PrefetchScalarGridSpec(
        num_scalar_prefetch=1, grid=(1,), in_specs=[VMEM] * len(args), out_specs=[VMEM] * len(out_shape))
    res = pl.pallas_call(body, name="small_adam", grid_spec=grid_spec, out_shape=out_shape)(me, *args)
    out = {name: tuple(res[4 * i:4 * i + 4]) for i, name in enumerate(SMALL)}
    return out, res[4 * ns][0, 0]


MATS = ("w_in", "w_o_attn", "w_pw_conv", "w_out", "w_ffn_in", "w_ffn_out")
TRANSPOSED = ("w_in", "w_ffn_in")
WEIGHTS = ("norm1_w", "w_in", "b_gate", "q_norm_w", "k_norm_w", "w_o_attn", "conv_w", "conv_b", "conv_ln_w",
           "conv_ln_b", "w_pw_conv", "w_out", "norm2_w", "w_ffn_in", "w_ffn_out")


def _blocks_to_cols(blocks):
    n, R, C = blocks.shape
    return blocks.transpose(1, 0, 2).reshape(R, n * C)


def kernel(x, positions, norm1_w, w_in, b_gate, q_norm_w, k_norm_w, w_o_attn, conv_w, conv_b, conv_ln_w, conv_ln_b, w_pw_conv, w_out, norm2_w, w_ffn_in, w_ffn_out, loss_target, m_norm1_w, m_w_in, m_b_gate, m_q_norm_w, m_k_norm_w, m_w_o_attn, m_conv_w, m_conv_b, m_conv_ln_w, m_conv_ln_b, m_w_pw_conv, m_w_out, m_norm2_w, m_w_ffn_in, m_w_ffn_out, v_norm1_w, v_w_in, v_b_gate, v_q_norm_w, v_k_norm_w, v_w_o_attn, v_conv_w, v_conv_b, v_conv_ln_w, v_conv_ln_b, v_w_pw_conv, v_w_out, v_norm2_w, v_w_ffn_in, v_w_ffn_out):
    w = dict(norm1_w=norm1_w, w_in=w_in, b_gate=b_gate, q_norm_w=q_norm_w, k_norm_w=k_norm_w, w_o_attn=w_o_attn,
             conv_w=conv_w, conv_b=conv_b, conv_ln_w=conv_ln_w, conv_ln_b=conv_ln_b, w_pw_conv=w_pw_conv,
             w_out=w_out, norm2_w=norm2_w, w_ffn_in=w_ffn_in, w_ffn_out=w_ffn_out)
    m = dict(norm1_w=m_norm1_w, w_in=m_w_in, b_gate=m_b_gate, q_norm_w=m_q_norm_w, k_norm_w=m_k_norm_w,
             w_o_attn=m_w_o_attn, conv_w=m_conv_w, conv_b=m_conv_b, conv_ln_w=m_conv_ln_w,
             conv_ln_b=m_conv_ln_b, w_pw_conv=m_w_pw_conv, w_out=m_w_out, norm2_w=m_norm2_w,
             w_ffn_in=m_w_ffn_in, w_ffn_out=m_w_ffn_out)
    v = dict(norm1_w=v_norm1_w, w_in=v_w_in, b_gate=v_b_gate, q_norm_w=v_q_norm_w, k_norm_w=v_k_norm_w,
             w_o_attn=v_w_o_attn, conv_w=v_conv_w, conv_b=v_conv_b, conv_ln_w=v_conv_ln_w,
             conv_ln_b=v_conv_ln_b, w_pw_conv=v_w_pw_conv, w_out=v_w_out, norm2_w=v_norm2_w,
             w_ffn_in=v_w_ffn_in, w_ffn_out=v_w_ffn_out)
    def two_d(t):
        t = {k: (a[0] if a.ndim == 3 else a) for k, a in t.items()}
        return {k: (a.T if k in TRANSPOSED else a) for k, a in t.items()}

    w, m, v = two_d(w), two_d(m), two_d(v)

    x2, target = x[0], loss_target[0]
    c_idx = lax.axis_index("c").astype(jnp.int32)
    chip_idx = (2 * lax.axis_index("x") + lax.axis_index("y")).astype(jnp.int32)
    qw2 = jnp.tile(w["q_norm_w"], (1, 2))
    kw2 = jnp.tile(w["k_norm_w"], (1, 2))

    ax, ay = lax.axis_index("x"), lax.axis_index("y")
    chip_order = jnp.stack([2 * ax + ay, 2 * (1 - ax) + ay, 2 * ax + 1 - ay, 2 * (1 - ax) + 1 - ay]).astype(jnp.int32)
    h_t, proj, w_in_blocks, tabs = in_proj_gather(x2, w["norm1_w"], w["w_in"], chip_order, positions.reshape(S, 1))
    w_in_t = w_in_blocks.reshape(INW, D)
    (attn, lse), ((w_ffn_in_blocks,), (w_out_blocks,), (w_o_blocks,), (w_pw_blocks,), (bg_blocks,), (cw_blocks,)) = attn_fwd(
        proj, tabs, qw2, kw2, sides=(ag_blocks_relay(w["w_ffn_in"], BF16), ag_blocks_relay(w["w_out"], BF16),
                                     ag_blocks_relay(w["w_o_attn"], BF16, transpose=True),
                                     ag_blocks_relay(w["w_pw_conv"], BF16, transpose=True),
                                     ag_blocks(w["b_gate"], F32), ag_blocks(w["conv_w"], F32)))
    w_ffn_in_t = w_ffn_in_blocks.reshape(2 * FF, D)
    w_out_f = w_out_blocks.reshape(D, D)
    w_o_t, w_pw_t = w_o_blocks.reshape(D, CC), w_pw_blocks.reshape(D, CC)
    b_gate_f, conv_w_f = _blocks_to_cols(bg_blocks), _blocks_to_cols(cw_blocks)
    cpre, u3 = conv_fwd(proj, conv_w_f, w["conv_b"], w["conv_ln_w"], w["conv_ln_b"])
    x1, z, ya, yb = mix_out(x2, proj, b_gate_f, attn, u3, w_o_t, w_pw_t, w_out_f)
    (h2, gu, f), ((w_ffn_out_blocks,),) = ffn_in(x1, w["norm2_w"], w_ffn_in_t, sides=(ag_blocks_relay(w["w_ffn_out"], BF16),))
    w_ffn_out_f = w_ffn_out_blocks.reshape(FF, D)
    dy, dyb, sq = ffn_out_loss(x1, f, w_ffn_out_f, target)

    g = {}
    g_ffn_out = mm_tn("gw_ffn_out", f, dyb, FF // 2, D).reshape(NDEV, FF // NDEV, D)
    (d_gu, d_x1, d_x1b, g["norm2_w"]), ((ra_ffn_out,),) = ffn_bwd(
        dy, dyb, gu, x1, w["norm2_w"], w_ffn_in_t, w_ffn_out_f, sides=(rs_to_sibling([g_ffn_out]),))
    pb_ffn_out, own_ffn_out = chip_sum("chip_sum_w_ffn_out", g_ffn_out, ra_ffn_out, c_idx, chip_idx)
    g_ffn_in = mm_tn("gw_ffn_in", d_gu, h2, FF // 2, D).reshape(NDEV, 2 * FF // NDEV, D)
    g_out = mm_tn("gw_out", z, d_x1b, D // 2, D).reshape(NDEV, D // NDEV, D)
    (d_ya, d_yb, d_gl, d_attn, d_u3, g["b_gate"]), ((ra_ffn_in,),) = out_bwd(
        d_x1b, proj, b_gate_f, ya, yb, w_o_t, w_pw_t, w_out_f, sides=(rs_to_sibling([g_ffn_in]),))
    pb_ffn_in, own_ffn_in = chip_sum("chip_sum_w_ffn_in", g_ffn_in, ra_ffn_in, c_idx, chip_idx)
    g_w_o = mm_tn("gw_o_attn", d_ya, attn, D // 2, CC).reshape(NDEV, D // NDEV, CC)
    g_w_pw = mm_tn("gw_pw_conv", d_yb, u3, D // 2, CC).reshape(NDEV, D // NDEV, CC)
    (d_conv, g["conv_w"], g["conv_b"], g["conv_ln_w"], g["conv_ln_b"]), ((ra_out, ra_w_o, ra_w_pw),) = conv_bwd(
        proj, cpre, d_u3, conv_w_f, conv_w_f[::-1], w["conv_ln_w"], w["conv_ln_b"],
        sides=(rs_to_sibling([g_out, g_w_o, g_w_pw]),))
    pb_out, own_out = chip_sum("chip_sum_w_out", g_out, ra_out, c_idx, chip_idx)
    pb_w_o, own_w_o = chip_sum("chip_sum_w_o_attn", g_w_o, ra_w_o, c_idx, chip_idx)
    pb_w_pw, own_w_pw = chip_sum("chip_sum_w_pw_conv", g_w_pw, ra_w_pw, c_idx, chip_idx)
    (d_q, d_k, d_v, gqw, gkw), ((rb_ffn_out, rb_ffn_in, rb_out, rb_w_o, rb_w_pw),) = attn_bwd(
        proj, tabs, qw2, kw2, d_attn, attn, lse,
        sides=(rs_to_chips([pb_ffn_out, pb_ffn_in, pb_out, pb_w_o, pb_w_pw]),))
    g["q_norm_w"] = gqw[0:1, 0:HD] + gqw[0:1, HD:LANES]
    g["k_norm_w"] = gkw[0:1, 0:HD] + gkw[0:1, HD:LANES]
    d_segs = (d_q, d_k, d_v, d_conv, d_gl)
    parts, to_sibling, to_chips, owns, from_chips = [], None, None, [], []
    for k, hw in enumerate(GW_IN_SPLIT):
        sides = tuple(s for s in (to_chips, to_sibling) if s is not None)
        part = gw_in_t("gw_in_%d" % k, h_t, d_segs, sum(GW_IN_SPLIT[:k]), hw, sides=sides)
        part, outs = part if sides else (part, [])
        outs = list(outs)
        if to_chips is not None:
            from_chips.append(outs.pop(0)[0])
        if to_sibling is not None:
            pb, own = chip_sum("chip_sum_w_in_%d" % (k - 1), parts[-1], outs.pop(0)[0], c_idx, chip_idx)
            owns.append(own)
            to_chips = rs_to_chips_combined(pb)
        else:
            to_chips = None
        parts.append(part.reshape(NDEV, INW // NDEV, hw))
        to_sibling = rs_to_sibling([parts[-1]])
    (grad_x, g["norm1_w"]), ((rb_prev,), (ra_last,)) = in_bwd(
        d_q, d_k, d_v, d_conv, d_gl, w_in_t, x2, d_x1, w["norm1_w"], sides=(to_chips, to_sibling))
    from_chips.append(rb_prev)
    pb, own = chip_sum("chip_sum_w_in_%d" % (len(GW_IN_SPLIT) - 1), parts[-1], ra_last, c_idx, chip_idx)
    owns.append(own)
    small, loss = small_adam(small_sync(g, sq), w, m, v, (4 * ax + 2 * ay + c_idx).astype(jnp.int32).reshape(1))
    adam_ffn_in, ((rb_last,),) = shard_adam("adam_w_ffn_in", [own_ffn_in], [rb_ffn_in], w["w_ffn_in"], m["w_ffn_in"],
                                            v["w_ffn_in"], sides=(rs_to_chips_combined(pb),))
    from_chips.append(rb_last)

    res = {
        "w_in": shard_adam("adam_w_in", owns, from_chips, w["w_in"], m["w_in"], v["w_in"]),
        "w_ffn_in": adam_ffn_in,
        "w_o_attn": shard_adam("adam_w_o_attn", [own_w_o], [rb_w_o], w["w_o_attn"], m["w_o_attn"], v["w_o_attn"], io_t=True),
        "w_pw_conv": shard_adam("adam_w_pw_conv", [own_w_pw], [rb_w_pw],
                                w["w_pw_conv"], m["w_pw_conv"], v["w_pw_conv"], io_t=True),
        "w_out": shard_adam("adam_w_out", [own_out], [rb_out], w["w_out"], m["w_out"], v["w_out"]),
        "w_ffn_out": shard_adam("adam_w_ffn_out", [own_ffn_out], [rb_ffn_out],
                                w["w_ffn_out"], m["w_ffn_out"], v["w_ffn_out"]),
    }
    res = {k: tuple(a.T if k in TRANSPOSED else a for a in r) for k, r in res.items()}
    res.update(small)

    def shaped(name, a):
        return a.reshape((1,) + a.shape) if name in MATS or name in ("b_gate", "conv_w") else a

    outs = [loss, grad_x.reshape(1, S, D)]
    for i in range(4):
        outs += [shaped(k, res[k][i]) for k in WEIGHTS]
    return tuple(outs)
```

```python
import functools
from typing import Callable, NamedTuple, Optional

import numpy as np
import jax
import jax.numpy as jnp
from jax import lax
from jax.experimental import pallas as pl
from jax.experimental.pallas import tpu as pltpu

F32 = jnp.float32
BF16 = jnp.bfloat16

S = 2048
D = 1024
HD = 64
QKV = 1536
CC = 512
KW = 31
FF = 2816
INW = 7680
OFF_Q, OFF_K, OFF_V, OFF_CA, OFF_CB, OFF_GA, OFF_GB = 0, 1536, 3072, 4608, 5120, 5632, 6656
DILATIONS = (1, 4, 16)
HALF_SPAN = 64
EPS = 1e-6
NEG_INF = -1e30
ROPE_THETA = 500000.0
ROT_DIM = 16

ADAM_LR = 0.001
ADAM_B1 = 0.9
ADAM_B2 = 0.999
ADAM_EPS = 1e-08
ADAM_WD = 0.01
ADAM_STEP = 10

NDEV = 8
LANES = 128
TM = 256
IN_PROJ_TM = 512
TQ = 128
VMEM_LIMIT = 56 * 1024 * 1024
MESH = pl.DeviceIdType.MESH


def _cp(**kw):
    return pltpu.CompilerParams(vmem_limit_bytes=VMEM_LIMIT, **kw)


def _row(width, col=0, tm=TM):
    return pl.BlockSpec((tm, width), lambda i: (i, col))


PLANE = 512


def _planes(width, tm=TM):
    return pl.BlockSpec((width // PLANE, tm, PLANE), lambda i: (0, i, 0))


def _res(shape):
    nd = len(shape)
    return pl.BlockSpec(shape, lambda *_: (0,) * nd, pipeline_mode=pl.Buffered(1))


def _dot(a, b):
    return jnp.dot(a, b, preferred_element_type=F32)


def _dot_nt(a, b):
    return lax.dot_general(a, b, (((1,), (1,)), ((), ())), preferred_element_type=F32)


def _dot_tn(a, b):
    return lax.dot_general(a, b, (((0,), (0,)), ((), ())), preferred_element_type=F32)


def _sigmoid(x):
    return jax.nn.sigmoid(x)


def _dsilu(x, sg):
    return sg * (1.0 + x * (1.0 - sg))


ANY = pl.BlockSpec(memory_space=pl.ANY)
VMEM = pl.BlockSpec(memory_space=pltpu.VMEM)


class Side(NamedTuple):
    args: tuple
    in_specs: tuple
    out_shape: tuple
    scratch: tuple
    start: Callable
    finish: Callable
    mid: Optional[Callable] = None
    peers: str = ""


BARRIER_IDS = {"s": 0, "dxy": 1, "dsxy": 2, "sxy": 3, "xy": 4}


def _peer_barrier(peers):
    x, y, c = lax.axis_index("x"), lax.axis_index("y"), lax.axis_index("c")
    where = {"s": (x, y, 1 - c), "x": (1 - x, y, c), "y": (x, 1 - y, c), "d": (1 - x, 1 - y, c)}
    barrier = pltpu.get_barrier_semaphore()
    for p in peers:
        pl.semaphore_signal(barrier, inc=1, device_id=where[p], device_id_type=MESH)
    pl.semaphore_wait(barrier, len(peers))


def _call(body, sides=(), *, name, grid, in_specs, out_specs, out_shape, scratch_shapes=(), args, own_comm=False,
          tail=None):
    assert tail is None or int(np.prod(grid)) == 1
    ni, no, ns = len(in_specs), len(out_specs), len(scratch_shapes)
    cnt = [(len(s.args), len(s.out_shape), len(s.scratch)) for s in sides]
    peers = "".join(sorted(set("".join(s.peers for s in sides))))
    if own_comm or not sides or any(not s.peers for s in sides):
        peers = ""

    def take(refs, pos, n):
        return refs[pos:pos + n], pos + n

    def full(*refs):
        m_in, pos = take(refs, 0, ni)
        s_in = []
        for a, _, _ in cnt:
            r, pos = take(refs, pos, a)
            s_in.append(r)
        m_out, pos = take(refs, pos, no)
        s_out = []
        for _, o, _ in cnt:
            r, pos = take(refs, pos, o)
            s_out.append(r)
        m_scr, pos = take(refs, pos, ns)
        s_scr = []
        for _, _, c in cnt:
            r, pos = take(refs, pos, c)
            s_scr.append(r)
        if sides:
            first = functools.reduce(jnp.logical_and, [pl.program_id(d) == 0 for d in range(len(grid))])
            last = functools.reduce(jnp.logical_and, [pl.program_id(d) == g - 1 for d, g in enumerate(grid)])

            @pl.when(first)
            def _():
                if peers:
                    _peer_barrier(peers)
                for s, a, o, c in zip(sides, s_in, s_out, s_scr):
                    s.start(a, o, c)

            steps = int(np.prod(grid))
            mid_step = (2 * steps) // 3
            if steps > 1 and any(s.mid is not None for s in sides):
                step = functools.reduce(lambda acc, d: acc * grid[d] + pl.program_id(d), range(len(grid)), 0)

                @pl.when(step == mid_step)
                def _():
                    for s, a, o, c in zip(sides, s_in, s_out, s_scr):
                        if s.mid is not None:
                            s.mid(a, o, c)

        body(*m_in, *m_out, *m_scr)
        if sides:
            @pl.when(last)
            def _():
                for s, a, o, c in zip(sides, s_in, s_out, s_scr):
                    if s.mid is not None and steps == 1:
                        s.mid(a, o, c)
                if tail is not None:
                    tail(*m_in, *m_out, *m_scr)
                for s, a, o, c in zip(sides, s_in, s_out, s_scr):
                    s.finish(a, o, c)
        elif tail is not None:
            tail(*m_in, *m_out, *m_scr)

    res = pl.pallas_call(
        full, name=name, grid=grid,
        in_specs=list(in_specs) + [sp for s in sides for sp in s.in_specs],
        out_specs=list(out_specs) + [ANY for s in sides for _ in s.out_shape],
        out_shape=list(out_shape) + [o for s in sides for o in s.out_shape],
        scratch_shapes=list(scratch_shapes) + [c for s in sides for c in s.scratch],
        compiler_params=_cp(dimension_semantics=("arbitrary",) * len(grid),
                            **({"collective_id": BARRIER_IDS[peers]} if peers else {})),
    )(*args, *[a for s in sides for a in s.args])
    res = list(res)
    if not sides:
        return res
    outs, pos = take(res, 0, no)
    side_outs = []
    for _, o, _ in cnt:
        r, pos = take(res, pos, o)
        side_outs.append(r)
    return outs, side_outs


def _inv_freq_lanes():
    inv = np.float32(ROPE_THETA) ** (-np.arange(0, ROT_DIM, 2, dtype=np.float32) / np.float32(ROT_DIM))
    lane = np.arange(LANES) % HD
    out = np.where(lane < ROT_DIM, inv[lane % (ROT_DIM // 2)], 0.0).astype(np.float32)
    return jnp.asarray(out.reshape(1, LANES))


def _rope_tables(pos, inv_freq):
    ang = pos.astype(F32) * inv_freq
    lane = lax.broadcasted_iota(jnp.int32, ang.shape, 1) % HD
    cs = jnp.cos(ang)
    sn = jnp.sin(ang)
    return (jnp.where(lane < ROT_DIM, cs, 1.0), jnp.where(lane < ROT_DIM // 2, -sn, 0.0),
            jnp.where(lane < ROT_DIM // 2, 0.0, jnp.where(lane < ROT_DIM, sn, 0.0)))


def _rope(v, c, s1, s2):
    return v * c + pltpu.roll(v, LANES - 8, axis=1) * s1 + pltpu.roll(v, 8, axis=1) * s2


def _rope_t(d, c, s1, s2):
    return d * c - pltpu.roll(d, LANES - 8, axis=1) * s1 - pltpu.roll(d, 8, axis=1) * s2


def _head_mat():
    r = lax.broadcasted_iota(jnp.int32, (LANES, LANES), 0) // HD
    c = lax.broadcasted_iota(jnp.int32, (LANES, LANES), 1) // HD
    return jnp.where(r == c, 1.0 / HD, 0.0).astype(BF16)


def _head_mean(t, e):
    hi = t.astype(BF16)
    rest = (t - hi.astype(F32)).astype(BF16)
    return _dot(hi, e) + _dot(rest, e)


def in_proj_gather(x, norm_w, shard_t, chip_order, pos_col):
    R = INW // NDEV
    tm = IN_PROJ_TM
    half, nt = R // 2, S // tm

    def body(ord_ref, x_ref, nw_ref, sh_ref, pos_ref, f_ref, ht_ref, p_ref, wfull_ref, c_ref, s1_ref, s2_ref,
             wt, hs, send, recv, loc):
        kk, i = pl.program_id(0), pl.program_id(1)
        x, y, c, _ = _place()
        me, flip = 4 * x + 2 * y + c, 1 - 2 * c
        here, sib, xn, yn = (x, y, c), (x, y, 1 - c), (1 - x, y, c), (x, 1 - y, c)
        b_xn, b_yn, b_dg = 4 * (1 - x) + 2 * y + c, 4 * x + 2 * (1 - y) + c, 4 * (1 - x) + 2 * (1 - y) + c

        def cp(k, block, to, rows=None):
            dst = wt.at[block] if rows is None else wt.at[block, pl.ds(rows * half, half), :]
            return _remote(dst, dst, send, recv, k, to)

        def sends():
            return [cp(0, me, sib), cp(1, me, xn), cp(2, me, yn), cp(3, b_xn, sib), cp(4, b_yn, sib),
                    cp(5, b_xn, yn, rows=0), cp(6, b_yn, xn, rows=1), cp(7, b_dg, sib, rows=0), cp(8, b_dg, sib, rows=1)]

        def keep(j, blk0):
            pair = pl.ds(pl.multiple_of(blk0, 2), 2)
            return pltpu.make_async_copy(wt.at[pair], wfull_ref.at[pair], loc.at[j])

        @pl.when((kk == 0) & (i == 0))
        def _():
            _peer_barrier("sxy")
            _cast_rows(wt.at[me], sh_ref)
            for s_ in sends()[0:3]:
                s_.start()

            def tables(j, _):
                chunk = pl.ds(pl.multiple_of(j * TM, TM), TM)
                c_ref[chunk, :], s1_ref[chunk, :], s2_ref[chunk, :] = _rope_tables(pos_ref[chunk, :], f_ref[...])
                return 0

            lax.fori_loop(0, S // TM, tables, 0)
            cp(0, me + flip, here).wait_recv()
            keep(0, me - c).start()

        @pl.when((kk == 1) & (i == 0))
        def _():
            cp(1, b_xn, here).wait_recv()
            sends()[5].start()
            sends()[3].start()
            cp(2, b_yn, here).wait_recv()
            sends()[6].start()
            sends()[4].start()
            cp(3, b_xn + flip, here).wait_recv()
            keep(1, b_xn - c).start()

        @pl.when((kk == 2) & (i == 0))
        def _():
            cp(4, b_yn + flip, here).wait_recv()
            keep(2, b_yn - c).start()

        @pl.when((kk == 3) & (i == 0))
        def _():
            cp(5, b_dg, here, rows=0).wait_recv()
            sends()[7].start()
            cp(6, b_dg, here, rows=1).wait_recv()
            sends()[8].start()
            cp(7, b_dg + flip, here, rows=0).wait_recv()
            cp(8, b_dg + flip, here, rows=1).wait_recv()
            keep(3, b_dg - c).start()

        rows = pl.ds(pl.multiple_of(i * tm, tm), tm)

        @pl.when(kk == 0)
        def _():
            xv = x_ref[...]
            r = lax.rsqrt(jnp.mean(xv * xv, axis=-1, keepdims=True) + EPS)
            hf = xv * r * nw_ref[...]
            ht_ref[...] = hf.T.astype(BF16)
            hs[rows, :] = hf.astype(BF16)

        h = hs[rows, :]
        chip = ord_ref[kk]
        for cc in range(2):
            p_ref[:, cc * R:(cc + 1) * R] = _dot_nt(h, wt[2 * chip + cc])

        @pl.when((kk == 3) & (i == nt - 1))
        def _():
            for s_ in sends():
                s_.wait_send()
            for j, blk in enumerate((me, b_xn, b_yn, b_dg)):
                keep(j, blk - c).wait()

    def first_pass(kk, i):
        return jnp.where(kk == 0, i, nt - 1)

    grid_spec = pltpu.PrefetchScalarGridSpec(
        num_scalar_prefetch=1, grid=(4, nt),
        in_specs=[pl.BlockSpec((tm, D), lambda kk, i, o: (first_pass(kk, i), 0)),
                  pl.BlockSpec((1, D), lambda kk, i, o: (0, 0)), VMEM, VMEM,
                  pl.BlockSpec((1, LANES), lambda kk, i, o: (0, 0))],
        out_specs=[pl.BlockSpec((D, tm), lambda kk, i, o: (0, first_pass(kk, i))),
                   pl.BlockSpec((tm, 2 * R), lambda kk, i, o: (i, o[kk])), ANY]
        + [pl.BlockSpec((S, LANES), lambda kk, i, o: (0, 0))] * 3,
        scratch_shapes=[pltpu.VMEM((NDEV, R, D), BF16), pltpu.VMEM((S, D), BF16), _sems(9), _sems(9), _sems(4)])
    res = pl.pallas_call(
        body, name="in_proj_gather", grid_spec=grid_spec,
        out_shape=[jax.ShapeDtypeStruct((D, S), BF16), jax.ShapeDtypeStruct((S, INW), F32),
                   jax.ShapeDtypeStruct((NDEV, R, D), BF16)] + [jax.ShapeDtypeStruct((S, LANES), F32)] * 3,
        compiler_params=_cp(dimension_semantics=("arbitrary", "arbitrary"), collective_id=BARRIER_IDS["sxy"]),
    )(chip_order, x, norm_w, shard_t, pos_col, _inv_freq_lanes())
    return res[0], res[1], res[2], tuple(res[3:])


def _qk_specs():
    nb = QKV // LANES
    return [pl.BlockSpec((S, LANES), functools.partial(lambda hp, g, o: (0, o + g * 4 + hp), o=o))
            for o in (OFF_Q // LANES, OFF_K // LANES, OFF_V // LANES)]


def _tab_specs():
    return [pl.BlockSpec((S, LANES), lambda hp, g: (0, 0), pipeline_mode=pl.Buffered(1))] * 3


def _vec_spec():
    return pl.BlockSpec((1, LANES), lambda hp, g: (0, 0))


def _sub_rows(r, d, start, n):
    if d == 1:
        return pl.ds(start, n)
    return pl.ds(r + d * start, n, stride=d)


def _band_window(i, L):
    W = min(TQ + 2 * HALF_SPAN, L)
    q0 = pl.multiple_of(i * TQ, TQ)
    k0 = pl.multiple_of(jnp.clip(q0 - HALF_SPAN, 0, L - W), HALF_SPAN)
    qpos = q0 + (lax.broadcasted_iota(jnp.int32, (2 * TQ, W), 0) & (TQ - 1))
    kpos = k0 + lax.broadcasted_iota(jnp.int32, (2 * TQ, W), 1)
    valid = jnp.abs(qpos - kpos) <= HALF_SPAN
    return W, q0, k0, valid


def _stack_heads(t, lo):
    z = jnp.zeros_like(t)
    return jnp.concatenate([jnp.where(lo, t, z), jnp.where(lo, z, t)], axis=0)


def _unstack_heads(t2, lo):
    return jnp.where(lo, t2[0:TQ], t2[TQ:2 * TQ])


CHAINS = 8


def _interleave(d):
    ru = min(d, CHAINS)
    return ru, min(CHAINS // ru, S // d // TQ)


def _for_blocks(n, fn):
    if n == 1:
        fn(0)
    else:
        def it(j, _):
            fn(j)
            return 0
        lax.fori_loop(0, n, it, 0)


def attn_fwd(proj, tabs, qw2, kw2, sides=()):
    CH = 256

    def body(q_ref, k_ref, v_ref, c_ref, s1_ref, s2_ref, qw_ref, kw_ref, at_ref, ls_ref,
             qs, ks, vs, osub, lsub, onat, lnat, qn, kn):
        g = pl.program_id(1)
        lo = lax.broadcasted_iota(jnp.int32, (1, LANES), 1) < HD
        e = _head_mat()

        def prep(i, _):
            rows = pl.ds(pl.multiple_of(i * CH, CH), CH)
            c, s1, s2 = c_ref[rows, :], s1_ref[rows, :], s2_ref[rows, :]
            for t_ref, w_ref, out, scale in ((q_ref, qw_ref, qn, HD ** -0.5), (k_ref, kw_ref, kn, 1.0)):
                t = t_ref[rows, :]
                r = lax.rsqrt(_head_mean(t * t, e) + EPS)
                out[rows, :] = _rope(t * r * w_ref[...], c, s1, s2) * scale
            return 0

        lax.fori_loop(0, S // CH, prep, 0, unroll=4)

        def group(gi, d):
            L = S // d

            ru, nb = _interleave(d)

            def stage(r, off):
                for c0 in range(0, L, CH):
                    n = min(CH, L)
                    rows = _sub_rows(r, d, c0, n)
                    dst = pl.ds(off + c0, n)
                    qs[dst, :] = qn[rows, :].astype(BF16)
                    ks[dst, :] = kn[rows, :].astype(BF16)
                    vs[dst, :] = v_ref[rows, :].astype(BF16)

            def one(off, i):
                W, q0, k0, valid = _band_window(i, L)
                q2 = _stack_heads(qs[pl.ds(off + q0, TQ), :], lo)
                sc = jnp.where(valid, _dot_nt(q2, ks[pl.ds(off + k0, W), :]), NEG_INF)
                m = jnp.max(sc, axis=-1, keepdims=True)
                p = jnp.exp(sc - m)
                den = jnp.sum(p, axis=-1, keepdims=True)
                o2 = _dot(p.astype(BF16), vs[pl.ds(off + k0, W), :]) / den
                l2 = jnp.broadcast_to(m + jnp.log(den), (2 * TQ, LANES))
                osub[pl.ds(off + q0, TQ), :] = _unstack_heads(o2, lo)
                lsub[pl.ds(off + q0, TQ), :] = _unstack_heads(l2, lo)

            def unstage(r, off):
                for c0 in range(0, L, CH):
                    n = min(CH, L)
                    rows = _sub_rows(r, d, c0, n)
                    onat[gi, rows, :] = osub[pl.ds(off + c0, n), :]
                    lnat[gi, rows, :] = lsub[pl.ds(off + c0, n), :]

            def step(t, _):
                for u in range(ru):
                    stage(t * ru + u, u * L)
                _for_blocks(L // TQ // nb, lambda j: [one(u * L, j * nb + b) for u in range(ru) for b in range(nb)])
                for u in range(ru):
                    unstage(t * ru + u, u * L)
                return 0

            lax.fori_loop(0, d // ru, step, 0)

        for gi, d in enumerate(DILATIONS):
            pl.when(g == gi)(functools.partial(group, gi, d))

        @pl.when(g == len(DILATIONS) - 1)
        def _():
            def mix(i, _):
                rows = pl.ds(pl.multiple_of(i * CH, CH), CH)
                l0, l1, l2 = lnat[0, rows, :], lnat[1, rows, :], lnat[2, rows, :]
                m = jnp.maximum(jnp.maximum(l0, l1), l2)
                e0, e1, e2 = jnp.exp(l0 - m), jnp.exp(l1 - m), jnp.exp(l2 - m)
                den = e0 + e1 + e2
                a = (e0 * onat[0, rows, :] + e1 * onat[1, rows, :] + e2 * onat[2, rows, :]) / den
                at_ref[rows, :] = a.astype(BF16)
                ls_ref[rows, :] = m + jnp.log(den)
                return 0

            lax.fori_loop(0, S // CH, mix, 0)

    out_spec = pl.BlockSpec((S, LANES), lambda hp, g: (0, hp))
    return _call(
        body, sides, name="attn_fwd", grid=(4, 3),
        in_specs=_qk_specs() + _tab_specs() + [_vec_spec(), _vec_spec()],
        out_specs=[out_spec, out_spec],
        out_shape=[jax.ShapeDtypeStruct((S, CC), BF16), jax.ShapeDtypeStruct((S, CC), F32)],
        scratch_shapes=[pltpu.VMEM((S, LANES), BF16)] * 3 + [pltpu.VMEM((S, LANES), F32)] * 2
        + [pltpu.VMEM((3, S, LANES), F32)] * 2 + [pltpu.VMEM((S, LANES), F32)] * 2,
        args=(proj, proj, proj, *tabs, qw2, kw2))


def attn_bwd(proj, tabs, qw2, kw2, d_attn, attn, lse, sides=()):
    CH = 256

    def body(q_ref, k_ref, v_ref, c_ref, s1_ref, s2_ref, qw_ref, kw_ref, do_ref, at_ref, ls_ref,
             dq_ref, dk_ref, dv_ref, gqw_ref, gkw_ref,
             qs, ks, vs, dos, dsub, lsub, dqs, dks, dvs, dnat, qx, kx, dvn, tnq, tnk, rrq, rrk):
        hp, g = pl.program_id(0), pl.program_id(1)
        lo = lax.broadcasted_iota(jnp.int32, (1, LANES), 1) < HD
        e = _head_mat()
        both = ((q_ref, qw_ref, qx, tnq, rrq, HD ** -0.5), (k_ref, kw_ref, kx, tnk, rrk, 1.0))

        @pl.when((hp == 0) & (g == 0))
        def _():
            gqw_ref[...] = jnp.zeros_like(gqw_ref)
            gkw_ref[...] = jnp.zeros_like(gkw_ref)

        def prep(i, _):
            rows = pl.ds(pl.multiple_of(i * CH, CH), CH)
            dnat[rows, :] = _head_mean(do_ref[rows, :] * at_ref[rows, :].astype(F32), e) * float(HD)
            c, s1, s2 = c_ref[rows, :], s1_ref[rows, :], s2_ref[rows, :]
            for t_ref, w_ref, x, tn_s, rr_s, scale in both:
                t = t_ref[rows, :]
                rr = lax.rsqrt(_head_mean(t * t, e) + EPS)
                tn = t * rr
                rr_s[rows, :] = rr
                tn_s[rows, :] = tn
                x[rows, :] = _rope(tn * w_ref[...], c, s1, s2) * scale
            return 0

        lax.fori_loop(0, S // CH, prep, 0, unroll=4)

        def group(d):
            L = S // d

            ru, nb = _interleave(d)

            def stage(r, off):
                for c0 in range(0, L, CH):
                    n = min(CH, L)
                    rows = _sub_rows(r, d, c0, n)
                    dst = pl.ds(off + c0, n)
                    qs[dst, :] = qx[rows, :].astype(BF16)
                    ks[dst, :] = kx[rows, :].astype(BF16)
                    vs[dst, :] = v_ref[rows, :].astype(BF16)
                    dos[dst, :] = do_ref[rows, :].astype(BF16)
                    dsub[dst, :] = dnat[rows, :]
                    lsub[dst, :] = ls_ref[rows, :]
                    dks[dst, :] = jnp.zeros((n, LANES), F32)
                    dvs[dst, :] = jnp.zeros((n, LANES), F32)

            def one(off, i):
                W, q0, k0, valid = _band_window(i, L)
                qrows, krows = pl.ds(off + q0, TQ), pl.ds(off + k0, W)
                q2 = _stack_heads(qs[qrows, :], lo)
                do2 = _stack_heads(dos[qrows, :], lo)
                kk, vv = ks[krows, :], vs[krows, :]
                lse_b, dd_b = lsub[qrows, :], dsub[qrows, :]
                lse2 = jnp.concatenate([lse_b[:, 0:1], lse_b[:, HD:HD + 1]], axis=0)
                dd2 = jnp.concatenate([dd_b[:, 0:1], dd_b[:, HD:HD + 1]], axis=0)
                sc = jnp.where(valid, _dot_nt(q2, kk), NEG_INF)
                p = jnp.exp(sc - lse2)
                ds = (p * (_dot_nt(do2, vv) - dd2)).astype(BF16)
                dqs[qrows, :] = _unstack_heads(_dot(ds, kk), lo)
                dks[krows, :] = dks[krows, :] + _dot_tn(ds, q2)
                dvs[krows, :] = dvs[krows, :] + _dot_tn(p.astype(BF16), do2)

            def unstage(r, off):
                for c0 in range(0, L, CH):
                    n = min(CH, L)
                    rows = _sub_rows(r, d, c0, n)
                    src = pl.ds(off + c0, n)
                    qx[rows, :] = dqs[src, :]
                    kx[rows, :] = dks[src, :]
                    dvn[rows, :] = dvs[src, :]

            def step(t, _):
                for u in range(ru):
                    stage(t * ru + u, u * L)
                _for_blocks(L // TQ // nb, lambda j: [one(u * L, j * nb + b) for u in range(ru) for b in range(nb)])
                for u in range(ru):
                    unstage(t * ru + u, u * L)
                return 0

            lax.fori_loop(0, d // ru, step, 0)

        for gi, d in enumerate(DILATIONS):
            pl.when(g == gi)(functools.partial(group, d))

        def emit(i, _):
            rows = pl.ds(pl.multiple_of(i * CH, CH), CH)
            c, s1, s2 = c_ref[rows, :], s1_ref[rows, :], s2_ref[rows, :]
            for (_, w_ref, x, tn_s, rr_s, scale), out, gw_ref in zip(both, (dq_ref, dk_ref), (gqw_ref, gkw_ref)):
                tn = tn_s[rows, :]
                dy = _rope_t(x[rows, :] * scale, c, s1, s2)
                gw_ref[0:1, :] = gw_ref[0:1, :] + jnp.sum(dy * tn, axis=0, keepdims=True)
                dtn = dy * w_ref[...]
                out[rows, :] = (rr_s[rows, :] * (dtn - tn * _head_mean(dtn * tn, e))).astype(BF16)
            dv_ref[rows, :] = dvn[rows, :].astype(BF16)
            return 0

        lax.fori_loop(0, S // CH, emit, 0, unroll=4)

    nat_spec = pl.BlockSpec((S, LANES), lambda hp, g: (0, hp))
    out_spec = pl.BlockSpec((None, S, LANES), lambda hp, g: (g, 0, hp))
    acc_spec = pl.BlockSpec((8, LANES), lambda hp, g: (0, 0))
    return _call(
        body, sides, name="attn_bwd", grid=(4, 3),
        in_specs=_qk_specs() + _tab_specs() + [_vec_spec(), _vec_spec(), nat_spec, nat_spec, nat_spec],
        out_specs=[out_spec] * 3 + [acc_spec] * 2,
        out_shape=[jax.ShapeDtypeStruct((QKV // PLANE, S, PLANE), BF16)] * 3 + [jax.ShapeDtypeStruct((8, LANES), F32)] * 2,
        scratch_shapes=[pltpu.VMEM((S, LANES), BF16)] * 4 + [pltpu.VMEM((S, LANES), F32)] * 13,
        args=(proj, proj, proj, *tabs, qw2, kw2, d_attn, attn, lse))


PADR = 16
CT = 128


def _conv_specs():
    return [pl.BlockSpec((S, CC), lambda i: (0, OFF_CA // CC)), pl.BlockSpec((S, CC), lambda i: (0, OFF_CB // CC))]


NCB = CC // LANES


def _pad_zero(pad):
    for cb in range(NCB):
        pad[cb, 0:PADR, :] = jnp.zeros((PADR, LANES), F32)
        pad[cb, PADR + S:PADR + S + PADR, :] = jnp.zeros((PADR, LANES), F32)


def _pad_store(pad, row0, n, val):
    for cb in range(NCB):
        pad[cb, pl.ds(pl.multiple_of(row0 + PADR, 8), n), :] = val[:, cb * LANES:(cb + 1) * LANES]


def _taps(pad_ref, cb, s0, weights):
    acc = jnp.zeros((CT, LANES), F32)
    for k in range(KW):
        acc = acc + weights[k] * pad_ref[cb, pl.ds(s0 + k + 1, CT), :]
    return acc


def conv_fwd(proj, conv_w, conv_b, ln_w, ln_b, sides=()):
    def body(a_ref, b_ref, w_ref, cb_ref, lw_ref, lb_ref, c_ref, u3_ref, upad):
        _pad_zero(upad)

        def glu(i, _):
            rows = pl.ds(pl.multiple_of(i * TM, TM), TM)
            _pad_store(upad, i * TM, TM, a_ref[rows, :] * _sigmoid(b_ref[rows, :]))
            return 0

        lax.fori_loop(0, S // TM, glu, 0)

        def chunk(i, _):
            s0 = pl.multiple_of(i * CT, CT)
            for cb in range(CC // LANES):
                cols = slice(cb * LANES, (cb + 1) * LANES)
                w = [w_ref[k:k + 1, cols] for k in range(KW)]
                c_ref[pl.ds(s0, CT), cols] = _taps(upad, cb, s0, w) + cb_ref[:, cols]
            cv = c_ref[pl.ds(s0, CT), :]
            mu = jnp.mean(cv, axis=-1, keepdims=True)
            xc = cv - mu
            rstd = lax.rsqrt(jnp.mean(xc * xc, axis=-1, keepdims=True) + EPS)
            yl = xc * rstd * lw_ref[...] + lb_ref[...]
            u3_ref[pl.ds(s0, CT), :] = (yl * _sigmoid(yl)).astype(BF16)
            return 0

        lax.fori_loop(0, S // CT, chunk, 0)

    vec = pl.BlockSpec((1, CC), lambda i: (0, 0))
    full = pl.BlockSpec((S, CC), lambda i: (0, 0))
    return _call(
        body, sides, name="conv_fwd", grid=(1,),
        in_specs=_conv_specs() + [pl.BlockSpec((KW, CC), lambda i: (0, 0)), vec, vec, vec],
        out_specs=[full, full],
        out_shape=[jax.ShapeDtypeStruct((S, CC), F32), jax.ShapeDtypeStruct((S, CC), BF16)],
        scratch_shapes=[pltpu.VMEM((NCB, S + 2 * PADR, LANES), F32)],
        args=(proj, proj, conv_w, conv_b, ln_w, ln_b))


def conv_bwd(proj, cpre, d_u3, conv_w, conv_w_rev, ln_w, ln_b, sides=()):
    def body(a_ref, b_ref, c_ref, du3_ref, w_ref, wr_ref, lw_ref, lb_ref,
             dc_ref, gw_ref, gcb_ref, glw_ref, glb_ref, upad, dpad):
        _pad_zero(upad)
        _pad_zero(dpad)
        gw_ref[...] = jnp.zeros_like(gw_ref)

        def ln_bwd(i, carry):
            gcb, glw, glb = carry
            rows = pl.ds(pl.multiple_of(i * TM, TM), TM)
            _pad_store(upad, i * TM, TM, a_ref[rows, :] * _sigmoid(b_ref[rows, :]))
            cv = c_ref[rows, :]
            mu = jnp.mean(cv, axis=-1, keepdims=True)
            xc = cv - mu
            rstd = lax.rsqrt(jnp.mean(xc * xc, axis=-1, keepdims=True) + EPS)
            xh = xc * rstd
            yl = xh * lw_ref[...] + lb_ref[...]
            dyl = du3_ref[rows, :] * _dsilu(yl, _sigmoid(yl))
            dxh = dyl * lw_ref[...]
            dcv = rstd * (dxh - jnp.mean(dxh, axis=-1, keepdims=True)
                          - xh * jnp.mean(dxh * xh, axis=-1, keepdims=True))
            _pad_store(dpad, i * TM, TM, dcv)
            return (gcb + jnp.sum(dcv, axis=0, keepdims=True),
                    glw + jnp.sum(dyl * xh, axis=0, keepdims=True),
                    glb + jnp.sum(dyl, axis=0, keepdims=True))

        z = jnp.zeros((1, CC), F32)
        gcb, glw, glb = lax.fori_loop(0, S // TM, ln_bwd, (z, z, z))
        gcb_ref[...] = gcb
        glw_ref[...] = glw
        glb_ref[...] = glb

        def chunk(i, _):
            s0 = pl.multiple_of(i * CT, CT)
            for cb in range(CC // LANES):
                cols = slice(cb * LANES, (cb + 1) * LANES)
                wr = [wr_ref[k:k + 1, cols] for k in range(KW)]
                du = _taps(dpad, cb, s0, wr)
                dcv = dpad[cb, pl.ds(s0 + PADR, CT), :]
                for k in range(KW):
                    gw_ref[k:k + 1, cols] = gw_ref[k:k + 1, cols] + jnp.sum(
                        upad[cb, pl.ds(s0 + k + 1, CT), :] * dcv, axis=0, keepdims=True)
                av = a_ref[pl.ds(s0, CT), cols]
                sb = _sigmoid(b_ref[pl.ds(s0, CT), cols])
                dc_ref[0, pl.ds(s0, CT), cols] = (du * sb).astype(BF16)
                dc_ref[1, pl.ds(s0, CT), cols] = (du * av * sb * (1.0 - sb)).astype(BF16)
            return 0

        lax.fori_loop(0, S // CT, chunk, 0)

    vec = pl.BlockSpec((1, CC), lambda i: (0, 0))
    full = pl.BlockSpec((S, CC), lambda i: (0, 0))
    wsp = pl.BlockSpec((KW, CC), lambda i: (0, 0))
    return _call(
        body, sides, name="conv_bwd", grid=(1,),
        in_specs=_conv_specs() + [full, full, wsp, wsp, vec, vec],
        out_specs=[pl.BlockSpec((2, S, CC), lambda i: (0, 0, 0)), wsp, vec, vec, vec],
        out_shape=[jax.ShapeDtypeStruct((2, S, CC), BF16), jax.ShapeDtypeStruct((KW, CC), F32)]
        + [jax.ShapeDtypeStruct((1, CC), F32)] * 3,
        scratch_shapes=[pltpu.VMEM((NCB, S + 2 * PADR, LANES), F32)] * 2,
        args=(proj, proj, cpre, d_u3, conv_w, conv_w_rev, ln_w, ln_b))


def _gate_specs():
    return [_row(CC, col=OFF_GA // CC + j) for j in range(4)]


def _gates(g_refs, bg_ref):
    ga = _sigmoid(jnp.concatenate([g_refs[0][...], g_refs[1][...]], axis=1) + bg_ref[0:1, :])
    gb = _sigmoid(jnp.concatenate([g_refs[2][...], g_refs[3][...]], axis=1) + bg_ref[1:2, :])
    return ga, gb


def mix_out(x, proj, b_gate, attn, u3, w_o, w_pw, w_out):
    def body(x_ref, g0, g1, g2, g3, bg_ref, at_ref, u3_ref, wo_ref, wp_ref, wout_ref,
             x1_ref, z_ref, ya_ref, yb_ref):
        ga, gb = _gates((g0, g1, g2, g3), bg_ref)
        ya = _dot_nt(at_ref[...], wo_ref[...])
        yb = _dot_nt(u3_ref[...], wp_ref[...])
        z = (ga * ya + gb * yb).astype(BF16)
        ya_ref[...] = ya.astype(BF16)
        yb_ref[...] = yb.astype(BF16)
        z_ref[...] = z
        x1_ref[...] = x_ref[...] + _dot(z, wout_ref[...])

    return pl.pallas_call(
        body, name="mix_out", grid=(S // TM,),
        in_specs=[_row(D)] + _gate_specs() + [_res((2, D)), _row(CC), _row(CC),
                                              _res((D, CC)), _res((D, CC)), _res((D, D))],
        out_specs=[_row(D)] * 4,
        out_shape=[jax.ShapeDtypeStruct((S, D), F32)] + [jax.ShapeDtypeStruct((S, D), BF16)] * 3,
        compiler_params=_cp(dimension_semantics=("arbitrary",)),
    )(x, proj, proj, proj, proj, b_gate, attn, u3, w_o, w_pw, w_out)


def out_bwd(d_x1b, proj, b_gate, ya, yb, w_o, w_pw, w_out, sides=()):
    def body(dx_ref, g0, g1, g2, g3, bg_ref, ya_ref, yb_ref, wo_ref, wp_ref, wout_ref,
             dya_ref, dyb_ref, dgl_ref, dat_ref, du3_ref, gbg_ref):
        @pl.when(pl.program_id(0) == 0)
        def _():
            gbg_ref[...] = jnp.zeros_like(gbg_ref)

        ga, gb = _gates((g0, g1, g2, g3), bg_ref)
        dz = _dot_nt(dx_ref[...], wout_ref[...])
        dya = (dz * ga).astype(BF16)
        dyb = (dz * gb).astype(BF16)
        dgla = dz * ya_ref[...].astype(F32) * ga * (1.0 - ga)
        dglb = dz * yb_ref[...].astype(F32) * gb * (1.0 - gb)
        dya_ref[...] = dya
        dyb_ref[...] = dyb
        for j in range(2):
            dgl_ref[j] = dgla[:, j * PLANE:(j + 1) * PLANE].astype(BF16)
            dgl_ref[2 + j] = dglb[:, j * PLANE:(j + 1) * PLANE].astype(BF16)
        gbg_ref[0:1, :] = gbg_ref[0:1, :] + jnp.sum(dgla, axis=0, keepdims=True)
        gbg_ref[1:2, :] = gbg_ref[1:2, :] + jnp.sum(dglb, axis=0, keepdims=True)
        dat_ref[...] = _dot(dya, wo_ref[...])
        du3_ref[...] = _dot(dyb, wp_ref[...])

    return _call(
        body, sides, name="out_bwd", grid=(S // TM,),
        in_specs=[_row(D)] + _gate_specs() + [_res((2, D)), _row(D), _row(D),
                                              _res((D, CC)), _res((D, CC)), _res((D, D))],
        out_specs=[_row(D), _row(D), _planes(2 * D), _row(CC), _row(CC), pl.BlockSpec((2, D), lambda i: (0, 0))],
        out_shape=[jax.ShapeDtypeStruct((S, D), BF16)] * 2 + [jax.ShapeDtypeStruct((2 * D // PLANE, S, PLANE), BF16)]
        + [jax.ShapeDtypeStruct((S, CC), F32)] * 2 + [jax.ShapeDtypeStruct((2, D), F32)],
        args=(d_x1b, proj, proj, proj, proj, b_gate, ya, yb, w_o, w_pw, w_out))


def ffn_in(x1, norm_w, w_ffn_in, sides=()):
    half = FF // 2

    def body(x_ref, nw_ref, w_ref, h_ref, gu_ref, f_ref):
        xv = x_ref[...]
        r = lax.rsqrt(jnp.mean(xv * xv, axis=-1, keepdims=True) + EPS)
        h = (xv * r * nw_ref[...]).astype(BF16)
        h_ref[...] = h
        for j in range(2):
            gt = _dot_nt(h, w_ref[j * half:(j + 1) * half, :])
            up = _dot_nt(h, w_ref[FF + j * half:FF + (j + 1) * half, :])
            gu_ref[:, j * half:(j + 1) * half] = gt.astype(BF16)
            gu_ref[:, FF + j * half:FF + (j + 1) * half] = up.astype(BF16)
            f_ref[:, j * half:(j + 1) * half] = (gt * _sigmoid(gt) * up).astype(BF16)

    return _call(
        body, sides, name="ffn_in", grid=(S // TM,),
        in_specs=[_row(D), _res((1, D)), _res((2 * FF, D))],
        out_specs=[_row(D), _row(2 * FF), _row(FF)],
        out_shape=[jax.ShapeDtypeStruct((S, D), BF16), jax.ShapeDtypeStruct((S, 2 * FF), BF16),
                   jax.ShapeDtypeStruct((S, FF), BF16)],
        args=(x1, norm_w, w_ffn_in))


def ffn_out_loss(x1, f, w_ffn_out, target):
    def body(x_ref, f_ref, w_ref, t_ref, dy_ref, dyb_ref, sq_ref):
        @pl.when(pl.program_id(0) == 0)
        def _():
            sq_ref[...] = jnp.zeros_like(sq_ref)

        diff = x_ref[...] + _dot(f_ref[...], w_ref[...]) - t_ref[...]
        dy = diff * (1.0 / D)
        dy_ref[...] = dy
        dyb_ref[...] = dy.astype(BF16)
        sq_ref[...] = sq_ref[...] + jnp.sum((diff * diff).reshape(TM // 8, 8, D), axis=0)

    return pl.pallas_call(
        body, name="ffn_out_loss", grid=(S // TM,),
        in_specs=[_row(D), _row(FF), _res((FF, D)), _row(D)],
        out_specs=[_row(D), _row(D), pl.BlockSpec((8, D), lambda i: (0, 0))],
        out_shape=[jax.ShapeDtypeStruct((S, D), F32), jax.ShapeDtypeStruct((S, D), BF16),
                   jax.ShapeDtypeStruct((8, D), F32)],
        compiler_params=_cp(dimension_semantics=("arbitrary",)),
    )(x1, f, w_ffn_out, target)


def _rms_bwd(xv, nw, dh):
    r = lax.rsqrt(jnp.mean(xv * xv, axis=-1, keepdims=True) + EPS)
    xn = xv * r
    dxn = dh * nw
    dx = r * (dxn - xn * jnp.mean(dxn * xn, axis=-1, keepdims=True))
    return dx, dh * xn


def ffn_bwd(dy, dyb, gu, x1, norm_w, w_ffn_in, w_ffn_out, sides=()):
    def body(dy_ref, dyb_ref, gu_ref, x_ref, nw_ref, wi_ref, wo_ref, dgu_ref, dx_ref, dxb_ref, gn_ref):
        @pl.when(pl.program_id(0) == 0)
        def _():
            gn_ref[...] = jnp.zeros_like(gn_ref)

        df = _dot_nt(dyb_ref[...], wo_ref[...])
        gt = gu_ref[:, 0:FF].astype(F32)
        up = gu_ref[:, FF:2 * FF].astype(F32)
        sg = _sigmoid(gt)
        dgt = (df * up * _dsilu(gt, sg)).astype(BF16)
        dup = (df * gt * sg).astype(BF16)
        dgu_ref[:, 0:FF] = dgt
        dgu_ref[:, FF:2 * FF] = dup
        dh = _dot(dgt, wi_ref[0:FF, :]) + _dot(dup, wi_ref[FF:2 * FF, :])
        dxn, gw = _rms_bwd(x_ref[...], nw_ref[...], dh)
        dx = dy_ref[...] + dxn
        dx_ref[...] = dx
        dxb_ref[...] = dx.astype(BF16)
        gn_ref[...] = gn_ref[...] + jnp.sum(gw, axis=0, keepdims=True)

    return _call(
        body, sides, name="ffn_bwd", grid=(S // TM,),
        in_specs=[_row(D), _row(D), _row(2 * FF), _row(D), _res((1, D)), _res((2 * FF, D)), _res((FF, D))],
        out_specs=[_row(2 * FF), _row(D), _row(D), pl.BlockSpec((1, D), lambda i: (0, 0))],
        out_shape=[jax.ShapeDtypeStruct((S, 2 * FF), BF16), jax.ShapeDtypeStruct((S, D), F32),
                   jax.ShapeDtypeStruct((S, D), BF16), jax.ShapeDtypeStruct((1, D), F32)],
        args=(dy, dyb, gu, x1, norm_w, w_ffn_in, w_ffn_out))


def in_bwd(d_q, d_k, d_v, d_conv, d_gl, w_in, x, d_x1, norm_w, sides=()):
    segs = ((OFF_Q, QKV), (OFF_K, QKV), (OFF_V, QKV), (OFF_CA, 2 * CC), (OFF_GA, 2 * D))

    def body(dq_ref, dk_ref, dv_ref, dc_ref, dg_ref, w_ref, x_ref, dx1_ref, nw_ref, gx_ref, gn_ref):
        @pl.when(pl.program_id(0) == 0)
        def _():
            gn_ref[...] = jnp.zeros_like(gn_ref)

        dh = jnp.zeros((TM, D), F32)
        for ref, (off, width) in zip((dq_ref, dk_ref, dv_ref, dc_ref, dg_ref), segs):
            for j in range(width // PLANE):
                dh = dh + _dot(ref[j], w_ref[off + j * PLANE:off + (j + 1) * PLANE, :])
        dxn, gw = _rms_bwd(x_ref[...], nw_ref[...], dh)
        gx_ref[...] = dx1_ref[...] + dxn
        gn_ref[...] = gn_ref[...] + jnp.sum(gw, axis=0, keepdims=True)

    return _call(
        body, sides, name="in_bwd", grid=(S // TM,),
        in_specs=[_planes(QKV)] * 3 + [_planes(2 * CC), _planes(2 * D), _res((INW, D)), _row(D), _row(D), _res((1, D))],
        out_specs=[_row(D), pl.BlockSpec((1, D), lambda i: (0, 0))],
        out_shape=[jax.ShapeDtypeStruct((S, D), F32), jax.ShapeDtypeStruct((1, D), F32)],
        args=(d_q, d_k, d_v, d_conv, d_gl, w_in, x, d_x1, norm_w))


def mm_tn(name, a, b, tm, tn):
    M, N = a.shape[1], b.shape[1]

    def body(a_ref, b_ref, o_ref, ob_ref):
        r = _dot_tn(a_ref[...], b_ref[...])
        o_ref[...] = r
        ob_ref[...] = r.astype(BF16)

    return _call(
        body, name=name, grid=(M // tm, N // tn),
        in_specs=[pl.BlockSpec((S, tm), lambda i, j: (0, i)), pl.BlockSpec((S, tn), lambda i, j: (0, j))],
        out_specs=[pl.BlockSpec((tm, tn), lambda i, j: (i, j))] * 2,
        out_shape=[jax.ShapeDtypeStruct((M, N), F32), jax.ShapeDtypeStruct((M, N), BF16)],
        args=(a, b))


GW_IN_TN = PLANE
GW_IN_SPLIT = (768, 256)


def gw_in_t(name, ht, d_segs, col0, hw, sides=()):
    tn = GW_IN_TN
    starts, t0 = [], 0
    for seg in d_segs:
        starts.append(t0)
        t0 += seg.shape[0]
    ntiles = [seg.shape[0] for seg in d_segs]

    def body(h_ref, *refs):
        a_refs, o_ref, ob_ref = refs[:-2], refs[-2], refs[-1]
        n = pl.program_id(0)
        for a_ref, st, nt in zip(a_refs, starts, ntiles):
            @pl.when((n >= st) & (n < st + nt))
            def _(a_ref=a_ref):
                r = _dot(h_ref[...], a_ref[...]).T
                o_ref[...] = r
                ob_ref[...] = r.astype(BF16)

    def seg_spec(st, nt):
        return pl.BlockSpec((None, S, tn), lambda n: (jnp.clip(n - st, 0, nt - 1), 0, 0))

    res = _call(
        body, sides, name=name, grid=(INW // tn,),
        in_specs=[pl.BlockSpec((hw, S), lambda n: (col0 // hw, 0))] + [seg_spec(st, nt) for st, nt in zip(starts, ntiles)],
        out_specs=[pl.BlockSpec((tn, hw), lambda n: (n, 0))] * 2,
        out_shape=[jax.ShapeDtypeStruct((INW, hw), F32), jax.ShapeDtypeStruct((INW, hw), BF16)],
        args=(ht, *d_segs))
    return (res[0], res[1]) if sides else (res, [])


def _place():
    x, y, c = lax.axis_index("x"), lax.axis_index("y"), lax.axis_index("c")
    chips = [(1 - x, y), (x, 1 - y), (1 - x, 1 - y)]
    return x, y, c, chips


def _sems(n):
    return pltpu.SemaphoreType.DMA((n,))


def _remote(src, dst, send, recv, k, to):
    return pltpu.make_async_remote_copy(src_ref=src, dst_ref=dst, send_sem=send.at[k], recv_sem=recv.at[k],
                                        device_id=to, device_id_type=MESH)


def _cast_rows(dst, src, cols=slice(None)):
    rows = src.shape[0]
    step = next((s for s in (128, 64, 32, 16) if rows % s == 0), rows)
    for r0 in range(0, rows, step):
        dst[r0:r0 + step, cols] = src[r0:r0 + step, :].astype(dst.dtype)


def comm_only(name, sides):
    def body():
        pass

    return _call(body, sides, name=name, grid=(1,), in_specs=[], out_specs=[], out_shape=[], args=())[1]


def ag_blocks(shard, dtype):
    R, W = shard.shape

    def copy(outs, scr, k, block, to, src=None):
        dst = outs[0].at[block]
        return _remote(dst if src is None else src, dst, scr[1], scr[2], k, to)

    def local(outs, scr, me):
        return pltpu.make_async_copy(scr[0], outs[0].at[me], scr[3].at[0])

    def start(ins, outs, scr):
        x, y, c, chips = _place()
        me = 4 * x + 2 * y + c
        _cast_rows(scr[0], ins[0])
        local(outs, scr, me).start()
        copy(outs, scr, 0, me, (x, y, 1 - c), src=scr[0]).start()
        for j, (cx, cy) in enumerate(chips):
            copy(outs, scr, 1 + j, me, (cx, cy, c), src=scr[0]).start()

    def finish(ins, outs, scr):
        x, y, c, chips = _place()
        me, sib = 4 * x + 2 * y + c, (x, y, 1 - c)
        passed = []
        for j, (cx, cy) in enumerate(chips):
            theirs = 4 * cx + 2 * cy + c
            copy(outs, scr, 1 + j, theirs, (x, y, c)).wait_recv()
            fwd = copy(outs, scr, 4 + j, theirs, sib)
            fwd.start()
            passed.append(fwd)
        copy(outs, scr, 0, 4 * x + 2 * y + 1 - c, (x, y, c)).wait_recv()
        for j, (cx, cy) in enumerate(chips):
            copy(outs, scr, 4 + j, 4 * cx + 2 * cy + 1 - c, (x, y, c)).wait_recv()
        copy(outs, scr, 0, me, sib, src=scr[0]).wait_send()
        for j, (cx, cy) in enumerate(chips):
            copy(outs, scr, 1 + j, me, (cx, cy, c), src=scr[0]).wait_send()
        for fwd in passed:
            fwd.wait_send()
        local(outs, scr, me).wait()

    return Side((shard,), (VMEM,), (jax.ShapeDtypeStruct((NDEV, R, W), dtype),),
                (pltpu.VMEM((R, W), dtype), _sems(7), _sems(7), _sems(1)), start, finish, None, "dsxy")


def ag_blocks_relay(shard, dtype, transpose=False):
    R, W = shard.shape[::-1] if transpose else shard.shape
    half = R // 2

    def copy(outs, scr, k, block, to, src=None, rows=None):
        dst = outs[0].at[block] if rows is None else outs[0].at[block, pl.ds(rows * half, half), :]
        return _remote(dst if src is None else src, dst, scr[1], scr[2], k, to)

    def local(outs, scr, me):
        return pltpu.make_async_copy(scr[0], outs[0].at[me], scr[3].at[0])

    def own(outs, scr):
        x, y, c, _ = _place()
        me = 4 * x + 2 * y + c
        return [copy(outs, scr, k, me, to, src=scr[0])
                for k, to in enumerate([(x, y, 1 - c), (1 - x, y, c), (x, 1 - y, c)])]

    def start(ins, outs, scr):
        x, y, c, _ = _place()
        if transpose:
            scr[0][...] = ins[0][...].T.astype(dtype)
        else:
            _cast_rows(scr[0], ins[0])
        local(outs, scr, 4 * x + 2 * y + c).start()
        for cp in own(outs, scr):
            cp.start()

    def passed_on(outs, scr):
        x, y, c, _ = _place()
        sib, xn, yn = (x, y, 1 - c), (1 - x, y, c), (x, 1 - y, c)
        b_xn, b_yn, b_dg = 4 * (1 - x) + 2 * y + c, 4 * x + 2 * (1 - y) + c, 4 * (1 - x) + 2 * (1 - y) + c
        near = [copy(outs, scr, 5, b_xn, yn, rows=0), copy(outs, scr, 3, b_xn, sib),
                copy(outs, scr, 6, b_yn, xn, rows=1), copy(outs, scr, 4, b_yn, sib)]
        far = [copy(outs, scr, 7, b_dg, sib, rows=0), copy(outs, scr, 8, b_dg, sib, rows=1)]
        return (b_xn, b_yn, b_dg), near, far

    def mid(ins, outs, scr):
        x, y, c, _ = _place()
        (b_xn, b_yn, _), near, _ = passed_on(outs, scr)
        copy(outs, scr, 1, b_xn, (x, y, c)).wait_recv()
        near[0].start()
        near[1].start()
        copy(outs, scr, 2, b_yn, (x, y, c)).wait_recv()
        near[2].start()
        near[3].start()

    def finish(ins, outs, scr):
        x, y, c, _ = _place()
        here = (x, y, c)
        (b_xn, b_yn, b_dg), near, far = passed_on(outs, scr)
        copy(outs, scr, 5, b_dg, here, rows=0).wait_recv()
        far[0].start()
        copy(outs, scr, 6, b_dg, here, rows=1).wait_recv()
        far[1].start()
        flip = 1 - 2 * c
        copy(outs, scr, 0, 4 * x + 2 * y + 1 - c, here).wait_recv()
        copy(outs, scr, 3, b_xn + flip, here).wait_recv()
        copy(outs, scr, 4, b_yn + flip, here).wait_recv()
        copy(outs, scr, 7, b_dg + flip, here, rows=0).wait_recv()
        copy(outs, scr, 8, b_dg + flip, here, rows=1).wait_recv()
        for cp in own(outs, scr) + near + far:
            cp.wait_send()
        local(outs, scr, 4 * x + 2 * y + c).wait()

    return Side((shard,), (VMEM,), (jax.ShapeDtypeStruct((NDEV, R, W), dtype),),
                (pltpu.VMEM((R, W), dtype), _sems(9), _sems(9), _sems(1)), start, finish, mid, "sxy")


def copies_side(args, out_shape, n_copies, plan, peers):
    def copies(ins, outs, scr):
        return [_remote(s_, d_, scr[0], scr[1], i, to) for i, (s_, d_, to) in enumerate(plan(ins, outs))]

    def start(ins, outs, scr):
        for cp in copies(ins, outs, scr):
            cp.start()

    def finish(ins, outs, scr):
        for cp in copies(ins, outs, scr):
            cp.wait()

    return Side(tuple(args), (ANY,) * len(args), tuple(out_shape), (_sems(n_copies), _sems(n_copies)),
                start, finish, None, peers)


def rs_to_sibling(grads):
    out_shape = [jax.ShapeDtypeStruct((4,) + g.shape[1:], BF16) for g in grads]

    def plan(ins, outs):
        x, y, c, _ = _place()
        return [(g.at[2 * k + 1 - c], r.at[k], (x, y, 1 - c)) for g, r in zip(ins, outs) for k in range(4)]

    return copies_side(grads, out_shape, 4 * len(grads), plan, "s")


def rs_to_chips(parts):
    out_shape = [jax.ShapeDtypeStruct((3,) + p.shape[1:], BF16) for p in parts]

    def plan(ins, outs):
        x, y, c, chips = _place()
        return [(p.at[2 * cx + cy], r.at[j], (cx, cy, c))
                for p, r in zip(ins, outs) for j, (cx, cy) in enumerate(chips)]

    return copies_side(parts, out_shape, 3 * len(parts), plan, "dxy")


def rs_to_chips_combined(part):
    _, R, W = part.shape
    half = R // 2
    top, bot = pl.ds(0, half), pl.ds(half, half)

    def copies(ins, outs, scr):
        p, r = ins[0], outs[0]
        loc_a, loc_b, in_x, in_y, comb_a, comb_b, send, recv, loc = scr
        x, y, c, _ = _place()
        xn, yn = (1 - x, y, c), (x, 1 - y, c)
        k_xn, k_yn, k_dg = 2 * (1 - x) + y, 2 * x + 1 - y, 2 * (1 - x) + 1 - y
        direct = [_remote(p.at[k_xn, top, :], r.at[0, top, :], send, recv, 0, xn),
                  _remote(p.at[k_yn, bot, :], r.at[1, bot, :], send, recv, 1, yn),
                  _remote(p.at[k_dg, top, :], in_x, send, recv, 2, xn),
                  _remote(p.at[k_dg, bot, :], in_y, send, recv, 3, yn)]
        combined = [_remote(comb_a, r.at[1, top, :], send, recv, 4, yn),
                    _remote(comb_b, r.at[0, bot, :], send, recv, 5, xn)]
        local = [pltpu.make_async_copy(p.at[k_yn, top, :], loc_a, loc.at[0]),
                 pltpu.make_async_copy(p.at[k_xn, bot, :], loc_b, loc.at[1])]
        return direct, combined, local

    def start(ins, outs, scr):
        direct, _, local = copies(ins, outs, scr)
        for cp in local + direct:
            cp.start()

    def mid(ins, outs, scr):
        loc_a, loc_b, in_x, in_y, comb_a, comb_b = scr[:6]
        direct, combined, local = copies(ins, outs, scr)
        for mine, arrival, inbox, out, nxt in ((local[0], direct[2], in_x, comb_a, combined[0]),
                                               (local[1], direct[3], in_y, comb_b, combined[1])):
            mine.wait()
            arrival.wait_recv()
            src = loc_a if out is comb_a else loc_b
            out[...] = (src[...].astype(F32) + inbox[...].astype(F32)).astype(BF16)
            nxt.start()

    def finish(ins, outs, scr):
        direct, combined, _ = copies(ins, outs, scr)
        direct[0].wait_recv()
        direct[1].wait_recv()
        combined[0].wait_recv()
        combined[1].wait_recv()
        for cp in direct + combined:
            cp.wait_send()

    buf = pltpu.VMEM((half, W), BF16)
    return Side((part,), (ANY,), (jax.ShapeDtypeStruct((2, R, W), BF16),),
                (buf, buf, buf, buf, buf, buf, _sems(6), _sems(6), _sems(2)), start, finish, mid, "xy")


ADAM_TILE_BYTES = 3 * 512 * 1024


def _row_tiles(rows, width):
    return 2 if rows % 32 == 0 and rows * width * 4 > ADAM_TILE_BYTES else 1


def chip_sum(name, grad, recv, c_idx, chip_idx):
    _, R, C = grad.shape
    nt = 1
    tr = R // nt

    def body(s_ref, g_ref, r_ref, p_ref, own_ref):
        k = pl.program_id(1)
        tot = g_ref[0] + r_ref[0].astype(F32)
        p_ref[0] = tot.astype(BF16)

        @pl.when(k == s_ref[1])
        def _():
            own_ref[...] = tot

    grid_spec = pltpu.PrefetchScalarGridSpec(
        num_scalar_prefetch=1, grid=(nt, 4),
        in_specs=[pl.BlockSpec((1, tr, C), lambda i, k, s: (2 * k + s[0], i, 0)),
                  pl.BlockSpec((1, tr, C), lambda i, k, s: (k, i, 0))],
        out_specs=[pl.BlockSpec((1, tr, C), lambda i, k, s: (k, i, 0)),
                   pl.BlockSpec((tr, C), lambda i, k, s: (i, 0))])
    return pl.pallas_call(
        body, name=name, grid_spec=grid_spec,
        out_shape=[jax.ShapeDtypeStruct((4, R, C), BF16), jax.ShapeDtypeStruct((R, C), F32)],
        compiler_params=_cp(dimension_semantics=("arbitrary", "arbitrary")),
    )(jnp.stack([c_idx, chip_idx]), grad, recv)


def _adamw(w, g, m, v):
    m2 = ADAM_B1 * m + (1.0 - ADAM_B1) * g
    v2 = ADAM_B2 * v + (1.0 - ADAM_B2) * (g * g)
    m_hat = m2 / (1.0 - ADAM_B1 ** ADAM_STEP)
    v_hat = v2 / (1.0 - ADAM_B2 ** ADAM_STEP)
    delta = -ADAM_LR * (m_hat / (jnp.sqrt(v_hat) + ADAM_EPS) + ADAM_WD * w)
    return delta, m2, v2


def shard_adam(name, owns, recvs, w, m, v, io_t=False):
    n = len(owns)
    R = owns[0].shape[0]
    ct = min(o.shape[1] for o in owns)
    first = [sum(o.shape[1] for o in owns[:j]) // ct for j in range(n)]
    count = [o.shape[1] // ct for o in owns]
    nt = _row_tiles(R, ct)
    tr = R // nt

    def body(*refs):
        o_refs, r_refs = refs[:n], refs[n:2 * n]
        w_ref, m_ref, v_ref, g_ref, d_ref, nm_ref, nv_ref = refs[2 * n:]
        g = None
        for j in range(n):
            gj = o_refs[j][...]
            for q in range(recvs[j].shape[0]):
                gj = gj + r_refs[j][q].astype(F32)
            g = gj if g is None else jnp.where(pl.program_id(0) >= first[j], gj, g)
        t = (lambda a: a.T) if io_t else (lambda a: a)
        delta, m2, v2 = _adamw(t(w_ref[...]), g, t(m_ref[...]), t(v_ref[...]))
        g_ref[...] = t(g)
        d_ref[...] = t(delta)
        nm_ref[...] = t(m2)
        nv_ref[...] = t(v2)

    def part(j):
        return pl.BlockSpec((tr, ct), lambda k, i: (i, jnp.clip(k - first[j], 0, count[j] - 1)))

    def part3(j):
        return pl.BlockSpec((recvs[j].shape[0], tr, ct), lambda k, i: (0, i, jnp.clip(k - first[j], 0, count[j] - 1)))

    C = sum(count) * ct
    tile = pl.BlockSpec((ct, tr), lambda k, i: (k, i)) if io_t else pl.BlockSpec((tr, ct), lambda k, i: (i, k))
    return pl.pallas_call(
        body, name=name, grid=(sum(count), nt),
        in_specs=[part(j) for j in range(n)] + [part3(j) for j in range(n)] + [tile, tile, tile],
        out_specs=[tile] * 4, out_shape=[jax.ShapeDtypeStruct((C, R) if io_t else (R, C), F32)] * 4,
        compiler_params=_cp(dimension_semantics=("arbitrary", "arbitrary")),
    )(*owns, *recvs, w, m, v)


ROW_N1, ROW_N2, ROW_BG, ROW_QN, ROW_KN, ROW_CB, ROW_LW, ROW_LB, ROW_CW = 0, 1, 2, 4, 5, 6, 7, 8, 9
PACK_ROWS = 40
SMALL = ("norm1_w", "norm2_w", "b_gate", "q_norm_w", "k_norm_w", "conv_b", "conv_ln_w", "conv_ln_b", "conv_w")


def small_sync(g, sq, sides=()):
    ns = len(SMALL)

    def copies(refs):
        pack, recv, send_sems, recv_sems = refs[ns + 2:]
        x, y, c, _ = _place()
        return [pltpu.make_async_remote_copy(
            src_ref=pack, dst_ref=recv.at[4 * x + 2 * y + c], send_sem=send_sems.at[k - 1],
            recv_sem=recv_sems.at[k - 1], device_id=(x ^ (k >> 2), y ^ ((k >> 1) & 1), c ^ (k & 1)),
            device_id_type=MESH) for k in range(1, NDEV)]

    def body(*refs):
        gi = dict(zip(SMALL, refs[:ns]))
        sq_ref, tot, pack, recv, send_sems, recv_sems = refs[ns:]
        x, y, c, _ = _place()
        me = 4 * x + 2 * y + c

        pack[...] = jnp.zeros_like(pack)
        pack[ROW_KN:ROW_KN + 1, LANES:2 * LANES] = jnp.full((1, LANES), (0.5 / D) * jnp.sum(sq_ref[...]), F32)
        pack[ROW_N1:ROW_N1 + 1, :] = gi["norm1_w"][...]
        pack[ROW_N2:ROW_N2 + 1, :] = gi["norm2_w"][...]
        pack[ROW_BG:ROW_BG + 2, :] = gi["b_gate"][...]
        pack[ROW_QN:ROW_QN + 1, 0:HD] = gi["q_norm_w"][...]
        pack[ROW_KN:ROW_KN + 1, 0:HD] = gi["k_norm_w"][...]
        pack[ROW_CB:ROW_CB + 1, 0:CC] = gi["conv_b"][...]
        pack[ROW_LW:ROW_LW + 1, 0:CC] = gi["conv_ln_w"][...]
        pack[ROW_LB:ROW_LB + 1, 0:CC] = gi["conv_ln_b"][...]
        pack[ROW_CW:ROW_CW + KW, 0:CC] = gi["conv_w"][...]

        for cp in copies(refs):
            cp.start()
        recv[me] = pack[...]

    def tail(*refs):
        tot, recv = refs[ns + 1], refs[ns + 3]
        for cp in copies(refs):
            cp.wait()
        acc = recv[0]
        for p in range(1, NDEV):
            acc = acc + recv[p]
        tot[...] = acc

    args = [g[k] for k in SMALL] + [sq]
    res = _call(
        body, sides, name="small_sync", grid=(1,), in_specs=[VMEM] * len(args), out_specs=[VMEM],
        out_shape=[jax.ShapeDtypeStruct((PACK_ROWS, D), F32)],
        scratch_shapes=[pltpu.VMEM((PACK_ROWS, D), F32), pltpu.VMEM((NDEV, PACK_ROWS, D), F32),
                        _sems(NDEV - 1), _sems(NDEV - 1)],
        args=args, own_comm=True, tail=tail)
    return (res[0][0], res[1]) if sides else res[0]


def small_adam(tot, w, m, v, me):
    ns = len(SMALL)

    def body(me_ref, tot, *refs):
        wi = dict(zip(SMALL, refs[:ns]))
        mi = dict(zip(SMALL, refs[ns:2 * ns]))
        vi = dict(zip(SMALL, refs[2 * ns:3 * ns]))
        outs = refs[3 * ns:7 * ns]
        loss_ref = refs[7 * ns]
        me = me_ref[0]

        def shard_grad(name):
            if name == "b_gate":
                return tot[ROW_BG:ROW_BG + 2, pl.ds(pl.multiple_of(me * LANES, LANES), LANES)]
            if name == "conv_w":
                win = tot[ROW_CW:ROW_CW + KW, pl.ds(pl.multiple_of((me // 2) * LANES, LANES), LANES)]
                return jnp.where(me % 2 == 1, win[:, HD:LANES], win[:, 0:HD])
            row = {"norm1_w": ROW_N1, "norm2_w": ROW_N2, "q_norm_w": ROW_QN, "k_norm_w": ROW_KN,
                   "conv_b": ROW_CB, "conv_ln_w": ROW_LW, "conv_ln_b": ROW_LB}[name]
            return tot[row:row + 1, 0:wi[name].shape[1]]

        for i, name in enumerate(SMALL):
            gr = shard_grad(name)
            delta, m2, v2 = _adamw(wi[name][...], gr, mi[name][...], vi[name][...])
            outs[4 * i][...] = gr
            outs[4 * i + 1][...] = delta
            outs[4 * i + 2][...] = m2
            outs[4 * i + 3][...] = v2
        loss_ref[...] = tot[ROW_KN:ROW_KN + 1, LANES:2 * LANES]

    out_shape = []
    for name in SMALL:
        out_shape += [jax.ShapeDtypeStruct(w[name].shape, F32)] * 4
    out_shape.append(jax.ShapeDtypeStruct((1, LANES), F32))
    args = [tot] + [w[k] for k in SMALL] + [m[k] for k in SMALL] + [v[k] for k in SMALL]
    grid_spec = pltpu.PrefetchScalarGridSpec(
        num_scalar_prefetch=1, grid=(1,), in_specs=[VMEM] * len(args), out_specs=[VMEM] * len(out_shape))
    res = pl.pallas_call(body, name="small_adam", grid_spec=grid_spec, out_shape=out_shape)(me, *args)
    out = {name: tuple(res[4 * i:4 * i + 4]) for i, name in enumerate(SMALL)}
    return out, res[4 * ns][0, 0]


MATS = ("w_in", "w_o_attn", "w_pw_conv", "w_out", "w_ffn_in", "w_ffn_out")
TRANSPOSED = ("w_in", "w_ffn_in")
WEIGHTS = ("norm1_w", "w_in", "b_gate", "q_norm_w", "k_norm_w", "w_o_attn", "conv_w", "conv_b", "conv_ln_w",
           "conv_ln_b", "w_pw_conv", "w_out", "norm2_w", "w_ffn_in", "w_ffn_out")


def _blocks_to_cols(blocks):
    n, R, C = blocks.shape
    return blocks.transpose(1, 0, 2).reshape(R, n * C)


def kernel(x, positions, norm1_w, w_in, b_gate, q_norm_w, k_norm_w, w_o_attn, conv_w, conv_b, conv_ln_w, conv_ln_b, w_pw_conv, w_out, norm2_w, w_ffn_in, w_ffn_out, loss_target, m_norm1_w, m_w_in, m_b_gate, m_q_norm_w, m_k_norm_w, m_w_o_attn, m_conv_w, m_conv_b, m_conv_ln_w, m_conv_ln_b, m_w_pw_conv, m_w_out, m_norm2_w, m_w_ffn_in, m_w_ffn_out, v_norm1_w, v_w_in, v_b_gate, v_q_norm_w, v_k_norm_w, v_w_o_attn, v_conv_w, v_conv_b, v_conv_ln_w, v_conv_ln_b, v_w_pw_conv, v_w_out, v_norm2_w, v_w_ffn_in, v_w_ffn_out):
    w = dict(norm1_w=norm1_w, w_in=w_in, b_gate=b_gate, q_norm_w=q_norm_w, k_norm_w=k_norm_w, w_o_attn=w_o_attn,
             conv_w=conv_w, conv_b=conv_b, conv_ln_w=conv_ln_w, conv_ln_b=conv_ln_b, w_pw_conv=w_pw_conv,
             w_out=w_out, norm2_w=norm2_w, w_ffn_in=w_ffn_in, w_ffn_out=w_ffn_out)
    m = dict(norm1_w=m_norm1_w, w_in=m_w_in, b_gate=m_b_gate, q_norm_w=m_q_norm_w, k_norm_w=m_k_norm_w,
             w_o_attn=m_w_o_attn, conv_w=m_conv_w, conv_b=m_conv_b, conv_ln_w=m_conv_ln_w,
             conv_ln_b=m_conv_ln_b, w_pw_conv=m_w_pw_conv, w_out=m_w_out, norm2_w=m_norm2_w,
             w_ffn_in=m_w_ffn_in, w_ffn_out=m_w_ffn_out)
    v = dict(norm1_w=v_norm1_w, w_in=v_w_in, b_gate=v_b_gate, q_norm_w=v_q_norm_w, k_norm_w=v_k_norm_w,
             w_o_attn=v_w_o_attn, conv_w=v_conv_w, conv_b=v_conv_b, conv_ln_w=v_conv_ln_w,
             conv_ln_b=v_conv_ln_b, w_pw_conv=v_w_pw_conv, w_out=v_w_out, norm2_w=v_norm2_w,
             w_ffn_in=v_w_ffn_in, w_ffn_out=v_w_ffn_out)
    def two_d(t):
        t = {k: (a[0] if a.ndim == 3 else a) for k, a in t.items()}
        return {k: (a.T if k in TRANSPOSED else a) for k, a in t.items()}

    w, m, v = two_d(w), two_d(m), two_d(v)

    x2, target = x[0], loss_target[0]
    c_idx = lax.axis_index("c").astype(jnp.int32)
    chip_idx = (2 * lax.axis_index("x") + lax.axis_index("y")).astype(jnp.int32)
    qw2 = jnp.tile(w["q_norm_w"], (1, 2))
    kw2 = jnp.tile(w["k_norm_w"], (1, 2))

    ax, ay = lax.axis_index("x"), lax.axis_index("y")
    chip_order = jnp.stack([2 * ax + ay, 2 * (1 - ax) + ay, 2 * ax + 1 - ay, 2 * (1 - ax) + 1 - ay]).astype(jnp.int32)
    h_t, proj, w_in_blocks, tabs = in_proj_gather(x2, w["norm1_w"], w["w_in"], chip_order, positions.reshape(S, 1))
    w_in_t = w_in_blocks.reshape(INW, D)
    (attn, lse), ((w_ffn_in_blocks,), (w_out_blocks,), (w_o_blocks,), (w_pw_blocks,), (bg_blocks,), (cw_blocks,)) = attn_fwd(
        proj, tabs, qw2, kw2, sides=(ag_blocks_relay(w["w_ffn_in"], BF16), ag_blocks_relay(w["w_out"], BF16),
                                     ag_blocks_relay(w["w_o_attn"], BF16, transpose=True),
                                     ag_blocks_relay(w["w_pw_conv"], BF16, transpose=True),
                                     ag_blocks(w["b_gate"], F32), ag_blocks(w["conv_w"], F32)))
    w_ffn_in_t = w_ffn_in_blocks.reshape(2 * FF, D)
    w_out_f = w_out_blocks.reshape(D, D)
    w_o_t, w_pw_t = w_o_blocks.reshape(D, CC), w_pw_blocks.reshape(D, CC)
    b_gate_f, conv_w_f = _blocks_to_cols(bg_blocks), _blocks_to_cols(cw_blocks)
    cpre, u3 = conv_fwd(proj, conv_w_f, w["conv_b"], w["conv_ln_w"], w["conv_ln_b"])
    x1, z, ya, yb = mix_out(x2, proj, b_gate_f, attn, u3, w_o_t, w_pw_t, w_out_f)
    (h2, gu, f), ((w_ffn_out_blocks,),) = ffn_in(x1, w["norm2_w"], w_ffn_in_t, sides=(ag_blocks_relay(w["w_ffn_out"], BF16),))
    w_ffn_out_f = w_ffn_out_blocks.reshape(FF, D)
    dy, dyb, sq = ffn_out_loss(x1, f, w_ffn_out_f, target)

    g = {}
    def blocks(name, a, b, tm):
        return [t.reshape(NDEV, a.shape[1] // NDEV, b.shape[1]) for t in mm_tn(name, a, b, tm, b.shape[1])]

    g_ffn_out, gb_ffn_out = blocks("gw_ffn_out", f, dyb, FF // 2)
    (d_gu, d_x1, d_x1b, g["norm2_w"]), ((ra_ffn_out,),) = ffn_bwd(
        dy, dyb, gu, x1, w["norm2_w"], w_ffn_in_t, w_ffn_out_f, sides=(rs_to_sibling([gb_ffn_out]),))
    pb_ffn_out, own_ffn_out = chip_sum("chip_sum_w_ffn_out", g_ffn_out, ra_ffn_out, c_idx, chip_idx)
    g_ffn_in, gb_ffn_in = blocks("gw_ffn_in", d_gu, h2, FF // 2)
    g_out, gb_out = blocks("gw_out", z, d_x1b, D // 2)
    (d_ya, d_yb, d_gl, d_attn, d_u3, g["b_gate"]), ((ra_ffn_in,),) = out_bwd(
        d_x1b, proj, b_gate_f, ya, yb, w_o_t, w_pw_t, w_out_f, sides=(rs_to_sibling([gb_ffn_in]),))
    pb_ffn_in, own_ffn_in = chip_sum("chip_sum_w_ffn_in", g_ffn_in, ra_ffn_in, c_idx, chip_idx)
    g_w_o, gb_w_o = blocks("gw_o_attn", d_ya, attn, D // 2)
    g_w_pw, gb_w_pw = blocks("gw_pw_conv", d_yb, u3, D // 2)
    (d_conv, g["conv_w"], g["conv_b"], g["conv_ln_w"], g["conv_ln_b"]), ((ra_out, ra_w_o, ra_w_pw),) = conv_bwd(
        proj, cpre, d_u3, conv_w_f, conv_w_f[::-1], w["conv_ln_w"], w["conv_ln_b"],
        sides=(rs_to_sibling([gb_out, gb_w_o, gb_w_pw]),))
    pb_out, own_out = chip_sum("chip_sum_w_out", g_out, ra_out, c_idx, chip_idx)
    pb_w_o, own_w_o = chip_sum("chip_sum_w_o_attn", g_w_o, ra_w_o, c_idx, chip_idx)
    pb_w_pw, own_w_pw = chip_sum("chip_sum_w_pw_conv", g_w_pw, ra_w_pw, c_idx, chip_idx)
    (d_q, d_k, d_v, gqw, gkw), ((rb_ffn_out, rb_ffn_in, rb_out, rb_w_o, rb_w_pw),) = attn_bwd(
        proj, tabs, qw2, kw2, d_attn, attn, lse,
        sides=(rs_to_chips([pb_ffn_out, pb_ffn_in, pb_out, pb_w_o, pb_w_pw]),))
    g["q_norm_w"] = gqw[0:1, 0:HD] + gqw[0:1, HD:LANES]
    g["k_norm_w"] = gkw[0:1, 0:HD] + gkw[0:1, HD:LANES]
    d_segs = (d_q, d_k, d_v, d_conv, d_gl)
    parts, to_sibling, to_chips, owns, from_chips = [], None, None, [], []
    for k, hw in enumerate(GW_IN_SPLIT):
        sides = tuple(s for s in (to_chips, to_sibling) if s is not None)
        (part, part_b), outs = gw_in_t("gw_in_%d" % k, h_t, d_segs, sum(GW_IN_SPLIT[:k]), hw, sides=sides)
        outs = list(outs)
        if to_chips is not None:
            from_chips.append(outs.pop(0)[0])
        if to_sibling is not None:
            pb, own = chip_sum("chip_sum_w_in_%d" % (k - 1), parts[-1], outs.pop(0)[0], c_idx, chip_idx)
            owns.append(own)
            to_chips = rs_to_chips_combined(pb)
        else:
            to_chips = None
        parts.append(part.reshape(NDEV, INW // NDEV, hw))
        to_sibling = rs_to_sibling([part_b.reshape(NDEV, INW // NDEV, hw)])
    (grad_x, g["norm1_w"]), ((rb_prev,), (ra_last,)) = in_bwd(
        d_q, d_k, d_v, d_conv, d_gl, w_in_t, x2, d_x1, w["norm1_w"], sides=(to_chips, to_sibling))
    from_chips.append(rb_prev)
    pb, own = chip_sum("chip_sum_w_in_%d" % (len(GW_IN_SPLIT) - 1), parts[-1], ra_last, c_idx, chip_idx)
    owns.append(own)
    small_sums, ((rb_last,),) = small_sync(g, sq, sides=(rs_to_chips_combined(pb),))
    small, loss = small_adam(small_sums, w, m, v, (4 * ax + 2 * ay + c_idx).astype(jnp.int32).reshape(1))
    from_chips.append(rb_last)

    res = {
        "w_in": shard_adam("adam_w_in", owns, from_chips, w["w_in"], m["w_in"], v["w_in"]),
        "w_ffn_in": shard_adam("adam_w_ffn_in", [own_ffn_in], [rb_ffn_in], w["w_ffn_in"], m["w_ffn_in"], v["w_ffn_in"]),
        "w_o_attn": shard_adam("adam_w_o_attn", [own_w_o], [rb_w_o], w["w_o_attn"], m["w_o_attn"], v["w_o_attn"], io_t=True),
        "w_pw_conv": shard_adam("adam_w_pw_conv", [own_w_pw], [rb_w_pw],
                                w["w_pw_conv"], m["w_pw_conv"], v["w_pw_conv"], io_t=True),
        "w_out": shard_adam("adam_w_out", [own_out], [rb_out], w["w_out"], m["w_out"], v["w_out"]),
        "w_ffn_out": shard_adam("adam_w_ffn_out", [own_ffn_out], [rb_ffn_out],
                                w["w_ffn_out"], m["w_ffn_out"], v["w_ffn_out"]),
    }
    res = {k: tuple(a.T if k in TRANSPOSED else a for a in r) for k, r in res.items()}
    res.update(small)

    def shaped(name, a):
        return a.reshape((1,) + a.shape) if name in MATS or name in ("b_gate", "conv_w") else a

    outs = [loss, grad_x.reshape(1, S, D)]
    for i in range(4):
        outs += [shaped(k, res[k][i]) for k in WEIGHTS]
    return tuple(outs)
```

```python
import functools
from typing import Callable, NamedTuple, Optional

import numpy as np
import jax
import jax.numpy as jnp
from jax import lax
from jax.experimental import pallas as pl
from jax.experimental.pallas import tpu as pltpu

F32 = jnp.float32
BF16 = jnp.bfloat16

S = 2048
D = 1024
HD = 64
QKV = 1536
CC = 512
KW = 31
FF = 2816
INW = 7680
OFF_Q, OFF_K, OFF_V, OFF_CA, OFF_CB, OFF_GA, OFF_GB = 0, 1536, 3072, 4608, 5120, 5632, 6656
DILATIONS = (1, 4, 16)
HALF_SPAN = 64
EPS = 1e-6
NEG_INF = -1e30
ROPE_THETA = 500000.0
ROT_DIM = 16

ADAM_LR = 0.001
ADAM_B1 = 0.9
ADAM_B2 = 0.999
ADAM_EPS = 1e-08
ADAM_WD = 0.01
ADAM_STEP = 10

NDEV = 8
LANES = 128
TM = 256
IN_PROJ_TM = 512
TQ = 128
VMEM_LIMIT = 56 * 1024 * 1024
MESH = pl.DeviceIdType.MESH


def _cp(**kw):
    return pltpu.CompilerParams(vmem_limit_bytes=VMEM_LIMIT, **kw)


def _row(width, col=0, tm=TM):
    return pl.BlockSpec((tm, width), lambda i: (i, col))


PLANE = 512


def _planes(width, tm=TM):
    return pl.BlockSpec((width // PLANE, tm, PLANE), lambda i: (0, i, 0))


def _res(shape):
    nd = len(shape)
    return pl.BlockSpec(shape, lambda *_: (0,) * nd, pipeline_mode=pl.Buffered(1))


def _dot(a, b):
    return jnp.dot(a, b, preferred_element_type=F32)


def _dot_nt(a, b):
    return lax.dot_general(a, b, (((1,), (1,)), ((), ())), preferred_element_type=F32)


def _dot_tn(a, b):
    return lax.dot_general(a, b, (((0,), (0,)), ((), ())), preferred_element_type=F32)


def _sigmoid(x):
    return jax.nn.sigmoid(x)


def _dsilu(x, sg):
    return sg * (1.0 + x * (1.0 - sg))


ANY = pl.BlockSpec(memory_space=pl.ANY)
VMEM = pl.BlockSpec(memory_space=pltpu.VMEM)


class Side(NamedTuple):
    args: tuple
    in_specs: tuple
    out_shape: tuple
    scratch: tuple
    start: Callable
    finish: Callable
    mid: Optional[Callable] = None
    peers: str = ""


BARRIER_IDS = {"s": 0, "dxy": 1, "dsxy": 2, "sxy": 3, "xy": 4}


def _peer_barrier(peers):
    x, y, c = lax.axis_index("x"), lax.axis_index("y"), lax.axis_index("c")
    where = {"s": (x, y, 1 - c), "x": (1 - x, y, c), "y": (x, 1 - y, c), "d": (1 - x, 1 - y, c)}
    barrier = pltpu.get_barrier_semaphore()
    for p in peers:
        pl.semaphore_signal(barrier, inc=1, device_id=where[p], device_id_type=MESH)
    pl.semaphore_wait(barrier, len(peers))


def _call(body, sides=(), *, name, grid, in_specs, out_specs, out_shape, scratch_shapes=(), args, own_comm=False,
          tail=None):
    assert tail is None or int(np.prod(grid)) == 1
    ni, no, ns = len(in_specs), len(out_specs), len(scratch_shapes)
    cnt = [(len(s.args), len(s.out_shape), len(s.scratch)) for s in sides]
    peers = "".join(sorted(set("".join(s.peers for s in sides))))
    if own_comm or not sides or any(not s.peers for s in sides):
        peers = ""

    def take(refs, pos, n):
        return refs[pos:pos + n], pos + n

    def full(*refs):
        m_in, pos = take(refs, 0, ni)
        s_in = []
        for a, _, _ in cnt:
            r, pos = take(refs, pos, a)
            s_in.append(r)
        m_out, pos = take(refs, pos, no)
        s_out = []
        for _, o, _ in cnt:
            r, pos = take(refs, pos, o)
            s_out.append(r)
        m_scr, pos = take(refs, pos, ns)
        s_scr = []
        for _, _, c in cnt:
            r, pos = take(refs, pos, c)
            s_scr.append(r)
        if sides:
            first = functools.reduce(jnp.logical_and, [pl.program_id(d) == 0 for d in range(len(grid))])
            last = functools.reduce(jnp.logical_and, [pl.program_id(d) == g - 1 for d, g in enumerate(grid)])

            @pl.when(first)
            def _():
                if peers:
                    _peer_barrier(peers)
                for s, a, o, c in zip(sides, s_in, s_out, s_scr):
                    s.start(a, o, c)

            steps = int(np.prod(grid))
            mid_step = (2 * steps) // 3
            if steps > 1 and any(s.mid is not None for s in sides):
                step = functools.reduce(lambda acc, d: acc * grid[d] + pl.program_id(d), range(len(grid)), 0)

                @pl.when(step == mid_step)
                def _():
                    for s, a, o, c in zip(sides, s_in, s_out, s_scr):
                        if s.mid is not None:
                            s.mid(a, o, c)

        body(*m_in, *m_out, *m_scr)
        if sides:
            @pl.when(last)
            def _():
                for s, a, o, c in zip(sides, s_in, s_out, s_scr):
                    if s.mid is not None and steps == 1:
                        s.mid(a, o, c)
                if tail is not None:
                    tail(*m_in, *m_out, *m_scr)
                for s, a, o, c in zip(sides, s_in, s_out, s_scr):
                    s.finish(a, o, c)
        elif tail is not None:
            tail(*m_in, *m_out, *m_scr)

    res = pl.pallas_call(
        full, name=name, grid=grid,
        in_specs=list(in_specs) + [sp for s in sides for sp in s.in_specs],
        out_specs=list(out_specs) + [ANY for s in sides for _ in s.out_shape],
        out_shape=list(out_shape) + [o for s in sides for o in s.out_shape],
        scratch_shapes=list(scratch_shapes) + [c for s in sides for c in s.scratch],
        compiler_params=_cp(dimension_semantics=("arbitrary",) * len(grid),
                            **({"collective_id": BARRIER_IDS[peers]} if peers else {})),
    )(*args, *[a for s in sides for a in s.args])
    res = list(res)
    if not sides:
        return res
    outs, pos = take(res, 0, no)
    side_outs = []
    for _, o, _ in cnt:
        r, pos = take(res, pos, o)
        side_outs.append(r)
    return outs, side_outs


def _inv_freq_lanes():
    inv = np.float32(ROPE_THETA) ** (-np.arange(0, ROT_DIM, 2, dtype=np.float32) / np.float32(ROT_DIM))
    lane = np.arange(LANES) % HD
    out = np.where(lane < ROT_DIM, inv[lane % (ROT_DIM // 2)], 0.0).astype(np.float32)
    return jnp.asarray(out.reshape(1, LANES))


def _rope_tables(pos, inv_freq):
    ang = pos.astype(F32) * inv_freq
    lane = lax.broadcasted_iota(jnp.int32, ang.shape, 1) % HD
    cs = jnp.cos(ang)
    sn = jnp.sin(ang)
    return (jnp.where(lane < ROT_DIM, cs, 1.0), jnp.where(lane < ROT_DIM // 2, -sn, 0.0),
            jnp.where(lane < ROT_DIM // 2, 0.0, jnp.where(lane < ROT_DIM, sn, 0.0)))


def _rope(v, c, s1, s2):
    return v * c + pltpu.roll(v, LANES - 8, axis=1) * s1 + pltpu.roll(v, 8, axis=1) * s2


def _rope_t(d, c, s1, s2):
    return d * c - pltpu.roll(d, LANES - 8, axis=1) * s1 - pltpu.roll(d, 8, axis=1) * s2


def _head_mat():
    r = lax.broadcasted_iota(jnp.int32, (LANES, LANES), 0) // HD
    c = lax.broadcasted_iota(jnp.int32, (LANES, LANES), 1) // HD
    return jnp.where(r == c, 1.0 / HD, 0.0).astype(BF16)


def _head_mean(t, e):
    hi = t.astype(BF16)
    rest = (t - hi.astype(F32)).astype(BF16)
    return _dot(hi, e) + _dot(rest, e)


def in_proj_gather(x, norm_w, shard_t, chip_order, pos_col):
    R = INW // NDEV
    tm = IN_PROJ_TM
    half, nt = R // 2, S // tm

    def body(ord_ref, x_ref, nw_ref, sh_ref, pos_ref, f_ref, ht_ref, p_ref, wfull_ref, c_ref, s1_ref, s2_ref,
             wt, hs, send, recv, loc):
        kk, i = pl.program_id(0), pl.program_id(1)
        x, y, c, _ = _place()
        me, flip = 4 * x + 2 * y + c, 1 - 2 * c
        here, sib, xn, yn = (x, y, c), (x, y, 1 - c), (1 - x, y, c), (x, 1 - y, c)
        b_xn, b_yn, b_dg = 4 * (1 - x) + 2 * y + c, 4 * x + 2 * (1 - y) + c, 4 * (1 - x) + 2 * (1 - y) + c

        def cp(k, block, to, rows=None):
            dst = wt.at[block] if rows is None else wt.at[block, pl.ds(rows * half, half), :]
            return _remote(dst, dst, send, recv, k, to)

        def sends():
            return [cp(0, me, sib), cp(1, me, xn), cp(2, me, yn), cp(3, b_xn, sib), cp(4, b_yn, sib),
                    cp(5, b_xn, yn, rows=0), cp(6, b_yn, xn, rows=1), cp(7, b_dg, sib, rows=0), cp(8, b_dg, sib, rows=1)]

        def keep(j, blk0):
            pair = pl.ds(pl.multiple_of(blk0, 2), 2)
            return pltpu.make_async_copy(wt.at[pair], wfull_ref.at[pair], loc.at[j])

        @pl.when((kk == 0) & (i == 0))
        def _():
            _peer_barrier("sxy")
            _cast_rows(wt.at[me], sh_ref)
            for s_ in sends()[0:3]:
                s_.start()

            def tables(j, _):
                chunk = pl.ds(pl.multiple_of(j * TM, TM), TM)
                c_ref[chunk, :], s1_ref[chunk, :], s2_ref[chunk, :] = _rope_tables(pos_ref[chunk, :], f_ref[...])
                return 0

            lax.fori_loop(0, S // TM, tables, 0)
            cp(0, me + flip, here).wait_recv()
            keep(0, me - c).start()

        @pl.when((kk == 1) & (i == 0))
        def _():
            cp(1, b_xn, here).wait_recv()
            sends()[5].start()
            sends()[3].start()
            cp(2, b_yn, here).wait_recv()
            sends()[6].start()
            sends()[4].start()
            cp(3, b_xn + flip, here).wait_recv()
            keep(1, b_xn - c).start()

        @pl.when((kk == 2) & (i == 0))
        def _():
            cp(4, b_yn + flip, here).wait_recv()
            keep(2, b_yn - c).start()

        @pl.when((kk == 3) & (i == 0))
        def _():
            cp(5, b_dg, here, rows=0).wait_recv()
            sends()[7].start()
            cp(6, b_dg, here, rows=1).wait_recv()
            sends()[8].start()
            cp(7, b_dg + flip, here, rows=0).wait_recv()
            cp(8, b_dg + flip, here, rows=1).wait_recv()
            keep(3, b_dg - c).start()

        rows = pl.ds(pl.multiple_of(i * tm, tm), tm)

        @pl.when(kk == 0)
        def _():
            xv = x_ref[...]
            r = lax.rsqrt(jnp.mean(xv * xv, axis=-1, keepdims=True) + EPS)
            hf = xv * r * nw_ref[...]
            ht_ref[...] = hf.T.astype(BF16)
            hs[rows, :] = hf.astype(BF16)

        h = hs[rows, :]
        chip = ord_ref[kk]
        for cc in range(2):
            p_ref[:, cc * R:(cc + 1) * R] = _dot_nt(h, wt[2 * chip + cc])

        @pl.when((kk == 3) & (i == nt - 1))
        def _():
            for s_ in sends():
                s_.wait_send()
            for j, blk in enumerate((me, b_xn, b_yn, b_dg)):
                keep(j, blk - c).wait()

    def first_pass(kk, i):
        return jnp.where(kk == 0, i, nt - 1)

    grid_spec = pltpu.PrefetchScalarGridSpec(
        num_scalar_prefetch=1, grid=(4, nt),
        in_specs=[pl.BlockSpec((tm, D), lambda kk, i, o: (first_pass(kk, i), 0)),
                  pl.BlockSpec((1, D), lambda kk, i, o: (0, 0)), VMEM, VMEM,
                  pl.BlockSpec((1, LANES), lambda kk, i, o: (0, 0))],
        out_specs=[pl.BlockSpec((D, tm), lambda kk, i, o: (0, first_pass(kk, i))),
                   pl.BlockSpec((tm, 2 * R), lambda kk, i, o: (i, o[kk])), ANY]
        + [pl.BlockSpec((S, LANES), lambda kk, i, o: (0, 0))] * 3,
        scratch_shapes=[pltpu.VMEM((NDEV, R, D), BF16), pltpu.VMEM((S, D), BF16), _sems(9), _sems(9), _sems(4)])
    res = pl.pallas_call(
        body, name="in_proj_gather", grid_spec=grid_spec,
        out_shape=[jax.ShapeDtypeStruct((D, S), BF16), jax.ShapeDtypeStruct((S, INW), F32),
                   jax.ShapeDtypeStruct((NDEV, R, D), BF16)] + [jax.ShapeDtypeStruct((S, LANES), F32)] * 3,
        compiler_params=_cp(dimension_semantics=("arbitrary", "arbitrary"), collective_id=BARRIER_IDS["sxy"]),
    )(chip_order, x, norm_w, shard_t, pos_col, _inv_freq_lanes())
    return res[0], res[1], res[2], tuple(res[3:])


def _qk_specs():
    nb = QKV // LANES
    return [pl.BlockSpec((S, LANES), functools.partial(lambda hp, g, o: (0, o + g * 4 + hp), o=o))
            for o in (OFF_Q // LANES, OFF_K // LANES, OFF_V // LANES)]


def _tab_specs():
    return [pl.BlockSpec((S, LANES), lambda hp, g: (0, 0), pipeline_mode=pl.Buffered(1))] * 3


def _vec_spec():
    return pl.BlockSpec((1, LANES), lambda hp, g: (0, 0))


def _sub_rows(r, d, start, n):
    if d == 1:
        return pl.ds(start, n)
    return pl.ds(r + d * start, n, stride=d)


def _band_window(i, L):
    W = min(TQ + 2 * HALF_SPAN, L)
    q0 = pl.multiple_of(i * TQ, TQ)
    k0 = pl.multiple_of(jnp.clip(q0 - HALF_SPAN, 0, L - W), HALF_SPAN)
    qpos = q0 + (lax.broadcasted_iota(jnp.int32, (2 * TQ, W), 0) & (TQ - 1))
    kpos = k0 + lax.broadcasted_iota(jnp.int32, (2 * TQ, W), 1)
    valid = jnp.abs(qpos - kpos) <= HALF_SPAN
    return W, q0, k0, valid


def _stack_heads(t, lo):
    z = jnp.zeros_like(t)
    return jnp.concatenate([jnp.where(lo, t, z), jnp.where(lo, z, t)], axis=0)


def _unstack_heads(t2, lo):
    return jnp.where(lo, t2[0:TQ], t2[TQ:2 * TQ])


CHAINS = 8


def _interleave(d):
    ru = min(d, CHAINS)
    return ru, min(CHAINS // ru, S // d // TQ)


def _for_blocks(n, fn):
    if n == 1:
        fn(0)
    else:
        def it(j, _):
            fn(j)
            return 0
        lax.fori_loop(0, n, it, 0)


def attn_fwd(proj, tabs, qw2, kw2, sides=()):
    CH = 256

    def body(q_ref, k_ref, v_ref, c_ref, s1_ref, s2_ref, qw_ref, kw_ref, at_ref, ls_ref,
             qs, ks, vs, osub, lsub, onat, lnat, qn, kn):
        g = pl.program_id(1)
        lo = lax.broadcasted_iota(jnp.int32, (1, LANES), 1) < HD
        e = _head_mat()

        def prep(i, _):
            rows = pl.ds(pl.multiple_of(i * CH, CH), CH)
            c, s1, s2 = c_ref[rows, :], s1_ref[rows, :], s2_ref[rows, :]
            for t_ref, w_ref, out, scale in ((q_ref, qw_ref, qn, HD ** -0.5), (k_ref, kw_ref, kn, 1.0)):
                t = t_ref[rows, :]
                r = lax.rsqrt(_head_mean(t * t, e) + EPS)
                out[rows, :] = _rope(t * r * w_ref[...], c, s1, s2) * scale
            return 0

        lax.fori_loop(0, S // CH, prep, 0, unroll=4)

        def group(gi, d):
            L = S // d

            ru, nb = _interleave(d)

            def stage(r, off):
                for c0 in range(0, L, CH):
                    n = min(CH, L)
                    rows = _sub_rows(r, d, c0, n)
                    dst = pl.ds(off + c0, n)
                    qs[dst, :] = qn[rows, :].astype(BF16)
                    ks[dst, :] = kn[rows, :].astype(BF16)
                    vs[dst, :] = v_ref[rows, :].astype(BF16)

            def one(off, i):
                W, q0, k0, valid = _band_window(i, L)
                q2 = _stack_heads(qs[pl.ds(off + q0, TQ), :], lo)
                sc = jnp.where(valid, _dot_nt(q2, ks[pl.ds(off + k0, W), :]), NEG_INF)
                m = jnp.max(sc, axis=-1, keepdims=True)
                p = jnp.exp(sc - m)
                den = jnp.sum(p, axis=-1, keepdims=True)
                o2 = _dot(p.astype(BF16), vs[pl.ds(off + k0, W), :]) / den
                l2 = jnp.broadcast_to(m + jnp.log(den), (2 * TQ, LANES))
                osub[pl.ds(off + q0, TQ), :] = _unstack_heads(o2, lo)
                lsub[pl.ds(off + q0, TQ), :] = _unstack_heads(l2, lo)

            def unstage(r, off):
                for c0 in range(0, L, CH):
                    n = min(CH, L)
                    rows = _sub_rows(r, d, c0, n)
                    onat[gi, rows, :] = osub[pl.ds(off + c0, n), :]
                    lnat[gi, rows, :] = lsub[pl.ds(off + c0, n), :]

            def step(t, _):
                for u in range(ru):
                    stage(t * ru + u, u * L)
                _for_blocks(L // TQ // nb, lambda j: [one(u * L, j * nb + b) for u in range(ru) for b in range(nb)])
                for u in range(ru):
                    unstage(t * ru + u, u * L)
                return 0

            lax.fori_loop(0, d // ru, step, 0)

        for gi, d in enumerate(DILATIONS):
            pl.when(g == gi)(functools.partial(group, gi, d))

        @pl.when(g == len(DILATIONS) - 1)
        def _():
            def mix(i, _):
                rows = pl.ds(pl.multiple_of(i * CH, CH), CH)
                l0, l1, l2 = lnat[0, rows, :], lnat[1, rows, :], lnat[2, rows, :]
                m = jnp.maximum(jnp.maximum(l0, l1), l2)
                e0, e1, e2 = jnp.exp(l0 - m), jnp.exp(l1 - m), jnp.exp(l2 - m)
                den = e0 + e1 + e2
                a = (e0 * onat[0, rows, :] + e1 * onat[1, rows, :] + e2 * onat[2, rows, :]) / den
                at_ref[rows, :] = a.astype(BF16)
                ls_ref[rows, :] = m + jnp.log(den)
                return 0

            lax.fori_loop(0, S // CH, mix, 0)

    out_spec = pl.BlockSpec((S, LANES), lambda hp, g: (0, hp))
    return _call(
        body, sides, name="attn_fwd", grid=(4, 3),
        in_specs=_qk_specs() + _tab_specs() + [_vec_spec(), _vec_spec()],
        out_specs=[out_spec, out_spec],
        out_shape=[jax.ShapeDtypeStruct((S, CC), BF16), jax.ShapeDtypeStruct((S, CC), F32)],
        scratch_shapes=[pltpu.VMEM((S, LANES), BF16)] * 3 + [pltpu.VMEM((S, LANES), F32)] * 2
        + [pltpu.VMEM((3, S, LANES), F32)] * 2 + [pltpu.VMEM((S, LANES), F32)] * 2,
        args=(proj, proj, proj, *tabs, qw2, kw2))


def attn_bwd(proj, tabs, qw2, kw2, d_attn, attn, lse, sides=()):
    CH = 256

    def body(q_ref, k_ref, v_ref, c_ref, s1_ref, s2_ref, qw_ref, kw_ref, do_ref, at_ref, ls_ref,
             dq_ref, dk_ref, dv_ref, gqw_ref, gkw_ref,
             qs, ks, vs, dos, dsub, lsub, dqs, dks, dvs, dnat, qx, kx, dvn, tnq, tnk, rrq, rrk):
        hp, g = pl.program_id(0), pl.program_id(1)
        lo = lax.broadcasted_iota(jnp.int32, (1, LANES), 1) < HD
        e = _head_mat()
        both = ((q_ref, qw_ref, qx, tnq, rrq, HD ** -0.5), (k_ref, kw_ref, kx, tnk, rrk, 1.0))

        @pl.when((hp == 0) & (g == 0))
        def _():
            gqw_ref[...] = jnp.zeros_like(gqw_ref)
            gkw_ref[...] = jnp.zeros_like(gkw_ref)

        def prep(i, _):
            rows = pl.ds(pl.multiple_of(i * CH, CH), CH)
            dnat[rows, :] = _head_mean(do_ref[rows, :] * at_ref[rows, :].astype(F32), e) * float(HD)
            c, s1, s2 = c_ref[rows, :], s1_ref[rows, :], s2_ref[rows, :]
            for t_ref, w_ref, x, tn_s, rr_s, scale in both:
                t = t_ref[rows, :]
                rr = lax.rsqrt(_head_mean(t * t, e) + EPS)
                tn = t * rr
                rr_s[rows, :] = rr
                tn_s[rows, :] = tn
                x[rows, :] = _rope(tn * w_ref[...], c, s1, s2) * scale
            return 0

        lax.fori_loop(0, S // CH, prep, 0, unroll=4)

        def group(d):
            L = S // d

            ru, nb = _interleave(d)

            def stage(r, off):
                for c0 in range(0, L, CH):
                    n = min(CH, L)
                    rows = _sub_rows(r, d, c0, n)
                    dst = pl.ds(off + c0, n)
                    qs[dst, :] = qx[rows, :].astype(BF16)
                    ks[dst, :] = kx[rows, :].astype(BF16)
                    vs[dst, :] = v_ref[rows, :].astype(BF16)
                    dos[dst, :] = do_ref[rows, :].astype(BF16)
                    dsub[dst, :] = dnat[rows, :]
                    lsub[dst, :] = ls_ref[rows, :]
                    dks[dst, :] = jnp.zeros((n, LANES), F32)
                    dvs[dst, :] = jnp.zeros((n, LANES), F32)

            def one(off, i):
                W, q0, k0, valid = _band_window(i, L)
                qrows, krows = pl.ds(off + q0, TQ), pl.ds(off + k0, W)
                q2 = _stack_heads(qs[qrows, :], lo)
                do2 = _stack_heads(dos[qrows, :], lo)
                kk, vv = ks[krows, :], vs[krows, :]
                lse_b, dd_b = lsub[qrows, :], dsub[qrows, :]
                lse2 = jnp.concatenate([lse_b[:, 0:1], lse_b[:, HD:HD + 1]], axis=0)
                dd2 = jnp.concatenate([dd_b[:, 0:1], dd_b[:, HD:HD + 1]], axis=0)
                sc = jnp.where(valid, _dot_nt(q2, kk), NEG_INF)
                p = jnp.exp(sc - lse2)
                ds = (p * (_dot_nt(do2, vv) - dd2)).astype(BF16)
                dqs[qrows, :] = _unstack_heads(_dot(ds, kk), lo)
                dks[krows, :] = dks[krows, :] + _dot_tn(ds, q2)
                dvs[krows, :] = dvs[krows, :] + _dot_tn(p.astype(BF16), do2)

            def unstage(r, off):
                for c0 in range(0, L, CH):
                    n = min(CH, L)
                    rows = _sub_rows(r, d, c0, n)
                    src = pl.ds(off + c0, n)
                    qx[rows, :] = dqs[src, :]
                    kx[rows, :] = dks[src, :]
                    dvn[rows, :] = dvs[src, :]

            def step(t, _):
                for u in range(ru):
                    stage(t * ru + u, u * L)
                _for_blocks(L // TQ // nb, lambda j: [one(u * L, j * nb + b) for u in range(ru) for b in range(nb)])
                for u in range(ru):
                    unstage(t * ru + u, u * L)
                return 0

            lax.fori_loop(0, d // ru, step, 0)

        for gi, d in enumerate(DILATIONS):
            pl.when(g == gi)(functools.partial(group, d))

        def emit(i, _):
            rows = pl.ds(pl.multiple_of(i * CH, CH), CH)
            c, s1, s2 = c_ref[rows, :], s1_ref[rows, :], s2_ref[rows, :]
            for (_, w_ref, x, tn_s, rr_s, scale), out, gw_ref in zip(both, (dq_ref, dk_ref), (gqw_ref, gkw_ref)):
                tn = tn_s[rows, :]
                dy = _rope_t(x[rows, :] * scale, c, s1, s2)
                gw_ref[0:1, :] = gw_ref[0:1, :] + jnp.sum(dy * tn, axis=0, keepdims=True)
                dtn = dy * w_ref[...]
                out[rows, :] = (rr_s[rows, :] * (dtn - tn * _head_mean(dtn * tn, e))).astype(BF16)
            dv_ref[rows, :] = dvn[rows, :].astype(BF16)
            return 0

        lax.fori_loop(0, S // CH, emit, 0, unroll=4)

    nat_spec = pl.BlockSpec((S, LANES), lambda hp, g: (0, hp))
    out_spec = pl.BlockSpec((None, S, LANES), lambda hp, g: (g, 0, hp))
    acc_spec = pl.BlockSpec((8, LANES), lambda hp, g: (0, 0))
    return _call(
        body, sides, name="attn_bwd", grid=(4, 3),
        in_specs=_qk_specs() + _tab_specs() + [_vec_spec(), _vec_spec(), nat_spec, nat_spec, nat_spec],
        out_specs=[out_spec] * 3 + [acc_spec] * 2,
        out_shape=[jax.ShapeDtypeStruct((QKV // PLANE, S, PLANE), BF16)] * 3 + [jax.ShapeDtypeStruct((8, LANES), F32)] * 2,
        scratch_shapes=[pltpu.VMEM((S, LANES), BF16)] * 4 + [pltpu.VMEM((S, LANES), F32)] * 13,
        args=(proj, proj, proj, *tabs, qw2, kw2, d_attn, attn, lse))


PADR = 16
CT = 128


def _conv_specs():
    return [pl.BlockSpec((S, CC), lambda i: (0, OFF_CA // CC)), pl.BlockSpec((S, CC), lambda i: (0, OFF_CB // CC))]


NCB = CC // LANES


def _pad_zero(pad):
    for cb in range(NCB):
        pad[cb, 0:PADR, :] = jnp.zeros((PADR, LANES), F32)
        pad[cb, PADR + S:PADR + S + PADR, :] = jnp.zeros((PADR, LANES), F32)


def _pad_store(pad, row0, n, val):
    for cb in range(NCB):
        pad[cb, pl.ds(pl.multiple_of(row0 + PADR, 8), n), :] = val[:, cb * LANES:(cb + 1) * LANES]


def _taps(pad_ref, cb, s0, weights):
    acc = jnp.zeros((CT, LANES), F32)
    for k in range(KW):
        acc = acc + weights[k] * pad_ref[cb, pl.ds(s0 + k + 1, CT), :]
    return acc


def conv_fwd(proj, conv_w, conv_b, ln_w, ln_b, sides=()):
    def body(a_ref, b_ref, w_ref, cb_ref, lw_ref, lb_ref, c_ref, u3_ref, upad):
        _pad_zero(upad)

        def glu(i, _):
            rows = pl.ds(pl.multiple_of(i * TM, TM), TM)
            _pad_store(upad, i * TM, TM, a_ref[rows, :] * _sigmoid(b_ref[rows, :]))
            return 0

        lax.fori_loop(0, S // TM, glu, 0)

        def chunk(i, _):
            s0 = pl.multiple_of(i * CT, CT)
            for cb in range(CC // LANES):
                cols = slice(cb * LANES, (cb + 1) * LANES)
                w = [w_ref[k:k + 1, cols] for k in range(KW)]
                c_ref[pl.ds(s0, CT), cols] = _taps(upad, cb, s0, w) + cb_ref[:, cols]
            cv = c_ref[pl.ds(s0, CT), :]
            mu = jnp.mean(cv, axis=-1, keepdims=True)
            xc = cv - mu
            rstd = lax.rsqrt(jnp.mean(xc * xc, axis=-1, keepdims=True) + EPS)
            yl = xc * rstd * lw_ref[...] + lb_ref[...]
            u3_ref[pl.ds(s0, CT), :] = (yl * _sigmoid(yl)).astype(BF16)
            return 0

        lax.fori_loop(0, S // CT, chunk, 0)

    vec = pl.BlockSpec((1, CC), lambda i: (0, 0))
    full = pl.BlockSpec((S, CC), lambda i: (0, 0))
    return _call(
        body, sides, name="conv_fwd", grid=(1,),
        in_specs=_conv_specs() + [pl.BlockSpec((KW, CC), lambda i: (0, 0)), vec, vec, vec],
        out_specs=[full, full],
        out_shape=[jax.ShapeDtypeStruct((S, CC), F32), jax.ShapeDtypeStruct((S, CC), BF16)],
        scratch_shapes=[pltpu.VMEM((NCB, S + 2 * PADR, LANES), F32)],
        args=(proj, proj, conv_w, conv_b, ln_w, ln_b))


def conv_bwd(proj, cpre, d_u3, conv_w, conv_w_rev, ln_w, ln_b, sides=()):
    def body(a_ref, b_ref, c_ref, du3_ref, w_ref, wr_ref, lw_ref, lb_ref,
             dc_ref, gw_ref, gcb_ref, glw_ref, glb_ref, upad, dpad):
        _pad_zero(upad)
        _pad_zero(dpad)
        gw_ref[...] = jnp.zeros_like(gw_ref)

        def ln_bwd(i, carry):
            gcb, glw, glb = carry
            rows = pl.ds(pl.multiple_of(i * TM, TM), TM)
            _pad_store(upad, i * TM, TM, a_ref[rows, :] * _sigmoid(b_ref[rows, :]))
            cv = c_ref[rows, :]
            mu = jnp.mean(cv, axis=-1, keepdims=True)
            xc = cv - mu
            rstd = lax.rsqrt(jnp.mean(xc * xc, axis=-1, keepdims=True) + EPS)
            xh = xc * rstd
            yl = xh * lw_ref[...] + lb_ref[...]
            dyl = du3_ref[rows, :] * _dsilu(yl, _sigmoid(yl))
            dxh = dyl * lw_ref[...]
            dcv = rstd * (dxh - jnp.mean(dxh, axis=-1, keepdims=True)
                          - xh * jnp.mean(dxh * xh, axis=-1, keepdims=True))
            _pad_store(dpad, i * TM, TM, dcv)
            return (gcb + jnp.sum(dcv, axis=0, keepdims=True),
                    glw + jnp.sum(dyl * xh, axis=0, keepdims=True),
                    glb + jnp.sum(dyl, axis=0, keepdims=True))

        z = jnp.zeros((1, CC), F32)
        gcb, glw, glb = lax.fori_loop(0, S // TM, ln_bwd, (z, z, z))
        gcb_ref[...] = gcb
        glw_ref[...] = glw
        glb_ref[...] = glb

        def chunk(i, _):
            s0 = pl.multiple_of(i * CT, CT)
            for cb in range(CC // LANES):
                cols = slice(cb * LANES, (cb + 1) * LANES)
                wr = [wr_ref[k:k + 1, cols] for k in range(KW)]
                du = _taps(dpad, cb, s0, wr)
                dcv = dpad[cb, pl.ds(s0 + PADR, CT), :]
                for k in range(KW):
                    gw_ref[k:k + 1, cols] = gw_ref[k:k + 1, cols] + jnp.sum(
                        upad[cb, pl.ds(s0 + k + 1, CT), :] * dcv, axis=0, keepdims=True)
                av = a_ref[pl.ds(s0, CT), cols]
                sb = _sigmoid(b_ref[pl.ds(s0, CT), cols])
                dc_ref[0, pl.ds(s0, CT), cols] = (du * sb).astype(BF16)
                dc_ref[1, pl.ds(s0, CT), cols] = (du * av * sb * (1.0 - sb)).astype(BF16)
            return 0

        lax.fori_loop(0, S // CT, chunk, 0)

    vec = pl.BlockSpec((1, CC), lambda i: (0, 0))
    full = pl.BlockSpec((S, CC), lambda i: (0, 0))
    wsp = pl.BlockSpec((KW, CC), lambda i: (0, 0))
    return _call(
        body, sides, name="conv_bwd", grid=(1,),
        in_specs=_conv_specs() + [full, full, wsp, wsp, vec, vec],
        out_specs=[pl.BlockSpec((2, S, CC), lambda i: (0, 0, 0)), wsp, vec, vec, vec],
        out_shape=[jax.ShapeDtypeStruct((2, S, CC), BF16), jax.ShapeDtypeStruct((KW, CC), F32)]
        + [jax.ShapeDtypeStruct((1, CC), F32)] * 3,
        scratch_shapes=[pltpu.VMEM((NCB, S + 2 * PADR, LANES), F32)] * 2,
        args=(proj, proj, cpre, d_u3, conv_w, conv_w_rev, ln_w, ln_b))


def _gate_specs():
    return [_row(CC, col=OFF_GA // CC + j) for j in range(4)]


def _gates(g_refs, bg_ref):
    ga = _sigmoid(jnp.concatenate([g_refs[0][...], g_refs[1][...]], axis=1) + bg_ref[0:1, :])
    gb = _sigmoid(jnp.concatenate([g_refs[2][...], g_refs[3][...]], axis=1) + bg_ref[1:2, :])
    return ga, gb


def mix_out(x, proj, b_gate, attn, u3, w_o, w_pw, w_out):
    def body(x_ref, g0, g1, g2, g3, bg_ref, at_ref, u3_ref, wo_ref, wp_ref, wout_ref,
             x1_ref, z_ref, ya_ref, yb_ref):
        ga, gb = _gates((g0, g1, g2, g3), bg_ref)
        ya = _dot_nt(at_ref[...], wo_ref[...])
        yb = _dot_nt(u3_ref[...], wp_ref[...])
        z = (ga * ya + gb * yb).astype(BF16)
        ya_ref[...] = ya.astype(BF16)
        yb_ref[...] = yb.astype(BF16)
        z_ref[...] = z
        x1_ref[...] = x_ref[...] + _dot(z, wout_ref[...])

    return pl.pallas_call(
        body, name="mix_out", grid=(S // TM,),
        in_specs=[_row(D)] + _gate_specs() + [_res((2, D)), _row(CC), _row(CC),
                                              _res((D, CC)), _res((D, CC)), _res((D, D))],
        out_specs=[_row(D)] * 4,
        out_shape=[jax.ShapeDtypeStruct((S, D), F32)] + [jax.ShapeDtypeStruct((S, D), BF16)] * 3,
        compiler_params=_cp(dimension_semantics=("arbitrary",)),
    )(x, proj, proj, proj, proj, b_gate, attn, u3, w_o, w_pw, w_out)


def out_bwd(d_x1b, proj, b_gate, ya, yb, w_o, w_pw, w_out, sides=()):
    def body(dx_ref, g0, g1, g2, g3, bg_ref, ya_ref, yb_ref, wo_ref, wp_ref, wout_ref,
             dya_ref, dyb_ref, dgl_ref, dat_ref, du3_ref, gbg_ref):
        @pl.when(pl.program_id(0) == 0)
        def _():
            gbg_ref[...] = jnp.zeros_like(gbg_ref)

        ga, gb = _gates((g0, g1, g2, g3), bg_ref)
        dz = _dot_nt(dx_ref[...], wout_ref[...])
        dya = (dz * ga).astype(BF16)
        dyb = (dz * gb).astype(BF16)
        dgla = dz * ya_ref[...].astype(F32) * ga * (1.0 - ga)
        dglb = dz * yb_ref[...].astype(F32) * gb * (1.0 - gb)
        dya_ref[...] = dya
        dyb_ref[...] = dyb
        for j in range(2):
            dgl_ref[j] = dgla[:, j * PLANE:(j + 1) * PLANE].astype(BF16)
            dgl_ref[2 + j] = dglb[:, j * PLANE:(j + 1) * PLANE].astype(BF16)
        gbg_ref[0:1, :] = gbg_ref[0:1, :] + jnp.sum(dgla, axis=0, keepdims=True)
        gbg_ref[1:2, :] = gbg_ref[1:2, :] + jnp.sum(dglb, axis=0, keepdims=True)
        dat_ref[...] = _dot(dya, wo_ref[...])
        du3_ref[...] = _dot(dyb, wp_ref[...])

    return _call(
        body, sides, name="out_bwd", grid=(S // TM,),
        in_specs=[_row(D)] + _gate_specs() + [_res((2, D)), _row(D), _row(D),
                                              _res((D, CC)), _res((D, CC)), _res((D, D))],
        out_specs=[_row(D), _row(D), _planes(2 * D), _row(CC), _row(CC), pl.BlockSpec((2, D), lambda i: (0, 0))],
        out_shape=[jax.ShapeDtypeStruct((S, D), BF16)] * 2 + [jax.ShapeDtypeStruct((2 * D // PLANE, S, PLANE), BF16)]
        + [jax.ShapeDtypeStruct((S, CC), F32)] * 2 + [jax.ShapeDtypeStruct((2, D), F32)],
        args=(d_x1b, proj, proj, proj, proj, b_gate, ya, yb, w_o, w_pw, w_out))


def ffn_in(x1, norm_w, w_ffn_in, sides=()):
    half = FF // 2

    def body(x_ref, nw_ref, w_ref, h_ref, gu_ref, f_ref):
        xv = x_ref[...]
        r = lax.rsqrt(jnp.mean(xv * xv, axis=-1, keepdims=True) + EPS)
        h = (xv * r * nw_ref[...]).astype(BF16)
        h_ref[...] = h
        for j in range(2):
            gt = _dot_nt(h, w_ref[j * half:(j + 1) * half, :])
            up = _dot_nt(h, w_ref[FF + j * half:FF + (j + 1) * half, :])
            gu_ref[:, j * half:(j + 1) * half] = gt.astype(BF16)
            gu_ref[:, FF + j * half:FF + (j + 1) * half] = up.astype(BF16)
            f_ref[:, j * half:(j + 1) * half] = (gt * _sigmoid(gt) * up).astype(BF16)

    return _call(
        body, sides, name="ffn_in", grid=(S // TM,),
        in_specs=[_row(D), _res((1, D)), _res((2 * FF, D))],
        out_specs=[_row(D), _row(2 * FF), _row(FF)],
        out_shape=[jax.ShapeDtypeStruct((S, D), BF16), jax.ShapeDtypeStruct((S, 2 * FF), BF16),
                   jax.ShapeDtypeStruct((S, FF), BF16)],
        args=(x1, norm_w, w_ffn_in))


def ffn_out_loss(x1, f, w_ffn_out, target):
    def body(x_ref, f_ref, w_ref, t_ref, dy_ref, dyb_ref, sq_ref):
        @pl.when(pl.program_id(0) == 0)
        def _():
            sq_ref[...] = jnp.zeros_like(sq_ref)

        diff = x_ref[...] + _dot(f_ref[...], w_ref[...]) - t_ref[...]
        dy = diff * (1.0 / D)
        dy_ref[...] = dy
        dyb_ref[...] = dy.astype(BF16)
        sq_ref[...] = sq_ref[...] + jnp.sum((diff * diff).reshape(TM // 8, 8, D), axis=0)

    return pl.pallas_call(
        body, name="ffn_out_loss", grid=(S // TM,),
        in_specs=[_row(D), _row(FF), _res((FF, D)), _row(D)],
        out_specs=[_row(D), _row(D), pl.BlockSpec((8, D), lambda i: (0, 0))],
        out_shape=[jax.ShapeDtypeStruct((S, D), F32), jax.ShapeDtypeStruct((S, D), BF16),
                   jax.ShapeDtypeStruct((8, D), F32)],
        compiler_params=_cp(dimension_semantics=("arbitrary",)),
    )(x1, f, w_ffn_out, target)


def _rms_bwd(xv, nw, dh):
    r = lax.rsqrt(jnp.mean(xv * xv, axis=-1, keepdims=True) + EPS)
    xn = xv * r
    dxn = dh * nw
    dx = r * (dxn - xn * jnp.mean(dxn * xn, axis=-1, keepdims=True))
    return dx, dh * xn


def ffn_bwd(dy, dyb, gu, x1, norm_w, w_ffn_in, w_ffn_out, sides=()):
    def body(dy_ref, dyb_ref, gu_ref, x_ref, nw_ref, wi_ref, wo_ref, dgu_ref, dx_ref, dxb_ref, gn_ref):
        @pl.when(pl.program_id(0) == 0)
        def _():
            gn_ref[...] = jnp.zeros_like(gn_ref)

        df = _dot_nt(dyb_ref[...], wo_ref[...])
        gt = gu_ref[:, 0:FF].astype(F32)
        up = gu_ref[:, FF:2 * FF].astype(F32)
        sg = _sigmoid(gt)
        dgt = (df * up * _dsilu(gt, sg)).astype(BF16)
        dup = (df * gt * sg).astype(BF16)
        dgu_ref[:, 0:FF] = dgt
        dgu_ref[:, FF:2 * FF] = dup
        dh = _dot(dgt, wi_ref[0:FF, :]) + _dot(dup, wi_ref[FF:2 * FF, :])
        dxn, gw = _rms_bwd(x_ref[...], nw_ref[...], dh)
        dx = dy_ref[...] + dxn
        dx_ref[...] = dx
        dxb_ref[...] = dx.astype(BF16)
        gn_ref[...] = gn_ref[...] + jnp.sum(gw, axis=0, keepdims=True)

    return _call(
        body, sides, name="ffn_bwd", grid=(S // TM,),
        in_specs=[_row(D), _row(D), _row(2 * FF), _row(D), _res((1, D)), _res((2 * FF, D)), _res((FF, D))],
        out_specs=[_row(2 * FF), _row(D), _row(D), pl.BlockSpec((1, D), lambda i: (0, 0))],
        out_shape=[jax.ShapeDtypeStruct((S, 2 * FF), BF16), jax.ShapeDtypeStruct((S, D), F32),
                   jax.ShapeDtypeStruct((S, D), BF16), jax.ShapeDtypeStruct((1, D), F32)],
        args=(dy, dyb, gu, x1, norm_w, w_ffn_in, w_ffn_out))


def in_bwd(d_q, d_k, d_v, d_conv, d_gl, w_in, x, d_x1, norm_w, sides=()):
    segs = ((OFF_Q, QKV), (OFF_K, QKV), (OFF_V, QKV), (OFF_CA, 2 * CC), (OFF_GA, 2 * D))

    def body(dq_ref, dk_ref, dv_ref, dc_ref, dg_ref, w_ref, x_ref, dx1_ref, nw_ref, gx_ref, gn_ref):
        @pl.when(pl.program_id(0) == 0)
        def _():
            gn_ref[...] = jnp.zeros_like(gn_ref)

        dh = jnp.zeros((TM, D), F32)
        for ref, (off, width) in zip((dq_ref, dk_ref, dv_ref, dc_ref, dg_ref), segs):
            for j in range(width // PLANE):
                dh = dh + _dot(ref[j], w_ref[off + j * PLANE:off + (j + 1) * PLANE, :])
        dxn, gw = _rms_bwd(x_ref[...], nw_ref[...], dh)
        gx_ref[...] = dx1_ref[...] + dxn
        gn_ref[...] = gn_ref[...] + jnp.sum(gw, axis=0, keepdims=True)

    return _call(
        body, sides, name="in_bwd", grid=(S // TM,),
        in_specs=[_planes(QKV)] * 3 + [_planes(2 * CC), _planes(2 * D), _res((INW, D)), _row(D), _row(D), _res((1, D))],
        out_specs=[_row(D), pl.BlockSpec((1, D), lambda i: (0, 0))],
        out_shape=[jax.ShapeDtypeStruct((S, D), F32), jax.ShapeDtypeStruct((1, D), F32)],
        args=(d_q, d_k, d_v, d_conv, d_gl, w_in, x, d_x1, norm_w))


def mm_tn(name, a, b, tm, tn):
    M, N = a.shape[1], b.shape[1]

    def body(a_ref, b_ref, o_ref, ob_ref):
        r = _dot_tn(a_ref[...], b_ref[...])
        o_ref[...] = r
        ob_ref[...] = r.astype(BF16)

    return _call(
        body, name=name, grid=(M // tm, N // tn),
        in_specs=[pl.BlockSpec((S, tm), lambda i, j: (0, i)), pl.BlockSpec((S, tn), lambda i, j: (0, j))],
        out_specs=[pl.BlockSpec((tm, tn), lambda i, j: (i, j))] * 2,
        out_shape=[jax.ShapeDtypeStruct((M, N), F32), jax.ShapeDtypeStruct((M, N), BF16)],
        args=(a, b))


GW_IN_TN = PLANE
GW_IN_STREAMS = 4
GW_IN_SPLIT = (768, 256)


def gw_in_t(name, ht, d_segs, col0, hw, sides=()):
    tn = GW_IN_TN
    starts, t0 = [], 0
    for seg in d_segs:
        starts.append(t0)
        t0 += seg.shape[0]
    ntiles = [seg.shape[0] for seg in d_segs]

    ns, slab = GW_IN_STREAMS, S // GW_IN_STREAMS

    def body(h_ref, *refs):
        a_refs, o_ref, ob_ref = refs[:-2], refs[-2], refs[-1]
        n = pl.program_id(0)
        for j, (st, nt) in enumerate(zip(starts, ntiles)):
            @pl.when((n >= st) & (n < st + nt))
            def _(slabs=a_refs[ns * j:ns * (j + 1)]):
                r = sum(_dot(h_ref[:, q * slab:(q + 1) * slab], a[...]) for q, a in enumerate(slabs)).T
                o_ref[...] = r
                ob_ref[...] = r.astype(BF16)

    def seg_specs(st, nt):
        return [pl.BlockSpec((None, slab, tn), lambda n, q=q: (jnp.clip(n - st, 0, nt - 1), q, 0)) for q in range(ns)]

    res = _call(
        body, sides, name=name, grid=(INW // tn,),
        in_specs=[pl.BlockSpec((hw, S), lambda n: (col0 // hw, 0))]
        + [sp for st, nt in zip(starts, ntiles) for sp in seg_specs(st, nt)],
        out_specs=[pl.BlockSpec((tn, hw), lambda n: (n, 0))] * 2,
        out_shape=[jax.ShapeDtypeStruct((INW, hw), F32), jax.ShapeDtypeStruct((INW, hw), BF16)],
        args=(ht, *[seg for seg in d_segs for _ in range(ns)]))
    return (res[0], res[1]) if sides else (res, [])


def _place():
    x, y, c = lax.axis_index("x"), lax.axis_index("y"), lax.axis_index("c")
    chips = [(1 - x, y), (x, 1 - y), (1 - x, 1 - y)]
    return x, y, c, chips


def _sems(n):
    return pltpu.SemaphoreType.DMA((n,))


def _remote(src, dst, send, recv, k, to):
    return pltpu.make_async_remote_copy(src_ref=src, dst_ref=dst, send_sem=send.at[k], recv_sem=recv.at[k],
                                        device_id=to, device_id_type=MESH)


def _cast_rows(dst, src, cols=slice(None)):
    rows = src.shape[0]
    step = next((s for s in (128, 64, 32, 16) if rows % s == 0), rows)
    for r0 in range(0, rows, step):
        dst[r0:r0 + step, cols] = src[r0:r0 + step, :].astype(dst.dtype)


def comm_only(name, sides):
    def body():
        pass

    return _call(body, sides, name=name, grid=(1,), in_specs=[], out_specs=[], out_shape=[], args=())[1]


def ag_blocks(shard, dtype):
    R, W = shard.shape

    def copy(outs, scr, k, block, to, src=None):
        dst = outs[0].at[block]
        return _remote(dst if src is None else src, dst, scr[1], scr[2], k, to)

    def local(outs, scr, me):
        return pltpu.make_async_copy(scr[0], outs[0].at[me], scr[3].at[0])

    def start(ins, outs, scr):
        x, y, c, chips = _place()
        me = 4 * x + 2 * y + c
        _cast_rows(scr[0], ins[0])
        local(outs, scr, me).start()
        copy(outs, scr, 0, me, (x, y, 1 - c), src=scr[0]).start()
        for j, (cx, cy) in enumerate(chips):
            copy(outs, scr, 1 + j, me, (cx, cy, c), src=scr[0]).start()

    def finish(ins, outs, scr):
        x, y, c, chips = _place()
        me, sib = 4 * x + 2 * y + c, (x, y, 1 - c)
        passed = []
        for j, (cx, cy) in enumerate(chips):
            theirs = 4 * cx + 2 * cy + c
            copy(outs, scr, 1 + j, theirs, (x, y, c)).wait_recv()
            fwd = copy(outs, scr, 4 + j, theirs, sib)
            fwd.start()
            passed.append(fwd)
        copy(outs, scr, 0, 4 * x + 2 * y + 1 - c, (x, y, c)).wait_recv()
        for j, (cx, cy) in enumerate(chips):
            copy(outs, scr, 4 + j, 4 * cx + 2 * cy + 1 - c, (x, y, c)).wait_recv()
        copy(outs, scr, 0, me, sib, src=scr[0]).wait_send()
        for j, (cx, cy) in enumerate(chips):
            copy(outs, scr, 1 + j, me, (cx, cy, c), src=scr[0]).wait_send()
        for fwd in passed:
            fwd.wait_send()
        local(outs, scr, me).wait()

    return Side((shard,), (VMEM,), (jax.ShapeDtypeStruct((NDEV, R, W), dtype),),
                (pltpu.VMEM((R, W), dtype), _sems(7), _sems(7), _sems(1)), start, finish, None, "dsxy")


def ag_blocks_relay(shard, dtype, transpose=False):
    R, W = shard.shape[::-1] if transpose else shard.shape
    half = R // 2

    def copy(outs, scr, k, block, to, src=None, rows=None):
        dst = outs[0].at[block] if rows is None else outs[0].at[block, pl.ds(rows * half, half), :]
        return _remote(dst if src is None else src, dst, scr[1], scr[2], k, to)

    def local(outs, scr, me):
        return pltpu.make_async_copy(scr[0], outs[0].at[me], scr[3].at[0])

    def own(outs, scr):
        x, y, c, _ = _place()
        me = 4 * x + 2 * y + c
        return [copy(outs, scr, k, me, to, src=scr[0])
                for k, to in enumerate([(x, y, 1 - c), (1 - x, y, c), (x, 1 - y, c)])]

    def start(ins, outs, scr):
        x, y, c, _ = _place()
        if transpose:
            scr[0][...] = ins[0][...].T.astype(dtype)
        else:
            _cast_rows(scr[0], ins[0])
        local(outs, scr, 4 * x + 2 * y + c).start()
        for cp in own(outs, scr):
            cp.start()

    def passed_on(outs, scr):
        x, y, c, _ = _place()
        sib, xn, yn = (x, y, 1 - c), (1 - x, y, c), (x, 1 - y, c)
        b_xn, b_yn, b_dg = 4 * (1 - x) + 2 * y + c, 4 * x + 2 * (1 - y) + c, 4 * (1 - x) + 2 * (1 - y) + c
        near = [copy(outs, scr, 5, b_xn, yn, rows=0), copy(outs, scr, 3, b_xn, sib),
                copy(outs, scr, 6, b_yn, xn, rows=1), copy(outs, scr, 4, b_yn, sib)]
        far = [copy(outs, scr, 7, b_dg, sib, rows=0), copy(outs, scr, 8, b_dg, sib, rows=1)]
        return (b_xn, b_yn, b_dg), near, far

    def mid(ins, outs, scr):
        x, y, c, _ = _place()
        (b_xn, b_yn, _), near, _ = passed_on(outs, scr)
        copy(outs, scr, 1, b_xn, (x, y, c)).wait_recv()
        near[0].start()
        near[1].start()
        copy(outs, scr, 2, b_yn, (x, y, c)).wait_recv()
        near[2].start()
        near[3].start()

    def finish(ins, outs, scr):
        x, y, c, _ = _place()
        here = (x, y, c)
        (b_xn, b_yn, b_dg), near, far = passed_on(outs, scr)
        copy(outs, scr, 5, b_dg, here, rows=0).wait_recv()
        far[0].start()
        copy(outs, scr, 6, b_dg, here, rows=1).wait_recv()
        far[1].start()
        flip = 1 - 2 * c
        copy(outs, scr, 0, 4 * x + 2 * y + 1 - c, here).wait_recv()
        copy(outs, scr, 3, b_xn + flip, here).wait_recv()
        copy(outs, scr, 4, b_yn + flip, here).wait_recv()
        copy(outs, scr, 7, b_dg + flip, here, rows=0).wait_recv()
        copy(outs, scr, 8, b_dg + flip, here, rows=1).wait_recv()
        for cp in own(outs, scr) + near + far:
            cp.wait_send()
        local(outs, scr, 4 * x + 2 * y + c).wait()

    return Side((shard,), (VMEM,), (jax.ShapeDtypeStruct((NDEV, R, W), dtype),),
                (pltpu.VMEM((R, W), dtype), _sems(9), _sems(9), _sems(1)), start, finish, mid, "sxy")


def copies_side(args, out_shape, n_copies, plan, peers):
    def copies(ins, outs, scr):
        return [_remote(s_, d_, scr[0], scr[1], i, to) for i, (s_, d_, to) in enumerate(plan(ins, outs))]

    def start(ins, outs, scr):
        for cp in copies(ins, outs, scr):
            cp.start()

    def finish(ins, outs, scr):
        for cp in copies(ins, outs, scr):
            cp.wait()

    return Side(tuple(args), (ANY,) * len(args), tuple(out_shape), (_sems(n_copies), _sems(n_copies)),
                start, finish, None, peers)


def rs_to_sibling(grads):
    out_shape = [jax.ShapeDtypeStruct((4,) + g.shape[1:], BF16) for g in grads]

    def plan(ins, outs):
        x, y, c, _ = _place()
        return [(g.at[2 * k + 1 - c], r.at[k], (x, y, 1 - c)) for g, r in zip(ins, outs) for k in range(4)]

    return copies_side(grads, out_shape, 4 * len(grads), plan, "s")


def rs_to_chips(parts):
    out_shape = [jax.ShapeDtypeStruct((3,) + p.shape[1:], BF16) for p in parts]

    def plan(ins, outs):
        x, y, c, chips = _place()
        return [(p.at[2 * cx + cy], r.at[j], (cx, cy, c))
                for p, r in zip(ins, outs) for j, (cx, cy) in enumerate(chips)]

    return copies_side(parts, out_shape, 3 * len(parts), plan, "dxy")


def rs_to_chips_combined(part):
    _, R, W = part.shape
    half = R // 2
    top, bot = pl.ds(0, half), pl.ds(half, half)

    def copies(ins, outs, scr):
        p, r = ins[0], outs[0]
        loc_a, loc_b, in_x, in_y, comb_a, comb_b, send, recv, loc = scr
        x, y, c, _ = _place()
        xn, yn = (1 - x, y, c), (x, 1 - y, c)
        k_xn, k_yn, k_dg = 2 * (1 - x) + y, 2 * x + 1 - y, 2 * (1 - x) + 1 - y
        direct = [_remote(p.at[k_xn, top, :], r.at[0, top, :], send, recv, 0, xn),
                  _remote(p.at[k_yn, bot, :], r.at[1, bot, :], send, recv, 1, yn),
                  _remote(p.at[k_dg, top, :], in_x, send, recv, 2, xn),
                  _remote(p.at[k_dg, bot, :], in_y, send, recv, 3, yn)]
        combined = [_remote(comb_a, r.at[1, top, :], send, recv, 4, yn),
                    _remote(comb_b, r.at[0, bot, :], send, recv, 5, xn)]
        local = [pltpu.make_async_copy(p.at[k_yn, top, :], loc_a, loc.at[0]),
                 pltpu.make_async_copy(p.at[k_xn, bot, :], loc_b, loc.at[1])]
        return direct, combined, local

    def start(ins, outs, scr):
        direct, _, local = copies(ins, outs, scr)
        for cp in local + direct:
            cp.start()

    def mid(ins, outs, scr):
        loc_a, loc_b, in_x, in_y, comb_a, comb_b = scr[:6]
        direct, combined, local = copies(ins, outs, scr)
        for mine, arrival, inbox, out, nxt in ((local[0], direct[2], in_x, comb_a, combined[0]),
                                               (local[1], direct[3], in_y, comb_b, combined[1])):
            mine.wait()
            arrival.wait_recv()
            src = loc_a if out is comb_a else loc_b
            out[...] = (src[...].astype(F32) + inbox[...].astype(F32)).astype(BF16)
            nxt.start()

    def finish(ins, outs, scr):
        direct, combined, _ = copies(ins, outs, scr)
        direct[0].wait_recv()
        direct[1].wait_recv()
        combined[0].wait_recv()
        combined[1].wait_recv()
        for cp in direct + combined:
            cp.wait_send()

    buf = pltpu.VMEM((half, W), BF16)
    return Side((part,), (ANY,), (jax.ShapeDtypeStruct((2, R, W), BF16),),
                (buf, buf, buf, buf, buf, buf, _sems(6), _sems(6), _sems(2)), start, finish, mid, "xy")


ADAM_TILE_BYTES = 3 * 512 * 1024


def _row_tiles(rows, width):
    return 2 if rows % 32 == 0 and rows * width * 4 > ADAM_TILE_BYTES else 1


def chip_sum(name, grad, recv, c_idx, chip_idx):
    _, R, C = grad.shape
    nt = 1
    tr = R // nt

    def body(s_ref, g_ref, r_ref, p_ref, own_ref):
        k = pl.program_id(1)
        tot = g_ref[0] + r_ref[0].astype(F32)
        p_ref[0] = tot.astype(BF16)

        @pl.when(k == s_ref[1])
        def _():
            own_ref[...] = tot

    grid_spec = pltpu.PrefetchScalarGridSpec(
        num_scalar_prefetch=1, grid=(nt, 4),
        in_specs=[pl.BlockSpec((1, tr, C), lambda i, k, s: (2 * k + s[0], i, 0)),
                  pl.BlockSpec((1, tr, C), lambda i, k, s: (k, i, 0))],
        out_specs=[pl.BlockSpec((1, tr, C), lambda i, k, s: (k, i, 0)),
                   pl.BlockSpec((tr, C), lambda i, k, s: (i, 0))])
    return pl.pallas_call(
        body, name=name, grid_spec=grid_spec,
        out_shape=[jax.ShapeDtypeStruct((4, R, C), BF16), jax.ShapeDtypeStruct((R, C), F32)],
        compiler_params=_cp(dimension_semantics=("arbitrary", "arbitrary")),
    )(jnp.stack([c_idx, chip_idx]), grad, recv)


def _adamw(w, g, m, v):
    m2 = ADAM_B1 * m + (1.0 - ADAM_B1) * g
    v2 = ADAM_B2 * v + (1.0 - ADAM_B2) * (g * g)
    m_hat = m2 / (1.0 - ADAM_B1 ** ADAM_STEP)
    v_hat = v2 / (1.0 - ADAM_B2 ** ADAM_STEP)
    delta = -ADAM_LR * (m_hat / (jnp.sqrt(v_hat) + ADAM_EPS) + ADAM_WD * w)
    return delta, m2, v2


def shard_adam(name, owns, recvs, w, m, v, io_t=False):
    n = len(owns)
    R = owns[0].shape[0]
    ct = min(o.shape[1] for o in owns)
    first = [sum(o.shape[1] for o in owns[:j]) // ct for j in range(n)]
    count = [o.shape[1] // ct for o in owns]
    nt = _row_tiles(R, ct)
    tr = R // nt

    def body(*refs):
        o_refs, r_refs = refs[:n], refs[n:2 * n]
        w_ref, m_ref, v_ref, g_ref, d_ref, nm_ref, nv_ref = refs[2 * n:]
        g = None
        for j in range(n):
            gj = o_refs[j][...]
            for q in range(recvs[j].shape[0]):
                gj = gj + r_refs[j][q].astype(F32)
            g = gj if g is None else jnp.where(pl.program_id(0) >= first[j], gj, g)
        t = (lambda a: a.T) if io_t else (lambda a: a)
        delta, m2, v2 = _adamw(t(w_ref[...]), g, t(m_ref[...]), t(v_ref[...]))
        g_ref[...] = t(g)
        d_ref[...] = t(delta)
        nm_ref[...] = t(m2)
        nv_ref[...] = t(v2)

    def part(j):
        return pl.BlockSpec((tr, ct), lambda k, i: (i, jnp.clip(k - first[j], 0, count[j] - 1)))

    def part3(j):
        return pl.BlockSpec((recvs[j].shape[0], tr, ct), lambda k, i: (0, i, jnp.clip(k - first[j], 0, count[j] - 1)))

    C = sum(count) * ct
    tile = pl.BlockSpec((ct, tr), lambda k, i: (k, i)) if io_t else pl.BlockSpec((tr, ct), lambda k, i: (i, k))
    return pl.pallas_call(
        body, name=name, grid=(sum(count), nt),
        in_specs=[part(j) for j in range(n)] + [part3(j) for j in range(n)] + [tile, tile, tile],
        out_specs=[tile] * 4, out_shape=[jax.ShapeDtypeStruct((C, R) if io_t else (R, C), F32)] * 4,
        compiler_params=_cp(dimension_semantics=("arbitrary", "arbitrary")),
    )(*owns, *recvs, w, m, v)


ROW_N1, ROW_N2, ROW_BG, ROW_QN, ROW_KN, ROW_CB, ROW_LW, ROW_LB, ROW_CW = 0, 1, 2, 4, 5, 6, 7, 8, 9
PACK_ROWS = 40
SMALL = ("norm1_w", "norm2_w", "b_gate", "q_norm_w", "k_norm_w", "conv_b", "conv_ln_w", "conv_ln_b", "conv_w")


def small_sync(g, sq, sides=()):
    ns = len(SMALL)

    def copies(refs):
        pack, recv, send_sems, recv_sems = refs[ns + 2:]
        x, y, c, _ = _place()
        return [pltpu.make_async_remote_copy(
            src_ref=pack, dst_ref=recv.at[4 * x + 2 * y + c], send_sem=send_sems.at[k - 1],
            recv_sem=recv_sems.at[k - 1], device_id=(x ^ (k >> 2), y ^ ((k >> 1) & 1), c ^ (k & 1)),
            device_id_type=MESH) for k in range(1, NDEV)]

    def body(*refs):
        gi = dict(zip(SMALL, refs[:ns]))
        sq_ref, tot, pack, recv, send_sems, recv_sems = refs[ns:]
        x, y, c, _ = _place()
        me = 4 * x + 2 * y + c

        pack[...] = jnp.zeros_like(pack)
        pack[ROW_KN:ROW_KN + 1, LANES:2 * LANES] = jnp.full((1, LANES), (0.5 / D) * jnp.sum(sq_ref[...]), F32)
        pack[ROW_N1:ROW_N1 + 1, :] = gi["norm1_w"][...]
        pack[ROW_N2:ROW_N2 + 1, :] = gi["norm2_w"][...]
        pack[ROW_BG:ROW_BG + 2, :] = gi["b_gate"][...]
        pack[ROW_QN:ROW_QN + 1, 0:HD] = gi["q_norm_w"][...]
        pack[ROW_KN:ROW_KN + 1, 0:HD] = gi["k_norm_w"][...]
        pack[ROW_CB:ROW_CB + 1, 0:CC] = gi["conv_b"][...]
        pack[ROW_LW:ROW_LW + 1, 0:CC] = gi["conv_ln_w"][...]
        pack[ROW_LB:ROW_LB + 1, 0:CC] = gi["conv_ln_b"][...]
        pack[ROW_CW:ROW_CW + KW, 0:CC] = gi["conv_w"][...]

        for cp in copies(refs):
            cp.start()
        recv[me] = pack[...]

    def tail(*refs):
        tot, recv = refs[ns + 1], refs[ns + 3]
        for cp in copies(refs):
            cp.wait()
        acc = recv[0]
        for p in range(1, NDEV):
            acc = acc + recv[p]
        tot[...] = acc

    args = [g[k] for k in SMALL] + [sq]
    res = _call(
        body, sides, name="small_sync", grid=(1,), in_specs=[VMEM] * len(args), out_specs=[VMEM],
        out_shape=[jax.ShapeDtypeStruct((PACK_ROWS, D), F32)],
        scratch_shapes=[pltpu.VMEM((PACK_ROWS, D), F32), pltpu.VMEM((NDEV, PACK_ROWS, D), F32),
                        _sems(NDEV - 1), _sems(NDEV - 1)],
        args=args, own_comm=True, tail=tail)
    return (res[0][0], res[1]) if sides else res[0]


def small_adam(tot, w, m, v, me):
    ns = len(SMALL)

    def body(me_ref, tot, *refs):
        wi = dict(zip(SMALL, refs[:ns]))
        mi = dict(zip(SMALL, refs[ns:2 * ns]))
        vi = dict(zip(SMALL, refs[2 * ns:3 * ns]))
        outs = refs[3 * ns:7 * ns]
        loss_ref = refs[7 * ns]
        me = me_ref[0]

        def shard_grad(name):
            if name == "b_gate":
                return tot[ROW_BG:ROW_BG + 2, pl.ds(pl.multiple_of(me * LANES, LANES), LANES)]
            if name == "conv_w":
                win = tot[ROW_CW:ROW_CW + KW, pl.ds(pl.multiple_of((me // 2) * LANES, LANES), LANES)]
                return jnp.where(me % 2 == 1, win[:, HD:LANES], win[:, 0:HD])
            row = {"norm1_w": ROW_N1, "norm2_w": ROW_N2, "q_norm_w": ROW_QN, "k_norm_w": ROW_KN,
                   "conv_b": ROW_CB, "conv_ln_w": ROW_LW, "conv_ln_b": ROW_LB}[name]
            return tot[row:row + 1, 0:wi[name].shape[1]]

        for i, name in enumerate(SMALL):
            gr = shard_grad(name)
            delta, m2, v2 = _adamw(wi[name][...], gr, mi[name][...], vi[name][...])
            outs[4 * i][...] = gr
            outs[4 * i + 1][...] = delta
            outs[4 * i + 2][...] = m2
            outs[4 * i + 3][...] = v2
        loss_ref[...] = tot[ROW_KN:ROW_KN + 1, LANES:2 * LANES]

    out_shape = []
    for name in SMALL:
        out_shape += [jax.ShapeDtypeStruct(w[name].shape, F32)] * 4
    out_shape.append(jax.ShapeDtypeStruct((1, LANES), F32))
    args = [tot] + [w[k] for k in SMALL] + [m[k] for k in SMALL] + [v[k] for k in SMALL]
    grid_spec = pltpu.PrefetchScalarGridSpec(
        num_scalar_prefetch=1, grid=(1,), in_specs=[VMEM] * len(args), out_specs=[VMEM] * len(out_shape))
    res = pl.pallas_call(body, name="small_adam", grid_spec=grid_spec, out_shape=out_shape)(me, *args)
    out = {name: tuple(res[4 * i:4 * i + 4]) for i, name in enumerate(SMALL)}
    return out, res[4 * ns][0, 0]


MATS = ("w_in", "w_o_attn", "w_pw_conv", "w_out", "w_ffn_in", "w_ffn_out")
TRANSPOSED = ("w_in", "w_ffn_in")
WEIGHTS = ("norm1_w", "w_in", "b_gate", "q_norm_w", "k_norm_w", "w_o_attn", "conv_w", "conv_b", "conv_ln_w",
           "conv_ln_b", "w_pw_conv", "w_out", "norm2_w", "w_ffn_in", "w_ffn_out")


def _blocks_to_cols(blocks):
    n, R, C = blocks.shape
    return blocks.transpose(1, 0, 2).reshape(R, n * C)


def kernel(x, positions, norm1_w, w_in, b_gate, q_norm_w, k_norm_w, w_o_attn, conv_w, conv_b, conv_ln_w, conv_ln_b, w_pw_conv, w_out, norm2_w, w_ffn_in, w_ffn_out, loss_target, m_norm1_w, m_w_in, m_b_gate, m_q_norm_w, m_k_norm_w, m_w_o_attn, m_conv_w, m_conv_b, m_conv_ln_w, m_conv_ln_b, m_w_pw_conv, m_w_out, m_norm2_w, m_w_ffn_in, m_w_ffn_out, v_norm1_w, v_w_in, v_b_gate, v_q_norm_w, v_k_norm_w, v_w_o_attn, v_conv_w, v_conv_b, v_conv_ln_w, v_conv_ln_b, v_w_pw_conv, v_w_out, v_norm2_w, v_w_ffn_in, v_w_ffn_out):
    w = dict(norm1_w=norm1_w, w_in=w_in, b_gate=b_gate, q_norm_w=q_norm_w, k_norm_w=k_norm_w, w_o_attn=w_o_attn,
             conv_w=conv_w, conv_b=conv_b, conv_ln_w=conv_ln_w, conv_ln_b=conv_ln_b, w_pw_conv=w_pw_conv,
             w_out=w_out, norm2_w=norm2_w, w_ffn_in=w_ffn_in, w_ffn_out=w_ffn_out)
    m = dict(norm1_w=m_norm1_w, w_in=m_w_in, b_gate=m_b_gate, q_norm_w=m_q_norm_w, k_norm_w=m_k_norm_w,
             w_o_attn=m_w_o_attn, conv_w=m_conv_w, conv_b=m_conv_b, conv_ln_w=m_conv_ln_w,
             conv_ln_b=m_conv_ln_b, w_pw_conv=m_w_pw_conv, w_out=m_w_out, norm2_w=m_norm2_w,
             w_ffn_in=m_w_ffn_in, w_ffn_out=m_w_ffn_out)
    v = dict(norm1_w=v_norm1_w, w_in=v_w_in, b_gate=v_b_gate, q_norm_w=v_q_norm_w, k_norm_w=v_k_norm_w,
             w_o_attn=v_w_o_attn, conv_w=v_conv_w, conv_b=v_conv_b, conv_ln_w=v_conv_ln_w,
             conv_ln_b=v_conv_ln_b, w_pw_conv=v_w_pw_conv, w_out=v_w_out, norm2_w=v_norm2_w,
             w_ffn_in=v_w_ffn_in, w_ffn_out=v_w_ffn_out)
    def two_d(t):
        t = {k: (a[0] if a.ndim == 3 else a) for k, a in t.items()}
        return {k: (a.T if k in TRANSPOSED else a) for k, a in t.items()}

    w, m, v = two_d(w), two_d(m), two_d(v)

    x2, target = x[0], loss_target[0]
    c_idx = lax.axis_index("c").astype(jnp.int32)
    chip_idx = (2 * lax.axis_index("x") + lax.axis_index("y")).astype(jnp.int32)
    qw2 = jnp.tile(w["q_norm_w"], (1, 2))
    kw2 = jnp.tile(w["k_norm_w"], (1, 2))

    ax, ay = lax.axis_index("x"), lax.axis_index("y")
    chip_order = jnp.stack([2 * ax + ay, 2 * (1 - ax) + ay, 2 * ax + 1 - ay, 2 * (1 - ax) + 1 - ay]).astype(jnp.int32)
    h_t, proj, w_in_blocks, tabs = in_proj_gather(x2, w["norm1_w"], w["w_in"], chip_order, positions.reshape(S, 1))
    w_in_t = w_in_blocks.reshape(INW, D)
    (attn, lse), ((w_ffn_in_blocks,), (w_out_blocks,), (w_o_blocks,), (w_pw_blocks,), (bg_blocks,), (cw_blocks,)) = attn_fwd(
        proj, tabs, qw2, kw2, sides=(ag_blocks_relay(w["w_ffn_in"], BF16), ag_blocks_relay(w["w_out"], BF16),
                                     ag_blocks_relay(w["w_o_attn"], BF16, transpose=True),
                                     ag_blocks_relay(w["w_pw_conv"], BF16, transpose=True),
                                     ag_blocks(w["b_gate"], F32), ag_blocks(w["conv_w"], F32)))
    w_ffn_in_t = w_ffn_in_blocks.reshape(2 * FF, D)
    w_out_f = w_out_blocks.reshape(D, D)
    w_o_t, w_pw_t = w_o_blocks.reshape(D, CC), w_pw_blocks.reshape(D, CC)
    b_gate_f, conv_w_f = _blocks_to_cols(bg_blocks), _blocks_to_cols(cw_blocks)
    cpre, u3 = conv_fwd(proj, conv_w_f, w["conv_b"], w["conv_ln_w"], w["conv_ln_b"])
    x1, z, ya, yb = mix_out(x2, proj, b_gate_f, attn, u3, w_o_t, w_pw_t, w_out_f)
    (h2, gu, f), ((w_ffn_out_blocks,),) = ffn_in(x1, w["norm2_w"], w_ffn_in_t, sides=(ag_blocks_relay(w["w_ffn_out"], BF16),))
    w_ffn_out_f = w_ffn_out_blocks.reshape(FF, D)
    dy, dyb, sq = ffn_out_loss(x1, f, w_ffn_out_f, target)

    g = {}
    def blocks(name, a, b, tm):
        return [t.reshape(NDEV, a.shape[1] // NDEV, b.shape[1]) for t in mm_tn(name, a, b, tm, b.shape[1])]

    g_ffn_out, gb_ffn_out = blocks("gw_ffn_out", f, dyb, FF // 2)
    (d_gu, d_x1, d_x1b, g["norm2_w"]), ((ra_ffn_out,),) = ffn_bwd(
        dy, dyb, gu, x1, w["norm2_w"], w_ffn_in_t, w_ffn_out_f, sides=(rs_to_sibling([gb_ffn_out]),))
    pb_ffn_out, own_ffn_out = chip_sum("chip_sum_w_ffn_out", g_ffn_out, ra_ffn_out, c_idx, chip_idx)
    g_ffn_in, gb_ffn_in = blocks("gw_ffn_in", d_gu, h2, FF // 2)
    g_out, gb_out = blocks("gw_out", z, d_x1b, D // 2)
    (d_ya, d_yb, d_gl, d_attn, d_u3, g["b_gate"]), ((ra_ffn_in,),) = out_bwd(
        d_x1b, proj, b_gate_f, ya, yb, w_o_t, w_pw_t, w_out_f, sides=(rs_to_sibling([gb_ffn_in]),))
    pb_ffn_in, own_ffn_in = chip_sum("chip_sum_w_ffn_in", g_ffn_in, ra_ffn_in, c_idx, chip_idx)
    g_w_o, gb_w_o = blocks("gw_o_attn", d_ya, attn, D // 2)
    g_w_pw, gb_w_pw = blocks("gw_pw_conv", d_yb, u3, D // 2)
    (d_conv, g["conv_w"], g["conv_b"], g["conv_ln_w"], g["conv_ln_b"]), ((ra_out, ra_w_o, ra_w_pw),) = conv_bwd(
        proj, cpre, d_u3, conv_w_f, conv_w_f[::-1], w["conv_ln_w"], w["conv_ln_b"],
        sides=(rs_to_sibling([gb_out, gb_w_o, gb_w_pw]),))
    pb_out, own_out = chip_sum("chip_sum_w_out", g_out, ra_out, c_idx, chip_idx)
    pb_w_o, own_w_o = chip_sum("chip_sum_w_o_attn", g_w_o, ra_w_o, c_idx, chip_idx)
    pb_w_pw, own_w_pw = chip_sum("chip_sum_w_pw_conv", g_w_pw, ra_w_pw, c_idx, chip_idx)
    (d_q, d_k, d_v, gqw, gkw), ((rb_ffn_out, rb_ffn_in, rb_out, rb_w_o, rb_w_pw),) = attn_bwd(
        proj, tabs, qw2, kw2, d_attn, attn, lse,
        sides=(rs_to_chips([pb_ffn_out, pb_ffn_in, pb_out, pb_w_o, pb_w_pw]),))
    g["q_norm_w"] = gqw[0:1, 0:HD] + gqw[0:1, HD:LANES]
    g["k_norm_w"] = gkw[0:1, 0:HD] + gkw[0:1, HD:LANES]
    d_segs = (d_q, d_k, d_v, d_conv, d_gl)
    parts, to_sibling, to_chips, owns, from_chips = [], None, None, [], []
    for k, hw in enumerate(GW_IN_SPLIT):
        sides = tuple(s for s in (to_chips, to_sibling) if s is not None)
        (part, part_b), outs = gw_in_t("gw_in_%d" % k, h_t, d_segs, sum(GW_IN_SPLIT[:k]), hw, sides=sides)
        outs = list(outs)
        if to_chips is not None:
            from_chips.append(outs.pop(0)[0])
        if to_sibling is not None:
            pb, own = chip_sum("chip_sum_w_in_%d" % (k - 1), parts[-1], outs.pop(0)[0], c_idx, chip_idx)
            owns.append(own)
            to_chips = rs_to_chips_combined(pb)
        else:
            to_chips = None
        parts.append(part.reshape(NDEV, INW // NDEV, hw))
        to_sibling = rs_to_sibling([part_b.reshape(NDEV, INW // NDEV, hw)])
    (grad_x, g["norm1_w"]), ((rb_prev,), (ra_last,)) = in_bwd(
        d_q, d_k, d_v, d_conv, d_gl, w_in_t, x2, d_x1, w["norm1_w"], sides=(to_chips, to_sibling))
    from_chips.append(rb_prev)
    pb, own = chip_sum("chip_sum_w_in_%d" % (len(GW_IN_SPLIT) - 1), parts[-1], ra_last, c_idx, chip_idx)
    owns.append(own)
    small_sums, ((rb_last,),) = small_sync(g, sq, sides=(rs_to_chips_combined(pb),))
    small, loss = small_adam(small_sums, w, m, v, (4 * ax + 2 * ay + c_idx).astype(jnp.int32).reshape(1))
    from_chips.append(rb_last)

    res = {
        "w_in": shard_adam("adam_w_in", owns, from_chips, w["w_in"], m["w_in"], v["w_in"]),
        "w_ffn_in": shard_adam("adam_w_ffn_in", [own_ffn_in], [rb_ffn_in], w["w_ffn_in"], m["w_ffn_in"], v["w_ffn_in"]),
        "w_o_attn": shard_adam("adam_w_o_attn", [own_w_o], [rb_w_o], w["w_o_attn"], m["w_o_attn"], v["w_o_attn"], io_t=True),
        "w_pw_conv": shard_adam("adam_w_pw_conv", [own_w_pw], [rb_w_pw],
                                w["w_pw_conv"], m["w_pw_conv"], v["w_pw_conv"], io_t=True),
        "w_out": shard_adam("adam_w_out", [own_out], [rb_out], w["w_out"], m["w_out"], v["w_out"]),
        "w_ffn_out": shard_adam("adam_w_ffn_out", [own_ffn_out], [rb_ffn_out],
                                w["w_ffn_out"], m["w_ffn_out"], v["w_ffn_out"]),
    }
    res = {k: tuple(a.T if k in TRANSPOSED else a for a in r) for k, r in res.items()}
    res.update(small)

    def shaped(name, a):
        return a.reshape((1,) + a.shape) if name in MATS or name in ("b_gate", "conv_w") else a

    outs = [loss, grad_x.reshape(1, S, D)]
    for i in range(4):
        outs += [shaped(k, res[k][i]) for k in WEIGHTS]
    return tuple(outs)
```

```python
import functools
from typing import Callable, NamedTuple, Optional

import numpy as np
import jax
import jax.numpy as jnp
from jax import lax
from jax.experimental import pallas as pl
from jax.experimental.pallas import tpu as pltpu

F32 = jnp.float32
BF16 = jnp.bfloat16

S = 2048
D = 1024
HD = 64
QKV = 1536
CC = 512
KW = 31
FF = 2816
INW = 7680
OFF_Q, OFF_K, OFF_V, OFF_CA, OFF_CB, OFF_GA, OFF_GB = 0, 1536, 3072, 4608, 5120, 5632, 6656
DILATIONS = (1, 4, 16)
HALF_SPAN = 64
EPS = 1e-6
NEG_INF = -1e30
ROPE_THETA = 500000.0
ROT_DIM = 16

ADAM_LR = 0.001
ADAM_B1 = 0.9
ADAM_B2 = 0.999
ADAM_EPS = 1e-08
ADAM_WD = 0.01
ADAM_STEP = 10

NDEV = 8
LANES = 128
TM = 256
IN_PROJ_TM = 512
TQ = 128
VMEM_LIMIT = 56 * 1024 * 1024
MESH = pl.DeviceIdType.MESH


def _cp(**kw):
    return pltpu.CompilerParams(vmem_limit_bytes=VMEM_LIMIT, **kw)


def _row(width, col=0, tm=TM):
    return pl.BlockSpec((tm, width), lambda i: (i, col))


PLANE = 512


def _planes(width, tm=TM):
    return pl.BlockSpec((width // PLANE, tm, PLANE), lambda i: (0, i, 0))


def _res(shape):
    nd = len(shape)
    return pl.BlockSpec(shape, lambda *_: (0,) * nd, pipeline_mode=pl.Buffered(1))


def _dot(a, b):
    return jnp.dot(a, b, preferred_element_type=F32)


def _dot_nt(a, b):
    return lax.dot_general(a, b, (((1,), (1,)), ((), ())), preferred_element_type=F32)


def _dot_tn(a, b):
    return lax.dot_general(a, b, (((0,), (0,)), ((), ())), preferred_element_type=F32)


def _sigmoid(x):
    return jax.nn.sigmoid(x)


def _dsilu(x, sg):
    return sg * (1.0 + x * (1.0 - sg))


ANY = pl.BlockSpec(memory_space=pl.ANY)
VMEM = pl.BlockSpec(memory_space=pltpu.VMEM)


class Side(NamedTuple):
    args: tuple
    in_specs: tuple
    out_shape: tuple
    scratch: tuple
    start: Callable
    finish: Callable
    mid: Optional[Callable] = None
    peers: str = ""


BARRIER_IDS = {"s": 0, "dxy": 1, "dsxy": 2, "sxy": 3, "xy": 4}


def _peer_barrier(peers):
    x, y, c = lax.axis_index("x"), lax.axis_index("y"), lax.axis_index("c")
    where = {"s": (x, y, 1 - c), "x": (1 - x, y, c), "y": (x, 1 - y, c), "d": (1 - x, 1 - y, c)}
    barrier = pltpu.get_barrier_semaphore()
    for p in peers:
        pl.semaphore_signal(barrier, inc=1, device_id=where[p], device_id_type=MESH)
    pl.semaphore_wait(barrier, len(peers))


def _call(body, sides=(), *, name, grid, in_specs, out_specs, out_shape, scratch_shapes=(), args, own_comm=False,
          tail=None):
    assert tail is None or int(np.prod(grid)) == 1
    ni, no, ns = len(in_specs), len(out_specs), len(scratch_shapes)
    cnt = [(len(s.args), len(s.out_shape), len(s.scratch)) for s in sides]
    peers = "".join(sorted(set("".join(s.peers for s in sides))))
    if own_comm or not sides or any(not s.peers for s in sides):
        peers = ""

    def take(refs, pos, n):
        return refs[pos:pos + n], pos + n

    def full(*refs):
        m_in, pos = take(refs, 0, ni)
        s_in = []
        for a, _, _ in cnt:
            r, pos = take(refs, pos, a)
            s_in.append(r)
        m_out, pos = take(refs, pos, no)
        s_out = []
        for _, o, _ in cnt:
            r, pos = take(refs, pos, o)
            s_out.append(r)
        m_scr, pos = take(refs, pos, ns)
        s_scr = []
        for _, _, c in cnt:
            r, pos = take(refs, pos, c)
            s_scr.append(r)
        if sides:
            first = functools.reduce(jnp.logical_and, [pl.program_id(d) == 0 for d in range(len(grid))])
            last = functools.reduce(jnp.logical_and, [pl.program_id(d) == g - 1 for d, g in enumerate(grid)])

            @pl.when(first)
            def _():
                if peers:
                    _peer_barrier(peers)
                for s, a, o, c in zip(sides, s_in, s_out, s_scr):
                    s.start(a, o, c)

            steps = int(np.prod(grid))
            mid_step = (2 * steps) // 3
            if steps > 1 and any(s.mid is not None for s in sides):
                step = functools.reduce(lambda acc, d: acc * grid[d] + pl.program_id(d), range(len(grid)), 0)

                @pl.when(step == mid_step)
                def _():
                    for s, a, o, c in zip(sides, s_in, s_out, s_scr):
                        if s.mid is not None:
                            s.mid(a, o, c)

        body(*m_in, *m_out, *m_scr)
        if sides:
            @pl.when(last)
            def _():
                for s, a, o, c in zip(sides, s_in, s_out, s_scr):
                    if s.mid is not None and steps == 1:
                        s.mid(a, o, c)
                if tail is not None:
                    tail(*m_in, *m_out, *m_scr)
                for s, a, o, c in zip(sides, s_in, s_out, s_scr):
                    s.finish(a, o, c)
        elif tail is not None:
            tail(*m_in, *m_out, *m_scr)

    res = pl.pallas_call(
        full, name=name, grid=grid,
        in_specs=list(in_specs) + [sp for s in sides for sp in s.in_specs],
        out_specs=list(out_specs) + [ANY for s in sides for _ in s.out_shape],
        out_shape=list(out_shape) + [o for s in sides for o in s.out_shape],
        scratch_shapes=list(scratch_shapes) + [c for s in sides for c in s.scratch],
        compiler_params=_cp(dimension_semantics=("arbitrary",) * len(grid),
                            **({"collective_id": BARRIER_IDS[peers]} if peers else {})),
    )(*args, *[a for s in sides for a in s.args])
    res = list(res)
    if not sides:
        return res
    outs, pos = take(res, 0, no)
    side_outs = []
    for _, o, _ in cnt:
        r, pos = take(res, pos, o)
        side_outs.append(r)
    return outs, side_outs


def _inv_freq_lanes():
    inv = np.float32(ROPE_THETA) ** (-np.arange(0, ROT_DIM, 2, dtype=np.float32) / np.float32(ROT_DIM))
    lane = np.arange(LANES) % HD
    out = np.where(lane < ROT_DIM, inv[lane % (ROT_DIM // 2)], 0.0).astype(np.float32)
    return jnp.asarray(out.reshape(1, LANES))


def _rope_tables(pos, inv_freq):
    ang = pos.astype(F32) * inv_freq
    lane = lax.broadcasted_iota(jnp.int32, ang.shape, 1) % HD
    cs = jnp.cos(ang)
    sn = jnp.sin(ang)
    return (jnp.where(lane < ROT_DIM, cs, 1.0), jnp.where(lane < ROT_DIM // 2, -sn, 0.0),
            jnp.where(lane < ROT_DIM // 2, 0.0, jnp.where(lane < ROT_DIM, sn, 0.0)))


def _rope(v, c, s1, s2):
    return v * c + pltpu.roll(v, LANES - 8, axis=1) * s1 + pltpu.roll(v, 8, axis=1) * s2


def _rope_t(d, c, s1, s2):
    return d * c - pltpu.roll(d, LANES - 8, axis=1) * s1 - pltpu.roll(d, 8, axis=1) * s2


def _head_mat():
    r = lax.broadcasted_iota(jnp.int32, (LANES, LANES), 0) // HD
    c = lax.broadcasted_iota(jnp.int32, (LANES, LANES), 1) // HD
    return jnp.where(r == c, 1.0 / HD, 0.0).astype(BF16)


def _head_mean(t, e):
    hi = t.astype(BF16)
    rest = (t - hi.astype(F32)).astype(BF16)
    return _dot(hi, e) + _dot(rest, e)


def in_proj_gather(x, norm_w, shard_t, chip_order, pos_col):
    R = INW // NDEV
    tm = IN_PROJ_TM
    half, nt = R // 2, S // tm

    def body(ord_ref, x_ref, nw_ref, sh_ref, pos_ref, f_ref, h_ref, p_ref, wfull_ref, c_ref, s1_ref, s2_ref,
             wt, hs, send, recv, loc):
        kk, i = pl.program_id(0), pl.program_id(1)
        x, y, c, _ = _place()
        me, flip = 4 * x + 2 * y + c, 1 - 2 * c
        here, sib, xn, yn = (x, y, c), (x, y, 1 - c), (1 - x, y, c), (x, 1 - y, c)
        b_xn, b_yn, b_dg = 4 * (1 - x) + 2 * y + c, 4 * x + 2 * (1 - y) + c, 4 * (1 - x) + 2 * (1 - y) + c

        def cp(k, block, to, rows=None):
            dst = wt.at[block] if rows is None else wt.at[block, pl.ds(rows * half, half), :]
            return _remote(dst, dst, send, recv, k, to)

        def sends():
            return [cp(0, me, sib), cp(1, me, xn), cp(2, me, yn), cp(3, b_xn, sib), cp(4, b_yn, sib),
                    cp(5, b_xn, yn, rows=0), cp(6, b_yn, xn, rows=1), cp(7, b_dg, sib, rows=0), cp(8, b_dg, sib, rows=1)]

        def keep(j, blk0):
            pair = pl.ds(pl.multiple_of(blk0, 2), 2)
            return pltpu.make_async_copy(wt.at[pair], wfull_ref.at[pair], loc.at[j])

        @pl.when((kk == 0) & (i == 0))
        def _():
            _peer_barrier("sxy")
            _cast_rows(wt.at[me], sh_ref)
            for s_ in sends()[0:3]:
                s_.start()

            def tables(j, _):
                chunk = pl.ds(pl.multiple_of(j * TM, TM), TM)
                c_ref[chunk, :], s1_ref[chunk, :], s2_ref[chunk, :] = _rope_tables(pos_ref[chunk, :], f_ref[...])
                return 0

            lax.fori_loop(0, S // TM, tables, 0)
            cp(0, me + flip, here).wait_recv()
            keep(0, me - c).start()

        @pl.when((kk == 1) & (i == 0))
        def _():
            cp(1, b_xn, here).wait_recv()
            sends()[5].start()
            sends()[3].start()
            cp(2, b_yn, here).wait_recv()
            sends()[6].start()
            sends()[4].start()
            cp(3, b_xn + flip, here).wait_recv()
            keep(1, b_xn - c).start()

        @pl.when((kk == 2) & (i == 0))
        def _():
            cp(4, b_yn + flip, here).wait_recv()
            keep(2, b_yn - c).start()

        @pl.when((kk == 3) & (i == 0))
        def _():
            cp(5, b_dg, here, rows=0).wait_recv()
            sends()[7].start()
            cp(6, b_dg, here, rows=1).wait_recv()
            sends()[8].start()
            cp(7, b_dg + flip, here, rows=0).wait_recv()
            cp(8, b_dg + flip, here, rows=1).wait_recv()
            keep(3, b_dg - c).start()

        rows = pl.ds(pl.multiple_of(i * tm, tm), tm)

        @pl.when(kk == 0)
        def _():
            xv = x_ref[...]
            r = lax.rsqrt(jnp.mean(xv * xv, axis=-1, keepdims=True) + EPS)
            hb = (xv * r * nw_ref[...]).astype(BF16)
            h_ref[...] = hb
            hs[rows, :] = hb

        h = hs[rows, :]
        chip = ord_ref[kk]
        for cc in range(2):
            p_ref[:, cc * R:(cc + 1) * R] = _dot_nt(h, wt[2 * chip + cc])

        @pl.when((kk == 3) & (i == nt - 1))
        def _():
            for s_ in sends():
                s_.wait_send()
            for j, blk in enumerate((me, b_xn, b_yn, b_dg)):
                keep(j, blk - c).wait()

    def first_pass(kk, i):
        return jnp.where(kk == 0, i, nt - 1)

    grid_spec = pltpu.PrefetchScalarGridSpec(
        num_scalar_prefetch=1, grid=(4, nt),
        in_specs=[pl.BlockSpec((tm, D), lambda kk, i, o: (first_pass(kk, i), 0)),
                  pl.BlockSpec((1, D), lambda kk, i, o: (0, 0)), VMEM, VMEM,
                  pl.BlockSpec((1, LANES), lambda kk, i, o: (0, 0))],
        out_specs=[pl.BlockSpec((tm, D), lambda kk, i, o: (first_pass(kk, i), 0)),
                   pl.BlockSpec((tm, 2 * R), lambda kk, i, o: (i, o[kk])), ANY]
        + [pl.BlockSpec((S, LANES), lambda kk, i, o: (0, 0))] * 3,
        scratch_shapes=[pltpu.VMEM((NDEV, R, D), BF16), pltpu.VMEM((S, D), BF16), _sems(9), _sems(9), _sems(4)])
    res = pl.pallas_call(
        body, name="in_proj_gather", grid_spec=grid_spec,
        out_shape=[jax.ShapeDtypeStruct((S, D), BF16), jax.ShapeDtypeStruct((S, INW), F32),
                   jax.ShapeDtypeStruct((NDEV, R, D), BF16)] + [jax.ShapeDtypeStruct((S, LANES), F32)] * 3,
        compiler_params=_cp(dimension_semantics=("arbitrary", "arbitrary"), collective_id=BARRIER_IDS["sxy"]),
    )(chip_order, x, norm_w, shard_t, pos_col, _inv_freq_lanes())
    return res[0], res[1], res[2], tuple(res[3:])


def _qk_specs():
    nb = QKV // LANES
    return [pl.BlockSpec((S, LANES), functools.partial(lambda hp, g, o: (0, o + g * 4 + hp), o=o))
            for o in (OFF_Q // LANES, OFF_K // LANES, OFF_V // LANES)]


def _tab_specs():
    return [pl.BlockSpec((S, LANES), lambda hp, g: (0, 0), pipeline_mode=pl.Buffered(1))] * 3


def _vec_spec():
    return pl.BlockSpec((1, LANES), lambda hp, g: (0, 0))


def _sub_rows(r, d, start, n):
    if d == 1:
        return pl.ds(start, n)
    return pl.ds(r + d * start, n, stride=d)


def _band_window(i, L):
    W = min(TQ + 2 * HALF_SPAN, L)
    q0 = pl.multiple_of(i * TQ, TQ)
    k0 = pl.multiple_of(jnp.clip(q0 - HALF_SPAN, 0, L - W), HALF_SPAN)
    qpos = q0 + (lax.broadcasted_iota(jnp.int32, (2 * TQ, W), 0) & (TQ - 1))
    kpos = k0 + lax.broadcasted_iota(jnp.int32, (2 * TQ, W), 1)
    valid = jnp.abs(qpos - kpos) <= HALF_SPAN
    return W, q0, k0, valid


def _stack_heads(t, lo):
    z = jnp.zeros_like(t)
    return jnp.concatenate([jnp.where(lo, t, z), jnp.where(lo, z, t)], axis=0)


def _unstack_heads(t2, lo):
    return jnp.where(lo, t2[0:TQ], t2[TQ:2 * TQ])


CHAINS = 8


def _interleave(d):
    ru = min(d, CHAINS)
    return ru, min(CHAINS // ru, S // d // TQ)


def _for_blocks(n, fn):
    if n == 1:
        fn(0)
    else:
        def it(j, _):
            fn(j)
            return 0
        lax.fori_loop(0, n, it, 0)


def attn_fwd(proj, tabs, qw2, kw2, sides=()):
    CH = 256

    def body(q_ref, k_ref, v_ref, c_ref, s1_ref, s2_ref, qw_ref, kw_ref, at_ref, ls_ref,
             qs, ks, vs, osub, lsub, onat, lnat, qn, kn):
        g = pl.program_id(1)
        lo = lax.broadcasted_iota(jnp.int32, (1, LANES), 1) < HD
        e = _head_mat()

        def prep(i, _):
            rows = pl.ds(pl.multiple_of(i * CH, CH), CH)
            c, s1, s2 = c_ref[rows, :], s1_ref[rows, :], s2_ref[rows, :]
            for t_ref, w_ref, out, scale in ((q_ref, qw_ref, qn, HD ** -0.5), (k_ref, kw_ref, kn, 1.0)):
                t = t_ref[rows, :]
                r = lax.rsqrt(_head_mean(t * t, e) + EPS)
                out[rows, :] = _rope(t * r * w_ref[...], c, s1, s2) * scale
            return 0

        lax.fori_loop(0, S // CH, prep, 0, unroll=4)

        def group(gi, d):
            L = S // d

            ru, nb = _interleave(d)

            def stage(r, off):
                for c0 in range(0, L, CH):
                    n = min(CH, L)
                    rows = _sub_rows(r, d, c0, n)
                    dst = pl.ds(off + c0, n)
                    qs[dst, :] = qn[rows, :].astype(BF16)
                    ks[dst, :] = kn[rows, :].astype(BF16)
                    vs[dst, :] = v_ref[rows, :].astype(BF16)

            def one(off, i):
                W, q0, k0, valid = _band_window(i, L)
                q2 = _stack_heads(qs[pl.ds(off + q0, TQ), :], lo)
                sc = jnp.where(valid, _dot_nt(q2, ks[pl.ds(off + k0, W), :]), NEG_INF)
                m = jnp.max(sc, axis=-1, keepdims=True)
                p = jnp.exp(sc - m)
                den = jnp.sum(p, axis=-1, keepdims=True)
                o2 = _dot(p.astype(BF16), vs[pl.ds(off + k0, W), :]) / den
                l2 = jnp.broadcast_to(m + jnp.log(den), (2 * TQ, LANES))
                osub[pl.ds(off + q0, TQ), :] = _unstack_heads(o2, lo)
                lsub[pl.ds(off + q0, TQ), :] = _unstack_heads(l2, lo)

            def unstage(r, off):
                for c0 in range(0, L, CH):
                    n = min(CH, L)
                    rows = _sub_rows(r, d, c0, n)
                    onat[gi, rows, :] = osub[pl.ds(off + c0, n), :]
                    lnat[gi, rows, :] = lsub[pl.ds(off + c0, n), :]

            def step(t, _):
                for u in range(ru):
                    stage(t * ru + u, u * L)
                _for_blocks(L // TQ // nb, lambda j: [one(u * L, j * nb + b) for u in range(ru) for b in range(nb)])
                for u in range(ru):
                    unstage(t * ru + u, u * L)
                return 0

            lax.fori_loop(0, d // ru, step, 0)

        for gi, d in enumerate(DILATIONS):
            pl.when(g == gi)(functools.partial(group, gi, d))

        @pl.when(g == len(DILATIONS) - 1)
        def _():
            def mix(i, _):
                rows = pl.ds(pl.multiple_of(i * CH, CH), CH)
                l0, l1, l2 = lnat[0, rows, :], lnat[1, rows, :], lnat[2, rows, :]
                m = jnp.maximum(jnp.maximum(l0, l1), l2)
                e0, e1, e2 = jnp.exp(l0 - m), jnp.exp(l1 - m), jnp.exp(l2 - m)
                den = e0 + e1 + e2
                a = (e0 * onat[0, rows, :] + e1 * onat[1, rows, :] + e2 * onat[2, rows, :]) / den
                at_ref[rows, :] = a.astype(BF16)
                ls_ref[rows, :] = m + jnp.log(den)
                return 0

            lax.fori_loop(0, S // CH, mix, 0)

    out_spec = pl.BlockSpec((S, LANES), lambda hp, g: (0, hp))
    return _call(
        body, sides, name="attn_fwd", grid=(4, 3),
        in_specs=_qk_specs() + _tab_specs() + [_vec_spec(), _vec_spec()],
        out_specs=[out_spec, out_spec],
        out_shape=[jax.ShapeDtypeStruct((S, CC), BF16), jax.ShapeDtypeStruct((S, CC), F32)],
        scratch_shapes=[pltpu.VMEM((S, LANES), BF16)] * 3 + [pltpu.VMEM((S, LANES), F32)] * 2
        + [pltpu.VMEM((3, S, LANES), F32)] * 2 + [pltpu.VMEM((S, LANES), F32)] * 2,
        args=(proj, proj, proj, *tabs, qw2, kw2))


def attn_bwd(proj, tabs, qw2, kw2, d_attn, attn, lse, sides=()):
    CH = 256

    def body(q_ref, k_ref, v_ref, c_ref, s1_ref, s2_ref, qw_ref, kw_ref, do_ref, at_ref, ls_ref,
             dq_ref, dk_ref, dv_ref, gqw_ref, gkw_ref,
             qs, ks, vs, dos, dsub, lsub, dqs, dks, dvs, dnat, qx, kx, dvn, tnq, tnk, rrq, rrk):
        hp, g = pl.program_id(0), pl.program_id(1)
        lo = lax.broadcasted_iota(jnp.int32, (1, LANES), 1) < HD
        e = _head_mat()
        both = ((q_ref, qw_ref, qx, tnq, rrq, HD ** -0.5), (k_ref, kw_ref, kx, tnk, rrk, 1.0))

        @pl.when((hp == 0) & (g == 0))
        def _():
            gqw_ref[...] = jnp.zeros_like(gqw_ref)
            gkw_ref[...] = jnp.zeros_like(gkw_ref)

        def prep(i, _):
            rows = pl.ds(pl.multiple_of(i * CH, CH), CH)
            dnat[rows, :] = _head_mean(do_ref[rows, :] * at_ref[rows, :].astype(F32), e) * float(HD)
            c, s1, s2 = c_ref[rows, :], s1_ref[rows, :], s2_ref[rows, :]
            for t_ref, w_ref, x, tn_s, rr_s, scale in both:
                t = t_ref[rows, :]
                rr = lax.rsqrt(_head_mean(t * t, e) + EPS)
                tn = t * rr
                rr_s[rows, :] = rr
                tn_s[rows, :] = tn
                x[rows, :] = _rope(tn * w_ref[...], c, s1, s2) * scale
            return 0

        lax.fori_loop(0, S // CH, prep, 0, unroll=4)

        def group(d):
            L = S // d

            ru, nb = _interleave(d)

            def stage(r, off):
                for c0 in range(0, L, CH):
                    n = min(CH, L)
                    rows = _sub_rows(r, d, c0, n)
                    dst = pl.ds(off + c0, n)
                    qs[dst, :] = qx[rows, :].astype(BF16)
                    ks[dst, :] = kx[rows, :].astype(BF16)
                    vs[dst, :] = v_ref[rows, :].astype(BF16)
                    dos[dst, :] = do_ref[rows, :].astype(BF16)
                    dsub[dst, :] = dnat[rows, :]
                    lsub[dst, :] = ls_ref[rows, :]
                    dks[dst, :] = jnp.zeros((n, LANES), F32)
                    dvs[dst, :] = jnp.zeros((n, LANES), F32)

            def one(off, i):
                W, q0, k0, valid = _band_window(i, L)
                qrows, krows = pl.ds(off + q0, TQ), pl.ds(off + k0, W)
                q2 = _stack_heads(qs[qrows, :], lo)
                do2 = _stack_heads(dos[qrows, :], lo)
                kk, vv = ks[krows, :], vs[krows, :]
                lse_b, dd_b = lsub[qrows, :], dsub[qrows, :]
                lse2 = jnp.concatenate([lse_b[:, 0:1], lse_b[:, HD:HD + 1]], axis=0)
                dd2 = jnp.concatenate([dd_b[:, 0:1], dd_b[:, HD:HD + 1]], axis=0)
                sc = jnp.where(valid, _dot_nt(q2, kk), NEG_INF)
                p = jnp.exp(sc - lse2)
                ds = (p * (_dot_nt(do2, vv) - dd2)).astype(BF16)
                dqs[qrows, :] = _unstack_heads(_dot(ds, kk), lo)
                dks[krows, :] = dks[krows, :] + _dot_tn(ds, q2)
                dvs[krows, :] = dvs[krows, :] + _dot_tn(p.astype(BF16), do2)

            def unstage(r, off):
                for c0 in range(0, L, CH):
                    n = min(CH, L)
                    rows = _sub_rows(r, d, c0, n)
                    src = pl.ds(off + c0, n)
                    qx[rows, :] = dqs[src, :]
                    kx[rows, :] = dks[src, :]
                    dvn[rows, :] = dvs[src, :]

            def step(t, _):
                for u in range(ru):
                    stage(t * ru + u, u * L)
                _for_blocks(L // TQ // nb, lambda j: [one(u * L, j * nb + b) for u in range(ru) for b in range(nb)])
                for u in range(ru):
                    unstage(t * ru + u, u * L)
                return 0

            lax.fori_loop(0, d // ru, step, 0)

        for gi, d in enumerate(DILATIONS):
            pl.when(g == gi)(functools.partial(group, d))

        def emit(i, _):
            rows = pl.ds(pl.multiple_of(i * CH, CH), CH)
            c, s1, s2 = c_ref[rows, :], s1_ref[rows, :], s2_ref[rows, :]
            for (_, w_ref, x, tn_s, rr_s, scale), out, gw_ref in zip(both, (dq_ref, dk_ref), (gqw_ref, gkw_ref)):
                tn = tn_s[rows, :]
                dy = _rope_t(x[rows, :] * scale, c, s1, s2)
                gw_ref[0:1, :] = gw_ref[0:1, :] + jnp.sum(dy * tn, axis=0, keepdims=True)
                dtn = dy * w_ref[...]
                out[rows, :] = (rr_s[rows, :] * (dtn - tn * _head_mean(dtn * tn, e))).astype(BF16)
            dv_ref[rows, :] = dvn[rows, :].astype(BF16)
            return 0

        lax.fori_loop(0, S // CH, emit, 0, unroll=4)

    nat_spec = pl.BlockSpec((S, LANES), lambda hp, g: (0, hp))
    out_spec = pl.BlockSpec((None, S, LANES), lambda hp, g: (g, 0, hp))
    acc_spec = pl.BlockSpec((8, LANES), lambda hp, g: (0, 0))
    return _call(
        body, sides, name="attn_bwd", grid=(4, 3),
        in_specs=_qk_specs() + _tab_specs() + [_vec_spec(), _vec_spec(), nat_spec, nat_spec, nat_spec],
        out_specs=[out_spec] * 3 + [acc_spec] * 2,
        out_shape=[jax.ShapeDtypeStruct((QKV // PLANE, S, PLANE), BF16)] * 3 + [jax.ShapeDtypeStruct((8, LANES), F32)] * 2,
        scratch_shapes=[pltpu.VMEM((S, LANES), BF16)] * 4 + [pltpu.VMEM((S, LANES), F32)] * 13,
        args=(proj, proj, proj, *tabs, qw2, kw2, d_attn, attn, lse))


PADR = 16
CT = 128


def _conv_specs():
    return [pl.BlockSpec((S, CC), lambda i: (0, OFF_CA // CC)), pl.BlockSpec((S, CC), lambda i: (0, OFF_CB // CC))]


NCB = CC // LANES


def _pad_zero(pad):
    for cb in range(NCB):
        pad[cb, 0:PADR, :] = jnp.zeros((PADR, LANES), F32)
        pad[cb, PADR + S:PADR + S + PADR, :] = jnp.zeros((PADR, LANES), F32)


def _pad_store(pad, row0, n, val):
    for cb in range(NCB):
        pad[cb, pl.ds(pl.multiple_of(row0 + PADR, 8), n), :] = val[:, cb * LANES:(cb + 1) * LANES]


def _taps(pad_ref, cb, s0, weights):
    acc = jnp.zeros((CT, LANES), F32)
    for k in range(KW):
        acc = acc + weights[k] * pad_ref[cb, pl.ds(s0 + k + 1, CT), :]
    return acc


def conv_fwd(proj, conv_w, conv_b, ln_w, ln_b, sides=()):
    def body(a_ref, b_ref, w_ref, cb_ref, lw_ref, lb_ref, c_ref, u3_ref, upad):
        _pad_zero(upad)

        def glu(i, _):
            rows = pl.ds(pl.multiple_of(i * TM, TM), TM)
            _pad_store(upad, i * TM, TM, a_ref[rows, :] * _sigmoid(b_ref[rows, :]))
            return 0

        lax.fori_loop(0, S // TM, glu, 0)

        def chunk(i, _):
            s0 = pl.multiple_of(i * CT, CT)
            for cb in range(CC // LANES):
                cols = slice(cb * LANES, (cb + 1) * LANES)
                w = [w_ref[k:k + 1, cols] for k in range(KW)]
                c_ref[pl.ds(s0, CT), cols] = _taps(upad, cb, s0, w) + cb_ref[:, cols]
            cv = c_ref[pl.ds(s0, CT), :]
            mu = jnp.mean(cv, axis=-1, keepdims=True)
            xc = cv - mu
            rstd = lax.rsqrt(jnp.mean(xc * xc, axis=-1, keepdims=True) + EPS)
            yl = xc * rstd * lw_ref[...] + lb_ref[...]
            u3_ref[pl.ds(s0, CT), :] = (yl * _sigmoid(yl)).astype(BF16)
            return 0

        lax.fori_loop(0, S // CT, chunk, 0)

    vec = pl.BlockSpec((1, CC), lambda i: (0, 0))
    full = pl.BlockSpec((S, CC), lambda i: (0, 0))
    return _call(
        body, sides, name="conv_fwd", grid=(1,),
        in_specs=_conv_specs() + [pl.BlockSpec((KW, CC), lambda i: (0, 0)), vec, vec, vec],
        out_specs=[full, full],
        out_shape=[jax.ShapeDtypeStruct((S, CC), F32), jax.ShapeDtypeStruct((S, CC), BF16)],
        scratch_shapes=[pltpu.VMEM((NCB, S + 2 * PADR, LANES), F32)],
        args=(proj, proj, conv_w, conv_b, ln_w, ln_b))


def conv_bwd(proj, cpre, d_u3, conv_w, conv_w_rev, ln_w, ln_b, sides=()):
    def body(a_ref, b_ref, c_ref, du3_ref, w_ref, wr_ref, lw_ref, lb_ref,
             dc_ref, gw_ref, gcb_ref, glw_ref, glb_ref, upad, dpad):
        _pad_zero(upad)
        _pad_zero(dpad)
        gw_ref[...] = jnp.zeros_like(gw_ref)

        def ln_bwd(i, carry):
            gcb, glw, glb = carry
            rows = pl.ds(pl.multiple_of(i * TM, TM), TM)
            _pad_store(upad, i * TM, TM, a_ref[rows, :] * _sigmoid(b_ref[rows, :]))
            cv = c_ref[rows, :]
            mu = jnp.mean(cv, axis=-1, keepdims=True)
            xc = cv - mu
            rstd = lax.rsqrt(jnp.mean(xc * xc, axis=-1, keepdims=True) + EPS)
            xh = xc * rstd
            yl = xh * lw_ref[...] + lb_ref[...]
            dyl = du3_ref[rows, :] * _dsilu(yl, _sigmoid(yl))
            dxh = dyl * lw_ref[...]
            dcv = rstd * (dxh - jnp.mean(dxh, axis=-1, keepdims=True)
                          - xh * jnp.mean(dxh * xh, axis=-1, keepdims=True))
            _pad_store(dpad, i * TM, TM, dcv)
            return (gcb + jnp.sum(dcv, axis=0, keepdims=True),
                    glw + jnp.sum(dyl * xh, axis=0, keepdims=True),
                    glb + jnp.sum(dyl, axis=0, keepdims=True))

        z = jnp.zeros((1, CC), F32)
        gcb, glw, glb = lax.fori_loop(0, S // TM, ln_bwd, (z, z, z))
        gcb_ref[...] = gcb
        glw_ref[...] = glw
        glb_ref[...] = glb

        def chunk(i, _):
            s0 = pl.multiple_of(i * CT, CT)
            for cb in range(CC // LANES):
                cols = slice(cb * LANES, (cb + 1) * LANES)
                wr = [wr_ref[k:k + 1, cols] for k in range(KW)]
                du = _taps(dpad, cb, s0, wr)
                dcv = dpad[cb, pl.ds(s0 + PADR, CT), :]
                for k in range(KW):
                    gw_ref[k:k + 1, cols] = gw_ref[k:k + 1, cols] + jnp.sum(
                        upad[cb, pl.ds(s0 + k + 1, CT), :] * dcv, axis=0, keepdims=True)
                av = a_ref[pl.ds(s0, CT), cols]
                sb = _sigmoid(b_ref[pl.ds(s0, CT), cols])
                dc_ref[0, pl.ds(s0, CT), cols] = (du * sb).astype(BF16)
                dc_ref[1, pl.ds(s0, CT), cols] = (du * av * sb * (1.0 - sb)).astype(BF16)
            return 0

        lax.fori_loop(0, S // CT, chunk, 0)

    vec = pl.BlockSpec((1, CC), lambda i: (0, 0))
    full = pl.BlockSpec((S, CC), lambda i: (0, 0))
    wsp = pl.BlockSpec((KW, CC), lambda i: (0, 0))
    return _call(
        body, sides, name="conv_bwd", grid=(1,),
        in_specs=_conv_specs() + [full, full, wsp, wsp, vec, vec],
        out_specs=[pl.BlockSpec((2, S, CC), lambda i: (0, 0, 0)), wsp, vec, vec, vec],
        out_shape=[jax.ShapeDtypeStruct((2, S, CC), BF16), jax.ShapeDtypeStruct((KW, CC), F32)]
        + [jax.ShapeDtypeStruct((1, CC), F32)] * 3,
        scratch_shapes=[pltpu.VMEM((NCB, S + 2 * PADR, LANES), F32)] * 2,
        args=(proj, proj, cpre, d_u3, conv_w, conv_w_rev, ln_w, ln_b))


def _gate_specs():
    return [_row(CC, col=OFF_GA // CC + j) for j in range(4)]


def _gates(g_refs, bg_ref):
    ga = _sigmoid(jnp.concatenate([g_refs[0][...], g_refs[1][...]], axis=1) + bg_ref[0:1, :])
    gb = _sigmoid(jnp.concatenate([g_refs[2][...], g_refs[3][...]], axis=1) + bg_ref[1:2, :])
    return ga, gb


def mix_out(x, proj, b_gate, attn, u3, w_o, w_pw, w_out):
    def body(x_ref, g0, g1, g2, g3, bg_ref, at_ref, u3_ref, wo_ref, wp_ref, wout_ref,
             x1_ref, z_ref, ya_ref, yb_ref):
        ga, gb = _gates((g0, g1, g2, g3), bg_ref)
        ya = _dot_nt(at_ref[...], wo_ref[...])
        yb = _dot_nt(u3_ref[...], wp_ref[...])
        z = (ga * ya + gb * yb).astype(BF16)
        ya_ref[...] = ya.astype(BF16)
        yb_ref[...] = yb.astype(BF16)
        z_ref[...] = z
        x1_ref[...] = x_ref[...] + _dot(z, wout_ref[...])

    return pl.pallas_call(
        body, name="mix_out", grid=(S // TM,),
        in_specs=[_row(D)] + _gate_specs() + [_res((2, D)), _row(CC), _row(CC),
                                              _res((D, CC)), _res((D, CC)), _res((D, D))],
        out_specs=[_row(D)] * 4,
        out_shape=[jax.ShapeDtypeStruct((S, D), F32)] + [jax.ShapeDtypeStruct((S, D), BF16)] * 3,
        compiler_params=_cp(dimension_semantics=("arbitrary",)),
    )(x, proj, proj, proj, proj, b_gate, attn, u3, w_o, w_pw, w_out)


def out_bwd(d_x1b, proj, b_gate, ya, yb, w_o, w_pw, w_out, sides=()):
    def body(dx_ref, g0, g1, g2, g3, bg_ref, ya_ref, yb_ref, wo_ref, wp_ref, wout_ref,
             dya_ref, dyb_ref, dgl_ref, dat_ref, du3_ref, gbg_ref):
        @pl.when(pl.program_id(0) == 0)
        def _():
            gbg_ref[...] = jnp.zeros_like(gbg_ref)

        ga, gb = _gates((g0, g1, g2, g3), bg_ref)
        dz = _dot_nt(dx_ref[...], wout_ref[...])
        dya = (dz * ga).astype(BF16)
        dyb = (dz * gb).astype(BF16)
        dgla = dz * ya_ref[...].astype(F32) * ga * (1.0 - ga)
        dglb = dz * yb_ref[...].astype(F32) * gb * (1.0 - gb)
        dya_ref[...] = dya
        dyb_ref[...] = dyb
        for j in range(2):
            dgl_ref[j] = dgla[:, j * PLANE:(j + 1) * PLANE].astype(BF16)
            dgl_ref[2 + j] = dglb[:, j * PLANE:(j + 1) * PLANE].astype(BF16)
        gbg_ref[0:1, :] = gbg_ref[0:1, :] + jnp.sum(dgla, axis=0, keepdims=True)
        gbg_ref[1:2, :] = gbg_ref[1:2, :] + jnp.sum(dglb, axis=0, keepdims=True)
        dat_ref[...] = _dot(dya, wo_ref[...])
        du3_ref[...] = _dot(dyb, wp_ref[...])

    return _call(
        body, sides, name="out_bwd", grid=(S // TM,),
        in_specs=[_row(D)] + _gate_specs() + [_res((2, D)), _row(D), _row(D),
                                              _res((D, CC)), _res((D, CC)), _res((D, D))],
        out_specs=[_row(D), _row(D), _planes(2 * D), _row(CC), _row(CC), pl.BlockSpec((2, D), lambda i: (0, 0))],
        out_shape=[jax.ShapeDtypeStruct((S, D), BF16)] * 2 + [jax.ShapeDtypeStruct((2 * D // PLANE, S, PLANE), BF16)]
        + [jax.ShapeDtypeStruct((S, CC), F32)] * 2 + [jax.ShapeDtypeStruct((2, D), F32)],
        args=(d_x1b, proj, proj, proj, proj, b_gate, ya, yb, w_o, w_pw, w_out))


def ffn_in(x1, norm_w, w_ffn_in, sides=()):
    half = FF // 2

    def body(x_ref, nw_ref, w_ref, h_ref, gu_ref, f_ref):
        xv = x_ref[...]
        r = lax.rsqrt(jnp.mean(xv * xv, axis=-1, keepdims=True) + EPS)
        h = (xv * r * nw_ref[...]).astype(BF16)
        h_ref[...] = h
        for j in range(2):
            gt = _dot_nt(h, w_ref[j * half:(j + 1) * half, :])
            up = _dot_nt(h, w_ref[FF + j * half:FF + (j + 1) * half, :])
            gu_ref[:, j * half:(j + 1) * half] = gt.astype(BF16)
            gu_ref[:, FF + j * half:FF + (j + 1) * half] = up.astype(BF16)
            f_ref[:, j * half:(j + 1) * half] = (gt * _sigmoid(gt) * up).astype(BF16)

    return _call(
        body, sides, name="ffn_in", grid=(S // TM,),
        in_specs=[_row(D), _res((1, D)), _res((2 * FF, D))],
        out_specs=[_row(D), _row(2 * FF), _row(FF)],
        out_shape=[jax.ShapeDtypeStruct((S, D), BF16), jax.ShapeDtypeStruct((S, 2 * FF), BF16),
                   jax.ShapeDtypeStruct((S, FF), BF16)],
        args=(x1, norm_w, w_ffn_in))


def ffn_out_loss(x1, f, w_ffn_out, target):
    def body(x_ref, f_ref, w_ref, t_ref, dy_ref, dyb_ref, sq_ref):
        @pl.when(pl.program_id(0) == 0)
        def _():
            sq_ref[...] = jnp.zeros_like(sq_ref)

        diff = x_ref[...] + _dot(f_ref[...], w_ref[...]) - t_ref[...]
        dy = diff * (1.0 / D)
        dy_ref[...] = dy
        dyb_ref[...] = dy.astype(BF16)
        sq_ref[...] = sq_ref[...] + jnp.sum((diff * diff).reshape(TM // 8, 8, D), axis=0)

    return pl.pallas_call(
        body, name="ffn_out_loss", grid=(S // TM,),
        in_specs=[_row(D), _row(FF), _res((FF, D)), _row(D)],
        out_specs=[_row(D), _row(D), pl.BlockSpec((8, D), lambda i: (0, 0))],
        out_shape=[jax.ShapeDtypeStruct((S, D), F32), jax.ShapeDtypeStruct((S, D), BF16),
                   jax.ShapeDtypeStruct((8, D), F32)],
        compiler_params=_cp(dimension_semantics=("arbitrary",)),
    )(x1, f, w_ffn_out, target)


def _rms_bwd(xv, nw, dh):
    r = lax.rsqrt(jnp.mean(xv * xv, axis=-1, keepdims=True) + EPS)
    xn = xv * r
    dxn = dh * nw
    dx = r * (dxn - xn * jnp.mean(dxn * xn, axis=-1, keepdims=True))
    return dx, dh * xn


def ffn_bwd(dy, dyb, gu, x1, norm_w, w_ffn_in, w_ffn_out, sides=()):
    def body(dy_ref, dyb_ref, gu_ref, x_ref, nw_ref, wi_ref, wo_ref, dgu_ref, dx_ref, dxb_ref, gn_ref):
        @pl.when(pl.program_id(0) == 0)
        def _():
            gn_ref[...] = jnp.zeros_like(gn_ref)

        df = _dot_nt(dyb_ref[...], wo_ref[...])
        gt = gu_ref[:, 0:FF].astype(F32)
        up = gu_ref[:, FF:2 * FF].astype(F32)
        sg = _sigmoid(gt)
        dgt = (df * up * _dsilu(gt, sg)).astype(BF16)
        dup = (df * gt * sg).astype(BF16)
        dgu_ref[:, 0:FF] = dgt
        dgu_ref[:, FF:2 * FF] = dup
        dh = _dot(dgt, wi_ref[0:FF, :]) + _dot(dup, wi_ref[FF:2 * FF, :])
        dxn, gw = _rms_bwd(x_ref[...], nw_ref[...], dh)
        dx = dy_ref[...] + dxn
        dx_ref[...] = dx
        dxb_ref[...] = dx.astype(BF16)
        gn_ref[...] = gn_ref[...] + jnp.sum(gw, axis=0, keepdims=True)

    return _call(
        body, sides, name="ffn_bwd", grid=(S // TM,),
        in_specs=[_row(D), _row(D), _row(2 * FF), _row(D), _res((1, D)), _res((2 * FF, D)), _res((FF, D))],
        out_specs=[_row(2 * FF), _row(D), _row(D), pl.BlockSpec((1, D), lambda i: (0, 0))],
        out_shape=[jax.ShapeDtypeStruct((S, 2 * FF), BF16), jax.ShapeDtypeStruct((S, D), F32),
                   jax.ShapeDtypeStruct((S, D), BF16), jax.ShapeDtypeStruct((1, D), F32)],
        args=(dy, dyb, gu, x1, norm_w, w_ffn_in, w_ffn_out))


def in_bwd(d_q, d_k, d_v, d_conv, d_gl, w_in, x, d_x1, norm_w, sides=()):
    segs = ((OFF_Q, QKV), (OFF_K, QKV), (OFF_V, QKV), (OFF_CA, 2 * CC), (OFF_GA, 2 * D))

    def body(dq_ref, dk_ref, dv_ref, dc_ref, dg_ref, w_ref, x_ref, dx1_ref, nw_ref, gx_ref, gn_ref):
        @pl.when(pl.program_id(0) == 0)
        def _():
            gn_ref[...] = jnp.zeros_like(gn_ref)

        dh = jnp.zeros((TM, D), F32)
        for ref, (off, width) in zip((dq_ref, dk_ref, dv_ref, dc_ref, dg_ref), segs):
            for j in range(width // PLANE):
                dh = dh + _dot(ref[j], w_ref[off + j * PLANE:off + (j + 1) * PLANE, :])
        dxn, gw = _rms_bwd(x_ref[...], nw_ref[...], dh)
        gx_ref[...] = dx1_ref[...] + dxn
        gn_ref[...] = gn_ref[...] + jnp.sum(gw, axis=0, keepdims=True)

    return _call(
        body, sides, name="in_bwd", grid=(S // TM,),
        in_specs=[_planes(QKV)] * 3 + [_planes(2 * CC), _planes(2 * D), _res((INW, D)), _row(D), _row(D), _res((1, D))],
        out_specs=[_row(D), pl.BlockSpec((1, D), lambda i: (0, 0))],
        out_shape=[jax.ShapeDtypeStruct((S, D), F32), jax.ShapeDtypeStruct((1, D), F32)],
        args=(d_q, d_k, d_v, d_conv, d_gl, w_in, x, d_x1, norm_w))


def mm_tn(name, a, b, tm, tn):
    M, N = a.shape[1], b.shape[1]

    def body(a_ref, b_ref, o_ref, ob_ref):
        r = _dot_tn(a_ref[...], b_ref[...])
        o_ref[...] = r
        ob_ref[...] = r.astype(BF16)

    return _call(
        body, name=name, grid=(M // tm, N // tn),
        in_specs=[pl.BlockSpec((S, tm), lambda i, j: (0, i)), pl.BlockSpec((S, tn), lambda i, j: (0, j))],
        out_specs=[pl.BlockSpec((tm, tn), lambda i, j: (i, j))] * 2,
        out_shape=[jax.ShapeDtypeStruct((M, N), F32), jax.ShapeDtypeStruct((M, N), BF16)],
        args=(a, b))


GW_IN_TN = PLANE
GW_IN_SPLIT = (768, 256)


def gw_in_t(name, h, d_segs, col0, hw, sides=()):
    tn = GW_IN_TN
    starts, t0 = [], 0
    for seg in d_segs:
        starts.append(t0)
        t0 += seg.shape[0]
    ntiles = [seg.shape[0] for seg in d_segs]

    def body(h_ref, *refs):
        a_refs, o_ref, ob_ref = refs[:-2], refs[-2], refs[-1]
        n = pl.program_id(0)
        for a_ref, st, nt in zip(a_refs, starts, ntiles):
            @pl.when((n >= st) & (n < st + nt))
            def _(a_ref=a_ref):
                r = _dot_tn(a_ref[...], h_ref[...])
                o_ref[...] = r
                ob_ref[...] = r.astype(BF16)

    def seg_spec(st, nt):
        return pl.BlockSpec((None, S, tn), lambda n: (jnp.clip(n - st, 0, nt - 1), 0, 0))

    res = _call(
        body, sides, name=name, grid=(INW // tn,),
        in_specs=[pl.BlockSpec((S, hw), lambda n: (0, col0 // hw))] + [seg_spec(st, nt) for st, nt in zip(starts, ntiles)],
        out_specs=[pl.BlockSpec((tn, hw), lambda n: (n, 0))] * 2,
        out_shape=[jax.ShapeDtypeStruct((INW, hw), F32), jax.ShapeDtypeStruct((INW, hw), BF16)],
        args=(h, *d_segs))
    return (res[0], res[1]) if sides else (res, [])


def _place():
    x, y, c = lax.axis_index("x"), lax.axis_index("y"), lax.axis_index("c")
    chips = [(1 - x, y), (x, 1 - y), (1 - x, 1 - y)]
    return x, y, c, chips


def _sems(n):
    return pltpu.SemaphoreType.DMA((n,))


def _remote(src, dst, send, recv, k, to):
    return pltpu.make_async_remote_copy(src_ref=src, dst_ref=dst, send_sem=send.at[k], recv_sem=recv.at[k],
                                        device_id=to, device_id_type=MESH)


def _cast_rows(dst, src, cols=slice(None)):
    rows = src.shape[0]
    step = next((s for s in (128, 64, 32, 16) if rows % s == 0), rows)
    for r0 in range(0, rows, step):
        dst[r0:r0 + step, cols] = src[r0:r0 + step, :].astype(dst.dtype)


def comm_only(name, sides):
    def body():
        pass

    return _call(body, sides, name=name, grid=(1,), in_specs=[], out_specs=[], out_shape=[], args=())[1]


def ag_blocks(shard, dtype):
    R, W = shard.shape

    def copy(outs, scr, k, block, to, src=None):
        dst = outs[0].at[block]
        return _remote(dst if src is None else src, dst, scr[1], scr[2], k, to)

    def local(outs, scr, me):
        return pltpu.make_async_copy(scr[0], outs[0].at[me], scr[3].at[0])

    def start(ins, outs, scr):
        x, y, c, chips = _place()
        me = 4 * x + 2 * y + c
        _cast_rows(scr[0], ins[0])
        local(outs, scr, me).start()
        copy(outs, scr, 0, me, (x, y, 1 - c), src=scr[0]).start()
        for j, (cx, cy) in enumerate(chips):
            copy(outs, scr, 1 + j, me, (cx, cy, c), src=scr[0]).start()

    def finish(ins, outs, scr):
        x, y, c, chips = _place()
        me, sib = 4 * x + 2 * y + c, (x, y, 1 - c)
        passed = []
        for j, (cx, cy) in enumerate(chips):
            theirs = 4 * cx + 2 * cy + c
            copy(outs, scr, 1 + j, theirs, (x, y, c)).wait_recv()
            fwd = copy(outs, scr, 4 + j, theirs, sib)
            fwd.start()
            passed.append(fwd)
        copy(outs, scr, 0, 4 * x + 2 * y + 1 - c, (x, y, c)).wait_recv()
        for j, (cx, cy) in enumerate(chips):
            copy(outs, scr, 4 + j, 4 * cx + 2 * cy + 1 - c, (x, y, c)).wait_recv()
        copy(outs, scr, 0, me, sib, src=scr[0]).wait_send()
        for j, (cx, cy) in enumerate(chips):
            copy(outs, scr, 1 + j, me, (cx, cy, c), src=scr[0]).wait_send()
        for fwd in passed:
            fwd.wait_send()
        local(outs, scr, me).wait()

    return Side((shard,), (VMEM,), (jax.ShapeDtypeStruct((NDEV, R, W), dtype),),
                (pltpu.VMEM((R, W), dtype), _sems(7), _sems(7), _sems(1)), start, finish, None, "dsxy")


def ag_blocks_relay(shard, dtype, transpose=False):
    R, W = shard.shape[::-1] if transpose else shard.shape
    half = R // 2

    def copy(outs, scr, k, block, to, src=None, rows=None):
        dst = outs[0].at[block] if rows is None else outs[0].at[block, pl.ds(rows * half, half), :]
        return _remote(dst if src is None else src, dst, scr[1], scr[2], k, to)

    def local(outs, scr, me):
        return pltpu.make_async_copy(scr[0], outs[0].at[me], scr[3].at[0])

    def own(outs, scr):
        x, y, c, _ = _place()
        me = 4 * x + 2 * y + c
        return [copy(outs, scr, k, me, to, src=scr[0])
                for k, to in enumerate([(x, y, 1 - c), (1 - x, y, c), (x, 1 - y, c)])]

    def start(ins, outs, scr):
        x, y, c, _ = _place()
        if transpose:
            scr[0][...] = ins[0][...].T.astype(dtype)
        else:
            _cast_rows(scr[0], ins[0])
        local(outs, scr, 4 * x + 2 * y + c).start()
        for cp in own(outs, scr):
            cp.start()

    def passed_on(outs, scr):
        x, y, c, _ = _place()
        sib, xn, yn = (x, y, 1 - c), (1 - x, y, c), (x, 1 - y, c)
        b_xn, b_yn, b_dg = 4 * (1 - x) + 2 * y + c, 4 * x + 2 * (1 - y) + c, 4 * (1 - x) + 2 * (1 - y) + c
        near = [copy(outs, scr, 5, b_xn, yn, rows=0), copy(outs, scr, 3, b_xn, sib),
                copy(outs, scr, 6, b_yn, xn, rows=1), copy(outs, scr, 4, b_yn, sib)]
        far = [copy(outs, scr, 7, b_dg, sib, rows=0), copy(outs, scr, 8, b_dg, sib, rows=1)]
        return (b_xn, b_yn, b_dg), near, far

    def mid(ins, outs, scr):
        x, y, c, _ = _place()
        (b_xn, b_yn, _), near, _ = passed_on(outs, scr)
        copy(outs, scr, 1, b_xn, (x, y, c)).wait_recv()
        near[0].start()
        near[1].start()
        copy(outs, scr, 2, b_yn, (x, y, c)).wait_recv()
        near[2].start()
        near[3].start()

    def finish(ins, outs, scr):
        x, y, c, _ = _place()
        here = (x, y, c)
        (b_xn, b_yn, b_dg), near, far = passed_on(outs, scr)
        copy(outs, scr, 5, b_dg, here, rows=0).wait_recv()
        far[0].start()
        copy(outs, scr, 6, b_dg, here, rows=1).wait_recv()
        far[1].start()
        flip = 1 - 2 * c
        copy(outs, scr, 0, 4 * x + 2 * y + 1 - c, here).wait_recv()
        copy(outs, scr, 3, b_xn + flip, here).wait_recv()
        copy(outs, scr, 4, b_yn + flip, here).wait_recv()
        copy(outs, scr, 7, b_dg + flip, here, rows=0).wait_recv()
        copy(outs, scr, 8, b_dg + flip, here, rows=1).wait_recv()
        for cp in own(outs, scr) + near + far:
            cp.wait_send()
        local(outs, scr, 4 * x + 2 * y + c).wait()

    return Side((shard,), (VMEM,), (jax.ShapeDtypeStruct((NDEV, R, W), dtype),),
                (pltpu.VMEM((R, W), dtype), _sems(9), _sems(9), _sems(1)), start, finish, mid, "sxy")


def copies_side(args, out_shape, n_copies, plan, peers):
    def copies(ins, outs, scr):
        return [_remote(s_, d_, scr[0], scr[1], i, to) for i, (s_, d_, to) in enumerate(plan(ins, outs))]

    def start(ins, outs, scr):
        for cp in copies(ins, outs, scr):
            cp.start()

    def finish(ins, outs, scr):
        for cp in copies(ins, outs, scr):
            cp.wait()

    return Side(tuple(args), (ANY,) * len(args), tuple(out_shape), (_sems(n_copies), _sems(n_copies)),
                start, finish, None, peers)


def rs_to_sibling(grads):
    out_shape = [jax.ShapeDtypeStruct((4,) + g.shape[1:], BF16) for g in grads]

    def plan(ins, outs):
        x, y, c, _ = _place()
        return [(g.at[2 * k + 1 - c], r.at[k], (x, y, 1 - c)) for g, r in zip(ins, outs) for k in range(4)]

    return copies_side(grads, out_shape, 4 * len(grads), plan, "s")


def rs_to_chips(parts):
    out_shape = [jax.ShapeDtypeStruct((3,) + p.shape[1:], BF16) for p in parts]

    def plan(ins, outs):
        x, y, c, chips = _place()
        return [(p.at[2 * cx + cy], r.at[j], (cx, cy, c))
                for p, r in zip(ins, outs) for j, (cx, cy) in enumerate(chips)]

    return copies_side(parts, out_shape, 3 * len(parts), plan, "dxy")


def rs_to_chips_combined(part):
    _, R, W = part.shape
    half = R // 2
    top, bot = pl.ds(0, half), pl.ds(half, half)

    def copies(ins, outs, scr):
        p, r = ins[0], outs[0]
        loc_a, loc_b, in_x, in_y, comb_a, comb_b, send, recv, loc = scr
        x, y, c, _ = _place()
        xn, yn = (1 - x, y, c), (x, 1 - y, c)
        k_xn, k_yn, k_dg = 2 * (1 - x) + y, 2 * x + 1 - y, 2 * (1 - x) + 1 - y
        direct = [_remote(p.at[k_xn, top, :], r.at[0, top, :], send, recv, 0, xn),
                  _remote(p.at[k_yn, bot, :], r.at[1, bot, :], send, recv, 1, yn),
                  _remote(p.at[k_dg, top, :], in_x, send, recv, 2, xn),
                  _remote(p.at[k_dg, bot, :], in_y, send, recv, 3, yn)]
        combined = [_remote(comb_a, r.at[1, top, :], send, recv, 4, yn),
                    _remote(comb_b, r.at[0, bot, :], send, recv, 5, xn)]
        local = [pltpu.make_async_copy(p.at[k_yn, top, :], loc_a, loc.at[0]),
                 pltpu.make_async_copy(p.at[k_xn, bot, :], loc_b, loc.at[1])]
        return direct, combined, local

    def start(ins, outs, scr):
        direct, _, local = copies(ins, outs, scr)
        for cp in local + direct:
            cp.start()

    def mid(ins, outs, scr):
        loc_a, loc_b, in_x, in_y, comb_a, comb_b = scr[:6]
        direct, combined, local = copies(ins, outs, scr)
        for mine, arrival, inbox, out, nxt in ((local[0], direct[2], in_x, comb_a, combined[0]),
                                               (local[1], direct[3], in_y, comb_b, combined[1])):
            mine.wait()
            arrival.wait_recv()
            src = loc_a if out is comb_a else loc_b
            out[...] = (src[...].astype(F32) + inbox[...].astype(F32)).astype(BF16)
            nxt.start()

    def finish(ins, outs, scr):
        direct, combined, _ = copies(ins, outs, scr)
        direct[0].wait_recv()
        direct[1].wait_recv()
        combined[0].wait_recv()
        combined[1].wait_recv()
        for cp in direct + combined:
            cp.wait_send()

    buf = pltpu.VMEM((half, W), BF16)
    return Side((part,), (ANY,), (jax.ShapeDtypeStruct((2, R, W), BF16),),
                (buf, buf, buf, buf, buf, buf, _sems(6), _sems(6), _sems(2)), start, finish, mid, "xy")


ADAM_TILE_BYTES = 3 * 512 * 1024


def _row_tiles(rows, width):
    return 2 if rows % 32 == 0 and rows * width * 4 > ADAM_TILE_BYTES else 1


def chip_sum(name, grad, recv, c_idx, chip_idx):
    _, R, C = grad.shape
    nt = 1
    tr = R // nt

    def body(s_ref, g_ref, r_ref, p_ref, own_ref):
        k = pl.program_id(1)
        tot = g_ref[0] + r_ref[0].astype(F32)
        p_ref[0] = tot.astype(BF16)

        @pl.when(k == s_ref[1])
        def _():
            own_ref[...] = tot

    grid_spec = pltpu.PrefetchScalarGridSpec(
        num_scalar_prefetch=1, grid=(nt, 4),
        in_specs=[pl.BlockSpec((1, tr, C), lambda i, k, s: (2 * k + s[0], i, 0)),
                  pl.BlockSpec((1, tr, C), lambda i, k, s: (k, i, 0))],
        out_specs=[pl.BlockSpec((1, tr, C), lambda i, k, s: (k, i, 0)),
                   pl.BlockSpec((tr, C), lambda i, k, s: (i, 0))])
    return pl.pallas_call(
        body, name=name, grid_spec=grid_spec,
        out_shape=[jax.ShapeDtypeStruct((4, R, C), BF16), jax.ShapeDtypeStruct((R, C), F32)],
        compiler_params=_cp(dimension_semantics=("arbitrary", "arbitrary")),
    )(jnp.stack([c_idx, chip_idx]), grad, recv)


def _adamw(w, g, m, v):
    m2 = ADAM_B1 * m + (1.0 - ADAM_B1) * g
    v2 = ADAM_B2 * v + (1.0 - ADAM_B2) * (g * g)
    m_hat = m2 / (1.0 - ADAM_B1 ** ADAM_STEP)
    v_hat = v2 / (1.0 - ADAM_B2 ** ADAM_STEP)
    delta = -ADAM_LR * (m_hat / (jnp.sqrt(v_hat) + ADAM_EPS) + ADAM_WD * w)
    return delta, m2, v2


def shard_adam(name, owns, recvs, w, m, v, io_t=False):
    n = len(owns)
    R = owns[0].shape[0]
    ct = min(o.shape[1] for o in owns)
    first = [sum(o.shape[1] for o in owns[:j]) // ct for j in range(n)]
    count = [o.shape[1] // ct for o in owns]
    nt = _row_tiles(R, ct)
    tr = R // nt

    def body(*refs):
        o_refs, r_refs = refs[:n], refs[n:2 * n]
        w_ref, m_ref, v_ref, g_ref, d_ref, nm_ref, nv_ref = refs[2 * n:]
        g = None
        for j in range(n):
            gj = o_refs[j][...]
            for q in range(recvs[j].shape[0]):
                gj = gj + r_refs[j][q].astype(F32)
            g = gj if g is None else jnp.where(pl.program_id(0) >= first[j], gj, g)
        t = (lambda a: a.T) if io_t else (lambda a: a)
        delta, m2, v2 = _adamw(t(w_ref[...]), g, t(m_ref[...]), t(v_ref[...]))
        g_ref[...] = t(g)
        d_ref[...] = t(delta)
        nm_ref[...] = t(m2)
        nv_ref[...] = t(v2)

    def part(j):
        return pl.BlockSpec((tr, ct), lambda k, i: (i, jnp.clip(k - first[j], 0, count[j] - 1)))

    def part3(j):
        return pl.BlockSpec((recvs[j].shape[0], tr, ct), lambda k, i: (0, i, jnp.clip(k - first[j], 0, count[j] - 1)))

    C = sum(count) * ct
    tile = pl.BlockSpec((ct, tr), lambda k, i: (k, i)) if io_t else pl.BlockSpec((tr, ct), lambda k, i: (i, k))
    return pl.pallas_call(
        body, name=name, grid=(sum(count), nt),
        in_specs=[part(j) for j in range(n)] + [part3(j) for j in range(n)] + [tile, tile, tile],
        out_specs=[tile] * 4, out_shape=[jax.ShapeDtypeStruct((C, R) if io_t else (R, C), F32)] * 4,
        compiler_params=_cp(dimension_semantics=("arbitrary", "arbitrary")),
    )(*owns, *recvs, w, m, v)


ROW_N1, ROW_N2, ROW_BG, ROW_QN, ROW_KN, ROW_CB, ROW_LW, ROW_LB, ROW_CW = 0, 1, 2, 4, 5, 6, 7, 8, 9
PACK_ROWS = 40
SMALL = ("norm1_w", "norm2_w", "b_gate", "q_norm_w", "k_norm_w", "conv_b", "conv_ln_w", "conv_ln_b", "conv_w")


def small_sync(g, sq, sides=()):
    ns = len(SMALL)

    def copies(refs):
        pack, recv, send_sems, recv_sems = refs[ns + 2:]
        x, y, c, _ = _place()
        return [pltpu.make_async_remote_copy(
            src_ref=pack, dst_ref=recv.at[4 * x + 2 * y + c], send_sem=send_sems.at[k - 1],
            recv_sem=recv_sems.at[k - 1], device_id=(x ^ (k >> 2), y ^ ((k >> 1) & 1), c ^ (k & 1)),
            device_id_type=MESH) for k in range(1, NDEV)]

    def body(*refs):
        gi = dict(zip(SMALL, refs[:ns]))
        sq_ref, tot, pack, recv, send_sems, recv_sems = refs[ns:]
        x, y, c, _ = _place()
        me = 4 * x + 2 * y + c

        pack[...] = jnp.zeros_like(pack)
        pack[ROW_KN:ROW_KN + 1, LANES:2 * LANES] = jnp.full((1, LANES), (0.5 / D) * jnp.sum(sq_ref[...]), F32)
        pack[ROW_N1:ROW_N1 + 1, :] = gi["norm1_w"][...]
        pack[ROW_N2:ROW_N2 + 1, :] = gi["norm2_w"][...]
        pack[ROW_BG:ROW_BG + 2, :] = gi["b_gate"][...]
        pack[ROW_QN:ROW_QN + 1, 0:HD] = gi["q_norm_w"][...]
        pack[ROW_KN:ROW_KN + 1, 0:HD] = gi["k_norm_w"][...]
        pack[ROW_CB:ROW_CB + 1, 0:CC] = gi["conv_b"][...]
        pack[ROW_LW:ROW_LW + 1, 0:CC] = gi["conv_ln_w"][...]
        pack[ROW_LB:ROW_LB + 1, 0:CC] = gi["conv_ln_b"][...]
        pack[ROW_CW:ROW_CW + KW, 0:CC] = gi["conv_w"][...]

        for cp in copies(refs):
            cp.start()
        recv[me] = pack[...]

    def tail(*refs):
        tot, recv = refs[ns + 1], refs[ns + 3]
        for cp in copies(refs):
            cp.wait()
        acc = recv[0]
        for p in range(1, NDEV):
            acc = acc + recv[p]
        tot[...] = acc

    args = [g[k] for k in SMALL] + [sq]
    res = _call(
        body, sides, name="small_sync", grid=(1,), in_specs=[VMEM] * len(args), out_specs=[VMEM],
        out_shape=[jax.ShapeDtypeStruct((PACK_ROWS, D), F32)],
        scratch_shapes=[pltpu.VMEM((PACK_ROWS, D), F32), pltpu.VMEM((NDEV, PACK_ROWS, D), F32),
                        _sems(NDEV - 1), _sems(NDEV - 1)],
        args=args, own_comm=True, tail=tail)
    return (res[0][0], res[1]) if sides else res[0]


def small_adam(tot, w, m, v, me):
    ns = len(SMALL)

    def body(me_ref, tot, *refs):
        wi = dict(zip(SMALL, refs[:ns]))
        mi = dict(zip(SMALL, refs[ns:2 * ns]))
        vi = dict(zip(SMALL, refs[2 * ns:3 * ns]))
        outs = refs[3 * ns:7 * ns]
        loss_ref = refs[7 * ns]
        me = me_ref[0]

        def shard_grad(name):
            if name == "b_gate":
                return tot[ROW_BG:ROW_BG + 2, pl.ds(pl.multiple_of(me * LANES, LANES), LANES)]
            if name == "conv_w":
                win = tot[ROW_CW:ROW_CW + KW, pl.ds(pl.multiple_of((me // 2) * LANES, LANES), LANES)]
                return jnp.where(me % 2 == 1, win[:, HD:LANES], win[:, 0:HD])
            row = {"norm1_w": ROW_N1, "norm2_w": ROW_N2, "q_norm_w": ROW_QN, "k_norm_w": ROW_KN,
                   "conv_b": ROW_CB, "conv_ln_w": ROW_LW, "conv_ln_b": ROW_LB}[name]
            return tot[row:row + 1, 0:wi[name].shape[1]]

        for i, name in enumerate(SMALL):
            gr = shard_grad(name)
            delta, m2, v2 = _adamw(wi[name][...], gr, mi[name][...], vi[name][...])
            outs[4 * i][...] = gr
            outs[4 * i + 1][...] = delta
            outs[4 * i + 2][...] = m2
            outs[4 * i + 3][...] = v2
        loss_ref[...] = tot[ROW_KN:ROW_KN + 1, LANES:2 * LANES]

    out_shape = []
    for name in SMALL:
        out_shape += [jax.ShapeDtypeStruct(w[name].shape, F32)] * 4
    out_shape.append(jax.ShapeDtypeStruct((1, LANES), F32))
    args = [tot] + [w[k] for k in SMALL] + [m[k] for k in SMALL] + [v[k] for k in SMALL]
    grid_spec = pltpu.PrefetchScalarGridSpec(
        num_scalar_prefetch=1, grid=(1,), in_specs=[VMEM] * len(args), out_specs=[VMEM] * len(out_shape))
    res = pl.pallas_call(body, name="small_adam", grid_spec=grid_spec, out_shape=out_shape)(me, *args)
    out = {name: tuple(res[4 * i:4 * i + 4]) for i, name in enumerate(SMALL)}
    return out, res[4 * ns][0, 0]


MATS = ("w_in", "w_o_attn", "w_pw_conv", "w_out", "w_ffn_in", "w_ffn_out")
TRANSPOSED = ("w_in", "w_ffn_in")
WEIGHTS = ("norm1_w", "w_in", "b_gate", "q_norm_w", "k_norm_w", "w_o_attn", "conv_w", "conv_b", "conv_ln_w",
           "conv_ln_b", "w_pw_conv", "w_out", "norm2_w", "w_ffn_in", "w_ffn_out")


def _blocks_to_cols(blocks):
    n, R, C = blocks.shape
    return blocks.transpose(1, 0, 2).reshape(R, n * C)


def kernel(x, positions, norm1_w, w_in, b_gate, q_norm_w, k_norm_w, w_o_attn, conv_w, conv_b, conv_ln_w, conv_ln_b, w_pw_conv, w_out, norm2_w, w_ffn_in, w_ffn_out, loss_target, m_norm1_w, m_w_in, m_b_gate, m_q_norm_w, m_k_norm_w, m_w_o_attn, m_conv_w, m_conv_b, m_conv_ln_w, m_conv_ln_b, m_w_pw_conv, m_w_out, m_norm2_w, m_w_ffn_in, m_w_ffn_out, v_norm1_w, v_w_in, v_b_gate, v_q_norm_w, v_k_norm_w, v_w_o_attn, v_conv_w, v_conv_b, v_conv_ln_w, v_conv_ln_b, v_w_pw_conv, v_w_out, v_norm2_w, v_w_ffn_in, v_w_ffn_out):
    w = dict(norm1_w=norm1_w, w_in=w_in, b_gate=b_gate, q_norm_w=q_norm_w, k_norm_w=k_norm_w, w_o_attn=w_o_attn,
             conv_w=conv_w, conv_b=conv_b, conv_ln_w=conv_ln_w, conv_ln_b=conv_ln_b, w_pw_conv=w_pw_conv,
             w_out=w_out, norm2_w=norm2_w, w_ffn_in=w_ffn_in, w_ffn_out=w_ffn_out)
    m = dict(norm1_w=m_norm1_w, w_in=m_w_in, b_gate=m_b_gate, q_norm_w=m_q_norm_w, k_norm_w=m_k_norm_w,
             w_o_attn=m_w_o_attn, conv_w=m_conv_w, conv_b=m_conv_b, conv_ln_w=m_conv_ln_w,
             conv_ln_b=m_conv_ln_b, w_pw_conv=m_w_pw_conv, w_out=m_w_out, norm2_w=m_norm2_w,
             w_ffn_in=m_w_ffn_in, w_ffn_out=m_w_ffn_out)
    v = dict(norm1_w=v_norm1_w, w_in=v_w_in, b_gate=v_b_gate, q_norm_w=v_q_norm_w, k_norm_w=v_k_norm_w,
             w_o_attn=v_w_o_attn, conv_w=v_conv_w, conv_b=v_conv_b, conv_ln_w=v_conv_ln_w,
             conv_ln_b=v_conv_ln_b, w_pw_conv=v_w_pw_conv, w_out=v_w_out, norm2_w=v_norm2_w,
             w_ffn_in=v_w_ffn_in, w_ffn_out=v_w_ffn_out)
    def two_d(t):
        t = {k: (a[0] if a.ndim == 3 else a) for k, a in t.items()}
        return {k: (a.T if k in TRANSPOSED else a) for k, a in t.items()}

    w, m, v = two_d(w), two_d(m), two_d(v)

    x2, target = x[0], loss_target[0]
    c_idx = lax.axis_index("c").astype(jnp.int32)
    chip_idx = (2 * lax.axis_index("x") + lax.axis_index("y")).astype(jnp.int32)
    qw2 = jnp.tile(w["q_norm_w"], (1, 2))
    kw2 = jnp.tile(w["k_norm_w"], (1, 2))

    ax, ay = lax.axis_index("x"), lax.axis_index("y")
    chip_order = jnp.stack([2 * ax + ay, 2 * (1 - ax) + ay, 2 * ax + 1 - ay, 2 * (1 - ax) + 1 - ay]).astype(jnp.int32)
    h, proj, w_in_blocks, tabs = in_proj_gather(x2, w["norm1_w"], w["w_in"], chip_order, positions.reshape(S, 1))
    w_in_t = w_in_blocks.reshape(INW, D)
    (attn, lse), ((w_ffn_in_blocks,), (w_out_blocks,), (w_o_blocks,), (w_pw_blocks,), (bg_blocks,), (cw_blocks,)) = attn_fwd(
        proj, tabs, qw2, kw2, sides=(ag_blocks_relay(w["w_ffn_in"], BF16), ag_blocks_relay(w["w_out"], BF16),
                                     ag_blocks_relay(w["w_o_attn"], BF16, transpose=True),
                                     ag_blocks_relay(w["w_pw_conv"], BF16, transpose=True),
                                     ag_blocks(w["b_gate"], F32), ag_blocks(w["conv_w"], F32)))
    w_ffn_in_t = w_ffn_in_blocks.reshape(2 * FF, D)
    w_out_f = w_out_blocks.reshape(D, D)
    w_o_t, w_pw_t = w_o_blocks.reshape(D, CC), w_pw_blocks.reshape(D, CC)
    b_gate_f, conv_w_f = _blocks_to_cols(bg_blocks), _blocks_to_cols(cw_blocks)
    cpre, u3 = conv_fwd(proj, conv_w_f, w["conv_b"], w["conv_ln_w"], w["conv_ln_b"])
    x1, z, ya, yb = mix_out(x2, proj, b_gate_f, attn, u3, w_o_t, w_pw_t, w_out_f)
    (h2, gu, f), ((w_ffn_out_blocks,),) = ffn_in(x1, w["norm2_w"], w_ffn_in_t, sides=(ag_blocks_relay(w["w_ffn_out"], BF16),))
    w_ffn_out_f = w_ffn_out_blocks.reshape(FF, D)
    dy, dyb, sq = ffn_out_loss(x1, f, w_ffn_out_f, target)

    g = {}
    def blocks(name, a, b, tm):
        return [t.reshape(NDEV, a.shape[1] // NDEV, b.shape[1]) for t in mm_tn(name, a, b, tm, b.shape[1])]

    g_ffn_out, gb_ffn_out = blocks("gw_ffn_out", f, dyb, FF // 2)
    (d_gu, d_x1, d_x1b, g["norm2_w"]), ((ra_ffn_out,),) = ffn_bwd(
        dy, dyb, gu, x1, w["norm2_w"], w_ffn_in_t, w_ffn_out_f, sides=(rs_to_sibling([gb_ffn_out]),))
    pb_ffn_out, own_ffn_out = chip_sum("chip_sum_w_ffn_out", g_ffn_out, ra_ffn_out, c_idx, chip_idx)
    g_ffn_in, gb_ffn_in = blocks("gw_ffn_in", d_gu, h2, FF // 2)
    g_out, gb_out = blocks("gw_out", z, d_x1b, D // 2)
    (d_ya, d_yb, d_gl, d_attn, d_u3, g["b_gate"]), ((ra_ffn_in,),) = out_bwd(
        d_x1b, proj, b_gate_f, ya, yb, w_o_t, w_pw_t, w_out_f, sides=(rs_to_sibling([gb_ffn_in]),))
    pb_ffn_in, own_ffn_in = chip_sum("chip_sum_w_ffn_in", g_ffn_in, ra_ffn_in, c_idx, chip_idx)
    g_w_o, gb_w_o = blocks("gw_o_attn", d_ya, attn, D // 2)
    g_w_pw, gb_w_pw = blocks("gw_pw_conv", d_yb, u3, D // 2)
    (d_conv, g["conv_w"], g["conv_b"], g["conv_ln_w"], g["conv_ln_b"]), ((ra_out, ra_w_o, ra_w_pw),) = conv_bwd(
        proj, cpre, d_u3, conv_w_f, conv_w_f[::-1], w["conv_ln_w"], w["conv_ln_b"],
        sides=(rs_to_sibling([gb_out, gb_w_o, gb_w_pw]),))
    pb_out, own_out = chip_sum("chip_sum_w_out", g_out, ra_out, c_idx, chip_idx)
    pb_w_o, own_w_o = chip_sum("chip_sum_w_o_attn", g_w_o, ra_w_o, c_idx, chip_idx)
    pb_w_pw, own_w_pw = chip_sum("chip_sum_w_pw_conv", g_w_pw, ra_w_pw, c_idx, chip_idx)
    (d_q, d_k, d_v, gqw, gkw), ((rb_ffn_out, rb_ffn_in, rb_out, rb_w_o, rb_w_pw),) = attn_bwd(
        proj, tabs, qw2, kw2, d_attn, attn, lse,
        sides=(rs_to_chips([pb_ffn_out, pb_ffn_in, pb_out, pb_w_o, pb_w_pw]),))
    g["q_norm_w"] = gqw[0:1, 0:HD] + gqw[0:1, HD:LANES]
    g["k_norm_w"] = gkw[0:1, 0:HD] + gkw[0:1, HD:LANES]
    d_segs = (d_q, d_k, d_v, d_conv, d_gl)
    parts, to_sibling, to_chips, owns, from_chips = [], None, None, [], []
    for k, hw in enumerate(GW_IN_SPLIT):
        sides = tuple(s for s in (to_chips, to_sibling) if s is not None)
        (part, part_b), outs = gw_in_t("gw_in_%d" % k, h, d_segs, sum(GW_IN_SPLIT[:k]), hw, sides=sides)
        outs = list(outs)
        if to_chips is not None:
            from_chips.append(outs.pop(0)[0])
        if to_sibling is not None:
            pb, own = chip_sum("chip_sum_w_in_%d" % (k - 1), parts[-1], outs.pop(0)[0], c_idx, chip_idx)
            owns.append(own)
            to_chips = rs_to_chips_combined(pb)
        else:
            to_chips = None
        parts.append(part.reshape(NDEV, INW // NDEV, hw))
        to_sibling = rs_to_sibling([part_b.reshape(NDEV, INW // NDEV, hw)])
    (grad_x, g["norm1_w"]), ((rb_prev,), (ra_last,)) = in_bwd(
        d_q, d_k, d_v, d_conv, d_gl, w_in_t, x2, d_x1, w["norm1_w"], sides=(to_chips, to_sibling))
    from_chips.append(rb_prev)
    pb, own = chip_sum("chip_sum_w_in_%d" % (len(GW_IN_SPLIT) - 1), parts[-1], ra_last, c_idx, chip_idx)
    owns.append(own)
    small_sums, ((rb_last,),) = small_sync(g, sq, sides=(rs_to_chips_combined(pb),))
    small, loss = small_adam(small_sums, w, m, v, (4 * ax + 2 * ay + c_idx).astype(jnp.int32).reshape(1))
    from_chips.append(rb_last)

    res = {
        "w_in": shard_adam("adam_w_in", owns, from_chips, w["w_in"], m["w_in"], v["w_in"]),
        "w_ffn_in": shard_adam("adam_w_ffn_in", [own_ffn_in], [rb_ffn_in], w["w_ffn_in"], m["w_ffn_in"], v["w_ffn_in"]),
        "w_o_attn": shard_adam("adam_w_o_attn", [own_w_o], [rb_w_o], w["w_o_attn"], m["w_o_attn"], v["w_o_attn"], io_t=True),
        "w_pw_conv": shard_adam("adam_w_pw_conv", [own_w_pw], [rb_w_pw],
                                w["w_pw_conv"], m["w_pw_conv"], v["w_pw_conv"], io_t=True),
        "w_out": shard_adam("adam_w_out", [own_out], [rb_out], w["w_out"], m["w_out"], v["w_out"]),
        "w_ffn_out": shard_adam("adam_w_ffn_out", [own_ffn_out], [rb_ffn_out],
                                w["w_ffn_out"], m["w_ffn_out"], v["w_ffn_out"]),
    }
    res = {k: tuple(a.T if k in TRANSPOSED else a for a in r) for k, r in res.items()}
    res.update(small)

    def shaped(name, a):
        return a.reshape((1,) + a.shape) if name in MATS or name in ("b_gate", "conv_w") else a

    outs = [loss, grad_x.reshape(1, S, D)]
    for i in range(4):
        outs += [shaped(k, res[k][i]) for k in WEIGHTS]
    return tuple(outs)
```

```python
import functools
from typing import Callable, NamedTuple, Optional

import numpy as np
import jax
import jax.numpy as jnp
from jax import lax
from jax.experimental import pallas as pl
from jax.experimental.pallas import tpu as pltpu

F32 = jnp.float32
BF16 = jnp.bfloat16

S = 2048
D = 1024
HD = 64
QKV = 1536
CC = 512
KW = 31
FF = 2816
INW = 7680
OFF_Q, OFF_K, OFF_V, OFF_CA, OFF_CB, OFF_GA, OFF_GB = 0, 1536, 3072, 4608, 5120, 5632, 6656
DILATIONS = (1, 4, 16)
HALF_SPAN = 64
EPS = 1e-6
NEG_INF = -1e30
ROPE_THETA = 500000.0
ROT_DIM = 16

ADAM_LR = 0.001
ADAM_B1 = 0.9
ADAM_B2 = 0.999
ADAM_EPS = 1e-08
ADAM_WD = 0.01
ADAM_STEP = 10

NDEV = 8
LANES = 128
TM = 256
IN_PROJ_TM = 512
TQ = 128
VMEM_LIMIT = 56 * 1024 * 1024
MESH = pl.DeviceIdType.MESH


def _cp(**kw):
    return pltpu.CompilerParams(vmem_limit_bytes=VMEM_LIMIT, **kw)


def _row(width, col=0, tm=TM):
    return pl.BlockSpec((tm, width), lambda i: (i, col))


PLANE = 512


def _planes(width, tm=TM):
    return pl.BlockSpec((width // PLANE, tm, PLANE), lambda i: (0, i, 0))


def _res(shape):
    nd = len(shape)
    return pl.BlockSpec(shape, lambda *_: (0,) * nd, pipeline_mode=pl.Buffered(1))


def _dot(a, b):
    return jnp.dot(a, b, preferred_element_type=F32)


def _dot_nt(a, b):
    return lax.dot_general(a, b, (((1,), (1,)), ((), ())), preferred_element_type=F32)


def _dot_tn(a, b):
    return lax.dot_general(a, b, (((0,), (0,)), ((), ())), preferred_element_type=F32)


def _sigmoid(x):
    return jax.nn.sigmoid(x)


def _dsilu(x, sg):
    return sg * (1.0 + x * (1.0 - sg))


ANY = pl.BlockSpec(memory_space=pl.ANY)
VMEM = pl.BlockSpec(memory_space=pltpu.VMEM)


class Side(NamedTuple):
    args: tuple
    in_specs: tuple
    out_shape: tuple
    scratch: tuple
    start: Callable
    finish: Callable
    mid: Optional[Callable] = None
    peers: str = ""


BARRIER_IDS = {"s": 0, "dxy": 1, "dsxy": 2, "sxy": 3, "xy": 4}


def _peer_barrier(peers):
    x, y, c = lax.axis_index("x"), lax.axis_index("y"), lax.axis_index("c")
    where = {"s": (x, y, 1 - c), "x": (1 - x, y, c), "y": (x, 1 - y, c), "d": (1 - x, 1 - y, c)}
    barrier = pltpu.get_barrier_semaphore()
    for p in peers:
        pl.semaphore_signal(barrier, inc=1, device_id=where[p], device_id_type=MESH)
    pl.semaphore_wait(barrier, len(peers))


def _call(body, sides=(), *, name, grid, in_specs, out_specs, out_shape, scratch_shapes=(), args, own_comm=False,
          tail=None):
    assert tail is None or int(np.prod(grid)) == 1
    ni, no, ns = len(in_specs), len(out_specs), len(scratch_shapes)
    cnt = [(len(s.args), len(s.out_shape), len(s.scratch)) for s in sides]
    peers = "".join(sorted(set("".join(s.peers for s in sides))))
    if own_comm or not sides or any(not s.peers for s in sides):
        peers = ""

    def take(refs, pos, n):
        return refs[pos:pos + n], pos + n

    def full(*refs):
        m_in, pos = take(refs, 0, ni)
        s_in = []
        for a, _, _ in cnt:
            r, pos = take(refs, pos, a)
            s_in.append(r)
        m_out, pos = take(refs, pos, no)
        s_out = []
        for _, o, _ in cnt:
            r, pos = take(refs, pos, o)
            s_out.append(r)
        m_scr, pos = take(refs, pos, ns)
        s_scr = []
        for _, _, c in cnt:
            r, pos = take(refs, pos, c)
            s_scr.append(r)
        if sides:
            first = functools.reduce(jnp.logical_and, [pl.program_id(d) == 0 for d in range(len(grid))])
            last = functools.reduce(jnp.logical_and, [pl.program_id(d) == g - 1 for d, g in enumerate(grid)])

            @pl.when(first)
            def _():
                if peers:
                    _peer_barrier(peers)
                for s, a, o, c in zip(sides, s_in, s_out, s_scr):
                    s.start(a, o, c)

            steps = int(np.prod(grid))
            mid_step = (2 * steps) // 3
            if steps > 1 and any(s.mid is not None for s in sides):
                step = functools.reduce(lambda acc, d: acc * grid[d] + pl.program_id(d), range(len(grid)), 0)

                @pl.when(step == mid_step)
                def _():
                    for s, a, o, c in zip(sides, s_in, s_out, s_scr):
                        if s.mid is not None:
                            s.mid(a, o, c)

        body(*m_in, *m_out, *m_scr)
        if sides:
            @pl.when(last)
            def _():
                for s, a, o, c in zip(sides, s_in, s_out, s_scr):
                    if s.mid is not None and steps == 1:
                        s.mid(a, o, c)
                if tail is not None:
                    tail(*m_in, *m_out, *m_scr)
                for s, a, o, c in zip(sides, s_in, s_out, s_scr):
                    s.finish(a, o, c)
        elif tail is not None:
            tail(*m_in, *m_out, *m_scr)

    res = pl.pallas_call(
        full, name=name, grid=grid,
        in_specs=list(in_specs) + [sp for s in sides for sp in s.in_specs],
        out_specs=list(out_specs) + [ANY for s in sides for _ in s.out_shape],
        out_shape=list(out_shape) + [o for s in sides for o in s.out_shape],
        scratch_shapes=list(scratch_shapes) + [c for s in sides for c in s.scratch],
        compiler_params=_cp(dimension_semantics=("arbitrary",) * len(grid),
                            **({"collective_id": BARRIER_IDS[peers]} if peers else {})),
    )(*args, *[a for s in sides for a in s.args])
    res = list(res)
    if not sides:
        return res
    outs, pos = take(res, 0, no)
    side_outs = []
    for _, o, _ in cnt:
        r, pos = take(res, pos, o)
        side_outs.append(r)
    return outs, side_outs


def _inv_freq_lanes():
    inv = np.float32(ROPE_THETA) ** (-np.arange(0, ROT_DIM, 2, dtype=np.float32) / np.float32(ROT_DIM))
    lane = np.arange(LANES) % HD
    out = np.where(lane < ROT_DIM, inv[lane % (ROT_DIM // 2)], 0.0).astype(np.float32)
    return jnp.asarray(out.reshape(1, LANES))


def _rope_tables(pos, inv_freq):
    ang = pos.astype(F32) * inv_freq
    lane = lax.broadcasted_iota(jnp.int32, ang.shape, 1) % HD
    cs = jnp.cos(ang)
    sn = jnp.sin(ang)
    return (jnp.where(lane < ROT_DIM, cs, 1.0), jnp.where(lane < ROT_DIM // 2, -sn, 0.0),
            jnp.where(lane < ROT_DIM // 2, 0.0, jnp.where(lane < ROT_DIM, sn, 0.0)))


def _rope(v, c, s1, s2):
    return v * c + pltpu.roll(v, LANES - 8, axis=1) * s1 + pltpu.roll(v, 8, axis=1) * s2


def _rope_t(d, c, s1, s2):
    return d * c - pltpu.roll(d, LANES - 8, axis=1) * s1 - pltpu.roll(d, 8, axis=1) * s2


def _head_mat():
    r = lax.broadcasted_iota(jnp.int32, (LANES, LANES), 0) // HD
    c = lax.broadcasted_iota(jnp.int32, (LANES, LANES), 1) // HD
    return jnp.where(r == c, 1.0 / HD, 0.0).astype(BF16)


def _head_mean(t, e):
    hi = t.astype(BF16)
    rest = (t - hi.astype(F32)).astype(BF16)
    return _dot(hi, e) + _dot(rest, e)


def in_proj_gather(x, norm_w, shard_t, chip_order, pos_col):
    R = INW // NDEV
    tm = IN_PROJ_TM
    half, nt = R // 2, S // tm

    def body(ord_ref, x_ref, nw_ref, sh_ref, pos_ref, f_ref, h_ref, p_ref, wfull_ref, c_ref, s1_ref, s2_ref,
             wt, hs, send, recv, loc):
        kk, i = pl.program_id(0), pl.program_id(1)
        x, y, c, _ = _place()
        me, flip = 4 * x + 2 * y + c, 1 - 2 * c
        here, sib, xn, yn = (x, y, c), (x, y, 1 - c), (1 - x, y, c), (x, 1 - y, c)
        b_xn, b_yn, b_dg = 4 * (1 - x) + 2 * y + c, 4 * x + 2 * (1 - y) + c, 4 * (1 - x) + 2 * (1 - y) + c

        def cp(k, block, to, rows=None):
            dst = wt.at[block] if rows is None else wt.at[block, pl.ds(rows * half, half), :]
            return _remote(dst, dst, send, recv, k, to)

        def sends():
            return [cp(0, me, sib), cp(1, me, xn), cp(2, me, yn), cp(3, b_xn, sib), cp(4, b_yn, sib),
                    cp(5, b_xn, yn, rows=0), cp(6, b_yn, xn, rows=1), cp(7, b_dg, sib, rows=0), cp(8, b_dg, sib, rows=1)]

        def keep(j, blk0):
            pair = pl.ds(pl.multiple_of(blk0, 2), 2)
            return pltpu.make_async_copy(wt.at[pair], wfull_ref.at[pair], loc.at[j])

        @pl.when((kk == 0) & (i == 0))
        def _():
            _peer_barrier("sxy")
            _cast_rows(wt.at[me], sh_ref)
            for s_ in sends()[0:3]:
                s_.start()

            def tables(j, _):
                chunk = pl.ds(pl.multiple_of(j * TM, TM), TM)
                c_ref[chunk, :], s1_ref[chunk, :], s2_ref[chunk, :] = _rope_tables(pos_ref[chunk, :], f_ref[...])
                return 0

            lax.fori_loop(0, S // TM, tables, 0)
            cp(0, me + flip, here).wait_recv()
            keep(0, me - c).start()

        @pl.when((kk == 1) & (i == 0))
        def _():
            cp(1, b_xn, here).wait_recv()
            sends()[5].start()
            sends()[3].start()
            cp(2, b_yn, here).wait_recv()
            sends()[6].start()
            sends()[4].start()
            cp(3, b_xn + flip, here).wait_recv()
            keep(1, b_xn - c).start()

        @pl.when((kk == 2) & (i == 0))
        def _():
            cp(4, b_yn + flip, here).wait_recv()
            keep(2, b_yn - c).start()

        @pl.when((kk == 3) & (i == 0))
        def _():
            cp(5, b_dg, here, rows=0).wait_recv()
            sends()[7].start()
            cp(6, b_dg, here, rows=1).wait_recv()
            sends()[8].start()
            cp(7, b_dg + flip, here, rows=0).wait_recv()
            cp(8, b_dg + flip, here, rows=1).wait_recv()
            keep(3, b_dg - c).start()

        rows = pl.ds(pl.multiple_of(i * tm, tm), tm)

        @pl.when(kk == 0)
        def _():
            xv = x_ref[...]
            r = lax.rsqrt(jnp.mean(xv * xv, axis=-1, keepdims=True) + EPS)
            hb = (xv * r * nw_ref[...]).astype(BF16)
            h_ref[...] = hb
            hs[rows, :] = hb

        h = hs[rows, :]
        chip = ord_ref[kk]
        for cc in range(2):
            p_ref[:, cc * R:(cc + 1) * R] = _dot_nt(h, wt[2 * chip + cc])

        @pl.when((kk == 3) & (i == nt - 1))
        def _():
            for s_ in sends():
                s_.wait_send()
            for j, blk in enumerate((me, b_xn, b_yn, b_dg)):
                keep(j, blk - c).wait()

    def first_pass(kk, i):
        return jnp.where(kk == 0, i, nt - 1)

    grid_spec = pltpu.PrefetchScalarGridSpec(
        num_scalar_prefetch=1, grid=(4, nt),
        in_specs=[pl.BlockSpec((tm, D), lambda kk, i, o: (first_pass(kk, i), 0)),
                  pl.BlockSpec((1, D), lambda kk, i, o: (0, 0)), VMEM, VMEM,
                  pl.BlockSpec((1, LANES), lambda kk, i, o: (0, 0))],
        out_specs=[pl.BlockSpec((tm, D), lambda kk, i, o: (first_pass(kk, i), 0)),
                   pl.BlockSpec((tm, 2 * R), lambda kk, i, o: (i, o[kk])), ANY]
        + [pl.BlockSpec((S, LANES), lambda kk, i, o: (0, 0))] * 3,
        scratch_shapes=[pltpu.VMEM((NDEV, R, D), BF16), pltpu.VMEM((S, D), BF16), _sems(9), _sems(9), _sems(4)])
    res = pl.pallas_call(
        body, name="in_proj_gather", grid_spec=grid_spec,
        out_shape=[jax.ShapeDtypeStruct((S, D), BF16), jax.ShapeDtypeStruct((S, INW), F32),
                   jax.ShapeDtypeStruct((NDEV, R, D), BF16)] + [jax.ShapeDtypeStruct((S, LANES), F32)] * 3,
        compiler_params=_cp(dimension_semantics=("arbitrary", "arbitrary"), collective_id=BARRIER_IDS["sxy"]),
    )(chip_order, x, norm_w, shard_t, pos_col, _inv_freq_lanes())
    return res[0], res[1], res[2], tuple(res[3:])


def _qk_specs():
    nb = QKV // LANES
    return [pl.BlockSpec((S, LANES), functools.partial(lambda hp, g, o: (0, o + g * 4 + hp), o=o))
            for o in (OFF_Q // LANES, OFF_K // LANES, OFF_V // LANES)]


def _tab_specs():
    return [pl.BlockSpec((S, LANES), lambda hp, g: (0, 0), pipeline_mode=pl.Buffered(1))] * 3


def _vec_spec():
    return pl.BlockSpec((1, LANES), lambda hp, g: (0, 0))


def _sub_rows(r, d, start, n):
    if d == 1:
        return pl.ds(start, n)
    return pl.ds(r + d * start, n, stride=d)


def _band_window(i, L):
    W = min(TQ + 2 * HALF_SPAN, L)
    q0 = pl.multiple_of(i * TQ, TQ)
    k0 = pl.multiple_of(jnp.clip(q0 - HALF_SPAN, 0, L - W), HALF_SPAN)
    qpos = q0 + (lax.broadcasted_iota(jnp.int32, (2 * TQ, W), 0) & (TQ - 1))
    kpos = k0 + lax.broadcasted_iota(jnp.int32, (2 * TQ, W), 1)
    valid = jnp.abs(qpos - kpos) <= HALF_SPAN
    return W, q0, k0, valid


def _stack_heads(t, lo):
    z = jnp.zeros_like(t)
    return jnp.concatenate([jnp.where(lo, t, z), jnp.where(lo, z, t)], axis=0)


def _unstack_heads(t2, lo):
    return jnp.where(lo, t2[0:TQ], t2[TQ:2 * TQ])


CHAINS = 8


def _interleave(d):
    ru = min(d, CHAINS)
    return ru, min(CHAINS // ru, S // d // TQ)


def _for_blocks(n, fn):
    if n == 1:
        fn(0)
    else:
        def it(j, _):
            fn(j)
            return 0
        lax.fori_loop(0, n, it, 0)


def attn_fwd(proj, tabs, qw2, kw2, sides=()):
    CH = 256

    def body(q_ref, k_ref, v_ref, c_ref, s1_ref, s2_ref, qw_ref, kw_ref, at_ref, ls_ref,
             qs, ks, vs, osub, lsub, onat, lnat, qn, kn):
        g = pl.program_id(1)
        lo = lax.broadcasted_iota(jnp.int32, (1, LANES), 1) < HD
        e = _head_mat()

        def prep(i, _):
            rows = pl.ds(pl.multiple_of(i * CH, CH), CH)
            c, s1, s2 = c_ref[rows, :], s1_ref[rows, :], s2_ref[rows, :]
            for t_ref, w_ref, out, scale in ((q_ref, qw_ref, qn, HD ** -0.5), (k_ref, kw_ref, kn, 1.0)):
                t = t_ref[rows, :]
                r = lax.rsqrt(_head_mean(t * t, e) + EPS)
                out[rows, :] = _rope(t * r * w_ref[...], c, s1, s2) * scale
            return 0

        lax.fori_loop(0, S // CH, prep, 0, unroll=4)

        def group(gi, d):
            L = S // d

            ru, nb = _interleave(d)

            def stage(r, off):
                for c0 in range(0, L, CH):
                    n = min(CH, L)
                    rows = _sub_rows(r, d, c0, n)
                    dst = pl.ds(off + c0, n)
                    qs[dst, :] = qn[rows, :].astype(BF16)
                    ks[dst, :] = kn[rows, :].astype(BF16)
                    vs[dst, :] = v_ref[rows, :].astype(BF16)

            def one(off, i):
                W, q0, k0, valid = _band_window(i, L)
                q2 = _stack_heads(qs[pl.ds(off + q0, TQ), :], lo)
                sc = jnp.where(valid, _dot_nt(q2, ks[pl.ds(off + k0, W), :]), NEG_INF)
                m = jnp.max(sc, axis=-1, keepdims=True)
                p = jnp.exp(sc - m)
                den = jnp.sum(p, axis=-1, keepdims=True)
                o2 = _dot(p.astype(BF16), vs[pl.ds(off + k0, W), :]) / den
                l2 = jnp.broadcast_to(m + jnp.log(den), (2 * TQ, LANES))
                osub[pl.ds(off + q0, TQ), :] = _unstack_heads(o2, lo)
                lsub[pl.ds(off + q0, TQ), :] = _unstack_heads(l2, lo)

            def unstage(r, off):
                for c0 in range(0, L, CH):
                    n = min(CH, L)
                    rows = _sub_rows(r, d, c0, n)
                    onat[gi, rows, :] = osub[pl.ds(off + c0, n), :]
                    lnat[gi, rows, :] = lsub[pl.ds(off + c0, n), :]

            def step(t, _):
                for u in range(ru):
                    stage(t * ru + u, u * L)
                _for_blocks(L // TQ // nb, lambda j: [one(u * L, j * nb + b) for u in range(ru) for b in range(nb)])
                for u in range(ru):
                    unstage(t * ru + u, u * L)
                return 0

            lax.fori_loop(0, d // ru, step, 0)

        for gi, d in enumerate(DILATIONS):
            pl.when(g == gi)(functools.partial(group, gi, d))

        @pl.when(g == len(DILATIONS) - 1)
        def _():
            def mix(i, _):
                rows = pl.ds(pl.multiple_of(i * CH, CH), CH)
                l0, l1, l2 = lnat[0, rows, :], lnat[1, rows, :], lnat[2, rows, :]
                m = jnp.maximum(jnp.maximum(l0, l1), l2)
                e0, e1, e2 = jnp.exp(l0 - m), jnp.exp(l1 - m), jnp.exp(l2 - m)
                den = e0 + e1 + e2
                a = (e0 * onat[0, rows, :] + e1 * onat[1, rows, :] + e2 * onat[2, rows, :]) / den
                at_ref[rows, :] = a.astype(BF16)
                ls_ref[rows, :] = m + jnp.log(den)
                return 0

            lax.fori_loop(0, S // CH, mix, 0)

    out_spec = pl.BlockSpec((S, LANES), lambda hp, g: (0, hp))
    return _call(
        body, sides, name="attn_fwd", grid=(4, 3),
        in_specs=_qk_specs() + _tab_specs() + [_vec_spec(), _vec_spec()],
        out_specs=[out_spec, out_spec],
        out_shape=[jax.ShapeDtypeStruct((S, CC), BF16), jax.ShapeDtypeStruct((S, CC), F32)],
        scratch_shapes=[pltpu.VMEM((S, LANES), BF16)] * 3 + [pltpu.VMEM((S, LANES), F32)] * 2
        + [pltpu.VMEM((3, S, LANES), F32)] * 2 + [pltpu.VMEM((S, LANES), F32)] * 2,
        args=(proj, proj, proj, *tabs, qw2, kw2))


def attn_bwd(proj, tabs, qw2, kw2, d_attn, attn, lse, sides=()):
    CH = 256

    def body(q_ref, k_ref, v_ref, c_ref, s1_ref, s2_ref, qw_ref, kw_ref, do_ref, at_ref, ls_ref,
             dq_ref, dk_ref, dv_ref, gqw_ref, gkw_ref,
             qs, ks, vs, dos, dsub, lsub, dqs, dks, dvs, dnat, qx, kx, dvn, tnq, tnk, rrq, rrk):
        hp, g = pl.program_id(0), pl.program_id(1)
        lo = lax.broadcasted_iota(jnp.int32, (1, LANES), 1) < HD
        e = _head_mat()
        both = ((q_ref, qw_ref, qx, tnq, rrq, HD ** -0.5), (k_ref, kw_ref, kx, tnk, rrk, 1.0))

        @pl.when((hp == 0) & (g == 0))
        def _():
            gqw_ref[...] = jnp.zeros_like(gqw_ref)
            gkw_ref[...] = jnp.zeros_like(gkw_ref)

        def prep(i, _):
            rows = pl.ds(pl.multiple_of(i * CH, CH), CH)
            dnat[rows, :] = _head_mean(do_ref[rows, :] * at_ref[rows, :].astype(F32), e) * float(HD)
            c, s1, s2 = c_ref[rows, :], s1_ref[rows, :], s2_ref[rows, :]
            for t_ref, w_ref, x, tn_s, rr_s, scale in both:
                t = t_ref[rows, :]
                rr = lax.rsqrt(_head_mean(t * t, e) + EPS)
                tn = t * rr
                rr_s[rows, :] = rr
                tn_s[rows, :] = tn
                x[rows, :] = _rope(tn * w_ref[...], c, s1, s2) * scale
            return 0

        lax.fori_loop(0, S // CH, prep, 0, unroll=4)

        def group(d):
            L = S // d

            ru, nb = _interleave(d)

            def stage(r, off):
                for c0 in range(0, L, CH):
                    n = min(CH, L)
                    rows = _sub_rows(r, d, c0, n)
                    dst = pl.ds(off + c0, n)
                    qs[dst, :] = qx[rows, :].astype(BF16)
                    ks[dst, :] = kx[rows, :].astype(BF16)
                    vs[dst, :] = v_ref[rows, :].astype(BF16)
                    dos[dst, :] = do_ref[rows, :].astype(BF16)
                    dsub[dst, :] = dnat[rows, :]
                    lsub[dst, :] = ls_ref[rows, :]
                    dks[dst, :] = jnp.zeros((n, LANES), F32)
                    dvs[dst, :] = jnp.zeros((n, LANES), F32)

            def one(off, i):
                W, q0, k0, valid = _band_window(i, L)
                qrows, krows = pl.ds(off + q0, TQ), pl.ds(off + k0, W)
                q2 = _stack_heads(qs[qrows, :], lo)
                do2 = _stack_heads(dos[qrows, :], lo)
                kk, vv = ks[krows, :], vs[krows, :]
                lse_b, dd_b = lsub[qrows, :], dsub[qrows, :]
                lse2 = jnp.concatenate([lse_b[:, 0:1], lse_b[:, HD:HD + 1]], axis=0)
                dd2 = jnp.concatenate([dd_b[:, 0:1], dd_b[:, HD:HD + 1]], axis=0)
                sc = jnp.where(valid, _dot_nt(q2, kk), NEG_INF)
                p = jnp.exp(sc - lse2)
                ds = (p * (_dot_nt(do2, vv) - dd2)).astype(BF16)
                dqs[qrows, :] = _unstack_heads(_dot(ds, kk), lo)
                dks[krows, :] = dks[krows, :] + _dot_tn(ds, q2)
                dvs[krows, :] = dvs[krows, :] + _dot_tn(p.astype(BF16), do2)

            def unstage(r, off):
                for c0 in range(0, L, CH):
                    n = min(CH, L)
                    rows = _sub_rows(r, d, c0, n)
                    src = pl.ds(off + c0, n)
                    qx[rows, :] = dqs[src, :]
                    kx[rows, :] = dks[src, :]
                    dvn[rows, :] = dvs[src, :]

            def step(t, _):
                for u in range(ru):
                    stage(t * ru + u, u * L)
                _for_blocks(L // TQ // nb, lambda j: [one(u * L, j * nb + b) for u in range(ru) for b in range(nb)])
                for u in range(ru):
                    unstage(t * ru + u, u * L)
                return 0

            lax.fori_loop(0, d // ru, step, 0)

        for gi, d in enumerate(DILATIONS):
            pl.when(g == gi)(functools.partial(group, d))

        def emit(i, _):
            rows = pl.ds(pl.multiple_of(i * CH, CH), CH)
            c, s1, s2 = c_ref[rows, :], s1_ref[rows, :], s2_ref[rows, :]
            for (_, w_ref, x, tn_s, rr_s, scale), out, gw_ref in zip(both, (dq_ref, dk_ref), (gqw_ref, gkw_ref)):
                tn = tn_s[rows, :]
                dy = _rope_t(x[rows, :] * scale, c, s1, s2)
                gw_ref[0:1, :] = gw_ref[0:1, :] + jnp.sum(dy * tn, axis=0, keepdims=True)
                dtn = dy * w_ref[...]
                out[rows, :] = (rr_s[rows, :] * (dtn - tn * _head_mean(dtn * tn, e))).astype(BF16)
            dv_ref[rows, :] = dvn[rows, :].astype(BF16)
            return 0

        lax.fori_loop(0, S // CH, emit, 0, unroll=4)

    nat_spec = pl.BlockSpec((S, LANES), lambda hp, g: (0, hp))
    out_spec = pl.BlockSpec((None, S, LANES), lambda hp, g: (g, 0, hp))
    acc_spec = pl.BlockSpec((8, LANES), lambda hp, g: (0, 0))
    return _call(
        body, sides, name="attn_bwd", grid=(4, 3),
        in_specs=_qk_specs() + _tab_specs() + [_vec_spec(), _vec_spec(), nat_spec, nat_spec, nat_spec],
        out_specs=[out_spec] * 3 + [acc_spec] * 2,
        out_shape=[jax.ShapeDtypeStruct((QKV // PLANE, S, PLANE), BF16)] * 3 + [jax.ShapeDtypeStruct((8, LANES), F32)] * 2,
        scratch_shapes=[pltpu.VMEM((S, LANES), BF16)] * 4 + [pltpu.VMEM((S, LANES), F32)] * 13,
        args=(proj, proj, proj, *tabs, qw2, kw2, d_attn, attn, lse))


PADR = 16
CT = 128


def _conv_specs():
    return [pl.BlockSpec((S, CC), lambda i: (0, OFF_CA // CC)), pl.BlockSpec((S, CC), lambda i: (0, OFF_CB // CC))]


NCB = CC // LANES


def _pad_zero(pad):
    for cb in range(NCB):
        pad[cb, 0:PADR, :] = jnp.zeros((PADR, LANES), F32)
        pad[cb, PADR + S:PADR + S + PADR, :] = jnp.zeros((PADR, LANES), F32)


def _pad_store(pad, row0, n, val):
    for cb in range(NCB):
        pad[cb, pl.ds(pl.multiple_of(row0 + PADR, 8), n), :] = val[:, cb * LANES:(cb + 1) * LANES]


def _taps(pad_ref, cb, s0, weights):
    acc = jnp.zeros((CT, LANES), F32)
    for k in range(KW):
        acc = acc + weights[k] * pad_ref[cb, pl.ds(s0 + k + 1, CT), :]
    return acc


def conv_fwd(proj, conv_w, conv_b, ln_w, ln_b, sides=()):
    def body(a_ref, b_ref, w_ref, cb_ref, lw_ref, lb_ref, c_ref, u3_ref, upad):
        _pad_zero(upad)

        def glu(i, _):
            rows = pl.ds(pl.multiple_of(i * TM, TM), TM)
            _pad_store(upad, i * TM, TM, a_ref[rows, :] * _sigmoid(b_ref[rows, :]))
            return 0

        lax.fori_loop(0, S // TM, glu, 0)

        def chunk(i, _):
            s0 = pl.multiple_of(i * CT, CT)
            for cb in range(CC // LANES):
                cols = slice(cb * LANES, (cb + 1) * LANES)
                w = [w_ref[k:k + 1, cols] for k in range(KW)]
                c_ref[pl.ds(s0, CT), cols] = _taps(upad, cb, s0, w) + cb_ref[:, cols]
            cv = c_ref[pl.ds(s0, CT), :]
            mu = jnp.mean(cv, axis=-1, keepdims=True)
            xc = cv - mu
            rstd = lax.rsqrt(jnp.mean(xc * xc, axis=-1, keepdims=True) + EPS)
            yl = xc * rstd * lw_ref[...] + lb_ref[...]
            u3_ref[pl.ds(s0, CT), :] = (yl * _sigmoid(yl)).astype(BF16)
            return 0

        lax.fori_loop(0, S // CT, chunk, 0)

    vec = pl.BlockSpec((1, CC), lambda i: (0, 0))
    full = pl.BlockSpec((S, CC), lambda i: (0, 0))
    return _call(
        body, sides, name="conv_fwd", grid=(1,),
        in_specs=_conv_specs() + [pl.BlockSpec((KW, CC), lambda i: (0, 0)), vec, vec, vec],
        out_specs=[full, full],
        out_shape=[jax.ShapeDtypeStruct((S, CC), F32), jax.ShapeDtypeStruct((S, CC), BF16)],
        scratch_shapes=[pltpu.VMEM((NCB, S + 2 * PADR, LANES), F32)],
        args=(proj, proj, conv_w, conv_b, ln_w, ln_b))


def conv_bwd(proj, cpre, d_u3, conv_w, conv_w_rev, ln_w, ln_b, sides=()):
    def body(a_ref, b_ref, c_ref, du3_ref, w_ref, wr_ref, lw_ref, lb_ref,
             dc_ref, gw_ref, gcb_ref, glw_ref, glb_ref, upad, dpad):
        _pad_zero(upad)
        _pad_zero(dpad)
        gw_ref[...] = jnp.zeros_like(gw_ref)

        def ln_bwd(i, carry):
            gcb, glw, glb = carry
            rows = pl.ds(pl.multiple_of(i * TM, TM), TM)
            _pad_store(upad, i * TM, TM, a_ref[rows, :] * _sigmoid(b_ref[rows, :]))
            cv = c_ref[rows, :]
            mu = jnp.mean(cv, axis=-1, keepdims=True)
            xc = cv - mu
            rstd = lax.rsqrt(jnp.mean(xc * xc, axis=-1, keepdims=True) + EPS)
            xh = xc * rstd
            yl = xh * lw_ref[...] + lb_ref[...]
            dyl = du3_ref[rows, :] * _dsilu(yl, _sigmoid(yl))
            dxh = dyl * lw_ref[...]
            dcv = rstd * (dxh - jnp.mean(dxh, axis=-1, keepdims=True)
                          - xh * jnp.mean(dxh * xh, axis=-1, keepdims=True))
            _pad_store(dpad, i * TM, TM, dcv)
            return (gcb + jnp.sum(dcv, axis=0, keepdims=True),
                    glw + jnp.sum(dyl * xh, axis=0, keepdims=True),
                    glb + jnp.sum(dyl, axis=0, keepdims=True))

        z = jnp.zeros((1, CC), F32)
        gcb, glw, glb = lax.fori_loop(0, S // TM, ln_bwd, (z, z, z))
        gcb_ref[...] = gcb
        glw_ref[...] = glw
        glb_ref[...] = glb

        def chunk(i, _):
            s0 = pl.multiple_of(i * CT, CT)
            for cb in range(CC // LANES):
                cols = slice(cb * LANES, (cb + 1) * LANES)
                wr = [wr_ref[k:k + 1, cols] for k in range(KW)]
                du = _taps(dpad, cb, s0, wr)
                dcv = dpad[cb, pl.ds(s0 + PADR, CT), :]
                for k in range(KW):
                    gw_ref[k:k + 1, cols] = gw_ref[k:k + 1, cols] + jnp.sum(
                        upad[cb, pl.ds(s0 + k + 1, CT), :] * dcv, axis=0, keepdims=True)
                av = a_ref[pl.ds(s0, CT), cols]
                sb = _sigmoid(b_ref[pl.ds(s0, CT), cols])
                dc_ref[0, pl.ds(s0, CT), cols] = (du * sb).astype(BF16)
                dc_ref[1, pl.ds(s0, CT), cols] = (du * av * sb * (1.0 - sb)).astype(BF16)
            return 0

        lax.fori_loop(0, S // CT, chunk, 0)

    vec = pl.BlockSpec((1, CC), lambda i: (0, 0))
    full = pl.BlockSpec((S, CC), lambda i: (0, 0))
    wsp = pl.BlockSpec((KW, CC), lambda i: (0, 0))
    return _call(
        body, sides, name="conv_bwd", grid=(1,),
        in_specs=_conv_specs() + [full, full, wsp, wsp, vec, vec],
        out_specs=[pl.BlockSpec((2, S, CC), lambda i: (0, 0, 0)), wsp, vec, vec, vec],
        out_shape=[jax.ShapeDtypeStruct((2, S, CC), BF16), jax.ShapeDtypeStruct((KW, CC), F32)]
        + [jax.ShapeDtypeStruct((1, CC), F32)] * 3,
        scratch_shapes=[pltpu.VMEM((NCB, S + 2 * PADR, LANES), F32)] * 2,
        args=(proj, proj, cpre, d_u3, conv_w, conv_w_rev, ln_w, ln_b))


def _gate_specs():
    return [_row(CC, col=OFF_GA // CC + j) for j in range(4)]


def _gates(g_refs, bg_ref):
    ga = _sigmoid(jnp.concatenate([g_refs[0][...], g_refs[1][...]], axis=1) + bg_ref[0:1, :])
    gb = _sigmoid(jnp.concatenate([g_refs[2][...], g_refs[3][...]], axis=1) + bg_ref[1:2, :])
    return ga, gb


def mix_out(x, proj, b_gate, attn, u3, w_o, w_pw, w_out):
    def body(x_ref, g0, g1, g2, g3, bg_ref, at_ref, u3_ref, wo_ref, wp_ref, wout_ref,
             x1_ref, z_ref, ya_ref, yb_ref):
        ga, gb = _gates((g0, g1, g2, g3), bg_ref)
        ya = _dot_nt(at_ref[...], wo_ref[...])
        yb = _dot_nt(u3_ref[...], wp_ref[...])
        z = (ga * ya + gb * yb).astype(BF16)
        ya_ref[...] = ya.astype(BF16)
        yb_ref[...] = yb.astype(BF16)
        z_ref[...] = z
        x1_ref[...] = x_ref[...] + _dot(z, wout_ref[...])

    return pl.pallas_call(
        body, name="mix_out", grid=(S // TM,),
        in_specs=[_row(D)] + _gate_specs() + [_res((2, D)), _row(CC), _row(CC),
                                              _res((D, CC)), _res((D, CC)), _res((D, D))],
        out_specs=[_row(D)] * 4,
        out_shape=[jax.ShapeDtypeStruct((S, D), F32)] + [jax.ShapeDtypeStruct((S, D), BF16)] * 3,
        compiler_params=_cp(dimension_semantics=("arbitrary",)),
    )(x, proj, proj, proj, proj, b_gate, attn, u3, w_o, w_pw, w_out)


def out_bwd(d_x1b, proj, b_gate, ya, yb, w_o, w_pw, w_out, sides=()):
    def body(dx_ref, g0, g1, g2, g3, bg_ref, ya_ref, yb_ref, wo_ref, wp_ref, wout_ref,
             dya_ref, dyb_ref, dgl_ref, dat_ref, du3_ref, gbg_ref):
        @pl.when(pl.program_id(0) == 0)
        def _():
            gbg_ref[...] = jnp.zeros_like(gbg_ref)

        ga, gb = _gates((g0, g1, g2, g3), bg_ref)
        dz = _dot_nt(dx_ref[...], wout_ref[...])
        dya = (dz * ga).astype(BF16)
        dyb = (dz * gb).astype(BF16)
        dgla = dz * ya_ref[...].astype(F32) * ga * (1.0 - ga)
        dglb = dz * yb_ref[...].astype(F32) * gb * (1.0 - gb)
        dya_ref[...] = dya
        dyb_ref[...] = dyb
        for j in range(2):
            dgl_ref[j] = dgla[:, j * PLANE:(j + 1) * PLANE].astype(BF16)
            dgl_ref[2 + j] = dglb[:, j * PLANE:(j + 1) * PLANE].astype(BF16)
        gbg_ref[0:1, :] = gbg_ref[0:1, :] + jnp.sum(dgla, axis=0, keepdims=True)
        gbg_ref[1:2, :] = gbg_ref[1:2, :] + jnp.sum(dglb, axis=0, keepdims=True)
        dat_ref[...] = _dot(dya, wo_ref[...])
        du3_ref[...] = _dot(dyb, wp_ref[...])

    return _call(
        body, sides, name="out_bwd", grid=(S // TM,),
        in_specs=[_row(D)] + _gate_specs() + [_res((2, D)), _row(D), _row(D),
                                              _res((D, CC)), _res((D, CC)), _res((D, D))],
        out_specs=[_row(D), _row(D), _planes(2 * D), _row(CC), _row(CC), pl.BlockSpec((2, D), lambda i: (0, 0))],
        out_shape=[jax.ShapeDtypeStruct((S, D), BF16)] * 2 + [jax.ShapeDtypeStruct((2 * D // PLANE, S, PLANE), BF16)]
        + [jax.ShapeDtypeStruct((S, CC), F32)] * 2 + [jax.ShapeDtypeStruct((2, D), F32)],
        args=(d_x1b, proj, proj, proj, proj, b_gate, ya, yb, w_o, w_pw, w_out))


def ffn_in(x1, norm_w, w_ffn_in, sides=()):
    half = FF // 2

    def body(x_ref, nw_ref, w_ref, h_ref, gu_ref, f_ref):
        xv = x_ref[...]
        r = lax.rsqrt(jnp.mean(xv * xv, axis=-1, keepdims=True) + EPS)
        h = (xv * r * nw_ref[...]).astype(BF16)
        h_ref[...] = h
        for j in range(2):
            gt = _dot_nt(h, w_ref[j * half:(j + 1) * half, :])
            up = _dot_nt(h, w_ref[FF + j * half:FF + (j + 1) * half, :])
            gu_ref[:, j * half:(j + 1) * half] = gt.astype(BF16)
            gu_ref[:, FF + j * half:FF + (j + 1) * half] = up.astype(BF16)
            f_ref[:, j * half:(j + 1) * half] = (gt * _sigmoid(gt) * up).astype(BF16)

    return _call(
        body, sides, name="ffn_in", grid=(S // TM,),
        in_specs=[_row(D), _res((1, D)), _res((2 * FF, D))],
        out_specs=[_row(D), _row(2 * FF), _row(FF)],
        out_shape=[jax.ShapeDtypeStruct((S, D), BF16), jax.ShapeDtypeStruct((S, 2 * FF), BF16),
                   jax.ShapeDtypeStruct((S, FF), BF16)],
        args=(x1, norm_w, w_ffn_in))


def ffn_out_loss(x1, f, w_ffn_out, target):
    def body(x_ref, f_ref, w_ref, t_ref, dy_ref, dyb_ref, sq_ref):
        @pl.when(pl.program_id(0) == 0)
        def _():
            sq_ref[...] = jnp.zeros_like(sq_ref)

        diff = x_ref[...] + _dot(f_ref[...], w_ref[...]) - t_ref[...]
        dy = diff * (1.0 / D)
        dy_ref[...] = dy
        dyb_ref[...] = dy.astype(BF16)
        sq_ref[...] = sq_ref[...] + jnp.sum((diff * diff).reshape(TM // 8, 8, D), axis=0)

    return pl.pallas_call(
        body, name="ffn_out_loss", grid=(S // TM,),
        in_specs=[_row(D), _row(FF), _res((FF, D)), _row(D)],
        out_specs=[_row(D), _row(D), pl.BlockSpec((8, D), lambda i: (0, 0))],
        out_shape=[jax.ShapeDtypeStruct((S, D), F32), jax.ShapeDtypeStruct((S, D), BF16),
                   jax.ShapeDtypeStruct((8, D), F32)],
        compiler_params=_cp(dimension_semantics=("arbitrary",)),
    )(x1, f, w_ffn_out, target)


def _rms_bwd(xv, nw, dh):
    r = lax.rsqrt(jnp.mean(xv * xv, axis=-1, keepdims=True) + EPS)
    xn = xv * r
    dxn = dh * nw
    dx = r * (dxn - xn * jnp.mean(dxn * xn, axis=-1, keepdims=True))
    return dx, dh * xn


def ffn_bwd(dy, dyb, gu, x1, norm_w, w_ffn_in, w_ffn_out, sides=()):
    def body(dy_ref, dyb_ref, gu_ref, x_ref, nw_ref, wi_ref, wo_ref, dgu_ref, dx_ref, dxb_ref, gn_ref):
        @pl.when(pl.program_id(0) == 0)
        def _():
            gn_ref[...] = jnp.zeros_like(gn_ref)

        df = _dot_nt(dyb_ref[...], wo_ref[...])
        gt = gu_ref[:, 0:FF].astype(F32)
        up = gu_ref[:, FF:2 * FF].astype(F32)
        sg = _sigmoid(gt)
        dgt = (df * up * _dsilu(gt, sg)).astype(BF16)
        dup = (df * gt * sg).astype(BF16)
        dgu_ref[:, 0:FF] = dgt
        dgu_ref[:, FF:2 * FF] = dup
        dh = _dot(dgt, wi_ref[0:FF, :]) + _dot(dup, wi_ref[FF:2 * FF, :])
        dxn, gw = _rms_bwd(x_ref[...], nw_ref[...], dh)
        dx = dy_ref[...] + dxn
        dx_ref[...] = dx
        dxb_ref[...] = dx.astype(BF16)
        gn_ref[...] = gn_ref[...] + jnp.sum(gw, axis=0, keepdims=True)

    return _call(
        body, sides, name="ffn_bwd", grid=(S // TM,),
        in_specs=[_row(D), _row(D), _row(2 * FF), _row(D), _res((1, D)), _res((2 * FF, D)), _res((FF, D))],
        out_specs=[_row(2 * FF), _row(D), _row(D), pl.BlockSpec((1, D), lambda i: (0, 0))],
        out_shape=[jax.ShapeDtypeStruct((S, 2 * FF), BF16), jax.ShapeDtypeStruct((S, D), F32),
                   jax.ShapeDtypeStruct((S, D), BF16), jax.ShapeDtypeStruct((1, D), F32)],
        args=(dy, dyb, gu, x1, norm_w, w_ffn_in, w_ffn_out))


def in_bwd(d_q, d_k, d_v, d_conv, d_gl, w_in, x, d_x1, norm_w, sides=()):
    segs = ((OFF_Q, QKV), (OFF_K, QKV), (OFF_V, QKV), (OFF_CA, 2 * CC), (OFF_GA, 2 * D))

    def body(dq_ref, dk_ref, dv_ref, dc_ref, dg_ref, w_ref, x_ref, dx1_ref, nw_ref, gx_ref, gn_ref):
        @pl.when(pl.program_id(0) == 0)
        def _():
            gn_ref[...] = jnp.zeros_like(gn_ref)

        dh = jnp.zeros((TM, D), F32)
        for ref, (off, width) in zip((dq_ref, dk_ref, dv_ref, dc_ref, dg_ref), segs):
            for j in range(width // PLANE):
                dh = dh + _dot(ref[j], w_ref[off + j * PLANE:off + (j + 1) * PLANE, :])
        dxn, gw = _rms_bwd(x_ref[...], nw_ref[...], dh)
        gx_ref[...] = dx1_ref[...] + dxn
        gn_ref[...] = gn_ref[...] + jnp.sum(gw, axis=0, keepdims=True)

    return _call(
        body, sides, name="in_bwd", grid=(S // TM,),
        in_specs=[_planes(QKV)] * 3 + [_planes(2 * CC), _planes(2 * D), _res((INW, D)), _row(D), _row(D), _res((1, D))],
        out_specs=[_row(D), pl.BlockSpec((1, D), lambda i: (0, 0))],
        out_shape=[jax.ShapeDtypeStruct((S, D), F32), jax.ShapeDtypeStruct((1, D), F32)],
        args=(d_q, d_k, d_v, d_conv, d_gl, w_in, x, d_x1, norm_w))


def mm_tn(name, pairs, tm, tn):
    n = len(pairs)
    M, N = pairs[0][0].shape[1], pairs[0][1].shape[1]

    def body(*refs):
        for a_ref, b_ref, o_ref, ob_ref in zip(refs[0:2 * n:2], refs[1:2 * n:2], refs[2 * n::2], refs[2 * n + 1::2]):
            r = _dot_tn(a_ref[...], b_ref[...])
            o_ref[...] = r
            ob_ref[...] = r.astype(BF16)

    return _call(
        body, name=name, grid=(M // tm, N // tn),
        in_specs=[pl.BlockSpec((S, tm), lambda i, j: (0, i)), pl.BlockSpec((S, tn), lambda i, j: (0, j))] * n,
        out_specs=[pl.BlockSpec((tm, tn), lambda i, j: (i, j))] * (2 * n),
        out_shape=[jax.ShapeDtypeStruct((M, N), F32), jax.ShapeDtypeStruct((M, N), BF16)] * n,
        args=[t for pair in pairs for t in pair])


GW_IN_TN = PLANE
GW_IN_SPLIT = (768, 256)


def gw_in_t(name, h, d_segs, col0, hw, sides=()):
    tn = GW_IN_TN
    starts, t0 = [], 0
    for seg in d_segs:
        starts.append(t0)
        t0 += seg.shape[0]
    ntiles = [seg.shape[0] for seg in d_segs]

    def body(h_ref, *refs):
        a_refs, o_ref, ob_ref = refs[:-2], refs[-2], refs[-1]
        n = pl.program_id(0)
        for a_ref, st, nt in zip(a_refs, starts, ntiles):
            @pl.when((n >= st) & (n < st + nt))
            def _(a_ref=a_ref):
                r = _dot_tn(a_ref[...], h_ref[...])
                o_ref[...] = r
                ob_ref[...] = r.astype(BF16)

    def seg_spec(st, nt):
        return pl.BlockSpec((None, S, tn), lambda n: (jnp.clip(n - st, 0, nt - 1), 0, 0))

    res = _call(
        body, sides, name=name, grid=(INW // tn,),
        in_specs=[pl.BlockSpec((S, hw), lambda n: (0, col0 // hw))] + [seg_spec(st, nt) for st, nt in zip(starts, ntiles)],
        out_specs=[pl.BlockSpec((tn, hw), lambda n: (n, 0))] * 2,
        out_shape=[jax.ShapeDtypeStruct((INW, hw), F32), jax.ShapeDtypeStruct((INW, hw), BF16)],
        args=(h, *d_segs))
    return (res[0], res[1]) if sides else (res, [])


def _place():
    x, y, c = lax.axis_index("x"), lax.axis_index("y"), lax.axis_index("c")
    chips = [(1 - x, y), (x, 1 - y), (1 - x, 1 - y)]
    return x, y, c, chips


def _sems(n):
    return pltpu.SemaphoreType.DMA((n,))


def _remote(src, dst, send, recv, k, to):
    return pltpu.make_async_remote_copy(src_ref=src, dst_ref=dst, send_sem=send.at[k], recv_sem=recv.at[k],
                                        device_id=to, device_id_type=MESH)


def _cast_rows(dst, src, cols=slice(None)):
    rows = src.shape[0]
    step = next((s for s in (128, 64, 32, 16) if rows % s == 0), rows)
    for r0 in range(0, rows, step):
        dst[r0:r0 + step, cols] = src[r0:r0 + step, :].astype(dst.dtype)


def comm_only(name, sides):
    def body():
        pass

    return _call(body, sides, name=name, grid=(1,), in_specs=[], out_specs=[], out_shape=[], args=())[1]


def ag_blocks(shard, dtype):
    R, W = shard.shape

    def copy(outs, scr, k, block, to, src=None):
        dst = outs[0].at[block]
        return _remote(dst if src is None else src, dst, scr[1], scr[2], k, to)

    def local(outs, scr, me):
        return pltpu.make_async_copy(scr[0], outs[0].at[me], scr[3].at[0])

    def start(ins, outs, scr):
        x, y, c, chips = _place()
        me = 4 * x + 2 * y + c
        _cast_rows(scr[0], ins[0])
        local(outs, scr, me).start()
        copy(outs, scr, 0, me, (x, y, 1 - c), src=scr[0]).start()
        for j, (cx, cy) in enumerate(chips):
            copy(outs, scr, 1 + j, me, (cx, cy, c), src=scr[0]).start()

    def finish(ins, outs, scr):
        x, y, c, chips = _place()
        me, sib = 4 * x + 2 * y + c, (x, y, 1 - c)
        passed = []
        for j, (cx, cy) in enumerate(chips):
            theirs = 4 * cx + 2 * cy + c
            copy(outs, scr, 1 + j, theirs, (x, y, c)).wait_recv()
            fwd = copy(outs, scr, 4 + j, theirs, sib)
            fwd.start()
            passed.append(fwd)
        copy(outs, scr, 0, 4 * x + 2 * y + 1 - c, (x, y, c)).wait_recv()
        for j, (cx, cy) in enumerate(chips):
            copy(outs, scr, 4 + j, 4 * cx + 2 * cy + 1 - c, (x, y, c)).wait_recv()
        copy(outs, scr, 0, me, sib, src=scr[0]).wait_send()
        for j, (cx, cy) in enumerate(chips):
            copy(outs, scr, 1 + j, me, (cx, cy, c), src=scr[0]).wait_send()
        for fwd in passed:
            fwd.wait_send()
        local(outs, scr, me).wait()

    return Side((shard,), (VMEM,), (jax.ShapeDtypeStruct((NDEV, R, W), dtype),),
                (pltpu.VMEM((R, W), dtype), _sems(7), _sems(7), _sems(1)), start, finish, None, "dsxy")


def ag_blocks_relay(shard, dtype, transpose=False):
    R, W = shard.shape[::-1] if transpose else shard.shape
    half = R // 2

    def copy(outs, scr, k, block, to, src=None, rows=None):
        dst = outs[0].at[block] if rows is None else outs[0].at[block, pl.ds(rows * half, half), :]
        return _remote(dst if src is None else src, dst, scr[1], scr[2], k, to)

    def local(outs, scr, me):
        return pltpu.make_async_copy(scr[0], outs[0].at[me], scr[3].at[0])

    def own(outs, scr):
        x, y, c, _ = _place()
        me = 4 * x + 2 * y + c
        return [copy(outs, scr, k, me, to, src=scr[0])
                for k, to in enumerate([(x, y, 1 - c), (1 - x, y, c), (x, 1 - y, c)])]

    def start(ins, outs, scr):
        x, y, c, _ = _place()
        if transpose:
            scr[0][...] = ins[0][...].T.astype(dtype)
        else:
            _cast_rows(scr[0], ins[0])
        local(outs, scr, 4 * x + 2 * y + c).start()
        for cp in own(outs, scr):
            cp.start()

    def passed_on(outs, scr):
        x, y, c, _ = _place()
        sib, xn, yn = (x, y, 1 - c), (1 - x, y, c), (x, 1 - y, c)
        b_xn, b_yn, b_dg = 4 * (1 - x) + 2 * y + c, 4 * x + 2 * (1 - y) + c, 4 * (1 - x) + 2 * (1 - y) + c
        near = [copy(outs, scr, 5, b_xn, yn, rows=0), copy(outs, scr, 3, b_xn, sib),
                copy(outs, scr, 6, b_yn, xn, rows=1), copy(outs, scr, 4, b_yn, sib)]
        far = [copy(outs, scr, 7, b_dg, sib, rows=0), copy(outs, scr, 8, b_dg, sib, rows=1)]
        return (b_xn, b_yn, b_dg), near, far

    def mid(ins, outs, scr):
        x, y, c, _ = _place()
        (b_xn, b_yn, _), near, _ = passed_on(outs, scr)
        copy(outs, scr, 1, b_xn, (x, y, c)).wait_recv()
        near[0].start()
        near[1].start()
        copy(outs, scr, 2, b_yn, (x, y, c)).wait_recv()
        near[2].start()
        near[3].start()

    def finish(ins, outs, scr):
        x, y, c, _ = _place()
        here = (x, y, c)
        (b_xn, b_yn, b_dg), near, far = passed_on(outs, scr)
        copy(outs, scr, 5, b_dg, here, rows=0).wait_recv()
        far[0].start()
        copy(outs, scr, 6, b_dg, here, rows=1).wait_recv()
        far[1].start()
        flip = 1 - 2 * c
        copy(outs, scr, 0, 4 * x + 2 * y + 1 - c, here).wait_recv()
        copy(outs, scr, 3, b_xn + flip, here).wait_recv()
        copy(outs, scr, 4, b_yn + flip, here).wait_recv()
        copy(outs, scr, 7, b_dg + flip, here, rows=0).wait_recv()
        copy(outs, scr, 8, b_dg + flip, here, rows=1).wait_recv()
        for cp in own(outs, scr) + near + far:
            cp.wait_send()
        local(outs, scr, 4 * x + 2 * y + c).wait()

    return Side((shard,), (VMEM,), (jax.ShapeDtypeStruct((NDEV, R, W), dtype),),
                (pltpu.VMEM((R, W), dtype), _sems(9), _sems(9), _sems(1)), start, finish, mid, "sxy")


def copies_side(args, out_shape, n_copies, plan, peers):
    def copies(ins, outs, scr):
        return [_remote(s_, d_, scr[0], scr[1], i, to) for i, (s_, d_, to) in enumerate(plan(ins, outs))]

    def start(ins, outs, scr):
        for cp in copies(ins, outs, scr):
            cp.start()

    def finish(ins, outs, scr):
        for cp in copies(ins, outs, scr):
            cp.wait()

    return Side(tuple(args), (ANY,) * len(args), tuple(out_shape), (_sems(n_copies), _sems(n_copies)),
                start, finish, None, peers)


def rs_to_sibling(grads):
    out_shape = [jax.ShapeDtypeStruct((4,) + g.shape[1:], BF16) for g in grads]

    def plan(ins, outs):
        x, y, c, _ = _place()
        return [(g.at[2 * k + 1 - c], r.at[k], (x, y, 1 - c)) for g, r in zip(ins, outs) for k in range(4)]

    return copies_side(grads, out_shape, 4 * len(grads), plan, "s")


def rs_to_chips(parts):
    out_shape = [jax.ShapeDtypeStruct((3,) + p.shape[1:], BF16) for p in parts]

    def plan(ins, outs):
        x, y, c, chips = _place()
        return [(p.at[2 * cx + cy], r.at[j], (cx, cy, c))
                for p, r in zip(ins, outs) for j, (cx, cy) in enumerate(chips)]

    return copies_side(parts, out_shape, 3 * len(parts), plan, "dxy")


def rs_to_chips_combined(part):
    _, R, W = part.shape
    half = R // 2
    top, bot = pl.ds(0, half), pl.ds(half, half)

    def copies(ins, outs, scr):
        p, r = ins[0], outs[0]
        loc_a, loc_b, in_x, in_y, comb_a, comb_b, send, recv, loc = scr
        x, y, c, _ = _place()
        xn, yn = (1 - x, y, c), (x, 1 - y, c)
        k_xn, k_yn, k_dg = 2 * (1 - x) + y, 2 * x + 1 - y, 2 * (1 - x) + 1 - y
        direct = [_remote(p.at[k_xn, top, :], r.at[0, top, :], send, recv, 0, xn),
                  _remote(p.at[k_yn, bot, :], r.at[1, bot, :], send, recv, 1, yn),
                  _remote(p.at[k_dg, top, :], in_x, send, recv, 2, xn),
                  _remote(p.at[k_dg, bot, :], in_y, send, recv, 3, yn)]
        combined = [_remote(comb_a, r.at[1, top, :], send, recv, 4, yn),
                    _remote(comb_b, r.at[0, bot, :], send, recv, 5, xn)]
        local = [pltpu.make_async_copy(p.at[k_yn, top, :], loc_a, loc.at[0]),
                 pltpu.make_async_copy(p.at[k_xn, bot, :], loc_b, loc.at[1])]
        return direct, combined, local

    def start(ins, outs, scr):
        direct, _, local = copies(ins, outs, scr)
        for cp in local + direct:
            cp.start()

    def mid(ins, outs, scr):
        loc_a, loc_b, in_x, in_y, comb_a, comb_b = scr[:6]
        direct, combined, local = copies(ins, outs, scr)
        for mine, arrival, inbox, out, nxt in ((local[0], direct[2], in_x, comb_a, combined[0]),
                                               (local[1], direct[3], in_y, comb_b, combined[1])):
            mine.wait()
            arrival.wait_recv()
            src = loc_a if out is comb_a else loc_b
            out[...] = (src[...].astype(F32) + inbox[...].astype(F32)).astype(BF16)
            nxt.start()

    def finish(ins, outs, scr):
        direct, combined, _ = copies(ins, outs, scr)
        direct[0].wait_recv()
        direct[1].wait_recv()
        combined[0].wait_recv()
        combined[1].wait_recv()
        for cp in direct + combined:
            cp.wait_send()

    buf = pltpu.VMEM((half, W), BF16)
    return Side((part,), (ANY,), (jax.ShapeDtypeStruct((2, R, W), BF16),),
                (buf, buf, buf, buf, buf, buf, _sems(6), _sems(6), _sems(2)), start, finish, mid, "xy")


ADAM_TILE_BYTES = 3 * 512 * 1024


def _row_tiles(rows, width):
    return 2 if rows % 32 == 0 and rows * width * 4 > ADAM_TILE_BYTES else 1


def chip_sum(name, grads, recvs, c_idx, chip_idx):
    n = len(grads)

    def body(s_ref, *refs):
        k = pl.program_id(0)
        for g_ref, r_ref, p_ref, own_ref in zip(refs[:n], refs[n:2 * n], refs[2 * n::2], refs[2 * n + 1::2]):
            tot = g_ref[0] + r_ref[0].astype(F32)
            p_ref[0] = tot.astype(BF16)

            @pl.when(k == s_ref[1])
            def _(own_ref=own_ref, tot=tot):
                own_ref[...] = tot

    def block(g):
        return (1,) + g.shape[1:]

    grid_spec = pltpu.PrefetchScalarGridSpec(
        num_scalar_prefetch=1, grid=(4,),
        in_specs=[pl.BlockSpec(block(g), lambda k, s: (2 * k + s[0], 0, 0)) for g in grads]
        + [pl.BlockSpec(block(g), lambda k, s: (k, 0, 0)) for g in grads],
        out_specs=[sp for g in grads for sp in (pl.BlockSpec(block(g), lambda k, s: (k, 0, 0)),
                                                pl.BlockSpec(g.shape[1:], lambda k, s: (0, 0)))])
    res = pl.pallas_call(
        body, name=name, grid_spec=grid_spec,
        out_shape=[sh for g in grads for sh in (jax.ShapeDtypeStruct((4,) + g.shape[1:], BF16),
                                                jax.ShapeDtypeStruct(g.shape[1:], F32))],
        compiler_params=_cp(dimension_semantics=("arbitrary",)),
    )(jnp.stack([c_idx, chip_idx]), *grads, *recvs)
    return [(res[2 * j], res[2 * j + 1]) for j in range(n)]


def _adamw(w, g, m, v):
    m2 = ADAM_B1 * m + (1.0 - ADAM_B1) * g
    v2 = ADAM_B2 * v + (1.0 - ADAM_B2) * (g * g)
    m_hat = m2 / (1.0 - ADAM_B1 ** ADAM_STEP)
    v_hat = v2 / (1.0 - ADAM_B2 ** ADAM_STEP)
    delta = -ADAM_LR * (m_hat / (jnp.sqrt(v_hat) + ADAM_EPS) + ADAM_WD * w)
    return delta, m2, v2


def shard_adam(name, owns, recvs, w, m, v, io_t=False):
    n = len(owns)
    R = owns[0].shape[0]
    ct = min(o.shape[1] for o in owns)
    first = [sum(o.shape[1] for o in owns[:j]) // ct for j in range(n)]
    count = [o.shape[1] // ct for o in owns]
    nt = _row_tiles(R, ct)
    tr = R // nt

    def body(*refs):
        o_refs, r_refs = refs[:n], refs[n:2 * n]
        w_ref, m_ref, v_ref, g_ref, d_ref, nm_ref, nv_ref = refs[2 * n:]
        g = None
        for j in range(n):
            gj = o_refs[j][...]
            for q in range(recvs[j].shape[0]):
                gj = gj + r_refs[j][q].astype(F32)
            g = gj if g is None else jnp.where(pl.program_id(0) >= first[j], gj, g)
        t = (lambda a: a.T) if io_t else (lambda a: a)
        delta, m2, v2 = _adamw(t(w_ref[...]), g, t(m_ref[...]), t(v_ref[...]))
        g_ref[...] = t(g)
        d_ref[...] = t(delta)
        nm_ref[...] = t(m2)
        nv_ref[...] = t(v2)

    def part(j):
        return pl.BlockSpec((tr, ct), lambda k, i: (i, jnp.clip(k - first[j], 0, count[j] - 1)))

    def part3(j):
        return pl.BlockSpec((recvs[j].shape[0], tr, ct), lambda k, i: (0, i, jnp.clip(k - first[j], 0, count[j] - 1)))

    C = sum(count) * ct
    tile = pl.BlockSpec((ct, tr), lambda k, i: (k, i)) if io_t else pl.BlockSpec((tr, ct), lambda k, i: (i, k))
    return pl.pallas_call(
        body, name=name, grid=(sum(count), nt),
        in_specs=[part(j) for j in range(n)] + [part3(j) for j in range(n)] + [tile, tile, tile],
        out_specs=[tile] * 4, out_shape=[jax.ShapeDtypeStruct((C, R) if io_t else (R, C), F32)] * 4,
        compiler_params=_cp(dimension_semantics=("arbitrary", "arbitrary")),
    )(*owns, *recvs, w, m, v)


ROW_N1, ROW_N2, ROW_BG, ROW_QN, ROW_KN, ROW_CB, ROW_LW, ROW_LB, ROW_CW = 0, 1, 2, 4, 5, 6, 7, 8, 9
PACK_ROWS = 40
SMALL = ("norm1_w", "norm2_w", "b_gate", "q_norm_w", "k_norm_w", "conv_b", "conv_ln_w", "conv_ln_b", "conv_w")


def small_sync(g, sq, sides=()):
    ns = len(SMALL)

    def copies(refs):
        pack, recv, send_sems, recv_sems = refs[ns + 2:]
        x, y, c, _ = _place()
        return [pltpu.make_async_remote_copy(
            src_ref=pack, dst_ref=recv.at[4 * x + 2 * y + c], send_sem=send_sems.at[k - 1],
            recv_sem=recv_sems.at[k - 1], device_id=(x ^ (k >> 2), y ^ ((k >> 1) & 1), c ^ (k & 1)),
            device_id_type=MESH) for k in range(1, NDEV)]

    def body(*refs):
        gi = dict(zip(SMALL, refs[:ns]))
        sq_ref, tot, pack, recv, send_sems, recv_sems = refs[ns:]
        x, y, c, _ = _place()
        me = 4 * x + 2 * y + c

        pack[...] = jnp.zeros_like(pack)
        pack[ROW_KN:ROW_KN + 1, LANES:2 * LANES] = jnp.full((1, LANES), (0.5 / D) * jnp.sum(sq_ref[...]), F32)
        pack[ROW_N1:ROW_N1 + 1, :] = gi["norm1_w"][...]
        pack[ROW_N2:ROW_N2 + 1, :] = gi["norm2_w"][...]
        pack[ROW_BG:ROW_BG + 2, :] = gi["b_gate"][...]
        pack[ROW_QN:ROW_QN + 1, 0:HD] = gi["q_norm_w"][...]
        pack[ROW_KN:ROW_KN + 1, 0:HD] = gi["k_norm_w"][...]
        pack[ROW_CB:ROW_CB + 1, 0:CC] = gi["conv_b"][...]
        pack[ROW_LW:ROW_LW + 1, 0:CC] = gi["conv_ln_w"][...]
        pack[ROW_LB:ROW_LB + 1, 0:CC] = gi["conv_ln_b"][...]
        pack[ROW_CW:ROW_CW + KW, 0:CC] = gi["conv_w"][...]

        for cp in copies(refs):
            cp.start()
        recv[me] = pack[...]

    def tail(*refs):
        tot, recv = refs[ns + 1], refs[ns + 3]
        for cp in copies(refs):
            cp.wait()
        acc = recv[0]
        for p in range(1, NDEV):
            acc = acc + recv[p]
        tot[...] = acc

    args = [g[k] for k in SMALL] + [sq]
    res = _call(
        body, sides, name="small_sync", grid=(1,), in_specs=[VMEM] * len(args), out_specs=[VMEM],
        out_shape=[jax.ShapeDtypeStruct((PACK_ROWS, D), F32)],
        scratch_shapes=[pltpu.VMEM((PACK_ROWS, D), F32), pltpu.VMEM((NDEV, PACK_ROWS, D), F32),
                        _sems(NDEV - 1), _sems(NDEV - 1)],
        args=args, own_comm=True, tail=tail)
    return (res[0][0], res[1]) if sides else res[0]


def small_adam(tot, w, m, v, me):
    ns = len(SMALL)

    def body(me_ref, tot, *refs):
        wi = dict(zip(SMALL, refs[:ns]))
        mi = dict(zip(SMALL, refs[ns:2 * ns]))
        vi = dict(zip(SMALL, refs[2 * ns:3 * ns]))
        outs = refs[3 * ns:7 * ns]
        loss_ref = refs[7 * ns]
        me = me_ref[0]

        def shard_grad(name):
            if name == "b_gate":
                return tot[ROW_BG:ROW_BG + 2, pl.ds(pl.multiple_of(me * LANES, LANES), LANES)]
            if name == "conv_w":
                win = tot[ROW_CW:ROW_CW + KW, pl.ds(pl.multiple_of((me // 2) * LANES, LANES), LANES)]
                return jnp.where(me % 2 == 1, win[:, HD:LANES], win[:, 0:HD])
            row = {"norm1_w": ROW_N1, "norm2_w": ROW_N2, "q_norm_w": ROW_QN, "k_norm_w": ROW_KN,
                   "conv_b": ROW_CB, "conv_ln_w": ROW_LW, "conv_ln_b": ROW_LB}[name]
            return tot[row:row + 1, 0:wi[name].shape[1]]

        for i, name in enumerate(SMALL):
            gr = shard_grad(name)
            delta, m2, v2 = _adamw(wi[name][...], gr, mi[name][...], vi[name][...])
            outs[4 * i][...] = gr
            outs[4 * i + 1][...] = delta
            outs[4 * i + 2][...] = m2
            outs[4 * i + 3][...] = v2
        loss_ref[...] = tot[ROW_KN:ROW_KN + 1, LANES:2 * LANES]

    out_shape = []
    for name in SMALL:
        out_shape += [jax.ShapeDtypeStruct(w[name].shape, F32)] * 4
    out_shape.append(jax.ShapeDtypeStruct((1, LANES), F32))
    args = [tot] + [w[k] for k in SMALL] + [m[k] for k in SMALL] + [v[k] for k in SMALL]
    grid_spec = pltpu.PrefetchScalarGridSpec(
        num_scalar_prefetch=1, grid=(1,), in_specs=[VMEM] * len(args), out_specs=[VMEM] * len(out_shape))
    res = pl.pallas_call(body, name="small_adam", grid_spec=grid_spec, out_shape=out_shape)(me, *args)
    out = {name: tuple(res[4 * i:4 * i + 4]) for i, name in enumerate(SMALL)}
    return out, res[4 * ns][0, 0]


MATS = ("w_in", "w_o_attn", "w_pw_conv", "w_out", "w_ffn_in", "w_ffn_out")
TRANSPOSED = ("w_in", "w_ffn_in")
WEIGHTS = ("norm1_w", "w_in", "b_gate", "q_norm_w", "k_norm_w", "w_o_attn", "conv_w", "conv_b", "conv_ln_w",
           "conv_ln_b", "w_pw_conv", "w_out", "norm2_w", "w_ffn_in", "w_ffn_out")


def _blocks_to_cols(blocks):
    n, R, C = blocks.shape
    return blocks.transpose(1, 0, 2).reshape(R, n * C)


def kernel(x, positions, norm1_w, w_in, b_gate, q_norm_w, k_norm_w, w_o_attn, conv_w, conv_b, conv_ln_w, conv_ln_b, w_pw_conv, w_out, norm2_w, w_ffn_in, w_ffn_out, loss_target, m_norm1_w, m_w_in, m_b_gate, m_q_norm_w, m_k_norm_w, m_w_o_attn, m_conv_w, m_conv_b, m_conv_ln_w, m_conv_ln_b, m_w_pw_conv, m_w_out, m_norm2_w, m_w_ffn_in, m_w_ffn_out, v_norm1_w, v_w_in, v_b_gate, v_q_norm_w, v_k_norm_w, v_w_o_attn, v_conv_w, v_conv_b, v_conv_ln_w, v_conv_ln_b, v_w_pw_conv, v_w_out, v_norm2_w, v_w_ffn_in, v_w_ffn_out):
    w = dict(norm1_w=norm1_w, w_in=w_in, b_gate=b_gate, q_norm_w=q_norm_w, k_norm_w=k_norm_w, w_o_attn=w_o_attn,
             conv_w=conv_w, conv_b=conv_b, conv_ln_w=conv_ln_w, conv_ln_b=conv_ln_b, w_pw_conv=w_pw_conv,
             w_out=w_out, norm2_w=norm2_w, w_ffn_in=w_ffn_in, w_ffn_out=w_ffn_out)
    m = dict(norm1_w=m_norm1_w, w_in=m_w_in, b_gate=m_b_gate, q_norm_w=m_q_norm_w, k_norm_w=m_k_norm_w,
             w_o_attn=m_w_o_attn, conv_w=m_conv_w, conv_b=m_conv_b, conv_ln_w=m_conv_ln_w,
             conv_ln_b=m_conv_ln_b, w_pw_conv=m_w_pw_conv, w_out=m_w_out, norm2_w=m_norm2_w,
             w_ffn_in=m_w_ffn_in, w_ffn_out=m_w_ffn_out)
    v = dict(norm1_w=v_norm1_w, w_in=v_w_in, b_gate=v_b_gate, q_norm_w=v_q_norm_w, k_norm_w=v_k_norm_w,
             w_o_attn=v_w_o_attn, conv_w=v_conv_w, conv_b=v_conv_b, conv_ln_w=v_conv_ln_w,
             conv_ln_b=v_conv_ln_b, w_pw_conv=v_w_pw_conv, w_out=v_w_out, norm2_w=v_norm2_w,
             w_ffn_in=v_w_ffn_in, w_ffn_out=v_w_ffn_out)
    def two_d(t):
        t = {k: (a[0] if a.ndim == 3 else a) for k, a in t.items()}
        return {k: (a.T if k in TRANSPOSED else a) for k, a in t.items()}

    w, m, v = two_d(w), two_d(m), two_d(v)

    x2, target = x[0], loss_target[0]
    c_idx = lax.axis_index("c").astype(jnp.int32)
    chip_idx = (2 * lax.axis_index("x") + lax.axis_index("y")).astype(jnp.int32)
    qw2 = jnp.tile(w["q_norm_w"], (1, 2))
    kw2 = jnp.tile(w["k_norm_w"], (1, 2))

    ax, ay = lax.axis_index("x"), lax.axis_index("y")
    chip_order = jnp.stack([2 * ax + ay, 2 * (1 - ax) + ay, 2 * ax + 1 - ay, 2 * (1 - ax) + 1 - ay]).astype(jnp.int32)
    h, proj, w_in_blocks, tabs = in_proj_gather(x2, w["norm1_w"], w["w_in"], chip_order, positions.reshape(S, 1))
    w_in_t = w_in_blocks.reshape(INW, D)
    (attn, lse), ((w_ffn_in_blocks,), (w_out_blocks,), (w_o_blocks,), (w_pw_blocks,), (bg_blocks,), (cw_blocks,)) = attn_fwd(
        proj, tabs, qw2, kw2, sides=(ag_blocks_relay(w["w_ffn_in"], BF16), ag_blocks_relay(w["w_out"], BF16),
                                     ag_blocks_relay(w["w_o_attn"], BF16, transpose=True),
                                     ag_blocks_relay(w["w_pw_conv"], BF16, transpose=True),
                                     ag_blocks(w["b_gate"], F32), ag_blocks(w["conv_w"], F32)))
    w_ffn_in_t = w_ffn_in_blocks.reshape(2 * FF, D)
    w_out_f = w_out_blocks.reshape(D, D)
    w_o_t, w_pw_t = w_o_blocks.reshape(D, CC), w_pw_blocks.reshape(D, CC)
    b_gate_f, conv_w_f = _blocks_to_cols(bg_blocks), _blocks_to_cols(cw_blocks)
    cpre, u3 = conv_fwd(proj, conv_w_f, w["conv_b"], w["conv_ln_w"], w["conv_ln_b"])
    x1, z, ya, yb = mix_out(x2, proj, b_gate_f, attn, u3, w_o_t, w_pw_t, w_out_f)
    (h2, gu, f), ((w_ffn_out_blocks,),) = ffn_in(x1, w["norm2_w"], w_ffn_in_t, sides=(ag_blocks_relay(w["w_ffn_out"], BF16),))
    w_ffn_out_f = w_ffn_out_blocks.reshape(FF, D)
    dy, dyb, sq = ffn_out_loss(x1, f, w_ffn_out_f, target)

    g = {}
    def blocks(name, pairs, tm):
        a, b = pairs[0]
        return [t.reshape(NDEV, a.shape[1] // NDEV, b.shape[1]) for t in mm_tn(name, pairs, tm, b.shape[1])]

    g_ffn_out, gb_ffn_out = blocks("gw_ffn_out", [(f, dyb)], FF // 2)
    (d_gu, d_x1, d_x1b, g["norm2_w"]), ((ra_ffn_out,),) = ffn_bwd(
        dy, dyb, gu, x1, w["norm2_w"], w_ffn_in_t, w_ffn_out_f, sides=(rs_to_sibling([gb_ffn_out]),))
    (pb_ffn_out, own_ffn_out), = chip_sum("chip_sum_w_ffn_out", [g_ffn_out], [ra_ffn_out], c_idx, chip_idx)
    g_ffn_in, gb_ffn_in = blocks("gw_ffn_in", [(d_gu, h2)], FF // 2)
    g_out, gb_out = blocks("gw_out", [(z, d_x1b)], D // 2)
    (d_ya, d_yb, d_gl, d_attn, d_u3, g["b_gate"]), ((ra_ffn_in,),) = out_bwd(
        d_x1b, proj, b_gate_f, ya, yb, w_o_t, w_pw_t, w_out_f, sides=(rs_to_sibling([gb_ffn_in]),))
    (pb_ffn_in, own_ffn_in), = chip_sum("chip_sum_w_ffn_in", [g_ffn_in], [ra_ffn_in], c_idx, chip_idx)
    g_w_o, gb_w_o, g_w_pw, gb_w_pw = blocks("gw_o_pw", [(d_ya, attn), (d_yb, u3)], D // 2)
    (d_conv, g["conv_w"], g["conv_b"], g["conv_ln_w"], g["conv_ln_b"]), ((ra_out, ra_w_o, ra_w_pw),) = conv_bwd(
        proj, cpre, d_u3, conv_w_f, conv_w_f[::-1], w["conv_ln_w"], w["conv_ln_b"],
        sides=(rs_to_sibling([gb_out, gb_w_o, gb_w_pw]),))
    (pb_out, own_out), (pb_w_o, own_w_o), (pb_w_pw, own_w_pw) = chip_sum(
        "chip_sum_w_out_o_pw", [g_out, g_w_o, g_w_pw], [ra_out, ra_w_o, ra_w_pw], c_idx, chip_idx)
    (d_q, d_k, d_v, gqw, gkw), ((rb_ffn_out, rb_ffn_in, rb_out, rb_w_o, rb_w_pw),) = attn_bwd(
        proj, tabs, qw2, kw2, d_attn, attn, lse,
        sides=(rs_to_chips([pb_ffn_out, pb_ffn_in, pb_out, pb_w_o, pb_w_pw]),))
    g["q_norm_w"] = gqw[0:1, 0:HD] + gqw[0:1, HD:LANES]
    g["k_norm_w"] = gkw[0:1, 0:HD] + gkw[0:1, HD:LANES]
    d_segs = (d_q, d_k, d_v, d_conv, d_gl)
    parts, to_sibling, to_chips, owns, from_chips = [], None, None, [], []
    for k, hw in enumerate(GW_IN_SPLIT):
        sides = tuple(s for s in (to_chips, to_sibling) if s is not None)
        (part, part_b), outs = gw_in_t("gw_in_%d" % k, h, d_segs, sum(GW_IN_SPLIT[:k]), hw, sides=sides)
        outs = list(outs)
        if to_chips is not None:
            from_chips.append(outs.pop(0)[0])
        if to_sibling is not None:
            (pb, own), = chip_sum("chip_sum_w_in_%d" % (k - 1), [parts[-1]], [outs.pop(0)[0]], c_idx, chip_idx)
            owns.append(own)
            to_chips = rs_to_chips_combined(pb)
        else:
            to_chips = None
        parts.append(part.reshape(NDEV, INW // NDEV, hw))
        to_sibling = rs_to_sibling([part_b.reshape(NDEV, INW // NDEV, hw)])
    (grad_x, g["norm1_w"]), ((rb_prev,), (ra_last,)) = in_bwd(
        d_q, d_k, d_v, d_conv, d_gl, w_in_t, x2, d_x1, w["norm1_w"], sides=(to_chips, to_sibling))
    from_chips.append(rb_prev)
    (pb, own), = chip_sum("chip_sum_w_in_%d" % (len(GW_IN_SPLIT) - 1), [parts[-1]], [ra_last], c_idx, chip_idx)
    owns.append(own)
    small_sums, ((rb_last,),) = small_sync(g, sq, sides=(rs_to_chips_combined(pb),))
    small, loss = small_adam(small_sums, w, m, v, (4 * ax + 2 * ay + c_idx).astype(jnp.int32).reshape(1))
    from_chips.append(rb_last)

    res = {
        "w_in": shard_adam("adam_w_in", owns, from_chips, w["w_in"], m["w_in"], v["w_in"]),
        "w_ffn_in": shard_adam("adam_w_ffn_in", [own_ffn_in], [rb_ffn_in], w["w_ffn_in"], m["w_ffn_in"], v["w_ffn_in"]),
        "w_o_attn": shard_adam("adam_w_o_attn", [own_w_o], [rb_w_o], w["w_o_attn"], m["w_o_attn"], v["w_o_attn"], io_t=True),
        "w_pw_conv": shard_adam("adam_w_pw_conv", [own_w_pw], [rb_w_pw],
                                w["w_pw_conv"], m["w_pw_conv"], v["w_pw_conv"], io_t=True),
        "w_out": shard_adam("adam_w_out", [own_out], [rb_out], w["w_out"], m["w_out"], v["w_out"]),
        "w_ffn_out": shard_adam("adam_w_ffn_out", [own_ffn_out], [rb_ffn_out],
                                w["w_ffn_out"], m["w_ffn_out"], v["w_ffn_out"]),
    }
    res = {k: tuple(a.T if k in TRANSPOSED else a for a in r) for k, r in res.items()}
    res.update(small)

    def shaped(name, a):
        return a.reshape((1,) + a.shape) if name in MATS or name in ("b_gate", "conv_w") else a

    outs = [loss, grad_x.reshape(1, S, D)]
    for i in range(4):
        outs += [shaped(k, res[k][i]) for k in WEIGHTS]
    return tuple(outs)
```

```python
import functools
from typing import Callable, NamedTuple, Optional

import numpy as np
import jax
import jax.numpy as jnp
from jax import lax
from jax.experimental import pallas as pl
from jax.experimental.pallas import tpu as pltpu

F32 = jnp.float32
BF16 = jnp.bfloat16

S = 2048
D = 1024
HD = 64
QKV = 1536
CC = 512
KW = 31
FF = 2816
INW = 7680
OFF_Q, OFF_K, OFF_V, OFF_CA, OFF_CB, OFF_GA, OFF_GB = 0, 1536, 3072, 4608, 5120, 5632, 6656
DILATIONS = (1, 4, 16)
HALF_SPAN = 64
EPS = 1e-6
NEG_INF = -1e30
ROPE_THETA = 500000.0
ROT_DIM = 16

ADAM_LR = 0.001
ADAM_B1 = 0.9
ADAM_B2 = 0.999
ADAM_EPS = 1e-08
ADAM_WD = 0.01
ADAM_STEP = 10

NDEV = 8
LANES = 128
TM = 256
IN_PROJ_TM = 512
TQ = 128
VMEM_LIMIT = 56 * 1024 * 1024
MESH = pl.DeviceIdType.MESH


def _cp(**kw):
    return pltpu.CompilerParams(vmem_limit_bytes=VMEM_LIMIT, **kw)


def _row(width, col=0, tm=TM):
    return pl.BlockSpec((tm, width), lambda i: (i, col))


PLANE = 512


def _planes(width, tm=TM):
    return pl.BlockSpec((width // PLANE, tm, PLANE), lambda i: (0, i, 0))


def _res(shape):
    nd = len(shape)
    return pl.BlockSpec(shape, lambda *_: (0,) * nd, pipeline_mode=pl.Buffered(1))


def _dot(a, b):
    return jnp.dot(a, b, preferred_element_type=F32)


def _dot_nt(a, b):
    return lax.dot_general(a, b, (((1,), (1,)), ((), ())), preferred_element_type=F32)


def _dot_tn(a, b):
    return lax.dot_general(a, b, (((0,), (0,)), ((), ())), preferred_element_type=F32)


def _sigmoid(x):
    return jax.nn.sigmoid(x)


def _dsilu(x, sg):
    return sg * (1.0 + x * (1.0 - sg))


ANY = pl.BlockSpec(memory_space=pl.ANY)
VMEM = pl.BlockSpec(memory_space=pltpu.VMEM)


class Side(NamedTuple):
    args: tuple
    in_specs: tuple
    out_shape: tuple
    scratch: tuple
    start: Callable
    finish: Callable
    mid: Optional[Callable] = None
    peers: str = ""


BARRIER_IDS = {"s": 0, "dxy": 1, "dsxy": 2, "sxy": 3, "xy": 4}


def _peer_barrier(peers):
    x, y, c = lax.axis_index("x"), lax.axis_index("y"), lax.axis_index("c")
    where = {"s": (x, y, 1 - c), "x": (1 - x, y, c), "y": (x, 1 - y, c), "d": (1 - x, 1 - y, c)}
    barrier = pltpu.get_barrier_semaphore()
    for p in peers:
        pl.semaphore_signal(barrier, inc=1, device_id=where[p], device_id_type=MESH)
    pl.semaphore_wait(barrier, len(peers))


def _call(body, sides=(), *, name, grid, in_specs, out_specs, out_shape, scratch_shapes=(), args, own_comm=False,
          tail=None):
    assert tail is None or int(np.prod(grid)) == 1
    ni, no, ns = len(in_specs), len(out_specs), len(scratch_shapes)
    cnt = [(len(s.args), len(s.out_shape), len(s.scratch)) for s in sides]
    peers = "".join(sorted(set("".join(s.peers for s in sides))))
    if own_comm or not sides or any(not s.peers for s in sides):
        peers = ""

    def take(refs, pos, n):
        return refs[pos:pos + n], pos + n

    def full(*refs):
        m_in, pos = take(refs, 0, ni)
        s_in = []
        for a, _, _ in cnt:
            r, pos = take(refs, pos, a)
            s_in.append(r)
        m_out, pos = take(refs, pos, no)
        s_out = []
        for _, o, _ in cnt:
            r, pos = take(refs, pos, o)
            s_out.append(r)
        m_scr, pos = take(refs, pos, ns)
        s_scr = []
        for _, _, c in cnt:
            r, pos = take(refs, pos, c)
            s_scr.append(r)
        if sides:
            first = functools.reduce(jnp.logical_and, [pl.program_id(d) == 0 for d in range(len(grid))])
            last = functools.reduce(jnp.logical_and, [pl.program_id(d) == g - 1 for d, g in enumerate(grid)])

            @pl.when(first)
            def _():
                if peers:
                    _peer_barrier(peers)
                for s, a, o, c in zip(sides, s_in, s_out, s_scr):
                    s.start(a, o, c)

            steps = int(np.prod(grid))
            mid_step = (2 * steps) // 3
            if steps > 1 and any(s.mid is not None for s in sides):
                step = functools.reduce(lambda acc, d: acc * grid[d] + pl.program_id(d), range(len(grid)), 0)

                @pl.when(step == mid_step)
                def _():
                    for s, a, o, c in zip(sides, s_in, s_out, s_scr):
                        if s.mid is not None:
                            s.mid(a, o, c)

        body(*m_in, *m_out, *m_scr)
        if sides:
            @pl.when(last)
            def _():
                for s, a, o, c in zip(sides, s_in, s_out, s_scr):
                    if s.mid is not None and steps == 1:
                        s.mid(a, o, c)
                if tail is not None:
                    tail(*m_in, *m_out, *m_scr)
                for s, a, o, c in zip(sides, s_in, s_out, s_scr):
                    s.finish(a, o, c)
        elif tail is not None:
            tail(*m_in, *m_out, *m_scr)

    res = pl.pallas_call(
        full, name=name, grid=grid,
        in_specs=list(in_specs) + [sp for s in sides for sp in s.in_specs],
        out_specs=list(out_specs) + [ANY for s in sides for _ in s.out_shape],
        out_shape=list(out_shape) + [o for s in sides for o in s.out_shape],
        scratch_shapes=list(scratch_shapes) + [c for s in sides for c in s.scratch],
        compiler_params=_cp(dimension_semantics=("arbitrary",) * len(grid),
                            **({"collective_id": BARRIER_IDS[peers]} if peers else {})),
    )(*args, *[a for s in sides for a in s.args])
    res = list(res)
    if not sides:
        return res
    outs, pos = take(res, 0, no)
    side_outs = []
    for _, o, _ in cnt:
        r, pos = take(res, pos, o)
        side_outs.append(r)
    return outs, side_outs


def _inv_freq_lanes():
    inv = np.float32(ROPE_THETA) ** (-np.arange(0, ROT_DIM, 2, dtype=np.float32) / np.float32(ROT_DIM))
    lane = np.arange(LANES) % HD
    out = np.where(lane < ROT_DIM, inv[lane % (ROT_DIM // 2)], 0.0).astype(np.float32)
    return jnp.asarray(out.reshape(1, LANES))


def _rope_tables(pos, inv_freq):
    ang = pos.astype(F32) * inv_freq
    lane = lax.broadcasted_iota(jnp.int32, ang.shape, 1) % HD
    cs = jnp.cos(ang)
    sn = jnp.sin(ang)
    return (jnp.where(lane < ROT_DIM, cs, 1.0), jnp.where(lane < ROT_DIM // 2, -sn, 0.0),
            jnp.where(lane < ROT_DIM // 2, 0.0, jnp.where(lane < ROT_DIM, sn, 0.0)))


def _rope(v, c, s1, s2):
    return v * c + pltpu.roll(v, LANES - 8, axis=1) * s1 + pltpu.roll(v, 8, axis=1) * s2


def _rope_t(d, c, s1, s2):
    return d * c - pltpu.roll(d, LANES - 8, axis=1) * s1 - pltpu.roll(d, 8, axis=1) * s2


def _head_mat():
    r = lax.broadcasted_iota(jnp.int32, (LANES, LANES), 0) // HD
    c = lax.broadcasted_iota(jnp.int32, (LANES, LANES), 1) // HD
    return jnp.where(r == c, 1.0 / HD, 0.0).astype(BF16)


def _head_mean(t, e):
    hi = t.astype(BF16)
    rest = (t - hi.astype(F32)).astype(BF16)
    return _dot(hi, e) + _dot(rest, e)


def in_proj_gather(x, norm_w, shard_t, chip_order, pos_col):
    R = INW // NDEV
    tm = IN_PROJ_TM
    half, nt = R // 2, S // tm

    def body(ord_ref, x_ref, nw_ref, sh_ref, pos_ref, f_ref, h_ref, p_ref, wfull_ref, c_ref, s1_ref, s2_ref,
             wt, hs, send, recv, loc):
        kk, i = pl.program_id(0), pl.program_id(1)
        x, y, c, _ = _place()
        me, flip = 4 * x + 2 * y + c, 1 - 2 * c
        here, sib, xn, yn = (x, y, c), (x, y, 1 - c), (1 - x, y, c), (x, 1 - y, c)
        b_xn, b_yn, b_dg = 4 * (1 - x) + 2 * y + c, 4 * x + 2 * (1 - y) + c, 4 * (1 - x) + 2 * (1 - y) + c

        def cp(k, block, to, rows=None):
            dst = wt.at[block] if rows is None else wt.at[block, pl.ds(rows * half, half), :]
            return _remote(dst, dst, send, recv, k, to)

        def sends():
            return [cp(0, me, sib), cp(1, me, xn), cp(2, me, yn), cp(3, b_xn, sib), cp(4, b_yn, sib),
                    cp(5, b_xn, yn, rows=0), cp(6, b_yn, xn, rows=1), cp(7, b_dg, sib, rows=0), cp(8, b_dg, sib, rows=1)]

        def keep(j, blk0):
            pair = pl.ds(pl.multiple_of(blk0, 2), 2)
            return pltpu.make_async_copy(wt.at[pair], wfull_ref.at[pair], loc.at[j])

        @pl.when((kk == 0) & (i == 0))
        def _():
            _peer_barrier("sxy")
            _cast_rows(wt.at[me], sh_ref)
            for s_ in sends()[0:3]:
                s_.start()

            def tables(j, _):
                chunk = pl.ds(pl.multiple_of(j * TM, TM), TM)
                c_ref[chunk, :], s1_ref[chunk, :], s2_ref[chunk, :] = _rope_tables(pos_ref[chunk, :], f_ref[...])
                return 0

            lax.fori_loop(0, S // TM, tables, 0)
            cp(0, me + flip, here).wait_recv()
            keep(0, me - c).start()

        @pl.when((kk == 1) & (i == 0))
        def _():
            cp(1, b_xn, here).wait_recv()
            sends()[5].start()
            sends()[3].start()
            cp(2, b_yn, here).wait_recv()
            sends()[6].start()
            sends()[4].start()
            cp(3, b_xn + flip, here).wait_recv()
            keep(1, b_xn - c).start()

        @pl.when((kk == 2) & (i == 0))
        def _():
            cp(4, b_yn + flip, here).wait_recv()
            keep(2, b_yn - c).start()

        @pl.when((kk == 3) & (i == 0))
        def _():
            cp(5, b_dg, here, rows=0).wait_recv()
            sends()[7].start()
            cp(6, b_dg, here, rows=1).wait_recv()
            sends()[8].start()
            cp(7, b_dg + flip, here, rows=0).wait_recv()
            cp(8, b_dg + flip, here, rows=1).wait_recv()
            keep(3, b_dg - c).start()

        rows = pl.ds(pl.multiple_of(i * tm, tm), tm)

        @pl.when(kk == 0)
        def _():
            xv = x_ref[...]
            r = lax.rsqrt(jnp.mean(xv * xv, axis=-1, keepdims=True) + EPS)
            hb = (xv * r * nw_ref[...]).astype(BF16)
            h_ref[...] = hb
            hs[rows, :] = hb

        h = hs[rows, :]
        chip = ord_ref[kk]
        for cc in range(2):
            p_ref[:, cc * R:(cc + 1) * R] = _dot_nt(h, wt[2 * chip + cc])

        @pl.when((kk == 3) & (i == nt - 1))
        def _():
            for s_ in sends():
                s_.wait_send()
            for j, blk in enumerate((me, b_xn, b_yn, b_dg)):
                keep(j, blk - c).wait()

    def first_pass(kk, i):
        return jnp.where(kk == 0, i, nt - 1)

    grid_spec = pltpu.PrefetchScalarGridSpec(
        num_scalar_prefetch=1, grid=(4, nt),
        in_specs=[pl.BlockSpec((tm, D), lambda kk, i, o: (first_pass(kk, i), 0)),
                  pl.BlockSpec((1, D), lambda kk, i, o: (0, 0)), VMEM, VMEM,
                  pl.BlockSpec((1, LANES), lambda kk, i, o: (0, 0))],
        out_specs=[pl.BlockSpec((tm, D), lambda kk, i, o: (first_pass(kk, i), 0)),
                   pl.BlockSpec((tm, 2 * R), lambda kk, i, o: (i, o[kk])), ANY]
        + [pl.BlockSpec((S, LANES), lambda kk, i, o: (0, 0))] * 3,
        scratch_shapes=[pltpu.VMEM((NDEV, R, D), BF16), pltpu.VMEM((S, D), BF16), _sems(9), _sems(9), _sems(4)])
    res = pl.pallas_call(
        body, name="in_proj_gather", grid_spec=grid_spec,
        out_shape=[jax.ShapeDtypeStruct((S, D), BF16), jax.ShapeDtypeStruct((S, INW), F32),
                   jax.ShapeDtypeStruct((NDEV, R, D), BF16)] + [jax.ShapeDtypeStruct((S, LANES), F32)] * 3,
        compiler_params=_cp(dimension_semantics=("arbitrary", "arbitrary"), collective_id=BARRIER_IDS["sxy"]),
    )(chip_order, x, norm_w, shard_t, pos_col, _inv_freq_lanes())
    return res[0], res[1], res[2], tuple(res[3:])


def _qk_specs():
    nb = QKV // LANES
    return [pl.BlockSpec((S, LANES), functools.partial(lambda hp, g, o: (0, o + g * 4 + hp), o=o))
            for o in (OFF_Q // LANES, OFF_K // LANES, OFF_V // LANES)]


def _tab_specs():
    return [pl.BlockSpec((S, LANES), lambda hp, g: (0, 0), pipeline_mode=pl.Buffered(1))] * 3


def _vec_spec():
    return pl.BlockSpec((1, LANES), lambda hp, g: (0, 0))


def _sub_rows(r, d, start, n):
    if d == 1:
        return pl.ds(start, n)
    return pl.ds(r + d * start, n, stride=d)


def _band_window(i, L):
    W = min(TQ + 2 * HALF_SPAN, L)
    q0 = pl.multiple_of(i * TQ, TQ)
    k0 = pl.multiple_of(jnp.clip(q0 - HALF_SPAN, 0, L - W), HALF_SPAN)
    qpos = q0 + (lax.broadcasted_iota(jnp.int32, (2 * TQ, W), 0) & (TQ - 1))
    kpos = k0 + lax.broadcasted_iota(jnp.int32, (2 * TQ, W), 1)
    valid = jnp.abs(qpos - kpos) <= HALF_SPAN
    return W, q0, k0, valid


def _stack_heads(t, lo):
    z = jnp.zeros_like(t)
    return jnp.concatenate([jnp.where(lo, t, z), jnp.where(lo, z, t)], axis=0)


def _unstack_heads(t2, lo):
    return jnp.where(lo, t2[0:TQ], t2[TQ:2 * TQ])


CHAINS = 8


def _interleave(d):
    ru = min(d, CHAINS)
    return ru, min(CHAINS // ru, S // d // TQ)


def _for_blocks(n, fn):
    if n == 1:
        fn(0)
    else:
        def it(j, _):
            fn(j)
            return 0
        lax.fori_loop(0, n, it, 0)


def attn_fwd(proj, tabs, qw2, kw2, sides=()):
    CH = 256

    def body(q_ref, k_ref, v_ref, c_ref, s1_ref, s2_ref, qw_ref, kw_ref, at_ref, ls_ref,
             qs, ks, vs, osub, lsub, onat, lnat, qn, kn):
        g = pl.program_id(1)
        lo = lax.broadcasted_iota(jnp.int32, (1, LANES), 1) < HD
        e = _head_mat()

        def prep(i, _):
            rows = pl.ds(pl.multiple_of(i * CH, CH), CH)
            c, s1, s2 = c_ref[rows, :], s1_ref[rows, :], s2_ref[rows, :]
            for t_ref, w_ref, out, scale in ((q_ref, qw_ref, qn, HD ** -0.5), (k_ref, kw_ref, kn, 1.0)):
                t = t_ref[rows, :]
                r = lax.rsqrt(_head_mean(t * t, e) + EPS)
                out[rows, :] = _rope(t * r * w_ref[...], c, s1, s2) * scale
            return 0

        lax.fori_loop(0, S // CH, prep, 0, unroll=4)

        def group(gi, d):
            L = S // d

            ru, nb = _interleave(d)

            def stage(r, off):
                for c0 in range(0, L, CH):
                    n = min(CH, L)
                    rows = _sub_rows(r, d, c0, n)
                    dst = pl.ds(off + c0, n)
                    qs[dst, :] = qn[rows, :].astype(BF16)
                    ks[dst, :] = kn[rows, :].astype(BF16)
                    vs[dst, :] = v_ref[rows, :].astype(BF16)

            def one(off, i):
                W, q0, k0, valid = _band_window(i, L)
                q2 = _stack_heads(qs[pl.ds(off + q0, TQ), :], lo)
                sc = jnp.where(valid, _dot_nt(q2, ks[pl.ds(off + k0, W), :]), NEG_INF)
                m = jnp.max(sc, axis=-1, keepdims=True)
                p = jnp.exp(sc - m)
                den = jnp.sum(p, axis=-1, keepdims=True)
                o2 = _dot(p.astype(BF16), vs[pl.ds(off + k0, W), :]) / den
                l2 = jnp.broadcast_to(m + jnp.log(den), (2 * TQ, LANES))
                osub[pl.ds(off + q0, TQ), :] = _unstack_heads(o2, lo)
                lsub[pl.ds(off + q0, TQ), :] = _unstack_heads(l2, lo)

            def unstage(r, off):
                for c0 in range(0, L, CH):
                    n = min(CH, L)
                    rows = _sub_rows(r, d, c0, n)
                    onat[gi, rows, :] = osub[pl.ds(off + c0, n), :]
                    lnat[gi, rows, :] = lsub[pl.ds(off + c0, n), :]

            def step(t, _):
                for u in range(ru):
                    stage(t * ru + u, u * L)
                _for_blocks(L // TQ // nb, lambda j: [one(u * L, j * nb + b) for u in range(ru) for b in range(nb)])
                for u in range(ru):
                    unstage(t * ru + u, u * L)
                return 0

            lax.fori_loop(0, d // ru, step, 0)

        for gi, d in enumerate(DILATIONS):
            pl.when(g == gi)(functools.partial(group, gi, d))

        @pl.when(g == len(DILATIONS) - 1)
        def _():
            def mix(i, _):
                rows = pl.ds(pl.multiple_of(i * CH, CH), CH)
                l0, l1, l2 = lnat[0, rows, :], lnat[1, rows, :], lnat[2, rows, :]
                m = jnp.maximum(jnp.maximum(l0, l1), l2)
                e0, e1, e2 = jnp.exp(l0 - m), jnp.exp(l1 - m), jnp.exp(l2 - m)
                den = e0 + e1 + e2
                a = (e0 * onat[0, rows, :] + e1 * onat[1, rows, :] + e2 * onat[2, rows, :]) / den
                at_ref[rows, :] = a.astype(BF16)
                ls_ref[rows, :] = m + jnp.log(den)
                return 0

            lax.fori_loop(0, S // CH, mix, 0)

    out_spec = pl.BlockSpec((S, LANES), lambda hp, g: (0, hp))
    return _call(
        body, sides, name="attn_fwd", grid=(4, 3),
        in_specs=_qk_specs() + _tab_specs() + [_vec_spec(), _vec_spec()],
        out_specs=[out_spec, out_spec],
        out_shape=[jax.ShapeDtypeStruct((S, CC), BF16), jax.ShapeDtypeStruct((S, CC), F32)],
        scratch_shapes=[pltpu.VMEM((S, LANES), BF16)] * 3 + [pltpu.VMEM((S, LANES), F32)] * 2
        + [pltpu.VMEM((3, S, LANES), F32)] * 2 + [pltpu.VMEM((S, LANES), F32)] * 2,
        args=(proj, proj, proj, *tabs, qw2, kw2))


def attn_bwd(proj, tabs, qw2, kw2, d_attn, attn, lse, sides=()):
    CH = 256

    def body(q_ref, k_ref, v_ref, c_ref, s1_ref, s2_ref, qw_ref, kw_ref, do_ref, at_ref, ls_ref,
             dq_ref, dk_ref, dv_ref, gqw_ref, gkw_ref,
             qs, ks, vs, dos, dsub, lsub, dqs, dks, dvs, dnat, qx, kx, dvn, tnq, tnk, rrq, rrk):
        hp, g = pl.program_id(0), pl.program_id(1)
        lo = lax.broadcasted_iota(jnp.int32, (1, LANES), 1) < HD
        e = _head_mat()
        both = ((q_ref, qw_ref, qx, tnq, rrq, HD ** -0.5), (k_ref, kw_ref, kx, tnk, rrk, 1.0))

        @pl.when((hp == 0) & (g == 0))
        def _():
            gqw_ref[...] = jnp.zeros_like(gqw_ref)
            gkw_ref[...] = jnp.zeros_like(gkw_ref)

        def prep(i, _):
            rows = pl.ds(pl.multiple_of(i * CH, CH), CH)
            dnat[rows, :] = _head_mean(do_ref[rows, :] * at_ref[rows, :].astype(F32), e) * float(HD)
            c, s1, s2 = c_ref[rows, :], s1_ref[rows, :], s2_ref[rows, :]
            for t_ref, w_ref, x, tn_s, rr_s, scale in both:
                t = t_ref[rows, :]
                rr = lax.rsqrt(_head_mean(t * t, e) + EPS)
                tn = t * rr
                rr_s[rows, :] = rr
                tn_s[rows, :] = tn
                x[rows, :] = _rope(tn * w_ref[...], c, s1, s2) * scale
            return 0

        lax.fori_loop(0, S // CH, prep, 0, unroll=4)

        def group(d):
            L = S // d

            ru, nb = _interleave(d)

            def stage(r, off):
                for c0 in range(0, L, CH):
                    n = min(CH, L)
                    rows = _sub_rows(r, d, c0, n)
                    dst = pl.ds(off + c0, n)
                    qs[dst, :] = qx[rows, :].astype(BF16)
                    ks[dst, :] = kx[rows, :].astype(BF16)
                    vs[dst, :] = v_ref[rows, :].astype(BF16)
                    dos[dst, :] = do_ref[rows, :].astype(BF16)
                    dsub[dst, :] = dnat[rows, :]
                    lsub[dst, :] = ls_ref[rows, :]
                    dks[dst, :] = jnp.zeros((n, LANES), F32)
                    dvs[dst, :] = jnp.zeros((n, LANES), F32)

            def one(off, i):
                W, q0, k0, valid = _band_window(i, L)
                qrows, krows = pl.ds(off + q0, TQ), pl.ds(off + k0, W)
                q2 = _stack_heads(qs[qrows, :], lo)
                do2 = _stack_heads(dos[qrows, :], lo)
                kk, vv = ks[krows, :], vs[krows, :]
                lse_b, dd_b = lsub[qrows, :], dsub[qrows, :]
                lse2 = jnp.concatenate([lse_b[:, 0:1], lse_b[:, HD:HD + 1]], axis=0)
                dd2 = jnp.concatenate([dd_b[:, 0:1], dd_b[:, HD:HD + 1]], axis=0)
                sc = jnp.where(valid, _dot_nt(q2, kk), NEG_INF)
                p = jnp.exp(sc - lse2)
                ds = (p * (_dot_nt(do2, vv) - dd2)).astype(BF16)
                dqs[qrows, :] = _unstack_heads(_dot(ds, kk), lo)
                dks[krows, :] = dks[krows, :] + _dot_tn(ds, q2)
                dvs[krows, :] = dvs[krows, :] + _dot_tn(p.astype(BF16), do2)

            def unstage(r, off):
                for c0 in range(0, L, CH):
                    n = min(CH, L)
                    rows = _sub_rows(r, d, c0, n)
                    src = pl.ds(off + c0, n)
                    qx[rows, :] = dqs[src, :]
                    kx[rows, :] = dks[src, :]
                    dvn[rows, :] = dvs[src, :]

            def step(t, _):
                for u in range(ru):
                    stage(t * ru + u, u * L)
                _for_blocks(L // TQ // nb, lambda j: [one(u * L, j * nb + b) for u in range(ru) for b in range(nb)])
                for u in range(ru):
                    unstage(t * ru + u, u * L)
                return 0

            lax.fori_loop(0, d // ru, step, 0)

        for gi, d in enumerate(DILATIONS):
            pl.when(g == gi)(functools.partial(group, d))

        def emit(i, _):
            rows = pl.ds(pl.multiple_of(i * CH, CH), CH)
            c, s1, s2 = c_ref[rows, :], s1_ref[rows, :], s2_ref[rows, :]
            for (_, w_ref, x, tn_s, rr_s, scale), out, gw_ref in zip(both, (dq_ref, dk_ref), (gqw_ref, gkw_ref)):
                tn = tn_s[rows, :]
                dy = _rope_t(x[rows, :] * scale, c, s1, s2)
                gw_ref[0:1, :] = gw_ref[0:1, :] + jnp.sum(dy * tn, axis=0, keepdims=True)
                dtn = dy * w_ref[...]
                out[rows, :] = (rr_s[rows, :] * (dtn - tn * _head_mean(dtn * tn, e))).astype(BF16)
            dv_ref[rows, :] = dvn[rows, :].astype(BF16)
            return 0

        lax.fori_loop(0, S // CH, emit, 0, unroll=4)

    nat_spec = pl.BlockSpec((S, LANES), lambda hp, g: (0, hp))
    out_spec = pl.BlockSpec((None, S, LANES), lambda hp, g: (g, 0, hp))
    acc_spec = pl.BlockSpec((8, LANES), lambda hp, g: (0, 0))
    return _call(
        body, sides, name="attn_bwd", grid=(4, 3),
        in_specs=_qk_specs() + _tab_specs() + [_vec_spec(), _vec_spec(), nat_spec, nat_spec, nat_spec],
        out_specs=[out_spec] * 3 + [acc_spec] * 2,
        out_shape=[jax.ShapeDtypeStruct((QKV // PLANE, S, PLANE), BF16)] * 3 + [jax.ShapeDtypeStruct((8, LANES), F32)] * 2,
        scratch_shapes=[pltpu.VMEM((S, LANES), BF16)] * 4 + [pltpu.VMEM((S, LANES), F32)] * 13,
        args=(proj, proj, proj, *tabs, qw2, kw2, d_attn, attn, lse))


PADR = 16
CT = 128


def _conv_specs():
    return [pl.BlockSpec((S, CC), lambda i: (0, OFF_CA // CC)), pl.BlockSpec((S, CC), lambda i: (0, OFF_CB // CC))]


NCB = CC // LANES


def _pad_zero(pad):
    for cb in range(NCB):
        pad[cb, 0:PADR, :] = jnp.zeros((PADR, LANES), F32)
        pad[cb, PADR + S:PADR + S + PADR, :] = jnp.zeros((PADR, LANES), F32)


def _pad_store(pad, row0, n, val):
    for cb in range(NCB):
        pad[cb, pl.ds(pl.multiple_of(row0 + PADR, 8), n), :] = val[:, cb * LANES:(cb + 1) * LANES]


def _taps(pad_ref, cb, s0, weights):
    acc = jnp.zeros((CT, LANES), F32)
    for k in range(KW):
        acc = acc + weights[k] * pad_ref[cb, pl.ds(s0 + k + 1, CT), :]
    return acc


def conv_fwd(proj, conv_w, conv_b, ln_w, ln_b, sides=()):
    def body(a_ref, b_ref, w_ref, cb_ref, lw_ref, lb_ref, c_ref, u3_ref, upad):
        _pad_zero(upad)

        def glu(i, _):
            rows = pl.ds(pl.multiple_of(i * TM, TM), TM)
            _pad_store(upad, i * TM, TM, a_ref[rows, :] * _sigmoid(b_ref[rows, :]))
            return 0

        lax.fori_loop(0, S // TM, glu, 0)

        def chunk(i, _):
            s0 = pl.multiple_of(i * CT, CT)
            for cb in range(CC // LANES):
                cols = slice(cb * LANES, (cb + 1) * LANES)
                w = [w_ref[k:k + 1, cols] for k in range(KW)]
                c_ref[pl.ds(s0, CT), cols] = _taps(upad, cb, s0, w) + cb_ref[:, cols]
            cv = c_ref[pl.ds(s0, CT), :]
            mu = jnp.mean(cv, axis=-1, keepdims=True)
            xc = cv - mu
            rstd = lax.rsqrt(jnp.mean(xc * xc, axis=-1, keepdims=True) + EPS)
            yl = xc * rstd * lw_ref[...] + lb_ref[...]
            u3_ref[pl.ds(s0, CT), :] = (yl * _sigmoid(yl)).astype(BF16)
            return 0

        lax.fori_loop(0, S // CT, chunk, 0)

    vec = pl.BlockSpec((1, CC), lambda i: (0, 0))
    full = pl.BlockSpec((S, CC), lambda i: (0, 0))
    return _call(
        body, sides, name="conv_fwd", grid=(1,),
        in_specs=_conv_specs() + [pl.BlockSpec((KW, CC), lambda i: (0, 0)), vec, vec, vec],
        out_specs=[full, full],
        out_shape=[jax.ShapeDtypeStruct((S, CC), F32), jax.ShapeDtypeStruct((S, CC), BF16)],
        scratch_shapes=[pltpu.VMEM((NCB, S + 2 * PADR, LANES), F32)],
        args=(proj, proj, conv_w, conv_b, ln_w, ln_b))


def conv_bwd(proj, cpre, d_u3, conv_w, conv_w_rev, ln_w, ln_b, sides=()):
    def body(a_ref, b_ref, c_ref, du3_ref, w_ref, wr_ref, lw_ref, lb_ref,
             dc_ref, gw_ref, gcb_ref, glw_ref, glb_ref, upad, dpad):
        _pad_zero(upad)
        _pad_zero(dpad)
        gw_ref[...] = jnp.zeros_like(gw_ref)

        def ln_bwd(i, carry):
            gcb, glw, glb = carry
            rows = pl.ds(pl.multiple_of(i * TM, TM), TM)
            _pad_store(upad, i * TM, TM, a_ref[rows, :] * _sigmoid(b_ref[rows, :]))
            cv = c_ref[rows, :]
            mu = jnp.mean(cv, axis=-1, keepdims=True)
            xc = cv - mu
            rstd = lax.rsqrt(jnp.mean(xc * xc, axis=-1, keepdims=True) + EPS)
            xh = xc * rstd
            yl = xh * lw_ref[...] + lb_ref[...]
            dyl = du3_ref[rows, :] * _dsilu(yl, _sigmoid(yl))
            dxh = dyl * lw_ref[...]
            dcv = rstd * (dxh - jnp.mean(dxh, axis=-1, keepdims=True)
                          - xh * jnp.mean(dxh * xh, axis=-1, keepdims=True))
            _pad_store(dpad, i * TM, TM, dcv)
            return (gcb + jnp.sum(dcv, axis=0, keepdims=True),
                    glw + jnp.sum(dyl * xh, axis=0, keepdims=True),
                    glb + jnp.sum(dyl, axis=0, keepdims=True))

        z = jnp.zeros((1, CC), F32)
        gcb, glw, glb = lax.fori_loop(0, S // TM, ln_bwd, (z, z, z))
        gcb_ref[...] = gcb
        glw_ref[...] = glw
        glb_ref[...] = glb

        def chunk(i, _):
            s0 = pl.multiple_of(i * CT, CT)
            for cb in range(CC // LANES):
                cols = slice(cb * LANES, (cb + 1) * LANES)
                wr = [wr_ref[k:k + 1, cols] for k in range(KW)]
                du = _taps(dpad, cb, s0, wr)
                dcv = dpad[cb, pl.ds(s0 + PADR, CT), :]
                for k in range(KW):
                    gw_ref[k:k + 1, cols] = gw_ref[k:k + 1, cols] + jnp.sum(
                        upad[cb, pl.ds(s0 + k + 1, CT), :] * dcv, axis=0, keepdims=True)
                av = a_ref[pl.ds(s0, CT), cols]
                sb = _sigmoid(b_ref[pl.ds(s0, CT), cols])
                dc_ref[0, pl.ds(s0, CT), cols] = (du * sb).astype(BF16)
                dc_ref[1, pl.ds(s0, CT), cols] = (du * av * sb * (1.0 - sb)).astype(BF16)
            return 0

        lax.fori_loop(0, S // CT, chunk, 0)

    vec = pl.BlockSpec((1, CC), lambda i: (0, 0))
    full = pl.BlockSpec((S, CC), lambda i: (0, 0))
    wsp = pl.BlockSpec((KW, CC), lambda i: (0, 0))
    return _call(
        body, sides, name="conv_bwd", grid=(1,),
        in_specs=_conv_specs() + [full, full, wsp, wsp, vec, vec],
        out_specs=[pl.BlockSpec((2, S, CC), lambda i: (0, 0, 0)), wsp, vec, vec, vec],
        out_shape=[jax.ShapeDtypeStruct((2, S, CC), BF16), jax.ShapeDtypeStruct((KW, CC), F32)]
        + [jax.ShapeDtypeStruct((1, CC), F32)] * 3,
        scratch_shapes=[pltpu.VMEM((NCB, S + 2 * PADR, LANES), F32)] * 2,
        args=(proj, proj, cpre, d_u3, conv_w, conv_w_rev, ln_w, ln_b))


def _gate_specs():
    return [_row(CC, col=OFF_GA // CC + j) for j in range(4)]


def _gates(g_refs, bg_ref):
    ga = _sigmoid(jnp.concatenate([g_refs[0][...], g_refs[1][...]], axis=1) + bg_ref[0:1, :])
    gb = _sigmoid(jnp.concatenate([g_refs[2][...], g_refs[3][...]], axis=1) + bg_ref[1:2, :])
    return ga, gb


def mix_out(x, proj, b_gate, attn, u3, w_o, w_pw, w_out):
    def body(x_ref, g0, g1, g2, g3, bg_ref, at_ref, u3_ref, wo_ref, wp_ref, wout_ref,
             x1_ref, z_ref, ya_ref, yb_ref):
        ga, gb = _gates((g0, g1, g2, g3), bg_ref)
        ya = _dot_nt(at_ref[...], wo_ref[...])
        yb = _dot_nt(u3_ref[...], wp_ref[...])
        z = (ga * ya + gb * yb).astype(BF16)
        ya_ref[...] = ya.astype(BF16)
        yb_ref[...] = yb.astype(BF16)
        z_ref[...] = z
        x1_ref[...] = x_ref[...] + _dot(z, wout_ref[...])

    return pl.pallas_call(
        body, name="mix_out", grid=(S // TM,),
        in_specs=[_row(D)] + _gate_specs() + [_res((2, D)), _row(CC), _row(CC),
                                              _res((D, CC)), _res((D, CC)), _res((D, D))],
        out_specs=[_row(D)] * 4,
        out_shape=[jax.ShapeDtypeStruct((S, D), F32)] + [jax.ShapeDtypeStruct((S, D), BF16)] * 3,
        compiler_params=_cp(dimension_semantics=("arbitrary",)),
    )(x, proj, proj, proj, proj, b_gate, attn, u3, w_o, w_pw, w_out)


def out_bwd(d_x1b, proj, b_gate, ya, yb, w_o, w_pw, w_out, sides=()):
    def body(dx_ref, g0, g1, g2, g3, bg_ref, ya_ref, yb_ref, wo_ref, wp_ref, wout_ref,
             dya_ref, dyb_ref, dgl_ref, dat_ref, du3_ref, gbg_ref):
        @pl.when(pl.program_id(0) == 0)
        def _():
            gbg_ref[...] = jnp.zeros_like(gbg_ref)

        ga, gb = _gates((g0, g1, g2, g3), bg_ref)
        dz = _dot_nt(dx_ref[...], wout_ref[...])
        dya = (dz * ga).astype(BF16)
        dyb = (dz * gb).astype(BF16)
        dgla = dz * ya_ref[...].astype(F32) * ga * (1.0 - ga)
        dglb = dz * yb_ref[...].astype(F32) * gb * (1.0 - gb)
        dya_ref[...] = dya
        dyb_ref[...] = dyb
        for j in range(2):
            dgl_ref[j] = dgla[:, j * PLANE:(j + 1) * PLANE].astype(BF16)
            dgl_ref[2 + j] = dglb[:, j * PLANE:(j + 1) * PLANE].astype(BF16)
        gbg_ref[0:1, :] = gbg_ref[0:1, :] + jnp.sum(dgla, axis=0, keepdims=True)
        gbg_ref[1:2, :] = gbg_ref[1:2, :] + jnp.sum(dglb, axis=0, keepdims=True)
        dat_ref[...] = _dot(dya, wo_ref[...])
        du3_ref[...] = _dot(dyb, wp_ref[...])

    return _call(
        body, sides, name="out_bwd", grid=(S // TM,),
        in_specs=[_row(D)] + _gate_specs() + [_res((2, D)), _row(D), _row(D),
                                              _res((D, CC)), _res((D, CC)), _res((D, D))],
        out_specs=[_row(D), _row(D), _planes(2 * D), _row(CC), _row(CC), pl.BlockSpec((2, D), lambda i: (0, 0))],
        out_shape=[jax.ShapeDtypeStruct((S, D), BF16)] * 2 + [jax.ShapeDtypeStruct((2 * D // PLANE, S, PLANE), BF16)]
        + [jax.ShapeDtypeStruct((S, CC), F32)] * 2 + [jax.ShapeDtypeStruct((2, D), F32)],
        args=(d_x1b, proj, proj, proj, proj, b_gate, ya, yb, w_o, w_pw, w_out))


def ffn_in(x1, norm_w, w_ffn_in, sides=()):
    half = FF // 2

    def body(x_ref, nw_ref, w_ref, h_ref, gu_ref, f_ref):
        xv = x_ref[...]
        r = lax.rsqrt(jnp.mean(xv * xv, axis=-1, keepdims=True) + EPS)
        h = (xv * r * nw_ref[...]).astype(BF16)
        h_ref[...] = h
        for j in range(2):
            gt = _dot_nt(h, w_ref[j * half:(j + 1) * half, :])
            up = _dot_nt(h, w_ref[FF + j * half:FF + (j + 1) * half, :])
            gu_ref[:, j * half:(j + 1) * half] = gt.astype(BF16)
            gu_ref[:, FF + j * half:FF + (j + 1) * half] = up.astype(BF16)
            f_ref[:, j * half:(j + 1) * half] = (gt * _sigmoid(gt) * up).astype(BF16)

    return _call(
        body, sides, name="ffn_in", grid=(S // TM,),
        in_specs=[_row(D), _res((1, D)), _res((2 * FF, D))],
        out_specs=[_row(D), _row(2 * FF), _row(FF)],
        out_shape=[jax.ShapeDtypeStruct((S, D), BF16), jax.ShapeDtypeStruct((S, 2 * FF), BF16),
                   jax.ShapeDtypeStruct((S, FF), BF16)],
        args=(x1, norm_w, w_ffn_in))


def ffn_out_loss(x1, f, w_ffn_out, target):
    def body(x_ref, f_ref, w_ref, t_ref, dy_ref, dyb_ref, sq_ref):
        @pl.when(pl.program_id(0) == 0)
        def _():
            sq_ref[...] = jnp.zeros_like(sq_ref)

        diff = x_ref[...] + _dot(f_ref[...], w_ref[...]) - t_ref[...]
        dy = diff * (1.0 / D)
        dy_ref[...] = dy
        dyb_ref[...] = dy.astype(BF16)
        sq_ref[...] = sq_ref[...] + jnp.sum((diff * diff).reshape(TM // 8, 8, D), axis=0)

    return pl.pallas_call(
        body, name="ffn_out_loss", grid=(S // TM,),
        in_specs=[_row(D), _row(FF), _res((FF, D)), _row(D)],
        out_specs=[_row(D), _row(D), pl.BlockSpec((8, D), lambda i: (0, 0))],
        out_shape=[jax.ShapeDtypeStruct((S, D), F32), jax.ShapeDtypeStruct((S, D), BF16),
                   jax.ShapeDtypeStruct((8, D), F32)],
        compiler_params=_cp(dimension_semantics=("arbitrary",)),
    )(x1, f, w_ffn_out, target)


def _rms_bwd(xv, nw, dh):
    r = lax.rsqrt(jnp.mean(xv * xv, axis=-1, keepdims=True) + EPS)
    xn = xv * r
    dxn = dh * nw
    dx = r * (dxn - xn * jnp.mean(dxn * xn, axis=-1, keepdims=True))
    return dx, dh * xn


def ffn_bwd(dy, dyb, gu, x1, norm_w, w_ffn_in, w_ffn_out, sides=()):
    def body(dy_ref, dyb_ref, gu_ref, x_ref, nw_ref, wi_ref, wo_ref, dgu_ref, dx_ref, dxb_ref, gn_ref):
        @pl.when(pl.program_id(0) == 0)
        def _():
            gn_ref[...] = jnp.zeros_like(gn_ref)

        df = _dot_nt(dyb_ref[...], wo_ref[...])
        gt = gu_ref[:, 0:FF].astype(F32)
        up = gu_ref[:, FF:2 * FF].astype(F32)
        sg = _sigmoid(gt)
        dgt = (df * up * _dsilu(gt, sg)).astype(BF16)
        dup = (df * gt * sg).astype(BF16)
        dgu_ref[:, 0:FF] = dgt
        dgu_ref[:, FF:2 * FF] = dup
        dh = _dot(dgt, wi_ref[0:FF, :]) + _dot(dup, wi_ref[FF:2 * FF, :])
        dxn, gw = _rms_bwd(x_ref[...], nw_ref[...], dh)
        dx = dy_ref[...] + dxn
        dx_ref[...] = dx
        dxb_ref[...] = dx.astype(BF16)
        gn_ref[...] = gn_ref[...] + jnp.sum(gw, axis=0, keepdims=True)

    return _call(
        body, sides, name="ffn_bwd", grid=(S // TM,),
        in_specs=[_row(D), _row(D), _row(2 * FF), _row(D), _res((1, D)), _res((2 * FF, D)), _res((FF, D))],
        out_specs=[_row(2 * FF), _row(D), _row(D), pl.BlockSpec((1, D), lambda i: (0, 0))],
        out_shape=[jax.ShapeDtypeStruct((S, 2 * FF), BF16), jax.ShapeDtypeStruct((S, D), F32),
                   jax.ShapeDtypeStruct((S, D), BF16), jax.ShapeDtypeStruct((1, D), F32)],
        args=(dy, dyb, gu, x1, norm_w, w_ffn_in, w_ffn_out))


def in_bwd(d_q, d_k, d_v, d_conv, d_gl, w_in, x, d_x1, norm_w, sides=()):
    segs = ((OFF_Q, QKV), (OFF_K, QKV), (OFF_V, QKV), (OFF_CA, 2 * CC), (OFF_GA, 2 * D))

    def body(dq_ref, dk_ref, dv_ref, dc_ref, dg_ref, w_ref, x_ref, dx1_ref, nw_ref, gx_ref, gn_ref):
        @pl.when(pl.program_id(0) == 0)
        def _():
            gn_ref[...] = jnp.zeros_like(gn_ref)

        dh = jnp.zeros((TM, D), F32)
        for ref, (off, width) in zip((dq_ref, dk_ref, dv_ref, dc_ref, dg_ref), segs):
            for j in range(width // PLANE):
                dh = dh + _dot(ref[j], w_ref[off + j * PLANE:off + (j + 1) * PLANE, :])
        dxn, gw = _rms_bwd(x_ref[...], nw_ref[...], dh)
        gx_ref[...] = dx1_ref[...] + dxn
        gn_ref[...] = gn_ref[...] + jnp.sum(gw, axis=0, keepdims=True)

    return _call(
        body, sides, name="in_bwd", grid=(S // TM,),
        in_specs=[_planes(QKV)] * 3 + [_planes(2 * CC), _planes(2 * D), _res((INW, D)), _row(D), _row(D), _res((1, D))],
        out_specs=[_row(D), pl.BlockSpec((1, D), lambda i: (0, 0))],
        out_shape=[jax.ShapeDtypeStruct((S, D), F32), jax.ShapeDtypeStruct((1, D), F32)],
        args=(d_q, d_k, d_v, d_conv, d_gl, w_in, x, d_x1, norm_w))


def mm_tn(name, pairs, tm):
    n = len(pairs)
    M = pairs[0][0].shape[1]
    widths = [b.shape[1] for _, b in pairs]

    def body(*refs):
        for a_ref, b_ref, o_ref, ob_ref in zip(refs[0:2 * n:2], refs[1:2 * n:2], refs[2 * n::2], refs[2 * n + 1::2]):
            r = _dot_tn(a_ref[...], b_ref[...])
            o_ref[...] = r
            ob_ref[...] = r.astype(BF16)

    return _call(
        body, name=name, grid=(M // tm,),
        in_specs=[sp for N in widths for sp in (pl.BlockSpec((S, tm), lambda i: (0, i)), _res((S, N)))],
        out_specs=[pl.BlockSpec((tm, N), lambda i: (i, 0)) for N in widths for _ in range(2)],
        out_shape=[jax.ShapeDtypeStruct((M, N), dt) for N in widths for dt in (F32, BF16)],
        args=[t for pair in pairs for t in pair])


GW_IN_TN = PLANE
GW_IN_SPLIT = (768, 256)


def gw_in_t(name, h, d_segs, col0, hw, sides=()):
    tn = GW_IN_TN
    starts, t0 = [], 0
    for seg in d_segs:
        starts.append(t0)
        t0 += seg.shape[0]
    ntiles = [seg.shape[0] for seg in d_segs]

    def body(h_ref, *refs):
        a_refs, o_ref, ob_ref = refs[:-2], refs[-2], refs[-1]
        n = pl.program_id(0)
        for a_ref, st, nt in zip(a_refs, starts, ntiles):
            @pl.when((n >= st) & (n < st + nt))
            def _(a_ref=a_ref):
                r = _dot_tn(a_ref[...], h_ref[...])
                o_ref[...] = r
                ob_ref[...] = r.astype(BF16)

    def seg_spec(st, nt):
        return pl.BlockSpec((None, S, tn), lambda n: (jnp.clip(n - st, 0, nt - 1), 0, 0))

    res = _call(
        body, sides, name=name, grid=(INW // tn,),
        in_specs=[pl.BlockSpec((S, hw), lambda n: (0, col0 // hw))] + [seg_spec(st, nt) for st, nt in zip(starts, ntiles)],
        out_specs=[pl.BlockSpec((tn, hw), lambda n: (n, 0))] * 2,
        out_shape=[jax.ShapeDtypeStruct((INW, hw), F32), jax.ShapeDtypeStruct((INW, hw), BF16)],
        args=(h, *d_segs))
    return (res[0], res[1]) if sides else (res, [])


def _place():
    x, y, c = lax.axis_index("x"), lax.axis_index("y"), lax.axis_index("c")
    chips = [(1 - x, y), (x, 1 - y), (1 - x, 1 - y)]
    return x, y, c, chips


def _sems(n):
    return pltpu.SemaphoreType.DMA((n,))


def _remote(src, dst, send, recv, k, to):
    return pltpu.make_async_remote_copy(src_ref=src, dst_ref=dst, send_sem=send.at[k], recv_sem=recv.at[k],
                                        device_id=to, device_id_type=MESH)


def _cast_rows(dst, src, cols=slice(None)):
    rows = src.shape[0]
    step = next((s for s in (128, 64, 32, 16) if rows % s == 0), rows)
    for r0 in range(0, rows, step):
        dst[r0:r0 + step, cols] = src[r0:r0 + step, :].astype(dst.dtype)


def comm_only(name, sides):
    def body():
        pass

    return _call(body, sides, name=name, grid=(1,), in_specs=[], out_specs=[], out_shape=[], args=())[1]


def ag_blocks(shard, dtype):
    R, W = shard.shape

    def copy(outs, scr, k, block, to, src=None):
        dst = outs[0].at[block]
        return _remote(dst if src is None else src, dst, scr[1], scr[2], k, to)

    def local(outs, scr, me):
        return pltpu.make_async_copy(scr[0], outs[0].at[me], scr[3].at[0])

    def start(ins, outs, scr):
        x, y, c, chips = _place()
        me = 4 * x + 2 * y + c
        _cast_rows(scr[0], ins[0])
        local(outs, scr, me).start()
        copy(outs, scr, 0, me, (x, y, 1 - c), src=scr[0]).start()
        for j, (cx, cy) in enumerate(chips):
            copy(outs, scr, 1 + j, me, (cx, cy, c), src=scr[0]).start()

    def finish(ins, outs, scr):
        x, y, c, chips = _place()
        me, sib = 4 * x + 2 * y + c, (x, y, 1 - c)
        passed = []
        for j, (cx, cy) in enumerate(chips):
            theirs = 4 * cx + 2 * cy + c
            copy(outs, scr, 1 + j, theirs, (x, y, c)).wait_recv()
            fwd = copy(outs, scr, 4 + j, theirs, sib)
            fwd.start()
            passed.append(fwd)
        copy(outs, scr, 0, 4 * x + 2 * y + 1 - c, (x, y, c)).wait_recv()
        for j, (cx, cy) in enumerate(chips):
            copy(outs, scr, 4 + j, 4 * cx + 2 * cy + 1 - c, (x, y, c)).wait_recv()
        copy(outs, scr, 0, me, sib, src=scr[0]).wait_send()
        for j, (cx, cy) in enumerate(chips):
            copy(outs, scr, 1 + j, me, (cx, cy, c), src=scr[0]).wait_send()
        for fwd in passed:
            fwd.wait_send()
        local(outs, scr, me).wait()

    return Side((shard,), (VMEM,), (jax.ShapeDtypeStruct((NDEV, R, W), dtype),),
                (pltpu.VMEM((R, W), dtype), _sems(7), _sems(7), _sems(1)), start, finish, None, "dsxy")


def ag_blocks_relay(shard, dtype, transpose=False):
    R, W = shard.shape[::-1] if transpose else shard.shape
    half = R // 2

    def copy(outs, scr, k, block, to, src=None, rows=None):
        dst = outs[0].at[block] if rows is None else outs[0].at[block, pl.ds(rows * half, half), :]
        return _remote(dst if src is None else src, dst, scr[1], scr[2], k, to)

    def local(outs, scr, me):
        return pltpu.make_async_copy(scr[0], outs[0].at[me], scr[3].at[0])

    def own(outs, scr):
        x, y, c, _ = _place()
        me = 4 * x + 2 * y + c
        return [copy(outs, scr, k, me, to, src=scr[0])
                for k, to in enumerate([(x, y, 1 - c), (1 - x, y, c), (x, 1 - y, c)])]

    def start(ins, outs, scr):
        x, y, c, _ = _place()
        if transpose:
            scr[0][...] = ins[0][...].T.astype(dtype)
        else:
            _cast_rows(scr[0], ins[0])
        local(outs, scr, 4 * x + 2 * y + c).start()
        for cp in own(outs, scr):
            cp.start()

    def passed_on(outs, scr):
        x, y, c, _ = _place()
        sib, xn, yn = (x, y, 1 - c), (1 - x, y, c), (x, 1 - y, c)
        b_xn, b_yn, b_dg = 4 * (1 - x) + 2 * y + c, 4 * x + 2 * (1 - y) + c, 4 * (1 - x) + 2 * (1 - y) + c
        near = [copy(outs, scr, 5, b_xn, yn, rows=0), copy(outs, scr, 3, b_xn, sib),
                copy(outs, scr, 6, b_yn, xn, rows=1), copy(outs, scr, 4, b_yn, sib)]
        far = [copy(outs, scr, 7, b_dg, sib, rows=0), copy(outs, scr, 8, b_dg, sib, rows=1)]
        return (b_xn, b_yn, b_dg), near, far

    def mid(ins, outs, scr):
        x, y, c, _ = _place()
        (b_xn, b_yn, _), near, _ = passed_on(outs, scr)
        copy(outs, scr, 1, b_xn, (x, y, c)).wait_recv()
        near[0].start()
        near[1].start()
        copy(outs, scr, 2, b_yn, (x, y, c)).wait_recv()
        near[2].start()
        near[3].start()

    def finish(ins, outs, scr):
        x, y, c, _ = _place()
        here = (x, y, c)
        (b_xn, b_yn, b_dg), near, far = passed_on(outs, scr)
        copy(outs, scr, 5, b_dg, here, rows=0).wait_recv()
        far[0].start()
        copy(outs, scr, 6, b_dg, here, rows=1).wait_recv()
        far[1].start()
        flip = 1 - 2 * c
        copy(outs, scr, 0, 4 * x + 2 * y + 1 - c, here).wait_recv()
        copy(outs, scr, 3, b_xn + flip, here).wait_recv()
        copy(outs, scr, 4, b_yn + flip, here).wait_recv()
        copy(outs, scr, 7, b_dg + flip, here, rows=0).wait_recv()
        copy(outs, scr, 8, b_dg + flip, here, rows=1).wait_recv()
        for cp in own(outs, scr) + near + far:
            cp.wait_send()
        local(outs, scr, 4 * x + 2 * y + c).wait()

    return Side((shard,), (VMEM,), (jax.ShapeDtypeStruct((NDEV, R, W), dtype),),
                (pltpu.VMEM((R, W), dtype), _sems(9), _sems(9), _sems(1)), start, finish, mid, "sxy")


def copies_side(args, out_shape, n_copies, plan, peers):
    def copies(ins, outs, scr):
        return [_remote(s_, d_, scr[0], scr[1], i, to) for i, (s_, d_, to) in enumerate(plan(ins, outs))]

    def start(ins, outs, scr):
        for cp in copies(ins, outs, scr):
            cp.start()

    def finish(ins, outs, scr):
        for cp in copies(ins, outs, scr):
            cp.wait()

    return Side(tuple(args), (ANY,) * len(args), tuple(out_shape), (_sems(n_copies), _sems(n_copies)),
                start, finish, None, peers)


def rs_to_sibling(grads):
    out_shape = [jax.ShapeDtypeStruct((4,) + g.shape[1:], BF16) for g in grads]

    def plan(ins, outs):
        x, y, c, _ = _place()
        return [(g.at[2 * k + 1 - c], r.at[k], (x, y, 1 - c)) for g, r in zip(ins, outs) for k in range(4)]

    return copies_side(grads, out_shape, 4 * len(grads), plan, "s")


def rs_to_chips(parts):
    out_shape = [jax.ShapeDtypeStruct((3,) + p.shape[1:], BF16) for p in parts]

    def plan(ins, outs):
        x, y, c, chips = _place()
        return [(p.at[2 * cx + cy], r.at[j], (cx, cy, c))
                for p, r in zip(ins, outs) for j, (cx, cy) in enumerate(chips)]

    return copies_side(parts, out_shape, 3 * len(parts), plan, "dxy")


def rs_to_chips_combined(part):
    _, R, W = part.shape
    half = R // 2
    top, bot = pl.ds(0, half), pl.ds(half, half)

    def copies(ins, outs, scr):
        p, r = ins[0], outs[0]
        loc_a, loc_b, in_x, in_y, comb_a, comb_b, send, recv, loc = scr
        x, y, c, _ = _place()
        xn, yn = (1 - x, y, c), (x, 1 - y, c)
        k_xn, k_yn, k_dg = 2 * (1 - x) + y, 2 * x + 1 - y, 2 * (1 - x) + 1 - y
        direct = [_remote(p.at[k_xn, top, :], r.at[0, top, :], send, recv, 0, xn),
                  _remote(p.at[k_yn, bot, :], r.at[1, bot, :], send, recv, 1, yn),
                  _remote(p.at[k_dg, top, :], in_x, send, recv, 2, xn),
                  _remote(p.at[k_dg, bot, :], in_y, send, recv, 3, yn)]
        combined = [_remote(comb_a, r.at[1, top, :], send, recv, 4, yn),
                    _remote(comb_b, r.at[0, bot, :], send, recv, 5, xn)]
        local = [pltpu.make_async_copy(p.at[k_yn, top, :], loc_a, loc.at[0]),
                 pltpu.make_async_copy(p.at[k_xn, bot, :], loc_b, loc.at[1])]
        return direct, combined, local

    def start(ins, outs, scr):
        direct, _, local = copies(ins, outs, scr)
        for cp in local + direct:
            cp.start()

    def mid(ins, outs, scr):
        loc_a, loc_b, in_x, in_y, comb_a, comb_b = scr[:6]
        direct, combined, local = copies(ins, outs, scr)
        for mine, arrival, inbox, out, nxt in ((local[0], direct[2], in_x, comb_a, combined[0]),
                                               (local[1], direct[3], in_y, comb_b, combined[1])):
            mine.wait()
            arrival.wait_recv()
            src = loc_a if out is comb_a else loc_b
            out[...] = (src[...].astype(F32) + inbox[...].astype(F32)).astype(BF16)
            nxt.start()

    def finish(ins, outs, scr):
        direct, combined, _ = copies(ins, outs, scr)
        direct[0].wait_recv()
        direct[1].wait_recv()
        combined[0].wait_recv()
        combined[1].wait_recv()
        for cp in direct + combined:
            cp.wait_send()

    buf = pltpu.VMEM((half, W), BF16)
    return Side((part,), (ANY,), (jax.ShapeDtypeStruct((2, R, W), BF16),),
                (buf, buf, buf, buf, buf, buf, _sems(6), _sems(6), _sems(2)), start, finish, mid, "xy")


ADAM_TILE_BYTES = 3 * 512 * 1024


def _row_tiles(rows, width):
    return 2 if rows % 32 == 0 and rows * width * 4 > ADAM_TILE_BYTES else 1


def chip_sum(name, grads, recvs, c_idx, chip_idx):
    n = len(grads)

    def body(s_ref, *refs):
        k = pl.program_id(0)
        for g_ref, r_ref, p_ref, own_ref in zip(refs[:n], refs[n:2 * n], refs[2 * n::2], refs[2 * n + 1::2]):
            tot = g_ref[0] + r_ref[0].astype(F32)
            p_ref[0] = tot.astype(BF16)

            @pl.when(k == s_ref[1])
            def _(own_ref=own_ref, tot=tot):
                own_ref[...] = tot

    def block(g):
        return (1,) + g.shape[1:]

    grid_spec = pltpu.PrefetchScalarGridSpec(
        num_scalar_prefetch=1, grid=(4,),
        in_specs=[pl.BlockSpec(block(g), lambda k, s: (2 * k + s[0], 0, 0)) for g in grads]
        + [pl.BlockSpec(block(g), lambda k, s: (k, 0, 0)) for g in grads],
        out_specs=[sp for g in grads for sp in (pl.BlockSpec(block(g), lambda k, s: (k, 0, 0)),
                                                pl.BlockSpec(g.shape[1:], lambda k, s: (0, 0)))])
    res = pl.pallas_call(
        body, name=name, grid_spec=grid_spec,
        out_shape=[sh for g in grads for sh in (jax.ShapeDtypeStruct((4,) + g.shape[1:], BF16),
                                                jax.ShapeDtypeStruct(g.shape[1:], F32))],
        compiler_params=_cp(dimension_semantics=("arbitrary",)),
    )(jnp.stack([c_idx, chip_idx]), *grads, *recvs)
    return [(res[2 * j], res[2 * j + 1]) for j in range(n)]


def _adamw(w, g, m, v):
    m2 = ADAM_B1 * m + (1.0 - ADAM_B1) * g
    v2 = ADAM_B2 * v + (1.0 - ADAM_B2) * (g * g)
    m_hat = m2 / (1.0 - ADAM_B1 ** ADAM_STEP)
    v_hat = v2 / (1.0 - ADAM_B2 ** ADAM_STEP)
    delta = -ADAM_LR * (m_hat / (jnp.sqrt(v_hat) + ADAM_EPS) + ADAM_WD * w)
    return delta, m2, v2


def shard_adam(name, owns, recvs, w, m, v):
    n = len(owns)
    R = owns[0].shape[0]
    ct = min(o.shape[1] for o in owns)
    first = [sum(o.shape[1] for o in owns[:j]) // ct for j in range(n)]
    count = [o.shape[1] // ct for o in owns]
    nt = _row_tiles(R, ct)
    tr = R // nt

    def body(*refs):
        o_refs, r_refs = refs[:n], refs[n:2 * n]
        w_ref, m_ref, v_ref, g_ref, d_ref, nm_ref, nv_ref = refs[2 * n:]
        g = None
        for j in range(n):
            gj = o_refs[j][...]
            for q in range(recvs[j].shape[0]):
                gj = gj + r_refs[j][q].astype(F32)
            g = gj if g is None else jnp.where(pl.program_id(0) >= first[j], gj, g)
        delta, m2, v2 = _adamw(w_ref[...], g, m_ref[...], v_ref[...])
        g_ref[...] = g
        d_ref[...] = delta
        nm_ref[...] = m2
        nv_ref[...] = v2

    def part(j):
        return pl.BlockSpec((tr, ct), lambda k, i: (i, jnp.clip(k - first[j], 0, count[j] - 1)))

    def part3(j):
        return pl.BlockSpec((recvs[j].shape[0], tr, ct), lambda k, i: (0, i, jnp.clip(k - first[j], 0, count[j] - 1)))

    C = sum(count) * ct
    tile = pl.BlockSpec((tr, ct), lambda k, i: (i, k))
    return pl.pallas_call(
        body, name=name, grid=(sum(count), nt),
        in_specs=[part(j) for j in range(n)] + [part3(j) for j in range(n)] + [tile, tile, tile],
        out_specs=[tile] * 4, out_shape=[jax.ShapeDtypeStruct((R, C), F32)] * 4,
        compiler_params=_cp(dimension_semantics=("arbitrary", "arbitrary")),
    )(*owns, *recvs, w, m, v)


def block_adam(name, items):
    n = len(items)

    def body(*refs):
        for j, item in enumerate(items):
            o_ref, r_ref, w_ref, m_ref, v_ref = refs[5 * j:5 * j + 5]
            g = o_ref[...]
            for q in range(r_ref.shape[0]):
                g = g + r_ref[q].astype(F32)
            t = (lambda a: a.T) if item[5] else (lambda a: a)
            delta, m2, v2 = _adamw(t(w_ref[...]), g, t(m_ref[...]), t(v_ref[...]))
            for ref, val in zip(refs[5 * n + 4 * j:5 * n + 4 * j + 4], (g, delta, m2, v2)):
                ref[...] = t(val)

    args = [a for item in items for a in item[:5]]
    out_shape = [jax.ShapeDtypeStruct(item[2].shape, F32) for item in items for _ in range(4)]
    res = pl.pallas_call(
        body, name=name, grid=(1,), in_specs=[VMEM] * len(args), out_specs=[VMEM] * len(out_shape),
        out_shape=out_shape, compiler_params=_cp(dimension_semantics=("arbitrary",)))(*args)
    return [tuple(res[4 * j:4 * j + 4]) for j in range(n)]


ROW_N1, ROW_N2, ROW_BG, ROW_QN, ROW_KN, ROW_CB, ROW_LW, ROW_LB, ROW_CW = 0, 1, 2, 4, 5, 6, 7, 8, 9
PACK_ROWS = 40
SMALL = ("norm1_w", "norm2_w", "b_gate", "q_norm_w", "k_norm_w", "conv_b", "conv_ln_w", "conv_ln_b", "conv_w")


def small_sync(g, sq, sides=()):
    ns = len(SMALL)

    def copies(refs):
        pack, recv, send_sems, recv_sems = refs[ns + 2:]
        x, y, c, _ = _place()
        return [pltpu.make_async_remote_copy(
            src_ref=pack, dst_ref=recv.at[4 * x + 2 * y + c], send_sem=send_sems.at[k - 1],
            recv_sem=recv_sems.at[k - 1], device_id=(x ^ (k >> 2), y ^ ((k >> 1) & 1), c ^ (k & 1)),
            device_id_type=MESH) for k in range(1, NDEV)]

    def body(*refs):
        gi = dict(zip(SMALL, refs[:ns]))
        sq_ref, tot, pack, recv, send_sems, recv_sems = refs[ns:]
        x, y, c, _ = _place()
        me = 4 * x + 2 * y + c

        pack[...] = jnp.zeros_like(pack)
        pack[ROW_KN:ROW_KN + 1, LANES:2 * LANES] = jnp.full((1, LANES), (0.5 / D) * jnp.sum(sq_ref[...]), F32)
        pack[ROW_N1:ROW_N1 + 1, :] = gi["norm1_w"][...]
        pack[ROW_N2:ROW_N2 + 1, :] = gi["norm2_w"][...]
        pack[ROW_BG:ROW_BG + 2, :] = gi["b_gate"][...]
        pack[ROW_QN:ROW_QN + 1, 0:HD] = gi["q_norm_w"][...]
        pack[ROW_KN:ROW_KN + 1, 0:HD] = gi["k_norm_w"][...]
        pack[ROW_CB:ROW_CB + 1, 0:CC] = gi["conv_b"][...]
        pack[ROW_LW:ROW_LW + 1, 0:CC] = gi["conv_ln_w"][...]
        pack[ROW_LB:ROW_LB + 1, 0:CC] = gi["conv_ln_b"][...]
        pack[ROW_CW:ROW_CW + KW, 0:CC] = gi["conv_w"][...]

        for cp in copies(refs):
            cp.start()
        recv[me] = pack[...]

    def tail(*refs):
        tot, recv = refs[ns + 1], refs[ns + 3]
        for cp in copies(refs):
            cp.wait()
        acc = recv[0]
        for p in range(1, NDEV):
            acc = acc + recv[p]
        tot[...] = acc

    args = [g[k] for k in SMALL] + [sq]
    res = _call(
        body, sides, name="small_sync", grid=(1,), in_specs=[VMEM] * len(args), out_specs=[VMEM],
        out_shape=[jax.ShapeDtypeStruct((PACK_ROWS, D), F32)],
        scratch_shapes=[pltpu.VMEM((PACK_ROWS, D), F32), pltpu.VMEM((NDEV, PACK_ROWS, D), F32),
                        _sems(NDEV - 1), _sems(NDEV - 1)],
        args=args, own_comm=True, tail=tail)
    return (res[0][0], res[1]) if sides else res[0]


def small_adam(tot, w, m, v, me):
    ns = len(SMALL)

    def body(me_ref, tot, *refs):
        wi = dict(zip(SMALL, refs[:ns]))
        mi = dict(zip(SMALL, refs[ns:2 * ns]))
        vi = dict(zip(SMALL, refs[2 * ns:3 * ns]))
        outs = refs[3 * ns:7 * ns]
        loss_ref = refs[7 * ns]
        me = me_ref[0]

        def shard_grad(name):
            if name == "b_gate":
                return tot[ROW_BG:ROW_BG + 2, pl.ds(pl.multiple_of(me * LANES, LANES), LANES)]
            if name == "conv_w":
                win = tot[ROW_CW:ROW_CW + KW, pl.ds(pl.multiple_of((me // 2) * LANES, LANES), LANES)]
                return jnp.where(me % 2 == 1, win[:, HD:LANES], win[:, 0:HD])
            row = {"norm1_w": ROW_N1, "norm2_w": ROW_N2, "q_norm_w": ROW_QN, "k_norm_w": ROW_KN,
                   "conv_b": ROW_CB, "conv_ln_w": ROW_LW, "conv_ln_b": ROW_LB}[name]
            return tot[row:row + 1, 0:wi[name].shape[1]]

        for i, name in enumerate(SMALL):
            gr = shard_grad(name)
            delta, m2, v2 = _adamw(wi[name][...], gr, mi[name][...], vi[name][...])
            outs[4 * i][...] = gr
            outs[4 * i + 1][...] = delta
            outs[4 * i + 2][...] = m2
            outs[4 * i + 3][...] = v2
        loss_ref[...] = tot[ROW_KN:ROW_KN + 1, LANES:2 * LANES]

    out_shape = []
    for name in SMALL:
        out_shape += [jax.ShapeDtypeStruct(w[name].shape, F32)] * 4
    out_shape.append(jax.ShapeDtypeStruct((1, LANES), F32))
    args = [tot] + [w[k] for k in SMALL] + [m[k] for k in SMALL] + [v[k] for k in SMALL]
    grid_spec = pltpu.PrefetchScalarGridSpec(
        num_scalar_prefetch=1, grid=(1,), in_specs=[VMEM] * len(args), out_specs=[VMEM] * len(out_shape))
    res = pl.pallas_call(body, name="small_adam", grid_spec=grid_spec, out_shape=out_shape)(me, *args)
    out = {name: tuple(res[4 * i:4 * i + 4]) for i, name in enumerate(SMALL)}
    return out, res[4 * ns][0, 0]


MATS = ("w_in", "w_o_attn", "w_pw_conv", "w_out", "w_ffn_in", "w_ffn_out")
TRANSPOSED = ("w_in", "w_ffn_in")
WEIGHTS = ("norm1_w", "w_in", "b_gate", "q_norm_w", "k_norm_w", "w_o_attn", "conv_w", "conv_b", "conv_ln_w",
           "conv_ln_b", "w_pw_conv", "w_out", "norm2_w", "w_ffn_in", "w_ffn_out")


def _blocks_to_cols(blocks):
    n, R, C = blocks.shape
    return blocks.transpose(1, 0, 2).reshape(R, n * C)


def kernel(x, positions, norm1_w, w_in, b_gate, q_norm_w, k_norm_w, w_o_attn, conv_w, conv_b, conv_ln_w, conv_ln_b, w_pw_conv, w_out, norm2_w, w_ffn_in, w_ffn_out, loss_target, m_norm1_w, m_w_in, m_b_gate, m_q_norm_w, m_k_norm_w, m_w_o_attn, m_conv_w, m_conv_b, m_conv_ln_w, m_conv_ln_b, m_w_pw_conv, m_w_out, m_norm2_w, m_w_ffn_in, m_w_ffn_out, v_norm1_w, v_w_in, v_b_gate, v_q_norm_w, v_k_norm_w, v_w_o_attn, v_conv_w, v_conv_b, v_conv_ln_w, v_conv_ln_b, v_w_pw_conv, v_w_out, v_norm2_w, v_w_ffn_in, v_w_ffn_out):
    w = dict(norm1_w=norm1_w, w_in=w_in, b_gate=b_gate, q_norm_w=q_norm_w, k_norm_w=k_norm_w, w_o_attn=w_o_attn,
             conv_w=conv_w, conv_b=conv_b, conv_ln_w=conv_ln_w, conv_ln_b=conv_ln_b, w_pw_conv=w_pw_conv,
             w_out=w_out, norm2_w=norm2_w, w_ffn_in=w_ffn_in, w_ffn_out=w_ffn_out)
    m = dict(norm1_w=m_norm1_w, w_in=m_w_in, b_gate=m_b_gate, q_norm_w=m_q_norm_w, k_norm_w=m_k_norm_w,
             w_o_attn=m_w_o_attn, conv_w=m_conv_w, conv_b=m_conv_b, conv_ln_w=m_conv_ln_w,
             conv_ln_b=m_conv_ln_b, w_pw_conv=m_w_pw_conv, w_out=m_w_out, norm2_w=m_norm2_w,
             w_ffn_in=m_w_ffn_in, w_ffn_out=m_w_ffn_out)
    v = dict(norm1_w=v_norm1_w, w_in=v_w_in, b_gate=v_b_gate, q_norm_w=v_q_norm_w, k_norm_w=v_k_norm_w,
             w_o_attn=v_w_o_attn, conv_w=v_conv_w, conv_b=v_conv_b, conv_ln_w=v_conv_ln_w,
             conv_ln_b=v_conv_ln_b, w_pw_conv=v_w_pw_conv, w_out=v_w_out, norm2_w=v_norm2_w,
             w_ffn_in=v_w_ffn_in, w_ffn_out=v_w_ffn_out)
    def two_d(t):
        t = {k: (a[0] if a.ndim == 3 else a) for k, a in t.items()}
        return {k: (a.T if k in TRANSPOSED else a) for k, a in t.items()}

    w, m, v = two_d(w), two_d(m), two_d(v)

    x2, target = x[0], loss_target[0]
    c_idx = lax.axis_index("c").astype(jnp.int32)
    chip_idx = (2 * lax.axis_index("x") + lax.axis_index("y")).astype(jnp.int32)
    qw2 = jnp.tile(w["q_norm_w"], (1, 2))
    kw2 = jnp.tile(w["k_norm_w"], (1, 2))

    ax, ay = lax.axis_index("x"), lax.axis_index("y")
    chip_order = jnp.stack([2 * ax + ay, 2 * (1 - ax) + ay, 2 * ax + 1 - ay, 2 * (1 - ax) + 1 - ay]).astype(jnp.int32)
    h, proj, w_in_blocks, tabs = in_proj_gather(x2, w["norm1_w"], w["w_in"], chip_order, positions.reshape(S, 1))
    w_in_t = w_in_blocks.reshape(INW, D)
    (attn, lse), ((w_ffn_in_blocks,), (w_out_blocks,), (w_o_blocks,), (w_pw_blocks,), (bg_blocks,), (cw_blocks,)) = attn_fwd(
        proj, tabs, qw2, kw2, sides=(ag_blocks_relay(w["w_ffn_in"], BF16), ag_blocks_relay(w["w_out"], BF16),
                                     ag_blocks_relay(w["w_o_attn"], BF16, transpose=True),
                                     ag_blocks_relay(w["w_pw_conv"], BF16, transpose=True),
                                     ag_blocks(w["b_gate"], F32), ag_blocks(w["conv_w"], F32)))
    w_ffn_in_t = w_ffn_in_blocks.reshape(2 * FF, D)
    w_out_f = w_out_blocks.reshape(D, D)
    w_o_t, w_pw_t = w_o_blocks.reshape(D, CC), w_pw_blocks.reshape(D, CC)
    b_gate_f, conv_w_f = _blocks_to_cols(bg_blocks), _blocks_to_cols(cw_blocks)
    cpre, u3 = conv_fwd(proj, conv_w_f, w["conv_b"], w["conv_ln_w"], w["conv_ln_b"])
    x1, z, ya, yb = mix_out(x2, proj, b_gate_f, attn, u3, w_o_t, w_pw_t, w_out_f)
    (h2, gu, f), ((w_ffn_out_blocks,),) = ffn_in(x1, w["norm2_w"], w_ffn_in_t, sides=(ag_blocks_relay(w["w_ffn_out"], BF16),))
    w_ffn_out_f = w_ffn_out_blocks.reshape(FF, D)
    dy, dyb, sq = ffn_out_loss(x1, f, w_ffn_out_f, target)

    g = {}
    def blocks(name, pairs, tm):
        return [t.reshape(NDEV, t.shape[0] // NDEV, t.shape[1]) for t in mm_tn(name, pairs, tm)]

    g_ffn_out, gb_ffn_out = blocks("gw_ffn_out", [(f, dyb)], FF // 2)
    (d_gu, d_x1, d_x1b, g["norm2_w"]), ((ra_ffn_out,),) = ffn_bwd(
        dy, dyb, gu, x1, w["norm2_w"], w_ffn_in_t, w_ffn_out_f, sides=(rs_to_sibling([gb_ffn_out]),))
    g_ffn_in, gb_ffn_in = blocks("gw_ffn_in", [(d_gu, h2)], FF // 2)
    (d_ya, d_yb, d_gl, d_attn, d_u3, g["b_gate"]), ((ra_ffn_in,),) = out_bwd(
        d_x1b, proj, b_gate_f, ya, yb, w_o_t, w_pw_t, w_out_f, sides=(rs_to_sibling([gb_ffn_in]),))
    g_out, gb_out, g_w_o, gb_w_o, g_w_pw, gb_w_pw = blocks(
        "gw_out_o_pw", [(z, d_x1b), (d_ya, attn), (d_yb, u3)], D // 2)
    (d_conv, g["conv_w"], g["conv_b"], g["conv_ln_w"], g["conv_ln_b"]), ((ra_out, ra_w_o, ra_w_pw),) = conv_bwd(
        proj, cpre, d_u3, conv_w_f, conv_w_f[::-1], w["conv_ln_w"], w["conv_ln_b"],
        sides=(rs_to_sibling([gb_out, gb_w_o, gb_w_pw]),))
    (pb_ffn_out, own_ffn_out), (pb_ffn_in, own_ffn_in), (pb_out, own_out), (pb_w_o, own_w_o), (pb_w_pw, own_w_pw) = chip_sum(
        "chip_sum_early", [g_ffn_out, g_ffn_in, g_out, g_w_o, g_w_pw],
        [ra_ffn_out, ra_ffn_in, ra_out, ra_w_o, ra_w_pw], c_idx, chip_idx)
    (d_q, d_k, d_v, gqw, gkw), ((rb_ffn_out, rb_ffn_in, rb_out, rb_w_o, rb_w_pw),) = attn_bwd(
        proj, tabs, qw2, kw2, d_attn, attn, lse,
        sides=(rs_to_chips([pb_ffn_out, pb_ffn_in, pb_out, pb_w_o, pb_w_pw]),))
    g["q_norm_w"] = gqw[0:1, 0:HD] + gqw[0:1, HD:LANES]
    g["k_norm_w"] = gkw[0:1, 0:HD] + gkw[0:1, HD:LANES]
    d_segs = (d_q, d_k, d_v, d_conv, d_gl)
    parts, to_sibling, to_chips, owns, from_chips = [], None, None, [], []
    for k, hw in enumerate(GW_IN_SPLIT):
        sides = tuple(s for s in (to_chips, to_sibling) if s is not None)
        (part, part_b), outs = gw_in_t("gw_in_%d" % k, h, d_segs, sum(GW_IN_SPLIT[:k]), hw, sides=sides)
        outs = list(outs)
        if to_chips is not None:
            from_chips.append(outs.pop(0)[0])
        if to_sibling is not None:
            (pb, own), = chip_sum("chip_sum_w_in_%d" % (k - 1), [parts[-1]], [outs.pop(0)[0]], c_idx, chip_idx)
            owns.append(own)
            to_chips = rs_to_chips_combined(pb)
        else:
            to_chips = None
        parts.append(part.reshape(NDEV, INW // NDEV, hw))
        to_sibling = rs_to_sibling([part_b.reshape(NDEV, INW // NDEV, hw)])
    (grad_x, g["norm1_w"]), ((rb_prev,), (ra_last,)) = in_bwd(
        d_q, d_k, d_v, d_conv, d_gl, w_in_t, x2, d_x1, w["norm1_w"], sides=(to_chips, to_sibling))
    from_chips.append(rb_prev)
    (pb, own), = chip_sum("chip_sum_w_in_%d" % (len(GW_IN_SPLIT) - 1), [parts[-1]], [ra_last], c_idx, chip_idx)
    owns.append(own)
    small_sums, ((rb_last,),) = small_sync(g, sq, sides=(rs_to_chips_combined(pb),))
    small, loss = small_adam(small_sums, w, m, v, (4 * ax + 2 * ay + c_idx).astype(jnp.int32).reshape(1))
    from_chips.append(rb_last)

    adam_o, adam_pw, adam_out = block_adam("adam_w_o_pw_out", [
        (own_w_o, rb_w_o, w["w_o_attn"], m["w_o_attn"], v["w_o_attn"], True),
        (own_w_pw, rb_w_pw, w["w_pw_conv"], m["w_pw_conv"], v["w_pw_conv"], True),
        (own_out, rb_out, w["w_out"], m["w_out"], v["w_out"], False)])
    res = {
        "w_in": shard_adam("adam_w_in", owns, from_chips, w["w_in"], m["w_in"], v["w_in"]),
        "w_ffn_in": shard_adam("adam_w_ffn_in", [own_ffn_in], [rb_ffn_in], w["w_ffn_in"], m["w_ffn_in"], v["w_ffn_in"]),
        "w_o_attn": adam_o, "w_pw_conv": adam_pw, "w_out": adam_out,
        "w_ffn_out": shard_adam("adam_w_ffn_out", [own_ffn_out], [rb_ffn_out],
                                w["w_ffn_out"], m["w_ffn_out"], v["w_ffn_out"]),
    }
    res = {k: tuple(a.T if k in TRANSPOSED else a for a in r) for k, r in res.items()}
    res.update(small)

    def shaped(name, a):
        return a.reshape((1,) + a.shape) if name in MATS or name in ("b_gate", "conv_w") else a

    outs = [loss, grad_x.reshape(1, S, D)]
    for i in range(4):
        outs += [shaped(k, res[k][i]) for k in WEIGHTS]
    return tuple(outs)
```

```python
import functools
from typing import Callable, NamedTuple, Optional

import numpy as np
import jax
import jax.numpy as jnp
from jax import lax
from jax.experimental import pallas as pl
from jax.experimental.pallas import tpu as pltpu

F32 = jnp.float32
BF16 = jnp.bfloat16

S = 2048
D = 1024
HD = 64
QKV = 1536
CC = 512
KW = 31
FF = 2816
INW = 7680
OFF_Q, OFF_K, OFF_V, OFF_CA, OFF_CB, OFF_GA, OFF_GB = 0, 1536, 3072, 4608, 5120, 5632, 6656
DILATIONS = (1, 4, 16)
HALF_SPAN = 64
EPS = 1e-6
NEG_INF = -1e30
ROPE_THETA = 500000.0
ROT_DIM = 16

ADAM_LR = 0.001
ADAM_B1 = 0.9
ADAM_B2 = 0.999
ADAM_EPS = 1e-08
ADAM_WD = 0.01
ADAM_STEP = 10

NDEV = 8
LANES = 128
TM = 256
IN_PROJ_TM = 512
TQ = 128
VMEM_LIMIT = 56 * 1024 * 1024
MESH = pl.DeviceIdType.MESH


def _cp(**kw):
    return pltpu.CompilerParams(vmem_limit_bytes=VMEM_LIMIT, **kw)


def _row(width, col=0, tm=TM):
    return pl.BlockSpec((tm, width), lambda i: (i, col))


PLANE = 512


def _planes(width, tm=TM):
    return pl.BlockSpec((width // PLANE, tm, PLANE), lambda i: (0, i, 0))


def _res(shape):
    nd = len(shape)
    return pl.BlockSpec(shape, lambda *_: (0,) * nd, pipeline_mode=pl.Buffered(1))


def _dot(a, b):
    return jnp.dot(a, b, preferred_element_type=F32)


def _dot_nt(a, b):
    return lax.dot_general(a, b, (((1,), (1,)), ((), ())), preferred_element_type=F32)


def _dot_tn(a, b):
    return lax.dot_general(a, b, (((0,), (0,)), ((), ())), preferred_element_type=F32)


def _sigmoid(x):
    return jax.nn.sigmoid(x)


def _dsilu(x, sg):
    return sg * (1.0 + x * (1.0 - sg))


ANY = pl.BlockSpec(memory_space=pl.ANY)
VMEM = pl.BlockSpec(memory_space=pltpu.VMEM)


class Side(NamedTuple):
    args: tuple
    in_specs: tuple
    out_shape: tuple
    scratch: tuple
    start: Callable
    finish: Callable
    mid: Optional[Callable] = None
    peers: str = ""


BARRIER_IDS = {"s": 0, "dxy": 1, "dsxy": 2, "sxy": 3, "xy": 4}


def _peer_barrier(peers):
    x, y, c = lax.axis_index("x"), lax.axis_index("y"), lax.axis_index("c")
    where = {"s": (x, y, 1 - c), "x": (1 - x, y, c), "y": (x, 1 - y, c), "d": (1 - x, 1 - y, c)}
    barrier = pltpu.get_barrier_semaphore()
    for p in peers:
        pl.semaphore_signal(barrier, inc=1, device_id=where[p], device_id_type=MESH)
    pl.semaphore_wait(barrier, len(peers))


def _call(body, sides=(), *, name, grid, in_specs, out_specs, out_shape, scratch_shapes=(), args, own_comm=False,
          tail=None):
    assert tail is None or int(np.prod(grid)) == 1
    ni, no, ns = len(in_specs), len(out_specs), len(scratch_shapes)
    cnt = [(len(s.args), len(s.out_shape), len(s.scratch)) for s in sides]
    peers = "".join(sorted(set("".join(s.peers for s in sides))))
    if own_comm or not sides or any(not s.peers for s in sides):
        peers = ""

    def take(refs, pos, n):
        return refs[pos:pos + n], pos + n

    def full(*refs):
        m_in, pos = take(refs, 0, ni)
        s_in = []
        for a, _, _ in cnt:
            r, pos = take(refs, pos, a)
            s_in.append(r)
        m_out, pos = take(refs, pos, no)
        s_out = []
        for _, o, _ in cnt:
            r, pos = take(refs, pos, o)
            s_out.append(r)
        m_scr, pos = take(refs, pos, ns)
        s_scr = []
        for _, _, c in cnt:
            r, pos = take(refs, pos, c)
            s_scr.append(r)
        if sides:
            first = functools.reduce(jnp.logical_and, [pl.program_id(d) == 0 for d in range(len(grid))])
            last = functools.reduce(jnp.logical_and, [pl.program_id(d) == g - 1 for d, g in enumerate(grid)])

            @pl.when(first)
            def _():
                if peers:
                    _peer_barrier(peers)
                for s, a, o, c in zip(sides, s_in, s_out, s_scr):
                    s.start(a, o, c)

            steps = int(np.prod(grid))
            mid_step = (2 * steps) // 3
            if steps > 1 and any(s.mid is not None for s in sides):
                step = functools.reduce(lambda acc, d: acc * grid[d] + pl.program_id(d), range(len(grid)), 0)

                @pl.when(step == mid_step)
                def _():
                    for s, a, o, c in zip(sides, s_in, s_out, s_scr):
                        if s.mid is not None:
                            s.mid(a, o, c)

        body(*m_in, *m_out, *m_scr)
        if sides:
            @pl.when(last)
            def _():
                for s, a, o, c in zip(sides, s_in, s_out, s_scr):
                    if s.mid is not None and steps == 1:
                        s.mid(a, o, c)
                if tail is not None:
                    tail(*m_in, *m_out, *m_scr)
                for s, a, o, c in zip(sides, s_in, s_out, s_scr):
                    s.finish(a, o, c)
        elif tail is not None:
            tail(*m_in, *m_out, *m_scr)

    res = pl.pallas_call(
        full, name=name, grid=grid,
        in_specs=list(in_specs) + [sp for s in sides for sp in s.in_specs],
        out_specs=list(out_specs) + [ANY for s in sides for _ in s.out_shape],
        out_shape=list(out_shape) + [o for s in sides for o in s.out_shape],
        scratch_shapes=list(scratch_shapes) + [c for s in sides for c in s.scratch],
        compiler_params=_cp(dimension_semantics=("arbitrary",) * len(grid),
                            **({"collective_id": BARRIER_IDS[peers]} if peers else {})),
    )(*args, *[a for s in sides for a in s.args])
    res = list(res)
    if not sides:
        return res
    outs, pos = take(res, 0, no)
    side_outs = []
    for _, o, _ in cnt:
        r, pos = take(res, pos, o)
        side_outs.append(r)
    return outs, side_outs


def _inv_freq_lanes():
    inv = np.float32(ROPE_THETA) ** (-np.arange(0, ROT_DIM, 2, dtype=np.float32) / np.float32(ROT_DIM))
    lane = np.arange(LANES) % HD
    out = np.where(lane < ROT_DIM, inv[lane % (ROT_DIM // 2)], 0.0).astype(np.float32)
    return jnp.asarray(out.reshape(1, LANES))


def _rope_tables(pos, inv_freq):
    ang = pos.astype(F32) * inv_freq
    lane = lax.broadcasted_iota(jnp.int32, ang.shape, 1) % HD
    cs = jnp.cos(ang)
    sn = jnp.sin(ang)
    return (jnp.where(lane < ROT_DIM, cs, 1.0), jnp.where(lane < ROT_DIM // 2, -sn, 0.0),
            jnp.where(lane < ROT_DIM // 2, 0.0, jnp.where(lane < ROT_DIM, sn, 0.0)))


def _rope(v, c, s1, s2):
    return v * c + pltpu.roll(v, LANES - 8, axis=1) * s1 + pltpu.roll(v, 8, axis=1) * s2


def _rope_t(d, c, s1, s2):
    return d * c - pltpu.roll(d, LANES - 8, axis=1) * s1 - pltpu.roll(d, 8, axis=1) * s2


def _head_mat():
    r = lax.broadcasted_iota(jnp.int32, (LANES, LANES), 0) // HD
    c = lax.broadcasted_iota(jnp.int32, (LANES, LANES), 1) // HD
    return jnp.where(r == c, 1.0 / HD, 0.0).astype(BF16)


def _head_mean(t, e):
    hi = t.astype(BF16)
    rest = (t - hi.astype(F32)).astype(BF16)
    return _dot(hi, e) + _dot(rest, e)


def in_proj_gather(x, norm_w, shard_t, chip_order, pos_col):
    R = INW // NDEV
    tm = IN_PROJ_TM
    half, nt = R // 2, S // tm

    def body(ord_ref, x_ref, nw_ref, sh_ref, pos_ref, f_ref, h_ref, p_ref, wfull_ref, c_ref, s1_ref, s2_ref,
             wt, hs, send, recv, loc):
        kk, i = pl.program_id(0), pl.program_id(1)
        x, y, c, _ = _place()
        me, flip = 4 * x + 2 * y + c, 1 - 2 * c
        here, sib, xn, yn = (x, y, c), (x, y, 1 - c), (1 - x, y, c), (x, 1 - y, c)
        b_xn, b_yn, b_dg = 4 * (1 - x) + 2 * y + c, 4 * x + 2 * (1 - y) + c, 4 * (1 - x) + 2 * (1 - y) + c

        def cp(k, block, to, rows=None):
            dst = wt.at[block] if rows is None else wt.at[block, pl.ds(rows * half, half), :]
            return _remote(dst, dst, send, recv, k, to)

        def sends():
            return [cp(0, me, sib), cp(1, me, xn), cp(2, me, yn), cp(3, b_xn, sib), cp(4, b_yn, sib),
                    cp(5, b_xn, yn, rows=0), cp(6, b_yn, xn, rows=1), cp(7, b_dg, sib, rows=0), cp(8, b_dg, sib, rows=1)]

        def keep(j, blk0):
            pair = pl.ds(pl.multiple_of(blk0, 2), 2)
            return pltpu.make_async_copy(wt.at[pair], wfull_ref.at[pair], loc.at[j])

        @pl.when((kk == 0) & (i == 0))
        def _():
            _peer_barrier("sxy")
            _cast_rows(wt.at[me], sh_ref)
            for s_ in sends()[0:3]:
                s_.start()

            def tables(j, _):
                chunk = pl.ds(pl.multiple_of(j * TM, TM), TM)
                c_ref[chunk, :], s1_ref[chunk, :], s2_ref[chunk, :] = _rope_tables(pos_ref[chunk, :], f_ref[...])
                return 0

            lax.fori_loop(0, S // TM, tables, 0)
            cp(0, me + flip, here).wait_recv()
            keep(0, me - c).start()

        @pl.when((kk == 1) & (i == 0))
        def _():
            cp(1, b_xn, here).wait_recv()
            sends()[5].start()
            sends()[3].start()
            cp(2, b_yn, here).wait_recv()
            sends()[6].start()
            sends()[4].start()
            cp(3, b_xn + flip, here).wait_recv()
            keep(1, b_xn - c).start()

        @pl.when((kk == 2) & (i == 0))
        def _():
            cp(4, b_yn + flip, here).wait_recv()
            keep(2, b_yn - c).start()

        @pl.when((kk == 3) & (i == 0))
        def _():
            cp(5, b_dg, here, rows=0).wait_recv()
            sends()[7].start()
            cp(6, b_dg, here, rows=1).wait_recv()
            sends()[8].start()
            cp(7, b_dg + flip, here, rows=0).wait_recv()
            cp(8, b_dg + flip, here, rows=1).wait_recv()
            keep(3, b_dg - c).start()

        rows = pl.ds(pl.multiple_of(i * tm, tm), tm)

        @pl.when(kk == 0)
        def _():
            xv = x_ref[...]
            r = lax.rsqrt(jnp.mean(xv * xv, axis=-1, keepdims=True) + EPS)
            hb = (xv * r * nw_ref[...]).astype(BF16)
            h_ref[...] = hb
            hs[rows, :] = hb

        h = hs[rows, :]
        chip = ord_ref[kk]
        for cc in range(2):
            p_ref[:, cc * R:(cc + 1) * R] = _dot_nt(h, wt[2 * chip + cc])

        @pl.when((kk == 3) & (i == nt - 1))
        def _():
            for s_ in sends():
                s_.wait_send()
            for j, blk in enumerate((me, b_xn, b_yn, b_dg)):
                keep(j, blk - c).wait()

    def first_pass(kk, i):
        return jnp.where(kk == 0, i, nt - 1)

    grid_spec = pltpu.PrefetchScalarGridSpec(
        num_scalar_prefetch=1, grid=(4, nt),
        in_specs=[pl.BlockSpec((tm, D), lambda kk, i, o: (first_pass(kk, i), 0)),
                  pl.BlockSpec((1, D), lambda kk, i, o: (0, 0)), VMEM, VMEM,
                  pl.BlockSpec((1, LANES), lambda kk, i, o: (0, 0))],
        out_specs=[pl.BlockSpec((tm, D), lambda kk, i, o: (first_pass(kk, i), 0)),
                   pl.BlockSpec((tm, 2 * R), lambda kk, i, o: (i, o[kk])), ANY]
        + [pl.BlockSpec((S, LANES), lambda kk, i, o: (0, 0))] * 3,
        scratch_shapes=[pltpu.VMEM((NDEV, R, D), BF16), pltpu.VMEM((S, D), BF16), _sems(9), _sems(9), _sems(4)])
    res = pl.pallas_call(
        body, name="in_proj_gather", grid_spec=grid_spec,
        out_shape=[jax.ShapeDtypeStruct((S, D), BF16), jax.ShapeDtypeStruct((S, INW), F32),
                   jax.ShapeDtypeStruct((NDEV, R, D), BF16)] + [jax.ShapeDtypeStruct((S, LANES), F32)] * 3,
        compiler_params=_cp(dimension_semantics=("arbitrary", "arbitrary"), collective_id=BARRIER_IDS["sxy"]),
    )(chip_order, x, norm_w, shard_t, pos_col, _inv_freq_lanes())
    return res[0], res[1], res[2], tuple(res[3:])


def _qk_specs():
    nb = QKV // LANES
    return [pl.BlockSpec((S, LANES), functools.partial(lambda hp, g, o: (0, o + g * 4 + hp), o=o))
            for o in (OFF_Q // LANES, OFF_K // LANES, OFF_V // LANES)]


def _tab_specs():
    return [pl.BlockSpec((S, LANES), lambda hp, g: (0, 0), pipeline_mode=pl.Buffered(1))] * 3


def _vec_spec():
    return pl.BlockSpec((1, LANES), lambda hp, g: (0, 0))


def _sub_rows(r, d, start, n):
    if d == 1:
        return pl.ds(start, n)
    return pl.ds(r + d * start, n, stride=d)


def _band_window(i, L):
    W = min(TQ + 2 * HALF_SPAN, L)
    q0 = pl.multiple_of(i * TQ, TQ)
    k0 = pl.multiple_of(jnp.clip(q0 - HALF_SPAN, 0, L - W), HALF_SPAN)
    qpos = q0 + (lax.broadcasted_iota(jnp.int32, (2 * TQ, W), 0) & (TQ - 1))
    kpos = k0 + lax.broadcasted_iota(jnp.int32, (2 * TQ, W), 1)
    valid = jnp.abs(qpos - kpos) <= HALF_SPAN
    return W, q0, k0, valid


def _stack_heads(t, lo):
    z = jnp.zeros_like(t)
    return jnp.concatenate([jnp.where(lo, t, z), jnp.where(lo, z, t)], axis=0)


def _unstack_heads(t2, lo):
    return jnp.where(lo, t2[0:TQ], t2[TQ:2 * TQ])


CHAINS = 8


def _interleave(d):
    ru = min(d, CHAINS)
    return ru, min(CHAINS // ru, S // d // TQ)


def _for_blocks(n, fn):
    if n == 1:
        fn(0)
    else:
        def it(j, _):
            fn(j)
            return 0
        lax.fori_loop(0, n, it, 0)


def attn_fwd(proj, tabs, qw2, kw2, sides=()):
    CH = 256

    def body(q_ref, k_ref, v_ref, c_ref, s1_ref, s2_ref, qw_ref, kw_ref, at_ref, ls_ref,
             qs, ks, vs, osub, lsub, onat, lnat, qn, kn):
        g = pl.program_id(1)
        lo = lax.broadcasted_iota(jnp.int32, (1, LANES), 1) < HD
        e = _head_mat()

        def prep(i, _):
            rows = pl.ds(pl.multiple_of(i * CH, CH), CH)
            c, s1, s2 = c_ref[rows, :], s1_ref[rows, :], s2_ref[rows, :]
            for t_ref, w_ref, out, scale in ((q_ref, qw_ref, qn, HD ** -0.5), (k_ref, kw_ref, kn, 1.0)):
                t = t_ref[rows, :]
                r = lax.rsqrt(_head_mean(t * t, e) + EPS)
                out[rows, :] = _rope(t * r * w_ref[...], c, s1, s2) * scale
            return 0

        lax.fori_loop(0, S // CH, prep, 0, unroll=4)

        def group(gi, d):
            L = S // d

            ru, nb = _interleave(d)

            def stage(r, off):
                for c0 in range(0, L, CH):
                    n = min(CH, L)
                    rows = _sub_rows(r, d, c0, n)
                    dst = pl.ds(off + c0, n)
                    qs[dst, :] = qn[rows, :].astype(BF16)
                    ks[dst, :] = kn[rows, :].astype(BF16)
                    vs[dst, :] = v_ref[rows, :].astype(BF16)

            def one(off, i):
                W, q0, k0, valid = _band_window(i, L)
                q2 = _stack_heads(qs[pl.ds(off + q0, TQ), :], lo)
                sc = jnp.where(valid, _dot_nt(q2, ks[pl.ds(off + k0, W), :]), NEG_INF)
                m = jnp.max(sc, axis=-1, keepdims=True)
                p = jnp.exp(sc - m)
                den = jnp.sum(p, axis=-1, keepdims=True)
                o2 = _dot(p.astype(BF16), vs[pl.ds(off + k0, W), :]) / den
                l2 = jnp.broadcast_to(m + jnp.log(den), (2 * TQ, LANES))
                osub[pl.ds(off + q0, TQ), :] = _unstack_heads(o2, lo)
                lsub[pl.ds(off + q0, TQ), :] = _unstack_heads(l2, lo)

            def unstage(r, off):
                for c0 in range(0, L, CH):
                    n = min(CH, L)
                    rows = _sub_rows(r, d, c0, n)
                    onat[gi, rows, :] = osub[pl.ds(off + c0, n), :]
                    lnat[gi, rows, :] = lsub[pl.ds(off + c0, n), :]

            def step(t, _):
                for u in range(ru):
                    stage(t * ru + u, u * L)
                _for_blocks(L // TQ // nb, lambda j: [one(u * L, j * nb + b) for u in range(ru) for b in range(nb)])
                for u in range(ru):
                    unstage(t * ru + u, u * L)
                return 0

            lax.fori_loop(0, d // ru, step, 0)

        for gi, d in enumerate(DILATIONS):
            pl.when(g == gi)(functools.partial(group, gi, d))

        @pl.when(g == len(DILATIONS) - 1)
        def _():
            def mix(i, _):
                rows = pl.ds(pl.multiple_of(i * CH, CH), CH)
                l0, l1, l2 = lnat[0, rows, :], lnat[1, rows, :], lnat[2, rows, :]
                m = jnp.maximum(jnp.maximum(l0, l1), l2)
                e0, e1, e2 = jnp.exp(l0 - m), jnp.exp(l1 - m), jnp.exp(l2 - m)
                den = e0 + e1 + e2
                a = (e0 * onat[0, rows, :] + e1 * onat[1, rows, :] + e2 * onat[2, rows, :]) / den
                at_ref[rows, :] = a.astype(BF16)
                ls_ref[rows, :] = m + jnp.log(den)
                return 0

            lax.fori_loop(0, S // CH, mix, 0)

    out_spec = pl.BlockSpec((S, LANES), lambda hp, g: (0, hp))
    return _call(
        body, sides, name="attn_fwd", grid=(4, 3),
        in_specs=_qk_specs() + _tab_specs() + [_vec_spec(), _vec_spec()],
        out_specs=[out_spec, out_spec],
        out_shape=[jax.ShapeDtypeStruct((S, CC), BF16), jax.ShapeDtypeStruct((S, CC), F32)],
        scratch_shapes=[pltpu.VMEM((S, LANES), BF16)] * 3 + [pltpu.VMEM((S, LANES), F32)] * 2
        + [pltpu.VMEM((3, S, LANES), F32)] * 2 + [pltpu.VMEM((S, LANES), F32)] * 2,
        args=(proj, proj, proj, *tabs, qw2, kw2))


def attn_bwd(proj, tabs, qw2, kw2, d_attn, attn, lse, sides=()):
    CH = 256

    def body(q_ref, k_ref, v_ref, c_ref, s1_ref, s2_ref, qw_ref, kw_ref, do_ref, at_ref, ls_ref,
             dq_ref, dk_ref, dv_ref, gqw_ref, gkw_ref,
             qs, ks, vs, dos, dsub, lsub, dqs, dks, dvs, dnat, qx, kx, dvn, tnq, tnk, rrq, rrk):
        hp, g = pl.program_id(0), pl.program_id(1)
        lo = lax.broadcasted_iota(jnp.int32, (1, LANES), 1) < HD
        e = _head_mat()
        both = ((q_ref, qw_ref, qx, tnq, rrq, HD ** -0.5), (k_ref, kw_ref, kx, tnk, rrk, 1.0))

        @pl.when((hp == 0) & (g == 0))
        def _():
            gqw_ref[...] = jnp.zeros_like(gqw_ref)
            gkw_ref[...] = jnp.zeros_like(gkw_ref)

        def prep(i, _):
            rows = pl.ds(pl.multiple_of(i * CH, CH), CH)
            dnat[rows, :] = _head_mean(do_ref[rows, :] * at_ref[rows, :].astype(F32), e) * float(HD)
            c, s1, s2 = c_ref[rows, :], s1_ref[rows, :], s2_ref[rows, :]
            for t_ref, w_ref, x, tn_s, rr_s, scale in both:
                t = t_ref[rows, :]
                rr = lax.rsqrt(_head_mean(t * t, e) + EPS)
                tn = t * rr
                rr_s[rows, :] = rr
                tn_s[rows, :] = tn
                x[rows, :] = _rope(tn * w_ref[...], c, s1, s2) * scale
            return 0

        lax.fori_loop(0, S // CH, prep, 0, unroll=4)

        def group(d):
            L = S // d

            ru, nb = _interleave(d)

            def stage(r, off):
                for c0 in range(0, L, CH):
                    n = min(CH, L)
                    rows = _sub_rows(r, d, c0, n)
                    dst = pl.ds(off + c0, n)
                    qs[dst, :] = qx[rows, :].astype(BF16)
                    ks[dst, :] = kx[rows, :].astype(BF16)
                    vs[dst, :] = v_ref[rows, :].astype(BF16)
                    dos[dst, :] = do_ref[rows, :].astype(BF16)
                    dsub[dst, :] = dnat[rows, :]
                    lsub[dst, :] = ls_ref[rows, :]
                    dks[dst, :] = jnp.zeros((n, LANES), F32)
                    dvs[dst, :] = jnp.zeros((n, LANES), F32)

            def one(off, i):
                W, q0, k0, valid = _band_window(i, L)
                qrows, krows = pl.ds(off + q0, TQ), pl.ds(off + k0, W)
                q2 = _stack_heads(qs[qrows, :], lo)
                do2 = _stack_heads(dos[qrows, :], lo)
                kk, vv = ks[krows, :], vs[krows, :]
                lse_b, dd_b = lsub[qrows, :], dsub[qrows, :]
                lse2 = jnp.concatenate([lse_b[:, 0:1], lse_b[:, HD:HD + 1]], axis=0)
                dd2 = jnp.concatenate([dd_b[:, 0:1], dd_b[:, HD:HD + 1]], axis=0)
                sc = jnp.where(valid, _dot_nt(q2, kk), NEG_INF)
                p = jnp.exp(sc - lse2)
                ds = (p * (_dot_nt(do2, vv) - dd2)).astype(BF16)
                dqs[qrows, :] = _unstack_heads(_dot(ds, kk), lo)
                dks[krows, :] = dks[krows, :] + _dot_tn(ds, q2)
                dvs[krows, :] = dvs[krows, :] + _dot_tn(p.astype(BF16), do2)

            def unstage(r, off):
                for c0 in range(0, L, CH):
                    n = min(CH, L)
                    rows = _sub_rows(r, d, c0, n)
                    src = pl.ds(off + c0, n)
                    qx[rows, :] = dqs[src, :]
                    kx[rows, :] = dks[src, :]
                    dvn[rows, :] = dvs[src, :]

            def step(t, _):
                for u in range(ru):
                    stage(t * ru + u, u * L)
                _for_blocks(L // TQ // nb, lambda j: [one(u * L, j * nb + b) for u in range(ru) for b in range(nb)])
                for u in range(ru):
                    unstage(t * ru + u, u * L)
                return 0

            lax.fori_loop(0, d // ru, step, 0)

        for gi, d in enumerate(DILATIONS):
            pl.when(g == gi)(functools.partial(group, d))

        def emit(i, _):
            rows = pl.ds(pl.multiple_of(i * CH, CH), CH)
            c, s1, s2 = c_ref[rows, :], s1_ref[rows, :], s2_ref[rows, :]
            for (_, w_ref, x, tn_s, rr_s, scale), out, gw_ref in zip(both, (dq_ref, dk_ref), (gqw_ref, gkw_ref)):
                tn = tn_s[rows, :]
                dy = _rope_t(x[rows, :] * scale, c, s1, s2)
                gw_ref[0:1, :] = gw_ref[0:1, :] + jnp.sum(dy * tn, axis=0, keepdims=True)
                dtn = dy * w_ref[...]
                out[rows, :] = (rr_s[rows, :] * (dtn - tn * _head_mean(dtn * tn, e))).astype(BF16)
            dv_ref[rows, :] = dvn[rows, :].astype(BF16)
            return 0

        lax.fori_loop(0, S // CH, emit, 0, unroll=4)

    nat_spec = pl.BlockSpec((S, LANES), lambda hp, g: (0, hp))
    out_spec = pl.BlockSpec((None, S, LANES), lambda hp, g: (g, 0, hp))
    acc_spec = pl.BlockSpec((8, LANES), lambda hp, g: (0, 0))
    return _call(
        body, sides, name="attn_bwd", grid=(4, 3),
        in_specs=_qk_specs() + _tab_specs() + [_vec_spec(), _vec_spec(), nat_spec, nat_spec, nat_spec],
        out_specs=[out_spec] * 3 + [acc_spec] * 2,
        out_shape=[jax.ShapeDtypeStruct((QKV // PLANE, S, PLANE), BF16)] * 3 + [jax.ShapeDtypeStruct((8, LANES), F32)] * 2,
        scratch_shapes=[pltpu.VMEM((S, LANES), BF16)] * 4 + [pltpu.VMEM((S, LANES), F32)] * 13,
        args=(proj, proj, proj, *tabs, qw2, kw2, d_attn, attn, lse))


PADR = 16
CT = 128


def _conv_specs():
    return [pl.BlockSpec((S, CC), lambda i: (0, OFF_CA // CC)), pl.BlockSpec((S, CC), lambda i: (0, OFF_CB // CC))]


NCB = CC // LANES


def _pad_zero(pad):
    for cb in range(NCB):
        pad[cb, 0:PADR, :] = jnp.zeros((PADR, LANES), F32)
        pad[cb, PADR + S:PADR + S + PADR, :] = jnp.zeros((PADR, LANES), F32)


def _pad_store(pad, row0, n, val):
    for cb in range(NCB):
        pad[cb, pl.ds(pl.multiple_of(row0 + PADR, 8), n), :] = val[:, cb * LANES:(cb + 1) * LANES]


def _taps(pad_ref, cb, s0, weights):
    acc = jnp.zeros((CT, LANES), F32)
    for k in range(KW):
        acc = acc + weights[k] * pad_ref[cb, pl.ds(s0 + k + 1, CT), :]
    return acc


def conv_fwd(proj, conv_w, conv_b, ln_w, ln_b, sides=()):
    def body(a_ref, b_ref, w_ref, cb_ref, lw_ref, lb_ref, c_ref, u3_ref, upad):
        _pad_zero(upad)

        def glu(i, _):
            rows = pl.ds(pl.multiple_of(i * TM, TM), TM)
            _pad_store(upad, i * TM, TM, a_ref[rows, :] * _sigmoid(b_ref[rows, :]))
            return 0

        lax.fori_loop(0, S // TM, glu, 0)

        def chunk(i, _):
            s0 = pl.multiple_of(i * CT, CT)
            for cb in range(CC // LANES):
                cols = slice(cb * LANES, (cb + 1) * LANES)
                w = [w_ref[k:k + 1, cols] for k in range(KW)]
                c_ref[pl.ds(s0, CT), cols] = _taps(upad, cb, s0, w) + cb_ref[:, cols]
            cv = c_ref[pl.ds(s0, CT), :]
            mu = jnp.mean(cv, axis=-1, keepdims=True)
            xc = cv - mu
            rstd = lax.rsqrt(jnp.mean(xc * xc, axis=-1, keepdims=True) + EPS)
            yl = xc * rstd * lw_ref[...] + lb_ref[...]
            u3_ref[pl.ds(s0, CT), :] = (yl * _sigmoid(yl)).astype(BF16)
            return 0

        lax.fori_loop(0, S // CT, chunk, 0)

    vec = pl.BlockSpec((1, CC), lambda i: (0, 0))
    full = pl.BlockSpec((S, CC), lambda i: (0, 0))
    return _call(
        body, sides, name="conv_fwd", grid=(1,),
        in_specs=_conv_specs() + [pl.BlockSpec((KW, CC), lambda i: (0, 0)), vec, vec, vec],
        out_specs=[full, full],
        out_shape=[jax.ShapeDtypeStruct((S, CC), F32), jax.ShapeDtypeStruct((S, CC), BF16)],
        scratch_shapes=[pltpu.VMEM((NCB, S + 2 * PADR, LANES), F32)],
        args=(proj, proj, conv_w, conv_b, ln_w, ln_b))


def conv_bwd(proj, cpre, d_u3, conv_w, conv_w_rev, ln_w, ln_b, sides=()):
    def body(a_ref, b_ref, c_ref, du3_ref, w_ref, wr_ref, lw_ref, lb_ref,
             dc_ref, gw_ref, gcb_ref, glw_ref, glb_ref, upad, dpad):
        _pad_zero(upad)
        _pad_zero(dpad)
        gw_ref[...] = jnp.zeros_like(gw_ref)

        def ln_bwd(i, carry):
            gcb, glw, glb = carry
            rows = pl.ds(pl.multiple_of(i * TM, TM), TM)
            _pad_store(upad, i * TM, TM, a_ref[rows, :] * _sigmoid(b_ref[rows, :]))
            cv = c_ref[rows, :]
            mu = jnp.mean(cv, axis=-1, keepdims=True)
            xc = cv - mu
            rstd = lax.rsqrt(jnp.mean(xc * xc, axis=-1, keepdims=True) + EPS)
            xh = xc * rstd
            yl = xh * lw_ref[...] + lb_ref[...]
            dyl = du3_ref[rows, :] * _dsilu(yl, _sigmoid(yl))
            dxh = dyl * lw_ref[...]
            dcv = rstd * (dxh - jnp.mean(dxh, axis=-1, keepdims=True)
                          - xh * jnp.mean(dxh * xh, axis=-1, keepdims=True))
            _pad_store(dpad, i * TM, TM, dcv)
            return (gcb + jnp.sum(dcv, axis=0, keepdims=True),
                    glw + jnp.sum(dyl * xh, axis=0, keepdims=True),
                    glb + jnp.sum(dyl, axis=0, keepdims=True))

        z = jnp.zeros((1, CC), F32)
        gcb, glw, glb = lax.fori_loop(0, S // TM, ln_bwd, (z, z, z))
        gcb_ref[...] = gcb
        glw_ref[...] = glw
        glb_ref[...] = glb

        def chunk(i, _):
            s0 = pl.multiple_of(i * CT, CT)
            for cb in range(CC // LANES):
                cols = slice(cb * LANES, (cb + 1) * LANES)
                wr = [wr_ref[k:k + 1, cols] for k in range(KW)]
                du = _taps(dpad, cb, s0, wr)
                dcv = dpad[cb, pl.ds(s0 + PADR, CT), :]
                for k in range(KW):
                    gw_ref[k:k + 1, cols] = gw_ref[k:k + 1, cols] + jnp.sum(
                        upad[cb, pl.ds(s0 + k + 1, CT), :] * dcv, axis=0, keepdims=True)
                av = a_ref[pl.ds(s0, CT), cols]
                sb = _sigmoid(b_ref[pl.ds(s0, CT), cols])
                dc_ref[0, pl.ds(s0, CT), cols] = (du * sb).astype(BF16)
                dc_ref[1, pl.ds(s0, CT), cols] = (du * av * sb * (1.0 - sb)).astype(BF16)
            return 0

        lax.fori_loop(0, S // CT, chunk, 0)

    vec = pl.BlockSpec((1, CC), lambda i: (0, 0))
    full = pl.BlockSpec((S, CC), lambda i: (0, 0))
    wsp = pl.BlockSpec((KW, CC), lambda i: (0, 0))
    return _call(
        body, sides, name="conv_bwd", grid=(1,),
        in_specs=_conv_specs() + [full, full, wsp, wsp, vec, vec],
        out_specs=[pl.BlockSpec((2, S, CC), lambda i: (0, 0, 0)), wsp, vec, vec, vec],
        out_shape=[jax.ShapeDtypeStruct((2, S, CC), BF16), jax.ShapeDtypeStruct((KW, CC), F32)]
        + [jax.ShapeDtypeStruct((1, CC), F32)] * 3,
        scratch_shapes=[pltpu.VMEM((NCB, S + 2 * PADR, LANES), F32)] * 2,
        args=(proj, proj, cpre, d_u3, conv_w, conv_w_rev, ln_w, ln_b))


def _gate_specs():
    return [_row(CC, col=OFF_GA // CC + j) for j in range(4)]


def _gates(g_refs, bg_ref):
    ga = _sigmoid(jnp.concatenate([g_refs[0][...], g_refs[1][...]], axis=1) + bg_ref[0:1, :])
    gb = _sigmoid(jnp.concatenate([g_refs[2][...], g_refs[3][...]], axis=1) + bg_ref[1:2, :])
    return ga, gb


def mix_out(x, proj, b_gate, attn, u3, w_o, w_pw, w_out):
    def body(x_ref, g0, g1, g2, g3, bg_ref, at_ref, u3_ref, wo_ref, wp_ref, wout_ref,
             x1_ref, z_ref, ya_ref, yb_ref):
        ga, gb = _gates((g0, g1, g2, g3), bg_ref)
        ya = _dot_nt(at_ref[...], wo_ref[...])
        yb = _dot_nt(u3_ref[...], wp_ref[...])
        z = (ga * ya + gb * yb).astype(BF16)
        ya_ref[...] = ya.astype(BF16)
        yb_ref[...] = yb.astype(BF16)
        z_ref[...] = z
        x1_ref[...] = x_ref[...] + _dot(z, wout_ref[...])

    return pl.pallas_call(
        body, name="mix_out", grid=(S // TM,),
        in_specs=[_row(D)] + _gate_specs() + [_res((2, D)), _row(CC), _row(CC),
                                              _res((D, CC)), _res((D, CC)), _res((D, D))],
        out_specs=[_row(D)] * 4,
        out_shape=[jax.ShapeDtypeStruct((S, D), F32)] + [jax.ShapeDtypeStruct((S, D), BF16)] * 3,
        compiler_params=_cp(dimension_semantics=("arbitrary",)),
    )(x, proj, proj, proj, proj, b_gate, attn, u3, w_o, w_pw, w_out)


def out_bwd(d_x1b, proj, b_gate, ya, yb, w_o, w_pw, w_out, sides=()):
    def body(dx_ref, g0, g1, g2, g3, bg_ref, ya_ref, yb_ref, wo_ref, wp_ref, wout_ref,
             dya_ref, dyb_ref, dgl_ref, dat_ref, du3_ref, gbg_ref):
        @pl.when(pl.program_id(0) == 0)
        def _():
            gbg_ref[...] = jnp.zeros_like(gbg_ref)

        ga, gb = _gates((g0, g1, g2, g3), bg_ref)
        dz = _dot_nt(dx_ref[...], wout_ref[...])
        dya = (dz * ga).astype(BF16)
        dyb = (dz * gb).astype(BF16)
        dgla = dz * ya_ref[...].astype(F32) * ga * (1.0 - ga)
        dglb = dz * yb_ref[...].astype(F32) * gb * (1.0 - gb)
        dya_ref[...] = dya
        dyb_ref[...] = dyb
        for j in range(2):
            dgl_ref[j] = dgla[:, j * PLANE:(j + 1) * PLANE].astype(BF16)
            dgl_ref[2 + j] = dglb[:, j * PLANE:(j + 1) * PLANE].astype(BF16)
        gbg_ref[0:1, :] = gbg_ref[0:1, :] + jnp.sum(dgla, axis=0, keepdims=True)
        gbg_ref[1:2, :] = gbg_ref[1:2, :] + jnp.sum(dglb, axis=0, keepdims=True)
        dat_ref[...] = _dot(dya, wo_ref[...])
        du3_ref[...] = _dot(dyb, wp_ref[...])

    return _call(
        body, sides, name="out_bwd", grid=(S // TM,),
        in_specs=[_row(D)] + _gate_specs() + [_res((2, D)), _row(D), _row(D),
                                              _res((D, CC)), _res((D, CC)), _res((D, D))],
        out_specs=[_row(D), _row(D), _planes(2 * D), _row(CC), _row(CC), pl.BlockSpec((2, D), lambda i: (0, 0))],
        out_shape=[jax.ShapeDtypeStruct((S, D), BF16)] * 2 + [jax.ShapeDtypeStruct((2 * D // PLANE, S, PLANE), BF16)]
        + [jax.ShapeDtypeStruct((S, CC), F32)] * 2 + [jax.ShapeDtypeStruct((2, D), F32)],
        args=(d_x1b, proj, proj, proj, proj, b_gate, ya, yb, w_o, w_pw, w_out))


def ffn_in(x1, norm_w, w_ffn_in, sides=()):
    half = FF // 2

    def body(x_ref, nw_ref, w_ref, h_ref, gu_ref, f_ref):
        xv = x_ref[...]
        r = lax.rsqrt(jnp.mean(xv * xv, axis=-1, keepdims=True) + EPS)
        h = (xv * r * nw_ref[...]).astype(BF16)
        h_ref[...] = h
        for j in range(2):
            gt = _dot_nt(h, w_ref[j * half:(j + 1) * half, :])
            up = _dot_nt(h, w_ref[FF + j * half:FF + (j + 1) * half, :])
            gu_ref[:, j * half:(j + 1) * half] = gt.astype(BF16)
            gu_ref[:, FF + j * half:FF + (j + 1) * half] = up.astype(BF16)
            f_ref[:, j * half:(j + 1) * half] = (gt * _sigmoid(gt) * up).astype(BF16)

    return _call(
        body, sides, name="ffn_in", grid=(S // TM,),
        in_specs=[_row(D), _res((1, D)), _res((2 * FF, D))],
        out_specs=[_row(D), _row(2 * FF), _row(FF)],
        out_shape=[jax.ShapeDtypeStruct((S, D), BF16), jax.ShapeDtypeStruct((S, 2 * FF), BF16),
                   jax.ShapeDtypeStruct((S, FF), BF16)],
        args=(x1, norm_w, w_ffn_in))


def ffn_out_loss(x1, f, w_ffn_out, target):
    def body(x_ref, f_ref, w_ref, t_ref, dy_ref, dyb_ref, sq_ref):
        @pl.when(pl.program_id(0) == 0)
        def _():
            sq_ref[...] = jnp.zeros_like(sq_ref)

        diff = x_ref[...] + _dot(f_ref[...], w_ref[...]) - t_ref[...]
        dy = diff * (1.0 / D)
        dy_ref[...] = dy
        dyb_ref[...] = dy.astype(BF16)
        sq_ref[...] = sq_ref[...] + jnp.sum((diff * diff).reshape(TM // 8, 8, D), axis=0)

    return pl.pallas_call(
        body, name="ffn_out_loss", grid=(S // TM,),
        in_specs=[_row(D), _row(FF), _res((FF, D)), _row(D)],
        out_specs=[_row(D), _row(D), pl.BlockSpec((8, D), lambda i: (0, 0))],
        out_shape=[jax.ShapeDtypeStruct((S, D), F32), jax.ShapeDtypeStruct((S, D), BF16),
                   jax.ShapeDtypeStruct((8, D), F32)],
        compiler_params=_cp(dimension_semantics=("arbitrary",)),
    )(x1, f, w_ffn_out, target)


def _rms_bwd(xv, nw, dh):
    r = lax.rsqrt(jnp.mean(xv * xv, axis=-1, keepdims=True) + EPS)
    xn = xv * r
    dxn = dh * nw
    dx = r * (dxn - xn * jnp.mean(dxn * xn, axis=-1, keepdims=True))
    return dx, dh * xn


def ffn_bwd(dy, dyb, gu, x1, norm_w, w_ffn_in, w_ffn_out, sides=()):
    def body(dy_ref, dyb_ref, gu_ref, x_ref, nw_ref, wi_ref, wo_ref, dgu_ref, dx_ref, dxb_ref, gn_ref):
        @pl.when(pl.program_id(0) == 0)
        def _():
            gn_ref[...] = jnp.zeros_like(gn_ref)

        df = _dot_nt(dyb_ref[...], wo_ref[...])
        gt = gu_ref[:, 0:FF].astype(F32)
        up = gu_ref[:, FF:2 * FF].astype(F32)
        sg = _sigmoid(gt)
        dgt = (df * up * _dsilu(gt, sg)).astype(BF16)
        dup = (df * gt * sg).astype(BF16)
        dgu_ref[:, 0:FF] = dgt
        dgu_ref[:, FF:2 * FF] = dup
        dh = _dot(dgt, wi_ref[0:FF, :]) + _dot(dup, wi_ref[FF:2 * FF, :])
        dxn, gw = _rms_bwd(x_ref[...], nw_ref[...], dh)
        dx = dy_ref[...] + dxn
        dx_ref[...] = dx
        dxb_ref[...] = dx.astype(BF16)
        gn_ref[...] = gn_ref[...] + jnp.sum(gw, axis=0, keepdims=True)

    return _call(
        body, sides, name="ffn_bwd", grid=(S // TM,),
        in_specs=[_row(D), _row(D), _row(2 * FF), _row(D), _res((1, D)), _res((2 * FF, D)), _res((FF, D))],
        out_specs=[_row(2 * FF), _row(D), _row(D), pl.BlockSpec((1, D), lambda i: (0, 0))],
        out_shape=[jax.ShapeDtypeStruct((S, 2 * FF), BF16), jax.ShapeDtypeStruct((S, D), F32),
                   jax.ShapeDtypeStruct((S, D), BF16), jax.ShapeDtypeStruct((1, D), F32)],
        args=(dy, dyb, gu, x1, norm_w, w_ffn_in, w_ffn_out))


def in_bwd(d_q, d_k, d_v, d_conv, d_gl, w_in, x, d_x1, norm_w, sides=()):
    segs = ((OFF_Q, QKV), (OFF_K, QKV), (OFF_V, QKV), (OFF_CA, 2 * CC), (OFF_GA, 2 * D))

    def body(dq_ref, dk_ref, dv_ref, dc_ref, dg_ref, w_ref, x_ref, dx1_ref, nw_ref, gx_ref, gn_ref):
        @pl.when(pl.program_id(0) == 0)
        def _():
            gn_ref[...] = jnp.zeros_like(gn_ref)

        dh = jnp.zeros((TM, D), F32)
        for ref, (off, width) in zip((dq_ref, dk_ref, dv_ref, dc_ref, dg_ref), segs):
            for j in range(width // PLANE):
                dh = dh + _dot(ref[j], w_ref[off + j * PLANE:off + (j + 1) * PLANE, :])
        dxn, gw = _rms_bwd(x_ref[...], nw_ref[...], dh)
        gx_ref[...] = dx1_ref[...] + dxn
        gn_ref[...] = gn_ref[...] + jnp.sum(gw, axis=0, keepdims=True)

    return _call(
        body, sides, name="in_bwd", grid=(S // TM,),
        in_specs=[_planes(QKV)] * 3 + [_planes(2 * CC), _planes(2 * D), _res((INW, D)), _row(D), _row(D), _res((1, D))],
        out_specs=[_row(D), pl.BlockSpec((1, D), lambda i: (0, 0))],
        out_shape=[jax.ShapeDtypeStruct((S, D), F32), jax.ShapeDtypeStruct((1, D), F32)],
        args=(d_q, d_k, d_v, d_conv, d_gl, w_in, x, d_x1, norm_w))


def mm_tn(name, pairs, tm):
    n = len(pairs)
    M = pairs[0][0].shape[1]
    widths = [b.shape[1] for _, b in pairs]

    def body(*refs):
        for a_ref, b_ref, o_ref, ob_ref in zip(refs[0:2 * n:2], refs[1:2 * n:2], refs[2 * n::2], refs[2 * n + 1::2]):
            r = _dot_tn(a_ref[...], b_ref[...])
            o_ref[...] = r
            ob_ref[...] = r.astype(BF16)

    return _call(
        body, name=name, grid=(M // tm,),
        in_specs=[sp for N in widths for sp in (pl.BlockSpec((S, tm), lambda i: (0, i)), _res((S, N)))],
        out_specs=[pl.BlockSpec((tm, N), lambda i: (i, 0)) for N in widths for _ in range(2)],
        out_shape=[jax.ShapeDtypeStruct((M, N), dt) for N in widths for dt in (F32, BF16)],
        args=[t for pair in pairs for t in pair])


GW_IN_TN = PLANE
GW_IN_SPLIT = (512, 512)


def gw_in_t(name, h, d_segs, col0, hw, sides=()):
    tn = GW_IN_TN
    starts, t0 = [], 0
    for seg in d_segs:
        starts.append(t0)
        t0 += seg.shape[0]
    ntiles = [seg.shape[0] for seg in d_segs]

    def body(h_ref, *refs):
        a_refs, o_ref, ob_ref = refs[:-2], refs[-2], refs[-1]
        n = pl.program_id(0)
        for a_ref, st, nt in zip(a_refs, starts, ntiles):
            @pl.when((n >= st) & (n < st + nt))
            def _(a_ref=a_ref):
                r = _dot_tn(a_ref[...], h_ref[...])
                o_ref[...] = r
                ob_ref[...] = r.astype(BF16)

    def seg_spec(st, nt):
        return pl.BlockSpec((None, S, tn), lambda n: (jnp.clip(n - st, 0, nt - 1), 0, 0))

    res = _call(
        body, sides, name=name, grid=(INW // tn,),
        in_specs=[pl.BlockSpec((S, hw), lambda n: (0, col0 // hw))] + [seg_spec(st, nt) for st, nt in zip(starts, ntiles)],
        out_specs=[pl.BlockSpec((tn, hw), lambda n: (n, 0))] * 2,
        out_shape=[jax.ShapeDtypeStruct((INW, hw), F32), jax.ShapeDtypeStruct((INW, hw), BF16)],
        args=(h, *d_segs))
    return (res[0], res[1]) if sides else (res, [])


def _place():
    x, y, c = lax.axis_index("x"), lax.axis_index("y"), lax.axis_index("c")
    chips = [(1 - x, y), (x, 1 - y), (1 - x, 1 - y)]
    return x, y, c, chips


def _sems(n):
    return pltpu.SemaphoreType.DMA((n,))


def _remote(src, dst, send, recv, k, to):
    return pltpu.make_async_remote_copy(src_ref=src, dst_ref=dst, send_sem=send.at[k], recv_sem=recv.at[k],
                                        device_id=to, device_id_type=MESH)


def _cast_rows(dst, src, cols=slice(None)):
    rows = src.shape[0]
    step = next((s for s in (128, 64, 32, 16) if rows % s == 0), rows)
    for r0 in range(0, rows, step):
        dst[r0:r0 + step, cols] = src[r0:r0 + step, :].astype(dst.dtype)


def comm_only(name, sides):
    def body():
        pass

    return _call(body, sides, name=name, grid=(1,), in_specs=[], out_specs=[], out_shape=[], args=())[1]


def ag_blocks(shard, dtype):
    R, W = shard.shape

    def copy(outs, scr, k, block, to, src=None):
        dst = outs[0].at[block]
        return _remote(dst if src is None else src, dst, scr[1], scr[2], k, to)

    def local(outs, scr, me):
        return pltpu.make_async_copy(scr[0], outs[0].at[me], scr[3].at[0])

    def start(ins, outs, scr):
        x, y, c, chips = _place()
        me = 4 * x + 2 * y + c
        _cast_rows(scr[0], ins[0])
        local(outs, scr, me).start()
        copy(outs, scr, 0, me, (x, y, 1 - c), src=scr[0]).start()
        for j, (cx, cy) in enumerate(chips):
            copy(outs, scr, 1 + j, me, (cx, cy, c), src=scr[0]).start()

    def finish(ins, outs, scr):
        x, y, c, chips = _place()
        me, sib = 4 * x + 2 * y + c, (x, y, 1 - c)
        passed = []
        for j, (cx, cy) in enumerate(chips):
            theirs = 4 * cx + 2 * cy + c
            copy(outs, scr, 1 + j, theirs, (x, y, c)).wait_recv()
            fwd = copy(outs, scr, 4 + j, theirs, sib)
            fwd.start()
            passed.append(fwd)
        copy(outs, scr, 0, 4 * x + 2 * y + 1 - c, (x, y, c)).wait_recv()
        for j, (cx, cy) in enumerate(chips):
            copy(outs, scr, 4 + j, 4 * cx + 2 * cy + 1 - c, (x, y, c)).wait_recv()
        copy(outs, scr, 0, me, sib, src=scr[0]).wait_send()
        for j, (cx, cy) in enumerate(chips):
            copy(outs, scr, 1 + j, me, (cx, cy, c), src=scr[0]).wait_send()
        for fwd in passed:
            fwd.wait_send()
        local(outs, scr, me).wait()

    return Side((shard,), (VMEM,), (jax.ShapeDtypeStruct((NDEV, R, W), dtype),),
                (pltpu.VMEM((R, W), dtype), _sems(7), _sems(7), _sems(1)), start, finish, None, "dsxy")


def ag_blocks_relay(shard, dtype, transpose=False):
    R, W = shard.shape[::-1] if transpose else shard.shape
    half = R // 2

    def copy(outs, scr, k, block, to, src=None, rows=None):
        dst = outs[0].at[block] if rows is None else outs[0].at[block, pl.ds(rows * half, half), :]
        return _remote(dst if src is None else src, dst, scr[1], scr[2], k, to)

    def local(outs, scr, me):
        return pltpu.make_async_copy(scr[0], outs[0].at[me], scr[3].at[0])

    def own(outs, scr):
        x, y, c, _ = _place()
        me = 4 * x + 2 * y + c
        return [copy(outs, scr, k, me, to, src=scr[0])
                for k, to in enumerate([(x, y, 1 - c), (1 - x, y, c), (x, 1 - y, c)])]

    def start(ins, outs, scr):
        x, y, c, _ = _place()
        if transpose:
            scr[0][...] = ins[0][...].T.astype(dtype)
        else:
            _cast_rows(scr[0], ins[0])
        local(outs, scr, 4 * x + 2 * y + c).start()
        for cp in own(outs, scr):
            cp.start()

    def passed_on(outs, scr):
        x, y, c, _ = _place()
        sib, xn, yn = (x, y, 1 - c), (1 - x, y, c), (x, 1 - y, c)
        b_xn, b_yn, b_dg = 4 * (1 - x) + 2 * y + c, 4 * x + 2 * (1 - y) + c, 4 * (1 - x) + 2 * (1 - y) + c
        near = [copy(outs, scr, 5, b_xn, yn, rows=0), copy(outs, scr, 3, b_xn, sib),
                copy(outs, scr, 6, b_yn, xn, rows=1), copy(outs, scr, 4, b_yn, sib)]
        far = [copy(outs, scr, 7, b_dg, sib, rows=0), copy(outs, scr, 8, b_dg, sib, rows=1)]
        return (b_xn, b_yn, b_dg), near, far

    def mid(ins, outs, scr):
        x, y, c, _ = _place()
        (b_xn, b_yn, _), near, _ = passed_on(outs, scr)
        copy(outs, scr, 1, b_xn, (x, y, c)).wait_recv()
        near[0].start()
        near[1].start()
        copy(outs, scr, 2, b_yn, (x, y, c)).wait_recv()
        near[2].start()
        near[3].start()

    def finish(ins, outs, scr):
        x, y, c, _ = _place()
        here = (x, y, c)
        (b_xn, b_yn, b_dg), near, far = passed_on(outs, scr)
        copy(outs, scr, 5, b_dg, here, rows=0).wait_recv()
        far[0].start()
        copy(outs, scr, 6, b_dg, here, rows=1).wait_recv()
        far[1].start()
        flip = 1 - 2 * c
        copy(outs, scr, 0, 4 * x + 2 * y + 1 - c, here).wait_recv()
        copy(outs, scr, 3, b_xn + flip, here).wait_recv()
        copy(outs, scr, 4, b_yn + flip, here).wait_recv()
        copy(outs, scr, 7, b_dg + flip, here, rows=0).wait_recv()
        copy(outs, scr, 8, b_dg + flip, here, rows=1).wait_recv()
        for cp in own(outs, scr) + near + far:
            cp.wait_send()
        local(outs, scr, 4 * x + 2 * y + c).wait()

    return Side((shard,), (VMEM,), (jax.ShapeDtypeStruct((NDEV, R, W), dtype),),
                (pltpu.VMEM((R, W), dtype), _sems(9), _sems(9), _sems(1)), start, finish, mid, "sxy")


def copies_side(args, out_shape, n_copies, plan, peers):
    def copies(ins, outs, scr):
        return [_remote(s_, d_, scr[0], scr[1], i, to) for i, (s_, d_, to) in enumerate(plan(ins, outs))]

    def start(ins, outs, scr):
        for cp in copies(ins, outs, scr):
            cp.start()

    def finish(ins, outs, scr):
        for cp in copies(ins, outs, scr):
            cp.wait()

    return Side(tuple(args), (ANY,) * len(args), tuple(out_shape), (_sems(n_copies), _sems(n_copies)),
                start, finish, None, peers)


def rs_to_sibling(grads):
    out_shape = [jax.ShapeDtypeStruct((4,) + g.shape[1:], BF16) for g in grads]

    def plan(ins, outs):
        x, y, c, _ = _place()
        return [(g.at[2 * k + 1 - c], r.at[k], (x, y, 1 - c)) for g, r in zip(ins, outs) for k in range(4)]

    return copies_side(grads, out_shape, 4 * len(grads), plan, "s")


def rs_to_chips(parts):
    out_shape = [jax.ShapeDtypeStruct((3,) + p.shape[1:], BF16) for p in parts]

    def plan(ins, outs):
        x, y, c, chips = _place()
        return [(p.at[2 * cx + cy], r.at[j], (cx, cy, c))
                for p, r in zip(ins, outs) for j, (cx, cy) in enumerate(chips)]

    return copies_side(parts, out_shape, 3 * len(parts), plan, "dxy")


def rs_to_chips_combined(part):
    _, R, W = part.shape
    half = R // 2
    top, bot = pl.ds(0, half), pl.ds(half, half)

    def copies(ins, outs, scr):
        p, r = ins[0], outs[0]
        loc_a, loc_b, in_x, in_y, comb_a, comb_b, send, recv, loc = scr
        x, y, c, _ = _place()
        xn, yn = (1 - x, y, c), (x, 1 - y, c)
        k_xn, k_yn, k_dg = 2 * (1 - x) + y, 2 * x + 1 - y, 2 * (1 - x) + 1 - y
        direct = [_remote(p.at[k_xn, top, :], r.at[0, top, :], send, recv, 0, xn),
                  _remote(p.at[k_yn, bot, :], r.at[1, bot, :], send, recv, 1, yn),
                  _remote(p.at[k_dg, top, :], in_x, send, recv, 2, xn),
                  _remote(p.at[k_dg, bot, :], in_y, send, recv, 3, yn)]
        combined = [_remote(comb_a, r.at[1, top, :], send, recv, 4, yn),
                    _remote(comb_b, r.at[0, bot, :], send, recv, 5, xn)]
        local = [pltpu.make_async_copy(p.at[k_yn, top, :], loc_a, loc.at[0]),
                 pltpu.make_async_copy(p.at[k_xn, bot, :], loc_b, loc.at[1])]
        return direct, combined, local

    def start(ins, outs, scr):
        direct, _, local = copies(ins, outs, scr)
        for cp in local + direct:
            cp.start()

    def mid(ins, outs, scr):
        loc_a, loc_b, in_x, in_y, comb_a, comb_b = scr[:6]
        direct, combined, local = copies(ins, outs, scr)
        for mine, arrival, inbox, out, nxt in ((local[0], direct[2], in_x, comb_a, combined[0]),
                                               (local[1], direct[3], in_y, comb_b, combined[1])):
            mine.wait()
            arrival.wait_recv()
            src = loc_a if out is comb_a else loc_b
            out[...] = (src[...].astype(F32) + inbox[...].astype(F32)).astype(BF16)
            nxt.start()

    def finish(ins, outs, scr):
        direct, combined, _ = copies(ins, outs, scr)
        direct[0].wait_recv()
        direct[1].wait_recv()
        combined[0].wait_recv()
        combined[1].wait_recv()
        for cp in direct + combined:
            cp.wait_send()

    buf = pltpu.VMEM((half, W), BF16)
    return Side((part,), (ANY,), (jax.ShapeDtypeStruct((2, R, W), BF16),),
                (buf, buf, buf, buf, buf, buf, _sems(6), _sems(6), _sems(2)), start, finish, mid, "xy")


ADAM_TILE_BYTES = 3 * 512 * 1024


def _row_tiles(rows, width):
    return 2 if rows % 32 == 0 and rows * width * 4 > ADAM_TILE_BYTES else 1


def chip_sum(name, grads, recvs, c_idx, chip_idx):
    n = len(grads)

    def body(s_ref, *refs):
        k = pl.program_id(0)
        for g_ref, r_ref, p_ref, own_ref in zip(refs[:n], refs[n:2 * n], refs[2 * n::2], refs[2 * n + 1::2]):
            tot = g_ref[0] + r_ref[0].astype(F32)
            p_ref[0] = tot.astype(BF16)

            @pl.when(k == s_ref[1])
            def _(own_ref=own_ref, tot=tot):
                own_ref[...] = tot

    def block(g):
        return (1,) + g.shape[1:]

    grid_spec = pltpu.PrefetchScalarGridSpec(
        num_scalar_prefetch=1, grid=(4,),
        in_specs=[pl.BlockSpec(block(g), lambda k, s: (2 * k + s[0], 0, 0)) for g in grads]
        + [pl.BlockSpec(block(g), lambda k, s: (k, 0, 0)) for g in grads],
        out_specs=[sp for g in grads for sp in (pl.BlockSpec(block(g), lambda k, s: (k, 0, 0)),
                                                pl.BlockSpec(g.shape[1:], lambda k, s: (0, 0)))])
    res = pl.pallas_call(
        body, name=name, grid_spec=grid_spec,
        out_shape=[sh for g in grads for sh in (jax.ShapeDtypeStruct((4,) + g.shape[1:], BF16),
                                                jax.ShapeDtypeStruct(g.shape[1:], F32))],
        compiler_params=_cp(dimension_semantics=("arbitrary",)),
    )(jnp.stack([c_idx, chip_idx]), *grads, *recvs)
    return [(res[2 * j], res[2 * j + 1]) for j in range(n)]


def _adamw(w, g, m, v):
    m2 = ADAM_B1 * m + (1.0 - ADAM_B1) * g
    v2 = ADAM_B2 * v + (1.0 - ADAM_B2) * (g * g)
    m_hat = m2 / (1.0 - ADAM_B1 ** ADAM_STEP)
    v_hat = v2 / (1.0 - ADAM_B2 ** ADAM_STEP)
    delta = -ADAM_LR * (m_hat / (jnp.sqrt(v_hat) + ADAM_EPS) + ADAM_WD * w)
    return delta, m2, v2


def shard_adam(name, owns, recvs, w, m, v):
    n = len(owns)
    R = owns[0].shape[0]
    ct = min(o.shape[1] for o in owns)
    first = [sum(o.shape[1] for o in owns[:j]) // ct for j in range(n)]
    count = [o.shape[1] // ct for o in owns]
    nt = _row_tiles(R, ct)
    tr = R // nt

    def body(*refs):
        o_refs, r_refs = refs[:n], refs[n:2 * n]
        w_ref, m_ref, v_ref, g_ref, d_ref, nm_ref, nv_ref = refs[2 * n:]
        g = None
        for j in range(n):
            gj = o_refs[j][...]
            for q in range(recvs[j].shape[0]):
                gj = gj + r_refs[j][q].astype(F32)
            g = gj if g is None else jnp.where(pl.program_id(0) >= first[j], gj, g)
        delta, m2, v2 = _adamw(w_ref[...], g, m_ref[...], v_ref[...])
        g_ref[...] = g
        d_ref[...] = delta
        nm_ref[...] = m2
        nv_ref[...] = v2

    def part(j):
        return pl.BlockSpec((tr, ct), lambda k, i: (i, jnp.clip(k - first[j], 0, count[j] - 1)))

    def part3(j):
        return pl.BlockSpec((recvs[j].shape[0], tr, ct), lambda k, i: (0, i, jnp.clip(k - first[j], 0, count[j] - 1)))

    C = sum(count) * ct
    tile = pl.BlockSpec((tr, ct), lambda k, i: (i, k))
    return pl.pallas_call(
        body, name=name, grid=(sum(count), nt),
        in_specs=[part(j) for j in range(n)] + [part3(j) for j in range(n)] + [tile, tile, tile],
        out_specs=[tile] * 4, out_shape=[jax.ShapeDtypeStruct((R, C), F32)] * 4,
        compiler_params=_cp(dimension_semantics=("arbitrary", "arbitrary")),
    )(*owns, *recvs, w, m, v)


def block_adam(name, items):
    n = len(items)

    def body(*refs):
        for j, item in enumerate(items):
            o_ref, r_ref, w_ref, m_ref, v_ref = refs[5 * j:5 * j + 5]
            g = o_ref[...]
            for q in range(r_ref.shape[0]):
                g = g + r_ref[q].astype(F32)
            t = (lambda a: a.T) if item[5] else (lambda a: a)
            delta, m2, v2 = _adamw(t(w_ref[...]), g, t(m_ref[...]), t(v_ref[...]))
            for ref, val in zip(refs[5 * n + 4 * j:5 * n + 4 * j + 4], (g, delta, m2, v2)):
                ref[...] = t(val)

    args = [a for item in items for a in item[:5]]
    out_shape = [jax.ShapeDtypeStruct(item[2].shape, F32) for item in items for _ in range(4)]
    res = pl.pallas_call(
        body, name=name, grid=(1,), in_specs=[VMEM] * len(args), out_specs=[VMEM] * len(out_shape),
        out_shape=out_shape, compiler_params=_cp(dimension_semantics=("arbitrary",)))(*args)
    return [tuple(res[4 * j:4 * j + 4]) for j in range(n)]


ROW_N1, ROW_N2, ROW_BG, ROW_QN, ROW_KN, ROW_CB, ROW_LW, ROW_LB, ROW_CW = 0, 1, 2, 4, 5, 6, 7, 8, 9
PACK_ROWS = 40
SMALL = ("norm1_w", "norm2_w", "b_gate", "q_norm_w", "k_norm_w", "conv_b", "conv_ln_w", "conv_ln_b", "conv_w")


def small_sync(g, sq, sides=()):
    ns = len(SMALL)

    def copies(refs):
        pack, recv, send_sems, recv_sems = refs[ns + 2:]
        x, y, c, _ = _place()
        return [pltpu.make_async_remote_copy(
            src_ref=pack, dst_ref=recv.at[4 * x + 2 * y + c], send_sem=send_sems.at[k - 1],
            recv_sem=recv_sems.at[k - 1], device_id=(x ^ (k >> 2), y ^ ((k >> 1) & 1), c ^ (k & 1)),
            device_id_type=MESH) for k in range(1, NDEV)]

    def body(*refs):
        gi = dict(zip(SMALL, refs[:ns]))
        sq_ref, tot, pack, recv, send_sems, recv_sems = refs[ns:]
        x, y, c, _ = _place()
        me = 4 * x + 2 * y + c

        pack[...] = jnp.zeros_like(pack)
        pack[ROW_KN:ROW_KN + 1, LANES:2 * LANES] = jnp.full((1, LANES), (0.5 / D) * jnp.sum(sq_ref[...]), F32)
        pack[ROW_N1:ROW_N1 + 1, :] = gi["norm1_w"][...]
        pack[ROW_N2:ROW_N2 + 1, :] = gi["norm2_w"][...]
        pack[ROW_BG:ROW_BG + 2, :] = gi["b_gate"][...]
        pack[ROW_QN:ROW_QN + 1, 0:HD] = gi["q_norm_w"][...]
        pack[ROW_KN:ROW_KN + 1, 0:HD] = gi["k_norm_w"][...]
        pack[ROW_CB:ROW_CB + 1, 0:CC] = gi["conv_b"][...]
        pack[ROW_LW:ROW_LW + 1, 0:CC] = gi["conv_ln_w"][...]
        pack[ROW_LB:ROW_LB + 1, 0:CC] = gi["conv_ln_b"][...]
        pack[ROW_CW:ROW_CW + KW, 0:CC] = gi["conv_w"][...]

        for cp in copies(refs):
            cp.start()
        recv[me] = pack[...]

    def tail(*refs):
        tot, recv = refs[ns + 1], refs[ns + 3]
        for cp in copies(refs):
            cp.wait()
        acc = recv[0]
        for p in range(1, NDEV):
            acc = acc + recv[p]
        tot[...] = acc

    args = [g[k] for k in SMALL] + [sq]
    res = _call(
        body, sides, name="small_sync", grid=(1,), in_specs=[VMEM] * len(args), out_specs=[VMEM],
        out_shape=[jax.ShapeDtypeStruct((PACK_ROWS, D), F32)],
        scratch_shapes=[pltpu.VMEM((PACK_ROWS, D), F32), pltpu.VMEM((NDEV, PACK_ROWS, D), F32),
                        _sems(NDEV - 1), _sems(NDEV - 1)],
        args=args, own_comm=True, tail=tail)
    return (res[0][0], res[1]) if sides else res[0]


def small_adam(tot, w, m, v, me):
    ns = len(SMALL)

    def body(me_ref, tot, *refs):
        wi = dict(zip(SMALL, refs[:ns]))
        mi = dict(zip(SMALL, refs[ns:2 * ns]))
        vi = dict(zip(SMALL, refs[2 * ns:3 * ns]))
        outs = refs[3 * ns:7 * ns]
        loss_ref = refs[7 * ns]
        me = me_ref[0]

        def shard_grad(name):
            if name == "b_gate":
                return tot[ROW_BG:ROW_BG + 2, pl.ds(pl.multiple_of(me * LANES, LANES), LANES)]
            if name == "conv_w":
                win = tot[ROW_CW:ROW_CW + KW, pl.ds(pl.multiple_of((me // 2) * LANES, LANES), LANES)]
                return jnp.where(me % 2 == 1, win[:, HD:LANES], win[:, 0:HD])
            row = {"norm1_w": ROW_N1, "norm2_w": ROW_N2, "q_norm_w": ROW_QN, "k_norm_w": ROW_KN,
                   "conv_b": ROW_CB, "conv_ln_w": ROW_LW, "conv_ln_b": ROW_LB}[name]
            return tot[row:row + 1, 0:wi[name].shape[1]]

        for i, name in enumerate(SMALL):
            gr = shard_grad(name)
            delta, m2, v2 = _adamw(wi[name][...], gr, mi[name][...], vi[name][...])
            outs[4 * i][...] = gr
            outs[4 * i + 1][...] = delta
            outs[4 * i + 2][...] = m2
            outs[4 * i + 3][...] = v2
        loss_ref[...] = tot[ROW_KN:ROW_KN + 1, LANES:2 * LANES]

    out_shape = []
    for name in SMALL:
        out_shape += [jax.ShapeDtypeStruct(w[name].shape, F32)] * 4
    out_shape.append(jax.ShapeDtypeStruct((1, LANES), F32))
    args = [tot] + [w[k] for k in SMALL] + [m[k] for k in SMALL] + [v[k] for k in SMALL]
    grid_spec = pltpu.PrefetchScalarGridSpec(
        num_scalar_prefetch=1, grid=(1,), in_specs=[VMEM] * len(args), out_specs=[VMEM] * len(out_shape))
    res = pl.pallas_call(body, name="small_adam", grid_spec=grid_spec, out_shape=out_shape)(me, *args)
    out = {name: tuple(res[4 * i:4 * i + 4]) for i, name in enumerate(SMALL)}
    return out, res[4 * ns][0, 0]


MATS = ("w_in", "w_o_attn", "w_pw_conv", "w_out", "w_ffn_in", "w_ffn_out")
TRANSPOSED = ("w_in", "w_ffn_in")
WEIGHTS = ("norm1_w", "w_in", "b_gate", "q_norm_w", "k_norm_w", "w_o_attn", "conv_w", "conv_b", "conv_ln_w",
           "conv_ln_b", "w_pw_conv", "w_out", "norm2_w", "w_ffn_in", "w_ffn_out")


def _blocks_to_cols(blocks):
    n, R, C = blocks.shape
    return blocks.transpose(1, 0, 2).reshape(R, n * C)


def kernel(x, positions, norm1_w, w_in, b_gate, q_norm_w, k_norm_w, w_o_attn, conv_w, conv_b, conv_ln_w, conv_ln_b, w_pw_conv, w_out, norm2_w, w_ffn_in, w_ffn_out, loss_target, m_norm1_w, m_w_in, m_b_gate, m_q_norm_w, m_k_norm_w, m_w_o_attn, m_conv_w, m_conv_b, m_conv_ln_w, m_conv_ln_b, m_w_pw_conv, m_w_out, m_norm2_w, m_w_ffn_in, m_w_ffn_out, v_norm1_w, v_w_in, v_b_gate, v_q_norm_w, v_k_norm_w, v_w_o_attn, v_conv_w, v_conv_b, v_conv_ln_w, v_conv_ln_b, v_w_pw_conv, v_w_out, v_norm2_w, v_w_ffn_in, v_w_ffn_out):
    w = dict(norm1_w=norm1_w, w_in=w_in, b_gate=b_gate, q_norm_w=q_norm_w, k_norm_w=k_norm_w, w_o_attn=w_o_attn,
             conv_w=conv_w, conv_b=conv_b, conv_ln_w=conv_ln_w, conv_ln_b=conv_ln_b, w_pw_conv=w_pw_conv,
             w_out=w_out, norm2_w=norm2_w, w_ffn_in=w_ffn_in, w_ffn_out=w_ffn_out)
    m = dict(norm1_w=m_norm1_w, w_in=m_w_in, b_gate=m_b_gate, q_norm_w=m_q_norm_w, k_norm_w=m_k_norm_w,
             w_o_attn=m_w_o_attn, conv_w=m_conv_w, conv_b=m_conv_b, conv_ln_w=m_conv_ln_w,
             conv_ln_b=m_conv_ln_b, w_pw_conv=m_w_pw_conv, w_out=m_w_out, norm2_w=m_norm2_w,
             w_ffn_in=m_w_ffn_in, w_ffn_out=m_w_ffn_out)
    v = dict(norm1_w=v_norm1_w, w_in=v_w_in, b_gate=v_b_gate, q_norm_w=v_q_norm_w, k_norm_w=v_k_norm_w,
             w_o_attn=v_w_o_attn, conv_w=v_conv_w, conv_b=v_conv_b, conv_ln_w=v_conv_ln_w,
             conv_ln_b=v_conv_ln_b, w_pw_conv=v_w_pw_conv, w_out=v_w_out, norm2_w=v_norm2_w,
             w_ffn_in=v_w_ffn_in, w_ffn_out=v_w_ffn_out)
    def two_d(t):
        t = {k: (a[0] if a.ndim == 3 else a) for k, a in t.items()}
        return {k: (a.T if k in TRANSPOSED else a) for k, a in t.items()}

    w, m, v = two_d(w), two_d(m), two_d(v)

    x2, target = x[0], loss_target[0]
    c_idx = lax.axis_index("c").astype(jnp.int32)
    chip_idx = (2 * lax.axis_index("x") + lax.axis_index("y")).astype(jnp.int32)
    qw2 = jnp.tile(w["q_norm_w"], (1, 2))
    kw2 = jnp.tile(w["k_norm_w"], (1, 2))

    ax, ay = lax.axis_index("x"), lax.axis_index("y")
    chip_order = jnp.stack([2 * ax + ay, 2 * (1 - ax) + ay, 2 * ax + 1 - ay, 2 * (1 - ax) + 1 - ay]).astype(jnp.int32)
    h, proj, w_in_blocks, tabs = in_proj_gather(x2, w["norm1_w"], w["w_in"], chip_order, positions.reshape(S, 1))
    w_in_t = w_in_blocks.reshape(INW, D)
    (attn, lse), ((w_ffn_in_blocks,), (w_out_blocks,), (w_o_blocks,), (w_pw_blocks,), (bg_blocks,), (cw_blocks,)) = attn_fwd(
        proj, tabs, qw2, kw2, sides=(ag_blocks_relay(w["w_ffn_in"], BF16), ag_blocks_relay(w["w_out"], BF16),
                                     ag_blocks_relay(w["w_o_attn"], BF16, transpose=True),
                                     ag_blocks_relay(w["w_pw_conv"], BF16, transpose=True),
                                     ag_blocks(w["b_gate"], F32), ag_blocks(w["conv_w"], F32)))
    w_ffn_in_t = w_ffn_in_blocks.reshape(2 * FF, D)
    w_out_f = w_out_blocks.reshape(D, D)
    w_o_t, w_pw_t = w_o_blocks.reshape(D, CC), w_pw_blocks.reshape(D, CC)
    b_gate_f, conv_w_f = _blocks_to_cols(bg_blocks), _blocks_to_cols(cw_blocks)
    cpre, u3 = conv_fwd(proj, conv_w_f, w["conv_b"], w["conv_ln_w"], w["conv_ln_b"])
    x1, z, ya, yb = mix_out(x2, proj, b_gate_f, attn, u3, w_o_t, w_pw_t, w_out_f)
    (h2, gu, f), ((w_ffn_out_blocks,),) = ffn_in(x1, w["norm2_w"], w_ffn_in_t, sides=(ag_blocks_relay(w["w_ffn_out"], BF16),))
    w_ffn_out_f = w_ffn_out_blocks.reshape(FF, D)
    dy, dyb, sq = ffn_out_loss(x1, f, w_ffn_out_f, target)

    g = {}
    def blocks(name, pairs, tm):
        return [t.reshape(NDEV, t.shape[0] // NDEV, t.shape[1]) for t in mm_tn(name, pairs, tm)]

    g_ffn_out, gb_ffn_out = blocks("gw_ffn_out", [(f, dyb)], FF // 2)
    (d_gu, d_x1, d_x1b, g["norm2_w"]), ((ra_ffn_out,),) = ffn_bwd(
        dy, dyb, gu, x1, w["norm2_w"], w_ffn_in_t, w_ffn_out_f, sides=(rs_to_sibling([gb_ffn_out]),))
    g_ffn_in, gb_ffn_in = blocks("gw_ffn_in", [(d_gu, h2)], FF // 2)
    (d_ya, d_yb, d_gl, d_attn, d_u3, g["b_gate"]), ((ra_ffn_in,),) = out_bwd(
        d_x1b, proj, b_gate_f, ya, yb, w_o_t, w_pw_t, w_out_f, sides=(rs_to_sibling([gb_ffn_in]),))
    g_out, gb_out, g_w_o, gb_w_o, g_w_pw, gb_w_pw = blocks(
        "gw_out_o_pw", [(z, d_x1b), (d_ya, attn), (d_yb, u3)], D // 2)
    (d_conv, g["conv_w"], g["conv_b"], g["conv_ln_w"], g["conv_ln_b"]), ((ra_out, ra_w_o, ra_w_pw),) = conv_bwd(
        proj, cpre, d_u3, conv_w_f, conv_w_f[::-1], w["conv_ln_w"], w["conv_ln_b"],
        sides=(rs_to_sibling([gb_out, gb_w_o, gb_w_pw]),))
    (pb_ffn_out, own_ffn_out), (pb_ffn_in, own_ffn_in), (pb_out, own_out), (pb_w_o, own_w_o), (pb_w_pw, own_w_pw) = chip_sum(
        "chip_sum_early", [g_ffn_out, g_ffn_in, g_out, g_w_o, g_w_pw],
        [ra_ffn_out, ra_ffn_in, ra_out, ra_w_o, ra_w_pw], c_idx, chip_idx)
    (d_q, d_k, d_v, gqw, gkw), ((rb_ffn_out, rb_ffn_in, rb_out, rb_w_o, rb_w_pw),) = attn_bwd(
        proj, tabs, qw2, kw2, d_attn, attn, lse,
        sides=(rs_to_chips([pb_ffn_out, pb_ffn_in, pb_out, pb_w_o, pb_w_pw]),))
    g["q_norm_w"] = gqw[0:1, 0:HD] + gqw[0:1, HD:LANES]
    g["k_norm_w"] = gkw[0:1, 0:HD] + gkw[0:1, HD:LANES]
    d_segs = (d_q, d_k, d_v, d_conv, d_gl)
    parts, to_sibling, to_chips, owns, from_chips = [], None, None, [], []
    for k, hw in enumerate(GW_IN_SPLIT):
        sides = tuple(s for s in (to_chips, to_sibling) if s is not None)
        (part, part_b), outs = gw_in_t("gw_in_%d" % k, h, d_segs, sum(GW_IN_SPLIT[:k]), hw, sides=sides)
        outs = list(outs)
        if to_chips is not None:
            from_chips.append(outs.pop(0)[0])
        if to_sibling is not None:
            (pb, own), = chip_sum("chip_sum_w_in_%d" % (k - 1), [parts[-1]], [outs.pop(0)[0]], c_idx, chip_idx)
            owns.append(own)
            to_chips = rs_to_chips_combined(pb)
        else:
            to_chips = None
        parts.append(part.reshape(NDEV, INW // NDEV, hw))
        to_sibling = rs_to_sibling([part_b.reshape(NDEV, INW // NDEV, hw)])
    (grad_x, g["norm1_w"]), ((rb_prev,), (ra_last,)) = in_bwd(
        d_q, d_k, d_v, d_conv, d_gl, w_in_t, x2, d_x1, w["norm1_w"], sides=(to_chips, to_sibling))
    from_chips.append(rb_prev)
    (pb, own), = chip_sum("chip_sum_w_in_%d" % (len(GW_IN_SPLIT) - 1), [parts[-1]], [ra_last], c_idx, chip_idx)
    owns.append(own)
    small_sums, ((rb_last,),) = small_sync(g, sq, sides=(rs_to_chips_combined(pb),))
    small, loss = small_adam(small_sums, w, m, v, (4 * ax + 2 * ay + c_idx).astype(jnp.int32).reshape(1))
    from_chips.append(rb_last)

    adam_o, adam_pw, adam_out = block_adam("adam_w_o_pw_out", [
        (own_w_o, rb_w_o, w["w_o_attn"], m["w_o_attn"], v["w_o_attn"], True),
        (own_w_pw, rb_w_pw, w["w_pw_conv"], m["w_pw_conv"], v["w_pw_conv"], True),
        (own_out, rb_out, w["w_out"], m["w_out"], v["w_out"], False)])
    res = {
        "w_in": shard_adam("adam_w_in", owns, from_chips, w["w_in"], m["w_in"], v["w_in"]),
        "w_ffn_in": shard_adam("adam_w_ffn_in", [own_ffn_in], [rb_ffn_in], w["w_ffn_in"], m["w_ffn_in"], v["w_ffn_in"]),
        "w_o_attn": adam_o, "w_pw_conv": adam_pw, "w_out": adam_out,
        "w_ffn_out": shard_adam("adam_w_ffn_out", [own_ffn_out], [rb_ffn_out],
                                w["w_ffn_out"], m["w_ffn_out"], v["w_ffn_out"]),
    }
    res = {k: tuple(a.T if k in TRANSPOSED else a for a in r) for k, r in res.items()}
    res.update(small)

    def shaped(name, a):
        return a.reshape((1,) + a.shape) if name in MATS or name in ("b_gate", "conv_w") else a

    outs = [loss, grad_x.reshape(1, S, D)]
    for i in range(4):
        outs += [shaped(k, res[k][i]) for k in WEIGHTS]
    return tuple(outs)
```

```python
import functools
from typing import Callable, NamedTuple, Optional

import numpy as np
import jax
import jax.numpy as jnp
from jax import lax
from jax.experimental import pallas as pl
from jax.experimental.pallas import tpu as pltpu

F32 = jnp.float32
BF16 = jnp.bfloat16

S = 2048
D = 1024
HD = 64
QKV = 1536
CC = 512
KW = 31
FF = 2816
INW = 7680
OFF_Q, OFF_K, OFF_V, OFF_CA, OFF_CB, OFF_GA, OFF_GB = 0, 1536, 3072, 4608, 5120, 5632, 6656
DILATIONS = (1, 4, 16)
HALF_SPAN = 64
EPS = 1e-6
NEG_INF = -1e30
ROPE_THETA = 500000.0
ROT_DIM = 16

ADAM_LR = 0.001
ADAM_B1 = 0.9
ADAM_B2 = 0.999
ADAM_EPS = 1e-08
ADAM_WD = 0.01
ADAM_STEP = 10

NDEV = 8
LANES = 128
TM = 256
IN_PROJ_TM = 512
TQ = 128
VMEM_LIMIT = 56 * 1024 * 1024
MESH = pl.DeviceIdType.MESH


def _cp(**kw):
    return pltpu.CompilerParams(vmem_limit_bytes=VMEM_LIMIT, **kw)


def _row(width, col=0, tm=TM):
    return pl.BlockSpec((tm, width), lambda i: (i, col))


PLANE = 512


def _planes(width, tm=TM):
    return pl.BlockSpec((width // PLANE, tm, PLANE), lambda i: (0, i, 0))


def _res(shape):
    nd = len(shape)
    return pl.BlockSpec(shape, lambda *_: (0,) * nd, pipeline_mode=pl.Buffered(1))


def _dot(a, b):
    return jnp.dot(a, b, preferred_element_type=F32)


def _dot_nt(a, b):
    return lax.dot_general(a, b, (((1,), (1,)), ((), ())), preferred_element_type=F32)


def _dot_tn(a, b):
    return lax.dot_general(a, b, (((0,), (0,)), ((), ())), preferred_element_type=F32)


def _sigmoid(x):
    return jax.nn.sigmoid(x)


def _dsilu(x, sg):
    return sg * (1.0 + x * (1.0 - sg))


ANY = pl.BlockSpec(memory_space=pl.ANY)
VMEM = pl.BlockSpec(memory_space=pltpu.VMEM)


class Side(NamedTuple):
    args: tuple
    in_specs: tuple
    out_shape: tuple
    scratch: tuple
    start: Callable
    finish: Callable
    mid: Optional[Callable] = None
    peers: str = ""


BARRIER_IDS = {"s": 0, "dxy": 1, "dsxy": 2, "sxy": 3, "xy": 4}


def _peer_barrier(peers):
    x, y, c = lax.axis_index("x"), lax.axis_index("y"), lax.axis_index("c")
    where = {"s": (x, y, 1 - c), "x": (1 - x, y, c), "y": (x, 1 - y, c), "d": (1 - x, 1 - y, c)}
    barrier = pltpu.get_barrier_semaphore()
    for p in peers:
        pl.semaphore_signal(barrier, inc=1, device_id=where[p], device_id_type=MESH)
    pl.semaphore_wait(barrier, len(peers))


def _call(body, sides=(), *, name, grid, in_specs, out_specs, out_shape, scratch_shapes=(), args, own_comm=False,
          tail=None):
    assert tail is None or int(np.prod(grid)) == 1
    ni, no, ns = len(in_specs), len(out_specs), len(scratch_shapes)
    cnt = [(len(s.args), len(s.out_shape), len(s.scratch)) for s in sides]
    peers = "".join(sorted(set("".join(s.peers for s in sides))))
    if own_comm or not sides or any(not s.peers for s in sides):
        peers = ""

    def take(refs, pos, n):
        return refs[pos:pos + n], pos + n

    def full(*refs):
        m_in, pos = take(refs, 0, ni)
        s_in = []
        for a, _, _ in cnt:
            r, pos = take(refs, pos, a)
            s_in.append(r)
        m_out, pos = take(refs, pos, no)
        s_out = []
        for _, o, _ in cnt:
            r, pos = take(refs, pos, o)
            s_out.append(r)
        m_scr, pos = take(refs, pos, ns)
        s_scr = []
        for _, _, c in cnt:
            r, pos = take(refs, pos, c)
            s_scr.append(r)
        if sides:
            first = functools.reduce(jnp.logical_and, [pl.program_id(d) == 0 for d in range(len(grid))])
            last = functools.reduce(jnp.logical_and, [pl.program_id(d) == g - 1 for d, g in enumerate(grid)])

            @pl.when(first)
            def _():
                if peers:
                    _peer_barrier(peers)
                for s, a, o, c in zip(sides, s_in, s_out, s_scr):
                    s.start(a, o, c)

            steps = int(np.prod(grid))
            mid_step = (2 * steps) // 3
            if steps > 1 and any(s.mid is not None for s in sides):
                step = functools.reduce(lambda acc, d: acc * grid[d] + pl.program_id(d), range(len(grid)), 0)

                @pl.when(step == mid_step)
                def _():
                    for s, a, o, c in zip(sides, s_in, s_out, s_scr):
                        if s.mid is not None:
                            s.mid(a, o, c)

        body(*m_in, *m_out, *m_scr)
        if sides:
            @pl.when(last)
            def _():
                for s, a, o, c in zip(sides, s_in, s_out, s_scr):
                    if s.mid is not None and steps == 1:
                        s.mid(a, o, c)
                if tail is not None:
                    tail(*m_in, *m_out, *m_scr)
                for s, a, o, c in zip(sides, s_in, s_out, s_scr):
                    s.finish(a, o, c)
        elif tail is not None:
            tail(*m_in, *m_out, *m_scr)

    res = pl.pallas_call(
        full, name=name, grid=grid,
        in_specs=list(in_specs) + [sp for s in sides for sp in s.in_specs],
        out_specs=list(out_specs) + [ANY for s in sides for _ in s.out_shape],
        out_shape=list(out_shape) + [o for s in sides for o in s.out_shape],
        scratch_shapes=list(scratch_shapes) + [c for s in sides for c in s.scratch],
        compiler_params=_cp(dimension_semantics=("arbitrary",) * len(grid),
                            **({"collective_id": BARRIER_IDS[peers]} if peers else {})),
    )(*args, *[a for s in sides for a in s.args])
    res = list(res)
    if not sides:
        return res
    outs, pos = take(res, 0, no)
    side_outs = []
    for _, o, _ in cnt:
        r, pos = take(res, pos, o)
        side_outs.append(r)
    return outs, side_outs


def _inv_freq_lanes():
    inv = np.float32(ROPE_THETA) ** (-np.arange(0, ROT_DIM, 2, dtype=np.float32) / np.float32(ROT_DIM))
    lane = np.arange(LANES) % HD
    out = np.where(lane < ROT_DIM, inv[lane % (ROT_DIM // 2)], 0.0).astype(np.float32)
    return jnp.asarray(out.reshape(1, LANES))


def _rope_tables(pos, inv_freq):
    ang = pos.astype(F32) * inv_freq
    lane = lax.broadcasted_iota(jnp.int32, ang.shape, 1) % HD
    cs = jnp.cos(ang)
    sn = jnp.sin(ang)
    return (jnp.where(lane < ROT_DIM, cs, 1.0), jnp.where(lane < ROT_DIM // 2, -sn, 0.0),
            jnp.where(lane < ROT_DIM // 2, 0.0, jnp.where(lane < ROT_DIM, sn, 0.0)))


def _rope(v, c, s1, s2):
    return v * c + pltpu.roll(v, LANES - 8, axis=1) * s1 + pltpu.roll(v, 8, axis=1) * s2


def _rope_t(d, c, s1, s2):
    return d * c - pltpu.roll(d, LANES - 8, axis=1) * s1 - pltpu.roll(d, 8, axis=1) * s2


def _head_mat():
    r = lax.broadcasted_iota(jnp.int32, (LANES, LANES), 0) // HD
    c = lax.broadcasted_iota(jnp.int32, (LANES, LANES), 1) // HD
    return jnp.where(r == c, 1.0 / HD, 0.0).astype(BF16)


def _head_mean(t, e):
    hi = t.astype(BF16)
    rest = (t - hi.astype(F32)).astype(BF16)
    return _dot(hi, e) + _dot(rest, e)


def in_proj_gather(x, norm_w, shard_t, chip_order, pos_col):
    R = INW // NDEV
    tm = IN_PROJ_TM
    half, nt = R // 2, S // tm

    def body(ord_ref, x_ref, nw_ref, sh_ref, pos_ref, f_ref, h_ref, p_ref, wfull_ref, c_ref, s1_ref, s2_ref,
             wt, hs, send, recv, loc):
        kk, i = pl.program_id(0), pl.program_id(1)
        x, y, c, _ = _place()
        me, flip = 4 * x + 2 * y + c, 1 - 2 * c
        here, sib, xn, yn = (x, y, c), (x, y, 1 - c), (1 - x, y, c), (x, 1 - y, c)
        b_xn, b_yn, b_dg = 4 * (1 - x) + 2 * y + c, 4 * x + 2 * (1 - y) + c, 4 * (1 - x) + 2 * (1 - y) + c

        def cp(k, block, to, rows=None):
            dst = wt.at[block] if rows is None else wt.at[block, pl.ds(rows * half, half), :]
            return _remote(dst, dst, send, recv, k, to)

        def sends():
            return [cp(0, me, sib), cp(1, me, xn), cp(2, me, yn), cp(3, b_xn, sib), cp(4, b_yn, sib),
                    cp(5, b_xn, yn, rows=0), cp(6, b_yn, xn, rows=1), cp(7, b_dg, sib, rows=0), cp(8, b_dg, sib, rows=1)]

        def keep(j, blk0):
            pair = pl.ds(pl.multiple_of(blk0, 2), 2)
            return pltpu.make_async_copy(wt.at[pair], wfull_ref.at[pair], loc.at[j])

        @pl.when((kk == 0) & (i == 0))
        def _():
            _peer_barrier("sxy")
            _cast_rows(wt.at[me], sh_ref)
            for s_ in sends()[0:3]:
                s_.start()

            def tables(j, _):
                chunk = pl.ds(pl.multiple_of(j * TM, TM), TM)
                c_ref[chunk, :], s1_ref[chunk, :], s2_ref[chunk, :] = _rope_tables(pos_ref[chunk, :], f_ref[...])
                return 0

            lax.fori_loop(0, S // TM, tables, 0)
            cp(0, me + flip, here).wait_recv()
            keep(0, me - c).start()

        @pl.when((kk == 1) & (i == 0))
        def _():
            cp(1, b_xn, here).wait_recv()
            sends()[5].start()
            sends()[3].start()
            cp(2, b_yn, here).wait_recv()
            sends()[6].start()
            sends()[4].start()
            cp(3, b_xn + flip, here).wait_recv()
            keep(1, b_xn - c).start()

        @pl.when((kk == 2) & (i == 0))
        def _():
            cp(4, b_yn + flip, here).wait_recv()
            keep(2, b_yn - c).start()

        @pl.when((kk == 3) & (i == 0))
        def _():
            cp(5, b_dg, here, rows=0).wait_recv()
            sends()[7].start()
            cp(6, b_dg, here, rows=1).wait_recv()
            sends()[8].start()
            cp(7, b_dg + flip, here, rows=0).wait_recv()
            cp(8, b_dg + flip, here, rows=1).wait_recv()
            keep(3, b_dg - c).start()

        rows = pl.ds(pl.multiple_of(i * tm, tm), tm)

        @pl.when(kk == 0)
        def _():
            xv = x_ref[...]
            r = lax.rsqrt(jnp.mean(xv * xv, axis=-1, keepdims=True) + EPS)
            hb = (xv * r * nw_ref[...]).astype(BF16)
            h_ref[...] = hb
            hs[rows, :] = hb

        h = hs[rows, :]
        chip = ord_ref[kk]
        for cc in range(2):
            p_ref[:, cc * R:(cc + 1) * R] = _dot_nt(h, wt[2 * chip + cc])

        @pl.when((kk == 3) & (i == nt - 1))
        def _():
            for s_ in sends():
                s_.wait_send()
            for j, blk in enumerate((me, b_xn, b_yn, b_dg)):
                keep(j, blk - c).wait()

    def first_pass(kk, i):
        return jnp.where(kk == 0, i, nt - 1)

    grid_spec = pltpu.PrefetchScalarGridSpec(
        num_scalar_prefetch=1, grid=(4, nt),
        in_specs=[pl.BlockSpec((tm, D), lambda kk, i, o: (first_pass(kk, i), 0)),
                  pl.BlockSpec((1, D), lambda kk, i, o: (0, 0)), VMEM, VMEM,
                  pl.BlockSpec((1, LANES), lambda kk, i, o: (0, 0))],
        out_specs=[pl.BlockSpec((tm, D), lambda kk, i, o: (first_pass(kk, i), 0)),
                   pl.BlockSpec((tm, 2 * R), lambda kk, i, o: (i, o[kk])), ANY]
        + [pl.BlockSpec((S, LANES), lambda kk, i, o: (0, 0))] * 3,
        scratch_shapes=[pltpu.VMEM((NDEV, R, D), BF16), pltpu.VMEM((S, D), BF16), _sems(9), _sems(9), _sems(4)])
    res = pl.pallas_call(
        body, name="in_proj_gather", grid_spec=grid_spec,
        out_shape=[jax.ShapeDtypeStruct((S, D), BF16), jax.ShapeDtypeStruct((S, INW), F32),
                   jax.ShapeDtypeStruct((NDEV, R, D), BF16)] + [jax.ShapeDtypeStruct((S, LANES), F32)] * 3,
        compiler_params=_cp(dimension_semantics=("arbitrary", "arbitrary"), collective_id=BARRIER_IDS["sxy"]),
    )(chip_order, x, norm_w, shard_t, pos_col, _inv_freq_lanes())
    return res[0], res[1], res[2], tuple(res[3:])


def _qk_specs():
    nb = QKV // LANES
    return [pl.BlockSpec((S, LANES), functools.partial(lambda hp, g, o: (0, o + g * 4 + hp), o=o))
            for o in (OFF_Q // LANES, OFF_K // LANES, OFF_V // LANES)]


def _tab_specs():
    return [pl.BlockSpec((S, LANES), lambda hp, g: (0, 0), pipeline_mode=pl.Buffered(1))] * 3


def _vec_spec():
    return pl.BlockSpec((1, LANES), lambda hp, g: (0, 0))


def _sub_rows(r, d, start, n):
    if d == 1:
        return pl.ds(start, n)
    return pl.ds(r + d * start, n, stride=d)


def _band_window(i, L):
    W = min(TQ + 2 * HALF_SPAN, L)
    q0 = pl.multiple_of(i * TQ, TQ)
    k0 = pl.multiple_of(jnp.clip(q0 - HALF_SPAN, 0, L - W), HALF_SPAN)
    qpos = q0 + (lax.broadcasted_iota(jnp.int32, (2 * TQ, W), 0) & (TQ - 1))
    kpos = k0 + lax.broadcasted_iota(jnp.int32, (2 * TQ, W), 1)
    valid = jnp.abs(qpos - kpos) <= HALF_SPAN
    return W, q0, k0, valid


def _stack_heads(t, lo):
    z = jnp.zeros_like(t)
    return jnp.concatenate([jnp.where(lo, t, z), jnp.where(lo, z, t)], axis=0)


def _unstack_heads(t2, lo):
    return jnp.where(lo, t2[0:TQ], t2[TQ:2 * TQ])


CHAINS = 8


def _interleave(d):
    ru = min(d, CHAINS)
    return ru, min(CHAINS // ru, S // d // TQ)


def _for_blocks(n, fn):
    if n == 1:
        fn(0)
    else:
        def it(j, _):
            fn(j)
            return 0
        lax.fori_loop(0, n, it, 0)


def attn_fwd(proj, tabs, qw2, kw2, sides=()):
    CH = 256

    def body(q_ref, k_ref, v_ref, c_ref, s1_ref, s2_ref, qw_ref, kw_ref, at_ref, ls_ref,
             qs, ks, vs, osub, lsub, onat, lnat, qn, kn):
        g = pl.program_id(1)
        lo = lax.broadcasted_iota(jnp.int32, (1, LANES), 1) < HD
        e = _head_mat()

        def prep(i, _):
            rows = pl.ds(pl.multiple_of(i * CH, CH), CH)
            c, s1, s2 = c_ref[rows, :], s1_ref[rows, :], s2_ref[rows, :]
            for t_ref, w_ref, out, scale in ((q_ref, qw_ref, qn, HD ** -0.5), (k_ref, kw_ref, kn, 1.0)):
                t = t_ref[rows, :]
                r = lax.rsqrt(_head_mean(t * t, e) + EPS)
                out[rows, :] = _rope(t * r * w_ref[...], c, s1, s2) * scale
            return 0

        lax.fori_loop(0, S // CH, prep, 0, unroll=4)

        def group(gi, d):
            L = S // d

            ru, nb = _interleave(d)

            def stage(r, off):
                for c0 in range(0, L, CH):
                    n = min(CH, L)
                    rows = _sub_rows(r, d, c0, n)
                    dst = pl.ds(off + c0, n)
                    qs[dst, :] = qn[rows, :].astype(BF16)
                    ks[dst, :] = kn[rows, :].astype(BF16)
                    vs[dst, :] = v_ref[rows, :].astype(BF16)

            def one(off, i):
                W, q0, k0, valid = _band_window(i, L)
                q2 = _stack_heads(qs[pl.ds(off + q0, TQ), :], lo)
                sc = jnp.where(valid, _dot_nt(q2, ks[pl.ds(off + k0, W), :]), NEG_INF)
                m = jnp.max(sc, axis=-1, keepdims=True)
                p = jnp.exp(sc - m)
                den = jnp.sum(p, axis=-1, keepdims=True)
                o2 = _dot(p.astype(BF16), vs[pl.ds(off + k0, W), :]) / den
                l2 = jnp.broadcast_to(m + jnp.log(den), (2 * TQ, LANES))
                osub[pl.ds(off + q0, TQ), :] = _unstack_heads(o2, lo)
                lsub[pl.ds(off + q0, TQ), :] = _unstack_heads(l2, lo)

            def unstage(r, off):
                for c0 in range(0, L, CH):
                    n = min(CH, L)
                    rows = _sub_rows(r, d, c0, n)
                    onat[gi, rows, :] = osub[pl.ds(off + c0, n), :]
                    lnat[gi, rows, :] = lsub[pl.ds(off + c0, n), :]

            def step(t, _):
                for u in range(ru):
                    stage(t * ru + u, u * L)
                _for_blocks(L // TQ // nb, lambda j: [one(u * L, j * nb + b) for u in range(ru) for b in range(nb)])
                for u in range(ru):
                    unstage(t * ru + u, u * L)
                return 0

            lax.fori_loop(0, d // ru, step, 0)

        for gi, d in enumerate(DILATIONS):
            pl.when(g == gi)(functools.partial(group, gi, d))

        @pl.when(g == len(DILATIONS) - 1)
        def _():
            def mix(i, _):
                rows = pl.ds(pl.multiple_of(i * CH, CH), CH)
                l0, l1, l2 = lnat[0, rows, :], lnat[1, rows, :], lnat[2, rows, :]
                m = jnp.maximum(jnp.maximum(l0, l1), l2)
                e0, e1, e2 = jnp.exp(l0 - m), jnp.exp(l1 - m), jnp.exp(l2 - m)
                den = e0 + e1 + e2
                a = (e0 * onat[0, rows, :] + e1 * onat[1, rows, :] + e2 * onat[2, rows, :]) / den
                at_ref[rows, :] = a.astype(BF16)
                ls_ref[rows, :] = m + jnp.log(den)
                return 0

            lax.fori_loop(0, S // CH, mix, 0)

    out_spec = pl.BlockSpec((S, LANES), lambda hp, g: (0, hp))
    return _call(
        body, sides, name="attn_fwd", grid=(4, 3),
        in_specs=_qk_specs() + _tab_specs() + [_vec_spec(), _vec_spec()],
        out_specs=[out_spec, out_spec],
        out_shape=[jax.ShapeDtypeStruct((S, CC), BF16), jax.ShapeDtypeStruct((S, CC), F32)],
        scratch_shapes=[pltpu.VMEM((S, LANES), BF16)] * 3 + [pltpu.VMEM((S, LANES), F32)] * 2
        + [pltpu.VMEM((3, S, LANES), F32)] * 2 + [pltpu.VMEM((S, LANES), F32)] * 2,
        args=(proj, proj, proj, *tabs, qw2, kw2))


def attn_bwd(proj, tabs, qw2, kw2, d_attn, attn, lse, sides=()):
    CH = 256

    def body(q_ref, k_ref, v_ref, c_ref, s1_ref, s2_ref, qw_ref, kw_ref, do_ref, at_ref, ls_ref,
             dq_ref, dk_ref, dv_ref, gqw_ref, gkw_ref,
             qs, ks, vs, dos, dsub, lsub, dqs, dks, dvs, dnat, qx, kx, dvn, tnq, tnk, rrq, rrk):
        hp, g = pl.program_id(0), pl.program_id(1)
        lo = lax.broadcasted_iota(jnp.int32, (1, LANES), 1) < HD
        e = _head_mat()
        both = ((q_ref, qw_ref, qx, tnq, rrq, HD ** -0.5), (k_ref, kw_ref, kx, tnk, rrk, 1.0))

        @pl.when((hp == 0) & (g == 0))
        def _():
            gqw_ref[...] = jnp.zeros_like(gqw_ref)
            gkw_ref[...] = jnp.zeros_like(gkw_ref)

        def prep(i, _):
            rows = pl.ds(pl.multiple_of(i * CH, CH), CH)
            dnat[rows, :] = _head_mean(do_ref[rows, :] * at_ref[rows, :].astype(F32), e) * float(HD)
            c, s1, s2 = c_ref[rows, :], s1_ref[rows, :], s2_ref[rows, :]
            for t_ref, w_ref, x, tn_s, rr_s, scale in both:
                t = t_ref[rows, :]
                rr = lax.rsqrt(_head_mean(t * t, e) + EPS)
                tn = t * rr
                rr_s[rows, :] = rr
                tn_s[rows, :] = tn
                x[rows, :] = _rope(tn * w_ref[...], c, s1, s2) * scale
            return 0

        lax.fori_loop(0, S // CH, prep, 0, unroll=4)

        def group(d):
            L = S // d

            ru, nb = _interleave(d)

            def stage(r, off):
                for c0 in range(0, L, CH):
                    n = min(CH, L)
                    rows = _sub_rows(r, d, c0, n)
                    dst = pl.ds(off + c0, n)
                    qs[dst, :] = qx[rows, :].astype(BF16)
                    ks[dst, :] = kx[rows, :].astype(BF16)
                    vs[dst, :] = v_ref[rows, :].astype(BF16)
                    dos[dst, :] = do_ref[rows, :].astype(BF16)
                    dsub[dst, :] = dnat[rows, :]
                    lsub[dst, :] = ls_ref[rows, :]
                    dks[dst, :] = jnp.zeros((n, LANES), F32)
                    dvs[dst, :] = jnp.zeros((n, LANES), F32)

            def one(off, i):
                W, q0, k0, valid = _band_window(i, L)
                qrows, krows = pl.ds(off + q0, TQ), pl.ds(off + k0, W)
                q2 = _stack_heads(qs[qrows, :], lo)
                do2 = _stack_heads(dos[qrows, :], lo)
                kk, vv = ks[krows, :], vs[krows, :]
                lse_b, dd_b = lsub[qrows, :], dsub[qrows, :]
                lse2 = jnp.concatenate([lse_b[:, 0:1], lse_b[:, HD:HD + 1]], axis=0)
                dd2 = jnp.concatenate([dd_b[:, 0:1], dd_b[:, HD:HD + 1]], axis=0)
                sc = jnp.where(valid, _dot_nt(q2, kk), NEG_INF)
                p = jnp.exp(sc - lse2)
                ds = (p * (_dot_nt(do2, vv) - dd2)).astype(BF16)
                dqs[qrows, :] = _unstack_heads(_dot(ds, kk), lo)
                dks[krows, :] = dks[krows, :] + _dot_tn(ds, q2)
                dvs[krows, :] = dvs[krows, :] + _dot_tn(p.astype(BF16), do2)

            def unstage(r, off):
                for c0 in range(0, L, CH):
                    n = min(CH, L)
                    rows = _sub_rows(r, d, c0, n)
                    src = pl.ds(off + c0, n)
                    qx[rows, :] = dqs[src, :]
                    kx[rows, :] = dks[src, :]
                    dvn[rows, :] = dvs[src, :]

            def step(t, _):
                for u in range(ru):
                    stage(t * ru + u, u * L)
                _for_blocks(L // TQ // nb, lambda j: [one(u * L, j * nb + b) for u in range(ru) for b in range(nb)])
                for u in range(ru):
                    unstage(t * ru + u, u * L)
                return 0

            lax.fori_loop(0, d // ru, step, 0)

        for gi, d in enumerate(DILATIONS):
            pl.when(g == gi)(functools.partial(group, d))

        def emit(i, _):
            rows = pl.ds(pl.multiple_of(i * CH, CH), CH)
            c, s1, s2 = c_ref[rows, :], s1_ref[rows, :], s2_ref[rows, :]
            for (_, w_ref, x, tn_s, rr_s, scale), out, gw_ref in zip(both, (dq_ref, dk_ref), (gqw_ref, gkw_ref)):
                tn = tn_s[rows, :]
                dy = _rope_t(x[rows, :] * scale, c, s1, s2)
                gw_ref[0:1, :] = gw_ref[0:1, :] + jnp.sum(dy * tn, axis=0, keepdims=True)
                dtn = dy * w_ref[...]
                out[rows, :] = (rr_s[rows, :] * (dtn - tn * _head_mean(dtn * tn, e))).astype(BF16)
            dv_ref[rows, :] = dvn[rows, :].astype(BF16)
            return 0

        lax.fori_loop(0, S // CH, emit, 0, unroll=4)

    nat_spec = pl.BlockSpec((S, LANES), lambda hp, g: (0, hp))
    out_spec = pl.BlockSpec((None, S, LANES), lambda hp, g: (g, 0, hp))
    acc_spec = pl.BlockSpec((8, LANES), lambda hp, g: (0, 0))
    return _call(
        body, sides, name="attn_bwd", grid=(4, 3),
        in_specs=_qk_specs() + _tab_specs() + [_vec_spec(), _vec_spec(), nat_spec, nat_spec, nat_spec],
        out_specs=[out_spec] * 3 + [acc_spec] * 2,
        out_shape=[jax.ShapeDtypeStruct((QKV // PLANE, S, PLANE), BF16)] * 3 + [jax.ShapeDtypeStruct((8, LANES), F32)] * 2,
        scratch_shapes=[pltpu.VMEM((S, LANES), BF16)] * 4 + [pltpu.VMEM((S, LANES), F32)] * 13,
        args=(proj, proj, proj, *tabs, qw2, kw2, d_attn, attn, lse))


PADR = 16
CT = 128


def _conv_specs():
    return [pl.BlockSpec((S, CC), lambda i: (0, OFF_CA // CC)), pl.BlockSpec((S, CC), lambda i: (0, OFF_CB // CC))]


NCB = CC // LANES


def _pad_zero(pad):
    for cb in range(NCB):
        pad[cb, 0:PADR, :] = jnp.zeros((PADR, LANES), F32)
        pad[cb, PADR + S:PADR + S + PADR, :] = jnp.zeros((PADR, LANES), F32)


def _pad_store(pad, row0, n, val):
    for cb in range(NCB):
        pad[cb, pl.ds(pl.multiple_of(row0 + PADR, 8), n), :] = val[:, cb * LANES:(cb + 1) * LANES]


def _taps(pad_ref, cb, s0, weights):
    acc = jnp.zeros((CT, LANES), F32)
    for k in range(KW):
        acc = acc + weights[k] * pad_ref[cb, pl.ds(s0 + k + 1, CT), :]
    return acc


def conv_fwd(proj, conv_w, conv_b, ln_w, ln_b, sides=()):
    def body(a_ref, b_ref, w_ref, cb_ref, lw_ref, lb_ref, c_ref, u3_ref, upad):
        _pad_zero(upad)

        def glu(i, _):
            rows = pl.ds(pl.multiple_of(i * TM, TM), TM)
            _pad_store(upad, i * TM, TM, a_ref[rows, :] * _sigmoid(b_ref[rows, :]))
            return 0

        lax.fori_loop(0, S // TM, glu, 0)

        def chunk(i, _):
            s0 = pl.multiple_of(i * CT, CT)
            for cb in range(CC // LANES):
                cols = slice(cb * LANES, (cb + 1) * LANES)
                w = [w_ref[k:k + 1, cols] for k in range(KW)]
                c_ref[pl.ds(s0, CT), cols] = _taps(upad, cb, s0, w) + cb_ref[:, cols]
            cv = c_ref[pl.ds(s0, CT), :]
            mu = jnp.mean(cv, axis=-1, keepdims=True)
            xc = cv - mu
            rstd = lax.rsqrt(jnp.mean(xc * xc, axis=-1, keepdims=True) + EPS)
            yl = xc * rstd * lw_ref[...] + lb_ref[...]
            u3_ref[pl.ds(s0, CT), :] = (yl * _sigmoid(yl)).astype(BF16)
            return 0

        lax.fori_loop(0, S // CT, chunk, 0)

    vec = pl.BlockSpec((1, CC), lambda i: (0, 0))
    full = pl.BlockSpec((S, CC), lambda i: (0, 0))
    return _call(
        body, sides, name="conv_fwd", grid=(1,),
        in_specs=_conv_specs() + [pl.BlockSpec((KW, CC), lambda i: (0, 0)), vec, vec, vec],
        out_specs=[full, full],
        out_shape=[jax.ShapeDtypeStruct((S, CC), F32), jax.ShapeDtypeStruct((S, CC), BF16)],
        scratch_shapes=[pltpu.VMEM((NCB, S + 2 * PADR, LANES), F32)],
        args=(proj, proj, conv_w, conv_b, ln_w, ln_b))


def conv_bwd(proj, cpre, d_u3, conv_w, conv_w_rev, ln_w, ln_b, sides=()):
    def body(a_ref, b_ref, c_ref, du3_ref, w_ref, wr_ref, lw_ref, lb_ref,
             dc_ref, gw_ref, gcb_ref, glw_ref, glb_ref, upad, dpad):
        _pad_zero(upad)
        _pad_zero(dpad)
        gw_ref[...] = jnp.zeros_like(gw_ref)

        def ln_bwd(i, carry):
            gcb, glw, glb = carry
            rows = pl.ds(pl.multiple_of(i * TM, TM), TM)
            _pad_store(upad, i * TM, TM, a_ref[rows, :] * _sigmoid(b_ref[rows, :]))
            cv = c_ref[rows, :]
            mu = jnp.mean(cv, axis=-1, keepdims=True)
            xc = cv - mu
            rstd = lax.rsqrt(jnp.mean(xc * xc, axis=-1, keepdims=True) + EPS)
            xh = xc * rstd
            yl = xh * lw_ref[...] + lb_ref[...]
            dyl = du3_ref[rows, :] * _dsilu(yl, _sigmoid(yl))
            dxh = dyl * lw_ref[...]
            dcv = rstd * (dxh - jnp.mean(dxh, axis=-1, keepdims=True)
                          - xh * jnp.mean(dxh * xh, axis=-1, keepdims=True))
            _pad_store(dpad, i * TM, TM, dcv)
            return (gcb + jnp.sum(dcv, axis=0, keepdims=True),
                    glw + jnp.sum(dyl * xh, axis=0, keepdims=True),
                    glb + jnp.sum(dyl, axis=0, keepdims=True))

        z = jnp.zeros((1, CC), F32)
        gcb, glw, glb = lax.fori_loop(0, S // TM, ln_bwd, (z, z, z))
        gcb_ref[...] = gcb
        glw_ref[...] = glw
        glb_ref[...] = glb

        def chunk(i, _):
            s0 = pl.multiple_of(i * CT, CT)
            for cb in range(CC // LANES):
                cols = slice(cb * LANES, (cb + 1) * LANES)
                wr = [wr_ref[k:k + 1, cols] for k in range(KW)]
                du = _taps(dpad, cb, s0, wr)
                dcv = dpad[cb, pl.ds(s0 + PADR, CT), :]
                for k in range(KW):
                    gw_ref[k:k + 1, cols] = gw_ref[k:k + 1, cols] + jnp.sum(
                        upad[cb, pl.ds(s0 + k + 1, CT), :] * dcv, axis=0, keepdims=True)
                av = a_ref[pl.ds(s0, CT), cols]
                sb = _sigmoid(b_ref[pl.ds(s0, CT), cols])
                dc_ref[0, pl.ds(s0, CT), cols] = (du * sb).astype(BF16)
                dc_ref[1, pl.ds(s0, CT), cols] = (du * av * sb * (1.0 - sb)).astype(BF16)
            return 0

        lax.fori_loop(0, S // CT, chunk, 0)

    vec = pl.BlockSpec((1, CC), lambda i: (0, 0))
    full = pl.BlockSpec((S, CC), lambda i: (0, 0))
    wsp = pl.BlockSpec((KW, CC), lambda i: (0, 0))
    return _call(
        body, sides, name="conv_bwd", grid=(1,),
        in_specs=_conv_specs() + [full, full, wsp, wsp, vec, vec],
        out_specs=[pl.BlockSpec((2, S, CC), lambda i: (0, 0, 0)), wsp, vec, vec, vec],
        out_shape=[jax.ShapeDtypeStruct((2, S, CC), BF16), jax.ShapeDtypeStruct((KW, CC), F32)]
        + [jax.ShapeDtypeStruct((1, CC), F32)] * 3,
        scratch_shapes=[pltpu.VMEM((NCB, S + 2 * PADR, LANES), F32)] * 2,
        args=(proj, proj, cpre, d_u3, conv_w, conv_w_rev, ln_w, ln_b))


def _gate_specs():
    return [_row(CC, col=OFF_GA // CC + j) for j in range(4)]


def _gates(g_refs, bg_ref):
    ga = _sigmoid(jnp.concatenate([g_refs[0][...], g_refs[1][...]], axis=1) + bg_ref[0:1, :])
    gb = _sigmoid(jnp.concatenate([g_refs[2][...], g_refs[3][...]], axis=1) + bg_ref[1:2, :])
    return ga, gb


def mix_out(x, proj, b_gate, attn, u3, w_o, w_pw, w_out):
    def body(x_ref, g0, g1, g2, g3, bg_ref, at_ref, u3_ref, wo_ref, wp_ref, wout_ref,
             x1_ref, z_ref, ya_ref, yb_ref):
        ga, gb = _gates((g0, g1, g2, g3), bg_ref)
        ya = _dot_nt(at_ref[...], wo_ref[...])
        yb = _dot_nt(u3_ref[...], wp_ref[...])
        z = (ga * ya + gb * yb).astype(BF16)
        ya_ref[...] = ya.astype(BF16)
        yb_ref[...] = yb.astype(BF16)
        z_ref[...] = z
        x1_ref[...] = x_ref[...] + _dot(z, wout_ref[...])

    return pl.pallas_call(
        body, name="mix_out", grid=(S // TM,),
        in_specs=[_row(D)] + _gate_specs() + [_res((2, D)), _row(CC), _row(CC),
                                              _res((D, CC)), _res((D, CC)), _res((D, D))],
        out_specs=[_row(D)] * 4,
        out_shape=[jax.ShapeDtypeStruct((S, D), F32)] + [jax.ShapeDtypeStruct((S, D), BF16)] * 3,
        compiler_params=_cp(dimension_semantics=("arbitrary",)),
    )(x, proj, proj, proj, proj, b_gate, attn, u3, w_o, w_pw, w_out)


def out_bwd(d_x1b, proj, b_gate, ya, yb, w_o, w_pw, w_out, sides=()):
    def body(dx_ref, g0, g1, g2, g3, bg_ref, ya_ref, yb_ref, wo_ref, wp_ref, wout_ref,
             dya_ref, dyb_ref, dgl_ref, dat_ref, du3_ref, gbg_ref):
        @pl.when(pl.program_id(0) == 0)
        def _():
            gbg_ref[...] = jnp.zeros_like(gbg_ref)

        ga, gb = _gates((g0, g1, g2, g3), bg_ref)
        dz = _dot_nt(dx_ref[...], wout_ref[...])
        dya = (dz * ga).astype(BF16)
        dyb = (dz * gb).astype(BF16)
        dgla = dz * ya_ref[...].astype(F32) * ga * (1.0 - ga)
        dglb = dz * yb_ref[...].astype(F32) * gb * (1.0 - gb)
        dya_ref[...] = dya
        dyb_ref[...] = dyb
        for j in range(2):
            dgl_ref[j] = dgla[:, j * PLANE:(j + 1) * PLANE].astype(BF16)
            dgl_ref[2 + j] = dglb[:, j * PLANE:(j + 1) * PLANE].astype(BF16)
        gbg_ref[0:1, :] = gbg_ref[0:1, :] + jnp.sum(dgla, axis=0, keepdims=True)
        gbg_ref[1:2, :] = gbg_ref[1:2, :] + jnp.sum(dglb, axis=0, keepdims=True)
        dat_ref[...] = _dot(dya, wo_ref[...])
        du3_ref[...] = _dot(dyb, wp_ref[...])

    return _call(
        body, sides, name="out_bwd", grid=(S // TM,),
        in_specs=[_row(D)] + _gate_specs() + [_res((2, D)), _row(D), _row(D),
                                              _res((D, CC)), _res((D, CC)), _res((D, D))],
        out_specs=[_row(D), _row(D), _planes(2 * D), _row(CC), _row(CC), pl.BlockSpec((2, D), lambda i: (0, 0))],
        out_shape=[jax.ShapeDtypeStruct((S, D), BF16)] * 2 + [jax.ShapeDtypeStruct((2 * D // PLANE, S, PLANE), BF16)]
        + [jax.ShapeDtypeStruct((S, CC), F32)] * 2 + [jax.ShapeDtypeStruct((2, D), F32)],
        args=(d_x1b, proj, proj, proj, proj, b_gate, ya, yb, w_o, w_pw, w_out))


def ffn_in(x1, norm_w, w_ffn_in, sides=()):
    half = FF // 2

    def body(x_ref, nw_ref, w_ref, h_ref, gu_ref, f_ref):
        xv = x_ref[...]
        r = lax.rsqrt(jnp.mean(xv * xv, axis=-1, keepdims=True) + EPS)
        h = (xv * r * nw_ref[...]).astype(BF16)
        h_ref[...] = h
        for j in range(2):
            gt = _dot_nt(h, w_ref[j * half:(j + 1) * half, :])
            up = _dot_nt(h, w_ref[FF + j * half:FF + (j + 1) * half, :])
            gu_ref[:, j * half:(j + 1) * half] = gt.astype(BF16)
            gu_ref[:, FF + j * half:FF + (j + 1) * half] = up.astype(BF16)
            f_ref[:, j * half:(j + 1) * half] = (gt * _sigmoid(gt) * up).astype(BF16)

    return _call(
        body, sides, name="ffn_in", grid=(S // TM,),
        in_specs=[_row(D), _res((1, D)), _res((2 * FF, D))],
        out_specs=[_row(D), _row(2 * FF), _row(FF)],
        out_shape=[jax.ShapeDtypeStruct((S, D), BF16), jax.ShapeDtypeStruct((S, 2 * FF), BF16),
                   jax.ShapeDtypeStruct((S, FF), BF16)],
        args=(x1, norm_w, w_ffn_in))


def ffn_out_loss(x1, f, w_ffn_out, target):
    def body(x_ref, f_ref, w_ref, t_ref, dy_ref, dyb_ref, sq_ref):
        @pl.when(pl.program_id(0) == 0)
        def _():
            sq_ref[...] = jnp.zeros_like(sq_ref)

        diff = x_ref[...] + _dot(f_ref[...], w_ref[...]) - t_ref[...]
        dy = diff * (1.0 / D)
        dy_ref[...] = dy
        dyb_ref[...] = dy.astype(BF16)
        sq_ref[...] = sq_ref[...] + jnp.sum((diff * diff).reshape(TM // 8, 8, D), axis=0)

    return pl.pallas_call(
        body, name="ffn_out_loss", grid=(S // TM,),
        in_specs=[_row(D), _row(FF), _res((FF, D)), _row(D)],
        out_specs=[_row(D), _row(D), pl.BlockSpec((8, D), lambda i: (0, 0))],
        out_shape=[jax.ShapeDtypeStruct((S, D), F32), jax.ShapeDtypeStruct((S, D), BF16),
                   jax.ShapeDtypeStruct((8, D), F32)],
        compiler_params=_cp(dimension_semantics=("arbitrary",)),
    )(x1, f, w_ffn_out, target)


def _rms_bwd(xv, nw, dh):
    r = lax.rsqrt(jnp.mean(xv * xv, axis=-1, keepdims=True) + EPS)
    xn = xv * r
    dxn = dh * nw
    dx = r * (dxn - xn * jnp.mean(dxn * xn, axis=-1, keepdims=True))
    return dx, dh * xn


def ffn_bwd(dy, dyb, gu, x1, norm_w, w_ffn_in, w_ffn_out, sides=()):
    def body(dy_ref, dyb_ref, gu_ref, x_ref, nw_ref, wi_ref, wo_ref, dgu_ref, dx_ref, dxb_ref, gn_ref):
        @pl.when(pl.program_id(0) == 0)
        def _():
            gn_ref[...] = jnp.zeros_like(gn_ref)

        df = _dot_nt(dyb_ref[...], wo_ref[...])
        gt = gu_ref[:, 0:FF].astype(F32)
        up = gu_ref[:, FF:2 * FF].astype(F32)
        sg = _sigmoid(gt)
        dgt = (df * up * _dsilu(gt, sg)).astype(BF16)
        dup = (df * gt * sg).astype(BF16)
        dgu_ref[:, 0:FF] = dgt
        dgu_ref[:, FF:2 * FF] = dup
        dh = _dot(dgt, wi_ref[0:FF, :]) + _dot(dup, wi_ref[FF:2 * FF, :])
        dxn, gw = _rms_bwd(x_ref[...], nw_ref[...], dh)
        dx = dy_ref[...] + dxn
        dx_ref[...] = dx
        dxb_ref[...] = dx.astype(BF16)
        gn_ref[...] = gn_ref[...] + jnp.sum(gw, axis=0, keepdims=True)

    return _call(
        body, sides, name="ffn_bwd", grid=(S // TM,),
        in_specs=[_row(D), _row(D), _row(2 * FF), _row(D), _res((1, D)), _res((2 * FF, D)), _res((FF, D))],
        out_specs=[_row(2 * FF), _row(D), _row(D), pl.BlockSpec((1, D), lambda i: (0, 0))],
        out_shape=[jax.ShapeDtypeStruct((S, 2 * FF), BF16), jax.ShapeDtypeStruct((S, D), F32),
                   jax.ShapeDtypeStruct((S, D), BF16), jax.ShapeDtypeStruct((1, D), F32)],
        args=(dy, dyb, gu, x1, norm_w, w_ffn_in, w_ffn_out))


def in_bwd(d_q, d_k, d_v, d_conv, d_gl, w_in, x, d_x1, norm_w, sides=()):
    segs = ((OFF_Q, QKV), (OFF_K, QKV), (OFF_V, QKV), (OFF_CA, 2 * CC), (OFF_GA, 2 * D))

    def body(dq_ref, dk_ref, dv_ref, dc_ref, dg_ref, w_ref, x_ref, dx1_ref, nw_ref, gx_ref, gn_ref):
        @pl.when(pl.program_id(0) == 0)
        def _():
            gn_ref[...] = jnp.zeros_like(gn_ref)

        dh = jnp.zeros((TM, D), F32)
        for ref, (off, width) in zip((dq_ref, dk_ref, dv_ref, dc_ref, dg_ref), segs):
            for j in range(width // PLANE):
                dh = dh + _dot(ref[j], w_ref[off + j * PLANE:off + (j + 1) * PLANE, :])
        dxn, gw = _rms_bwd(x_ref[...], nw_ref[...], dh)
        gx_ref[...] = dx1_ref[...] + dxn
        gn_ref[...] = gn_ref[...] + jnp.sum(gw, axis=0, keepdims=True)

    return _call(
        body, sides, name="in_bwd", grid=(S // TM,),
        in_specs=[_planes(QKV)] * 3 + [_planes(2 * CC), _planes(2 * D), _res((INW, D)), _row(D), _row(D), _res((1, D))],
        out_specs=[_row(D), pl.BlockSpec((1, D), lambda i: (0, 0))],
        out_shape=[jax.ShapeDtypeStruct((S, D), F32), jax.ShapeDtypeStruct((1, D), F32)],
        args=(d_q, d_k, d_v, d_conv, d_gl, w_in, x, d_x1, norm_w))


def mm_tn(name, pairs, tm):
    n = len(pairs)
    M = pairs[0][0].shape[1]
    widths = [b.shape[1] for _, b in pairs]

    def body(*refs):
        for a_ref, b_ref, o_ref, ob_ref in zip(refs[0:2 * n:2], refs[1:2 * n:2], refs[2 * n::2], refs[2 * n + 1::2]):
            r = _dot_tn(a_ref[...], b_ref[...])
            o_ref[...] = r
            ob_ref[...] = r.astype(BF16)

    return _call(
        body, name=name, grid=(M // tm,),
        in_specs=[sp for N in widths for sp in (pl.BlockSpec((S, tm), lambda i: (0, i)), _res((S, N)))],
        out_specs=[pl.BlockSpec((tm, N), lambda i: (i, 0)) for N in widths for _ in range(2)],
        out_shape=[jax.ShapeDtypeStruct((M, N), dt) for N in widths for dt in (F32, BF16)],
        args=[t for pair in pairs for t in pair])


GW_IN_TN = PLANE
GW_IN_SPLIT = (768, 256)


def gw_in_t(name, h, d_segs, col0, hw, sides=()):
    tn = GW_IN_TN
    starts, t0 = [], 0
    for seg in d_segs:
        starts.append(t0)
        t0 += seg.shape[0]
    ntiles = [seg.shape[0] for seg in d_segs]

    def body(h_ref, *refs):
        a_refs, o_ref, ob_ref, tile = refs[:-3], refs[-3], refs[-2], refs[-1]
        n = pl.program_id(0)
        for a_ref, st, nt in zip(a_refs, starts, ntiles):
            @pl.when((n >= st) & (n < st + nt))
            def _(a_ref=a_ref):
                tile[...] = a_ref[...]
        r = _dot_tn(tile[...], h_ref[...])
        o_ref[...] = r
        ob_ref[...] = r.astype(BF16)

    def seg_spec(st, nt):
        return pl.BlockSpec((None, S, tn), lambda n: (jnp.clip(n - st, 0, nt - 1), 0, 0))

    res = _call(
        body, sides, name=name, grid=(INW // tn,),
        in_specs=[pl.BlockSpec((S, hw), lambda n: (0, col0 // hw))] + [seg_spec(st, nt) for st, nt in zip(starts, ntiles)],
        out_specs=[pl.BlockSpec((tn, hw), lambda n: (n, 0))] * 2,
        out_shape=[jax.ShapeDtypeStruct((INW, hw), F32), jax.ShapeDtypeStruct((INW, hw), BF16)],
        scratch_shapes=[pltpu.VMEM((S, tn), BF16)], args=(h, *d_segs))
    return (res[0], res[1]) if sides else (res, [])


def _place():
    x, y, c = lax.axis_index("x"), lax.axis_index("y"), lax.axis_index("c")
    chips = [(1 - x, y), (x, 1 - y), (1 - x, 1 - y)]
    return x, y, c, chips


def _sems(n):
    return pltpu.SemaphoreType.DMA((n,))


def _remote(src, dst, send, recv, k, to):
    return pltpu.make_async_remote_copy(src_ref=src, dst_ref=dst, send_sem=send.at[k], recv_sem=recv.at[k],
                                        device_id=to, device_id_type=MESH)


def _cast_rows(dst, src, cols=slice(None)):
    rows = src.shape[0]
    step = next((s for s in (128, 64, 32, 16) if rows % s == 0), rows)
    for r0 in range(0, rows, step):
        dst[r0:r0 + step, cols] = src[r0:r0 + step, :].astype(dst.dtype)


def comm_only(name, sides):
    def body():
        pass

    return _call(body, sides, name=name, grid=(1,), in_specs=[], out_specs=[], out_shape=[], args=())[1]


def ag_blocks(shard, dtype):
    R, W = shard.shape

    def copy(outs, scr, k, block, to, src=None):
        dst = outs[0].at[block]
        return _remote(dst if src is None else src, dst, scr[1], scr[2], k, to)

    def local(outs, scr, me):
        return pltpu.make_async_copy(scr[0], outs[0].at[me], scr[3].at[0])

    def start(ins, outs, scr):
        x, y, c, chips = _place()
        me = 4 * x + 2 * y + c
        _cast_rows(scr[0], ins[0])
        local(outs, scr, me).start()
        copy(outs, scr, 0, me, (x, y, 1 - c), src=scr[0]).start()
        for j, (cx, cy) in enumerate(chips):
            copy(outs, scr, 1 + j, me, (cx, cy, c), src=scr[0]).start()

    def finish(ins, outs, scr):
        x, y, c, chips = _place()
        me, sib = 4 * x + 2 * y + c, (x, y, 1 - c)
        passed = []
        for j, (cx, cy) in enumerate(chips):
            theirs = 4 * cx + 2 * cy + c
            copy(outs, scr, 1 + j, theirs, (x, y, c)).wait_recv()
            fwd = copy(outs, scr, 4 + j, theirs, sib)
            fwd.start()
            passed.append(fwd)
        copy(outs, scr, 0, 4 * x + 2 * y + 1 - c, (x, y, c)).wait_recv()
        for j, (cx, cy) in enumerate(chips):
            copy(outs, scr, 4 + j, 4 * cx + 2 * cy + 1 - c, (x, y, c)).wait_recv()
        copy(outs, scr, 0, me, sib, src=scr[0]).wait_send()
        for j, (cx, cy) in enumerate(chips):
            copy(outs, scr, 1 + j, me, (cx, cy, c), src=scr[0]).wait_send()
        for fwd in passed:
            fwd.wait_send()
        local(outs, scr, me).wait()

    return Side((shard,), (VMEM,), (jax.ShapeDtypeStruct((NDEV, R, W), dtype),),
                (pltpu.VMEM((R, W), dtype), _sems(7), _sems(7), _sems(1)), start, finish, None, "dsxy")


def ag_blocks_relay(shard, dtype, transpose=False):
    R, W = shard.shape[::-1] if transpose else shard.shape
    half = R // 2

    def copy(outs, scr, k, block, to, src=None, rows=None):
        dst = outs[0].at[block] if rows is None else outs[0].at[block, pl.ds(rows * half, half), :]
        return _remote(dst if src is None else src, dst, scr[1], scr[2], k, to)

    def local(outs, scr, me):
        return pltpu.make_async_copy(scr[0], outs[0].at[me], scr[3].at[0])

    def own(outs, scr):
        x, y, c, _ = _place()
        me = 4 * x + 2 * y + c
        return [copy(outs, scr, k, me, to, src=scr[0])
                for k, to in enumerate([(x, y, 1 - c), (1 - x, y, c), (x, 1 - y, c)])]

    def start(ins, outs, scr):
        x, y, c, _ = _place()
        if transpose:
            scr[0][...] = ins[0][...].T.astype(dtype)
        else:
            _cast_rows(scr[0], ins[0])
        local(outs, scr, 4 * x + 2 * y + c).start()
        for cp in own(outs, scr):
            cp.start()

    def passed_on(outs, scr):
        x, y, c, _ = _place()
        sib, xn, yn = (x, y, 1 - c), (1 - x, y, c), (x, 1 - y, c)
        b_xn, b_yn, b_dg = 4 * (1 - x) + 2 * y + c, 4 * x + 2 * (1 - y) + c, 4 * (1 - x) + 2 * (1 - y) + c
        near = [copy(outs, scr, 5, b_xn, yn, rows=0), copy(outs, scr, 3, b_xn, sib),
                copy(outs, scr, 6, b_yn, xn, rows=1), copy(outs, scr, 4, b_yn, sib)]
        far = [copy(outs, scr, 7, b_dg, sib, rows=0), copy(outs, scr, 8, b_dg, sib, rows=1)]
        return (b_xn, b_yn, b_dg), near, far

    def mid(ins, outs, scr):
        x, y, c, _ = _place()
        (b_xn, b_yn, _), near, _ = passed_on(outs, scr)
        copy(outs, scr, 1, b_xn, (x, y, c)).wait_recv()
        near[0].start()
        near[1].start()
        copy(outs, scr, 2, b_yn, (x, y, c)).wait_recv()
        near[2].start()
        near[3].start()

    def finish(ins, outs, scr):
        x, y, c, _ = _place()
        here = (x, y, c)
        (b_xn, b_yn, b_dg), near, far = passed_on(outs, scr)
        copy(outs, scr, 5, b_dg, here, rows=0).wait_recv()
        far[0].start()
        copy(outs, scr, 6, b_dg, here, rows=1).wait_recv()
        far[1].start()
        flip = 1 - 2 * c
        copy(outs, scr, 0, 4 * x + 2 * y + 1 - c, here).wait_recv()
        copy(outs, scr, 3, b_xn + flip, here).wait_recv()
        copy(outs, scr, 4, b_yn + flip, here).wait_recv()
        copy(outs, scr, 7, b_dg + flip, here, rows=0).wait_recv()
        copy(outs, scr, 8, b_dg + flip, here, rows=1).wait_recv()
        for cp in own(outs, scr) + near + far:
            cp.wait_send()
        local(outs, scr, 4 * x + 2 * y + c).wait()

    return Side((shard,), (VMEM,), (jax.ShapeDtypeStruct((NDEV, R, W), dtype),),
                (pltpu.VMEM((R, W), dtype), _sems(9), _sems(9), _sems(1)), start, finish, mid, "sxy")


def copies_side(args, out_shape, n_copies, plan, peers):
    def copies(ins, outs, scr):
        return [_remote(s_, d_, scr[0], scr[1], i, to) for i, (s_, d_, to) in enumerate(plan(ins, outs))]

    def start(ins, outs, scr):
        for cp in copies(ins, outs, scr):
            cp.start()

    def finish(ins, outs, scr):
        for cp in copies(ins, outs, scr):
            cp.wait()

    return Side(tuple(args), (ANY,) * len(args), tuple(out_shape), (_sems(n_copies), _sems(n_copies)),
                start, finish, None, peers)


def rs_to_sibling(grads):
    out_shape = [jax.ShapeDtypeStruct((4,) + g.shape[1:], BF16) for g in grads]

    def plan(ins, outs):
        x, y, c, _ = _place()
        return [(g.at[2 * k + 1 - c], r.at[k], (x, y, 1 - c)) for g, r in zip(ins, outs) for k in range(4)]

    return copies_side(grads, out_shape, 4 * len(grads), plan, "s")


def rs_to_chips(parts):
    out_shape = [jax.ShapeDtypeStruct((3,) + p.shape[1:], BF16) for p in parts]

    def plan(ins, outs):
        x, y, c, chips = _place()
        return [(p.at[2 * cx + cy], r.at[j], (cx, cy, c))
                for p, r in zip(ins, outs) for j, (cx, cy) in enumerate(chips)]

    return copies_side(parts, out_shape, 3 * len(parts), plan, "dxy")


def rs_to_chips_combined(part):
    _, R, W = part.shape
    half = R // 2
    top, bot = pl.ds(0, half), pl.ds(half, half)

    def copies(ins, outs, scr):
        p, r = ins[0], outs[0]
        loc_a, loc_b, in_x, in_y, comb_a, comb_b, send, recv, loc = scr
        x, y, c, _ = _place()
        xn, yn = (1 - x, y, c), (x, 1 - y, c)
        k_xn, k_yn, k_dg = 2 * (1 - x) + y, 2 * x + 1 - y, 2 * (1 - x) + 1 - y
        direct = [_remote(p.at[k_xn, top, :], r.at[0, top, :], send, recv, 0, xn),
                  _remote(p.at[k_yn, bot, :], r.at[1, bot, :], send, recv, 1, yn),
                  _remote(p.at[k_dg, top, :], in_x, send, recv, 2, xn),
                  _remote(p.at[k_dg, bot, :], in_y, send, recv, 3, yn)]
        combined = [_remote(comb_a, r.at[1, top, :], send, recv, 4, yn),
                    _remote(comb_b, r.at[0, bot, :], send, recv, 5, xn)]
        local = [pltpu.make_async_copy(p.at[k_yn, top, :], loc_a, loc.at[0]),
                 pltpu.make_async_copy(p.at[k_xn, bot, :], loc_b, loc.at[1])]
        return direct, combined, local

    def start(ins, outs, scr):
        direct, _, local = copies(ins, outs, scr)
        for cp in local + direct:
            cp.start()

    def mid(ins, outs, scr):
        loc_a, loc_b, in_x, in_y, comb_a, comb_b = scr[:6]
        direct, combined, local = copies(ins, outs, scr)
        for mine, arrival, inbox, out, nxt in ((local[0], direct[2], in_x, comb_a, combined[0]),
                                               (local[1], direct[3], in_y, comb_b, combined[1])):
            mine.wait()
            arrival.wait_recv()
            src = loc_a if out is comb_a else loc_b
            out[...] = (src[...].astype(F32) + inbox[...].astype(F32)).astype(BF16)
            nxt.start()

    def finish(ins, outs, scr):
        direct, combined, _ = copies(ins, outs, scr)
        direct[0].wait_recv()
        direct[1].wait_recv()
        combined[0].wait_recv()
        combined[1].wait_recv()
        for cp in direct + combined:
            cp.wait_send()

    buf = pltpu.VMEM((half, W), BF16)
    return Side((part,), (ANY,), (jax.ShapeDtypeStruct((2, R, W), BF16),),
                (buf, buf, buf, buf, buf, buf, _sems(6), _sems(6), _sems(2)), start, finish, mid, "xy")


ADAM_TILE_BYTES = 3 * 512 * 1024


def _row_tiles(rows, width):
    return 2 if rows % 32 == 0 and rows * width * 4 > ADAM_TILE_BYTES else 1


def chip_sum(name, grads, recvs, c_idx, chip_idx):
    n = len(grads)

    def body(s_ref, *refs):
        k = pl.program_id(0)
        for g_ref, r_ref, p_ref, own_ref in zip(refs[:n], refs[n:2 * n], refs[2 * n::2], refs[2 * n + 1::2]):
            tot = g_ref[0] + r_ref[0].astype(F32)
            p_ref[0] = tot.astype(BF16)

            @pl.when(k == s_ref[1])
            def _(own_ref=own_ref, tot=tot):
                own_ref[...] = tot

    def block(g):
        return (1,) + g.shape[1:]

    grid_spec = pltpu.PrefetchScalarGridSpec(
        num_scalar_prefetch=1, grid=(4,),
        in_specs=[pl.BlockSpec(block(g), lambda k, s: (2 * k + s[0], 0, 0)) for g in grads]
        + [pl.BlockSpec(block(g), lambda k, s: (k, 0, 0)) for g in grads],
        out_specs=[sp for g in grads for sp in (pl.BlockSpec(block(g), lambda k, s: (k, 0, 0)),
                                                pl.BlockSpec(g.shape[1:], lambda k, s: (0, 0)))])
    res = pl.pallas_call(
        body, name=name, grid_spec=grid_spec,
        out_shape=[sh for g in grads for sh in (jax.ShapeDtypeStruct((4,) + g.shape[1:], BF16),
                                                jax.ShapeDtypeStruct(g.shape[1:], F32))],
        compiler_params=_cp(dimension_semantics=("arbitrary",)),
    )(jnp.stack([c_idx, chip_idx]), *grads, *recvs)
    return [(res[2 * j], res[2 * j + 1]) for j in range(n)]


def _adamw(w, g, m, v):
    m2 = ADAM_B1 * m + (1.0 - ADAM_B1) * g
    v2 = ADAM_B2 * v + (1.0 - ADAM_B2) * (g * g)
    m_hat = m2 / (1.0 - ADAM_B1 ** ADAM_STEP)
    v_hat = v2 / (1.0 - ADAM_B2 ** ADAM_STEP)
    delta = -ADAM_LR * (m_hat / (jnp.sqrt(v_hat) + ADAM_EPS) + ADAM_WD * w)
    return delta, m2, v2


def shard_adam(name, owns, recvs, w, m, v):
    n = len(owns)
    R = owns[0].shape[0]
    ct = min(o.shape[1] for o in owns)
    first = [sum(o.shape[1] for o in owns[:j]) // ct for j in range(n)]
    count = [o.shape[1] // ct for o in owns]
    nt = _row_tiles(R, ct)
    tr = R // nt

    def body(*refs):
        o_refs, r_refs = refs[:n], refs[n:2 * n]
        w_ref, m_ref, v_ref, g_ref, d_ref, nm_ref, nv_ref = refs[2 * n:]
        g = None
        for j in range(n):
            gj = o_refs[j][...]
            for q in range(recvs[j].shape[0]):
                gj = gj + r_refs[j][q].astype(F32)
            g = gj if g is None else jnp.where(pl.program_id(0) >= first[j], gj, g)
        delta, m2, v2 = _adamw(w_ref[...], g, m_ref[...], v_ref[...])
        g_ref[...] = g
        d_ref[...] = delta
        nm_ref[...] = m2
        nv_ref[...] = v2

    def part(j):
        return pl.BlockSpec((tr, ct), lambda k, i: (i, jnp.clip(k - first[j], 0, count[j] - 1)))

    def part3(j):
        return pl.BlockSpec((recvs[j].shape[0], tr, ct), lambda k, i: (0, i, jnp.clip(k - first[j], 0, count[j] - 1)))

    C = sum(count) * ct
    tile = pl.BlockSpec((tr, ct), lambda k, i: (i, k))
    return pl.pallas_call(
        body, name=name, grid=(sum(count), nt),
        in_specs=[part(j) for j in range(n)] + [part3(j) for j in range(n)] + [tile, tile, tile],
        out_specs=[tile] * 4, out_shape=[jax.ShapeDtypeStruct((R, C), F32)] * 4,
        compiler_params=_cp(dimension_semantics=("arbitrary", "arbitrary")),
    )(*owns, *recvs, w, m, v)


def block_adam(name, items):
    n = len(items)

    def body(*refs):
        for j, item in enumerate(items):
            o_ref, r_ref, w_ref, m_ref, v_ref = refs[5 * j:5 * j + 5]
            g = o_ref[...]
            for q in range(r_ref.shape[0]):
                g = g + r_ref[q].astype(F32)
            t = (lambda a: a.T) if item[5] else (lambda a: a)
            delta, m2, v2 = _adamw(t(w_ref[...]), g, t(m_ref[...]), t(v_ref[...]))
            for ref, val in zip(refs[5 * n + 4 * j:5 * n + 4 * j + 4], (g, delta, m2, v2)):
                ref[...] = t(val)

    args = [a for item in items for a in item[:5]]
    out_shape = [jax.ShapeDtypeStruct(item[2].shape, F32) for item in items for _ in range(4)]
    res = pl.pallas_call(
        body, name=name, grid=(1,), in_specs=[VMEM] * len(args), out_specs=[VMEM] * len(out_shape),
        out_shape=out_shape, compiler_params=_cp(dimension_semantics=("arbitrary",)))(*args)
    return [tuple(res[4 * j:4 * j + 4]) for j in range(n)]


ROW_N1, ROW_N2, ROW_BG, ROW_QN, ROW_KN, ROW_CB, ROW_LW, ROW_LB, ROW_CW = 0, 1, 2, 4, 5, 6, 7, 8, 9
PACK_ROWS = 40
SMALL = ("norm1_w", "norm2_w", "b_gate", "q_norm_w", "k_norm_w", "conv_b", "conv_ln_w", "conv_ln_b", "conv_w")


def small_sync(g, sq, sides=()):
    ns = len(SMALL)

    def copies(refs):
        pack, recv, send_sems, recv_sems = refs[ns + 2:]
        x, y, c, _ = _place()
        return [pltpu.make_async_remote_copy(
            src_ref=pack, dst_ref=recv.at[4 * x + 2 * y + c], send_sem=send_sems.at[k - 1],
            recv_sem=recv_sems.at[k - 1], device_id=(x ^ (k >> 2), y ^ ((k >> 1) & 1), c ^ (k & 1)),
            device_id_type=MESH) for k in range(1, NDEV)]

    def body(*refs):
        gi = dict(zip(SMALL, refs[:ns]))
        sq_ref, tot, pack, recv, send_sems, recv_sems = refs[ns:]
        x, y, c, _ = _place()
        me = 4 * x + 2 * y + c

        pack[...] = jnp.zeros_like(pack)
        pack[ROW_KN:ROW_KN + 1, LANES:2 * LANES] = jnp.full((1, LANES), (0.5 / D) * jnp.sum(sq_ref[...]), F32)
        pack[ROW_N1:ROW_N1 + 1, :] = gi["norm1_w"][...]
        pack[ROW_N2:ROW_N2 + 1, :] = gi["norm2_w"][...]
        pack[ROW_BG:ROW_BG + 2, :] = gi["b_gate"][...]
        pack[ROW_QN:ROW_QN + 1, 0:HD] = gi["q_norm_w"][...]
        pack[ROW_KN:ROW_KN + 1, 0:HD] = gi["k_norm_w"][...]
        pack[ROW_CB:ROW_CB + 1, 0:CC] = gi["conv_b"][...]
        pack[ROW_LW:ROW_LW + 1, 0:CC] = gi["conv_ln_w"][...]
        pack[ROW_LB:ROW_LB + 1, 0:CC] = gi["conv_ln_b"][...]
        pack[ROW_CW:ROW_CW + KW, 0:CC] = gi["conv_w"][...]

        for cp in copies(refs):
            cp.start()
        recv[me] = pack[...]

    def tail(*refs):
        tot, recv = refs[ns + 1], refs[ns + 3]
        for cp in copies(refs):
            cp.wait()
        acc = recv[0]
        for p in range(1, NDEV):
            acc = acc + recv[p]
        tot[...] = acc

    args = [g[k] for k in SMALL] + [sq]
    res = _call(
        body, sides, name="small_sync", grid=(1,), in_specs=[VMEM] * len(args), out_specs=[VMEM],
        out_shape=[jax.ShapeDtypeStruct((PACK_ROWS, D), F32)],
        scratch_shapes=[pltpu.VMEM((PACK_ROWS, D), F32), pltpu.VMEM((NDEV, PACK_ROWS, D), F32),
                        _sems(NDEV - 1), _sems(NDEV - 1)],
        args=args, own_comm=True, tail=tail)
    return (res[0][0], res[1]) if sides else res[0]


def small_adam(tot, w, m, v, me):
    ns = len(SMALL)

    def body(me_ref, tot, *refs):
        wi = dict(zip(SMALL, refs[:ns]))
        mi = dict(zip(SMALL, refs[ns:2 * ns]))
        vi = dict(zip(SMALL, refs[2 * ns:3 * ns]))
        outs = refs[3 * ns:7 * ns]
        loss_ref = refs[7 * ns]
        me = me_ref[0]

        def shard_grad(name):
            if name == "b_gate":
                return tot[ROW_BG:ROW_BG + 2, pl.ds(pl.multiple_of(me * LANES, LANES), LANES)]
            if name == "conv_w":
                win = tot[ROW_CW:ROW_CW + KW, pl.ds(pl.multiple_of((me // 2) * LANES, LANES), LANES)]
                return jnp.where(me % 2 == 1, win[:, HD:LANES], win[:, 0:HD])
            row = {"norm1_w": ROW_N1, "norm2_w": ROW_N2, "q_norm_w": ROW_QN, "k_norm_w": ROW_KN,
                   "conv_b": ROW_CB, "conv_ln_w": ROW_LW, "conv_ln_b": ROW_LB}[name]
            return tot[row:row + 1, 0:wi[name].shape[1]]

        for i, name in enumerate(SMALL):
            gr = shard_grad(name)
            delta, m2, v2 = _adamw(wi[name][...], gr, mi[name][...], vi[name][...])
            outs[4 * i][...] = gr
            outs[4 * i + 1][...] = delta
            outs[4 * i + 2][...] = m2
            outs[4 * i + 3][...] = v2
        loss_ref[...] = tot[ROW_KN:ROW_KN + 1, LANES:2 * LANES]

    out_shape = []
    for name in SMALL:
        out_shape += [jax.ShapeDtypeStruct(w[name].shape, F32)] * 4
    out_shape.append(jax.ShapeDtypeStruct((1, LANES), F32))
    args = [tot] + [w[k] for k in SMALL] + [m[k] for k in SMALL] + [v[k] for k in SMALL]
    grid_spec = pltpu.PrefetchScalarGridSpec(
        num_scalar_prefetch=1, grid=(1,), in_specs=[VMEM] * len(args), out_specs=[VMEM] * len(out_shape))
    res = pl.pallas_call(body, name="small_adam", grid_spec=grid_spec, out_shape=out_shape)(me, *args)
    out = {name: tuple(res[4 * i:4 * i + 4]) for i, name in enumerate(SMALL)}
    return out, res[4 * ns][0, 0]


MATS = ("w_in", "w_o_attn", "w_pw_conv", "w_out", "w_ffn_in", "w_ffn_out")
TRANSPOSED = ("w_in", "w_ffn_in")
WEIGHTS = ("norm1_w", "w_in", "b_gate", "q_norm_w", "k_norm_w", "w_o_attn", "conv_w", "conv_b", "conv_ln_w",
           "conv_ln_b", "w_pw_conv", "w_out", "norm2_w", "w_ffn_in", "w_ffn_out")


def _blocks_to_cols(blocks):
    n, R, C = blocks.shape
    return blocks.transpose(1, 0, 2).reshape(R, n * C)


def kernel(x, positions, norm1_w, w_in, b_gate, q_norm_w, k_norm_w, w_o_attn, conv_w, conv_b, conv_ln_w, conv_ln_b, w_pw_conv, w_out, norm2_w, w_ffn_in, w_ffn_out, loss_target, m_norm1_w, m_w_in, m_b_gate, m_q_norm_w, m_k_norm_w, m_w_o_attn, m_conv_w, m_conv_b, m_conv_ln_w, m_conv_ln_b, m_w_pw_conv, m_w_out, m_norm2_w, m_w_ffn_in, m_w_ffn_out, v_norm1_w, v_w_in, v_b_gate, v_q_norm_w, v_k_norm_w, v_w_o_attn, v_conv_w, v_conv_b, v_conv_ln_w, v_conv_ln_b, v_w_pw_conv, v_w_out, v_norm2_w, v_w_ffn_in, v_w_ffn_out):
    w = dict(norm1_w=norm1_w, w_in=w_in, b_gate=b_gate, q_norm_w=q_norm_w, k_norm_w=k_norm_w, w_o_attn=w_o_attn,
             conv_w=conv_w, conv_b=conv_b, conv_ln_w=conv_ln_w, conv_ln_b=conv_ln_b, w_pw_conv=w_pw_conv,
             w_out=w_out, norm2_w=norm2_w, w_ffn_in=w_ffn_in, w_ffn_out=w_ffn_out)
    m = dict(norm1_w=m_norm1_w, w_in=m_w_in, b_gate=m_b_gate, q_norm_w=m_q_norm_w, k_norm_w=m_k_norm_w,
             w_o_attn=m_w_o_attn, conv_w=m_conv_w, conv_b=m_conv_b, conv_ln_w=m_conv_ln_w,
             conv_ln_b=m_conv_ln_b, w_pw_conv=m_w_pw_conv, w_out=m_w_out, norm2_w=m_norm2_w,
             w_ffn_in=m_w_ffn_in, w_ffn_out=m_w_ffn_out)
    v = dict(norm1_w=v_norm1_w, w_in=v_w_in, b_gate=v_b_gate, q_norm_w=v_q_norm_w, k_norm_w=v_k_norm_w,
             w_o_attn=v_w_o_attn, conv_w=v_conv_w, conv_b=v_conv_b, conv_ln_w=v_conv_ln_w,
             conv_ln_b=v_conv_ln_b, w_pw_conv=v_w_pw_conv, w_out=v_w_out, norm2_w=v_norm2_w,
             w_ffn_in=v_w_ffn_in, w_ffn_out=v_w_ffn_out)
    def two_d(t):
        t = {k: (a[0] if a.ndim == 3 else a) for k, a in t.items()}
        return {k: (a.T if k in TRANSPOSED else a) for k, a in t.items()}

    w, m, v = two_d(w), two_d(m), two_d(v)

    x2, target = x[0], loss_target[0]
    c_idx = lax.axis_index("c").astype(jnp.int32)
    chip_idx = (2 * lax.axis_index("x") + lax.axis_index("y")).astype(jnp.int32)
    qw2 = jnp.tile(w["q_norm_w"], (1, 2))
    kw2 = jnp.tile(w["k_norm_w"], (1, 2))

    ax, ay = lax.axis_index("x"), lax.axis_index("y")
    chip_order = jnp.stack([2 * ax + ay, 2 * (1 - ax) + ay, 2 * ax + 1 - ay, 2 * (1 - ax) + 1 - ay]).astype(jnp.int32)
    h, proj, w_in_blocks, tabs = in_proj_gather(x2, w["norm1_w"], w["w_in"], chip_order, positions.reshape(S, 1))
    w_in_t = w_in_blocks.reshape(INW, D)
    (attn, lse), ((w_ffn_in_blocks,), (w_out_blocks,), (w_o_blocks,), (w_pw_blocks,), (bg_blocks,), (cw_blocks,)) = attn_fwd(
        proj, tabs, qw2, kw2, sides=(ag_blocks_relay(w["w_ffn_in"], BF16), ag_blocks_relay(w["w_out"], BF16),
                                     ag_blocks_relay(w["w_o_attn"], BF16, transpose=True),
                                     ag_blocks_relay(w["w_pw_conv"], BF16, transpose=True),
                                     ag_blocks(w["b_gate"], F32), ag_blocks(w["conv_w"], F32)))
    w_ffn_in_t = w_ffn_in_blocks.reshape(2 * FF, D)
    w_out_f = w_out_blocks.reshape(D, D)
    w_o_t, w_pw_t = w_o_blocks.reshape(D, CC), w_pw_blocks.reshape(D, CC)
    b_gate_f, conv_w_f = _blocks_to_cols(bg_blocks), _blocks_to_cols(cw_blocks)
    cpre, u3 = conv_fwd(proj, conv_w_f, w["conv_b"], w["conv_ln_w"], w["conv_ln_b"])
    x1, z, ya, yb = mix_out(x2, proj, b_gate_f, attn, u3, w_o_t, w_pw_t, w_out_f)
    (h2, gu, f), ((w_ffn_out_blocks,),) = ffn_in(x1, w["norm2_w"], w_ffn_in_t, sides=(ag_blocks_relay(w["w_ffn_out"], BF16),))
    w_ffn_out_f = w_ffn_out_blocks.reshape(FF, D)
    dy, dyb, sq = ffn_out_loss(x1, f, w_ffn_out_f, target)

    g = {}
    def blocks(name, pairs, tm):
        return [t.reshape(NDEV, t.shape[0] // NDEV, t.shape[1]) for t in mm_tn(name, pairs, tm)]

    g_ffn_out, gb_ffn_out = blocks("gw_ffn_out", [(f, dyb)], FF // 2)
    (d_gu, d_x1, d_x1b, g["norm2_w"]), ((ra_ffn_out,),) = ffn_bwd(
        dy, dyb, gu, x1, w["norm2_w"], w_ffn_in_t, w_ffn_out_f, sides=(rs_to_sibling([gb_ffn_out]),))
    g_ffn_in, gb_ffn_in = blocks("gw_ffn_in", [(d_gu, h2)], FF // 2)
    (d_ya, d_yb, d_gl, d_attn, d_u3, g["b_gate"]), ((ra_ffn_in,),) = out_bwd(
        d_x1b, proj, b_gate_f, ya, yb, w_o_t, w_pw_t, w_out_f, sides=(rs_to_sibling([gb_ffn_in]),))
    g_out, gb_out, g_w_o, gb_w_o, g_w_pw, gb_w_pw = blocks(
        "gw_out_o_pw", [(z, d_x1b), (d_ya, attn), (d_yb, u3)], D // 2)
    (d_conv, g["conv_w"], g["conv_b"], g["conv_ln_w"], g["conv_ln_b"]), ((ra_out, ra_w_o, ra_w_pw),) = conv_bwd(
        proj, cpre, d_u3, conv_w_f, conv_w_f[::-1], w["conv_ln_w"], w["conv_ln_b"],
        sides=(rs_to_sibling([gb_out, gb_w_o, gb_w_pw]),))
    (pb_ffn_out, own_ffn_out), (pb_ffn_in, own_ffn_in), (pb_out, own_out), (pb_w_o, own_w_o), (pb_w_pw, own_w_pw) = chip_sum(
        "chip_sum_early", [g_ffn_out, g_ffn_in, g_out, g_w_o, g_w_pw],
        [ra_ffn_out, ra_ffn_in, ra_out, ra_w_o, ra_w_pw], c_idx, chip_idx)
    (d_q, d_k, d_v, gqw, gkw), ((rb_ffn_out, rb_ffn_in, rb_out, rb_w_o, rb_w_pw),) = attn_bwd(
        proj, tabs, qw2, kw2, d_attn, attn, lse,
        sides=(rs_to_chips([pb_ffn_out, pb_ffn_in, pb_out, pb_w_o, pb_w_pw]),))
    g["q_norm_w"] = gqw[0:1, 0:HD] + gqw[0:1, HD:LANES]
    g["k_norm_w"] = gkw[0:1, 0:HD] + gkw[0:1, HD:LANES]
    d_segs = (d_q, d_k, d_v, d_conv, d_gl)
    parts, to_sibling, to_chips, owns, from_chips = [], None, None, [], []
    for k, hw in enumerate(GW_IN_SPLIT):
        sides = tuple(s for s in (to_chips, to_sibling) if s is not None)
        (part, part_b), outs = gw_in_t("gw_in_%d" % k, h, d_segs, sum(GW_IN_SPLIT[:k]), hw, sides=sides)
        outs = list(outs)
        if to_chips is not None:
            from_chips.append(outs.pop(0)[0])
        if to_sibling is not None:
            (pb, own), = chip_sum("chip_sum_w_in_%d" % (k - 1), [parts[-1]], [outs.pop(0)[0]], c_idx, chip_idx)
            owns.append(own)
            to_chips = rs_to_chips_combined(pb)
        else:
            to_chips = None
        parts.append(part.reshape(NDEV, INW // NDEV, hw))
        to_sibling = rs_to_sibling([part_b.reshape(NDEV, INW // NDEV, hw)])
    (grad_x, g["norm1_w"]), ((rb_prev,), (ra_last,)) = in_bwd(
        d_q, d_k, d_v, d_conv, d_gl, w_in_t, x2, d_x1, w["norm1_w"], sides=(to_chips, to_sibling))
    from_chips.append(rb_prev)
    (pb, own), = chip_sum("chip_sum_w_in_%d" % (len(GW_IN_SPLIT) - 1), [parts[-1]], [ra_last], c_idx, chip_idx)
    owns.append(own)
    small_sums, ((rb_last,),) = small_sync(g, sq, sides=(rs_to_chips_combined(pb),))
    small, loss = small_adam(small_sums, w, m, v, (4 * ax + 2 * ay + c_idx).astype(jnp.int32).reshape(1))
    from_chips.append(rb_last)

    adam_o, adam_pw, adam_out = block_adam("adam_w_o_pw_out", [
        (own_w_o, rb_w_o, w["w_o_attn"], m["w_o_attn"], v["w_o_attn"], True),
        (own_w_pw, rb_w_pw, w["w_pw_conv"], m["w_pw_conv"], v["w_pw_conv"], True),
        (own_out, rb_out, w["w_out"], m["w_out"], v["w_out"], False)])
    res = {
        "w_in": shard_adam("adam_w_in", owns, from_chips, w["w_in"], m["w_in"], v["w_in"]),
        "w_ffn_in": shard_adam("adam_w_ffn_in", [own_ffn_in], [rb_ffn_in], w["w_ffn_in"], m["w_ffn_in"], v["w_ffn_in"]),
        "w_o_attn": adam_o, "w_pw_conv": adam_pw, "w_out": adam_out,
        "w_ffn_out": shard_adam("adam_w_ffn_out", [own_ffn_out], [rb_ffn_out],
                                w["w_ffn_out"], m["w_ffn_out"], v["w_ffn_out"]),
    }
    res = {k: tuple(a.T if k in TRANSPOSED else a for a in r) for k, r in res.items()}
    res.update(small)

    def shaped(name, a):
        return a.reshape((1,) + a.shape) if name in MATS or name in ("b_gate", "conv_w") else a

    outs = [loss, grad_x.reshape(1, S, D)]
    for i in range(4):
        outs += [shaped(k, res[k][i]) for k in WEIGHTS]
    return tuple(outs)
```

```python
import functools
from typing import Callable, NamedTuple, Optional

import numpy as np
import jax
import jax.numpy as jnp
from jax import lax
from jax.experimental import pallas as pl
from jax.experimental.pallas import tpu as pltpu

F32 = jnp.float32
BF16 = jnp.bfloat16

S = 2048
D = 1024
HD = 64
QKV = 1536
CC = 512
KW = 31
FF = 2816
INW = 7680
OFF_Q, OFF_K, OFF_V, OFF_CA, OFF_CB, OFF_GA, OFF_GB = 0, 1536, 3072, 4608, 5120, 5632, 6656
DILATIONS = (1, 4, 16)
HALF_SPAN = 64
EPS = 1e-6
NEG_INF = -1e30
ROPE_THETA = 500000.0
ROT_DIM = 16

ADAM_LR = 0.001
ADAM_B1 = 0.9
ADAM_B2 = 0.999
ADAM_EPS = 1e-08
ADAM_WD = 0.01
ADAM_STEP = 10

NDEV = 8
LANES = 128
TM = 256
IN_PROJ_TM = 512
TQ = 128
VMEM_LIMIT = 56 * 1024 * 1024
MESH = pl.DeviceIdType.MESH


def _cp(**kw):
    return pltpu.CompilerParams(vmem_limit_bytes=VMEM_LIMIT, **kw)


def _row(width, col=0, tm=TM):
    return pl.BlockSpec((tm, width), lambda i: (i, col))


PLANE = 512


def _planes(width, tm=TM):
    return pl.BlockSpec((width // PLANE, tm, PLANE), lambda i: (0, i, 0))


def _res(shape):
    nd = len(shape)
    return pl.BlockSpec(shape, lambda *_: (0,) * nd, pipeline_mode=pl.Buffered(1))


def _weight_chunks(chunks, sem):
    first = pl.program_id(0) == 0
    copies = [pltpu.make_async_copy(src.at[pl.ds(a, n)], dst.at[pl.ds(a, n)], sem.at[k])
              for k, (src, dst, a, n) in enumerate(chunks)]

    @pl.when(first)
    def _():
        for cp in copies:
            cp.start()

    def arrived(k):
        @pl.when(first)
        def _():
            copies[k].wait()

    return arrived


def _dot(a, b):
    return jnp.dot(a, b, preferred_element_type=F32)


def _dot_nt(a, b):
    return lax.dot_general(a, b, (((1,), (1,)), ((), ())), preferred_element_type=F32)


def _dot_tn(a, b):
    return lax.dot_general(a, b, (((0,), (0,)), ((), ())), preferred_element_type=F32)


def _sigmoid(x):
    return jax.nn.sigmoid(x)


def _dsilu(x, sg):
    return sg * (1.0 + x * (1.0 - sg))


ANY = pl.BlockSpec(memory_space=pl.ANY)
VMEM = pl.BlockSpec(memory_space=pltpu.VMEM)


class Side(NamedTuple):
    args: tuple
    in_specs: tuple
    out_shape: tuple
    scratch: tuple
    start: Callable
    finish: Callable
    mid: Optional[Callable] = None
    peers: str = ""


BARRIER_IDS = {"s": 0, "dxy": 1, "dsxy": 2, "sxy": 3, "xy": 4}


def _peer_barrier(peers):
    x, y, c = lax.axis_index("x"), lax.axis_index("y"), lax.axis_index("c")
    where = {"s": (x, y, 1 - c), "x": (1 - x, y, c), "y": (x, 1 - y, c), "d": (1 - x, 1 - y, c)}
    barrier = pltpu.get_barrier_semaphore()
    for p in peers:
        pl.semaphore_signal(barrier, inc=1, device_id=where[p], device_id_type=MESH)
    pl.semaphore_wait(barrier, len(peers))


def _call(body, sides=(), *, name, grid, in_specs, out_specs, out_shape, scratch_shapes=(), args, own_comm=False,
          tail=None):
    assert tail is None or int(np.prod(grid)) == 1
    ni, no, ns = len(in_specs), len(out_specs), len(scratch_shapes)
    cnt = [(len(s.args), len(s.out_shape), len(s.scratch)) for s in sides]
    peers = "".join(sorted(set("".join(s.peers for s in sides))))
    if own_comm or not sides or any(not s.peers for s in sides):
        peers = ""

    def take(refs, pos, n):
        return refs[pos:pos + n], pos + n

    def full(*refs):
        m_in, pos = take(refs, 0, ni)
        s_in = []
        for a, _, _ in cnt:
            r, pos = take(refs, pos, a)
            s_in.append(r)
        m_out, pos = take(refs, pos, no)
        s_out = []
        for _, o, _ in cnt:
            r, pos = take(refs, pos, o)
            s_out.append(r)
        m_scr, pos = take(refs, pos, ns)
        s_scr = []
        for _, _, c in cnt:
            r, pos = take(refs, pos, c)
            s_scr.append(r)
        if sides:
            first = functools.reduce(jnp.logical_and, [pl.program_id(d) == 0 for d in range(len(grid))])
            last = functools.reduce(jnp.logical_and, [pl.program_id(d) == g - 1 for d, g in enumerate(grid)])

            @pl.when(first)
            def _():
                if peers:
                    _peer_barrier(peers)
                for s, a, o, c in zip(sides, s_in, s_out, s_scr):
                    s.start(a, o, c)

            steps = int(np.prod(grid))
            mid_step = (2 * steps) // 3
            if steps > 1 and any(s.mid is not None for s in sides):
                step = functools.reduce(lambda acc, d: acc * grid[d] + pl.program_id(d), range(len(grid)), 0)

                @pl.when(step == mid_step)
                def _():
                    for s, a, o, c in zip(sides, s_in, s_out, s_scr):
                        if s.mid is not None:
                            s.mid(a, o, c)

        body(*m_in, *m_out, *m_scr)
        if sides:
            @pl.when(last)
            def _():
                for s, a, o, c in zip(sides, s_in, s_out, s_scr):
                    if s.mid is not None and steps == 1:
                        s.mid(a, o, c)
                if tail is not None:
                    tail(*m_in, *m_out, *m_scr)
                for s, a, o, c in zip(sides, s_in, s_out, s_scr):
                    s.finish(a, o, c)
        elif tail is not None:
            tail(*m_in, *m_out, *m_scr)

    res = pl.pallas_call(
        full, name=name, grid=grid,
        in_specs=list(in_specs) + [sp for s in sides for sp in s.in_specs],
        out_specs=list(out_specs) + [ANY for s in sides for _ in s.out_shape],
        out_shape=list(out_shape) + [o for s in sides for o in s.out_shape],
        scratch_shapes=list(scratch_shapes) + [c for s in sides for c in s.scratch],
        compiler_params=_cp(dimension_semantics=("arbitrary",) * len(grid),
                            **({"collective_id": BARRIER_IDS[peers]} if peers else {})),
    )(*args, *[a for s in sides for a in s.args])
    res = list(res)
    if not sides:
        return res
    outs, pos = take(res, 0, no)
    side_outs = []
    for _, o, _ in cnt:
        r, pos = take(res, pos, o)
        side_outs.append(r)
    return outs, side_outs


def _inv_freq_lanes():
    inv = np.float32(ROPE_THETA) ** (-np.arange(0, ROT_DIM, 2, dtype=np.float32) / np.float32(ROT_DIM))
    lane = np.arange(LANES) % HD
    out = np.where(lane < ROT_DIM, inv[lane % (ROT_DIM // 2)], 0.0).astype(np.float32)
    return jnp.asarray(out.reshape(1, LANES))


def _rope_tables(pos, inv_freq):
    ang = pos.astype(F32) * inv_freq
    lane = lax.broadcasted_iota(jnp.int32, ang.shape, 1) % HD
    cs = jnp.cos(ang)
    sn = jnp.sin(ang)
    return (jnp.where(lane < ROT_DIM, cs, 1.0), jnp.where(lane < ROT_DIM // 2, -sn, 0.0),
            jnp.where(lane < ROT_DIM // 2, 0.0, jnp.where(lane < ROT_DIM, sn, 0.0)))


def _rope(v, c, s1, s2):
    return v * c + pltpu.roll(v, LANES - 8, axis=1) * s1 + pltpu.roll(v, 8, axis=1) * s2


def _rope_t(d, c, s1, s2):
    return d * c - pltpu.roll(d, LANES - 8, axis=1) * s1 - pltpu.roll(d, 8, axis=1) * s2


def _head_mat():
    r = lax.broadcasted_iota(jnp.int32, (LANES, LANES), 0) // HD
    c = lax.broadcasted_iota(jnp.int32, (LANES, LANES), 1) // HD
    return jnp.where(r == c, 1.0 / HD, 0.0).astype(BF16)


def _head_mean(t, e):
    hi = t.astype(BF16)
    rest = (t - hi.astype(F32)).astype(BF16)
    return _dot(hi, e) + _dot(rest, e)


def in_proj_gather(x, norm_w, shard_t, chip_order, pos_col):
    R = INW // NDEV
    tm = IN_PROJ_TM
    half, nt = R // 2, S // tm

    def body(ord_ref, x_ref, nw_ref, sh_ref, pos_ref, f_ref, h_ref, p_ref, wfull_ref, c_ref, s1_ref, s2_ref,
             wt, hs, send, recv, loc):
        kk, i = pl.program_id(0), pl.program_id(1)
        x, y, c, _ = _place()
        me, flip = 4 * x + 2 * y + c, 1 - 2 * c
        here, sib, xn, yn = (x, y, c), (x, y, 1 - c), (1 - x, y, c), (x, 1 - y, c)
        b_xn, b_yn, b_dg = 4 * (1 - x) + 2 * y + c, 4 * x + 2 * (1 - y) + c, 4 * (1 - x) + 2 * (1 - y) + c

        def cp(k, block, to, rows=None):
            dst = wt.at[block] if rows is None else wt.at[block, pl.ds(rows * half, half), :]
            return _remote(dst, dst, send, recv, k, to)

        def sends():
            return [cp(0, me, sib), cp(1, me, xn), cp(2, me, yn), cp(3, b_xn, sib), cp(4, b_yn, sib),
                    cp(5, b_xn, yn, rows=0), cp(6, b_yn, xn, rows=1), cp(7, b_dg, sib, rows=0), cp(8, b_dg, sib, rows=1)]

        def keep(j, blk0):
            pair = pl.ds(pl.multiple_of(blk0, 2), 2)
            return pltpu.make_async_copy(wt.at[pair], wfull_ref.at[pair], loc.at[j])

        @pl.when((kk == 0) & (i == 0))
        def _():
            _peer_barrier("sxy")
            _cast_rows(wt.at[me], sh_ref)
            for s_ in sends()[0:3]:
                s_.start()

            def tables(j, _):
                chunk = pl.ds(pl.multiple_of(j * TM, TM), TM)
                c_ref[chunk, :], s1_ref[chunk, :], s2_ref[chunk, :] = _rope_tables(pos_ref[chunk, :], f_ref[...])
                return 0

            lax.fori_loop(0, S // TM, tables, 0)
            cp(0, me + flip, here).wait_recv()
            keep(0, me - c).start()

        @pl.when((kk == 1) & (i == 0))
        def _():
            cp(1, b_xn, here).wait_recv()
            sends()[5].start()
            sends()[3].start()
            cp(2, b_yn, here).wait_recv()
            sends()[6].start()
            sends()[4].start()
            cp(3, b_xn + flip, here).wait_recv()
            keep(1, b_xn - c).start()

        @pl.when((kk == 2) & (i == 0))
        def _():
            cp(4, b_yn + flip, here).wait_recv()
            keep(2, b_yn - c).start()

        @pl.when((kk == 3) & (i == 0))
        def _():
            cp(5, b_dg, here, rows=0).wait_recv()
            sends()[7].start()
            cp(6, b_dg, here, rows=1).wait_recv()
            sends()[8].start()
            cp(7, b_dg + flip, here, rows=0).wait_recv()
            cp(8, b_dg + flip, here, rows=1).wait_recv()
            keep(3, b_dg - c).start()

        rows = pl.ds(pl.multiple_of(i * tm, tm), tm)

        @pl.when(kk == 0)
        def _():
            xv = x_ref[...]
            r = lax.rsqrt(jnp.mean(xv * xv, axis=-1, keepdims=True) + EPS)
            hb = (xv * r * nw_ref[...]).astype(BF16)
            h_ref[...] = hb
            hs[rows, :] = hb

        h = hs[rows, :]
        chip = ord_ref[kk]
        for cc in range(2):
            p_ref[:, cc * R:(cc + 1) * R] = _dot_nt(h, wt[2 * chip + cc])

        @pl.when((kk == 3) & (i == nt - 1))
        def _():
            for s_ in sends():
                s_.wait_send()
            for j, blk in enumerate((me, b_xn, b_yn, b_dg)):
                keep(j, blk - c).wait()

    def first_pass(kk, i):
        return jnp.where(kk == 0, i, nt - 1)

    grid_spec = pltpu.PrefetchScalarGridSpec(
        num_scalar_prefetch=1, grid=(4, nt),
        in_specs=[pl.BlockSpec((tm, D), lambda kk, i, o: (first_pass(kk, i), 0)),
                  pl.BlockSpec((1, D), lambda kk, i, o: (0, 0)), VMEM, VMEM,
                  pl.BlockSpec((1, LANES), lambda kk, i, o: (0, 0))],
        out_specs=[pl.BlockSpec((tm, D), lambda kk, i, o: (first_pass(kk, i), 0)),
                   pl.BlockSpec((tm, 2 * R), lambda kk, i, o: (i, o[kk])), ANY]
        + [pl.BlockSpec((S, LANES), lambda kk, i, o: (0, 0))] * 3,
        scratch_shapes=[pltpu.VMEM((NDEV, R, D), BF16), pltpu.VMEM((S, D), BF16), _sems(9), _sems(9), _sems(4)])
    res = pl.pallas_call(
        body, name="in_proj_gather", grid_spec=grid_spec,
        out_shape=[jax.ShapeDtypeStruct((S, D), BF16), jax.ShapeDtypeStruct((S, INW), F32),
                   jax.ShapeDtypeStruct((NDEV, R, D), BF16)] + [jax.ShapeDtypeStruct((S, LANES), F32)] * 3,
        compiler_params=_cp(dimension_semantics=("arbitrary", "arbitrary"), collective_id=BARRIER_IDS["sxy"]),
    )(chip_order, x, norm_w, shard_t, pos_col, _inv_freq_lanes())
    return res[0], res[1], res[2], tuple(res[3:])


def _qk_specs():
    nb = QKV // LANES
    return [pl.BlockSpec((S, LANES), functools.partial(lambda hp, g, o: (0, o + g * 4 + hp), o=o))
            for o in (OFF_Q // LANES, OFF_K // LANES, OFF_V // LANES)]


def _tab_specs():
    return [pl.BlockSpec((S, LANES), lambda hp, g: (0, 0), pipeline_mode=pl.Buffered(1))] * 3


def _vec_spec():
    return pl.BlockSpec((1, LANES), lambda hp, g: (0, 0))


def _sub_rows(r, d, start, n):
    if d == 1:
        return pl.ds(start, n)
    return pl.ds(r + d * start, n, stride=d)


def _band_window(i, L):
    W = min(TQ + 2 * HALF_SPAN, L)
    q0 = pl.multiple_of(i * TQ, TQ)
    k0 = pl.multiple_of(jnp.clip(q0 - HALF_SPAN, 0, L - W), HALF_SPAN)
    qpos = q0 + (lax.broadcasted_iota(jnp.int32, (2 * TQ, W), 0) & (TQ - 1))
    kpos = k0 + lax.broadcasted_iota(jnp.int32, (2 * TQ, W), 1)
    valid = jnp.abs(qpos - kpos) <= HALF_SPAN
    return W, q0, k0, valid


def _stack_heads(t, lo):
    z = jnp.zeros_like(t)
    return jnp.concatenate([jnp.where(lo, t, z), jnp.where(lo, z, t)], axis=0)


def _unstack_heads(t2, lo):
    return jnp.where(lo, t2[0:TQ], t2[TQ:2 * TQ])


CHAINS = 8


def _interleave(d):
    ru = min(d, CHAINS)
    return ru, min(CHAINS // ru, S // d // TQ)


def _for_blocks(n, fn):
    if n == 1:
        fn(0)
    else:
        def it(j, _):
            fn(j)
            return 0
        lax.fori_loop(0, n, it, 0)


def attn_fwd(proj, tabs, qw2, kw2, sides=()):
    CH = 256

    def body(q_ref, k_ref, v_ref, c_ref, s1_ref, s2_ref, qw_ref, kw_ref, at_ref, ls_ref,
             qs, ks, vs, osub, lsub, onat, lnat, qn, kn):
        g = pl.program_id(1)
        lo = lax.broadcasted_iota(jnp.int32, (1, LANES), 1) < HD
        e = _head_mat()

        def prep(i, _):
            rows = pl.ds(pl.multiple_of(i * CH, CH), CH)
            c, s1, s2 = c_ref[rows, :], s1_ref[rows, :], s2_ref[rows, :]
            for t_ref, w_ref, out, scale in ((q_ref, qw_ref, qn, HD ** -0.5), (k_ref, kw_ref, kn, 1.0)):
                t = t_ref[rows, :]
                r = lax.rsqrt(_head_mean(t * t, e) + EPS)
                out[rows, :] = _rope(t * r * w_ref[...], c, s1, s2) * scale
            return 0

        lax.fori_loop(0, S // CH, prep, 0, unroll=4)

        def group(gi, d):
            L = S // d

            ru, nb = _interleave(d)

            def stage(r, off):
                for c0 in range(0, L, CH):
                    n = min(CH, L)
                    rows = _sub_rows(r, d, c0, n)
                    dst = pl.ds(off + c0, n)
                    qs[dst, :] = qn[rows, :].astype(BF16)
                    ks[dst, :] = kn[rows, :].astype(BF16)
                    vs[dst, :] = v_ref[rows, :].astype(BF16)

            def one(off, i):
                W, q0, k0, valid = _band_window(i, L)
                q2 = _stack_heads(qs[pl.ds(off + q0, TQ), :], lo)
                sc = jnp.where(valid, _dot_nt(q2, ks[pl.ds(off + k0, W), :]), NEG_INF)
                m = jnp.max(sc, axis=-1, keepdims=True)
                p = jnp.exp(sc - m)
                den = jnp.sum(p, axis=-1, keepdims=True)
                o2 = _dot(p.astype(BF16), vs[pl.ds(off + k0, W), :]) / den
                l2 = jnp.broadcast_to(m + jnp.log(den), (2 * TQ, LANES))
                osub[pl.ds(off + q0, TQ), :] = _unstack_heads(o2, lo)
                lsub[pl.ds(off + q0, TQ), :] = _unstack_heads(l2, lo)

            def unstage(r, off):
                for c0 in range(0, L, CH):
                    n = min(CH, L)
                    rows = _sub_rows(r, d, c0, n)
                    onat[gi, rows, :] = osub[pl.ds(off + c0, n), :]
                    lnat[gi, rows, :] = lsub[pl.ds(off + c0, n), :]

            def step(t, _):
                for u in range(ru):
                    stage(t * ru + u, u * L)
                _for_blocks(L // TQ // nb, lambda j: [one(u * L, j * nb + b) for u in range(ru) for b in range(nb)])
                for u in range(ru):
                    unstage(t * ru + u, u * L)
                return 0

            lax.fori_loop(0, d // ru, step, 0)

        for gi, d in enumerate(DILATIONS):
            pl.when(g == gi)(functools.partial(group, gi, d))

        @pl.when(g == len(DILATIONS) - 1)
        def _():
            def mix(i, _):
                rows = pl.ds(pl.multiple_of(i * CH, CH), CH)
                l0, l1, l2 = lnat[0, rows, :], lnat[1, rows, :], lnat[2, rows, :]
                m = jnp.maximum(jnp.maximum(l0, l1), l2)
                e0, e1, e2 = jnp.exp(l0 - m), jnp.exp(l1 - m), jnp.exp(l2 - m)
                den = e0 + e1 + e2
                a = (e0 * onat[0, rows, :] + e1 * onat[1, rows, :] + e2 * onat[2, rows, :]) / den
                at_ref[rows, :] = a.astype(BF16)
                ls_ref[rows, :] = m + jnp.log(den)
                return 0

            lax.fori_loop(0, S // CH, mix, 0)

    out_spec = pl.BlockSpec((S, LANES), lambda hp, g: (0, hp))
    return _call(
        body, sides, name="attn_fwd", grid=(4, 3),
        in_specs=_qk_specs() + _tab_specs() + [_vec_spec(), _vec_spec()],
        out_specs=[out_spec, out_spec],
        out_shape=[jax.ShapeDtypeStruct((S, CC), BF16), jax.ShapeDtypeStruct((S, CC), F32)],
        scratch_shapes=[pltpu.VMEM((S, LANES), BF16)] * 3 + [pltpu.VMEM((S, LANES), F32)] * 2
        + [pltpu.VMEM((3, S, LANES), F32)] * 2 + [pltpu.VMEM((S, LANES), F32)] * 2,
        args=(proj, proj, proj, *tabs, qw2, kw2))


def attn_bwd(proj, tabs, qw2, kw2, d_attn, attn, lse, sides=()):
    CH = 256

    def body(q_ref, k_ref, v_ref, c_ref, s1_ref, s2_ref, qw_ref, kw_ref, do_ref, at_ref, ls_ref,
             dq_ref, dk_ref, dv_ref, gqw_ref, gkw_ref,
             qs, ks, vs, dos, dsub, lsub, dqs, dks, dvs, dnat, qx, kx, dvn, tnq, tnk, rrq, rrk):
        hp, g = pl.program_id(0), pl.program_id(1)
        lo = lax.broadcasted_iota(jnp.int32, (1, LANES), 1) < HD
        e = _head_mat()
        both = ((q_ref, qw_ref, qx, tnq, rrq, HD ** -0.5), (k_ref, kw_ref, kx, tnk, rrk, 1.0))

        @pl.when((hp == 0) & (g == 0))
        def _():
            gqw_ref[...] = jnp.zeros_like(gqw_ref)
            gkw_ref[...] = jnp.zeros_like(gkw_ref)

        def prep(i, _):
            rows = pl.ds(pl.multiple_of(i * CH, CH), CH)
            dnat[rows, :] = _head_mean(do_ref[rows, :] * at_ref[rows, :].astype(F32), e) * float(HD)
            c, s1, s2 = c_ref[rows, :], s1_ref[rows, :], s2_ref[rows, :]
            for t_ref, w_ref, x, tn_s, rr_s, scale in both:
                t = t_ref[rows, :]
                rr = lax.rsqrt(_head_mean(t * t, e) + EPS)
                tn = t * rr
                rr_s[rows, :] = rr
                tn_s[rows, :] = tn
                x[rows, :] = _rope(tn * w_ref[...], c, s1, s2) * scale
            return 0

        lax.fori_loop(0, S // CH, prep, 0, unroll=4)

        def group(d):
            L = S // d

            ru, nb = _interleave(d)

            def stage(r, off):
                for c0 in range(0, L, CH):
                    n = min(CH, L)
                    rows = _sub_rows(r, d, c0, n)
                    dst = pl.ds(off + c0, n)
                    qs[dst, :] = qx[rows, :].astype(BF16)
                    ks[dst, :] = kx[rows, :].astype(BF16)
                    vs[dst, :] = v_ref[rows, :].astype(BF16)
                    dos[dst, :] = do_ref[rows, :].astype(BF16)
                    dsub[dst, :] = dnat[rows, :]
                    lsub[dst, :] = ls_ref[rows, :]
                    dks[dst, :] = jnp.zeros((n, LANES), F32)
                    dvs[dst, :] = jnp.zeros((n, LANES), F32)

            def one(off, i):
                W, q0, k0, valid = _band_window(i, L)
                qrows, krows = pl.ds(off + q0, TQ), pl.ds(off + k0, W)
                q2 = _stack_heads(qs[qrows, :], lo)
                do2 = _stack_heads(dos[qrows, :], lo)
                kk, vv = ks[krows, :], vs[krows, :]
                lse_b, dd_b = lsub[qrows, :], dsub[qrows, :]
                lse2 = jnp.concatenate([lse_b[:, 0:1], lse_b[:, HD:HD + 1]], axis=0)
                dd2 = jnp.concatenate([dd_b[:, 0:1], dd_b[:, HD:HD + 1]], axis=0)
                sc = jnp.where(valid, _dot_nt(q2, kk), NEG_INF)
                p = jnp.exp(sc - lse2)
                ds = (p * (_dot_nt(do2, vv) - dd2)).astype(BF16)
                dqs[qrows, :] = _unstack_heads(_dot(ds, kk), lo)
                dks[krows, :] = dks[krows, :] + _dot_tn(ds, q2)
                dvs[krows, :] = dvs[krows, :] + _dot_tn(p.astype(BF16), do2)

            def unstage(r, off):
                for c0 in range(0, L, CH):
                    n = min(CH, L)
                    rows = _sub_rows(r, d, c0, n)
                    src = pl.ds(off + c0, n)
                    qx[rows, :] = dqs[src, :]
                    kx[rows, :] = dks[src, :]
                    dvn[rows, :] = dvs[src, :]

            def step(t, _):
                for u in range(ru):
                    stage(t * ru + u, u * L)
                _for_blocks(L // TQ // nb, lambda j: [one(u * L, j * nb + b) for u in range(ru) for b in range(nb)])
                for u in range(ru):
                    unstage(t * ru + u, u * L)
                return 0

            lax.fori_loop(0, d // ru, step, 0)

        for gi, d in enumerate(DILATIONS):
            pl.when(g == gi)(functools.partial(group, d))

        def emit(i, _):
            rows = pl.ds(pl.multiple_of(i * CH, CH), CH)
            c, s1, s2 = c_ref[rows, :], s1_ref[rows, :], s2_ref[rows, :]
            for (_, w_ref, x, tn_s, rr_s, scale), out, gw_ref in zip(both, (dq_ref, dk_ref), (gqw_ref, gkw_ref)):
                tn = tn_s[rows, :]
                dy = _rope_t(x[rows, :] * scale, c, s1, s2)
                gw_ref[0:1, :] = gw_ref[0:1, :] + jnp.sum(dy * tn, axis=0, keepdims=True)
                dtn = dy * w_ref[...]
                out[rows, :] = (rr_s[rows, :] * (dtn - tn * _head_mean(dtn * tn, e))).astype(BF16)
            dv_ref[rows, :] = dvn[rows, :].astype(BF16)
            return 0

        lax.fori_loop(0, S // CH, emit, 0, unroll=4)

    nat_spec = pl.BlockSpec((S, LANES), lambda hp, g: (0, hp))
    out_spec = pl.BlockSpec((None, S, LANES), lambda hp, g: (g, 0, hp))
    acc_spec = pl.BlockSpec((8, LANES), lambda hp, g: (0, 0))
    return _call(
        body, sides, name="attn_bwd", grid=(4, 3),
        in_specs=_qk_specs() + _tab_specs() + [_vec_spec(), _vec_spec(), nat_spec, nat_spec, nat_spec],
        out_specs=[out_spec] * 3 + [acc_spec] * 2,
        out_shape=[jax.ShapeDtypeStruct((QKV // PLANE, S, PLANE), BF16)] * 3 + [jax.ShapeDtypeStruct((8, LANES), F32)] * 2,
        scratch_shapes=[pltpu.VMEM((S, LANES), BF16)] * 4 + [pltpu.VMEM((S, LANES), F32)] * 13,
        args=(proj, proj, proj, *tabs, qw2, kw2, d_attn, attn, lse))


PADR = 16
CT = 128


def _conv_specs():
    return [pl.BlockSpec((S, CC), lambda i: (0, OFF_CA // CC)), pl.BlockSpec((S, CC), lambda i: (0, OFF_CB // CC))]


NCB = CC // LANES


def _pad_zero(pad):
    for cb in range(NCB):
        pad[cb, 0:PADR, :] = jnp.zeros((PADR, LANES), F32)
        pad[cb, PADR + S:PADR + S + PADR, :] = jnp.zeros((PADR, LANES), F32)


def _pad_store(pad, row0, n, val):
    for cb in range(NCB):
        pad[cb, pl.ds(pl.multiple_of(row0 + PADR, 8), n), :] = val[:, cb * LANES:(cb + 1) * LANES]


def _taps(pad_ref, cb, s0, weights):
    acc = jnp.zeros((CT, LANES), F32)
    for k in range(KW):
        acc = acc + weights[k] * pad_ref[cb, pl.ds(s0 + k + 1, CT), :]
    return acc


def conv_fwd(proj, conv_w, conv_b, ln_w, ln_b, sides=()):
    def body(a_ref, b_ref, w_ref, cb_ref, lw_ref, lb_ref, c_ref, u3_ref, upad):
        _pad_zero(upad)

        def glu(i, _):
            rows = pl.ds(pl.multiple_of(i * TM, TM), TM)
            _pad_store(upad, i * TM, TM, a_ref[rows, :] * _sigmoid(b_ref[rows, :]))
            return 0

        lax.fori_loop(0, S // TM, glu, 0)

        def chunk(i, _):
            s0 = pl.multiple_of(i * CT, CT)
            for cb in range(CC // LANES):
                cols = slice(cb * LANES, (cb + 1) * LANES)
                w = [w_ref[k:k + 1, cols] for k in range(KW)]
                c_ref[pl.ds(s0, CT), cols] = _taps(upad, cb, s0, w) + cb_ref[:, cols]
            cv = c_ref[pl.ds(s0, CT), :]
            mu = jnp.mean(cv, axis=-1, keepdims=True)
            xc = cv - mu
            rstd = lax.rsqrt(jnp.mean(xc * xc, axis=-1, keepdims=True) + EPS)
            yl = xc * rstd * lw_ref[...] + lb_ref[...]
            u3_ref[pl.ds(s0, CT), :] = (yl * _sigmoid(yl)).astype(BF16)
            return 0

        lax.fori_loop(0, S // CT, chunk, 0)

    vec = pl.BlockSpec((1, CC), lambda i: (0, 0))
    full = pl.BlockSpec((S, CC), lambda i: (0, 0))
    return _call(
        body, sides, name="conv_fwd", grid=(1,),
        in_specs=_conv_specs() + [pl.BlockSpec((KW, CC), lambda i: (0, 0)), vec, vec, vec],
        out_specs=[full, full],
        out_shape=[jax.ShapeDtypeStruct((S, CC), F32), jax.ShapeDtypeStruct((S, CC), BF16)],
        scratch_shapes=[pltpu.VMEM((NCB, S + 2 * PADR, LANES), F32)],
        args=(proj, proj, conv_w, conv_b, ln_w, ln_b))


def conv_bwd(proj, cpre, d_u3, conv_w, conv_w_rev, ln_w, ln_b, sides=()):
    def body(a_ref, b_ref, c_ref, du3_ref, w_ref, wr_ref, lw_ref, lb_ref,
             dc_ref, gw_ref, gcb_ref, glw_ref, glb_ref, upad, dpad):
        _pad_zero(upad)
        _pad_zero(dpad)
        gw_ref[...] = jnp.zeros_like(gw_ref)

        def ln_bwd(i, carry):
            gcb, glw, glb = carry
            rows = pl.ds(pl.multiple_of(i * TM, TM), TM)
            _pad_store(upad, i * TM, TM, a_ref[rows, :] * _sigmoid(b_ref[rows, :]))
            cv = c_ref[rows, :]
            mu = jnp.mean(cv, axis=-1, keepdims=True)
            xc = cv - mu
            rstd = lax.rsqrt(jnp.mean(xc * xc, axis=-1, keepdims=True) + EPS)
            xh = xc * rstd
            yl = xh * lw_ref[...] + lb_ref[...]
            dyl = du3_ref[rows, :] * _dsilu(yl, _sigmoid(yl))
            dxh = dyl * lw_ref[...]
            dcv = rstd * (dxh - jnp.mean(dxh, axis=-1, keepdims=True)
                          - xh * jnp.mean(dxh * xh, axis=-1, keepdims=True))
            _pad_store(dpad, i * TM, TM, dcv)
            return (gcb + jnp.sum(dcv, axis=0, keepdims=True),
                    glw + jnp.sum(dyl * xh, axis=0, keepdims=True),
                    glb + jnp.sum(dyl, axis=0, keepdims=True))

        z = jnp.zeros((1, CC), F32)
        gcb, glw, glb = lax.fori_loop(0, S // TM, ln_bwd, (z, z, z))
        gcb_ref[...] = gcb
        glw_ref[...] = glw
        glb_ref[...] = glb

        def chunk(i, _):
            s0 = pl.multiple_of(i * CT, CT)
            for cb in range(CC // LANES):
                cols = slice(cb * LANES, (cb + 1) * LANES)
                wr = [wr_ref[k:k + 1, cols] for k in range(KW)]
                du = _taps(dpad, cb, s0, wr)
                dcv = dpad[cb, pl.ds(s0 + PADR, CT), :]
                for k in range(KW):
                    gw_ref[k:k + 1, cols] = gw_ref[k:k + 1, cols] + jnp.sum(
                        upad[cb, pl.ds(s0 + k + 1, CT), :] * dcv, axis=0, keepdims=True)
                av = a_ref[pl.ds(s0, CT), cols]
                sb = _sigmoid(b_ref[pl.ds(s0, CT), cols])
                dc_ref[0, pl.ds(s0, CT), cols] = (du * sb).astype(BF16)
                dc_ref[1, pl.ds(s0, CT), cols] = (du * av * sb * (1.0 - sb)).astype(BF16)
            return 0

        lax.fori_loop(0, S // CT, chunk, 0)

    vec = pl.BlockSpec((1, CC), lambda i: (0, 0))
    full = pl.BlockSpec((S, CC), lambda i: (0, 0))
    wsp = pl.BlockSpec((KW, CC), lambda i: (0, 0))
    return _call(
        body, sides, name="conv_bwd", grid=(1,),
        in_specs=_conv_specs() + [full, full, wsp, wsp, vec, vec],
        out_specs=[pl.BlockSpec((2, S, CC), lambda i: (0, 0, 0)), wsp, vec, vec, vec],
        out_shape=[jax.ShapeDtypeStruct((2, S, CC), BF16), jax.ShapeDtypeStruct((KW, CC), F32)]
        + [jax.ShapeDtypeStruct((1, CC), F32)] * 3,
        scratch_shapes=[pltpu.VMEM((NCB, S + 2 * PADR, LANES), F32)] * 2,
        args=(proj, proj, cpre, d_u3, conv_w, conv_w_rev, ln_w, ln_b))


def _gate_specs():
    return [_row(CC, col=OFF_GA // CC + j) for j in range(4)]


def _gates(g_refs, bg_ref):
    ga = _sigmoid(jnp.concatenate([g_refs[0][...], g_refs[1][...]], axis=1) + bg_ref[0:1, :])
    gb = _sigmoid(jnp.concatenate([g_refs[2][...], g_refs[3][...]], axis=1) + bg_ref[1:2, :])
    return ga, gb


def mix_out(x, proj, b_gate, attn, u3, w_o, w_pw, w_out):
    def body(x_ref, g0, g1, g2, g3, bg_ref, at_ref, u3_ref, wo_ref, wp_ref, wout_ref,
             x1_ref, z_ref, ya_ref, yb_ref):
        ga, gb = _gates((g0, g1, g2, g3), bg_ref)
        ya = _dot_nt(at_ref[...], wo_ref[...])
        yb = _dot_nt(u3_ref[...], wp_ref[...])
        z = (ga * ya + gb * yb).astype(BF16)
        ya_ref[...] = ya.astype(BF16)
        yb_ref[...] = yb.astype(BF16)
        z_ref[...] = z
        x1_ref[...] = x_ref[...] + _dot(z, wout_ref[...])

    return pl.pallas_call(
        body, name="mix_out", grid=(S // TM,),
        in_specs=[_row(D)] + _gate_specs() + [_res((2, D)), _row(CC), _row(CC),
                                              _res((D, CC)), _res((D, CC)), _res((D, D))],
        out_specs=[_row(D)] * 4,
        out_shape=[jax.ShapeDtypeStruct((S, D), F32)] + [jax.ShapeDtypeStruct((S, D), BF16)] * 3,
        compiler_params=_cp(dimension_semantics=("arbitrary",)),
    )(x, proj, proj, proj, proj, b_gate, attn, u3, w_o, w_pw, w_out)


def out_bwd(d_x1b, proj, b_gate, ya, yb, w_o, w_pw, w_out, sides=()):
    def body(dx_ref, g0, g1, g2, g3, bg_ref, ya_ref, yb_ref, wo_ref, wp_ref, wout_ref,
             dya_ref, dyb_ref, dgl_ref, dat_ref, du3_ref, gbg_ref):
        @pl.when(pl.program_id(0) == 0)
        def _():
            gbg_ref[...] = jnp.zeros_like(gbg_ref)

        ga, gb = _gates((g0, g1, g2, g3), bg_ref)
        dz = _dot_nt(dx_ref[...], wout_ref[...])
        dya = (dz * ga).astype(BF16)
        dyb = (dz * gb).astype(BF16)
        dgla = dz * ya_ref[...].astype(F32) * ga * (1.0 - ga)
        dglb = dz * yb_ref[...].astype(F32) * gb * (1.0 - gb)
        dya_ref[...] = dya
        dyb_ref[...] = dyb
        for j in range(2):
            dgl_ref[j] = dgla[:, j * PLANE:(j + 1) * PLANE].astype(BF16)
            dgl_ref[2 + j] = dglb[:, j * PLANE:(j + 1) * PLANE].astype(BF16)
        gbg_ref[0:1, :] = gbg_ref[0:1, :] + jnp.sum(dgla, axis=0, keepdims=True)
        gbg_ref[1:2, :] = gbg_ref[1:2, :] + jnp.sum(dglb, axis=0, keepdims=True)
        dat_ref[...] = _dot(dya, wo_ref[...])
        du3_ref[...] = _dot(dyb, wp_ref[...])

    return _call(
        body, sides, name="out_bwd", grid=(S // TM,),
        in_specs=[_row(D)] + _gate_specs() + [_res((2, D)), _row(D), _row(D),
                                              _res((D, CC)), _res((D, CC)), _res((D, D))],
        out_specs=[_row(D), _row(D), _planes(2 * D), _row(CC), _row(CC), pl.BlockSpec((2, D), lambda i: (0, 0))],
        out_shape=[jax.ShapeDtypeStruct((S, D), BF16)] * 2 + [jax.ShapeDtypeStruct((2 * D // PLANE, S, PLANE), BF16)]
        + [jax.ShapeDtypeStruct((S, CC), F32)] * 2 + [jax.ShapeDtypeStruct((2, D), F32)],
        args=(d_x1b, proj, proj, proj, proj, b_gate, ya, yb, w_o, w_pw, w_out))


def ffn_in(x1, norm_w, w_ffn_in, sides=()):
    half = FF // 2

    def body(x_ref, nw_ref, w_hbm, h_ref, gu_ref, f_ref, w_ref, sem):
        arrived = _weight_chunks([(w_hbm, w_ref, a, half) for j in range(2) for a in (j * half, FF + j * half)], sem)
        xv = x_ref[...]
        r = lax.rsqrt(jnp.mean(xv * xv, axis=-1, keepdims=True) + EPS)
        h = (xv * r * nw_ref[...]).astype(BF16)
        h_ref[...] = h
        for j in range(2):
            arrived(2 * j)
            gt = _dot_nt(h, w_ref[j * half:(j + 1) * half, :])
            arrived(2 * j + 1)
            up = _dot_nt(h, w_ref[FF + j * half:FF + (j + 1) * half, :])
            gu_ref[:, j * half:(j + 1) * half] = gt.astype(BF16)
            gu_ref[:, FF + j * half:FF + (j + 1) * half] = up.astype(BF16)
            f_ref[:, j * half:(j + 1) * half] = (gt * _sigmoid(gt) * up).astype(BF16)

    return _call(
        body, sides, name="ffn_in", grid=(S // TM,),
        in_specs=[_row(D), _res((1, D)), ANY],
        out_specs=[_row(D), _row(2 * FF), _row(FF)],
        out_shape=[jax.ShapeDtypeStruct((S, D), BF16), jax.ShapeDtypeStruct((S, 2 * FF), BF16),
                   jax.ShapeDtypeStruct((S, FF), BF16)],
        scratch_shapes=[pltpu.VMEM((2 * FF, D), BF16), _sems(4)],
        args=(x1, norm_w, w_ffn_in))


def ffn_out_loss(x1, f, w_ffn_out, target):
    def body(x_ref, f_ref, w_ref, t_ref, dy_ref, dyb_ref, sq_ref):
        @pl.when(pl.program_id(0) == 0)
        def _():
            sq_ref[...] = jnp.zeros_like(sq_ref)

        diff = x_ref[...] + _dot(f_ref[...], w_ref[...]) - t_ref[...]
        dy = diff * (1.0 / D)
        dy_ref[...] = dy
        dyb_ref[...] = dy.astype(BF16)
        sq_ref[...] = sq_ref[...] + jnp.sum((diff * diff).reshape(TM // 8, 8, D), axis=0)

    return pl.pallas_call(
        body, name="ffn_out_loss", grid=(S // TM,),
        in_specs=[_row(D), _row(FF), _res((FF, D)), _row(D)],
        out_specs=[_row(D), _row(D), pl.BlockSpec((8, D), lambda i: (0, 0))],
        out_shape=[jax.ShapeDtypeStruct((S, D), F32), jax.ShapeDtypeStruct((S, D), BF16),
                   jax.ShapeDtypeStruct((8, D), F32)],
        compiler_params=_cp(dimension_semantics=("arbitrary",)),
    )(x1, f, w_ffn_out, target)


def _rms_bwd(xv, nw, dh):
    r = lax.rsqrt(jnp.mean(xv * xv, axis=-1, keepdims=True) + EPS)
    xn = xv * r
    dxn = dh * nw
    dx = r * (dxn - xn * jnp.mean(dxn * xn, axis=-1, keepdims=True))
    return dx, dh * xn


def ffn_bwd(dy, dyb, gu, x1, norm_w, w_ffn_in, w_ffn_out, sides=()):
    half = FF // 2

    def body(dy_ref, dyb_ref, gu_ref, x_ref, nw_ref, wi_hbm, wo_hbm, dgu_ref, dx_ref, dxb_ref, gn_ref,
             wi_ref, wo_ref, sem):
        arrived = _weight_chunks([chunk for j in range(2) for chunk in (
            (wo_hbm, wo_ref, j * half, half), (wi_hbm, wi_ref, j * half, half), (wi_hbm, wi_ref, FF + j * half, half))], sem)

        @pl.when(pl.program_id(0) == 0)
        def _():
            gn_ref[...] = jnp.zeros_like(gn_ref)

        dh = jnp.zeros((TM, D), F32)
        for j in range(2):
            gate, upper = slice(j * half, (j + 1) * half), slice(FF + j * half, FF + (j + 1) * half)
            arrived(3 * j)
            df = _dot_nt(dyb_ref[...], wo_ref[gate, :])
            gt = gu_ref[:, gate].astype(F32)
            up = gu_ref[:, upper].astype(F32)
            sg = _sigmoid(gt)
            dgt = (df * up * _dsilu(gt, sg)).astype(BF16)
            dup = (df * gt * sg).astype(BF16)
            dgu_ref[:, gate] = dgt
            dgu_ref[:, upper] = dup
            arrived(3 * j + 1)
            arrived(3 * j + 2)
            dh = dh + _dot(dgt, wi_ref[gate, :]) + _dot(dup, wi_ref[upper, :])
        dxn, gw = _rms_bwd(x_ref[...], nw_ref[...], dh)
        dx = dy_ref[...] + dxn
        dx_ref[...] = dx
        dxb_ref[...] = dx.astype(BF16)
        gn_ref[...] = gn_ref[...] + jnp.sum(gw, axis=0, keepdims=True)

    return _call(
        body, sides, name="ffn_bwd", grid=(S // TM,),
        in_specs=[_row(D), _row(D), _row(2 * FF), _row(D), _res((1, D)), ANY, ANY],
        out_specs=[_row(2 * FF), _row(D), _row(D), pl.BlockSpec((1, D), lambda i: (0, 0))],
        out_shape=[jax.ShapeDtypeStruct((S, 2 * FF), BF16), jax.ShapeDtypeStruct((S, D), F32),
                   jax.ShapeDtypeStruct((S, D), BF16), jax.ShapeDtypeStruct((1, D), F32)],
        scratch_shapes=[pltpu.VMEM((2 * FF, D), BF16), pltpu.VMEM((FF, D), BF16), _sems(6)],
        args=(dy, dyb, gu, x1, norm_w, w_ffn_in, w_ffn_out))


def in_bwd(d_q, d_k, d_v, d_conv, d_gl, w_in, x, d_x1, norm_w, sides=()):
    segs = ((OFF_Q, QKV), (OFF_K, QKV), (OFF_V, QKV), (OFF_CA, 2 * CC), (OFF_GA, 2 * D))

    def body(dq_ref, dk_ref, dv_ref, dc_ref, dg_ref, w_ref, x_ref, dx1_ref, nw_ref, gx_ref, gn_ref):
        @pl.when(pl.program_id(0) == 0)
        def _():
            gn_ref[...] = jnp.zeros_like(gn_ref)

        dh = jnp.zeros((TM, D), F32)
        for ref, (off, width) in zip((dq_ref, dk_ref, dv_ref, dc_ref, dg_ref), segs):
            for j in range(width // PLANE):
                dh = dh + _dot(ref[j], w_ref[off + j * PLANE:off + (j + 1) * PLANE, :])
        dxn, gw = _rms_bwd(x_ref[...], nw_ref[...], dh)
        gx_ref[...] = dx1_ref[...] + dxn
        gn_ref[...] = gn_ref[...] + jnp.sum(gw, axis=0, keepdims=True)

    return _call(
        body, sides, name="in_bwd", grid=(S // TM,),
        in_specs=[_planes(QKV)] * 3 + [_planes(2 * CC), _planes(2 * D), _res((INW, D)), _row(D), _row(D), _res((1, D))],
        out_specs=[_row(D), pl.BlockSpec((1, D), lambda i: (0, 0))],
        out_shape=[jax.ShapeDtypeStruct((S, D), F32), jax.ShapeDtypeStruct((1, D), F32)],
        args=(d_q, d_k, d_v, d_conv, d_gl, w_in, x, d_x1, norm_w))


def mm_tn(name, pairs, tm):
    n = len(pairs)
    M = pairs[0][0].shape[1]
    widths = [b.shape[1] for _, b in pairs]

    def body(*refs):
        for a_ref, b_ref, o_ref, ob_ref in zip(refs[0:2 * n:2], refs[1:2 * n:2], refs[2 * n::2], refs[2 * n + 1::2]):
            r = _dot_tn(a_ref[...], b_ref[...])
            o_ref[...] = r
            ob_ref[...] = r.astype(BF16)

    return _call(
        body, name=name, grid=(M // tm,),
        in_specs=[sp for N in widths for sp in (pl.BlockSpec((S, tm), lambda i: (0, i)), _res((S, N)))],
        out_specs=[pl.BlockSpec((tm, N), lambda i: (i, 0)) for N in widths for _ in range(2)],
        out_shape=[jax.ShapeDtypeStruct((M, N), dt) for N in widths for dt in (F32, BF16)],
        args=[t for pair in pairs for t in pair])


GW_IN_TN = PLANE
GW_IN_SPLIT = (768, 256)


def gw_in_t(name, h, d_segs, col0, hw, sides=()):
    tn = GW_IN_TN
    starts, t0 = [], 0
    for seg in d_segs:
        starts.append(t0)
        t0 += seg.shape[0]
    ntiles = [seg.shape[0] for seg in d_segs]

    def body(h_ref, *refs):
        a_refs, o_ref, ob_ref = refs[:-2], refs[-2], refs[-1]
        n = pl.program_id(0)
        for a_ref, st, nt in zip(a_refs, starts, ntiles):
            @pl.when((n >= st) & (n < st + nt))
            def _(a_ref=a_ref):
                r = _dot_tn(a_ref[...], h_ref[...])
                o_ref[...] = r
                ob_ref[...] = r.astype(BF16)

    def seg_spec(st, nt):
        return pl.BlockSpec((None, S, tn), lambda n: (jnp.clip(n - st, 0, nt - 1), 0, 0))

    res = _call(
        body, sides, name=name, grid=(INW // tn,),
        in_specs=[pl.BlockSpec((S, hw), lambda n: (0, col0 // hw))] + [seg_spec(st, nt) for st, nt in zip(starts, ntiles)],
        out_specs=[pl.BlockSpec((tn, hw), lambda n: (n, 0))] * 2,
        out_shape=[jax.ShapeDtypeStruct((INW, hw), F32), jax.ShapeDtypeStruct((INW, hw), BF16)],
        args=(h, *d_segs))
    return (res[0], res[1]) if sides else (res, [])


def _place():
    x, y, c = lax.axis_index("x"), lax.axis_index("y"), lax.axis_index("c")
    chips = [(1 - x, y), (x, 1 - y), (1 - x, 1 - y)]
    return x, y, c, chips


def _sems(n):
    return pltpu.SemaphoreType.DMA((n,))


def _remote(src, dst, send, recv, k, to):
    return pltpu.make_async_remote_copy(src_ref=src, dst_ref=dst, send_sem=send.at[k], recv_sem=recv.at[k],
                                        device_id=to, device_id_type=MESH)


def _cast_rows(dst, src, cols=slice(None)):
    rows = src.shape[0]
    step = next((s for s in (128, 64, 32, 16) if rows % s == 0), rows)
    for r0 in range(0, rows, step):
        dst[r0:r0 + step, cols] = src[r0:r0 + step, :].astype(dst.dtype)


def comm_only(name, sides):
    def body():
        pass

    return _call(body, sides, name=name, grid=(1,), in_specs=[], out_specs=[], out_shape=[], args=())[1]


def ag_blocks(shard, dtype):
    R, W = shard.shape

    def copy(outs, scr, k, block, to, src=None):
        dst = outs[0].at[block]
        return _remote(dst if src is None else src, dst, scr[1], scr[2], k, to)

    def local(outs, scr, me):
        return pltpu.make_async_copy(scr[0], outs[0].at[me], scr[3].at[0])

    def start(ins, outs, scr):
        x, y, c, chips = _place()
        me = 4 * x + 2 * y + c
        _cast_rows(scr[0], ins[0])
        local(outs, scr, me).start()
        copy(outs, scr, 0, me, (x, y, 1 - c), src=scr[0]).start()
        for j, (cx, cy) in enumerate(chips):
            copy(outs, scr, 1 + j, me, (cx, cy, c), src=scr[0]).start()

    def finish(ins, outs, scr):
        x, y, c, chips = _place()
        me, sib = 4 * x + 2 * y + c, (x, y, 1 - c)
        passed = []
        for j, (cx, cy) in enumerate(chips):
            theirs = 4 * cx + 2 * cy + c
            copy(outs, scr, 1 + j, theirs, (x, y, c)).wait_recv()
            fwd = copy(outs, scr, 4 + j, theirs, sib)
            fwd.start()
            passed.append(fwd)
        copy(outs, scr, 0, 4 * x + 2 * y + 1 - c, (x, y, c)).wait_recv()
        for j, (cx, cy) in enumerate(chips):
            copy(outs, scr, 4 + j, 4 * cx + 2 * cy + 1 - c, (x, y, c)).wait_recv()
        copy(outs, scr, 0, me, sib, src=scr[0]).wait_send()
        for j, (cx, cy) in enumerate(chips):
            copy(outs, scr, 1 + j, me, (cx, cy, c), src=scr[0]).wait_send()
        for fwd in passed:
            fwd.wait_send()
        local(outs, scr, me).wait()

    return Side((shard,), (VMEM,), (jax.ShapeDtypeStruct((NDEV, R, W), dtype),),
                (pltpu.VMEM((R, W), dtype), _sems(7), _sems(7), _sems(1)), start, finish, None, "dsxy")


def ag_blocks_relay(shard, dtype, transpose=False):
    R, W = shard.shape[::-1] if transpose else shard.shape
    half = R // 2

    def copy(outs, scr, k, block, to, src=None, rows=None):
        dst = outs[0].at[block] if rows is None else outs[0].at[block, pl.ds(rows * half, half), :]
        return _remote(dst if src is None else src, dst, scr[1], scr[2], k, to)

    def local(outs, scr, me):
        return pltpu.make_async_copy(scr[0], outs[0].at[me], scr[3].at[0])

    def own(outs, scr):
        x, y, c, _ = _place()
        me = 4 * x + 2 * y + c
        return [copy(outs, scr, k, me, to, src=scr[0])
                for k, to in enumerate([(x, y, 1 - c), (1 - x, y, c), (x, 1 - y, c)])]

    def start(ins, outs, scr):
        x, y, c, _ = _place()
        if transpose:
            scr[0][...] = ins[0][...].T.astype(dtype)
        else:
            _cast_rows(scr[0], ins[0])
        local(outs, scr, 4 * x + 2 * y + c).start()
        for cp in own(outs, scr):
            cp.start()

    def passed_on(outs, scr):
        x, y, c, _ = _place()
        sib, xn, yn = (x, y, 1 - c), (1 - x, y, c), (x, 1 - y, c)
        b_xn, b_yn, b_dg = 4 * (1 - x) + 2 * y + c, 4 * x + 2 * (1 - y) + c, 4 * (1 - x) + 2 * (1 - y) + c
        near = [copy(outs, scr, 5, b_xn, yn, rows=0), copy(outs, scr, 3, b_xn, sib),
                copy(outs, scr, 6, b_yn, xn, rows=1), copy(outs, scr, 4, b_yn, sib)]
        far = [copy(outs, scr, 7, b_dg, sib, rows=0), copy(outs, scr, 8, b_dg, sib, rows=1)]
        return (b_xn, b_yn, b_dg), near, far

    def mid(ins, outs, scr):
        x, y, c, _ = _place()
        (b_xn, b_yn, _), near, _ = passed_on(outs, scr)
        copy(outs, scr, 1, b_xn, (x, y, c)).wait_recv()
        near[0].start()
        near[1].start()
        copy(outs, scr, 2, b_yn, (x, y, c)).wait_recv()
        near[2].start()
        near[3].start()

    def finish(ins, outs, scr):
        x, y, c, _ = _place()
        here = (x, y, c)
        (b_xn, b_yn, b_dg), near, far = passed_on(outs, scr)
        copy(outs, scr, 5, b_dg, here, rows=0).wait_recv()
        far[0].start()
        copy(outs, scr, 6, b_dg, here, rows=1).wait_recv()
        far[1].start()
        flip = 1 - 2 * c
        copy(outs, scr, 0, 4 * x + 2 * y + 1 - c, here).wait_recv()
        copy(outs, scr, 3, b_xn + flip, here).wait_recv()
        copy(outs, scr, 4, b_yn + flip, here).wait_recv()
        copy(outs, scr, 7, b_dg + flip, here, rows=0).wait_recv()
        copy(outs, scr, 8, b_dg + flip, here, rows=1).wait_recv()
        for cp in own(outs, scr) + near + far:
            cp.wait_send()
        local(outs, scr, 4 * x + 2 * y + c).wait()

    return Side((shard,), (VMEM,), (jax.ShapeDtypeStruct((NDEV, R, W), dtype),),
                (pltpu.VMEM((R, W), dtype), _sems(9), _sems(9), _sems(1)), start, finish, mid, "sxy")


def copies_side(args, out_shape, n_copies, plan, peers):
    def copies(ins, outs, scr):
        return [_remote(s_, d_, scr[0], scr[1], i, to) for i, (s_, d_, to) in enumerate(plan(ins, outs))]

    def start(ins, outs, scr):
        for cp in copies(ins, outs, scr):
            cp.start()

    def finish(ins, outs, scr):
        for cp in copies(ins, outs, scr):
            cp.wait()

    return Side(tuple(args), (ANY,) * len(args), tuple(out_shape), (_sems(n_copies), _sems(n_copies)),
                start, finish, None, peers)


def rs_to_sibling(grads):
    out_shape = [jax.ShapeDtypeStruct((4,) + g.shape[1:], BF16) for g in grads]

    def plan(ins, outs):
        x, y, c, _ = _place()
        return [(g.at[2 * k + 1 - c], r.at[k], (x, y, 1 - c)) for g, r in zip(ins, outs) for k in range(4)]

    return copies_side(grads, out_shape, 4 * len(grads), plan, "s")


def rs_to_chips(parts):
    out_shape = [jax.ShapeDtypeStruct((3,) + p.shape[1:], BF16) for p in parts]

    def plan(ins, outs):
        x, y, c, chips = _place()
        return [(p.at[2 * cx + cy], r.at[j], (cx, cy, c))
                for p, r in zip(ins, outs) for j, (cx, cy) in enumerate(chips)]

    return copies_side(parts, out_shape, 3 * len(parts), plan, "dxy")


def rs_to_chips_combined(part):
    _, R, W = part.shape
    half = R // 2
    top, bot = pl.ds(0, half), pl.ds(half, half)

    def copies(ins, outs, scr):
        p, r = ins[0], outs[0]
        loc_a, loc_b, in_x, in_y, comb_a, comb_b, send, recv, loc = scr
        x, y, c, _ = _place()
        xn, yn = (1 - x, y, c), (x, 1 - y, c)
        k_xn, k_yn, k_dg = 2 * (1 - x) + y, 2 * x + 1 - y, 2 * (1 - x) + 1 - y
        direct = [_remote(p.at[k_xn, top, :], r.at[0, top, :], send, recv, 0, xn),
                  _remote(p.at[k_yn, bot, :], r.at[1, bot, :], send, recv, 1, yn),
                  _remote(p.at[k_dg, top, :], in_x, send, recv, 2, xn),
                  _remote(p.at[k_dg, bot, :], in_y, send, recv, 3, yn)]
        combined = [_remote(comb_a, r.at[1, top, :], send, recv, 4, yn),
                    _remote(comb_b, r.at[0, bot, :], send, recv, 5, xn)]
        local = [pltpu.make_async_copy(p.at[k_yn, top, :], loc_a, loc.at[0]),
                 pltpu.make_async_copy(p.at[k_xn, bot, :], loc_b, loc.at[1])]
        return direct, combined, local

    def start(ins, outs, scr):
        direct, _, local = copies(ins, outs, scr)
        for cp in local + direct:
            cp.start()

    def mid(ins, outs, scr):
        loc_a, loc_b, in_x, in_y, comb_a, comb_b = scr[:6]
        direct, combined, local = copies(ins, outs, scr)
        for mine, arrival, inbox, out, nxt in ((local[0], direct[2], in_x, comb_a, combined[0]),
                                               (local[1], direct[3], in_y, comb_b, combined[1])):
            mine.wait()
            arrival.wait_recv()
            src = loc_a if out is comb_a else loc_b
            out[...] = (src[...].astype(F32) + inbox[...].astype(F32)).astype(BF16)
            nxt.start()

    def finish(ins, outs, scr):
        direct, combined, _ = copies(ins, outs, scr)
        direct[0].wait_recv()
        direct[1].wait_recv()
        combined[0].wait_recv()
        combined[1].wait_recv()
        for cp in direct + combined:
            cp.wait_send()

    buf = pltpu.VMEM((half, W), BF16)
    return Side((part,), (ANY,), (jax.ShapeDtypeStruct((2, R, W), BF16),),
                (buf, buf, buf, buf, buf, buf, _sems(6), _sems(6), _sems(2)), start, finish, mid, "xy")


ADAM_TILE_BYTES = 3 * 512 * 1024


def _row_tiles(rows, width):
    return 2 if rows % 32 == 0 and rows * width * 4 > ADAM_TILE_BYTES else 1


def chip_sum(name, grads, recvs, c_idx, chip_idx):
    n = len(grads)

    def body(s_ref, *refs):
        k = pl.program_id(0)
        for g_ref, r_ref, p_ref, own_ref in zip(refs[:n], refs[n:2 * n], refs[2 * n::2], refs[2 * n + 1::2]):
            tot = g_ref[0] + r_ref[0].astype(F32)
            p_ref[0] = tot.astype(BF16)

            @pl.when(k == s_ref[1])
            def _(own_ref=own_ref, tot=tot):
                own_ref[...] = tot

    def block(g):
        return (1,) + g.shape[1:]

    grid_spec = pltpu.PrefetchScalarGridSpec(
        num_scalar_prefetch=1, grid=(4,),
        in_specs=[pl.BlockSpec(block(g), lambda k, s: (2 * k + s[0], 0, 0)) for g in grads]
        + [pl.BlockSpec(block(g), lambda k, s: (k, 0, 0)) for g in grads],
        out_specs=[sp for g in grads for sp in (pl.BlockSpec(block(g), lambda k, s: (k, 0, 0)),
                                                pl.BlockSpec(g.shape[1:], lambda k, s: (0, 0)))])
    res = pl.pallas_call(
        body, name=name, grid_spec=grid_spec,
        out_shape=[sh for g in grads for sh in (jax.ShapeDtypeStruct((4,) + g.shape[1:], BF16),
                                                jax.ShapeDtypeStruct(g.shape[1:], F32))],
        compiler_params=_cp(dimension_semantics=("arbitrary",)),
    )(jnp.stack([c_idx, chip_idx]), *grads, *recvs)
    return [(res[2 * j], res[2 * j + 1]) for j in range(n)]


def _adamw(w, g, m, v):
    m2 = ADAM_B1 * m + (1.0 - ADAM_B1) * g
    v2 = ADAM_B2 * v + (1.0 - ADAM_B2) * (g * g)
    m_hat = m2 / (1.0 - ADAM_B1 ** ADAM_STEP)
    v_hat = v2 / (1.0 - ADAM_B2 ** ADAM_STEP)
    delta = -ADAM_LR * (m_hat / (jnp.sqrt(v_hat) + ADAM_EPS) + ADAM_WD * w)
    return delta, m2, v2


def shard_adam(name, owns, recvs, w, m, v):
    n = len(owns)
    R = owns[0].shape[0]
    ct = min(o.shape[1] for o in owns)
    first = [sum(o.shape[1] for o in owns[:j]) // ct for j in range(n)]
    count = [o.shape[1] // ct for o in owns]
    nt = _row_tiles(R, ct)
    tr = R // nt

    def body(*refs):
        o_refs, r_refs = refs[:n], refs[n:2 * n]
        w_ref, m_ref, v_ref, g_ref, d_ref, nm_ref, nv_ref = refs[2 * n:]
        g = None
        for j in range(n):
            gj = o_refs[j][...]
            for q in range(recvs[j].shape[0]):
                gj = gj + r_refs[j][q].astype(F32)
            g = gj if g is None else jnp.where(pl.program_id(0) >= first[j], gj, g)
        delta, m2, v2 = _adamw(w_ref[...], g, m_ref[...], v_ref[...])
        g_ref[...] = g
        d_ref[...] = delta
        nm_ref[...] = m2
        nv_ref[...] = v2

    def part(j):
        return pl.BlockSpec((tr, ct), lambda k, i: (i, jnp.clip(k - first[j], 0, count[j] - 1)))

    def part3(j):
        return pl.BlockSpec((recvs[j].shape[0], tr, ct), lambda k, i: (0, i, jnp.clip(k - first[j], 0, count[j] - 1)))

    C = sum(count) * ct
    tile = pl.BlockSpec((tr, ct), lambda k, i: (i, k))
    return pl.pallas_call(
        body, name=name, grid=(sum(count), nt),
        in_specs=[part(j) for j in range(n)] + [part3(j) for j in range(n)] + [tile, tile, tile],
        out_specs=[tile] * 4, out_shape=[jax.ShapeDtypeStruct((R, C), F32)] * 4,
        compiler_params=_cp(dimension_semantics=("arbitrary", "arbitrary")),
    )(*owns, *recvs, w, m, v)


def block_adam(name, items):
    n = len(items)

    def body(*refs):
        for j, item in enumerate(items):
            o_ref, r_ref, w_ref, m_ref, v_ref = refs[5 * j:5 * j + 5]
            g = o_ref[...]
            for q in range(r_ref.shape[0]):
                g = g + r_ref[q].astype(F32)
            t = (lambda a: a.T) if item[5] else (lambda a: a)
            delta, m2, v2 = _adamw(t(w_ref[...]), g, t(m_ref[...]), t(v_ref[...]))
            for ref, val in zip(refs[5 * n + 4 * j:5 * n + 4 * j + 4], (g, delta, m2, v2)):
                ref[...] = t(val)

    args = [a for item in items for a in item[:5]]
    out_shape = [jax.ShapeDtypeStruct(item[2].shape, F32) for item in items for _ in range(4)]
    res = pl.pallas_call(
        body, name=name, grid=(1,), in_specs=[VMEM] * len(args), out_specs=[VMEM] * len(out_shape),
        out_shape=out_shape, compiler_params=_cp(dimension_semantics=("arbitrary",)))(*args)
    return [tuple(res[4 * j:4 * j + 4]) for j in range(n)]


ROW_N1, ROW_N2, ROW_BG, ROW_QN, ROW_KN, ROW_CB, ROW_LW, ROW_LB, ROW_CW = 0, 1, 2, 4, 5, 6, 7, 8, 9
PACK_ROWS = 40
SMALL = ("norm1_w", "norm2_w", "b_gate", "q_norm_w", "k_norm_w", "conv_b", "conv_ln_w", "conv_ln_b", "conv_w")


def small_sync(g, sq, sides=()):
    ns = len(SMALL)

    def copies(refs):
        pack, recv, send_sems, recv_sems = refs[ns + 2:]
        x, y, c, _ = _place()
        return [pltpu.make_async_remote_copy(
            src_ref=pack, dst_ref=recv.at[4 * x + 2 * y + c], send_sem=send_sems.at[k - 1],
            recv_sem=recv_sems.at[k - 1], device_id=(x ^ (k >> 2), y ^ ((k >> 1) & 1), c ^ (k & 1)),
            device_id_type=MESH) for k in range(1, NDEV)]

    def body(*refs):
        gi = dict(zip(SMALL, refs[:ns]))
        sq_ref, tot, pack, recv, send_sems, recv_sems = refs[ns:]
        x, y, c, _ = _place()
        me = 4 * x + 2 * y + c

        pack[...] = jnp.zeros_like(pack)
        pack[ROW_KN:ROW_KN + 1, LANES:2 * LANES] = jnp.full((1, LANES), (0.5 / D) * jnp.sum(sq_ref[...]), F32)
        pack[ROW_N1:ROW_N1 + 1, :] = gi["norm1_w"][...]
        pack[ROW_N2:ROW_N2 + 1, :] = gi["norm2_w"][...]
        pack[ROW_BG:ROW_BG + 2, :] = gi["b_gate"][...]
        pack[ROW_QN:ROW_QN + 1, 0:HD] = gi["q_norm_w"][...]
        pack[ROW_KN:ROW_KN + 1, 0:HD] = gi["k_norm_w"][...]
        pack[ROW_CB:ROW_CB + 1, 0:CC] = gi["conv_b"][...]
        pack[ROW_LW:ROW_LW + 1, 0:CC] = gi["conv_ln_w"][...]
        pack[ROW_LB:ROW_LB + 1, 0:CC] = gi["conv_ln_b"][...]
        pack[ROW_CW:ROW_CW + KW, 0:CC] = gi["conv_w"][...]

        for cp in copies(refs):
            cp.start()
        recv[me] = pack[...]

    def tail(*refs):
        tot, recv = refs[ns + 1], refs[ns + 3]
        for cp in copies(refs):
            cp.wait()
        acc = recv[0]
        for p in range(1, NDEV):
            acc = acc + recv[p]
        tot[...] = acc

    args = [g[k] for k in SMALL] + [sq]
    res = _call(
        body, sides, name="small_sync", grid=(1,), in_specs=[VMEM] * len(args), out_specs=[VMEM],
        out_shape=[jax.ShapeDtypeStruct((PACK_ROWS, D), F32)],
        scratch_shapes=[pltpu.VMEM((PACK_ROWS, D), F32), pltpu.VMEM((NDEV, PACK_ROWS, D), F32),
                        _sems(NDEV - 1), _sems(NDEV - 1)],
        args=args, own_comm=True, tail=tail)
    return (res[0][0], res[1]) if sides else res[0]


def small_adam(tot, w, m, v, me):
    ns = len(SMALL)

    def body(me_ref, tot, *refs):
        wi = dict(zip(SMALL, refs[:ns]))
        mi = dict(zip(SMALL, refs[ns:2 * ns]))
        vi = dict(zip(SMALL, refs[2 * ns:3 * ns]))
        outs = refs[3 * ns:7 * ns]
        loss_ref = refs[7 * ns]
        me = me_ref[0]

        def shard_grad(name):
            if name == "b_gate":
                return tot[ROW_BG:ROW_BG + 2, pl.ds(pl.multiple_of(me * LANES, LANES), LANES)]
            if name == "conv_w":
                win = tot[ROW_CW:ROW_CW + KW, pl.ds(pl.multiple_of((me // 2) * LANES, LANES), LANES)]
                return jnp.where(me % 2 == 1, win[:, HD:LANES], win[:, 0:HD])
            row = {"norm1_w": ROW_N1, "norm2_w": ROW_N2, "q_norm_w": ROW_QN, "k_norm_w": ROW_KN,
                   "conv_b": ROW_CB, "conv_ln_w": ROW_LW, "conv_ln_b": ROW_LB}[name]
            return tot[row:row + 1, 0:wi[name].shape[1]]

        for i, name in enumerate(SMALL):
            gr = shard_grad(name)
            delta, m2, v2 = _adamw(wi[name][...], gr, mi[name][...], vi[name][...])
            outs[4 * i][...] = gr
            outs[4 * i + 1][...] = delta
            outs[4 * i + 2][...] = m2
            outs[4 * i + 3][...] = v2
        loss_ref[...] = tot[ROW_KN:ROW_KN + 1, LANES:2 * LANES]

    out_shape = []
    for name in SMALL:
        out_shape += [jax.ShapeDtypeStruct(w[name].shape, F32)] * 4
    out_shape.append(jax.ShapeDtypeStruct((1, LANES), F32))
    args = [tot] + [w[k] for k in SMALL] + [m[k] for k in SMALL] + [v[k] for k in SMALL]
    grid_spec = pltpu.PrefetchScalarGridSpec(
        num_scalar_prefetch=1, grid=(1,), in_specs=[VMEM] * len(args), out_specs=[VMEM] * len(out_shape))
    res = pl.pallas_call(body, name="small_adam", grid_spec=grid_spec, out_shape=out_shape)(me, *args)
    out = {name: tuple(res[4 * i:4 * i + 4]) for i, name in enumerate(SMALL)}
    return out, res[4 * ns][0, 0]


MATS = ("w_in", "w_o_attn", "w_pw_conv", "w_out", "w_ffn_in", "w_ffn_out")
TRANSPOSED = ("w_in", "w_ffn_in")
WEIGHTS = ("norm1_w", "w_in", "b_gate", "q_norm_w", "k_norm_w", "w_o_attn", "conv_w", "conv_b", "conv_ln_w",
           "conv_ln_b", "w_pw_conv", "w_out", "norm2_w", "w_ffn_in", "w_ffn_out")


def _blocks_to_cols(blocks):
    n, R, C = blocks.shape
    return blocks.transpose(1, 0, 2).reshape(R, n * C)


def kernel(x, positions, norm1_w, w_in, b_gate, q_norm_w, k_norm_w, w_o_attn, conv_w, conv_b, conv_ln_w, conv_ln_b, w_pw_conv, w_out, norm2_w, w_ffn_in, w_ffn_out, loss_target, m_norm1_w, m_w_in, m_b_gate, m_q_norm_w, m_k_norm_w, m_w_o_attn, m_conv_w, m_conv_b, m_conv_ln_w, m_conv_ln_b, m_w_pw_conv, m_w_out, m_norm2_w, m_w_ffn_in, m_w_ffn_out, v_norm1_w, v_w_in, v_b_gate, v_q_norm_w, v_k_norm_w, v_w_o_attn, v_conv_w, v_conv_b, v_conv_ln_w, v_conv_ln_b, v_w_pw_conv, v_w_out, v_norm2_w, v_w_ffn_in, v_w_ffn_out):
    w = dict(norm1_w=norm1_w, w_in=w_in, b_gate=b_gate, q_norm_w=q_norm_w, k_norm_w=k_norm_w, w_o_attn=w_o_attn,
             conv_w=conv_w, conv_b=conv_b, conv_ln_w=conv_ln_w, conv_ln_b=conv_ln_b, w_pw_conv=w_pw_conv,
             w_out=w_out, norm2_w=norm2_w, w_ffn_in=w_ffn_in, w_ffn_out=w_ffn_out)
    m = dict(norm1_w=m_norm1_w, w_in=m_w_in, b_gate=m_b_gate, q_norm_w=m_q_norm_w, k_norm_w=m_k_norm_w,
             w_o_attn=m_w_o_attn, conv_w=m_conv_w, conv_b=m_conv_b, conv_ln_w=m_conv_ln_w,
             conv_ln_b=m_conv_ln_b, w_pw_conv=m_w_pw_conv, w_out=m_w_out, norm2_w=m_norm2_w,
             w_ffn_in=m_w_ffn_in, w_ffn_out=m_w_ffn_out)
    v = dict(norm1_w=v_norm1_w, w_in=v_w_in, b_gate=v_b_gate, q_norm_w=v_q_norm_w, k_norm_w=v_k_norm_w,
             w_o_attn=v_w_o_attn, conv_w=v_conv_w, conv_b=v_conv_b, conv_ln_w=v_conv_ln_w,
             conv_ln_b=v_conv_ln_b, w_pw_conv=v_w_pw_conv, w_out=v_w_out, norm2_w=v_norm2_w,
             w_ffn_in=v_w_ffn_in, w_ffn_out=v_w_ffn_out)
    def two_d(t):
        t = {k: (a[0] if a.ndim == 3 else a) for k, a in t.items()}
        return {k: (a.T if k in TRANSPOSED else a) for k, a in t.items()}

    w, m, v = two_d(w), two_d(m), two_d(v)

    x2, target = x[0], loss_target[0]
    c_idx = lax.axis_index("c").astype(jnp.int32)
    chip_idx = (2 * lax.axis_index("x") + lax.axis_index("y")).astype(jnp.int32)
    qw2 = jnp.tile(w["q_norm_w"], (1, 2))
    kw2 = jnp.tile(w["k_norm_w"], (1, 2))

    ax, ay = lax.axis_index("x"), lax.axis_index("y")
    chip_order = jnp.stack([2 * ax + ay, 2 * (1 - ax) + ay, 2 * ax + 1 - ay, 2 * (1 - ax) + 1 - ay]).astype(jnp.int32)
    h, proj, w_in_blocks, tabs = in_proj_gather(x2, w["norm1_w"], w["w_in"], chip_order, positions.reshape(S, 1))
    w_in_t = w_in_blocks.reshape(INW, D)
    (attn, lse), ((w_ffn_in_blocks,), (w_out_blocks,), (w_o_blocks,), (w_pw_blocks,), (bg_blocks,), (cw_blocks,)) = attn_fwd(
        proj, tabs, qw2, kw2, sides=(ag_blocks_relay(w["w_ffn_in"], BF16), ag_blocks_relay(w["w_out"], BF16),
                                     ag_blocks_relay(w["w_o_attn"], BF16, transpose=True),
                                     ag_blocks_relay(w["w_pw_conv"], BF16, transpose=True),
                                     ag_blocks(w["b_gate"], F32), ag_blocks(w["conv_w"], F32)))
    w_ffn_in_t = w_ffn_in_blocks.reshape(2 * FF, D)
    w_out_f = w_out_blocks.reshape(D, D)
    w_o_t, w_pw_t = w_o_blocks.reshape(D, CC), w_pw_blocks.reshape(D, CC)
    b_gate_f, conv_w_f = _blocks_to_cols(bg_blocks), _blocks_to_cols(cw_blocks)
    cpre, u3 = conv_fwd(proj, conv_w_f, w["conv_b"], w["conv_ln_w"], w["conv_ln_b"])
    x1, z, ya, yb = mix_out(x2, proj, b_gate_f, attn, u3, w_o_t, w_pw_t, w_out_f)
    (h2, gu, f), ((w_ffn_out_blocks,),) = ffn_in(x1, w["norm2_w"], w_ffn_in_t, sides=(ag_blocks_relay(w["w_ffn_out"], BF16),))
    w_ffn_out_f = w_ffn_out_blocks.reshape(FF, D)
    dy, dyb, sq = ffn_out_loss(x1, f, w_ffn_out_f, target)

    g = {}
    def blocks(name, pairs, tm):
        return [t.reshape(NDEV, t.shape[0] // NDEV, t.shape[1]) for t in mm_tn(name, pairs, tm)]

    g_ffn_out, gb_ffn_out = blocks("gw_ffn_out", [(f, dyb)], FF // 2)
    (d_gu, d_x1, d_x1b, g["norm2_w"]), ((ra_ffn_out,),) = ffn_bwd(
        dy, dyb, gu, x1, w["norm2_w"], w_ffn_in_t, w_ffn_out_f, sides=(rs_to_sibling([gb_ffn_out]),))
    g_ffn_in, gb_ffn_in = blocks("gw_ffn_in", [(d_gu, h2)], FF // 2)
    (d_ya, d_yb, d_gl, d_attn, d_u3, g["b_gate"]), ((ra_ffn_in,),) = out_bwd(
        d_x1b, proj, b_gate_f, ya, yb, w_o_t, w_pw_t, w_out_f, sides=(rs_to_sibling([gb_ffn_in]),))
    g_out, gb_out, g_w_o, gb_w_o, g_w_pw, gb_w_pw = blocks(
        "gw_out_o_pw", [(z, d_x1b), (d_ya, attn), (d_yb, u3)], D // 2)
    (d_conv, g["conv_w"], g["conv_b"], g["conv_ln_w"], g["conv_ln_b"]), ((ra_out, ra_w_o, ra_w_pw),) = conv_bwd(
        proj, cpre, d_u3, conv_w_f, conv_w_f[::-1], w["conv_ln_w"], w["conv_ln_b"],
        sides=(rs_to_sibling([gb_out, gb_w_o, gb_w_pw]),))
    (pb_ffn_out, own_ffn_out), (pb_ffn_in, own_ffn_in), (pb_out, own_out), (pb_w_o, own_w_o), (pb_w_pw, own_w_pw) = chip_sum(
        "chip_sum_early", [g_ffn_out, g_ffn_in, g_out, g_w_o, g_w_pw],
        [ra_ffn_out, ra_ffn_in, ra_out, ra_w_o, ra_w_pw], c_idx, chip_idx)
    (d_q, d_k, d_v, gqw, gkw), ((rb_ffn_out, rb_ffn_in, rb_out, rb_w_o, rb_w_pw),) = attn_bwd(
        proj, tabs, qw2, kw2, d_attn, attn, lse,
        sides=(rs_to_chips([pb_ffn_out, pb_ffn_in, pb_out, pb_w_o, pb_w_pw]),))
    g["q_norm_w"] = gqw[0:1, 0:HD] + gqw[0:1, HD:LANES]
    g["k_norm_w"] = gkw[0:1, 0:HD] + gkw[0:1, HD:LANES]
    d_segs = (d_q, d_k, d_v, d_conv, d_gl)
    parts, to_sibling, to_chips, owns, from_chips = [], None, None, [], []
    for k, hw in enumerate(GW_IN_SPLIT):
        sides = tuple(s for s in (to_chips, to_sibling) if s is not None)
        (part, part_b), outs = gw_in_t("gw_in_%d" % k, h, d_segs, sum(GW_IN_SPLIT[:k]), hw, sides=sides)
        outs = list(outs)
        if to_chips is not None:
            from_chips.append(outs.pop(0)[0])
        if to_sibling is not None:
            (pb, own), = chip_sum("chip_sum_w_in_%d" % (k - 1), [parts[-1]], [outs.pop(0)[0]], c_idx, chip_idx)
            owns.append(own)
            to_chips = rs_to_chips_combined(pb)
        else:
            to_chips = None
        parts.append(part.reshape(NDEV, INW // NDEV, hw))
        to_sibling = rs_to_sibling([part_b.reshape(NDEV, INW // NDEV, hw)])
    (grad_x, g["norm1_w"]), ((rb_prev,), (ra_last,)) = in_bwd(
        d_q, d_k, d_v, d_conv, d_gl, w_in_t, x2, d_x1, w["norm1_w"], sides=(to_chips, to_sibling))
    from_chips.append(rb_prev)
    (pb, own), = chip_sum("chip_sum_w_in_%d" % (len(GW_IN_SPLIT) - 1), [parts[-1]], [ra_last], c_idx, chip_idx)
    owns.append(own)
    small_sums, ((rb_last,),) = small_sync(g, sq, sides=(rs_to_chips_combined(pb),))
    small, loss = small_adam(small_sums, w, m, v, (4 * ax + 2 * ay + c_idx).astype(jnp.int32).reshape(1))
    from_chips.append(rb_last)

    adam_o, adam_pw, adam_out = block_adam("adam_w_o_pw_out", [
        (own_w_o, rb_w_o, w["w_o_attn"], m["w_o_attn"], v["w_o_attn"], True),
        (own_w_pw, rb_w_pw, w["w_pw_conv"], m["w_pw_conv"], v["w_pw_conv"], True),
        (own_out, rb_out, w["w_out"], m["w_out"], v["w_out"], False)])
    res = {
        "w_in": shard_adam("adam_w_in", owns, from_chips, w["w_in"], m["w_in"], v["w_in"]),
        "w_ffn_in": shard_adam("adam_w_ffn_in", [own_ffn_in], [rb_ffn_in], w["w_ffn_in"], m["w_ffn_in"], v["w_ffn_in"]),
        "w_o_attn": adam_o, "w_pw_conv": adam_pw, "w_out": adam_out,
        "w_ffn_out": shard_adam("adam_w_ffn_out", [own_ffn_out], [rb_ffn_out],
                                w["w_ffn_out"], m["w_ffn_out"], v["w_ffn_out"]),
    }
    res = {k: tuple(a.T if k in TRANSPOSED else a for a in r) for k, r in res.items()}
    res.update(small)

    def shaped(name, a):
        return a.reshape((1,) + a.shape) if name in MATS or name in ("b_gate", "conv_w") else a

    outs = [loss, grad_x.reshape(1, S, D)]
    for i in range(4):
        outs += [shaped(k, res[k][i]) for k in WEIGHTS]
    return tuple(outs)
```

```python
import functools
from typing import Callable, NamedTuple, Optional

import numpy as np
import jax
import jax.numpy as jnp
from jax import lax
from jax.experimental import pallas as pl
from jax.experimental.pallas import tpu as pltpu

F32 = jnp.float32
BF16 = jnp.bfloat16

S = 2048
D = 1024
HD = 64
QKV = 1536
CC = 512
KW = 31
FF = 2816
INW = 7680
OFF_Q, OFF_K, OFF_V, OFF_CA, OFF_CB, OFF_GA, OFF_GB = 0, 1536, 3072, 4608, 5120, 5632, 6656
DILATIONS = (1, 4, 16)
HALF_SPAN = 64
EPS = 1e-6
NEG_INF = -1e30
ROPE_THETA = 500000.0
ROT_DIM = 16

ADAM_LR = 0.001
ADAM_B1 = 0.9
ADAM_B2 = 0.999
ADAM_EPS = 1e-08
ADAM_WD = 0.01
ADAM_STEP = 10

NDEV = 8
LANES = 128
TM = 256
IN_PROJ_TM = 512
TQ = 128
VMEM_LIMIT = 56 * 1024 * 1024
MESH = pl.DeviceIdType.MESH


def _cp(**kw):
    return pltpu.CompilerParams(vmem_limit_bytes=VMEM_LIMIT, **kw)


def _row(width, col=0, tm=TM):
    return pl.BlockSpec((tm, width), lambda i: (i, col))


PLANE = 512


def _planes(width, tm=TM):
    return pl.BlockSpec((width // PLANE, tm, PLANE), lambda i: (0, i, 0))


def _res(shape):
    nd = len(shape)
    return pl.BlockSpec(shape, lambda *_: (0,) * nd, pipeline_mode=pl.Buffered(1))


def _dot(a, b):
    return jnp.dot(a, b, preferred_element_type=F32)


def _dot_nt(a, b):
    return lax.dot_general(a, b, (((1,), (1,)), ((), ())), preferred_element_type=F32)


def _dot_tn(a, b):
    return lax.dot_general(a, b, (((0,), (0,)), ((), ())), preferred_element_type=F32)


def _sigmoid(x):
    return jax.nn.sigmoid(x)


def _dsilu(x, sg):
    return sg * (1.0 + x * (1.0 - sg))


ANY = pl.BlockSpec(memory_space=pl.ANY)
VMEM = pl.BlockSpec(memory_space=pltpu.VMEM)


class Side(NamedTuple):
    args: tuple
    in_specs: tuple
    out_shape: tuple
    scratch: tuple
    start: Callable
    finish: Callable
    mid: Optional[Callable] = None
    peers: str = ""


BARRIER_IDS = {"s": 0, "dxy": 1, "dsxy": 2, "sxy": 3, "xy": 4}


def _peer_barrier(peers):
    x, y, c = lax.axis_index("x"), lax.axis_index("y"), lax.axis_index("c")
    where = {"s": (x, y, 1 - c), "x": (1 - x, y, c), "y": (x, 1 - y, c), "d": (1 - x, 1 - y, c)}
    barrier = pltpu.get_barrier_semaphore()
    for p in peers:
        pl.semaphore_signal(barrier, inc=1, device_id=where[p], device_id_type=MESH)
    pl.semaphore_wait(barrier, len(peers))


def _call(body, sides=(), *, name, grid, in_specs, out_specs, out_shape, scratch_shapes=(), args, own_comm=False,
          tail=None):
    assert tail is None or int(np.prod(grid)) == 1
    ni, no, ns = len(in_specs), len(out_specs), len(scratch_shapes)
    cnt = [(len(s.args), len(s.out_shape), len(s.scratch)) for s in sides]
    peers = "".join(sorted(set("".join(s.peers for s in sides))))
    if own_comm or not sides or any(not s.peers for s in sides):
        peers = ""

    def take(refs, pos, n):
        return refs[pos:pos + n], pos + n

    def full(*refs):
        m_in, pos = take(refs, 0, ni)
        s_in = []
        for a, _, _ in cnt:
            r, pos = take(refs, pos, a)
            s_in.append(r)
        m_out, pos = take(refs, pos, no)
        s_out = []
        for _, o, _ in cnt:
            r, pos = take(refs, pos, o)
            s_out.append(r)
        m_scr, pos = take(refs, pos, ns)
        s_scr = []
        for _, _, c in cnt:
            r, pos = take(refs, pos, c)
            s_scr.append(r)
        if sides:
            first = functools.reduce(jnp.logical_and, [pl.program_id(d) == 0 for d in range(len(grid))])
            last = functools.reduce(jnp.logical_and, [pl.program_id(d) == g - 1 for d, g in enumerate(grid)])

            @pl.when(first)
            def _():
                if peers:
                    _peer_barrier(peers)
                for s, a, o, c in zip(sides, s_in, s_out, s_scr):
                    s.start(a, o, c)

            steps = int(np.prod(grid))
            mid_step = (2 * steps) // 3
            if steps > 1 and any(s.mid is not None for s in sides):
                step = functools.reduce(lambda acc, d: acc * grid[d] + pl.program_id(d), range(len(grid)), 0)

                @pl.when(step == mid_step)
                def _():
                    for s, a, o, c in zip(sides, s_in, s_out, s_scr):
                        if s.mid is not None:
                            s.mid(a, o, c)

        body(*m_in, *m_out, *m_scr)
        if sides:
            @pl.when(last)
            def _():
                for s, a, o, c in zip(sides, s_in, s_out, s_scr):
                    if s.mid is not None and steps == 1:
                        s.mid(a, o, c)
                if tail is not None:
                    tail(*m_in, *m_out, *m_scr)
                for s, a, o, c in zip(sides, s_in, s_out, s_scr):
                    s.finish(a, o, c)
        elif tail is not None:
            tail(*m_in, *m_out, *m_scr)

    res = pl.pallas_call(
        full, name=name, grid=grid,
        in_specs=list(in_specs) + [sp for s in sides for sp in s.in_specs],
        out_specs=list(out_specs) + [ANY for s in sides for _ in s.out_shape],
        out_shape=list(out_shape) + [o for s in sides for o in s.out_shape],
        scratch_shapes=list(scratch_shapes) + [c for s in sides for c in s.scratch],
        compiler_params=_cp(dimension_semantics=("arbitrary",) * len(grid),
                            **({"collective_id": BARRIER_IDS[peers]} if peers else {})),
    )(*args, *[a for s in sides for a in s.args])
    res = list(res)
    if not sides:
        return res
    outs, pos = take(res, 0, no)
    side_outs = []
    for _, o, _ in cnt:
        r, pos = take(res, pos, o)
        side_outs.append(r)
    return outs, side_outs


def _inv_freq_lanes():
    inv = np.float32(ROPE_THETA) ** (-np.arange(0, ROT_DIM, 2, dtype=np.float32) / np.float32(ROT_DIM))
    lane = np.arange(LANES) % HD
    out = np.where(lane < ROT_DIM, inv[lane % (ROT_DIM // 2)], 0.0).astype(np.float32)
    return jnp.asarray(out.reshape(1, LANES))


def _rope_tables(pos, inv_freq):
    ang = pos.astype(F32) * inv_freq
    lane = lax.broadcasted_iota(jnp.int32, ang.shape, 1) % HD
    cs = jnp.cos(ang)
    sn = jnp.sin(ang)
    return (jnp.where(lane < ROT_DIM, cs, 1.0), jnp.where(lane < ROT_DIM // 2, -sn, 0.0),
            jnp.where(lane < ROT_DIM // 2, 0.0, jnp.where(lane < ROT_DIM, sn, 0.0)))


def _rope(v, c, s1, s2):
    return v * c + pltpu.roll(v, LANES - 8, axis=1) * s1 + pltpu.roll(v, 8, axis=1) * s2


def _rope_t(d, c, s1, s2):
    return d * c - pltpu.roll(d, LANES - 8, axis=1) * s1 - pltpu.roll(d, 8, axis=1) * s2


def _head_mat():
    r = lax.broadcasted_iota(jnp.int32, (LANES, LANES), 0) // HD
    c = lax.broadcasted_iota(jnp.int32, (LANES, LANES), 1) // HD
    return jnp.where(r == c, 1.0 / HD, 0.0).astype(BF16)


def _head_mean(t, e):
    hi = t.astype(BF16)
    rest = (t - hi.astype(F32)).astype(BF16)
    return _dot(hi, e) + _dot(rest, e)


def in_proj_gather(x, norm_w, shard_t, chip_order, pos_col):
    R = INW // NDEV
    tm = IN_PROJ_TM
    half, nt = R // 2, S // tm

    def body(ord_ref, x_ref, nw_ref, sh_ref, pos_ref, f_ref, h_ref, p_ref, wfull_ref, c_ref, s1_ref, s2_ref,
             wt, hs, send, recv, loc):
        kk, i = pl.program_id(0), pl.program_id(1)
        x, y, c, _ = _place()
        me, flip = 4 * x + 2 * y + c, 1 - 2 * c
        here, sib, xn, yn = (x, y, c), (x, y, 1 - c), (1 - x, y, c), (x, 1 - y, c)
        b_xn, b_yn, b_dg = 4 * (1 - x) + 2 * y + c, 4 * x + 2 * (1 - y) + c, 4 * (1 - x) + 2 * (1 - y) + c

        def cp(k, block, to, rows=None):
            dst = wt.at[block] if rows is None else wt.at[block, pl.ds(rows * half, half), :]
            return _remote(dst, dst, send, recv, k, to)

        def sends():
            return [cp(0, me, sib), cp(1, me, xn), cp(2, me, yn), cp(3, b_xn, sib), cp(4, b_yn, sib),
                    cp(5, b_xn, yn, rows=0), cp(6, b_yn, xn, rows=1), cp(7, b_dg, sib, rows=0), cp(8, b_dg, sib, rows=1)]

        def keep(j, blk0):
            pair = pl.ds(pl.multiple_of(blk0, 2), 2)
            return pltpu.make_async_copy(wt.at[pair], wfull_ref.at[pair], loc.at[j])

        @pl.when((kk == 0) & (i == 0))
        def _():
            _peer_barrier("sxy")
            _cast_rows(wt.at[me], sh_ref)
            for s_ in sends()[0:3]:
                s_.start()

            def tables(j, _):
                chunk = pl.ds(pl.multiple_of(j * TM, TM), TM)
                c_ref[chunk, :], s1_ref[chunk, :], s2_ref[chunk, :] = _rope_tables(pos_ref[chunk, :], f_ref[...])
                return 0

            lax.fori_loop(0, S // TM, tables, 0)
            cp(0, me + flip, here).wait_recv()
            keep(0, me - c).start()

        @pl.when((kk == 1) & (i == 0))
        def _():
            cp(1, b_xn, here).wait_recv()
            sends()[5].start()
            sends()[3].start()
            cp(2, b_yn, here).wait_recv()
            sends()[6].start()
            sends()[4].start()
            cp(3, b_xn + flip, here).wait_recv()
            keep(1, b_xn - c).start()

        @pl.when((kk == 2) & (i == 0))
        def _():
            cp(4, b_yn + flip, here).wait_recv()
            keep(2, b_yn - c).start()

        @pl.when((kk == 3) & (i == 0))
        def _():
            cp(5, b_dg, here, rows=0).wait_recv()
            sends()[7].start()
            cp(6, b_dg, here, rows=1).wait_recv()
            sends()[8].start()
            cp(7, b_dg + flip, here, rows=0).wait_recv()
            cp(8, b_dg + flip, here, rows=1).wait_recv()
            keep(3, b_dg - c).start()

        rows = pl.ds(pl.multiple_of(i * tm, tm), tm)

        @pl.when(kk == 0)
        def _():
            xv = x_ref[...]
            r = lax.rsqrt(jnp.mean(xv * xv, axis=-1, keepdims=True) + EPS)
            hb = (xv * r * nw_ref[...]).astype(BF16)
            h_ref[...] = hb
            hs[rows, :] = hb

        h = hs[rows, :]
        chip = ord_ref[kk]
        for cc in range(2):
            p_ref[:, cc * R:(cc + 1) * R] = _dot_nt(h, wt[2 * chip + cc])

        @pl.when((kk == 3) & (i == nt - 1))
        def _():
            for s_ in sends():
                s_.wait_send()
            for j, blk in enumerate((me, b_xn, b_yn, b_dg)):
                keep(j, blk - c).wait()

    def first_pass(kk, i):
        return jnp.where(kk == 0, i, nt - 1)

    grid_spec = pltpu.PrefetchScalarGridSpec(
        num_scalar_prefetch=1, grid=(4, nt),
        in_specs=[pl.BlockSpec((tm, D), lambda kk, i, o: (first_pass(kk, i), 0)),
                  pl.BlockSpec((1, D), lambda kk, i, o: (0, 0)), VMEM, VMEM,
                  pl.BlockSpec((1, LANES), lambda kk, i, o: (0, 0))],
        out_specs=[pl.BlockSpec((tm, D), lambda kk, i, o: (first_pass(kk, i), 0)),
                   pl.BlockSpec((tm, 2 * R), lambda kk, i, o: (i, o[kk])), ANY]
        + [pl.BlockSpec((S, LANES), lambda kk, i, o: (0, 0))] * 3,
        scratch_shapes=[pltpu.VMEM((NDEV, R, D), BF16), pltpu.VMEM((S, D), BF16), _sems(9), _sems(9), _sems(4)])
    res = pl.pallas_call(
        body, name="in_proj_gather", grid_spec=grid_spec,
        out_shape=[jax.ShapeDtypeStruct((S, D), BF16), jax.ShapeDtypeStruct((S, INW), F32),
                   jax.ShapeDtypeStruct((NDEV, R, D), BF16)] + [jax.ShapeDtypeStruct((S, LANES), F32)] * 3,
        compiler_params=_cp(dimension_semantics=("arbitrary", "arbitrary"), collective_id=BARRIER_IDS["sxy"]),
    )(chip_order, x, norm_w, shard_t, pos_col, _inv_freq_lanes())
    return res[0], res[1], res[2], tuple(res[3:])


def _qk_specs():
    nb = QKV // LANES
    return [pl.BlockSpec((S, LANES), functools.partial(lambda hp, g, o: (0, o + g * 4 + hp), o=o))
            for o in (OFF_Q // LANES, OFF_K // LANES, OFF_V // LANES)]


def _tab_specs():
    return [pl.BlockSpec((S, LANES), lambda hp, g: (0, 0), pipeline_mode=pl.Buffered(1))] * 3


def _vec_spec():
    return pl.BlockSpec((1, LANES), lambda hp, g: (0, 0))


def _sub_rows(r, d, start, n):
    if d == 1:
        return pl.ds(start, n)
    return pl.ds(r + d * start, n, stride=d)


def _band_window(i, L):
    W = min(TQ + 2 * HALF_SPAN, L)
    q0 = pl.multiple_of(i * TQ, TQ)
    k0 = pl.multiple_of(jnp.clip(q0 - HALF_SPAN, 0, L - W), HALF_SPAN)
    qpos = q0 + (lax.broadcasted_iota(jnp.int32, (2 * TQ, W), 0) & (TQ - 1))
    kpos = k0 + lax.broadcasted_iota(jnp.int32, (2 * TQ, W), 1)
    valid = jnp.abs(qpos - kpos) <= HALF_SPAN
    return W, q0, k0, valid


def _stack_heads(t, lo):
    z = jnp.zeros_like(t)
    return jnp.concatenate([jnp.where(lo, t, z), jnp.where(lo, z, t)], axis=0)


def _unstack_heads(t2, lo):
    return jnp.where(lo, t2[0:TQ], t2[TQ:2 * TQ])


CHAINS = 8


def _interleave(d):
    ru = min(d, CHAINS)
    return ru, min(CHAINS // ru, S // d // TQ)


def _for_blocks(n, fn):
    if n == 1:
        fn(0)
    else:
        def it(j, _):
            fn(j)
            return 0
        lax.fori_loop(0, n, it, 0)


def attn_fwd(proj, tabs, qw2, kw2, sides=()):
    CH = 256

    def body(q_ref, k_ref, v_ref, c_ref, s1_ref, s2_ref, qw_ref, kw_ref, at_ref, ls_ref,
             qs, ks, vs, osub, lsub, onat, lnat, qn, kn):
        g = pl.program_id(1)
        lo = lax.broadcasted_iota(jnp.int32, (1, LANES), 1) < HD
        e = _head_mat()

        def prep(i, _):
            rows = pl.ds(pl.multiple_of(i * CH, CH), CH)
            c, s1, s2 = c_ref[rows, :], s1_ref[rows, :], s2_ref[rows, :]
            for t_ref, w_ref, out, scale in ((q_ref, qw_ref, qn, HD ** -0.5), (k_ref, kw_ref, kn, 1.0)):
                t = t_ref[rows, :]
                r = lax.rsqrt(_head_mean(t * t, e) + EPS)
                out[rows, :] = _rope(t * r * w_ref[...], c, s1, s2) * scale
            return 0

        lax.fori_loop(0, S // CH, prep, 0, unroll=4)

        def group(gi, d):
            L = S // d

            ru, nb = _interleave(d)

            def stage(r, off):
                for c0 in range(0, L, CH):
                    n = min(CH, L)
                    rows = _sub_rows(r, d, c0, n)
                    dst = pl.ds(off + c0, n)
                    qs[dst, :] = qn[rows, :].astype(BF16)
                    ks[dst, :] = kn[rows, :].astype(BF16)
                    vs[dst, :] = v_ref[rows, :].astype(BF16)

            def one(off, i):
                W, q0, k0, valid = _band_window(i, L)
                q2 = _stack_heads(qs[pl.ds(off + q0, TQ), :], lo)
                sc = jnp.where(valid, _dot_nt(q2, ks[pl.ds(off + k0, W), :]), NEG_INF)
                m = jnp.max(sc, axis=-1, keepdims=True)
                p = jnp.exp(sc - m)
                den = jnp.sum(p, axis=-1, keepdims=True)
                o2 = _dot(p.astype(BF16), vs[pl.ds(off + k0, W), :]) / den
                l2 = jnp.broadcast_to(m + jnp.log(den), (2 * TQ, LANES))
                osub[pl.ds(off + q0, TQ), :] = _unstack_heads(o2, lo)
                lsub[pl.ds(off + q0, TQ), :] = _unstack_heads(l2, lo)

            def unstage(r, off):
                for c0 in range(0, L, CH):
                    n = min(CH, L)
                    rows = _sub_rows(r, d, c0, n)
                    onat[gi, rows, :] = osub[pl.ds(off + c0, n), :]
                    lnat[gi, rows, :] = lsub[pl.ds(off + c0, n), :]

            def step(t, _):
                for u in range(ru):
                    stage(t * ru + u, u * L)
                _for_blocks(L // TQ // nb, lambda j: [one(u * L, j * nb + b) for u in range(ru) for b in range(nb)])
                for u in range(ru):
                    unstage(t * ru + u, u * L)
                return 0

            lax.fori_loop(0, d // ru, step, 0)

        for gi, d in enumerate(DILATIONS):
            pl.when(g == gi)(functools.partial(group, gi, d))

        @pl.when(g == len(DILATIONS) - 1)
        def _():
            def mix(i, _):
                rows = pl.ds(pl.multiple_of(i * CH, CH), CH)
                l0, l1, l2 = lnat[0, rows, :], lnat[1, rows, :], lnat[2, rows, :]
                m = jnp.maximum(jnp.maximum(l0, l1), l2)
                e0, e1, e2 = jnp.exp(l0 - m), jnp.exp(l1 - m), jnp.exp(l2 - m)
                den = e0 + e1 + e2
                a = (e0 * onat[0, rows, :] + e1 * onat[1, rows, :] + e2 * onat[2, rows, :]) / den
                at_ref[rows, :] = a.astype(BF16)
                ls_ref[rows, :] = m + jnp.log(den)
                return 0

            lax.fori_loop(0, S // CH, mix, 0)

    out_spec = pl.BlockSpec((S, LANES), lambda hp, g: (0, hp))
    return _call(
        body, sides, name="attn_fwd", grid=(4, 3),
        in_specs=_qk_specs() + _tab_specs() + [_vec_spec(), _vec_spec()],
        out_specs=[out_spec, out_spec],
        out_shape=[jax.ShapeDtypeStruct((S, CC), BF16), jax.ShapeDtypeStruct((S, CC), F32)],
        scratch_shapes=[pltpu.VMEM((S, LANES), BF16)] * 3 + [pltpu.VMEM((S, LANES), F32)] * 2
        + [pltpu.VMEM((3, S, LANES), F32)] * 2 + [pltpu.VMEM((S, LANES), F32)] * 2,
        args=(proj, proj, proj, *tabs, qw2, kw2))


def attn_bwd(proj, tabs, qw2, kw2, d_attn, attn, lse, sides=()):
    CH = 256

    def body(q_ref, k_ref, v_ref, c_ref, s1_ref, s2_ref, qw_ref, kw_ref, do_ref, at_ref, ls_ref,
             dq_ref, dk_ref, dv_ref, gqw_ref, gkw_ref,
             qs, ks, vs, dos, dsub, lsub, dqs, dks, dvs, dnat, qx, kx, dvn, tnq, tnk, rrq, rrk):
        hp, g = pl.program_id(0), pl.program_id(1)
        lo = lax.broadcasted_iota(jnp.int32, (1, LANES), 1) < HD
        e = _head_mat()
        both = ((q_ref, qw_ref, qx, tnq, rrq, HD ** -0.5), (k_ref, kw_ref, kx, tnk, rrk, 1.0))

        @pl.when((hp == 0) & (g == 0))
        def _():
            gqw_ref[...] = jnp.zeros_like(gqw_ref)
            gkw_ref[...] = jnp.zeros_like(gkw_ref)

        def prep(i, _):
            rows = pl.ds(pl.multiple_of(i * CH, CH), CH)
            dnat[rows, :] = _head_mean(do_ref[rows, :] * at_ref[rows, :].astype(F32), e) * float(HD)
            c, s1, s2 = c_ref[rows, :], s1_ref[rows, :], s2_ref[rows, :]
            for t_ref, w_ref, x, tn_s, rr_s, scale in both:
                t = t_ref[rows, :]
                rr = lax.rsqrt(_head_mean(t * t, e) + EPS)
                tn = t * rr
                rr_s[rows, :] = rr
                tn_s[rows, :] = tn
                x[rows, :] = _rope(tn * w_ref[...], c, s1, s2) * scale
            return 0

        lax.fori_loop(0, S // CH, prep, 0, unroll=4)

        def group(d):
            L = S // d

            ru, nb = _interleave(d)

            def stage(r, off):
                for c0 in range(0, L, CH):
                    n = min(CH, L)
                    rows = _sub_rows(r, d, c0, n)
                    dst = pl.ds(off + c0, n)
                    qs[dst, :] = qx[rows, :].astype(BF16)
                    ks[dst, :] = kx[rows, :].astype(BF16)
                    vs[dst, :] = v_ref[rows, :].astype(BF16)
                    dos[dst, :] = do_ref[rows, :].astype(BF16)
                    dsub[dst, :] = dnat[rows, :]
                    lsub[dst, :] = ls_ref[rows, :]
                    dks[dst, :] = jnp.zeros((n, LANES), F32)
                    dvs[dst, :] = jnp.zeros((n, LANES), F32)

            def one(off, i):
                W, q0, k0, valid = _band_window(i, L)
                qrows, krows = pl.ds(off + q0, TQ), pl.ds(off + k0, W)
                q2 = _stack_heads(qs[qrows, :], lo)
                do2 = _stack_heads(dos[qrows, :], lo)
                kk, vv = ks[krows, :], vs[krows, :]
                lse_b, dd_b = lsub[qrows, :], dsub[qrows, :]
                lse2 = jnp.concatenate([lse_b[:, 0:1], lse_b[:, HD:HD + 1]], axis=0)
                dd2 = jnp.concatenate([dd_b[:, 0:1], dd_b[:, HD:HD + 1]], axis=0)
                sc = jnp.where(valid, _dot_nt(q2, kk), NEG_INF)
                p = jnp.exp(sc - lse2)
                ds = (p * (_dot_nt(do2, vv) - dd2)).astype(BF16)
                dqs[qrows, :] = _unstack_heads(_dot(ds, kk), lo)
                dks[krows, :] = dks[krows, :] + _dot_tn(ds, q2)
                dvs[krows, :] = dvs[krows, :] + _dot_tn(p.astype(BF16), do2)

            def unstage(r, off):
                for c0 in range(0, L, CH):
                    n = min(CH, L)
                    rows = _sub_rows(r, d, c0, n)
                    src = pl.ds(off + c0, n)
                    qx[rows, :] = dqs[src, :]
                    kx[rows, :] = dks[src, :]
                    dvn[rows, :] = dvs[src, :]

            def step(t, _):
                for u in range(ru):
                    stage(t * ru + u, u * L)
                _for_blocks(L // TQ // nb, lambda j: [one(u * L, j * nb + b) for u in range(ru) for b in range(nb)])
                for u in range(ru):
                    unstage(t * ru + u, u * L)
                return 0

            lax.fori_loop(0, d // ru, step, 0)

        for gi, d in enumerate(DILATIONS):
            pl.when(g == gi)(functools.partial(group, d))

        def emit(i, _):
            rows = pl.ds(pl.multiple_of(i * CH, CH), CH)
            c, s1, s2 = c_ref[rows, :], s1_ref[rows, :], s2_ref[rows, :]
            for (_, w_ref, x, tn_s, rr_s, scale), out, gw_ref in zip(both, (dq_ref, dk_ref), (gqw_ref, gkw_ref)):
                tn = tn_s[rows, :]
                dy = _rope_t(x[rows, :] * scale, c, s1, s2)
                gw_ref[0:1, :] = gw_ref[0:1, :] + jnp.sum(dy * tn, axis=0, keepdims=True)
                dtn = dy * w_ref[...]
                out[rows, :] = (rr_s[rows, :] * (dtn - tn * _head_mean(dtn * tn, e))).astype(BF16)
            dv_ref[rows, :] = dvn[rows, :].astype(BF16)
            return 0

        lax.fori_loop(0, S // CH, emit, 0, unroll=4)

    nat_spec = pl.BlockSpec((S, LANES), lambda hp, g: (0, hp))
    out_spec = pl.BlockSpec((None, S, LANES), lambda hp, g: (g, 0, hp))
    acc_spec = pl.BlockSpec((8, LANES), lambda hp, g: (0, 0))
    return _call(
        body, sides, name="attn_bwd", grid=(4, 3),
        in_specs=_qk_specs() + _tab_specs() + [_vec_spec(), _vec_spec(), nat_spec, nat_spec, nat_spec],
        out_specs=[out_spec] * 3 + [acc_spec] * 2,
        out_shape=[jax.ShapeDtypeStruct((QKV // PLANE, S, PLANE), BF16)] * 3 + [jax.ShapeDtypeStruct((8, LANES), F32)] * 2,
        scratch_shapes=[pltpu.VMEM((S, LANES), BF16)] * 4 + [pltpu.VMEM((S, LANES), F32)] * 13,
        args=(proj, proj, proj, *tabs, qw2, kw2, d_attn, attn, lse))


PADR = 16
CT = 128


def _conv_specs():
    return [pl.BlockSpec((S, CC), lambda i: (0, OFF_CA // CC)), pl.BlockSpec((S, CC), lambda i: (0, OFF_CB // CC))]


NCB = CC // LANES


def _pad_zero(pad):
    for cb in range(NCB):
        pad[cb, 0:PADR, :] = jnp.zeros((PADR, LANES), F32)
        pad[cb, PADR + S:PADR + S + PADR, :] = jnp.zeros((PADR, LANES), F32)


def _pad_store(pad, row0, n, val):
    for cb in range(NCB):
        pad[cb, pl.ds(pl.multiple_of(row0 + PADR, 8), n), :] = val[:, cb * LANES:(cb + 1) * LANES]


def _taps(pad_ref, cb, s0, weights):
    acc = jnp.zeros((CT, LANES), F32)
    for k in range(KW):
        acc = acc + weights[k] * pad_ref[cb, pl.ds(s0 + k + 1, CT), :]
    return acc


def conv_fwd(proj, conv_w, conv_b, ln_w, ln_b):
    def body(a_ref, b_ref, w_ref, cb_ref, lw_ref, lb_ref, c_ref, u3_ref, upad):
        _pad_zero(upad)

        def glu(i, _):
            rows = pl.ds(pl.multiple_of(i * TM, TM), TM)
            _pad_store(upad, i * TM, TM, a_ref[rows, :] * _sigmoid(b_ref[rows, :]))
            return 0

        lax.fori_loop(0, S // TM, glu, 0)

        def chunk(i, _):
            s0 = pl.multiple_of(i * CT, CT)
            for cb in range(CC // LANES):
                cols = slice(cb * LANES, (cb + 1) * LANES)
                w = [w_ref[k:k + 1, cols] for k in range(KW)]
                c_ref[pl.ds(s0, CT), cols] = _taps(upad, cb, s0, w) + cb_ref[:, cols]
            cv = c_ref[pl.ds(s0, CT), :]
            mu = jnp.mean(cv, axis=-1, keepdims=True)
            xc = cv - mu
            rstd = lax.rsqrt(jnp.mean(xc * xc, axis=-1, keepdims=True) + EPS)
            yl = xc * rstd * lw_ref[...] + lb_ref[...]
            u3_ref[pl.ds(s0, CT), :] = (yl * _sigmoid(yl)).astype(BF16)
            return 0

        lax.fori_loop(0, S // CT, chunk, 0)

    vec = pl.BlockSpec((1, CC), lambda i: (0, 0))
    full = pl.BlockSpec((S, CC), lambda i: (0, 0))
    return _call(
        body, name="conv_fwd", grid=(1,),
        in_specs=_conv_specs() + [pl.BlockSpec((KW, CC), lambda i: (0, 0)), vec, vec, vec],
        out_specs=[full, full],
        out_shape=[jax.ShapeDtypeStruct((S, CC), F32), jax.ShapeDtypeStruct((S, CC), BF16)],
        scratch_shapes=[pltpu.VMEM((NCB, S + 2 * PADR, LANES), F32)],
        args=(proj, proj, conv_w, conv_b, ln_w, ln_b))


def conv_bwd(proj, cpre, d_u3, conv_w, ln_w, ln_b, sides=()):
    def body(a_ref, b_ref, c_ref, du3_ref, w_ref, lw_ref, lb_ref,
             dc_ref, gw_ref, gcb_ref, glw_ref, glb_ref, upad, dpad):
        _pad_zero(upad)
        _pad_zero(dpad)
        gw_ref[...] = jnp.zeros_like(gw_ref)

        def ln_bwd(i, carry):
            gcb, glw, glb = carry
            rows = pl.ds(pl.multiple_of(i * TM, TM), TM)
            _pad_store(upad, i * TM, TM, a_ref[rows, :] * _sigmoid(b_ref[rows, :]))
            cv = c_ref[rows, :]
            mu = jnp.mean(cv, axis=-1, keepdims=True)
            xc = cv - mu
            rstd = lax.rsqrt(jnp.mean(xc * xc, axis=-1, keepdims=True) + EPS)
            xh = xc * rstd
            yl = xh * lw_ref[...] + lb_ref[...]
            dyl = du3_ref[rows, :] * _dsilu(yl, _sigmoid(yl))
            dxh = dyl * lw_ref[...]
            dcv = rstd * (dxh - jnp.mean(dxh, axis=-1, keepdims=True)
                          - xh * jnp.mean(dxh * xh, axis=-1, keepdims=True))
            _pad_store(dpad, i * TM, TM, dcv)
            return (gcb + jnp.sum(dcv, axis=0, keepdims=True),
                    glw + jnp.sum(dyl * xh, axis=0, keepdims=True),
                    glb + jnp.sum(dyl, axis=0, keepdims=True))

        z = jnp.zeros((1, CC), F32)
        gcb, glw, glb = lax.fori_loop(0, S // TM, ln_bwd, (z, z, z))
        gcb_ref[...] = gcb
        glw_ref[...] = glw
        glb_ref[...] = glb

        def chunk(i, _):
            s0 = pl.multiple_of(i * CT, CT)
            for cb in range(CC // LANES):
                cols = slice(cb * LANES, (cb + 1) * LANES)
                wr = [w_ref[KW - 1 - k:KW - k, cols] for k in range(KW)]
                du = _taps(dpad, cb, s0, wr)
                dcv = dpad[cb, pl.ds(s0 + PADR, CT), :]
                for k in range(KW):
                    gw_ref[k:k + 1, cols] = gw_ref[k:k + 1, cols] + jnp.sum(
                        upad[cb, pl.ds(s0 + k + 1, CT), :] * dcv, axis=0, keepdims=True)
                av = a_ref[pl.ds(s0, CT), cols]
                sb = _sigmoid(b_ref[pl.ds(s0, CT), cols])
                dc_ref[0, pl.ds(s0, CT), cols] = (du * sb).astype(BF16)
                dc_ref[1, pl.ds(s0, CT), cols] = (du * av * sb * (1.0 - sb)).astype(BF16)
            return 0

        lax.fori_loop(0, S // CT, chunk, 0)

    vec = pl.BlockSpec((1, CC), lambda i: (0, 0))
    full = pl.BlockSpec((S, CC), lambda i: (0, 0))
    wsp = pl.BlockSpec((KW, CC), lambda i: (0, 0))
    return _call(
        body, sides, name="conv_bwd", grid=(1,),
        in_specs=_conv_specs() + [full, full, wsp, vec, vec],
        out_specs=[pl.BlockSpec((2, S, CC), lambda i: (0, 0, 0)), wsp, vec, vec, vec],
        out_shape=[jax.ShapeDtypeStruct((2, S, CC), BF16), jax.ShapeDtypeStruct((KW, CC), F32)]
        + [jax.ShapeDtypeStruct((1, CC), F32)] * 3,
        scratch_shapes=[pltpu.VMEM((NCB, S + 2 * PADR, LANES), F32)] * 2,
        args=(proj, proj, cpre, d_u3, conv_w, ln_w, ln_b))


def _gate_specs():
    return [_row(CC, col=OFF_GA // CC + j) for j in range(4)]


def _gates(g_refs, bg_ref):
    ga = _sigmoid(jnp.concatenate([g_refs[0][...], g_refs[1][...]], axis=1) + bg_ref[0:1, :])
    gb = _sigmoid(jnp.concatenate([g_refs[2][...], g_refs[3][...]], axis=1) + bg_ref[1:2, :])
    return ga, gb


def mix_out(x, proj, b_gate, attn, u3, w_o, w_pw, w_out):
    def body(x_ref, g0, g1, g2, g3, bg_ref, at_ref, u3_ref, wo_ref, wp_ref, wout_ref,
             x1_ref, z_ref, ya_ref, yb_ref):
        ga, gb = _gates((g0, g1, g2, g3), bg_ref)
        ya = _dot_nt(at_ref[...], wo_ref[...])
        yb = _dot_nt(u3_ref[...], wp_ref[...])
        z = (ga * ya + gb * yb).astype(BF16)
        ya_ref[...] = ya.astype(BF16)
        yb_ref[...] = yb.astype(BF16)
        z_ref[...] = z
        x1_ref[...] = x_ref[...] + _dot(z, wout_ref[...])

    return pl.pallas_call(
        body, name="mix_out", grid=(S // TM,),
        in_specs=[_row(D)] + _gate_specs() + [_res((2, D)), _row(CC), _row(CC),
                                              _res((D, CC)), _res((D, CC)), _res((D, D))],
        out_specs=[_row(D)] * 4,
        out_shape=[jax.ShapeDtypeStruct((S, D), F32)] + [jax.ShapeDtypeStruct((S, D), BF16)] * 3,
        compiler_params=_cp(dimension_semantics=("arbitrary",)),
    )(x, proj, proj, proj, proj, b_gate, attn, u3, w_o, w_pw, w_out)


def out_bwd(d_x1b, proj, b_gate, ya, yb, w_o, w_pw, w_out, sides=()):
    def body(dx_ref, g0, g1, g2, g3, bg_ref, ya_ref, yb_ref, wo_ref, wp_ref, wout_ref,
             dya_ref, dyb_ref, dgl_ref, dat_ref, du3_ref, gbg_ref):
        @pl.when(pl.program_id(0) == 0)
        def _():
            gbg_ref[...] = jnp.zeros_like(gbg_ref)

        ga, gb = _gates((g0, g1, g2, g3), bg_ref)
        dz = _dot_nt(dx_ref[...], wout_ref[...])
        dya = (dz * ga).astype(BF16)
        dyb = (dz * gb).astype(BF16)
        dgla = dz * ya_ref[...].astype(F32) * ga * (1.0 - ga)
        dglb = dz * yb_ref[...].astype(F32) * gb * (1.0 - gb)
        dya_ref[...] = dya
        dyb_ref[...] = dyb
        for j in range(2):
            dgl_ref[j] = dgla[:, j * PLANE:(j + 1) * PLANE].astype(BF16)
            dgl_ref[2 + j] = dglb[:, j * PLANE:(j + 1) * PLANE].astype(BF16)
        gbg_ref[0:1, :] = gbg_ref[0:1, :] + jnp.sum(dgla, axis=0, keepdims=True)
        gbg_ref[1:2, :] = gbg_ref[1:2, :] + jnp.sum(dglb, axis=0, keepdims=True)
        dat_ref[...] = _dot(dya, wo_ref[...])
        du3_ref[...] = _dot(dyb, wp_ref[...])

    return _call(
        body, sides, name="out_bwd", grid=(S // TM,),
        in_specs=[_row(D)] + _gate_specs() + [_res((2, D)), _row(D), _row(D),
                                              _res((D, CC)), _res((D, CC)), _res((D, D))],
        out_specs=[_row(D), _row(D), _planes(2 * D), _row(CC), _row(CC), pl.BlockSpec((2, D), lambda i: (0, 0))],
        out_shape=[jax.ShapeDtypeStruct((S, D), BF16)] * 2 + [jax.ShapeDtypeStruct((2 * D // PLANE, S, PLANE), BF16)]
        + [jax.ShapeDtypeStruct((S, CC), F32)] * 2 + [jax.ShapeDtypeStruct((2, D), F32)],
        args=(d_x1b, proj, proj, proj, proj, b_gate, ya, yb, w_o, w_pw, w_out))


def ffn_in(x1, norm_w, w_ffn_in, sides=()):
    half = FF // 2

    def body(x_ref, nw_ref, w_ref, h_ref, gu_ref, f_ref):
        xv = x_ref[...]
        r = lax.rsqrt(jnp.mean(xv * xv, axis=-1, keepdims=True) + EPS)
        h = (xv * r * nw_ref[...]).astype(BF16)
        h_ref[...] = h
        for j in range(2):
            gt = _dot_nt(h, w_ref[j * half:(j + 1) * half, :])
            up = _dot_nt(h, w_ref[FF + j * half:FF + (j + 1) * half, :])
            gu_ref[:, j * half:(j + 1) * half] = gt.astype(BF16)
            gu_ref[:, FF + j * half:FF + (j + 1) * half] = up.astype(BF16)
            f_ref[:, j * half:(j + 1) * half] = (gt * _sigmoid(gt) * up).astype(BF16)

    return _call(
        body, sides, name="ffn_in", grid=(S // TM,),
        in_specs=[_row(D), _res((1, D)), _res((2 * FF, D))],
        out_specs=[_row(D), _row(2 * FF), _row(FF)],
        out_shape=[jax.ShapeDtypeStruct((S, D), BF16), jax.ShapeDtypeStruct((S, 2 * FF), BF16),
                   jax.ShapeDtypeStruct((S, FF), BF16)],
        args=(x1, norm_w, w_ffn_in))


def ffn_out_loss(x1, f, w_ffn_out, target):
    def body(x_ref, f_ref, w_ref, t_ref, dy_ref, dyb_ref, sq_ref):
        @pl.when(pl.program_id(0) == 0)
        def _():
            sq_ref[...] = jnp.zeros_like(sq_ref)

        diff = x_ref[...] + _dot(f_ref[...], w_ref[...]) - t_ref[...]
        dy = diff * (1.0 / D)
        dy_ref[...] = dy
        dyb_ref[...] = dy.astype(BF16)
        sq_ref[...] = sq_ref[...] + jnp.sum((diff * diff).reshape(TM // 8, 8, D), axis=0)

    return pl.pallas_call(
        body, name="ffn_out_loss", grid=(S // TM,),
        in_specs=[_row(D), _row(FF), _res((FF, D)), _row(D)],
        out_specs=[_row(D), _row(D), pl.BlockSpec((8, D), lambda i: (0, 0))],
        out_shape=[jax.ShapeDtypeStruct((S, D), F32), jax.ShapeDtypeStruct((S, D), BF16),
                   jax.ShapeDtypeStruct((8, D), F32)],
        compiler_params=_cp(dimension_semantics=("arbitrary",)),
    )(x1, f, w_ffn_out, target)


def _rms_bwd(xv, nw, dh):
    r = lax.rsqrt(jnp.mean(xv * xv, axis=-1, keepdims=True) + EPS)
    xn = xv * r
    dxn = dh * nw
    dx = r * (dxn - xn * jnp.mean(dxn * xn, axis=-1, keepdims=True))
    return dx, dh * xn


def ffn_bwd(dy, dyb, gu, x1, norm_w, w_ffn_in, w_ffn_out, sides=()):
    def body(dy_ref, dyb_ref, gu_ref, x_ref, nw_ref, wi_ref, wo_ref, dgu_ref, dx_ref, dxb_ref, gn_ref):
        @pl.when(pl.program_id(0) == 0)
        def _():
            gn_ref[...] = jnp.zeros_like(gn_ref)

        df = _dot_nt(dyb_ref[...], wo_ref[...])
        gt = gu_ref[:, 0:FF].astype(F32)
        up = gu_ref[:, FF:2 * FF].astype(F32)
        sg = _sigmoid(gt)
        dgt = (df * up * _dsilu(gt, sg)).astype(BF16)
        dup = (df * gt * sg).astype(BF16)
        dgu_ref[:, 0:FF] = dgt
        dgu_ref[:, FF:2 * FF] = dup
        dh = _dot(dgt, wi_ref[0:FF, :]) + _dot(dup, wi_ref[FF:2 * FF, :])
        dxn, gw = _rms_bwd(x_ref[...], nw_ref[...], dh)
        dx = dy_ref[...] + dxn
        dx_ref[...] = dx
        dxb_ref[...] = dx.astype(BF16)
        gn_ref[...] = gn_ref[...] + jnp.sum(gw, axis=0, keepdims=True)

    return _call(
        body, sides, name="ffn_bwd", grid=(S // TM,),
        in_specs=[_row(D), _row(D), _row(2 * FF), _row(D), _res((1, D)), _res((2 * FF, D)), _res((FF, D))],
        out_specs=[_row(2 * FF), _row(D), _row(D), pl.BlockSpec((1, D), lambda i: (0, 0))],
        out_shape=[jax.ShapeDtypeStruct((S, 2 * FF), BF16), jax.ShapeDtypeStruct((S, D), F32),
                   jax.ShapeDtypeStruct((S, D), BF16), jax.ShapeDtypeStruct((1, D), F32)],
        args=(dy, dyb, gu, x1, norm_w, w_ffn_in, w_ffn_out))


def in_bwd(d_q, d_k, d_v, d_conv, d_gl, w_in, x, d_x1, norm_w, sides=()):
    segs = ((OFF_Q, QKV), (OFF_K, QKV), (OFF_V, QKV), (OFF_CA, 2 * CC), (OFF_GA, 2 * D))

    def body(dq_ref, dk_ref, dv_ref, dc_ref, dg_ref, w_ref, x_ref, dx1_ref, nw_ref, gx_ref, gn_ref):
        @pl.when(pl.program_id(0) == 0)
        def _():
            gn_ref[...] = jnp.zeros_like(gn_ref)

        dh = jnp.zeros((TM, D), F32)
        for ref, (off, width) in zip((dq_ref, dk_ref, dv_ref, dc_ref, dg_ref), segs):
            for j in range(width // PLANE):
                dh = dh + _dot(ref[j], w_ref[off + j * PLANE:off + (j + 1) * PLANE, :])
        dxn, gw = _rms_bwd(x_ref[...], nw_ref[...], dh)
        gx_ref[...] = dx1_ref[...] + dxn
        gn_ref[...] = gn_ref[...] + jnp.sum(gw, axis=0, keepdims=True)

    return _call(
        body, sides, name="in_bwd", grid=(S // TM,),
        in_specs=[_planes(QKV)] * 3 + [_planes(2 * CC), _planes(2 * D), _res((INW, D)), _row(D), _row(D), _res((1, D))],
        out_specs=[_row(D), pl.BlockSpec((1, D), lambda i: (0, 0))],
        out_shape=[jax.ShapeDtypeStruct((S, D), F32), jax.ShapeDtypeStruct((1, D), F32)],
        args=(d_q, d_k, d_v, d_conv, d_gl, w_in, x, d_x1, norm_w))


def mm_tn(name, pairs, tm):
    n = len(pairs)
    M = pairs[0][0].shape[1]
    widths = [b.shape[1] for _, b in pairs]

    def body(*refs):
        for a_ref, b_ref, o_ref, ob_ref in zip(refs[0:2 * n:2], refs[1:2 * n:2], refs[2 * n::2], refs[2 * n + 1::2]):
            r = _dot_tn(a_ref[...], b_ref[...])
            o_ref[...] = r
            ob_ref[...] = r.astype(BF16)

    return _call(
        body, name=name, grid=(M // tm,),
        in_specs=[sp for N in widths for sp in (pl.BlockSpec((S, tm), lambda i: (0, i)), _res((S, N)))],
        out_specs=[pl.BlockSpec((tm, N), lambda i: (i, 0)) for N in widths for _ in range(2)],
        out_shape=[jax.ShapeDtypeStruct((M, N), dt) for N in widths for dt in (F32, BF16)],
        args=[t for pair in pairs for t in pair])


GW_IN_TN = PLANE
GW_IN_SPLIT = (768, 256)


def gw_in(name, h, d_segs, col0, hw, sides=()):
    tn = GW_IN_TN
    starts, t0 = [], 0
    for seg in d_segs:
        starts.append(t0)
        t0 += seg.shape[0]
    ntiles = [seg.shape[0] for seg in d_segs]

    def body(h_ref, *refs):
        a_refs, o_ref, ob_ref = refs[:-2], refs[-2], refs[-1]
        n = pl.program_id(0)
        for a_ref, st, nt in zip(a_refs, starts, ntiles):
            @pl.when((n >= st) & (n < st + nt))
            def _(a_ref=a_ref):
                r = _dot_tn(a_ref[...], h_ref[...])
                o_ref[...] = r
                ob_ref[...] = r.astype(BF16)

    def seg_spec(st, nt):
        return pl.BlockSpec((None, S, tn), lambda n: (jnp.clip(n - st, 0, nt - 1), 0, 0))

    res = _call(
        body, sides, name=name, grid=(INW // tn,),
        in_specs=[pl.BlockSpec((S, hw), lambda n: (0, col0 // hw))] + [seg_spec(st, nt) for st, nt in zip(starts, ntiles)],
        out_specs=[pl.BlockSpec((tn, hw), lambda n: (n, 0))] * 2,
        out_shape=[jax.ShapeDtypeStruct((INW, hw), F32), jax.ShapeDtypeStruct((INW, hw), BF16)],
        args=(h, *d_segs))
    return (res[0], res[1]) if sides else (res, [])


def _place():
    x, y, c = lax.axis_index("x"), lax.axis_index("y"), lax.axis_index("c")
    chips = [(1 - x, y), (x, 1 - y), (1 - x, 1 - y)]
    return x, y, c, chips


def _sems(n):
    return pltpu.SemaphoreType.DMA((n,))


def _remote(src, dst, send, recv, k, to):
    return pltpu.make_async_remote_copy(src_ref=src, dst_ref=dst, send_sem=send.at[k], recv_sem=recv.at[k],
                                        device_id=to, device_id_type=MESH)


def _cast_rows(dst, src, cols=slice(None)):
    rows = src.shape[0]
    step = next((s for s in (128, 64, 32, 16) if rows % s == 0), rows)
    for r0 in range(0, rows, step):
        dst[r0:r0 + step, cols] = src[r0:r0 + step, :].astype(dst.dtype)


def comm_only(name, sides):
    def body():
        pass

    return _call(body, sides, name=name, grid=(1,), in_specs=[], out_specs=[], out_shape=[], args=())[1]


def ag_blocks(shard, dtype):
    R, W = shard.shape

    def copy(outs, scr, k, block, to, src=None):
        dst = outs[0].at[block]
        return _remote(dst if src is None else src, dst, scr[1], scr[2], k, to)

    def local(outs, scr, me):
        return pltpu.make_async_copy(scr[0], outs[0].at[me], scr[3].at[0])

    def start(ins, outs, scr):
        x, y, c, chips = _place()
        me = 4 * x + 2 * y + c
        _cast_rows(scr[0], ins[0])
        local(outs, scr, me).start()
        copy(outs, scr, 0, me, (x, y, 1 - c), src=scr[0]).start()
        for j, (cx, cy) in enumerate(chips):
            copy(outs, scr, 1 + j, me, (cx, cy, c), src=scr[0]).start()

    def finish(ins, outs, scr):
        x, y, c, chips = _place()
        me, sib = 4 * x + 2 * y + c, (x, y, 1 - c)
        passed = []
        for j, (cx, cy) in enumerate(chips):
            theirs = 4 * cx + 2 * cy + c
            copy(outs, scr, 1 + j, theirs, (x, y, c)).wait_recv()
            fwd = copy(outs, scr, 4 + j, theirs, sib)
            fwd.start()
            passed.append(fwd)
        copy(outs, scr, 0, 4 * x + 2 * y + 1 - c, (x, y, c)).wait_recv()
        for j, (cx, cy) in enumerate(chips):
            copy(outs, scr, 4 + j, 4 * cx + 2 * cy + 1 - c, (x, y, c)).wait_recv()
        copy(outs, scr, 0, me, sib, src=scr[0]).wait_send()
        for j, (cx, cy) in enumerate(chips):
            copy(outs, scr, 1 + j, me, (cx, cy, c), src=scr[0]).wait_send()
        for fwd in passed:
            fwd.wait_send()
        local(outs, scr, me).wait()

    return Side((shard,), (VMEM,), (jax.ShapeDtypeStruct((NDEV, R, W), dtype),),
                (pltpu.VMEM((R, W), dtype), _sems(7), _sems(7), _sems(1)), start, finish, None, "dsxy")


def ag_blocks_relay(shard, dtype, transpose=False):
    R, W = shard.shape[::-1] if transpose else shard.shape
    half = R // 2

    def copy(outs, scr, k, block, to, src=None, rows=None):
        dst = outs[0].at[block] if rows is None else outs[0].at[block, pl.ds(rows * half, half), :]
        return _remote(dst if src is None else src, dst, scr[1], scr[2], k, to)

    def local(outs, scr, me):
        return pltpu.make_async_copy(scr[0], outs[0].at[me], scr[3].at[0])

    def own(outs, scr):
        x, y, c, _ = _place()
        me = 4 * x + 2 * y + c
        return [copy(outs, scr, k, me, to, src=scr[0])
                for k, to in enumerate([(x, y, 1 - c), (1 - x, y, c), (x, 1 - y, c)])]

    def start(ins, outs, scr):
        x, y, c, _ = _place()
        if transpose:
            scr[0][...] = ins[0][...].T.astype(dtype)
        else:
            _cast_rows(scr[0], ins[0])
        local(outs, scr, 4 * x + 2 * y + c).start()
        for cp in own(outs, scr):
            cp.start()

    def passed_on(outs, scr):
        x, y, c, _ = _place()
        sib, xn, yn = (x, y, 1 - c), (1 - x, y, c), (x, 1 - y, c)
        b_xn, b_yn, b_dg = 4 * (1 - x) + 2 * y + c, 4 * x + 2 * (1 - y) + c, 4 * (1 - x) + 2 * (1 - y) + c
        near = [copy(outs, scr, 5, b_xn, yn, rows=0), copy(outs, scr, 3, b_xn, sib),
                copy(outs, scr, 6, b_yn, xn, rows=1), copy(outs, scr, 4, b_yn, sib)]
        far = [copy(outs, scr, 7, b_dg, sib, rows=0), copy(outs, scr, 8, b_dg, sib, rows=1)]
        return (b_xn, b_yn, b_dg), near, far

    def mid(ins, outs, scr):
        x, y, c, _ = _place()
        (b_xn, b_yn, _), near, _ = passed_on(outs, scr)
        copy(outs, scr, 1, b_xn, (x, y, c)).wait_recv()
        near[0].start()
        near[1].start()
        copy(outs, scr, 2, b_yn, (x, y, c)).wait_recv()
        near[2].start()
        near[3].start()

    def finish(ins, outs, scr):
        x, y, c, _ = _place()
        here = (x, y, c)
        (b_xn, b_yn, b_dg), near, far = passed_on(outs, scr)
        copy(outs, scr, 5, b_dg, here, rows=0).wait_recv()
        far[0].start()
        copy(outs, scr, 6, b_dg, here, rows=1).wait_recv()
        far[1].start()
        flip = 1 - 2 * c
        copy(outs, scr, 0, 4 * x + 2 * y + 1 - c, here).wait_recv()
        copy(outs, scr, 3, b_xn + flip, here).wait_recv()
        copy(outs, scr, 4, b_yn + flip, here).wait_recv()
        copy(outs, scr, 7, b_dg + flip, here, rows=0).wait_recv()
        copy(outs, scr, 8, b_dg + flip, here, rows=1).wait_recv()
        for cp in own(outs, scr) + near + far:
            cp.wait_send()
        local(outs, scr, 4 * x + 2 * y + c).wait()

    return Side((shard,), (VMEM,), (jax.ShapeDtypeStruct((NDEV, R, W), dtype),),
                (pltpu.VMEM((R, W), dtype), _sems(9), _sems(9), _sems(1)), start, finish, mid, "sxy")


def copies_side(args, out_shape, n_copies, plan, peers):
    def copies(ins, outs, scr):
        return [_remote(s_, d_, scr[0], scr[1], i, to) for i, (s_, d_, to) in enumerate(plan(ins, outs))]

    def start(ins, outs, scr):
        for cp in copies(ins, outs, scr):
            cp.start()

    def finish(ins, outs, scr):
        for cp in copies(ins, outs, scr):
            cp.wait()

    return Side(tuple(args), (ANY,) * len(args), tuple(out_shape), (_sems(n_copies), _sems(n_copies)),
                start, finish, None, peers)


def rs_to_sibling(grads):
    out_shape = [jax.ShapeDtypeStruct((4,) + g.shape[1:], BF16) for g in grads]

    def plan(ins, outs):
        x, y, c, _ = _place()
        return [(g.at[2 * k + 1 - c], r.at[k], (x, y, 1 - c)) for g, r in zip(ins, outs) for k in range(4)]

    return copies_side(grads, out_shape, 4 * len(grads), plan, "s")


def rs_to_chips(parts):
    out_shape = [jax.ShapeDtypeStruct((3,) + p.shape[1:], BF16) for p in parts]

    def plan(ins, outs):
        x, y, c, chips = _place()
        return [(p.at[2 * cx + cy], r.at[j], (cx, cy, c))
                for p, r in zip(ins, outs) for j, (cx, cy) in enumerate(chips)]

    return copies_side(parts, out_shape, 3 * len(parts), plan, "dxy")


def rs_to_chips_combined(part):
    _, R, W = part.shape
    half = R // 2
    top, bot = pl.ds(0, half), pl.ds(half, half)

    def copies(ins, outs, scr):
        p, r = ins[0], outs[0]
        loc_a, loc_b, in_x, in_y, comb_a, comb_b, send, recv, loc = scr
        x, y, c, _ = _place()
        xn, yn = (1 - x, y, c), (x, 1 - y, c)
        k_xn, k_yn, k_dg = 2 * (1 - x) + y, 2 * x + 1 - y, 2 * (1 - x) + 1 - y
        direct = [_remote(p.at[k_xn, top, :], r.at[0, top, :], send, recv, 0, xn),
                  _remote(p.at[k_yn, bot, :], r.at[1, bot, :], send, recv, 1, yn),
                  _remote(p.at[k_dg, top, :], in_x, send, recv, 2, xn),
                  _remote(p.at[k_dg, bot, :], in_y, send, recv, 3, yn)]
        combined = [_remote(comb_a, r.at[1, top, :], send, recv, 4, yn),
                    _remote(comb_b, r.at[0, bot, :], send, recv, 5, xn)]
        local = [pltpu.make_async_copy(p.at[k_yn, top, :], loc_a, loc.at[0]),
                 pltpu.make_async_copy(p.at[k_xn, bot, :], loc_b, loc.at[1])]
        return direct, combined, local

    def start(ins, outs, scr):
        direct, _, local = copies(ins, outs, scr)
        for cp in local + direct:
            cp.start()

    def mid(ins, outs, scr):
        loc_a, loc_b, in_x, in_y, comb_a, comb_b = scr[:6]
        direct, combined, local = copies(ins, outs, scr)
        for mine, arrival, inbox, out, nxt in ((local[0], direct[2], in_x, comb_a, combined[0]),
                                               (local[1], direct[3], in_y, comb_b, combined[1])):
            mine.wait()
            arrival.wait_recv()
            src = loc_a if out is comb_a else loc_b
            out[...] = (src[...].astype(F32) + inbox[...].astype(F32)).astype(BF16)
            nxt.start()

    def finish(ins, outs, scr):
        direct, combined, _ = copies(ins, outs, scr)
        direct[0].wait_recv()
        direct[1].wait_recv()
        combined[0].wait_recv()
        combined[1].wait_recv()
        for cp in direct + combined:
            cp.wait_send()

    buf = pltpu.VMEM((half, W), BF16)
    return Side((part,), (ANY,), (jax.ShapeDtypeStruct((2, R, W), BF16),),
                (buf, buf, buf, buf, buf, buf, _sems(6), _sems(6), _sems(2)), start, finish, mid, "xy")


ADAM_TILE_BYTES = 3 * 512 * 1024


def _row_tiles(rows, width):
    return 2 if rows % 32 == 0 and rows * width * 4 > ADAM_TILE_BYTES else 1


def chip_sum(name, grads, recvs, c_idx, chip_idx):
    n = len(grads)

    def body(s_ref, *refs):
        k = pl.program_id(0)
        for g_ref, r_ref, p_ref, own_ref in zip(refs[:n], refs[n:2 * n], refs[2 * n::2], refs[2 * n + 1::2]):
            tot = g_ref[0] + r_ref[0].astype(F32)
            p_ref[0] = tot.astype(BF16)

            @pl.when(k == s_ref[1])
            def _(own_ref=own_ref, tot=tot):
                own_ref[...] = tot

    def block(g):
        return (1,) + g.shape[1:]

    grid_spec = pltpu.PrefetchScalarGridSpec(
        num_scalar_prefetch=1, grid=(4,),
        in_specs=[pl.BlockSpec(block(g), lambda k, s: (2 * k + s[0], 0, 0)) for g in grads]
        + [pl.BlockSpec(block(g), lambda k, s: (k, 0, 0)) for g in grads],
        out_specs=[sp for g in grads for sp in (pl.BlockSpec(block(g), lambda k, s: (k, 0, 0)),
                                                pl.BlockSpec(g.shape[1:], lambda k, s: (0, 0)))])
    res = pl.pallas_call(
        body, name=name, grid_spec=grid_spec,
        out_shape=[sh for g in grads for sh in (jax.ShapeDtypeStruct((4,) + g.shape[1:], BF16),
                                                jax.ShapeDtypeStruct(g.shape[1:], F32))],
        compiler_params=_cp(dimension_semantics=("arbitrary",)),
    )(jnp.stack([c_idx, chip_idx]), *grads, *recvs)
    return [(res[2 * j], res[2 * j + 1]) for j in range(n)]


def _adamw(w, g, m, v):
    m2 = ADAM_B1 * m + (1.0 - ADAM_B1) * g
    v2 = ADAM_B2 * v + (1.0 - ADAM_B2) * (g * g)
    m_hat = m2 / (1.0 - ADAM_B1 ** ADAM_STEP)
    v_hat = v2 / (1.0 - ADAM_B2 ** ADAM_STEP)
    delta = -ADAM_LR * (m_hat / (jnp.sqrt(v_hat) + ADAM_EPS) + ADAM_WD * w)
    return delta, m2, v2


def shard_adam(name, owns, recvs, w, m, v):
    n = len(owns)
    R = owns[0].shape[0]
    ct = min(o.shape[1] for o in owns)
    first = [sum(o.shape[1] for o in owns[:j]) // ct for j in range(n)]
    count = [o.shape[1] // ct for o in owns]
    nt = _row_tiles(R, ct)
    tr = R // nt

    def body(*refs):
        o_refs, r_refs = refs[:n], refs[n:2 * n]
        w_ref, m_ref, v_ref, g_ref, d_ref, nm_ref, nv_ref = refs[2 * n:]
        g = None
        for j in range(n):
            gj = o_refs[j][...]
            for q in range(recvs[j].shape[0]):
                gj = gj + r_refs[j][q].astype(F32)
            g = gj if g is None else jnp.where(pl.program_id(0) >= first[j], gj, g)
        delta, m2, v2 = _adamw(w_ref[...], g, m_ref[...], v_ref[...])
        g_ref[...] = g
        d_ref[...] = delta
        nm_ref[...] = m2
        nv_ref[...] = v2

    def part(j):
        return pl.BlockSpec((tr, ct), lambda k, i: (i, jnp.clip(k - first[j], 0, count[j] - 1)))

    def part3(j):
        return pl.BlockSpec((recvs[j].shape[0], tr, ct), lambda k, i: (0, i, jnp.clip(k - first[j], 0, count[j] - 1)))

    C = sum(count) * ct
    tile = pl.BlockSpec((tr, ct), lambda k, i: (i, k))
    return pl.pallas_call(
        body, name=name, grid=(sum(count), nt),
        in_specs=[part(j) for j in range(n)] + [part3(j) for j in range(n)] + [tile, tile, tile],
        out_specs=[tile] * 4, out_shape=[jax.ShapeDtypeStruct((R, C), F32)] * 4,
        compiler_params=_cp(dimension_semantics=("arbitrary", "arbitrary")),
    )(*owns, *recvs, w, m, v)


def block_adam(name, items):
    n = len(items)

    def body(*refs):
        for j, item in enumerate(items):
            o_ref, r_ref, w_ref, m_ref, v_ref = refs[5 * j:5 * j + 5]
            g = o_ref[...]
            for q in range(r_ref.shape[0]):
                g = g + r_ref[q].astype(F32)
            t = (lambda a: a.T) if item[5] else (lambda a: a)
            delta, m2, v2 = _adamw(t(w_ref[...]), g, t(m_ref[...]), t(v_ref[...]))
            for ref, val in zip(refs[5 * n + 4 * j:5 * n + 4 * j + 4], (g, delta, m2, v2)):
                ref[...] = t(val)

    args = [a for item in items for a in item[:5]]
    out_shape = [jax.ShapeDtypeStruct(item[2].shape, F32) for item in items for _ in range(4)]
    res = pl.pallas_call(
        body, name=name, grid=(1,), in_specs=[VMEM] * len(args), out_specs=[VMEM] * len(out_shape),
        out_shape=out_shape, compiler_params=_cp(dimension_semantics=("arbitrary",)))(*args)
    return [tuple(res[4 * j:4 * j + 4]) for j in range(n)]


ROW_N1, ROW_N2, ROW_BG, ROW_QN, ROW_KN, ROW_CB, ROW_LW, ROW_LB, ROW_CW = 0, 1, 2, 4, 5, 6, 7, 8, 9
PACK_ROWS = 40
SMALL = ("norm1_w", "norm2_w", "b_gate", "q_norm_w", "k_norm_w", "conv_b", "conv_ln_w", "conv_ln_b", "conv_w")


def small_sync(g, sq, sides=()):
    ns = len(SMALL)

    def copies(refs):
        pack, recv, send_sems, recv_sems = refs[ns + 2:]
        x, y, c, _ = _place()
        return [pltpu.make_async_remote_copy(
            src_ref=pack, dst_ref=recv.at[4 * x + 2 * y + c], send_sem=send_sems.at[k - 1],
            recv_sem=recv_sems.at[k - 1], device_id=(x ^ (k >> 2), y ^ ((k >> 1) & 1), c ^ (k & 1)),
            device_id_type=MESH) for k in range(1, NDEV)]

    def body(*refs):
        gi = dict(zip(SMALL, refs[:ns]))
        sq_ref, tot, pack, recv, send_sems, recv_sems = refs[ns:]
        x, y, c, _ = _place()
        me = 4 * x + 2 * y + c

        pack[...] = jnp.zeros_like(pack)
        pack[ROW_KN:ROW_KN + 1, LANES:2 * LANES] = jnp.full((1, LANES), (0.5 / D) * jnp.sum(sq_ref[...]), F32)
        pack[ROW_N1:ROW_N1 + 1, :] = gi["norm1_w"][...]
        pack[ROW_N2:ROW_N2 + 1, :] = gi["norm2_w"][...]
        pack[ROW_BG:ROW_BG + 2, :] = gi["b_gate"][...]
        for row, name in ((ROW_QN, "q_norm_w"), (ROW_KN, "k_norm_w")):
            pack[row:row + 1, 0:HD] = gi[name][0:1, 0:HD] + gi[name][0:1, HD:LANES]
        pack[ROW_CB:ROW_CB + 1, 0:CC] = gi["conv_b"][...]
        pack[ROW_LW:ROW_LW + 1, 0:CC] = gi["conv_ln_w"][...]
        pack[ROW_LB:ROW_LB + 1, 0:CC] = gi["conv_ln_b"][...]
        pack[ROW_CW:ROW_CW + KW, 0:CC] = gi["conv_w"][...]

        for cp in copies(refs):
            cp.start()
        recv[me] = pack[...]

    def tail(*refs):
        tot, recv = refs[ns + 1], refs[ns + 3]
        for cp in copies(refs):
            cp.wait()
        acc = recv[0]
        for p in range(1, NDEV):
            acc = acc + recv[p]
        tot[...] = acc

    args = [g[k] for k in SMALL] + [sq]
    res = _call(
        body, sides, name="small_sync", grid=(1,), in_specs=[VMEM] * len(args), out_specs=[VMEM],
        out_shape=[jax.ShapeDtypeStruct((PACK_ROWS, D), F32)],
        scratch_shapes=[pltpu.VMEM((PACK_ROWS, D), F32), pltpu.VMEM((NDEV, PACK_ROWS, D), F32),
                        _sems(NDEV - 1), _sems(NDEV - 1)],
        args=args, own_comm=True, tail=tail)
    return (res[0][0], res[1]) if sides else res[0]


def small_adam(tot, w, m, v, me):
    ns = len(SMALL)

    def body(me_ref, tot, *refs):
        wi = dict(zip(SMALL, refs[:ns]))
        mi = dict(zip(SMALL, refs[ns:2 * ns]))
        vi = dict(zip(SMALL, refs[2 * ns:3 * ns]))
        outs = refs[3 * ns:7 * ns]
        loss_ref = refs[7 * ns]
        me = me_ref[0]

        def shard_grad(name):
            if name == "b_gate":
                return tot[ROW_BG:ROW_BG + 2, pl.ds(pl.multiple_of(me * LANES, LANES), LANES)]
            if name == "conv_w":
                win = tot[ROW_CW:ROW_CW + KW, pl.ds(pl.multiple_of((me // 2) * LANES, LANES), LANES)]
                return jnp.where(me % 2 == 1, win[:, HD:LANES], win[:, 0:HD])
            row = {"norm1_w": ROW_N1, "norm2_w": ROW_N2, "q_norm_w": ROW_QN, "k_norm_w": ROW_KN,
                   "conv_b": ROW_CB, "conv_ln_w": ROW_LW, "conv_ln_b": ROW_LB}[name]
            return tot[row:row + 1, 0:wi[name].shape[1]]

        for i, name in enumerate(SMALL):
            gr = shard_grad(name)
            delta, m2, v2 = _adamw(wi[name][...], gr, mi[name][...], vi[name][...])
            outs[4 * i][...] = gr
            outs[4 * i + 1][...] = delta
            outs[4 * i + 2][...] = m2
            outs[4 * i + 3][...] = v2
        loss_ref[...] = tot[ROW_KN:ROW_KN + 1, LANES:2 * LANES]

    out_shape = []
    for name in SMALL:
        out_shape += [jax.ShapeDtypeStruct(w[name].shape, F32)] * 4
    out_shape.append(jax.ShapeDtypeStruct((1, LANES), F32))
    args = [tot] + [w[k] for k in SMALL] + [m[k] for k in SMALL] + [v[k] for k in SMALL]
    grid_spec = pltpu.PrefetchScalarGridSpec(
        num_scalar_prefetch=1, grid=(1,), in_specs=[VMEM] * len(args), out_specs=[VMEM] * len(out_shape))
    res = pl.pallas_call(body, name="small_adam", grid_spec=grid_spec, out_shape=out_shape)(me, *args)
    out = {name: tuple(res[4 * i:4 * i + 4]) for i, name in enumerate(SMALL)}
    return out, res[4 * ns][0, 0]


MATS = ("w_in", "w_o_attn", "w_pw_conv", "w_out", "w_ffn_in", "w_ffn_out")
TRANSPOSED = ("w_in", "w_ffn_in")
WEIGHTS = ("norm1_w", "w_in", "b_gate", "q_norm_w", "k_norm_w", "w_o_attn", "conv_w", "conv_b", "conv_ln_w",
           "conv_ln_b", "w_pw_conv", "w_out", "norm2_w", "w_ffn_in", "w_ffn_out")


def _blocks_to_cols(blocks):
    n, R, C = blocks.shape
    return blocks.transpose(1, 0, 2).reshape(R, n * C)


def kernel(x, positions, norm1_w, w_in, b_gate, q_norm_w, k_norm_w, w_o_attn, conv_w, conv_b, conv_ln_w, conv_ln_b, w_pw_conv, w_out, norm2_w, w_ffn_in, w_ffn_out, loss_target, m_norm1_w, m_w_in, m_b_gate, m_q_norm_w, m_k_norm_w, m_w_o_attn, m_conv_w, m_conv_b, m_conv_ln_w, m_conv_ln_b, m_w_pw_conv, m_w_out, m_norm2_w, m_w_ffn_in, m_w_ffn_out, v_norm1_w, v_w_in, v_b_gate, v_q_norm_w, v_k_norm_w, v_w_o_attn, v_conv_w, v_conv_b, v_conv_ln_w, v_conv_ln_b, v_w_pw_conv, v_w_out, v_norm2_w, v_w_ffn_in, v_w_ffn_out):
    w = dict(norm1_w=norm1_w, w_in=w_in, b_gate=b_gate, q_norm_w=q_norm_w, k_norm_w=k_norm_w, w_o_attn=w_o_attn,
             conv_w=conv_w, conv_b=conv_b, conv_ln_w=conv_ln_w, conv_ln_b=conv_ln_b, w_pw_conv=w_pw_conv,
             w_out=w_out, norm2_w=norm2_w, w_ffn_in=w_ffn_in, w_ffn_out=w_ffn_out)
    m = dict(norm1_w=m_norm1_w, w_in=m_w_in, b_gate=m_b_gate, q_norm_w=m_q_norm_w, k_norm_w=m_k_norm_w,
             w_o_attn=m_w_o_attn, conv_w=m_conv_w, conv_b=m_conv_b, conv_ln_w=m_conv_ln_w,
             conv_ln_b=m_conv_ln_b, w_pw_conv=m_w_pw_conv, w_out=m_w_out, norm2_w=m_norm2_w,
             w_ffn_in=m_w_ffn_in, w_ffn_out=m_w_ffn_out)
    v = dict(norm1_w=v_norm1_w, w_in=v_w_in, b_gate=v_b_gate, q_norm_w=v_q_norm_w, k_norm_w=v_k_norm_w,
             w_o_attn=v_w_o_attn, conv_w=v_conv_w, conv_b=v_conv_b, conv_ln_w=v_conv_ln_w,
             conv_ln_b=v_conv_ln_b, w_pw_conv=v_w_pw_conv, w_out=v_w_out, norm2_w=v_norm2_w,
             w_ffn_in=v_w_ffn_in, w_ffn_out=v_w_ffn_out)
    def two_d(t):
        t = {k: (a[0] if a.ndim == 3 else a) for k, a in t.items()}
        return {k: (a.T if k in TRANSPOSED else a) for k, a in t.items()}

    w, m, v = two_d(w), two_d(m), two_d(v)

    x2, target = x[0], loss_target[0]
    c_idx = lax.axis_index("c").astype(jnp.int32)
    chip_idx = (2 * lax.axis_index("x") + lax.axis_index("y")).astype(jnp.int32)
    qw2 = jnp.tile(w["q_norm_w"], (1, 2))
    kw2 = jnp.tile(w["k_norm_w"], (1, 2))

    ax, ay = lax.axis_index("x"), lax.axis_index("y")
    chip_order = jnp.stack([2 * ax + ay, 2 * (1 - ax) + ay, 2 * ax + 1 - ay, 2 * (1 - ax) + 1 - ay]).astype(jnp.int32)
    h, proj, w_in_blocks, tabs = in_proj_gather(x2, w["norm1_w"], w["w_in"], chip_order, positions.reshape(S, 1))
    w_in_t = w_in_blocks.reshape(INW, D)
    (attn, lse), ((w_ffn_in_blocks,), (w_out_blocks,), (w_o_blocks,), (w_pw_blocks,), (bg_blocks,), (cw_blocks,)) = attn_fwd(
        proj, tabs, qw2, kw2, sides=(ag_blocks_relay(w["w_ffn_in"], BF16), ag_blocks_relay(w["w_out"], BF16),
                                     ag_blocks_relay(w["w_o_attn"], BF16, transpose=True),
                                     ag_blocks_relay(w["w_pw_conv"], BF16, transpose=True),
                                     ag_blocks(w["b_gate"], F32), ag_blocks(w["conv_w"], F32)))
    w_ffn_in_t = w_ffn_in_blocks.reshape(2 * FF, D)
    w_out_f = w_out_blocks.reshape(D, D)
    w_o_t, w_pw_t = w_o_blocks.reshape(D, CC), w_pw_blocks.reshape(D, CC)
    b_gate_f, conv_w_f = _blocks_to_cols(bg_blocks), _blocks_to_cols(cw_blocks)
    cpre, u3 = conv_fwd(proj, conv_w_f, w["conv_b"], w["conv_ln_w"], w["conv_ln_b"])
    x1, z, ya, yb = mix_out(x2, proj, b_gate_f, attn, u3, w_o_t, w_pw_t, w_out_f)
    (h2, gu, f), ((w_ffn_out_blocks,),) = ffn_in(x1, w["norm2_w"], w_ffn_in_t, sides=(ag_blocks_relay(w["w_ffn_out"], BF16),))
    w_ffn_out_f = w_ffn_out_blocks.reshape(FF, D)
    dy, dyb, sq = ffn_out_loss(x1, f, w_ffn_out_f, target)

    g = {}
    def blocks(name, pairs, tm):
        return [t.reshape(NDEV, t.shape[0] // NDEV, t.shape[1]) for t in mm_tn(name, pairs, tm)]

    g_ffn_out, gb_ffn_out = blocks("gw_ffn_out", [(f, dyb)], FF // 2)
    (d_gu, d_x1, d_x1b, g["norm2_w"]), ((ra_ffn_out,),) = ffn_bwd(
        dy, dyb, gu, x1, w["norm2_w"], w_ffn_in_t, w_ffn_out_f, sides=(rs_to_sibling([gb_ffn_out]),))
    g_ffn_in, gb_ffn_in = blocks("gw_ffn_in", [(d_gu, h2)], FF // 2)
    (d_ya, d_yb, d_gl, d_attn, d_u3, g["b_gate"]), ((ra_ffn_in,),) = out_bwd(
        d_x1b, proj, b_gate_f, ya, yb, w_o_t, w_pw_t, w_out_f, sides=(rs_to_sibling([gb_ffn_in]),))
    g_out, gb_out, g_w_o, gb_w_o, g_w_pw, gb_w_pw = blocks(
        "gw_out_o_pw", [(z, d_x1b), (d_ya, attn), (d_yb, u3)], D // 2)
    (d_conv, g["conv_w"], g["conv_b"], g["conv_ln_w"], g["conv_ln_b"]), ((ra_out, ra_w_o, ra_w_pw),) = conv_bwd(
        proj, cpre, d_u3, conv_w_f, w["conv_ln_w"], w["conv_ln_b"],
        sides=(rs_to_sibling([gb_out, gb_w_o, gb_w_pw]),))
    (pb_ffn_out, own_ffn_out), (pb_ffn_in, own_ffn_in), (pb_out, own_out), (pb_w_o, own_w_o), (pb_w_pw, own_w_pw) = chip_sum(
        "chip_sum_early", [g_ffn_out, g_ffn_in, g_out, g_w_o, g_w_pw],
        [ra_ffn_out, ra_ffn_in, ra_out, ra_w_o, ra_w_pw], c_idx, chip_idx)
    (d_q, d_k, d_v, gqw, gkw), ((rb_ffn_out, rb_ffn_in, rb_out, rb_w_o, rb_w_pw),) = attn_bwd(
        proj, tabs, qw2, kw2, d_attn, attn, lse,
        sides=(rs_to_chips([pb_ffn_out, pb_ffn_in, pb_out, pb_w_o, pb_w_pw]),))
    g["q_norm_w"], g["k_norm_w"] = gqw, gkw
    d_segs = (d_q, d_k, d_v, d_conv, d_gl)
    parts, to_sibling, to_chips, owns, from_chips = [], None, None, [], []
    for k, hw in enumerate(GW_IN_SPLIT):
        sides = tuple(s for s in (to_chips, to_sibling) if s is not None)
        (part, part_b), outs = gw_in("gw_in_%d" % k, h, d_segs, sum(GW_IN_SPLIT[:k]), hw, sides=sides)
        outs = list(outs)
        if to_chips is not None:
            from_chips.append(outs.pop(0)[0])
        if to_sibling is not None:
            (pb, own), = chip_sum("chip_sum_w_in_%d" % (k - 1), [parts[-1]], [outs.pop(0)[0]], c_idx, chip_idx)
            owns.append(own)
            to_chips = rs_to_chips_combined(pb)
        else:
            to_chips = None
        parts.append(part.reshape(NDEV, INW // NDEV, hw))
        to_sibling = rs_to_sibling([part_b.reshape(NDEV, INW // NDEV, hw)])
    (grad_x, g["norm1_w"]), ((rb_prev,), (ra_last,)) = in_bwd(
        d_q, d_k, d_v, d_conv, d_gl, w_in_t, x2, d_x1, w["norm1_w"], sides=(to_chips, to_sibling))
    from_chips.append(rb_prev)
    (pb, own), = chip_sum("chip_sum_w_in_%d" % (len(GW_IN_SPLIT) - 1), [parts[-1]], [ra_last], c_idx, chip_idx)
    owns.append(own)
    small_sums, ((rb_last,),) = small_sync(g, sq, sides=(rs_to_chips_combined(pb),))
    small, loss = small_adam(small_sums, w, m, v, (4 * ax + 2 * ay + c_idx).astype(jnp.int32).reshape(1))
    from_chips.append(rb_last)

    adam_o, adam_pw, adam_out = block_adam("adam_w_o_pw_out", [
        (own_w_o, rb_w_o, w["w_o_attn"], m["w_o_attn"], v["w_o_attn"], True),
        (own_w_pw, rb_w_pw, w["w_pw_conv"], m["w_pw_conv"], v["w_pw_conv"], True),
        (own_out, rb_out, w["w_out"], m["w_out"], v["w_out"], False)])
    res = {
        "w_in": shard_adam("adam_w_in", owns, from_chips, w["w_in"], m["w_in"], v["w_in"]),
        "w_ffn_in": shard_adam("adam_w_ffn_in", [own_ffn_in], [rb_ffn_in], w["w_ffn_in"], m["w_ffn_in"], v["w_ffn_in"]),
        "w_o_attn": adam_o, "w_pw_conv": adam_pw, "w_out": adam_out,
        "w_ffn_out": shard_adam("adam_w_ffn_out", [own_ffn_out], [rb_ffn_out],
                                w["w_ffn_out"], m["w_ffn_out"], v["w_ffn_out"]),
    }
    res = {k: tuple(a.T if k in TRANSPOSED else a for a in r) for k, r in res.items()}
    res.update(small)

    def shaped(name, a):
        return a.reshape((1,) + a.shape) if name in MATS or name in ("b_gate", "conv_w") else a

    outs = [loss, grad_x.reshape(1, S, D)]
    for i in range(4):
        outs += [shaped(k, res[k][i]) for k in WEIGHTS]
    return tuple(outs)
```

```python
import functools
from typing import Callable, NamedTuple, Optional

import numpy as np
import jax
import jax.numpy as jnp
from jax import lax
from jax.experimental import pallas as pl
from jax.experimental.pallas import tpu as pltpu

F32 = jnp.float32
BF16 = jnp.bfloat16

S = 2048
D = 1024
HD = 64
QKV = 1536
CC = 512
KW = 31
FF = 2816
INW = 7680
OFF_Q, OFF_K, OFF_V, OFF_CA, OFF_CB, OFF_GA, OFF_GB = 0, 1536, 3072, 4608, 5120, 5632, 6656
DILATIONS = (1, 4, 16)
HALF_SPAN = 64
EPS = 1e-6
NEG_INF = -1e30
ROPE_THETA = 500000.0
ROT_DIM = 16

ADAM_LR = 0.001
ADAM_B1 = 0.9
ADAM_B2 = 0.999
ADAM_EPS = 1e-08
ADAM_WD = 0.01
ADAM_STEP = 10

NDEV = 8
LANES = 128
TM = 256
IN_PROJ_TM = 512
TQ = 128
VMEM_LIMIT = 56 * 1024 * 1024
MESH = pl.DeviceIdType.MESH


def _cp(**kw):
    return pltpu.CompilerParams(vmem_limit_bytes=VMEM_LIMIT, **kw)


def _row(width, col=0, tm=TM):
    return pl.BlockSpec((tm, width), lambda i: (i, col))


PLANE = 512


def _planes(width, tm=TM):
    return pl.BlockSpec((width // PLANE, tm, PLANE), lambda i: (0, i, 0))


def _res(shape):
    nd = len(shape)
    return pl.BlockSpec(shape, lambda *_: (0,) * nd, pipeline_mode=pl.Buffered(1))


def _dot(a, b):
    return jnp.dot(a, b, preferred_element_type=F32)


def _dot_nt(a, b):
    return lax.dot_general(a, b, (((1,), (1,)), ((), ())), preferred_element_type=F32)


def _dot_tn(a, b):
    return lax.dot_general(a, b, (((0,), (0,)), ((), ())), preferred_element_type=F32)


def _sigmoid(x):
    return jax.nn.sigmoid(x)


def _dsilu(x, sg):
    return sg * (1.0 + x * (1.0 - sg))


ANY = pl.BlockSpec(memory_space=pl.ANY)
VMEM = pl.BlockSpec(memory_space=pltpu.VMEM)


class Side(NamedTuple):
    args: tuple
    in_specs: tuple
    out_shape: tuple
    scratch: tuple
    start: Callable
    finish: Callable
    mid: Optional[Callable] = None
    peers: str = ""


BARRIER_IDS = {"s": 0, "dxy": 1, "dsxy": 2, "sxy": 3, "xy": 4}


def _peer_barrier(peers):
    x, y, c = lax.axis_index("x"), lax.axis_index("y"), lax.axis_index("c")
    where = {"s": (x, y, 1 - c), "x": (1 - x, y, c), "y": (x, 1 - y, c), "d": (1 - x, 1 - y, c)}
    barrier = pltpu.get_barrier_semaphore()
    for p in peers:
        pl.semaphore_signal(barrier, inc=1, device_id=where[p], device_id_type=MESH)
    pl.semaphore_wait(barrier, len(peers))


def _call(body, sides=(), *, name, grid, in_specs, out_specs, out_shape, scratch_shapes=(), args, own_comm=False,
          tail=None):
    assert tail is None or int(np.prod(grid)) == 1
    ni, no, ns = len(in_specs), len(out_specs), len(scratch_shapes)
    cnt = [(len(s.args), len(s.out_shape), len(s.scratch)) for s in sides]
    peers = "".join(sorted(set("".join(s.peers for s in sides))))
    if own_comm or not sides or any(not s.peers for s in sides):
        peers = ""

    def take(refs, pos, n):
        return refs[pos:pos + n], pos + n

    def full(*refs):
        m_in, pos = take(refs, 0, ni)
        s_in = []
        for a, _, _ in cnt:
            r, pos = take(refs, pos, a)
            s_in.append(r)
        m_out, pos = take(refs, pos, no)
        s_out = []
        for _, o, _ in cnt:
            r, pos = take(refs, pos, o)
            s_out.append(r)
        m_scr, pos = take(refs, pos, ns)
        s_scr = []
        for _, _, c in cnt:
            r, pos = take(refs, pos, c)
            s_scr.append(r)
        if sides:
            first = functools.reduce(jnp.logical_and, [pl.program_id(d) == 0 for d in range(len(grid))])
            last = functools.reduce(jnp.logical_and, [pl.program_id(d) == g - 1 for d, g in enumerate(grid)])

            @pl.when(first)
            def _():
                if peers:
                    _peer_barrier(peers)
                for s, a, o, c in zip(sides, s_in, s_out, s_scr):
                    s.start(a, o, c)

            steps = int(np.prod(grid))
            mid_step = (2 * steps) // 3
            if steps > 1 and any(s.mid is not None for s in sides):
                step = functools.reduce(lambda acc, d: acc * grid[d] + pl.program_id(d), range(len(grid)), 0)

                @pl.when(step == mid_step)
                def _():
                    for s, a, o, c in zip(sides, s_in, s_out, s_scr):
                        if s.mid is not None:
                            s.mid(a, o, c)

        body(*m_in, *m_out, *m_scr)
        if sides:
            @pl.when(last)
            def _():
                for s, a, o, c in zip(sides, s_in, s_out, s_scr):
                    if s.mid is not None and steps == 1:
                        s.mid(a, o, c)
                if tail is not None:
                    tail(*m_in, *m_out, *m_scr)
                for s, a, o, c in zip(sides, s_in, s_out, s_scr):
                    s.finish(a, o, c)
        elif tail is not None:
            tail(*m_in, *m_out, *m_scr)

    res = pl.pallas_call(
        full, name=name, grid=grid,
        in_specs=list(in_specs) + [sp for s in sides for sp in s.in_specs],
        out_specs=list(out_specs) + [ANY for s in sides for _ in s.out_shape],
        out_shape=list(out_shape) + [o for s in sides for o in s.out_shape],
        scratch_shapes=list(scratch_shapes) + [c for s in sides for c in s.scratch],
        compiler_params=_cp(dimension_semantics=("arbitrary",) * len(grid),
                            **({"collective_id": BARRIER_IDS[peers]} if peers else {})),
    )(*args, *[a for s in sides for a in s.args])
    res = list(res)
    if not sides:
        return res
    outs, pos = take(res, 0, no)
    side_outs = []
    for _, o, _ in cnt:
        r, pos = take(res, pos, o)
        side_outs.append(r)
    return outs, side_outs


def _inv_freq_lanes():
    inv = np.float32(ROPE_THETA) ** (-np.arange(0, ROT_DIM, 2, dtype=np.float32) / np.float32(ROT_DIM))
    lane = np.arange(LANES) % HD
    out = np.where(lane < ROT_DIM, inv[lane % (ROT_DIM // 2)], 0.0).astype(np.float32)
    return jnp.asarray(out.reshape(1, LANES))


def _rope_tables(pos, inv_freq):
    ang = pos.astype(F32) * inv_freq
    lane = lax.broadcasted_iota(jnp.int32, ang.shape, 1) % HD
    cs = jnp.cos(ang)
    sn = jnp.sin(ang)
    return (jnp.where(lane < ROT_DIM, cs, 1.0), jnp.where(lane < ROT_DIM // 2, -sn, 0.0),
            jnp.where(lane < ROT_DIM // 2, 0.0, jnp.where(lane < ROT_DIM, sn, 0.0)))


def _rope(v, c, s1, s2):
    return v * c + pltpu.roll(v, LANES - 8, axis=1) * s1 + pltpu.roll(v, 8, axis=1) * s2


def _rope_t(d, c, s1, s2):
    return d * c - pltpu.roll(d, LANES - 8, axis=1) * s1 - pltpu.roll(d, 8, axis=1) * s2


def _head_mat():
    r = lax.broadcasted_iota(jnp.int32, (LANES, LANES), 0) // HD
    c = lax.broadcasted_iota(jnp.int32, (LANES, LANES), 1) // HD
    return jnp.where(r == c, 1.0 / HD, 0.0).astype(BF16)


def _head_mean(t, e):
    hi = t.astype(BF16)
    rest = (t - hi.astype(F32)).astype(BF16)
    return _dot(hi, e) + _dot(rest, e)


def in_proj_gather(x, norm_w, shard_t, chip_order, pos_col):
    R = INW // NDEV
    tm = IN_PROJ_TM
    half, nt = R // 2, S // tm

    def body(ord_ref, x_ref, nw_ref, sh_ref, pos_ref, f_ref, h_ref, p_ref, wfull_ref, c_ref, s1_ref, s2_ref,
             wt, hs, send, recv, loc):
        kk, i = pl.program_id(0), pl.program_id(1)
        x, y, c, _ = _place()
        me, flip = 4 * x + 2 * y + c, 1 - 2 * c
        here, sib, xn, yn = (x, y, c), (x, y, 1 - c), (1 - x, y, c), (x, 1 - y, c)
        b_xn, b_yn, b_dg = 4 * (1 - x) + 2 * y + c, 4 * x + 2 * (1 - y) + c, 4 * (1 - x) + 2 * (1 - y) + c

        def cp(k, block, to, rows=None):
            dst = wt.at[block] if rows is None else wt.at[block, pl.ds(rows * half, half), :]
            return _remote(dst, dst, send, recv, k, to)

        def sends():
            return [cp(0, me, sib), cp(1, me, xn), cp(2, me, yn), cp(3, b_xn, sib), cp(4, b_yn, sib),
                    cp(5, b_xn, yn, rows=0), cp(6, b_yn, xn, rows=1), cp(7, b_dg, sib, rows=0), cp(8, b_dg, sib, rows=1)]

        def keep(j, blk0):
            pair = pl.ds(pl.multiple_of(blk0, 2), 2)
            return pltpu.make_async_copy(wt.at[pair], wfull_ref.at[pair], loc.at[j])

        @pl.when((kk == 0) & (i == 0))
        def _():
            _peer_barrier("sxy")
            _cast_rows(wt.at[me], sh_ref)
            for s_ in sends()[0:3]:
                s_.start()

            def tables(j, _):
                chunk = pl.ds(pl.multiple_of(j * TM, TM), TM)
                c_ref[chunk, :], s1_ref[chunk, :], s2_ref[chunk, :] = _rope_tables(pos_ref[chunk, :], f_ref[...])
                return 0

            lax.fori_loop(0, S // TM, tables, 0)
            cp(0, me + flip, here).wait_recv()
            keep(0, me - c).start()

        @pl.when((kk == 1) & (i == 0))
        def _():
            cp(1, b_xn, here).wait_recv()
            sends()[5].start()
            sends()[3].start()
            cp(2, b_yn, here).wait_recv()
            sends()[6].start()
            sends()[4].start()
            cp(3, b_xn + flip, here).wait_recv()
            keep(1, b_xn - c).start()

        @pl.when((kk == 2) & (i == 0))
        def _():
            cp(4, b_yn + flip, here).wait_recv()
            keep(2, b_yn - c).start()

        @pl.when((kk == 3) & (i == 0))
        def _():
            cp(5, b_dg, here, rows=0).wait_recv()
            sends()[7].start()
            cp(6, b_dg, here, rows=1).wait_recv()
            sends()[8].start()
            cp(7, b_dg + flip, here, rows=0).wait_recv()
            cp(8, b_dg + flip, here, rows=1).wait_recv()
            keep(3, b_dg - c).start()

        rows = pl.ds(pl.multiple_of(i * tm, tm), tm)

        @pl.when(kk == 0)
        def _():
            xv = x_ref[...]
            r = lax.rsqrt(jnp.mean(xv * xv, axis=-1, keepdims=True) + EPS)
            hb = (xv * r * nw_ref[...]).astype(BF16)
            h_ref[...] = hb
            hs[rows, :] = hb

        h = hs[rows, :]
        chip = ord_ref[kk]
        for cc in range(2):
            p_ref[:, cc * R:(cc + 1) * R] = _dot_nt(h, wt[2 * chip + cc])

        @pl.when((kk == 3) & (i == nt - 1))
        def _():
            for s_ in sends():
                s_.wait_send()
            for j, blk in enumerate((me, b_xn, b_yn, b_dg)):
                keep(j, blk - c).wait()

    def first_pass(kk, i):
        return jnp.where(kk == 0, i, nt - 1)

    grid_spec = pltpu.PrefetchScalarGridSpec(
        num_scalar_prefetch=1, grid=(4, nt),
        in_specs=[pl.BlockSpec((tm, D), lambda kk, i, o: (first_pass(kk, i), 0)),
                  pl.BlockSpec((1, D), lambda kk, i, o: (0, 0)), VMEM, VMEM,
                  pl.BlockSpec((1, LANES), lambda kk, i, o: (0, 0))],
        out_specs=[pl.BlockSpec((tm, D), lambda kk, i, o: (first_pass(kk, i), 0)),
                   pl.BlockSpec((tm, 2 * R), lambda kk, i, o: (i, o[kk])), ANY]
        + [pl.BlockSpec((S, LANES), lambda kk, i, o: (0, 0))] * 3,
        scratch_shapes=[pltpu.VMEM((NDEV, R, D), BF16), pltpu.VMEM((S, D), BF16), _sems(9), _sems(9), _sems(4)])
    res = pl.pallas_call(
        body, name="in_proj_gather", grid_spec=grid_spec,
        out_shape=[jax.ShapeDtypeStruct((S, D), BF16), jax.ShapeDtypeStruct((S, INW), F32),
                   jax.ShapeDtypeStruct((NDEV, R, D), BF16)] + [jax.ShapeDtypeStruct((S, LANES), F32)] * 3,
        compiler_params=_cp(dimension_semantics=("arbitrary", "arbitrary"), collective_id=BARRIER_IDS["sxy"]),
    )(chip_order, x, norm_w, shard_t, pos_col, _inv_freq_lanes())
    return res[0], res[1], res[2], tuple(res[3:])


def _qk_specs():
    nb = QKV // LANES
    return [pl.BlockSpec((S, LANES), functools.partial(lambda hp, g, o: (0, o + g * 4 + hp), o=o))
            for o in (OFF_Q // LANES, OFF_K // LANES, OFF_V // LANES)]


def _tab_specs():
    return [pl.BlockSpec((S, LANES), lambda hp, g: (0, 0), pipeline_mode=pl.Buffered(1))] * 3


def _vec_spec():
    return pl.BlockSpec((1, LANES), lambda hp, g: (0, 0))


def _sub_rows(r, d, start, n):
    if d == 1:
        return pl.ds(start, n)
    return pl.ds(r + d * start, n, stride=d)


def _band_window(i, L):
    W = min(TQ + 2 * HALF_SPAN, L)
    q0 = pl.multiple_of(i * TQ, TQ)
    k0 = pl.multiple_of(jnp.clip(q0 - HALF_SPAN, 0, L - W), HALF_SPAN)
    qpos = q0 + (lax.broadcasted_iota(jnp.int32, (2 * TQ, W), 0) & (TQ - 1))
    kpos = k0 + lax.broadcasted_iota(jnp.int32, (2 * TQ, W), 1)
    valid = jnp.abs(qpos - kpos) <= HALF_SPAN
    return W, q0, k0, valid


def _stack_heads(t, lo):
    z = jnp.zeros_like(t)
    return jnp.concatenate([jnp.where(lo, t, z), jnp.where(lo, z, t)], axis=0)


def _unstack_heads(t2, lo):
    return jnp.where(lo, t2[0:TQ], t2[TQ:2 * TQ])


CHAINS = 8


def _interleave(d):
    ru = min(d, CHAINS)
    return ru, min(CHAINS // ru, S // d // TQ)


def _for_blocks(n, fn):
    if n == 1:
        fn(0)
    else:
        def it(j, _):
            fn(j)
            return 0
        lax.fori_loop(0, n, it, 0)


def attn_fwd(proj, tabs, qw2, kw2, sides=()):
    CH = 256

    def body(q_ref, k_ref, v_ref, c_ref, s1_ref, s2_ref, qw_ref, kw_ref, at_ref, ls_ref,
             qs, ks, vs, osub, lsub, onat, lnat, qn, kn):
        g = pl.program_id(1)
        lo = lax.broadcasted_iota(jnp.int32, (1, LANES), 1) < HD
        e = _head_mat()

        def prep(i, _):
            rows = pl.ds(pl.multiple_of(i * CH, CH), CH)
            c, s1, s2 = c_ref[rows, :], s1_ref[rows, :], s2_ref[rows, :]
            for t_ref, w_ref, out, scale in ((q_ref, qw_ref, qn, HD ** -0.5), (k_ref, kw_ref, kn, 1.0)):
                t = t_ref[rows, :]
                r = lax.rsqrt(_head_mean(t * t, e) + EPS)
                out[rows, :] = _rope(t * r * w_ref[...], c, s1, s2) * scale
            return 0

        lax.fori_loop(0, S // CH, prep, 0, unroll=4)

        def group(gi, d):
            L = S // d

            ru, nb = _interleave(d)

            def stage(r, off):
                for c0 in range(0, L, CH):
                    n = min(CH, L)
                    rows = _sub_rows(r, d, c0, n)
                    dst = pl.ds(off + c0, n)
                    qs[dst, :] = qn[rows, :].astype(BF16)
                    ks[dst, :] = kn[rows, :].astype(BF16)
                    vs[dst, :] = v_ref[rows, :].astype(BF16)

            def one(off, i):
                W, q0, k0, valid = _band_window(i, L)
                q2 = _stack_heads(qs[pl.ds(off + q0, TQ), :], lo)
                sc = jnp.where(valid, _dot_nt(q2, ks[pl.ds(off + k0, W), :]), NEG_INF)
                m = jnp.max(sc, axis=-1, keepdims=True)
                p = jnp.exp(sc - m)
                den = jnp.sum(p, axis=-1, keepdims=True)
                o2 = _dot(p.astype(BF16), vs[pl.ds(off + k0, W), :]) / den
                l2 = jnp.broadcast_to(m + jnp.log(den), (2 * TQ, LANES))
                osub[pl.ds(off + q0, TQ), :] = _unstack_heads(o2, lo)
                lsub[pl.ds(off + q0, TQ), :] = _unstack_heads(l2, lo)

            def unstage(r, off):
                for c0 in range(0, L, CH):
                    n = min(CH, L)
                    rows = _sub_rows(r, d, c0, n)
                    onat[gi, rows, :] = osub[pl.ds(off + c0, n), :]
                    lnat[gi, rows, :] = lsub[pl.ds(off + c0, n), :]

            def step(t, _):
                for u in range(ru):
                    stage(t * ru + u, u * L)
                _for_blocks(L // TQ // nb, lambda j: [one(u * L, j * nb + b) for u in range(ru) for b in range(nb)])
                for u in range(ru):
                    unstage(t * ru + u, u * L)
                return 0

            lax.fori_loop(0, d // ru, step, 0)

        for gi, d in enumerate(DILATIONS):
            pl.when(g == gi)(functools.partial(group, gi, d))

        @pl.when(g == len(DILATIONS) - 1)
        def _():
            def mix(i, _):
                rows = pl.ds(pl.multiple_of(i * CH, CH), CH)
                l0, l1, l2 = lnat[0, rows, :], lnat[1, rows, :], lnat[2, rows, :]
                m = jnp.maximum(jnp.maximum(l0, l1), l2)
                e0, e1, e2 = jnp.exp(l0 - m), jnp.exp(l1 - m), jnp.exp(l2 - m)
                den = e0 + e1 + e2
                a = (e0 * onat[0, rows, :] + e1 * onat[1, rows, :] + e2 * onat[2, rows, :]) / den
                at_ref[rows, :] = a.astype(BF16)
                ls_ref[rows, :] = m + jnp.log(den)
                return 0

            lax.fori_loop(0, S // CH, mix, 0)

    out_spec = pl.BlockSpec((S, LANES), lambda hp, g: (0, hp))
    return _call(
        body, sides, name="attn_fwd", grid=(4, 3),
        in_specs=_qk_specs() + _tab_specs() + [_vec_spec(), _vec_spec()],
        out_specs=[out_spec, out_spec],
        out_shape=[jax.ShapeDtypeStruct((S, CC), BF16), jax.ShapeDtypeStruct((S, CC), F32)],
        scratch_shapes=[pltpu.VMEM((S, LANES), BF16)] * 3 + [pltpu.VMEM((S, LANES), F32)] * 2
        + [pltpu.VMEM((3, S, LANES), F32)] * 2 + [pltpu.VMEM((S, LANES), F32)] * 2,
        args=(proj, proj, proj, *tabs, qw2, kw2))


def attn_bwd(proj, tabs, qw2, kw2, d_attn, attn, lse, sides=()):
    CH = 256

    def body(q_ref, k_ref, v_ref, c_ref, s1_ref, s2_ref, qw_ref, kw_ref, do_ref, at_ref, ls_ref,
             dq_ref, dk_ref, dv_ref, gqw_ref, gkw_ref,
             qs, ks, vs, dos, dsub, lsub, dqs, dks, dvs, dnat, qx, kx, dvn, tnq, tnk, rrq, rrk):
        hp, g = pl.program_id(0), pl.program_id(1)
        lo = lax.broadcasted_iota(jnp.int32, (1, LANES), 1) < HD
        e = _head_mat()
        both = ((q_ref, qw_ref, qx, tnq, rrq, HD ** -0.5), (k_ref, kw_ref, kx, tnk, rrk, 1.0))

        @pl.when((hp == 0) & (g == 0))
        def _():
            gqw_ref[...] = jnp.zeros_like(gqw_ref)
            gkw_ref[...] = jnp.zeros_like(gkw_ref)

        def prep(i, _):
            rows = pl.ds(pl.multiple_of(i * CH, CH), CH)
            dnat[rows, :] = _head_mean(do_ref[rows, :] * at_ref[rows, :].astype(F32), e) * float(HD)
            c, s1, s2 = c_ref[rows, :], s1_ref[rows, :], s2_ref[rows, :]
            for t_ref, w_ref, x, tn_s, rr_s, scale in both:
                t = t_ref[rows, :]
                rr = lax.rsqrt(_head_mean(t * t, e) + EPS)
                tn = t * rr
                rr_s[rows, :] = rr
                tn_s[rows, :] = tn
                x[rows, :] = _rope(tn * w_ref[...], c, s1, s2) * scale
            return 0

        lax.fori_loop(0, S // CH, prep, 0, unroll=4)

        def group(d):
            L = S // d

            ru, nb = _interleave(d)

            def stage(r, off):
                for c0 in range(0, L, CH):
                    n = min(CH, L)
                    rows = _sub_rows(r, d, c0, n)
                    dst = pl.ds(off + c0, n)
                    qs[dst, :] = qx[rows, :].astype(BF16)
                    ks[dst, :] = kx[rows, :].astype(BF16)
                    vs[dst, :] = v_ref[rows, :].astype(BF16)
                    dos[dst, :] = do_ref[rows, :].astype(BF16)
                    dsub[dst, :] = dnat[rows, :]
                    lsub[dst, :] = ls_ref[rows, :]
                    dks[dst, :] = jnp.zeros((n, LANES), F32)
                    dvs[dst, :] = jnp.zeros((n, LANES), F32)

            def one(off, i):
                W, q0, k0, valid = _band_window(i, L)
                qrows, krows = pl.ds(off + q0, TQ), pl.ds(off + k0, W)
                q2 = _stack_heads(qs[qrows, :], lo)
                do2 = _stack_heads(dos[qrows, :], lo)
                kk, vv = ks[krows, :], vs[krows, :]
                lse_b, dd_b = lsub[qrows, :], dsub[qrows, :]
                lse2 = jnp.concatenate([lse_b[:, 0:1], lse_b[:, HD:HD + 1]], axis=0)
                dd2 = jnp.concatenate([dd_b[:, 0:1], dd_b[:, HD:HD + 1]], axis=0)
                sc = jnp.where(valid, _dot_nt(q2, kk), NEG_INF)
                p = jnp.exp(sc - lse2)
                ds = (p * (_dot_nt(do2, vv) - dd2)).astype(BF16)
                dqs[qrows, :] = _unstack_heads(_dot(ds, kk), lo)
                dks[krows, :] = dks[krows, :] + _dot_tn(ds, q2)
                dvs[krows, :] = dvs[krows, :] + _dot_tn(p.astype(BF16), do2)

            def unstage(r, off):
                for c0 in range(0, L, CH):
                    n = min(CH, L)
                    rows = _sub_rows(r, d, c0, n)
                    src = pl.ds(off + c0, n)
                    qx[rows, :] = dqs[src, :]
                    kx[rows, :] = dks[src, :]
                    dvn[rows, :] = dvs[src, :]

            def step(t, _):
                for u in range(ru):
                    stage(t * ru + u, u * L)
                _for_blocks(L // TQ // nb, lambda j: [one(u * L, j * nb + b) for u in range(ru) for b in range(nb)])
                for u in range(ru):
                    unstage(t * ru + u, u * L)
                return 0

            lax.fori_loop(0, d // ru, step, 0)

        for gi, d in enumerate(DILATIONS):
            pl.when(g == gi)(functools.partial(group, d))

        def emit(i, _):
            rows = pl.ds(pl.multiple_of(i * CH, CH), CH)
            c, s1, s2 = c_ref[rows, :], s1_ref[rows, :], s2_ref[rows, :]
            for (_, w_ref, x, tn_s, rr_s, scale), out, gw_ref in zip(both, (dq_ref, dk_ref), (gqw_ref, gkw_ref)):
                tn = tn_s[rows, :]
                dy = _rope_t(x[rows, :] * scale, c, s1, s2)
                gw_ref[0:1, :] = gw_ref[0:1, :] + jnp.sum(dy * tn, axis=0, keepdims=True)
                dtn = dy * w_ref[...]
                out[rows, :] = (rr_s[rows, :] * (dtn - tn * _head_mean(dtn * tn, e))).astype(BF16)
            dv_ref[rows, :] = dvn[rows, :].astype(BF16)
            return 0

        lax.fori_loop(0, S // CH, emit, 0, unroll=4)

    nat_spec = pl.BlockSpec((S, LANES), lambda hp, g: (0, hp))
    out_spec = pl.BlockSpec((None, S, LANES), lambda hp, g: (g, 0, hp))
    acc_spec = pl.BlockSpec((8, LANES), lambda hp, g: (0, 0))
    return _call(
        body, sides, name="attn_bwd", grid=(4, 3),
        in_specs=_qk_specs() + _tab_specs() + [_vec_spec(), _vec_spec(), nat_spec, nat_spec, nat_spec],
        out_specs=[out_spec] * 3 + [acc_spec] * 2,
        out_shape=[jax.ShapeDtypeStruct((QKV // PLANE, S, PLANE), BF16)] * 3 + [jax.ShapeDtypeStruct((8, LANES), F32)] * 2,
        scratch_shapes=[pltpu.VMEM((S, LANES), BF16)] * 4 + [pltpu.VMEM((S, LANES), F32)] * 13,
        args=(proj, proj, proj, *tabs, qw2, kw2, d_attn, attn, lse))


PADR = 16
CT = 128


def _conv_specs():
    return [pl.BlockSpec((S, CC), lambda i: (0, OFF_CA // CC)), pl.BlockSpec((S, CC), lambda i: (0, OFF_CB // CC))]


NCB = CC // LANES


def _pad_zero(pad):
    for cb in range(NCB):
        pad[cb, 0:PADR, :] = jnp.zeros((PADR, LANES), F32)
        pad[cb, PADR + S:PADR + S + PADR, :] = jnp.zeros((PADR, LANES), F32)


def _pad_store(pad, row0, n, val):
    for cb in range(NCB):
        pad[cb, pl.ds(pl.multiple_of(row0 + PADR, 8), n), :] = val[:, cb * LANES:(cb + 1) * LANES]


def _taps(pad_ref, cb, s0, weights):
    acc = jnp.zeros((CT, LANES), F32)
    for k in range(KW):
        acc = acc + weights[k] * pad_ref[cb, pl.ds(s0 + k + 1, CT), :]
    return acc


def conv_fwd(proj, conv_w, conv_b, ln_w, ln_b):
    def body(a_ref, b_ref, w_ref, cb_ref, lw_ref, lb_ref, c_ref, u3_ref, upad):
        _pad_zero(upad)

        def glu(i, _):
            rows = pl.ds(pl.multiple_of(i * TM, TM), TM)
            _pad_store(upad, i * TM, TM, a_ref[rows, :] * _sigmoid(b_ref[rows, :]))
            return 0

        lax.fori_loop(0, S // TM, glu, 0)

        def chunk(i, _):
            s0 = pl.multiple_of(i * CT, CT)
            for cb in range(CC // LANES):
                cols = slice(cb * LANES, (cb + 1) * LANES)
                w = [w_ref[k:k + 1, cols] for k in range(KW)]
                c_ref[pl.ds(s0, CT), cols] = _taps(upad, cb, s0, w) + cb_ref[:, cols]
            cv = c_ref[pl.ds(s0, CT), :]
            mu = jnp.mean(cv, axis=-1, keepdims=True)
            xc = cv - mu
            rstd = lax.rsqrt(jnp.mean(xc * xc, axis=-1, keepdims=True) + EPS)
            yl = xc * rstd * lw_ref[...] + lb_ref[...]
            u3_ref[pl.ds(s0, CT), :] = (yl * _sigmoid(yl)).astype(BF16)
            return 0

        lax.fori_loop(0, S // CT, chunk, 0)

    vec = pl.BlockSpec((1, CC), lambda i: (0, 0))
    full = pl.BlockSpec((S, CC), lambda i: (0, 0))
    return _call(
        body, name="conv_fwd", grid=(1,),
        in_specs=_conv_specs() + [pl.BlockSpec((KW, CC), lambda i: (0, 0)), vec, vec, vec],
        out_specs=[full, full],
        out_shape=[jax.ShapeDtypeStruct((S, CC), F32), jax.ShapeDtypeStruct((S, CC), BF16)],
        scratch_shapes=[pltpu.VMEM((NCB, S + 2 * PADR, LANES), F32)],
        args=(proj, proj, conv_w, conv_b, ln_w, ln_b))


def conv_bwd(proj, cpre, d_u3, conv_w, ln_w, ln_b, sides=()):
    def body(a_ref, b_ref, c_ref, du3_ref, w_ref, lw_ref, lb_ref,
             dc_ref, gw_ref, gcb_ref, glw_ref, glb_ref, upad, dpad):
        _pad_zero(upad)
        _pad_zero(dpad)
        gw_ref[...] = jnp.zeros_like(gw_ref)

        def ln_bwd(i, carry):
            gcb, glw, glb = carry
            rows = pl.ds(pl.multiple_of(i * TM, TM), TM)
            _pad_store(upad, i * TM, TM, a_ref[rows, :] * _sigmoid(b_ref[rows, :]))
            cv = c_ref[rows, :]
            mu = jnp.mean(cv, axis=-1, keepdims=True)
            xc = cv - mu
            rstd = lax.rsqrt(jnp.mean(xc * xc, axis=-1, keepdims=True) + EPS)
            xh = xc * rstd
            yl = xh * lw_ref[...] + lb_ref[...]
            dyl = du3_ref[rows, :] * _dsilu(yl, _sigmoid(yl))
            dxh = dyl * lw_ref[...]
            dcv = rstd * (dxh - jnp.mean(dxh, axis=-1, keepdims=True)
                          - xh * jnp.mean(dxh * xh, axis=-1, keepdims=True))
            _pad_store(dpad, i * TM, TM, dcv)
            return (gcb + jnp.sum(dcv, axis=0, keepdims=True),
                    glw + jnp.sum(dyl * xh, axis=0, keepdims=True),
                    glb + jnp.sum(dyl, axis=0, keepdims=True))

        z = jnp.zeros((1, CC), F32)
        gcb, glw, glb = lax.fori_loop(0, S // TM, ln_bwd, (z, z, z))
        gcb_ref[...] = gcb
        glw_ref[...] = glw
        glb_ref[...] = glb

        def chunk(i, _):
            s0 = pl.multiple_of(i * CT, CT)
            for cb in range(CC // LANES):
                cols = slice(cb * LANES, (cb + 1) * LANES)
                wr = [w_ref[KW - 1 - k:KW - k, cols] for k in range(KW)]
                du = _taps(dpad, cb, s0, wr)
                dcv = dpad[cb, pl.ds(s0 + PADR, CT), :]
                for k in range(KW):
                    gw_ref[k:k + 1, cols] = gw_ref[k:k + 1, cols] + jnp.sum(
                        upad[cb, pl.ds(s0 + k + 1, CT), :] * dcv, axis=0, keepdims=True)
                av = a_ref[pl.ds(s0, CT), cols]
                sb = _sigmoid(b_ref[pl.ds(s0, CT), cols])
                dc_ref[0, pl.ds(s0, CT), cols] = (du * sb).astype(BF16)
                dc_ref[1, pl.ds(s0, CT), cols] = (du * av * sb * (1.0 - sb)).astype(BF16)
            return 0

        lax.fori_loop(0, S // CT, chunk, 0)

    vec = pl.BlockSpec((1, CC), lambda i: (0, 0))
    full = pl.BlockSpec((S, CC), lambda i: (0, 0))
    wsp = pl.BlockSpec((KW, CC), lambda i: (0, 0))
    return _call(
        body, sides, name="conv_bwd", grid=(1,),
        in_specs=_conv_specs() + [full, full, wsp, vec, vec],
        out_specs=[pl.BlockSpec((2, S, CC), lambda i: (0, 0, 0)), wsp, vec, vec, vec],
        out_shape=[jax.ShapeDtypeStruct((2, S, CC), BF16), jax.ShapeDtypeStruct((KW, CC), F32)]
        + [jax.ShapeDtypeStruct((1, CC), F32)] * 3,
        scratch_shapes=[pltpu.VMEM((NCB, S + 2 * PADR, LANES), F32)] * 2,
        args=(proj, proj, cpre, d_u3, conv_w, ln_w, ln_b))


def _gate_specs():
    return [_row(CC, col=OFF_GA // CC + j) for j in range(4)]


def _gates(g_refs, bg_ref):
    ga = _sigmoid(jnp.concatenate([g_refs[0][...], g_refs[1][...]], axis=1) + bg_ref[0:1, :])
    gb = _sigmoid(jnp.concatenate([g_refs[2][...], g_refs[3][...]], axis=1) + bg_ref[1:2, :])
    return ga, gb


def mix_out(x, proj, b_gate, attn, u3, w_o, w_pw, w_out):
    def body(x_ref, g0, g1, g2, g3, bg_ref, at_ref, u3_ref, wo_ref, wp_ref, wout_ref,
             x1_ref, z_ref, ya_ref, yb_ref):
        ga, gb = _gates((g0, g1, g2, g3), bg_ref)
        ya = _dot_nt(at_ref[...], wo_ref[...])
        yb = _dot_nt(u3_ref[...], wp_ref[...])
        z = (ga * ya + gb * yb).astype(BF16)
        ya_ref[...] = ya.astype(BF16)
        yb_ref[...] = yb.astype(BF16)
        z_ref[...] = z
        x1_ref[...] = x_ref[...] + _dot(z, wout_ref[...])

    return pl.pallas_call(
        body, name="mix_out", grid=(S // TM,),
        in_specs=[_row(D)] + _gate_specs() + [_res((2, D)), _row(CC), _row(CC),
                                              _res((D, CC)), _res((D, CC)), _res((D, D))],
        out_specs=[_row(D)] * 4,
        out_shape=[jax.ShapeDtypeStruct((S, D), F32)] + [jax.ShapeDtypeStruct((S, D), BF16)] * 3,
        compiler_params=_cp(dimension_semantics=("arbitrary",)),
    )(x, proj, proj, proj, proj, b_gate, attn, u3, w_o, w_pw, w_out)


def out_bwd(d_x1b, proj, b_gate, ya, yb, w_o, w_pw, w_out, sides=()):
    def body(dx_ref, g0, g1, g2, g3, bg_ref, ya_ref, yb_ref, wo_ref, wp_ref, wout_ref,
             dya_ref, dyb_ref, dgl_ref, dat_ref, du3_ref, gbg_ref):
        @pl.when(pl.program_id(0) == 0)
        def _():
            gbg_ref[...] = jnp.zeros_like(gbg_ref)

        ga, gb = _gates((g0, g1, g2, g3), bg_ref)
        dz = _dot_nt(dx_ref[...], wout_ref[...])
        dya = (dz * ga).astype(BF16)
        dyb = (dz * gb).astype(BF16)
        dgla = dz * ya_ref[...].astype(F32) * ga * (1.0 - ga)
        dglb = dz * yb_ref[...].astype(F32) * gb * (1.0 - gb)
        dya_ref[...] = dya
        dyb_ref[...] = dyb
        for j in range(2):
            dgl_ref[j] = dgla[:, j * PLANE:(j + 1) * PLANE].astype(BF16)
            dgl_ref[2 + j] = dglb[:, j * PLANE:(j + 1) * PLANE].astype(BF16)
        gbg_ref[0:1, :] = gbg_ref[0:1, :] + jnp.sum(dgla, axis=0, keepdims=True)
        gbg_ref[1:2, :] = gbg_ref[1:2, :] + jnp.sum(dglb, axis=0, keepdims=True)
        dat_ref[...] = _dot(dya, wo_ref[...])
        du3_ref[...] = _dot(dyb, wp_ref[...])

    return _call(
        body, sides, name="out_bwd", grid=(S // TM,),
        in_specs=[_row(D)] + _gate_specs() + [_res((2, D)), _row(D), _row(D),
                                              _res((D, CC)), _res((D, CC)), _res((D, D))],
        out_specs=[_row(D), _row(D), _planes(2 * D), _row(CC), _row(CC), pl.BlockSpec((2, D), lambda i: (0, 0))],
        out_shape=[jax.ShapeDtypeStruct((S, D), BF16)] * 2 + [jax.ShapeDtypeStruct((2 * D // PLANE, S, PLANE), BF16)]
        + [jax.ShapeDtypeStruct((S, CC), F32)] * 2 + [jax.ShapeDtypeStruct((2, D), F32)],
        args=(d_x1b, proj, proj, proj, proj, b_gate, ya, yb, w_o, w_pw, w_out))


def ffn_in(x1, norm_w, w_ffn_in, sides=()):
    half = FF // 2

    def body(x_ref, nw_ref, w_ref, h_ref, gu_ref, f_ref):
        xv = x_ref[...]
        r = lax.rsqrt(jnp.mean(xv * xv, axis=-1, keepdims=True) + EPS)
        h = (xv * r * nw_ref[...]).astype(BF16)
        h_ref[...] = h
        for j in range(2):
            gt = _dot_nt(h, w_ref[j * half:(j + 1) * half, :])
            up = _dot_nt(h, w_ref[FF + j * half:FF + (j + 1) * half, :])
            gu_ref[:, j * half:(j + 1) * half] = gt.astype(BF16)
            gu_ref[:, FF + j * half:FF + (j + 1) * half] = up.astype(BF16)
            f_ref[:, j * half:(j + 1) * half] = (gt * _sigmoid(gt) * up).astype(BF16)

    return _call(
        body, sides, name="ffn_in", grid=(S // TM,),
        in_specs=[_row(D), _res((1, D)), _res((2 * FF, D))],
        out_specs=[_row(D), _row(2 * FF), _row(FF)],
        out_shape=[jax.ShapeDtypeStruct((S, D), BF16), jax.ShapeDtypeStruct((S, 2 * FF), BF16),
                   jax.ShapeDtypeStruct((S, FF), BF16)],
        args=(x1, norm_w, w_ffn_in))


def ffn_out_loss(x1, f, w_ffn_out, target):
    def body(x_ref, f_ref, w_ref, t_ref, dy_ref, dyb_ref, sq_ref):
        @pl.when(pl.program_id(0) == 0)
        def _():
            sq_ref[...] = jnp.zeros_like(sq_ref)

        diff = x_ref[...] + _dot(f_ref[...], w_ref[...]) - t_ref[...]
        dy = diff * (1.0 / D)
        dy_ref[...] = dy
        dyb_ref[...] = dy.astype(BF16)
        sq_ref[...] = sq_ref[...] + jnp.sum((diff * diff).reshape(TM // 8, 8, D), axis=0)

    return pl.pallas_call(
        body, name="ffn_out_loss", grid=(S // TM,),
        in_specs=[_row(D), _row(FF), _res((FF, D)), _row(D)],
        out_specs=[_row(D), _row(D), pl.BlockSpec((8, D), lambda i: (0, 0))],
        out_shape=[jax.ShapeDtypeStruct((S, D), F32), jax.ShapeDtypeStruct((S, D), BF16),
                   jax.ShapeDtypeStruct((8, D), F32)],
        compiler_params=_cp(dimension_semantics=("arbitrary",)),
    )(x1, f, w_ffn_out, target)


def _rms_bwd(xv, nw, dh):
    r = lax.rsqrt(jnp.mean(xv * xv, axis=-1, keepdims=True) + EPS)
    xn = xv * r
    dxn = dh * nw
    dx = r * (dxn - xn * jnp.mean(dxn * xn, axis=-1, keepdims=True))
    return dx, dh * xn


def ffn_bwd(dy, dyb, gu, x1, norm_w, w_ffn_in, w_ffn_out, sides=()):
    def body(dy_ref, dyb_ref, gu_ref, x_ref, nw_ref, wi_ref, wo_ref, dgu_ref, dx_ref, dxb_ref, gn_ref):
        @pl.when(pl.program_id(0) == 0)
        def _():
            gn_ref[...] = jnp.zeros_like(gn_ref)

        df = _dot_nt(dyb_ref[...], wo_ref[...])
        gt = gu_ref[:, 0:FF].astype(F32)
        up = gu_ref[:, FF:2 * FF].astype(F32)
        sg = _sigmoid(gt)
        dgt = (df * up * _dsilu(gt, sg)).astype(BF16)
        dup = (df * gt * sg).astype(BF16)
        dgu_ref[:, 0:FF] = dgt
        dgu_ref[:, FF:2 * FF] = dup
        dh = _dot(dgt, wi_ref[0:FF, :]) + _dot(dup, wi_ref[FF:2 * FF, :])
        dxn, gw = _rms_bwd(x_ref[...], nw_ref[...], dh)
        dx = dy_ref[...] + dxn
        dx_ref[...] = dx
        dxb_ref[...] = dx.astype(BF16)
        gn_ref[...] = gn_ref[...] + jnp.sum(gw, axis=0, keepdims=True)

    return _call(
        body, sides, name="ffn_bwd", grid=(S // TM,),
        in_specs=[_row(D), _row(D), _row(2 * FF), _row(D), _res((1, D)), _res((2 * FF, D)), _res((FF, D))],
        out_specs=[_row(2 * FF), _row(D), _row(D), pl.BlockSpec((1, D), lambda i: (0, 0))],
        out_shape=[jax.ShapeDtypeStruct((S, 2 * FF), BF16), jax.ShapeDtypeStruct((S, D), F32),
                   jax.ShapeDtypeStruct((S, D), BF16), jax.ShapeDtypeStruct((1, D), F32)],
        args=(dy, dyb, gu, x1, norm_w, w_ffn_in, w_ffn_out))


def in_bwd(d_q, d_k, d_v, d_conv, d_gl, w_in, x, d_x1, norm_w, sides=()):
    segs = ((OFF_Q, QKV), (OFF_K, QKV), (OFF_V, QKV), (OFF_CA, 2 * CC), (OFF_GA, 2 * D))

    def body(dq_ref, dk_ref, dv_ref, dc_ref, dg_ref, w_ref, x_ref, dx1_ref, nw_ref, gx_ref, gn_ref):
        @pl.when(pl.program_id(0) == 0)
        def _():
            gn_ref[...] = jnp.zeros_like(gn_ref)

        dh = jnp.zeros((TM, D), F32)
        for ref, (off, width) in zip((dq_ref, dk_ref, dv_ref, dc_ref, dg_ref), segs):
            for j in range(width // PLANE):
                dh = dh + _dot(ref[j], w_ref[off + j * PLANE:off + (j + 1) * PLANE, :])
        dxn, gw = _rms_bwd(x_ref[...], nw_ref[...], dh)
        gx_ref[...] = dx1_ref[...] + dxn
        gn_ref[...] = gn_ref[...] + jnp.sum(gw, axis=0, keepdims=True)

    return _call(
        body, sides, name="in_bwd", grid=(S // TM,),
        in_specs=[_planes(QKV)] * 3 + [_planes(2 * CC), _planes(2 * D), _res((INW, D)), _row(D), _row(D), _res((1, D))],
        out_specs=[_row(D), pl.BlockSpec((1, D), lambda i: (0, 0))],
        out_shape=[jax.ShapeDtypeStruct((S, D), F32), jax.ShapeDtypeStruct((1, D), F32)],
        args=(d_q, d_k, d_v, d_conv, d_gl, w_in, x, d_x1, norm_w))


def mm_tn(name, pairs, tm):
    n = len(pairs)
    M = pairs[0][0].shape[1]
    widths = [b.shape[1] for _, b in pairs]

    def body(*refs):
        for a_ref, b_ref, o_ref, ob_ref in zip(refs[0:2 * n:2], refs[1:2 * n:2], refs[2 * n::2], refs[2 * n + 1::2]):
            r = _dot_tn(a_ref[...], b_ref[...])
            o_ref[...] = r
            ob_ref[...] = r.astype(BF16)

    return _call(
        body, name=name, grid=(M // tm,),
        in_specs=[sp for N in widths for sp in (pl.BlockSpec((S, tm), lambda i: (0, i)), _res((S, N)))],
        out_specs=[pl.BlockSpec((tm, N), lambda i: (i, 0)) for N in widths for _ in range(2)],
        out_shape=[jax.ShapeDtypeStruct((M, N), dt) for N in widths for dt in (F32, BF16)],
        args=[t for pair in pairs for t in pair])


GW_IN_TN = PLANE
GW_IN_SPLIT = (768, 256)


def gw_in(name, h, d_segs, col0, hw, sides=()):
    tn = GW_IN_TN
    starts, t0 = [], 0
    for seg in d_segs:
        starts.append(t0)
        t0 += seg.shape[0]
    ntiles = [seg.shape[0] for seg in d_segs]

    def body(h_ref, *refs):
        a_refs, o_ref, ob_ref = refs[:-2], refs[-2], refs[-1]
        n = pl.program_id(0)
        for a_ref, st, nt in zip(a_refs, starts, ntiles):
            @pl.when((n >= st) & (n < st + nt))
            def _(a_ref=a_ref):
                r = _dot_tn(a_ref[...], h_ref[...])
                o_ref[...] = r
                ob_ref[...] = r.astype(BF16)

    def seg_spec(st, nt):
        return pl.BlockSpec((None, S, tn), lambda n: (jnp.clip(n - st, 0, nt - 1), 0, 0))

    res = _call(
        body, sides, name=name, grid=(INW // tn,),
        in_specs=[pl.BlockSpec((S, hw), lambda n: (0, col0 // hw))] + [seg_spec(st, nt) for st, nt in zip(starts, ntiles)],
        out_specs=[pl.BlockSpec((tn, hw), lambda n: (n, 0))] * 2,
        out_shape=[jax.ShapeDtypeStruct((INW, hw), F32), jax.ShapeDtypeStruct((INW, hw), BF16)],
        args=(h, *d_segs))
    return (res[0], res[1]) if sides else (res, [])


def _place():
    x, y, c = lax.axis_index("x"), lax.axis_index("y"), lax.axis_index("c")
    chips = [(1 - x, y), (x, 1 - y), (1 - x, 1 - y)]
    return x, y, c, chips


def _sems(n):
    return pltpu.SemaphoreType.DMA((n,))


def _remote(src, dst, send, recv, k, to):
    return pltpu.make_async_remote_copy(src_ref=src, dst_ref=dst, send_sem=send.at[k], recv_sem=recv.at[k],
                                        device_id=to, device_id_type=MESH)


def _cast_rows(dst, src, cols=slice(None)):
    rows = src.shape[0]
    step = next((s for s in (128, 64, 32, 16) if rows % s == 0), rows)
    for r0 in range(0, rows, step):
        dst[r0:r0 + step, cols] = src[r0:r0 + step, :].astype(dst.dtype)


def comm_only(name, sides):
    def body():
        pass

    return _call(body, sides, name=name, grid=(1,), in_specs=[], out_specs=[], out_shape=[], args=())[1]


def ag_blocks(shard, dtype):
    R, W = shard.shape

    def copy(outs, scr, k, block, to, src=None):
        dst = outs[0].at[block]
        return _remote(dst if src is None else src, dst, scr[1], scr[2], k, to)

    def local(outs, scr, me):
        return pltpu.make_async_copy(scr[0], outs[0].at[me], scr[3].at[0])

    def start(ins, outs, scr):
        x, y, c, chips = _place()
        me = 4 * x + 2 * y + c
        _cast_rows(scr[0], ins[0])
        local(outs, scr, me).start()
        copy(outs, scr, 0, me, (x, y, 1 - c), src=scr[0]).start()
        for j, (cx, cy) in enumerate(chips):
            copy(outs, scr, 1 + j, me, (cx, cy, c), src=scr[0]).start()

    def finish(ins, outs, scr):
        x, y, c, chips = _place()
        me, sib = 4 * x + 2 * y + c, (x, y, 1 - c)
        passed = []
        for j, (cx, cy) in enumerate(chips):
            theirs = 4 * cx + 2 * cy + c
            copy(outs, scr, 1 + j, theirs, (x, y, c)).wait_recv()
            fwd = copy(outs, scr, 4 + j, theirs, sib)
            fwd.start()
            passed.append(fwd)
        copy(outs, scr, 0, 4 * x + 2 * y + 1 - c, (x, y, c)).wait_recv()
        for j, (cx, cy) in enumerate(chips):
            copy(outs, scr, 4 + j, 4 * cx + 2 * cy + 1 - c, (x, y, c)).wait_recv()
        copy(outs, scr, 0, me, sib, src=scr[0]).wait_send()
        for j, (cx, cy) in enumerate(chips):
            copy(outs, scr, 1 + j, me, (cx, cy, c), src=scr[0]).wait_send()
        for fwd in passed:
            fwd.wait_send()
        local(outs, scr, me).wait()

    return Side((shard,), (VMEM,), (jax.ShapeDtypeStruct((NDEV, R, W), dtype),),
                (pltpu.VMEM((R, W), dtype), _sems(7), _sems(7), _sems(1)), start, finish, None, "dsxy")


def ag_blocks_relay(shard, dtype, transpose=False):
    R, W = shard.shape[::-1] if transpose else shard.shape
    half = R // 2

    def copy(outs, scr, k, block, to, src=None, rows=None):
        dst = outs[0].at[block] if rows is None else outs[0].at[block, pl.ds(rows * half, half), :]
        return _remote(dst if src is None else src, dst, scr[1], scr[2], k, to)

    def local(outs, scr, me):
        return pltpu.make_async_copy(scr[0], outs[0].at[me], scr[3].at[0])

    def own(outs, scr):
        x, y, c, _ = _place()
        me = 4 * x + 2 * y + c
        return [copy(outs, scr, k, me, to, src=scr[0])
                for k, to in enumerate([(x, y, 1 - c), (1 - x, y, c), (x, 1 - y, c)])]

    def start(ins, outs, scr):
        x, y, c, _ = _place()
        if transpose:
            scr[0][...] = ins[0][...].T.astype(dtype)
        else:
            _cast_rows(scr[0], ins[0])
        local(outs, scr, 4 * x + 2 * y + c).start()
        for cp in own(outs, scr):
            cp.start()

    def passed_on(outs, scr):
        x, y, c, _ = _place()
        sib, xn, yn = (x, y, 1 - c), (1 - x, y, c), (x, 1 - y, c)
        b_xn, b_yn, b_dg = 4 * (1 - x) + 2 * y + c, 4 * x + 2 * (1 - y) + c, 4 * (1 - x) + 2 * (1 - y) + c
        near = [copy(outs, scr, 5, b_xn, yn, rows=0), copy(outs, scr, 3, b_xn, sib),
                copy(outs, scr, 6, b_yn, xn, rows=1), copy(outs, scr, 4, b_yn, sib)]
        far = [copy(outs, scr, 7, b_dg, sib, rows=0), copy(outs, scr, 8, b_dg, sib, rows=1)]
        return (b_xn, b_yn, b_dg), near, far

    def mid(ins, outs, scr):
        x, y, c, _ = _place()
        (b_xn, b_yn, _), near, _ = passed_on(outs, scr)
        copy(outs, scr, 1, b_xn, (x, y, c)).wait_recv()
        near[0].start()
        near[1].start()
        copy(outs, scr, 2, b_yn, (x, y, c)).wait_recv()
        near[2].start()
        near[3].start()

    def finish(ins, outs, scr):
        x, y, c, _ = _place()
        here = (x, y, c)
        (b_xn, b_yn, b_dg), near, far = passed_on(outs, scr)
        copy(outs, scr, 5, b_dg, here, rows=0).wait_recv()
        far[0].start()
        copy(outs, scr, 6, b_dg, here, rows=1).wait_recv()
        far[1].start()
        flip = 1 - 2 * c
        copy(outs, scr, 0, 4 * x + 2 * y + 1 - c, here).wait_recv()
        copy(outs, scr, 3, b_xn + flip, here).wait_recv()
        copy(outs, scr, 4, b_yn + flip, here).wait_recv()
        copy(outs, scr, 7, b_dg + flip, here, rows=0).wait_recv()
        copy(outs, scr, 8, b_dg + flip, here, rows=1).wait_recv()
        for cp in own(outs, scr) + near + far:
            cp.wait_send()
        local(outs, scr, 4 * x + 2 * y + c).wait()

    return Side((shard,), (VMEM,), (jax.ShapeDtypeStruct((NDEV, R, W), dtype),),
                (pltpu.VMEM((R, W), dtype), _sems(9), _sems(9), _sems(1)), start, finish, mid, "sxy")


def copies_side(args, out_shape, n_copies, plan, peers):
    def copies(ins, outs, scr):
        return [_remote(s_, d_, scr[0], scr[1], i, to) for i, (s_, d_, to) in enumerate(plan(ins, outs))]

    def start(ins, outs, scr):
        for cp in copies(ins, outs, scr):
            cp.start()

    def finish(ins, outs, scr):
        for cp in copies(ins, outs, scr):
            cp.wait()

    return Side(tuple(args), (ANY,) * len(args), tuple(out_shape), (_sems(n_copies), _sems(n_copies)),
                start, finish, None, peers)


def rs_to_sibling(grads):
    out_shape = [jax.ShapeDtypeStruct((4,) + g.shape[1:], BF16) for g in grads]

    def plan(ins, outs):
        x, y, c, _ = _place()
        return [(g.at[2 * k + 1 - c], r.at[k], (x, y, 1 - c)) for g, r in zip(ins, outs) for k in range(4)]

    return copies_side(grads, out_shape, 4 * len(grads), plan, "s")


def rs_to_chips(parts):
    out_shape = [jax.ShapeDtypeStruct((3,) + p.shape[1:], BF16) for p in parts]

    def plan(ins, outs):
        x, y, c, chips = _place()
        return [(p.at[2 * cx + cy], r.at[j], (cx, cy, c))
                for p, r in zip(ins, outs) for j, (cx, cy) in enumerate(chips)]

    return copies_side(parts, out_shape, 3 * len(parts), plan, "dxy")


def rs_to_chips_combined(part):
    _, R, W = part.shape
    half = R // 2
    top, bot = pl.ds(0, half), pl.ds(half, half)

    def copies(ins, outs, scr):
        p, r = ins[0], outs[0]
        loc_a, loc_b, in_x, in_y, comb_a, comb_b, send, recv, loc = scr
        x, y, c, _ = _place()
        xn, yn = (1 - x, y, c), (x, 1 - y, c)
        k_xn, k_yn, k_dg = 2 * (1 - x) + y, 2 * x + 1 - y, 2 * (1 - x) + 1 - y
        direct = [_remote(p.at[k_xn, top, :], r.at[0, top, :], send, recv, 0, xn),
                  _remote(p.at[k_yn, bot, :], r.at[1, bot, :], send, recv, 1, yn),
                  _remote(p.at[k_dg, top, :], in_x, send, recv, 2, xn),
                  _remote(p.at[k_dg, bot, :], in_y, send, recv, 3, yn)]
        combined = [_remote(comb_a, r.at[1, top, :], send, recv, 4, yn),
                    _remote(comb_b, r.at[0, bot, :], send, recv, 5, xn)]
        local = [pltpu.make_async_copy(p.at[k_yn, top, :], loc_a, loc.at[0]),
                 pltpu.make_async_copy(p.at[k_xn, bot, :], loc_b, loc.at[1])]
        return direct, combined, local

    def start(ins, outs, scr):
        direct, _, local = copies(ins, outs, scr)
        for cp in local + direct:
            cp.start()

    def mid(ins, outs, scr):
        loc_a, loc_b, in_x, in_y, comb_a, comb_b = scr[:6]
        direct, combined, local = copies(ins, outs, scr)
        for mine, arrival, inbox, out, nxt in ((local[0], direct[2], in_x, comb_a, combined[0]),
                                               (local[1], direct[3], in_y, comb_b, combined[1])):
            mine.wait()
            arrival.wait_recv()
            src = loc_a if out is comb_a else loc_b
            out[...] = (src[...].astype(F32) + inbox[...].astype(F32)).astype(BF16)
            nxt.start()

    def finish(ins, outs, scr):
        direct, combined, _ = copies(ins, outs, scr)
        direct[0].wait_recv()
        direct[1].wait_recv()
        combined[0].wait_recv()
        combined[1].wait_recv()
        for cp in direct + combined:
            cp.wait_send()

    buf = pltpu.VMEM((half, W), BF16)
    return Side((part,), (ANY,), (jax.ShapeDtypeStruct((2, R, W), BF16),),
                (buf, buf, buf, buf, buf, buf, _sems(6), _sems(6), _sems(2)), start, finish, mid, "xy")


ADAM_TILE_BYTES = 3 * 512 * 1024


def _row_tiles(rows, width):
    return 2 if rows % 32 == 0 and rows * width * 4 > ADAM_TILE_BYTES else 1


ID_CORE, ID_CHIP, ID_DEVICE = 4, 5, 6


def chip_sum(name, grads, recvs, ids):
    n = len(grads)

    def body(s_ref, *refs):
        k = pl.program_id(0)
        for g_ref, r_ref, p_ref, own_ref in zip(refs[:n], refs[n:2 * n], refs[2 * n::2], refs[2 * n + 1::2]):
            tot = g_ref[0] + r_ref[0].astype(F32)
            p_ref[0] = tot.astype(BF16)

            @pl.when(k == s_ref[ID_CHIP])
            def _(own_ref=own_ref, tot=tot):
                own_ref[...] = tot

    def block(g):
        return (1,) + g.shape[1:]

    grid_spec = pltpu.PrefetchScalarGridSpec(
        num_scalar_prefetch=1, grid=(4,),
        in_specs=[pl.BlockSpec(block(g), lambda k, s: (2 * k + s[ID_CORE], 0, 0)) for g in grads]
        + [pl.BlockSpec(block(g), lambda k, s: (k, 0, 0)) for g in grads],
        out_specs=[sp for g in grads for sp in (pl.BlockSpec(block(g), lambda k, s: (k, 0, 0)),
                                                pl.BlockSpec(g.shape[1:], lambda k, s: (0, 0)))])
    res = pl.pallas_call(
        body, name=name, grid_spec=grid_spec,
        out_shape=[sh for g in grads for sh in (jax.ShapeDtypeStruct((4,) + g.shape[1:], BF16),
                                                jax.ShapeDtypeStruct(g.shape[1:], F32))],
        compiler_params=_cp(dimension_semantics=("arbitrary",)),
    )(ids, *grads, *recvs)
    return [(res[2 * j], res[2 * j + 1]) for j in range(n)]


def _adamw(w, g, m, v):
    m2 = ADAM_B1 * m + (1.0 - ADAM_B1) * g
    v2 = ADAM_B2 * v + (1.0 - ADAM_B2) * (g * g)
    m_hat = m2 / (1.0 - ADAM_B1 ** ADAM_STEP)
    v_hat = v2 / (1.0 - ADAM_B2 ** ADAM_STEP)
    delta = -ADAM_LR * (m_hat / (jnp.sqrt(v_hat) + ADAM_EPS) + ADAM_WD * w)
    return delta, m2, v2


def shard_adam(name, owns, recvs, w, m, v):
    n = len(owns)
    R = owns[0].shape[0]
    ct = min(o.shape[1] for o in owns)
    first = [sum(o.shape[1] for o in owns[:j]) // ct for j in range(n)]
    count = [o.shape[1] // ct for o in owns]
    nt = _row_tiles(R, ct)
    tr = R // nt

    def body(*refs):
        o_refs, r_refs = refs[:n], refs[n:2 * n]
        w_ref, m_ref, v_ref, g_ref, d_ref, nm_ref, nv_ref = refs[2 * n:]
        g = None
        for j in range(n):
            gj = o_refs[j][...]
            for q in range(recvs[j].shape[0]):
                gj = gj + r_refs[j][q].astype(F32)
            g = gj if g is None else jnp.where(pl.program_id(0) >= first[j], gj, g)
        delta, m2, v2 = _adamw(w_ref[...], g, m_ref[...], v_ref[...])
        g_ref[...] = g
        d_ref[...] = delta
        nm_ref[...] = m2
        nv_ref[...] = v2

    def part(j):
        return pl.BlockSpec((tr, ct), lambda k, i: (i, jnp.clip(k - first[j], 0, count[j] - 1)))

    def part3(j):
        return pl.BlockSpec((recvs[j].shape[0], tr, ct), lambda k, i: (0, i, jnp.clip(k - first[j], 0, count[j] - 1)))

    C = sum(count) * ct
    tile = pl.BlockSpec((tr, ct), lambda k, i: (i, k))
    return pl.pallas_call(
        body, name=name, grid=(sum(count), nt),
        in_specs=[part(j) for j in range(n)] + [part3(j) for j in range(n)] + [tile, tile, tile],
        out_specs=[tile] * 4, out_shape=[jax.ShapeDtypeStruct((R, C), F32)] * 4,
        compiler_params=_cp(dimension_semantics=("arbitrary", "arbitrary")),
    )(*owns, *recvs, w, m, v)


def block_adam(name, items):
    n = len(items)

    def body(*refs):
        for j, item in enumerate(items):
            o_ref, r_ref, w_ref, m_ref, v_ref = refs[5 * j:5 * j + 5]
            g = o_ref[...]
            for q in range(r_ref.shape[0]):
                g = g + r_ref[q].astype(F32)
            t = (lambda a: a.T) if item[5] else (lambda a: a)
            delta, m2, v2 = _adamw(t(w_ref[...]), g, t(m_ref[...]), t(v_ref[...]))
            for ref, val in zip(refs[5 * n + 4 * j:5 * n + 4 * j + 4], (g, delta, m2, v2)):
                ref[...] = t(val)

    args = [a for item in items for a in item[:5]]
    out_shape = [jax.ShapeDtypeStruct(item[2].shape, F32) for item in items for _ in range(4)]
    res = pl.pallas_call(
        body, name=name, grid=(1,), in_specs=[VMEM] * len(args), out_specs=[VMEM] * len(out_shape),
        out_shape=out_shape, compiler_params=_cp(dimension_semantics=("arbitrary",)))(*args)
    return [tuple(res[4 * j:4 * j + 4]) for j in range(n)]


ROW_N1, ROW_N2, ROW_BG, ROW_QN, ROW_KN, ROW_CB, ROW_LW, ROW_LB, ROW_CW = 0, 1, 2, 4, 5, 6, 7, 8, 9
PACK_ROWS = 40
SMALL = ("norm1_w", "norm2_w", "b_gate", "q_norm_w", "k_norm_w", "conv_b", "conv_ln_w", "conv_ln_b", "conv_w")


def small_sync(g, sq, sides=()):
    ns = len(SMALL)

    def copies(refs):
        pack, recv, send_sems, recv_sems = refs[ns + 2:]
        x, y, c, _ = _place()
        return [pltpu.make_async_remote_copy(
            src_ref=pack, dst_ref=recv.at[4 * x + 2 * y + c], send_sem=send_sems.at[k - 1],
            recv_sem=recv_sems.at[k - 1], device_id=(x ^ (k >> 2), y ^ ((k >> 1) & 1), c ^ (k & 1)),
            device_id_type=MESH) for k in range(1, NDEV)]

    def body(*refs):
        gi = dict(zip(SMALL, refs[:ns]))
        sq_ref, tot, pack, recv, send_sems, recv_sems = refs[ns:]
        x, y, c, _ = _place()
        me = 4 * x + 2 * y + c

        pack[...] = jnp.zeros_like(pack)
        pack[ROW_KN:ROW_KN + 1, LANES:2 * LANES] = jnp.full((1, LANES), (0.5 / D) * jnp.sum(sq_ref[...]), F32)
        pack[ROW_N1:ROW_N1 + 1, :] = gi["norm1_w"][...]
        pack[ROW_N2:ROW_N2 + 1, :] = gi["norm2_w"][...]
        pack[ROW_BG:ROW_BG + 2, :] = gi["b_gate"][...]
        for row, name in ((ROW_QN, "q_norm_w"), (ROW_KN, "k_norm_w")):
            pack[row:row + 1, 0:HD] = gi[name][0:1, 0:HD] + gi[name][0:1, HD:LANES]
        pack[ROW_CB:ROW_CB + 1, 0:CC] = gi["conv_b"][...]
        pack[ROW_LW:ROW_LW + 1, 0:CC] = gi["conv_ln_w"][...]
        pack[ROW_LB:ROW_LB + 1, 0:CC] = gi["conv_ln_b"][...]
        pack[ROW_CW:ROW_CW + KW, 0:CC] = gi["conv_w"][...]

        for cp in copies(refs):
            cp.start()
        recv[me] = pack[...]

    def tail(*refs):
        tot, recv = refs[ns + 1], refs[ns + 3]
        for cp in copies(refs):
            cp.wait()
        acc = recv[0]
        for p in range(1, NDEV):
            acc = acc + recv[p]
        tot[...] = acc

    args = [g[k] for k in SMALL] + [sq]
    res = _call(
        body, sides, name="small_sync", grid=(1,), in_specs=[VMEM] * len(args), out_specs=[VMEM],
        out_shape=[jax.ShapeDtypeStruct((PACK_ROWS, D), F32)],
        scratch_shapes=[pltpu.VMEM((PACK_ROWS, D), F32), pltpu.VMEM((NDEV, PACK_ROWS, D), F32),
                        _sems(NDEV - 1), _sems(NDEV - 1)],
        args=args, own_comm=True, tail=tail)
    return (res[0][0], res[1]) if sides else res[0]


def small_adam(tot, w, m, v, ids):
    ns = len(SMALL)

    def body(me_ref, tot, *refs):
        wi = dict(zip(SMALL, refs[:ns]))
        mi = dict(zip(SMALL, refs[ns:2 * ns]))
        vi = dict(zip(SMALL, refs[2 * ns:3 * ns]))
        outs = refs[3 * ns:7 * ns]
        loss_ref = refs[7 * ns]
        me = me_ref[ID_DEVICE]

        def shard_grad(name):
            if name == "b_gate":
                return tot[ROW_BG:ROW_BG + 2, pl.ds(pl.multiple_of(me * LANES, LANES), LANES)]
            if name == "conv_w":
                win = tot[ROW_CW:ROW_CW + KW, pl.ds(pl.multiple_of((me // 2) * LANES, LANES), LANES)]
                return jnp.where(me % 2 == 1, win[:, HD:LANES], win[:, 0:HD])
            row = {"norm1_w": ROW_N1, "norm2_w": ROW_N2, "q_norm_w": ROW_QN, "k_norm_w": ROW_KN,
                   "conv_b": ROW_CB, "conv_ln_w": ROW_LW, "conv_ln_b": ROW_LB}[name]
            return tot[row:row + 1, 0:wi[name].shape[1]]

        for i, name in enumerate(SMALL):
            gr = shard_grad(name)
            delta, m2, v2 = _adamw(wi[name][...], gr, mi[name][...], vi[name][...])
            outs[4 * i][...] = gr
            outs[4 * i + 1][...] = delta
            outs[4 * i + 2][...] = m2
            outs[4 * i + 3][...] = v2
        loss_ref[...] = tot[ROW_KN:ROW_KN + 1, LANES:2 * LANES]

    out_shape = []
    for name in SMALL:
        out_shape += [jax.ShapeDtypeStruct(w[name].shape, F32)] * 4
    out_shape.append(jax.ShapeDtypeStruct((1, LANES), F32))
    args = [tot] + [w[k] for k in SMALL] + [m[k] for k in SMALL] + [v[k] for k in SMALL]
    grid_spec = pltpu.PrefetchScalarGridSpec(
        num_scalar_prefetch=1, grid=(1,), in_specs=[VMEM] * len(args), out_specs=[VMEM] * len(out_shape))
    res = pl.pallas_call(body, name="small_adam", grid_spec=grid_spec, out_shape=out_shape)(ids, *args)
    out = {name: tuple(res[4 * i:4 * i + 4]) for i, name in enumerate(SMALL)}
    return out, res[4 * ns][0, 0]


MATS = ("w_in", "w_o_attn", "w_pw_conv", "w_out", "w_ffn_in", "w_ffn_out")
TRANSPOSED = ("w_in", "w_ffn_in")
WEIGHTS = ("norm1_w", "w_in", "b_gate", "q_norm_w", "k_norm_w", "w_o_attn", "conv_w", "conv_b", "conv_ln_w",
           "conv_ln_b", "w_pw_conv", "w_out", "norm2_w", "w_ffn_in", "w_ffn_out")


def _blocks_to_cols(blocks):
    n, R, C = blocks.shape
    return blocks.transpose(1, 0, 2).reshape(R, n * C)


def kernel(x, positions, norm1_w, w_in, b_gate, q_norm_w, k_norm_w, w_o_attn, conv_w, conv_b, conv_ln_w, conv_ln_b, w_pw_conv, w_out, norm2_w, w_ffn_in, w_ffn_out, loss_target, m_norm1_w, m_w_in, m_b_gate, m_q_norm_w, m_k_norm_w, m_w_o_attn, m_conv_w, m_conv_b, m_conv_ln_w, m_conv_ln_b, m_w_pw_conv, m_w_out, m_norm2_w, m_w_ffn_in, m_w_ffn_out, v_norm1_w, v_w_in, v_b_gate, v_q_norm_w, v_k_norm_w, v_w_o_attn, v_conv_w, v_conv_b, v_conv_ln_w, v_conv_ln_b, v_w_pw_conv, v_w_out, v_norm2_w, v_w_ffn_in, v_w_ffn_out):
    w = dict(norm1_w=norm1_w, w_in=w_in, b_gate=b_gate, q_norm_w=q_norm_w, k_norm_w=k_norm_w, w_o_attn=w_o_attn,
             conv_w=conv_w, conv_b=conv_b, conv_ln_w=conv_ln_w, conv_ln_b=conv_ln_b, w_pw_conv=w_pw_conv,
             w_out=w_out, norm2_w=norm2_w, w_ffn_in=w_ffn_in, w_ffn_out=w_ffn_out)
    m = dict(norm1_w=m_norm1_w, w_in=m_w_in, b_gate=m_b_gate, q_norm_w=m_q_norm_w, k_norm_w=m_k_norm_w,
             w_o_attn=m_w_o_attn, conv_w=m_conv_w, conv_b=m_conv_b, conv_ln_w=m_conv_ln_w,
             conv_ln_b=m_conv_ln_b, w_pw_conv=m_w_pw_conv, w_out=m_w_out, norm2_w=m_norm2_w,
             w_ffn_in=m_w_ffn_in, w_ffn_out=m_w_ffn_out)
    v = dict(norm1_w=v_norm1_w, w_in=v_w_in, b_gate=v_b_gate, q_norm_w=v_q_norm_w, k_norm_w=v_k_norm_w,
             w_o_attn=v_w_o_attn, conv_w=v_conv_w, conv_b=v_conv_b, conv_ln_w=v_conv_ln_w,
             conv_ln_b=v_conv_ln_b, w_pw_conv=v_w_pw_conv, w_out=v_w_out, norm2_w=v_norm2_w,
             w_ffn_in=v_w_ffn_in, w_ffn_out=v_w_ffn_out)
    def two_d(t):
        t = {k: (a[0] if a.ndim == 3 else a) for k, a in t.items()}
        return {k: (a.T if k in TRANSPOSED else a) for k, a in t.items()}

    w, m, v = two_d(w), two_d(m), two_d(v)

    x2, target = x[0], loss_target[0]
    ax, ay, ac = lax.axis_index("x"), lax.axis_index("y"), lax.axis_index("c")
    ids = jnp.stack([2 * ax + ay, 2 * (1 - ax) + ay, 2 * ax + 1 - ay, 2 * (1 - ax) + 1 - ay,
                     ac, 2 * ax + ay, 4 * ax + 2 * ay + ac]).astype(jnp.int32)
    qw2 = jnp.tile(w["q_norm_w"], (1, 2))
    kw2 = jnp.tile(w["k_norm_w"], (1, 2))

    h, proj, w_in_blocks, tabs = in_proj_gather(x2, w["norm1_w"], w["w_in"], ids, positions.reshape(S, 1))
    w_in_t = w_in_blocks.reshape(INW, D)
    (attn, lse), ((w_ffn_in_blocks,), (w_out_blocks,), (w_o_blocks,), (w_pw_blocks,), (bg_blocks,), (cw_blocks,)) = attn_fwd(
        proj, tabs, qw2, kw2, sides=(ag_blocks_relay(w["w_ffn_in"], BF16), ag_blocks_relay(w["w_out"], BF16),
                                     ag_blocks_relay(w["w_o_attn"], BF16, transpose=True),
                                     ag_blocks_relay(w["w_pw_conv"], BF16, transpose=True),
                                     ag_blocks(w["b_gate"], F32), ag_blocks(w["conv_w"], F32)))
    w_ffn_in_t = w_ffn_in_blocks.reshape(2 * FF, D)
    w_out_f = w_out_blocks.reshape(D, D)
    w_o_t, w_pw_t = w_o_blocks.reshape(D, CC), w_pw_blocks.reshape(D, CC)
    b_gate_f, conv_w_f = _blocks_to_cols(bg_blocks), _blocks_to_cols(cw_blocks)
    cpre, u3 = conv_fwd(proj, conv_w_f, w["conv_b"], w["conv_ln_w"], w["conv_ln_b"])
    x1, z, ya, yb = mix_out(x2, proj, b_gate_f, attn, u3, w_o_t, w_pw_t, w_out_f)
    (h2, gu, f), ((w_ffn_out_blocks,),) = ffn_in(x1, w["norm2_w"], w_ffn_in_t, sides=(ag_blocks_relay(w["w_ffn_out"], BF16),))
    w_ffn_out_f = w_ffn_out_blocks.reshape(FF, D)
    dy, dyb, sq = ffn_out_loss(x1, f, w_ffn_out_f, target)

    g = {}
    def blocks(name, pairs, tm):
        return [t.reshape(NDEV, t.shape[0] // NDEV, t.shape[1]) for t in mm_tn(name, pairs, tm)]

    g_ffn_out, gb_ffn_out = blocks("gw_ffn_out", [(f, dyb)], FF // 2)
    (d_gu, d_x1, d_x1b, g["norm2_w"]), ((ra_ffn_out,),) = ffn_bwd(
        dy, dyb, gu, x1, w["norm2_w"], w_ffn_in_t, w_ffn_out_f, sides=(rs_to_sibling([gb_ffn_out]),))
    g_ffn_in, gb_ffn_in = blocks("gw_ffn_in", [(d_gu, h2)], FF // 2)
    (d_ya, d_yb, d_gl, d_attn, d_u3, g["b_gate"]), ((ra_ffn_in,),) = out_bwd(
        d_x1b, proj, b_gate_f, ya, yb, w_o_t, w_pw_t, w_out_f, sides=(rs_to_sibling([gb_ffn_in]),))
    g_out, gb_out, g_w_o, gb_w_o, g_w_pw, gb_w_pw = blocks(
        "gw_out_o_pw", [(z, d_x1b), (d_ya, attn), (d_yb, u3)], D // 2)
    (d_conv, g["conv_w"], g["conv_b"], g["conv_ln_w"], g["conv_ln_b"]), ((ra_out, ra_w_o, ra_w_pw),) = conv_bwd(
        proj, cpre, d_u3, conv_w_f, w["conv_ln_w"], w["conv_ln_b"],
        sides=(rs_to_sibling([gb_out, gb_w_o, gb_w_pw]),))
    (pb_ffn_out, own_ffn_out), (pb_ffn_in, own_ffn_in), (pb_out, own_out), (pb_w_o, own_w_o), (pb_w_pw, own_w_pw) = chip_sum(
        "chip_sum_early", [g_ffn_out, g_ffn_in, g_out, g_w_o, g_w_pw],
        [ra_ffn_out, ra_ffn_in, ra_out, ra_w_o, ra_w_pw], ids)
    (d_q, d_k, d_v, gqw, gkw), ((rb_ffn_out, rb_ffn_in, rb_out, rb_w_o, rb_w_pw),) = attn_bwd(
        proj, tabs, qw2, kw2, d_attn, attn, lse,
        sides=(rs_to_chips([pb_ffn_out, pb_ffn_in, pb_out, pb_w_o, pb_w_pw]),))
    g["q_norm_w"], g["k_norm_w"] = gqw, gkw
    d_segs = (d_q, d_k, d_v, d_conv, d_gl)
    parts, to_sibling, to_chips, owns, from_chips = [], None, None, [], []
    for k, hw in enumerate(GW_IN_SPLIT):
        sides = tuple(s for s in (to_chips, to_sibling) if s is not None)
        (part, part_b), outs = gw_in("gw_in_%d" % k, h, d_segs, sum(GW_IN_SPLIT[:k]), hw, sides=sides)
        outs = list(outs)
        if to_chips is not None:
            from_chips.append(outs.pop(0)[0])
        if to_sibling is not None:
            (pb, own), = chip_sum("chip_sum_w_in_%d" % (k - 1), [parts[-1]], [outs.pop(0)[0]], ids)
            owns.append(own)
            to_chips = rs_to_chips_combined(pb)
        else:
            to_chips = None
        parts.append(part.reshape(NDEV, INW // NDEV, hw))
        to_sibling = rs_to_sibling([part_b.reshape(NDEV, INW // NDEV, hw)])
    (grad_x, g["norm1_w"]), ((rb_prev,), (ra_last,)) = in_bwd(
        d_q, d_k, d_v, d_conv, d_gl, w_in_t, x2, d_x1, w["norm1_w"], sides=(to_chips, to_sibling))
    from_chips.append(rb_prev)
    (pb, own), = chip_sum("chip_sum_w_in_%d" % (len(GW_IN_SPLIT) - 1), [parts[-1]], [ra_last], ids)
    owns.append(own)
    small_sums, ((rb_last,),) = small_sync(g, sq, sides=(rs_to_chips_combined(pb),))
    small, loss = small_adam(small_sums, w, m, v, ids)
    from_chips.append(rb_last)

    adam_o, adam_pw, adam_out = block_adam("adam_w_o_pw_out", [
        (own_w_o, rb_w_o, w["w_o_attn"], m["w_o_attn"], v["w_o_attn"], True),
        (own_w_pw, rb_w_pw, w["w_pw_conv"], m["w_pw_conv"], v["w_pw_conv"], True),
        (own_out, rb_out, w["w_out"], m["w_out"], v["w_out"], False)])
    res = {
        "w_in": shard_adam("adam_w_in", owns, from_chips, w["w_in"], m["w_in"], v["w_in"]),
        "w_ffn_in": shard_adam("adam_w_ffn_in", [own_ffn_in], [rb_ffn_in], w["w_ffn_in"], m["w_ffn_in"], v["w_ffn_in"]),
        "w_o_attn": adam_o, "w_pw_conv": adam_pw, "w_out": adam_out,
        "w_ffn_out": shard_adam("adam_w_ffn_out", [own_ffn_out], [rb_ffn_out],
                                w["w_ffn_out"], m["w_ffn_out"], v["w_ffn_out"]),
    }
    res = {k: tuple(a.T if k in TRANSPOSED else a for a in r) for k, r in res.items()}
    res.update(small)

    def shaped(name, a):
        return a.reshape((1,) + a.shape) if name in MATS or name in ("b_gate", "conv_w") else a

    outs = [loss, grad_x.reshape(1, S, D)]
    for i in range(4):
        outs += [shaped(k, res[k][i]) for k in WEIGHTS]
    return tuple(outs)
```

```python
import functools
from typing import Callable, NamedTuple, Optional

import numpy as np
import jax
import jax.numpy as jnp
from jax import lax
from jax.experimental import pallas as pl
from jax.experimental.pallas import tpu as pltpu

F32 = jnp.float32
BF16 = jnp.bfloat16

S = 2048
D = 1024
HD = 64
QKV = 1536
CC = 512
KW = 31
FF = 2816
INW = 7680
OFF_Q, OFF_K, OFF_V, OFF_CA, OFF_CB, OFF_GA, OFF_GB = 0, 1536, 3072, 4608, 5120, 5632, 6656
DILATIONS = (1, 4, 16)
HALF_SPAN = 64
EPS = 1e-6
NEG_INF = -1e30
ROPE_THETA = 500000.0
ROT_DIM = 16

ADAM_LR = 0.001
ADAM_B1 = 0.9
ADAM_B2 = 0.999
ADAM_EPS = 1e-08
ADAM_WD = 0.01
ADAM_STEP = 10

NDEV = 8
LANES = 128
TM = 256
IN_PROJ_TM = 512
TQ = 128
VMEM_LIMIT = 56 * 1024 * 1024
MESH = pl.DeviceIdType.MESH


def _cp(**kw):
    return pltpu.CompilerParams(vmem_limit_bytes=VMEM_LIMIT, **kw)


def _row(width, col=0, tm=TM):
    return pl.BlockSpec((tm, width), lambda i: (i, col))


PLANE = 512


def _planes(width, tm=TM):
    return pl.BlockSpec((width // PLANE, tm, PLANE), lambda i: (0, i, 0))


def _res(shape):
    nd = len(shape)
    return pl.BlockSpec(shape, lambda *_: (0,) * nd, pipeline_mode=pl.Buffered(1))


def _dot(a, b):
    return jnp.dot(a, b, preferred_element_type=F32)


def _dot_nt(a, b):
    return lax.dot_general(a, b, (((1,), (1,)), ((), ())), preferred_element_type=F32)


def _dot_tn(a, b):
    return lax.dot_general(a, b, (((0,), (0,)), ((), ())), preferred_element_type=F32)


def _sigmoid(x):
    return jax.nn.sigmoid(x)


def _dsilu(x, sg):
    return sg * (1.0 + x * (1.0 - sg))


ANY = pl.BlockSpec(memory_space=pl.ANY)
VMEM = pl.BlockSpec(memory_space=pltpu.VMEM)


class Side(NamedTuple):
    args: tuple
    in_specs: tuple
    out_shape: tuple
    scratch: tuple
    start: Callable
    finish: Callable
    mid: Optional[Callable] = None
    peers: str = ""


BARRIER_IDS = {"s": 0, "dxy": 1, "dsxy": 2, "sxy": 3, "xy": 4}


def _peer_barrier(peers):
    x, y, c = lax.axis_index("x"), lax.axis_index("y"), lax.axis_index("c")
    where = {"s": (x, y, 1 - c), "x": (1 - x, y, c), "y": (x, 1 - y, c), "d": (1 - x, 1 - y, c)}
    barrier = pltpu.get_barrier_semaphore()
    for p in peers:
        pl.semaphore_signal(barrier, inc=1, device_id=where[p], device_id_type=MESH)
    pl.semaphore_wait(barrier, len(peers))


def _call(body, sides=(), *, name, grid, in_specs, out_specs, out_shape, scratch_shapes=(), args, own_comm=False,
          tail=None):
    assert tail is None or int(np.prod(grid)) == 1
    ni, no, ns = len(in_specs), len(out_specs), len(scratch_shapes)
    cnt = [(len(s.args), len(s.out_shape), len(s.scratch)) for s in sides]
    peers = "".join(sorted(set("".join(s.peers for s in sides))))
    if own_comm or not sides or any(not s.peers for s in sides):
        peers = ""

    def take(refs, pos, n):
        return refs[pos:pos + n], pos + n

    def full(*refs):
        m_in, pos = take(refs, 0, ni)
        s_in = []
        for a, _, _ in cnt:
            r, pos = take(refs, pos, a)
            s_in.append(r)
        m_out, pos = take(refs, pos, no)
        s_out = []
        for _, o, _ in cnt:
            r, pos = take(refs, pos, o)
            s_out.append(r)
        m_scr, pos = take(refs, pos, ns)
        s_scr = []
        for _, _, c in cnt:
            r, pos = take(refs, pos, c)
            s_scr.append(r)
        if sides:
            first = functools.reduce(jnp.logical_and, [pl.program_id(d) == 0 for d in range(len(grid))])
            last = functools.reduce(jnp.logical_and, [pl.program_id(d) == g - 1 for d, g in enumerate(grid)])

            @pl.when(first)
            def _():
                if peers:
                    _peer_barrier(peers)
                for s, a, o, c in zip(sides, s_in, s_out, s_scr):
                    s.start(a, o, c)

            steps = int(np.prod(grid))
            mid_step = (2 * steps) // 3
            if steps > 1 and any(s.mid is not None for s in sides):
                step = functools.reduce(lambda acc, d: acc * grid[d] + pl.program_id(d), range(len(grid)), 0)

                @pl.when(step == mid_step)
                def _():
                    for s, a, o, c in zip(sides, s_in, s_out, s_scr):
                        if s.mid is not None:
                            s.mid(a, o, c)

        body(*m_in, *m_out, *m_scr)
        if sides:
            @pl.when(last)
            def _():
                for s, a, o, c in zip(sides, s_in, s_out, s_scr):
                    if s.mid is not None and steps == 1:
                        s.mid(a, o, c)
                if tail is not None:
                    tail(*m_in, *m_out, *m_scr)
                for s, a, o, c in zip(sides, s_in, s_out, s_scr):
                    s.finish(a, o, c)
        elif tail is not None:
            tail(*m_in, *m_out, *m_scr)

    res = pl.pallas_call(
        full, name=name, grid=grid,
        in_specs=list(in_specs) + [sp for s in sides for sp in s.in_specs],
        out_specs=list(out_specs) + [ANY for s in sides for _ in s.out_shape],
        out_shape=list(out_shape) + [o for s in sides for o in s.out_shape],
        scratch_shapes=list(scratch_shapes) + [c for s in sides for c in s.scratch],
        compiler_params=_cp(dimension_semantics=("arbitrary",) * len(grid),
                            **({"collective_id": BARRIER_IDS[peers]} if peers else {})),
    )(*args, *[a for s in sides for a in s.args])
    res = list(res)
    if not sides:
        return res
    outs, pos = take(res, 0, no)
    side_outs = []
    for _, o, _ in cnt:
        r, pos = take(res, pos, o)
        side_outs.append(r)
    return outs, side_outs


def _inv_freq_lanes():
    inv = np.float32(ROPE_THETA) ** (-np.arange(0, ROT_DIM, 2, dtype=np.float32) / np.float32(ROT_DIM))
    lane = np.arange(LANES) % HD
    out = np.where(lane < ROT_DIM, inv[lane % (ROT_DIM // 2)], 0.0).astype(np.float32)
    return jnp.asarray(out.reshape(1, LANES))


def _rope_tables(pos, inv_freq):
    ang = pos.astype(F32) * inv_freq
    lane = lax.broadcasted_iota(jnp.int32, ang.shape, 1) % HD
    cs = jnp.cos(ang)
    sn = jnp.sin(ang)
    return (jnp.where(lane < ROT_DIM, cs, 1.0), jnp.where(lane < ROT_DIM // 2, -sn, 0.0),
            jnp.where(lane < ROT_DIM // 2, 0.0, jnp.where(lane < ROT_DIM, sn, 0.0)))


def _rope(v, c, s1, s2):
    return v * c + pltpu.roll(v, LANES - 8, axis=1) * s1 + pltpu.roll(v, 8, axis=1) * s2


def _rope_t(d, c, s1, s2):
    return d * c - pltpu.roll(d, LANES - 8, axis=1) * s1 - pltpu.roll(d, 8, axis=1) * s2


def _head_mat():
    r = lax.broadcasted_iota(jnp.int32, (LANES, LANES), 0) // HD
    c = lax.broadcasted_iota(jnp.int32, (LANES, LANES), 1) // HD
    return jnp.where(r == c, 1.0 / HD, 0.0).astype(BF16)


def _head_mean(t, e):
    hi = t.astype(BF16)
    rest = (t - hi.astype(F32)).astype(BF16)
    return _dot(hi, e) + _dot(rest, e)


def in_proj_gather(x, norm_w, shard_t, chip_order, pos_col):
    R = INW // NDEV
    tm = IN_PROJ_TM
    half, nt = R // 2, S // tm

    def body(ord_ref, x_ref, nw_ref, sh_ref, pos_ref, f_ref, h_ref, p_ref, wfull_ref, c_ref, s1_ref, s2_ref,
             wt, hs, send, recv, loc):
        kk, i = pl.program_id(0), pl.program_id(1)
        x, y, c, _ = _place()
        me, flip = 4 * x + 2 * y + c, 1 - 2 * c
        here, sib, xn, yn = (x, y, c), (x, y, 1 - c), (1 - x, y, c), (x, 1 - y, c)
        b_xn, b_yn, b_dg = 4 * (1 - x) + 2 * y + c, 4 * x + 2 * (1 - y) + c, 4 * (1 - x) + 2 * (1 - y) + c

        def cp(k, block, to, rows=None):
            dst = wt.at[block] if rows is None else wt.at[block, pl.ds(rows * half, half), :]
            return _remote(dst, dst, send, recv, k, to)

        def sends():
            return [cp(0, me, sib), cp(1, me, xn), cp(2, me, yn), cp(3, b_xn, sib), cp(4, b_yn, sib),
                    cp(5, b_xn, yn, rows=0), cp(6, b_yn, xn, rows=1), cp(7, b_dg, sib, rows=0), cp(8, b_dg, sib, rows=1)]

        def keep(j, blk0):
            pair = pl.ds(pl.multiple_of(blk0, 2), 2)
            return pltpu.make_async_copy(wt.at[pair], wfull_ref.at[pair], loc.at[j])

        @pl.when((kk == 0) & (i == 0))
        def _():
            _peer_barrier("sxy")
            _cast_rows(wt.at[me], sh_ref)
            for s_ in sends()[0:3]:
                s_.start()

            def tables(j, _):
                posf = pos_ref[pl.ds(j, 1), :].astype(F32)
                eye = (lax.broadcasted_iota(jnp.int32, (LANES, LANES), 0)
                       == lax.broadcasted_iota(jnp.int32, (LANES, LANES), 1))
                col = jnp.sum(jnp.where(eye, posf, 0.0), axis=1, keepdims=True)
                chunk = pl.ds(pl.multiple_of(j * LANES, LANES), LANES)
                c_ref[chunk, :], s1_ref[chunk, :], s2_ref[chunk, :] = _rope_tables(col, f_ref[...])
                return 0

            lax.fori_loop(0, S // LANES, tables, 0)
            cp(0, me + flip, here).wait_recv()
            keep(0, me - c).start()

        @pl.when((kk == 1) & (i == 0))
        def _():
            cp(1, b_xn, here).wait_recv()
            sends()[5].start()
            sends()[3].start()
            cp(2, b_yn, here).wait_recv()
            sends()[6].start()
            sends()[4].start()
            cp(3, b_xn + flip, here).wait_recv()
            keep(1, b_xn - c).start()

        @pl.when((kk == 2) & (i == 0))
        def _():
            cp(4, b_yn + flip, here).wait_recv()
            keep(2, b_yn - c).start()

        @pl.when((kk == 3) & (i == 0))
        def _():
            cp(5, b_dg, here, rows=0).wait_recv()
            sends()[7].start()
            cp(6, b_dg, here, rows=1).wait_recv()
            sends()[8].start()
            cp(7, b_dg + flip, here, rows=0).wait_recv()
            cp(8, b_dg + flip, here, rows=1).wait_recv()
            keep(3, b_dg - c).start()

        rows = pl.ds(pl.multiple_of(i * tm, tm), tm)

        @pl.when(kk == 0)
        def _():
            xv = x_ref[...]
            r = lax.rsqrt(jnp.mean(xv * xv, axis=-1, keepdims=True) + EPS)
            hb = (xv * r * nw_ref[...]).astype(BF16)
            h_ref[...] = hb
            hs[rows, :] = hb

        h = hs[rows, :]
        chip = ord_ref[kk]
        for cc in range(2):
            p_ref[:, cc * R:(cc + 1) * R] = _dot_nt(h, wt[2 * chip + cc])

        @pl.when((kk == 3) & (i == nt - 1))
        def _():
            for s_ in sends():
                s_.wait_send()
            for j, blk in enumerate((me, b_xn, b_yn, b_dg)):
                keep(j, blk - c).wait()

    def first_pass(kk, i):
        return jnp.where(kk == 0, i, nt - 1)

    grid_spec = pltpu.PrefetchScalarGridSpec(
        num_scalar_prefetch=1, grid=(4, nt),
        in_specs=[pl.BlockSpec((tm, D), lambda kk, i, o: (first_pass(kk, i), 0)),
                  pl.BlockSpec((1, D), lambda kk, i, o: (0, 0)), VMEM, VMEM,
                  pl.BlockSpec((1, LANES), lambda kk, i, o: (0, 0))],
        out_specs=[pl.BlockSpec((tm, D), lambda kk, i, o: (first_pass(kk, i), 0)),
                   pl.BlockSpec((tm, 2 * R), lambda kk, i, o: (i, o[kk])), ANY]
        + [pl.BlockSpec((S, LANES), lambda kk, i, o: (0, 0))] * 3,
        scratch_shapes=[pltpu.VMEM((NDEV, R, D), BF16), pltpu.VMEM((S, D), BF16), _sems(9), _sems(9), _sems(4)])
    res = pl.pallas_call(
        body, name="in_proj_gather", grid_spec=grid_spec,
        out_shape=[jax.ShapeDtypeStruct((S, D), BF16), jax.ShapeDtypeStruct((S, INW), F32),
                   jax.ShapeDtypeStruct((NDEV, R, D), BF16)] + [jax.ShapeDtypeStruct((S, LANES), F32)] * 3,
        compiler_params=_cp(dimension_semantics=("arbitrary", "arbitrary"), collective_id=BARRIER_IDS["sxy"]),
    )(chip_order, x, norm_w, shard_t, pos_col, _inv_freq_lanes())
    return res[0], res[1], res[2], tuple(res[3:])


def _qk_specs():
    nb = QKV // LANES
    return [pl.BlockSpec((S, LANES), functools.partial(lambda hp, g, o: (0, o + g * 4 + hp), o=o))
            for o in (OFF_Q // LANES, OFF_K // LANES, OFF_V // LANES)]


def _tab_specs():
    return [pl.BlockSpec((S, LANES), lambda hp, g: (0, 0), pipeline_mode=pl.Buffered(1))] * 3


def _vec_spec():
    return pl.BlockSpec((1, LANES), lambda hp, g: (0, 0))


def _sub_rows(r, d, start, n):
    if d == 1:
        return pl.ds(start, n)
    return pl.ds(r + d * start, n, stride=d)


def _band_window(i, L):
    W = min(TQ + 2 * HALF_SPAN, L)
    q0 = pl.multiple_of(i * TQ, TQ)
    k0 = pl.multiple_of(jnp.clip(q0 - HALF_SPAN, 0, L - W), HALF_SPAN)
    qpos = q0 + (lax.broadcasted_iota(jnp.int32, (2 * TQ, W), 0) & (TQ - 1))
    kpos = k0 + lax.broadcasted_iota(jnp.int32, (2 * TQ, W), 1)
    valid = jnp.abs(qpos - kpos) <= HALF_SPAN
    return W, q0, k0, valid


def _stack_heads(t, lo):
    z = jnp.zeros_like(t)
    return jnp.concatenate([jnp.where(lo, t, z), jnp.where(lo, z, t)], axis=0)


def _unstack_heads(t2, lo):
    return jnp.where(lo, t2[0:TQ], t2[TQ:2 * TQ])


CHAINS = 8


def _interleave(d):
    ru = min(d, CHAINS)
    return ru, min(CHAINS // ru, S // d // TQ)


def _for_blocks(n, fn):
    if n == 1:
        fn(0)
    else:
        def it(j, _):
            fn(j)
            return 0
        lax.fori_loop(0, n, it, 0)


def attn_fwd(proj, tabs, qw2, kw2, sides=()):
    CH = 256

    def body(q_ref, k_ref, v_ref, c_ref, s1_ref, s2_ref, qw_ref, kw_ref, at_ref, ls_ref,
             qs, ks, vs, osub, lsub, onat, lnat, qn, kn):
        g = pl.program_id(1)
        lo = lax.broadcasted_iota(jnp.int32, (1, LANES), 1) < HD
        e = _head_mat()

        def prep(i, _):
            rows = pl.ds(pl.multiple_of(i * CH, CH), CH)
            c, s1, s2 = c_ref[rows, :], s1_ref[rows, :], s2_ref[rows, :]
            for t_ref, w_ref, out, scale in ((q_ref, qw_ref, qn, HD ** -0.5), (k_ref, kw_ref, kn, 1.0)):
                t = t_ref[rows, :]
                r = lax.rsqrt(_head_mean(t * t, e) + EPS)
                out[rows, :] = _rope(t * r * w_ref[...], c, s1, s2) * scale
            return 0

        lax.fori_loop(0, S // CH, prep, 0, unroll=4)

        def group(gi, d):
            L = S // d

            ru, nb = _interleave(d)

            def stage(r, off):
                for c0 in range(0, L, CH):
                    n = min(CH, L)
                    rows = _sub_rows(r, d, c0, n)
                    dst = pl.ds(off + c0, n)
                    qs[dst, :] = qn[rows, :].astype(BF16)
                    ks[dst, :] = kn[rows, :].astype(BF16)
                    vs[dst, :] = v_ref[rows, :].astype(BF16)

            def one(off, i):
                W, q0, k0, valid = _band_window(i, L)
                q2 = _stack_heads(qs[pl.ds(off + q0, TQ), :], lo)
                sc = jnp.where(valid, _dot_nt(q2, ks[pl.ds(off + k0, W), :]), NEG_INF)
                m = jnp.max(sc, axis=-1, keepdims=True)
                p = jnp.exp(sc - m)
                den = jnp.sum(p, axis=-1, keepdims=True)
                o2 = _dot(p.astype(BF16), vs[pl.ds(off + k0, W), :]) / den
                l2 = jnp.broadcast_to(m + jnp.log(den), (2 * TQ, LANES))
                osub[pl.ds(off + q0, TQ), :] = _unstack_heads(o2, lo)
                lsub[pl.ds(off + q0, TQ), :] = _unstack_heads(l2, lo)

            def unstage(r, off):
                for c0 in range(0, L, CH):
                    n = min(CH, L)
                    rows = _sub_rows(r, d, c0, n)
                    onat[gi, rows, :] = osub[pl.ds(off + c0, n), :]
                    lnat[gi, rows, :] = lsub[pl.ds(off + c0, n), :]

            def step(t, _):
                for u in range(ru):
                    stage(t * ru + u, u * L)
                _for_blocks(L // TQ // nb, lambda j: [one(u * L, j * nb + b) for u in range(ru) for b in range(nb)])
                for u in range(ru):
                    unstage(t * ru + u, u * L)
                return 0

            lax.fori_loop(0, d // ru, step, 0)

        for gi, d in enumerate(DILATIONS):
            pl.when(g == gi)(functools.partial(group, gi, d))

        @pl.when(g == len(DILATIONS) - 1)
        def _():
            def mix(i, _):
                rows = pl.ds(pl.multiple_of(i * CH, CH), CH)
                l0, l1, l2 = lnat[0, rows, :], lnat[1, rows, :], lnat[2, rows, :]
                m = jnp.maximum(jnp.maximum(l0, l1), l2)
                e0, e1, e2 = jnp.exp(l0 - m), jnp.exp(l1 - m), jnp.exp(l2 - m)
                den = e0 + e1 + e2
                a = (e0 * onat[0, rows, :] + e1 * onat[1, rows, :] + e2 * onat[2, rows, :]) / den
                at_ref[rows, :] = a.astype(BF16)
                ls_ref[rows, :] = m + jnp.log(den)
                return 0

            lax.fori_loop(0, S // CH, mix, 0)

    out_spec = pl.BlockSpec((S, LANES), lambda hp, g: (0, hp))
    return _call(
        body, sides, name="attn_fwd", grid=(4, 3),
        in_specs=_qk_specs() + _tab_specs() + [_vec_spec(), _vec_spec()],
        out_specs=[out_spec, out_spec],
        out_shape=[jax.ShapeDtypeStruct((S, CC), BF16), jax.ShapeDtypeStruct((S, CC), F32)],
        scratch_shapes=[pltpu.VMEM((S, LANES), BF16)] * 3 + [pltpu.VMEM((S, LANES), F32)] * 2
        + [pltpu.VMEM((3, S, LANES), F32)] * 2 + [pltpu.VMEM((S, LANES), F32)] * 2,
        args=(proj, proj, proj, *tabs, qw2, kw2))


def attn_bwd(proj, tabs, qw2, kw2, d_attn, attn, lse, sides=()):
    CH = 256

    def body(q_ref, k_ref, v_ref, c_ref, s1_ref, s2_ref, qw_ref, kw_ref, do_ref, at_ref, ls_ref,
             dq_ref, dk_ref, dv_ref, gqw_ref, gkw_ref,
             qs, ks, vs, dos, dsub, lsub, dqs, dks, dvs, dnat, qx, kx, dvn, tnq, tnk, rrq, rrk):
        hp, g = pl.program_id(0), pl.program_id(1)
        lo = lax.broadcasted_iota(jnp.int32, (1, LANES), 1) < HD
        e = _head_mat()
        both = ((q_ref, qw_ref, qx, tnq, rrq, HD ** -0.5), (k_ref, kw_ref, kx, tnk, rrk, 1.0))

        @pl.when((hp == 0) & (g == 0))
        def _():
            gqw_ref[...] = jnp.zeros_like(gqw_ref)
            gkw_ref[...] = jnp.zeros_like(gkw_ref)

        def prep(i, _):
            rows = pl.ds(pl.multiple_of(i * CH, CH), CH)
            dnat[rows, :] = _head_mean(do_ref[rows, :] * at_ref[rows, :].astype(F32), e) * float(HD)
            c, s1, s2 = c_ref[rows, :], s1_ref[rows, :], s2_ref[rows, :]
            for t_ref, w_ref, x, tn_s, rr_s, scale in both:
                t = t_ref[rows, :]
                rr = lax.rsqrt(_head_mean(t * t, e) + EPS)
                tn = t * rr
                rr_s[rows, :] = rr
                tn_s[rows, :] = tn
                x[rows, :] = _rope(tn * w_ref[...], c, s1, s2) * scale
            return 0

        lax.fori_loop(0, S // CH, prep, 0, unroll=4)

        def group(d):
            L = S // d

            ru, nb = _interleave(d)

            def stage(r, off):
                for c0 in range(0, L, CH):
                    n = min(CH, L)
                    rows = _sub_rows(r, d, c0, n)
                    dst = pl.ds(off + c0, n)
                    qs[dst, :] = qx[rows, :].astype(BF16)
                    ks[dst, :] = kx[rows, :].astype(BF16)
                    vs[dst, :] = v_ref[rows, :].astype(BF16)
                    dos[dst, :] = do_ref[rows, :].astype(BF16)
                    dsub[dst, :] = dnat[rows, :]
                    lsub[dst, :] = ls_ref[rows, :]
                    dks[dst, :] = jnp.zeros((n, LANES), F32)
                    dvs[dst, :] = jnp.zeros((n, LANES), F32)

            def one(off, i):
                W, q0, k0, valid = _band_window(i, L)
                qrows, krows = pl.ds(off + q0, TQ), pl.ds(off + k0, W)
                q2 = _stack_heads(qs[qrows, :], lo)
                do2 = _stack_heads(dos[qrows, :], lo)
                kk, vv = ks[krows, :], vs[krows, :]
                lse_b, dd_b = lsub[qrows, :], dsub[qrows, :]
                lse2 = jnp.concatenate([lse_b[:, 0:1], lse_b[:, HD:HD + 1]], axis=0)
                dd2 = jnp.concatenate([dd_b[:, 0:1], dd_b[:, HD:HD + 1]], axis=0)
                sc = jnp.where(valid, _dot_nt(q2, kk), NEG_INF)
                p = jnp.exp(sc - lse2)
                ds = (p * (_dot_nt(do2, vv) - dd2)).astype(BF16)
                dqs[qrows, :] = _unstack_heads(_dot(ds, kk), lo)
                dks[krows, :] = dks[krows, :] + _dot_tn(ds, q2)
                dvs[krows, :] = dvs[krows, :] + _dot_tn(p.astype(BF16), do2)

            def unstage(r, off):
                for c0 in range(0, L, CH):
                    n = min(CH, L)
                    rows = _sub_rows(r, d, c0, n)
                    src = pl.ds(off + c0, n)
                    qx[rows, :] = dqs[src, :]
                    kx[rows, :] = dks[src, :]
                    dvn[rows, :] = dvs[src, :]

            def step(t, _):
                for u in range(ru):
                    stage(t * ru + u, u * L)
                _for_blocks(L // TQ // nb, lambda j: [one(u * L, j * nb + b) for u in range(ru) for b in range(nb)])
                for u in range(ru):
                    unstage(t * ru + u, u * L)
                return 0

            lax.fori_loop(0, d // ru, step, 0)

        for gi, d in enumerate(DILATIONS):
            pl.when(g == gi)(functools.partial(group, d))

        def emit(i, _):
            rows = pl.ds(pl.multiple_of(i * CH, CH), CH)
            c, s1, s2 = c_ref[rows, :], s1_ref[rows, :], s2_ref[rows, :]
            for (_, w_ref, x, tn_s, rr_s, scale), out, gw_ref in zip(both, (dq_ref, dk_ref), (gqw_ref, gkw_ref)):
                tn = tn_s[rows, :]
                dy = _rope_t(x[rows, :] * scale, c, s1, s2)
                gw_ref[0:1, :] = gw_ref[0:1, :] + jnp.sum(dy * tn, axis=0, keepdims=True)
                dtn = dy * w_ref[...]
                out[rows, :] = (rr_s[rows, :] * (dtn - tn * _head_mean(dtn * tn, e))).astype(BF16)
            dv_ref[rows, :] = dvn[rows, :].astype(BF16)
            return 0

        lax.fori_loop(0, S // CH, emit, 0, unroll=4)

    nat_spec = pl.BlockSpec((S, LANES), lambda hp, g: (0, hp))
    out_spec = pl.BlockSpec((None, S, LANES), lambda hp, g: (g, 0, hp))
    acc_spec = pl.BlockSpec((8, LANES), lambda hp, g: (0, 0))
    return _call(
        body, sides, name="attn_bwd", grid=(4, 3),
        in_specs=_qk_specs() + _tab_specs() + [_vec_spec(), _vec_spec(), nat_spec, nat_spec, nat_spec],
        out_specs=[out_spec] * 3 + [acc_spec] * 2,
        out_shape=[jax.ShapeDtypeStruct((QKV // PLANE, S, PLANE), BF16)] * 3 + [jax.ShapeDtypeStruct((8, LANES), F32)] * 2,
        scratch_shapes=[pltpu.VMEM((S, LANES), BF16)] * 4 + [pltpu.VMEM((S, LANES), F32)] * 13,
        args=(proj, proj, proj, *tabs, qw2, kw2, d_attn, attn, lse))


PADR = 16
CT = 128


def _conv_specs():
    return [pl.BlockSpec((S, CC), lambda i: (0, OFF_CA // CC)), pl.BlockSpec((S, CC), lambda i: (0, OFF_CB // CC))]


NCB = CC // LANES


def _pad_zero(pad):
    for cb in range(NCB):
        pad[cb, 0:PADR, :] = jnp.zeros((PADR, LANES), F32)
        pad[cb, PADR + S:PADR + S + PADR, :] = jnp.zeros((PADR, LANES), F32)


def _pad_store(pad, row0, n, val):
    for cb in range(NCB):
        pad[cb, pl.ds(pl.multiple_of(row0 + PADR, 8), n), :] = val[:, cb * LANES:(cb + 1) * LANES]


def _taps(pad_ref, cb, s0, weights):
    acc = jnp.zeros((CT, LANES), F32)
    for k in range(KW):
        acc = acc + weights[k] * pad_ref[cb, pl.ds(s0 + k + 1, CT), :]
    return acc


def conv_fwd(proj, conv_w, conv_b, ln_w, ln_b):
    def body(a_ref, b_ref, w_ref, cb_ref, lw_ref, lb_ref, c_ref, u3_ref, upad):
        _pad_zero(upad)

        def glu(i, _):
            rows = pl.ds(pl.multiple_of(i * TM, TM), TM)
            _pad_store(upad, i * TM, TM, a_ref[rows, :] * _sigmoid(b_ref[rows, :]))
            return 0

        lax.fori_loop(0, S // TM, glu, 0)

        def chunk(i, _):
            s0 = pl.multiple_of(i * CT, CT)
            for cb in range(CC // LANES):
                cols = slice(cb * LANES, (cb + 1) * LANES)
                w = [w_ref[k:k + 1, cols] for k in range(KW)]
                c_ref[pl.ds(s0, CT), cols] = _taps(upad, cb, s0, w) + cb_ref[:, cols]
            cv = c_ref[pl.ds(s0, CT), :]
            mu = jnp.mean(cv, axis=-1, keepdims=True)
            xc = cv - mu
            rstd = lax.rsqrt(jnp.mean(xc * xc, axis=-1, keepdims=True) + EPS)
            yl = xc * rstd * lw_ref[...] + lb_ref[...]
            u3_ref[pl.ds(s0, CT), :] = (yl * _sigmoid(yl)).astype(BF16)
            return 0

        lax.fori_loop(0, S // CT, chunk, 0)

    vec = pl.BlockSpec((1, CC), lambda i: (0, 0))
    full = pl.BlockSpec((S, CC), lambda i: (0, 0))
    return _call(
        body, name="conv_fwd", grid=(1,),
        in_specs=_conv_specs() + [pl.BlockSpec((KW, CC), lambda i: (0, 0)), vec, vec, vec],
        out_specs=[full, full],
        out_shape=[jax.ShapeDtypeStruct((S, CC), F32), jax.ShapeDtypeStruct((S, CC), BF16)],
        scratch_shapes=[pltpu.VMEM((NCB, S + 2 * PADR, LANES), F32)],
        args=(proj, proj, conv_w, conv_b, ln_w, ln_b))


def conv_bwd(proj, cpre, d_u3, conv_w, ln_w, ln_b, sides=()):
    def body(a_ref, b_ref, c_ref, du3_ref, w_ref, lw_ref, lb_ref,
             dc_ref, gw_ref, gcb_ref, glw_ref, glb_ref, upad, dpad):
        _pad_zero(upad)
        _pad_zero(dpad)
        gw_ref[...] = jnp.zeros_like(gw_ref)

        def ln_bwd(i, carry):
            gcb, glw, glb = carry
            rows = pl.ds(pl.multiple_of(i * TM, TM), TM)
            _pad_store(upad, i * TM, TM, a_ref[rows, :] * _sigmoid(b_ref[rows, :]))
            cv = c_ref[rows, :]
            mu = jnp.mean(cv, axis=-1, keepdims=True)
            xc = cv - mu
            rstd = lax.rsqrt(jnp.mean(xc * xc, axis=-1, keepdims=True) + EPS)
            xh = xc * rstd
            yl = xh * lw_ref[...] + lb_ref[...]
            dyl = du3_ref[rows, :] * _dsilu(yl, _sigmoid(yl))
            dxh = dyl * lw_ref[...]
            dcv = rstd * (dxh - jnp.mean(dxh, axis=-1, keepdims=True)
                          - xh * jnp.mean(dxh * xh, axis=-1, keepdims=True))
            _pad_store(dpad, i * TM, TM, dcv)
            return (gcb + jnp.sum(dcv, axis=0, keepdims=True),
                    glw + jnp.sum(dyl * xh, axis=0, keepdims=True),
                    glb + jnp.sum(dyl, axis=0, keepdims=True))

        z = jnp.zeros((1, CC), F32)
        gcb, glw, glb = lax.fori_loop(0, S // TM, ln_bwd, (z, z, z))
        gcb_ref[...] = gcb
        glw_ref[...] = glw
        glb_ref[...] = glb

        def chunk(i, _):
            s0 = pl.multiple_of(i * CT, CT)
            for cb in range(CC // LANES):
                cols = slice(cb * LANES, (cb + 1) * LANES)
                wr = [w_ref[KW - 1 - k:KW - k, cols] for k in range(KW)]
                du = _taps(dpad, cb, s0, wr)
                dcv = dpad[cb, pl.ds(s0 + PADR, CT), :]
                for k in range(KW):
                    gw_ref[k:k + 1, cols] = gw_ref[k:k + 1, cols] + jnp.sum(
                        upad[cb, pl.ds(s0 + k + 1, CT), :] * dcv, axis=0, keepdims=True)
                av = a_ref[pl.ds(s0, CT), cols]
                sb = _sigmoid(b_ref[pl.ds(s0, CT), cols])
                dc_ref[0, pl.ds(s0, CT), cols] = (du * sb).astype(BF16)
                dc_ref[1, pl.ds(s0, CT), cols] = (du * av * sb * (1.0 - sb)).astype(BF16)
            return 0

        lax.fori_loop(0, S // CT, chunk, 0)

    vec = pl.BlockSpec((1, CC), lambda i: (0, 0))
    full = pl.BlockSpec((S, CC), lambda i: (0, 0))
    wsp = pl.BlockSpec((KW, CC), lambda i: (0, 0))
    return _call(
        body, sides, name="conv_bwd", grid=(1,),
        in_specs=_conv_specs() + [full, full, wsp, vec, vec],
        out_specs=[pl.BlockSpec((2, S, CC), lambda i: (0, 0, 0)), wsp, vec, vec, vec],
        out_shape=[jax.ShapeDtypeStruct((2, S, CC), BF16), jax.ShapeDtypeStruct((KW, CC), F32)]
        + [jax.ShapeDtypeStruct((1, CC), F32)] * 3,
        scratch_shapes=[pltpu.VMEM((NCB, S + 2 * PADR, LANES), F32)] * 2,
        args=(proj, proj, cpre, d_u3, conv_w, ln_w, ln_b))


def _gate_specs():
    return [_row(CC, col=OFF_GA // CC + j) for j in range(4)]


def _gates(g_refs, bg_ref):
    ga = _sigmoid(jnp.concatenate([g_refs[0][...], g_refs[1][...]], axis=1) + bg_ref[0:1, :])
    gb = _sigmoid(jnp.concatenate([g_refs[2][...], g_refs[3][...]], axis=1) + bg_ref[1:2, :])
    return ga, gb


def mix_out(x, proj, b_gate, attn, u3, w_o, w_pw, w_out):
    def body(x_ref, g0, g1, g2, g3, bg_ref, at_ref, u3_ref, wo_ref, wp_ref, wout_ref,
             x1_ref, z_ref, ya_ref, yb_ref):
        ga, gb = _gates((g0, g1, g2, g3), bg_ref)
        ya = _dot_nt(at_ref[...], wo_ref[...])
        yb = _dot_nt(u3_ref[...], wp_ref[...])
        z = (ga * ya + gb * yb).astype(BF16)
        ya_ref[...] = ya.astype(BF16)
        yb_ref[...] = yb.astype(BF16)
        z_ref[...] = z
        x1_ref[...] = x_ref[...] + _dot(z, wout_ref[...])

    return pl.pallas_call(
        body, name="mix_out", grid=(S // TM,),
        in_specs=[_row(D)] + _gate_specs() + [_res((2, D)), _row(CC), _row(CC),
                                              _res((D, CC)), _res((D, CC)), _res((D, D))],
        out_specs=[_row(D)] * 4,
        out_shape=[jax.ShapeDtypeStruct((S, D), F32)] + [jax.ShapeDtypeStruct((S, D), BF16)] * 3,
        compiler_params=_cp(dimension_semantics=("arbitrary",)),
    )(x, proj, proj, proj, proj, b_gate, attn, u3, w_o, w_pw, w_out)


def out_bwd(d_x1b, proj, b_gate, ya, yb, w_o, w_pw, w_out, sides=()):
    def body(dx_ref, g0, g1, g2, g3, bg_ref, ya_ref, yb_ref, wo_ref, wp_ref, wout_ref,
             dya_ref, dyb_ref, dgl_ref, dat_ref, du3_ref, gbg_ref):
        @pl.when(pl.program_id(0) == 0)
        def _():
            gbg_ref[...] = jnp.zeros_like(gbg_ref)

        ga, gb = _gates((g0, g1, g2, g3), bg_ref)
        dz = _dot_nt(dx_ref[...], wout_ref[...])
        dya = (dz * ga).astype(BF16)
        dyb = (dz * gb).astype(BF16)
        dgla = dz * ya_ref[...].astype(F32) * ga * (1.0 - ga)
        dglb = dz * yb_ref[...].astype(F32) * gb * (1.0 - gb)
        dya_ref[...] = dya
        dyb_ref[...] = dyb
        for j in range(2):
            dgl_ref[j] = dgla[:, j * PLANE:(j + 1) * PLANE].astype(BF16)
            dgl_ref[2 + j] = dglb[:, j * PLANE:(j + 1) * PLANE].astype(BF16)
        gbg_ref[0:1, :] = gbg_ref[0:1, :] + jnp.sum(dgla, axis=0, keepdims=True)
        gbg_ref[1:2, :] = gbg_ref[1:2, :] + jnp.sum(dglb, axis=0, keepdims=True)
        dat_ref[...] = _dot(dya, wo_ref[...])
        du3_ref[...] = _dot(dyb, wp_ref[...])

    return _call(
        body, sides, name="out_bwd", grid=(S // TM,),
        in_specs=[_row(D)] + _gate_specs() + [_res((2, D)), _row(D), _row(D),
                                              _res((D, CC)), _res((D, CC)), _res((D, D))],
        out_specs=[_row(D), _row(D), _planes(2 * D), _row(CC), _row(CC), pl.BlockSpec((2, D), lambda i: (0, 0))],
        out_shape=[jax.ShapeDtypeStruct((S, D), BF16)] * 2 + [jax.ShapeDtypeStruct((2 * D // PLANE, S, PLANE), BF16)]
        + [jax.ShapeDtypeStruct((S, CC), F32)] * 2 + [jax.ShapeDtypeStruct((2, D), F32)],
        args=(d_x1b, proj, proj, proj, proj, b_gate, ya, yb, w_o, w_pw, w_out))


def ffn_in(x1, norm_w, w_ffn_in, sides=()):
    half = FF // 2

    def body(x_ref, nw_ref, w_ref, h_ref, gu_ref, f_ref):
        xv = x_ref[...]
        r = lax.rsqrt(jnp.mean(xv * xv, axis=-1, keepdims=True) + EPS)
        h = (xv * r * nw_ref[...]).astype(BF16)
        h_ref[...] = h
        for j in range(2):
            gt = _dot_nt(h, w_ref[j * half:(j + 1) * half, :])
            up = _dot_nt(h, w_ref[FF + j * half:FF + (j + 1) * half, :])
            gu_ref[:, j * half:(j + 1) * half] = gt.astype(BF16)
            gu_ref[:, FF + j * half:FF + (j + 1) * half] = up.astype(BF16)
            f_ref[:, j * half:(j + 1) * half] = (gt * _sigmoid(gt) * up).astype(BF16)

    return _call(
        body, sides, name="ffn_in", grid=(S // TM,),
        in_specs=[_row(D), _res((1, D)), _res((2 * FF, D))],
        out_specs=[_row(D), _row(2 * FF), _row(FF)],
        out_shape=[jax.ShapeDtypeStruct((S, D), BF16), jax.ShapeDtypeStruct((S, 2 * FF), BF16),
                   jax.ShapeDtypeStruct((S, FF), BF16)],
        args=(x1, norm_w, w_ffn_in))


def ffn_out_loss(x1, f, w_ffn_out, target):
    def body(x_ref, f_ref, w_ref, t_ref, dy_ref, dyb_ref, sq_ref):
        @pl.when(pl.program_id(0) == 0)
        def _():
            sq_ref[...] = jnp.zeros_like(sq_ref)

        diff = x_ref[...] + _dot(f_ref[...], w_ref[...]) - t_ref[...]
        dy = diff * (1.0 / D)
        dy_ref[...] = dy
        dyb_ref[...] = dy.astype(BF16)
        sq_ref[...] = sq_ref[...] + jnp.sum((diff * diff).reshape(TM // 8, 8, D), axis=0)

    return pl.pallas_call(
        body, name="ffn_out_loss", grid=(S // TM,),
        in_specs=[_row(D), _row(FF), _res((FF, D)), _row(D)],
        out_specs=[_row(D), _row(D), pl.BlockSpec((8, D), lambda i: (0, 0))],
        out_shape=[jax.ShapeDtypeStruct((S, D), F32), jax.ShapeDtypeStruct((S, D), BF16),
                   jax.ShapeDtypeStruct((8, D), F32)],
        compiler_params=_cp(dimension_semantics=("arbitrary",)),
    )(x1, f, w_ffn_out, target)


def _rms_bwd(xv, nw, dh):
    r = lax.rsqrt(jnp.mean(xv * xv, axis=-1, keepdims=True) + EPS)
    xn = xv * r
    dxn = dh * nw
    dx = r * (dxn - xn * jnp.mean(dxn * xn, axis=-1, keepdims=True))
    return dx, dh * xn


def ffn_bwd(dy, dyb, gu, x1, norm_w, w_ffn_in, w_ffn_out, sides=()):
    def body(dy_ref, dyb_ref, gu_ref, x_ref, nw_ref, wi_ref, wo_ref, dgu_ref, dx_ref, dxb_ref, gn_ref):
        @pl.when(pl.program_id(0) == 0)
        def _():
            gn_ref[...] = jnp.zeros_like(gn_ref)

        df = _dot_nt(dyb_ref[...], wo_ref[...])
        gt = gu_ref[:, 0:FF].astype(F32)
        up = gu_ref[:, FF:2 * FF].astype(F32)
        sg = _sigmoid(gt)
        dgt = (df * up * _dsilu(gt, sg)).astype(BF16)
        dup = (df * gt * sg).astype(BF16)
        dgu_ref[:, 0:FF] = dgt
        dgu_ref[:, FF:2 * FF] = dup
        dh = _dot(dgt, wi_ref[0:FF, :]) + _dot(dup, wi_ref[FF:2 * FF, :])
        dxn, gw = _rms_bwd(x_ref[...], nw_ref[...], dh)
        dx = dy_ref[...] + dxn
        dx_ref[...] = dx
        dxb_ref[...] = dx.astype(BF16)
        gn_ref[...] = gn_ref[...] + jnp.sum(gw, axis=0, keepdims=True)

    return _call(
        body, sides, name="ffn_bwd", grid=(S // TM,),
        in_specs=[_row(D), _row(D), _row(2 * FF), _row(D), _res((1, D)), _res((2 * FF, D)), _res((FF, D))],
        out_specs=[_row(2 * FF), _row(D), _row(D), pl.BlockSpec((1, D), lambda i: (0, 0))],
        out_shape=[jax.ShapeDtypeStruct((S, 2 * FF), BF16), jax.ShapeDtypeStruct((S, D), F32),
                   jax.ShapeDtypeStruct((S, D), BF16), jax.ShapeDtypeStruct((1, D), F32)],
        args=(dy, dyb, gu, x1, norm_w, w_ffn_in, w_ffn_out))


def in_bwd(d_q, d_k, d_v, d_conv, d_gl, w_in, x, d_x1, norm_w, sides=()):
    segs = ((OFF_Q, QKV), (OFF_K, QKV), (OFF_V, QKV), (OFF_CA, 2 * CC), (OFF_GA, 2 * D))

    def body(dq_ref, dk_ref, dv_ref, dc_ref, dg_ref, w_ref, x_ref, dx1_ref, nw_ref, gx_ref, gn_ref):
        @pl.when(pl.program_id(0) == 0)
        def _():
            gn_ref[...] = jnp.zeros_like(gn_ref)

        dh = jnp.zeros((TM, D), F32)
        for ref, (off, width) in zip((dq_ref, dk_ref, dv_ref, dc_ref, dg_ref), segs):
            for j in range(width // PLANE):
                dh = dh + _dot(ref[j], w_ref[off + j * PLANE:off + (j + 1) * PLANE, :])
        dxn, gw = _rms_bwd(x_ref[...], nw_ref[...], dh)
        gx_ref[...] = dx1_ref[...] + dxn
        gn_ref[...] = gn_ref[...] + jnp.sum(gw, axis=0, keepdims=True)

    return _call(
        body, sides, name="in_bwd", grid=(S // TM,),
        in_specs=[_planes(QKV)] * 3 + [_planes(2 * CC), _planes(2 * D), _res((INW, D)), _row(D), _row(D), _res((1, D))],
        out_specs=[_row(D), pl.BlockSpec((1, D), lambda i: (0, 0))],
        out_shape=[jax.ShapeDtypeStruct((S, D), F32), jax.ShapeDtypeStruct((1, D), F32)],
        args=(d_q, d_k, d_v, d_conv, d_gl, w_in, x, d_x1, norm_w))


def mm_tn(name, pairs, tm):
    n = len(pairs)
    M = pairs[0][0].shape[1]
    widths = [b.shape[1] for _, b in pairs]

    def body(*refs):
        for a_ref, b_ref, o_ref, ob_ref in zip(refs[0:2 * n:2], refs[1:2 * n:2], refs[2 * n::2], refs[2 * n + 1::2]):
            r = _dot_tn(a_ref[...], b_ref[...])
            o_ref[...] = r
            ob_ref[...] = r.astype(BF16)

    return _call(
        body, name=name, grid=(M // tm,),
        in_specs=[sp for N in widths for sp in (pl.BlockSpec((S, tm), lambda i: (0, i)), _res((S, N)))],
        out_specs=[pl.BlockSpec((tm, N), lambda i: (i, 0)) for N in widths for _ in range(2)],
        out_shape=[jax.ShapeDtypeStruct((M, N), dt) for N in widths for dt in (F32, BF16)],
        args=[t for pair in pairs for t in pair])


GW_IN_TN = PLANE
GW_IN_SPLIT = (768, 256)


def gw_in(name, h, d_segs, col0, hw, sides=()):
    tn = GW_IN_TN
    starts, t0 = [], 0
    for seg in d_segs:
        starts.append(t0)
        t0 += seg.shape[0]
    ntiles = [seg.shape[0] for seg in d_segs]

    def body(h_ref, *refs):
        a_refs, o_ref, ob_ref = refs[:-2], refs[-2], refs[-1]
        n = pl.program_id(0)
        for a_ref, st, nt in zip(a_refs, starts, ntiles):
            @pl.when((n >= st) & (n < st + nt))
            def _(a_ref=a_ref):
                r = _dot_tn(a_ref[...], h_ref[...])
                o_ref[...] = r
                ob_ref[...] = r.astype(BF16)

    def seg_spec(st, nt):
        return pl.BlockSpec((None, S, tn), lambda n: (jnp.clip(n - st, 0, nt - 1), 0, 0))

    res = _call(
        body, sides, name=name, grid=(INW // tn,),
        in_specs=[pl.BlockSpec((S, hw), lambda n: (0, col0 // hw))] + [seg_spec(st, nt) for st, nt in zip(starts, ntiles)],
        out_specs=[pl.BlockSpec((tn, hw), lambda n: (n, 0))] * 2,
        out_shape=[jax.ShapeDtypeStruct((INW, hw), F32), jax.ShapeDtypeStruct((INW, hw), BF16)],
        args=(h, *d_segs))
    return (res[0], res[1]) if sides else (res, [])


def _place():
    x, y, c = lax.axis_index("x"), lax.axis_index("y"), lax.axis_index("c")
    chips = [(1 - x, y), (x, 1 - y), (1 - x, 1 - y)]
    return x, y, c, chips


def _sems(n):
    return pltpu.SemaphoreType.DMA((n,))


def _remote(src, dst, send, recv, k, to):
    return pltpu.make_async_remote_copy(src_ref=src, dst_ref=dst, send_sem=send.at[k], recv_sem=recv.at[k],
                                        device_id=to, device_id_type=MESH)


def _cast_rows(dst, src, cols=slice(None)):
    rows = src.shape[0]
    step = next((s for s in (128, 64, 32, 16) if rows % s == 0), rows)
    for r0 in range(0, rows, step):
        dst[r0:r0 + step, cols] = src[r0:r0 + step, :].astype(dst.dtype)


def comm_only(name, sides):
    def body():
        pass

    return _call(body, sides, name=name, grid=(1,), in_specs=[], out_specs=[], out_shape=[], args=())[1]


def ag_blocks(shard, dtype):
    R, W = shard.shape

    def copy(outs, scr, k, block, to, src=None):
        dst = outs[0].at[block]
        return _remote(dst if src is None else src, dst, scr[1], scr[2], k, to)

    def local(outs, scr, me):
        return pltpu.make_async_copy(scr[0], outs[0].at[me], scr[3].at[0])

    def start(ins, outs, scr):
        x, y, c, chips = _place()
        me = 4 * x + 2 * y + c
        _cast_rows(scr[0], ins[0])
        local(outs, scr, me).start()
        copy(outs, scr, 0, me, (x, y, 1 - c), src=scr[0]).start()
        for j, (cx, cy) in enumerate(chips):
            copy(outs, scr, 1 + j, me, (cx, cy, c), src=scr[0]).start()

    def finish(ins, outs, scr):
        x, y, c, chips = _place()
        me, sib = 4 * x + 2 * y + c, (x, y, 1 - c)
        passed = []
        for j, (cx, cy) in enumerate(chips):
            theirs = 4 * cx + 2 * cy + c
            copy(outs, scr, 1 + j, theirs, (x, y, c)).wait_recv()
            fwd = copy(outs, scr, 4 + j, theirs, sib)
            fwd.start()
            passed.append(fwd)
        copy(outs, scr, 0, 4 * x + 2 * y + 1 - c, (x, y, c)).wait_recv()
        for j, (cx, cy) in enumerate(chips):
            copy(outs, scr, 4 + j, 4 * cx + 2 * cy + 1 - c, (x, y, c)).wait_recv()
        copy(outs, scr, 0, me, sib, src=scr[0]).wait_send()
        for j, (cx, cy) in enumerate(chips):
            copy(outs, scr, 1 + j, me, (cx, cy, c), src=scr[0]).wait_send()
        for fwd in passed:
            fwd.wait_send()
        local(outs, scr, me).wait()

    return Side((shard,), (VMEM,), (jax.ShapeDtypeStruct((NDEV, R, W), dtype),),
                (pltpu.VMEM((R, W), dtype), _sems(7), _sems(7), _sems(1)), start, finish, None, "dsxy")


def ag_blocks_relay(shard, dtype, transpose=False):
    R, W = shard.shape[::-1] if transpose else shard.shape
    half = R // 2

    def copy(outs, scr, k, block, to, src=None, rows=None):
        dst = outs[0].at[block] if rows is None else outs[0].at[block, pl.ds(rows * half, half), :]
        return _remote(dst if src is None else src, dst, scr[1], scr[2], k, to)

    def local(outs, scr, me):
        return pltpu.make_async_copy(scr[0], outs[0].at[me], scr[3].at[0])

    def own(outs, scr):
        x, y, c, _ = _place()
        me = 4 * x + 2 * y + c
        return [copy(outs, scr, k, me, to, src=scr[0])
                for k, to in enumerate([(x, y, 1 - c), (1 - x, y, c), (x, 1 - y, c)])]

    def start(ins, outs, scr):
        x, y, c, _ = _place()
        if transpose:
            scr[0][...] = ins[0][...].T.astype(dtype)
        else:
            _cast_rows(scr[0], ins[0])
        local(outs, scr, 4 * x + 2 * y + c).start()
        for cp in own(outs, scr):
            cp.start()

    def passed_on(outs, scr):
        x, y, c, _ = _place()
        sib, xn, yn = (x, y, 1 - c), (1 - x, y, c), (x, 1 - y, c)
        b_xn, b_yn, b_dg = 4 * (1 - x) + 2 * y + c, 4 * x + 2 * (1 - y) + c, 4 * (1 - x) + 2 * (1 - y) + c
        near = [copy(outs, scr, 5, b_xn, yn, rows=0), copy(outs, scr, 3, b_xn, sib),
                copy(outs, scr, 6, b_yn, xn, rows=1), copy(outs, scr, 4, b_yn, sib)]
        far = [copy(outs, scr, 7, b_dg, sib, rows=0), copy(outs, scr, 8, b_dg, sib, rows=1)]
        return (b_xn, b_yn, b_dg), near, far

    def mid(ins, outs, scr):
        x, y, c, _ = _place()
        (b_xn, b_yn, _), near, _ = passed_on(outs, scr)
        copy(outs, scr, 1, b_xn, (x, y, c)).wait_recv()
        near[0].start()
        near[1].start()
        copy(outs, scr, 2, b_yn, (x, y, c)).wait_recv()
        near[2].start()
        near[3].start()

    def finish(ins, outs, scr):
        x, y, c, _ = _place()
        here = (x, y, c)
        (b_xn, b_yn, b_dg), near, far = passed_on(outs, scr)
        copy(outs, scr, 5, b_dg, here, rows=0).wait_recv()
        far[0].start()
        copy(outs, scr, 6, b_dg, here, rows=1).wait_recv()
        far[1].start()
        flip = 1 - 2 * c
        copy(outs, scr, 0, 4 * x + 2 * y + 1 - c, here).wait_recv()
        copy(outs, scr, 3, b_xn + flip, here).wait_recv()
        copy(outs, scr, 4, b_yn + flip, here).wait_recv()
        copy(outs, scr, 7, b_dg + flip, here, rows=0).wait_recv()
        copy(outs, scr, 8, b_dg + flip, here, rows=1).wait_recv()
        for cp in own(outs, scr) + near + far:
            cp.wait_send()
        local(outs, scr, 4 * x + 2 * y + c).wait()

    return Side((shard,), (VMEM,), (jax.ShapeDtypeStruct((NDEV, R, W), dtype),),
                (pltpu.VMEM((R, W), dtype), _sems(9), _sems(9), _sems(1)), start, finish, mid, "sxy")


def copies_side(args, out_shape, n_copies, plan, peers):
    def copies(ins, outs, scr):
        return [_remote(s_, d_, scr[0], scr[1], i, to) for i, (s_, d_, to) in enumerate(plan(ins, outs))]

    def start(ins, outs, scr):
        for cp in copies(ins, outs, scr):
            cp.start()

    def finish(ins, outs, scr):
        for cp in copies(ins, outs, scr):
            cp.wait()

    return Side(tuple(args), (ANY,) * len(args), tuple(out_shape), (_sems(n_copies), _sems(n_copies)),
                start, finish, None, peers)


def rs_to_sibling(grads):
    out_shape = [jax.ShapeDtypeStruct((4,) + g.shape[1:], BF16) for g in grads]

    def plan(ins, outs):
        x, y, c, _ = _place()
        return [(g.at[2 * k + 1 - c], r.at[k], (x, y, 1 - c)) for g, r in zip(ins, outs) for k in range(4)]

    return copies_side(grads, out_shape, 4 * len(grads), plan, "s")


def rs_to_chips(parts):
    out_shape = [jax.ShapeDtypeStruct((3,) + p.shape[1:], BF16) for p in parts]

    def plan(ins, outs):
        x, y, c, chips = _place()
        return [(p.at[2 * cx + cy], r.at[j], (cx, cy, c))
                for p, r in zip(ins, outs) for j, (cx, cy) in enumerate(chips)]

    return copies_side(parts, out_shape, 3 * len(parts), plan, "dxy")


def rs_to_chips_combined(part):
    _, R, W = part.shape
    half = R // 2
    top, bot = pl.ds(0, half), pl.ds(half, half)

    def copies(ins, outs, scr):
        p, r = ins[0], outs[0]
        loc_a, loc_b, in_x, in_y, comb_a, comb_b, send, recv, loc = scr
        x, y, c, _ = _place()
        xn, yn = (1 - x, y, c), (x, 1 - y, c)
        k_xn, k_yn, k_dg = 2 * (1 - x) + y, 2 * x + 1 - y, 2 * (1 - x) + 1 - y
        direct = [_remote(p.at[k_xn, top, :], r.at[0, top, :], send, recv, 0, xn),
                  _remote(p.at[k_yn, bot, :], r.at[1, bot, :], send, recv, 1, yn),
                  _remote(p.at[k_dg, top, :], in_x, send, recv, 2, xn),
                  _remote(p.at[k_dg, bot, :], in_y, send, recv, 3, yn)]
        combined = [_remote(comb_a, r.at[1, top, :], send, recv, 4, yn),
                    _remote(comb_b, r.at[0, bot, :], send, recv, 5, xn)]
        local = [pltpu.make_async_copy(p.at[k_yn, top, :], loc_a, loc.at[0]),
                 pltpu.make_async_copy(p.at[k_xn, bot, :], loc_b, loc.at[1])]
        return direct, combined, local

    def start(ins, outs, scr):
        direct, _, local = copies(ins, outs, scr)
        for cp in local + direct:
            cp.start()

    def mid(ins, outs, scr):
        loc_a, loc_b, in_x, in_y, comb_a, comb_b = scr[:6]
        direct, combined, local = copies(ins, outs, scr)
        for mine, arrival, inbox, out, nxt in ((local[0], direct[2], in_x, comb_a, combined[0]),
                                               (local[1], direct[3], in_y, comb_b, combined[1])):
            mine.wait()
            arrival.wait_recv()
            src = loc_a if out is comb_a else loc_b
            out[...] = (src[...].astype(F32) + inbox[...].astype(F32)).astype(BF16)
            nxt.start()

    def finish(ins, outs, scr):
        direct, combined, _ = copies(ins, outs, scr)
        direct[0].wait_recv()
        direct[1].wait_recv()
        combined[0].wait_recv()
        combined[1].wait_recv()
        for cp in direct + combined:
            cp.wait_send()

    buf = pltpu.VMEM((half, W), BF16)
    return Side((part,), (ANY,), (jax.ShapeDtypeStruct((2, R, W), BF16),),
                (buf, buf, buf, buf, buf, buf, _sems(6), _sems(6), _sems(2)), start, finish, mid, "xy")


ADAM_TILE_BYTES = 3 * 512 * 1024


def _row_tiles(rows, width):
    return 2 if rows % 32 == 0 and rows * width * 4 > ADAM_TILE_BYTES else 1


def chip_sum(name, grads, recvs, c_idx, chip_idx):
    n = len(grads)

    def body(s_ref, *refs):
        k = pl.program_id(0)
        for g_ref, r_ref, p_ref, own_ref in zip(refs[:n], refs[n:2 * n], refs[2 * n::2], refs[2 * n + 1::2]):
            tot = g_ref[0] + r_ref[0].astype(F32)
            p_ref[0] = tot.astype(BF16)

            @pl.when(k == s_ref[1])
            def _(own_ref=own_ref, tot=tot):
                own_ref[...] = tot

    def block(g):
        return (1,) + g.shape[1:]

    grid_spec = pltpu.PrefetchScalarGridSpec(
        num_scalar_prefetch=1, grid=(4,),
        in_specs=[pl.BlockSpec(block(g), lambda k, s: (2 * k + s[0], 0, 0)) for g in grads]
        + [pl.BlockSpec(block(g), lambda k, s: (k, 0, 0)) for g in grads],
        out_specs=[sp for g in grads for sp in (pl.BlockSpec(block(g), lambda k, s: (k, 0, 0)),
                                                pl.BlockSpec(g.shape[1:], lambda k, s: (0, 0)))])
    res = pl.pallas_call(
        body, name=name, grid_spec=grid_spec,
        out_shape=[sh for g in grads for sh in (jax.ShapeDtypeStruct((4,) + g.shape[1:], BF16),
                                                jax.ShapeDtypeStruct(g.shape[1:], F32))],
        compiler_params=_cp(dimension_semantics=("arbitrary",)),
    )(jnp.stack([c_idx, chip_idx]), *grads, *recvs)
    return [(res[2 * j], res[2 * j + 1]) for j in range(n)]


def _adamw(w, g, m, v):
    m2 = ADAM_B1 * m + (1.0 - ADAM_B1) * g
    v2 = ADAM_B2 * v + (1.0 - ADAM_B2) * (g * g)
    m_hat = m2 / (1.0 - ADAM_B1 ** ADAM_STEP)
    v_hat = v2 / (1.0 - ADAM_B2 ** ADAM_STEP)
    delta = -ADAM_LR * (m_hat / (jnp.sqrt(v_hat) + ADAM_EPS) + ADAM_WD * w)
    return delta, m2, v2


def shard_adam(name, owns, recvs, w, m, v):
    n = len(owns)
    R = owns[0].shape[0]
    ct = min(o.shape[1] for o in owns)
    first = [sum(o.shape[1] for o in owns[:j]) // ct for j in range(n)]
    count = [o.shape[1] // ct for o in owns]
    nt = _row_tiles(R, ct)
    tr = R // nt

    def body(*refs):
        o_refs, r_refs = refs[:n], refs[n:2 * n]
        w_ref, m_ref, v_ref, g_ref, d_ref, nm_ref, nv_ref = refs[2 * n:]
        g = None
        for j in range(n):
            gj = o_refs[j][...]
            for q in range(recvs[j].shape[0]):
                gj = gj + r_refs[j][q].astype(F32)
            g = gj if g is None else jnp.where(pl.program_id(0) >= first[j], gj, g)
        delta, m2, v2 = _adamw(w_ref[...], g, m_ref[...], v_ref[...])
        g_ref[...] = g
        d_ref[...] = delta
        nm_ref[...] = m2
        nv_ref[...] = v2

    def part(j):
        return pl.BlockSpec((tr, ct), lambda k, i: (i, jnp.clip(k - first[j], 0, count[j] - 1)))

    def part3(j):
        return pl.BlockSpec((recvs[j].shape[0], tr, ct), lambda k, i: (0, i, jnp.clip(k - first[j], 0, count[j] - 1)))

    C = sum(count) * ct
    tile = pl.BlockSpec((tr, ct), lambda k, i: (i, k))
    return pl.pallas_call(
        body, name=name, grid=(sum(count), nt),
        in_specs=[part(j) for j in range(n)] + [part3(j) for j in range(n)] + [tile, tile, tile],
        out_specs=[tile] * 4, out_shape=[jax.ShapeDtypeStruct((R, C), F32)] * 4,
        compiler_params=_cp(dimension_semantics=("arbitrary", "arbitrary")),
    )(*owns, *recvs, w, m, v)


def block_adam(name, items):
    n = len(items)

    def body(*refs):
        for j, item in enumerate(items):
            o_ref, r_ref, w_ref, m_ref, v_ref = refs[5 * j:5 * j + 5]
            g = o_ref[...]
            for q in range(r_ref.shape[0]):
                g = g + r_ref[q].astype(F32)
            t = (lambda a: a.T) if item[5] else (lambda a: a)
            delta, m2, v2 = _adamw(t(w_ref[...]), g, t(m_ref[...]), t(v_ref[...]))
            for ref, val in zip(refs[5 * n + 4 * j:5 * n + 4 * j + 4], (g, delta, m2, v2)):
                ref[...] = t(val)

    args = [a for item in items for a in item[:5]]
    out_shape = [jax.ShapeDtypeStruct(item[2].shape, F32) for item in items for _ in range(4)]
    res = pl.pallas_call(
        body, name=name, grid=(1,), in_specs=[VMEM] * len(args), out_specs=[VMEM] * len(out_shape),
        out_shape=out_shape, compiler_params=_cp(dimension_semantics=("arbitrary",)))(*args)
    return [tuple(res[4 * j:4 * j + 4]) for j in range(n)]


ROW_N1, ROW_N2, ROW_BG, ROW_QN, ROW_KN, ROW_CB, ROW_LW, ROW_LB, ROW_CW = 0, 1, 2, 4, 5, 6, 7, 8, 9
PACK_ROWS = 40
SMALL = ("norm1_w", "norm2_w", "b_gate", "q_norm_w", "k_norm_w", "conv_b", "conv_ln_w", "conv_ln_b", "conv_w")


def small_sync(g, sq, sides=()):
    ns = len(SMALL)

    def copies(refs):
        pack, recv, send_sems, recv_sems = refs[ns + 2:]
        x, y, c, _ = _place()
        return [pltpu.make_async_remote_copy(
            src_ref=pack, dst_ref=recv.at[4 * x + 2 * y + c], send_sem=send_sems.at[k - 1],
            recv_sem=recv_sems.at[k - 1], device_id=(x ^ (k >> 2), y ^ ((k >> 1) & 1), c ^ (k & 1)),
            device_id_type=MESH) for k in range(1, NDEV)]

    def body(*refs):
        gi = dict(zip(SMALL, refs[:ns]))
        sq_ref, tot, pack, recv, send_sems, recv_sems = refs[ns:]
        x, y, c, _ = _place()
        me = 4 * x + 2 * y + c

        pack[...] = jnp.zeros_like(pack)
        pack[ROW_KN:ROW_KN + 1, LANES:2 * LANES] = jnp.full((1, LANES), (0.5 / D) * jnp.sum(sq_ref[...]), F32)
        pack[ROW_N1:ROW_N1 + 1, :] = gi["norm1_w"][...]
        pack[ROW_N2:ROW_N2 + 1, :] = gi["norm2_w"][...]
        pack[ROW_BG:ROW_BG + 2, :] = gi["b_gate"][...]
        for row, name in ((ROW_QN, "q_norm_w"), (ROW_KN, "k_norm_w")):
            pack[row:row + 1, 0:HD] = gi[name][0:1, 0:HD] + gi[name][0:1, HD:LANES]
        pack[ROW_CB:ROW_CB + 1, 0:CC] = gi["conv_b"][...]
        pack[ROW_LW:ROW_LW + 1, 0:CC] = gi["conv_ln_w"][...]
        pack[ROW_LB:ROW_LB + 1, 0:CC] = gi["conv_ln_b"][...]
        pack[ROW_CW:ROW_CW + KW, 0:CC] = gi["conv_w"][...]

        for cp in copies(refs):
            cp.start()
        recv[me] = pack[...]

    def tail(*refs):
        tot, recv = refs[ns + 1], refs[ns + 3]
        for cp in copies(refs):
            cp.wait()
        acc = recv[0]
        for p in range(1, NDEV):
            acc = acc + recv[p]
        tot[...] = acc

    args = [g[k] for k in SMALL] + [sq]
    res = _call(
        body, sides, name="small_sync", grid=(1,), in_specs=[VMEM] * len(args), out_specs=[VMEM],
        out_shape=[jax.ShapeDtypeStruct((PACK_ROWS, D), F32)],
        scratch_shapes=[pltpu.VMEM((PACK_ROWS, D), F32), pltpu.VMEM((NDEV, PACK_ROWS, D), F32),
                        _sems(NDEV - 1), _sems(NDEV - 1)],
        args=args, own_comm=True, tail=tail)
    return (res[0][0], res[1]) if sides else res[0]


def small_adam(tot, w, m, v, me):
    ns = len(SMALL)

    def body(me_ref, tot, *refs):
        wi = dict(zip(SMALL, refs[:ns]))
        mi = dict(zip(SMALL, refs[ns:2 * ns]))
        vi = dict(zip(SMALL, refs[2 * ns:3 * ns]))
        outs = refs[3 * ns:7 * ns]
        loss_ref = refs[7 * ns]
        me = me_ref[0]

        def shard_grad(name):
            if name == "b_gate":
                return tot[ROW_BG:ROW_BG + 2, pl.ds(pl.multiple_of(me * LANES, LANES), LANES)]
            if name == "conv_w":
                win = tot[ROW_CW:ROW_CW + KW, pl.ds(pl.multiple_of((me // 2) * LANES, LANES), LANES)]
                return jnp.where(me % 2 == 1, win[:, HD:LANES], win[:, 0:HD])
            row = {"norm1_w": ROW_N1, "norm2_w": ROW_N2, "q_norm_w": ROW_QN, "k_norm_w": ROW_KN,
                   "conv_b": ROW_CB, "conv_ln_w": ROW_LW, "conv_ln_b": ROW_LB}[name]
            return tot[row:row + 1, 0:wi[name].shape[1]]

        for i, name in enumerate(SMALL):
            gr = shard_grad(name)
            delta, m2, v2 = _adamw(wi[name][...], gr, mi[name][...], vi[name][...])
            outs[4 * i][...] = gr
            outs[4 * i + 1][...] = delta
            outs[4 * i + 2][...] = m2
            outs[4 * i + 3][...] = v2
        loss_ref[...] = tot[ROW_KN:ROW_KN + 1, LANES:2 * LANES]

    out_shape = []
    for name in SMALL:
        out_shape += [jax.ShapeDtypeStruct(w[name].shape, F32)] * 4
    out_shape.append(jax.ShapeDtypeStruct((1, LANES), F32))
    args = [tot] + [w[k] for k in SMALL] + [m[k] for k in SMALL] + [v[k] for k in SMALL]
    grid_spec = pltpu.PrefetchScalarGridSpec(
        num_scalar_prefetch=1, grid=(1,), in_specs=[VMEM] * len(args), out_specs=[VMEM] * len(out_shape))
    res = pl.pallas_call(body, name="small_adam", grid_spec=grid_spec, out_shape=out_shape)(me, *args)
    out = {name: tuple(res[4 * i:4 * i + 4]) for i, name in enumerate(SMALL)}
    return out, res[4 * ns][0, 0]


MATS = ("w_in", "w_o_attn", "w_pw_conv", "w_out", "w_ffn_in", "w_ffn_out")
TRANSPOSED = ("w_in", "w_ffn_in")
WEIGHTS = ("norm1_w", "w_in", "b_gate", "q_norm_w", "k_norm_w", "w_o_attn", "conv_w", "conv_b", "conv_ln_w",
           "conv_ln_b", "w_pw_conv", "w_out", "norm2_w", "w_ffn_in", "w_ffn_out")


def _blocks_to_cols(blocks):
    n, R, C = blocks.shape
    return blocks.transpose(1, 0, 2).reshape(R, n * C)


def kernel(x, positions, norm1_w, w_in, b_gate, q_norm_w, k_norm_w, w_o_attn, conv_w, conv_b, conv_ln_w, conv_ln_b, w_pw_conv, w_out, norm2_w, w_ffn_in, w_ffn_out, loss_target, m_norm1_w, m_w_in, m_b_gate, m_q_norm_w, m_k_norm_w, m_w_o_attn, m_conv_w, m_conv_b, m_conv_ln_w, m_conv_ln_b, m_w_pw_conv, m_w_out, m_norm2_w, m_w_ffn_in, m_w_ffn_out, v_norm1_w, v_w_in, v_b_gate, v_q_norm_w, v_k_norm_w, v_w_o_attn, v_conv_w, v_conv_b, v_conv_ln_w, v_conv_ln_b, v_w_pw_conv, v_w_out, v_norm2_w, v_w_ffn_in, v_w_ffn_out):
    w = dict(norm1_w=norm1_w, w_in=w_in, b_gate=b_gate, q_norm_w=q_norm_w, k_norm_w=k_norm_w, w_o_attn=w_o_attn,
             conv_w=conv_w, conv_b=conv_b, conv_ln_w=conv_ln_w, conv_ln_b=conv_ln_b, w_pw_conv=w_pw_conv,
             w_out=w_out, norm2_w=norm2_w, w_ffn_in=w_ffn_in, w_ffn_out=w_ffn_out)
    m = dict(norm1_w=m_norm1_w, w_in=m_w_in, b_gate=m_b_gate, q_norm_w=m_q_norm_w, k_norm_w=m_k_norm_w,
             w_o_attn=m_w_o_attn, conv_w=m_conv_w, conv_b=m_conv_b, conv_ln_w=m_conv_ln_w,
             conv_ln_b=m_conv_ln_b, w_pw_conv=m_w_pw_conv, w_out=m_w_out, norm2_w=m_norm2_w,
             w_ffn_in=m_w_ffn_in, w_ffn_out=m_w_ffn_out)
    v = dict(norm1_w=v_norm1_w, w_in=v_w_in, b_gate=v_b_gate, q_norm_w=v_q_norm_w, k_norm_w=v_k_norm_w,
             w_o_attn=v_w_o_attn, conv_w=v_conv_w, conv_b=v_conv_b, conv_ln_w=v_conv_ln_w,
             conv_ln_b=v_conv_ln_b, w_pw_conv=v_w_pw_conv, w_out=v_w_out, norm2_w=v_norm2_w,
             w_ffn_in=v_w_ffn_in, w_ffn_out=v_w_ffn_out)
    def two_d(t):
        t = {k: (a[0] if a.ndim == 3 else a) for k, a in t.items()}
        return {k: (a.T if k in TRANSPOSED else a) for k, a in t.items()}

    w, m, v = two_d(w), two_d(m), two_d(v)

    x2, target = x[0], loss_target[0]
    c_idx = lax.axis_index("c").astype(jnp.int32)
    chip_idx = (2 * lax.axis_index("x") + lax.axis_index("y")).astype(jnp.int32)
    qw2 = jnp.tile(w["q_norm_w"], (1, 2))
    kw2 = jnp.tile(w["k_norm_w"], (1, 2))

    ax, ay = lax.axis_index("x"), lax.axis_index("y")
    chip_order = jnp.stack([2 * ax + ay, 2 * (1 - ax) + ay, 2 * ax + 1 - ay, 2 * (1 - ax) + 1 - ay]).astype(jnp.int32)
    h, proj, w_in_blocks, tabs = in_proj_gather(x2, w["norm1_w"], w["w_in"], chip_order, positions.reshape(S // LANES, LANES))
    w_in_t = w_in_blocks.reshape(INW, D)
    (attn, lse), ((w_ffn_in_blocks,), (w_out_blocks,), (w_o_blocks,), (w_pw_blocks,), (bg_blocks,), (cw_blocks,)) = attn_fwd(
        proj, tabs, qw2, kw2, sides=(ag_blocks_relay(w["w_ffn_in"], BF16), ag_blocks_relay(w["w_out"], BF16),
                                     ag_blocks_relay(w["w_o_attn"], BF16, transpose=True),
                                     ag_blocks_relay(w["w_pw_conv"], BF16, transpose=True),
                                     ag_blocks(w["b_gate"], F32), ag_blocks(w["conv_w"], F32)))
    w_ffn_in_t = w_ffn_in_blocks.reshape(2 * FF, D)
    w_out_f = w_out_blocks.reshape(D, D)
    w_o_t, w_pw_t = w_o_blocks.reshape(D, CC), w_pw_blocks.reshape(D, CC)
    b_gate_f, conv_w_f = _blocks_to_cols(bg_blocks), _blocks_to_cols(cw_blocks)
    cpre, u3 = conv_fwd(proj, conv_w_f, w["conv_b"], w["conv_ln_w"], w["conv_ln_b"])
    x1, z, ya, yb = mix_out(x2, proj, b_gate_f, attn, u3, w_o_t, w_pw_t, w_out_f)
    (h2, gu, f), ((w_ffn_out_blocks,),) = ffn_in(x1, w["norm2_w"], w_ffn_in_t, sides=(ag_blocks_relay(w["w_ffn_out"], BF16),))
    w_ffn_out_f = w_ffn_out_blocks.reshape(FF, D)
    dy, dyb, sq = ffn_out_loss(x1, f, w_ffn_out_f, target)

    g = {}
    def blocks(name, pairs, tm):
        return [t.reshape(NDEV, t.shape[0] // NDEV, t.shape[1]) for t in mm_tn(name, pairs, tm)]

    g_ffn_out, gb_ffn_out = blocks("gw_ffn_out", [(f, dyb)], FF // 2)
    (d_gu, d_x1, d_x1b, g["norm2_w"]), ((ra_ffn_out,),) = ffn_bwd(
        dy, dyb, gu, x1, w["norm2_w"], w_ffn_in_t, w_ffn_out_f, sides=(rs_to_sibling([gb_ffn_out]),))
    g_ffn_in, gb_ffn_in = blocks("gw_ffn_in", [(d_gu, h2)], FF // 2)
    (d_ya, d_yb, d_gl, d_attn, d_u3, g["b_gate"]), ((ra_ffn_in,),) = out_bwd(
        d_x1b, proj, b_gate_f, ya, yb, w_o_t, w_pw_t, w_out_f, sides=(rs_to_sibling([gb_ffn_in]),))
    g_out, gb_out, g_w_o, gb_w_o, g_w_pw, gb_w_pw = blocks(
        "gw_out_o_pw", [(z, d_x1b), (d_ya, attn), (d_yb, u3)], D // 2)
    (d_conv, g["conv_w"], g["conv_b"], g["conv_ln_w"], g["conv_ln_b"]), ((ra_out, ra_w_o, ra_w_pw),) = conv_bwd(
        proj, cpre, d_u3, conv_w_f, w["conv_ln_w"], w["conv_ln_b"],
        sides=(rs_to_sibling([gb_out, gb_w_o, gb_w_pw]),))
    (pb_ffn_out, own_ffn_out), (pb_ffn_in, own_ffn_in), (pb_out, own_out), (pb_w_o, own_w_o), (pb_w_pw, own_w_pw) = chip_sum(
        "chip_sum_early", [g_ffn_out, g_ffn_in, g_out, g_w_o, g_w_pw],
        [ra_ffn_out, ra_ffn_in, ra_out, ra_w_o, ra_w_pw], c_idx, chip_idx)
    (d_q, d_k, d_v, gqw, gkw), ((rb_ffn_out, rb_ffn_in, rb_out, rb_w_o, rb_w_pw),) = attn_bwd(
        proj, tabs, qw2, kw2, d_attn, attn, lse,
        sides=(rs_to_chips([pb_ffn_out, pb_ffn_in, pb_out, pb_w_o, pb_w_pw]),))
    g["q_norm_w"], g["k_norm_w"] = gqw, gkw
    d_segs = (d_q, d_k, d_v, d_conv, d_gl)
    parts, to_sibling, to_chips, owns, from_chips = [], None, None, [], []
    for k, hw in enumerate(GW_IN_SPLIT):
        sides = tuple(s for s in (to_chips, to_sibling) if s is not None)
        (part, part_b), outs = gw_in("gw_in_%d" % k, h, d_segs, sum(GW_IN_SPLIT[:k]), hw, sides=sides)
        outs = list(outs)
        if to_chips is not None:
            from_chips.append(outs.pop(0)[0])
        if to_sibling is not None:
            (pb, own), = chip_sum("chip_sum_w_in_%d" % (k - 1), [parts[-1]], [outs.pop(0)[0]], c_idx, chip_idx)
            owns.append(own)
            to_chips = rs_to_chips_combined(pb)
        else:
            to_chips = None
        parts.append(part.reshape(NDEV, INW // NDEV, hw))
        to_sibling = rs_to_sibling([part_b.reshape(NDEV, INW // NDEV, hw)])
    (grad_x, g["norm1_w"]), ((rb_prev,), (ra_last,)) = in_bwd(
        d_q, d_k, d_v, d_conv, d_gl, w_in_t, x2, d_x1, w["norm1_w"], sides=(to_chips, to_sibling))
    from_chips.append(rb_prev)
    (pb, own), = chip_sum("chip_sum_w_in_%d" % (len(GW_IN_SPLIT) - 1), [parts[-1]], [ra_last], c_idx, chip_idx)
    owns.append(own)
    small_sums, ((rb_last,),) = small_sync(g, sq, sides=(rs_to_chips_combined(pb),))
    small, loss = small_adam(small_sums, w, m, v, (4 * ax + 2 * ay + c_idx).astype(jnp.int32).reshape(1))
    from_chips.append(rb_last)

    adam_o, adam_pw, adam_out = block_adam("adam_w_o_pw_out", [
        (own_w_o, rb_w_o, w["w_o_attn"], m["w_o_attn"], v["w_o_attn"], True),
        (own_w_pw, rb_w_pw, w["w_pw_conv"], m["w_pw_conv"], v["w_pw_conv"], True),
        (own_out, rb_out, w["w_out"], m["w_out"], v["w_out"], False)])
    res = {
        "w_in": shard_adam("adam_w_in", owns, from_chips, w["w_in"], m["w_in"], v["w_in"]),
        "w_ffn_in": shard_adam("adam_w_ffn_in", [own_ffn_in], [rb_ffn_in], w["w_ffn_in"], m["w_ffn_in"], v["w_ffn_in"]),
        "w_o_attn": adam_o, "w_pw_conv": adam_pw, "w_out": adam_out,
        "w_ffn_out": shard_adam("adam_w_ffn_out", [own_ffn_out], [rb_ffn_out],
                                w["w_ffn_out"], m["w_ffn_out"], v["w_ffn_out"]),
    }
    res = {k: tuple(a.T if k in TRANSPOSED else a for a in r) for k, r in res.items()}
    res.update(small)

    def shaped(name, a):
        return a.reshape((1,) + a.shape) if name in MATS or name in ("b_gate", "conv_w") else a

    outs = [loss, grad_x.reshape(1, S, D)]
    for i in range(4):
        outs += [shaped(k, res[k][i]) for k in WEIGHTS]
    return tuple(outs)
```

```python
import functools
from typing import Callable, NamedTuple, Optional

import numpy as np
import jax
import jax.numpy as jnp
from jax import lax
from jax.experimental import pallas as pl
from jax.experimental.pallas import tpu as pltpu

F32 = jnp.float32
BF16 = jnp.bfloat16

S = 2048
D = 1024
HD = 64
QKV = 1536
CC = 512
KW = 31
FF = 2816
INW = 7680
OFF_Q, OFF_K, OFF_V, OFF_CA, OFF_CB, OFF_GA, OFF_GB = 0, 1536, 3072, 4608, 5120, 5632, 6656
DILATIONS = (1, 4, 16)
HALF_SPAN = 64
EPS = 1e-6
NEG_INF = -1e30
ROPE_THETA = 500000.0
ROT_DIM = 16

ADAM_LR = 0.001
ADAM_B1 = 0.9
ADAM_B2 = 0.999
ADAM_EPS = 1e-08
ADAM_WD = 0.01
ADAM_STEP = 10

NDEV = 8
LANES = 128
TM = 256
IN_PROJ_TM = 512
TQ = 128
VMEM_LIMIT = 56 * 1024 * 1024
MESH = pl.DeviceIdType.MESH


def _cp(**kw):
    return pltpu.CompilerParams(vmem_limit_bytes=VMEM_LIMIT, **kw)


def _row(width, col=0, tm=TM):
    return pl.BlockSpec((tm, width), lambda i: (i, col))


PLANE = 512


def _planes(width, tm=TM):
    return pl.BlockSpec((width // PLANE, tm, PLANE), lambda i: (0, i, 0))


def _res(shape):
    nd = len(shape)
    return pl.BlockSpec(shape, lambda *_: (0,) * nd, pipeline_mode=pl.Buffered(1))


def _dot(a, b):
    return jnp.dot(a, b, preferred_element_type=F32)


def _dot_nt(a, b):
    return lax.dot_general(a, b, (((1,), (1,)), ((), ())), preferred_element_type=F32)


def _dot_tn(a, b):
    return lax.dot_general(a, b, (((0,), (0,)), ((), ())), preferred_element_type=F32)


def _sigmoid(x):
    return jax.nn.sigmoid(x)


def _dsilu(x, sg):
    return sg * (1.0 + x * (1.0 - sg))


ANY = pl.BlockSpec(memory_space=pl.ANY)
VMEM = pl.BlockSpec(memory_space=pltpu.VMEM)


class Side(NamedTuple):
    args: tuple
    in_specs: tuple
    out_shape: tuple
    scratch: tuple
    start: Callable
    finish: Callable
    mid: Optional[Callable] = None
    peers: str = ""


BARRIER_IDS = {"s": 0, "dxy": 1, "dsxy": 2, "sxy": 3, "xy": 4}


def _peer_barrier(peers):
    x, y, c = lax.axis_index("x"), lax.axis_index("y"), lax.axis_index("c")
    where = {"s": (x, y, 1 - c), "x": (1 - x, y, c), "y": (x, 1 - y, c), "d": (1 - x, 1 - y, c)}
    barrier = pltpu.get_barrier_semaphore()
    for p in peers:
        pl.semaphore_signal(barrier, inc=1, device_id=where[p], device_id_type=MESH)
    pl.semaphore_wait(barrier, len(peers))


def _call(body, sides=(), *, name, grid, in_specs, out_specs, out_shape, scratch_shapes=(), args, own_comm=False,
          tail=None):
    assert tail is None or int(np.prod(grid)) == 1
    ni, no, ns = len(in_specs), len(out_specs), len(scratch_shapes)
    cnt = [(len(s.args), len(s.out_shape), len(s.scratch)) for s in sides]
    peers = "".join(sorted(set("".join(s.peers for s in sides))))
    if own_comm or not sides or any(not s.peers for s in sides):
        peers = ""

    def take(refs, pos, n):
        return refs[pos:pos + n], pos + n

    def full(*refs):
        m_in, pos = take(refs, 0, ni)
        s_in = []
        for a, _, _ in cnt:
            r, pos = take(refs, pos, a)
            s_in.append(r)
        m_out, pos = take(refs, pos, no)
        s_out = []
        for _, o, _ in cnt:
            r, pos = take(refs, pos, o)
            s_out.append(r)
        m_scr, pos = take(refs, pos, ns)
        s_scr = []
        for _, _, c in cnt:
            r, pos = take(refs, pos, c)
            s_scr.append(r)
        if sides:
            first = functools.reduce(jnp.logical_and, [pl.program_id(d) == 0 for d in range(len(grid))])
            last = functools.reduce(jnp.logical_and, [pl.program_id(d) == g - 1 for d, g in enumerate(grid)])

            @pl.when(first)
            def _():
                if peers:
                    _peer_barrier(peers)
                for s, a, o, c in zip(sides, s_in, s_out, s_scr):
                    s.start(a, o, c)

            steps = int(np.prod(grid))
            mid_step = (2 * steps) // 3
            if steps > 1 and any(s.mid is not None for s in sides):
                step = functools.reduce(lambda acc, d: acc * grid[d] + pl.program_id(d), range(len(grid)), 0)

                @pl.when(step == mid_step)
                def _():
                    for s, a, o, c in zip(sides, s_in, s_out, s_scr):
                        if s.mid is not None:
                            s.mid(a, o, c)

        body(*m_in, *m_out, *m_scr)
        if sides:
            @pl.when(last)
            def _():
                for s, a, o, c in zip(sides, s_in, s_out, s_scr):
                    if s.mid is not None and steps == 1:
                        s.mid(a, o, c)
                if tail is not None:
                    tail(*m_in, *m_out, *m_scr)
                for s, a, o, c in zip(sides, s_in, s_out, s_scr):
                    s.finish(a, o, c)
        elif tail is not None:
            tail(*m_in, *m_out, *m_scr)

    res = pl.pallas_call(
        full, name=name, grid=grid,
        in_specs=list(in_specs) + [sp for s in sides for sp in s.in_specs],
        out_specs=list(out_specs) + [ANY for s in sides for _ in s.out_shape],
        out_shape=list(out_shape) + [o for s in sides for o in s.out_shape],
        scratch_shapes=list(scratch_shapes) + [c for s in sides for c in s.scratch],
        compiler_params=_cp(dimension_semantics=("arbitrary",) * len(grid),
                            **({"collective_id": BARRIER_IDS[peers]} if peers else {})),
    )(*args, *[a for s in sides for a in s.args])
    res = list(res)
    if not sides:
        return res
    outs, pos = take(res, 0, no)
    side_outs = []
    for _, o, _ in cnt:
        r, pos = take(res, pos, o)
        side_outs.append(r)
    return outs, side_outs


def _inv_freq_lanes():
    inv = np.float32(ROPE_THETA) ** (-np.arange(0, ROT_DIM, 2, dtype=np.float32) / np.float32(ROT_DIM))
    lane = np.arange(LANES) % HD
    out = np.where(lane < ROT_DIM, inv[lane % (ROT_DIM // 2)], 0.0).astype(np.float32)
    return jnp.asarray(out.reshape(1, LANES))


def _rope_tables(pos, inv_freq):
    ang = pos.astype(F32) * inv_freq
    lane = lax.broadcasted_iota(jnp.int32, ang.shape, 1) % HD
    cs = jnp.cos(ang)
    sn = jnp.sin(ang)
    return (jnp.where(lane < ROT_DIM, cs, 1.0), jnp.where(lane < ROT_DIM // 2, -sn, 0.0),
            jnp.where(lane < ROT_DIM // 2, 0.0, jnp.where(lane < ROT_DIM, sn, 0.0)))


def _rope(v, c, s1, s2):
    return v * c + pltpu.roll(v, LANES - 8, axis=1) * s1 + pltpu.roll(v, 8, axis=1) * s2


def _rope_t(d, c, s1, s2):
    return d * c - pltpu.roll(d, LANES - 8, axis=1) * s1 - pltpu.roll(d, 8, axis=1) * s2


def _head_mat():
    r = lax.broadcasted_iota(jnp.int32, (LANES, LANES), 0) // HD
    c = lax.broadcasted_iota(jnp.int32, (LANES, LANES), 1) // HD
    return jnp.where(r == c, 1.0 / HD, 0.0).astype(BF16)


def _head_mean(t, e):
    hi = t.astype(BF16)
    rest = (t - hi.astype(F32)).astype(BF16)
    return _dot(hi, e) + _dot(rest, e)


def in_proj_gather(x, norm_w, shard_t, chip_order, pos_col):
    R = INW // NDEV
    tm = IN_PROJ_TM
    half, nt = R // 2, S // tm

    def body(ord_ref, x_ref, nw_ref, sh_ref, pos_ref, f_ref, h_ref, p_ref, wfull_ref, c_ref, s1_ref, s2_ref,
             wt, hs, send, recv, loc):
        kk, i = pl.program_id(0), pl.program_id(1)
        x, y, c, _ = _place()
        me, flip = 4 * x + 2 * y + c, 1 - 2 * c
        here, sib, xn, yn = (x, y, c), (x, y, 1 - c), (1 - x, y, c), (x, 1 - y, c)
        b_xn, b_yn, b_dg = 4 * (1 - x) + 2 * y + c, 4 * x + 2 * (1 - y) + c, 4 * (1 - x) + 2 * (1 - y) + c

        def cp(k, block, to, rows=None):
            dst = wt.at[block] if rows is None else wt.at[block, pl.ds(rows * half, half), :]
            return _remote(dst, dst, send, recv, k, to)

        def sends():
            return [cp(0, me, sib), cp(1, me, xn), cp(2, me, yn), cp(3, b_xn, sib), cp(4, b_yn, sib),
                    cp(5, b_xn, yn, rows=0), cp(6, b_yn, xn, rows=1), cp(7, b_dg, sib, rows=0), cp(8, b_dg, sib, rows=1)]

        def keep(j, blk0):
            pair = pl.ds(pl.multiple_of(blk0, 2), 2)
            return pltpu.make_async_copy(wt.at[pair], wfull_ref.at[pair], loc.at[j])

        @pl.when((kk == 0) & (i == 0))
        def _():
            _peer_barrier("sxy")
            _cast_rows(wt.at[me], sh_ref)
            for s_ in sends()[0:3]:
                s_.start()

            def tables(j, _):
                posf = pos_ref[pl.ds(j, 1), :].astype(F32)
                eye = (lax.broadcasted_iota(jnp.int32, (LANES, LANES), 0)
                       == lax.broadcasted_iota(jnp.int32, (LANES, LANES), 1))
                col = jnp.sum(jnp.where(eye, posf, 0.0), axis=1, keepdims=True)
                chunk = pl.ds(pl.multiple_of(j * LANES, LANES), LANES)
                c_ref[chunk, :], s1_ref[chunk, :], s2_ref[chunk, :] = _rope_tables(col, f_ref[...])
                return 0

            lax.fori_loop(0, S // LANES, tables, 0)
            cp(0, me + flip, here).wait_recv()
            keep(0, me - c).start()

        @pl.when((kk == 1) & (i == 0))
        def _():
            cp(1, b_xn, here).wait_recv()
            sends()[5].start()
            sends()[3].start()
            cp(2, b_yn, here).wait_recv()
            sends()[6].start()
            sends()[4].start()
            cp(3, b_xn + flip, here).wait_recv()
            keep(1, b_xn - c).start()

        @pl.when((kk == 2) & (i == 0))
        def _():
            cp(4, b_yn + flip, here).wait_recv()
            keep(2, b_yn - c).start()

        @pl.when((kk == 3) & (i == 0))
        def _():
            cp(5, b_dg, here, rows=0).wait_recv()
            sends()[7].start()
            cp(6, b_dg, here, rows=1).wait_recv()
            sends()[8].start()
            cp(7, b_dg + flip, here, rows=0).wait_recv()
            cp(8, b_dg + flip, here, rows=1).wait_recv()
            keep(3, b_dg - c).start()

        rows = pl.ds(pl.multiple_of(i * tm, tm), tm)

        @pl.when(kk == 0)
        def _():
            xv = x_ref[...]
            r = lax.rsqrt(jnp.mean(xv * xv, axis=-1, keepdims=True) + EPS)
            hb = (xv * r * nw_ref[...]).astype(BF16)
            h_ref[...] = hb
            hs[rows, :] = hb

        h = hs[rows, :]
        chip = ord_ref[kk]
        for cc in range(2):
            p_ref[:, cc * R:(cc + 1) * R] = _dot_nt(h, wt[2 * chip + cc])

        @pl.when((kk == 3) & (i == nt - 1))
        def _():
            for s_ in sends():
                s_.wait_send()
            for j, blk in enumerate((me, b_xn, b_yn, b_dg)):
                keep(j, blk - c).wait()

    def first_pass(kk, i):
        return jnp.where(kk == 0, i, nt - 1)

    grid_spec = pltpu.PrefetchScalarGridSpec(
        num_scalar_prefetch=1, grid=(4, nt),
        in_specs=[pl.BlockSpec((tm, D), lambda kk, i, o: (first_pass(kk, i), 0)),
                  pl.BlockSpec((1, D), lambda kk, i, o: (0, 0)), VMEM, VMEM,
                  pl.BlockSpec((1, LANES), lambda kk, i, o: (0, 0))],
        out_specs=[pl.BlockSpec((tm, D), lambda kk, i, o: (first_pass(kk, i), 0)),
                   pl.BlockSpec((tm, 2 * R), lambda kk, i, o: (i, o[kk])), ANY]
        + [pl.BlockSpec((S, LANES), lambda kk, i, o: (0, 0))] * 3,
        scratch_shapes=[pltpu.VMEM((NDEV, R, D), BF16), pltpu.VMEM((S, D), BF16), _sems(9), _sems(9), _sems(4)])
    res = pl.pallas_call(
        body, name="in_proj_gather", grid_spec=grid_spec,
        out_shape=[jax.ShapeDtypeStruct((S, D), BF16), jax.ShapeDtypeStruct((S, INW), F32),
                   jax.ShapeDtypeStruct((NDEV, R, D), BF16)] + [jax.ShapeDtypeStruct((S, LANES), F32)] * 3,
        compiler_params=_cp(dimension_semantics=("arbitrary", "arbitrary"), collective_id=BARRIER_IDS["sxy"]),
    )(chip_order, x, norm_w, shard_t, pos_col, _inv_freq_lanes())
    return res[0], res[1], res[2], tuple(res[3:])


def _qk_specs():
    nb = QKV // LANES
    return [pl.BlockSpec((S, LANES), functools.partial(lambda hp, g, o: (0, o + g * 4 + hp), o=o))
            for o in (OFF_Q // LANES, OFF_K // LANES, OFF_V // LANES)]


def _tab_specs():
    return [pl.BlockSpec((S, LANES), lambda hp, g: (0, 0), pipeline_mode=pl.Buffered(1))] * 3


def _vec_spec():
    return pl.BlockSpec((1, LANES), lambda hp, g: (0, 0))


def _sub_rows(r, d, start, n):
    if d == 1:
        return pl.ds(start, n)
    return pl.ds(r + d * start, n, stride=d)


def _band_window(i, L):
    W = min(TQ + 2 * HALF_SPAN, L)
    q0 = pl.multiple_of(i * TQ, TQ)
    k0 = pl.multiple_of(jnp.clip(q0 - HALF_SPAN, 0, L - W), HALF_SPAN)
    qpos = q0 + (lax.broadcasted_iota(jnp.int32, (2 * TQ, W), 0) & (TQ - 1))
    kpos = k0 + lax.broadcasted_iota(jnp.int32, (2 * TQ, W), 1)
    valid = jnp.abs(qpos - kpos) <= HALF_SPAN
    return W, q0, k0, valid


def _stack_heads(t, lo):
    z = jnp.zeros_like(t)
    return jnp.concatenate([jnp.where(lo, t, z), jnp.where(lo, z, t)], axis=0)


def _unstack_heads(t2, lo):
    return jnp.where(lo, t2[0:TQ], t2[TQ:2 * TQ])


CHAINS = 8


def _interleave(d):
    ru = min(d, CHAINS)
    return ru, min(CHAINS // ru, S // d // TQ)


def _for_blocks(n, fn):
    if n == 1:
        fn(0)
    else:
        def it(j, _):
            fn(j)
            return 0
        lax.fori_loop(0, n, it, 0)


def attn_fwd(proj, tabs, qw2, kw2, sides=()):
    CH = 256

    def body(q_ref, k_ref, v_ref, c_ref, s1_ref, s2_ref, qw_ref, kw_ref, at_ref, ls_ref,
             qs, ks, vs, osub, lsub, onat, lnat, qn, kn):
        g = pl.program_id(1)
        lo = lax.broadcasted_iota(jnp.int32, (1, LANES), 1) < HD
        e = _head_mat()

        def prep(i, _):
            rows = pl.ds(pl.multiple_of(i * CH, CH), CH)
            c, s1, s2 = c_ref[rows, :], s1_ref[rows, :], s2_ref[rows, :]
            for t_ref, w_ref, out, scale in ((q_ref, qw_ref, qn, HD ** -0.5), (k_ref, kw_ref, kn, 1.0)):
                t = t_ref[rows, :]
                r = lax.rsqrt(_head_mean(t * t, e) + EPS)
                out[rows, :] = _rope(t * r * w_ref[...], c, s1, s2) * scale
            return 0

        lax.fori_loop(0, S // CH, prep, 0, unroll=4)

        def group(gi, d):
            L = S // d

            ru, nb = _interleave(d)

            def stage(r, off):
                for c0 in range(0, L, CH):
                    n = min(CH, L)
                    rows = _sub_rows(r, d, c0, n)
                    dst = pl.ds(off + c0, n)
                    qs[dst, :] = qn[rows, :].astype(BF16)
                    ks[dst, :] = kn[rows, :].astype(BF16)
                    vs[dst, :] = v_ref[rows, :].astype(BF16)

            def one(off, i):
                W, q0, k0, valid = _band_window(i, L)
                q2 = _stack_heads(qs[pl.ds(off + q0, TQ), :], lo)
                sc = jnp.where(valid, _dot_nt(q2, ks[pl.ds(off + k0, W), :]), NEG_INF)
                m = jnp.max(sc, axis=-1, keepdims=True)
                p = jnp.exp(sc - m)
                den = jnp.sum(p, axis=-1, keepdims=True)
                o2 = _dot(p.astype(BF16), vs[pl.ds(off + k0, W), :]) / den
                l2 = jnp.broadcast_to(m + jnp.log(den), (2 * TQ, LANES))
                osub[pl.ds(off + q0, TQ), :] = _unstack_heads(o2, lo)
                lsub[pl.ds(off + q0, TQ), :] = _unstack_heads(l2, lo)

            def unstage(r, off):
                for c0 in range(0, L, CH):
                    n = min(CH, L)
                    rows = _sub_rows(r, d, c0, n)
                    onat[gi, rows, :] = osub[pl.ds(off + c0, n), :]
                    lnat[gi, rows, :] = lsub[pl.ds(off + c0, n), :]

            def step(t, _):
                for u in range(ru):
                    stage(t * ru + u, u * L)
                _for_blocks(L // TQ // nb, lambda j: [one(u * L, j * nb + b) for u in range(ru) for b in range(nb)])
                for u in range(ru):
                    unstage(t * ru + u, u * L)
                return 0

            lax.fori_loop(0, d // ru, step, 0)

        for gi, d in enumerate(DILATIONS):
            pl.when(g == gi)(functools.partial(group, gi, d))

        @pl.when(g == len(DILATIONS) - 1)
        def _():
            def mix(i, _):
                rows = pl.ds(pl.multiple_of(i * CH, CH), CH)
                l0, l1, l2 = lnat[0, rows, :], lnat[1, rows, :], lnat[2, rows, :]
                m = jnp.maximum(jnp.maximum(l0, l1), l2)
                e0, e1, e2 = jnp.exp(l0 - m), jnp.exp(l1 - m), jnp.exp(l2 - m)
                den = e0 + e1 + e2
                a = (e0 * onat[0, rows, :] + e1 * onat[1, rows, :] + e2 * onat[2, rows, :]) / den
                at_ref[rows, :] = a.astype(BF16)
                ls_ref[rows, :] = m + jnp.log(den)
                return 0

            lax.fori_loop(0, S // CH, mix, 0)

    out_spec = pl.BlockSpec((S, LANES), lambda hp, g: (0, hp))
    return _call(
        body, sides, name="attn_fwd", grid=(4, 3),
        in_specs=_qk_specs() + _tab_specs() + [_vec_spec(), _vec_spec()],
        out_specs=[out_spec, out_spec],
        out_shape=[jax.ShapeDtypeStruct((S, CC), BF16), jax.ShapeDtypeStruct((S, CC), F32)],
        scratch_shapes=[pltpu.VMEM((S, LANES), BF16)] * 3 + [pltpu.VMEM((S, LANES), F32)] * 2
        + [pltpu.VMEM((3, S, LANES), F32)] * 2 + [pltpu.VMEM((S, LANES), F32)] * 2,
        args=(proj, proj, proj, *tabs, qw2, kw2))


def attn_bwd(proj, tabs, qw2, kw2, d_attn, attn, lse, sides=()):
    CH = 256

    def body(q_ref, k_ref, v_ref, c_ref, s1_ref, s2_ref, qw_ref, kw_ref, do_ref, at_ref, ls_ref,
             dq_ref, dk_ref, dv_ref, gqw_ref, gkw_ref,
             qs, ks, vs, dos, dsub, lsub, dqs, dks, dvs, dnat, qx, kx, dvn, tnq, tnk, rrq, rrk):
        hp, g = pl.program_id(0), pl.program_id(1)
        lo = lax.broadcasted_iota(jnp.int32, (1, LANES), 1) < HD
        e = _head_mat()
        both = ((q_ref, qw_ref, qx, tnq, rrq, HD ** -0.5), (k_ref, kw_ref, kx, tnk, rrk, 1.0))

        @pl.when((hp == 0) & (g == 0))
        def _():
            gqw_ref[...] = jnp.zeros_like(gqw_ref)
            gkw_ref[...] = jnp.zeros_like(gkw_ref)

        def prep(i, _):
            rows = pl.ds(pl.multiple_of(i * CH, CH), CH)
            dnat[rows, :] = _head_mean(do_ref[rows, :] * at_ref[rows, :].astype(F32), e) * float(HD)
            c, s1, s2 = c_ref[rows, :], s1_ref[rows, :], s2_ref[rows, :]
            for t_ref, w_ref, x, tn_s, rr_s, scale in both:
                t = t_ref[rows, :]
                rr = lax.rsqrt(_head_mean(t * t, e) + EPS)
                tn = t * rr
                rr_s[rows, :] = rr
                tn_s[rows, :] = tn
                x[rows, :] = _rope(tn * w_ref[...], c, s1, s2) * scale
            return 0

        lax.fori_loop(0, S // CH, prep, 0, unroll=4)

        def group(d):
            L = S // d

            ru, nb = _interleave(d)

            def stage(r, off):
                for c0 in range(0, L, CH):
                    n = min(CH, L)
                    rows = _sub_rows(r, d, c0, n)
                    dst = pl.ds(off + c0, n)
                    qs[dst, :] = qx[rows, :].astype(BF16)
                    ks[dst, :] = kx[rows, :].astype(BF16)
                    vs[dst, :] = v_ref[rows, :].astype(BF16)
                    dos[dst, :] = do_ref[rows, :].astype(BF16)
                    dsub[dst, :] = dnat[rows, :]
                    lsub[dst, :] = ls_ref[rows, :]
                    dks[dst, :] = jnp.zeros((n, LANES), F32)
                    dvs[dst, :] = jnp.zeros((n, LANES), F32)

            def one(off, i):
                W, q0, k0, valid = _band_window(i, L)
                qrows, krows = pl.ds(off + q0, TQ), pl.ds(off + k0, W)
                q2 = _stack_heads(qs[qrows, :], lo)
                do2 = _stack_heads(dos[qrows, :], lo)
                kk, vv = ks[krows, :], vs[krows, :]
                lse_b, dd_b = lsub[qrows, :], dsub[qrows, :]
                lse2 = jnp.concatenate([lse_b[:, 0:1], lse_b[:, HD:HD + 1]], axis=0)
                dd2 = jnp.concatenate([dd_b[:, 0:1], dd_b[:, HD:HD + 1]], axis=0)
                sc = jnp.where(valid, _dot_nt(q2, kk), NEG_INF)
                p = jnp.exp(sc - lse2)
                ds = (p * (_dot_nt(do2, vv) - dd2)).astype(BF16)
                dqs[qrows, :] = _unstack_heads(_dot(ds, kk), lo)
                dks[krows, :] = dks[krows, :] + _dot_tn(ds, q2)
                dvs[krows, :] = dvs[krows, :] + _dot_tn(p.astype(BF16), do2)

            def unstage(r, off):
                for c0 in range(0, L, CH):
                    n = min(CH, L)
                    rows = _sub_rows(r, d, c0, n)
                    src = pl.ds(off + c0, n)
                    qx[rows, :] = dqs[src, :]
                    kx[rows, :] = dks[src, :]
                    dvn[rows, :] = dvs[src, :]

            def step(t, _):
                for u in range(ru):
                    stage(t * ru + u, u * L)
                _for_blocks(L // TQ // nb, lambda j: [one(u * L, j * nb + b) for u in range(ru) for b in range(nb)])
                for u in range(ru):
                    unstage(t * ru + u, u * L)
                return 0

            lax.fori_loop(0, d // ru, step, 0)

        for gi, d in enumerate(DILATIONS):
            pl.when(g == gi)(functools.partial(group, d))

        def emit(i, _):
            rows = pl.ds(pl.multiple_of(i * CH, CH), CH)
            c, s1, s2 = c_ref[rows, :], s1_ref[rows, :], s2_ref[rows, :]
            for (_, w_ref, x, tn_s, rr_s, scale), out, gw_ref in zip(both, (dq_ref, dk_ref), (gqw_ref, gkw_ref)):
                tn = tn_s[rows, :]
                dy = _rope_t(x[rows, :] * scale, c, s1, s2)
                gw_ref[0:1, :] = gw_ref[0:1, :] + jnp.sum(dy * tn, axis=0, keepdims=True)
                dtn = dy * w_ref[...]
                out[rows, :] = (rr_s[rows, :] * (dtn - tn * _head_mean(dtn * tn, e))).astype(BF16)
            dv_ref[rows, :] = dvn[rows, :].astype(BF16)
            return 0

        lax.fori_loop(0, S // CH, emit, 0, unroll=4)

    nat_spec = pl.BlockSpec((S, LANES), lambda hp, g: (0, hp))
    out_spec = pl.BlockSpec((None, S, LANES), lambda hp, g: (g, 0, hp))
    acc_spec = pl.BlockSpec((8, LANES), lambda hp, g: (0, 0))
    return _call(
        body, sides, name="attn_bwd", grid=(4, 3),
        in_specs=_qk_specs() + _tab_specs() + [_vec_spec(), _vec_spec(), nat_spec, nat_spec, nat_spec],
        out_specs=[out_spec] * 3 + [acc_spec] * 2,
        out_shape=[jax.ShapeDtypeStruct((QKV // PLANE, S, PLANE), BF16)] * 3 + [jax.ShapeDtypeStruct((8, LANES), F32)] * 2,
        scratch_shapes=[pltpu.VMEM((S, LANES), BF16)] * 4 + [pltpu.VMEM((S, LANES), F32)] * 13,
        args=(proj, proj, proj, *tabs, qw2, kw2, d_attn, attn, lse))


PADR = 16
CT = 128


def _conv_specs():
    return [pl.BlockSpec((S, CC), lambda i: (0, OFF_CA // CC)), pl.BlockSpec((S, CC), lambda i: (0, OFF_CB // CC))]


NCB = CC // LANES


def _pad_zero(pad):
    for cb in range(NCB):
        pad[cb, 0:PADR, :] = jnp.zeros((PADR, LANES), F32)
        pad[cb, PADR + S:PADR + S + PADR, :] = jnp.zeros((PADR, LANES), F32)


def _pad_store(pad, row0, n, val):
    for cb in range(NCB):
        pad[cb, pl.ds(pl.multiple_of(row0 + PADR, 8), n), :] = val[:, cb * LANES:(cb + 1) * LANES]


def _taps(pad_ref, cb, s0, weights):
    acc = jnp.zeros((CT, LANES), F32)
    for k in range(KW):
        acc = acc + weights[k] * pad_ref[cb, pl.ds(s0 + k + 1, CT), :]
    return acc


def conv_fwd(proj, conv_w, conv_b, ln_w, ln_b):
    def body(a_ref, b_ref, w_ref, cb_ref, lw_ref, lb_ref, c_ref, u3_ref, upad):
        _pad_zero(upad)

        def glu(i, _):
            rows = pl.ds(pl.multiple_of(i * TM, TM), TM)
            _pad_store(upad, i * TM, TM, a_ref[rows, :] * _sigmoid(b_ref[rows, :]))
            return 0

        lax.fori_loop(0, S // TM, glu, 0)

        def chunk(i, _):
            s0 = pl.multiple_of(i * CT, CT)
            for cb in range(CC // LANES):
                cols = slice(cb * LANES, (cb + 1) * LANES)
                w = [w_ref[k:k + 1, cols] for k in range(KW)]
                c_ref[pl.ds(s0, CT), cols] = _taps(upad, cb, s0, w) + cb_ref[:, cols]
            cv = c_ref[pl.ds(s0, CT), :]
            mu = jnp.mean(cv, axis=-1, keepdims=True)
            xc = cv - mu
            rstd = lax.rsqrt(jnp.mean(xc * xc, axis=-1, keepdims=True) + EPS)
            yl = xc * rstd * lw_ref[...] + lb_ref[...]
            u3_ref[pl.ds(s0, CT), :] = (yl * _sigmoid(yl)).astype(BF16)
            return 0

        lax.fori_loop(0, S // CT, chunk, 0)

    vec = pl.BlockSpec((1, CC), lambda i: (0, 0))
    full = pl.BlockSpec((S, CC), lambda i: (0, 0))
    return _call(
        body, name="conv_fwd", grid=(1,),
        in_specs=_conv_specs() + [pl.BlockSpec((KW, CC), lambda i: (0, 0)), vec, vec, vec],
        out_specs=[full, full],
        out_shape=[jax.ShapeDtypeStruct((S, CC), F32), jax.ShapeDtypeStruct((S, CC), BF16)],
        scratch_shapes=[pltpu.VMEM((NCB, S + 2 * PADR, LANES), F32)],
        args=(proj, proj, conv_w, conv_b, ln_w, ln_b))


def conv_bwd(proj, cpre, d_u3, conv_w, ln_w, ln_b, sides=()):
    def body(a_ref, b_ref, c_ref, du3_ref, w_ref, lw_ref, lb_ref,
             dc_ref, gw_ref, gcb_ref, glw_ref, glb_ref, upad, dpad):
        _pad_zero(upad)
        _pad_zero(dpad)
        gw_ref[...] = jnp.zeros_like(gw_ref)

        def ln_bwd(i, carry):
            gcb, glw, glb = carry
            rows = pl.ds(pl.multiple_of(i * TM, TM), TM)
            _pad_store(upad, i * TM, TM, a_ref[rows, :] * _sigmoid(b_ref[rows, :]))
            cv = c_ref[rows, :]
            mu = jnp.mean(cv, axis=-1, keepdims=True)
            xc = cv - mu
            rstd = lax.rsqrt(jnp.mean(xc * xc, axis=-1, keepdims=True) + EPS)
            xh = xc * rstd
            yl = xh * lw_ref[...] + lb_ref[...]
            dyl = du3_ref[rows, :] * _dsilu(yl, _sigmoid(yl))
            dxh = dyl * lw_ref[...]
            dcv = rstd * (dxh - jnp.mean(dxh, axis=-1, keepdims=True)
                          - xh * jnp.mean(dxh * xh, axis=-1, keepdims=True))
            _pad_store(dpad, i * TM, TM, dcv)
            return (gcb + jnp.sum(dcv, axis=0, keepdims=True),
                    glw + jnp.sum(dyl * xh, axis=0, keepdims=True),
                    glb + jnp.sum(dyl, axis=0, keepdims=True))

        z = jnp.zeros((1, CC), F32)
        gcb, glw, glb = lax.fori_loop(0, S // TM, ln_bwd, (z, z, z))
        gcb_ref[...] = gcb
        glw_ref[...] = glw
        glb_ref[...] = glb

        def chunk(i, _):
            s0 = pl.multiple_of(i * CT, CT)
            for cb in range(CC // LANES):
                cols = slice(cb * LANES, (cb + 1) * LANES)
                wr = [w_ref[KW - 1 - k:KW - k, cols] for k in range(KW)]
                du = _taps(dpad, cb, s0, wr)
                dcv = dpad[cb, pl.ds(s0 + PADR, CT), :]
                for k in range(KW):
                    gw_ref[k:k + 1, cols] = gw_ref[k:k + 1, cols] + jnp.sum(
                        upad[cb, pl.ds(s0 + k + 1, CT), :] * dcv, axis=0, keepdims=True)
                av = a_ref[pl.ds(s0, CT), cols]
                sb = _sigmoid(b_ref[pl.ds(s0, CT), cols])
                dc_ref[0, pl.ds(s0, CT), cols] = (du * sb).astype(BF16)
                dc_ref[1, pl.ds(s0, CT), cols] = (du * av * sb * (1.0 - sb)).astype(BF16)
            return 0

        lax.fori_loop(0, S // CT, chunk, 0)

    vec = pl.BlockSpec((1, CC), lambda i: (0, 0))
    full = pl.BlockSpec((S, CC), lambda i: (0, 0))
    wsp = pl.BlockSpec((KW, CC), lambda i: (0, 0))
    return _call(
        body, sides, name="conv_bwd", grid=(1,),
        in_specs=_conv_specs() + [full, full, wsp, vec, vec],
        out_specs=[pl.BlockSpec((2, S, CC), lambda i: (0, 0, 0)), wsp, vec, vec, vec],
        out_shape=[jax.ShapeDtypeStruct((2, S, CC), BF16), jax.ShapeDtypeStruct((KW, CC), F32)]
        + [jax.ShapeDtypeStruct((1, CC), F32)] * 3,
        scratch_shapes=[pltpu.VMEM((NCB, S + 2 * PADR, LANES), F32)] * 2,
        args=(proj, proj, cpre, d_u3, conv_w, ln_w, ln_b))


def _gate_specs():
    return [_row(CC, col=OFF_GA // CC + j) for j in range(4)]


def _gates(g_refs, bg_ref):
    ga = _sigmoid(jnp.concatenate([g_refs[0][...], g_refs[1][...]], axis=1) + bg_ref[0:1, :])
    gb = _sigmoid(jnp.concatenate([g_refs[2][...], g_refs[3][...]], axis=1) + bg_ref[1:2, :])
    return ga, gb


def mix_out(x, proj, b_gate, attn, u3, w_o, w_pw, w_out):
    def body(x_ref, g0, g1, g2, g3, bg_ref, at_ref, u3_ref, wo_ref, wp_ref, wout_ref,
             x1_ref, z_ref, ya_ref, yb_ref):
        ga, gb = _gates((g0, g1, g2, g3), bg_ref)
        ya = _dot_nt(at_ref[...], wo_ref[...])
        yb = _dot_nt(u3_ref[...], wp_ref[...])
        z = (ga * ya + gb * yb).astype(BF16)
        ya_ref[...] = ya.astype(BF16)
        yb_ref[...] = yb.astype(BF16)
        z_ref[...] = z
        x1_ref[...] = x_ref[...] + _dot(z, wout_ref[...])

    return pl.pallas_call(
        body, name="mix_out", grid=(S // TM,),
        in_specs=[_row(D)] + _gate_specs() + [_res((2, D)), _row(CC), _row(CC),
                                              _res((D, CC)), _res((D, CC)), _res((D, D))],
        out_specs=[_row(D)] * 4,
        out_shape=[jax.ShapeDtypeStruct((S, D), F32)] + [jax.ShapeDtypeStruct((S, D), BF16)] * 3,
        compiler_params=_cp(dimension_semantics=("arbitrary",)),
    )(x, proj, proj, proj, proj, b_gate, attn, u3, w_o, w_pw, w_out)


def out_bwd(d_x1b, proj, b_gate, ya, yb, w_o, w_pw, w_out, sides=()):
    def body(dx_ref, g0, g1, g2, g3, bg_ref, ya_ref, yb_ref, wo_ref, wp_ref, wout_ref,
             dya_ref, dyb_ref, dgl_ref, dat_ref, du3_ref, gbg_ref):
        @pl.when(pl.program_id(0) == 0)
        def _():
            gbg_ref[...] = jnp.zeros_like(gbg_ref)

        ga, gb = _gates((g0, g1, g2, g3), bg_ref)
        dz = _dot_nt(dx_ref[...], wout_ref[...])
        dya = (dz * ga).astype(BF16)
        dyb = (dz * gb).astype(BF16)
        dgla = dz * ya_ref[...].astype(F32) * ga * (1.0 - ga)
        dglb = dz * yb_ref[...].astype(F32) * gb * (1.0 - gb)
        dya_ref[...] = dya
        dyb_ref[...] = dyb
        for j in range(2):
            dgl_ref[j] = dgla[:, j * PLANE:(j + 1) * PLANE].astype(BF16)
            dgl_ref[2 + j] = dglb[:, j * PLANE:(j + 1) * PLANE].astype(BF16)
        gbg_ref[0:1, :] = gbg_ref[0:1, :] + jnp.sum(dgla, axis=0, keepdims=True)
        gbg_ref[1:2, :] = gbg_ref[1:2, :] + jnp.sum(dglb, axis=0, keepdims=True)
        dat_ref[...] = _dot(dya, wo_ref[...])
        du3_ref[...] = _dot(dyb, wp_ref[...])

    return _call(
        body, sides, name="out_bwd", grid=(S // TM,),
        in_specs=[_row(D)] + _gate_specs() + [_res((2, D)), _row(D), _row(D),
                                              _res((D, CC)), _res((D, CC)), _res((D, D))],
        out_specs=[_row(D), _row(D), _planes(2 * D), _row(CC), _row(CC), pl.BlockSpec((2, D), lambda i: (0, 0))],
        out_shape=[jax.ShapeDtypeStruct((S, D), BF16)] * 2 + [jax.ShapeDtypeStruct((2 * D // PLANE, S, PLANE), BF16)]
        + [jax.ShapeDtypeStruct((S, CC), F32)] * 2 + [jax.ShapeDtypeStruct((2, D), F32)],
        args=(d_x1b, proj, proj, proj, proj, b_gate, ya, yb, w_o, w_pw, w_out))


def ffn_in(x1, norm_w, w_ffn_in, sides=()):
    half = FF // 2

    def body(x_ref, nw_ref, w_ref, h_ref, gu_ref, f_ref):
        xv = x_ref[...]
        r = lax.rsqrt(jnp.mean(xv * xv, axis=-1, keepdims=True) + EPS)
        h = (xv * r * nw_ref[...]).astype(BF16)
        h_ref[...] = h
        for j in range(2):
            gt = _dot_nt(h, w_ref[j * half:(j + 1) * half, :])
            up = _dot_nt(h, w_ref[FF + j * half:FF + (j + 1) * half, :])
            gu_ref[:, j * half:(j + 1) * half] = gt.astype(BF16)
            gu_ref[:, FF + j * half:FF + (j + 1) * half] = up.astype(BF16)
            f_ref[:, j * half:(j + 1) * half] = (gt * _sigmoid(gt) * up).astype(BF16)

    return _call(
        body, sides, name="ffn_in", grid=(S // TM,),
        in_specs=[_row(D), _res((1, D)), _res((2 * FF, D))],
        out_specs=[_row(D), _row(2 * FF), _row(FF)],
        out_shape=[jax.ShapeDtypeStruct((S, D), BF16), jax.ShapeDtypeStruct((S, 2 * FF), BF16),
                   jax.ShapeDtypeStruct((S, FF), BF16)],
        args=(x1, norm_w, w_ffn_in))


def ffn_out_loss(x1, f, w_ffn_out, target):
    def body(x_ref, f_ref, w_ref, t_ref, dy_ref, dyb_ref, sq_ref):
        @pl.when(pl.program_id(0) == 0)
        def _():
            sq_ref[...] = jnp.zeros_like(sq_ref)

        diff = x_ref[...] + _dot(f_ref[...], w_ref[...]) - t_ref[...]
        dy = diff * (1.0 / D)
        dy_ref[...] = dy
        dyb_ref[...] = dy.astype(BF16)
        sq_ref[...] = sq_ref[...] + jnp.sum((diff * diff).reshape(TM // 8, 8, D), axis=0)

    return pl.pallas_call(
        body, name="ffn_out_loss", grid=(S // TM,),
        in_specs=[_row(D), _row(FF), _res((FF, D)), _row(D)],
        out_specs=[_row(D), _row(D), pl.BlockSpec((8, D), lambda i: (0, 0))],
        out_shape=[jax.ShapeDtypeStruct((S, D), F32), jax.ShapeDtypeStruct((S, D), BF16),
                   jax.ShapeDtypeStruct((8, D), F32)],
        compiler_params=_cp(dimension_semantics=("arbitrary",)),
    )(x1, f, w_ffn_out, target)


def _rms_bwd(xv, nw, dh):
    r = lax.rsqrt(jnp.mean(xv * xv, axis=-1, keepdims=True) + EPS)
    xn = xv * r
    dxn = dh * nw
    dx = r * (dxn - xn * jnp.mean(dxn * xn, axis=-1, keepdims=True))
    return dx, dh * xn


def ffn_bwd(dy, dyb, gu, x1, norm_w, w_ffn_in, w_ffn_out, sides=()):
    def body(dy_ref, dyb_ref, gu_ref, x_ref, nw_ref, wi_ref, wo_ref, dgu_ref, dx_ref, dxb_ref, gn_ref):
        @pl.when(pl.program_id(0) == 0)
        def _():
            gn_ref[...] = jnp.zeros_like(gn_ref)

        df = _dot_nt(dyb_ref[...], wo_ref[...])
        gt = gu_ref[:, 0:FF].astype(F32)
        up = gu_ref[:, FF:2 * FF].astype(F32)
        sg = _sigmoid(gt)
        dgt = (df * up * _dsilu(gt, sg)).astype(BF16)
        dup = (df * gt * sg).astype(BF16)
        dgu_ref[:, 0:FF] = dgt
        dgu_ref[:, FF:2 * FF] = dup
        dh = _dot(dgt, wi_ref[0:FF, :]) + _dot(dup, wi_ref[FF:2 * FF, :])
        dxn, gw = _rms_bwd(x_ref[...], nw_ref[...], dh)
        dx = dy_ref[...] + dxn
        dx_ref[...] = dx
        dxb_ref[...] = dx.astype(BF16)
        gn_ref[...] = gn_ref[...] + jnp.sum(gw, axis=0, keepdims=True)

    return _call(
        body, sides, name="ffn_bwd", grid=(S // TM,),
        in_specs=[_row(D), _row(D), _row(2 * FF), _row(D), _res((1, D)), _res((2 * FF, D)), _res((FF, D))],
        out_specs=[_row(2 * FF), _row(D), _row(D), pl.BlockSpec((1, D), lambda i: (0, 0))],
        out_shape=[jax.ShapeDtypeStruct((S, 2 * FF), BF16), jax.ShapeDtypeStruct((S, D), F32),
                   jax.ShapeDtypeStruct((S, D), BF16), jax.ShapeDtypeStruct((1, D), F32)],
        args=(dy, dyb, gu, x1, norm_w, w_ffn_in, w_ffn_out))


def in_bwd(d_q, d_k, d_v, d_conv, d_gl, w_in, x, d_x1, norm_w, sides=()):
    segs = ((OFF_Q, QKV), (OFF_K, QKV), (OFF_V, QKV), (OFF_CA, 2 * CC), (OFF_GA, 2 * D))

    def body(dq_ref, dk_ref, dv_ref, dc_ref, dg_ref, w_ref, x_ref, dx1_ref, nw_ref, gx_ref, gn_ref):
        @pl.when(pl.program_id(0) == 0)
        def _():
            gn_ref[...] = jnp.zeros_like(gn_ref)

        dh = jnp.zeros((TM, D), F32)
        for ref, (off, width) in zip((dq_ref, dk_ref, dv_ref, dc_ref, dg_ref), segs):
            for j in range(width // PLANE):
                dh = dh + _dot(ref[j], w_ref[off + j * PLANE:off + (j + 1) * PLANE, :])
        dxn, gw = _rms_bwd(x_ref[...], nw_ref[...], dh)
        gx_ref[...] = dx1_ref[...] + dxn
        gn_ref[...] = gn_ref[...] + jnp.sum(gw, axis=0, keepdims=True)

    return _call(
        body, sides, name="in_bwd", grid=(S // TM,),
        in_specs=[_planes(QKV)] * 3 + [_planes(2 * CC), _planes(2 * D), _res((INW, D)), _row(D), _row(D), _res((1, D))],
        out_specs=[_row(D), pl.BlockSpec((1, D), lambda i: (0, 0))],
        out_shape=[jax.ShapeDtypeStruct((S, D), F32), jax.ShapeDtypeStruct((1, D), F32)],
        args=(d_q, d_k, d_v, d_conv, d_gl, w_in, x, d_x1, norm_w))


def mm_tn(name, pairs, tm):
    n = len(pairs)
    M = pairs[0][0].shape[1]
    widths = [b.shape[1] for _, b in pairs]

    def body(*refs):
        for a_ref, b_ref, o_ref, ob_ref in zip(refs[0:2 * n:2], refs[1:2 * n:2], refs[2 * n::2], refs[2 * n + 1::2]):
            r = _dot_tn(a_ref[...], b_ref[...])
            o_ref[...] = r
            ob_ref[...] = r.astype(BF16)

    return _call(
        body, name=name, grid=(M // tm,),
        in_specs=[sp for N in widths for sp in (pl.BlockSpec((S, tm), lambda i: (0, i)), _res((S, N)))],
        out_specs=[pl.BlockSpec((tm, N), lambda i: (i, 0)) for N in widths for _ in range(2)],
        out_shape=[jax.ShapeDtypeStruct((M, N), dt) for N in widths for dt in (F32, BF16)],
        args=[t for pair in pairs for t in pair])


GW_IN_TN = PLANE
GW_IN_SPLIT = (768, 256)


def gw_in(name, h, d_segs, col0, hw, sides=()):
    tn = GW_IN_TN
    starts, t0 = [], 0
    for seg in d_segs:
        starts.append(t0)
        t0 += seg.shape[0]
    ntiles = [seg.shape[0] for seg in d_segs]

    def body(h_ref, *refs):
        a_refs, o_ref, ob_ref = refs[:-2], refs[-2], refs[-1]
        n = pl.program_id(0)
        for a_ref, st, nt in zip(a_refs, starts, ntiles):
            @pl.when((n >= st) & (n < st + nt))
            def _(a_ref=a_ref):
                r = _dot_tn(a_ref[...], h_ref[...])
                o_ref[...] = r
                ob_ref[...] = r.astype(BF16)

    def seg_spec(st, nt):
        return pl.BlockSpec((None, S, tn), lambda n: (jnp.clip(n - st, 0, nt - 1), 0, 0))

    res = _call(
        body, sides, name=name, grid=(INW // tn,),
        in_specs=[pl.BlockSpec((S, hw), lambda n: (0, col0 // hw))] + [seg_spec(st, nt) for st, nt in zip(starts, ntiles)],
        out_specs=[pl.BlockSpec((tn, hw), lambda n: (n, 0))] * 2,
        out_shape=[jax.ShapeDtypeStruct((INW, hw), F32), jax.ShapeDtypeStruct((INW, hw), BF16)],
        args=(h, *d_segs))
    return (res[0], res[1]) if sides else (res, [])


def _place():
    x, y, c = lax.axis_index("x"), lax.axis_index("y"), lax.axis_index("c")
    chips = [(1 - x, y), (x, 1 - y), (1 - x, 1 - y)]
    return x, y, c, chips


def _sems(n):
    return pltpu.SemaphoreType.DMA((n,))


def _remote(src, dst, send, recv, k, to):
    return pltpu.make_async_remote_copy(src_ref=src, dst_ref=dst, send_sem=send.at[k], recv_sem=recv.at[k],
                                        device_id=to, device_id_type=MESH)


def _cast_rows(dst, src, cols=slice(None)):
    rows = src.shape[0]
    step = next((s for s in (128, 64, 32, 16) if rows % s == 0), rows)
    for r0 in range(0, rows, step):
        dst[r0:r0 + step, cols] = src[r0:r0 + step, :].astype(dst.dtype)


def comm_only(name, sides):
    def body():
        pass

    return _call(body, sides, name=name, grid=(1,), in_specs=[], out_specs=[], out_shape=[], args=())[1]


def ag_blocks(shard, dtype):
    R, W = shard.shape

    def copy(outs, scr, k, block, to, src=None):
        dst = outs[0].at[block]
        return _remote(dst if src is None else src, dst, scr[1], scr[2], k, to)

    def local(outs, scr, me):
        return pltpu.make_async_copy(scr[0], outs[0].at[me], scr[3].at[0])

    def start(ins, outs, scr):
        x, y, c, chips = _place()
        me = 4 * x + 2 * y + c
        _cast_rows(scr[0], ins[0])
        local(outs, scr, me).start()
        copy(outs, scr, 0, me, (x, y, 1 - c), src=scr[0]).start()
        for j, (cx, cy) in enumerate(chips):
            copy(outs, scr, 1 + j, me, (cx, cy, c), src=scr[0]).start()

    def finish(ins, outs, scr):
        x, y, c, chips = _place()
        me, sib = 4 * x + 2 * y + c, (x, y, 1 - c)
        passed = []
        for j, (cx, cy) in enumerate(chips):
            theirs = 4 * cx + 2 * cy + c
            copy(outs, scr, 1 + j, theirs, (x, y, c)).wait_recv()
            fwd = copy(outs, scr, 4 + j, theirs, sib)
            fwd.start()
            passed.append(fwd)
        copy(outs, scr, 0, 4 * x + 2 * y + 1 - c, (x, y, c)).wait_recv()
        for j, (cx, cy) in enumerate(chips):
            copy(outs, scr, 4 + j, 4 * cx + 2 * cy + 1 - c, (x, y, c)).wait_recv()
        copy(outs, scr, 0, me, sib, src=scr[0]).wait_send()
        for j, (cx, cy) in enumerate(chips):
            copy(outs, scr, 1 + j, me, (cx, cy, c), src=scr[0]).wait_send()
        for fwd in passed:
            fwd.wait_send()
        local(outs, scr, me).wait()

    return Side((shard,), (VMEM,), (jax.ShapeDtypeStruct((NDEV, R, W), dtype),),
                (pltpu.VMEM((R, W), dtype), _sems(7), _sems(7), _sems(1)), start, finish, None, "dsxy")


def ag_blocks_relay(shard, dtype, transpose=False):
    R, W = shard.shape[::-1] if transpose else shard.shape
    half = R // 2

    def copy(outs, scr, k, block, to, src=None, rows=None):
        dst = outs[0].at[block] if rows is None else outs[0].at[block, pl.ds(rows * half, half), :]
        return _remote(dst if src is None else src, dst, scr[1], scr[2], k, to)

    def local(outs, scr, me):
        return pltpu.make_async_copy(scr[0], outs[0].at[me], scr[3].at[0])

    def own(outs, scr):
        x, y, c, _ = _place()
        me = 4 * x + 2 * y + c
        return [copy(outs, scr, k, me, to, src=scr[0])
                for k, to in enumerate([(x, y, 1 - c), (1 - x, y, c), (x, 1 - y, c)])]

    def start(ins, outs, scr):
        x, y, c, _ = _place()
        if transpose:
            scr[0][...] = ins[0][...].T.astype(dtype)
        else:
            _cast_rows(scr[0], ins[0])
        local(outs, scr, 4 * x + 2 * y + c).start()
        for cp in own(outs, scr):
            cp.start()

    def passed_on(outs, scr):
        x, y, c, _ = _place()
        sib, xn, yn = (x, y, 1 - c), (1 - x, y, c), (x, 1 - y, c)
        b_xn, b_yn, b_dg = 4 * (1 - x) + 2 * y + c, 4 * x + 2 * (1 - y) + c, 4 * (1 - x) + 2 * (1 - y) + c
        near = [copy(outs, scr, 5, b_xn, yn, rows=0), copy(outs, scr, 3, b_xn, sib),
                copy(outs, scr, 6, b_yn, xn, rows=1), copy(outs, scr, 4, b_yn, sib)]
        far = [copy(outs, scr, 7, b_dg, sib, rows=0), copy(outs, scr, 8, b_dg, sib, rows=1)]
        return (b_xn, b_yn, b_dg), near, far

    def mid(ins, outs, scr):
        x, y, c, _ = _place()
        (b_xn, b_yn, _), near, _ = passed_on(outs, scr)
        copy(outs, scr, 1, b_xn, (x, y, c)).wait_recv()
        near[0].start()
        near[1].start()
        copy(outs, scr, 2, b_yn, (x, y, c)).wait_recv()
        near[2].start()
        near[3].start()

    def finish(ins, outs, scr):
        x, y, c, _ = _place()
        here = (x, y, c)
        (b_xn, b_yn, b_dg), near, far = passed_on(outs, scr)
        copy(outs, scr, 5, b_dg, here, rows=0).wait_recv()
        far[0].start()
        copy(outs, scr, 6, b_dg, here, rows=1).wait_recv()
        far[1].start()
        flip = 1 - 2 * c
        copy(outs, scr, 0, 4 * x + 2 * y + 1 - c, here).wait_recv()
        copy(outs, scr, 3, b_xn + flip, here).wait_recv()
        copy(outs, scr, 4, b_yn + flip, here).wait_recv()
        copy(outs, scr, 7, b_dg + flip, here, rows=0).wait_recv()
        copy(outs, scr, 8, b_dg + flip, here, rows=1).wait_recv()
        for cp in own(outs, scr) + near + far:
            cp.wait_send()
        local(outs, scr, 4 * x + 2 * y + c).wait()

    return Side((shard,), (VMEM,), (jax.ShapeDtypeStruct((NDEV, R, W), dtype),),
                (pltpu.VMEM((R, W), dtype), _sems(9), _sems(9), _sems(1)), start, finish, mid, "sxy")


def copies_side(args, out_shape, n_copies, plan, peers):
    def copies(ins, outs, scr):
        return [_remote(s_, d_, scr[0], scr[1], i, to) for i, (s_, d_, to) in enumerate(plan(ins, outs))]

    def start(ins, outs, scr):
        for cp in copies(ins, outs, scr):
            cp.start()

    def finish(ins, outs, scr):
        for cp in copies(ins, outs, scr):
            cp.wait()

    return Side(tuple(args), (ANY,) * len(args), tuple(out_shape), (_sems(n_copies), _sems(n_copies)),
                start, finish, None, peers)


def rs_to_sibling(grads):
    out_shape = [jax.ShapeDtypeStruct((4,) + g.shape[1:], BF16) for g in grads]

    def plan(ins, outs):
        x, y, c, _ = _place()
        return [(g.at[2 * k + 1 - c], r.at[k], (x, y, 1 - c)) for g, r in zip(ins, outs) for k in range(4)]

    return copies_side(grads, out_shape, 4 * len(grads), plan, "s")


def rs_to_chips(parts):
    out_shape = [jax.ShapeDtypeStruct((3,) + p.shape[1:], BF16) for p in parts]

    def plan(ins, outs):
        x, y, c, chips = _place()
        return [(p.at[2 * cx + cy], r.at[j], (cx, cy, c))
                for p, r in zip(ins, outs) for j, (cx, cy) in enumerate(chips)]

    return copies_side(parts, out_shape, 3 * len(parts), plan, "dxy")


def rs_to_chips_combined(part):
    _, R, W = part.shape
    half = R // 2
    top, bot = pl.ds(0, half), pl.ds(half, half)

    def copies(ins, outs, scr):
        p, r = ins[0], outs[0]
        loc_a, loc_b, in_x, in_y, comb_a, comb_b, send, recv, loc = scr
        x, y, c, _ = _place()
        xn, yn = (1 - x, y, c), (x, 1 - y, c)
        k_xn, k_yn, k_dg = 2 * (1 - x) + y, 2 * x + 1 - y, 2 * (1 - x) + 1 - y
        direct = [_remote(p.at[k_xn, top, :], r.at[0, top, :], send, recv, 0, xn),
                  _remote(p.at[k_yn, bot, :], r.at[1, bot, :], send, recv, 1, yn),
                  _remote(p.at[k_dg, top, :], in_x, send, recv, 2, xn),
                  _remote(p.at[k_dg, bot, :], in_y, send, recv, 3, yn)]
        combined = [_remote(comb_a, r.at[1, top, :], send, recv, 4, yn),
                    _remote(comb_b, r.at[0, bot, :], send, recv, 5, xn)]
        local = [pltpu.make_async_copy(p.at[k_yn, top, :], loc_a, loc.at[0]),
                 pltpu.make_async_copy(p.at[k_xn, bot, :], loc_b, loc.at[1])]
        return direct, combined, local

    def start(ins, outs, scr):
        direct, _, local = copies(ins, outs, scr)
        for cp in local + direct:
            cp.start()

    def mid(ins, outs, scr):
        loc_a, loc_b, in_x, in_y, comb_a, comb_b = scr[:6]
        direct, combined, local = copies(ins, outs, scr)
        for mine, arrival, inbox, out, nxt in ((local[0], direct[2], in_x, comb_a, combined[0]),
                                               (local[1], direct[3], in_y, comb_b, combined[1])):
            mine.wait()
            arrival.wait_recv()
            src = loc_a if out is comb_a else loc_b
            out[...] = (src[...].astype(F32) + inbox[...].astype(F32)).astype(BF16)
            nxt.start()

    def finish(ins, outs, scr):
        direct, combined, _ = copies(ins, outs, scr)
        direct[0].wait_recv()
        direct[1].wait_recv()
        combined[0].wait_recv()
        combined[1].wait_recv()
        for cp in direct + combined:
            cp.wait_send()

    buf = pltpu.VMEM((half, W), BF16)
    return Side((part,), (ANY,), (jax.ShapeDtypeStruct((2, R, W), BF16),),
                (buf, buf, buf, buf, buf, buf, _sems(6), _sems(6), _sems(2)), start, finish, mid, "xy")


ADAM_TILE_BYTES = 3 * 512 * 1024


def _row_tiles(rows, width):
    return 2 if rows % 32 == 0 and rows * width * 4 > ADAM_TILE_BYTES else 1


def chip_sum(name, grads, recvs, c_idx, chip_idx):
    n = len(grads)

    def body(s_ref, *refs):
        k = pl.program_id(0)
        for g_ref, r_ref, p_ref, own_ref in zip(refs[:n], refs[n:2 * n], refs[2 * n::2], refs[2 * n + 1::2]):
            tot = g_ref[0] + r_ref[0].astype(F32)
            p_ref[0] = tot.astype(BF16)

            @pl.when(k == s_ref[1])
            def _(own_ref=own_ref, tot=tot):
                own_ref[...] = tot

    def block(g):
        return (1,) + g.shape[1:]

    grid_spec = pltpu.PrefetchScalarGridSpec(
        num_scalar_prefetch=1, grid=(4,),
        in_specs=[pl.BlockSpec(block(g), lambda k, s: (2 * k + s[0], 0, 0)) for g in grads]
        + [pl.BlockSpec(block(g), lambda k, s: (k, 0, 0)) for g in grads],
        out_specs=[sp for g in grads for sp in (pl.BlockSpec(block(g), lambda k, s: (k, 0, 0)),
                                                pl.BlockSpec(g.shape[1:], lambda k, s: (0, 0)))])
    res = pl.pallas_call(
        body, name=name, grid_spec=grid_spec,
        out_shape=[sh for g in grads for sh in (jax.ShapeDtypeStruct((4,) + g.shape[1:], BF16),
                                                jax.ShapeDtypeStruct(g.shape[1:], F32))],
        compiler_params=_cp(dimension_semantics=("arbitrary",)),
    )(jnp.stack([c_idx, chip_idx]), *grads, *recvs)
    return [(res[2 * j], res[2 * j + 1]) for j in range(n)]


def _adamw(w, g, m, v):
    m2 = ADAM_B1 * m + (1.0 - ADAM_B1) * g
    v2 = ADAM_B2 * v + (1.0 - ADAM_B2) * (g * g)
    m_hat = m2 / (1.0 - ADAM_B1 ** ADAM_STEP)
    v_hat = v2 / (1.0 - ADAM_B2 ** ADAM_STEP)
    delta = -ADAM_LR * (m_hat / (jnp.sqrt(v_hat) + ADAM_EPS) + ADAM_WD * w)
    return delta, m2, v2


def shard_adam(name, owns, recvs, w, m, v):
    n = len(owns)
    R = owns[0].shape[0]
    ct = min(o.shape[1] for o in owns)
    first = [sum(o.shape[1] for o in owns[:j]) // ct for j in range(n)]
    count = [o.shape[1] // ct for o in owns]
    nt = _row_tiles(R, ct)
    tr = R // nt

    def body(*refs):
        o_refs, r_refs = refs[:n], refs[n:2 * n]
        w_ref, m_ref, v_ref, g_ref, d_ref, nm_ref, nv_ref = refs[2 * n:]
        g = None
        for j in range(n):
            gj = o_refs[j][...]
            for q in range(recvs[j].shape[0]):
                gj = gj + r_refs[j][q].astype(F32)
            g = gj if g is None else jnp.where(pl.program_id(0) >= first[j], gj, g)
        delta, m2, v2 = _adamw(w_ref[...], g, m_ref[...], v_ref[...])
        g_ref[...] = g
        d_ref[...] = delta
        nm_ref[...] = m2
        nv_ref[...] = v2

    def part(j):
        return pl.BlockSpec((tr, ct), lambda k, i: (i, jnp.clip(k - first[j], 0, count[j] - 1)))

    def part3(j):
        return pl.BlockSpec((recvs[j].shape[0], tr, ct), lambda k, i: (0, i, jnp.clip(k - first[j], 0, count[j] - 1)))

    C = sum(count) * ct
    tile = pl.BlockSpec((tr, ct), lambda k, i: (i, k))
    return pl.pallas_call(
        body, name=name, grid=(sum(count), nt),
        in_specs=[part(j) for j in range(n)] + [part3(j) for j in range(n)] + [tile, tile, tile],
        out_specs=[tile] * 4, out_shape=[jax.ShapeDtypeStruct((R, C), F32)] * 4,
        compiler_params=_cp(dimension_semantics=("arbitrary", "arbitrary")),
    )(*owns, *recvs, w, m, v)


def rows_adam(name, items, steps):
    n = len(items)

    def body(*refs):
        for j in range(n):
            o_ref, r_ref, w_ref, m_ref, v_ref = refs[5 * j:5 * j + 5]
            g = o_ref[...]
            for q in range(r_ref.shape[0]):
                g = g + r_ref[q].astype(F32)
            delta, m2, v2 = _adamw(w_ref[...], g, m_ref[...], v_ref[...])
            for ref, val in zip(refs[5 * n + 4 * j:5 * n + 4 * j + 4], (g, delta, m2, v2)):
                ref[...] = val

    def tile(a):
        return pl.BlockSpec((a.shape[0] // steps, a.shape[1]), lambda i: (i, 0))

    def tile3(a):
        return pl.BlockSpec((a.shape[0], a.shape[1] // steps, a.shape[2]), lambda i: (0, i, 0))

    res = pl.pallas_call(
        body, name=name, grid=(steps,),
        in_specs=[sp for own, recv, w, _, _ in items for sp in (tile(own), tile3(recv), tile(w), tile(w), tile(w))],
        out_specs=[tile(item[2]) for item in items for _ in range(4)],
        out_shape=[jax.ShapeDtypeStruct(item[2].shape, F32) for item in items for _ in range(4)],
        compiler_params=_cp(dimension_semantics=("arbitrary",)))(*[a for item in items for a in item])
    return [tuple(res[4 * j:4 * j + 4]) for j in range(n)]


def block_adam(name, items):
    n = len(items)

    def body(*refs):
        for j, item in enumerate(items):
            o_ref, r_ref, w_ref, m_ref, v_ref = refs[5 * j:5 * j + 5]
            g = o_ref[...]
            for q in range(r_ref.shape[0]):
                g = g + r_ref[q].astype(F32)
            t = (lambda a: a.T) if item[5] else (lambda a: a)
            delta, m2, v2 = _adamw(t(w_ref[...]), g, t(m_ref[...]), t(v_ref[...]))
            for ref, val in zip(refs[5 * n + 4 * j:5 * n + 4 * j + 4], (g, delta, m2, v2)):
                ref[...] = t(val)

    args = [a for item in items for a in item[:5]]
    out_shape = [jax.ShapeDtypeStruct(item[2].shape, F32) for item in items for _ in range(4)]
    res = pl.pallas_call(
        body, name=name, grid=(1,), in_specs=[VMEM] * len(args), out_specs=[VMEM] * len(out_shape),
        out_shape=out_shape, compiler_params=_cp(dimension_semantics=("arbitrary",)))(*args)
    return [tuple(res[4 * j:4 * j + 4]) for j in range(n)]


ROW_N1, ROW_N2, ROW_BG, ROW_QN, ROW_KN, ROW_CB, ROW_LW, ROW_LB, ROW_CW = 0, 1, 2, 4, 5, 6, 7, 8, 9
PACK_ROWS = 40
SMALL = ("norm1_w", "norm2_w", "b_gate", "q_norm_w", "k_norm_w", "conv_b", "conv_ln_w", "conv_ln_b", "conv_w")


def small_sync(g, sq, sides=()):
    ns = len(SMALL)

    def copies(refs):
        pack, recv, send_sems, recv_sems = refs[ns + 2:]
        x, y, c, _ = _place()
        return [pltpu.make_async_remote_copy(
            src_ref=pack, dst_ref=recv.at[4 * x + 2 * y + c], send_sem=send_sems.at[k - 1],
            recv_sem=recv_sems.at[k - 1], device_id=(x ^ (k >> 2), y ^ ((k >> 1) & 1), c ^ (k & 1)),
            device_id_type=MESH) for k in range(1, NDEV)]

    def body(*refs):
        gi = dict(zip(SMALL, refs[:ns]))
        sq_ref, tot, pack, recv, send_sems, recv_sems = refs[ns:]
        x, y, c, _ = _place()
        me = 4 * x + 2 * y + c

        pack[...] = jnp.zeros_like(pack)
        pack[ROW_KN:ROW_KN + 1, LANES:2 * LANES] = jnp.full((1, LANES), (0.5 / D) * jnp.sum(sq_ref[...]), F32)
        pack[ROW_N1:ROW_N1 + 1, :] = gi["norm1_w"][...]
        pack[ROW_N2:ROW_N2 + 1, :] = gi["norm2_w"][...]
        pack[ROW_BG:ROW_BG + 2, :] = gi["b_gate"][...]
        for row, name in ((ROW_QN, "q_norm_w"), (ROW_KN, "k_norm_w")):
            pack[row:row + 1, 0:HD] = gi[name][0:1, 0:HD] + gi[name][0:1, HD:LANES]
        pack[ROW_CB:ROW_CB + 1, 0:CC] = gi["conv_b"][...]
        pack[ROW_LW:ROW_LW + 1, 0:CC] = gi["conv_ln_w"][...]
        pack[ROW_LB:ROW_LB + 1, 0:CC] = gi["conv_ln_b"][...]
        pack[ROW_CW:ROW_CW + KW, 0:CC] = gi["conv_w"][...]

        for cp in copies(refs):
            cp.start()
        recv[me] = pack[...]

    def tail(*refs):
        tot, recv = refs[ns + 1], refs[ns + 3]
        for cp in copies(refs):
            cp.wait()
        acc = recv[0]
        for p in range(1, NDEV):
            acc = acc + recv[p]
        tot[...] = acc

    args = [g[k] for k in SMALL] + [sq]
    res = _call(
        body, sides, name="small_sync", grid=(1,), in_specs=[VMEM] * len(args), out_specs=[VMEM],
        out_shape=[jax.ShapeDtypeStruct((PACK_ROWS, D), F32)],
        scratch_shapes=[pltpu.VMEM((PACK_ROWS, D), F32), pltpu.VMEM((NDEV, PACK_ROWS, D), F32),
                        _sems(NDEV - 1), _sems(NDEV - 1)],
        args=args, own_comm=True, tail=tail)
    return (res[0][0], res[1]) if sides else res[0]


def small_adam(tot, w, m, v, me):
    ns = len(SMALL)

    def body(me_ref, tot, *refs):
        wi = dict(zip(SMALL, refs[:ns]))
        mi = dict(zip(SMALL, refs[ns:2 * ns]))
        vi = dict(zip(SMALL, refs[2 * ns:3 * ns]))
        outs = refs[3 * ns:7 * ns]
        loss_ref = refs[7 * ns]
        me = me_ref[0]

        def shard_grad(name):
            if name == "b_gate":
                return tot[ROW_BG:ROW_BG + 2, pl.ds(pl.multiple_of(me * LANES, LANES), LANES)]
            if name == "conv_w":
                win = tot[ROW_CW:ROW_CW + KW, pl.ds(pl.multiple_of((me // 2) * LANES, LANES), LANES)]
                return jnp.where(me % 2 == 1, win[:, HD:LANES], win[:, 0:HD])
            row = {"norm1_w": ROW_N1, "norm2_w": ROW_N2, "q_norm_w": ROW_QN, "k_norm_w": ROW_KN,
                   "conv_b": ROW_CB, "conv_ln_w": ROW_LW, "conv_ln_b": ROW_LB}[name]
            return tot[row:row + 1, 0:wi[name].shape[1]]

        for i, name in enumerate(SMALL):
            gr = shard_grad(name)
            delta, m2, v2 = _adamw(wi[name][...], gr, mi[name][...], vi[name][...])
            outs[4 * i][...] = gr
            outs[4 * i + 1][...] = delta
            outs[4 * i + 2][...] = m2
            outs[4 * i + 3][...] = v2
        loss_ref[...] = tot[ROW_KN:ROW_KN + 1, LANES:2 * LANES]

    out_shape = []
    for name in SMALL:
        out_shape += [jax.ShapeDtypeStruct(w[name].shape, F32)] * 4
    out_shape.append(jax.ShapeDtypeStruct((1, LANES), F32))
    args = [tot] + [w[k] for k in SMALL] + [m[k] for k in SMALL] + [v[k] for k in SMALL]
    grid_spec = pltpu.PrefetchScalarGridSpec(
        num_scalar_prefetch=1, grid=(1,), in_specs=[VMEM] * len(args), out_specs=[VMEM] * len(out_shape))
    res = pl.pallas_call(body, name="small_adam", grid_spec=grid_spec, out_shape=out_shape)(me, *args)
    out = {name: tuple(res[4 * i:4 * i + 4]) for i, name in enumerate(SMALL)}
    return out, res[4 * ns][0, 0]


MATS = ("w_in", "w_o_attn", "w_pw_conv", "w_out", "w_ffn_in", "w_ffn_out")
TRANSPOSED = ("w_in", "w_ffn_in")
WEIGHTS = ("norm1_w", "w_in", "b_gate", "q_norm_w", "k_norm_w", "w_o_attn", "conv_w", "conv_b", "conv_ln_w",
           "conv_ln_b", "w_pw_conv", "w_out", "norm2_w", "w_ffn_in", "w_ffn_out")


def _blocks_to_cols(blocks):
    n, R, C = blocks.shape
    return blocks.transpose(1, 0, 2).reshape(R, n * C)


def kernel(x, positions, norm1_w, w_in, b_gate, q_norm_w, k_norm_w, w_o_attn, conv_w, conv_b, conv_ln_w, conv_ln_b, w_pw_conv, w_out, norm2_w, w_ffn_in, w_ffn_out, loss_target, m_norm1_w, m_w_in, m_b_gate, m_q_norm_w, m_k_norm_w, m_w_o_attn, m_conv_w, m_conv_b, m_conv_ln_w, m_conv_ln_b, m_w_pw_conv, m_w_out, m_norm2_w, m_w_ffn_in, m_w_ffn_out, v_norm1_w, v_w_in, v_b_gate, v_q_norm_w, v_k_norm_w, v_w_o_attn, v_conv_w, v_conv_b, v_conv_ln_w, v_conv_ln_b, v_w_pw_conv, v_w_out, v_norm2_w, v_w_ffn_in, v_w_ffn_out):
    w = dict(norm1_w=norm1_w, w_in=w_in, b_gate=b_gate, q_norm_w=q_norm_w, k_norm_w=k_norm_w, w_o_attn=w_o_attn,
             conv_w=conv_w, conv_b=conv_b, conv_ln_w=conv_ln_w, conv_ln_b=conv_ln_b, w_pw_conv=w_pw_conv,
             w_out=w_out, norm2_w=norm2_w, w_ffn_in=w_ffn_in, w_ffn_out=w_ffn_out)
    m = dict(norm1_w=m_norm1_w, w_in=m_w_in, b_gate=m_b_gate, q_norm_w=m_q_norm_w, k_norm_w=m_k_norm_w,
             w_o_attn=m_w_o_attn, conv_w=m_conv_w, conv_b=m_conv_b, conv_ln_w=m_conv_ln_w,
             conv_ln_b=m_conv_ln_b, w_pw_conv=m_w_pw_conv, w_out=m_w_out, norm2_w=m_norm2_w,
             w_ffn_in=m_w_ffn_in, w_ffn_out=m_w_ffn_out)
    v = dict(norm1_w=v_norm1_w, w_in=v_w_in, b_gate=v_b_gate, q_norm_w=v_q_norm_w, k_norm_w=v_k_norm_w,
             w_o_attn=v_w_o_attn, conv_w=v_conv_w, conv_b=v_conv_b, conv_ln_w=v_conv_ln_w,
             conv_ln_b=v_conv_ln_b, w_pw_conv=v_w_pw_conv, w_out=v_w_out, norm2_w=v_norm2_w,
             w_ffn_in=v_w_ffn_in, w_ffn_out=v_w_ffn_out)
    def two_d(t):
        t = {k: (a[0] if a.ndim == 3 else a) for k, a in t.items()}
        return {k: (a.T if k in TRANSPOSED else a) for k, a in t.items()}

    w, m, v = two_d(w), two_d(m), two_d(v)

    x2, target = x[0], loss_target[0]
    c_idx = lax.axis_index("c").astype(jnp.int32)
    chip_idx = (2 * lax.axis_index("x") + lax.axis_index("y")).astype(jnp.int32)
    qw2 = jnp.tile(w["q_norm_w"], (1, 2))
    kw2 = jnp.tile(w["k_norm_w"], (1, 2))

    ax, ay = lax.axis_index("x"), lax.axis_index("y")
    chip_order = jnp.stack([2 * ax + ay, 2 * (1 - ax) + ay, 2 * ax + 1 - ay, 2 * (1 - ax) + 1 - ay]).astype(jnp.int32)
    h, proj, w_in_blocks, tabs = in_proj_gather(x2, w["norm1_w"], w["w_in"], chip_order, positions.reshape(S // LANES, LANES))
    w_in_t = w_in_blocks.reshape(INW, D)
    (attn, lse), ((w_ffn_in_blocks,), (w_out_blocks,), (w_o_blocks,), (w_pw_blocks,), (bg_blocks,), (cw_blocks,)) = attn_fwd(
        proj, tabs, qw2, kw2, sides=(ag_blocks_relay(w["w_ffn_in"], BF16), ag_blocks_relay(w["w_out"], BF16),
                                     ag_blocks_relay(w["w_o_attn"], BF16, transpose=True),
                                     ag_blocks_relay(w["w_pw_conv"], BF16, transpose=True),
                                     ag_blocks(w["b_gate"], F32), ag_blocks(w["conv_w"], F32)))
    w_ffn_in_t = w_ffn_in_blocks.reshape(2 * FF, D)
    w_out_f = w_out_blocks.reshape(D, D)
    w_o_t, w_pw_t = w_o_blocks.reshape(D, CC), w_pw_blocks.reshape(D, CC)
    b_gate_f, conv_w_f = _blocks_to_cols(bg_blocks), _blocks_to_cols(cw_blocks)
    cpre, u3 = conv_fwd(proj, conv_w_f, w["conv_b"], w["conv_ln_w"], w["conv_ln_b"])
    x1, z, ya, yb = mix_out(x2, proj, b_gate_f, attn, u3, w_o_t, w_pw_t, w_out_f)
    (h2, gu, f), ((w_ffn_out_blocks,),) = ffn_in(x1, w["norm2_w"], w_ffn_in_t, sides=(ag_blocks_relay(w["w_ffn_out"], BF16),))
    w_ffn_out_f = w_ffn_out_blocks.reshape(FF, D)
    dy, dyb, sq = ffn_out_loss(x1, f, w_ffn_out_f, target)

    g = {}
    def blocks(name, pairs, tm):
        return [t.reshape(NDEV, t.shape[0] // NDEV, t.shape[1]) for t in mm_tn(name, pairs, tm)]

    g_ffn_out, gb_ffn_out = blocks("gw_ffn_out", [(f, dyb)], FF // 2)
    (d_gu, d_x1, d_x1b, g["norm2_w"]), ((ra_ffn_out,),) = ffn_bwd(
        dy, dyb, gu, x1, w["norm2_w"], w_ffn_in_t, w_ffn_out_f, sides=(rs_to_sibling([gb_ffn_out]),))
    g_ffn_in, gb_ffn_in = blocks("gw_ffn_in", [(d_gu, h2)], FF // 2)
    (d_ya, d_yb, d_gl, d_attn, d_u3, g["b_gate"]), ((ra_ffn_in,),) = out_bwd(
        d_x1b, proj, b_gate_f, ya, yb, w_o_t, w_pw_t, w_out_f, sides=(rs_to_sibling([gb_ffn_in]),))
    g_out, gb_out, g_w_o, gb_w_o, g_w_pw, gb_w_pw = blocks(
        "gw_out_o_pw", [(z, d_x1b), (d_ya, attn), (d_yb, u3)], D // 2)
    (d_conv, g["conv_w"], g["conv_b"], g["conv_ln_w"], g["conv_ln_b"]), ((ra_out, ra_w_o, ra_w_pw),) = conv_bwd(
        proj, cpre, d_u3, conv_w_f, w["conv_ln_w"], w["conv_ln_b"],
        sides=(rs_to_sibling([gb_out, gb_w_o, gb_w_pw]),))
    (pb_ffn_out, own_ffn_out), (pb_ffn_in, own_ffn_in), (pb_out, own_out), (pb_w_o, own_w_o), (pb_w_pw, own_w_pw) = chip_sum(
        "chip_sum_early", [g_ffn_out, g_ffn_in, g_out, g_w_o, g_w_pw],
        [ra_ffn_out, ra_ffn_in, ra_out, ra_w_o, ra_w_pw], c_idx, chip_idx)
    (d_q, d_k, d_v, gqw, gkw), ((rb_ffn_out, rb_ffn_in, rb_out, rb_w_o, rb_w_pw),) = attn_bwd(
        proj, tabs, qw2, kw2, d_attn, attn, lse,
        sides=(rs_to_chips([pb_ffn_out, pb_ffn_in, pb_out, pb_w_o, pb_w_pw]),))
    g["q_norm_w"], g["k_norm_w"] = gqw, gkw
    d_segs = (d_q, d_k, d_v, d_conv, d_gl)
    parts, to_sibling, to_chips, owns, from_chips = [], None, None, [], []
    for k, hw in enumerate(GW_IN_SPLIT):
        sides = tuple(s for s in (to_chips, to_sibling) if s is not None)
        (part, part_b), outs = gw_in("gw_in_%d" % k, h, d_segs, sum(GW_IN_SPLIT[:k]), hw, sides=sides)
        outs = list(outs)
        if to_chips is not None:
            from_chips.append(outs.pop(0)[0])
        if to_sibling is not None:
            (pb, own), = chip_sum("chip_sum_w_in_%d" % (k - 1), [parts[-1]], [outs.pop(0)[0]], c_idx, chip_idx)
            owns.append(own)
            to_chips = rs_to_chips_combined(pb)
        else:
            to_chips = None
        parts.append(part.reshape(NDEV, INW // NDEV, hw))
        to_sibling = rs_to_sibling([part_b.reshape(NDEV, INW // NDEV, hw)])
    (grad_x, g["norm1_w"]), ((rb_prev,), (ra_last,)) = in_bwd(
        d_q, d_k, d_v, d_conv, d_gl, w_in_t, x2, d_x1, w["norm1_w"], sides=(to_chips, to_sibling))
    from_chips.append(rb_prev)
    (pb, own), = chip_sum("chip_sum_w_in_%d" % (len(GW_IN_SPLIT) - 1), [parts[-1]], [ra_last], c_idx, chip_idx)
    owns.append(own)
    small_sums, ((rb_last,),) = small_sync(g, sq, sides=(rs_to_chips_combined(pb),))
    small, loss = small_adam(small_sums, w, m, v, (4 * ax + 2 * ay + c_idx).astype(jnp.int32).reshape(1))
    from_chips.append(rb_last)

    adam_o, adam_pw, adam_out = block_adam("adam_w_o_pw_out", [
        (own_w_o, rb_w_o, w["w_o_attn"], m["w_o_attn"], v["w_o_attn"], True),
        (own_w_pw, rb_w_pw, w["w_pw_conv"], m["w_pw_conv"], v["w_pw_conv"], True),
        (own_out, rb_out, w["w_out"], m["w_out"], v["w_out"], False)])
    adam_ffn_in, adam_ffn_out = rows_adam("adam_w_ffn", [
        (own_ffn_in, rb_ffn_in, w["w_ffn_in"], m["w_ffn_in"], v["w_ffn_in"]),
        (own_ffn_out, rb_ffn_out, w["w_ffn_out"], m["w_ffn_out"], v["w_ffn_out"])], 2)
    res = {
        "w_in": shard_adam("adam_w_in", owns, from_chips, w["w_in"], m["w_in"], v["w_in"]),
        "w_ffn_in": adam_ffn_in, "w_ffn_out": adam_ffn_out,
        "w_o_attn": adam_o, "w_pw_conv": adam_pw, "w_out": adam_out,
    }
    res = {k: tuple(a.T if k in TRANSPOSED else a for a in r) for k, r in res.items()}
    res.update(small)

    def shaped(name, a):
        return a.reshape((1,) + a.shape) if name in MATS or name in ("b_gate", "conv_w") else a

    outs = [loss, grad_x.reshape(1, S, D)]
    for i in range(4):
        outs += [shaped(k, res[k][i]) for k in WEIGHTS]
    return tuple(outs)
```

```python
import functools
from typing import Callable, NamedTuple, Optional

import numpy as np
import jax
import jax.numpy as jnp
from jax import lax
from jax.experimental import pallas as pl
from jax.experimental.pallas import tpu as pltpu

F32 = jnp.float32
BF16 = jnp.bfloat16

S = 2048
D = 1024
HD = 64
QKV = 1536
CC = 512
KW = 31
FF = 2816
INW = 7680
OFF_Q, OFF_K, OFF_V, OFF_CA, OFF_CB, OFF_GA, OFF_GB = 0, 1536, 3072, 4608, 5120, 5632, 6656
DILATIONS = (1, 4, 16)
HALF_SPAN = 64
EPS = 1e-6
NEG_INF = -1e30
ROPE_THETA = 500000.0
ROT_DIM = 16

ADAM_LR = 0.001
ADAM_B1 = 0.9
ADAM_B2 = 0.999
ADAM_EPS = 1e-08
ADAM_WD = 0.01
ADAM_STEP = 10

NDEV = 8
LANES = 128
TM = 256
IN_PROJ_TM = 512
TQ = 128
VMEM_LIMIT = 56 * 1024 * 1024
MESH = pl.DeviceIdType.MESH


def _cp(**kw):
    return pltpu.CompilerParams(vmem_limit_bytes=VMEM_LIMIT, **kw)


def _row(width, col=0, tm=TM):
    return pl.BlockSpec((tm, width), lambda i: (i, col))


PLANE = 512


def _planes(width, tm=TM):
    return pl.BlockSpec((width // PLANE, tm, PLANE), lambda i: (0, i, 0))


def _res(shape):
    nd = len(shape)
    return pl.BlockSpec(shape, lambda *_: (0,) * nd, pipeline_mode=pl.Buffered(1))


def _dot(a, b):
    return jnp.dot(a, b, preferred_element_type=F32)


def _dot_nt(a, b):
    return lax.dot_general(a, b, (((1,), (1,)), ((), ())), preferred_element_type=F32)


def _dot_tn(a, b):
    return lax.dot_general(a, b, (((0,), (0,)), ((), ())), preferred_element_type=F32)


def _sigmoid(x):
    return jax.nn.sigmoid(x)


def _dsilu(x, sg):
    return sg * (1.0 + x * (1.0 - sg))


ANY = pl.BlockSpec(memory_space=pl.ANY)
VMEM = pl.BlockSpec(memory_space=pltpu.VMEM)


class Side(NamedTuple):
    args: tuple
    in_specs: tuple
    out_shape: tuple
    scratch: tuple
    start: Callable
    finish: Callable
    mid: Optional[Callable] = None
    peers: str = ""


BARRIER_IDS = {"s": 0, "dxy": 1, "dsxy": 2, "sxy": 3, "xy": 4}


def _peer_barrier(peers):
    x, y, c = lax.axis_index("x"), lax.axis_index("y"), lax.axis_index("c")
    where = {"s": (x, y, 1 - c), "x": (1 - x, y, c), "y": (x, 1 - y, c), "d": (1 - x, 1 - y, c)}
    barrier = pltpu.get_barrier_semaphore()
    for p in peers:
        pl.semaphore_signal(barrier, inc=1, device_id=where[p], device_id_type=MESH)
    pl.semaphore_wait(barrier, len(peers))


def _call(body, sides=(), *, name, grid, in_specs, out_specs, out_shape, scratch_shapes=(), args, own_comm=False,
          tail=None):
    assert tail is None or int(np.prod(grid)) == 1
    ni, no, ns = len(in_specs), len(out_specs), len(scratch_shapes)
    cnt = [(len(s.args), len(s.out_shape), len(s.scratch)) for s in sides]
    peers = "".join(sorted(set("".join(s.peers for s in sides))))
    if own_comm or not sides or any(not s.peers for s in sides):
        peers = ""

    def take(refs, pos, n):
        return refs[pos:pos + n], pos + n

    def full(*refs):
        m_in, pos = take(refs, 0, ni)
        s_in = []
        for a, _, _ in cnt:
            r, pos = take(refs, pos, a)
            s_in.append(r)
        m_out, pos = take(refs, pos, no)
        s_out = []
        for _, o, _ in cnt:
            r, pos = take(refs, pos, o)
            s_out.append(r)
        m_scr, pos = take(refs, pos, ns)
        s_scr = []
        for _, _, c in cnt:
            r, pos = take(refs, pos, c)
            s_scr.append(r)
        if sides:
            first = functools.reduce(jnp.logical_and, [pl.program_id(d) == 0 for d in range(len(grid))])
            last = functools.reduce(jnp.logical_and, [pl.program_id(d) == g - 1 for d, g in enumerate(grid)])

            @pl.when(first)
            def _():
                if peers:
                    _peer_barrier(peers)
                for s, a, o, c in zip(sides, s_in, s_out, s_scr):
                    s.start(a, o, c)

            steps = int(np.prod(grid))
            mid_step = (2 * steps) // 3
            if steps > 1 and any(s.mid is not None for s in sides):
                step = functools.reduce(lambda acc, d: acc * grid[d] + pl.program_id(d), range(len(grid)), 0)

                @pl.when(step == mid_step)
                def _():
                    for s, a, o, c in zip(sides, s_in, s_out, s_scr):
                        if s.mid is not None:
                            s.mid(a, o, c)

        body(*m_in, *m_out, *m_scr)
        if sides:
            @pl.when(last)
            def _():
                for s, a, o, c in zip(sides, s_in, s_out, s_scr):
                    if s.mid is not None and steps == 1:
                        s.mid(a, o, c)
                if tail is not None:
                    tail(*m_in, *m_out, *m_scr)
                for s, a, o, c in zip(sides, s_in, s_out, s_scr):
                    s.finish(a, o, c)
        elif tail is not None:
            tail(*m_in, *m_out, *m_scr)

    res = pl.pallas_call(
        full, name=name, grid=grid,
        in_specs=list(in_specs) + [sp for s in sides for sp in s.in_specs],
        out_specs=list(out_specs) + [ANY for s in sides for _ in s.out_shape],
        out_shape=list(out_shape) + [o for s in sides for o in s.out_shape],
        scratch_shapes=list(scratch_shapes) + [c for s in sides for c in s.scratch],
        compiler_params=_cp(dimension_semantics=("arbitrary",) * len(grid),
                            **({"collective_id": BARRIER_IDS[peers]} if peers else {})),
    )(*args, *[a for s in sides for a in s.args])
    res = list(res)
    if not sides:
        return res
    outs, pos = take(res, 0, no)
    side_outs = []
    for _, o, _ in cnt:
        r, pos = take(res, pos, o)
        side_outs.append(r)
    return outs, side_outs


def _inv_freq_lanes():
    inv = np.float32(ROPE_THETA) ** (-np.arange(0, ROT_DIM, 2, dtype=np.float32) / np.float32(ROT_DIM))
    lane = np.arange(LANES) % HD
    out = np.where(lane < ROT_DIM, inv[lane % (ROT_DIM // 2)], 0.0).astype(np.float32)
    return jnp.asarray(out.reshape(1, LANES))


def _rope_tables(pos, inv_freq):
    ang = pos.astype(F32) * inv_freq
    lane = lax.broadcasted_iota(jnp.int32, ang.shape, 1) % HD
    cs = jnp.cos(ang)
    sn = jnp.sin(ang)
    return (jnp.where(lane < ROT_DIM, cs, 1.0), jnp.where(lane < ROT_DIM // 2, -sn, 0.0),
            jnp.where(lane < ROT_DIM // 2, 0.0, jnp.where(lane < ROT_DIM, sn, 0.0)))


def _rope(v, c, s1, s2):
    return v * c + pltpu.roll(v, LANES - 8, axis=1) * s1 + pltpu.roll(v, 8, axis=1) * s2


def _rope_t(d, c, s1, s2):
    return d * c - pltpu.roll(d, LANES - 8, axis=1) * s1 - pltpu.roll(d, 8, axis=1) * s2


def _head_mat():
    r = lax.broadcasted_iota(jnp.int32, (LANES, LANES), 0) // HD
    c = lax.broadcasted_iota(jnp.int32, (LANES, LANES), 1) // HD
    return jnp.where(r == c, 1.0 / HD, 0.0).astype(BF16)


def _head_mean(t, e):
    hi = t.astype(BF16)
    rest = (t - hi.astype(F32)).astype(BF16)
    return _dot(hi, e) + _dot(rest, e)


def in_proj_gather(x, norm_w, shard_t, chip_order, pos_col):
    R = INW // NDEV
    tm = IN_PROJ_TM
    half, nt = R // 2, S // tm

    def body(ord_ref, x_ref, nw_ref, sh_ref, pos_ref, f_ref, h_ref, p_ref, wfull_ref, c_ref, s1_ref, s2_ref,
             wt, hs, send, recv, loc):
        kk, i = pl.program_id(0), pl.program_id(1)
        x, y, c, _ = _place()
        me, flip = 4 * x + 2 * y + c, 1 - 2 * c
        here, sib, xn, yn = (x, y, c), (x, y, 1 - c), (1 - x, y, c), (x, 1 - y, c)
        b_xn, b_yn, b_dg = 4 * (1 - x) + 2 * y + c, 4 * x + 2 * (1 - y) + c, 4 * (1 - x) + 2 * (1 - y) + c

        def cp(k, block, to, rows=None):
            dst = wt.at[block] if rows is None else wt.at[block, pl.ds(rows * half, half), :]
            return _remote(dst, dst, send, recv, k, to)

        def sends():
            return [cp(0, me, sib), cp(1, me, xn), cp(2, me, yn), cp(3, b_xn, sib), cp(4, b_yn, sib),
                    cp(5, b_xn, yn, rows=0), cp(6, b_yn, xn, rows=1), cp(7, b_dg, sib, rows=0), cp(8, b_dg, sib, rows=1)]

        def keep(j, blk0):
            pair = pl.ds(pl.multiple_of(blk0, 2), 2)
            return pltpu.make_async_copy(wt.at[pair], wfull_ref.at[pair], loc.at[j])

        @pl.when((kk == 0) & (i == 0))
        def _():
            _peer_barrier("sxy")
            _cast_rows(wt.at[me], sh_ref)
            for s_ in sends()[0:3]:
                s_.start()

            def tables(j, _):
                posf = pos_ref[pl.ds(j, 1), :].astype(F32)
                eye = (lax.broadcasted_iota(jnp.int32, (LANES, LANES), 0)
                       == lax.broadcasted_iota(jnp.int32, (LANES, LANES), 1))
                col = jnp.sum(jnp.where(eye, posf, 0.0), axis=1, keepdims=True)
                chunk = pl.ds(pl.multiple_of(j * LANES, LANES), LANES)
                c_ref[chunk, :], s1_ref[chunk, :], s2_ref[chunk, :] = _rope_tables(col, f_ref[...])
                return 0

            lax.fori_loop(0, S // LANES, tables, 0)
            cp(0, me + flip, here).wait_recv()
            keep(0, me - c).start()

        @pl.when((kk == 1) & (i == 0))
        def _():
            cp(1, b_xn, here).wait_recv()
            sends()[5].start()
            sends()[3].start()
            cp(2, b_yn, here).wait_recv()
            sends()[6].start()
            sends()[4].start()
            cp(3, b_xn + flip, here).wait_recv()
            keep(1, b_xn - c).start()

        @pl.when((kk == 2) & (i == 0))
        def _():
            cp(4, b_yn + flip, here).wait_recv()
            keep(2, b_yn - c).start()

        @pl.when((kk == 3) & (i == 0))
        def _():
            cp(5, b_dg, here, rows=0).wait_recv()
            sends()[7].start()
            cp(6, b_dg, here, rows=1).wait_recv()
            sends()[8].start()
            cp(7, b_dg + flip, here, rows=0).wait_recv()
            cp(8, b_dg + flip, here, rows=1).wait_recv()
            keep(3, b_dg - c).start()

        rows = pl.ds(pl.multiple_of(i * tm, tm), tm)

        @pl.when(kk == 0)
        def _():
            xv = x_ref[...]
            r = lax.rsqrt(jnp.mean(xv * xv, axis=-1, keepdims=True) + EPS)
            hb = (xv * r * nw_ref[...]).astype(BF16)
            h_ref[...] = hb
            hs[rows, :] = hb

        h = hs[rows, :]
        chip = ord_ref[kk]
        for cc in range(2):
            p_ref[:, cc * R:(cc + 1) * R] = _dot_nt(h, wt[2 * chip + cc])

        @pl.when((kk == 3) & (i == nt - 1))
        def _():
            for s_ in sends():
                s_.wait_send()
            for j, blk in enumerate((me, b_xn, b_yn, b_dg)):
                keep(j, blk - c).wait()

    def first_pass(kk, i):
        return jnp.where(kk == 0, i, nt - 1)

    grid_spec = pltpu.PrefetchScalarGridSpec(
        num_scalar_prefetch=1, grid=(4, nt),
        in_specs=[pl.BlockSpec((tm, D), lambda kk, i, o: (first_pass(kk, i), 0)),
                  pl.BlockSpec((1, D), lambda kk, i, o: (0, 0)), VMEM, VMEM,
                  pl.BlockSpec((1, LANES), lambda kk, i, o: (0, 0))],
        out_specs=[pl.BlockSpec((tm, D), lambda kk, i, o: (first_pass(kk, i), 0)),
                   pl.BlockSpec((tm, 2 * R), lambda kk, i, o: (i, o[kk])), ANY]
        + [pl.BlockSpec((S, LANES), lambda kk, i, o: (0, 0))] * 3,
        scratch_shapes=[pltpu.VMEM((NDEV, R, D), BF16), pltpu.VMEM((S, D), BF16), _sems(9), _sems(9), _sems(4)])
    res = pl.pallas_call(
        body, name="in_proj_gather", grid_spec=grid_spec,
        out_shape=[jax.ShapeDtypeStruct((S, D), BF16), jax.ShapeDtypeStruct((S, INW), F32),
                   jax.ShapeDtypeStruct((NDEV, R, D), BF16)] + [jax.ShapeDtypeStruct((S, LANES), F32)] * 3,
        compiler_params=_cp(dimension_semantics=("arbitrary", "arbitrary"), collective_id=BARRIER_IDS["sxy"]),
    )(chip_order, x, norm_w, shard_t, pos_col, _inv_freq_lanes())
    return res[0], res[1], res[2], tuple(res[3:])


def _qk_specs():
    nb = QKV // LANES
    return [pl.BlockSpec((S, LANES), functools.partial(lambda hp, g, o: (0, o + g * 4 + hp), o=o))
            for o in (OFF_Q // LANES, OFF_K // LANES, OFF_V // LANES)]


def _tab_specs():
    return [pl.BlockSpec((S, LANES), lambda hp, g: (0, 0), pipeline_mode=pl.Buffered(1))] * 3


def _vec_spec():
    return pl.BlockSpec((1, LANES), lambda hp, g: (0, 0))


def _sub_rows(r, d, start, n):
    if d == 1:
        return pl.ds(start, n)
    return pl.ds(r + d * start, n, stride=d)


def _band_window(i, L):
    W = min(TQ + 2 * HALF_SPAN, L)
    q0 = pl.multiple_of(i * TQ, TQ)
    k0 = pl.multiple_of(jnp.clip(q0 - HALF_SPAN, 0, L - W), HALF_SPAN)
    qpos = q0 + (lax.broadcasted_iota(jnp.int32, (2 * TQ, W), 0) & (TQ - 1))
    kpos = k0 + lax.broadcasted_iota(jnp.int32, (2 * TQ, W), 1)
    valid = jnp.abs(qpos - kpos) <= HALF_SPAN
    return W, q0, k0, valid


def _stack_heads(t, lo):
    z = jnp.zeros_like(t)
    return jnp.concatenate([jnp.where(lo, t, z), jnp.where(lo, z, t)], axis=0)


def _unstack_heads(t2, lo):
    return jnp.where(lo, t2[0:TQ], t2[TQ:2 * TQ])


CHAINS = 8


def _interleave(d):
    ru = min(d, CHAINS)
    return ru, min(CHAINS // ru, S // d // TQ)


def _for_blocks(n, fn):
    if n == 1:
        fn(0)
    else:
        def it(j, _):
            fn(j)
            return 0
        lax.fori_loop(0, n, it, 0)


def attn_fwd(proj, tabs, qw2, kw2, sides=()):
    CH = 256

    def body(q_ref, k_ref, v_ref, c_ref, s1_ref, s2_ref, qw_ref, kw_ref, at_ref, ls_ref,
             qs, ks, vs, osub, lsub, onat, lnat, qn, kn):
        g = pl.program_id(1)
        lo = lax.broadcasted_iota(jnp.int32, (1, LANES), 1) < HD
        e = _head_mat()

        def prep(i, _):
            rows = pl.ds(pl.multiple_of(i * CH, CH), CH)
            c, s1, s2 = c_ref[rows, :], s1_ref[rows, :], s2_ref[rows, :]
            for t_ref, w_ref, out, scale in ((q_ref, qw_ref, qn, HD ** -0.5), (k_ref, kw_ref, kn, 1.0)):
                t = t_ref[rows, :]
                r = lax.rsqrt(_head_mean(t * t, e) + EPS)
                out[rows, :] = _rope(t * r * w_ref[...], c, s1, s2) * scale
            return 0

        lax.fori_loop(0, S // CH, prep, 0, unroll=4)

        def group(gi, d):
            L = S // d

            ru, nb = _interleave(d)

            def stage(r, off):
                for c0 in range(0, L, CH):
                    n = min(CH, L)
                    rows = _sub_rows(r, d, c0, n)
                    dst = pl.ds(off + c0, n)
                    qs[dst, :] = qn[rows, :].astype(BF16)
                    ks[dst, :] = kn[rows, :].astype(BF16)
                    vs[dst, :] = v_ref[rows, :].astype(BF16)

            def one(off, i):
                W, q0, k0, valid = _band_window(i, L)
                q2 = _stack_heads(qs[pl.ds(off + q0, TQ), :], lo)
                sc = jnp.where(valid, _dot_nt(q2, ks[pl.ds(off + k0, W), :]), NEG_INF)
                m = jnp.max(sc, axis=-1, keepdims=True)
                p = jnp.exp(sc - m)
                den = jnp.sum(p, axis=-1, keepdims=True)
                o2 = _dot(p.astype(BF16), vs[pl.ds(off + k0, W), :]) / den
                l2 = jnp.broadcast_to(m + jnp.log(den), (2 * TQ, LANES))
                osub[pl.ds(off + q0, TQ), :] = _unstack_heads(o2, lo)
                lsub[pl.ds(off + q0, TQ), :] = _unstack_heads(l2, lo)

            def unstage(r, off):
                for c0 in range(0, L, CH):
                    n = min(CH, L)
                    rows = _sub_rows(r, d, c0, n)
                    onat[gi, rows, :] = osub[pl.ds(off + c0, n), :]
                    lnat[gi, rows, :] = lsub[pl.ds(off + c0, n), :]

            def step(t, _):
                for u in range(ru):
                    stage(t * ru + u, u * L)
                _for_blocks(L // TQ // nb, lambda j: [one(u * L, j * nb + b) for u in range(ru) for b in range(nb)])
                for u in range(ru):
                    unstage(t * ru + u, u * L)
                return 0

            lax.fori_loop(0, d // ru, step, 0)

        for gi, d in enumerate(DILATIONS):
            pl.when(g == gi)(functools.partial(group, gi, d))

        @pl.when(g == len(DILATIONS) - 1)
        def _():
            def mix(i, _):
                rows = pl.ds(pl.multiple_of(i * CH, CH), CH)
                l0, l1, l2 = lnat[0, rows, :], lnat[1, rows, :], lnat[2, rows, :]
                m = jnp.maximum(jnp.maximum(l0, l1), l2)
                e0, e1, e2 = jnp.exp(l0 - m), jnp.exp(l1 - m), jnp.exp(l2 - m)
                den = e0 + e1 + e2
                a = (e0 * onat[0, rows, :] + e1 * onat[1, rows, :] + e2 * onat[2, rows, :]) / den
                at_ref[rows, :] = a.astype(BF16)
                ls_ref[rows, :] = m + jnp.log(den)
                return 0

            lax.fori_loop(0, S // CH, mix, 0)

    out_spec = pl.BlockSpec((S, LANES), lambda hp, g: (0, hp))
    return _call(
        body, sides, name="attn_fwd", grid=(4, 3),
        in_specs=_qk_specs() + _tab_specs() + [_vec_spec(), _vec_spec()],
        out_specs=[out_spec, out_spec],
        out_shape=[jax.ShapeDtypeStruct((S, CC), BF16), jax.ShapeDtypeStruct((S, CC), F32)],
        scratch_shapes=[pltpu.VMEM((S, LANES), BF16)] * 3 + [pltpu.VMEM((S, LANES), F32)] * 2
        + [pltpu.VMEM((3, S, LANES), F32)] * 2 + [pltpu.VMEM((S, LANES), F32)] * 2,
        args=(proj, proj, proj, *tabs, qw2, kw2))


def attn_bwd(proj, tabs, qw2, kw2, d_attn, attn, lse, sides=()):
    CH = 256

    def body(q_ref, k_ref, v_ref, c_ref, s1_ref, s2_ref, qw_ref, kw_ref, do_ref, at_ref, ls_ref,
             dq_ref, dk_ref, dv_ref, gqw_ref, gkw_ref,
             qs, ks, vs, dos, dsub, lsub, dqs, dks, dvs, dnat, qx, kx, dvn, tnq, tnk, rrq, rrk):
        hp, g = pl.program_id(0), pl.program_id(1)
        lo = lax.broadcasted_iota(jnp.int32, (1, LANES), 1) < HD
        e = _head_mat()
        both = ((q_ref, qw_ref, qx, tnq, rrq, HD ** -0.5), (k_ref, kw_ref, kx, tnk, rrk, 1.0))

        @pl.when((hp == 0) & (g == 0))
        def _():
            gqw_ref[...] = jnp.zeros_like(gqw_ref)
            gkw_ref[...] = jnp.zeros_like(gkw_ref)

        def prep(i, _):
            rows = pl.ds(pl.multiple_of(i * CH, CH), CH)
            dnat[rows, :] = _head_mean(do_ref[rows, :] * at_ref[rows, :].astype(F32), e) * float(HD)
            c, s1, s2 = c_ref[rows, :], s1_ref[rows, :], s2_ref[rows, :]
            for t_ref, w_ref, x, tn_s, rr_s, scale in both:
                t = t_ref[rows, :]
                rr = lax.rsqrt(_head_mean(t * t, e) + EPS)
                tn = t * rr
                rr_s[rows, :] = rr
                tn_s[rows, :] = tn
                x[rows, :] = _rope(tn * w_ref[...], c, s1, s2) * scale
            return 0

        lax.fori_loop(0, S // CH, prep, 0, unroll=4)

        def group(d):
            L = S // d

            ru, nb = _interleave(d)

            def stage(r, off):
                for c0 in range(0, L, CH):
                    n = min(CH, L)
                    rows = _sub_rows(r, d, c0, n)
                    dst = pl.ds(off + c0, n)
                    qs[dst, :] = qx[rows, :].astype(BF16)
                    ks[dst, :] = kx[rows, :].astype(BF16)
                    vs[dst, :] = v_ref[rows, :].astype(BF16)
                    dos[dst, :] = do_ref[rows, :].astype(BF16)
                    dsub[dst, :] = dnat[rows, :]
                    lsub[dst, :] = ls_ref[rows, :]
                    dks[dst, :] = jnp.zeros((n, LANES), F32)
                    dvs[dst, :] = jnp.zeros((n, LANES), F32)

            def one(off, i):
                W, q0, k0, valid = _band_window(i, L)
                qrows, krows = pl.ds(off + q0, TQ), pl.ds(off + k0, W)
                q2 = _stack_heads(qs[qrows, :], lo)
                do2 = _stack_heads(dos[qrows, :], lo)
                kk, vv = ks[krows, :], vs[krows, :]
                lse_b, dd_b = lsub[qrows, :], dsub[qrows, :]
                lse2 = jnp.concatenate([lse_b[:, 0:1], lse_b[:, HD:HD + 1]], axis=0)
                dd2 = jnp.concatenate([dd_b[:, 0:1], dd_b[:, HD:HD + 1]], axis=0)
                sc = jnp.where(valid, _dot_nt(q2, kk), NEG_INF)
                p = jnp.exp(sc - lse2)
                ds = (p * (_dot_nt(do2, vv) - dd2)).astype(BF16)
                dqs[qrows, :] = _unstack_heads(_dot(ds, kk), lo)
                dks[krows, :] = dks[krows, :] + _dot_tn(ds, q2)
                dvs[krows, :] = dvs[krows, :] + _dot_tn(p.astype(BF16), do2)

            def unstage(r, off):
                for c0 in range(0, L, CH):
                    n = min(CH, L)
                    rows = _sub_rows(r, d, c0, n)
                    src = pl.ds(off + c0, n)
                    qx[rows, :] = dqs[src, :]
                    kx[rows, :] = dks[src, :]
                    dvn[rows, :] = dvs[src, :]

            def step(t, _):
                for u in range(ru):
                    stage(t * ru + u, u * L)
                _for_blocks(L // TQ // nb, lambda j: [one(u * L, j * nb + b) for u in range(ru) for b in range(nb)])
                for u in range(ru):
                    unstage(t * ru + u, u * L)
                return 0

            lax.fori_loop(0, d // ru, step, 0)

        for gi, d in enumerate(DILATIONS):
            pl.when(g == gi)(functools.partial(group, d))

        def emit(i, _):
            rows = pl.ds(pl.multiple_of(i * CH, CH), CH)
            c, s1, s2 = c_ref[rows, :], s1_ref[rows, :], s2_ref[rows, :]
            for (_, w_ref, x, tn_s, rr_s, scale), out, gw_ref in zip(both, (dq_ref, dk_ref), (gqw_ref, gkw_ref)):
                tn = tn_s[rows, :]
                dy = _rope_t(x[rows, :] * scale, c, s1, s2)
                gw_ref[0:1, :] = gw_ref[0:1, :] + jnp.sum(dy * tn, axis=0, keepdims=True)
                dtn = dy * w_ref[...]
                out[rows, :] = (rr_s[rows, :] * (dtn - tn * _head_mean(dtn * tn, e))).astype(BF16)
            dv_ref[rows, :] = dvn[rows, :].astype(BF16)
            return 0

        lax.fori_loop(0, S // CH, emit, 0, unroll=4)

    nat_spec = pl.BlockSpec((S, LANES), lambda hp, g: (0, hp))
    out_spec = pl.BlockSpec((None, S, LANES), lambda hp, g: (g, 0, hp))
    acc_spec = pl.BlockSpec((8, LANES), lambda hp, g: (0, 0))
    return _call(
        body, sides, name="attn_bwd", grid=(4, 3),
        in_specs=_qk_specs() + _tab_specs() + [_vec_spec(), _vec_spec(), nat_spec, nat_spec, nat_spec],
        out_specs=[out_spec] * 3 + [acc_spec] * 2,
        out_shape=[jax.ShapeDtypeStruct((QKV // PLANE, S, PLANE), BF16)] * 3 + [jax.ShapeDtypeStruct((8, LANES), F32)] * 2,
        scratch_shapes=[pltpu.VMEM((S, LANES), BF16)] * 4 + [pltpu.VMEM((S, LANES), F32)] * 13,
        args=(proj, proj, proj, *tabs, qw2, kw2, d_attn, attn, lse))


PADR = 16
CT = 128


def _conv_specs():
    return [pl.BlockSpec((S, CC), lambda i: (0, OFF_CA // CC)), pl.BlockSpec((S, CC), lambda i: (0, OFF_CB // CC))]


NCB = CC // LANES


def _pad_zero(pad):
    for cb in range(NCB):
        pad[cb, 0:PADR, :] = jnp.zeros((PADR, LANES), F32)
        pad[cb, PADR + S:PADR + S + PADR, :] = jnp.zeros((PADR, LANES), F32)


def _pad_store(pad, row0, n, val):
    for cb in range(NCB):
        pad[cb, pl.ds(pl.multiple_of(row0 + PADR, 8), n), :] = val[:, cb * LANES:(cb + 1) * LANES]


def _taps(pad_ref, cb, s0, weights):
    acc = jnp.zeros((CT, LANES), F32)
    for k in range(KW):
        acc = acc + weights[k] * pad_ref[cb, pl.ds(s0 + k + 1, CT), :]
    return acc


def conv_fwd(proj, conv_w, conv_b, ln_w, ln_b):
    def body(a_ref, b_ref, w_ref, cb_ref, lw_ref, lb_ref, c_ref, u3_ref, upad):
        _pad_zero(upad)

        def glu(i, _):
            rows = pl.ds(pl.multiple_of(i * TM, TM), TM)
            _pad_store(upad, i * TM, TM, a_ref[rows, :] * _sigmoid(b_ref[rows, :]))
            return 0

        lax.fori_loop(0, S // TM, glu, 0)

        def chunk(i, _):
            s0 = pl.multiple_of(i * CT, CT)
            for cb in range(CC // LANES):
                cols = slice(cb * LANES, (cb + 1) * LANES)
                w = [w_ref[k:k + 1, cols] for k in range(KW)]
                c_ref[pl.ds(s0, CT), cols] = _taps(upad, cb, s0, w) + cb_ref[:, cols]
            cv = c_ref[pl.ds(s0, CT), :]
            mu = jnp.mean(cv, axis=-1, keepdims=True)
            xc = cv - mu
            rstd = lax.rsqrt(jnp.mean(xc * xc, axis=-1, keepdims=True) + EPS)
            yl = xc * rstd * lw_ref[...] + lb_ref[...]
            u3_ref[pl.ds(s0, CT), :] = (yl * _sigmoid(yl)).astype(BF16)
            return 0

        lax.fori_loop(0, S // CT, chunk, 0)

    vec = pl.BlockSpec((1, CC), lambda i: (0, 0))
    full = pl.BlockSpec((S, CC), lambda i: (0, 0))
    return _call(
        body, name="conv_fwd", grid=(1,),
        in_specs=_conv_specs() + [pl.BlockSpec((KW, CC), lambda i: (0, 0)), vec, vec, vec],
        out_specs=[full, full],
        out_shape=[jax.ShapeDtypeStruct((S, CC), F32), jax.ShapeDtypeStruct((S, CC), BF16)],
        scratch_shapes=[pltpu.VMEM((NCB, S + 2 * PADR, LANES), F32)],
        args=(proj, proj, conv_w, conv_b, ln_w, ln_b))


def conv_bwd(proj, cpre, d_u3, conv_w, ln_w, ln_b, sides=()):
    def body(a_ref, b_ref, c_ref, du3_ref, w_ref, lw_ref, lb_ref,
             dc_ref, gw_ref, gcb_ref, glw_ref, glb_ref, upad, dpad):
        _pad_zero(upad)
        _pad_zero(dpad)
        gw_ref[...] = jnp.zeros_like(gw_ref)

        def ln_bwd(i, carry):
            gcb, glw, glb = carry
            rows = pl.ds(pl.multiple_of(i * TM, TM), TM)
            _pad_store(upad, i * TM, TM, a_ref[rows, :] * _sigmoid(b_ref[rows, :]))
            cv = c_ref[rows, :]
            mu = jnp.mean(cv, axis=-1, keepdims=True)
            xc = cv - mu
            rstd = lax.rsqrt(jnp.mean(xc * xc, axis=-1, keepdims=True) + EPS)
            xh = xc * rstd
            yl = xh * lw_ref[...] + lb_ref[...]
            dyl = du3_ref[rows, :] * _dsilu(yl, _sigmoid(yl))
            dxh = dyl * lw_ref[...]
            dcv = rstd * (dxh - jnp.mean(dxh, axis=-1, keepdims=True)
                          - xh * jnp.mean(dxh * xh, axis=-1, keepdims=True))
            _pad_store(dpad, i * TM, TM, dcv)
            return (gcb + jnp.sum(dcv, axis=0, keepdims=True),
                    glw + jnp.sum(dyl * xh, axis=0, keepdims=True),
                    glb + jnp.sum(dyl, axis=0, keepdims=True))

        z = jnp.zeros((1, CC), F32)
        gcb, glw, glb = lax.fori_loop(0, S // TM, ln_bwd, (z, z, z))
        gcb_ref[...] = gcb
        glw_ref[...] = glw
        glb_ref[...] = glb

        def chunk(i, _):
            s0 = pl.multiple_of(i * CT, CT)
            for cb in range(CC // LANES):
                cols = slice(cb * LANES, (cb + 1) * LANES)
                wr = [w_ref[KW - 1 - k:KW - k, cols] for k in range(KW)]
                du = _taps(dpad, cb, s0, wr)
                dcv = dpad[cb, pl.ds(s0 + PADR, CT), :]
                for k in range(KW):
                    gw_ref[k:k + 1, cols] = gw_ref[k:k + 1, cols] + jnp.sum(
                        upad[cb, pl.ds(s0 + k + 1, CT), :] * dcv, axis=0, keepdims=True)
                av = a_ref[pl.ds(s0, CT), cols]
                sb = _sigmoid(b_ref[pl.ds(s0, CT), cols])
                dc_ref[0, pl.ds(s0, CT), cols] = (du * sb).astype(BF16)
                dc_ref[1, pl.ds(s0, CT), cols] = (du * av * sb * (1.0 - sb)).astype(BF16)
            return 0

        lax.fori_loop(0, S // CT, chunk, 0)

    vec = pl.BlockSpec((1, CC), lambda i: (0, 0))
    full = pl.BlockSpec((S, CC), lambda i: (0, 0))
    wsp = pl.BlockSpec((KW, CC), lambda i: (0, 0))
    return _call(
        body, sides, name="conv_bwd", grid=(1,),
        in_specs=_conv_specs() + [full, full, wsp, vec, vec],
        out_specs=[pl.BlockSpec((2, S, CC), lambda i: (0, 0, 0)), wsp, vec, vec, vec],
        out_shape=[jax.ShapeDtypeStruct((2, S, CC), BF16), jax.ShapeDtypeStruct((KW, CC), F32)]
        + [jax.ShapeDtypeStruct((1, CC), F32)] * 3,
        scratch_shapes=[pltpu.VMEM((NCB, S + 2 * PADR, LANES), F32)] * 2,
        args=(proj, proj, cpre, d_u3, conv_w, ln_w, ln_b))


def _gate_specs():
    return [_row(CC, col=OFF_GA // CC + j) for j in range(4)]


def _gates(g_refs, bg_ref):
    ga = _sigmoid(jnp.concatenate([g_refs[0][...], g_refs[1][...]], axis=1) + bg_ref[0:1, :])
    gb = _sigmoid(jnp.concatenate([g_refs[2][...], g_refs[3][...]], axis=1) + bg_ref[1:2, :])
    return ga, gb


def mix_out(x, proj, b_gate, attn, u3, w_o, w_pw, w_out):
    def body(x_ref, g0, g1, g2, g3, bg_ref, at_ref, u3_ref, wo_ref, wp_ref, wout_ref,
             x1_ref, z_ref, ya_ref, yb_ref):
        ga, gb = _gates((g0, g1, g2, g3), bg_ref)
        ya = _dot_nt(at_ref[...], wo_ref[...])
        yb = _dot_nt(u3_ref[...], wp_ref[...])
        z = (ga * ya + gb * yb).astype(BF16)
        ya_ref[...] = ya.astype(BF16)
        yb_ref[...] = yb.astype(BF16)
        z_ref[...] = z
        x1_ref[...] = x_ref[...] + _dot(z, wout_ref[...])

    return pl.pallas_call(
        body, name="mix_out", grid=(S // TM,),
        in_specs=[_row(D)] + _gate_specs() + [_res((2, D)), _row(CC), _row(CC),
                                              _res((D, CC)), _res((D, CC)), _res((D, D))],
        out_specs=[_row(D)] * 4,
        out_shape=[jax.ShapeDtypeStruct((S, D), F32)] + [jax.ShapeDtypeStruct((S, D), BF16)] * 3,
        compiler_params=_cp(dimension_semantics=("arbitrary",)),
    )(x, proj, proj, proj, proj, b_gate, attn, u3, w_o, w_pw, w_out)


def out_bwd(d_x1b, proj, b_gate, ya, yb, w_o, w_pw, w_out, sides=()):
    def body(dx_ref, g0, g1, g2, g3, bg_ref, ya_ref, yb_ref, wo_ref, wp_ref, wout_ref,
             dya_ref, dyb_ref, dgl_ref, dat_ref, du3_ref, gbg_ref):
        @pl.when(pl.program_id(0) == 0)
        def _():
            gbg_ref[...] = jnp.zeros_like(gbg_ref)

        ga, gb = _gates((g0, g1, g2, g3), bg_ref)
        dz = _dot_nt(dx_ref[...], wout_ref[...])
        dya = (dz * ga).astype(BF16)
        dyb = (dz * gb).astype(BF16)
        dgla = dz * ya_ref[...].astype(F32) * ga * (1.0 - ga)
        dglb = dz * yb_ref[...].astype(F32) * gb * (1.0 - gb)
        dya_ref[...] = dya
        dyb_ref[...] = dyb
        for j in range(2):
            dgl_ref[j] = dgla[:, j * PLANE:(j + 1) * PLANE].astype(BF16)
            dgl_ref[2 + j] = dglb[:, j * PLANE:(j + 1) * PLANE].astype(BF16)
        gbg_ref[0:1, :] = gbg_ref[0:1, :] + jnp.sum(dgla, axis=0, keepdims=True)
        gbg_ref[1:2, :] = gbg_ref[1:2, :] + jnp.sum(dglb, axis=0, keepdims=True)
        dat_ref[...] = _dot(dya, wo_ref[...])
        du3_ref[...] = _dot(dyb, wp_ref[...])

    return _call(
        body, sides, name="out_bwd", grid=(S // TM,),
        in_specs=[_row(D)] + _gate_specs() + [_res((2, D)), _row(D), _row(D),
                                              _res((D, CC)), _res((D, CC)), _res((D, D))],
        out_specs=[_row(D), _row(D), _planes(2 * D), _row(CC), _row(CC), pl.BlockSpec((2, D), lambda i: (0, 0))],
        out_shape=[jax.ShapeDtypeStruct((S, D), BF16)] * 2 + [jax.ShapeDtypeStruct((2 * D // PLANE, S, PLANE), BF16)]
        + [jax.ShapeDtypeStruct((S, CC), F32)] * 2 + [jax.ShapeDtypeStruct((2, D), F32)],
        args=(d_x1b, proj, proj, proj, proj, b_gate, ya, yb, w_o, w_pw, w_out))


def ffn_in(x1, norm_w, w_ffn_in, sides=()):
    half = FF // 2

    def body(x_ref, nw_ref, w_ref, h_ref, gu_ref, f_ref):
        xv = x_ref[...]
        r = lax.rsqrt(jnp.mean(xv * xv, axis=-1, keepdims=True) + EPS)
        h = (xv * r * nw_ref[...]).astype(BF16)
        h_ref[...] = h
        for j in range(2):
            gt = _dot_nt(h, w_ref[j * half:(j + 1) * half, :])
            up = _dot_nt(h, w_ref[FF + j * half:FF + (j + 1) * half, :])
            gu_ref[:, j * half:(j + 1) * half] = gt.astype(BF16)
            gu_ref[:, FF + j * half:FF + (j + 1) * half] = up.astype(BF16)
            f_ref[:, j * half:(j + 1) * half] = (gt * _sigmoid(gt) * up).astype(BF16)

    return _call(
        body, sides, name="ffn_in", grid=(S // TM,),
        in_specs=[_row(D), _res((1, D)), _res((2 * FF, D))],
        out_specs=[_row(D), _row(2 * FF), _row(FF)],
        out_shape=[jax.ShapeDtypeStruct((S, D), BF16), jax.ShapeDtypeStruct((S, 2 * FF), BF16),
                   jax.ShapeDtypeStruct((S, FF), BF16)],
        args=(x1, norm_w, w_ffn_in))


def ffn_out_loss(x1, f, w_ffn_out, target):
    def body(x_ref, f_ref, w_ref, t_ref, dy_ref, dyb_ref, sq_ref):
        @pl.when(pl.program_id(0) == 0)
        def _():
            sq_ref[...] = jnp.zeros_like(sq_ref)

        diff = x_ref[...] + _dot(f_ref[...], w_ref[...]) - t_ref[...]
        dy = diff * (1.0 / D)
        dy_ref[...] = dy
        dyb_ref[...] = dy.astype(BF16)
        sq_ref[...] = sq_ref[...] + jnp.sum((diff * diff).reshape(TM // 8, 8, D), axis=0)

    return pl.pallas_call(
        body, name="ffn_out_loss", grid=(S // TM,),
        in_specs=[_row(D), _row(FF), _res((FF, D)), _row(D)],
        out_specs=[_row(D), _row(D), pl.BlockSpec((8, D), lambda i: (0, 0))],
        out_shape=[jax.ShapeDtypeStruct((S, D), F32), jax.ShapeDtypeStruct((S, D), BF16),
                   jax.ShapeDtypeStruct((8, D), F32)],
        compiler_params=_cp(dimension_semantics=("arbitrary",)),
    )(x1, f, w_ffn_out, target)


def _rms_bwd(xv, nw, dh):
    r = lax.rsqrt(jnp.mean(xv * xv, axis=-1, keepdims=True) + EPS)
    xn = xv * r
    dxn = dh * nw
    dx = r * (dxn - xn * jnp.mean(dxn * xn, axis=-1, keepdims=True))
    return dx, dh * xn


def ffn_bwd(dy, dyb, gu, x1, norm_w, w_ffn_in, w_ffn_out, sides=()):
    def body(dy_ref, dyb_ref, gu_ref, x_ref, nw_ref, wi_ref, wo_ref, dgu_ref, dx_ref, dxb_ref, gn_ref):
        @pl.when(pl.program_id(0) == 0)
        def _():
            gn_ref[...] = jnp.zeros_like(gn_ref)

        df = _dot_nt(dyb_ref[...], wo_ref[...])
        gt = gu_ref[:, 0:FF].astype(F32)
        up = gu_ref[:, FF:2 * FF].astype(F32)
        sg = _sigmoid(gt)
        dgt = (df * up * _dsilu(gt, sg)).astype(BF16)
        dup = (df * gt * sg).astype(BF16)
        dgu_ref[:, 0:FF] = dgt
        dgu_ref[:, FF:2 * FF] = dup
        dh = _dot(dgt, wi_ref[0:FF, :]) + _dot(dup, wi_ref[FF:2 * FF, :])
        dxn, gw = _rms_bwd(x_ref[...], nw_ref[...], dh)
        dx = dy_ref[...] + dxn
        dx_ref[...] = dx
        dxb_ref[...] = dx.astype(BF16)
        gn_ref[...] = gn_ref[...] + jnp.sum(gw, axis=0, keepdims=True)

    return _call(
        body, sides, name="ffn_bwd", grid=(S // TM,),
        in_specs=[_row(D), _row(D), _row(2 * FF), _row(D), _res((1, D)), _res((2 * FF, D)), _res((FF, D))],
        out_specs=[_row(2 * FF), _row(D), _row(D), pl.BlockSpec((1, D), lambda i: (0, 0))],
        out_shape=[jax.ShapeDtypeStruct((S, 2 * FF), BF16), jax.ShapeDtypeStruct((S, D), F32),
                   jax.ShapeDtypeStruct((S, D), BF16), jax.ShapeDtypeStruct((1, D), F32)],
        args=(dy, dyb, gu, x1, norm_w, w_ffn_in, w_ffn_out))


def in_bwd(d_q, d_k, d_v, d_conv, d_gl, w_in, x, d_x1, norm_w, sides=()):
    segs = ((OFF_Q, QKV), (OFF_K, QKV), (OFF_V, QKV), (OFF_CA, 2 * CC), (OFF_GA, 2 * D))

    def body(dq_ref, dk_ref, dv_ref, dc_ref, dg_ref, w_ref, x_ref, dx1_ref, nw_ref, gx_ref, gn_ref):
        @pl.when(pl.program_id(0) == 0)
        def _():
            gn_ref[...] = jnp.zeros_like(gn_ref)

        dh = jnp.zeros((TM, D), F32)
        for ref, (off, width) in zip((dq_ref, dk_ref, dv_ref, dc_ref, dg_ref), segs):
            for j in range(width // PLANE):
                dh = dh + _dot(ref[j], w_ref[off + j * PLANE:off + (j + 1) * PLANE, :])
        dxn, gw = _rms_bwd(x_ref[...], nw_ref[...], dh)
        gx_ref[...] = dx1_ref[...] + dxn
        gn_ref[...] = gn_ref[...] + jnp.sum(gw, axis=0, keepdims=True)

    return _call(
        body, sides, name="in_bwd", grid=(S // TM,),
        in_specs=[_planes(QKV)] * 3 + [_planes(2 * CC), _planes(2 * D), _res((INW, D)), _row(D), _row(D), _res((1, D))],
        out_specs=[_row(D), pl.BlockSpec((1, D), lambda i: (0, 0))],
        out_shape=[jax.ShapeDtypeStruct((S, D), F32), jax.ShapeDtypeStruct((1, D), F32)],
        args=(d_q, d_k, d_v, d_conv, d_gl, w_in, x, d_x1, norm_w))


def mm_tn(name, pairs, tm):
    n = len(pairs)
    M = pairs[0][0].shape[1]
    widths = [b.shape[1] for _, b in pairs]

    def body(*refs):
        for a_ref, b_ref, o_ref, ob_ref in zip(refs[0:2 * n:2], refs[1:2 * n:2], refs[2 * n::2], refs[2 * n + 1::2]):
            r = _dot_tn(a_ref[...], b_ref[...])
            o_ref[...] = r
            ob_ref[...] = r.astype(BF16)

    return _call(
        body, name=name, grid=(M // tm,),
        in_specs=[sp for N in widths for sp in (pl.BlockSpec((S, tm), lambda i: (0, i)), _res((S, N)))],
        out_specs=[pl.BlockSpec((tm, N), lambda i: (i, 0)) for N in widths for _ in range(2)],
        out_shape=[jax.ShapeDtypeStruct((M, N), dt) for N in widths for dt in (F32, BF16)],
        args=[t for pair in pairs for t in pair])


GW_IN_TN = PLANE
GW_IN_SPLIT = (768, 256)


def gw_in(name, h, d_segs, col0, hw, sides=()):
    tn = GW_IN_TN
    starts, t0 = [], 0
    for seg in d_segs:
        starts.append(t0)
        t0 += seg.shape[0]
    ntiles = [seg.shape[0] for seg in d_segs]

    def body(h_ref, *refs):
        a_refs, o_ref, ob_ref = refs[:-2], refs[-2], refs[-1]
        n = pl.program_id(0)
        for a_ref, st, nt in zip(a_refs, starts, ntiles):
            @pl.when((n >= st) & (n < st + nt))
            def _(a_ref=a_ref):
                r = _dot_tn(a_ref[...], h_ref[...])
                o_ref[...] = r
                ob_ref[...] = r.astype(BF16)

    def seg_spec(st, nt):
        return pl.BlockSpec((None, S, tn), lambda n: (jnp.clip(n - st, 0, nt - 1), 0, 0))

    res = _call(
        body, sides, name=name, grid=(INW // tn,),
        in_specs=[pl.BlockSpec((S, hw), lambda n: (0, col0 // hw))] + [seg_spec(st, nt) for st, nt in zip(starts, ntiles)],
        out_specs=[pl.BlockSpec((tn, hw), lambda n: (n, 0))] * 2,
        out_shape=[jax.ShapeDtypeStruct((INW, hw), F32), jax.ShapeDtypeStruct((INW, hw), BF16)],
        args=(h, *d_segs))
    return (res[0], res[1]) if sides else (res, [])


def _place():
    x, y, c = lax.axis_index("x"), lax.axis_index("y"), lax.axis_index("c")
    chips = [(1 - x, y), (x, 1 - y), (1 - x, 1 - y)]
    return x, y, c, chips


def _sems(n):
    return pltpu.SemaphoreType.DMA((n,))


def _remote(src, dst, send, recv, k, to):
    return pltpu.make_async_remote_copy(src_ref=src, dst_ref=dst, send_sem=send.at[k], recv_sem=recv.at[k],
                                        device_id=to, device_id_type=MESH)


def _cast_rows(dst, src, cols=slice(None)):
    rows = src.shape[0]
    step = next((s for s in (128, 64, 32, 16) if rows % s == 0), rows)
    for r0 in range(0, rows, step):
        dst[r0:r0 + step, cols] = src[r0:r0 + step, :].astype(dst.dtype)


def comm_only(name, sides):
    def body():
        pass

    return _call(body, sides, name=name, grid=(1,), in_specs=[], out_specs=[], out_shape=[], args=())[1]


def ag_blocks(shard, dtype):
    R, W = shard.shape

    def copy(outs, scr, k, block, to, src=None):
        dst = outs[0].at[block]
        return _remote(dst if src is None else src, dst, scr[1], scr[2], k, to)

    def local(outs, scr, me):
        return pltpu.make_async_copy(scr[0], outs[0].at[me], scr[3].at[0])

    def start(ins, outs, scr):
        x, y, c, chips = _place()
        me = 4 * x + 2 * y + c
        _cast_rows(scr[0], ins[0])
        local(outs, scr, me).start()
        copy(outs, scr, 0, me, (x, y, 1 - c), src=scr[0]).start()
        for j, (cx, cy) in enumerate(chips):
            copy(outs, scr, 1 + j, me, (cx, cy, c), src=scr[0]).start()

    def finish(ins, outs, scr):
        x, y, c, chips = _place()
        me, sib = 4 * x + 2 * y + c, (x, y, 1 - c)
        passed = []
        for j, (cx, cy) in enumerate(chips):
            theirs = 4 * cx + 2 * cy + c
            copy(outs, scr, 1 + j, theirs, (x, y, c)).wait_recv()
            fwd = copy(outs, scr, 4 + j, theirs, sib)
            fwd.start()
            passed.append(fwd)
        copy(outs, scr, 0, 4 * x + 2 * y + 1 - c, (x, y, c)).wait_recv()
        for j, (cx, cy) in enumerate(chips):
            copy(outs, scr, 4 + j, 4 * cx + 2 * cy + 1 - c, (x, y, c)).wait_recv()
        copy(outs, scr, 0, me, sib, src=scr[0]).wait_send()
        for j, (cx, cy) in enumerate(chips):
            copy(outs, scr, 1 + j, me, (cx, cy, c), src=scr[0]).wait_send()
        for fwd in passed:
            fwd.wait_send()
        local(outs, scr, me).wait()

    return Side((shard,), (VMEM,), (jax.ShapeDtypeStruct((NDEV, R, W), dtype),),
                (pltpu.VMEM((R, W), dtype), _sems(7), _sems(7), _sems(1)), start, finish, None, "dsxy")


def ag_blocks_relay(shard, dtype, transpose=False):
    R, W = shard.shape[::-1] if transpose else shard.shape
    half = R // 2

    def copy(outs, scr, k, block, to, src=None, rows=None):
        dst = outs[0].at[block] if rows is None else outs[0].at[block, pl.ds(rows * half, half), :]
        return _remote(dst if src is None else src, dst, scr[1], scr[2], k, to)

    def local(outs, scr, me):
        return pltpu.make_async_copy(scr[0], outs[0].at[me], scr[3].at[0])

    def own(outs, scr):
        x, y, c, _ = _place()
        me = 4 * x + 2 * y + c
        return [copy(outs, scr, k, me, to, src=scr[0])
                for k, to in enumerate([(x, y, 1 - c), (1 - x, y, c), (x, 1 - y, c)])]

    def start(ins, outs, scr):
        x, y, c, _ = _place()
        if transpose:
            scr[0][...] = ins[0][...].T.astype(dtype)
        else:
            _cast_rows(scr[0], ins[0])
        local(outs, scr, 4 * x + 2 * y + c).start()
        for cp in own(outs, scr):
            cp.start()

    def passed_on(outs, scr):
        x, y, c, _ = _place()
        sib, xn, yn = (x, y, 1 - c), (1 - x, y, c), (x, 1 - y, c)
        b_xn, b_yn, b_dg = 4 * (1 - x) + 2 * y + c, 4 * x + 2 * (1 - y) + c, 4 * (1 - x) + 2 * (1 - y) + c
        near = [copy(outs, scr, 5, b_xn, yn, rows=0), copy(outs, scr, 3, b_xn, sib),
                copy(outs, scr, 6, b_yn, xn, rows=1), copy(outs, scr, 4, b_yn, sib)]
        far = [copy(outs, scr, 7, b_dg, sib, rows=0), copy(outs, scr, 8, b_dg, sib, rows=1)]
        return (b_xn, b_yn, b_dg), near, far

    def mid(ins, outs, scr):
        x, y, c, _ = _place()
        (b_xn, b_yn, _), near, _ = passed_on(outs, scr)
        copy(outs, scr, 1, b_xn, (x, y, c)).wait_recv()
        near[0].start()
        near[1].start()
        copy(outs, scr, 2, b_yn, (x, y, c)).wait_recv()
        near[2].start()
        near[3].start()

    def finish(ins, outs, scr):
        x, y, c, _ = _place()
        here = (x, y, c)
        (b_xn, b_yn, b_dg), near, far = passed_on(outs, scr)
        copy(outs, scr, 5, b_dg, here, rows=0).wait_recv()
        far[0].start()
        copy(outs, scr, 6, b_dg, here, rows=1).wait_recv()
        far[1].start()
        flip = 1 - 2 * c
        copy(outs, scr, 0, 4 * x + 2 * y + 1 - c, here).wait_recv()
        copy(outs, scr, 3, b_xn + flip, here).wait_recv()
        copy(outs, scr, 4, b_yn + flip, here).wait_recv()
        copy(outs, scr, 7, b_dg + flip, here, rows=0).wait_recv()
        copy(outs, scr, 8, b_dg + flip, here, rows=1).wait_recv()
        for cp in own(outs, scr) + near + far:
            cp.wait_send()
        local(outs, scr, 4 * x + 2 * y + c).wait()

    return Side((shard,), (VMEM,), (jax.ShapeDtypeStruct((NDEV, R, W), dtype),),
                (pltpu.VMEM((R, W), dtype), _sems(9), _sems(9), _sems(1)), start, finish, mid, "sxy")


def copies_side(args, out_shape, n_copies, plan, peers):
    def copies(ins, outs, scr):
        return [_remote(s_, d_, scr[0], scr[1], i, to) for i, (s_, d_, to) in enumerate(plan(ins, outs))]

    def start(ins, outs, scr):
        for cp in copies(ins, outs, scr):
            cp.start()

    def finish(ins, outs, scr):
        for cp in copies(ins, outs, scr):
            cp.wait()

    return Side(tuple(args), (ANY,) * len(args), tuple(out_shape), (_sems(n_copies), _sems(n_copies)),
                start, finish, None, peers)


def rs_to_sibling(grads):
    out_shape = [jax.ShapeDtypeStruct((4,) + g.shape[1:], BF16) for g in grads]

    def plan(ins, outs):
        x, y, c, _ = _place()
        return [(g.at[2 * k + 1 - c], r.at[k], (x, y, 1 - c)) for g, r in zip(ins, outs) for k in range(4)]

    return copies_side(grads, out_shape, 4 * len(grads), plan, "s")


def rs_to_chips(parts):
    out_shape = [jax.ShapeDtypeStruct((3,) + p.shape[1:], BF16) for p in parts]

    def plan(ins, outs):
        x, y, c, chips = _place()
        return [(p.at[2 * cx + cy], r.at[j], (cx, cy, c))
                for p, r in zip(ins, outs) for j, (cx, cy) in enumerate(chips)]

    return copies_side(parts, out_shape, 3 * len(parts), plan, "dxy")


def rs_to_chips_combined(part):
    _, R, W = part.shape
    half = R // 2
    top, bot = pl.ds(0, half), pl.ds(half, half)

    def copies(ins, outs, scr):
        p, r = ins[0], outs[0]
        loc_a, loc_b, in_x, in_y, comb_a, comb_b, send, recv, loc = scr
        x, y, c, _ = _place()
        xn, yn = (1 - x, y, c), (x, 1 - y, c)
        k_xn, k_yn, k_dg = 2 * (1 - x) + y, 2 * x + 1 - y, 2 * (1 - x) + 1 - y
        direct = [_remote(p.at[k_xn, top, :], r.at[0, top, :], send, recv, 0, xn),
                  _remote(p.at[k_yn, bot, :], r.at[1, bot, :], send, recv, 1, yn),
                  _remote(p.at[k_dg, top, :], in_x, send, recv, 2, xn),
                  _remote(p.at[k_dg, bot, :], in_y, send, recv, 3, yn)]
        combined = [_remote(comb_a, r.at[1, top, :], send, recv, 4, yn),
                    _remote(comb_b, r.at[0, bot, :], send, recv, 5, xn)]
        local = [pltpu.make_async_copy(p.at[k_yn, top, :], loc_a, loc.at[0]),
                 pltpu.make_async_copy(p.at[k_xn, bot, :], loc_b, loc.at[1])]
        return direct, combined, local

    def start(ins, outs, scr):
        direct, _, local = copies(ins, outs, scr)
        for cp in local + direct:
            cp.start()

    def mid(ins, outs, scr):
        loc_a, loc_b, in_x, in_y, comb_a, comb_b = scr[:6]
        direct, combined, local = copies(ins, outs, scr)
        for mine, arrival, inbox, out, nxt in ((local[0], direct[2], in_x, comb_a, combined[0]),
                                               (local[1], direct[3], in_y, comb_b, combined[1])):
            mine.wait()
            arrival.wait_recv()
            src = loc_a if out is comb_a else loc_b
            out[...] = (src[...].astype(F32) + inbox[...].astype(F32)).astype(BF16)
            nxt.start()

    def finish(ins, outs, scr):
        direct, combined, _ = copies(ins, outs, scr)
        direct[0].wait_recv()
        direct[1].wait_recv()
        combined[0].wait_recv()
        combined[1].wait_recv()
        for cp in direct + combined:
            cp.wait_send()

    buf = pltpu.VMEM((half, W), BF16)
    return Side((part,), (ANY,), (jax.ShapeDtypeStruct((2, R, W), BF16),),
                (buf, buf, buf, buf, buf, buf, _sems(6), _sems(6), _sems(2)), start, finish, mid, "xy")


ADAM_TILE_BYTES = 3 * 512 * 1024


def _row_tiles(rows, width):
    return 2 if rows % 32 == 0 and rows * width * 4 > ADAM_TILE_BYTES else 1


def chip_sum(name, grads, recvs, c_idx, chip_idx):
    n = len(grads)

    def body(s_ref, *refs):
        k = pl.program_id(0)
        for g_ref, r_ref, p_ref, own_ref in zip(refs[:n], refs[n:2 * n], refs[2 * n::2], refs[2 * n + 1::2]):
            tot = g_ref[0] + r_ref[0].astype(F32)
            p_ref[0] = tot.astype(BF16)

            @pl.when(k == s_ref[1])
            def _(own_ref=own_ref, tot=tot):
                own_ref[...] = tot

    def block(g):
        return (1,) + g.shape[1:]

    grid_spec = pltpu.PrefetchScalarGridSpec(
        num_scalar_prefetch=1, grid=(4,),
        in_specs=[pl.BlockSpec(block(g), lambda k, s: (2 * k + s[0], 0, 0)) for g in grads]
        + [pl.BlockSpec(block(g), lambda k, s: (k, 0, 0)) for g in grads],
        out_specs=[sp for g in grads for sp in (pl.BlockSpec(block(g), lambda k, s: (k, 0, 0)),
                                                pl.BlockSpec(g.shape[1:], lambda k, s: (0, 0)))])
    res = pl.pallas_call(
        body, name=name, grid_spec=grid_spec,
        out_shape=[sh for g in grads for sh in (jax.ShapeDtypeStruct((4,) + g.shape[1:], BF16),
                                                jax.ShapeDtypeStruct(g.shape[1:], F32))],
        compiler_params=_cp(dimension_semantics=("arbitrary",)),
    )(jnp.stack([c_idx, chip_idx]), *grads, *recvs)
    return [(res[2 * j], res[2 * j + 1]) for j in range(n)]


def _adamw(w, g, m, v):
    m2 = ADAM_B1 * m + (1.0 - ADAM_B1) * g
    v2 = ADAM_B2 * v + (1.0 - ADAM_B2) * (g * g)
    m_hat = m2 / (1.0 - ADAM_B1 ** ADAM_STEP)
    v_hat = v2 / (1.0 - ADAM_B2 ** ADAM_STEP)
    delta = -ADAM_LR * (m_hat / (jnp.sqrt(v_hat) + ADAM_EPS) + ADAM_WD * w)
    return delta, m2, v2


def shard_adam(name, owns, recvs, w, m, v):
    n = len(owns)
    R = owns[0].shape[0]
    ct = min(o.shape[1] for o in owns)
    first = [sum(o.shape[1] for o in owns[:j]) // ct for j in range(n)]
    count = [o.shape[1] // ct for o in owns]
    nt = _row_tiles(R, ct)
    tr = R // nt

    def body(*refs):
        o_refs, r_refs = refs[:n], refs[n:2 * n]
        w_ref, m_ref, v_ref, g_ref, d_ref, nm_ref, nv_ref = refs[2 * n:]
        g = None
        for j in range(n):
            gj = o_refs[j][...]
            for q in range(recvs[j].shape[0]):
                gj = gj + r_refs[j][q].astype(F32)
            g = gj if g is None else jnp.where(pl.program_id(0) >= first[j], gj, g)
        delta, m2, v2 = _adamw(w_ref[...], g, m_ref[...], v_ref[...])
        g_ref[...] = g
        d_ref[...] = delta
        nm_ref[...] = m2
        nv_ref[...] = v2

    def part(j):
        return pl.BlockSpec((tr, ct), lambda k, i: (i, jnp.clip(k - first[j], 0, count[j] - 1)))

    def part3(j):
        return pl.BlockSpec((recvs[j].shape[0], tr, ct), lambda k, i: (0, i, jnp.clip(k - first[j], 0, count[j] - 1)))

    C = sum(count) * ct
    tile = pl.BlockSpec((tr, ct), lambda k, i: (i, k))
    return pl.pallas_call(
        body, name=name, grid=(sum(count), nt),
        in_specs=[part(j) for j in range(n)] + [part3(j) for j in range(n)] + [tile, tile, tile],
        out_specs=[tile] * 4, out_shape=[jax.ShapeDtypeStruct((R, C), F32)] * 4,
        compiler_params=_cp(dimension_semantics=("arbitrary", "arbitrary")),
    )(*owns, *recvs, w, m, v)


def rows_adam(name, items, steps):
    n = len(items)

    def body(*refs):
        for j in range(n):
            o_ref, r_ref, w_ref, m_ref, v_ref = refs[5 * j:5 * j + 5]
            g = o_ref[...]
            for q in range(r_ref.shape[0]):
                g = g + r_ref[q].astype(F32)
            delta, m2, v2 = _adamw(w_ref[...], g, m_ref[...], v_ref[...])
            for ref, val in zip(refs[5 * n + 4 * j:5 * n + 4 * j + 4], (g, delta, m2, v2)):
                ref[...] = val

    def tile(a):
        return pl.BlockSpec((a.shape[0] // steps, a.shape[1]), lambda i: (i, 0))

    def tile3(a):
        return pl.BlockSpec((a.shape[0], a.shape[1] // steps, a.shape[2]), lambda i: (0, i, 0))

    res = pl.pallas_call(
        body, name=name, grid=(steps,),
        in_specs=[sp for own, recv, w, _, _ in items for sp in (tile(own), tile3(recv), tile(w), tile(w), tile(w))],
        out_specs=[tile(item[2]) for item in items for _ in range(4)],
        out_shape=[jax.ShapeDtypeStruct(item[2].shape, F32) for item in items for _ in range(4)],
        compiler_params=_cp(dimension_semantics=("arbitrary",)))(*[a for item in items for a in item])
    return [tuple(res[4 * j:4 * j + 4]) for j in range(n)]


def _block_adam(transposed, in_refs, out_refs):
    for j, io_t in enumerate(transposed):
        o_ref, r_ref, w_ref, m_ref, v_ref = in_refs[5 * j:5 * j + 5]
        g = o_ref[...]
        for q in range(r_ref.shape[0]):
            g = g + r_ref[q].astype(F32)
        t = (lambda a: a.T) if io_t else (lambda a: a)
        delta, m2, v2 = _adamw(t(w_ref[...]), g, t(m_ref[...]), t(v_ref[...]))
        for ref, val in zip(out_refs[4 * j:4 * j + 4], (g, delta, m2, v2)):
            ref[...] = t(val)


ROW_N1, ROW_N2, ROW_BG, ROW_QN, ROW_KN, ROW_CB, ROW_LW, ROW_LB, ROW_CW = 0, 1, 2, 4, 5, 6, 7, 8, 9
PACK_ROWS = 40
SMALL = ("norm1_w", "norm2_w", "b_gate", "q_norm_w", "k_norm_w", "conv_b", "conv_ln_w", "conv_ln_b", "conv_w")


def small_sync(g, sq, sides=()):
    ns = len(SMALL)

    def copies(refs):
        pack, recv, send_sems, recv_sems = refs[ns + 2:]
        x, y, c, _ = _place()
        return [pltpu.make_async_remote_copy(
            src_ref=pack, dst_ref=recv.at[4 * x + 2 * y + c], send_sem=send_sems.at[k - 1],
            recv_sem=recv_sems.at[k - 1], device_id=(x ^ (k >> 2), y ^ ((k >> 1) & 1), c ^ (k & 1)),
            device_id_type=MESH) for k in range(1, NDEV)]

    def body(*refs):
        gi = dict(zip(SMALL, refs[:ns]))
        sq_ref, tot, pack, recv, send_sems, recv_sems = refs[ns:]
        x, y, c, _ = _place()
        me = 4 * x + 2 * y + c

        pack[...] = jnp.zeros_like(pack)
        pack[ROW_KN:ROW_KN + 1, LANES:2 * LANES] = jnp.full((1, LANES), (0.5 / D) * jnp.sum(sq_ref[...]), F32)
        pack[ROW_N1:ROW_N1 + 1, :] = gi["norm1_w"][...]
        pack[ROW_N2:ROW_N2 + 1, :] = gi["norm2_w"][...]
        pack[ROW_BG:ROW_BG + 2, :] = gi["b_gate"][...]
        for row, name in ((ROW_QN, "q_norm_w"), (ROW_KN, "k_norm_w")):
            pack[row:row + 1, 0:HD] = gi[name][0:1, 0:HD] + gi[name][0:1, HD:LANES]
        pack[ROW_CB:ROW_CB + 1, 0:CC] = gi["conv_b"][...]
        pack[ROW_LW:ROW_LW + 1, 0:CC] = gi["conv_ln_w"][...]
        pack[ROW_LB:ROW_LB + 1, 0:CC] = gi["conv_ln_b"][...]
        pack[ROW_CW:ROW_CW + KW, 0:CC] = gi["conv_w"][...]

        for cp in copies(refs):
            cp.start()
        recv[me] = pack[...]

    def tail(*refs):
        tot, recv = refs[ns + 1], refs[ns + 3]
        for cp in copies(refs):
            cp.wait()
        acc = recv[0]
        for p in range(1, NDEV):
            acc = acc + recv[p]
        tot[...] = acc

    args = [g[k] for k in SMALL] + [sq]
    res = _call(
        body, sides, name="small_sync", grid=(1,), in_specs=[VMEM] * len(args), out_specs=[VMEM],
        out_shape=[jax.ShapeDtypeStruct((PACK_ROWS, D), F32)],
        scratch_shapes=[pltpu.VMEM((PACK_ROWS, D), F32), pltpu.VMEM((NDEV, PACK_ROWS, D), F32),
                        _sems(NDEV - 1), _sems(NDEV - 1)],
        args=args, own_comm=True, tail=tail)
    return (res[0][0], res[1]) if sides else res[0]


def small_adam(tot, w, m, v, me, blocks):
    ns, nb = len(SMALL), len(blocks)

    def body(me_ref, tot, *refs):
        wi = dict(zip(SMALL, refs[:ns]))
        mi = dict(zip(SMALL, refs[ns:2 * ns]))
        vi = dict(zip(SMALL, refs[2 * ns:3 * ns]))
        block_in, refs = refs[3 * ns:3 * ns + 5 * nb], refs[:3 * ns] + refs[3 * ns + 5 * nb:]
        outs = refs[3 * ns:7 * ns]
        loss_ref = refs[7 * ns]
        _block_adam([b[5] for b in blocks], block_in, refs[7 * ns + 1:])
        me = me_ref[0]

        def shard_grad(name):
            if name == "b_gate":
                return tot[ROW_BG:ROW_BG + 2, pl.ds(pl.multiple_of(me * LANES, LANES), LANES)]
            if name == "conv_w":
                win = tot[ROW_CW:ROW_CW + KW, pl.ds(pl.multiple_of((me // 2) * LANES, LANES), LANES)]
                return jnp.where(me % 2 == 1, win[:, HD:LANES], win[:, 0:HD])
            row = {"norm1_w": ROW_N1, "norm2_w": ROW_N2, "q_norm_w": ROW_QN, "k_norm_w": ROW_KN,
                   "conv_b": ROW_CB, "conv_ln_w": ROW_LW, "conv_ln_b": ROW_LB}[name]
            return tot[row:row + 1, 0:wi[name].shape[1]]

        for i, name in enumerate(SMALL):
            gr = shard_grad(name)
            delta, m2, v2 = _adamw(wi[name][...], gr, mi[name][...], vi[name][...])
            outs[4 * i][...] = gr
            outs[4 * i + 1][...] = delta
            outs[4 * i + 2][...] = m2
            outs[4 * i + 3][...] = v2
        loss_ref[...] = tot[ROW_KN:ROW_KN + 1, LANES:2 * LANES]

    out_shape = []
    for name in SMALL:
        out_shape += [jax.ShapeDtypeStruct(w[name].shape, F32)] * 4
    out_shape.append(jax.ShapeDtypeStruct((1, LANES), F32))
    out_shape += [jax.ShapeDtypeStruct(b[2].shape, F32) for b in blocks for _ in range(4)]
    args = ([tot] + [w[k] for k in SMALL] + [m[k] for k in SMALL] + [v[k] for k in SMALL]
            + [a for b in blocks for a in b[:5]])
    grid_spec = pltpu.PrefetchScalarGridSpec(
        num_scalar_prefetch=1, grid=(1,), in_specs=[VMEM] * len(args), out_specs=[VMEM] * len(out_shape))
    res = pl.pallas_call(body, name="small_adam", grid_spec=grid_spec, out_shape=out_shape,
                         compiler_params=_cp(dimension_semantics=("arbitrary",)))(me, *args)
    out = {name: tuple(res[4 * i:4 * i + 4]) for i, name in enumerate(SMALL)}
    return out, res[4 * ns][0, 0], [tuple(res[4 * ns + 1 + 4 * j:4 * ns + 5 + 4 * j]) for j in range(nb)]


MATS = ("w_in", "w_o_attn", "w_pw_conv", "w_out", "w_ffn_in", "w_ffn_out")
TRANSPOSED = ("w_in", "w_ffn_in")
WEIGHTS = ("norm1_w", "w_in", "b_gate", "q_norm_w", "k_norm_w", "w_o_attn", "conv_w", "conv_b", "conv_ln_w",
           "conv_ln_b", "w_pw_conv", "w_out", "norm2_w", "w_ffn_in", "w_ffn_out")


def _blocks_to_cols(blocks):
    n, R, C = blocks.shape
    return blocks.transpose(1, 0, 2).reshape(R, n * C)


def kernel(x, positions, norm1_w, w_in, b_gate, q_norm_w, k_norm_w, w_o_attn, conv_w, conv_b, conv_ln_w, conv_ln_b, w_pw_conv, w_out, norm2_w, w_ffn_in, w_ffn_out, loss_target, m_norm1_w, m_w_in, m_b_gate, m_q_norm_w, m_k_norm_w, m_w_o_attn, m_conv_w, m_conv_b, m_conv_ln_w, m_conv_ln_b, m_w_pw_conv, m_w_out, m_norm2_w, m_w_ffn_in, m_w_ffn_out, v_norm1_w, v_w_in, v_b_gate, v_q_norm_w, v_k_norm_w, v_w_o_attn, v_conv_w, v_conv_b, v_conv_ln_w, v_conv_ln_b, v_w_pw_conv, v_w_out, v_norm2_w, v_w_ffn_in, v_w_ffn_out):
    w = dict(norm1_w=norm1_w, w_in=w_in, b_gate=b_gate, q_norm_w=q_norm_w, k_norm_w=k_norm_w, w_o_attn=w_o_attn,
             conv_w=conv_w, conv_b=conv_b, conv_ln_w=conv_ln_w, conv_ln_b=conv_ln_b, w_pw_conv=w_pw_conv,
             w_out=w_out, norm2_w=norm2_w, w_ffn_in=w_ffn_in, w_ffn_out=w_ffn_out)
    m = dict(norm1_w=m_norm1_w, w_in=m_w_in, b_gate=m_b_gate, q_norm_w=m_q_norm_w, k_norm_w=m_k_norm_w,
             w_o_attn=m_w_o_attn, conv_w=m_conv_w, conv_b=m_conv_b, conv_ln_w=m_conv_ln_w,
             conv_ln_b=m_conv_ln_b, w_pw_conv=m_w_pw_conv, w_out=m_w_out, norm2_w=m_norm2_w,
             w_ffn_in=m_w_ffn_in, w_ffn_out=m_w_ffn_out)
    v = dict(norm1_w=v_norm1_w, w_in=v_w_in, b_gate=v_b_gate, q_norm_w=v_q_norm_w, k_norm_w=v_k_norm_w,
             w_o_attn=v_w_o_attn, conv_w=v_conv_w, conv_b=v_conv_b, conv_ln_w=v_conv_ln_w,
             conv_ln_b=v_conv_ln_b, w_pw_conv=v_w_pw_conv, w_out=v_w_out, norm2_w=v_norm2_w,
             w_ffn_in=v_w_ffn_in, w_ffn_out=v_w_ffn_out)
    def two_d(t):
        t = {k: (a[0] if a.ndim == 3 else a) for k, a in t.items()}
        return {k: (a.T if k in TRANSPOSED else a) for k, a in t.items()}

    w, m, v = two_d(w), two_d(m), two_d(v)

    x2, target = x[0], loss_target[0]
    c_idx = lax.axis_index("c").astype(jnp.int32)
    chip_idx = (2 * lax.axis_index("x") + lax.axis_index("y")).astype(jnp.int32)
    qw2 = jnp.tile(w["q_norm_w"], (1, 2))
    kw2 = jnp.tile(w["k_norm_w"], (1, 2))

    ax, ay = lax.axis_index("x"), lax.axis_index("y")
    chip_order = jnp.stack([2 * ax + ay, 2 * (1 - ax) + ay, 2 * ax + 1 - ay, 2 * (1 - ax) + 1 - ay]).astype(jnp.int32)
    h, proj, w_in_blocks, tabs = in_proj_gather(x2, w["norm1_w"], w["w_in"], chip_order, positions.reshape(S // LANES, LANES))
    w_in_t = w_in_blocks.reshape(INW, D)
    (attn, lse), ((w_ffn_in_blocks,), (w_out_blocks,), (w_o_blocks,), (w_pw_blocks,), (bg_blocks,), (cw_blocks,)) = attn_fwd(
        proj, tabs, qw2, kw2, sides=(ag_blocks_relay(w["w_ffn_in"], BF16), ag_blocks_relay(w["w_out"], BF16),
                                     ag_blocks_relay(w["w_o_attn"], BF16, transpose=True),
                                     ag_blocks_relay(w["w_pw_conv"], BF16, transpose=True),
                                     ag_blocks(w["b_gate"], F32), ag_blocks(w["conv_w"], F32)))
    w_ffn_in_t = w_ffn_in_blocks.reshape(2 * FF, D)
    w_out_f = w_out_blocks.reshape(D, D)
    w_o_t, w_pw_t = w_o_blocks.reshape(D, CC), w_pw_blocks.reshape(D, CC)
    b_gate_f, conv_w_f = _blocks_to_cols(bg_blocks), _blocks_to_cols(cw_blocks)
    cpre, u3 = conv_fwd(proj, conv_w_f, w["conv_b"], w["conv_ln_w"], w["conv_ln_b"])
    x1, z, ya, yb = mix_out(x2, proj, b_gate_f, attn, u3, w_o_t, w_pw_t, w_out_f)
    (h2, gu, f), ((w_ffn_out_blocks,),) = ffn_in(x1, w["norm2_w"], w_ffn_in_t, sides=(ag_blocks_relay(w["w_ffn_out"], BF16),))
    w_ffn_out_f = w_ffn_out_blocks.reshape(FF, D)
    dy, dyb, sq = ffn_out_loss(x1, f, w_ffn_out_f, target)

    g = {}
    def blocks(name, pairs, tm):
        return [t.reshape(NDEV, t.shape[0] // NDEV, t.shape[1]) for t in mm_tn(name, pairs, tm)]

    g_ffn_out, gb_ffn_out = blocks("gw_ffn_out", [(f, dyb)], FF // 2)
    (d_gu, d_x1, d_x1b, g["norm2_w"]), ((ra_ffn_out,),) = ffn_bwd(
        dy, dyb, gu, x1, w["norm2_w"], w_ffn_in_t, w_ffn_out_f, sides=(rs_to_sibling([gb_ffn_out]),))
    g_ffn_in, gb_ffn_in = blocks("gw_ffn_in", [(d_gu, h2)], FF // 2)
    (d_ya, d_yb, d_gl, d_attn, d_u3, g["b_gate"]), ((ra_ffn_in,),) = out_bwd(
        d_x1b, proj, b_gate_f, ya, yb, w_o_t, w_pw_t, w_out_f, sides=(rs_to_sibling([gb_ffn_in]),))
    g_out, gb_out, g_w_o, gb_w_o, g_w_pw, gb_w_pw = blocks(
        "gw_out_o_pw", [(z, d_x1b), (d_ya, attn), (d_yb, u3)], D // 2)
    (d_conv, g["conv_w"], g["conv_b"], g["conv_ln_w"], g["conv_ln_b"]), ((ra_out, ra_w_o, ra_w_pw),) = conv_bwd(
        proj, cpre, d_u3, conv_w_f, w["conv_ln_w"], w["conv_ln_b"],
        sides=(rs_to_sibling([gb_out, gb_w_o, gb_w_pw]),))
    (pb_ffn_out, own_ffn_out), (pb_ffn_in, own_ffn_in), (pb_out, own_out), (pb_w_o, own_w_o), (pb_w_pw, own_w_pw) = chip_sum(
        "chip_sum_early", [g_ffn_out, g_ffn_in, g_out, g_w_o, g_w_pw],
        [ra_ffn_out, ra_ffn_in, ra_out, ra_w_o, ra_w_pw], c_idx, chip_idx)
    (d_q, d_k, d_v, gqw, gkw), ((rb_ffn_out, rb_ffn_in, rb_out, rb_w_o, rb_w_pw),) = attn_bwd(
        proj, tabs, qw2, kw2, d_attn, attn, lse,
        sides=(rs_to_chips([pb_ffn_out, pb_ffn_in, pb_out, pb_w_o, pb_w_pw]),))
    g["q_norm_w"], g["k_norm_w"] = gqw, gkw
    d_segs = (d_q, d_k, d_v, d_conv, d_gl)
    parts, to_sibling, to_chips, owns, from_chips = [], None, None, [], []
    for k, hw in enumerate(GW_IN_SPLIT):
        sides = tuple(s for s in (to_chips, to_sibling) if s is not None)
        (part, part_b), outs = gw_in("gw_in_%d" % k, h, d_segs, sum(GW_IN_SPLIT[:k]), hw, sides=sides)
        outs = list(outs)
        if to_chips is not None:
            from_chips.append(outs.pop(0)[0])
        if to_sibling is not None:
            (pb, own), = chip_sum("chip_sum_w_in_%d" % (k - 1), [parts[-1]], [outs.pop(0)[0]], c_idx, chip_idx)
            owns.append(own)
            to_chips = rs_to_chips_combined(pb)
        else:
            to_chips = None
        parts.append(part.reshape(NDEV, INW // NDEV, hw))
        to_sibling = rs_to_sibling([part_b.reshape(NDEV, INW // NDEV, hw)])
    (grad_x, g["norm1_w"]), ((rb_prev,), (ra_last,)) = in_bwd(
        d_q, d_k, d_v, d_conv, d_gl, w_in_t, x2, d_x1, w["norm1_w"], sides=(to_chips, to_sibling))
    from_chips.append(rb_prev)
    (pb, own), = chip_sum("chip_sum_w_in_%d" % (len(GW_IN_SPLIT) - 1), [parts[-1]], [ra_last], c_idx, chip_idx)
    owns.append(own)
    small_sums, ((rb_last,),) = small_sync(g, sq, sides=(rs_to_chips_combined(pb),))
    small, loss, (adam_o, adam_pw, adam_out) = small_adam(
        small_sums, w, m, v, (4 * ax + 2 * ay + c_idx).astype(jnp.int32).reshape(1),
        [(own_w_o, rb_w_o, w["w_o_attn"], m["w_o_attn"], v["w_o_attn"], True),
         (own_w_pw, rb_w_pw, w["w_pw_conv"], m["w_pw_conv"], v["w_pw_conv"], True),
         (own_out, rb_out, w["w_out"], m["w_out"], v["w_out"], False)])
    from_chips.append(rb_last)

    adam_ffn_in, adam_ffn_out = rows_adam("adam_w_ffn", [
        (own_ffn_in, rb_ffn_in, w["w_ffn_in"], m["w_ffn_in"], v["w_ffn_in"]),
        (own_ffn_out, rb_ffn_out, w["w_ffn_out"], m["w_ffn_out"], v["w_ffn_out"])], 2)
    res = {
        "w_in": shard_adam("adam_w_in", owns, from_chips, w["w_in"], m["w_in"], v["w_in"]),
        "w_ffn_in": adam_ffn_in, "w_ffn_out": adam_ffn_out,
        "w_o_attn": adam_o, "w_pw_conv": adam_pw, "w_out": adam_out,
    }
    res = {k: tuple(a.T if k in TRANSPOSED else a for a in r) for k, r in res.items()}
    res.update(small)

    def shaped(name, a):
        return a.reshape((1,) + a.shape) if name in MATS or name in ("b_gate", "conv_w") else a

    outs = [loss, grad_x.reshape(1, S, D)]
    for i in range(4):
        outs += [shaped(k, res[k][i]) for k in WEIGHTS]
    return tuple(outs)
```

```python
import functools
from typing import Callable, NamedTuple, Optional

import numpy as np
import jax
import jax.numpy as jnp
from jax import lax
from jax.experimental import pallas as pl
from jax.experimental.pallas import tpu as pltpu

F32 = jnp.float32
BF16 = jnp.bfloat16

S = 2048
D = 1024
HD = 64
QKV = 1536
CC = 512
KW = 31
FF = 2816
INW = 7680
OFF_Q, OFF_K, OFF_V, OFF_CA, OFF_CB, OFF_GA, OFF_GB = 0, 1536, 3072, 4608, 5120, 5632, 6656
DILATIONS = (1, 4, 16)
HALF_SPAN = 64
EPS = 1e-6
NEG_INF = -1e30
ROPE_THETA = 500000.0
ROT_DIM = 16

ADAM_LR = 0.001
ADAM_B1 = 0.9
ADAM_B2 = 0.999
ADAM_EPS = 1e-08
ADAM_WD = 0.01
ADAM_STEP = 10

NDEV = 8
LANES = 128
TM = 256
IN_PROJ_TM = 512
TQ = 128
VMEM_LIMIT = 56 * 1024 * 1024
MESH = pl.DeviceIdType.MESH


def _cp(**kw):
    return pltpu.CompilerParams(vmem_limit_bytes=VMEM_LIMIT, **kw)


def _row(width, col=0, tm=TM):
    return pl.BlockSpec((tm, width), lambda i: (i, col))


PLANE = 512


def _planes(width, tm=TM):
    return pl.BlockSpec((width // PLANE, tm, PLANE), lambda i: (0, i, 0))


def _res(shape):
    nd = len(shape)
    return pl.BlockSpec(shape, lambda *_: (0,) * nd, pipeline_mode=pl.Buffered(1))


def _dot(a, b):
    return jnp.dot(a, b, preferred_element_type=F32)


def _dot_nt(a, b):
    return lax.dot_general(a, b, (((1,), (1,)), ((), ())), preferred_element_type=F32)


def _dot_tn(a, b):
    return lax.dot_general(a, b, (((0,), (0,)), ((), ())), preferred_element_type=F32)


def _sigmoid(x):
    return jax.nn.sigmoid(x)


def _dsilu(x, sg):
    return sg * (1.0 + x * (1.0 - sg))


ANY = pl.BlockSpec(memory_space=pl.ANY)
VMEM = pl.BlockSpec(memory_space=pltpu.VMEM)


class Side(NamedTuple):
    args: tuple
    in_specs: tuple
    out_shape: tuple
    scratch: tuple
    start: Callable
    finish: Callable
    mid: Optional[Callable] = None
    peers: str = ""


BARRIER_IDS = {"s": 0, "dxy": 1, "dsxy": 2, "sxy": 3, "xy": 4}


def _peer_barrier(peers):
    x, y, c = lax.axis_index("x"), lax.axis_index("y"), lax.axis_index("c")
    where = {"s": (x, y, 1 - c), "x": (1 - x, y, c), "y": (x, 1 - y, c), "d": (1 - x, 1 - y, c)}
    barrier = pltpu.get_barrier_semaphore()
    for p in peers:
        pl.semaphore_signal(barrier, inc=1, device_id=where[p], device_id_type=MESH)
    pl.semaphore_wait(barrier, len(peers))


def _call(body, sides=(), *, name, grid, in_specs, out_specs, out_shape, scratch_shapes=(), args, own_comm=False,
          tail=None):
    assert tail is None or int(np.prod(grid)) == 1
    ni, no, ns = len(in_specs), len(out_specs), len(scratch_shapes)
    cnt = [(len(s.args), len(s.out_shape), len(s.scratch)) for s in sides]
    peers = "".join(sorted(set("".join(s.peers for s in sides))))
    if own_comm or not sides or any(not s.peers for s in sides):
        peers = ""

    def take(refs, pos, n):
        return refs[pos:pos + n], pos + n

    def full(*refs):
        m_in, pos = take(refs, 0, ni)
        s_in = []
        for a, _, _ in cnt:
            r, pos = take(refs, pos, a)
            s_in.append(r)
        m_out, pos = take(refs, pos, no)
        s_out = []
        for _, o, _ in cnt:
            r, pos = take(refs, pos, o)
            s_out.append(r)
        m_scr, pos = take(refs, pos, ns)
        s_scr = []
        for _, _, c in cnt:
            r, pos = take(refs, pos, c)
            s_scr.append(r)
        if sides:
            first = functools.reduce(jnp.logical_and, [pl.program_id(d) == 0 for d in range(len(grid))])
            last = functools.reduce(jnp.logical_and, [pl.program_id(d) == g - 1 for d, g in enumerate(grid)])

            @pl.when(first)
            def _():
                if peers:
                    _peer_barrier(peers)
                for s, a, o, c in zip(sides, s_in, s_out, s_scr):
                    s.start(a, o, c)

            steps = int(np.prod(grid))
            mid_step = (2 * steps) // 3
            if steps > 1 and any(s.mid is not None for s in sides):
                step = functools.reduce(lambda acc, d: acc * grid[d] + pl.program_id(d), range(len(grid)), 0)

                @pl.when(step == mid_step)
                def _():
                    for s, a, o, c in zip(sides, s_in, s_out, s_scr):
                        if s.mid is not None:
                            s.mid(a, o, c)

        body(*m_in, *m_out, *m_scr)
        if sides:
            @pl.when(last)
            def _():
                for s, a, o, c in zip(sides, s_in, s_out, s_scr):
                    if s.mid is not None and steps == 1:
                        s.mid(a, o, c)
                if tail is not None:
                    tail(*m_in, *m_out, *m_scr)
                for s, a, o, c in zip(sides, s_in, s_out, s_scr):
                    s.finish(a, o, c)
        elif tail is not None:
            tail(*m_in, *m_out, *m_scr)

    res = pl.pallas_call(
        full, name=name, grid=grid,
        in_specs=list(in_specs) + [sp for s in sides for sp in s.in_specs],
        out_specs=list(out_specs) + [ANY for s in sides for _ in s.out_shape],
        out_shape=list(out_shape) + [o for s in sides for o in s.out_shape],
        scratch_shapes=list(scratch_shapes) + [c for s in sides for c in s.scratch],
        compiler_params=_cp(dimension_semantics=("arbitrary",) * len(grid),
                            **({"collective_id": BARRIER_IDS[peers]} if peers else {})),
    )(*args, *[a for s in sides for a in s.args])
    res = list(res)
    if not sides:
        return res
    outs, pos = take(res, 0, no)
    side_outs = []
    for _, o, _ in cnt:
        r, pos = take(res, pos, o)
        side_outs.append(r)
    return outs, side_outs


def _inv_freq_lanes():
    inv = np.float32(ROPE_THETA) ** (-np.arange(0, ROT_DIM, 2, dtype=np.float32) / np.float32(ROT_DIM))
    lane = np.arange(LANES) % HD
    out = np.where(lane < ROT_DIM, inv[lane % (ROT_DIM // 2)], 0.0).astype(np.float32)
    return jnp.asarray(out.reshape(1, LANES))


def _rope_tables(pos, inv_freq):
    ang = pos.astype(F32) * inv_freq
    lane = lax.broadcasted_iota(jnp.int32, ang.shape, 1) % HD
    cs = jnp.cos(ang)
    sn = jnp.sin(ang)
    return (jnp.where(lane < ROT_DIM, cs, 1.0), jnp.where(lane < ROT_DIM // 2, -sn, 0.0),
            jnp.where(lane < ROT_DIM // 2, 0.0, jnp.where(lane < ROT_DIM, sn, 0.0)))


def _rope(v, c, s1, s2):
    return v * c + pltpu.roll(v, LANES - 8, axis=1) * s1 + pltpu.roll(v, 8, axis=1) * s2


def _rope_t(d, c, s1, s2):
    return d * c - pltpu.roll(d, LANES - 8, axis=1) * s1 - pltpu.roll(d, 8, axis=1) * s2


def _head_mat():
    r = lax.broadcasted_iota(jnp.int32, (LANES, LANES), 0) // HD
    c = lax.broadcasted_iota(jnp.int32, (LANES, LANES), 1) // HD
    return jnp.where(r == c, 1.0 / HD, 0.0).astype(BF16)


def _head_mean(t, e):
    hi = t.astype(BF16)
    rest = (t - hi.astype(F32)).astype(BF16)
    return _dot(hi, e) + _dot(rest, e)


def in_proj_gather(x, norm_w, shard_t, chip_order, pos_col):
    R = INW // NDEV
    tm = IN_PROJ_TM
    half, nt = R // 2, S // tm

    def body(ord_ref, x_ref, nw_ref, sh_ref, pos_ref, f_ref, h_ref, p_ref, wfull_ref, c_ref, s1_ref, s2_ref,
             wt, hs, send, recv, loc):
        kk, i = pl.program_id(0), pl.program_id(1)
        x, y, c, _ = _place()
        me, flip = 4 * x + 2 * y + c, 1 - 2 * c
        here, sib, xn, yn = (x, y, c), (x, y, 1 - c), (1 - x, y, c), (x, 1 - y, c)
        b_xn, b_yn, b_dg = 4 * (1 - x) + 2 * y + c, 4 * x + 2 * (1 - y) + c, 4 * (1 - x) + 2 * (1 - y) + c

        def cp(k, block, to, rows=None):
            dst = wt.at[block] if rows is None else wt.at[block, pl.ds(rows * half, half), :]
            return _remote(dst, dst, send, recv, k, to)

        def sends():
            return [cp(0, me, sib), cp(1, me, xn), cp(2, me, yn), cp(3, b_xn, sib), cp(4, b_yn, sib),
                    cp(5, b_xn, yn, rows=0), cp(6, b_yn, xn, rows=1), cp(7, b_dg, sib, rows=0), cp(8, b_dg, sib, rows=1)]

        def keep(j, blk0):
            pair = pl.ds(pl.multiple_of(blk0, 2), 2)
            return pltpu.make_async_copy(wt.at[pair], wfull_ref.at[pair], loc.at[j])

        @pl.when((kk == 0) & (i == 0))
        def _():
            _peer_barrier("sxy")
            _cast_rows(wt.at[me], sh_ref)
            for s_ in sends()[0:3]:
                s_.start()

            def tables(j, _):
                posf = pos_ref[pl.ds(j, 1), :].astype(F32)
                eye = (lax.broadcasted_iota(jnp.int32, (LANES, LANES), 0)
                       == lax.broadcasted_iota(jnp.int32, (LANES, LANES), 1))
                col = jnp.sum(jnp.where(eye, posf, 0.0), axis=1, keepdims=True)
                chunk = pl.ds(pl.multiple_of(j * LANES, LANES), LANES)
                c_ref[chunk, :], s1_ref[chunk, :], s2_ref[chunk, :] = _rope_tables(col, f_ref[...])
                return 0

            lax.fori_loop(0, S // LANES, tables, 0)
            cp(0, me + flip, here).wait_recv()
            keep(0, me - c).start()

        @pl.when((kk == 1) & (i == 0))
        def _():
            cp(1, b_xn, here).wait_recv()
            sends()[5].start()
            sends()[3].start()
            cp(2, b_yn, here).wait_recv()
            sends()[6].start()
            sends()[4].start()
            cp(3, b_xn + flip, here).wait_recv()
            keep(1, b_xn - c).start()

        @pl.when((kk == 2) & (i == 0))
        def _():
            cp(4, b_yn + flip, here).wait_recv()
            keep(2, b_yn - c).start()

        @pl.when((kk == 3) & (i == 0))
        def _():
            cp(5, b_dg, here, rows=0).wait_recv()
            sends()[7].start()
            cp(6, b_dg, here, rows=1).wait_recv()
            sends()[8].start()
            cp(7, b_dg + flip, here, rows=0).wait_recv()
            cp(8, b_dg + flip, here, rows=1).wait_recv()
            keep(3, b_dg - c).start()

        rows = pl.ds(pl.multiple_of(i * tm, tm), tm)

        @pl.when(kk == 0)
        def _():
            xv = x_ref[...]
            r = lax.rsqrt(jnp.mean(xv * xv, axis=-1, keepdims=True) + EPS)
            hb = (xv * r * nw_ref[...]).astype(BF16)
            h_ref[...] = hb
            hs[rows, :] = hb

        h = hs[rows, :]
        chip = ord_ref[kk]
        for cc in range(2):
            p_ref[:, cc * R:(cc + 1) * R] = _dot_nt(h, wt[2 * chip + cc])

        @pl.when((kk == 3) & (i == nt - 1))
        def _():
            for s_ in sends():
                s_.wait_send()
            for j, blk in enumerate((me, b_xn, b_yn, b_dg)):
                keep(j, blk - c).wait()

    def first_pass(kk, i):
        return jnp.where(kk == 0, i, nt - 1)

    grid_spec = pltpu.PrefetchScalarGridSpec(
        num_scalar_prefetch=1, grid=(4, nt),
        in_specs=[pl.BlockSpec((tm, D), lambda kk, i, o: (first_pass(kk, i), 0)),
                  pl.BlockSpec((1, D), lambda kk, i, o: (0, 0)), VMEM, VMEM,
                  pl.BlockSpec((1, LANES), lambda kk, i, o: (0, 0))],
        out_specs=[pl.BlockSpec((tm, D), lambda kk, i, o: (first_pass(kk, i), 0)),
                   pl.BlockSpec((tm, 2 * R), lambda kk, i, o: (i, o[kk])), ANY]
        + [pl.BlockSpec((S, LANES), lambda kk, i, o: (0, 0))] * 3,
        scratch_shapes=[pltpu.VMEM((NDEV, R, D), BF16), pltpu.VMEM((S, D), BF16), _sems(9), _sems(9), _sems(4)])
    res = pl.pallas_call(
        body, name="in_proj_gather", grid_spec=grid_spec,
        out_shape=[jax.ShapeDtypeStruct((S, D), BF16), jax.ShapeDtypeStruct((S, INW), F32),
                   jax.ShapeDtypeStruct((NDEV, R, D), BF16)] + [jax.ShapeDtypeStruct((S, LANES), F32)] * 3,
        compiler_params=_cp(dimension_semantics=("arbitrary", "arbitrary"), collective_id=BARRIER_IDS["sxy"]),
    )(chip_order, x, norm_w, shard_t, pos_col, _inv_freq_lanes())
    return res[0], res[1], res[2], tuple(res[3:])


def _qk_specs():
    nb = QKV // LANES
    return [pl.BlockSpec((S, LANES), functools.partial(lambda hp, g, o: (0, o + g * 4 + hp), o=o))
            for o in (OFF_Q // LANES, OFF_K // LANES, OFF_V // LANES)]


def _tab_specs():
    return [pl.BlockSpec((S, LANES), lambda hp, g: (0, 0), pipeline_mode=pl.Buffered(1))] * 3


def _vec_spec():
    return pl.BlockSpec((1, LANES), lambda hp, g: (0, 0))


def _sub_rows(r, d, start, n):
    if d == 1:
        return pl.ds(start, n)
    return pl.ds(r + d * start, n, stride=d)


def _band_window(i, L):
    W = min(TQ + 2 * HALF_SPAN, L)
    q0 = pl.multiple_of(i * TQ, TQ)
    k0 = pl.multiple_of(jnp.clip(q0 - HALF_SPAN, 0, L - W), HALF_SPAN)
    qpos = q0 + (lax.broadcasted_iota(jnp.int32, (2 * TQ, W), 0) & (TQ - 1))
    kpos = k0 + lax.broadcasted_iota(jnp.int32, (2 * TQ, W), 1)
    valid = jnp.abs(qpos - kpos) <= HALF_SPAN
    return W, q0, k0, valid


def _stack_heads(t, lo):
    z = jnp.zeros_like(t)
    return jnp.concatenate([jnp.where(lo, t, z), jnp.where(lo, z, t)], axis=0)


def _unstack_heads(t2, lo):
    return jnp.where(lo, t2[0:TQ], t2[TQ:2 * TQ])


CHAINS = 8


def _interleave(d):
    ru = min(d, CHAINS)
    return ru, min(CHAINS // ru, S // d // TQ)


def _for_blocks(n, fn):
    if n == 1:
        fn(0)
    else:
        def it(j, _):
            fn(j)
            return 0
        lax.fori_loop(0, n, it, 0)


def attn_fwd(proj, tabs, qw2, kw2, sides=()):
    CH = 256

    def body(q_ref, k_ref, v_ref, c_ref, s1_ref, s2_ref, qw_ref, kw_ref, at_ref, ls_ref,
             qs, ks, vs, osub, lsub, onat, lnat, qn, kn):
        g = pl.program_id(1)
        lo = lax.broadcasted_iota(jnp.int32, (1, LANES), 1) < HD
        e = _head_mat()

        def prep(i, _):
            rows = pl.ds(pl.multiple_of(i * CH, CH), CH)
            c, s1, s2 = c_ref[rows, :], s1_ref[rows, :], s2_ref[rows, :]
            for t_ref, w_ref, out, scale in ((q_ref, qw_ref, qn, HD ** -0.5), (k_ref, kw_ref, kn, 1.0)):
                t = t_ref[rows, :]
                r = lax.rsqrt(_head_mean(t * t, e) + EPS)
                out[rows, :] = _rope(t * r * w_ref[...], c, s1, s2) * scale
            return 0

        lax.fori_loop(0, S // CH, prep, 0, unroll=4)

        def group(gi, d):
            L = S // d

            ru, nb = _interleave(d)

            def stage(r, off):
                for c0 in range(0, L, CH):
                    n = min(CH, L)
                    rows = _sub_rows(r, d, c0, n)
                    dst = pl.ds(off + c0, n)
                    qs[dst, :] = qn[rows, :].astype(BF16)
                    ks[dst, :] = kn[rows, :].astype(BF16)
                    vs[dst, :] = v_ref[rows, :].astype(BF16)

            def one(off, i):
                W, q0, k0, valid = _band_window(i, L)
                q2 = _stack_heads(qs[pl.ds(off + q0, TQ), :], lo)
                sc = jnp.where(valid, _dot_nt(q2, ks[pl.ds(off + k0, W), :]), NEG_INF)
                m = jnp.max(sc, axis=-1, keepdims=True)
                p = jnp.exp(sc - m)
                den = jnp.sum(p, axis=-1, keepdims=True)
                o2 = _dot(p.astype(BF16), vs[pl.ds(off + k0, W), :]) / den
                l2 = jnp.broadcast_to(m + jnp.log(den), (2 * TQ, LANES))
                osub[pl.ds(off + q0, TQ), :] = _unstack_heads(o2, lo)
                lsub[pl.ds(off + q0, TQ), :] = _unstack_heads(l2, lo)

            def unstage(r, off):
                for c0 in range(0, L, CH):
                    n = min(CH, L)
                    rows = _sub_rows(r, d, c0, n)
                    onat[gi, rows, :] = osub[pl.ds(off + c0, n), :]
                    lnat[gi, rows, :] = lsub[pl.ds(off + c0, n), :]

            def step(t, _):
                for u in range(ru):
                    stage(t * ru + u, u * L)
                _for_blocks(L // TQ // nb, lambda j: [one(u * L, j * nb + b) for u in range(ru) for b in range(nb)])
                for u in range(ru):
                    unstage(t * ru + u, u * L)
                return 0

            lax.fori_loop(0, d // ru, step, 0)

        for gi, d in enumerate(DILATIONS):
            pl.when(g == gi)(functools.partial(group, gi, d))

        @pl.when(g == len(DILATIONS) - 1)
        def _():
            def mix(i, _):
                rows = pl.ds(pl.multiple_of(i * CH, CH), CH)
                l0, l1, l2 = lnat[0, rows, :], lnat[1, rows, :], lnat[2, rows, :]
                m = jnp.maximum(jnp.maximum(l0, l1), l2)
                e0, e1, e2 = jnp.exp(l0 - m), jnp.exp(l1 - m), jnp.exp(l2 - m)
                den = e0 + e1 + e2
                a = (e0 * onat[0, rows, :] + e1 * onat[1, rows, :] + e2 * onat[2, rows, :]) / den
                at_ref[rows, :] = a.astype(BF16)
                ls_ref[rows, :] = m + jnp.log(den)
                return 0

            lax.fori_loop(0, S // CH, mix, 0)

    out_spec = pl.BlockSpec((S, LANES), lambda hp, g: (0, hp))
    return _call(
        body, sides, name="attn_fwd", grid=(4, 3),
        in_specs=_qk_specs() + _tab_specs() + [_vec_spec(), _vec_spec()],
        out_specs=[out_spec, out_spec],
        out_shape=[jax.ShapeDtypeStruct((S, CC), BF16), jax.ShapeDtypeStruct((S, CC), F32)],
        scratch_shapes=[pltpu.VMEM((S, LANES), BF16)] * 3 + [pltpu.VMEM((S, LANES), F32)] * 2
        + [pltpu.VMEM((3, S, LANES), F32)] * 2 + [pltpu.VMEM((S, LANES), F32)] * 2,
        args=(proj, proj, proj, *tabs, qw2, kw2))


def attn_bwd(proj, tabs, qw2, kw2, d_attn, attn, lse, sides=()):
    CH = 256

    def body(q_ref, k_ref, v_ref, c_ref, s1_ref, s2_ref, qw_ref, kw_ref, do_ref, at_ref, ls_ref,
             dq_ref, dk_ref, dv_ref, gqw_ref, gkw_ref,
             qs, ks, vs, dos, dsub, lsub, dqs, dks, dvs, dnat, qx, kx, dvn, tnq, tnk, rrq, rrk):
        hp, g = pl.program_id(0), pl.program_id(1)
        lo = lax.broadcasted_iota(jnp.int32, (1, LANES), 1) < HD
        e = _head_mat()
        both = ((q_ref, qw_ref, qx, tnq, rrq, HD ** -0.5), (k_ref, kw_ref, kx, tnk, rrk, 1.0))

        @pl.when((hp == 0) & (g == 0))
        def _():
            gqw_ref[...] = jnp.zeros_like(gqw_ref)
            gkw_ref[...] = jnp.zeros_like(gkw_ref)

        def prep(i, _):
            rows = pl.ds(pl.multiple_of(i * CH, CH), CH)
            dnat[rows, :] = _head_mean(do_ref[rows, :] * at_ref[rows, :].astype(F32), e) * float(HD)
            c, s1, s2 = c_ref[rows, :], s1_ref[rows, :], s2_ref[rows, :]
            for t_ref, w_ref, x, tn_s, rr_s, scale in both:
                t = t_ref[rows, :]
                rr = lax.rsqrt(_head_mean(t * t, e) + EPS)
                tn = t * rr
                rr_s[rows, :] = rr
                tn_s[rows, :] = tn
                x[rows, :] = _rope(tn * w_ref[...], c, s1, s2) * scale
            return 0

        lax.fori_loop(0, S // CH, prep, 0, unroll=4)

        def group(d):
            L = S // d

            ru, nb = _interleave(d)

            def stage(r, off):
                for c0 in range(0, L, CH):
                    n = min(CH, L)
                    rows = _sub_rows(r, d, c0, n)
                    dst = pl.ds(off + c0, n)
                    qs[dst, :] = qx[rows, :].astype(BF16)
                    ks[dst, :] = kx[rows, :].astype(BF16)
                    vs[dst, :] = v_ref[rows, :].astype(BF16)
                    dos[dst, :] = do_ref[rows, :].astype(BF16)
                    dsub[dst, :] = dnat[rows, :]
                    lsub[dst, :] = ls_ref[rows, :]
                    dks[dst, :] = jnp.zeros((n, LANES), F32)
                    dvs[dst, :] = jnp.zeros((n, LANES), F32)

            def one(off, i):
                W, q0, k0, valid = _band_window(i, L)
                qrows, krows = pl.ds(off + q0, TQ), pl.ds(off + k0, W)
                q2 = _stack_heads(qs[qrows, :], lo)
                do2 = _stack_heads(dos[qrows, :], lo)
                kk, vv = ks[krows, :], vs[krows, :]
                lse_b, dd_b = lsub[qrows, :], dsub[qrows, :]
                lse2 = jnp.concatenate([lse_b[:, 0:1], lse_b[:, HD:HD + 1]], axis=0)
                dd2 = jnp.concatenate([dd_b[:, 0:1], dd_b[:, HD:HD + 1]], axis=0)
                sc = jnp.where(valid, _dot_nt(q2, kk), NEG_INF)
                p = jnp.exp(sc - lse2)
                ds = (p * (_dot_nt(do2, vv) - dd2)).astype(BF16)
                dqs[qrows, :] = _unstack_heads(_dot(ds, kk), lo)
                dks[krows, :] = dks[krows, :] + _dot_tn(ds, q2)
                dvs[krows, :] = dvs[krows, :] + _dot_tn(p.astype(BF16), do2)

            def unstage(r, off):
                for c0 in range(0, L, CH):
                    n = min(CH, L)
                    rows = _sub_rows(r, d, c0, n)
                    src = pl.ds(off + c0, n)
                    qx[rows, :] = dqs[src, :]
                    kx[rows, :] = dks[src, :]
                    dvn[rows, :] = dvs[src, :]

            def step(t, _):
                for u in range(ru):
                    stage(t * ru + u, u * L)
                _for_blocks(L // TQ // nb, lambda j: [one(u * L, j * nb + b) for u in range(ru) for b in range(nb)])
                for u in range(ru):
                    unstage(t * ru + u, u * L)
                return 0

            lax.fori_loop(0, d // ru, step, 0)

        for gi, d in enumerate(DILATIONS):
            pl.when(g == gi)(functools.partial(group, d))

        def emit(i, _):
            rows = pl.ds(pl.multiple_of(i * CH, CH), CH)
            c, s1, s2 = c_ref[rows, :], s1_ref[rows, :], s2_ref[rows, :]
            for (_, w_ref, x, tn_s, rr_s, scale), out, gw_ref in zip(both, (dq_ref, dk_ref), (gqw_ref, gkw_ref)):
                tn = tn_s[rows, :]
                dy = _rope_t(x[rows, :] * scale, c, s1, s2)
                gw_ref[0:1, :] = gw_ref[0:1, :] + jnp.sum(dy * tn, axis=0, keepdims=True)
                dtn = dy * w_ref[...]
                out[rows, :] = (rr_s[rows, :] * (dtn - tn * _head_mean(dtn * tn, e))).astype(BF16)
            dv_ref[rows, :] = dvn[rows, :].astype(BF16)
            return 0

        lax.fori_loop(0, S // CH, emit, 0, unroll=4)

    nat_spec = pl.BlockSpec((S, LANES), lambda hp, g: (0, hp))
    out_spec = pl.BlockSpec((None, S, LANES), lambda hp, g: (g, 0, hp))
    acc_spec = pl.BlockSpec((8, LANES), lambda hp, g: (0, 0))
    return _call(
        body, sides, name="attn_bwd", grid=(4, 3),
        in_specs=_qk_specs() + _tab_specs() + [_vec_spec(), _vec_spec(), nat_spec, nat_spec, nat_spec],
        out_specs=[out_spec] * 3 + [acc_spec] * 2,
        out_shape=[jax.ShapeDtypeStruct((QKV // PLANE, S, PLANE), BF16)] * 3 + [jax.ShapeDtypeStruct((8, LANES), F32)] * 2,
        scratch_shapes=[pltpu.VMEM((S, LANES), BF16)] * 4 + [pltpu.VMEM((S, LANES), F32)] * 13,
        args=(proj, proj, proj, *tabs, qw2, kw2, d_attn, attn, lse))


PADR = 16
CT = 128


def _conv_specs():
    return [pl.BlockSpec((S, CC), lambda i: (0, OFF_CA // CC)), pl.BlockSpec((S, CC), lambda i: (0, OFF_CB // CC))]


NCB = CC // LANES


def _pad_zero(pad):
    for cb in range(NCB):
        pad[cb, 0:PADR, :] = jnp.zeros((PADR, LANES), F32)
        pad[cb, PADR + S:PADR + S + PADR, :] = jnp.zeros((PADR, LANES), F32)


def _pad_store(pad, row0, n, val):
    for cb in range(NCB):
        pad[cb, pl.ds(pl.multiple_of(row0 + PADR, 8), n), :] = val[:, cb * LANES:(cb + 1) * LANES]


def _taps(pad_ref, cb, s0, weights):
    acc = jnp.zeros((CT, LANES), F32)
    for k in range(KW):
        acc = acc + weights[k] * pad_ref[cb, pl.ds(s0 + k + 1, CT), :]
    return acc


def conv_fwd(proj, conv_w, conv_b, ln_w, ln_b):
    def body(a_ref, b_ref, w_ref, cb_ref, lw_ref, lb_ref, c_ref, u3_ref, upad):
        _pad_zero(upad)

        def glu(i, _):
            rows = pl.ds(pl.multiple_of(i * TM, TM), TM)
            _pad_store(upad, i * TM, TM, a_ref[rows, :] * _sigmoid(b_ref[rows, :]))
            return 0

        lax.fori_loop(0, S // TM, glu, 0)

        def chunk(i, _):
            s0 = pl.multiple_of(i * CT, CT)
            for cb in range(CC // LANES):
                cols = slice(cb * LANES, (cb + 1) * LANES)
                w = [w_ref[k:k + 1, cols] for k in range(KW)]
                c_ref[pl.ds(s0, CT), cols] = _taps(upad, cb, s0, w) + cb_ref[:, cols]
            cv = c_ref[pl.ds(s0, CT), :]
            mu = jnp.mean(cv, axis=-1, keepdims=True)
            xc = cv - mu
            rstd = lax.rsqrt(jnp.mean(xc * xc, axis=-1, keepdims=True) + EPS)
            yl = xc * rstd * lw_ref[...] + lb_ref[...]
            u3_ref[pl.ds(s0, CT), :] = (yl * _sigmoid(yl)).astype(BF16)
            return 0

        lax.fori_loop(0, S // CT, chunk, 0)

    vec = pl.BlockSpec((1, CC), lambda i: (0, 0))
    full = pl.BlockSpec((S, CC), lambda i: (0, 0))
    return _call(
        body, name="conv_fwd", grid=(1,),
        in_specs=_conv_specs() + [pl.BlockSpec((KW, CC), lambda i: (0, 0)), vec, vec, vec],
        out_specs=[full, full],
        out_shape=[jax.ShapeDtypeStruct((S, CC), F32), jax.ShapeDtypeStruct((S, CC), BF16)],
        scratch_shapes=[pltpu.VMEM((NCB, S + 2 * PADR, LANES), F32)],
        args=(proj, proj, conv_w, conv_b, ln_w, ln_b))


def conv_bwd(proj, cpre, d_u3, conv_w, ln_w, ln_b, sides=()):
    def body(a_ref, b_ref, c_ref, du3_ref, w_ref, lw_ref, lb_ref,
             dc_ref, gw_ref, gcb_ref, glw_ref, glb_ref, upad, dpad):
        _pad_zero(upad)
        _pad_zero(dpad)
        gw_ref[...] = jnp.zeros_like(gw_ref)

        def ln_bwd(i, carry):
            gcb, glw, glb = carry
            rows = pl.ds(pl.multiple_of(i * TM, TM), TM)
            _pad_store(upad, i * TM, TM, a_ref[rows, :] * _sigmoid(b_ref[rows, :]))
            cv = c_ref[rows, :]
            mu = jnp.mean(cv, axis=-1, keepdims=True)
            xc = cv - mu
            rstd = lax.rsqrt(jnp.mean(xc * xc, axis=-1, keepdims=True) + EPS)
            xh = xc * rstd
            yl = xh * lw_ref[...] + lb_ref[...]
            dyl = du3_ref[rows, :] * _dsilu(yl, _sigmoid(yl))
            dxh = dyl * lw_ref[...]
            dcv = rstd * (dxh - jnp.mean(dxh, axis=-1, keepdims=True)
                          - xh * jnp.mean(dxh * xh, axis=-1, keepdims=True))
            _pad_store(dpad, i * TM, TM, dcv)
            return (gcb + jnp.sum(dcv, axis=0, keepdims=True),
                    glw + jnp.sum(dyl * xh, axis=0, keepdims=True),
                    glb + jnp.sum(dyl, axis=0, keepdims=True))

        z = jnp.zeros((1, CC), F32)
        gcb, glw, glb = lax.fori_loop(0, S // TM, ln_bwd, (z, z, z))
        gcb_ref[...] = gcb
        glw_ref[...] = glw
        glb_ref[...] = glb

        def chunk(i, _):
            s0 = pl.multiple_of(i * CT, CT)
            for cb in range(CC // LANES):
                cols = slice(cb * LANES, (cb + 1) * LANES)
                wr = [w_ref[KW - 1 - k:KW - k, cols] for k in range(KW)]
                du = _taps(dpad, cb, s0, wr)
                dcv = dpad[cb, pl.ds(s0 + PADR, CT), :]
                for k in range(KW):
                    gw_ref[k:k + 1, cols] = gw_ref[k:k + 1, cols] + jnp.sum(
                        upad[cb, pl.ds(s0 + k + 1, CT), :] * dcv, axis=0, keepdims=True)
                av = a_ref[pl.ds(s0, CT), cols]
                sb = _sigmoid(b_ref[pl.ds(s0, CT), cols])
                dc_ref[0, pl.ds(s0, CT), cols] = (du * sb).astype(BF16)
                dc_ref[1, pl.ds(s0, CT), cols] = (du * av * sb * (1.0 - sb)).astype(BF16)
            return 0

        lax.fori_loop(0, S // CT, chunk, 0)

    vec = pl.BlockSpec((1, CC), lambda i: (0, 0))
    full = pl.BlockSpec((S, CC), lambda i: (0, 0))
    wsp = pl.BlockSpec((KW, CC), lambda i: (0, 0))
    return _call(
        body, sides, name="conv_bwd", grid=(1,),
        in_specs=_conv_specs() + [full, full, wsp, vec, vec],
        out_specs=[pl.BlockSpec((2, S, CC), lambda i: (0, 0, 0)), wsp, vec, vec, vec],
        out_shape=[jax.ShapeDtypeStruct((2, S, CC), BF16), jax.ShapeDtypeStruct((KW, CC), F32)]
        + [jax.ShapeDtypeStruct((1, CC), F32)] * 3,
        scratch_shapes=[pltpu.VMEM((NCB, S + 2 * PADR, LANES), F32)] * 2,
        args=(proj, proj, cpre, d_u3, conv_w, ln_w, ln_b))


def _gate_specs():
    return [_row(CC, col=OFF_GA // CC + j) for j in range(4)]


def _gates(g_refs, bg_ref):
    ga = _sigmoid(jnp.concatenate([g_refs[0][...], g_refs[1][...]], axis=1) + bg_ref[0:1, :])
    gb = _sigmoid(jnp.concatenate([g_refs[2][...], g_refs[3][...]], axis=1) + bg_ref[1:2, :])
    return ga, gb


def mix_out(x, proj, b_gate, attn, u3, w_o, w_pw, w_out):
    def body(x_ref, g0, g1, g2, g3, bg_ref, at_ref, u3_ref, wo_ref, wp_ref, wout_ref,
             x1_ref, z_ref, ya_ref, yb_ref):
        ga, gb = _gates((g0, g1, g2, g3), bg_ref)
        ya = _dot_nt(at_ref[...], wo_ref[...])
        yb = _dot_nt(u3_ref[...], wp_ref[...])
        z = (ga * ya + gb * yb).astype(BF16)
        ya_ref[...] = ya.astype(BF16)
        yb_ref[...] = yb.astype(BF16)
        z_ref[...] = z
        x1_ref[...] = x_ref[...] + _dot(z, wout_ref[...])

    return pl.pallas_call(
        body, name="mix_out", grid=(S // TM,),
        in_specs=[_row(D)] + _gate_specs() + [_res((2, D)), _row(CC), _row(CC),
                                              _res((D, CC)), _res((D, CC)), _res((D, D))],
        out_specs=[_row(D)] * 4,
        out_shape=[jax.ShapeDtypeStruct((S, D), F32)] + [jax.ShapeDtypeStruct((S, D), BF16)] * 3,
        compiler_params=_cp(dimension_semantics=("arbitrary",)),
    )(x, proj, proj, proj, proj, b_gate, attn, u3, w_o, w_pw, w_out)


def out_bwd(d_x1b, proj, b_gate, ya, yb, w_o, w_pw, w_out, sides=()):
    def body(dx_ref, g0, g1, g2, g3, bg_ref, ya_ref, yb_ref, wo_ref, wp_ref, wout_ref,
             dya_ref, dyb_ref, dgl_ref, dat_ref, du3_ref, gbg_ref):
        @pl.when(pl.program_id(0) == 0)
        def _():
            gbg_ref[...] = jnp.zeros_like(gbg_ref)

        ga, gb = _gates((g0, g1, g2, g3), bg_ref)
        dz = _dot_nt(dx_ref[...], wout_ref[...])
        dya = (dz * ga).astype(BF16)
        dyb = (dz * gb).astype(BF16)
        dgla = dz * ya_ref[...].astype(F32) * ga * (1.0 - ga)
        dglb = dz * yb_ref[...].astype(F32) * gb * (1.0 - gb)
        dya_ref[...] = dya
        dyb_ref[...] = dyb
        for j in range(2):
            dgl_ref[j] = dgla[:, j * PLANE:(j + 1) * PLANE].astype(BF16)
            dgl_ref[2 + j] = dglb[:, j * PLANE:(j + 1) * PLANE].astype(BF16)
        gbg_ref[0:1, :] = gbg_ref[0:1, :] + jnp.sum(dgla, axis=0, keepdims=True)
        gbg_ref[1:2, :] = gbg_ref[1:2, :] + jnp.sum(dglb, axis=0, keepdims=True)
        dat_ref[...] = _dot(dya, wo_ref[...])
        du3_ref[...] = _dot(dyb, wp_ref[...])

    return _call(
        body, sides, name="out_bwd", grid=(S // TM,),
        in_specs=[_row(D)] + _gate_specs() + [_res((2, D)), _row(D), _row(D),
                                              _res((D, CC)), _res((D, CC)), _res((D, D))],
        out_specs=[_row(D), _row(D), _planes(2 * D), _row(CC), _row(CC), pl.BlockSpec((2, D), lambda i: (0, 0))],
        out_shape=[jax.ShapeDtypeStruct((S, D), BF16)] * 2 + [jax.ShapeDtypeStruct((2 * D // PLANE, S, PLANE), BF16)]
        + [jax.ShapeDtypeStruct((S, CC), F32)] * 2 + [jax.ShapeDtypeStruct((2, D), F32)],
        args=(d_x1b, proj, proj, proj, proj, b_gate, ya, yb, w_o, w_pw, w_out))


def ffn_in(x1, norm_w, w_ffn_in, sides=()):
    half = FF // 2

    def body(x_ref, nw_ref, w_ref, h_ref, gu_ref, f_ref):
        xv = x_ref[...]
        r = lax.rsqrt(jnp.mean(xv * xv, axis=-1, keepdims=True) + EPS)
        h = (xv * r * nw_ref[...]).astype(BF16)
        h_ref[...] = h
        for j in range(2):
            gt = _dot_nt(h, w_ref[j * half:(j + 1) * half, :])
            up = _dot_nt(h, w_ref[FF + j * half:FF + (j + 1) * half, :])
            gu_ref[:, j * half:(j + 1) * half] = gt.astype(BF16)
            gu_ref[:, FF + j * half:FF + (j + 1) * half] = up.astype(BF16)
            f_ref[:, j * half:(j + 1) * half] = (gt * _sigmoid(gt) * up).astype(BF16)

    return _call(
        body, sides, name="ffn_in", grid=(S // TM,),
        in_specs=[_row(D), _res((1, D)), _res((2 * FF, D))],
        out_specs=[_row(D), _row(2 * FF), _row(FF)],
        out_shape=[jax.ShapeDtypeStruct((S, D), BF16), jax.ShapeDtypeStruct((S, 2 * FF), BF16),
                   jax.ShapeDtypeStruct((S, FF), BF16)],
        args=(x1, norm_w, w_ffn_in))


def ffn_out_loss(x1, f, w_ffn_out, target):
    def body(x_ref, f_ref, w_ref, t_ref, dy_ref, dyb_ref, sq_ref):
        @pl.when(pl.program_id(0) == 0)
        def _():
            sq_ref[...] = jnp.zeros_like(sq_ref)

        diff = x_ref[...] + _dot(f_ref[...], w_ref[...]) - t_ref[...]
        dy = diff * (1.0 / D)
        dy_ref[...] = dy
        dyb_ref[...] = dy.astype(BF16)
        sq_ref[...] = sq_ref[...] + jnp.sum((diff * diff).reshape(TM // 8, 8, D), axis=0)

    return pl.pallas_call(
        body, name="ffn_out_loss", grid=(S // TM,),
        in_specs=[_row(D), _row(FF), _res((FF, D)), _row(D)],
        out_specs=[_row(D), _row(D), pl.BlockSpec((8, D), lambda i: (0, 0))],
        out_shape=[jax.ShapeDtypeStruct((S, D), F32), jax.ShapeDtypeStruct((S, D), BF16),
                   jax.ShapeDtypeStruct((8, D), F32)],
        compiler_params=_cp(dimension_semantics=("arbitrary",)),
    )(x1, f, w_ffn_out, target)


def _rms_bwd(xv, nw, dh):
    r = lax.rsqrt(jnp.mean(xv * xv, axis=-1, keepdims=True) + EPS)
    xn = xv * r
    dxn = dh * nw
    dx = r * (dxn - xn * jnp.mean(dxn * xn, axis=-1, keepdims=True))
    return dx, dh * xn


def ffn_bwd(dy, dyb, gu, x1, norm_w, w_ffn_in, w_ffn_out, sides=()):
    def body(dy_ref, dyb_ref, gu_ref, x_ref, nw_ref, wi_ref, wo_ref, dgu_ref, dx_ref, dxb_ref, gn_ref):
        @pl.when(pl.program_id(0) == 0)
        def _():
            gn_ref[...] = jnp.zeros_like(gn_ref)

        df = _dot_nt(dyb_ref[...], wo_ref[...])
        gt = gu_ref[:, 0:FF].astype(F32)
        up = gu_ref[:, FF:2 * FF].astype(F32)
        sg = _sigmoid(gt)
        dgt = (df * up * _dsilu(gt, sg)).astype(BF16)
        dup = (df * gt * sg).astype(BF16)
        dgu_ref[:, 0:FF] = dgt
        dgu_ref[:, FF:2 * FF] = dup
        dh = _dot(dgt, wi_ref[0:FF, :]) + _dot(dup, wi_ref[FF:2 * FF, :])
        dxn, gw = _rms_bwd(x_ref[...], nw_ref[...], dh)
        dx = dy_ref[...] + dxn
        dx_ref[...] = dx
        dxb_ref[...] = dx.astype(BF16)
        gn_ref[...] = gn_ref[...] + jnp.sum(gw, axis=0, keepdims=True)

    return _call(
        body, sides, name="ffn_bwd", grid=(S // TM,),
        in_specs=[_row(D), _row(D), _row(2 * FF), _row(D), _res((1, D)), _res((2 * FF, D)), _res((FF, D))],
        out_specs=[_row(2 * FF), _row(D), _row(D), pl.BlockSpec((1, D), lambda i: (0, 0))],
        out_shape=[jax.ShapeDtypeStruct((S, 2 * FF), BF16), jax.ShapeDtypeStruct((S, D), F32),
                   jax.ShapeDtypeStruct((S, D), BF16), jax.ShapeDtypeStruct((1, D), F32)],
        args=(dy, dyb, gu, x1, norm_w, w_ffn_in, w_ffn_out))


def in_bwd(d_q, d_k, d_v, d_conv, d_gl, w_in, x, d_x1, norm_w, sides=()):
    segs = ((OFF_Q, QKV), (OFF_K, QKV), (OFF_V, QKV), (OFF_CA, 2 * CC), (OFF_GA, 2 * D))

    def body(dq_ref, dk_ref, dv_ref, dc_ref, dg_ref, w_ref, x_ref, dx1_ref, nw_ref, gx_ref, gn_ref):
        @pl.when(pl.program_id(0) == 0)
        def _():
            gn_ref[...] = jnp.zeros_like(gn_ref)

        dh = jnp.zeros((TM, D), F32)
        for ref, (off, width) in zip((dq_ref, dk_ref, dv_ref, dc_ref, dg_ref), segs):
            for j in range(width // PLANE):
                dh = dh + _dot(ref[j], w_ref[off + j * PLANE:off + (j + 1) * PLANE, :])
        dxn, gw = _rms_bwd(x_ref[...], nw_ref[...], dh)
        gx_ref[...] = dx1_ref[...] + dxn
        gn_ref[...] = gn_ref[...] + jnp.sum(gw, axis=0, keepdims=True)

    return _call(
        body, sides, name="in_bwd", grid=(S // TM,),
        in_specs=[_planes(QKV)] * 3 + [_planes(2 * CC), _planes(2 * D), _res((INW, D)), _row(D), _row(D), _res((1, D))],
        out_specs=[_row(D), pl.BlockSpec((1, D), lambda i: (0, 0))],
        out_shape=[jax.ShapeDtypeStruct((S, D), F32), jax.ShapeDtypeStruct((1, D), F32)],
        args=(d_q, d_k, d_v, d_conv, d_gl, w_in, x, d_x1, norm_w))


def mm_tn(name, pairs, tm):
    n = len(pairs)
    M = pairs[0][0].shape[1]
    widths = [b.shape[1] for _, b in pairs]

    def body(*refs):
        for a_ref, b_ref, o_ref, ob_ref in zip(refs[0:2 * n:2], refs[1:2 * n:2], refs[2 * n::2], refs[2 * n + 1::2]):
            r = _dot_tn(a_ref[...], b_ref[...])
            o_ref[...] = r
            ob_ref[...] = r.astype(BF16)

    return _call(
        body, name=name, grid=(M // tm,),
        in_specs=[sp for N in widths for sp in (pl.BlockSpec((S, tm), lambda i: (0, i)), _res((S, N)))],
        out_specs=[pl.BlockSpec((tm, N), lambda i: (i, 0)) for N in widths for _ in range(2)],
        out_shape=[jax.ShapeDtypeStruct((M, N), dt) for N in widths for dt in (F32, BF16)],
        args=[t for pair in pairs for t in pair])


GW_IN_TN = PLANE
GW_IN_RING = 3
GW_IN_SPLIT = (768, 256)


def gw_in(name, h, d_segs, col0, hw, sides=()):
    tn = GW_IN_TN
    starts, t0 = [], 0
    for seg in d_segs:
        starts.append(t0)
        t0 += seg.shape[0]
    ntiles = [seg.shape[0] for seg in d_segs]

    steps, slots = INW // tn, GW_IN_RING

    def body(h_ref, *refs):
        a_refs, o_ref, ob_ref, ring, sem = refs[:-4], refs[-4], refs[-3], refs[-2], refs[-1]
        n = pl.program_id(0)

        def fetch(t):
            for a_ref, st, nt in zip(a_refs, starts, ntiles):
                @pl.when((t >= st) & (t < st + nt))
                def _(a_ref=a_ref, st=st):
                    pltpu.make_async_copy(a_ref.at[t - st], ring.at[t % slots], sem.at[t % slots]).start()

        @pl.when(n == 0)
        def _():
            for t in range(slots - 1):
                fetch(jnp.int32(t))

        @pl.when(n + slots - 1 < steps)
        def _():
            fetch(n + slots - 1)

        slot = n % slots
        pltpu.make_async_copy(a_refs[0].at[0], ring.at[slot], sem.at[slot]).wait()
        r = _dot_tn(ring[slot], h_ref[...])
        o_ref[...] = r
        ob_ref[...] = r.astype(BF16)

    res = _call(
        body, sides, name=name, grid=(steps,),
        in_specs=[pl.BlockSpec((S, hw), lambda n: (0, col0 // hw))] + [ANY] * len(d_segs),
        out_specs=[pl.BlockSpec((tn, hw), lambda n: (n, 0))] * 2,
        out_shape=[jax.ShapeDtypeStruct((INW, hw), F32), jax.ShapeDtypeStruct((INW, hw), BF16)],
        scratch_shapes=[pltpu.VMEM((slots, S, tn), BF16), _sems(slots)],
        args=(h, *d_segs))
    return (res[0], res[1]) if sides else (res, [])


def _place():
    x, y, c = lax.axis_index("x"), lax.axis_index("y"), lax.axis_index("c")
    chips = [(1 - x, y), (x, 1 - y), (1 - x, 1 - y)]
    return x, y, c, chips


def _sems(n):
    return pltpu.SemaphoreType.DMA((n,))


def _remote(src, dst, send, recv, k, to):
    return pltpu.make_async_remote_copy(src_ref=src, dst_ref=dst, send_sem=send.at[k], recv_sem=recv.at[k],
                                        device_id=to, device_id_type=MESH)


def _cast_rows(dst, src, cols=slice(None)):
    rows = src.shape[0]
    step = next((s for s in (128, 64, 32, 16) if rows % s == 0), rows)
    for r0 in range(0, rows, step):
        dst[r0:r0 + step, cols] = src[r0:r0 + step, :].astype(dst.dtype)


def comm_only(name, sides):
    def body():
        pass

    return _call(body, sides, name=name, grid=(1,), in_specs=[], out_specs=[], out_shape=[], args=())[1]


def ag_blocks(shard, dtype):
    R, W = shard.shape

    def copy(outs, scr, k, block, to, src=None):
        dst = outs[0].at[block]
        return _remote(dst if src is None else src, dst, scr[1], scr[2], k, to)

    def local(outs, scr, me):
        return pltpu.make_async_copy(scr[0], outs[0].at[me], scr[3].at[0])

    def start(ins, outs, scr):
        x, y, c, chips = _place()
        me = 4 * x + 2 * y + c
        _cast_rows(scr[0], ins[0])
        local(outs, scr, me).start()
        copy(outs, scr, 0, me, (x, y, 1 - c), src=scr[0]).start()
        for j, (cx, cy) in enumerate(chips):
            copy(outs, scr, 1 + j, me, (cx, cy, c), src=scr[0]).start()

    def finish(ins, outs, scr):
        x, y, c, chips = _place()
        me, sib = 4 * x + 2 * y + c, (x, y, 1 - c)
        passed = []
        for j, (cx, cy) in enumerate(chips):
            theirs = 4 * cx + 2 * cy + c
            copy(outs, scr, 1 + j, theirs, (x, y, c)).wait_recv()
            fwd = copy(outs, scr, 4 + j, theirs, sib)
            fwd.start()
            passed.append(fwd)
        copy(outs, scr, 0, 4 * x + 2 * y + 1 - c, (x, y, c)).wait_recv()
        for j, (cx, cy) in enumerate(chips):
            copy(outs, scr, 4 + j, 4 * cx + 2 * cy + 1 - c, (x, y, c)).wait_recv()
        copy(outs, scr, 0, me, sib, src=scr[0]).wait_send()
        for j, (cx, cy) in enumerate(chips):
            copy(outs, scr, 1 + j, me, (cx, cy, c), src=scr[0]).wait_send()
        for fwd in passed:
            fwd.wait_send()
        local(outs, scr, me).wait()

    return Side((shard,), (VMEM,), (jax.ShapeDtypeStruct((NDEV, R, W), dtype),),
                (pltpu.VMEM((R, W), dtype), _sems(7), _sems(7), _sems(1)), start, finish, None, "dsxy")


def ag_blocks_relay(shard, dtype, transpose=False):
    R, W = shard.shape[::-1] if transpose else shard.shape
    half = R // 2

    def copy(outs, scr, k, block, to, src=None, rows=None):
        dst = outs[0].at[block] if rows is None else outs[0].at[block, pl.ds(rows * half, half), :]
        return _remote(dst if src is None else src, dst, scr[1], scr[2], k, to)

    def local(outs, scr, me):
        return pltpu.make_async_copy(scr[0], outs[0].at[me], scr[3].at[0])

    def own(outs, scr):
        x, y, c, _ = _place()
        me = 4 * x + 2 * y + c
        return [copy(outs, scr, k, me, to, src=scr[0])
                for k, to in enumerate([(x, y, 1 - c), (1 - x, y, c), (x, 1 - y, c)])]

    def start(ins, outs, scr):
        x, y, c, _ = _place()
        if transpose:
            scr[0][...] = ins[0][...].T.astype(dtype)
        else:
            _cast_rows(scr[0], ins[0])
        local(outs, scr, 4 * x + 2 * y + c).start()
        for cp in own(outs, scr):
            cp.start()

    def passed_on(outs, scr):
        x, y, c, _ = _place()
        sib, xn, yn = (x, y, 1 - c), (1 - x, y, c), (x, 1 - y, c)
        b_xn, b_yn, b_dg = 4 * (1 - x) + 2 * y + c, 4 * x + 2 * (1 - y) + c, 4 * (1 - x) + 2 * (1 - y) + c
        near = [copy(outs, scr, 5, b_xn, yn, rows=0), copy(outs, scr, 3, b_xn, sib),
                copy(outs, scr, 6, b_yn, xn, rows=1), copy(outs, scr, 4, b_yn, sib)]
        far = [copy(outs, scr, 7, b_dg, sib, rows=0), copy(outs, scr, 8, b_dg, sib, rows=1)]
        return (b_xn, b_yn, b_dg), near, far

    def mid(ins, outs, scr):
        x, y, c, _ = _place()
        (b_xn, b_yn, _), near, _ = passed_on(outs, scr)
        copy(outs, scr, 1, b_xn, (x, y, c)).wait_recv()
        near[0].start()
        near[1].start()
        copy(outs, scr, 2, b_yn, (x, y, c)).wait_recv()
        near[2].start()
        near[3].start()

    def finish(ins, outs, scr):
        x, y, c, _ = _place()
        here = (x, y, c)
        (b_xn, b_yn, b_dg), near, far = passed_on(outs, scr)
        copy(outs, scr, 5, b_dg, here, rows=0).wait_recv()
        far[0].start()
        copy(outs, scr, 6, b_dg, here, rows=1).wait_recv()
        far[1].start()
        flip = 1 - 2 * c
        copy(outs, scr, 0, 4 * x + 2 * y + 1 - c, here).wait_recv()
        copy(outs, scr, 3, b_xn + flip, here).wait_recv()
        copy(outs, scr, 4, b_yn + flip, here).wait_recv()
        copy(outs, scr, 7, b_dg + flip, here, rows=0).wait_recv()
        copy(outs, scr, 8, b_dg + flip, here, rows=1).wait_recv()
        for cp in own(outs, scr) + near + far:
            cp.wait_send()
        local(outs, scr, 4 * x + 2 * y + c).wait()

    return Side((shard,), (VMEM,), (jax.ShapeDtypeStruct((NDEV, R, W), dtype),),
                (pltpu.VMEM((R, W), dtype), _sems(9), _sems(9), _sems(1)), start, finish, mid, "sxy")


def copies_side(args, out_shape, n_copies, plan, peers):
    def copies(ins, outs, scr):
        return [_remote(s_, d_, scr[0], scr[1], i, to) for i, (s_, d_, to) in enumerate(plan(ins, outs))]

    def start(ins, outs, scr):
        for cp in copies(ins, outs, scr):
            cp.start()

    def finish(ins, outs, scr):
        for cp in copies(ins, outs, scr):
            cp.wait()

    return Side(tuple(args), (ANY,) * len(args), tuple(out_shape), (_sems(n_copies), _sems(n_copies)),
                start, finish, None, peers)


def rs_to_sibling(grads):
    out_shape = [jax.ShapeDtypeStruct((4,) + g.shape[1:], BF16) for g in grads]

    def plan(ins, outs):
        x, y, c, _ = _place()
        return [(g.at[2 * k + 1 - c], r.at[k], (x, y, 1 - c)) for g, r in zip(ins, outs) for k in range(4)]

    return copies_side(grads, out_shape, 4 * len(grads), plan, "s")


def rs_to_chips(parts):
    out_shape = [jax.ShapeDtypeStruct((3,) + p.shape[1:], BF16) for p in parts]

    def plan(ins, outs):
        x, y, c, chips = _place()
        return [(p.at[2 * cx + cy], r.at[j], (cx, cy, c))
                for p, r in zip(ins, outs) for j, (cx, cy) in enumerate(chips)]

    return copies_side(parts, out_shape, 3 * len(parts), plan, "dxy")


def rs_to_chips_combined(part):
    _, R, W = part.shape
    half = R // 2
    top, bot = pl.ds(0, half), pl.ds(half, half)

    def copies(ins, outs, scr):
        p, r = ins[0], outs[0]
        loc_a, loc_b, in_x, in_y, comb_a, comb_b, send, recv, loc = scr
        x, y, c, _ = _place()
        xn, yn = (1 - x, y, c), (x, 1 - y, c)
        k_xn, k_yn, k_dg = 2 * (1 - x) + y, 2 * x + 1 - y, 2 * (1 - x) + 1 - y
        direct = [_remote(p.at[k_xn, top, :], r.at[0, top, :], send, recv, 0, xn),
                  _remote(p.at[k_yn, bot, :], r.at[1, bot, :], send, recv, 1, yn),
                  _remote(p.at[k_dg, top, :], in_x, send, recv, 2, xn),
                  _remote(p.at[k_dg, bot, :], in_y, send, recv, 3, yn)]
        combined = [_remote(comb_a, r.at[1, top, :], send, recv, 4, yn),
                    _remote(comb_b, r.at[0, bot, :], send, recv, 5, xn)]
        local = [pltpu.make_async_copy(p.at[k_yn, top, :], loc_a, loc.at[0]),
                 pltpu.make_async_copy(p.at[k_xn, bot, :], loc_b, loc.at[1])]
        return direct, combined, local

    def start(ins, outs, scr):
        direct, _, local = copies(ins, outs, scr)
        for cp in local + direct:
            cp.start()

    def mid(ins, outs, scr):
        loc_a, loc_b, in_x, in_y, comb_a, comb_b = scr[:6]
        direct, combined, local = copies(ins, outs, scr)
        for mine, arrival, inbox, out, nxt in ((local[0], direct[2], in_x, comb_a, combined[0]),
                                               (local[1], direct[3], in_y, comb_b, combined[1])):
            mine.wait()
            arrival.wait_recv()
            src = loc_a if out is comb_a else loc_b
            out[...] = (src[...].astype(F32) + inbox[...].astype(F32)).astype(BF16)
            nxt.start()

    def finish(ins, outs, scr):
        direct, combined, _ = copies(ins, outs, scr)
        direct[0].wait_recv()
        direct[1].wait_recv()
        combined[0].wait_recv()
        combined[1].wait_recv()
        for cp in direct + combined:
            cp.wait_send()

    buf = pltpu.VMEM((half, W), BF16)
    return Side((part,), (ANY,), (jax.ShapeDtypeStruct((2, R, W), BF16),),
                (buf, buf, buf, buf, buf, buf, _sems(6), _sems(6), _sems(2)), start, finish, mid, "xy")


ADAM_TILE_BYTES = 3 * 512 * 1024


def _row_tiles(rows, width):
    return 2 if rows % 32 == 0 and rows * width * 4 > ADAM_TILE_BYTES else 1


def chip_sum(name, grads, recvs, c_idx, chip_idx):
    n = len(grads)

    def body(s_ref, *refs):
        k = pl.program_id(0)
        for g_ref, r_ref, p_ref, own_ref in zip(refs[:n], refs[n:2 * n], refs[2 * n::2], refs[2 * n + 1::2]):
            tot = g_ref[0] + r_ref[0].astype(F32)
            p_ref[0] = tot.astype(BF16)

            @pl.when(k == s_ref[1])
            def _(own_ref=own_ref, tot=tot):
                own_ref[...] = tot

    def block(g):
        return (1,) + g.shape[1:]

    grid_spec = pltpu.PrefetchScalarGridSpec(
        num_scalar_prefetch=1, grid=(4,),
        in_specs=[pl.BlockSpec(block(g), lambda k, s: (2 * k + s[0], 0, 0)) for g in grads]
        + [pl.BlockSpec(block(g), lambda k, s: (k, 0, 0)) for g in grads],
        out_specs=[sp for g in grads for sp in (pl.BlockSpec(block(g), lambda k, s: (k, 0, 0)),
                                                pl.BlockSpec(g.shape[1:], lambda k, s: (0, 0)))])
    res = pl.pallas_call(
        body, name=name, grid_spec=grid_spec,
        out_shape=[sh for g in grads for sh in (jax.ShapeDtypeStruct((4,) + g.shape[1:], BF16),
                                                jax.ShapeDtypeStruct(g.shape[1:], F32))],
        compiler_params=_cp(dimension_semantics=("arbitrary",)),
    )(jnp.stack([c_idx, chip_idx]), *grads, *recvs)
    return [(res[2 * j], res[2 * j + 1]) for j in range(n)]


def _adamw(w, g, m, v):
    m2 = ADAM_B1 * m + (1.0 - ADAM_B1) * g
    v2 = ADAM_B2 * v + (1.0 - ADAM_B2) * (g * g)
    m_hat = m2 / (1.0 - ADAM_B1 ** ADAM_STEP)
    v_hat = v2 / (1.0 - ADAM_B2 ** ADAM_STEP)
    delta = -ADAM_LR * (m_hat / (jnp.sqrt(v_hat) + ADAM_EPS) + ADAM_WD * w)
    return delta, m2, v2


def shard_adam(name, owns, recvs, w, m, v):
    n = len(owns)
    R = owns[0].shape[0]
    ct = min(o.shape[1] for o in owns)
    first = [sum(o.shape[1] for o in owns[:j]) // ct for j in range(n)]
    count = [o.shape[1] // ct for o in owns]
    nt = _row_tiles(R, ct)
    tr = R // nt

    def body(*refs):
        o_refs, r_refs = refs[:n], refs[n:2 * n]
        w_ref, m_ref, v_ref, g_ref, d_ref, nm_ref, nv_ref = refs[2 * n:]
        g = None
        for j in range(n):
            gj = o_refs[j][...]
            for q in range(recvs[j].shape[0]):
                gj = gj + r_refs[j][q].astype(F32)
            g = gj if g is None else jnp.where(pl.program_id(0) >= first[j], gj, g)
        delta, m2, v2 = _adamw(w_ref[...], g, m_ref[...], v_ref[...])
        g_ref[...] = g
        d_ref[...] = delta
        nm_ref[...] = m2
        nv_ref[...] = v2

    def part(j):
        return pl.BlockSpec((tr, ct), lambda k, i: (i, jnp.clip(k - first[j], 0, count[j] - 1)))

    def part3(j):
        return pl.BlockSpec((recvs[j].shape[0], tr, ct), lambda k, i: (0, i, jnp.clip(k - first[j], 0, count[j] - 1)))

    C = sum(count) * ct
    tile = pl.BlockSpec((tr, ct), lambda k, i: (i, k))
    return pl.pallas_call(
        body, name=name, grid=(sum(count), nt),
        in_specs=[part(j) for j in range(n)] + [part3(j) for j in range(n)] + [tile, tile, tile],
        out_specs=[tile] * 4, out_shape=[jax.ShapeDtypeStruct((R, C), F32)] * 4,
        compiler_params=_cp(dimension_semantics=("arbitrary", "arbitrary")),
    )(*owns, *recvs, w, m, v)


def rows_adam(name, items, steps):
    n = len(items)

    def body(*refs):
        for j in range(n):
            o_ref, r_ref, w_ref, m_ref, v_ref = refs[5 * j:5 * j + 5]
            g = o_ref[...]
            for q in range(r_ref.shape[0]):
                g = g + r_ref[q].astype(F32)
            delta, m2, v2 = _adamw(w_ref[...], g, m_ref[...], v_ref[...])
            for ref, val in zip(refs[5 * n + 4 * j:5 * n + 4 * j + 4], (g, delta, m2, v2)):
                ref[...] = val

    def tile(a):
        return pl.BlockSpec((a.shape[0] // steps, a.shape[1]), lambda i: (i, 0))

    def tile3(a):
        return pl.BlockSpec((a.shape[0], a.shape[1] // steps, a.shape[2]), lambda i: (0, i, 0))

    res = pl.pallas_call(
        body, name=name, grid=(steps,),
        in_specs=[sp for own, recv, w, _, _ in items for sp in (tile(own), tile3(recv), tile(w), tile(w), tile(w))],
        out_specs=[tile(item[2]) for item in items for _ in range(4)],
        out_shape=[jax.ShapeDtypeStruct(item[2].shape, F32) for item in items for _ in range(4)],
        compiler_params=_cp(dimension_semantics=("arbitrary",)))(*[a for item in items for a in item])
    return [tuple(res[4 * j:4 * j + 4]) for j in range(n)]


def _block_adam(transposed, in_refs, out_refs):
    for j, io_t in enumerate(transposed):
        o_ref, r_ref, w_ref, m_ref, v_ref = in_refs[5 * j:5 * j + 5]
        g = o_ref[...]
        for q in range(r_ref.shape[0]):
            g = g + r_ref[q].astype(F32)
        t = (lambda a: a.T) if io_t else (lambda a: a)
        delta, m2, v2 = _adamw(t(w_ref[...]), g, t(m_ref[...]), t(v_ref[...]))
        for ref, val in zip(out_refs[4 * j:4 * j + 4], (g, delta, m2, v2)):
            ref[...] = t(val)


ROW_N1, ROW_N2, ROW_BG, ROW_QN, ROW_KN, ROW_CB, ROW_LW, ROW_LB, ROW_CW = 0, 1, 2, 4, 5, 6, 7, 8, 9
PACK_ROWS = 40
SMALL = ("norm1_w", "norm2_w", "b_gate", "q_norm_w", "k_norm_w", "conv_b", "conv_ln_w", "conv_ln_b", "conv_w")


def small_sync(g, sq, sides=()):
    ns = len(SMALL)

    def copies(refs):
        pack, recv, send_sems, recv_sems = refs[ns + 2:]
        x, y, c, _ = _place()
        return [pltpu.make_async_remote_copy(
            src_ref=pack, dst_ref=recv.at[4 * x + 2 * y + c], send_sem=send_sems.at[k - 1],
            recv_sem=recv_sems.at[k - 1], device_id=(x ^ (k >> 2), y ^ ((k >> 1) & 1), c ^ (k & 1)),
            device_id_type=MESH) for k in range(1, NDEV)]

    def body(*refs):
        gi = dict(zip(SMALL, refs[:ns]))
        sq_ref, tot, pack, recv, send_sems, recv_sems = refs[ns:]
        x, y, c, _ = _place()
        me = 4 * x + 2 * y + c

        pack[...] = jnp.zeros_like(pack)
        pack[ROW_KN:ROW_KN + 1, LANES:2 * LANES] = jnp.full((1, LANES), (0.5 / D) * jnp.sum(sq_ref[...]), F32)
        pack[ROW_N1:ROW_N1 + 1, :] = gi["norm1_w"][...]
        pack[ROW_N2:ROW_N2 + 1, :] = gi["norm2_w"][...]
        pack[ROW_BG:ROW_BG + 2, :] = gi["b_gate"][...]
        for row, name in ((ROW_QN, "q_norm_w"), (ROW_KN, "k_norm_w")):
            pack[row:row + 1, 0:HD] = gi[name][0:1, 0:HD] + gi[name][0:1, HD:LANES]
        pack[ROW_CB:ROW_CB + 1, 0:CC] = gi["conv_b"][...]
        pack[ROW_LW:ROW_LW + 1, 0:CC] = gi["conv_ln_w"][...]
        pack[ROW_LB:ROW_LB + 1, 0:CC] = gi["conv_ln_b"][...]
        pack[ROW_CW:ROW_CW + KW, 0:CC] = gi["conv_w"][...]

        for cp in copies(refs):
            cp.start()
        recv[me] = pack[...]

    def tail(*refs):
        tot, recv = refs[ns + 1], refs[ns + 3]
        for cp in copies(refs):
            cp.wait()
        acc = recv[0]
        for p in range(1, NDEV):
            acc = acc + recv[p]
        tot[...] = acc

    args = [g[k] for k in SMALL] + [sq]
    res = _call(
        body, sides, name="small_sync", grid=(1,), in_specs=[VMEM] * len(args), out_specs=[VMEM],
        out_shape=[jax.ShapeDtypeStruct((PACK_ROWS, D), F32)],
        scratch_shapes=[pltpu.VMEM((PACK_ROWS, D), F32), pltpu.VMEM((NDEV, PACK_ROWS, D), F32),
                        _sems(NDEV - 1), _sems(NDEV - 1)],
        args=args, own_comm=True, tail=tail)
    return (res[0][0], res[1]) if sides else res[0]


def small_adam(tot, w, m, v, me, blocks):
    ns, nb = len(SMALL), len(blocks)

    def body(me_ref, tot, *refs):
        wi = dict(zip(SMALL, refs[:ns]))
        mi = dict(zip(SMALL, refs[ns:2 * ns]))
        vi = dict(zip(SMALL, refs[2 * ns:3 * ns]))
        block_in, refs = refs[3 * ns:3 * ns + 5 * nb], refs[:3 * ns] + refs[3 * ns + 5 * nb:]
        outs = refs[3 * ns:7 * ns]
        loss_ref = refs[7 * ns]
        _block_adam([b[5] for b in blocks], block_in, refs[7 * ns + 1:])
        me = me_ref[0]

        def shard_grad(name):
            if name == "b_gate":
                return tot[ROW_BG:ROW_BG + 2, pl.ds(pl.multiple_of(me * LANES, LANES), LANES)]
            if name == "conv_w":
                win = tot[ROW_CW:ROW_CW + KW, pl.ds(pl.multiple_of((me // 2) * LANES, LANES), LANES)]
                return jnp.where(me % 2 == 1, win[:, HD:LANES], win[:, 0:HD])
            row = {"norm1_w": ROW_N1, "norm2_w": ROW_N2, "q_norm_w": ROW_QN, "k_norm_w": ROW_KN,
                   "conv_b": ROW_CB, "conv_ln_w": ROW_LW, "conv_ln_b": ROW_LB}[name]
            return tot[row:row + 1, 0:wi[name].shape[1]]

        for i, name in enumerate(SMALL):
            gr = shard_grad(name)
            delta, m2, v2 = _adamw(wi[name][...], gr, mi[name][...], vi[name][...])
            outs[4 * i][...] = gr
            outs[4 * i + 1][...] = delta
            outs[4 * i + 2][...] = m2
            outs[4 * i + 3][...] = v2
        loss_ref[...] = tot[ROW_KN:ROW_KN + 1, LANES:2 * LANES]

    out_shape = []
    for name in SMALL:
        out_shape += [jax.ShapeDtypeStruct(w[name].shape, F32)] * 4
    out_shape.append(jax.ShapeDtypeStruct((1, LANES), F32))
    out_shape += [jax.ShapeDtypeStruct(b[2].shape, F32) for b in blocks for _ in range(4)]
    args = ([tot] + [w[k] for k in SMALL] + [m[k] for k in SMALL] + [v[k] for k in SMALL]
            + [a for b in blocks for a in b[:5]])
    grid_spec = pltpu.PrefetchScalarGridSpec(
        num_scalar_prefetch=1, grid=(1,), in_specs=[VMEM] * len(args), out_specs=[VMEM] * len(out_shape))
    res = pl.pallas_call(body, name="small_adam", grid_spec=grid_spec, out_shape=out_shape,
                         compiler_params=_cp(dimension_semantics=("arbitrary",)))(me, *args)
    out = {name: tuple(res[4 * i:4 * i + 4]) for i, name in enumerate(SMALL)}
    return out, res[4 * ns][0, 0], [tuple(res[4 * ns + 1 + 4 * j:4 * ns + 5 + 4 * j]) for j in range(nb)]


MATS = ("w_in", "w_o_attn", "w_pw_conv", "w_out", "w_ffn_in", "w_ffn_out")
TRANSPOSED = ("w_in", "w_ffn_in")
WEIGHTS = ("norm1_w", "w_in", "b_gate", "q_norm_w", "k_norm_w", "w_o_attn", "conv_w", "conv_b", "conv_ln_w",
           "conv_ln_b", "w_pw_conv", "w_out", "norm2_w", "w_ffn_in", "w_ffn_out")


def _blocks_to_cols(blocks):
    n, R, C = blocks.shape
    return blocks.transpose(1, 0, 2).reshape(R, n * C)


def kernel(x, positions, norm1_w, w_in, b_gate, q_norm_w, k_norm_w, w_o_attn, conv_w, conv_b, conv_ln_w, conv_ln_b, w_pw_conv, w_out, norm2_w, w_ffn_in, w_ffn_out, loss_target, m_norm1_w, m_w_in, m_b_gate, m_q_norm_w, m_k_norm_w, m_w_o_attn, m_conv_w, m_conv_b, m_conv_ln_w, m_conv_ln_b, m_w_pw_conv, m_w_out, m_norm2_w, m_w_ffn_in, m_w_ffn_out, v_norm1_w, v_w_in, v_b_gate, v_q_norm_w, v_k_norm_w, v_w_o_attn, v_conv_w, v_conv_b, v_conv_ln_w, v_conv_ln_b, v_w_pw_conv, v_w_out, v_norm2_w, v_w_ffn_in, v_w_ffn_out):
    w = dict(norm1_w=norm1_w, w_in=w_in, b_gate=b_gate, q_norm_w=q_norm_w, k_norm_w=k_norm_w, w_o_attn=w_o_attn,
             conv_w=conv_w, conv_b=conv_b, conv_ln_w=conv_ln_w, conv_ln_b=conv_ln_b, w_pw_conv=w_pw_conv,
             w_out=w_out, norm2_w=norm2_w, w_ffn_in=w_ffn_in, w_ffn_out=w_ffn_out)
    m = dict(norm1_w=m_norm1_w, w_in=m_w_in, b_gate=m_b_gate, q_norm_w=m_q_norm_w, k_norm_w=m_k_norm_w,
             w_o_attn=m_w_o_attn, conv_w=m_conv_w, conv_b=m_conv_b, conv_ln_w=m_conv_ln_w,
             conv_ln_b=m_conv_ln_b, w_pw_conv=m_w_pw_conv, w_out=m_w_out, norm2_w=m_norm2_w,
             w_ffn_in=m_w_ffn_in, w_ffn_out=m_w_ffn_out)
    v = dict(norm1_w=v_norm1_w, w_in=v_w_in, b_gate=v_b_gate, q_norm_w=v_q_norm_w, k_norm_w=v_k_norm_w,
             w_o_attn=v_w_o_attn, conv_w=v_conv_w, conv_b=v_conv_b, conv_ln_w=v_conv_ln_w,
             conv_ln_b=v_conv_ln_b, w_pw_conv=v_w_pw_conv, w_out=v_w_out, norm2_w=v_norm2_w,
             w_ffn_in=v_w_ffn_in, w_ffn_out=v_w_ffn_out)
    def two_d(t):
        t = {k: (a[0] if a.ndim == 3 else a) for k, a in t.items()}
        return {k: (a.T if k in TRANSPOSED else a) for k, a in t.items()}

    w, m, v = two_d(w), two_d(m), two_d(v)

    x2, target = x[0], loss_target[0]
    c_idx = lax.axis_index("c").astype(jnp.int32)
    chip_idx = (2 * lax.axis_index("x") + lax.axis_index("y")).astype(jnp.int32)
    qw2 = jnp.tile(w["q_norm_w"], (1, 2))
    kw2 = jnp.tile(w["k_norm_w"], (1, 2))

    ax, ay = lax.axis_index("x"), lax.axis_index("y")
    chip_order = jnp.stack([2 * ax + ay, 2 * (1 - ax) + ay, 2 * ax + 1 - ay, 2 * (1 - ax) + 1 - ay]).astype(jnp.int32)
    h, proj, w_in_blocks, tabs = in_proj_gather(x2, w["norm1_w"], w["w_in"], chip_order, positions.reshape(S // LANES, LANES))
    w_in_t = w_in_blocks.reshape(INW, D)
    (attn, lse), ((w_ffn_in_blocks,), (w_out_blocks,), (w_o_blocks,), (w_pw_blocks,), (bg_blocks,), (cw_blocks,)) = attn_fwd(
        proj, tabs, qw2, kw2, sides=(ag_blocks_relay(w["w_ffn_in"], BF16), ag_blocks_relay(w["w_out"], BF16),
                                     ag_blocks_relay(w["w_o_attn"], BF16, transpose=True),
                                     ag_blocks_relay(w["w_pw_conv"], BF16, transpose=True),
                                     ag_blocks(w["b_gate"], F32), ag_blocks(w["conv_w"], F32)))
    w_ffn_in_t = w_ffn_in_blocks.reshape(2 * FF, D)
    w_out_f = w_out_blocks.reshape(D, D)
    w_o_t, w_pw_t = w_o_blocks.reshape(D, CC), w_pw_blocks.reshape(D, CC)
    b_gate_f, conv_w_f = _blocks_to_cols(bg_blocks), _blocks_to_cols(cw_blocks)
    cpre, u3 = conv_fwd(proj, conv_w_f, w["conv_b"], w["conv_ln_w"], w["conv_ln_b"])
    x1, z, ya, yb = mix_out(x2, proj, b_gate_f, attn, u3, w_o_t, w_pw_t, w_out_f)
    (h2, gu, f), ((w_ffn_out_blocks,),) = ffn_in(x1, w["norm2_w"], w_ffn_in_t, sides=(ag_blocks_relay(w["w_ffn_out"], BF16),))
    w_ffn_out_f = w_ffn_out_blocks.reshape(FF, D)
    dy, dyb, sq = ffn_out_loss(x1, f, w_ffn_out_f, target)

    g = {}
    def blocks(name, pairs, tm):
        return [t.reshape(NDEV, t.shape[0] // NDEV, t.shape[1]) for t in mm_tn(name, pairs, tm)]

    g_ffn_out, gb_ffn_out = blocks("gw_ffn_out", [(f, dyb)], FF // 2)
    (d_gu, d_x1, d_x1b, g["norm2_w"]), ((ra_ffn_out,),) = ffn_bwd(
        dy, dyb, gu, x1, w["norm2_w"], w_ffn_in_t, w_ffn_out_f, sides=(rs_to_sibling([gb_ffn_out]),))
    g_ffn_in, gb_ffn_in = blocks("gw_ffn_in", [(d_gu, h2)], FF // 2)
    (d_ya, d_yb, d_gl, d_attn, d_u3, g["b_gate"]), ((ra_ffn_in,),) = out_bwd(
        d_x1b, proj, b_gate_f, ya, yb, w_o_t, w_pw_t, w_out_f, sides=(rs_to_sibling([gb_ffn_in]),))
    g_out, gb_out, g_w_o, gb_w_o, g_w_pw, gb_w_pw = blocks(
        "gw_out_o_pw", [(z, d_x1b), (d_ya, attn), (d_yb, u3)], D // 2)
    (d_conv, g["conv_w"], g["conv_b"], g["conv_ln_w"], g["conv_ln_b"]), ((ra_out, ra_w_o, ra_w_pw),) = conv_bwd(
        proj, cpre, d_u3, conv_w_f, w["conv_ln_w"], w["conv_ln_b"],
        sides=(rs_to_sibling([gb_out, gb_w_o, gb_w_pw]),))
    (pb_ffn_out, own_ffn_out), (pb_ffn_in, own_ffn_in), (pb_out, own_out), (pb_w_o, own_w_o), (pb_w_pw, own_w_pw) = chip_sum(
        "chip_sum_early", [g_ffn_out, g_ffn_in, g_out, g_w_o, g_w_pw],
        [ra_ffn_out, ra_ffn_in, ra_out, ra_w_o, ra_w_pw], c_idx, chip_idx)
    (d_q, d_k, d_v, gqw, gkw), ((rb_ffn_out, rb_ffn_in, rb_out, rb_w_o, rb_w_pw),) = attn_bwd(
        proj, tabs, qw2, kw2, d_attn, attn, lse,
        sides=(rs_to_chips([pb_ffn_out, pb_ffn_in, pb_out, pb_w_o, pb_w_pw]),))
    g["q_norm_w"], g["k_norm_w"] = gqw, gkw
    d_segs = (d_q, d_k, d_v, d_conv, d_gl)
    parts, to_sibling, to_chips, owns, from_chips = [], None, None, [], []
    for k, hw in enumerate(GW_IN_SPLIT):
        sides = tuple(s for s in (to_chips, to_sibling) if s is not None)
        (part, part_b), outs = gw_in("gw_in_%d" % k, h, d_segs, sum(GW_IN_SPLIT[:k]), hw, sides=sides)
        outs = list(outs)
        if to_chips is not None:
            from_chips.append(outs.pop(0)[0])
        if to_sibling is not None:
            (pb, own), = chip_sum("chip_sum_w_in_%d" % (k - 1), [parts[-1]], [outs.pop(0)[0]], c_idx, chip_idx)
            owns.append(own)
            to_chips = rs_to_chips_combined(pb)
        else:
            to_chips = None
        parts.append(part.reshape(NDEV, INW // NDEV, hw))
        to_sibling = rs_to_sibling([part_b.reshape(NDEV, INW // NDEV, hw)])
    (grad_x, g["norm1_w"]), ((rb_prev,), (ra_last,)) = in_bwd(
        d_q, d_k, d_v, d_conv, d_gl, w_in_t, x2, d_x1, w["norm1_w"], sides=(to_chips, to_sibling))
    from_chips.append(rb_prev)
    (pb, own), = chip_sum("chip_sum_w_in_%d" % (len(GW_IN_SPLIT) - 1), [parts[-1]], [ra_last], c_idx, chip_idx)
    owns.append(own)
    small_sums, ((rb_last,),) = small_sync(g, sq, sides=(rs_to_chips_combined(pb),))
    small, loss, (adam_o, adam_pw, adam_out) = small_adam(
        small_sums, w, m, v, (4 * ax + 2 * ay + c_idx).astype(jnp.int32).reshape(1),
        [(own_w_o, rb_w_o, w["w_o_attn"], m["w_o_attn"], v["w_o_attn"], True),
         (own_w_pw, rb_w_pw, w["w_pw_conv"], m["w_pw_conv"], v["w_pw_conv"], True),
         (own_out, rb_out, w["w_out"], m["w_out"], v["w_out"], False)])
    from_chips.append(rb_last)

    adam_ffn_in, adam_ffn_out = rows_adam("adam_w_ffn", [
        (own_ffn_in, rb_ffn_in, w["w_ffn_in"], m["w_ffn_in"], v["w_ffn_in"]),
        (own_ffn_out, rb_ffn_out, w["w_ffn_out"], m["w_ffn_out"], v["w_ffn_out"])], 2)
    res = {
        "w_in": shard_adam("adam_w_in", owns, from_chips, w["w_in"], m["w_in"], v["w_in"]),
        "w_ffn_in": adam_ffn_in, "w_ffn_out": adam_ffn_out,
        "w_o_attn": adam_o, "w_pw_conv": adam_pw, "w_out": adam_out,
    }
    res = {k: tuple(a.T if k in TRANSPOSED else a for a in r) for k, r in res.items()}
    res.update(small)

    def shaped(name, a):
        return a.reshape((1,) + a.shape) if name in MATS or name in ("b_gate", "conv_w") else a

    outs = [loss, grad_x.reshape(1, S, D)]
    for i in range(4):
        outs += [shaped(k, res[k][i]) for k in WEIGHTS]
    return tuple(outs)
```

```python
import functools
from typing import Callable, NamedTuple, Optional

import numpy as np
import jax
import jax.numpy as jnp
from jax import lax
from jax.experimental import pallas as pl
from jax.experimental.pallas import tpu as pltpu

F32 = jnp.float32
BF16 = jnp.bfloat16

S = 2048
D = 1024
HD = 64
QKV = 1536
CC = 512
KW = 31
FF = 2816
INW = 7680
OFF_Q, OFF_K, OFF_V, OFF_CA, OFF_CB, OFF_GA, OFF_GB = 0, 1536, 3072, 4608, 5120, 5632, 6656
DILATIONS = (1, 4, 16)
HALF_SPAN = 64
EPS = 1e-6
NEG_INF = -1e30
ROPE_THETA = 500000.0
ROT_DIM = 16

ADAM_LR = 0.001
ADAM_B1 = 0.9
ADAM_B2 = 0.999
ADAM_EPS = 1e-08
ADAM_WD = 0.01
ADAM_STEP = 10

NDEV = 8
LANES = 128
TM = 256
IN_PROJ_TM = 512
TQ = 128
VMEM_LIMIT = 56 * 1024 * 1024
MESH = pl.DeviceIdType.MESH


def _cp(**kw):
    return pltpu.CompilerParams(vmem_limit_bytes=VMEM_LIMIT, **kw)


def _row(width, col=0, tm=TM):
    return pl.BlockSpec((tm, width), lambda i: (i, col))


PLANE = 512


def _planes(width, tm=TM):
    return pl.BlockSpec((width // PLANE, tm, PLANE), lambda i: (0, i, 0))


def _res(shape):
    nd = len(shape)
    return pl.BlockSpec(shape, lambda *_: (0,) * nd, pipeline_mode=pl.Buffered(1))


def _dot(a, b):
    return jnp.dot(a, b, preferred_element_type=F32)


def _dot_nt(a, b):
    return lax.dot_general(a, b, (((1,), (1,)), ((), ())), preferred_element_type=F32)


def _dot_tn(a, b):
    return lax.dot_general(a, b, (((0,), (0,)), ((), ())), preferred_element_type=F32)


def _sigmoid(x):
    return jax.nn.sigmoid(x)


def _dsilu(x, sg):
    return sg * (1.0 + x * (1.0 - sg))


ANY = pl.BlockSpec(memory_space=pl.ANY)
VMEM = pl.BlockSpec(memory_space=pltpu.VMEM)


class Side(NamedTuple):
    args: tuple
    in_specs: tuple
    out_shape: tuple
    scratch: tuple
    start: Callable
    finish: Callable
    mid: Optional[Callable] = None
    peers: str = ""


BARRIER_IDS = {"s": 0, "dxy": 1, "dsxy": 2, "sxy": 3, "xy": 4}


def _peer_barrier(peers):
    x, y, c = lax.axis_index("x"), lax.axis_index("y"), lax.axis_index("c")
    where = {"s": (x, y, 1 - c), "x": (1 - x, y, c), "y": (x, 1 - y, c), "d": (1 - x, 1 - y, c)}
    barrier = pltpu.get_barrier_semaphore()
    for p in peers:
        pl.semaphore_signal(barrier, inc=1, device_id=where[p], device_id_type=MESH)
    pl.semaphore_wait(barrier, len(peers))


def _call(body, sides=(), *, name, grid, in_specs, out_specs, out_shape, scratch_shapes=(), args, own_comm=False,
          tail=None):
    assert tail is None or int(np.prod(grid)) == 1
    ni, no, ns = len(in_specs), len(out_specs), len(scratch_shapes)
    cnt = [(len(s.args), len(s.out_shape), len(s.scratch)) for s in sides]
    peers = "".join(sorted(set("".join(s.peers for s in sides))))
    if own_comm or not sides or any(not s.peers for s in sides):
        peers = ""

    def take(refs, pos, n):
        return refs[pos:pos + n], pos + n

    def full(*refs):
        m_in, pos = take(refs, 0, ni)
        s_in = []
        for a, _, _ in cnt:
            r, pos = take(refs, pos, a)
            s_in.append(r)
        m_out, pos = take(refs, pos, no)
        s_out = []
        for _, o, _ in cnt:
            r, pos = take(refs, pos, o)
            s_out.append(r)
        m_scr, pos = take(refs, pos, ns)
        s_scr = []
        for _, _, c in cnt:
            r, pos = take(refs, pos, c)
            s_scr.append(r)
        if sides:
            first = functools.reduce(jnp.logical_and, [pl.program_id(d) == 0 for d in range(len(grid))])
            last = functools.reduce(jnp.logical_and, [pl.program_id(d) == g - 1 for d, g in enumerate(grid)])

            @pl.when(first)
            def _():
                if peers:
                    _peer_barrier(peers)
                for s, a, o, c in zip(sides, s_in, s_out, s_scr):
                    s.start(a, o, c)

            steps = int(np.prod(grid))
            mid_step = (2 * steps) // 3
            if steps > 1 and any(s.mid is not None for s in sides):
                step = functools.reduce(lambda acc, d: acc * grid[d] + pl.program_id(d), range(len(grid)), 0)

                @pl.when(step == mid_step)
                def _():
                    for s, a, o, c in zip(sides, s_in, s_out, s_scr):
                        if s.mid is not None:
                            s.mid(a, o, c)

        body(*m_in, *m_out, *m_scr)
        if sides:
            @pl.when(last)
            def _():
                for s, a, o, c in zip(sides, s_in, s_out, s_scr):
                    if s.mid is not None and steps == 1:
                        s.mid(a, o, c)
                if tail is not None:
                    tail(*m_in, *m_out, *m_scr)
                for s, a, o, c in zip(sides, s_in, s_out, s_scr):
                    s.finish(a, o, c)
        elif tail is not None:
            tail(*m_in, *m_out, *m_scr)

    res = pl.pallas_call(
        full, name=name, grid=grid,
        in_specs=list(in_specs) + [sp for s in sides for sp in s.in_specs],
        out_specs=list(out_specs) + [ANY for s in sides for _ in s.out_shape],
        out_shape=list(out_shape) + [o for s in sides for o in s.out_shape],
        scratch_shapes=list(scratch_shapes) + [c for s in sides for c in s.scratch],
        compiler_params=_cp(dimension_semantics=("arbitrary",) * len(grid),
                            **({"collective_id": BARRIER_IDS[peers]} if peers else {})),
    )(*args, *[a for s in sides for a in s.args])
    res = list(res)
    if not sides:
        return res
    outs, pos = take(res, 0, no)
    side_outs = []
    for _, o, _ in cnt:
        r, pos = take(res, pos, o)
        side_outs.append(r)
    return outs, side_outs


def _inv_freq_lanes():
    inv = np.float32(ROPE_THETA) ** (-np.arange(0, ROT_DIM, 2, dtype=np.float32) / np.float32(ROT_DIM))
    lane = np.arange(LANES) % HD
    out = np.where(lane < ROT_DIM, inv[lane % (ROT_DIM // 2)], 0.0).astype(np.float32)
    return jnp.asarray(out.reshape(1, LANES))


def _rope_tables(pos, inv_freq):
    ang = pos.astype(F32) * inv_freq
    lane = lax.broadcasted_iota(jnp.int32, ang.shape, 1) % HD
    cs = jnp.cos(ang)
    sn = jnp.sin(ang)
    return (jnp.where(lane < ROT_DIM, cs, 1.0), jnp.where(lane < ROT_DIM // 2, -sn, 0.0),
            jnp.where(lane < ROT_DIM // 2, 0.0, jnp.where(lane < ROT_DIM, sn, 0.0)))


def _rope(v, c, s1, s2):
    return v * c + pltpu.roll(v, LANES - 8, axis=1) * s1 + pltpu.roll(v, 8, axis=1) * s2


def _rope_t(d, c, s1, s2):
    return d * c - pltpu.roll(d, LANES - 8, axis=1) * s1 - pltpu.roll(d, 8, axis=1) * s2


def _head_mat():
    r = lax.broadcasted_iota(jnp.int32, (LANES, LANES), 0) // HD
    c = lax.broadcasted_iota(jnp.int32, (LANES, LANES), 1) // HD
    return jnp.where(r == c, 1.0 / HD, 0.0).astype(BF16)


def _head_mean(t, e):
    hi = t.astype(BF16)
    rest = (t - hi.astype(F32)).astype(BF16)
    return _dot(hi, e) + _dot(rest, e)


def in_proj_gather(x, norm_w, shard_t, chip_order, pos_col):
    R = INW // NDEV
    tm = IN_PROJ_TM
    half, nt = R // 2, S // tm

    def body(ord_ref, x_ref, nw_ref, sh_ref, pos_ref, f_ref, h_ref, p_ref, wfull_ref, c_ref, s1_ref, s2_ref,
             wt, hs, send, recv, loc):
        kk, i = pl.program_id(0), pl.program_id(1)
        x, y, c, _ = _place()
        me, flip = 4 * x + 2 * y + c, 1 - 2 * c
        here, sib, xn, yn = (x, y, c), (x, y, 1 - c), (1 - x, y, c), (x, 1 - y, c)
        b_xn, b_yn, b_dg = 4 * (1 - x) + 2 * y + c, 4 * x + 2 * (1 - y) + c, 4 * (1 - x) + 2 * (1 - y) + c

        def cp(k, block, to, rows=None):
            dst = wt.at[block] if rows is None else wt.at[block, pl.ds(rows * half, half), :]
            return _remote(dst, dst, send, recv, k, to)

        def sends():
            return [cp(0, me, sib), cp(1, me, xn), cp(2, me, yn), cp(3, b_xn, sib), cp(4, b_yn, sib),
                    cp(5, b_xn, yn, rows=0), cp(6, b_yn, xn, rows=1), cp(7, b_dg, sib, rows=0), cp(8, b_dg, sib, rows=1)]

        def keep(j, blk0):
            pair = pl.ds(pl.multiple_of(blk0, 2), 2)
            return pltpu.make_async_copy(wt.at[pair], wfull_ref.at[pair], loc.at[j])

        @pl.when((kk == 0) & (i == 0))
        def _():
            _peer_barrier("sxy")
            _cast_rows(wt.at[me], sh_ref)
            for s_ in sends()[0:3]:
                s_.start()

            def tables(j, _):
                posf = pos_ref[pl.ds(j, 1), :].astype(F32)
                eye = (lax.broadcasted_iota(jnp.int32, (LANES, LANES), 0)
                       == lax.broadcasted_iota(jnp.int32, (LANES, LANES), 1))
                col = jnp.sum(jnp.where(eye, posf, 0.0), axis=1, keepdims=True)
                chunk = pl.ds(pl.multiple_of(j * LANES, LANES), LANES)
                c_ref[chunk, :], s1_ref[chunk, :], s2_ref[chunk, :] = _rope_tables(col, f_ref[...])
                return 0

            lax.fori_loop(0, S // LANES, tables, 0)
            cp(0, me + flip, here).wait_recv()
            keep(0, me - c).start()

        @pl.when((kk == 1) & (i == 0))
        def _():
            cp(1, b_xn, here).wait_recv()
            sends()[5].start()
            sends()[3].start()
            cp(2, b_yn, here).wait_recv()
            sends()[6].start()
            sends()[4].start()
            cp(3, b_xn + flip, here).wait_recv()
            keep(1, b_xn - c).start()

        @pl.when((kk == 2) & (i == 0))
        def _():
            cp(4, b_yn + flip, here).wait_recv()
            keep(2, b_yn - c).start()

        @pl.when((kk == 3) & (i == 0))
        def _():
            cp(5, b_dg, here, rows=0).wait_recv()
            sends()[7].start()
            cp(6, b_dg, here, rows=1).wait_recv()
            sends()[8].start()
            cp(7, b_dg + flip, here, rows=0).wait_recv()
            cp(8, b_dg + flip, here, rows=1).wait_recv()
            keep(3, b_dg - c).start()

        rows = pl.ds(pl.multiple_of(i * tm, tm), tm)

        @pl.when(kk == 0)
        def _():
            xv = x_ref[...]
            r = lax.rsqrt(jnp.mean(xv * xv, axis=-1, keepdims=True) + EPS)
            hb = (xv * r * nw_ref[...]).astype(BF16)
            h_ref[...] = hb
            hs[rows, :] = hb

        h = hs[rows, :]
        chip = ord_ref[kk]
        for cc in range(2):
            p_ref[:, cc * R:(cc + 1) * R] = _dot_nt(h, wt[2 * chip + cc])

        @pl.when((kk == 3) & (i == nt - 1))
        def _():
            for s_ in sends():
                s_.wait_send()
            for j, blk in enumerate((me, b_xn, b_yn, b_dg)):
                keep(j, blk - c).wait()

    def first_pass(kk, i):
        return jnp.where(kk == 0, i, nt - 1)

    grid_spec = pltpu.PrefetchScalarGridSpec(
        num_scalar_prefetch=1, grid=(4, nt),
        in_specs=[pl.BlockSpec((tm, D), lambda kk, i, o: (first_pass(kk, i), 0)),
                  pl.BlockSpec((1, D), lambda kk, i, o: (0, 0)), VMEM, VMEM,
                  pl.BlockSpec((1, LANES), lambda kk, i, o: (0, 0))],
        out_specs=[pl.BlockSpec((tm, D), lambda kk, i, o: (first_pass(kk, i), 0)),
                   pl.BlockSpec((tm, 2 * R), lambda kk, i, o: (i, o[kk])), ANY]
        + [pl.BlockSpec((S, LANES), lambda kk, i, o: (0, 0))] * 3,
        scratch_shapes=[pltpu.VMEM((NDEV, R, D), BF16), pltpu.VMEM((S, D), BF16), _sems(9), _sems(9), _sems(4)])
    res = pl.pallas_call(
        body, name="in_proj_gather", grid_spec=grid_spec,
        out_shape=[jax.ShapeDtypeStruct((S, D), BF16), jax.ShapeDtypeStruct((S, INW), F32),
                   jax.ShapeDtypeStruct((NDEV, R, D), BF16)] + [jax.ShapeDtypeStruct((S, LANES), F32)] * 3,
        compiler_params=_cp(dimension_semantics=("arbitrary", "arbitrary"), collective_id=BARRIER_IDS["sxy"]),
    )(chip_order, x, norm_w, shard_t, pos_col, _inv_freq_lanes())
    return res[0], res[1], res[2], tuple(res[3:])


def _qk_specs():
    nb = QKV // LANES
    return [pl.BlockSpec((S, LANES), functools.partial(lambda hp, g, o: (0, o + g * 4 + hp), o=o))
            for o in (OFF_Q // LANES, OFF_K // LANES, OFF_V // LANES)]


def _tab_specs():
    return [pl.BlockSpec((S, LANES), lambda hp, g: (0, 0), pipeline_mode=pl.Buffered(1))] * 3


def _vec_spec():
    return pl.BlockSpec((1, LANES), lambda hp, g: (0, 0))


def _sub_rows(r, d, start, n):
    if d == 1:
        return pl.ds(start, n)
    return pl.ds(r + d * start, n, stride=d)


def _band_window(i, L):
    W = min(TQ + 2 * HALF_SPAN, L)
    q0 = pl.multiple_of(i * TQ, TQ)
    k0 = pl.multiple_of(jnp.clip(q0 - HALF_SPAN, 0, L - W), HALF_SPAN)
    qpos = q0 + (lax.broadcasted_iota(jnp.int32, (2 * TQ, W), 0) & (TQ - 1))
    kpos = k0 + lax.broadcasted_iota(jnp.int32, (2 * TQ, W), 1)
    valid = jnp.abs(qpos - kpos) <= HALF_SPAN
    return W, q0, k0, valid


def _stack_heads(t, lo):
    z = jnp.zeros_like(t)
    return jnp.concatenate([jnp.where(lo, t, z), jnp.where(lo, z, t)], axis=0)


def _unstack_heads(t2, lo):
    return jnp.where(lo, t2[0:TQ], t2[TQ:2 * TQ])


CHAINS = 8


def _interleave(d):
    ru = min(d, CHAINS)
    return ru, min(CHAINS // ru, S // d // TQ)


def _for_blocks(n, fn):
    if n == 1:
        fn(0)
    else:
        def it(j, _):
            fn(j)
            return 0
        lax.fori_loop(0, n, it, 0)


def attn_fwd(proj, tabs, qw2, kw2, sides=()):
    CH = 256

    def body(q_ref, k_ref, v_ref, c_ref, s1_ref, s2_ref, qw_ref, kw_ref, at_ref, ls_ref,
             qs, ks, vs, osub, lsub, onat, lnat, qn, kn):
        g = pl.program_id(1)
        lo = lax.broadcasted_iota(jnp.int32, (1, LANES), 1) < HD
        e = _head_mat()

        def prep(i, _):
            rows = pl.ds(pl.multiple_of(i * CH, CH), CH)
            c, s1, s2 = c_ref[rows, :], s1_ref[rows, :], s2_ref[rows, :]
            for t_ref, w_ref, out, scale in ((q_ref, qw_ref, qn, HD ** -0.5), (k_ref, kw_ref, kn, 1.0)):
                t = t_ref[rows, :]
                r = lax.rsqrt(_head_mean(t * t, e) + EPS)
                out[rows, :] = _rope(t * r * w_ref[...], c, s1, s2) * scale
            return 0

        lax.fori_loop(0, S // CH, prep, 0, unroll=4)

        def group(gi, d):
            L = S // d

            ru, nb = _interleave(d)

            def stage(r, off):
                for c0 in range(0, L, CH):
                    n = min(CH, L)
                    rows = _sub_rows(r, d, c0, n)
                    dst = pl.ds(off + c0, n)
                    qs[dst, :] = qn[rows, :].astype(BF16)
                    ks[dst, :] = kn[rows, :].astype(BF16)
                    vs[dst, :] = v_ref[rows, :].astype(BF16)

            def one(off, i):
                W, q0, k0, valid = _band_window(i, L)
                q2 = _stack_heads(qs[pl.ds(off + q0, TQ), :], lo)
                sc = jnp.where(valid, _dot_nt(q2, ks[pl.ds(off + k0, W), :]), NEG_INF)
                m = jnp.max(sc, axis=-1, keepdims=True)
                p = jnp.exp(sc - m)
                den = jnp.sum(p, axis=-1, keepdims=True)
                o2 = _dot(p.astype(BF16), vs[pl.ds(off + k0, W), :]) / den
                l2 = jnp.broadcast_to(m + jnp.log(den), (2 * TQ, LANES))
                osub[pl.ds(off + q0, TQ), :] = _unstack_heads(o2, lo)
                lsub[pl.ds(off + q0, TQ), :] = _unstack_heads(l2, lo)

            def unstage(r, off):
                for c0 in range(0, L, CH):
                    n = min(CH, L)
                    rows = _sub_rows(r, d, c0, n)
                    onat[gi, rows, :] = osub[pl.ds(off + c0, n), :]
                    lnat[gi, rows, :] = lsub[pl.ds(off + c0, n), :]

            def step(t, _):
                for u in range(ru):
                    stage(t * ru + u, u * L)
                _for_blocks(L // TQ // nb, lambda j: [one(u * L, j * nb + b) for u in range(ru) for b in range(nb)])
                for u in range(ru):
                    unstage(t * ru + u, u * L)
                return 0

            lax.fori_loop(0, d // ru, step, 0)

        for gi, d in enumerate(DILATIONS):
            pl.when(g == gi)(functools.partial(group, gi, d))

        @pl.when(g == len(DILATIONS) - 1)
        def _():
            def mix(i, _):
                rows = pl.ds(pl.multiple_of(i * CH, CH), CH)
                l0, l1, l2 = lnat[0, rows, :], lnat[1, rows, :], lnat[2, rows, :]
                m = jnp.maximum(jnp.maximum(l0, l1), l2)
                e0, e1, e2 = jnp.exp(l0 - m), jnp.exp(l1 - m), jnp.exp(l2 - m)
                den = e0 + e1 + e2
                a = (e0 * onat[0, rows, :] + e1 * onat[1, rows, :] + e2 * onat[2, rows, :]) / den
                at_ref[rows, :] = a.astype(BF16)
                ls_ref[rows, :] = m + jnp.log(den)
                return 0

            lax.fori_loop(0, S // CH, mix, 0)

    out_spec = pl.BlockSpec((S, LANES), lambda hp, g: (0, hp))
    return _call(
        body, sides, name="attn_fwd", grid=(4, 3),
        in_specs=_qk_specs() + _tab_specs() + [_vec_spec(), _vec_spec()],
        out_specs=[out_spec, out_spec],
        out_shape=[jax.ShapeDtypeStruct((S, CC), BF16), jax.ShapeDtypeStruct((S, CC), F32)],
        scratch_shapes=[pltpu.VMEM((S, LANES), BF16)] * 3 + [pltpu.VMEM((S, LANES), F32)] * 2
        + [pltpu.VMEM((3, S, LANES), F32)] * 2 + [pltpu.VMEM((S, LANES), F32)] * 2,
        args=(proj, proj, proj, *tabs, qw2, kw2))


def attn_bwd(proj, tabs, qw2, kw2, d_attn, attn, lse, sides=()):
    CH = 256

    def body(q_ref, k_ref, v_ref, c_ref, s1_ref, s2_ref, qw_ref, kw_ref, do_ref, at_ref, ls_ref,
             dq_ref, dk_ref, dv_ref, gqw_ref, gkw_ref,
             qs, ks, vs, dos, dsub, lsub, dqs, dks, dvs, dnat, qx, kx, dvn, tnq, tnk, rrq, rrk):
        hp, g = pl.program_id(0), pl.program_id(1)
        lo = lax.broadcasted_iota(jnp.int32, (1, LANES), 1) < HD
        e = _head_mat()
        both = ((q_ref, qw_ref, qx, tnq, rrq, HD ** -0.5), (k_ref, kw_ref, kx, tnk, rrk, 1.0))

        @pl.when((hp == 0) & (g == 0))
        def _():
            gqw_ref[...] = jnp.zeros_like(gqw_ref)
            gkw_ref[...] = jnp.zeros_like(gkw_ref)

        def prep(i, _):
            rows = pl.ds(pl.multiple_of(i * CH, CH), CH)
            dnat[rows, :] = _head_mean(do_ref[rows, :] * at_ref[rows, :].astype(F32), e) * float(HD)
            c, s1, s2 = c_ref[rows, :], s1_ref[rows, :], s2_ref[rows, :]
            for t_ref, w_ref, x, tn_s, rr_s, scale in both:
                t = t_ref[rows, :]
                rr = lax.rsqrt(_head_mean(t * t, e) + EPS)
                tn = t * rr
                rr_s[rows, :] = rr
                tn_s[rows, :] = tn
                x[rows, :] = _rope(tn * w_ref[...], c, s1, s2) * scale
            return 0

        lax.fori_loop(0, S // CH, prep, 0, unroll=4)

        def group(d):
            L = S // d

            ru, nb = _interleave(d)

            def stage(r, off):
                for c0 in range(0, L, CH):
                    n = min(CH, L)
                    rows = _sub_rows(r, d, c0, n)
                    dst = pl.ds(off + c0, n)
                    qs[dst, :] = qx[rows, :].astype(BF16)
                    ks[dst, :] = kx[rows, :].astype(BF16)
                    vs[dst, :] = v_ref[rows, :].astype(BF16)
                    dos[dst, :] = do_ref[rows, :].astype(BF16)
                    dsub[dst, :] = dnat[rows, :]
                    lsub[dst, :] = ls_ref[rows, :]
                    dks[dst, :] = jnp.zeros((n, LANES), F32)
                    dvs[dst, :] = jnp.zeros((n, LANES), F32)

            def one(off, i):
                W, q0, k0, valid = _band_window(i, L)
                qrows, krows = pl.ds(off + q0, TQ), pl.ds(off + k0, W)
                q2 = _stack_heads(qs[qrows, :], lo)
                do2 = _stack_heads(dos[qrows, :], lo)
                kk, vv = ks[krows, :], vs[krows, :]
                lse_b, dd_b = lsub[qrows, :], dsub[qrows, :]
                lse2 = jnp.concatenate([lse_b[:, 0:1], lse_b[:, HD:HD + 1]], axis=0)
                dd2 = jnp.concatenate([dd_b[:, 0:1], dd_b[:, HD:HD + 1]], axis=0)
                sc = jnp.where(valid, _dot_nt(q2, kk), NEG_INF)
                p = jnp.exp(sc - lse2)
                ds = (p * (_dot_nt(do2, vv) - dd2)).astype(BF16)
                dqs[qrows, :] = _unstack_heads(_dot(ds, kk), lo)
                dks[krows, :] = dks[krows, :] + _dot_tn(ds, q2)
                dvs[krows, :] = dvs[krows, :] + _dot_tn(p.astype(BF16), do2)

            def unstage(r, off):
                for c0 in range(0, L, CH):
                    n = min(CH, L)
                    rows = _sub_rows(r, d, c0, n)
                    src = pl.ds(off + c0, n)
                    qx[rows, :] = dqs[src, :]
                    kx[rows, :] = dks[src, :]
                    dvn[rows, :] = dvs[src, :]

            def step(t, _):
                for u in range(ru):
                    stage(t * ru + u, u * L)
                _for_blocks(L // TQ // nb, lambda j: [one(u * L, j * nb + b) for u in range(ru) for b in range(nb)])
                for u in range(ru):
                    unstage(t * ru + u, u * L)
                return 0

            lax.fori_loop(0, d // ru, step, 0)

        for gi, d in enumerate(DILATIONS):
            pl.when(g == gi)(functools.partial(group, d))

        def emit(i, _):
            rows = pl.ds(pl.multiple_of(i * CH, CH), CH)
            c, s1, s2 = c_ref[rows, :], s1_ref[rows, :], s2_ref[rows, :]
            for (_, w_ref, x, tn_s, rr_s, scale), out, gw_ref in zip(both, (dq_ref, dk_ref), (gqw_ref, gkw_ref)):
                tn = tn_s[rows, :]
                dy = _rope_t(x[rows, :] * scale, c, s1, s2)
                gw_ref[0:1, :] = gw_ref[0:1, :] + jnp.sum(dy * tn, axis=0, keepdims=True)
                dtn = dy * w_ref[...]
                out[rows, :] = (rr_s[rows, :] * (dtn - tn * _head_mean(dtn * tn, e))).astype(BF16)
            dv_ref[rows, :] = dvn[rows, :].astype(BF16)
            return 0

        lax.fori_loop(0, S // CH, emit, 0, unroll=4)

    nat_spec = pl.BlockSpec((S, LANES), lambda hp, g: (0, hp))
    out_spec = pl.BlockSpec((None, S, LANES), lambda hp, g: (g, 0, hp))
    acc_spec = pl.BlockSpec((8, LANES), lambda hp, g: (0, 0))
    return _call(
        body, sides, name="attn_bwd", grid=(4, 3),
        in_specs=_qk_specs() + _tab_specs() + [_vec_spec(), _vec_spec(), nat_spec, nat_spec, nat_spec],
        out_specs=[out_spec] * 3 + [acc_spec] * 2,
        out_shape=[jax.ShapeDtypeStruct((QKV // PLANE, S, PLANE), BF16)] * 3 + [jax.ShapeDtypeStruct((8, LANES), F32)] * 2,
        scratch_shapes=[pltpu.VMEM((S, LANES), BF16)] * 4 + [pltpu.VMEM((S, LANES), F32)] * 13,
        args=(proj, proj, proj, *tabs, qw2, kw2, d_attn, attn, lse))


PADR = 16
CT = 128


def _conv_specs():
    return [pl.BlockSpec((S, CC), lambda i: (0, OFF_CA // CC)), pl.BlockSpec((S, CC), lambda i: (0, OFF_CB // CC))]


NCB = CC // LANES


def _pad_zero(pad):
    for cb in range(NCB):
        pad[cb, 0:PADR, :] = jnp.zeros((PADR, LANES), F32)
        pad[cb, PADR + S:PADR + S + PADR, :] = jnp.zeros((PADR, LANES), F32)


def _pad_store(pad, row0, n, val):
    for cb in range(NCB):
        pad[cb, pl.ds(pl.multiple_of(row0 + PADR, 8), n), :] = val[:, cb * LANES:(cb + 1) * LANES]


def _taps(pad_ref, cb, s0, weights):
    acc = jnp.zeros((CT, LANES), F32)
    for k in range(KW):
        acc = acc + weights[k] * pad_ref[cb, pl.ds(s0 + k + 1, CT), :]
    return acc


def conv_fwd(proj, conv_w, conv_b, ln_w, ln_b):
    def body(a_ref, b_ref, w_ref, cb_ref, lw_ref, lb_ref, c_ref, u3_ref, upad):
        _pad_zero(upad)

        def glu(i, _):
            rows = pl.ds(pl.multiple_of(i * TM, TM), TM)
            _pad_store(upad, i * TM, TM, a_ref[rows, :] * _sigmoid(b_ref[rows, :]))
            return 0

        lax.fori_loop(0, S // TM, glu, 0)

        def chunk(i, _):
            s0 = pl.multiple_of(i * CT, CT)
            for cb in range(CC // LANES):
                cols = slice(cb * LANES, (cb + 1) * LANES)
                w = [w_ref[k:k + 1, cols] for k in range(KW)]
                c_ref[pl.ds(s0, CT), cols] = _taps(upad, cb, s0, w) + cb_ref[:, cols]
            cv = c_ref[pl.ds(s0, CT), :]
            mu = jnp.mean(cv, axis=-1, keepdims=True)
            xc = cv - mu
            rstd = lax.rsqrt(jnp.mean(xc * xc, axis=-1, keepdims=True) + EPS)
            yl = xc * rstd * lw_ref[...] + lb_ref[...]
            u3_ref[pl.ds(s0, CT), :] = (yl * _sigmoid(yl)).astype(BF16)
            return 0

        lax.fori_loop(0, S // CT, chunk, 0)

    vec = pl.BlockSpec((1, CC), lambda i: (0, 0))
    full = pl.BlockSpec((S, CC), lambda i: (0, 0))
    return _call(
        body, name="conv_fwd", grid=(1,),
        in_specs=_conv_specs() + [pl.BlockSpec((KW, CC), lambda i: (0, 0)), vec, vec, vec],
        out_specs=[full, full],
        out_shape=[jax.ShapeDtypeStruct((S, CC), F32), jax.ShapeDtypeStruct((S, CC), BF16)],
        scratch_shapes=[pltpu.VMEM((NCB, S + 2 * PADR, LANES), F32)],
        args=(proj, proj, conv_w, conv_b, ln_w, ln_b))


def conv_bwd(proj, cpre, d_u3, conv_w, ln_w, ln_b, sides=()):
    def body(a_ref, b_ref, c_ref, du3_ref, w_ref, lw_ref, lb_ref,
             dc_ref, gw_ref, gcb_ref, glw_ref, glb_ref, upad, dpad):
        _pad_zero(upad)
        _pad_zero(dpad)
        gw_ref[...] = jnp.zeros_like(gw_ref)

        def ln_bwd(i, carry):
            gcb, glw, glb = carry
            rows = pl.ds(pl.multiple_of(i * TM, TM), TM)
            _pad_store(upad, i * TM, TM, a_ref[rows, :] * _sigmoid(b_ref[rows, :]))
            cv = c_ref[rows, :]
            mu = jnp.mean(cv, axis=-1, keepdims=True)
            xc = cv - mu
            rstd = lax.rsqrt(jnp.mean(xc * xc, axis=-1, keepdims=True) + EPS)
            xh = xc * rstd
            yl = xh * lw_ref[...] + lb_ref[...]
            dyl = du3_ref[rows, :] * _dsilu(yl, _sigmoid(yl))
            dxh = dyl * lw_ref[...]
            dcv = rstd * (dxh - jnp.mean(dxh, axis=-1, keepdims=True)
                          - xh * jnp.mean(dxh * xh, axis=-1, keepdims=True))
            _pad_store(dpad, i * TM, TM, dcv)
            return (gcb + jnp.sum(dcv, axis=0, keepdims=True),
                    glw + jnp.sum(dyl * xh, axis=0, keepdims=True),
                    glb + jnp.sum(dyl, axis=0, keepdims=True))

        z = jnp.zeros((1, CC), F32)
        gcb, glw, glb = lax.fori_loop(0, S // TM, ln_bwd, (z, z, z))
        gcb_ref[...] = gcb
        glw_ref[...] = glw
        glb_ref[...] = glb

        def chunk(i, _):
            s0 = pl.multiple_of(i * CT, CT)
            for cb in range(CC // LANES):
                cols = slice(cb * LANES, (cb + 1) * LANES)
                wr = [w_ref[KW - 1 - k:KW - k, cols] for k in range(KW)]
                du = _taps(dpad, cb, s0, wr)
                dcv = dpad[cb, pl.ds(s0 + PADR, CT), :]
                for k in range(KW):
                    gw_ref[k:k + 1, cols] = gw_ref[k:k + 1, cols] + jnp.sum(
                        upad[cb, pl.ds(s0 + k + 1, CT), :] * dcv, axis=0, keepdims=True)
                av = a_ref[pl.ds(s0, CT), cols]
                sb = _sigmoid(b_ref[pl.ds(s0, CT), cols])
                dc_ref[0, pl.ds(s0, CT), cols] = (du * sb).astype(BF16)
                dc_ref[1, pl.ds(s0, CT), cols] = (du * av * sb * (1.0 - sb)).astype(BF16)
            return 0

        lax.fori_loop(0, S // CT, chunk, 0)

    vec = pl.BlockSpec((1, CC), lambda i: (0, 0))
    full = pl.BlockSpec((S, CC), lambda i: (0, 0))
    wsp = pl.BlockSpec((KW, CC), lambda i: (0, 0))
    return _call(
        body, sides, name="conv_bwd", grid=(1,),
        in_specs=_conv_specs() + [full, full, wsp, vec, vec],
        out_specs=[pl.BlockSpec((2, S, CC), lambda i: (0, 0, 0)), wsp, vec, vec, vec],
        out_shape=[jax.ShapeDtypeStruct((2, S, CC), BF16), jax.ShapeDtypeStruct((KW, CC), F32)]
        + [jax.ShapeDtypeStruct((1, CC), F32)] * 3,
        scratch_shapes=[pltpu.VMEM((NCB, S + 2 * PADR, LANES), F32)] * 2,
        args=(proj, proj, cpre, d_u3, conv_w, ln_w, ln_b))


def _gate_specs():
    return [_row(CC, col=OFF_GA // CC + j) for j in range(4)]


def _gates(g_refs, bg_ref):
    ga = _sigmoid(jnp.concatenate([g_refs[0][...], g_refs[1][...]], axis=1) + bg_ref[0:1, :])
    gb = _sigmoid(jnp.concatenate([g_refs[2][...], g_refs[3][...]], axis=1) + bg_ref[1:2, :])
    return ga, gb


def mix_out(x, proj, b_gate, attn, u3, w_o, w_pw, w_out):
    def body(x_ref, g0, g1, g2, g3, bg_ref, at_ref, u3_ref, wo_ref, wp_ref, wout_ref,
             x1_ref, z_ref, ya_ref, yb_ref):
        ga, gb = _gates((g0, g1, g2, g3), bg_ref)
        ya = _dot_nt(at_ref[...], wo_ref[...])
        yb = _dot_nt(u3_ref[...], wp_ref[...])
        z = (ga * ya + gb * yb).astype(BF16)
        ya_ref[...] = ya.astype(BF16)
        yb_ref[...] = yb.astype(BF16)
        z_ref[...] = z
        x1_ref[...] = x_ref[...] + _dot(z, wout_ref[...])

    return pl.pallas_call(
        body, name="mix_out", grid=(S // TM,),
        in_specs=[_row(D)] + _gate_specs() + [_res((2, D)), _row(CC), _row(CC),
                                              _res((D, CC)), _res((D, CC)), _res((D, D))],
        out_specs=[_row(D)] * 4,
        out_shape=[jax.ShapeDtypeStruct((S, D), F32)] + [jax.ShapeDtypeStruct((S, D), BF16)] * 3,
        compiler_params=_cp(dimension_semantics=("arbitrary",)),
    )(x, proj, proj, proj, proj, b_gate, attn, u3, w_o, w_pw, w_out)


def out_bwd(d_x1b, proj, b_gate, ya, yb, w_o, w_pw, w_out, sides=()):
    def body(dx_ref, g0, g1, g2, g3, bg_ref, ya_ref, yb_ref, wo_ref, wp_ref, wout_ref,
             dya_ref, dyb_ref, dgl_ref, dat_ref, du3_ref, gbg_ref):
        @pl.when(pl.program_id(0) == 0)
        def _():
            gbg_ref[...] = jnp.zeros_like(gbg_ref)

        ga, gb = _gates((g0, g1, g2, g3), bg_ref)
        dz = _dot_nt(dx_ref[...], wout_ref[...])
        dya = (dz * ga).astype(BF16)
        dyb = (dz * gb).astype(BF16)
        dgla = dz * ya_ref[...].astype(F32) * ga * (1.0 - ga)
        dglb = dz * yb_ref[...].astype(F32) * gb * (1.0 - gb)
        dya_ref[...] = dya
        dyb_ref[...] = dyb
        for j in range(2):
            dgl_ref[j] = dgla[:, j * PLANE:(j + 1) * PLANE].astype(BF16)
            dgl_ref[2 + j] = dglb[:, j * PLANE:(j + 1) * PLANE].astype(BF16)
        gbg_ref[0:1, :] = gbg_ref[0:1, :] + jnp.sum(dgla, axis=0, keepdims=True)
        gbg_ref[1:2, :] = gbg_ref[1:2, :] + jnp.sum(dglb, axis=0, keepdims=True)
        dat_ref[...] = _dot(dya, wo_ref[...])
        du3_ref[...] = _dot(dyb, wp_ref[...])

    return _call(
        body, sides, name="out_bwd", grid=(S // TM,),
        in_specs=[_row(D)] + _gate_specs() + [_res((2, D)), _row(D), _row(D),
                                              _res((D, CC)), _res((D, CC)), _res((D, D))],
        out_specs=[_row(D), _row(D), _planes(2 * D), _row(CC), _row(CC), pl.BlockSpec((2, D), lambda i: (0, 0))],
        out_shape=[jax.ShapeDtypeStruct((S, D), BF16)] * 2 + [jax.ShapeDtypeStruct((2 * D // PLANE, S, PLANE), BF16)]
        + [jax.ShapeDtypeStruct((S, CC), F32)] * 2 + [jax.ShapeDtypeStruct((2, D), F32)],
        args=(d_x1b, proj, proj, proj, proj, b_gate, ya, yb, w_o, w_pw, w_out))


def ffn_in(x1, norm_w, w_ffn_in, sides=()):
    half = FF // 2

    def body(x_ref, nw_ref, w_ref, h_ref, gu_ref, f_ref):
        xv = x_ref[...]
        r = lax.rsqrt(jnp.mean(xv * xv, axis=-1, keepdims=True) + EPS)
        h = (xv * r * nw_ref[...]).astype(BF16)
        h_ref[...] = h
        for j in range(2):
            gt = _dot_nt(h, w_ref[j * half:(j + 1) * half, :])
            up = _dot_nt(h, w_ref[FF + j * half:FF + (j + 1) * half, :])
            gu_ref[:, j * half:(j + 1) * half] = gt.astype(BF16)
            gu_ref[:, FF + j * half:FF + (j + 1) * half] = up.astype(BF16)
            f_ref[:, j * half:(j + 1) * half] = (gt * _sigmoid(gt) * up).astype(BF16)

    return _call(
        body, sides, name="ffn_in", grid=(S // TM,),
        in_specs=[_row(D), _res((1, D)), _res((2 * FF, D))],
        out_specs=[_row(D), _row(2 * FF), _row(FF)],
        out_shape=[jax.ShapeDtypeStruct((S, D), BF16), jax.ShapeDtypeStruct((S, 2 * FF), BF16),
                   jax.ShapeDtypeStruct((S, FF), BF16)],
        args=(x1, norm_w, w_ffn_in))


def ffn_out_loss(x1, f, w_ffn_out, target):
    def body(x_ref, f_ref, w_ref, t_ref, dy_ref, dyb_ref, sq_ref):
        @pl.when(pl.program_id(0) == 0)
        def _():
            sq_ref[...] = jnp.zeros_like(sq_ref)

        diff = x_ref[...] + _dot(f_ref[...], w_ref[...]) - t_ref[...]
        dy = diff * (1.0 / D)
        dy_ref[...] = dy
        dyb_ref[...] = dy.astype(BF16)
        sq_ref[...] = sq_ref[...] + jnp.sum((diff * diff).reshape(TM // 8, 8, D), axis=0)

    return pl.pallas_call(
        body, name="ffn_out_loss", grid=(S // TM,),
        in_specs=[_row(D), _row(FF), _res((FF, D)), _row(D)],
        out_specs=[_row(D), _row(D), pl.BlockSpec((8, D), lambda i: (0, 0))],
        out_shape=[jax.ShapeDtypeStruct((S, D), F32), jax.ShapeDtypeStruct((S, D), BF16),
                   jax.ShapeDtypeStruct((8, D), F32)],
        compiler_params=_cp(dimension_semantics=("arbitrary",)),
    )(x1, f, w_ffn_out, target)


def _rms_bwd(xv, nw, dh):
    r = lax.rsqrt(jnp.mean(xv * xv, axis=-1, keepdims=True) + EPS)
    xn = xv * r
    dxn = dh * nw
    dx = r * (dxn - xn * jnp.mean(dxn * xn, axis=-1, keepdims=True))
    return dx, dh * xn


def ffn_bwd(dy, dyb, gu, x1, norm_w, w_ffn_in, w_ffn_out, sides=()):
    def body(dy_ref, dyb_ref, gu_ref, x_ref, nw_ref, wi_ref, wo_ref, dgu_ref, dx_ref, dxb_ref, gn_ref):
        @pl.when(pl.program_id(0) == 0)
        def _():
            gn_ref[...] = jnp.zeros_like(gn_ref)

        df = _dot_nt(dyb_ref[...], wo_ref[...])
        gt = gu_ref[:, 0:FF].astype(F32)
        up = gu_ref[:, FF:2 * FF].astype(F32)
        sg = _sigmoid(gt)
        dgt = (df * up * _dsilu(gt, sg)).astype(BF16)
        dup = (df * gt * sg).astype(BF16)
        dgu_ref[:, 0:FF] = dgt
        dgu_ref[:, FF:2 * FF] = dup
        dh = _dot(dgt, wi_ref[0:FF, :]) + _dot(dup, wi_ref[FF:2 * FF, :])
        dxn, gw = _rms_bwd(x_ref[...], nw_ref[...], dh)
        dx = dy_ref[...] + dxn
        dx_ref[...] = dx
        dxb_ref[...] = dx.astype(BF16)
        gn_ref[...] = gn_ref[...] + jnp.sum(gw, axis=0, keepdims=True)

    return _call(
        body, sides, name="ffn_bwd", grid=(S // TM,),
        in_specs=[_row(D), _row(D), _row(2 * FF), _row(D), _res((1, D)), _res((2 * FF, D)), _res((FF, D))],
        out_specs=[_row(2 * FF), _row(D), _row(D), pl.BlockSpec((1, D), lambda i: (0, 0))],
        out_shape=[jax.ShapeDtypeStruct((S, 2 * FF), BF16), jax.ShapeDtypeStruct((S, D), F32),
                   jax.ShapeDtypeStruct((S, D), BF16), jax.ShapeDtypeStruct((1, D), F32)],
        args=(dy, dyb, gu, x1, norm_w, w_ffn_in, w_ffn_out))


def in_bwd(d_q, d_k, d_v, d_conv, d_gl, w_in, x, d_x1, norm_w, sides=()):
    segs = ((OFF_Q, QKV), (OFF_K, QKV), (OFF_V, QKV), (OFF_CA, 2 * CC), (OFF_GA, 2 * D))

    def body(dq_ref, dk_ref, dv_ref, dc_ref, dg_ref, w_ref, x_ref, dx1_ref, nw_ref, gx_ref, gn_ref):
        @pl.when(pl.program_id(0) == 0)
        def _():
            gn_ref[...] = jnp.zeros_like(gn_ref)

        dh = jnp.zeros((TM, D), F32)
        for ref, (off, width) in zip((dq_ref, dk_ref, dv_ref, dc_ref, dg_ref), segs):
            for j in range(width // PLANE):
                dh = dh + _dot(ref[j], w_ref[off + j * PLANE:off + (j + 1) * PLANE, :])
        dxn, gw = _rms_bwd(x_ref[...], nw_ref[...], dh)
        gx_ref[...] = dx1_ref[...] + dxn
        gn_ref[...] = gn_ref[...] + jnp.sum(gw, axis=0, keepdims=True)

    return _call(
        body, sides, name="in_bwd", grid=(S // TM,),
        in_specs=[_planes(QKV)] * 3 + [_planes(2 * CC), _planes(2 * D), _res((INW, D)), _row(D), _row(D), _res((1, D))],
        out_specs=[_row(D), pl.BlockSpec((1, D), lambda i: (0, 0))],
        out_shape=[jax.ShapeDtypeStruct((S, D), F32), jax.ShapeDtypeStruct((1, D), F32)],
        args=(d_q, d_k, d_v, d_conv, d_gl, w_in, x, d_x1, norm_w))


def mm_tn(name, pairs, tm):
    n = len(pairs)
    M = pairs[0][0].shape[1]
    widths = [b.shape[1] for _, b in pairs]

    def body(*refs):
        for a_ref, b_ref, o_ref, ob_ref in zip(refs[0:2 * n:2], refs[1:2 * n:2], refs[2 * n::2], refs[2 * n + 1::2]):
            r = _dot_tn(a_ref[...], b_ref[...])
            o_ref[...] = r
            ob_ref[...] = r.astype(BF16)

    return _call(
        body, name=name, grid=(M // tm,),
        in_specs=[sp for N in widths for sp in (pl.BlockSpec((S, tm), lambda i: (0, i)), _res((S, N)))],
        out_specs=[pl.BlockSpec((tm, N), lambda i: (i, 0)) for N in widths for _ in range(2)],
        out_shape=[jax.ShapeDtypeStruct((M, N), dt) for N in widths for dt in (F32, BF16)],
        args=[t for pair in pairs for t in pair])


GW_IN_TN = PLANE
GW_IN_RING = 4
GW_IN_SPLIT = (768, 256)


def gw_in(name, h, d_segs, col0, hw, sides=()):
    tn = GW_IN_TN
    starts, t0 = [], 0
    for seg in d_segs:
        starts.append(t0)
        t0 += seg.shape[0]
    ntiles = [seg.shape[0] for seg in d_segs]

    steps, slots = INW // tn, GW_IN_RING

    def body(h_ref, *refs):
        a_refs, o_ref, ob_ref, ring, sem = refs[:-4], refs[-4], refs[-3], refs[-2], refs[-1]
        n = pl.program_id(0)

        def fetch(t):
            for a_ref, st, nt in zip(a_refs, starts, ntiles):
                @pl.when((t >= st) & (t < st + nt))
                def _(a_ref=a_ref, st=st):
                    pltpu.make_async_copy(a_ref.at[t - st], ring.at[t % slots], sem.at[t % slots]).start()

        @pl.when(n == 0)
        def _():
            for t in range(slots - 1):
                fetch(jnp.int32(t))

        @pl.when(n + slots - 1 < steps)
        def _():
            fetch(n + slots - 1)

        slot = n % slots
        pltpu.make_async_copy(a_refs[0].at[0], ring.at[slot], sem.at[slot]).wait()
        r = _dot_tn(ring[slot], h_ref[...])
        o_ref[...] = r
        ob_ref[...] = r.astype(BF16)

    res = _call(
        body, sides, name=name, grid=(steps,),
        in_specs=[pl.BlockSpec((S, hw), lambda n: (0, col0 // hw))] + [ANY] * len(d_segs),
        out_specs=[pl.BlockSpec((tn, hw), lambda n: (n, 0))] * 2,
        out_shape=[jax.ShapeDtypeStruct((INW, hw), F32), jax.ShapeDtypeStruct((INW, hw), BF16)],
        scratch_shapes=[pltpu.VMEM((slots, S, tn), BF16), _sems(slots)],
        args=(h, *d_segs))
    return (res[0], res[1]) if sides else (res, [])


def _place():
    x, y, c = lax.axis_index("x"), lax.axis_index("y"), lax.axis_index("c")
    chips = [(1 - x, y), (x, 1 - y), (1 - x, 1 - y)]
    return x, y, c, chips


def _sems(n):
    return pltpu.SemaphoreType.DMA((n,))


def _remote(src, dst, send, recv, k, to):
    return pltpu.make_async_remote_copy(src_ref=src, dst_ref=dst, send_sem=send.at[k], recv_sem=recv.at[k],
                                        device_id=to, device_id_type=MESH)


def _cast_rows(dst, src, cols=slice(None)):
    rows = src.shape[0]
    step = next((s for s in (128, 64, 32, 16) if rows % s == 0), rows)
    for r0 in range(0, rows, step):
        dst[r0:r0 + step, cols] = src[r0:r0 + step, :].astype(dst.dtype)


def comm_only(name, sides):
    def body():
        pass

    return _call(body, sides, name=name, grid=(1,), in_specs=[], out_specs=[], out_shape=[], args=())[1]


def ag_blocks(shard, dtype):
    R, W = shard.shape

    def copy(outs, scr, k, block, to, src=None):
        dst = outs[0].at[block]
        return _remote(dst if src is None else src, dst, scr[1], scr[2], k, to)

    def local(outs, scr, me):
        return pltpu.make_async_copy(scr[0], outs[0].at[me], scr[3].at[0])

    def start(ins, outs, scr):
        x, y, c, chips = _place()
        me = 4 * x + 2 * y + c
        _cast_rows(scr[0], ins[0])
        local(outs, scr, me).start()
        copy(outs, scr, 0, me, (x, y, 1 - c), src=scr[0]).start()
        for j, (cx, cy) in enumerate(chips):
            copy(outs, scr, 1 + j, me, (cx, cy, c), src=scr[0]).start()

    def finish(ins, outs, scr):
        x, y, c, chips = _place()
        me, sib = 4 * x + 2 * y + c, (x, y, 1 - c)
        passed = []
        for j, (cx, cy) in enumerate(chips):
            theirs = 4 * cx + 2 * cy + c
            copy(outs, scr, 1 + j, theirs, (x, y, c)).wait_recv()
            fwd = copy(outs, scr, 4 + j, theirs, sib)
            fwd.start()
            passed.append(fwd)
        copy(outs, scr, 0, 4 * x + 2 * y + 1 - c, (x, y, c)).wait_recv()
        for j, (cx, cy) in enumerate(chips):
            copy(outs, scr, 4 + j, 4 * cx + 2 * cy + 1 - c, (x, y, c)).wait_recv()
        copy(outs, scr, 0, me, sib, src=scr[0]).wait_send()
        for j, (cx, cy) in enumerate(chips):
            copy(outs, scr, 1 + j, me, (cx, cy, c), src=scr[0]).wait_send()
        for fwd in passed:
            fwd.wait_send()
        local(outs, scr, me).wait()

    return Side((shard,), (VMEM,), (jax.ShapeDtypeStruct((NDEV, R, W), dtype),),
                (pltpu.VMEM((R, W), dtype), _sems(7), _sems(7), _sems(1)), start, finish, None, "dsxy")


def ag_blocks_relay(shard, dtype, transpose=False):
    R, W = shard.shape[::-1] if transpose else shard.shape
    half = R // 2

    def copy(outs, scr, k, block, to, src=None, rows=None):
        dst = outs[0].at[block] if rows is None else outs[0].at[block, pl.ds(rows * half, half), :]
        return _remote(dst if src is None else src, dst, scr[1], scr[2], k, to)

    def local(outs, scr, me):
        return pltpu.make_async_copy(scr[0], outs[0].at[me], scr[3].at[0])

    def own(outs, scr):
        x, y, c, _ = _place()
        me = 4 * x + 2 * y + c
        return [copy(outs, scr, k, me, to, src=scr[0])
                for k, to in enumerate([(x, y, 1 - c), (1 - x, y, c), (x, 1 - y, c)])]

    def start(ins, outs, scr):
        x, y, c, _ = _place()
        if transpose:
            scr[0][...] = ins[0][...].T.astype(dtype)
        else:
            _cast_rows(scr[0], ins[0])
        local(outs, scr, 4 * x + 2 * y + c).start()
        for cp in own(outs, scr):
            cp.start()

    def passed_on(outs, scr):
        x, y, c, _ = _place()
        sib, xn, yn = (x, y, 1 - c), (1 - x, y, c), (x, 1 - y, c)
        b_xn, b_yn, b_dg = 4 * (1 - x) + 2 * y + c, 4 * x + 2 * (1 - y) + c, 4 * (1 - x) + 2 * (1 - y) + c
        near = [copy(outs, scr, 5, b_xn, yn, rows=0), copy(outs, scr, 3, b_xn, sib),
                copy(outs, scr, 6, b_yn, xn, rows=1), copy(outs, scr, 4, b_yn, sib)]
        far = [copy(outs, scr, 7, b_dg, sib, rows=0), copy(outs, scr, 8, b_dg, sib, rows=1)]
        return (b_xn, b_yn, b_dg), near, far

    def mid(ins, outs, scr):
        x, y, c, _ = _place()
        (b_xn, b_yn, _), near, _ = passed_on(outs, scr)
        copy(outs, scr, 1, b_xn, (x, y, c)).wait_recv()
        near[0].start()
        near[1].start()
        copy(outs, scr, 2, b_yn, (x, y, c)).wait_recv()
        near[2].start()
        near[3].start()

    def finish(ins, outs, scr):
        x, y, c, _ = _place()
        here = (x, y, c)
        (b_xn, b_yn, b_dg), near, far = passed_on(outs, scr)
        copy(outs, scr, 5, b_dg, here, rows=0).wait_recv()
        far[0].start()
        copy(outs, scr, 6, b_dg, here, rows=1).wait_recv()
        far[1].start()
        flip = 1 - 2 * c
        copy(outs, scr, 0, 4 * x + 2 * y + 1 - c, here).wait_recv()
        copy(outs, scr, 3, b_xn + flip, here).wait_recv()
        copy(outs, scr, 4, b_yn + flip, here).wait_recv()
        copy(outs, scr, 7, b_dg + flip, here, rows=0).wait_recv()
        copy(outs, scr, 8, b_dg + flip, here, rows=1).wait_recv()
        for cp in own(outs, scr) + near + far:
            cp.wait_send()
        local(outs, scr, 4 * x + 2 * y + c).wait()

    return Side((shard,), (VMEM,), (jax.ShapeDtypeStruct((NDEV, R, W), dtype),),
                (pltpu.VMEM((R, W), dtype), _sems(9), _sems(9), _sems(1)), start, finish, mid, "sxy")


def copies_side(args, out_shape, n_copies, plan, peers):
    def copies(ins, outs, scr):
        return [_remote(s_, d_, scr[0], scr[1], i, to) for i, (s_, d_, to) in enumerate(plan(ins, outs))]

    def start(ins, outs, scr):
        for cp in copies(ins, outs, scr):
            cp.start()

    def finish(ins, outs, scr):
        for cp in copies(ins, outs, scr):
            cp.wait()

    return Side(tuple(args), (ANY,) * len(args), tuple(out_shape), (_sems(n_copies), _sems(n_copies)),
                start, finish, None, peers)


def rs_to_sibling(grads):
    out_shape = [jax.ShapeDtypeStruct((4,) + g.shape[1:], BF16) for g in grads]

    def plan(ins, outs):
        x, y, c, _ = _place()
        return [(g.at[2 * k + 1 - c], r.at[k], (x, y, 1 - c)) for g, r in zip(ins, outs) for k in range(4)]

    return copies_side(grads, out_shape, 4 * len(grads), plan, "s")


def rs_to_chips(parts):
    out_shape = [jax.ShapeDtypeStruct((3,) + p.shape[1:], BF16) for p in parts]

    def plan(ins, outs):
        x, y, c, chips = _place()
        return [(p.at[2 * cx + cy], r.at[j], (cx, cy, c))
                for p, r in zip(ins, outs) for j, (cx, cy) in enumerate(chips)]

    return copies_side(parts, out_shape, 3 * len(parts), plan, "dxy")


def rs_to_chips_combined(part):
    _, R, W = part.shape
    half = R // 2
    top, bot = pl.ds(0, half), pl.ds(half, half)

    def copies(ins, outs, scr):
        p, r = ins[0], outs[0]
        loc_a, loc_b, in_x, in_y, comb_a, comb_b, send, recv, loc = scr
        x, y, c, _ = _place()
        xn, yn = (1 - x, y, c), (x, 1 - y, c)
        k_xn, k_yn, k_dg = 2 * (1 - x) + y, 2 * x + 1 - y, 2 * (1 - x) + 1 - y
        direct = [_remote(p.at[k_xn, top, :], r.at[0, top, :], send, recv, 0, xn),
                  _remote(p.at[k_yn, bot, :], r.at[1, bot, :], send, recv, 1, yn),
                  _remote(p.at[k_dg, top, :], in_x, send, recv, 2, xn),
                  _remote(p.at[k_dg, bot, :], in_y, send, recv, 3, yn)]
        combined = [_remote(comb_a, r.at[1, top, :], send, recv, 4, yn),
                    _remote(comb_b, r.at[0, bot, :], send, recv, 5, xn)]
        local = [pltpu.make_async_copy(p.at[k_yn, top, :], loc_a, loc.at[0]),
                 pltpu.make_async_copy(p.at[k_xn, bot, :], loc_b, loc.at[1])]
        return direct, combined, local

    def start(ins, outs, scr):
        direct, _, local = copies(ins, outs, scr)
        for cp in local + direct:
            cp.start()

    def mid(ins, outs, scr):
        loc_a, loc_b, in_x, in_y, comb_a, comb_b = scr[:6]
        direct, combined, local = copies(ins, outs, scr)
        for mine, arrival, inbox, out, nxt in ((local[0], direct[2], in_x, comb_a, combined[0]),
                                               (local[1], direct[3], in_y, comb_b, combined[1])):
            mine.wait()
            arrival.wait_recv()
            src = loc_a if out is comb_a else loc_b
            out[...] = (src[...].astype(F32) + inbox[...].astype(F32)).astype(BF16)
            nxt.start()

    def finish(ins, outs, scr):
        direct, combined, _ = copies(ins, outs, scr)
        direct[0].wait_recv()
        direct[1].wait_recv()
        combined[0].wait_recv()
        combined[1].wait_recv()
        for cp in direct + combined:
            cp.wait_send()

    buf = pltpu.VMEM((half, W), BF16)
    return Side((part,), (ANY,), (jax.ShapeDtypeStruct((2, R, W), BF16),),
                (buf, buf, buf, buf, buf, buf, _sems(6), _sems(6), _sems(2)), start, finish, mid, "xy")


ADAM_TILE_BYTES = 3 * 512 * 1024


def _row_tiles(rows, width):
    return 2 if rows % 32 == 0 and rows * width * 4 > ADAM_TILE_BYTES else 1


def chip_sum(name, grads, recvs, c_idx, chip_idx):
    n = len(grads)

    def body(s_ref, *refs):
        k = pl.program_id(0)
        for g_ref, r_ref, p_ref, own_ref in zip(refs[:n], refs[n:2 * n], refs[2 * n::2], refs[2 * n + 1::2]):
            tot = g_ref[0] + r_ref[0].astype(F32)
            p_ref[0] = tot.astype(BF16)

            @pl.when(k == s_ref[1])
            def _(own_ref=own_ref, tot=tot):
                own_ref[...] = tot

    def block(g):
        return (1,) + g.shape[1:]

    grid_spec = pltpu.PrefetchScalarGridSpec(
        num_scalar_prefetch=1, grid=(4,),
        in_specs=[pl.BlockSpec(block(g), lambda k, s: (2 * k + s[0], 0, 0)) for g in grads]
        + [pl.BlockSpec(block(g), lambda k, s: (k, 0, 0)) for g in grads],
        out_specs=[sp for g in grads for sp in (pl.BlockSpec(block(g), lambda k, s: (k, 0, 0)),
                                                pl.BlockSpec(g.shape[1:], lambda k, s: (0, 0)))])
    res = pl.pallas_call(
        body, name=name, grid_spec=grid_spec,
        out_shape=[sh for g in grads for sh in (jax.ShapeDtypeStruct((4,) + g.shape[1:], BF16),
                                                jax.ShapeDtypeStruct(g.shape[1:], F32))],
        compiler_params=_cp(dimension_semantics=("arbitrary",)),
    )(jnp.stack([c_idx, chip_idx]), *grads, *recvs)
    return [(res[2 * j], res[2 * j + 1]) for j in range(n)]


def _adamw(w, g, m, v):
    m2 = ADAM_B1 * m + (1.0 - ADAM_B1) * g
    v2 = ADAM_B2 * v + (1.0 - ADAM_B2) * (g * g)
    m_hat = m2 / (1.0 - ADAM_B1 ** ADAM_STEP)
    v_hat = v2 / (1.0 - ADAM_B2 ** ADAM_STEP)
    delta = -ADAM_LR * (m_hat / (jnp.sqrt(v_hat) + ADAM_EPS) + ADAM_WD * w)
    return delta, m2, v2


def shard_adam(name, owns, recvs, w, m, v):
    n = len(owns)
    R = owns[0].shape[0]
    ct = min(o.shape[1] for o in owns)
    first = [sum(o.shape[1] for o in owns[:j]) // ct for j in range(n)]
    count = [o.shape[1] // ct for o in owns]
    nt = _row_tiles(R, ct)
    tr = R // nt

    def body(*refs):
        o_refs, r_refs = refs[:n], refs[n:2 * n]
        w_ref, m_ref, v_ref, g_ref, d_ref, nm_ref, nv_ref = refs[2 * n:]
        g = None
        for j in range(n):
            gj = o_refs[j][...]
            for q in range(recvs[j].shape[0]):
                gj = gj + r_refs[j][q].astype(F32)
            g = gj if g is None else jnp.where(pl.program_id(0) >= first[j], gj, g)
        delta, m2, v2 = _adamw(w_ref[...], g, m_ref[...], v_ref[...])
        g_ref[...] = g
        d_ref[...] = delta
        nm_ref[...] = m2
        nv_ref[...] = v2

    def part(j):
        return pl.BlockSpec((tr, ct), lambda k, i: (i, jnp.clip(k - first[j], 0, count[j] - 1)))

    def part3(j):
        return pl.BlockSpec((recvs[j].shape[0], tr, ct), lambda k, i: (0, i, jnp.clip(k - first[j], 0, count[j] - 1)))

    C = sum(count) * ct
    tile = pl.BlockSpec((tr, ct), lambda k, i: (i, k))
    return pl.pallas_call(
        body, name=name, grid=(sum(count), nt),
        in_specs=[part(j) for j in range(n)] + [part3(j) for j in range(n)] + [tile, tile, tile],
        out_specs=[tile] * 4, out_shape=[jax.ShapeDtypeStruct((R, C), F32)] * 4,
        compiler_params=_cp(dimension_semantics=("arbitrary", "arbitrary")),
    )(*owns, *recvs, w, m, v)


def rows_adam(name, items, steps):
    n = len(items)

    def body(*refs):
        for j in range(n):
            o_ref, r_ref, w_ref, m_ref, v_ref = refs[5 * j:5 * j + 5]
            g = o_ref[...]
            for q in range(r_ref.shape[0]):
                g = g + r_ref[q].astype(F32)
            delta, m2, v2 = _adamw(w_ref[...], g, m_ref[...], v_ref[...])
            for ref, val in zip(refs[5 * n + 4 * j:5 * n + 4 * j + 4], (g, delta, m2, v2)):
                ref[...] = val

    def tile(a):
        return pl.BlockSpec((a.shape[0] // steps, a.shape[1]), lambda i: (i, 0))

    def tile3(a):
        return pl.BlockSpec((a.shape[0], a.shape[1] // steps, a.shape[2]), lambda i: (0, i, 0))

    res = pl.pallas_call(
        body, name=name, grid=(steps,),
        in_specs=[sp for own, recv, w, _, _ in items for sp in (tile(own), tile3(recv), tile(w), tile(w), tile(w))],
        out_specs=[tile(item[2]) for item in items for _ in range(4)],
        out_shape=[jax.ShapeDtypeStruct(item[2].shape, F32) for item in items for _ in range(4)],
        compiler_params=_cp(dimension_semantics=("arbitrary",)))(*[a for item in items for a in item])
    return [tuple(res[4 * j:4 * j + 4]) for j in range(n)]


def _block_adam(transposed, in_refs, out_refs):
    for j, io_t in enumerate(transposed):
        o_ref, r_ref, w_ref, m_ref, v_ref = in_refs[5 * j:5 * j + 5]
        g = o_ref[...]
        for q in range(r_ref.shape[0]):
            g = g + r_ref[q].astype(F32)
        t = (lambda a: a.T) if io_t else (lambda a: a)
        delta, m2, v2 = _adamw(t(w_ref[...]), g, t(m_ref[...]), t(v_ref[...]))
        for ref, val in zip(out_refs[4 * j:4 * j + 4], (g, delta, m2, v2)):
            ref[...] = t(val)


ROW_N1, ROW_N2, ROW_BG, ROW_QN, ROW_KN, ROW_CB, ROW_LW, ROW_LB, ROW_CW = 0, 1, 2, 4, 5, 6, 7, 8, 9
PACK_ROWS = 40
SMALL = ("norm1_w", "norm2_w", "b_gate", "q_norm_w", "k_norm_w", "conv_b", "conv_ln_w", "conv_ln_b", "conv_w")


def small_sync(g, sq, sides=()):
    ns = len(SMALL)

    def copies(refs):
        pack, recv, send_sems, recv_sems = refs[ns + 2:]
        x, y, c, _ = _place()
        return [pltpu.make_async_remote_copy(
            src_ref=pack, dst_ref=recv.at[4 * x + 2 * y + c], send_sem=send_sems.at[k - 1],
            recv_sem=recv_sems.at[k - 1], device_id=(x ^ (k >> 2), y ^ ((k >> 1) & 1), c ^ (k & 1)),
            device_id_type=MESH) for k in range(1, NDEV)]

    def body(*refs):
        gi = dict(zip(SMALL, refs[:ns]))
        sq_ref, tot, pack, recv, send_sems, recv_sems = refs[ns:]
        x, y, c, _ = _place()
        me = 4 * x + 2 * y + c

        pack[...] = jnp.zeros_like(pack)
        pack[ROW_KN:ROW_KN + 1, LANES:2 * LANES] = jnp.full((1, LANES), (0.5 / D) * jnp.sum(sq_ref[...]), F32)
        pack[ROW_N1:ROW_N1 + 1, :] = gi["norm1_w"][...]
        pack[ROW_N2:ROW_N2 + 1, :] = gi["norm2_w"][...]
        pack[ROW_BG:ROW_BG + 2, :] = gi["b_gate"][...]
        for row, name in ((ROW_QN, "q_norm_w"), (ROW_KN, "k_norm_w")):
            pack[row:row + 1, 0:HD] = gi[name][0:1, 0:HD] + gi[name][0:1, HD:LANES]
        pack[ROW_CB:ROW_CB + 1, 0:CC] = gi["conv_b"][...]
        pack[ROW_LW:ROW_LW + 1, 0:CC] = gi["conv_ln_w"][...]
        pack[ROW_LB:ROW_LB + 1, 0:CC] = gi["conv_ln_b"][...]
        pack[ROW_CW:ROW_CW + KW, 0:CC] = gi["conv_w"][...]

        for cp in copies(refs):
            cp.start()
        recv[me] = pack[...]

    def tail(*refs):
        tot, recv = refs[ns + 1], refs[ns + 3]
        for cp in copies(refs):
            cp.wait()
        acc = recv[0]
        for p in range(1, NDEV):
            acc = acc + recv[p]
        tot[...] = acc

    args = [g[k] for k in SMALL] + [sq]
    res = _call(
        body, sides, name="small_sync", grid=(1,), in_specs=[VMEM] * len(args), out_specs=[VMEM],
        out_shape=[jax.ShapeDtypeStruct((PACK_ROWS, D), F32)],
        scratch_shapes=[pltpu.VMEM((PACK_ROWS, D), F32), pltpu.VMEM((NDEV, PACK_ROWS, D), F32),
                        _sems(NDEV - 1), _sems(NDEV - 1)],
        args=args, own_comm=True, tail=tail)
    return (res[0][0], res[1]) if sides else res[0]


def small_adam(tot, w, m, v, me, blocks):
    ns, nb = len(SMALL), len(blocks)

    def body(me_ref, tot, *refs):
        wi = dict(zip(SMALL, refs[:ns]))
        mi = dict(zip(SMALL, refs[ns:2 * ns]))
        vi = dict(zip(SMALL, refs[2 * ns:3 * ns]))
        block_in, refs = refs[3 * ns:3 * ns + 5 * nb], refs[:3 * ns] + refs[3 * ns + 5 * nb:]
        outs = refs[3 * ns:7 * ns]
        loss_ref = refs[7 * ns]
        _block_adam([b[5] for b in blocks], block_in, refs[7 * ns + 1:])
        me = me_ref[0]

        def shard_grad(name):
            if name == "b_gate":
                return tot[ROW_BG:ROW_BG + 2, pl.ds(pl.multiple_of(me * LANES, LANES), LANES)]
            if name == "conv_w":
                win = tot[ROW_CW:ROW_CW + KW, pl.ds(pl.multiple_of((me // 2) * LANES, LANES), LANES)]
                return jnp.where(me % 2 == 1, win[:, HD:LANES], win[:, 0:HD])
            row = {"norm1_w": ROW_N1, "norm2_w": ROW_N2, "q_norm_w": ROW_QN, "k_norm_w": ROW_KN,
                   "conv_b": ROW_CB, "conv_ln_w": ROW_LW, "conv_ln_b": ROW_LB}[name]
            return tot[row:row + 1, 0:wi[name].shape[1]]

        for i, name in enumerate(SMALL):
            gr = shard_grad(name)
            delta, m2, v2 = _adamw(wi[name][...], gr, mi[name][...], vi[name][...])
            outs[4 * i][...] = gr
            outs[4 * i + 1][...] = delta
            outs[4 * i + 2][...] = m2
            outs[4 * i + 3][...] = v2
        loss_ref[...] = tot[ROW_KN:ROW_KN + 1, LANES:2 * LANES]

    out_shape = []
    for name in SMALL:
        out_shape += [jax.ShapeDtypeStruct(w[name].shape, F32)] * 4
    out_shape.append(jax.ShapeDtypeStruct((1, LANES), F32))
    out_shape += [jax.ShapeDtypeStruct(b[2].shape, F32) for b in blocks for _ in range(4)]
    args = ([tot] + [w[k] for k in SMALL] + [m[k] for k in SMALL] + [v[k] for k in SMALL]
            + [a for b in blocks for a in b[:5]])
    grid_spec = pltpu.PrefetchScalarGridSpec(
        num_scalar_prefetch=1, grid=(1,), in_specs=[VMEM] * len(args), out_specs=[VMEM] * len(out_shape))
    res = pl.pallas_call(body, name="small_adam", grid_spec=grid_spec, out_shape=out_shape,
                         compiler_params=_cp(dimension_semantics=("arbitrary",)))(me, *args)
    out = {name: tuple(res[4 * i:4 * i + 4]) for i, name in enumerate(SMALL)}
    return out, res[4 * ns][0, 0], [tuple(res[4 * ns + 1 + 4 * j:4 * ns + 5 + 4 * j]) for j in range(nb)]


MATS = ("w_in", "w_o_attn", "w_pw_conv", "w_out", "w_ffn_in", "w_ffn_out")
TRANSPOSED = ("w_in", "w_ffn_in")
WEIGHTS = ("norm1_w", "w_in", "b_gate", "q_norm_w", "k_norm_w", "w_o_attn", "conv_w", "conv_b", "conv_ln_w",
           "conv_ln_b", "w_pw_conv", "w_out", "norm2_w", "w_ffn_in", "w_ffn_out")


def _blocks_to_cols(blocks):
    n, R, C = blocks.shape
    return blocks.transpose(1, 0, 2).reshape(R, n * C)


def kernel(x, positions, norm1_w, w_in, b_gate, q_norm_w, k_norm_w, w_o_attn, conv_w, conv_b, conv_ln_w, conv_ln_b, w_pw_conv, w_out, norm2_w, w_ffn_in, w_ffn_out, loss_target, m_norm1_w, m_w_in, m_b_gate, m_q_norm_w, m_k_norm_w, m_w_o_attn, m_conv_w, m_conv_b, m_conv_ln_w, m_conv_ln_b, m_w_pw_conv, m_w_out, m_norm2_w, m_w_ffn_in, m_w_ffn_out, v_norm1_w, v_w_in, v_b_gate, v_q_norm_w, v_k_norm_w, v_w_o_attn, v_conv_w, v_conv_b, v_conv_ln_w, v_conv_ln_b, v_w_pw_conv, v_w_out, v_norm2_w, v_w_ffn_in, v_w_ffn_out):
    w = dict(norm1_w=norm1_w, w_in=w_in, b_gate=b_gate, q_norm_w=q_norm_w, k_norm_w=k_norm_w, w_o_attn=w_o_attn,
             conv_w=conv_w, conv_b=conv_b, conv_ln_w=conv_ln_w, conv_ln_b=conv_ln_b, w_pw_conv=w_pw_conv,
             w_out=w_out, norm2_w=norm2_w, w_ffn_in=w_ffn_in, w_ffn_out=w_ffn_out)
    m = dict(norm1_w=m_norm1_w, w_in=m_w_in, b_gate=m_b_gate, q_norm_w=m_q_norm_w, k_norm_w=m_k_norm_w,
             w_o_attn=m_w_o_attn, conv_w=m_conv_w, conv_b=m_conv_b, conv_ln_w=m_conv_ln_w,
             conv_ln_b=m_conv_ln_b, w_pw_conv=m_w_pw_conv, w_out=m_w_out, norm2_w=m_norm2_w,
             w_ffn_in=m_w_ffn_in, w_ffn_out=m_w_ffn_out)
    v = dict(norm1_w=v_norm1_w, w_in=v_w_in, b_gate=v_b_gate, q_norm_w=v_q_norm_w, k_norm_w=v_k_norm_w,
             w_o_attn=v_w_o_attn, conv_w=v_conv_w, conv_b=v_conv_b, conv_ln_w=v_conv_ln_w,
             conv_ln_b=v_conv_ln_b, w_pw_conv=v_w_pw_conv, w_out=v_w_out, norm2_w=v_norm2_w,
             w_ffn_in=v_w_ffn_in, w_ffn_out=v_w_ffn_out)
    def two_d(t):
        t = {k: (a[0] if a.ndim == 3 else a) for k, a in t.items()}
        return {k: (a.T if k in TRANSPOSED else a) for k, a in t.items()}

    w, m, v = two_d(w), two_d(m), two_d(v)

    x2, target = x[0], loss_target[0]
    c_idx = lax.axis_index("c").astype(jnp.int32)
    chip_idx = (2 * lax.axis_index("x") + lax.axis_index("y")).astype(jnp.int32)
    qw2 = jnp.tile(w["q_norm_w"], (1, 2))
    kw2 = jnp.tile(w["k_norm_w"], (1, 2))

    ax, ay = lax.axis_index("x"), lax.axis_index("y")
    chip_order = jnp.stack([2 * ax + ay, 2 * (1 - ax) + ay, 2 * ax + 1 - ay, 2 * (1 - ax) + 1 - ay]).astype(jnp.int32)
    h, proj, w_in_blocks, tabs = in_proj_gather(x2, w["norm1_w"], w["w_in"], chip_order, positions.reshape(S // LANES, LANES))
    w_in_t = w_in_blocks.reshape(INW, D)
    (attn, lse), ((w_ffn_in_blocks,), (w_out_blocks,), (w_o_blocks,), (w_pw_blocks,), (bg_blocks,), (cw_blocks,)) = attn_fwd(
        proj, tabs, qw2, kw2, sides=(ag_blocks_relay(w["w_ffn_in"], BF16), ag_blocks_relay(w["w_out"], BF16),
                                     ag_blocks_relay(w["w_o_attn"], BF16, transpose=True),
                                     ag_blocks_relay(w["w_pw_conv"], BF16, transpose=True),
                                     ag_blocks(w["b_gate"], F32), ag_blocks(w["conv_w"], F32)))
    w_ffn_in_t = w_ffn_in_blocks.reshape(2 * FF, D)
    w_out_f = w_out_blocks.reshape(D, D)
    w_o_t, w_pw_t = w_o_blocks.reshape(D, CC), w_pw_blocks.reshape(D, CC)
    b_gate_f, conv_w_f = _blocks_to_cols(bg_blocks), _blocks_to_cols(cw_blocks)
    cpre, u3 = conv_fwd(proj, conv_w_f, w["conv_b"], w["conv_ln_w"], w["conv_ln_b"])
    x1, z, ya, yb = mix_out(x2, proj, b_gate_f, attn, u3, w_o_t, w_pw_t, w_out_f)
    (h2, gu, f), ((w_ffn_out_blocks,),) = ffn_in(x1, w["norm2_w"], w_ffn_in_t, sides=(ag_blocks_relay(w["w_ffn_out"], BF16),))
    w_ffn_out_f = w_ffn_out_blocks.reshape(FF, D)
    dy, dyb, sq = ffn_out_loss(x1, f, w_ffn_out_f, target)

    g = {}
    def blocks(name, pairs, tm):
        return [t.reshape(NDEV, t.shape[0] // NDEV, t.shape[1]) for t in mm_tn(name, pairs, tm)]

    g_ffn_out, gb_ffn_out = blocks("gw_ffn_out", [(f, dyb)], FF // 2)
    (d_gu, d_x1, d_x1b, g["norm2_w"]), ((ra_ffn_out,),) = ffn_bwd(
        dy, dyb, gu, x1, w["norm2_w"], w_ffn_in_t, w_ffn_out_f, sides=(rs_to_sibling([gb_ffn_out]),))
    g_ffn_in, gb_ffn_in = blocks("gw_ffn_in", [(d_gu, h2)], FF // 2)
    (d_ya, d_yb, d_gl, d_attn, d_u3, g["b_gate"]), ((ra_ffn_in,),) = out_bwd(
        d_x1b, proj, b_gate_f, ya, yb, w_o_t, w_pw_t, w_out_f, sides=(rs_to_sibling([gb_ffn_in]),))
    g_out, gb_out, g_w_o, gb_w_o, g_w_pw, gb_w_pw = blocks(
        "gw_out_o_pw", [(z, d_x1b), (d_ya, attn), (d_yb, u3)], D // 2)
    (d_conv, g["conv_w"], g["conv_b"], g["conv_ln_w"], g["conv_ln_b"]), ((ra_out, ra_w_o, ra_w_pw),) = conv_bwd(
        proj, cpre, d_u3, conv_w_f, w["conv_ln_w"], w["conv_ln_b"],
        sides=(rs_to_sibling([gb_out, gb_w_o, gb_w_pw]),))
    (pb_ffn_out, own_ffn_out), (pb_ffn_in, own_ffn_in), (pb_out, own_out), (pb_w_o, own_w_o), (pb_w_pw, own_w_pw) = chip_sum(
        "chip_sum_early", [g_ffn_out, g_ffn_in, g_out, g_w_o, g_w_pw],
        [ra_ffn_out, ra_ffn_in, ra_out, ra_w_o, ra_w_pw], c_idx, chip_idx)
    (d_q, d_k, d_v, gqw, gkw), ((rb_ffn_out, rb_ffn_in, rb_out, rb_w_o, rb_w_pw),) = attn_bwd(
        proj, tabs, qw2, kw2, d_attn, attn, lse,
        sides=(rs_to_chips([pb_ffn_out, pb_ffn_in, pb_out, pb_w_o, pb_w_pw]),))
    g["q_norm_w"], g["k_norm_w"] = gqw, gkw
    d_segs = (d_q, d_k, d_v, d_conv, d_gl)
    parts, to_sibling, to_chips, owns, from_chips = [], None, None, [], []
    for k, hw in enumerate(GW_IN_SPLIT):
        sides = tuple(s for s in (to_chips, to_sibling) if s is not None)
        (part, part_b), outs = gw_in("gw_in_%d" % k, h, d_segs, sum(GW_IN_SPLIT[:k]), hw, sides=sides)
        outs = list(outs)
        if to_chips is not None:
            from_chips.append(outs.pop(0)[0])
        if to_sibling is not None:
            (pb, own), = chip_sum("chip_sum_w_in_%d" % (k - 1), [parts[-1]], [outs.pop(0)[0]], c_idx, chip_idx)
            owns.append(own)
            to_chips = rs_to_chips_combined(pb)
        else:
            to_chips = None
        parts.append(part.reshape(NDEV, INW // NDEV, hw))
        to_sibling = rs_to_sibling([part_b.reshape(NDEV, INW // NDEV, hw)])
    (grad_x, g["norm1_w"]), ((rb_prev,), (ra_last,)) = in_bwd(
        d_q, d_k, d_v, d_conv, d_gl, w_in_t, x2, d_x1, w["norm1_w"], sides=(to_chips, to_sibling))
    from_chips.append(rb_prev)
    (pb, own), = chip_sum("chip_sum_w_in_%d" % (len(GW_IN_SPLIT) - 1), [parts[-1]], [ra_last], c_idx, chip_idx)
    owns.append(own)
    small_sums, ((rb_last,),) = small_sync(g, sq, sides=(rs_to_chips_combined(pb),))
    small, loss, (adam_o, adam_pw, adam_out) = small_adam(
        small_sums, w, m, v, (4 * ax + 2 * ay + c_idx).astype(jnp.int32).reshape(1),
        [(own_w_o, rb_w_o, w["w_o_attn"], m["w_o_attn"], v["w_o_attn"], True),
         (own_w_pw, rb_w_pw, w["w_pw_conv"], m["w_pw_conv"], v["w_pw_conv"], True),
         (own_out, rb_out, w["w_out"], m["w_out"], v["w_out"], False)])
    from_chips.append(rb_last)

    adam_ffn_in, adam_ffn_out = rows_adam("adam_w_ffn", [
        (own_ffn_in, rb_ffn_in, w["w_ffn_in"], m["w_ffn_in"], v["w_ffn_in"]),
        (own_ffn_out, rb_ffn_out, w["w_ffn_out"], m["w_ffn_out"], v["w_ffn_out"])], 2)
    res = {
        "w_in": shard_adam("adam_w_in", owns, from_chips, w["w_in"], m["w_in"], v["w_in"]),
        "w_ffn_in": adam_ffn_in, "w_ffn_out": adam_ffn_out,
        "w_o_attn": adam_o, "w_pw_conv": adam_pw, "w_out": adam_out,
    }
    res = {k: tuple(a.T if k in TRANSPOSED else a for a in r) for k, r in res.items()}
    res.update(small)

    def shaped(name, a):
        return a.reshape((1,) + a.shape) if name in MATS or name in ("b_gate", "conv_w") else a

    outs = [loss, grad_x.reshape(1, S, D)]
    for i in range(4):
        outs += [shaped(k, res[k][i]) for k in WEIGHTS]
    return tuple(outs)
```
